```python
import math
import jax, jax.numpy as jnp
from jax import lax
import numpy as np

D_MODEL = 1024
BATCH = 8
SEQ = 2048
DEPTH = 2

N_MIXERS = 2
BRANCH_WIDTH = 2 * D_MODEL
XQ_WIDTH = BRANCH_WIDTH // 4
PRIMARY_WIDTH = BRANCH_WIDTH - XQ_WIDTH
MEM_LEN = 256
X_HEADS = 4
X_HEAD_DIM = XQ_WIDTH // X_HEADS
S5_GROUP_CH = 16
S5_GROUPS = PRIMARY_WIDTH // S5_GROUP_CH
S5_STATE = 64
S5_STEP_MIN = 1e-3
S5_STEP_MAX = 1e-1
MLA_NOPE = 128
MLA_ROPE = 64
MLA_V = 128
MLA_HEADS = PRIMARY_WIDTH // MLA_V
MLA_Q_LORA = D_MODEL // 2
MLA_KV_LORA = D_MODEL // 4
ROPE_THETA = 10000.0
Q_BLOCK = 128
EPS = 1e-6
N_S5 = (DEPTH + 1) // 2
N_MLA = DEPTH // 2
S5_IN_WIDTH = PRIMARY_WIDTH + XQ_WIDTH + BRANCH_WIDTH
MLA_IN_WIDTH = MLA_Q_LORA + MLA_KV_LORA + MLA_ROPE + XQ_WIDTH + BRANCH_WIDTH

kernel_name = "hybrid_s5_mla_memory_block"


def rms_norm(x, g):
    xf = x.astype(jnp.float32)
    y = xf * lax.rsqrt(jnp.mean(xf * xf, axis=-1, keepdims=True) + EPS)
    return (y * g.astype(jnp.float32)).astype(x.dtype)


def rotary_tables(positions):
    half = MLA_ROPE // 2
    inv_freq = ROPE_THETA ** (-jnp.arange(half, dtype=jnp.float32) / half)
    ang = positions.astype(jnp.float32)[:, :, None, None] * inv_freq
    return jnp.cos(ang), jnp.sin(ang)


def rotary(x, cos, sin):
    x1, x2 = jnp.split(x.astype(jnp.float32), 2, axis=-1)
    return jnp.concatenate([x1 * cos - x2 * sin, x1 * sin + x2 * cos], axis=-1).astype(x.dtype)


def _ssm_combine(left, right):
    a_l, b_l = left
    a_r, b_r = right
    return a_l * a_r, a_r * b_l + b_r


def s5_mix(u, lam_re, lam_im, log_step, b_re, b_im, c_re, c_im, d):
    bsz, seq, _ = u.shape
    f32 = jnp.float32
    uf = u.astype(f32).reshape(bsz, seq, S5_GROUPS, S5_GROUP_CH)
    lam = lax.complex(lam_re.astype(f32), lam_im.astype(f32))
    step = jnp.exp(log_step.astype(f32))[:, None]
    a_bar = jnp.exp(lam * step)
    b_mat = lax.complex(b_re.astype(f32), b_im.astype(f32))
    c_mat = lax.complex(c_re.astype(f32), c_im.astype(f32))
    b_bar = ((a_bar - 1.0) / lam)[..., None] * b_mat
    bu = jnp.einsum('blgc,gpc->blgp', uf.astype(jnp.complex64), b_bar)
    a_seq = jnp.broadcast_to(a_bar, (1, seq) + a_bar.shape)
    _, state = lax.associative_scan(_ssm_combine, (a_seq, bu), axis=1)
    y = jnp.einsum('blgp,gcp->blgc', state, c_mat).real + d.astype(f32).reshape(S5_GROUPS, S5_GROUP_CH) * uf
    return y.reshape(bsz, seq, PRIMARY_WIDTH).astype(u.dtype)


def causal_block_attention(q, k, v, scale):
    bsz, seq, heads, dk = q.shape
    dv = v.shape[-1]
    n_blocks = seq // Q_BLOCK
    q_blocks = q.reshape(bsz, n_blocks, Q_BLOCK, heads, dk).transpose(1, 0, 2, 3, 4)
    k_pos = jnp.arange(seq)

    def one_block(args):
        q_blk, blk = args
        s = jnp.einsum('bqhd,bkhd->bhqk', q_blk, k).astype(jnp.float32) * scale
        q_pos = blk * Q_BLOCK + jnp.arange(Q_BLOCK)
        s = jnp.where(k_pos[None, :] <= q_pos[:, None], s, jnp.finfo(jnp.float32).min)
        p = jax.nn.softmax(s, axis=-1).astype(v.dtype)
        return jnp.einsum('bhqk,bkhd->bqhd', p, v)

    out = lax.map(one_block, (q_blocks, jnp.arange(n_blocks)))
    return out.transpose(1, 0, 2, 3, 4).reshape(bsz, seq, heads, dv)


def memory_attention(xq, mem, mem_norm, w_mem_kv, xq_norm, xk_norm):
    bsz, seq, _ = xq.shape
    kv = rms_norm(mem, mem_norm) @ w_mem_kv
    k, v = jnp.split(kv, 2, axis=-1)
    k = rms_norm(k.reshape(bsz, -1, X_HEADS, X_HEAD_DIM), xk_norm)
    v = v.reshape(bsz, -1, X_HEADS, X_HEAD_DIM)
    q = rms_norm(xq.reshape(bsz, seq, X_HEADS, X_HEAD_DIM), xq_norm)
    s = jnp.einsum('blhd,bmhd->bhlm', q, k).astype(jnp.float32) * (X_HEAD_DIM ** -0.5)
    p = jax.nn.softmax(s, axis=-1).astype(v.dtype)
    return jnp.einsum('bhlm,bmhd->blhd', p, v).reshape(bsz, seq, XQ_WIDTH)


def merge_branches(x, mixer_out, xq, gate, mem, w_out, mem_norm, w_mem_kv, xq_norm, xk_norm):
    mem_out = memory_attention(xq, mem, mem_norm, w_mem_kv, xq_norm, xk_norm)
    o = jnp.concatenate([mixer_out, mem_out], axis=-1) * jax.nn.silu(gate)
    return x + o @ w_out


def s5_layer(x, mem, ln, w_in, lam_re, lam_im, log_step, b_re, b_im, c_re, c_im, d, w_glu,
             w_out, mem_norm, w_mem_kv, xq_norm, xk_norm):
    proj = rms_norm(x, ln) @ w_in
    u, xq, gate = jnp.split(proj, [PRIMARY_WIDTH, PRIMARY_WIDTH + XQ_WIDTH], axis=-1)
    y = s5_mix(u, lam_re, lam_im, log_step, b_re, b_im, c_re, c_im, d)
    y_a, y_b = jnp.split(jax.nn.gelu(y) @ w_glu, 2, axis=-1)
    y = y_a * jax.nn.sigmoid(y_b)
    return merge_branches(x, y, xq, gate, mem, w_out, mem_norm, w_mem_kv, xq_norm, xk_norm)


def mla_layer(x, mem, cos, sin, ln, w_in, q_lora_norm, kv_lora_norm, w_uq, w_ukv,
              q_nope_norm, k_nope_norm, q_rope_norm, k_rope_norm,
              w_out, mem_norm, w_mem_kv, xq_norm, xk_norm):
    bsz, seq, _ = x.shape
    proj = rms_norm(x, ln) @ w_in
    o1 = MLA_Q_LORA
    o2 = o1 + MLA_KV_LORA
    o3 = o2 + MLA_ROPE
    o4 = o3 + XQ_WIDTH
    c_q, c_kv, k_rope, xq, gate = jnp.split(proj, [o1, o2, o3, o4], axis=-1)
    q = (rms_norm(c_q, q_lora_norm) @ w_uq).reshape(bsz, seq, MLA_HEADS, MLA_NOPE + MLA_ROPE)
    kv = (rms_norm(c_kv, kv_lora_norm) @ w_ukv).reshape(bsz, seq, MLA_HEADS, MLA_NOPE + MLA_V)
    q_nope, q_rope = q[..., :MLA_NOPE], q[..., MLA_NOPE:]
    k_nope, v = kv[..., :MLA_NOPE], kv[..., MLA_NOPE:]
    q_rope = rotary(rms_norm(q_rope, q_rope_norm), cos, sin)
    k_rope = rotary(rms_norm(k_rope.reshape(bsz, seq, 1, MLA_ROPE), k_rope_norm), cos, sin)
    q_full = jnp.concatenate([rms_norm(q_nope, q_nope_norm), q_rope], axis=-1)
    k_full = jnp.concatenate([rms_norm(k_nope, k_nope_norm),
                              jnp.broadcast_to(k_rope, (bsz, seq, MLA_HEADS, MLA_ROPE))], axis=-1)
    attn = causal_block_attention(q_full, k_full, v, (MLA_NOPE + MLA_ROPE) ** -0.5)
    attn = attn.reshape(bsz, seq, PRIMARY_WIDTH)
    return merge_branches(x, attn, xq, gate, mem, w_out, mem_norm, w_mem_kv, xq_norm, xk_norm)


def _fwd_setup_inputs(seed: int = 0) -> dict:
    key = jax.random.key(seed)
    k = jax.random.split(key, 32)
    f32 = jnp.float32

    def w(kk, shape, fan_in):
        return jax.random.normal(kk, shape, f32) * (fan_in ** -0.5)

    def gain(kk, shape):
        return 1.0 + 0.02 * jax.random.normal(kk, shape, f32)

    x = jax.random.normal(k[0], (BATCH, SEQ, D_MODEL), f32)
    mem = jax.random.normal(k[1], (BATCH, MEM_LEN, D_MODEL), f32)
    offsets = jax.random.randint(k[2], (BATCH, 1), 0, 4096, dtype=jnp.int32)
    positions = offsets + jnp.arange(SEQ, dtype=jnp.int32)[None, :]

    lam_im_base = math.pi * jnp.arange(S5_STATE, dtype=f32)
    return {
        "x": x,
        "mem": mem,
        "positions": positions,
        "ln_gain": gain(k[3], (DEPTH, D_MODEL)),
        "w_out": w(k[4], (DEPTH, BRANCH_WIDTH, D_MODEL), BRANCH_WIDTH),
        "mem_norm": gain(k[5], (DEPTH, D_MODEL)),
        "w_mem_kv": w(k[6], (DEPTH, D_MODEL, 2 * XQ_WIDTH), D_MODEL),
        "xq_norm": gain(k[7], (DEPTH, X_HEAD_DIM)),
        "xk_norm": gain(k[8], (DEPTH, X_HEAD_DIM)),
        "s5_w_in": w(k[9], (N_S5, D_MODEL, S5_IN_WIDTH), D_MODEL),
        "s5_lambda_re": -0.5 + 0.01 * jax.random.normal(k[10], (N_S5, S5_GROUPS, S5_STATE), f32),
        "s5_lambda_im": lam_im_base + 0.01 * jax.random.normal(k[11], (N_S5, S5_GROUPS, S5_STATE), f32),
        "s5_log_step": jax.random.uniform(k[12], (N_S5, S5_GROUPS), f32,
                                          math.log(S5_STEP_MIN), math.log(S5_STEP_MAX)),
        "s5_b_re": w(k[13], (N_S5, S5_GROUPS, S5_STATE, S5_GROUP_CH), 2 * S5_GROUP_CH),
        "s5_b_im": w(k[14], (N_S5, S5_GROUPS, S5_STATE, S5_GROUP_CH), 2 * S5_GROUP_CH),
        "s5_c_re": w(k[15], (N_S5, S5_GROUPS, S5_GROUP_CH, S5_STATE), S5_STATE),
        "s5_c_im": w(k[16], (N_S5, S5_GROUPS, S5_GROUP_CH, S5_STATE), S5_STATE),
        "s5_d": jax.random.normal(k[17], (N_S5, PRIMARY_WIDTH), f32),
        "s5_w_glu": w(k[18], (N_S5, PRIMARY_WIDTH, 2 * PRIMARY_WIDTH), PRIMARY_WIDTH),
        "mla_w_in": w(k[19], (N_MLA, D_MODEL, MLA_IN_WIDTH), D_MODEL),
        "mla_q_lora_norm": gain(k[20], (N_MLA, MLA_Q_LORA)),
        "mla_kv_lora_norm": gain(k[21], (N_MLA, MLA_KV_LORA)),
        "mla_w_uq": w(k[22], (N_MLA, MLA_Q_LORA, MLA_HEADS * (MLA_NOPE + MLA_ROPE)), MLA_Q_LORA),
        "mla_w_ukv": w(k[23], (N_MLA, MLA_KV_LORA, MLA_HEADS * (MLA_NOPE + MLA_V)), MLA_KV_LORA),
        "mla_q_nope_norm": gain(k[24], (N_MLA, MLA_NOPE)),
        "mla_k_nope_norm": gain(k[25], (N_MLA, MLA_NOPE)),
        "mla_q_rope_norm": gain(k[26], (N_MLA, MLA_ROPE)),
        "mla_k_rope_norm": gain(k[27], (N_MLA, MLA_ROPE)),
    }


def _fwd_reference(x, mem, positions, ln_gain, w_out, mem_norm, w_mem_kv, xq_norm, xk_norm,
              s5_w_in, s5_lambda_re, s5_lambda_im, s5_log_step, s5_b_re, s5_b_im, s5_c_re, s5_c_im,
              s5_d, s5_w_glu, mla_w_in, mla_q_lora_norm, mla_kv_lora_norm, mla_w_uq, mla_w_ukv,
              mla_q_nope_norm, mla_k_nope_norm, mla_q_rope_norm, mla_k_rope_norm):
    cos, sin = rotary_tables(positions)
    for i in range(DEPTH):
        j = i // N_MIXERS
        if i % N_MIXERS == 0:
            x = s5_layer(x, mem, ln_gain[i], s5_w_in[j], s5_lambda_re[j], s5_lambda_im[j], s5_log_step[j],
                         s5_b_re[j], s5_b_im[j], s5_c_re[j], s5_c_im[j], s5_d[j], s5_w_glu[j],
                         w_out[i], mem_norm[i], w_mem_kv[i], xq_norm[i], xk_norm[i])
        else:
            x = mla_layer(x, mem, cos, sin, ln_gain[i], mla_w_in[j], mla_q_lora_norm[j], mla_kv_lora_norm[j],
                          mla_w_uq[j], mla_w_ukv[j], mla_q_nope_norm[j], mla_k_nope_norm[j],
                          mla_q_rope_norm[j], mla_k_rope_norm[j],
                          w_out[i], mem_norm[i], w_mem_kv[i], xq_norm[i], xk_norm[i])
    return x


import jax as _jax
import jax.numpy as _jnp

TWIN_FORMAT = 'train_step'
FWD_PARAMS = ['x', 'mem', 'positions', 'ln_gain', 'w_out', 'mem_norm', 'w_mem_kv', 'xq_norm', 'xk_norm', 's5_w_in', 's5_lambda_re', 's5_lambda_im', 's5_log_step', 's5_b_re', 's5_b_im', 's5_c_re', 's5_c_im', 's5_d', 's5_w_glu', 'mla_w_in', 'mla_q_lora_norm', 'mla_kv_lora_norm', 'mla_w_uq', 'mla_w_ukv', 'mla_q_nope_norm', 'mla_k_nope_norm', 'mla_q_rope_norm', 'mla_k_rope_norm']
TWIN_WEIGHTS = ['ln_gain', 'w_out', 'mem_norm', 'w_mem_kv', 'xq_norm', 'xk_norm', 's5_w_in', 's5_lambda_re', 's5_lambda_im', 's5_log_step', 's5_b_re', 's5_b_im', 's5_c_re', 's5_c_im', 's5_d', 's5_w_glu', 'mla_w_in', 'mla_q_lora_norm', 'mla_kv_lora_norm', 'mla_w_uq', 'mla_w_ukv', 'mla_q_nope_norm', 'mla_k_nope_norm', 'mla_q_rope_norm', 'mla_k_rope_norm']
TWIN_DIFF_INPUT = 'x'
TWIN_INPUTS = ['x', 'mem', 'positions', 'ln_gain', 'w_out', 'mem_norm', 'w_mem_kv', 'xq_norm', 'xk_norm', 's5_w_in', 's5_lambda_re', 's5_lambda_im', 's5_log_step', 's5_b_re', 's5_b_im', 's5_c_re', 's5_c_im', 's5_d', 's5_w_glu', 'mla_w_in', 'mla_q_lora_norm', 'mla_kv_lora_norm', 'mla_w_uq', 'mla_w_ukv', 'mla_q_nope_norm', 'mla_k_nope_norm', 'mla_q_rope_norm', 'mla_k_rope_norm', 'loss_target', 'm_ln_gain', 'm_w_out', 'm_mem_norm', 'm_w_mem_kv', 'm_xq_norm', 'm_xk_norm', 'm_s5_w_in', 'm_s5_lambda_re', 'm_s5_lambda_im', 'm_s5_log_step', 'm_s5_b_re', 'm_s5_b_im', 'm_s5_c_re', 'm_s5_c_im', 'm_s5_d', 'm_s5_w_glu', 'm_mla_w_in', 'm_mla_q_lora_norm', 'm_mla_kv_lora_norm', 'm_mla_w_uq', 'm_mla_w_ukv', 'm_mla_q_nope_norm', 'm_mla_k_nope_norm', 'm_mla_q_rope_norm', 'm_mla_k_rope_norm', 'v_ln_gain', 'v_w_out', 'v_mem_norm', 'v_w_mem_kv', 'v_xq_norm', 'v_xk_norm', 'v_s5_w_in', 'v_s5_lambda_re', 'v_s5_lambda_im', 'v_s5_log_step', 'v_s5_b_re', 'v_s5_b_im', 'v_s5_c_re', 'v_s5_c_im', 'v_s5_d', 'v_s5_w_glu', 'v_mla_w_in', 'v_mla_q_lora_norm', 'v_mla_kv_lora_norm', 'v_mla_w_uq', 'v_mla_w_ukv', 'v_mla_q_nope_norm', 'v_mla_k_nope_norm', 'v_mla_q_rope_norm', 'v_mla_k_rope_norm']
TWIN_OUTPUTS = ['loss', 'grad_x', 'grad_ln_gain', 'grad_w_out', 'grad_mem_norm', 'grad_w_mem_kv', 'grad_xq_norm', 'grad_xk_norm', 'grad_s5_w_in', 'grad_s5_lambda_re', 'grad_s5_lambda_im', 'grad_s5_log_step', 'grad_s5_b_re', 'grad_s5_b_im', 'grad_s5_c_re', 'grad_s5_c_im', 'grad_s5_d', 'grad_s5_w_glu', 'grad_mla_w_in', 'grad_mla_q_lora_norm', 'grad_mla_kv_lora_norm', 'grad_mla_w_uq', 'grad_mla_w_ukv', 'grad_mla_q_nope_norm', 'grad_mla_k_nope_norm', 'grad_mla_q_rope_norm', 'grad_mla_k_rope_norm', 'delta_ln_gain', 'delta_w_out', 'delta_mem_norm', 'delta_w_mem_kv', 'delta_xq_norm', 'delta_xk_norm', 'delta_s5_w_in', 'delta_s5_lambda_re', 'delta_s5_lambda_im', 'delta_s5_log_step', 'delta_s5_b_re', 'delta_s5_b_im', 'delta_s5_c_re', 'delta_s5_c_im', 'delta_s5_d', 'delta_s5_w_glu', 'delta_mla_w_in', 'delta_mla_q_lora_norm', 'delta_mla_kv_lora_norm', 'delta_mla_w_uq', 'delta_mla_w_ukv', 'delta_mla_q_nope_norm', 'delta_mla_k_nope_norm', 'delta_mla_q_rope_norm', 'delta_mla_k_rope_norm', 'new_m_ln_gain', 'new_m_w_out', 'new_m_mem_norm', 'new_m_w_mem_kv', 'new_m_xq_norm', 'new_m_xk_norm', 'new_m_s5_w_in', 'new_m_s5_lambda_re', 'new_m_s5_lambda_im', 'new_m_s5_log_step', 'new_m_s5_b_re', 'new_m_s5_b_im', 'new_m_s5_c_re', 'new_m_s5_c_im', 'new_m_s5_d', 'new_m_s5_w_glu', 'new_m_mla_w_in', 'new_m_mla_q_lora_norm', 'new_m_mla_kv_lora_norm', 'new_m_mla_w_uq', 'new_m_mla_w_ukv', 'new_m_mla_q_nope_norm', 'new_m_mla_k_nope_norm', 'new_m_mla_q_rope_norm', 'new_m_mla_k_rope_norm', 'new_v_ln_gain', 'new_v_w_out', 'new_v_mem_norm', 'new_v_w_mem_kv', 'new_v_xq_norm', 'new_v_xk_norm', 'new_v_s5_w_in', 'new_v_s5_lambda_re', 'new_v_s5_lambda_im', 'new_v_s5_log_step', 'new_v_s5_b_re', 'new_v_s5_b_im', 'new_v_s5_c_re', 'new_v_s5_c_im', 'new_v_s5_d', 'new_v_s5_w_glu', 'new_v_mla_w_in', 'new_v_mla_q_lora_norm', 'new_v_mla_kv_lora_norm', 'new_v_mla_w_uq', 'new_v_mla_w_ukv', 'new_v_mla_q_nope_norm', 'new_v_mla_k_nope_norm', 'new_v_mla_q_rope_norm', 'new_v_mla_k_rope_norm']
TWIN_LEAF_KINDS = {'loss': 'loss', 'grad_x': 'grad_x', 'grad_ln_gain': 'grad_w', 'grad_w_out': 'grad_w', 'grad_mem_norm': 'grad_w', 'grad_w_mem_kv': 'grad_w', 'grad_xq_norm': 'grad_w', 'grad_xk_norm': 'grad_w', 'grad_s5_w_in': 'grad_w', 'grad_s5_lambda_re': 'grad_w', 'grad_s5_lambda_im': 'grad_w', 'grad_s5_log_step': 'grad_w', 'grad_s5_b_re': 'grad_w', 'grad_s5_b_im': 'grad_w', 'grad_s5_c_re': 'grad_w', 'grad_s5_c_im': 'grad_w', 'grad_s5_d': 'grad_w', 'grad_s5_w_glu': 'grad_w', 'grad_mla_w_in': 'grad_w', 'grad_mla_q_lora_norm': 'grad_w', 'grad_mla_kv_lora_norm': 'grad_w', 'grad_mla_w_uq': 'grad_w', 'grad_mla_w_ukv': 'grad_w', 'grad_mla_q_nope_norm': 'grad_w', 'grad_mla_k_nope_norm': 'grad_w', 'grad_mla_q_rope_norm': 'grad_w', 'grad_mla_k_rope_norm': 'grad_w', 'delta_ln_gain': 'delta_w', 'delta_w_out': 'delta_w', 'delta_mem_norm': 'delta_w', 'delta_w_mem_kv': 'delta_w', 'delta_xq_norm': 'delta_w', 'delta_xk_norm': 'delta_w', 'delta_s5_w_in': 'delta_w', 'delta_s5_lambda_re': 'delta_w', 'delta_s5_lambda_im': 'delta_w', 'delta_s5_log_step': 'delta_w', 'delta_s5_b_re': 'delta_w', 'delta_s5_b_im': 'delta_w', 'delta_s5_c_re': 'delta_w', 'delta_s5_c_im': 'delta_w', 'delta_s5_d': 'delta_w', 'delta_s5_w_glu': 'delta_w', 'delta_mla_w_in': 'delta_w', 'delta_mla_q_lora_norm': 'delta_w', 'delta_mla_kv_lora_norm': 'delta_w', 'delta_mla_w_uq': 'delta_w', 'delta_mla_w_ukv': 'delta_w', 'delta_mla_q_nope_norm': 'delta_w', 'delta_mla_k_nope_norm': 'delta_w', 'delta_mla_q_rope_norm': 'delta_w', 'delta_mla_k_rope_norm': 'delta_w', 'new_m_ln_gain': 'new_m', 'new_m_w_out': 'new_m', 'new_m_mem_norm': 'new_m', 'new_m_w_mem_kv': 'new_m', 'new_m_xq_norm': 'new_m', 'new_m_xk_norm': 'new_m', 'new_m_s5_w_in': 'new_m', 'new_m_s5_lambda_re': 'new_m', 'new_m_s5_lambda_im': 'new_m', 'new_m_s5_log_step': 'new_m', 'new_m_s5_b_re': 'new_m', 'new_m_s5_b_im': 'new_m', 'new_m_s5_c_re': 'new_m', 'new_m_s5_c_im': 'new_m', 'new_m_s5_d': 'new_m', 'new_m_s5_w_glu': 'new_m', 'new_m_mla_w_in': 'new_m', 'new_m_mla_q_lora_norm': 'new_m', 'new_m_mla_kv_lora_norm': 'new_m', 'new_m_mla_w_uq': 'new_m', 'new_m_mla_w_ukv': 'new_m', 'new_m_mla_q_nope_norm': 'new_m', 'new_m_mla_k_nope_norm': 'new_m', 'new_m_mla_q_rope_norm': 'new_m', 'new_m_mla_k_rope_norm': 'new_m', 'new_v_ln_gain': 'new_v', 'new_v_w_out': 'new_v', 'new_v_mem_norm': 'new_v', 'new_v_w_mem_kv': 'new_v', 'new_v_xq_norm': 'new_v', 'new_v_xk_norm': 'new_v', 'new_v_s5_w_in': 'new_v', 'new_v_s5_lambda_re': 'new_v', 'new_v_s5_lambda_im': 'new_v', 'new_v_s5_log_step': 'new_v', 'new_v_s5_b_re': 'new_v', 'new_v_s5_b_im': 'new_v', 'new_v_s5_c_re': 'new_v', 'new_v_s5_c_im': 'new_v', 'new_v_s5_d': 'new_v', 'new_v_s5_w_glu': 'new_v', 'new_v_mla_w_in': 'new_v', 'new_v_mla_q_lora_norm': 'new_v', 'new_v_mla_kv_lora_norm': 'new_v', 'new_v_mla_w_uq': 'new_v', 'new_v_mla_w_ukv': 'new_v', 'new_v_mla_q_nope_norm': 'new_v', 'new_v_mla_k_nope_norm': 'new_v', 'new_v_mla_q_rope_norm': 'new_v', 'new_v_mla_k_rope_norm': 'new_v'}


def _forward(args):
    return _fwd_reference(*[args[k] for k in FWD_PARAMS])


def _output_shape():
    out = _jax.eval_shape(lambda: _forward(_fwd_setup_inputs(0)))
    return out.shape, out.dtype

N_MICROBATCH = 1
ADAM_LR = 0.001
ADAM_B1 = 0.9
ADAM_B2 = 0.999
ADAM_EPS = 1e-08
ADAM_WD = 0.01
ADAM_STEP = 10
PER_EXAMPLE_BATCH_AXIS = {'x': 0, 'mem': 0, 'positions': 0, 'loss_target': 0}
SHARED_INPUTS = []
_WEIGHT_DTYPES = {'ln_gain': _jnp.float32, 'w_out': _jnp.float32, 'mem_norm': _jnp.float32, 'w_mem_kv': _jnp.float32, 'xq_norm': _jnp.float32, 'xk_norm': _jnp.float32, 's5_w_in': _jnp.float32, 's5_lambda_re': _jnp.float32, 's5_lambda_im': _jnp.float32, 's5_log_step': _jnp.float32, 's5_b_re': _jnp.float32, 's5_b_im': _jnp.float32, 's5_c_re': _jnp.float32, 's5_c_im': _jnp.float32, 's5_d': _jnp.float32, 's5_w_glu': _jnp.float32, 'mla_w_in': _jnp.float32, 'mla_q_lora_norm': _jnp.float32, 'mla_kv_lora_norm': _jnp.float32, 'mla_w_uq': _jnp.float32, 'mla_w_ukv': _jnp.float32, 'mla_q_nope_norm': _jnp.float32, 'mla_k_nope_norm': _jnp.float32, 'mla_q_rope_norm': _jnp.float32, 'mla_k_rope_norm': _jnp.float32}
MOMENT_SCALE = {'ln_gain': 8.312214e-01, 'w_out': 3.211596e-02, 'mem_norm': 1.511262e-02, 'w_mem_kv': 9.003662e-03, 'xq_norm': 1.053947e-01, 'xk_norm': 1.055171e-01, 's5_w_in': 3.157497e-02, 's5_lambda_re': 2.379912e-03, 's5_lambda_im': 2.181912e-03, 's5_log_step': 1.151114e+00, 's5_b_re': 1.540825e-03, 's5_b_im': 1.566182e-03, 's5_c_re': 2.180771e-03, 's5_c_im': 2.186878e-03, 's5_d': 4.042989e-01, 's5_w_glu': 6.329505e-02, 'mla_w_in': 2.424237e-02, 'mla_q_lora_norm': 2.741091e-02, 'mla_kv_lora_norm': 2.247514e-01, 'mla_w_uq': 1.257017e-02, 'mla_w_ukv': 1.640949e-02, 'mla_q_nope_norm': 1.659029e-01, 'mla_k_nope_norm': 1.655114e-01, 'mla_q_rope_norm': 1.564985e-01, 'mla_k_rope_norm': 1.575024e-01}


def _to_microbatches(a, axis):
    t = _jnp.moveaxis(a, axis, 0)
    t = t.reshape((N_MICROBATCH, t.shape[0] // N_MICROBATCH) + t.shape[1:])
    return _jnp.moveaxis(t, 1, axis + 1)


def setup_inputs(seed: int = 0) -> dict:
    inp = _fwd_setup_inputs(seed)
    key = _jax.random.fold_in(_jax.random.key(seed), 7919)
    shape, _ = _output_shape()
    out = dict(inp)
    out["loss_target"] = _jax.random.normal(_jax.random.fold_in(key, 0), shape, _jnp.float32)
    for i, name in enumerate(TWIN_WEIGHTS):
        w = inp[name].astype(_jnp.float32)
        if MOMENT_SCALE is None:
            s = _jnp.sqrt(_jnp.mean(_jnp.square(w)) + 1e-30)
        else:
            s = MOMENT_SCALE[name]
        km, kv = _jax.random.split(_jax.random.fold_in(key, i + 1))
        out[name] = w
        out["m_" + name] = s * _jax.random.normal(km, w.shape, _jnp.float32)
        out["v_" + name] = (s * s) * _jax.random.uniform(kv, w.shape, _jnp.float32, 0.5, 1.5)
    if N_MICROBATCH > 1:
        for name, axis in PER_EXAMPLE_BATCH_AXIS.items():
            out[name] = _to_microbatches(out[name], axis)
    return {'x': out['x'], 'mem': out['mem'], 'positions': out['positions'], 'ln_gain': out['ln_gain'], 'w_out': out['w_out'], 'mem_norm': out['mem_norm'], 'w_mem_kv': out['w_mem_kv'], 'xq_norm': out['xq_norm'], 'xk_norm': out['xk_norm'], 's5_w_in': out['s5_w_in'], 's5_lambda_re': out['s5_lambda_re'], 's5_lambda_im': out['s5_lambda_im'], 's5_log_step': out['s5_log_step'], 's5_b_re': out['s5_b_re'], 's5_b_im': out['s5_b_im'], 's5_c_re': out['s5_c_re'], 's5_c_im': out['s5_c_im'], 's5_d': out['s5_d'], 's5_w_glu': out['s5_w_glu'], 'mla_w_in': out['mla_w_in'], 'mla_q_lora_norm': out['mla_q_lora_norm'], 'mla_kv_lora_norm': out['mla_kv_lora_norm'], 'mla_w_uq': out['mla_w_uq'], 'mla_w_ukv': out['mla_w_ukv'], 'mla_q_nope_norm': out['mla_q_nope_norm'], 'mla_k_nope_norm': out['mla_k_nope_norm'], 'mla_q_rope_norm': out['mla_q_rope_norm'], 'mla_k_rope_norm': out['mla_k_rope_norm'], 'loss_target': out['loss_target'], 'm_ln_gain': out['m_ln_gain'], 'm_w_out': out['m_w_out'], 'm_mem_norm': out['m_mem_norm'], 'm_w_mem_kv': out['m_w_mem_kv'], 'm_xq_norm': out['m_xq_norm'], 'm_xk_norm': out['m_xk_norm'], 'm_s5_w_in': out['m_s5_w_in'], 'm_s5_lambda_re': out['m_s5_lambda_re'], 'm_s5_lambda_im': out['m_s5_lambda_im'], 'm_s5_log_step': out['m_s5_log_step'], 'm_s5_b_re': out['m_s5_b_re'], 'm_s5_b_im': out['m_s5_b_im'], 'm_s5_c_re': out['m_s5_c_re'], 'm_s5_c_im': out['m_s5_c_im'], 'm_s5_d': out['m_s5_d'], 'm_s5_w_glu': out['m_s5_w_glu'], 'm_mla_w_in': out['m_mla_w_in'], 'm_mla_q_lora_norm': out['m_mla_q_lora_norm'], 'm_mla_kv_lora_norm': out['m_mla_kv_lora_norm'], 'm_mla_w_uq': out['m_mla_w_uq'], 'm_mla_w_ukv': out['m_mla_w_ukv'], 'm_mla_q_nope_norm': out['m_mla_q_nope_norm'], 'm_mla_k_nope_norm': out['m_mla_k_nope_norm'], 'm_mla_q_rope_norm': out['m_mla_q_rope_norm'], 'm_mla_k_rope_norm': out['m_mla_k_rope_norm'], 'v_ln_gain': out['v_ln_gain'], 'v_w_out': out['v_w_out'], 'v_mem_norm': out['v_mem_norm'], 'v_w_mem_kv': out['v_w_mem_kv'], 'v_xq_norm': out['v_xq_norm'], 'v_xk_norm': out['v_xk_norm'], 'v_s5_w_in': out['v_s5_w_in'], 'v_s5_lambda_re': out['v_s5_lambda_re'], 'v_s5_lambda_im': out['v_s5_lambda_im'], 'v_s5_log_step': out['v_s5_log_step'], 'v_s5_b_re': out['v_s5_b_re'], 'v_s5_b_im': out['v_s5_b_im'], 'v_s5_c_re': out['v_s5_c_re'], 'v_s5_c_im': out['v_s5_c_im'], 'v_s5_d': out['v_s5_d'], 'v_s5_w_glu': out['v_s5_w_glu'], 'v_mla_w_in': out['v_mla_w_in'], 'v_mla_q_lora_norm': out['v_mla_q_lora_norm'], 'v_mla_kv_lora_norm': out['v_mla_kv_lora_norm'], 'v_mla_w_uq': out['v_mla_w_uq'], 'v_mla_w_ukv': out['v_mla_w_ukv'], 'v_mla_q_nope_norm': out['v_mla_q_nope_norm'], 'v_mla_k_nope_norm': out['v_mla_k_nope_norm'], 'v_mla_q_rope_norm': out['v_mla_q_rope_norm'], 'v_mla_k_rope_norm': out['v_mla_k_rope_norm']}


def _loss(weights, diff, rest, loss_target):
    with _jax.named_scope("forward"):
        args = {**rest, TWIN_DIFF_INPUT: diff, **{k: w.astype(_WEIGHT_DTYPES[k]) for k, w in weights.items()}}
        y = _forward(args)
    with _jax.named_scope("loss_head"):
        err = _jnp.square(y.astype(_jnp.float32) - loss_target)
        return 0.5 * _jnp.sum(_jnp.mean(err, axis=-1)) if err.ndim else 0.5 * err


def _adamw(w, g, m, v):
    m = ADAM_B1 * m + (1.0 - ADAM_B1) * g
    v = ADAM_B2 * v + (1.0 - ADAM_B2) * _jnp.square(g)
    m_hat = m / (1.0 - ADAM_B1 ** ADAM_STEP)
    v_hat = v / (1.0 - ADAM_B2 ** ADAM_STEP)
    delta = -ADAM_LR * (m_hat / (_jnp.sqrt(v_hat) + ADAM_EPS) + ADAM_WD * w)
    return delta, m, v


def reference(x, mem, positions, ln_gain, w_out, mem_norm, w_mem_kv, xq_norm, xk_norm, s5_w_in, s5_lambda_re, s5_lambda_im, s5_log_step, s5_b_re, s5_b_im, s5_c_re, s5_c_im, s5_d, s5_w_glu, mla_w_in, mla_q_lora_norm, mla_kv_lora_norm, mla_w_uq, mla_w_ukv, mla_q_nope_norm, mla_k_nope_norm, mla_q_rope_norm, mla_k_rope_norm, loss_target, m_ln_gain, m_w_out, m_mem_norm, m_w_mem_kv, m_xq_norm, m_xk_norm, m_s5_w_in, m_s5_lambda_re, m_s5_lambda_im, m_s5_log_step, m_s5_b_re, m_s5_b_im, m_s5_c_re, m_s5_c_im, m_s5_d, m_s5_w_glu, m_mla_w_in, m_mla_q_lora_norm, m_mla_kv_lora_norm, m_mla_w_uq, m_mla_w_ukv, m_mla_q_nope_norm, m_mla_k_nope_norm, m_mla_q_rope_norm, m_mla_k_rope_norm, v_ln_gain, v_w_out, v_mem_norm, v_w_mem_kv, v_xq_norm, v_xk_norm, v_s5_w_in, v_s5_lambda_re, v_s5_lambda_im, v_s5_log_step, v_s5_b_re, v_s5_b_im, v_s5_c_re, v_s5_c_im, v_s5_d, v_s5_w_glu, v_mla_w_in, v_mla_q_lora_norm, v_mla_kv_lora_norm, v_mla_w_uq, v_mla_w_ukv, v_mla_q_nope_norm, v_mla_k_nope_norm, v_mla_q_rope_norm, v_mla_k_rope_norm):
    given = dict(x=x, mem=mem, positions=positions, ln_gain=ln_gain, w_out=w_out, mem_norm=mem_norm, w_mem_kv=w_mem_kv, xq_norm=xq_norm, xk_norm=xk_norm, s5_w_in=s5_w_in, s5_lambda_re=s5_lambda_re, s5_lambda_im=s5_lambda_im, s5_log_step=s5_log_step, s5_b_re=s5_b_re, s5_b_im=s5_b_im, s5_c_re=s5_c_re, s5_c_im=s5_c_im, s5_d=s5_d, s5_w_glu=s5_w_glu, mla_w_in=mla_w_in, mla_q_lora_norm=mla_q_lora_norm, mla_kv_lora_norm=mla_kv_lora_norm, mla_w_uq=mla_w_uq, mla_w_ukv=mla_w_ukv, mla_q_nope_norm=mla_q_nope_norm, mla_k_nope_norm=mla_k_nope_norm, mla_q_rope_norm=mla_q_rope_norm, mla_k_rope_norm=mla_k_rope_norm, loss_target=loss_target, m_ln_gain=m_ln_gain, m_w_out=m_w_out, m_mem_norm=m_mem_norm, m_w_mem_kv=m_w_mem_kv, m_xq_norm=m_xq_norm, m_xk_norm=m_xk_norm, m_s5_w_in=m_s5_w_in, m_s5_lambda_re=m_s5_lambda_re, m_s5_lambda_im=m_s5_lambda_im, m_s5_log_step=m_s5_log_step, m_s5_b_re=m_s5_b_re, m_s5_b_im=m_s5_b_im, m_s5_c_re=m_s5_c_re, m_s5_c_im=m_s5_c_im, m_s5_d=m_s5_d, m_s5_w_glu=m_s5_w_glu, m_mla_w_in=m_mla_w_in, m_mla_q_lora_norm=m_mla_q_lora_norm, m_mla_kv_lora_norm=m_mla_kv_lora_norm, m_mla_w_uq=m_mla_w_uq, m_mla_w_ukv=m_mla_w_ukv, m_mla_q_nope_norm=m_mla_q_nope_norm, m_mla_k_nope_norm=m_mla_k_nope_norm, m_mla_q_rope_norm=m_mla_q_rope_norm, m_mla_k_rope_norm=m_mla_k_rope_norm, v_ln_gain=v_ln_gain, v_w_out=v_w_out, v_mem_norm=v_mem_norm, v_w_mem_kv=v_w_mem_kv, v_xq_norm=v_xq_norm, v_xk_norm=v_xk_norm, v_s5_w_in=v_s5_w_in, v_s5_lambda_re=v_s5_lambda_re, v_s5_lambda_im=v_s5_lambda_im, v_s5_log_step=v_s5_log_step, v_s5_b_re=v_s5_b_re, v_s5_b_im=v_s5_b_im, v_s5_c_re=v_s5_c_re, v_s5_c_im=v_s5_c_im, v_s5_d=v_s5_d, v_s5_w_glu=v_s5_w_glu, v_mla_w_in=v_mla_w_in, v_mla_q_lora_norm=v_mla_q_lora_norm, v_mla_kv_lora_norm=v_mla_kv_lora_norm, v_mla_w_uq=v_mla_w_uq, v_mla_w_ukv=v_mla_w_ukv, v_mla_q_nope_norm=v_mla_q_nope_norm, v_mla_k_nope_norm=v_mla_k_nope_norm, v_mla_q_rope_norm=v_mla_q_rope_norm, v_mla_k_rope_norm=v_mla_k_rope_norm)
    weights = {n: given[n] for n in TWIN_WEIGHTS}
    shared = {n: given[n] for n in SHARED_INPUTS}
    per_example = {n: given[n] for n in ['x', 'mem', 'positions']}
    grad_fn = _jax.value_and_grad(_loss, argnums=(0, 1))

    def one_microbatch(ex, loss_target):
        ex = dict(ex)
        diff = ex.pop(TWIN_DIFF_INPUT)
        return grad_fn(weights, diff, {**shared, **ex}, loss_target)

    if N_MICROBATCH == 1:
        loss, (grad_w, grad_x) = one_microbatch(per_example, given["loss_target"])
    else:
        def body(carry, xs):
            loss_sum, grad_sum = carry
            l_k, (gw_k, gx_k) = one_microbatch(xs[0], xs[1])
            with _jax.named_scope("update"):
                return (loss_sum + l_k, _jax.tree.map(_jnp.add, grad_sum, gw_k)), gx_k

        init = (_jnp.zeros((), _jnp.float32), _jax.tree.map(_jnp.zeros_like, weights))
        (loss, grad_w), grad_x = _jax.lax.scan(body, init, (per_example, given["loss_target"]))
    with _jax.named_scope("update"):
        delta_w, new_m, new_v = {}, {}, {}
        for n in TWIN_WEIGHTS:
            delta_w[n], new_m[n], new_v[n] = _adamw(weights[n], grad_w[n], given["m_" + n], given["v_" + n])
    return (loss, grad_x, *[grad_w[n] for n in TWIN_WEIGHTS], *[delta_w[n] for n in TWIN_WEIGHTS],
            *[new_m[n] for n in TWIN_WEIGHTS], *[new_v[n] for n in TWIN_WEIGHTS])
```

```python
import functools
import math

import numpy as np
import jax
import jax.numpy as jnp
from jax import lax
from jax.experimental import pallas as pl
from jax.experimental.pallas import tpu as pltpu

F32 = jnp.float32
BF16 = jnp.bfloat16
EPS = 1e-6
NEG = float(np.finfo(np.float32).min)
MESH = pl.DeviceIdType.MESH

N_DEV = 8
D_MODEL = 1024
MEM_LEN = 256
XQ = 512
PRIM = 1536
BRANCH = 2048
X_HEADS = 4
HD = 128
S5_G = 96
S5_P = 64
S5_C = 16
S5_GB = 8
S5_W = S5_GB * S5_P
MLA_H = 12
ROPE = 64
Q_LORA = 512
KV_LORA = 256
ROPE_THETA = 10000.0

ADAM_LR = 0.001
ADAM_B1 = 0.9
ADAM_B2 = 0.999
ADAM_EPS = 1e-08
ADAM_WD = 0.01
ADAM_STEP = 10

VMEM_LIMIT = 56 * 1024 * 1024


def _dot(a, b):
    return jnp.dot(a, b, preferred_element_type=F32)


def _dot_nt(a, b):
    return lax.dot_general(a, b, (((1,), (1,)), ((), ())), preferred_element_type=F32)


def _dot_tn(a, b):
    return lax.dot_general(a, b, (((0,), (0,)), ((), ())), preferred_element_type=F32)


@jax.custom_vjp
def _mm(a, b):
    return _dot(a.astype(BF16), b.astype(BF16))


def _mm_fwd(a, b):
    return _mm(a, b), (a, b)


def _mm_bwd(res, g):
    a, b = res
    gb = g.astype(BF16)
    return _dot_nt(gb, b.astype(BF16)).astype(a.dtype), _dot_tn(a.astype(BF16), gb).astype(b.dtype)


_mm.defvjp(_mm_fwd, _mm_bwd)


@jax.custom_vjp
def _mm_nt(a, b):
    return _dot_nt(a.astype(BF16), b.astype(BF16))


def _mm_nt_fwd(a, b):
    return _mm_nt(a, b), (a, b)


def _mm_nt_bwd(res, g):
    a, b = res
    gb = g.astype(BF16)
    return _dot(gb, b.astype(BF16)).astype(a.dtype), _dot_tn(gb, a.astype(BF16)).astype(b.dtype)


_mm_nt.defvjp(_mm_nt_fwd, _mm_nt_bwd)


@jax.custom_vjp
def _softmax(s):
    m = jnp.max(s, axis=-1, keepdims=True)
    e = jnp.exp(s - m)
    return e / jnp.sum(e, axis=-1, keepdims=True)


def _softmax_fwd(s):
    p = _softmax(s)
    return p, p


def _softmax_bwd(p, g):
    return (p * (g - jnp.sum(p * g, axis=-1, keepdims=True)),)


_softmax.defvjp(_softmax_fwd, _softmax_bwd)


def _rms(x, g, n):
    ms = jnp.sum(x * x, axis=-1, keepdims=True) * (1.0 / n)
    return x * lax.rsqrt(ms + EPS) * g


def _sigmoid(x):
    return 1.0 / (1.0 + jnp.exp(-x))


def _silu(x):
    return x * _sigmoid(x)


def _gelu(x):
    c = math.sqrt(2.0 / math.pi)
    return 0.5 * x * (1.0 + jnp.tanh(c * (x + 0.044715 * (x * x * x))))


@jax.custom_vjp
def _rot(x, c, s1, s2):
    return x * c + pltpu.roll(x, 96, 1) * s1 + pltpu.roll(x, 32, 1) * s2


def _rot_fwd(x, c, s1, s2):
    return _rot(x, c, s1, s2), (c, s1, s2)


def _rot_bwd(res, g):
    c, s1, s2 = res
    dx = g * c + pltpu.roll(g * s1, 32, 1) + pltpu.roll(g * s2, 96, 1)
    return dx, jnp.zeros_like(c), jnp.zeros_like(s1), jnp.zeros_like(s2)


_rot.defvjp(_rot_fwd, _rot_bwd)


def _mem_attn(xq, k, v, gq):
    outs = []
    for h in range(X_HEADS):
        sl = slice(HD * h, HD * (h + 1))
        q = _rms(xq[:, sl], gq, HD)
        p = _softmax(_mm_nt(q, k[:, sl]) * (HD ** -0.5))
        outs.append(_mm(p, v[:, sl]))
    return jnp.concatenate(outs, axis=-1)


def _merge(mix, xq, gate, k, v, gq):
    return jnp.concatenate([mix, _mem_attn(xq, k, v, gq)], axis=-1) * _silu(gate)


def _q_post(q, gqn, gqr, c, s1, s2):
    pieces = []
    for h in range(MLA_H):
        pieces.append(_rms(q[:, HD * h:HD * (h + 1)], gqn, HD))
        pieces.append(_rot(_rms(q[:, PRIM + HD * h:PRIM + HD * (h + 1)], gqr, ROPE), c, s1, s2))
    return jnp.concatenate(pieces, axis=-1)


def _kv_post(kv, krp, gkn, gkr, c, s1, s2):
    kr = _rot(_rms(krp, gkr, ROPE), c, s1, s2)
    pieces = []
    for h in range(MLA_H):
        pieces.append(_rms(kv[:, HD * h:HD * (h + 1)], gkn, HD))
        pieces.append(kr)
    return jnp.concatenate(pieces, axis=-1), kv[:, PRIM:]


def _rowwise(name, fn, ins, outs, nblk, sub=1):
    n_in = len(ins)

    def spec(kind, shape):
        if kind == 'r':
            return pl.BlockSpec((shape[0] // nblk, shape[1]), lambda i: (i, 0))
        if kind == 'p':
            return pl.BlockSpec((shape[0] // sub, shape[1] // 8), lambda i: (i % sub, i // sub))
        zeros = (0,) * len(shape)
        return pl.BlockSpec(tuple(shape), lambda i: zeros)

    def body(*refs):
        i = pl.program_id(0)
        res = fn(*[r[...] for r in refs[:n_in]])
        for (kind, _, _), ref, val in zip(outs, refs[n_in:], res):
            if kind == 'a':
                @pl.when(i == 0)
                def _():
                    ref[...] = jnp.zeros_like(ref)
                ref[...] += val.astype(ref.dtype)
            else:
                ref[...] = val.astype(ref.dtype)

    res = pl.pallas_call(
        body, name=name, grid=(nblk,),
        in_specs=[spec(k, a.shape) for k, a in ins],
        out_specs=[spec(k, s) for k, s, _ in outs],
        out_shape=[jax.ShapeDtypeStruct(tuple(s), d) for _, s, d in outs],
        compiler_params=pltpu.CompilerParams(dimension_semantics=("arbitrary",), vmem_limit_bytes=VMEM_LIMIT),
    )(*[a for _, a in ins])
    return res


def _matmul_tn(a, g, name, out_dtype=BF16):
    L, K = a.shape
    N = g.shape[1]
    tn = next(t for t in (512, 384, 256, 128) if N % t == 0)
    tl = min(512, L)
    nl = L // tl

    def body(a_ref, g_ref, o_ref, acc):
        l = pl.program_id(1)

        @pl.when(l == 0)
        def _():
            acc[...] = jnp.zeros_like(acc)

        acc[...] += _dot_tn(a_ref[...], g_ref[...])

        @pl.when(l == nl - 1)
        def _():
            o_ref[...] = acc[...].astype(o_ref.dtype)

    return pl.pallas_call(
        body, name=name, grid=(N // tn, nl),
        in_specs=[pl.BlockSpec((tl, K), lambda n, l: (l, 0)), pl.BlockSpec((tl, tn), lambda n, l: (l, n))],
        out_specs=pl.BlockSpec((K, tn), lambda n, l: (0, n)),
        out_shape=jax.ShapeDtypeStruct((K, N), out_dtype),
        scratch_shapes=[pltpu.VMEM((K, tn), F32)],
        compiler_params=pltpu.CompilerParams(dimension_semantics=("arbitrary", "arbitrary"),
                                             vmem_limit_bytes=VMEM_LIMIT),
    )(a, g)


def _all_gather(xs, name):
    R, C = xs.shape

    def body(x_ref, out_ref, send_sems, recv_sems, local_sem):
        x, y, c = lax.axis_index("x"), lax.axis_index("y"), lax.axis_index("c")
        me, sibling = (x, y, c), (x, y, 1 - c)
        chips = [(1 - x, y), (x, 1 - y), (1 - x, 1 - y)]

        def rows(px, py, pc):
            return out_ref.at[4 * px + 2 * py + pc]

        def copy(k, block, to, src=None):
            return pltpu.make_async_remote_copy(
                src_ref=rows(*block) if src is None else src, dst_ref=rows(*block),
                send_sem=send_sems.at[k], recv_sem=recv_sems.at[k], device_id=to, device_id_type=MESH)

        mine = pltpu.make_async_copy(x_ref, rows(*me), local_sem)
        mine.start()
        first = [copy(0, me, sibling, src=x_ref)]
        first += [copy(1 + j, me, (*chip, c), src=x_ref) for j, chip in enumerate(chips)]
        for cp in first:
            cp.start()
        passed = [copy(4 + j, (*chip, c), sibling) for j, chip in enumerate(chips)]
        for j, chip in enumerate(chips):
            copy(1 + j, (*chip, c), me).wait_recv()
            passed[j].start()
        copy(0, sibling, me).wait_recv()
        for j, chip in enumerate(chips):
            copy(4 + j, (*chip, 1 - c), me).wait_recv()
        for cp in first + passed:
            cp.wait_send()
        mine.wait()

    return pl.pallas_call(
        body, name=name,
        out_shape=jax.ShapeDtypeStruct((N_DEV, R, C), xs.dtype),
        in_specs=[pl.BlockSpec(memory_space=pl.ANY)],
        out_specs=pl.BlockSpec(memory_space=pl.ANY),
        scratch_shapes=[pltpu.SemaphoreType.DMA((7,)), pltpu.SemaphoreType.DMA((7,)), pltpu.SemaphoreType.DMA],
    )(xs)


def _all_to_all(send, name):
    _, R, C = send.shape
    flips = [(0, 0, 1), (1, 0, 0), (0, 1, 0), (1, 1, 0), (1, 0, 1), (0, 1, 1), (1, 1, 1)]

    def body(send_ref, recv_ref, send_sems, recv_sems, local_sem):
        x, y, c = lax.axis_index("x"), lax.axis_index("y"), lax.axis_index("c")
        me = 4 * x + 2 * y + c
        local = pltpu.make_async_copy(send_ref.at[me], recv_ref.at[me], local_sem)
        local.start()
        copies = []
        for k, (fx, fy, fc) in enumerate(flips):
            px = 1 - x if fx else x
            py = 1 - y if fy else y
            pc = 1 - c if fc else c
            cp = pltpu.make_async_remote_copy(
                src_ref=send_ref.at[4 * px + 2 * py + pc], dst_ref=recv_ref.at[me],
                send_sem=send_sems.at[k], recv_sem=recv_sems.at[k], device_id=(px, py, pc), device_id_type=MESH)
            cp.start()
            copies.append(cp)
        for cp in copies:
            cp.wait_recv()
        for cp in copies:
            cp.wait_send()
        local.wait()

    return pl.pallas_call(
        body, name=name,
        out_shape=jax.ShapeDtypeStruct((N_DEV, R, C), send.dtype),
        in_specs=[pl.BlockSpec(memory_space=pl.ANY)],
        out_specs=pl.BlockSpec(memory_space=pl.ANY),
        scratch_shapes=[pltpu.SemaphoreType.DMA((7,)), pltpu.SemaphoreType.DMA((7,)), pltpu.SemaphoreType.DMA],
    )(send)


def _sum_slots(recv, name, br):
    _, R, C = recv.shape

    def body(r_ref, o_ref):
        acc = r_ref[0].astype(F32)
        for d in range(1, N_DEV):
            acc = acc + r_ref[d].astype(F32)
        o_ref[...] = acc

    return pl.pallas_call(
        body, name=name, grid=(R // br,),
        in_specs=[pl.BlockSpec((N_DEV, br, C), lambda i: (0, i, 0))],
        out_specs=pl.BlockSpec((br, C), lambda i: (i, 0)),
        out_shape=jax.ShapeDtypeStruct((R, C), F32),
        compiler_params=pltpu.CompilerParams(dimension_semantics=("arbitrary",)),
    )(recv)


def _adamw_vals(w, g, m, v):
    m2 = ADAM_B1 * m + (1.0 - ADAM_B1) * g
    v2 = ADAM_B2 * v + (1.0 - ADAM_B2) * (g * g)
    m_hat = m2 / (1.0 - ADAM_B1 ** ADAM_STEP)
    v_hat = v2 / (1.0 - ADAM_B2 ** ADAM_STEP)
    delta = -ADAM_LR * (m_hat / (jnp.sqrt(v_hat) + ADAM_EPS) + ADAM_WD * w)
    return delta, m2, v2


def _adamw(w, g, m, v, name):
    shape = w.shape
    C = shape[-1]
    R = int(np.prod(shape[:-1]))
    br = next((t for t in (256, 128, 64, 32, 16, 8) if R % t == 0), R)

    def body(w_ref, g_ref, m_ref, v_ref, d_ref, m2_ref, v2_ref):
        d, m2, v2 = _adamw_vals(w_ref[...], g_ref[...], m_ref[...], v_ref[...])
        d_ref[...] = d
        m2_ref[...] = m2
        v2_ref[...] = v2

    spec = pl.BlockSpec((br, C), lambda i: (i, 0))
    outs = pl.pallas_call(
        body, name=name, grid=(R // br,),
        in_specs=[spec] * 4, out_specs=[spec] * 3,
        out_shape=[jax.ShapeDtypeStruct((R, C), F32)] * 3,
        compiler_params=pltpu.CompilerParams(dimension_semantics=("arbitrary",)),
    )(*[a.reshape(R, C) for a in (w, g, m, v)])
    return tuple(o.reshape(shape) for o in outs)


def _small_update(gath, wp, mp, vp, name):
    _, R, C = gath.shape
    br = R // 3

    def body(g_ref, w_ref, m_ref, v_ref, go_ref, d_ref, m2_ref, v2_ref):
        g = g_ref[0]
        for d in range(1, N_DEV):
            g = g + g_ref[d]
        dl, m2, v2 = _adamw_vals(w_ref[...], g, m_ref[...], v_ref[...])
        go_ref[...] = g
        d_ref[...] = dl
        m2_ref[...] = m2
        v2_ref[...] = v2

    spec = pl.BlockSpec((br, C), lambda i: (i, 0))
    return pl.pallas_call(
        body, name=name, grid=(R // br,),
        in_specs=[pl.BlockSpec((N_DEV, br, C), lambda i: (0, i, 0)), spec, spec, spec],
        out_specs=[spec] * 4, out_shape=[jax.ShapeDtypeStruct((R, C), F32)] * 4,
        compiler_params=pltpu.CompilerParams(dimension_semantics=("arbitrary",)),
    )(gath, wp, mp, vp)


def _s5_param_fn(lr, li, ls, btr, bti):
    step = jnp.exp(ls)
    er = jnp.exp(lr * step)
    ang = li * step
    ar = er * jnp.cos(ang)
    ai = er * jnp.sin(ang)
    nr = ar - 1.0
    den = lr * lr + li * li
    fr = (nr * lr + ai * li) / den
    fi = (ai * lr - nr * li) / den
    return ar, ai, fr * btr - fi * bti, fr * bti + fi * btr


def _s5_params(lr, li, ls, btr, bti):
    def body(lr_ref, li_ref, ls_ref, br_ref, bi_ref, ar_ref, ai_ref, bbr_ref, bbi_ref):
        ar, ai, bbr, bbi = _s5_param_fn(lr_ref[...], li_ref[...], ls_ref[...], br_ref[...], bi_ref[...])
        ar_ref[...] = ar
        ai_ref[...] = ai
        bbr_ref[...] = bbr
        bbi_ref[...] = bbi

    sd = jax.ShapeDtypeStruct
    return pl.pallas_call(
        body, name="s5_params",
        out_shape=[sd(lr.shape, F32), sd(lr.shape, F32), sd(btr.shape, F32), sd(btr.shape, F32)],
    )(lr, li, ls, btr, bti)


def _s5_params_bwd(lr, li, ls, btr, bti, dar, dai, dbbr, dbbi):
    def body(lr_ref, li_ref, ls_ref, br_ref, bi_ref, dar_ref, dai_ref, dbbr_ref, dbbi_ref,
             dlr_ref, dli_ref, dls_ref, dbr_ref, dbi_ref):
        _, vjp = jax.vjp(_s5_param_fn, lr_ref[...], li_ref[...], ls_ref[...], br_ref[...], bi_ref[...])
        dlr, dli, dls, dbr, dbi = vjp((dar_ref[...], dai_ref[...], dbbr_ref[...], dbbi_ref[...]))
        dlr_ref[...] = dlr
        dli_ref[...] = dli
        dls_ref[...] = dls
        dbr_ref[...] = dbr
        dbi_ref[...] = dbi

    sd = jax.ShapeDtypeStruct
    return pl.pallas_call(
        body, name="s5_params_bwd",
        out_shape=[sd(lr.shape, F32), sd(lr.shape, F32), sd(ls.shape, F32), sd(btr.shape, F32), sd(btr.shape, F32)],
    )(lr, li, ls, btr, bti, dar, dai, dbbr, dbbi)


def _cpow(ar, ai, n):
    assert n & (n - 1) == 0
    while n > 1:
        ar, ai = ar * ar - ai * ai, 2.0 * ar * ai
        n //= 2
    return ar, ai


def _scan(st, cr, ci, init, nk, reverse, store, prev=None):
    W = S5_W

    def step(j, carry):
        k = nk - 1 - j if reverse else j
        rows = pl.ds(pl.multiple_of(k * 8, 8), 8)
        sr, si = carry[0], carry[1]
        nsr = cr * sr - ci * si + st[rows, 0:W]
        nsi = cr * si + ci * sr + st[rows, W:2 * W]
        if store:
            st[rows, 0:W] = nsr
            st[rows, W:2 * W] = nsi
        if prev is None:
            return nsr, nsi
        prows = pl.ds(pl.multiple_of(jnp.maximum(k - 1, 0) * 8, 8), 8)
        w = jnp.where(k > 0, 1.0, 0.0).astype(F32)
        pr = prev[prows, 0:W] * w
        pi = prev[prows, W:2 * W] * w
        return nsr, nsi, carry[2] + nsr * pr + nsi * pi, carry[3] + nsi * pr - nsr * pi

    return lax.fori_loop(0, nk, step, init, unroll=2)


def _chain(fin, fr, fi, pr, pi, reverse):
    W = S5_W
    fin[:, 0:W] = fr
    fin[:, W:2 * W] = fi
    rowid = lax.broadcasted_iota(jnp.int32, (8, W), 0)
    cr = jnp.zeros((1, W), F32)
    ci = jnp.zeros((1, W), F32)
    init_r = jnp.zeros((8, W), F32)
    init_i = jnp.zeros((8, W), F32)
    for s in (range(7, -1, -1) if reverse else range(8)):
        init_r = jnp.where(rowid == s, cr, init_r)
        init_i = jnp.where(rowid == s, ci, init_i)
        lr = fin[s:s + 1, 0:W]
        li = fin[s:s + 1, W:2 * W]
        cr, ci = lr + pr * cr - pi * ci, li + pr * ci + pi * cr
    return init_r, init_i


def _full_scan(st, fin, ar, ai, nk, reverse, prev=None):
    W = S5_W
    cr = jnp.broadcast_to(ar, (8, W))
    ci = jnp.broadcast_to(-ai if reverse else ai, (8, W))
    z = jnp.zeros((8, W), F32)
    fr, fi = _scan(st, cr, ci, (z, z), nk, reverse, store=False)
    pr, pi = _cpow(ar, -ai if reverse else ai, nk)
    init = _chain(fin, fr, fi, pr, pi, reverse)
    if prev is None:
        return _scan(st, cr, ci, init, nk, reverse, store=True)
    return _scan(st, cr, ci, init + (z, z), nk, reverse, store=True, prev=prev)


def _s5_specs(L):
    W2 = 2 * S5_W
    GC = S5_GB * S5_C
    col = pl.BlockSpec((L, GC), lambda g: (0, g))
    vec = pl.BlockSpec((1, GC), lambda g: (0, g))
    avec = pl.BlockSpec((1, S5_W), lambda g: (0, g))
    bmat = pl.BlockSpec((None, GC, W2), lambda g: (g, 0, 0))
    cmat = pl.BlockSpec((None, W2, GC), lambda g: (g, 0, 0))
    return col, vec, avec, bmat, cmat


def _s5_fwd(u, bm, cm, ar, ai, dvec):
    L = u.shape[0]
    nk = L // 8
    col, vec, avec, bmat, cmat = _s5_specs(L)

    def body(u_ref, b_ref, c_ref, ar_ref, ai_ref, d_ref, y_ref, st, fin):
        for r in range(8):
            rows = slice(r * nk, (r + 1) * nk)
            st[rows, :] = _dot(u_ref[rows, :].astype(BF16), b_ref[...])
        _full_scan(st, fin, ar_ref[...], ai_ref[...], nk, reverse=False)
        for r in range(8):
            rows = slice(r * nk, (r + 1) * nk)
            y_ref[rows, :] = _dot(st[rows, :].astype(BF16), c_ref[...]) + d_ref[...] * u_ref[rows, :]

    return pl.pallas_call(
        body, name="s5_fwd", grid=(S5_G // S5_GB,),
        in_specs=[col, bmat, cmat, avec, avec, vec], out_specs=col,
        out_shape=jax.ShapeDtypeStruct(u.shape, F32),
        scratch_shapes=[pltpu.VMEM((L, 2 * S5_W), F32), pltpu.VMEM((8, 2 * S5_W), F32)],
        compiler_params=pltpu.CompilerParams(dimension_semantics=("arbitrary",), vmem_limit_bytes=VMEM_LIMIT),
    )(u, bm, cm, ar, ai, dvec)


def _s5_bwd(u, dy, bm, bmt, cmt, ar, ai, dvec, mask, rmat):
    L = u.shape[0]
    nk = L // 8
    W = S5_W
    GC = S5_GB * S5_C
    col, vec, avec, bmat, cmat = _s5_specs(L)
    hi = lax.Precision.HIGHEST

    def body(u_ref, dy_ref, b_ref, bt_ref, ct_ref, ar_ref, ai_ref, d_ref, mask_ref, r_ref,
             du_ref, db_ref, dc_ref, dd_ref, dar_ref, dai_ref, sa, sb, fin):
        ar = ar_ref[...]
        ai = ai_ref[...]
        for r in range(8):
            rows = slice(r * nk, (r + 1) * nk)
            sa[rows, :] = _dot(u_ref[rows, :].astype(BF16), b_ref[...])
            sb[rows, :] = _dot(dy_ref[rows, :].astype(BF16), ct_ref[...])
        _full_scan(sa, fin, ar, ai, nk, reverse=False)
        gr, gi, accr, acci = _full_scan(sb, fin, ar, ai, nk, reverse=True, prev=sa)
        rowid = lax.broadcasted_iota(jnp.int32, (8, W), 0)
        last = pl.ds((nk - 1) * 8, 8)
        pr = jnp.where(rowid == 0, 0.0, pltpu.roll(sa[last, 0:W], 1, 0))
        pi = jnp.where(rowid == 0, 0.0, pltpu.roll(sa[last, W:2 * W], 1, 0))
        accr = accr + gr * pr + gi * pi
        acci = acci + gi * pr - gr * pi
        dar_ref[...] = jnp.sum(accr, axis=0, keepdims=True)
        dai_ref[...] = jnp.sum(acci, axis=0, keepdims=True)
        dbf = jnp.zeros((GC, 2 * W), F32)
        dcf = jnp.zeros((GC, 2 * W), F32)
        dd = jnp.zeros((1, GC), F32)
        for r in range(8):
            rows = slice(r * nk, (r + 1) * nk)
            ub = u_ref[rows, :]
            dyb = dy_ref[rows, :]
            gb = sb[rows, :].astype(BF16)
            du_ref[rows, :] = _dot(gb, bt_ref[...]) + d_ref[...] * dyb
            dbf = dbf + _dot_tn(ub.astype(BF16), gb)
            dcf = dcf + _dot_tn(dyb.astype(BF16), sa[rows, :].astype(BF16))
            dd = dd + jnp.sum(dyb * ub, axis=0, keepdims=True)
        db_ref[...] = jnp.dot(dbf * mask_ref[...], r_ref[...], precision=hi, preferred_element_type=F32)
        dc_ref[...] = jnp.dot(dcf * mask_ref[...], r_ref[...], precision=hi, preferred_element_type=F32)
        dd_ref[...] = dd

    cmp_spec = pl.BlockSpec((GC, 2 * S5_P), lambda g: (g, 0))
    whole = lambda shape: pl.BlockSpec(shape, lambda g: (0, 0))
    sd = jax.ShapeDtypeStruct
    return pl.pallas_call(
        body, name="s5_bwd", grid=(S5_G // S5_GB,),
        in_specs=[col, col, bmat, cmat, bmat, avec, avec, vec, whole(mask.shape), whole(rmat.shape)],
        out_specs=[col, cmp_spec, cmp_spec, vec, avec, avec],
        out_shape=[sd(u.shape, F32), sd((S5_G * S5_C, 2 * S5_P), F32), sd((S5_G * S5_C, 2 * S5_P), F32),
                   sd((1, PRIM), F32), sd((1, S5_G * S5_P), F32), sd((1, S5_G * S5_P), F32)],
        scratch_shapes=[pltpu.VMEM((L, 2 * W), F32), pltpu.VMEM((L, 2 * W), F32), pltpu.VMEM((8, 2 * W), F32)],
        compiler_params=pltpu.CompilerParams(dimension_semantics=("arbitrary",), vmem_limit_bytes=VMEM_LIMIT),
    )(u, dy, bm, bmt, cmt, ar, ai, dvec, mask, rmat)


def _s5_mats(bbr, bbi, cre, cim):
    nb = S5_G // S5_GB
    eye = jnp.eye(S5_GB, dtype=F32)
    bb = jnp.stack([bbr, bbi], axis=2).reshape(nb, S5_GB, S5_C, 2, S5_P)
    bm = jnp.einsum('ngcrp,gh->ngcrhp', bb, eye).reshape(nb, S5_GB * S5_C, 2 * S5_W)
    cc = jnp.stack([cre, -cim], axis=2).reshape(nb, S5_GB, S5_C, 2, S5_P)
    cmt = jnp.einsum('ngcrp,gh->ngcrhp', cc, eye).reshape(nb, S5_GB * S5_C, 2 * S5_W)
    return (bm.astype(BF16), jnp.swapaxes(bm, 1, 2).astype(BF16),
            jnp.swapaxes(cmt, 1, 2).astype(BF16), cmt.astype(BF16))


def _s5_compact_consts():
    g_row = np.arange(S5_GB * S5_C) // S5_C
    col = np.arange(2 * S5_W)
    g_col = (col % S5_W) // S5_P
    mask = (g_row[:, None] == g_col[None, :]).astype(np.float32)
    tgt = (col // S5_W) * S5_P + col % S5_P
    rmat = (tgt[:, None] == np.arange(2 * S5_P)[None, :]).astype(np.float32)
    return jnp.asarray(mask), jnp.asarray(rmat)


def _attn_probs(q_ref, k_ref, qb, bq, L, scale):
    s = _dot_nt(q_ref[...], k_ref[...]) * scale
    qpos = qb * bq + lax.broadcasted_iota(jnp.int32, (bq, L), 0)
    kpos = lax.broadcasted_iota(jnp.int32, (bq, L), 1)
    s = jnp.where(kpos <= qpos, s, NEG)
    m = jnp.max(s, axis=-1, keepdims=True)
    e = jnp.exp(s - m)
    return e / jnp.sum(e, axis=-1, keepdims=True)


def _attn_fwd(qp, kp, v, scale):
    L = qp.shape[0]
    bq = min(256, L)

    def body(q_ref, k_ref, v_ref, o_ref):
        p = _attn_probs(q_ref, k_ref, pl.program_id(1), bq, L, scale)
        o_ref[...] = _dot(p.astype(BF16), v_ref[...])

    return pl.pallas_call(
        body, name="mla_attn_fwd", grid=(MLA_H, L // bq),
        in_specs=[pl.BlockSpec((bq, 2 * HD), lambda h, i: (i, h)), pl.BlockSpec((L, 2 * HD), lambda h, i: (0, h)),
                  pl.BlockSpec((L, HD), lambda h, i: (0, h))],
        out_specs=pl.BlockSpec((bq, HD), lambda h, i: (i, h)),
        out_shape=jax.ShapeDtypeStruct((L, MLA_H * HD), F32),
        compiler_params=pltpu.CompilerParams(dimension_semantics=("arbitrary", "arbitrary"),
                                             vmem_limit_bytes=VMEM_LIMIT),
    )(qp, kp, v)


def _attn_bwd(qp, kp, v, do, scale):
    L = qp.shape[0]
    bq = min(256, L)
    nq = L // bq

    def body(q_ref, k_ref, v_ref, do_ref, dq_ref, dk_ref, dv_ref, dk_acc, dv_acc):
        qb = pl.program_id(1)

        @pl.when(qb == 0)
        def _():
            dk_acc[...] = jnp.zeros_like(dk_acc)
            dv_acc[...] = jnp.zeros_like(dv_acc)

        p = _attn_probs(q_ref, k_ref, qb, bq, L, scale)
        dob = do_ref[...].astype(BF16)
        dv_acc[...] += _dot_tn(p.astype(BF16), dob)
        dp = _dot_nt(dob, v_ref[...])
        ds = (p * (dp - jnp.sum(p * dp, axis=-1, keepdims=True)) * scale).astype(BF16)
        dq_ref[...] = _dot(ds, k_ref[...])
        dk_acc[...] += _dot_tn(ds, q_ref[...])

        @pl.when(qb == nq - 1)
        def _():
            dk_ref[...] = dk_acc[...]
            dv_ref[...] = dv_acc[...]

    sd = jax.ShapeDtypeStruct
    return pl.pallas_call(
        body, name="mla_attn_bwd", grid=(MLA_H, nq),
        in_specs=[pl.BlockSpec((bq, 2 * HD), lambda h, i: (i, h)), pl.BlockSpec((L, 2 * HD), lambda h, i: (0, h)),
                  pl.BlockSpec((L, HD), lambda h, i: (0, h)), pl.BlockSpec((bq, HD), lambda h, i: (i, h))],
        out_specs=[pl.BlockSpec((bq, 2 * HD), lambda h, i: (i, h)), pl.BlockSpec((L, 2 * HD), lambda h, i: (0, h)),
                   pl.BlockSpec((L, HD), lambda h, i: (0, h))],
        out_shape=[sd((L, MLA_H * 2 * HD), F32), sd((L, MLA_H * 2 * HD), F32), sd((L, MLA_H * HD), F32)],
        scratch_shapes=[pltpu.VMEM((L, 2 * HD), F32), pltpu.VMEM((L, HD), F32)],
        compiler_params=pltpu.CompilerParams(dimension_semantics=("arbitrary", "arbitrary"),
                                             vmem_limit_bytes=VMEM_LIMIT),
    )(qp, kp, v, do)


def _kv_fn(mem, gm, w, gk):
    kv = _mm(_rms(mem, gm, D_MODEL), w)
    k = jnp.concatenate([_rms(kv[:, HD * h:HD * (h + 1)], gk, HD) for h in range(X_HEADS)], axis=-1)
    return k, kv[:, XQ:]


def _kv_prep(mem, gm, w, gk, name):
    def fn(mem, gm, w, gk):
        return _kv_fn(mem, gm, w, gk)
    M = mem.shape[0]
    return _rowwise(name, fn, [('c', mem), ('c', gm), ('c', w), ('c', gk)],
                    [('c', (M, XQ), F32), ('c', (M, XQ), F32)], 1)


def _kv_prep_bwd(mem, gm, w, gk, dk, dv, name):
    def fn(mem, gm, w, gk, dk, dv):
        _, vjp = jax.vjp(lambda a, b, c: _kv_fn(mem, a, b, c), gm, w, gk)
        return vjp((dk, dv))
    return _rowwise(name, fn, [('c', mem), ('c', gm), ('c', w), ('c', gk), ('c', dk), ('c', dv)],
                    [('c', gm.shape, F32), ('c', w.shape, BF16), ('c', gk.shape, F32)], 1)


def _forward_merge(x, mix, mix_kind, xq, gate, k, v, gq, wout, name, nblk, sub):
    def fn(x, mix, xq, gate, k, v, gq, wout):
        o = _merge(mix, xq, gate, k, v, gq)
        return (x + _dot(o.astype(BF16), wout),)
    L = x.shape[0]
    return _rowwise(name, fn, [('r', x), (mix_kind, mix), ('r', xq), ('r', gate), ('c', k), ('c', v), ('c', gq),
                               ('c', wout)], [('r', (L, D_MODEL), F32)], nblk, sub)[0]


def _backward_merge(dx, mix, mix_kind, xq, gate, k, v, gq, wout, name, nblk, sub):
    def fn(dx, mix, xq, gate, k, v, gq, wout):
        g16 = dx.astype(BF16)
        do = _dot_nt(g16, wout)
        o, vjp = jax.vjp(_merge, mix, xq, gate, k, v, gq)
        dmix, dxq, dgate, dk, dv, dgq = vjp(do)
        return dmix, dxq, dgate, o, g16, dk, dv, dgq
    L = dx.shape[0]
    return _rowwise(
        name, fn,
        [('r', dx), (mix_kind, mix), ('r', xq), ('r', gate), ('c', k), ('c', v), ('c', gq), ('c', wout)],
        [('r', (L, PRIM), F32), ('r', (L, XQ), F32), ('r', (L, BRANCH), F32), ('r', (L, BRANCH), BF16),
         ('r', (L, D_MODEL), BF16), ('a', k.shape, F32), ('a', v.shape, F32), ('a', gq.shape, F32)], nblk, sub)


def _perm_shape(L, C):
    return (L // 8, 8 * C)


_PACK = (("w_out", 512), ("s5_w_in", 512), ("w_mem_kv", 256), ("s5_w_glu", 576), ("mla_w_in", 432),
         ("mla_w_uq", 144), ("mla_w_ukv", 96), ("lora", 16))
_PACK_ROWS = sum(n for _, n in _PACK)
_PACK_OFF = {name: sum(n for _, n in _PACK[:i]) for i, (name, _) in enumerate(_PACK)}
_MLA_IN = 3392
_MLA_IN_PAD = 3456


def _rows1024(a, rows):
    flat = a.reshape(-1, 1024)
    if flat.shape[0] < rows:
        flat = jnp.pad(flat, ((0, rows - flat.shape[0]), (0, 0)))
    return flat


def _pack_shards(w_out, s5_w_in, w_mem_kv, s5_w_glu, mla_w_in, mla_w_uq, mla_w_ukv, q_lora, kv_lora):
    lora = jnp.concatenate([q_lora.reshape(-1), kv_lora.reshape(-1)])
    lora16 = lax.bitcast_convert_type(lora, BF16).reshape(-1)
    parts = {"w_out": w_out, "s5_w_in": s5_w_in, "w_mem_kv": w_mem_kv, "s5_w_glu": s5_w_glu, "mla_w_in": mla_w_in,
             "mla_w_uq": mla_w_uq, "mla_w_ukv": mla_w_ukv}
    pieces = [_rows1024(parts[n].astype(BF16), r) for n, r in _PACK[:-1]]
    pieces.append(jnp.pad(lora16, (0, 16 * 1024 - lora16.shape[0])).reshape(16, 1024))
    return jnp.concatenate(pieces, axis=0)


def _piece(g, name, real_rows=None):
    off = _PACK_OFF[name]
    n = dict(_PACK)[name] if real_rows is None else real_rows
    return g[:, off:off + n, :]


def _mla_in_perm(w):
    return jnp.concatenate([w[:, :768], w[:, 832:], w[:, 768:832], jnp.zeros((w.shape[0], 64), w.dtype)], axis=1)


def _mla_in_unperm(w):
    return jnp.concatenate([w[:, :768], w[:, 3328:3392], w[:, 768:3328]], axis=1)


def _unpack_weights(g):
    def cols(name, k, n, real_rows=None):
        return jnp.transpose(_piece(g, name, real_rows).reshape(N_DEV, k, n), (1, 0, 2)).reshape(k, N_DEV * n)

    w_out = jnp.transpose(_piece(g, "w_out").reshape(N_DEV, 2, 256, 1024), (1, 0, 2, 3)).reshape(2, 2048, 1024)
    w_mem_kv = jnp.transpose(_piece(g, "w_mem_kv").reshape(N_DEV, 2, 128, 1024), (1, 0, 2, 3)).reshape(2, 1024, 1024)
    s5_w_in = cols("s5_w_in", 1024, 512)
    s5_w_glu = cols("s5_w_glu", 1536, 384)
    mla_w_in = _mla_in_perm(cols("mla_w_in", 1024, 424, 424))
    uq = cols("mla_w_uq", 512, 288).reshape(Q_LORA, MLA_H, HD + ROPE)
    w_q = jnp.concatenate([uq[:, :, :HD].reshape(Q_LORA, PRIM),
                           jnp.pad(uq[:, :, HD:], ((0, 0), (0, 0), (0, HD - ROPE))).reshape(Q_LORA, PRIM)], axis=1)
    ukv = cols("mla_w_ukv", 256, 384).reshape(KV_LORA, MLA_H, 2 * HD)
    w_kv = jnp.concatenate([ukv[:, :, :HD].reshape(KV_LORA, PRIM), ukv[:, :, HD:].reshape(KV_LORA, PRIM)], axis=1)
    lora = lax.bitcast_convert_type(_piece(g, "lora")[:, 0, :192].reshape(N_DEV, 96, 2), F32)
    q_lora = lora[:, :64].reshape(1, Q_LORA)
    kv_lora = lora[:, 64:].reshape(1, KV_LORA)
    return w_out, w_mem_kv, s5_w_in, s5_w_glu, mla_w_in, w_q, w_kv, q_lora, kv_lora


def _pack_grads(d_w_out, d_s5_w_in, d_w_mem_kv, d_s5_w_glu, d_mla_w_in, d_w_q, d_w_kv):
    def cols(w, n, rows):
        k = w.shape[0]
        s = jnp.transpose(w.reshape(k, N_DEV, n), (1, 0, 2)).reshape(N_DEV, -1, 1024)
        if s.shape[1] < rows:
            s = jnp.pad(s, ((0, 0), (0, rows - s.shape[1]), (0, 0)))
        return s

    uq = jnp.concatenate([d_w_q[:, :PRIM].reshape(Q_LORA, MLA_H, HD),
                          d_w_q[:, PRIM:].reshape(Q_LORA, MLA_H, HD)[:, :, :ROPE]], axis=2).reshape(Q_LORA, 2304)
    ukv = jnp.concatenate([d_w_kv[:, :PRIM].reshape(KV_LORA, MLA_H, HD),
                           d_w_kv[:, PRIM:].reshape(KV_LORA, MLA_H, HD)], axis=2).reshape(KV_LORA, 3072)
    parts = {
        "w_out": jnp.transpose(d_w_out.reshape(2, N_DEV, 256, 1024), (1, 0, 2, 3)).reshape(N_DEV, 512, 1024),
        "s5_w_in": cols(d_s5_w_in, 512, 512),
        "w_mem_kv": jnp.transpose(d_w_mem_kv.reshape(2, N_DEV, 128, 1024), (1, 0, 2, 3)).reshape(N_DEV, 256, 1024),
        "s5_w_glu": cols(d_s5_w_glu, 384, 576),
        "mla_w_in": cols(_mla_in_unperm(d_mla_w_in), 424, 432),
        "mla_w_uq": cols(uq, 288, 144),
        "mla_w_ukv": cols(ukv, 384, 96),
        "lora": jnp.zeros((N_DEV, 16, 1024), BF16),
    }
    return jnp.concatenate([parts[n] for n, _ in _PACK], axis=1)


_SMALL = (("ln_gain", 2048), ("mem_norm", 2048), ("xq_norm", 256), ("xk_norm", 256), ("s5_lambda_re", 6144),
          ("s5_lambda_im", 6144), ("s5_log_step", 96), ("s5_b_re", 98304), ("s5_b_im", 98304), ("s5_c_re", 98304),
          ("s5_c_im", 98304), ("s5_d", 1536), ("mla_q_lora_norm", 512), ("mla_kv_lora_norm", 256),
          ("mla_q_nope_norm", 128), ("mla_k_nope_norm", 128), ("mla_q_rope_norm", 64), ("mla_k_rope_norm", 64))
_SMALL_ROWS = 408
_SMALL_OFF = {name: sum(n for _, n in _SMALL[:i]) for i, (name, _) in enumerate(_SMALL)}


def _pack_small(d):
    flat = jnp.concatenate([d[n].reshape(-1).astype(F32) for n, _ in _SMALL])
    return jnp.pad(flat, (0, _SMALL_ROWS * 1024 - flat.shape[0])).reshape(_SMALL_ROWS, 1024)


def _unpack_small(p, name, shape):
    off = _SMALL_OFF[name]
    return p.reshape(-1)[off:off + int(np.prod(shape))].reshape(shape)


_WEIGHTS = ('ln_gain', 'w_out', 'mem_norm', 'w_mem_kv', 'xq_norm', 'xk_norm', 's5_w_in', 's5_lambda_re',
            's5_lambda_im', 's5_log_step', 's5_b_re', 's5_b_im', 's5_c_re', 's5_c_im', 's5_d', 's5_w_glu', 'mla_w_in',
            'mla_q_lora_norm', 'mla_kv_lora_norm', 'mla_w_uq', 'mla_w_ukv', 'mla_q_nope_norm', 'mla_k_nope_norm',
            'mla_q_rope_norm', 'mla_k_rope_norm')
_BIG = ('w_out', 'w_mem_kv', 's5_w_in', 's5_w_glu', 'mla_w_in', 'mla_w_uq', 'mla_w_ukv')


def _pad128(g):
    return jnp.pad(g.reshape(1, -1), ((0, 0), (0, HD - g.shape[-1])))


def kernel(x, mem, positions, ln_gain, w_out, mem_norm, w_mem_kv, xq_norm, xk_norm, s5_w_in, s5_lambda_re, s5_lambda_im, s5_log_step, s5_b_re, s5_b_im, s5_c_re, s5_c_im, s5_d, s5_w_glu, mla_w_in, mla_q_lora_norm, mla_kv_lora_norm, mla_w_uq, mla_w_ukv, mla_q_nope_norm, mla_k_nope_norm, mla_q_rope_norm, mla_k_rope_norm, loss_target, m_ln_gain, m_w_out, m_mem_norm, m_w_mem_kv, m_xq_norm, m_xk_norm, m_s5_w_in, m_s5_lambda_re, m_s5_lambda_im, m_s5_log_step, m_s5_b_re, m_s5_b_im, m_s5_c_re, m_s5_c_im, m_s5_d, m_s5_w_glu, m_mla_w_in, m_mla_q_lora_norm, m_mla_kv_lora_norm, m_mla_w_uq, m_mla_w_ukv, m_mla_q_nope_norm, m_mla_k_nope_norm, m_mla_q_rope_norm, m_mla_k_rope_norm, v_ln_gain, v_w_out, v_mem_norm, v_w_mem_kv, v_xq_norm, v_xk_norm, v_s5_w_in, v_s5_lambda_re, v_s5_lambda_im, v_s5_log_step, v_s5_b_re, v_s5_b_im, v_s5_c_re, v_s5_c_im, v_s5_d, v_s5_w_glu, v_mla_w_in, v_mla_q_lora_norm, v_mla_kv_lora_norm, v_mla_w_uq, v_mla_w_ukv, v_mla_q_nope_norm, v_mla_k_nope_norm, v_mla_q_rope_norm, v_mla_k_rope_norm):
    weights = dict(ln_gain=ln_gain, w_out=w_out, mem_norm=mem_norm, w_mem_kv=w_mem_kv, xq_norm=xq_norm,
                   xk_norm=xk_norm, s5_w_in=s5_w_in, s5_lambda_re=s5_lambda_re, s5_lambda_im=s5_lambda_im,
                   s5_log_step=s5_log_step, s5_b_re=s5_b_re, s5_b_im=s5_b_im, s5_c_re=s5_c_re, s5_c_im=s5_c_im,
                   s5_d=s5_d, s5_w_glu=s5_w_glu, mla_w_in=mla_w_in, mla_q_lora_norm=mla_q_lora_norm,
                   mla_kv_lora_norm=mla_kv_lora_norm, mla_w_uq=mla_w_uq, mla_w_ukv=mla_w_ukv,
                   mla_q_nope_norm=mla_q_nope_norm, mla_k_nope_norm=mla_k_nope_norm,
                   mla_q_rope_norm=mla_q_rope_norm, mla_k_rope_norm=mla_k_rope_norm)
    m_in = dict(zip(_WEIGHTS, (m_ln_gain, m_w_out, m_mem_norm, m_w_mem_kv, m_xq_norm, m_xk_norm, m_s5_w_in,
                               m_s5_lambda_re, m_s5_lambda_im, m_s5_log_step, m_s5_b_re, m_s5_b_im, m_s5_c_re,
                               m_s5_c_im, m_s5_d, m_s5_w_glu, m_mla_w_in, m_mla_q_lora_norm, m_mla_kv_lora_norm,
                               m_mla_w_uq, m_mla_w_ukv, m_mla_q_nope_norm, m_mla_k_nope_norm, m_mla_q_rope_norm,
                               m_mla_k_rope_norm)))
    v_in = dict(zip(_WEIGHTS, (v_ln_gain, v_w_out, v_mem_norm, v_w_mem_kv, v_xq_norm, v_xk_norm, v_s5_w_in,
                               v_s5_lambda_re, v_s5_lambda_im, v_s5_log_step, v_s5_b_re, v_s5_b_im, v_s5_c_re,
                               v_s5_c_im, v_s5_d, v_s5_w_glu, v_mla_w_in, v_mla_q_lora_norm, v_mla_kv_lora_norm,
                               v_mla_w_uq, v_mla_w_ukv, v_mla_q_nope_norm, v_mla_k_nope_norm, v_mla_q_rope_norm,
                               v_mla_k_rope_norm)))

    x0 = x[0]
    mem0 = mem[0]
    target = loss_target[0]
    L = x0.shape[0]
    nblk, sub = 8, 1
    me = 4 * lax.axis_index("x") + 2 * lax.axis_index("y") + lax.axis_index("c")

    packed = _pack_shards(w_out, s5_w_in[0], w_mem_kv, s5_w_glu[0], mla_w_in[0], mla_w_uq[0], mla_w_ukv[0],
                          mla_q_lora_norm, mla_kv_lora_norm)
    gathered = _all_gather(packed, "ag_weights")
    W_out, W_mkv, W_in_s5, W_glu, W_in_mla, W_q, W_kv, g_qlora, g_kvlora = _unpack_weights(gathered)

    ln0, ln1 = ln_gain[0:1], ln_gain[1:2]
    gq0, gq1 = xq_norm[0:1], xq_norm[1:2]
    gk0, gk1 = xk_norm[0:1], xk_norm[1:2]
    gm0, gm1 = mem_norm[0:1], mem_norm[1:2]
    gqn, gkn = mla_q_nope_norm, mla_k_nope_norm
    gqr, gkr = _pad128(mla_q_rope_norm), _pad128(mla_k_rope_norm)

    lr3 = s5_lambda_re.reshape(S5_G, 1, S5_P)
    li3 = s5_lambda_im.reshape(S5_G, 1, S5_P)
    ls3 = s5_log_step.reshape(S5_G, 1, 1)
    btr = jnp.swapaxes(s5_b_re[0], 1, 2)
    bti = jnp.swapaxes(s5_b_im[0], 1, 2)
    a_r, a_i, bbr, bbi = _s5_params(lr3, li3, ls3, btr, bti)
    bm, bmt, cm, cmt = _s5_mats(bbr, bbi, s5_c_re[0], s5_c_im[0])
    a_r2 = a_r.reshape(1, S5_G * S5_P)
    a_i2 = a_i.reshape(1, S5_G * S5_P)
    cmask, rmat = _s5_compact_consts()

    half = ROPE // 2
    inv_freq = ROPE_THETA ** (-jnp.arange(half, dtype=F32) / half)
    invf = jnp.concatenate([inv_freq, inv_freq, jnp.zeros((HD - ROPE,), F32)]).reshape(1, HD)

    def rot_tables(pos, invf):
        ang = pos.astype(F32) * invf
        lane = lax.broadcasted_iota(jnp.int32, ang.shape, 1)
        c = jnp.where(lane < ROPE, jnp.cos(ang), 0.0)
        s = jnp.sin(ang)
        return c, jnp.where(lane < half, -s, 0.0), jnp.where((lane >= half) & (lane < ROPE), s, 0.0)

    tc, ts1, ts2 = _rowwise("rot_tables", rot_tables, [('r', positions.reshape(L, 1)), ('c', invf)],
                            [('r', (L, HD), F32)] * 3, nblk, sub)

    def in_s5(x, g, w):
        proj = _dot(_rms(x, g, D_MODEL).astype(BF16), w)
        return proj[:, :PRIM], proj[:, PRIM:PRIM + XQ], proj[:, PRIM + XQ:]

    u_p, xq_a, gate_a = _rowwise("s5_in", in_s5, [('r', x0), ('c', ln0), ('c', W_in_s5)],
                                 [('p', _perm_shape(L, PRIM), F32), ('r', (L, XQ), F32), ('r', (L, BRANCH), F32)],
                                 nblk, sub)
    u_i = u_p.reshape(L, PRIM)
    y_i = _s5_fwd(u_i, bm, cm, a_r2, a_i2, s5_d)
    y_p = y_i.reshape(_perm_shape(L, PRIM))

    def glu(y, w):
        z = _dot(_gelu(y).astype(BF16), w)
        return (z[:, :PRIM] * _sigmoid(z[:, PRIM:]),)

    y2 = _rowwise("s5_glu", glu, [('p', y_p), ('c', W_glu)], [('r', (L, PRIM), F32)], nblk, sub)[0]
    k_a, v_a = _kv_prep(mem0, gm0, W_mkv[0], gk0, "kv_prep0")
    x1 = _forward_merge(x0, y2, 'r', xq_a, gate_a, k_a, v_a, gq0, W_out[0], "merge0", nblk, sub)

    def in_mla(x, g, w):
        proj = _dot(_rms(x, g, D_MODEL).astype(BF16), w)
        return proj[:, :512], proj[:, 512:768], proj[:, 768:1280], proj[:, 1280:3328], proj[:, 3328:]

    c_q, c_kv, xq_b, gate_b, krp = _rowwise(
        "mla_in", in_mla, [('r', x1), ('c', ln1), ('c', W_in_mla)],
        [('r', (L, Q_LORA), F32), ('r', (L, KV_LORA), F32), ('r', (L, XQ), F32), ('r', (L, BRANCH), F32),
         ('r', (L, HD), F32)], nblk, sub)

    def qkv(c_q, c_kv, krp, tc, ts1, ts2, gql, gkvl, wq, wkv, gqn, gkn, gqr, gkr):
        q = _dot(_rms(c_q, gql, Q_LORA).astype(BF16), wq)
        kv = _dot(_rms(c_kv, gkvl, KV_LORA).astype(BF16), wkv)
        kp, v = _kv_post(kv, krp, gkn, gkr, tc, ts1, ts2)
        return _q_post(q, gqn, gqr, tc, ts1, ts2), kp, v

    qkv_consts = [('c', g_qlora), ('c', g_kvlora), ('c', W_q), ('c', W_kv), ('c', gqn), ('c', gkn), ('c', gqr),
                  ('c', gkr)]
    q_pad, k_pad, v_h = _rowwise(
        "mla_qkv", qkv, [('r', c_q), ('r', c_kv), ('r', krp), ('r', tc), ('r', ts1), ('r', ts2)] + qkv_consts,
        [('r', (L, 2 * PRIM), BF16), ('r', (L, 2 * PRIM), BF16), ('r', (L, PRIM), BF16)], nblk, sub)
    scale = (HD + ROPE) ** -0.5
    attn = _attn_fwd(q_pad, k_pad, v_h, scale)
    k_b, v_b = _kv_prep(mem0, gm1, W_mkv[1], gk1, "kv_prep1")
    x2 = _forward_merge(x1, attn, 'r', xq_b, gate_b, k_b, v_b, gq1, W_out[1], "merge1", nblk, sub)

    def loss_fn(y, t):
        err = y - t
        part = 0.5 * jnp.sum(jnp.sum(err * err, axis=-1, keepdims=True) * (1.0 / D_MODEL), axis=0, keepdims=True)
        return err * (1.0 / D_MODEL), jnp.broadcast_to(part, (1, HD))

    dx2, loss_part = _rowwise("loss", loss_fn, [('r', x2), ('r', target)],
                              [('r', (L, D_MODEL), F32), ('a', (1, HD), F32)], nblk, sub)
    loss = lax.psum(loss_part[0, 0], ("x", "y", "c"))

    dattn, dxq_b, dgate_b, o_b, g_b, dk_b, dv_b, dgq1 = _backward_merge(
        dx2, attn, 'r', xq_b, gate_b, k_b, v_b, gq1, W_out[1], "merge1_bwd", nblk, sub)
    dgm1, dW_mkv1, dgk1 = _kv_prep_bwd(mem0, gm1, W_mkv[1], gk1, dk_b, dv_b, "kv_prep1_bwd")
    dW_out1 = _matmul_tn(o_b, g_b, "dw_out1")
    dq_pad, dk_pad, dv_h = _attn_bwd(q_pad, k_pad, v_h, dattn, scale)

    def qkv_bwd(c_q, c_kv, krp, tc, ts1, ts2, dqp, dkp, dv, gql, gkvl, wq, wkv, gqn, gkn, gqr, gkr):
        cqn, vjp_qn = jax.vjp(lambda a, b: _rms(a, b, Q_LORA), c_q, gql)
        ckvn, vjp_kvn = jax.vjp(lambda a, b: _rms(a, b, KV_LORA), c_kv, gkvl)
        cqn16 = cqn.astype(BF16)
        ckvn16 = ckvn.astype(BF16)
        q = _dot(cqn16, wq)
        kv = _dot(ckvn16, wkv)
        _, vjp_q = jax.vjp(lambda a, b, c: _q_post(a, b, c, tc, ts1, ts2), q, gqn, gqr)
        dq, dgqn, dgqr = vjp_q(dqp)
        _, vjp_kv = jax.vjp(lambda a, b, c, d: _kv_post(a, b, c, d, tc, ts1, ts2), kv, krp, gkn, gkr)
        dkv, dkrp, dgkn, dgkr = vjp_kv((dkp, dv))
        dq16 = dq.astype(BF16)
        dkv16 = dkv.astype(BF16)
        dc_q, dgql = vjp_qn(_dot_nt(dq16, wq))
        dc_kv, dgkvl = vjp_kvn(_dot_nt(dkv16, wkv))
        return dc_q, dc_kv, dkrp, cqn16, dq16, ckvn16, dkv16, dgql, dgkvl, dgqn, dgkn, dgqr, dgkr

    (dc_q, dc_kv, dkrp, cqn16, dq16, ckvn16, dkv16, dgql, dgkvl, dgqn, dgkn, dgqr, dgkr) = _rowwise(
        "mla_qkv_bwd", qkv_bwd,
        [('r', c_q), ('r', c_kv), ('r', krp), ('r', tc), ('r', ts1), ('r', ts2), ('r', dq_pad), ('r', dk_pad),
         ('r', dv_h)] + qkv_consts,
        [('r', (L, Q_LORA), F32), ('r', (L, KV_LORA), F32), ('r', (L, HD), F32), ('r', (L, Q_LORA), BF16),
         ('r', (L, 2 * PRIM), BF16), ('r', (L, KV_LORA), BF16), ('r', (L, 2 * PRIM), BF16),
         ('a', (1, Q_LORA), F32), ('a', (1, KV_LORA), F32), ('a', (1, HD), F32), ('a', (1, HD), F32),
         ('a', (1, HD), F32), ('a', (1, HD), F32)], nblk, sub)
    dW_q = _matmul_tn(cqn16, dq16, "dw_uq")
    dW_kv = _matmul_tn(ckvn16, dkv16, "dw_ukv")

    def in_bwd(x, dres, g, w, *dparts):
        dproj = jnp.concatenate(dparts, axis=-1).astype(BF16)
        xn, vjp = jax.vjp(lambda a, b: _rms(a, b, D_MODEL), x, g)
        dx, dg = vjp(_dot_nt(dproj, w))
        return dx + dres, xn, dproj, dg

    dx1, xn1, dproj1, dln1 = _rowwise(
        "mla_in_bwd", in_bwd,
        [('r', x1), ('r', dx2), ('c', ln1), ('c', W_in_mla), ('r', dc_q), ('r', dc_kv), ('r', dxq_b), ('r', dgate_b),
         ('r', dkrp)],
        [('r', (L, D_MODEL), F32), ('r', (L, D_MODEL), BF16), ('r', (L, _MLA_IN_PAD), BF16), ('a', (1, D_MODEL), F32)],
        nblk, sub)
    dW_in_mla = _matmul_tn(xn1, dproj1, "dw_mla_in")

    dy2, dxq_a, dgate_a, o_a, g_a, dk_a, dv_a, dgq0 = _backward_merge(
        dx1, y2, 'r', xq_a, gate_a, k_a, v_a, gq0, W_out[0], "merge0_bwd", nblk, sub)
    dgm0, dW_mkv0, dgk0 = _kv_prep_bwd(mem0, gm0, W_mkv[0], gk0, dk_a, dv_a, "kv_prep0_bwd")
    dW_out0 = _matmul_tn(o_a, g_a, "dw_out0")

    def glu_bwd(y, dy2, w):
        h, vjp_h = jax.vjp(_gelu, y)
        h16 = h.astype(BF16)
        z = _dot(h16, w)
        _, vjp_z = jax.vjp(lambda z: z[:, :PRIM] * _sigmoid(z[:, PRIM:]), z)
        dz16 = vjp_z(dy2)[0].astype(BF16)
        return vjp_h(_dot_nt(dz16, w))[0], h16, dz16

    dy_p, h16, dz16 = _rowwise("s5_glu_bwd", glu_bwd, [('p', y_p), ('r', dy2), ('c', W_glu)],
                               [('p', _perm_shape(L, PRIM), F32), ('r', (L, PRIM), BF16), ('r', (L, 2 * PRIM), BF16)],
                               nblk, sub)
    dW_glu = _matmul_tn(h16, dz16, "dw_glu")
    du_i, dbc, dcc, dd, dar, dai = _s5_bwd(u_i, dy_p.reshape(L, PRIM), bm, bmt, cmt, a_r2, a_i2, s5_d, cmask, rmat)
    dx0, xn0, dproj0, dln0 = _rowwise(
        "s5_in_bwd", in_bwd,
        [('r', x0), ('r', dx1), ('c', ln0), ('c', W_in_s5), ('p', du_i.reshape(_perm_shape(L, PRIM))), ('r', dxq_a),
         ('r', dgate_a)],
        [('r', (L, D_MODEL), F32), ('r', (L, D_MODEL), BF16), ('r', (L, 2 * BRANCH), BF16), ('a', (1, D_MODEL), F32)],
        nblk, sub)
    dW_in_s5 = _matmul_tn(xn0, dproj0, "dw_s5_in")

    dbc4 = dbc.reshape(S5_G, S5_C, 2, S5_P)
    dcc4 = dcc.reshape(S5_G, S5_C, 2, S5_P)
    dlr, dli, dls, dbtr, dbti = _s5_params_bwd(
        lr3, li3, ls3, btr, bti, dar.reshape(S5_G, 1, S5_P), dai.reshape(S5_G, 1, S5_P), dbc4[:, :, 0], dbc4[:, :, 1])

    send = _pack_grads(jnp.stack([dW_out0, dW_out1]), dW_in_s5, jnp.stack([dW_mkv0, dW_mkv1]), dW_glu, dW_in_mla,
                       dW_q, dW_kv)
    recv = _all_to_all(send, "rs_grads")
    gsum = _sum_slots(recv, "rs_sum", 48)

    def shard(name, shape, real_rows=None):
        off = _PACK_OFF[name]
        n = dict(_PACK)[name] if real_rows is None else real_rows
        return gsum[off:off + n].reshape(shape)

    grads = {
        "w_out": shard("w_out", w_out.shape), "w_mem_kv": shard("w_mem_kv", w_mem_kv.shape),
        "s5_w_in": shard("s5_w_in", s5_w_in.shape), "s5_w_glu": shard("s5_w_glu", s5_w_glu.shape),
        "mla_w_in": shard("mla_w_in", mla_w_in.shape, 424), "mla_w_uq": shard("mla_w_uq", mla_w_uq.shape),
        "mla_w_ukv": shard("mla_w_ukv", mla_w_ukv.shape),
    }

    small_part = {
        "ln_gain": jnp.concatenate([dln0, dln1]), "mem_norm": jnp.concatenate([dgm0, dgm1]),
        "xq_norm": jnp.concatenate([dgq0, dgq1]), "xk_norm": jnp.concatenate([dgk0, dgk1]),
        "s5_lambda_re": dlr, "s5_lambda_im": dli, "s5_log_step": dls,
        "s5_b_re": jnp.swapaxes(dbtr, 1, 2), "s5_b_im": jnp.swapaxes(dbti, 1, 2),
        "s5_c_re": dcc4[:, :, 0], "s5_c_im": -dcc4[:, :, 1], "s5_d": dd,
        "mla_q_lora_norm": dgql, "mla_kv_lora_norm": dgkvl, "mla_q_nope_norm": dgqn, "mla_k_nope_norm": dgkn,
        "mla_q_rope_norm": dgqr[:, :ROPE], "mla_k_rope_norm": dgkr[:, :ROPE],
    }
    small_gath = _all_gather(_pack_small(small_part), "ag_small_grads")

    def whole(name, a):
        if name == "mla_q_lora_norm":
            return lax.dynamic_update_slice(jnp.zeros((Q_LORA,), F32), a.reshape(-1), (me * 64,))
        if name == "mla_kv_lora_norm":
            return lax.dynamic_update_slice(jnp.zeros((KV_LORA,), F32), a.reshape(-1), (me * 32,))
        return a

    wp = _pack_small({n: whole(n, weights[n]) for n, _ in _SMALL})
    mp = _pack_small({n: whole(n, m_in[n]) for n, _ in _SMALL})
    vp = _pack_small({n: whole(n, v_in[n]) for n, _ in _SMALL})
    gs, ds, ms, vs = _small_update(small_gath, wp, mp, vp, "small_update")

    delta, new_m, new_v = {}, {}, {}
    for n, _ in _SMALL:
        shape = weights[n].shape
        if n == "mla_q_lora_norm":
            take = lambda p: lax.dynamic_slice(_unpack_small(p, n, (Q_LORA,)), (me * 64,), (64,)).reshape(shape)
        elif n == "mla_kv_lora_norm":
            take = lambda p: lax.dynamic_slice(_unpack_small(p, n, (KV_LORA,)), (me * 32,), (32,)).reshape(shape)
        else:
            take = lambda p: _unpack_small(p, n, shape)
        grads[n], delta[n], new_m[n], new_v[n] = take(gs), take(ds), take(ms), take(vs)
    for n in _BIG:
        delta[n], new_m[n], new_v[n] = _adamw(weights[n], grads[n], m_in[n], v_in[n], "adamw_" + n)

    return (loss, dx0[None], *[grads[n] for n in _WEIGHTS], *[delta[n] for n in _WEIGHTS],
            *[new_m[n] for n in _WEIGHTS], *[new_v[n] for n in _WEIGHTS])
```

```python
import functools
import math

import numpy as np
import jax
import jax.numpy as jnp
from jax import lax
from jax.experimental import pallas as pl
from jax.experimental.pallas import tpu as pltpu

F32 = jnp.float32
BF16 = jnp.bfloat16
EPS = 1e-6
NEG = float(np.finfo(np.float32).min)
MESH = pl.DeviceIdType.MESH

N_DEV = 8
D_MODEL = 1024
MEM_LEN = 256
XQ = 512
PRIM = 1536
BRANCH = 2048
X_HEADS = 4
HD = 128
S5_G = 96
S5_P = 64
S5_C = 16
S5_GB = 8
S5_W = S5_GB * S5_P
MLA_H = 12
ROPE = 64
Q_LORA = 512
KV_LORA = 256
ROPE_THETA = 10000.0

ADAM_LR = 0.001
ADAM_B1 = 0.9
ADAM_B2 = 0.999
ADAM_EPS = 1e-08
ADAM_WD = 0.01
ADAM_STEP = 10

VMEM_LIMIT = 56 * 1024 * 1024


def _dot(a, b):
    return jnp.dot(a, b, preferred_element_type=F32)


def _dot_nt(a, b):
    return lax.dot_general(a, b, (((1,), (1,)), ((), ())), preferred_element_type=F32)


def _dot_tn(a, b):
    return lax.dot_general(a, b, (((0,), (0,)), ((), ())), preferred_element_type=F32)


@jax.custom_vjp
def _mm(a, b):
    return _dot(a.astype(BF16), b.astype(BF16))


def _mm_fwd(a, b):
    return _mm(a, b), (a, b)


def _mm_bwd(res, g):
    a, b = res
    gb = g.astype(BF16)
    return _dot_nt(gb, b.astype(BF16)).astype(a.dtype), _dot_tn(a.astype(BF16), gb).astype(b.dtype)


_mm.defvjp(_mm_fwd, _mm_bwd)


@jax.custom_vjp
def _mm_nt(a, b):
    return _dot_nt(a.astype(BF16), b.astype(BF16))


def _mm_nt_fwd(a, b):
    return _mm_nt(a, b), (a, b)


def _mm_nt_bwd(res, g):
    a, b = res
    gb = g.astype(BF16)
    return _dot(gb, b.astype(BF16)).astype(a.dtype), _dot_tn(gb, a.astype(BF16)).astype(b.dtype)


_mm_nt.defvjp(_mm_nt_fwd, _mm_nt_bwd)


@jax.custom_vjp
def _softmax(s):
    m = jnp.max(s, axis=-1, keepdims=True)
    e = jnp.exp(s - m)
    return e / jnp.sum(e, axis=-1, keepdims=True)


def _softmax_fwd(s):
    p = _softmax(s)
    return p, p


def _softmax_bwd(p, g):
    return (p * (g - jnp.sum(p * g, axis=-1, keepdims=True)),)


_softmax.defvjp(_softmax_fwd, _softmax_bwd)


def _rms(x, g, n):
    ms = jnp.sum(x * x, axis=-1, keepdims=True) * (1.0 / n)
    return x * lax.rsqrt(ms + EPS) * g


def _sigmoid(x):
    return 1.0 / (1.0 + jnp.exp(-x))


def _silu(x):
    return x * _sigmoid(x)


def _gelu(x):
    c = math.sqrt(2.0 / math.pi)
    return 0.5 * x * (1.0 + jnp.tanh(c * (x + 0.044715 * (x * x * x))))


@jax.custom_vjp
def _rot(x, c, s1, s2):
    return x * c + pltpu.roll(x, 96, 1) * s1 + pltpu.roll(x, 32, 1) * s2


def _rot_fwd(x, c, s1, s2):
    return _rot(x, c, s1, s2), (c, s1, s2)


def _rot_bwd(res, g):
    c, s1, s2 = res
    dx = g * c + pltpu.roll(g * s1, 32, 1) + pltpu.roll(g * s2, 96, 1)
    return dx, jnp.zeros_like(c), jnp.zeros_like(s1), jnp.zeros_like(s2)


_rot.defvjp(_rot_fwd, _rot_bwd)


def _mem_attn(xq, k, v, gq):
    outs = []
    for h in range(X_HEADS):
        sl = slice(HD * h, HD * (h + 1))
        q = _rms(xq[:, sl], gq, HD)
        p = _softmax(_mm_nt(q, k[:, sl]) * (HD ** -0.5))
        outs.append(_mm(p, v[:, sl]))
    return jnp.concatenate(outs, axis=-1)


def _merge(mix, xq, gate, k, v, gq):
    return jnp.concatenate([mix, _mem_attn(xq, k, v, gq)], axis=-1) * _silu(gate)


def _q_post(q, gqn, gqr, c, s1, s2):
    pieces = []
    for h in range(MLA_H):
        pieces.append(_rms(q[:, HD * h:HD * (h + 1)], gqn, HD))
        pieces.append(_rot(_rms(q[:, PRIM + HD * h:PRIM + HD * (h + 1)], gqr, ROPE), c, s1, s2))
    return jnp.concatenate(pieces, axis=-1)


def _kv_post(kv, krp, gkn, gkr, c, s1, s2):
    kr = _rot(_rms(krp, gkr, ROPE), c, s1, s2)
    pieces = []
    for h in range(MLA_H):
        pieces.append(_rms(kv[:, HD * h:HD * (h + 1)], gkn, HD))
        pieces.append(kr)
    return jnp.concatenate(pieces, axis=-1), kv[:, PRIM:]


def _rowwise(name, fn, ins, outs, nblk, sub=1):
    n_in = len(ins)

    def spec(kind, shape):
        if kind == 'r':
            return pl.BlockSpec((shape[0] // nblk, shape[1]), lambda i: (i, 0))
        zeros = (0,) * len(shape)
        return pl.BlockSpec(tuple(shape), lambda i: zeros)

    def body(*refs):
        i = pl.program_id(0)
        res = fn(*[r[...] for r in refs[:n_in]])
        for (kind, _, _), ref, val in zip(outs, refs[n_in:], res):
            if kind == 'a':
                @pl.when(i == 0)
                def _():
                    ref[...] = jnp.zeros_like(ref)
                ref[...] += val.astype(ref.dtype)
            else:
                ref[...] = val.astype(ref.dtype)

    res = pl.pallas_call(
        body, name=name, grid=(nblk,),
        in_specs=[spec(k, a.shape) for k, a in ins],
        out_specs=[spec(k, s) for k, s, _ in outs],
        out_shape=[jax.ShapeDtypeStruct(tuple(s), d) for _, s, d in outs],
        compiler_params=pltpu.CompilerParams(dimension_semantics=("arbitrary",), vmem_limit_bytes=VMEM_LIMIT),
    )(*[a for _, a in ins])
    return res


def _matmul_tn(a, g, name, out_dtype=BF16):
    L, K = a.shape
    N = g.shape[1]
    tn = next(t for t in (512, 384, 256, 128) if N % t == 0)
    tl = min(512, L)
    nl = L // tl

    def body(a_ref, g_ref, o_ref, acc):
        l = pl.program_id(1)

        @pl.when(l == 0)
        def _():
            acc[...] = jnp.zeros_like(acc)

        acc[...] += _dot_tn(a_ref[...], g_ref[...])

        @pl.when(l == nl - 1)
        def _():
            o_ref[...] = acc[...].astype(o_ref.dtype)

    return pl.pallas_call(
        body, name=name, grid=(N // tn, nl),
        in_specs=[pl.BlockSpec((tl, K), lambda n, l: (l, 0)), pl.BlockSpec((tl, tn), lambda n, l: (l, n))],
        out_specs=pl.BlockSpec((K, tn), lambda n, l: (0, n)),
        out_shape=jax.ShapeDtypeStruct((K, N), out_dtype),
        scratch_shapes=[pltpu.VMEM((K, tn), F32)],
        compiler_params=pltpu.CompilerParams(dimension_semantics=("arbitrary", "arbitrary"),
                                             vmem_limit_bytes=VMEM_LIMIT),
    )(a, g)


def _all_gather(xs, name):
    R, C = xs.shape

    def body(x_ref, out_ref, send_sems, recv_sems, local_sem):
        x, y, c = lax.axis_index("x"), lax.axis_index("y"), lax.axis_index("c")
        me, sibling = (x, y, c), (x, y, 1 - c)
        chips = [(1 - x, y), (x, 1 - y), (1 - x, 1 - y)]

        def rows(px, py, pc):
            return out_ref.at[4 * px + 2 * py + pc]

        def copy(k, block, to, src=None):
            return pltpu.make_async_remote_copy(
                src_ref=rows(*block) if src is None else src, dst_ref=rows(*block),
                send_sem=send_sems.at[k], recv_sem=recv_sems.at[k], device_id=to, device_id_type=MESH)

        mine = pltpu.make_async_copy(x_ref, rows(*me), local_sem)
        mine.start()
        first = [copy(0, me, sibling, src=x_ref)]
        first += [copy(1 + j, me, (*chip, c), src=x_ref) for j, chip in enumerate(chips)]
        for cp in first:
            cp.start()
        passed = [copy(4 + j, (*chip, c), sibling) for j, chip in enumerate(chips)]
        for j, chip in enumerate(chips):
            copy(1 + j, (*chip, c), me).wait_recv()
            passed[j].start()
        copy(0, sibling, me).wait_recv()
        for j, chip in enumerate(chips):
            copy(4 + j, (*chip, 1 - c), me).wait_recv()
        for cp in first + passed:
            cp.wait_send()
        mine.wait()

    return pl.pallas_call(
        body, name=name,
        out_shape=jax.ShapeDtypeStruct((N_DEV, R, C), xs.dtype),
        in_specs=[pl.BlockSpec(memory_space=pl.ANY)],
        out_specs=pl.BlockSpec(memory_space=pl.ANY),
        scratch_shapes=[pltpu.SemaphoreType.DMA((7,)), pltpu.SemaphoreType.DMA((7,)), pltpu.SemaphoreType.DMA],
    )(xs)


def _all_to_all(send, name):
    _, R, C = send.shape
    flips = [(0, 0, 1), (1, 0, 0), (0, 1, 0), (1, 1, 0), (1, 0, 1), (0, 1, 1), (1, 1, 1)]

    def body(send_ref, recv_ref, send_sems, recv_sems, local_sem):
        x, y, c = lax.axis_index("x"), lax.axis_index("y"), lax.axis_index("c")
        me = 4 * x + 2 * y + c
        local = pltpu.make_async_copy(send_ref.at[me], recv_ref.at[me], local_sem)
        local.start()
        copies = []
        for k, (fx, fy, fc) in enumerate(flips):
            px = 1 - x if fx else x
            py = 1 - y if fy else y
            pc = 1 - c if fc else c
            cp = pltpu.make_async_remote_copy(
                src_ref=send_ref.at[4 * px + 2 * py + pc], dst_ref=recv_ref.at[me],
                send_sem=send_sems.at[k], recv_sem=recv_sems.at[k], device_id=(px, py, pc), device_id_type=MESH)
            cp.start()
            copies.append(cp)
        for cp in copies:
            cp.wait_recv()
        for cp in copies:
            cp.wait_send()
        local.wait()

    return pl.pallas_call(
        body, name=name,
        out_shape=jax.ShapeDtypeStruct((N_DEV, R, C), send.dtype),
        in_specs=[pl.BlockSpec(memory_space=pl.ANY)],
        out_specs=pl.BlockSpec(memory_space=pl.ANY),
        scratch_shapes=[pltpu.SemaphoreType.DMA((7,)), pltpu.SemaphoreType.DMA((7,)), pltpu.SemaphoreType.DMA],
    )(send)


def _sum_slots(recv, name, br):
    _, R, C = recv.shape

    def body(r_ref, o_ref):
        acc = r_ref[0].astype(F32)
        for d in range(1, N_DEV):
            acc = acc + r_ref[d].astype(F32)
        o_ref[...] = acc

    return pl.pallas_call(
        body, name=name, grid=(R // br,),
        in_specs=[pl.BlockSpec((N_DEV, br, C), lambda i: (0, i, 0))],
        out_specs=pl.BlockSpec((br, C), lambda i: (i, 0)),
        out_shape=jax.ShapeDtypeStruct((R, C), F32),
        compiler_params=pltpu.CompilerParams(dimension_semantics=("arbitrary",)),
    )(recv)


def _adamw_vals(w, g, m, v):
    m2 = ADAM_B1 * m + (1.0 - ADAM_B1) * g
    v2 = ADAM_B2 * v + (1.0 - ADAM_B2) * (g * g)
    m_hat = m2 / (1.0 - ADAM_B1 ** ADAM_STEP)
    v_hat = v2 / (1.0 - ADAM_B2 ** ADAM_STEP)
    delta = -ADAM_LR * (m_hat / (jnp.sqrt(v_hat) + ADAM_EPS) + ADAM_WD * w)
    return delta, m2, v2


def _adamw(w, g, m, v, name):
    shape = w.shape
    C = shape[-1]
    R = int(np.prod(shape[:-1]))
    br = next((t for t in (256, 128, 64, 32, 16, 8) if R % t == 0), R)

    def body(w_ref, g_ref, m_ref, v_ref, d_ref, m2_ref, v2_ref):
        d, m2, v2 = _adamw_vals(w_ref[...], g_ref[...], m_ref[...], v_ref[...])
        d_ref[...] = d
        m2_ref[...] = m2
        v2_ref[...] = v2

    spec = pl.BlockSpec((br, C), lambda i: (i, 0))
    outs = pl.pallas_call(
        body, name=name, grid=(R // br,),
        in_specs=[spec] * 4, out_specs=[spec] * 3,
        out_shape=[jax.ShapeDtypeStruct((R, C), F32)] * 3,
        compiler_params=pltpu.CompilerParams(dimension_semantics=("arbitrary",)),
    )(*[a.reshape(R, C) for a in (w, g, m, v)])
    return tuple(o.reshape(shape) for o in outs)


def _small_update(gath, wp, mp, vp, name):
    _, R, C = gath.shape
    br = R // 3

    def body(g_ref, w_ref, m_ref, v_ref, go_ref, d_ref, m2_ref, v2_ref):
        g = g_ref[0]
        for d in range(1, N_DEV):
            g = g + g_ref[d]
        dl, m2, v2 = _adamw_vals(w_ref[...], g, m_ref[...], v_ref[...])
        go_ref[...] = g
        d_ref[...] = dl
        m2_ref[...] = m2
        v2_ref[...] = v2

    spec = pl.BlockSpec((br, C), lambda i: (i, 0))
    return pl.pallas_call(
        body, name=name, grid=(R // br,),
        in_specs=[pl.BlockSpec((N_DEV, br, C), lambda i: (0, i, 0)), spec, spec, spec],
        out_specs=[spec] * 4, out_shape=[jax.ShapeDtypeStruct((R, C), F32)] * 4,
        compiler_params=pltpu.CompilerParams(dimension_semantics=("arbitrary",)),
    )(gath, wp, mp, vp)


def _s5_param_fn(lr, li, ls, btr, bti):
    step = jnp.exp(ls)
    er = jnp.exp(lr * step)
    ang = li * step
    ar = er * jnp.cos(ang)
    ai = er * jnp.sin(ang)
    nr = ar - 1.0
    den = lr * lr + li * li
    fr = (nr * lr + ai * li) / den
    fi = (ai * lr - nr * li) / den
    return ar, ai, fr * btr - fi * bti, fr * bti + fi * btr


def _s5_params(lr, li, ls, btr, bti):
    def body(lr_ref, li_ref, ls_ref, br_ref, bi_ref, ar_ref, ai_ref, bbr_ref, bbi_ref):
        ar, ai, bbr, bbi = _s5_param_fn(lr_ref[...], li_ref[...], ls_ref[...], br_ref[...], bi_ref[...])
        ar_ref[...] = ar
        ai_ref[...] = ai
        bbr_ref[...] = bbr
        bbi_ref[...] = bbi

    sd = jax.ShapeDtypeStruct
    return pl.pallas_call(
        body, name="s5_params",
        out_shape=[sd(lr.shape, F32), sd(lr.shape, F32), sd(btr.shape, F32), sd(btr.shape, F32)],
    )(lr, li, ls, btr, bti)


def _s5_params_bwd(lr, li, ls, btr, bti, dar, dai, dbbr, dbbi):
    def body(lr_ref, li_ref, ls_ref, br_ref, bi_ref, dar_ref, dai_ref, dbbr_ref, dbbi_ref,
             dlr_ref, dli_ref, dls_ref, dbr_ref, dbi_ref):
        _, vjp = jax.vjp(_s5_param_fn, lr_ref[...], li_ref[...], ls_ref[...], br_ref[...], bi_ref[...])
        dlr, dli, dls, dbr, dbi = vjp((dar_ref[...], dai_ref[...], dbbr_ref[...], dbbi_ref[...]))
        dlr_ref[...] = dlr
        dli_ref[...] = dli
        dls_ref[...] = dls
        dbr_ref[...] = dbr
        dbi_ref[...] = dbi

    sd = jax.ShapeDtypeStruct
    return pl.pallas_call(
        body, name="s5_params_bwd",
        out_shape=[sd(lr.shape, F32), sd(lr.shape, F32), sd(ls.shape, F32), sd(btr.shape, F32), sd(btr.shape, F32)],
    )(lr, li, ls, btr, bti, dar, dai, dbbr, dbbi)


def _cpow(ar, ai, n):
    assert n & (n - 1) == 0
    while n > 1:
        ar, ai = ar * ar - ai * ai, 2.0 * ar * ai
        n //= 2
    return ar, ai


def _scan(st, cr, ci, init, nk, reverse, store, prev=None):
    W = S5_W

    def step(j, carry):
        k = nk - 1 - j if reverse else j
        rows = pl.ds(pl.multiple_of(k * 8, 8), 8)
        sr, si = carry[0], carry[1]
        nsr = cr * sr - ci * si + st[rows, 0:W]
        nsi = cr * si + ci * sr + st[rows, W:2 * W]
        if store:
            st[rows, 0:W] = nsr
            st[rows, W:2 * W] = nsi
        if prev is None:
            return nsr, nsi
        prows = pl.ds(pl.multiple_of(jnp.maximum(k - 1, 0) * 8, 8), 8)
        w = jnp.where(k > 0, 1.0, 0.0).astype(F32)
        pr = prev[prows, 0:W] * w
        pi = prev[prows, W:2 * W] * w
        return nsr, nsi, carry[2] + nsr * pr + nsi * pi, carry[3] + nsi * pr - nsr * pi

    return lax.fori_loop(0, nk, step, init, unroll=2)


def _chain(fin, fr, fi, pr, pi, reverse):
    W = S5_W
    fin[:, 0:W] = fr
    fin[:, W:2 * W] = fi
    rowid = lax.broadcasted_iota(jnp.int32, (8, W), 0)
    cr = jnp.zeros((1, W), F32)
    ci = jnp.zeros((1, W), F32)
    init_r = jnp.zeros((8, W), F32)
    init_i = jnp.zeros((8, W), F32)
    for s in (range(7, -1, -1) if reverse else range(8)):
        init_r = jnp.where(rowid == s, cr, init_r)
        init_i = jnp.where(rowid == s, ci, init_i)
        lr = fin[s:s + 1, 0:W]
        li = fin[s:s + 1, W:2 * W]
        cr, ci = lr + pr * cr - pi * ci, li + pr * ci + pi * cr
    return init_r, init_i


def _full_scan(st, fin, ar, ai, nk, reverse, prev=None):
    W = S5_W
    cr = jnp.broadcast_to(ar, (8, W))
    ci = jnp.broadcast_to(-ai if reverse else ai, (8, W))
    z = jnp.zeros((8, W), F32)
    fr, fi = _scan(st, cr, ci, (z, z), nk, reverse, store=False)
    pr, pi = _cpow(ar, -ai if reverse else ai, nk)
    init = _chain(fin, fr, fi, pr, pi, reverse)
    if prev is None:
        return _scan(st, cr, ci, init, nk, reverse, store=True)
    return _scan(st, cr, ci, init + (z, z), nk, reverse, store=True, prev=prev)


def _s5_specs(L):
    W2 = 2 * S5_W
    GC = S5_GB * S5_C
    col = pl.BlockSpec((L, GC), lambda g: (0, g))
    vec = pl.BlockSpec((1, GC), lambda g: (0, g))
    avec = pl.BlockSpec((1, S5_W), lambda g: (0, g))
    bmat = pl.BlockSpec((None, GC, W2), lambda g: (g, 0, 0))
    cmat = pl.BlockSpec((None, W2, GC), lambda g: (g, 0, 0))
    return col, vec, avec, bmat, cmat


def _interleave(dst, src, nk):
    for s in range(8):
        dst[pl.ds(s, nk, stride=8), :] = src[s * nk:(s + 1) * nk, :]


def _deinterleave(dst, src, nk):
    for s in range(8):
        dst[s * nk:(s + 1) * nk, :] = src[pl.ds(s, nk, stride=8), :]


def _s5_fwd(u, bm, cm, ar, ai, dvec):
    L = u.shape[0]
    nk = L // 8
    GC = S5_GB * S5_C
    col, vec, avec, bmat, cmat = _s5_specs(L)

    def body(u_ref, b_ref, c_ref, ar_ref, ai_ref, d_ref, y_ref, st, fin, ui, yi):
        _interleave(ui, u_ref, nk)
        for r in range(8):
            rows = slice(r * nk, (r + 1) * nk)
            st[rows, :] = _dot(ui[rows, :].astype(BF16), b_ref[...])
        _full_scan(st, fin, ar_ref[...], ai_ref[...], nk, reverse=False)
        for r in range(8):
            rows = slice(r * nk, (r + 1) * nk)
            yi[rows, :] = _dot(st[rows, :].astype(BF16), c_ref[...]) + d_ref[...] * ui[rows, :]
        _deinterleave(y_ref, yi, nk)

    return pl.pallas_call(
        body, name="s5_fwd", grid=(S5_G // S5_GB,),
        in_specs=[col, bmat, cmat, avec, avec, vec], out_specs=col,
        out_shape=jax.ShapeDtypeStruct(u.shape, F32),
        scratch_shapes=[pltpu.VMEM((L, 2 * S5_W), F32), pltpu.VMEM((8, 2 * S5_W), F32), pltpu.VMEM((L, GC), F32),
                        pltpu.VMEM((L, GC), F32)],
        compiler_params=pltpu.CompilerParams(dimension_semantics=("arbitrary",), vmem_limit_bytes=VMEM_LIMIT),
    )(u, bm, cm, ar, ai, dvec)


def _s5_bwd(u, dy, bm, bmt, cmt, ar, ai, dvec, mask, rmat):
    L = u.shape[0]
    nk = L // 8
    W = S5_W
    GC = S5_GB * S5_C
    col, vec, avec, bmat, cmat = _s5_specs(L)
    hi = lax.Precision.HIGHEST

    def body(u_ref, dy_ref, b_ref, bt_ref, ct_ref, ar_ref, ai_ref, d_ref, mask_ref, r_ref,
             du_ref, db_ref, dc_ref, dd_ref, dar_ref, dai_ref, sa, sb, fin, ui, dyi, dui):
        ar = ar_ref[...]
        ai = ai_ref[...]
        _interleave(ui, u_ref, nk)
        _interleave(dyi, dy_ref, nk)
        for r in range(8):
            rows = slice(r * nk, (r + 1) * nk)
            sa[rows, :] = _dot(ui[rows, :].astype(BF16), b_ref[...])
            sb[rows, :] = _dot(dyi[rows, :].astype(BF16), ct_ref[...])
        _full_scan(sa, fin, ar, ai, nk, reverse=False)
        gr, gi, accr, acci = _full_scan(sb, fin, ar, ai, nk, reverse=True, prev=sa)
        rowid = lax.broadcasted_iota(jnp.int32, (8, W), 0)
        last = pl.ds((nk - 1) * 8, 8)
        pr = jnp.where(rowid == 0, 0.0, pltpu.roll(sa[last, 0:W], 1, 0))
        pi = jnp.where(rowid == 0, 0.0, pltpu.roll(sa[last, W:2 * W], 1, 0))
        accr = accr + gr * pr + gi * pi
        acci = acci + gi * pr - gr * pi
        dar_ref[...] = jnp.sum(accr, axis=0, keepdims=True)
        dai_ref[...] = jnp.sum(acci, axis=0, keepdims=True)
        dbf = jnp.zeros((GC, 2 * W), F32)
        dcf = jnp.zeros((GC, 2 * W), F32)
        dd = jnp.zeros((1, GC), F32)
        for r in range(8):
            rows = slice(r * nk, (r + 1) * nk)
            ub = ui[rows, :]
            dyb = dyi[rows, :]
            gb = sb[rows, :].astype(BF16)
            dui[rows, :] = _dot(gb, bt_ref[...]) + d_ref[...] * dyb
            dbf = dbf + _dot_tn(ub.astype(BF16), gb)
            dcf = dcf + _dot_tn(dyb.astype(BF16), sa[rows, :].astype(BF16))
            dd = dd + jnp.sum(dyb * ub, axis=0, keepdims=True)
        db_ref[...] = jnp.dot(dbf * mask_ref[...], r_ref[...], precision=hi, preferred_element_type=F32)
        dc_ref[...] = jnp.dot(dcf * mask_ref[...], r_ref[...], precision=hi, preferred_element_type=F32)
        dd_ref[...] = dd
        _deinterleave(du_ref, dui, nk)

    cmp_spec = pl.BlockSpec((GC, 2 * S5_P), lambda g: (g, 0))
    whole = lambda shape: pl.BlockSpec(shape, lambda g: (0, 0))
    sd = jax.ShapeDtypeStruct
    return pl.pallas_call(
        body, name="s5_bwd", grid=(S5_G // S5_GB,),
        in_specs=[col, col, bmat, cmat, bmat, avec, avec, vec, whole(mask.shape), whole(rmat.shape)],
        out_specs=[col, cmp_spec, cmp_spec, vec, avec, avec],
        out_shape=[sd(u.shape, F32), sd((S5_G * S5_C, 2 * S5_P), F32), sd((S5_G * S5_C, 2 * S5_P), F32),
                   sd((1, PRIM), F32), sd((1, S5_G * S5_P), F32), sd((1, S5_G * S5_P), F32)],
        scratch_shapes=[pltpu.VMEM((L, 2 * W), F32), pltpu.VMEM((L, 2 * W), F32), pltpu.VMEM((8, 2 * W), F32),
                        pltpu.VMEM((L, GC), F32), pltpu.VMEM((L, GC), F32), pltpu.VMEM((L, GC), F32)],
        compiler_params=pltpu.CompilerParams(dimension_semantics=("arbitrary",), vmem_limit_bytes=VMEM_LIMIT),
    )(u, dy, bm, bmt, cmt, ar, ai, dvec, mask, rmat)


def _s5_mats(bbr, bbi, cre, cim):
    nb = S5_G // S5_GB
    eye = jnp.eye(S5_GB, dtype=F32)
    bb = jnp.stack([bbr, bbi], axis=2).reshape(nb, S5_GB, S5_C, 2, S5_P)
    bm = jnp.einsum('ngcrp,gh->ngcrhp', bb, eye).reshape(nb, S5_GB * S5_C, 2 * S5_W)
    cc = jnp.stack([cre, -cim], axis=2).reshape(nb, S5_GB, S5_C, 2, S5_P)
    cmt = jnp.einsum('ngcrp,gh->ngcrhp', cc, eye).reshape(nb, S5_GB * S5_C, 2 * S5_W)
    return (bm.astype(BF16), jnp.swapaxes(bm, 1, 2).astype(BF16),
            jnp.swapaxes(cmt, 1, 2).astype(BF16), cmt.astype(BF16))


def _s5_compact_consts():
    g_row = np.arange(S5_GB * S5_C) // S5_C
    col = np.arange(2 * S5_W)
    g_col = (col % S5_W) // S5_P
    mask = (g_row[:, None] == g_col[None, :]).astype(np.float32)
    tgt = (col // S5_W) * S5_P + col % S5_P
    rmat = (tgt[:, None] == np.arange(2 * S5_P)[None, :]).astype(np.float32)
    return jnp.asarray(mask), jnp.asarray(rmat)


def _attn_scores(q, k_ref, kb, bq, scale, diagonal):
    rows = pl.ds(pl.multiple_of(kb * bq, bq), bq)
    s = _dot_nt(q, k_ref[rows, :]) * scale
    if diagonal:
        qpos = lax.broadcasted_iota(jnp.int32, (bq, bq), 0)
        kpos = lax.broadcasted_iota(jnp.int32, (bq, bq), 1)
        s = jnp.where(kpos <= qpos, s, NEG)
    return rows, s


def _attn_fwd(qp, kp, v, scale):
    L = qp.shape[0]
    bq = min(256, L)

    def body(q_ref, k_ref, v_ref, o_ref, lse_ref):
        qb = pl.program_id(1)
        q = q_ref[...]

        def block(kb, carry, diagonal):
            m, l, acc = carry
            rows, s = _attn_scores(q, k_ref, kb, bq, scale, diagonal)
            m2 = jnp.maximum(m, jnp.max(s, axis=-1, keepdims=True))
            a = jnp.exp(m - m2)
            p = jnp.exp(s - m2)
            return m2, a * l + jnp.sum(p, axis=-1, keepdims=True), a * acc + _dot(p.astype(BF16), v_ref[rows, :])

        init = (jnp.full((bq, 1), NEG, F32), jnp.zeros((bq, 1), F32), jnp.zeros((bq, HD), F32))
        carry = lax.fori_loop(0, qb, lambda kb, c: block(kb, c, False), init)
        m, l, acc = block(qb, carry, True)
        o_ref[...] = acc / l
        lse_ref[...] = jnp.broadcast_to(m + jnp.log(l), (bq, HD))

    blk = pl.BlockSpec((bq, HD), lambda h, i: (i, h))
    return pl.pallas_call(
        body, name="mla_attn_fwd", grid=(MLA_H, L // bq),
        in_specs=[pl.BlockSpec((bq, 2 * HD), lambda h, i: (i, h)), pl.BlockSpec((L, 2 * HD), lambda h, i: (0, h)),
                  pl.BlockSpec((L, HD), lambda h, i: (0, h))],
        out_specs=[blk, blk],
        out_shape=[jax.ShapeDtypeStruct((L, MLA_H * HD), F32)] * 2,
        compiler_params=pltpu.CompilerParams(dimension_semantics=("arbitrary", "arbitrary"),
                                             vmem_limit_bytes=VMEM_LIMIT),
    )(qp, kp, v)


def _attn_bwd(qp, kp, v, o, lse, do, scale):
    L = qp.shape[0]
    bq = min(256, L)
    nq = L // bq

    def body(q_ref, k_ref, v_ref, o_ref, lse_ref, do_ref, dq_ref, dk_ref, dv_ref, dk_acc, dv_acc):
        qb = pl.program_id(1)

        @pl.when(qb == 0)
        def _():
            dk_acc[...] = jnp.zeros_like(dk_acc)
            dv_acc[...] = jnp.zeros_like(dv_acc)

        q = q_ref[...]
        do = do_ref[...]
        dob = do.astype(BF16)
        lse = lse_ref[:, 0:1]
        dsum = jnp.sum(do * o_ref[...], axis=-1, keepdims=True)

        def block(kb, dq, diagonal):
            rows, s = _attn_scores(q, k_ref, kb, bq, scale, diagonal)
            p = jnp.exp(s - lse)
            dp = _dot_nt(dob, v_ref[rows, :])
            ds = (p * (dp - dsum) * scale).astype(BF16)
            dv_acc[rows, :] += _dot_tn(p.astype(BF16), dob)
            dk_acc[rows, :] += _dot_tn(ds, q)
            return dq + _dot(ds, k_ref[rows, :])

        dq = lax.fori_loop(0, qb, lambda kb, c: block(kb, c, False), jnp.zeros((bq, 2 * HD), F32))
        dq_ref[...] = block(qb, dq, True)

        @pl.when(qb == nq - 1)
        def _():
            dk_ref[...] = dk_acc[...]
            dv_ref[...] = dv_acc[...]

    sd = jax.ShapeDtypeStruct
    blk = pl.BlockSpec((bq, HD), lambda h, i: (i, h))
    return pl.pallas_call(
        body, name="mla_attn_bwd", grid=(MLA_H, nq),
        in_specs=[pl.BlockSpec((bq, 2 * HD), lambda h, i: (i, h)), pl.BlockSpec((L, 2 * HD), lambda h, i: (0, h)),
                  pl.BlockSpec((L, HD), lambda h, i: (0, h)), blk, blk, blk],
        out_specs=[pl.BlockSpec((bq, 2 * HD), lambda h, i: (i, h)), pl.BlockSpec((L, 2 * HD), lambda h, i: (0, h)),
                   pl.BlockSpec((L, HD), lambda h, i: (0, h))],
        out_shape=[sd((L, MLA_H * 2 * HD), F32), sd((L, MLA_H * 2 * HD), F32), sd((L, MLA_H * HD), F32)],
        scratch_shapes=[pltpu.VMEM((L, 2 * HD), F32), pltpu.VMEM((L, HD), F32)],
        compiler_params=pltpu.CompilerParams(dimension_semantics=("arbitrary", "arbitrary"),
                                             vmem_limit_bytes=VMEM_LIMIT),
    )(qp, kp, v, o, lse, do)


def _kv_fn(mem, gm, w, gk):
    kv = _mm(_rms(mem, gm, D_MODEL), w)
    k = jnp.concatenate([_rms(kv[:, HD * h:HD * (h + 1)], gk, HD) for h in range(X_HEADS)], axis=-1)
    return k, kv[:, XQ:]


def _kv_prep(mem, gm, w, gk, name):
    def fn(mem, gm, w, gk):
        return _kv_fn(mem, gm, w, gk)
    M = mem.shape[0]
    return _rowwise(name, fn, [('c', mem), ('c', gm), ('c', w), ('c', gk)],
                    [('c', (M, XQ), F32), ('c', (M, XQ), F32)], 1)


def _kv_prep_bwd(mem, gm, w, gk, dk, dv, name):
    def fn(mem, gm, w, gk, dk, dv):
        _, vjp = jax.vjp(lambda a, b, c: _kv_fn(mem, a, b, c), gm, w, gk)
        return vjp((dk, dv))
    return _rowwise(name, fn, [('c', mem), ('c', gm), ('c', w), ('c', gk), ('c', dk), ('c', dv)],
                    [('c', gm.shape, F32), ('c', w.shape, BF16), ('c', gk.shape, F32)], 1)


def _forward_merge(x, mix, mix_kind, xq, gate, k, v, gq, wout, name, nblk, sub):
    def fn(x, mix, xq, gate, k, v, gq, wout):
        o = _merge(mix, xq, gate, k, v, gq)
        return (x + _dot(o.astype(BF16), wout),)
    L = x.shape[0]
    return _rowwise(name, fn, [('r', x), (mix_kind, mix), ('r', xq), ('r', gate), ('c', k), ('c', v), ('c', gq),
                               ('c', wout)], [('r', (L, D_MODEL), F32)], nblk, sub)[0]


def _backward_merge(dx, mix, mix_kind, xq, gate, k, v, gq, wout, name, nblk, sub):
    def fn(dx, mix, xq, gate, k, v, gq, wout):
        g16 = dx.astype(BF16)
        do = _dot_nt(g16, wout)
        o, vjp = jax.vjp(_merge, mix, xq, gate, k, v, gq)
        dmix, dxq, dgate, dk, dv, dgq = vjp(do)
        return dmix, dxq, dgate, o, g16, dk, dv, dgq
    L = dx.shape[0]
    return _rowwise(
        name, fn,
        [('r', dx), (mix_kind, mix), ('r', xq), ('r', gate), ('c', k), ('c', v), ('c', gq), ('c', wout)],
        [('r', (L, PRIM), F32), ('r', (L, XQ), F32), ('r', (L, BRANCH), F32), ('r', (L, BRANCH), BF16),
         ('r', (L, D_MODEL), BF16), ('a', k.shape, F32), ('a', v.shape, F32), ('a', gq.shape, F32)], nblk, sub)


_PACK = (("w_out", 512), ("s5_w_in", 512), ("w_mem_kv", 256), ("s5_w_glu", 576), ("mla_w_in", 432),
         ("mla_w_uq", 144), ("mla_w_ukv", 96), ("lora", 16))
_PACK_ROWS = sum(n for _, n in _PACK)
_PACK_OFF = {name: sum(n for _, n in _PACK[:i]) for i, (name, _) in enumerate(_PACK)}
_MLA_IN = 3392
_MLA_IN_PAD = 3456


def _rows1024(a, rows):
    flat = a.reshape(-1, 1024)
    if flat.shape[0] < rows:
        flat = jnp.pad(flat, ((0, rows - flat.shape[0]), (0, 0)))
    return flat


def _pack_shards(w_out, s5_w_in, w_mem_kv, s5_w_glu, mla_w_in, mla_w_uq, mla_w_ukv, q_lora, kv_lora):
    lora = jnp.concatenate([q_lora.reshape(-1), kv_lora.reshape(-1)])
    lora16 = lax.bitcast_convert_type(lora, BF16).reshape(-1)
    parts = {"w_out": w_out, "s5_w_in": s5_w_in, "w_mem_kv": w_mem_kv, "s5_w_glu": s5_w_glu, "mla_w_in": mla_w_in,
             "mla_w_uq": mla_w_uq, "mla_w_ukv": mla_w_ukv}
    pieces = [_rows1024(parts[n].astype(BF16), r) for n, r in _PACK[:-1]]
    pieces.append(jnp.pad(lora16, (0, 16 * 1024 - lora16.shape[0])).reshape(16, 1024))
    return jnp.concatenate(pieces, axis=0)


def _piece(g, name, real_rows=None):
    off = _PACK_OFF[name]
    n = dict(_PACK)[name] if real_rows is None else real_rows
    return g[:, off:off + n, :]


def _mla_in_perm(w):
    return jnp.concatenate([w[:, :768], w[:, 832:], w[:, 768:832], jnp.zeros((w.shape[0], 64), w.dtype)], axis=1)


def _mla_in_unperm(w):
    return jnp.concatenate([w[:, :768], w[:, 3328:3392], w[:, 768:3328]], axis=1)


def _unpack_weights(g):
    def cols(name, k, n, real_rows=None):
        return jnp.transpose(_piece(g, name, real_rows).reshape(N_DEV, k, n), (1, 0, 2)).reshape(k, N_DEV * n)

    w_out = jnp.transpose(_piece(g, "w_out").reshape(N_DEV, 2, 256, 1024), (1, 0, 2, 3)).reshape(2, 2048, 1024)
    w_mem_kv = jnp.transpose(_piece(g, "w_mem_kv").reshape(N_DEV, 2, 128, 1024), (1, 0, 2, 3)).reshape(2, 1024, 1024)
    s5_w_in = cols("s5_w_in", 1024, 512)
    s5_w_glu = cols("s5_w_glu", 1536, 384)
    mla_w_in = _mla_in_perm(cols("mla_w_in", 1024, 424, 424))
    uq = cols("mla_w_uq", 512, 288).reshape(Q_LORA, MLA_H, HD + ROPE)
    w_q = jnp.concatenate([uq[:, :, :HD].reshape(Q_LORA, PRIM),
                           jnp.pad(uq[:, :, HD:], ((0, 0), (0, 0), (0, HD - ROPE))).reshape(Q_LORA, PRIM)], axis=1)
    ukv = cols("mla_w_ukv", 256, 384).reshape(KV_LORA, MLA_H, 2 * HD)
    w_kv = jnp.concatenate([ukv[:, :, :HD].reshape(KV_LORA, PRIM), ukv[:, :, HD:].reshape(KV_LORA, PRIM)], axis=1)
    lora = lax.bitcast_convert_type(_piece(g, "lora")[:, 0, :192].reshape(N_DEV, 96, 2), F32)
    q_lora = lora[:, :64].reshape(1, Q_LORA)
    kv_lora = lora[:, 64:].reshape(1, KV_LORA)
    return w_out, w_mem_kv, s5_w_in, s5_w_glu, mla_w_in, w_q, w_kv, q_lora, kv_lora


def _pack_grads(d_w_out, d_s5_w_in, d_w_mem_kv, d_s5_w_glu, d_mla_w_in, d_w_q, d_w_kv):
    def cols(w, n, rows):
        k = w.shape[0]
        s = jnp.transpose(w.reshape(k, N_DEV, n), (1, 0, 2)).reshape(N_DEV, -1, 1024)
        if s.shape[1] < rows:
            s = jnp.pad(s, ((0, 0), (0, rows - s.shape[1]), (0, 0)))
        return s

    uq = jnp.concatenate([d_w_q[:, :PRIM].reshape(Q_LORA, MLA_H, HD),
                          d_w_q[:, PRIM:].reshape(Q_LORA, MLA_H, HD)[:, :, :ROPE]], axis=2).reshape(Q_LORA, 2304)
    ukv = jnp.concatenate([d_w_kv[:, :PRIM].reshape(KV_LORA, MLA_H, HD),
                           d_w_kv[:, PRIM:].reshape(KV_LORA, MLA_H, HD)], axis=2).reshape(KV_LORA, 3072)
    parts = {
        "w_out": jnp.transpose(d_w_out.reshape(2, N_DEV, 256, 1024), (1, 0, 2, 3)).reshape(N_DEV, 512, 1024),
        "s5_w_in": cols(d_s5_w_in, 512, 512),
        "w_mem_kv": jnp.transpose(d_w_mem_kv.reshape(2, N_DEV, 128, 1024), (1, 0, 2, 3)).reshape(N_DEV, 256, 1024),
        "s5_w_glu": cols(d_s5_w_glu, 384, 576),
        "mla_w_in": cols(_mla_in_unperm(d_mla_w_in), 424, 432),
        "mla_w_uq": cols(uq, 288, 144),
        "mla_w_ukv": cols(ukv, 384, 96),
        "lora": jnp.zeros((N_DEV, 16, 1024), BF16),
    }
    return jnp.concatenate([parts[n] for n, _ in _PACK], axis=1)


_SMALL = (("ln_gain", 2048), ("mem_norm", 2048), ("xq_norm", 256), ("xk_norm", 256), ("s5_lambda_re", 6144),
          ("s5_lambda_im", 6144), ("s5_log_step", 96), ("s5_b_re", 98304), ("s5_b_im", 98304), ("s5_c_re", 98304),
          ("s5_c_im", 98304), ("s5_d", 1536), ("mla_q_lora_norm", 512), ("mla_kv_lora_norm", 256),
          ("mla_q_nope_norm", 128), ("mla_k_nope_norm", 128), ("mla_q_rope_norm", 64), ("mla_k_rope_norm", 64))
_SMALL_ROWS = 408
_SMALL_OFF = {name: sum(n for _, n in _SMALL[:i]) for i, (name, _) in enumerate(_SMALL)}


def _pack_small(d):
    flat = jnp.concatenate([d[n].reshape(-1).astype(F32) for n, _ in _SMALL])
    return jnp.pad(flat, (0, _SMALL_ROWS * 1024 - flat.shape[0])).reshape(_SMALL_ROWS, 1024)


def _unpack_small(p, name, shape):
    off = _SMALL_OFF[name]
    return p.reshape(-1)[off:off + int(np.prod(shape))].reshape(shape)


_WEIGHTS = ('ln_gain', 'w_out', 'mem_norm', 'w_mem_kv', 'xq_norm', 'xk_norm', 's5_w_in', 's5_lambda_re',
            's5_lambda_im', 's5_log_step', 's5_b_re', 's5_b_im', 's5_c_re', 's5_c_im', 's5_d', 's5_w_glu', 'mla_w_in',
            'mla_q_lora_norm', 'mla_kv_lora_norm', 'mla_w_uq', 'mla_w_ukv', 'mla_q_nope_norm', 'mla_k_nope_norm',
            'mla_q_rope_norm', 'mla_k_rope_norm')
_BIG = ('w_out', 'w_mem_kv', 's5_w_in', 's5_w_glu', 'mla_w_in', 'mla_w_uq', 'mla_w_ukv')


def _pad128(g):
    return jnp.pad(g.reshape(1, -1), ((0, 0), (0, HD - g.shape[-1])))


def kernel(x, mem, positions, ln_gain, w_out, mem_norm, w_mem_kv, xq_norm, xk_norm, s5_w_in, s5_lambda_re, s5_lambda_im, s5_log_step, s5_b_re, s5_b_im, s5_c_re, s5_c_im, s5_d, s5_w_glu, mla_w_in, mla_q_lora_norm, mla_kv_lora_norm, mla_w_uq, mla_w_ukv, mla_q_nope_norm, mla_k_nope_norm, mla_q_rope_norm, mla_k_rope_norm, loss_target, m_ln_gain, m_w_out, m_mem_norm, m_w_mem_kv, m_xq_norm, m_xk_norm, m_s5_w_in, m_s5_lambda_re, m_s5_lambda_im, m_s5_log_step, m_s5_b_re, m_s5_b_im, m_s5_c_re, m_s5_c_im, m_s5_d, m_s5_w_glu, m_mla_w_in, m_mla_q_lora_norm, m_mla_kv_lora_norm, m_mla_w_uq, m_mla_w_ukv, m_mla_q_nope_norm, m_mla_k_nope_norm, m_mla_q_rope_norm, m_mla_k_rope_norm, v_ln_gain, v_w_out, v_mem_norm, v_w_mem_kv, v_xq_norm, v_xk_norm, v_s5_w_in, v_s5_lambda_re, v_s5_lambda_im, v_s5_log_step, v_s5_b_re, v_s5_b_im, v_s5_c_re, v_s5_c_im, v_s5_d, v_s5_w_glu, v_mla_w_in, v_mla_q_lora_norm, v_mla_kv_lora_norm, v_mla_w_uq, v_mla_w_ukv, v_mla_q_nope_norm, v_mla_k_nope_norm, v_mla_q_rope_norm, v_mla_k_rope_norm):
    weights = dict(ln_gain=ln_gain, w_out=w_out, mem_norm=mem_norm, w_mem_kv=w_mem_kv, xq_norm=xq_norm,
                   xk_norm=xk_norm, s5_w_in=s5_w_in, s5_lambda_re=s5_lambda_re, s5_lambda_im=s5_lambda_im,
                   s5_log_step=s5_log_step, s5_b_re=s5_b_re, s5_b_im=s5_b_im, s5_c_re=s5_c_re, s5_c_im=s5_c_im,
                   s5_d=s5_d, s5_w_glu=s5_w_glu, mla_w_in=mla_w_in, mla_q_lora_norm=mla_q_lora_norm,
                   mla_kv_lora_norm=mla_kv_lora_norm, mla_w_uq=mla_w_uq, mla_w_ukv=mla_w_ukv,
                   mla_q_nope_norm=mla_q_nope_norm, mla_k_nope_norm=mla_k_nope_norm,
                   mla_q_rope_norm=mla_q_rope_norm, mla_k_rope_norm=mla_k_rope_norm)
    m_in = dict(zip(_WEIGHTS, (m_ln_gain, m_w_out, m_mem_norm, m_w_mem_kv, m_xq_norm, m_xk_norm, m_s5_w_in,
                               m_s5_lambda_re, m_s5_lambda_im, m_s5_log_step, m_s5_b_re, m_s5_b_im, m_s5_c_re,
                               m_s5_c_im, m_s5_d, m_s5_w_glu, m_mla_w_in, m_mla_q_lora_norm, m_mla_kv_lora_norm,
                               m_mla_w_uq, m_mla_w_ukv, m_mla_q_nope_norm, m_mla_k_nope_norm, m_mla_q_rope_norm,
                               m_mla_k_rope_norm)))
    v_in = dict(zip(_WEIGHTS, (v_ln_gain, v_w_out, v_mem_norm, v_w_mem_kv, v_xq_norm, v_xk_norm, v_s5_w_in,
                               v_s5_lambda_re, v_s5_lambda_im, v_s5_log_step, v_s5_b_re, v_s5_b_im, v_s5_c_re,
                               v_s5_c_im, v_s5_d, v_s5_w_glu, v_mla_w_in, v_mla_q_lora_norm, v_mla_kv_lora_norm,
                               v_mla_w_uq, v_mla_w_ukv, v_mla_q_nope_norm, v_mla_k_nope_norm, v_mla_q_rope_norm,
                               v_mla_k_rope_norm)))

    x0 = x[0]
    mem0 = mem[0]
    target = loss_target[0]
    L = x0.shape[0]
    nblk, sub = 8, 1
    me = 4 * lax.axis_index("x") + 2 * lax.axis_index("y") + lax.axis_index("c")

    packed = _pack_shards(w_out, s5_w_in[0], w_mem_kv, s5_w_glu[0], mla_w_in[0], mla_w_uq[0], mla_w_ukv[0],
                          mla_q_lora_norm, mla_kv_lora_norm)
    gathered = _all_gather(packed, "ag_weights")
    W_out, W_mkv, W_in_s5, W_glu, W_in_mla, W_q, W_kv, g_qlora, g_kvlora = _unpack_weights(gathered)

    ln0, ln1 = ln_gain[0:1], ln_gain[1:2]
    gq0, gq1 = xq_norm[0:1], xq_norm[1:2]
    gk0, gk1 = xk_norm[0:1], xk_norm[1:2]
    gm0, gm1 = mem_norm[0:1], mem_norm[1:2]
    gqn, gkn = mla_q_nope_norm, mla_k_nope_norm
    gqr, gkr = _pad128(mla_q_rope_norm), _pad128(mla_k_rope_norm)

    lr3 = s5_lambda_re.reshape(S5_G, 1, S5_P)
    li3 = s5_lambda_im.reshape(S5_G, 1, S5_P)
    ls3 = s5_log_step.reshape(S5_G, 1, 1)
    btr = jnp.swapaxes(s5_b_re[0], 1, 2)
    bti = jnp.swapaxes(s5_b_im[0], 1, 2)
    a_r, a_i, bbr, bbi = _s5_params(lr3, li3, ls3, btr, bti)
    bm, bmt, cm, cmt = _s5_mats(bbr, bbi, s5_c_re[0], s5_c_im[0])
    a_r2 = a_r.reshape(1, S5_G * S5_P)
    a_i2 = a_i.reshape(1, S5_G * S5_P)
    cmask, rmat = _s5_compact_consts()

    half = ROPE // 2
    inv_freq = ROPE_THETA ** (-jnp.arange(half, dtype=F32) / half)
    invf = jnp.concatenate([inv_freq, inv_freq, jnp.zeros((HD - ROPE,), F32)]).reshape(1, HD)

    def rot_tables(pos, invf):
        ang = pos.astype(F32) * invf
        lane = lax.broadcasted_iota(jnp.int32, ang.shape, 1)
        c = jnp.where(lane < ROPE, jnp.cos(ang), 0.0)
        s = jnp.sin(ang)
        return c, jnp.where(lane < half, -s, 0.0), jnp.where((lane >= half) & (lane < ROPE), s, 0.0)

    tc, ts1, ts2 = _rowwise("rot_tables", rot_tables, [('r', positions.reshape(L, 1)), ('c', invf)],
                            [('r', (L, HD), F32)] * 3, nblk, sub)

    def in_s5(x, g, w):
        proj = _dot(_rms(x, g, D_MODEL).astype(BF16), w)
        return proj[:, :PRIM], proj[:, PRIM:PRIM + XQ], proj[:, PRIM + XQ:]

    u_s5, xq_a, gate_a = _rowwise("s5_in", in_s5, [('r', x0), ('c', ln0), ('c', W_in_s5)],
                                  [('r', (L, PRIM), F32), ('r', (L, XQ), F32), ('r', (L, BRANCH), F32)], nblk, sub)
    y_s5 = _s5_fwd(u_s5, bm, cm, a_r2, a_i2, s5_d)

    def glu(y, w):
        z = _dot(_gelu(y).astype(BF16), w)
        return (z[:, :PRIM] * _sigmoid(z[:, PRIM:]),)

    y2 = _rowwise("s5_glu", glu, [('r', y_s5), ('c', W_glu)], [('r', (L, PRIM), F32)], nblk, sub)[0]
    k_a, v_a = _kv_prep(mem0, gm0, W_mkv[0], gk0, "kv_prep0")
    x1 = _forward_merge(x0, y2, 'r', xq_a, gate_a, k_a, v_a, gq0, W_out[0], "merge0", nblk, sub)

    def in_mla(x, g, w):
        proj = _dot(_rms(x, g, D_MODEL).astype(BF16), w)
        return proj[:, :512], proj[:, 512:768], proj[:, 768:1280], proj[:, 1280:3328], proj[:, 3328:]

    c_q, c_kv, xq_b, gate_b, krp = _rowwise(
        "mla_in", in_mla, [('r', x1), ('c', ln1), ('c', W_in_mla)],
        [('r', (L, Q_LORA), F32), ('r', (L, KV_LORA), F32), ('r', (L, XQ), F32), ('r', (L, BRANCH), F32),
         ('r', (L, HD), F32)], nblk, sub)

    def qkv(c_q, c_kv, krp, tc, ts1, ts2, gql, gkvl, wq, wkv, gqn, gkn, gqr, gkr):
        q = _dot(_rms(c_q, gql, Q_LORA).astype(BF16), wq)
        kv = _dot(_rms(c_kv, gkvl, KV_LORA).astype(BF16), wkv)
        kp, v = _kv_post(kv, krp, gkn, gkr, tc, ts1, ts2)
        return _q_post(q, gqn, gqr, tc, ts1, ts2), kp, v

    qkv_consts = [('c', g_qlora), ('c', g_kvlora), ('c', W_q), ('c', W_kv), ('c', gqn), ('c', gkn), ('c', gqr),
                  ('c', gkr)]
    q_pad, k_pad, v_h = _rowwise(
        "mla_qkv", qkv, [('r', c_q), ('r', c_kv), ('r', krp), ('r', tc), ('r', ts1), ('r', ts2)] + qkv_consts,
        [('r', (L, 2 * PRIM), BF16), ('r', (L, 2 * PRIM), BF16), ('r', (L, PRIM), BF16)], nblk, sub)
    scale = (HD + ROPE) ** -0.5
    attn, lse = _attn_fwd(q_pad, k_pad, v_h, scale)
    k_b, v_b = _kv_prep(mem0, gm1, W_mkv[1], gk1, "kv_prep1")
    x2 = _forward_merge(x1, attn, 'r', xq_b, gate_b, k_b, v_b, gq1, W_out[1], "merge1", nblk, sub)

    def loss_fn(y, t):
        err = y - t
        part = 0.5 * jnp.sum(jnp.sum(err * err, axis=-1, keepdims=True) * (1.0 / D_MODEL), axis=0, keepdims=True)
        return err * (1.0 / D_MODEL), jnp.broadcast_to(part, (1, HD))

    dx2, loss_part = _rowwise("loss", loss_fn, [('r', x2), ('r', target)],
                              [('r', (L, D_MODEL), F32), ('a', (1, HD), F32)], nblk, sub)
    loss = lax.psum(loss_part[0, 0], ("x", "y", "c"))

    dattn, dxq_b, dgate_b, o_b, g_b, dk_b, dv_b, dgq1 = _backward_merge(
        dx2, attn, 'r', xq_b, gate_b, k_b, v_b, gq1, W_out[1], "merge1_bwd", nblk, sub)
    dgm1, dW_mkv1, dgk1 = _kv_prep_bwd(mem0, gm1, W_mkv[1], gk1, dk_b, dv_b, "kv_prep1_bwd")
    dW_out1 = _matmul_tn(o_b, g_b, "dw_out1")
    dq_pad, dk_pad, dv_h = _attn_bwd(q_pad, k_pad, v_h, attn, lse, dattn, scale)

    def qkv_bwd(c_q, c_kv, krp, tc, ts1, ts2, dqp, dkp, dv, gql, gkvl, wq, wkv, gqn, gkn, gqr, gkr):
        cqn, vjp_qn = jax.vjp(lambda a, b: _rms(a, b, Q_LORA), c_q, gql)
        ckvn, vjp_kvn = jax.vjp(lambda a, b: _rms(a, b, KV_LORA), c_kv, gkvl)
        cqn16 = cqn.astype(BF16)
        ckvn16 = ckvn.astype(BF16)
        q = _dot(cqn16, wq)
        kv = _dot(ckvn16, wkv)
        _, vjp_q = jax.vjp(lambda a, b, c: _q_post(a, b, c, tc, ts1, ts2), q, gqn, gqr)
        dq, dgqn, dgqr = vjp_q(dqp)
        _, vjp_kv = jax.vjp(lambda a, b, c, d: _kv_post(a, b, c, d, tc, ts1, ts2), kv, krp, gkn, gkr)
        dkv, dkrp, dgkn, dgkr = vjp_kv((dkp, dv))
        dq16 = dq.astype(BF16)
        dkv16 = dkv.astype(BF16)
        dc_q, dgql = vjp_qn(_dot_nt(dq16, wq))
        dc_kv, dgkvl = vjp_kvn(_dot_nt(dkv16, wkv))
        return dc_q, dc_kv, dkrp, cqn16, dq16, ckvn16, dkv16, dgql, dgkvl, dgqn, dgkn, dgqr, dgkr

    (dc_q, dc_kv, dkrp, cqn16, dq16, ckvn16, dkv16, dgql, dgkvl, dgqn, dgkn, dgqr, dgkr) = _rowwise(
        "mla_qkv_bwd", qkv_bwd,
        [('r', c_q), ('r', c_kv), ('r', krp), ('r', tc), ('r', ts1), ('r', ts2), ('r', dq_pad), ('r', dk_pad),
         ('r', dv_h)] + qkv_consts,
        [('r', (L, Q_LORA), F32), ('r', (L, KV_LORA), F32), ('r', (L, HD), F32), ('r', (L, Q_LORA), BF16),
         ('r', (L, 2 * PRIM), BF16), ('r', (L, KV_LORA), BF16), ('r', (L, 2 * PRIM), BF16),
         ('a', (1, Q_LORA), F32), ('a', (1, KV_LORA), F32), ('a', (1, HD), F32), ('a', (1, HD), F32),
         ('a', (1, HD), F32), ('a', (1, HD), F32)], nblk, sub)
    dW_q = _matmul_tn(cqn16, dq16, "dw_uq")
    dW_kv = _matmul_tn(ckvn16, dkv16, "dw_ukv")

    def in_bwd(x, dres, g, w, *dparts):
        dproj = jnp.concatenate(dparts, axis=-1).astype(BF16)
        xn, vjp = jax.vjp(lambda a, b: _rms(a, b, D_MODEL), x, g)
        dx, dg = vjp(_dot_nt(dproj, w))
        return dx + dres, xn, dproj, dg

    dx1, xn1, dproj1, dln1 = _rowwise(
        "mla_in_bwd", in_bwd,
        [('r', x1), ('r', dx2), ('c', ln1), ('c', W_in_mla), ('r', dc_q), ('r', dc_kv), ('r', dxq_b), ('r', dgate_b),
         ('r', dkrp)],
        [('r', (L, D_MODEL), F32), ('r', (L, D_MODEL), BF16), ('r', (L, _MLA_IN_PAD), BF16), ('a', (1, D_MODEL), F32)],
        nblk, sub)
    dW_in_mla = _matmul_tn(xn1, dproj1, "dw_mla_in")

    dy2, dxq_a, dgate_a, o_a, g_a, dk_a, dv_a, dgq0 = _backward_merge(
        dx1, y2, 'r', xq_a, gate_a, k_a, v_a, gq0, W_out[0], "merge0_bwd", nblk, sub)
    dgm0, dW_mkv0, dgk0 = _kv_prep_bwd(mem0, gm0, W_mkv[0], gk0, dk_a, dv_a, "kv_prep0_bwd")
    dW_out0 = _matmul_tn(o_a, g_a, "dw_out0")

    def glu_bwd(y, dy2, w):
        h, vjp_h = jax.vjp(_gelu, y)
        h16 = h.astype(BF16)
        z = _dot(h16, w)
        _, vjp_z = jax.vjp(lambda z: z[:, :PRIM] * _sigmoid(z[:, PRIM:]), z)
        dz16 = vjp_z(dy2)[0].astype(BF16)
        return vjp_h(_dot_nt(dz16, w))[0], h16, dz16

    dy_s5, h16, dz16 = _rowwise("s5_glu_bwd", glu_bwd, [('r', y_s5), ('r', dy2), ('c', W_glu)],
                                [('r', (L, PRIM), F32), ('r', (L, PRIM), BF16), ('r', (L, 2 * PRIM), BF16)],
                                nblk, sub)
    dW_glu = _matmul_tn(h16, dz16, "dw_glu")
    du_s5, dbc, dcc, dd, dar, dai = _s5_bwd(u_s5, dy_s5, bm, bmt, cmt, a_r2, a_i2, s5_d, cmask, rmat)
    dx0, xn0, dproj0, dln0 = _rowwise(
        "s5_in_bwd", in_bwd,
        [('r', x0), ('r', dx1), ('c', ln0), ('c', W_in_s5), ('r', du_s5), ('r', dxq_a),
         ('r', dgate_a)],
        [('r', (L, D_MODEL), F32), ('r', (L, D_MODEL), BF16), ('r', (L, 2 * BRANCH), BF16), ('a', (1, D_MODEL), F32)],
        nblk, sub)
    dW_in_s5 = _matmul_tn(xn0, dproj0, "dw_s5_in")

    dbc4 = dbc.reshape(S5_G, S5_C, 2, S5_P)
    dcc4 = dcc.reshape(S5_G, S5_C, 2, S5_P)
    dlr, dli, dls, dbtr, dbti = _s5_params_bwd(
        lr3, li3, ls3, btr, bti, dar.reshape(S5_G, 1, S5_P), dai.reshape(S5_G, 1, S5_P), dbc4[:, :, 0], dbc4[:, :, 1])

    send = _pack_grads(jnp.stack([dW_out0, dW_out1]), dW_in_s5, jnp.stack([dW_mkv0, dW_mkv1]), dW_glu, dW_in_mla,
                       dW_q, dW_kv)
    recv = _all_to_all(send, "rs_grads")
    gsum = _sum_slots(recv, "rs_sum", 48)

    def shard(name, shape, real_rows=None):
        off = _PACK_OFF[name]
        n = dict(_PACK)[name] if real_rows is None else real_rows
        return gsum[off:off + n].reshape(shape)

    grads = {
        "w_out": shard("w_out", w_out.shape), "w_mem_kv": shard("w_mem_kv", w_mem_kv.shape),
        "s5_w_in": shard("s5_w_in", s5_w_in.shape), "s5_w_glu": shard("s5_w_glu", s5_w_glu.shape),
        "mla_w_in": shard("mla_w_in", mla_w_in.shape, 424), "mla_w_uq": shard("mla_w_uq", mla_w_uq.shape),
        "mla_w_ukv": shard("mla_w_ukv", mla_w_ukv.shape),
    }

    small_part = {
        "ln_gain": jnp.concatenate([dln0, dln1]), "mem_norm": jnp.concatenate([dgm0, dgm1]),
        "xq_norm": jnp.concatenate([dgq0, dgq1]), "xk_norm": jnp.concatenate([dgk0, dgk1]),
        "s5_lambda_re": dlr, "s5_lambda_im": dli, "s5_log_step": dls,
        "s5_b_re": jnp.swapaxes(dbtr, 1, 2), "s5_b_im": jnp.swapaxes(dbti, 1, 2),
        "s5_c_re": dcc4[:, :, 0], "s5_c_im": -dcc4[:, :, 1], "s5_d": dd,
        "mla_q_lora_norm": dgql, "mla_kv_lora_norm": dgkvl, "mla_q_nope_norm": dgqn, "mla_k_nope_norm": dgkn,
        "mla_q_rope_norm": dgqr[:, :ROPE], "mla_k_rope_norm": dgkr[:, :ROPE],
    }
    small_gath = _all_gather(_pack_small(small_part), "ag_small_grads")

    def whole(name, a):
        if name == "mla_q_lora_norm":
            return lax.dynamic_update_slice(jnp.zeros((Q_LORA,), F32), a.reshape(-1), (me * 64,))
        if name == "mla_kv_lora_norm":
            return lax.dynamic_update_slice(jnp.zeros((KV_LORA,), F32), a.reshape(-1), (me * 32,))
        return a

    wp = _pack_small({n: whole(n, weights[n]) for n, _ in _SMALL})
    mp = _pack_small({n: whole(n, m_in[n]) for n, _ in _SMALL})
    vp = _pack_small({n: whole(n, v_in[n]) for n, _ in _SMALL})
    gs, ds, ms, vs = _small_update(small_gath, wp, mp, vp, "small_update")

    delta, new_m, new_v = {}, {}, {}
    for n, _ in _SMALL:
        shape = weights[n].shape
        if n == "mla_q_lora_norm":
            take = lambda p: lax.dynamic_slice(_unpack_small(p, n, (Q_LORA,)), (me * 64,), (64,)).reshape(shape)
        elif n == "mla_kv_lora_norm":
            take = lambda p: lax.dynamic_slice(_unpack_small(p, n, (KV_LORA,)), (me * 32,), (32,)).reshape(shape)
        else:
            take = lambda p: _unpack_small(p, n, shape)
        grads[n], delta[n], new_m[n], new_v[n] = take(gs), take(ds), take(ms), take(vs)
    for n in _BIG:
        delta[n], new_m[n], new_v[n] = _adamw(weights[n], grads[n], m_in[n], v_in[n], "adamw_" + n)

    return (loss, dx0[None], *[grads[n] for n in _WEIGHTS], *[delta[n] for n in _WEIGHTS],
            *[new_m[n] for n in _WEIGHTS], *[new_v[n] for n in _WEIGHTS])
```

```python
import functools
import math

import numpy as np
import jax
import jax.numpy as jnp
from jax import lax
from jax.experimental import pallas as pl
from jax.experimental.pallas import tpu as pltpu

F32 = jnp.float32
BF16 = jnp.bfloat16
EPS = 1e-6
NEG = float(np.finfo(np.float32).min)
MESH = pl.DeviceIdType.MESH

N_DEV = 8
D_MODEL = 1024
MEM_LEN = 256
XQ = 512
PRIM = 1536
BRANCH = 2048
X_HEADS = 4
HD = 128
S5_G = 96
S5_P = 64
S5_C = 16
S5_GB = 8
S5_W = S5_GB * S5_P
MLA_H = 12
ROPE = 64
Q_LORA = 512
KV_LORA = 256
ROPE_THETA = 10000.0

ADAM_LR = 0.001
ADAM_B1 = 0.9
ADAM_B2 = 0.999
ADAM_EPS = 1e-08
ADAM_WD = 0.01
ADAM_STEP = 10

VMEM_LIMIT = 56 * 1024 * 1024


def _dot(a, b):
    return jnp.dot(a, b, preferred_element_type=F32)


def _dot_nt(a, b):
    return lax.dot_general(a, b, (((1,), (1,)), ((), ())), preferred_element_type=F32)


def _dot_tn(a, b):
    return lax.dot_general(a, b, (((0,), (0,)), ((), ())), preferred_element_type=F32)


@jax.custom_vjp
def _mm(a, b):
    return _dot(a.astype(BF16), b.astype(BF16))


def _mm_fwd(a, b):
    return _mm(a, b), (a, b)


def _mm_bwd(res, g):
    a, b = res
    gb = g.astype(BF16)
    return _dot_nt(gb, b.astype(BF16)).astype(a.dtype), _dot_tn(a.astype(BF16), gb).astype(b.dtype)


_mm.defvjp(_mm_fwd, _mm_bwd)


@jax.custom_vjp
def _mm_nt(a, b):
    return _dot_nt(a.astype(BF16), b.astype(BF16))


def _mm_nt_fwd(a, b):
    return _mm_nt(a, b), (a, b)


def _mm_nt_bwd(res, g):
    a, b = res
    gb = g.astype(BF16)
    return _dot(gb, b.astype(BF16)).astype(a.dtype), _dot_tn(gb, a.astype(BF16)).astype(b.dtype)


_mm_nt.defvjp(_mm_nt_fwd, _mm_nt_bwd)


@jax.custom_vjp
def _softmax(s):
    m = jnp.max(s, axis=-1, keepdims=True)
    e = jnp.exp(s - m)
    return e / jnp.sum(e, axis=-1, keepdims=True)


def _softmax_fwd(s):
    p = _softmax(s)
    return p, p


def _softmax_bwd(p, g):
    return (p * (g - jnp.sum(p * g, axis=-1, keepdims=True)),)


_softmax.defvjp(_softmax_fwd, _softmax_bwd)


def _rms(x, g, n):
    ms = jnp.sum(x * x, axis=-1, keepdims=True) * (1.0 / n)
    return x * lax.rsqrt(ms + EPS) * g


def _sigmoid(x):
    return 1.0 / (1.0 + jnp.exp(-x))


def _silu(x):
    return x * _sigmoid(x)


def _gelu(x):
    c = math.sqrt(2.0 / math.pi)
    return 0.5 * x * (1.0 + jnp.tanh(c * (x + 0.044715 * (x * x * x))))


@jax.custom_vjp
def _rot(x, c, s1, s2):
    return x * c + pltpu.roll(x, 96, 1) * s1 + pltpu.roll(x, 32, 1) * s2


def _rot_fwd(x, c, s1, s2):
    return _rot(x, c, s1, s2), (c, s1, s2)


def _rot_bwd(res, g):
    c, s1, s2 = res
    dx = g * c + pltpu.roll(g * s1, 32, 1) + pltpu.roll(g * s2, 96, 1)
    return dx, jnp.zeros_like(c), jnp.zeros_like(s1), jnp.zeros_like(s2)


_rot.defvjp(_rot_fwd, _rot_bwd)


def _mem_attn(xq, k, v, gq):
    outs = []
    for h in range(X_HEADS):
        sl = slice(HD * h, HD * (h + 1))
        q = _rms(xq[:, sl], gq, HD)
        p = _softmax(_mm_nt(q, k[:, sl]) * (HD ** -0.5))
        outs.append(_mm(p, v[:, sl]))
    return jnp.concatenate(outs, axis=-1)


def _merge(mix, xq, gate, k, v, gq):
    return jnp.concatenate([mix, _mem_attn(xq, k, v, gq)], axis=-1) * _silu(gate)


def _q_post(q, gqn, gqr, c, s1, s2):
    pieces = []
    for h in range(MLA_H):
        pieces.append(_rms(q[:, HD * h:HD * (h + 1)], gqn, HD))
        pieces.append(_rot(_rms(q[:, PRIM + HD * h:PRIM + HD * (h + 1)], gqr, ROPE), c, s1, s2))
    return jnp.concatenate(pieces, axis=-1)


def _kv_post(kv, krp, gkn, gkr, c, s1, s2):
    kr = _rot(_rms(krp, gkr, ROPE), c, s1, s2)
    pieces, vals = [], []
    for h in range(MLA_H):
        pieces.append(_rms(kv[:, 2 * HD * h:2 * HD * h + HD], gkn, HD))
        pieces.append(kr)
        vals.append(kv[:, 2 * HD * h + HD:2 * HD * (h + 1)])
    return jnp.concatenate(pieces, axis=-1), jnp.concatenate(vals, axis=-1)


def _rowwise(name, fn, ins, outs, nblk, sub=1):
    n_in = len(ins)

    def spec(kind, shape):
        if kind == 'r':
            return pl.BlockSpec((shape[0] // nblk, shape[1]), lambda i: (i, 0))
        zeros = (0,) * len(shape)
        return pl.BlockSpec(tuple(shape), lambda i: zeros)

    def body(*refs):
        i = pl.program_id(0)
        res = fn(*[r[...] for r in refs[:n_in]])
        for (kind, _, _), ref, val in zip(outs, refs[n_in:], res):
            if kind == 'a':
                @pl.when(i == 0)
                def _():
                    ref[...] = jnp.zeros_like(ref)
                ref[...] += val.astype(ref.dtype)
            else:
                ref[...] = val.astype(ref.dtype)

    res = pl.pallas_call(
        body, name=name, grid=(nblk,),
        in_specs=[spec(k, a.shape) for k, a in ins],
        out_specs=[spec(k, s) for k, s, _ in outs],
        out_shape=[jax.ShapeDtypeStruct(tuple(s), d) for _, s, d in outs],
        compiler_params=pltpu.CompilerParams(dimension_semantics=("arbitrary",), vmem_limit_bytes=VMEM_LIMIT),
    )(*[a for _, a in ins])
    return res


def _matmul_tn(a, g, name, out_dtype=BF16):
    L, K = a.shape
    N = g.shape[1]
    tn = next(t for t in (512, 384, 256, 128) if N % t == 0)
    tl = min(512, L)
    nl = L // tl

    def body(a_ref, g_ref, o_ref, acc):
        l = pl.program_id(1)

        @pl.when(l == 0)
        def _():
            acc[...] = jnp.zeros_like(acc)

        acc[...] += _dot_tn(a_ref[...], g_ref[...])

        @pl.when(l == nl - 1)
        def _():
            o_ref[...] = acc[...].astype(o_ref.dtype)

    return pl.pallas_call(
        body, name=name, grid=(N // tn, nl),
        in_specs=[pl.BlockSpec((tl, K), lambda n, l: (l, 0)), pl.BlockSpec((tl, tn), lambda n, l: (l, n))],
        out_specs=pl.BlockSpec((K, tn), lambda n, l: (0, n)),
        out_shape=jax.ShapeDtypeStruct((K, N), out_dtype),
        scratch_shapes=[pltpu.VMEM((K, tn), F32)],
        compiler_params=pltpu.CompilerParams(dimension_semantics=("arbitrary", "arbitrary"),
                                             vmem_limit_bytes=VMEM_LIMIT),
    )(a, g)


def _matmul_tn_slots(a, g, name):
    L, K = a.shape
    n = g.shape[1] // N_DEV
    tl = min(512, L)
    nl = L // tl

    def body(a_ref, g_ref, o_ref, acc):
        l = pl.program_id(1)

        @pl.when(l == 0)
        def _():
            acc[...] = jnp.zeros_like(acc)

        acc[...] += _dot_tn(a_ref[...], g_ref[...])

        @pl.when(l == nl - 1)
        def _():
            o_ref[...] = acc[...].astype(o_ref.dtype)

    return pl.pallas_call(
        body, name=name, grid=(N_DEV, nl),
        in_specs=[pl.BlockSpec((tl, K), lambda d, l: (l, 0)), pl.BlockSpec((tl, n), lambda d, l: (l, d))],
        out_specs=pl.BlockSpec((None, K, n), lambda d, l: (d, 0, 0)),
        out_shape=jax.ShapeDtypeStruct((N_DEV, K, n), BF16),
        scratch_shapes=[pltpu.VMEM((K, n), F32)],
        compiler_params=pltpu.CompilerParams(dimension_semantics=("arbitrary", "arbitrary"),
                                             vmem_limit_bytes=VMEM_LIMIT),
    )(a, g)


def _mm_slots(a16, w):
    return jnp.concatenate([_dot(a16, w[d]) for d in range(N_DEV)], axis=-1)


def _mm_slots_nt(g16, w):
    n = w.shape[2]
    out = _dot_nt(g16[:, 0:n], w[0])
    for d in range(1, N_DEV):
        out = out + _dot_nt(g16[:, d * n:(d + 1) * n], w[d])
    return out


def _all_gather(xs, name):
    n = len(xs)

    def body(*refs):
        x_refs, out_refs = refs[:n], refs[n:2 * n]
        send_sems, recv_sems, local_sems = refs[2 * n:]
        x, y, c = lax.axis_index("x"), lax.axis_index("y"), lax.axis_index("c")
        me, sibling = (x, y, c), (x, y, 1 - c)
        chips = [(1 - x, y), (x, 1 - y), (1 - x, 1 - y)]

        def copies(k, block, to, own=False):
            slot = 4 * block[0] + 2 * block[1] + block[2]
            return [pltpu.make_async_remote_copy(
                src_ref=x_refs[a] if own else out_refs[a].at[slot], dst_ref=out_refs[a].at[slot],
                send_sem=send_sems.at[k * n + a], recv_sem=recv_sems.at[k * n + a], device_id=to,
                device_id_type=MESH) for a in range(n)]

        mine = [pltpu.make_async_copy(x_refs[a], out_refs[a].at[4 * x + 2 * y + c], local_sems.at[a])
                for a in range(n)]
        for cp in mine:
            cp.start()
        first = copies(0, me, sibling, own=True)
        for j, chip in enumerate(chips):
            first += copies(1 + j, me, (*chip, c), own=True)
        for cp in first:
            cp.start()
        passed = []
        for j, chip in enumerate(chips):
            for cp in copies(1 + j, (*chip, c), me):
                cp.wait_recv()
            fwd = copies(4 + j, (*chip, c), sibling)
            for cp in fwd:
                cp.start()
            passed += fwd
        for cp in copies(0, sibling, me):
            cp.wait_recv()
        for j, chip in enumerate(chips):
            for cp in copies(4 + j, (*chip, 1 - c), me):
                cp.wait_recv()
        for cp in first + passed:
            cp.wait_send()
        for cp in mine:
            cp.wait()

    return pl.pallas_call(
        body, name=name,
        out_shape=[jax.ShapeDtypeStruct((N_DEV,) + a.shape, a.dtype) for a in xs],
        in_specs=[pl.BlockSpec(memory_space=pl.ANY)] * n,
        out_specs=[pl.BlockSpec(memory_space=pl.ANY)] * n,
        scratch_shapes=[pltpu.SemaphoreType.DMA((7 * n,)), pltpu.SemaphoreType.DMA((7 * n,)),
                        pltpu.SemaphoreType.DMA((n,))],
    )(*xs)


def _all_to_all(sends, name):
    n = len(sends)
    flips = [(0, 0, 1), (1, 0, 0), (0, 1, 0), (1, 1, 0), (1, 0, 1), (0, 1, 1), (1, 1, 1)]

    def body(*refs):
        send_refs, recv_refs = refs[:n], refs[n:2 * n]
        send_sems, recv_sems, local_sems = refs[2 * n:]
        x, y, c = lax.axis_index("x"), lax.axis_index("y"), lax.axis_index("c")
        me = 4 * x + 2 * y + c
        local = [pltpu.make_async_copy(send_refs[a].at[me], recv_refs[a].at[me], local_sems.at[a]) for a in range(n)]
        for cp in local:
            cp.start()
        copies = []
        for k, (fx, fy, fc) in enumerate(flips):
            px = 1 - x if fx else x
            py = 1 - y if fy else y
            pc = 1 - c if fc else c
            for a in range(n):
                cp = pltpu.make_async_remote_copy(
                    src_ref=send_refs[a].at[4 * px + 2 * py + pc], dst_ref=recv_refs[a].at[me],
                    send_sem=send_sems.at[k * n + a], recv_sem=recv_sems.at[k * n + a], device_id=(px, py, pc),
                    device_id_type=MESH)
                cp.start()
                copies.append(cp)
        for cp in copies:
            cp.wait_recv()
        for cp in copies:
            cp.wait_send()
        for cp in local:
            cp.wait()

    return pl.pallas_call(
        body, name=name,
        out_shape=[jax.ShapeDtypeStruct(a.shape, a.dtype) for a in sends],
        in_specs=[pl.BlockSpec(memory_space=pl.ANY)] * n,
        out_specs=[pl.BlockSpec(memory_space=pl.ANY)] * n,
        scratch_shapes=[pltpu.SemaphoreType.DMA((7 * n,)), pltpu.SemaphoreType.DMA((7 * n,)),
                        pltpu.SemaphoreType.DMA((n,))],
    )(*sends)


def _adamw_vals(w, g, m, v):
    m2 = ADAM_B1 * m + (1.0 - ADAM_B1) * g
    v2 = ADAM_B2 * v + (1.0 - ADAM_B2) * (g * g)
    m_hat = m2 / (1.0 - ADAM_B1 ** ADAM_STEP)
    v_hat = v2 / (1.0 - ADAM_B2 ** ADAM_STEP)
    delta = -ADAM_LR * (m_hat / (jnp.sqrt(v_hat) + ADAM_EPS) + ADAM_WD * w)
    return delta, m2, v2


def _sum_adamw(recv, w, m, v, name):
    R, C = w.shape
    br = next((t for t in (256, 128, 64, 32, 16) if R % t == 0), R)

    def body(r_ref, w_ref, m_ref, v_ref, g_ref, d_ref, m2_ref, v2_ref):
        g = r_ref[0].astype(F32)
        for d in range(1, N_DEV):
            g = g + r_ref[d].astype(F32)
        dl, m2, v2 = _adamw_vals(w_ref[...], g, m_ref[...], v_ref[...])
        g_ref[...] = g
        d_ref[...] = dl
        m2_ref[...] = m2
        v2_ref[...] = v2

    spec = pl.BlockSpec((br, C), lambda i: (i, 0))
    return pl.pallas_call(
        body, name=name, grid=(R // br,),
        in_specs=[pl.BlockSpec((N_DEV, br, C), lambda i: (0, i, 0)), spec, spec, spec], out_specs=[spec] * 4,
        out_shape=[jax.ShapeDtypeStruct((R, C), F32)] * 4,
        compiler_params=pltpu.CompilerParams(dimension_semantics=("arbitrary",)),
    )(recv, w, m, v)


def _small_update(gath, wp, mp, vp, name):
    _, R, C = gath.shape
    br = R // 3

    def body(g_ref, w_ref, m_ref, v_ref, go_ref, d_ref, m2_ref, v2_ref):
        g = g_ref[0]
        for d in range(1, N_DEV):
            g = g + g_ref[d]
        dl, m2, v2 = _adamw_vals(w_ref[...], g, m_ref[...], v_ref[...])
        go_ref[...] = g
        d_ref[...] = dl
        m2_ref[...] = m2
        v2_ref[...] = v2

    spec = pl.BlockSpec((br, C), lambda i: (i, 0))
    return pl.pallas_call(
        body, name=name, grid=(R // br,),
        in_specs=[pl.BlockSpec((N_DEV, br, C), lambda i: (0, i, 0)), spec, spec, spec],
        out_specs=[spec] * 4, out_shape=[jax.ShapeDtypeStruct((R, C), F32)] * 4,
        compiler_params=pltpu.CompilerParams(dimension_semantics=("arbitrary",)),
    )(gath, wp, mp, vp)


def _s5_param_fn(lr, li, ls, btr, bti):
    step = jnp.exp(ls)
    er = jnp.exp(lr * step)
    ang = li * step
    ar = er * jnp.cos(ang)
    ai = er * jnp.sin(ang)
    nr = ar - 1.0
    den = lr * lr + li * li
    fr = (nr * lr + ai * li) / den
    fi = (ai * lr - nr * li) / den
    return ar, ai, fr * btr - fi * bti, fr * bti + fi * btr


def _s5_params(lr, li, ls, btr, bti):
    def body(lr_ref, li_ref, ls_ref, br_ref, bi_ref, ar_ref, ai_ref, bbr_ref, bbi_ref):
        ar, ai, bbr, bbi = _s5_param_fn(lr_ref[...], li_ref[...], ls_ref[...], br_ref[...], bi_ref[...])
        ar_ref[...] = ar
        ai_ref[...] = ai
        bbr_ref[...] = bbr
        bbi_ref[...] = bbi

    sd = jax.ShapeDtypeStruct
    return pl.pallas_call(
        body, name="s5_params",
        out_shape=[sd(lr.shape, F32), sd(lr.shape, F32), sd(btr.shape, F32), sd(btr.shape, F32)],
    )(lr, li, ls, btr, bti)


def _s5_params_bwd(lr, li, ls, btr, bti, dar, dai, dbbr, dbbi):
    def body(lr_ref, li_ref, ls_ref, br_ref, bi_ref, dar_ref, dai_ref, dbbr_ref, dbbi_ref,
             dlr_ref, dli_ref, dls_ref, dbr_ref, dbi_ref):
        _, vjp = jax.vjp(_s5_param_fn, lr_ref[...], li_ref[...], ls_ref[...], br_ref[...], bi_ref[...])
        dlr, dli, dls, dbr, dbi = vjp((dar_ref[...], dai_ref[...], dbbr_ref[...], dbbi_ref[...]))
        dlr_ref[...] = dlr
        dli_ref[...] = dli
        dls_ref[...] = dls
        dbr_ref[...] = dbr
        dbi_ref[...] = dbi

    sd = jax.ShapeDtypeStruct
    return pl.pallas_call(
        body, name="s5_params_bwd",
        out_shape=[sd(lr.shape, F32), sd(lr.shape, F32), sd(ls.shape, F32), sd(btr.shape, F32), sd(btr.shape, F32)],
    )(lr, li, ls, btr, bti, dar, dai, dbbr, dbbi)


def _cpow(ar, ai, n):
    assert n & (n - 1) == 0
    while n > 1:
        ar, ai = ar * ar - ai * ai, 2.0 * ar * ai
        n //= 2
    return ar, ai


def _scan(st, cr, ci, init, nk, reverse, store, prev=None):
    W = S5_W

    def step(j, carry):
        k = nk - 1 - j if reverse else j
        rows = pl.ds(pl.multiple_of(k * 8, 8), 8)
        sr, si = carry[0], carry[1]
        nsr = cr * sr - ci * si + st[rows, 0:W]
        nsi = cr * si + ci * sr + st[rows, W:2 * W]
        if store:
            st[rows, 0:W] = nsr
            st[rows, W:2 * W] = nsi
        if prev is None:
            return nsr, nsi
        prows = pl.ds(pl.multiple_of(jnp.maximum(k - 1, 0) * 8, 8), 8)
        w = jnp.where(k > 0, 1.0, 0.0).astype(F32)
        pr = prev[prows, 0:W] * w
        pi = prev[prows, W:2 * W] * w
        return nsr, nsi, carry[2] + nsr * pr + nsi * pi, carry[3] + nsi * pr - nsr * pi

    return lax.fori_loop(0, nk, step, init, unroll=2)


def _chain(fin, fr, fi, pr, pi, reverse):
    W = S5_W
    fin[:, 0:W] = fr
    fin[:, W:2 * W] = fi
    rowid = lax.broadcasted_iota(jnp.int32, (8, W), 0)
    cr = jnp.zeros((1, W), F32)
    ci = jnp.zeros((1, W), F32)
    init_r = jnp.zeros((8, W), F32)
    init_i = jnp.zeros((8, W), F32)
    for s in (range(7, -1, -1) if reverse else range(8)):
        init_r = jnp.where(rowid == s, cr, init_r)
        init_i = jnp.where(rowid == s, ci, init_i)
        lr = fin[s:s + 1, 0:W]
        li = fin[s:s + 1, W:2 * W]
        cr, ci = lr + pr * cr - pi * ci, li + pr * ci + pi * cr
    return init_r, init_i


def _full_scan(st, fin, ar, ai, nk, reverse, prev=None):
    W = S5_W
    cr = jnp.broadcast_to(ar, (8, W))
    ci = jnp.broadcast_to(-ai if reverse else ai, (8, W))
    z = jnp.zeros((8, W), F32)
    fr, fi = _scan(st, cr, ci, (z, z), nk, reverse, store=False)
    pr, pi = _cpow(ar, -ai if reverse else ai, nk)
    init = _chain(fin, fr, fi, pr, pi, reverse)
    if prev is None:
        return _scan(st, cr, ci, init, nk, reverse, store=True)
    return _scan(st, cr, ci, init + (z, z), nk, reverse, store=True, prev=prev)


def _s5_specs(L):
    W2 = 2 * S5_W
    GC = S5_GB * S5_C
    col = pl.BlockSpec((L, GC), lambda g: (0, g))
    vec = pl.BlockSpec((1, GC), lambda g: (0, g))
    avec = pl.BlockSpec((1, S5_W), lambda g: (0, g))
    bmat = pl.BlockSpec((None, GC, W2), lambda g: (g, 0, 0))
    cmat = pl.BlockSpec((None, W2, GC), lambda g: (g, 0, 0))
    return col, vec, avec, bmat, cmat


def _interleave(dst, src, nk):
    for s in range(8):
        dst[pl.ds(s, nk, stride=8), :] = src[s * nk:(s + 1) * nk, :]


def _deinterleave(dst, src, nk):
    for s in range(8):
        dst[s * nk:(s + 1) * nk, :] = src[pl.ds(s, nk, stride=8), :]


def _s5_fwd(u, bm, cm, ar, ai, dvec):
    L = u.shape[0]
    nk = L // 8
    GC = S5_GB * S5_C
    col, vec, avec, bmat, cmat = _s5_specs(L)

    def body(u_ref, b_ref, c_ref, ar_ref, ai_ref, d_ref, y_ref, st, fin, ui, yi):
        _interleave(ui, u_ref, nk)
        for r in range(8):
            rows = slice(r * nk, (r + 1) * nk)
            st[rows, :] = _dot(ui[rows, :].astype(BF16), b_ref[...])
        _full_scan(st, fin, ar_ref[...], ai_ref[...], nk, reverse=False)
        for r in range(8):
            rows = slice(r * nk, (r + 1) * nk)
            yi[rows, :] = _dot(st[rows, :].astype(BF16), c_ref[...]) + d_ref[...] * ui[rows, :]
        _deinterleave(y_ref, yi, nk)

    return pl.pallas_call(
        body, name="s5_fwd", grid=(S5_G // S5_GB,),
        in_specs=[col, bmat, cmat, avec, avec, vec], out_specs=col,
        out_shape=jax.ShapeDtypeStruct(u.shape, F32),
        scratch_shapes=[pltpu.VMEM((L, 2 * S5_W), F32), pltpu.VMEM((8, 2 * S5_W), F32), pltpu.VMEM((L, GC), F32),
                        pltpu.VMEM((L, GC), F32)],
        compiler_params=pltpu.CompilerParams(dimension_semantics=("arbitrary",), vmem_limit_bytes=VMEM_LIMIT),
    )(u, bm, cm, ar, ai, dvec)


def _s5_bwd(u, dy, bm, bmt, cmt, ar, ai, dvec, mask, rmat):
    L = u.shape[0]
    nk = L // 8
    W = S5_W
    GC = S5_GB * S5_C
    col, vec, avec, bmat, cmat = _s5_specs(L)
    hi = lax.Precision.HIGHEST

    def body(u_ref, dy_ref, b_ref, bt_ref, ct_ref, ar_ref, ai_ref, d_ref, mask_ref, r_ref,
             du_ref, db_ref, dc_ref, dd_ref, dar_ref, dai_ref, sa, sb, fin, ui, dyi, dui):
        ar = ar_ref[...]
        ai = ai_ref[...]
        _interleave(ui, u_ref, nk)
        _interleave(dyi, dy_ref, nk)
        for r in range(8):
            rows = slice(r * nk, (r + 1) * nk)
            sa[rows, :] = _dot(ui[rows, :].astype(BF16), b_ref[...])
            sb[rows, :] = _dot(dyi[rows, :].astype(BF16), ct_ref[...])
        _full_scan(sa, fin, ar, ai, nk, reverse=False)
        gr, gi, accr, acci = _full_scan(sb, fin, ar, ai, nk, reverse=True, prev=sa)
        rowid = lax.broadcasted_iota(jnp.int32, (8, W), 0)
        last = pl.ds((nk - 1) * 8, 8)
        pr = jnp.where(rowid == 0, 0.0, pltpu.roll(sa[last, 0:W], 1, 0))
        pi = jnp.where(rowid == 0, 0.0, pltpu.roll(sa[last, W:2 * W], 1, 0))
        accr = accr + gr * pr + gi * pi
        acci = acci + gi * pr - gr * pi
        dar_ref[...] = jnp.sum(accr, axis=0, keepdims=True)
        dai_ref[...] = jnp.sum(acci, axis=0, keepdims=True)
        dbf = jnp.zeros((GC, 2 * W), F32)
        dcf = jnp.zeros((GC, 2 * W), F32)
        dd = jnp.zeros((1, GC), F32)
        for r in range(8):
            rows = slice(r * nk, (r + 1) * nk)
            ub = ui[rows, :]
            dyb = dyi[rows, :]
            gb = sb[rows, :].astype(BF16)
            dui[rows, :] = _dot(gb, bt_ref[...]) + d_ref[...] * dyb
            dbf = dbf + _dot_tn(ub.astype(BF16), gb)
            dcf = dcf + _dot_tn(dyb.astype(BF16), sa[rows, :].astype(BF16))
            dd = dd + jnp.sum(dyb * ub, axis=0, keepdims=True)
        db_ref[...] = jnp.dot(dbf * mask_ref[...], r_ref[...], precision=hi, preferred_element_type=F32)
        dc_ref[...] = jnp.dot(dcf * mask_ref[...], r_ref[...], precision=hi, preferred_element_type=F32)
        dd_ref[...] = dd
        _deinterleave(du_ref, dui, nk)

    cmp_spec = pl.BlockSpec((GC, 2 * S5_P), lambda g: (g, 0))
    whole = lambda shape: pl.BlockSpec(shape, lambda g: (0, 0))
    sd = jax.ShapeDtypeStruct
    return pl.pallas_call(
        body, name="s5_bwd", grid=(S5_G // S5_GB,),
        in_specs=[col, col, bmat, cmat, bmat, avec, avec, vec, whole(mask.shape), whole(rmat.shape)],
        out_specs=[col, cmp_spec, cmp_spec, vec, avec, avec],
        out_shape=[sd(u.shape, F32), sd((S5_G * S5_C, 2 * S5_P), F32), sd((S5_G * S5_C, 2 * S5_P), F32),
                   sd((1, PRIM), F32), sd((1, S5_G * S5_P), F32), sd((1, S5_G * S5_P), F32)],
        scratch_shapes=[pltpu.VMEM((L, 2 * W), F32), pltpu.VMEM((L, 2 * W), F32), pltpu.VMEM((8, 2 * W), F32),
                        pltpu.VMEM((L, GC), F32), pltpu.VMEM((L, GC), F32), pltpu.VMEM((L, GC), F32)],
        compiler_params=pltpu.CompilerParams(dimension_semantics=("arbitrary",), vmem_limit_bytes=VMEM_LIMIT),
    )(u, dy, bm, bmt, cmt, ar, ai, dvec, mask, rmat)


def _s5_mats(bbr, bbi, cre, cim):
    nb = S5_G // S5_GB
    eye = jnp.eye(S5_GB, dtype=F32)
    bb = jnp.stack([bbr, bbi], axis=2).reshape(nb, S5_GB, S5_C, 2, S5_P)
    bm = jnp.einsum('ngcrp,gh->ngcrhp', bb, eye).reshape(nb, S5_GB * S5_C, 2 * S5_W)
    cc = jnp.stack([cre, -cim], axis=2).reshape(nb, S5_GB, S5_C, 2, S5_P)
    cmt = jnp.einsum('ngcrp,gh->ngcrhp', cc, eye).reshape(nb, S5_GB * S5_C, 2 * S5_W)
    return (bm.astype(BF16), jnp.swapaxes(bm, 1, 2).astype(BF16),
            jnp.swapaxes(cmt, 1, 2).astype(BF16), cmt.astype(BF16))


def _s5_compact_consts():
    g_row = np.arange(S5_GB * S5_C) // S5_C
    col = np.arange(2 * S5_W)
    g_col = (col % S5_W) // S5_P
    mask = (g_row[:, None] == g_col[None, :]).astype(np.float32)
    tgt = (col // S5_W) * S5_P + col % S5_P
    rmat = (tgt[:, None] == np.arange(2 * S5_P)[None, :]).astype(np.float32)
    return jnp.asarray(mask), jnp.asarray(rmat)


def _attn_scores(q_ref, k_ref, qb, bq, scale):
    ext = (qb + 1) * bq
    s = _dot_nt(q_ref[qb * bq:ext, :], k_ref[0:ext, :]) * scale
    qpos = lax.broadcasted_iota(jnp.int32, (bq, bq), 0)
    kpos = lax.broadcasted_iota(jnp.int32, (bq, bq), 1)
    diag = jnp.where(kpos <= qpos, s[:, ext - bq:], NEG)
    return diag if qb == 0 else jnp.concatenate([s[:, :ext - bq], diag], axis=-1)


def _attn_fwd(qp, kp, v, scale):
    L = qp.shape[0]
    bq = min(256, L)

    def body(q_ref, k_ref, v_ref, o_ref, lse_ref):
        for qb in range(L // bq):
            rows = slice(qb * bq, (qb + 1) * bq)
            s = _attn_scores(q_ref, k_ref, qb, bq, scale)
            m = jnp.max(s, axis=-1, keepdims=True)
            e = jnp.exp(s - m)
            l = jnp.sum(e, axis=-1, keepdims=True)
            o_ref[rows, :] = _dot(e.astype(BF16), v_ref[0:(qb + 1) * bq, :]) / l
            lse_ref[rows, :] = jnp.broadcast_to(m + jnp.log(l), (bq, HD))

    blk = pl.BlockSpec((L, HD), lambda h: (0, h))
    wide = pl.BlockSpec((L, 2 * HD), lambda h: (0, h))
    return pl.pallas_call(
        body, name="mla_attn_fwd", grid=(MLA_H,),
        in_specs=[wide, wide, blk], out_specs=[blk, blk],
        out_shape=[jax.ShapeDtypeStruct((L, MLA_H * HD), F32)] * 2,
        compiler_params=pltpu.CompilerParams(dimension_semantics=("arbitrary",), vmem_limit_bytes=VMEM_LIMIT),
    )(qp, kp, v)


def _attn_bwd(qp, kp, v, o, lse, do, scale):
    L = qp.shape[0]
    bq = min(256, L)
    nq = L // bq

    def body(q_ref, k_ref, v_ref, o_ref, lse_ref, do_ref, dq_ref, dk_ref, dv_ref):
        dk_ref[...] = jnp.zeros_like(dk_ref)
        dv_ref[...] = jnp.zeros_like(dv_ref)
        for qb in range(nq):
            rows = slice(qb * bq, (qb + 1) * bq)
            ext = (qb + 1) * bq
            do = do_ref[rows, :]
            dob = do.astype(BF16)
            p = jnp.exp(_attn_scores(q_ref, k_ref, qb, bq, scale) - lse_ref[rows, 0:1])
            dp = _dot_nt(dob, v_ref[0:ext, :])
            dsum = jnp.sum(do * o_ref[rows, :], axis=-1, keepdims=True)
            ds = (p * (dp - dsum) * scale).astype(BF16)
            dq_ref[rows, :] = _dot(ds, k_ref[0:ext, :])
            dk_ref[0:ext, :] += _dot_tn(ds, q_ref[rows, :])
            dv_ref[0:ext, :] += _dot_tn(p.astype(BF16), dob)

    sd = jax.ShapeDtypeStruct
    blk = pl.BlockSpec((L, HD), lambda h: (0, h))
    wide = pl.BlockSpec((L, 2 * HD), lambda h: (0, h))
    return pl.pallas_call(
        body, name="mla_attn_bwd", grid=(MLA_H,),
        in_specs=[wide, wide, blk, blk, blk, blk], out_specs=[wide, wide, blk],
        out_shape=[sd((L, MLA_H * 2 * HD), F32), sd((L, MLA_H * 2 * HD), F32), sd((L, MLA_H * HD), F32)],
        compiler_params=pltpu.CompilerParams(dimension_semantics=("arbitrary",), vmem_limit_bytes=VMEM_LIMIT),
    )(qp, kp, v, o, lse, do)


def _kv_fn(mem, gm, w, gk):
    kv = _mm(_rms(mem, gm, D_MODEL), w)
    k = jnp.concatenate([_rms(kv[:, HD * h:HD * (h + 1)], gk, HD) for h in range(X_HEADS)], axis=-1)
    return k, kv[:, XQ:]


def _kv_prep(mem, gm, w, gk, name):
    def fn(mem, gm, w, gk):
        return _kv_fn(mem, gm, w, gk)
    M = mem.shape[0]
    return _rowwise(name, fn, [('c', mem), ('c', gm), ('c', w), ('c', gk)],
                    [('c', (M, XQ), F32), ('c', (M, XQ), F32)], 1)


def _kv_prep_bwd(mem, gm, w, gk, dk, dv, name):
    def fn(mem, gm, w, gk, dk, dv):
        _, vjp = jax.vjp(lambda a, b, c: _kv_fn(mem, a, b, c), gm, w, gk)
        return vjp((dk, dv))
    return _rowwise(name, fn, [('c', mem), ('c', gm), ('c', w), ('c', gk), ('c', dk), ('c', dv)],
                    [('c', gm.shape, F32), ('c', w.shape, BF16), ('c', gk.shape, F32)], 1)


def _forward_merge(x, mix, mix_kind, xq, gate, k, v, gq, wout, name, nblk, sub):
    def fn(x, mix, xq, gate, k, v, gq, wout):
        o = _merge(mix, xq, gate, k, v, gq)
        return (x + _dot(o.astype(BF16), wout),)
    L = x.shape[0]
    return _rowwise(name, fn, [('r', x), (mix_kind, mix), ('r', xq), ('r', gate), ('c', k), ('c', v), ('c', gq),
                               ('c', wout)], [('r', (L, D_MODEL), F32)], nblk, sub)[0]


def _backward_merge(dx, mix, mix_kind, xq, gate, k, v, gq, wout, name, nblk, sub):
    def fn(dx, mix, xq, gate, k, v, gq, wout):
        g16 = dx.astype(BF16)
        do = _dot_nt(g16, wout)
        o, vjp = jax.vjp(_merge, mix, xq, gate, k, v, gq)
        dmix, dxq, dgate, dk, dv, dgq = vjp(do)
        return dmix, dxq, dgate, o, g16, dk, dv, dgq
    L = dx.shape[0]
    return _rowwise(
        name, fn,
        [('r', dx), (mix_kind, mix), ('r', xq), ('r', gate), ('c', k), ('c', v), ('c', gq), ('c', wout)],
        [('r', (L, PRIM), F32), ('r', (L, XQ), F32), ('r', (L, BRANCH), F32), ('r', (L, BRANCH), BF16),
         ('r', (L, D_MODEL), BF16), ('a', k.shape, F32), ('a', v.shape, F32), ('a', gq.shape, F32)], nblk, sub)


_MLA_IN = 3392
_MLA_IN_PAD = 3456


def _from_slots(g):
    _, k, n = g.shape
    return jnp.transpose(g, (1, 0, 2)).reshape(k, N_DEV * n)


def _to_slots(w):
    k = w.shape[0]
    return jnp.transpose(w.reshape(k, N_DEV, -1), (1, 0, 2))


def _uq_to_kernel(g):
    uq = _from_slots(g).reshape(Q_LORA, MLA_H, HD + ROPE)
    return jnp.concatenate([uq[:, :, :HD].reshape(Q_LORA, PRIM),
                            jnp.pad(uq[:, :, HD:], ((0, 0), (0, 0), (0, HD - ROPE))).reshape(Q_LORA, PRIM)], axis=1)


def _uq_from_kernel(d_w_q):
    uq = jnp.concatenate([d_w_q[:, :PRIM].reshape(Q_LORA, MLA_H, HD),
                          d_w_q[:, PRIM:].reshape(Q_LORA, MLA_H, HD)[:, :, :ROPE]], axis=2)
    return _to_slots(uq.reshape(Q_LORA, MLA_H * (HD + ROPE)))


def _mla_in_perm(w):
    return jnp.concatenate([w[:, :768], w[:, 832:], w[:, 768:832], jnp.zeros((w.shape[0], 64), w.dtype)], axis=1)


def _mla_in_unperm(w):
    return jnp.concatenate([w[:, :768], w[:, 3328:3392], w[:, 768:3328]], axis=1)


_SMALL = (("ln_gain", 2048), ("mem_norm", 2048), ("xq_norm", 256), ("xk_norm", 256), ("s5_lambda_re", 6144),
          ("s5_lambda_im", 6144), ("s5_log_step", 96), ("s5_b_re", 98304), ("s5_b_im", 98304), ("s5_c_re", 98304),
          ("s5_c_im", 98304), ("s5_d", 1536), ("mla_q_lora_norm", 512), ("mla_kv_lora_norm", 256),
          ("mla_q_nope_norm", 128), ("mla_k_nope_norm", 128), ("mla_q_rope_norm", 64), ("mla_k_rope_norm", 64))
_SMALL_ROWS = 408
_SMALL_OFF = {name: sum(n for _, n in _SMALL[:i]) for i, (name, _) in enumerate(_SMALL)}


def _pack_small(d):
    flat = jnp.concatenate([d[n].reshape(-1).astype(F32) for n, _ in _SMALL])
    return jnp.pad(flat, (0, _SMALL_ROWS * 1024 - flat.shape[0])).reshape(_SMALL_ROWS, 1024)


def _unpack_small(p, name, shape):
    off = _SMALL_OFF[name]
    return p.reshape(-1)[off:off + int(np.prod(shape))].reshape(shape)


_WEIGHTS = ('ln_gain', 'w_out', 'mem_norm', 'w_mem_kv', 'xq_norm', 'xk_norm', 's5_w_in', 's5_lambda_re',
            's5_lambda_im', 's5_log_step', 's5_b_re', 's5_b_im', 's5_c_re', 's5_c_im', 's5_d', 's5_w_glu', 'mla_w_in',
            'mla_q_lora_norm', 'mla_kv_lora_norm', 'mla_w_uq', 'mla_w_ukv', 'mla_q_nope_norm', 'mla_k_nope_norm',
            'mla_q_rope_norm', 'mla_k_rope_norm')
_BIG = ('w_out', 'w_mem_kv', 's5_w_in', 's5_w_glu', 'mla_w_in', 'mla_w_uq', 'mla_w_ukv')


def _pad128(g):
    return jnp.pad(g.reshape(1, -1), ((0, 0), (0, HD - g.shape[-1])))


def kernel(x, mem, positions, ln_gain, w_out, mem_norm, w_mem_kv, xq_norm, xk_norm, s5_w_in, s5_lambda_re, s5_lambda_im, s5_log_step, s5_b_re, s5_b_im, s5_c_re, s5_c_im, s5_d, s5_w_glu, mla_w_in, mla_q_lora_norm, mla_kv_lora_norm, mla_w_uq, mla_w_ukv, mla_q_nope_norm, mla_k_nope_norm, mla_q_rope_norm, mla_k_rope_norm, loss_target, m_ln_gain, m_w_out, m_mem_norm, m_w_mem_kv, m_xq_norm, m_xk_norm, m_s5_w_in, m_s5_lambda_re, m_s5_lambda_im, m_s5_log_step, m_s5_b_re, m_s5_b_im, m_s5_c_re, m_s5_c_im, m_s5_d, m_s5_w_glu, m_mla_w_in, m_mla_q_lora_norm, m_mla_kv_lora_norm, m_mla_w_uq, m_mla_w_ukv, m_mla_q_nope_norm, m_mla_k_nope_norm, m_mla_q_rope_norm, m_mla_k_rope_norm, v_ln_gain, v_w_out, v_mem_norm, v_w_mem_kv, v_xq_norm, v_xk_norm, v_s5_w_in, v_s5_lambda_re, v_s5_lambda_im, v_s5_log_step, v_s5_b_re, v_s5_b_im, v_s5_c_re, v_s5_c_im, v_s5_d, v_s5_w_glu, v_mla_w_in, v_mla_q_lora_norm, v_mla_kv_lora_norm, v_mla_w_uq, v_mla_w_ukv, v_mla_q_nope_norm, v_mla_k_nope_norm, v_mla_q_rope_norm, v_mla_k_rope_norm):
    weights = dict(ln_gain=ln_gain, w_out=w_out, mem_norm=mem_norm, w_mem_kv=w_mem_kv, xq_norm=xq_norm,
                   xk_norm=xk_norm, s5_w_in=s5_w_in, s5_lambda_re=s5_lambda_re, s5_lambda_im=s5_lambda_im,
                   s5_log_step=s5_log_step, s5_b_re=s5_b_re, s5_b_im=s5_b_im, s5_c_re=s5_c_re, s5_c_im=s5_c_im,
                   s5_d=s5_d, s5_w_glu=s5_w_glu, mla_w_in=mla_w_in, mla_q_lora_norm=mla_q_lora_norm,
                   mla_kv_lora_norm=mla_kv_lora_norm, mla_w_uq=mla_w_uq, mla_w_ukv=mla_w_ukv,
                   mla_q_nope_norm=mla_q_nope_norm, mla_k_nope_norm=mla_k_nope_norm,
                   mla_q_rope_norm=mla_q_rope_norm, mla_k_rope_norm=mla_k_rope_norm)
    m_in = dict(zip(_WEIGHTS, (m_ln_gain, m_w_out, m_mem_norm, m_w_mem_kv, m_xq_norm, m_xk_norm, m_s5_w_in,
                               m_s5_lambda_re, m_s5_lambda_im, m_s5_log_step, m_s5_b_re, m_s5_b_im, m_s5_c_re,
                               m_s5_c_im, m_s5_d, m_s5_w_glu, m_mla_w_in, m_mla_q_lora_norm, m_mla_kv_lora_norm,
                               m_mla_w_uq, m_mla_w_ukv, m_mla_q_nope_norm, m_mla_k_nope_norm, m_mla_q_rope_norm,
                               m_mla_k_rope_norm)))
    v_in = dict(zip(_WEIGHTS, (v_ln_gain, v_w_out, v_mem_norm, v_w_mem_kv, v_xq_norm, v_xk_norm, v_s5_w_in,
                               v_s5_lambda_re, v_s5_lambda_im, v_s5_log_step, v_s5_b_re, v_s5_b_im, v_s5_c_re,
                               v_s5_c_im, v_s5_d, v_s5_w_glu, v_mla_w_in, v_mla_q_lora_norm, v_mla_kv_lora_norm,
                               v_mla_w_uq, v_mla_w_ukv, v_mla_q_nope_norm, v_mla_k_nope_norm, v_mla_q_rope_norm,
                               v_mla_k_rope_norm)))

    x0 = x[0]
    mem0 = mem[0]
    target = loss_target[0]
    L = x0.shape[0]
    nblk, sub = 8, 1
    me = 4 * lax.axis_index("x") + 2 * lax.axis_index("y") + lax.axis_index("c")

    lora = jnp.pad(jnp.concatenate([mla_q_lora_norm, mla_kv_lora_norm], axis=1), ((0, 7), (0, HD - 96)))
    shards = [w_out[0], w_out[1], w_mem_kv[0], w_mem_kv[1], s5_w_in[0], s5_w_glu[0], mla_w_in[0], mla_w_uq[0],
              mla_w_ukv[0]]
    (G_out0, G_out1, G_mkv0, G_mkv1, W_in_s5, W_glu, G_in_mla, G_uq, W_kv, G_lora) = _all_gather(
        [s.astype(BF16) for s in shards] + [lora], "ag_weights")
    W_out = (G_out0.reshape(BRANCH, D_MODEL), G_out1.reshape(BRANCH, D_MODEL))
    W_mkv = (G_mkv0.reshape(D_MODEL, 2 * XQ), G_mkv1.reshape(D_MODEL, 2 * XQ))
    W_in_mla = _mla_in_perm(_from_slots(G_in_mla))
    W_q = _uq_to_kernel(G_uq)
    g_qlora = G_lora[:, 0, :64].reshape(1, Q_LORA)
    g_kvlora = G_lora[:, 0, 64:96].reshape(1, KV_LORA)

    ln0, ln1 = ln_gain[0:1], ln_gain[1:2]
    gq0, gq1 = xq_norm[0:1], xq_norm[1:2]
    gk0, gk1 = xk_norm[0:1], xk_norm[1:2]
    gm0, gm1 = mem_norm[0:1], mem_norm[1:2]
    gqn, gkn = mla_q_nope_norm, mla_k_nope_norm
    gqr, gkr = _pad128(mla_q_rope_norm), _pad128(mla_k_rope_norm)

    lr3 = s5_lambda_re.reshape(S5_G, 1, S5_P)
    li3 = s5_lambda_im.reshape(S5_G, 1, S5_P)
    ls3 = s5_log_step.reshape(S5_G, 1, 1)
    btr = jnp.swapaxes(s5_b_re[0], 1, 2)
    bti = jnp.swapaxes(s5_b_im[0], 1, 2)
    a_r, a_i, bbr, bbi = _s5_params(lr3, li3, ls3, btr, bti)
    bm, bmt, cm, cmt = _s5_mats(bbr, bbi, s5_c_re[0], s5_c_im[0])
    a_r2 = a_r.reshape(1, S5_G * S5_P)
    a_i2 = a_i.reshape(1, S5_G * S5_P)
    cmask, rmat = _s5_compact_consts()

    half = ROPE // 2
    inv_freq = ROPE_THETA ** (-jnp.arange(half, dtype=F32) / half)
    invf = jnp.concatenate([inv_freq, inv_freq, jnp.zeros((HD - ROPE,), F32)]).reshape(1, HD)

    def rot_tables(pos, invf):
        ang = pos.astype(F32) * invf
        lane = lax.broadcasted_iota(jnp.int32, ang.shape, 1)
        c = jnp.where(lane < ROPE, jnp.cos(ang), 0.0)
        s = jnp.sin(ang)
        return c, jnp.where(lane < half, -s, 0.0), jnp.where((lane >= half) & (lane < ROPE), s, 0.0)

    tc, ts1, ts2 = _rowwise("rot_tables", rot_tables, [('r', positions.reshape(L, 1)), ('c', invf)],
                            [('r', (L, HD), F32)] * 3, nblk, sub)

    def in_s5(x, g, w):
        proj = _mm_slots(_rms(x, g, D_MODEL).astype(BF16), w)
        return proj[:, :PRIM], proj[:, PRIM:PRIM + XQ], proj[:, PRIM + XQ:]

    u_s5, xq_a, gate_a = _rowwise("s5_in", in_s5, [('r', x0), ('c', ln0), ('c', W_in_s5)],
                                  [('r', (L, PRIM), F32), ('r', (L, XQ), F32), ('r', (L, BRANCH), F32)], nblk, sub)
    y_s5 = _s5_fwd(u_s5, bm, cm, a_r2, a_i2, s5_d)

    def glu(y, w):
        z = _mm_slots(_gelu(y).astype(BF16), w)
        return (z[:, :PRIM] * _sigmoid(z[:, PRIM:]),)

    y2 = _rowwise("s5_glu", glu, [('r', y_s5), ('c', W_glu)], [('r', (L, PRIM), F32)], nblk, sub)[0]
    k_a, v_a = _kv_prep(mem0, gm0, W_mkv[0], gk0, "kv_prep0")
    x1 = _forward_merge(x0, y2, 'r', xq_a, gate_a, k_a, v_a, gq0, W_out[0], "merge0", nblk, sub)

    def in_mla(x, g, w):
        proj = _dot(_rms(x, g, D_MODEL).astype(BF16), w)
        return proj[:, :512], proj[:, 512:768], proj[:, 768:1280], proj[:, 1280:3328], proj[:, 3328:]

    c_q, c_kv, xq_b, gate_b, krp = _rowwise(
        "mla_in", in_mla, [('r', x1), ('c', ln1), ('c', W_in_mla)],
        [('r', (L, Q_LORA), F32), ('r', (L, KV_LORA), F32), ('r', (L, XQ), F32), ('r', (L, BRANCH), F32),
         ('r', (L, HD), F32)], nblk, sub)

    def qkv(c_q, c_kv, krp, tc, ts1, ts2, gql, gkvl, wq, wkv, gqn, gkn, gqr, gkr):
        q = _dot(_rms(c_q, gql, Q_LORA).astype(BF16), wq)
        kv = _mm_slots(_rms(c_kv, gkvl, KV_LORA).astype(BF16), wkv)
        kp, v = _kv_post(kv, krp, gkn, gkr, tc, ts1, ts2)
        return _q_post(q, gqn, gqr, tc, ts1, ts2), kp, v

    qkv_consts = [('c', g_qlora), ('c', g_kvlora), ('c', W_q), ('c', W_kv), ('c', gqn), ('c', gkn), ('c', gqr),
                  ('c', gkr)]
    q_pad, k_pad, v_h = _rowwise(
        "mla_qkv", qkv, [('r', c_q), ('r', c_kv), ('r', krp), ('r', tc), ('r', ts1), ('r', ts2)] + qkv_consts,
        [('r', (L, 2 * PRIM), BF16), ('r', (L, 2 * PRIM), BF16), ('r', (L, PRIM), BF16)], nblk, sub)
    scale = (HD + ROPE) ** -0.5
    attn, lse = _attn_fwd(q_pad, k_pad, v_h, scale)
    k_b, v_b = _kv_prep(mem0, gm1, W_mkv[1], gk1, "kv_prep1")
    x2 = _forward_merge(x1, attn, 'r', xq_b, gate_b, k_b, v_b, gq1, W_out[1], "merge1", nblk, sub)

    def loss_fn(y, t):
        err = y - t
        part = 0.5 * jnp.sum(jnp.sum(err * err, axis=-1, keepdims=True) * (1.0 / D_MODEL), axis=0, keepdims=True)
        return err * (1.0 / D_MODEL), jnp.broadcast_to(part, (1, HD))

    dx2, loss_part = _rowwise("loss", loss_fn, [('r', x2), ('r', target)],
                              [('r', (L, D_MODEL), F32), ('a', (1, HD), F32)], nblk, sub)
    loss = lax.psum(loss_part[0, 0], ("x", "y", "c"))

    dattn, dxq_b, dgate_b, o_b, g_b, dk_b, dv_b, dgq1 = _backward_merge(
        dx2, attn, 'r', xq_b, gate_b, k_b, v_b, gq1, W_out[1], "merge1_bwd", nblk, sub)
    dgm1, dW_mkv1, dgk1 = _kv_prep_bwd(mem0, gm1, W_mkv[1], gk1, dk_b, dv_b, "kv_prep1_bwd")
    dW_out1 = _matmul_tn(o_b, g_b, "dw_out1")
    dq_pad, dk_pad, dv_h = _attn_bwd(q_pad, k_pad, v_h, attn, lse, dattn, scale)

    def qkv_bwd(c_q, c_kv, krp, tc, ts1, ts2, dqp, dkp, dv, gql, gkvl, wq, wkv, gqn, gkn, gqr, gkr):
        cqn, vjp_qn = jax.vjp(lambda a, b: _rms(a, b, Q_LORA), c_q, gql)
        ckvn, vjp_kvn = jax.vjp(lambda a, b: _rms(a, b, KV_LORA), c_kv, gkvl)
        cqn16 = cqn.astype(BF16)
        ckvn16 = ckvn.astype(BF16)
        q = _dot(cqn16, wq)
        kv = _mm_slots(ckvn16, wkv)
        _, vjp_q = jax.vjp(lambda a, b, c: _q_post(a, b, c, tc, ts1, ts2), q, gqn, gqr)
        dq, dgqn, dgqr = vjp_q(dqp)
        _, vjp_kv = jax.vjp(lambda a, b, c, d: _kv_post(a, b, c, d, tc, ts1, ts2), kv, krp, gkn, gkr)
        dkv, dkrp, dgkn, dgkr = vjp_kv((dkp, dv))
        dq16 = dq.astype(BF16)
        dkv16 = dkv.astype(BF16)
        dc_q, dgql = vjp_qn(_dot_nt(dq16, wq))
        dc_kv, dgkvl = vjp_kvn(_mm_slots_nt(dkv16, wkv))
        return dc_q, dc_kv, dkrp, cqn16, dq16, ckvn16, dkv16, dgql, dgkvl, dgqn, dgkn, dgqr, dgkr

    (dc_q, dc_kv, dkrp, cqn16, dq16, ckvn16, dkv16, dgql, dgkvl, dgqn, dgkn, dgqr, dgkr) = _rowwise(
        "mla_qkv_bwd", qkv_bwd,
        [('r', c_q), ('r', c_kv), ('r', krp), ('r', tc), ('r', ts1), ('r', ts2), ('r', dq_pad), ('r', dk_pad),
         ('r', dv_h)] + qkv_consts,
        [('r', (L, Q_LORA), F32), ('r', (L, KV_LORA), F32), ('r', (L, HD), F32), ('r', (L, Q_LORA), BF16),
         ('r', (L, 2 * PRIM), BF16), ('r', (L, KV_LORA), BF16), ('r', (L, 2 * PRIM), BF16),
         ('a', (1, Q_LORA), F32), ('a', (1, KV_LORA), F32), ('a', (1, HD), F32), ('a', (1, HD), F32),
         ('a', (1, HD), F32), ('a', (1, HD), F32)], nblk, sub)
    dW_q = _matmul_tn(cqn16, dq16, "dw_uq")
    dW_kv = _matmul_tn_slots(ckvn16, dkv16, "dw_ukv")

    def in_bwd(x, dres, g, w, *dparts):
        dproj = jnp.concatenate(dparts, axis=-1).astype(BF16)
        xn, vjp = jax.vjp(lambda a, b: _rms(a, b, D_MODEL), x, g)
        dx, dg = vjp(_mm_slots_nt(dproj, w) if w.ndim == 3 else _dot_nt(dproj, w))
        return dx + dres, xn, dproj, dg

    dx1, xn1, dproj1, dln1 = _rowwise(
        "mla_in_bwd", in_bwd,
        [('r', x1), ('r', dx2), ('c', ln1), ('c', W_in_mla), ('r', dc_q), ('r', dc_kv), ('r', dxq_b), ('r', dgate_b),
         ('r', dkrp)],
        [('r', (L, D_MODEL), F32), ('r', (L, D_MODEL), BF16), ('r', (L, _MLA_IN_PAD), BF16), ('a', (1, D_MODEL), F32)],
        nblk, sub)
    dW_in_mla = _matmul_tn(xn1, dproj1, "dw_mla_in")

    dy2, dxq_a, dgate_a, o_a, g_a, dk_a, dv_a, dgq0 = _backward_merge(
        dx1, y2, 'r', xq_a, gate_a, k_a, v_a, gq0, W_out[0], "merge0_bwd", nblk, sub)
    dgm0, dW_mkv0, dgk0 = _kv_prep_bwd(mem0, gm0, W_mkv[0], gk0, dk_a, dv_a, "kv_prep0_bwd")
    dW_out0 = _matmul_tn(o_a, g_a, "dw_out0")

    def glu_bwd(y, dy2, w):
        h, vjp_h = jax.vjp(_gelu, y)
        h16 = h.astype(BF16)
        z = _mm_slots(h16, w)
        _, vjp_z = jax.vjp(lambda z: z[:, :PRIM] * _sigmoid(z[:, PRIM:]), z)
        dz16 = vjp_z(dy2)[0].astype(BF16)
        return vjp_h(_mm_slots_nt(dz16, w))[0], h16, dz16

    dy_s5, h16, dz16 = _rowwise("s5_glu_bwd", glu_bwd, [('r', y_s5), ('r', dy2), ('c', W_glu)],
                                [('r', (L, PRIM), F32), ('r', (L, PRIM), BF16), ('r', (L, 2 * PRIM), BF16)],
                                nblk, sub)
    dW_glu = _matmul_tn_slots(h16, dz16, "dw_glu")
    du_s5, dbc, dcc, dd, dar, dai = _s5_bwd(u_s5, dy_s5, bm, bmt, cmt, a_r2, a_i2, s5_d, cmask, rmat)
    dx0, xn0, dproj0, dln0 = _rowwise(
        "s5_in_bwd", in_bwd,
        [('r', x0), ('r', dx1), ('c', ln0), ('c', W_in_s5), ('r', du_s5), ('r', dxq_a),
         ('r', dgate_a)],
        [('r', (L, D_MODEL), F32), ('r', (L, D_MODEL), BF16), ('r', (L, 2 * BRANCH), BF16), ('a', (1, D_MODEL), F32)],
        nblk, sub)
    dW_in_s5 = _matmul_tn_slots(xn0, dproj0, "dw_s5_in")

    dbc4 = dbc.reshape(S5_G, S5_C, 2, S5_P)
    dcc4 = dcc.reshape(S5_G, S5_C, 2, S5_P)
    dlr, dli, dls, dbtr, dbti = _s5_params_bwd(
        lr3, li3, ls3, btr, bti, dar.reshape(S5_G, 1, S5_P), dai.reshape(S5_G, 1, S5_P), dbc4[:, :, 0], dbc4[:, :, 1])

    sends = [dW_out0.reshape(N_DEV, 256, D_MODEL), dW_out1.reshape(N_DEV, 256, D_MODEL),
             dW_mkv0.reshape(N_DEV, 128, 2 * XQ), dW_mkv1.reshape(N_DEV, 128, 2 * XQ), dW_in_s5, dW_glu,
             _to_slots(_mla_in_unperm(dW_in_mla)), _uq_from_kernel(dW_q), dW_kv]
    recvs = _all_to_all(sends, "rs_grads")
    owners = [("w_out", 0), ("w_out", 1), ("w_mem_kv", 0), ("w_mem_kv", 1), ("s5_w_in", 0), ("s5_w_glu", 0),
              ("mla_w_in", 0), ("mla_w_uq", 0), ("mla_w_ukv", 0)]
    upd = [_sum_adamw(r, weights[n][i], m_in[n][i], v_in[n][i], "update_%s%d" % (n, i))
           for r, (n, i) in zip(recvs, owners)]
    grads, delta, new_m, new_v = {}, {}, {}, {}
    for n in _BIG:
        parts = [u for u, (o, _) in zip(upd, owners) if o == n]
        grads[n], delta[n], new_m[n], new_v[n] = (jnp.stack([p[j] for p in parts]) for j in range(4))

    small_part = {
        "ln_gain": jnp.concatenate([dln0, dln1]), "mem_norm": jnp.concatenate([dgm0, dgm1]),
        "xq_norm": jnp.concatenate([dgq0, dgq1]), "xk_norm": jnp.concatenate([dgk0, dgk1]),
        "s5_lambda_re": dlr, "s5_lambda_im": dli, "s5_log_step": dls,
        "s5_b_re": jnp.swapaxes(dbtr, 1, 2), "s5_b_im": jnp.swapaxes(dbti, 1, 2),
        "s5_c_re": dcc4[:, :, 0], "s5_c_im": -dcc4[:, :, 1], "s5_d": dd,
        "mla_q_lora_norm": dgql, "mla_kv_lora_norm": dgkvl, "mla_q_nope_norm": dgqn, "mla_k_nope_norm": dgkn,
        "mla_q_rope_norm": dgqr[:, :ROPE], "mla_k_rope_norm": dgkr[:, :ROPE],
    }
    small_gath = _all_gather([_pack_small(small_part)], "ag_small_grads")[0]

    def whole(name, a):
        if name == "mla_q_lora_norm":
            return lax.dynamic_update_slice(jnp.zeros((Q_LORA,), F32), a.reshape(-1), (me * 64,))
        if name == "mla_kv_lora_norm":
            return lax.dynamic_update_slice(jnp.zeros((KV_LORA,), F32), a.reshape(-1), (me * 32,))
        return a

    wp = _pack_small({n: whole(n, weights[n]) for n, _ in _SMALL})
    mp = _pack_small({n: whole(n, m_in[n]) for n, _ in _SMALL})
    vp = _pack_small({n: whole(n, v_in[n]) for n, _ in _SMALL})
    gs, ds, ms, vs = _small_update(small_gath, wp, mp, vp, "small_update")

    for n, _ in _SMALL:
        shape = weights[n].shape
        if n == "mla_q_lora_norm":
            take = lambda p: lax.dynamic_slice(_unpack_small(p, n, (Q_LORA,)), (me * 64,), (64,)).reshape(shape)
        elif n == "mla_kv_lora_norm":
            take = lambda p: lax.dynamic_slice(_unpack_small(p, n, (KV_LORA,)), (me * 32,), (32,)).reshape(shape)
        else:
            take = lambda p: _unpack_small(p, n, shape)
        grads[n], delta[n], new_m[n], new_v[n] = take(gs), take(ds), take(ms), take(vs)
    return (loss, dx0[None], *[grads[n] for n in _WEIGHTS], *[delta[n] for n in _WEIGHTS],
            *[new_m[n] for n in _WEIGHTS], *[new_v[n] for n in _WEIGHTS])
```

```python
import functools
import math

import numpy as np
import jax
import jax.numpy as jnp
from jax import lax
from jax.experimental import pallas as pl
from jax.experimental.pallas import tpu as pltpu

F32 = jnp.float32
BF16 = jnp.bfloat16
EPS = 1e-6
NEG = float(np.finfo(np.float32).min)
MESH = pl.DeviceIdType.MESH

N_DEV = 8
D_MODEL = 1024
MEM_LEN = 256
XQ = 512
PRIM = 1536
BRANCH = 2048
X_HEADS = 4
HD = 128
S5_G = 96
S5_P = 64
S5_C = 16
S5_GB = 8
S5_W = S5_GB * S5_P
MLA_H = 12
ROPE = 64
Q_LORA = 512
KV_LORA = 256
ROPE_THETA = 10000.0

ADAM_LR = 0.001
ADAM_B1 = 0.9
ADAM_B2 = 0.999
ADAM_EPS = 1e-08
ADAM_WD = 0.01
ADAM_STEP = 10

VMEM_LIMIT = 56 * 1024 * 1024


def _dot(a, b):
    return jnp.dot(a, b, preferred_element_type=F32)


def _dot_nt(a, b):
    return lax.dot_general(a, b, (((1,), (1,)), ((), ())), preferred_element_type=F32)


def _dot_tn(a, b):
    return lax.dot_general(a, b, (((0,), (0,)), ((), ())), preferred_element_type=F32)


@jax.custom_vjp
def _mm(a, b):
    return _dot(a.astype(BF16), b.astype(BF16))


def _mm_fwd(a, b):
    return _mm(a, b), (a, b)


def _mm_bwd(res, g):
    a, b = res
    gb = g.astype(BF16)
    return _dot_nt(gb, b.astype(BF16)).astype(a.dtype), _dot_tn(a.astype(BF16), gb).astype(b.dtype)


_mm.defvjp(_mm_fwd, _mm_bwd)


@jax.custom_vjp
def _mm_nt(a, b):
    return _dot_nt(a.astype(BF16), b.astype(BF16))


def _mm_nt_fwd(a, b):
    return _mm_nt(a, b), (a, b)


def _mm_nt_bwd(res, g):
    a, b = res
    gb = g.astype(BF16)
    return _dot(gb, b.astype(BF16)).astype(a.dtype), _dot_tn(gb, a.astype(BF16)).astype(b.dtype)


_mm_nt.defvjp(_mm_nt_fwd, _mm_nt_bwd)


@jax.custom_vjp
def _softmax(s):
    m = jnp.max(s, axis=-1, keepdims=True)
    e = jnp.exp(s - m)
    return e / jnp.sum(e, axis=-1, keepdims=True)


def _softmax_fwd(s):
    p = _softmax(s)
    return p, p


def _softmax_bwd(p, g):
    return (p * (g - jnp.sum(p * g, axis=-1, keepdims=True)),)


_softmax.defvjp(_softmax_fwd, _softmax_bwd)


def _rms(x, g, n):
    ms = jnp.sum(x * x, axis=-1, keepdims=True) * (1.0 / n)
    return x * lax.rsqrt(ms + EPS) * g


def _sigmoid(x):
    return 1.0 / (1.0 + jnp.exp(-x))


def _silu(x):
    return x * _sigmoid(x)


def _gelu(x):
    c = math.sqrt(2.0 / math.pi)
    return 0.5 * x * (1.0 + jnp.tanh(c * (x + 0.044715 * (x * x * x))))


@jax.custom_vjp
def _rot(x, c, s1, s2):
    return x * c + pltpu.roll(x, 96, 1) * s1 + pltpu.roll(x, 32, 1) * s2


def _rot_fwd(x, c, s1, s2):
    return _rot(x, c, s1, s2), (c, s1, s2)


def _rot_bwd(res, g):
    c, s1, s2 = res
    dx = g * c + pltpu.roll(g * s1, 32, 1) + pltpu.roll(g * s2, 96, 1)
    return dx, jnp.zeros_like(c), jnp.zeros_like(s1), jnp.zeros_like(s2)


_rot.defvjp(_rot_fwd, _rot_bwd)


def _mem_attn(xq, k, v, gq):
    outs = []
    for h in range(X_HEADS):
        sl = slice(HD * h, HD * (h + 1))
        q = _rms(xq[:, sl], gq, HD)
        p = _softmax(_mm_nt(q, k[:, sl]) * (HD ** -0.5))
        outs.append(_mm(p, v[:, sl]))
    return jnp.concatenate(outs, axis=-1)


def _merge(mix, xq, gate, k, v, gq):
    return jnp.concatenate([mix, _mem_attn(xq, k, v, gq)], axis=-1) * _silu(gate)


def _q_post(q, gqn, gqr, c, s1, s2):
    pieces = []
    for h in range(MLA_H):
        pieces.append(_rms(q[:, HD * h:HD * (h + 1)], gqn, HD))
        pieces.append(_rot(_rms(q[:, PRIM + HD * h:PRIM + HD * (h + 1)], gqr, ROPE), c, s1, s2))
    return jnp.concatenate(pieces, axis=-1)


def _kv_post(kv, krp, gkn, gkr, c, s1, s2):
    kr = _rot(_rms(krp, gkr, ROPE), c, s1, s2)
    pieces, vals = [], []
    for h in range(MLA_H):
        pieces.append(_rms(kv[:, 2 * HD * h:2 * HD * h + HD], gkn, HD))
        pieces.append(kr)
        vals.append(kv[:, 2 * HD * h + HD:2 * HD * (h + 1)])
    return jnp.concatenate(pieces, axis=-1), jnp.concatenate(vals, axis=-1)


def _rowwise(name, fn, ins, outs, nblk, sub=1):
    n_in = len(ins)

    def spec(kind, shape):
        if kind == 'r':
            return pl.BlockSpec((shape[0] // nblk, shape[1]), lambda i: (i, 0))
        zeros = (0,) * len(shape)
        return pl.BlockSpec(tuple(shape), lambda i: zeros)

    def body(*refs):
        i = pl.program_id(0)
        res = fn(*[r[...] for r in refs[:n_in]])
        for (kind, _, _), ref, val in zip(outs, refs[n_in:], res):
            if kind == 'a':
                @pl.when(i == 0)
                def _():
                    ref[...] = jnp.zeros_like(ref)
                ref[...] += val.astype(ref.dtype)
            else:
                ref[...] = val.astype(ref.dtype)

    res = pl.pallas_call(
        body, name=name, grid=(nblk,),
        in_specs=[spec(k, a.shape) for k, a in ins],
        out_specs=[spec(k, s) for k, s, _ in outs],
        out_shape=[jax.ShapeDtypeStruct(tuple(s), d) for _, s, d in outs],
        compiler_params=pltpu.CompilerParams(dimension_semantics=("arbitrary",), vmem_limit_bytes=VMEM_LIMIT),
    )(*[a for _, a in ins])
    return res


def _matmul_tn(a, g, name, out_dtype=BF16):
    L, K = a.shape
    N = g.shape[1]
    tn = next(t for t in (512, 384, 256, 128) if N % t == 0)
    tl = min(512, L)
    nl = L // tl

    def body(a_ref, g_ref, o_ref, acc):
        l = pl.program_id(1)

        @pl.when(l == 0)
        def _():
            acc[...] = jnp.zeros_like(acc)

        acc[...] += _dot_tn(a_ref[...], g_ref[...])

        @pl.when(l == nl - 1)
        def _():
            o_ref[...] = acc[...].astype(o_ref.dtype)

    return pl.pallas_call(
        body, name=name, grid=(N // tn, nl),
        in_specs=[pl.BlockSpec((tl, K), lambda n, l: (l, 0)), pl.BlockSpec((tl, tn), lambda n, l: (l, n))],
        out_specs=pl.BlockSpec((K, tn), lambda n, l: (0, n)),
        out_shape=jax.ShapeDtypeStruct((K, N), out_dtype),
        scratch_shapes=[pltpu.VMEM((K, tn), F32)],
        compiler_params=pltpu.CompilerParams(dimension_semantics=("arbitrary", "arbitrary"),
                                             vmem_limit_bytes=VMEM_LIMIT),
    )(a, g)


def _matmul_tn_slots(a, g, name):
    L, K = a.shape
    n = g.shape[1] // N_DEV
    tl = min(512, L)
    nl = L // tl

    def body(a_ref, g_ref, o_ref, acc):
        l = pl.program_id(1)

        @pl.when(l == 0)
        def _():
            acc[...] = jnp.zeros_like(acc)

        acc[...] += _dot_tn(a_ref[...], g_ref[...])

        @pl.when(l == nl - 1)
        def _():
            o_ref[...] = acc[...].astype(o_ref.dtype)

    return pl.pallas_call(
        body, name=name, grid=(N_DEV, nl),
        in_specs=[pl.BlockSpec((tl, K), lambda d, l: (l, 0)), pl.BlockSpec((tl, n), lambda d, l: (l, d))],
        out_specs=pl.BlockSpec((None, K, n), lambda d, l: (d, 0, 0)),
        out_shape=jax.ShapeDtypeStruct((N_DEV, K, n), BF16),
        scratch_shapes=[pltpu.VMEM((K, n), F32)],
        compiler_params=pltpu.CompilerParams(dimension_semantics=("arbitrary", "arbitrary"),
                                             vmem_limit_bytes=VMEM_LIMIT),
    )(a, g)


def _mm_slots(a16, w):
    return jnp.concatenate([_dot(a16, w[d]) for d in range(N_DEV)], axis=-1)


def _mm_slots_nt(g16, w):
    n = w.shape[2]
    out = _dot_nt(g16[:, 0:n], w[0])
    for d in range(1, N_DEV):
        out = out + _dot_nt(g16[:, d * n:(d + 1) * n], w[d])
    return out


class _Exchange:
    def __init__(self, ins, outs, scratch, start, finish):
        self.ins, self.outs, self.scratch, self.start, self.finish = ins, outs, scratch, start, finish


def _xyc():
    return lax.axis_index("x"), lax.axis_index("y"), lax.axis_index("c")


def _plan_all_gather(xs):
    n = len(xs)

    def build(x_refs, out_refs, sems):
        send_sems, recv_sems, local_sems = sems
        x, y, c = _xyc()

        def copies(k, block, to, own=False):
            slot = 4 * block[0] + 2 * block[1] + block[2]
            return [pltpu.make_async_remote_copy(
                src_ref=x_refs[a] if own else out_refs[a].at[slot], dst_ref=out_refs[a].at[slot],
                send_sem=send_sems.at[k * n + a], recv_sem=recv_sems.at[k * n + a], device_id=to,
                device_id_type=MESH) for a in range(n)]

        mine = [pltpu.make_async_copy(x_refs[a], out_refs[a].at[4 * x + 2 * y + c], local_sems.at[a])
                for a in range(n)]
        return copies, mine, (x, y, c), [(1 - x, y), (x, 1 - y), (1 - x, 1 - y)]

    def first_copies(copies, me, chips):
        x, y, c = me
        first = copies(0, me, (x, y, 1 - c), own=True)
        for j, chip in enumerate(chips):
            first += copies(1 + j, me, (*chip, c), own=True)
        return first

    def start(x_refs, out_refs, sems):
        copies, mine, me, chips = build(x_refs, out_refs, sems)
        for cp in mine + first_copies(copies, me, chips):
            cp.start()

    def finish(x_refs, out_refs, sems):
        copies, mine, me, chips = build(x_refs, out_refs, sems)
        x, y, c = me
        passed = []
        for j, chip in enumerate(chips):
            for cp in copies(1 + j, (*chip, c), me):
                cp.wait_recv()
            fwd = copies(4 + j, (*chip, c), (x, y, 1 - c))
            for cp in fwd:
                cp.start()
            passed += fwd
        for cp in copies(0, (x, y, 1 - c), me):
            cp.wait_recv()
        for j, chip in enumerate(chips):
            for cp in copies(4 + j, (*chip, 1 - c), me):
                cp.wait_recv()
        for cp in first_copies(copies, me, chips) + passed:
            cp.wait_send()
        for cp in mine:
            cp.wait()

    return _Exchange(list(xs), [jax.ShapeDtypeStruct((N_DEV,) + a.shape, a.dtype) for a in xs],
                     [pltpu.SemaphoreType.DMA((7 * n,)), pltpu.SemaphoreType.DMA((7 * n,)),
                      pltpu.SemaphoreType.DMA((n,))], start, finish)


_CHIPS = ((0, 0), (0, 1), (1, 0), (1, 1))


def _plan_pair(sends):
    n = len(sends)

    def build(s_refs, o_refs, sems):
        send_sems, recv_sems = sems
        x, y, c = _xyc()
        return [pltpu.make_async_remote_copy(
            src_ref=s_refs[a].at[4 * px + 2 * py + 1 - c], dst_ref=o_refs[a].at[j],
            send_sem=send_sems.at[j * n + a], recv_sem=recv_sems.at[j * n + a], device_id=(x, y, 1 - c),
            device_id_type=MESH) for j, (px, py) in enumerate(_CHIPS) for a in range(n)]

    def start(s_refs, o_refs, sems):
        for cp in build(s_refs, o_refs, sems):
            cp.start()

    def finish(s_refs, o_refs, sems):
        for cp in build(s_refs, o_refs, sems):
            cp.wait_recv()
            cp.wait_send()

    return _Exchange(list(sends), [jax.ShapeDtypeStruct((4,) + a.shape[1:], a.dtype) for a in sends],
                     [pltpu.SemaphoreType.DMA((4 * n,)), pltpu.SemaphoreType.DMA((4 * n,))], start, finish)


def _plan_chips(ts):
    n = len(ts)
    flips = ((1, 0), (0, 1), (1, 1))

    def build(t_refs, o_refs, sems):
        send_sems, recv_sems, local_sems = sems
        x, y, c = _xyc()
        mine = 2 * x + y
        local = [pltpu.make_async_copy(t_refs[a].at[mine], o_refs[a].at[mine], local_sems.at[a]) for a in range(n)]
        remote = []
        for k, (fx, fy) in enumerate(flips):
            px = 1 - x if fx else x
            py = 1 - y if fy else y
            remote += [pltpu.make_async_remote_copy(
                src_ref=t_refs[a].at[2 * px + py], dst_ref=o_refs[a].at[mine],
                send_sem=send_sems.at[k * n + a], recv_sem=recv_sems.at[k * n + a], device_id=(px, py, c),
                device_id_type=MESH) for a in range(n)]
        return local, remote

    def start(t_refs, o_refs, sems):
        local, remote = build(t_refs, o_refs, sems)
        for cp in local + remote:
            cp.start()

    def finish(t_refs, o_refs, sems):
        local, remote = build(t_refs, o_refs, sems)
        for cp in remote:
            cp.wait_recv()
        for cp in remote:
            cp.wait_send()
        for cp in local:
            cp.wait()

    return _Exchange(list(ts), [jax.ShapeDtypeStruct(a.shape, a.dtype) for a in ts],
                     [pltpu.SemaphoreType.DMA((3 * n,)), pltpu.SemaphoreType.DMA((3 * n,)),
                      pltpu.SemaphoreType.DMA((n,))], start, finish)


def _exchange_call(plan, name):
    n = len(plan.ins)

    def body(*refs):
        ins, outs, sems = refs[:n], refs[n:2 * n], refs[2 * n:]
        plan.start(ins, outs, sems)
        plan.finish(ins, outs, sems)

    return pl.pallas_call(
        body, name=name, out_shape=plan.outs,
        in_specs=[pl.BlockSpec(memory_space=pl.ANY)] * n, out_specs=[pl.BlockSpec(memory_space=pl.ANY)] * n,
        scratch_shapes=plan.scratch,
    )(*plan.ins)


def _pair_add(sends, fromsib, name):
    n = len(sends)
    nb = 8

    def body(*refs):
        c = lax.axis_index("c")
        for a in range(n):
            s_ref, f_ref, t_ref = refs[a], refs[n + a], refs[2 * n + a]
            for j in range(4):
                t_ref[j] = (s_ref[2 * j + c].astype(F32) + f_ref[j].astype(F32)).astype(t_ref.dtype)

    def spec(a, lead):
        return pl.BlockSpec((lead, a.shape[1] // nb, a.shape[2]), lambda i: (0, i, 0))

    return pl.pallas_call(
        body, name=name, grid=(nb,),
        in_specs=[spec(a, N_DEV) for a in sends] + [spec(a, 4) for a in fromsib],
        out_specs=[spec(a, 4) for a in fromsib],
        out_shape=[jax.ShapeDtypeStruct(a.shape, a.dtype) for a in fromsib],
        compiler_params=pltpu.CompilerParams(dimension_semantics=("arbitrary",), vmem_limit_bytes=VMEM_LIMIT),
    )(*sends, *fromsib)


def _adamw_vals(w, g, m, v):
    m2 = ADAM_B1 * m + (1.0 - ADAM_B1) * g
    v2 = ADAM_B2 * v + (1.0 - ADAM_B2) * (g * g)
    m_hat = m2 / (1.0 - ADAM_B1 ** ADAM_STEP)
    v_hat = v2 / (1.0 - ADAM_B2 ** ADAM_STEP)
    delta = -ADAM_LR * (m_hat / (jnp.sqrt(v_hat) + ADAM_EPS) + ADAM_WD * w)
    return delta, m2, v2


def _sum_adamw(recv, w, m, v, name):
    R, C = w.shape
    ns = recv.shape[0]
    br = next((t for t in (256, 128, 64, 32, 16) if R % t == 0), R)

    def body(r_ref, w_ref, m_ref, v_ref, g_ref, d_ref, m2_ref, v2_ref):
        g = r_ref[0].astype(F32)
        for d in range(1, ns):
            g = g + r_ref[d].astype(F32)
        dl, m2, v2 = _adamw_vals(w_ref[...], g, m_ref[...], v_ref[...])
        g_ref[...] = g
        d_ref[...] = dl
        m2_ref[...] = m2
        v2_ref[...] = v2

    spec = pl.BlockSpec((br, C), lambda i: (i, 0))
    return pl.pallas_call(
        body, name=name, grid=(R // br,),
        in_specs=[pl.BlockSpec((ns, br, C), lambda i: (0, i, 0)), spec, spec, spec], out_specs=[spec] * 4,
        out_shape=[jax.ShapeDtypeStruct((R, C), F32)] * 4,
        compiler_params=pltpu.CompilerParams(dimension_semantics=("arbitrary",)),
    )(recv, w, m, v)


def _small_update(gath, wp, mp, vp, name):
    _, R, C = gath.shape
    br = R // 3

    def body(g_ref, w_ref, m_ref, v_ref, go_ref, d_ref, m2_ref, v2_ref):
        g = g_ref[0]
        for d in range(1, N_DEV):
            g = g + g_ref[d]
        dl, m2, v2 = _adamw_vals(w_ref[...], g, m_ref[...], v_ref[...])
        go_ref[...] = g
        d_ref[...] = dl
        m2_ref[...] = m2
        v2_ref[...] = v2

    spec = pl.BlockSpec((br, C), lambda i: (i, 0))
    return pl.pallas_call(
        body, name=name, grid=(R // br,),
        in_specs=[pl.BlockSpec((N_DEV, br, C), lambda i: (0, i, 0)), spec, spec, spec],
        out_specs=[spec] * 4, out_shape=[jax.ShapeDtypeStruct((R, C), F32)] * 4,
        compiler_params=pltpu.CompilerParams(dimension_semantics=("arbitrary",)),
    )(gath, wp, mp, vp)


def _s5_param_fn(lr, li, ls, btr, bti):
    step = jnp.exp(ls)
    er = jnp.exp(lr * step)
    ang = li * step
    ar = er * jnp.cos(ang)
    ai = er * jnp.sin(ang)
    nr = ar - 1.0
    den = lr * lr + li * li
    fr = (nr * lr + ai * li) / den
    fi = (ai * lr - nr * li) / den
    return ar, ai, fr * btr - fi * bti, fr * bti + fi * btr


def _s5_params(lr, li, ls, btr, bti):
    def body(lr_ref, li_ref, ls_ref, br_ref, bi_ref, ar_ref, ai_ref, bbr_ref, bbi_ref):
        ar, ai, bbr, bbi = _s5_param_fn(lr_ref[...], li_ref[...], ls_ref[...], br_ref[...], bi_ref[...])
        ar_ref[...] = ar
        ai_ref[...] = ai
        bbr_ref[...] = bbr
        bbi_ref[...] = bbi

    sd = jax.ShapeDtypeStruct
    return pl.pallas_call(
        body, name="s5_params",
        out_shape=[sd(lr.shape, F32), sd(lr.shape, F32), sd(btr.shape, F32), sd(btr.shape, F32)],
    )(lr, li, ls, btr, bti)


def _s5_params_bwd(lr, li, ls, btr, bti, dar, dai, dbbr, dbbi):
    def body(lr_ref, li_ref, ls_ref, br_ref, bi_ref, dar_ref, dai_ref, dbbr_ref, dbbi_ref,
             dlr_ref, dli_ref, dls_ref, dbr_ref, dbi_ref):
        _, vjp = jax.vjp(_s5_param_fn, lr_ref[...], li_ref[...], ls_ref[...], br_ref[...], bi_ref[...])
        dlr, dli, dls, dbr, dbi = vjp((dar_ref[...], dai_ref[...], dbbr_ref[...], dbbi_ref[...]))
        dlr_ref[...] = dlr
        dli_ref[...] = dli
        dls_ref[...] = dls
        dbr_ref[...] = dbr
        dbi_ref[...] = dbi

    sd = jax.ShapeDtypeStruct
    return pl.pallas_call(
        body, name="s5_params_bwd",
        out_shape=[sd(lr.shape, F32), sd(lr.shape, F32), sd(ls.shape, F32), sd(btr.shape, F32), sd(btr.shape, F32)],
    )(lr, li, ls, btr, bti, dar, dai, dbbr, dbbi)


def _cpow(ar, ai, n):
    assert n & (n - 1) == 0
    while n > 1:
        ar, ai = ar * ar - ai * ai, 2.0 * ar * ai
        n //= 2
    return ar, ai


def _scan(st, cr, ci, init, nk, reverse, store, prev=None):
    W = S5_W

    def step(j, carry):
        k = nk - 1 - j if reverse else j
        rows = pl.ds(pl.multiple_of(k * 8, 8), 8)
        sr, si = carry[0], carry[1]
        nsr = cr * sr - ci * si + st[rows, 0:W]
        nsi = cr * si + ci * sr + st[rows, W:2 * W]
        if store:
            st[rows, 0:W] = nsr
            st[rows, W:2 * W] = nsi
        if prev is None:
            return nsr, nsi
        prows = pl.ds(pl.multiple_of(jnp.maximum(k - 1, 0) * 8, 8), 8)
        w = jnp.where(k > 0, 1.0, 0.0).astype(F32)
        pr = prev[prows, 0:W] * w
        pi = prev[prows, W:2 * W] * w
        return nsr, nsi, carry[2] + nsr * pr + nsi * pi, carry[3] + nsi * pr - nsr * pi

    return lax.fori_loop(0, nk, step, init, unroll=2)


def _chain(fin, fr, fi, pr, pi, reverse):
    W = S5_W
    fin[:, 0:W] = fr
    fin[:, W:2 * W] = fi
    rowid = lax.broadcasted_iota(jnp.int32, (8, W), 0)
    cr = jnp.zeros((1, W), F32)
    ci = jnp.zeros((1, W), F32)
    init_r = jnp.zeros((8, W), F32)
    init_i = jnp.zeros((8, W), F32)
    for s in (range(7, -1, -1) if reverse else range(8)):
        init_r = jnp.where(rowid == s, cr, init_r)
        init_i = jnp.where(rowid == s, ci, init_i)
        lr = fin[s:s + 1, 0:W]
        li = fin[s:s + 1, W:2 * W]
        cr, ci = lr + pr * cr - pi * ci, li + pr * ci + pi * cr
    return init_r, init_i


def _full_scan(st, fin, ar, ai, nk, reverse, prev=None):
    W = S5_W
    cr = jnp.broadcast_to(ar, (8, W))
    ci = jnp.broadcast_to(-ai if reverse else ai, (8, W))
    z = jnp.zeros((8, W), F32)
    fr, fi = _scan(st, cr, ci, (z, z), nk, reverse, store=False)
    pr, pi = _cpow(ar, -ai if reverse else ai, nk)
    init = _chain(fin, fr, fi, pr, pi, reverse)
    if prev is None:
        return _scan(st, cr, ci, init, nk, reverse, store=True)
    return _scan(st, cr, ci, init + (z, z), nk, reverse, store=True, prev=prev)


def _s5_specs(L):
    W2 = 2 * S5_W
    GC = S5_GB * S5_C
    col = pl.BlockSpec((L, GC), lambda g: (0, g))
    vec = pl.BlockSpec((1, GC), lambda g: (0, g))
    avec = pl.BlockSpec((1, S5_W), lambda g: (0, g))
    bmat = pl.BlockSpec((None, GC, W2), lambda g: (g, 0, 0))
    cmat = pl.BlockSpec((None, W2, GC), lambda g: (g, 0, 0))
    return col, vec, avec, bmat, cmat


def _interleave(dst, src, nk):
    for s in range(8):
        dst[pl.ds(s, nk, stride=8), :] = src[s * nk:(s + 1) * nk, :]


def _deinterleave(dst, src, nk):
    for s in range(8):
        dst[s * nk:(s + 1) * nk, :] = src[pl.ds(s, nk, stride=8), :]


def _hosting_call(body, name, nsteps, host, ins, in_specs, outs, out_specs, scratch):
    if host is None:
        res = pl.pallas_call(
            body, name=name, grid=(nsteps,), in_specs=in_specs, out_specs=out_specs, out_shape=outs,
            scratch_shapes=scratch,
            compiler_params=pltpu.CompilerParams(dimension_semantics=("arbitrary",), vmem_limit_bytes=VMEM_LIMIT),
        )(*ins)
        return list(res), []
    n_in, n_out, n_sc = len(ins), len(outs), len(scratch)
    h_in, h_out = len(host.ins), len(host.outs)

    def hosted(*refs):
        a = refs[:n_in]
        ha = refs[n_in:n_in + h_in]
        o = refs[n_in + h_in:n_in + h_in + n_out]
        ho = refs[n_in + h_in + n_out:n_in + h_in + n_out + h_out]
        sc = refs[n_in + h_in + n_out + h_out:n_in + h_in + n_out + h_out + n_sc]
        hs = refs[n_in + h_in + n_out + h_out + n_sc:]
        step = pl.program_id(0)

        @pl.when(step == 0)
        def _():
            host.start(ha, ho, hs)

        body(*a, *o, *sc)

        @pl.when(step == nsteps - 1)
        def _():
            host.finish(ha, ho, hs)

    hbm = pl.BlockSpec(memory_space=pl.ANY)
    res = pl.pallas_call(
        hosted, name=name, grid=(nsteps,),
        in_specs=list(in_specs) + [hbm] * h_in, out_specs=list(out_specs) + [hbm] * h_out,
        out_shape=list(outs) + list(host.outs), scratch_shapes=list(scratch) + list(host.scratch),
        compiler_params=pltpu.CompilerParams(dimension_semantics=("arbitrary",), vmem_limit_bytes=VMEM_LIMIT),
    )(*ins, *host.ins)
    return list(res[:n_out]), list(res[n_out:])


def _s5_fwd(u, bm, cm, ar, ai, dvec, host=None):
    L = u.shape[0]
    nk = L // 8
    GC = S5_GB * S5_C
    col, vec, avec, bmat, cmat = _s5_specs(L)

    def body(u_ref, b_ref, c_ref, ar_ref, ai_ref, d_ref, y_ref, st, fin, ui, yi):
        _interleave(ui, u_ref, nk)
        for r in range(8):
            rows = slice(r * nk, (r + 1) * nk)
            st[rows, :] = _dot(ui[rows, :].astype(BF16), b_ref[...])
        _full_scan(st, fin, ar_ref[...], ai_ref[...], nk, reverse=False)
        for r in range(8):
            rows = slice(r * nk, (r + 1) * nk)
            yi[rows, :] = _dot(st[rows, :].astype(BF16), c_ref[...]) + d_ref[...] * ui[rows, :]
        _deinterleave(y_ref, yi, nk)

    return _hosting_call(
        body, "s5_fwd", S5_G // S5_GB, host,
        [u, bm, cm, ar, ai, dvec], [col, bmat, cmat, avec, avec, vec],
        [jax.ShapeDtypeStruct(u.shape, F32)], [col],
        [pltpu.VMEM((L, 2 * S5_W), F32), pltpu.VMEM((8, 2 * S5_W), F32), pltpu.VMEM((L, GC), F32),
         pltpu.VMEM((L, GC), F32)])


def _s5_bwd(u, dy, bm, bmt, cmt, ar, ai, dvec, mask, rmat, host=None):
    L = u.shape[0]
    nk = L // 8
    W = S5_W
    GC = S5_GB * S5_C
    col, vec, avec, bmat, cmat = _s5_specs(L)
    hi = lax.Precision.HIGHEST

    def body(u_ref, dy_ref, b_ref, bt_ref, ct_ref, ar_ref, ai_ref, d_ref, mask_ref, r_ref,
             du_ref, db_ref, dc_ref, dd_ref, dar_ref, dai_ref, sa, sb, fin, ui, dyi, dui):
        ar = ar_ref[...]
        ai = ai_ref[...]
        _interleave(ui, u_ref, nk)
        _interleave(dyi, dy_ref, nk)
        for r in range(8):
            rows = slice(r * nk, (r + 1) * nk)
            sa[rows, :] = _dot(ui[rows, :].astype(BF16), b_ref[...])
            sb[rows, :] = _dot(dyi[rows, :].astype(BF16), ct_ref[...])
        _full_scan(sa, fin, ar, ai, nk, reverse=False)
        gr, gi, accr, acci = _full_scan(sb, fin, ar, ai, nk, reverse=True, prev=sa)
        rowid = lax.broadcasted_iota(jnp.int32, (8, W), 0)
        last = pl.ds((nk - 1) * 8, 8)
        pr = jnp.where(rowid == 0, 0.0, pltpu.roll(sa[last, 0:W], 1, 0))
        pi = jnp.where(rowid == 0, 0.0, pltpu.roll(sa[last, W:2 * W], 1, 0))
        accr = accr + gr * pr + gi * pi
        acci = acci + gi * pr - gr * pi
        dar_ref[...] = jnp.sum(accr, axis=0, keepdims=True)
        dai_ref[...] = jnp.sum(acci, axis=0, keepdims=True)
        dbf = jnp.zeros((GC, 2 * W), F32)
        dcf = jnp.zeros((GC, 2 * W), F32)
        dd = jnp.zeros((1, GC), F32)
        for r in range(8):
            rows = slice(r * nk, (r + 1) * nk)
            ub = ui[rows, :]
            dyb = dyi[rows, :]
            gb = sb[rows, :].astype(BF16)
            dui[rows, :] = _dot(gb, bt_ref[...]) + d_ref[...] * dyb
            dbf = dbf + _dot_tn(ub.astype(BF16), gb)
            dcf = dcf + _dot_tn(dyb.astype(BF16), sa[rows, :].astype(BF16))
            dd = dd + jnp.sum(dyb * ub, axis=0, keepdims=True)
        db_ref[...] = jnp.dot(dbf * mask_ref[...], r_ref[...], precision=hi, preferred_element_type=F32)
        dc_ref[...] = jnp.dot(dcf * mask_ref[...], r_ref[...], precision=hi, preferred_element_type=F32)
        dd_ref[...] = dd
        _deinterleave(du_ref, dui, nk)

    cmp_spec = pl.BlockSpec((GC, 2 * S5_P), lambda g: (g, 0))
    whole = lambda shape: pl.BlockSpec(shape, lambda g: (0, 0))
    sd = jax.ShapeDtypeStruct
    return _hosting_call(
        body, "s5_bwd", S5_G // S5_GB, host,
        [u, dy, bm, bmt, cmt, ar, ai, dvec, mask, rmat],
        [col, col, bmat, cmat, bmat, avec, avec, vec, whole(mask.shape), whole(rmat.shape)],
        [sd(u.shape, F32), sd((S5_G * S5_C, 2 * S5_P), F32), sd((S5_G * S5_C, 2 * S5_P), F32),
         sd((1, PRIM), F32), sd((1, S5_G * S5_P), F32), sd((1, S5_G * S5_P), F32)],
        [col, cmp_spec, cmp_spec, vec, avec, avec],
        [pltpu.VMEM((L, 2 * W), F32), pltpu.VMEM((L, 2 * W), F32), pltpu.VMEM((8, 2 * W), F32),
         pltpu.VMEM((L, GC), F32), pltpu.VMEM((L, GC), F32), pltpu.VMEM((L, GC), F32)])


def _s5_mats(bbr, bbi, cre, cim):
    nb = S5_G // S5_GB
    eye = jnp.eye(S5_GB, dtype=F32)
    bb = jnp.stack([bbr, bbi], axis=2).reshape(nb, S5_GB, S5_C, 2, S5_P)
    bm = jnp.einsum('ngcrp,gh->ngcrhp', bb, eye).reshape(nb, S5_GB * S5_C, 2 * S5_W)
    cc = jnp.stack([cre, -cim], axis=2).reshape(nb, S5_GB, S5_C, 2, S5_P)
    cmt = jnp.einsum('ngcrp,gh->ngcrhp', cc, eye).reshape(nb, S5_GB * S5_C, 2 * S5_W)
    return (bm.astype(BF16), jnp.swapaxes(bm, 1, 2).astype(BF16),
            jnp.swapaxes(cmt, 1, 2).astype(BF16), cmt.astype(BF16))


def _s5_compact_consts():
    g_row = np.arange(S5_GB * S5_C) // S5_C
    col = np.arange(2 * S5_W)
    g_col = (col % S5_W) // S5_P
    mask = (g_row[:, None] == g_col[None, :]).astype(np.float32)
    tgt = (col // S5_W) * S5_P + col % S5_P
    rmat = (tgt[:, None] == np.arange(2 * S5_P)[None, :]).astype(np.float32)
    return jnp.asarray(mask), jnp.asarray(rmat)


def _attn_scores(q_ref, k_ref, qb, bq, scale):
    ext = (qb + 1) * bq
    s = _dot_nt(q_ref[qb * bq:ext, :], k_ref[0:ext, :]) * scale
    qpos = lax.broadcasted_iota(jnp.int32, (bq, bq), 0)
    kpos = lax.broadcasted_iota(jnp.int32, (bq, bq), 1)
    diag = jnp.where(kpos <= qpos, s[:, ext - bq:], NEG)
    return diag if qb == 0 else jnp.concatenate([s[:, :ext - bq], diag], axis=-1)


def _attn_fwd(qp, kp, v, scale):
    L = qp.shape[0]
    bq = min(256, L)

    def body(q_ref, k_ref, v_ref, o_ref, lse_ref):
        for qb in range(L // bq):
            rows = slice(qb * bq, (qb + 1) * bq)
            s = _attn_scores(q_ref, k_ref, qb, bq, scale)
            m = jnp.max(s, axis=-1, keepdims=True)
            e = jnp.exp(s - m)
            l = jnp.sum(e, axis=-1, keepdims=True)
            o_ref[rows, :] = _dot(e.astype(BF16), v_ref[0:(qb + 1) * bq, :]) / l
            lse_ref[rows, :] = jnp.broadcast_to(m + jnp.log(l), (bq, HD))

    blk = pl.BlockSpec((L, HD), lambda h: (0, h))
    wide = pl.BlockSpec((L, 2 * HD), lambda h: (0, h))
    return pl.pallas_call(
        body, name="mla_attn_fwd", grid=(MLA_H,),
        in_specs=[wide, wide, blk], out_specs=[blk, blk],
        out_shape=[jax.ShapeDtypeStruct((L, MLA_H * HD), F32)] * 2,
        compiler_params=pltpu.CompilerParams(dimension_semantics=("arbitrary",), vmem_limit_bytes=VMEM_LIMIT),
    )(qp, kp, v)


def _attn_bwd(qp, kp, v, o, lse, do, scale):
    L = qp.shape[0]
    bq = min(256, L)
    nq = L // bq

    def body(q_ref, k_ref, v_ref, o_ref, lse_ref, do_ref, dq_ref, dk_ref, dv_ref):
        dk_ref[...] = jnp.zeros_like(dk_ref)
        dv_ref[...] = jnp.zeros_like(dv_ref)
        for qb in range(nq):
            rows = slice(qb * bq, (qb + 1) * bq)
            ext = (qb + 1) * bq
            do = do_ref[rows, :]
            dob = do.astype(BF16)
            p = jnp.exp(_attn_scores(q_ref, k_ref, qb, bq, scale) - lse_ref[rows, 0:1])
            dp = _dot_nt(dob, v_ref[0:ext, :])
            dsum = jnp.sum(do * o_ref[rows, :], axis=-1, keepdims=True)
            ds = (p * (dp - dsum) * scale).astype(BF16)
            dq_ref[rows, :] = _dot(ds, k_ref[0:ext, :])
            dk_ref[0:ext, :] += _dot_tn(ds, q_ref[rows, :])
            dv_ref[0:ext, :] += _dot_tn(p.astype(BF16), dob)

    sd = jax.ShapeDtypeStruct
    blk = pl.BlockSpec((L, HD), lambda h: (0, h))
    wide = pl.BlockSpec((L, 2 * HD), lambda h: (0, h))
    return pl.pallas_call(
        body, name="mla_attn_bwd", grid=(MLA_H,),
        in_specs=[wide, wide, blk, blk, blk, blk], out_specs=[wide, wide, blk],
        out_shape=[sd((L, MLA_H * 2 * HD), F32), sd((L, MLA_H * 2 * HD), F32), sd((L, MLA_H * HD), F32)],
        compiler_params=pltpu.CompilerParams(dimension_semantics=("arbitrary",), vmem_limit_bytes=VMEM_LIMIT),
    )(qp, kp, v, o, lse, do)


def _kv_fn(mem, gm, w, gk):
    kv = _mm(_rms(mem, gm, D_MODEL), w)
    k = jnp.concatenate([_rms(kv[:, HD * h:HD * (h + 1)], gk, HD) for h in range(X_HEADS)], axis=-1)
    return k, kv[:, XQ:]


def _kv_prep(mem, gm, w, gk, name):
    def fn(mem, gm, w, gk):
        return _kv_fn(mem, gm, w, gk)
    M = mem.shape[0]
    return _rowwise(name, fn, [('c', mem), ('c', gm), ('c', w), ('c', gk)],
                    [('c', (M, XQ), F32), ('c', (M, XQ), F32)], 1)


def _kv_prep_bwd(mem, gm, w, gk, dk, dv, name):
    def fn(mem, gm, w, gk, dk, dv):
        _, vjp = jax.vjp(lambda a, b, c: _kv_fn(mem, a, b, c), gm, w, gk)
        return vjp((dk, dv))
    return _rowwise(name, fn, [('c', mem), ('c', gm), ('c', w), ('c', gk), ('c', dk), ('c', dv)],
                    [('c', gm.shape, F32), ('c', w.shape, BF16), ('c', gk.shape, F32)], 1)


def _forward_merge(x, mix, mix_kind, xq, gate, k, v, gq, wout, name, nblk, sub):
    def fn(x, mix, xq, gate, k, v, gq, wout):
        o = _merge(mix, xq, gate, k, v, gq)
        return (x + _dot(o.astype(BF16), wout),)
    L = x.shape[0]
    return _rowwise(name, fn, [('r', x), (mix_kind, mix), ('r', xq), ('r', gate), ('c', k), ('c', v), ('c', gq),
                               ('c', wout)], [('r', (L, D_MODEL), F32)], nblk, sub)[0]


def _backward_merge(dx, mix, mix_kind, xq, gate, k, v, gq, wout, name, nblk, sub):
    def fn(dx, mix, xq, gate, k, v, gq, wout):
        g16 = dx.astype(BF16)
        do = _dot_nt(g16, wout)
        o, vjp = jax.vjp(_merge, mix, xq, gate, k, v, gq)
        dmix, dxq, dgate, dk, dv, dgq = vjp(do)
        return dmix, dxq, dgate, o, g16, dk, dv, dgq
    L = dx.shape[0]
    return _rowwise(
        name, fn,
        [('r', dx), (mix_kind, mix), ('r', xq), ('r', gate), ('c', k), ('c', v), ('c', gq), ('c', wout)],
        [('r', (L, PRIM), F32), ('r', (L, XQ), F32), ('r', (L, BRANCH), F32), ('r', (L, BRANCH), BF16),
         ('r', (L, D_MODEL), BF16), ('a', k.shape, F32), ('a', v.shape, F32), ('a', gq.shape, F32)], nblk, sub)


_MLA_IN = 3392
_MLA_IN_PAD = 3456


def _from_slots(g):
    _, k, n = g.shape
    return jnp.transpose(g, (1, 0, 2)).reshape(k, N_DEV * n)


def _to_slots(w):
    k = w.shape[0]
    return jnp.transpose(w.reshape(k, N_DEV, -1), (1, 0, 2))


def _uq_to_kernel(g):
    uq = _from_slots(g).reshape(Q_LORA, MLA_H, HD + ROPE)
    return jnp.concatenate([uq[:, :, :HD].reshape(Q_LORA, PRIM),
                            jnp.pad(uq[:, :, HD:], ((0, 0), (0, 0), (0, HD - ROPE))).reshape(Q_LORA, PRIM)], axis=1)


def _uq_from_kernel(d_w_q):
    uq = jnp.concatenate([d_w_q[:, :PRIM].reshape(Q_LORA, MLA_H, HD),
                          d_w_q[:, PRIM:].reshape(Q_LORA, MLA_H, HD)[:, :, :ROPE]], axis=2)
    return _to_slots(uq.reshape(Q_LORA, MLA_H * (HD + ROPE)))


def _mla_in_perm(w):
    return jnp.concatenate([w[:, :768], w[:, 832:], w[:, 768:832], jnp.zeros((w.shape[0], 64), w.dtype)], axis=1)


def _mla_in_unperm(w):
    return jnp.concatenate([w[:, :768], w[:, 3328:3392], w[:, 768:3328]], axis=1)


_SMALL = (("ln_gain", 2048), ("mem_norm", 2048), ("xq_norm", 256), ("xk_norm", 256), ("s5_lambda_re", 6144),
          ("s5_lambda_im", 6144), ("s5_log_step", 96), ("s5_b_re", 98304), ("s5_b_im", 98304), ("s5_c_re", 98304),
          ("s5_c_im", 98304), ("s5_d", 1536), ("mla_q_lora_norm", 512), ("mla_kv_lora_norm", 256),
          ("mla_q_nope_norm", 128), ("mla_k_nope_norm", 128), ("mla_q_rope_norm", 64), ("mla_k_rope_norm", 64))
_SMALL_ROWS = 408
_SMALL_OFF = {name: sum(n for _, n in _SMALL[:i]) for i, (name, _) in enumerate(_SMALL)}


def _pack_small(d):
    flat = jnp.concatenate([d[n].reshape(-1).astype(F32) for n, _ in _SMALL])
    return jnp.pad(flat, (0, _SMALL_ROWS * 1024 - flat.shape[0])).reshape(_SMALL_ROWS, 1024)


def _unpack_small(p, name, shape):
    off = _SMALL_OFF[name]
    return p.reshape(-1)[off:off + int(np.prod(shape))].reshape(shape)


_WEIGHTS = ('ln_gain', 'w_out', 'mem_norm', 'w_mem_kv', 'xq_norm', 'xk_norm', 's5_w_in', 's5_lambda_re',
            's5_lambda_im', 's5_log_step', 's5_b_re', 's5_b_im', 's5_c_re', 's5_c_im', 's5_d', 's5_w_glu', 'mla_w_in',
            'mla_q_lora_norm', 'mla_kv_lora_norm', 'mla_w_uq', 'mla_w_ukv', 'mla_q_nope_norm', 'mla_k_nope_norm',
            'mla_q_rope_norm', 'mla_k_rope_norm')
_BIG = ('w_out', 'w_mem_kv', 's5_w_in', 's5_w_glu', 'mla_w_in', 'mla_w_uq', 'mla_w_ukv')


def _pad128(g):
    return jnp.pad(g.reshape(1, -1), ((0, 0), (0, HD - g.shape[-1])))


def kernel(x, mem, positions, ln_gain, w_out, mem_norm, w_mem_kv, xq_norm, xk_norm, s5_w_in, s5_lambda_re, s5_lambda_im, s5_log_step, s5_b_re, s5_b_im, s5_c_re, s5_c_im, s5_d, s5_w_glu, mla_w_in, mla_q_lora_norm, mla_kv_lora_norm, mla_w_uq, mla_w_ukv, mla_q_nope_norm, mla_k_nope_norm, mla_q_rope_norm, mla_k_rope_norm, loss_target, m_ln_gain, m_w_out, m_mem_norm, m_w_mem_kv, m_xq_norm, m_xk_norm, m_s5_w_in, m_s5_lambda_re, m_s5_lambda_im, m_s5_log_step, m_s5_b_re, m_s5_b_im, m_s5_c_re, m_s5_c_im, m_s5_d, m_s5_w_glu, m_mla_w_in, m_mla_q_lora_norm, m_mla_kv_lora_norm, m_mla_w_uq, m_mla_w_ukv, m_mla_q_nope_norm, m_mla_k_nope_norm, m_mla_q_rope_norm, m_mla_k_rope_norm, v_ln_gain, v_w_out, v_mem_norm, v_w_mem_kv, v_xq_norm, v_xk_norm, v_s5_w_in, v_s5_lambda_re, v_s5_lambda_im, v_s5_log_step, v_s5_b_re, v_s5_b_im, v_s5_c_re, v_s5_c_im, v_s5_d, v_s5_w_glu, v_mla_w_in, v_mla_q_lora_norm, v_mla_kv_lora_norm, v_mla_w_uq, v_mla_w_ukv, v_mla_q_nope_norm, v_mla_k_nope_norm, v_mla_q_rope_norm, v_mla_k_rope_norm):
    weights = dict(ln_gain=ln_gain, w_out=w_out, mem_norm=mem_norm, w_mem_kv=w_mem_kv, xq_norm=xq_norm,
                   xk_norm=xk_norm, s5_w_in=s5_w_in, s5_lambda_re=s5_lambda_re, s5_lambda_im=s5_lambda_im,
                   s5_log_step=s5_log_step, s5_b_re=s5_b_re, s5_b_im=s5_b_im, s5_c_re=s5_c_re, s5_c_im=s5_c_im,
                   s5_d=s5_d, s5_w_glu=s5_w_glu, mla_w_in=mla_w_in, mla_q_lora_norm=mla_q_lora_norm,
                   mla_kv_lora_norm=mla_kv_lora_norm, mla_w_uq=mla_w_uq, mla_w_ukv=mla_w_ukv,
                   mla_q_nope_norm=mla_q_nope_norm, mla_k_nope_norm=mla_k_nope_norm,
                   mla_q_rope_norm=mla_q_rope_norm, mla_k_rope_norm=mla_k_rope_norm)
    m_in = dict(zip(_WEIGHTS, (m_ln_gain, m_w_out, m_mem_norm, m_w_mem_kv, m_xq_norm, m_xk_norm, m_s5_w_in,
                               m_s5_lambda_re, m_s5_lambda_im, m_s5_log_step, m_s5_b_re, m_s5_b_im, m_s5_c_re,
                               m_s5_c_im, m_s5_d, m_s5_w_glu, m_mla_w_in, m_mla_q_lora_norm, m_mla_kv_lora_norm,
                               m_mla_w_uq, m_mla_w_ukv, m_mla_q_nope_norm, m_mla_k_nope_norm, m_mla_q_rope_norm,
                               m_mla_k_rope_norm)))
    v_in = dict(zip(_WEIGHTS, (v_ln_gain, v_w_out, v_mem_norm, v_w_mem_kv, v_xq_norm, v_xk_norm, v_s5_w_in,
                               v_s5_lambda_re, v_s5_lambda_im, v_s5_log_step, v_s5_b_re, v_s5_b_im, v_s5_c_re,
                               v_s5_c_im, v_s5_d, v_s5_w_glu, v_mla_w_in, v_mla_q_lora_norm, v_mla_kv_lora_norm,
                               v_mla_w_uq, v_mla_w_ukv, v_mla_q_nope_norm, v_mla_k_nope_norm, v_mla_q_rope_norm,
                               v_mla_k_rope_norm)))

    x0 = x[0]
    mem0 = mem[0]
    target = loss_target[0]
    L = x0.shape[0]
    nblk, sub = 8, 1
    me = 4 * lax.axis_index("x") + 2 * lax.axis_index("y") + lax.axis_index("c")

    lora = jnp.pad(jnp.concatenate([mla_q_lora_norm, mla_kv_lora_norm], axis=1), ((0, 7), (0, HD - 96)))
    G_out0, G_mkv0, W_in_s5, W_glu = _exchange_call(
        _plan_all_gather([s.astype(BF16) for s in (w_out[0], w_mem_kv[0], s5_w_in[0], s5_w_glu[0])]), "ag_weights0")
    gather1 = _plan_all_gather(
        [s.astype(BF16) for s in (w_out[1], w_mem_kv[1], mla_w_in[0], mla_w_uq[0], mla_w_ukv[0])] + [lora])

    ln0, ln1 = ln_gain[0:1], ln_gain[1:2]
    gq0, gq1 = xq_norm[0:1], xq_norm[1:2]
    gk0, gk1 = xk_norm[0:1], xk_norm[1:2]
    gm0, gm1 = mem_norm[0:1], mem_norm[1:2]
    gqn, gkn = mla_q_nope_norm, mla_k_nope_norm
    gqr, gkr = _pad128(mla_q_rope_norm), _pad128(mla_k_rope_norm)

    lr3 = s5_lambda_re.reshape(S5_G, 1, S5_P)
    li3 = s5_lambda_im.reshape(S5_G, 1, S5_P)
    ls3 = s5_log_step.reshape(S5_G, 1, 1)
    btr = jnp.swapaxes(s5_b_re[0], 1, 2)
    bti = jnp.swapaxes(s5_b_im[0], 1, 2)
    a_r, a_i, bbr, bbi = _s5_params(lr3, li3, ls3, btr, bti)
    bm, bmt, cm, cmt = _s5_mats(bbr, bbi, s5_c_re[0], s5_c_im[0])
    a_r2 = a_r.reshape(1, S5_G * S5_P)
    a_i2 = a_i.reshape(1, S5_G * S5_P)
    cmask, rmat = _s5_compact_consts()

    half = ROPE // 2
    inv_freq = ROPE_THETA ** (-jnp.arange(half, dtype=F32) / half)
    invf = jnp.concatenate([inv_freq, inv_freq, jnp.zeros((HD - ROPE,), F32)]).reshape(1, HD)

    def rot_tables(pos, invf):
        ang = pos.astype(F32) * invf
        lane = lax.broadcasted_iota(jnp.int32, ang.shape, 1)
        c = jnp.where(lane < ROPE, jnp.cos(ang), 0.0)
        s = jnp.sin(ang)
        return c, jnp.where(lane < half, -s, 0.0), jnp.where((lane >= half) & (lane < ROPE), s, 0.0)

    tc, ts1, ts2 = _rowwise("rot_tables", rot_tables, [('r', positions.reshape(L, 1)), ('c', invf)],
                            [('r', (L, HD), F32)] * 3, nblk, sub)

    def in_s5(x, g, w):
        proj = _mm_slots(_rms(x, g, D_MODEL).astype(BF16), w)
        return proj[:, :PRIM], proj[:, PRIM:PRIM + XQ], proj[:, PRIM + XQ:]

    u_s5, xq_a, gate_a = _rowwise("s5_in", in_s5, [('r', x0), ('c', ln0), ('c', W_in_s5)],
                                  [('r', (L, PRIM), F32), ('r', (L, XQ), F32), ('r', (L, BRANCH), F32)], nblk, sub)
    (y_s5,), (G_out1, G_mkv1, G_in_mla, G_uq, W_kv, G_lora) = _s5_fwd(u_s5, bm, cm, a_r2, a_i2, s5_d, host=gather1)
    W_out = (G_out0.reshape(BRANCH, D_MODEL), G_out1.reshape(BRANCH, D_MODEL))
    W_mkv = (G_mkv0.reshape(D_MODEL, 2 * XQ), G_mkv1.reshape(D_MODEL, 2 * XQ))
    W_in_mla = _mla_in_perm(_from_slots(G_in_mla))
    W_q = _uq_to_kernel(G_uq)
    g_qlora = G_lora[:, 0, :64].reshape(1, Q_LORA)
    g_kvlora = G_lora[:, 0, 64:96].reshape(1, KV_LORA)

    def glu(y, w):
        z = _mm_slots(_gelu(y).astype(BF16), w)
        return (z[:, :PRIM] * _sigmoid(z[:, PRIM:]),)

    y2 = _rowwise("s5_glu", glu, [('r', y_s5), ('c', W_glu)], [('r', (L, PRIM), F32)], nblk, sub)[0]
    k_a, v_a = _kv_prep(mem0, gm0, W_mkv[0], gk0, "kv_prep0")
    x1 = _forward_merge(x0, y2, 'r', xq_a, gate_a, k_a, v_a, gq0, W_out[0], "merge0", nblk, sub)

    def in_mla(x, g, w):
        proj = _dot(_rms(x, g, D_MODEL).astype(BF16), w)
        return proj[:, :512], proj[:, 512:768], proj[:, 768:1280], proj[:, 1280:3328], proj[:, 3328:]

    c_q, c_kv, xq_b, gate_b, krp = _rowwise(
        "mla_in", in_mla, [('r', x1), ('c', ln1), ('c', W_in_mla)],
        [('r', (L, Q_LORA), F32), ('r', (L, KV_LORA), F32), ('r', (L, XQ), F32), ('r', (L, BRANCH), F32),
         ('r', (L, HD), F32)], nblk, sub)

    def qkv(c_q, c_kv, krp, tc, ts1, ts2, gql, gkvl, wq, wkv, gqn, gkn, gqr, gkr):
        q = _dot(_rms(c_q, gql, Q_LORA).astype(BF16), wq)
        kv = _mm_slots(_rms(c_kv, gkvl, KV_LORA).astype(BF16), wkv)
        kp, v = _kv_post(kv, krp, gkn, gkr, tc, ts1, ts2)
        return _q_post(q, gqn, gqr, tc, ts1, ts2), kp, v

    qkv_consts = [('c', g_qlora), ('c', g_kvlora), ('c', W_q), ('c', W_kv), ('c', gqn), ('c', gkn), ('c', gqr),
                  ('c', gkr)]
    q_pad, k_pad, v_h = _rowwise(
        "mla_qkv", qkv, [('r', c_q), ('r', c_kv), ('r', krp), ('r', tc), ('r', ts1), ('r', ts2)] + qkv_consts,
        [('r', (L, 2 * PRIM), BF16), ('r', (L, 2 * PRIM), BF16), ('r', (L, PRIM), BF16)], nblk, sub)
    scale = (HD + ROPE) ** -0.5
    attn, lse = _attn_fwd(q_pad, k_pad, v_h, scale)
    k_b, v_b = _kv_prep(mem0, gm1, W_mkv[1], gk1, "kv_prep1")
    x2 = _forward_merge(x1, attn, 'r', xq_b, gate_b, k_b, v_b, gq1, W_out[1], "merge1", nblk, sub)

    def loss_fn(y, t):
        err = y - t
        part = 0.5 * jnp.sum(jnp.sum(err * err, axis=-1, keepdims=True) * (1.0 / D_MODEL), axis=0, keepdims=True)
        return err * (1.0 / D_MODEL), jnp.broadcast_to(part, (1, HD))

    dx2, loss_part = _rowwise("loss", loss_fn, [('r', x2), ('r', target)],
                              [('r', (L, D_MODEL), F32), ('a', (1, HD), F32)], nblk, sub)
    loss = lax.psum(loss_part[0, 0], ("x", "y", "c"))

    dattn, dxq_b, dgate_b, o_b, g_b, dk_b, dv_b, dgq1 = _backward_merge(
        dx2, attn, 'r', xq_b, gate_b, k_b, v_b, gq1, W_out[1], "merge1_bwd", nblk, sub)
    dgm1, dW_mkv1, dgk1 = _kv_prep_bwd(mem0, gm1, W_mkv[1], gk1, dk_b, dv_b, "kv_prep1_bwd")
    dW_out1 = _matmul_tn(o_b, g_b, "dw_out1")
    dq_pad, dk_pad, dv_h = _attn_bwd(q_pad, k_pad, v_h, attn, lse, dattn, scale)

    def qkv_bwd(c_q, c_kv, krp, tc, ts1, ts2, dqp, dkp, dv, gql, gkvl, wq, wkv, gqn, gkn, gqr, gkr):
        cqn, vjp_qn = jax.vjp(lambda a, b: _rms(a, b, Q_LORA), c_q, gql)
        ckvn, vjp_kvn = jax.vjp(lambda a, b: _rms(a, b, KV_LORA), c_kv, gkvl)
        cqn16 = cqn.astype(BF16)
        ckvn16 = ckvn.astype(BF16)
        q = _dot(cqn16, wq)
        kv = _mm_slots(ckvn16, wkv)
        _, vjp_q = jax.vjp(lambda a, b, c: _q_post(a, b, c, tc, ts1, ts2), q, gqn, gqr)
        dq, dgqn, dgqr = vjp_q(dqp)
        _, vjp_kv = jax.vjp(lambda a, b, c, d: _kv_post(a, b, c, d, tc, ts1, ts2), kv, krp, gkn, gkr)
        dkv, dkrp, dgkn, dgkr = vjp_kv((dkp, dv))
        dq16 = dq.astype(BF16)
        dkv16 = dkv.astype(BF16)
        dc_q, dgql = vjp_qn(_dot_nt(dq16, wq))
        dc_kv, dgkvl = vjp_kvn(_mm_slots_nt(dkv16, wkv))
        return dc_q, dc_kv, dkrp, cqn16, dq16, ckvn16, dkv16, dgql, dgkvl, dgqn, dgkn, dgqr, dgkr

    (dc_q, dc_kv, dkrp, cqn16, dq16, ckvn16, dkv16, dgql, dgkvl, dgqn, dgkn, dgqr, dgkr) = _rowwise(
        "mla_qkv_bwd", qkv_bwd,
        [('r', c_q), ('r', c_kv), ('r', krp), ('r', tc), ('r', ts1), ('r', ts2), ('r', dq_pad), ('r', dk_pad),
         ('r', dv_h)] + qkv_consts,
        [('r', (L, Q_LORA), F32), ('r', (L, KV_LORA), F32), ('r', (L, HD), F32), ('r', (L, Q_LORA), BF16),
         ('r', (L, 2 * PRIM), BF16), ('r', (L, KV_LORA), BF16), ('r', (L, 2 * PRIM), BF16),
         ('a', (1, Q_LORA), F32), ('a', (1, KV_LORA), F32), ('a', (1, HD), F32), ('a', (1, HD), F32),
         ('a', (1, HD), F32), ('a', (1, HD), F32)], nblk, sub)
    dW_q = _matmul_tn(cqn16, dq16, "dw_uq")
    dW_kv = _matmul_tn_slots(ckvn16, dkv16, "dw_ukv")

    def in_bwd(x, dres, g, w, *dparts):
        dproj = jnp.concatenate(dparts, axis=-1).astype(BF16)
        xn, vjp = jax.vjp(lambda a, b: _rms(a, b, D_MODEL), x, g)
        dx, dg = vjp(_mm_slots_nt(dproj, w) if w.ndim == 3 else _dot_nt(dproj, w))
        return dx + dres, xn, dproj, dg

    dx1, xn1, dproj1, dln1 = _rowwise(
        "mla_in_bwd", in_bwd,
        [('r', x1), ('r', dx2), ('c', ln1), ('c', W_in_mla), ('r', dc_q), ('r', dc_kv), ('r', dxq_b), ('r', dgate_b),
         ('r', dkrp)],
        [('r', (L, D_MODEL), F32), ('r', (L, D_MODEL), BF16), ('r', (L, _MLA_IN_PAD), BF16), ('a', (1, D_MODEL), F32)],
        nblk, sub)
    dW_in_mla = _matmul_tn(xn1, dproj1, "dw_mla_in")

    dy2, dxq_a, dgate_a, o_a, g_a, dk_a, dv_a, dgq0 = _backward_merge(
        dx1, y2, 'r', xq_a, gate_a, k_a, v_a, gq0, W_out[0], "merge0_bwd", nblk, sub)
    dgm0, dW_mkv0, dgk0 = _kv_prep_bwd(mem0, gm0, W_mkv[0], gk0, dk_a, dv_a, "kv_prep0_bwd")
    dW_out0 = _matmul_tn(o_a, g_a, "dw_out0")

    def glu_bwd(y, dy2, w):
        h, vjp_h = jax.vjp(_gelu, y)
        h16 = h.astype(BF16)
        z = _mm_slots(h16, w)
        _, vjp_z = jax.vjp(lambda z: z[:, :PRIM] * _sigmoid(z[:, PRIM:]), z)
        dz16 = vjp_z(dy2)[0].astype(BF16)
        return vjp_h(_mm_slots_nt(dz16, w))[0], h16, dz16

    dy_s5, h16, dz16 = _rowwise("s5_glu_bwd", glu_bwd, [('r', y_s5), ('r', dy2), ('c', W_glu)],
                                [('r', (L, PRIM), F32), ('r', (L, PRIM), BF16), ('r', (L, 2 * PRIM), BF16)],
                                nblk, sub)
    dW_glu = _matmul_tn_slots(h16, dz16, "dw_glu")
    early = [dW_out1.reshape(N_DEV, 256, D_MODEL), dW_mkv1.reshape(N_DEV, 128, 2 * XQ),
             _to_slots(_mla_in_unperm(dW_in_mla)), _uq_from_kernel(dW_q), dW_kv,
             dW_out0.reshape(N_DEV, 256, D_MODEL), dW_mkv0.reshape(N_DEV, 128, 2 * XQ), dW_glu]
    early_t = _pair_add(early, _exchange_call(_plan_pair(early), "rs_pair_early"), "rs_add_early")
    (du_s5, dbc, dcc, dd, dar, dai), early_recv = _s5_bwd(u_s5, dy_s5, bm, bmt, cmt, a_r2, a_i2, s5_d, cmask, rmat,
                                                          host=_plan_chips(early_t))
    dx0, xn0, dproj0, dln0 = _rowwise(
        "s5_in_bwd", in_bwd,
        [('r', x0), ('r', dx1), ('c', ln0), ('c', W_in_s5), ('r', du_s5), ('r', dxq_a),
         ('r', dgate_a)],
        [('r', (L, D_MODEL), F32), ('r', (L, D_MODEL), BF16), ('r', (L, 2 * BRANCH), BF16), ('a', (1, D_MODEL), F32)],
        nblk, sub)
    dW_in_s5 = _matmul_tn_slots(xn0, dproj0, "dw_s5_in")

    dbc4 = dbc.reshape(S5_G, S5_C, 2, S5_P)
    dcc4 = dcc.reshape(S5_G, S5_C, 2, S5_P)
    dlr, dli, dls, dbtr, dbti = _s5_params_bwd(
        lr3, li3, ls3, btr, bti, dar.reshape(S5_G, 1, S5_P), dai.reshape(S5_G, 1, S5_P), dbc4[:, :, 0], dbc4[:, :, 1])

    late = [dW_in_s5]
    late_t = _pair_add(late, _exchange_call(_plan_pair(late), "rs_pair_late"), "rs_add_late")
    recvs = list(early_recv) + list(_exchange_call(_plan_chips(late_t), "rs_chips_late"))
    owners = [("w_out", 1), ("w_mem_kv", 1), ("mla_w_in", 0), ("mla_w_uq", 0), ("mla_w_ukv", 0), ("w_out", 0),
              ("w_mem_kv", 0), ("s5_w_glu", 0), ("s5_w_in", 0)]
    upd = [_sum_adamw(r, weights[n][i], m_in[n][i], v_in[n][i], "update_%s%d" % (n, i))
           for r, (n, i) in zip(recvs, owners)]
    grads, delta, new_m, new_v = {}, {}, {}, {}
    for n in _BIG:
        parts = [u for u, (o, _) in sorted(zip(upd, owners), key=lambda t: t[1][1]) if o == n]
        grads[n], delta[n], new_m[n], new_v[n] = (jnp.stack([p[j] for p in parts]) for j in range(4))

    small_part = {
        "ln_gain": jnp.concatenate([dln0, dln1]), "mem_norm": jnp.concatenate([dgm0, dgm1]),
        "xq_norm": jnp.concatenate([dgq0, dgq1]), "xk_norm": jnp.concatenate([dgk0, dgk1]),
        "s5_lambda_re": dlr, "s5_lambda_im": dli, "s5_log_step": dls,
        "s5_b_re": jnp.swapaxes(dbtr, 1, 2), "s5_b_im": jnp.swapaxes(dbti, 1, 2),
        "s5_c_re": dcc4[:, :, 0], "s5_c_im": -dcc4[:, :, 1], "s5_d": dd,
        "mla_q_lora_norm": dgql, "mla_kv_lora_norm": dgkvl, "mla_q_nope_norm": dgqn, "mla_k_nope_norm": dgkn,
        "mla_q_rope_norm": dgqr[:, :ROPE], "mla_k_rope_norm": dgkr[:, :ROPE],
    }
    small_gath = _exchange_call(_plan_all_gather([_pack_small(small_part)]), "ag_small_grads")[0]

    def whole(name, a):
        if name == "mla_q_lora_norm":
            return lax.dynamic_update_slice(jnp.zeros((Q_LORA,), F32), a.reshape(-1), (me * 64,))
        if name == "mla_kv_lora_norm":
            return lax.dynamic_update_slice(jnp.zeros((KV_LORA,), F32), a.reshape(-1), (me * 32,))
        return a

    wp = _pack_small({n: whole(n, weights[n]) for n, _ in _SMALL})
    mp = _pack_small({n: whole(n, m_in[n]) for n, _ in _SMALL})
    vp = _pack_small({n: whole(n, v_in[n]) for n, _ in _SMALL})
    gs, ds, ms, vs = _small_update(small_gath, wp, mp, vp, "small_update")

    for n, _ in _SMALL:
        shape = weights[n].shape
        if n == "mla_q_lora_norm":
            take = lambda p: lax.dynamic_slice(_unpack_small(p, n, (Q_LORA,)), (me * 64,), (64,)).reshape(shape)
        elif n == "mla_kv_lora_norm":
            take = lambda p: lax.dynamic_slice(_unpack_small(p, n, (KV_LORA,)), (me * 32,), (32,)).reshape(shape)
        else:
            take = lambda p: _unpack_small(p, n, shape)
        grads[n], delta[n], new_m[n], new_v[n] = take(gs), take(ds), take(ms), take(vs)
    return (loss, dx0[None], *[grads[n] for n in _WEIGHTS], *[delta[n] for n in _WEIGHTS],
            *[new_m[n] for n in _WEIGHTS], *[new_v[n] for n in _WEIGHTS])
```

```python
import functools
import math

import numpy as np
import jax
import jax.numpy as jnp
from jax import lax
from jax.experimental import pallas as pl
from jax.experimental.pallas import tpu as pltpu

F32 = jnp.float32
BF16 = jnp.bfloat16
EPS = 1e-6
NEG = float(np.finfo(np.float32).min)
MESH = pl.DeviceIdType.MESH

N_DEV = 8
D_MODEL = 1024
MEM_LEN = 256
XQ = 512
PRIM = 1536
BRANCH = 2048
X_HEADS = 4
HD = 128
S5_G = 96
S5_P = 64
S5_C = 16
S5_GB = 8
S5_W = S5_GB * S5_P
MLA_H = 12
ROPE = 64
Q_LORA = 512
KV_LORA = 256
ROPE_THETA = 10000.0

ADAM_LR = 0.001
ADAM_B1 = 0.9
ADAM_B2 = 0.999
ADAM_EPS = 1e-08
ADAM_WD = 0.01
ADAM_STEP = 10

VMEM_LIMIT = 56 * 1024 * 1024


def _dot(a, b):
    return jnp.dot(a, b, preferred_element_type=F32)


def _dot_nt(a, b):
    return lax.dot_general(a, b, (((1,), (1,)), ((), ())), preferred_element_type=F32)


def _dot_tn(a, b):
    return lax.dot_general(a, b, (((0,), (0,)), ((), ())), preferred_element_type=F32)


@jax.custom_vjp
def _mm(a, b):
    return _dot(a.astype(BF16), b.astype(BF16))


def _mm_fwd(a, b):
    return _mm(a, b), (a, b)


def _mm_bwd(res, g):
    a, b = res
    gb = g.astype(BF16)
    return _dot_nt(gb, b.astype(BF16)).astype(a.dtype), _dot_tn(a.astype(BF16), gb).astype(b.dtype)


_mm.defvjp(_mm_fwd, _mm_bwd)


@jax.custom_vjp
def _mm_nt(a, b):
    return _dot_nt(a.astype(BF16), b.astype(BF16))


def _mm_nt_fwd(a, b):
    return _mm_nt(a, b), (a, b)


def _mm_nt_bwd(res, g):
    a, b = res
    gb = g.astype(BF16)
    return _dot(gb, b.astype(BF16)).astype(a.dtype), _dot_tn(gb, a.astype(BF16)).astype(b.dtype)


_mm_nt.defvjp(_mm_nt_fwd, _mm_nt_bwd)


@jax.custom_vjp
def _softmax(s):
    m = jnp.max(s, axis=-1, keepdims=True)
    e = jnp.exp(s - m)
    return e / jnp.sum(e, axis=-1, keepdims=True)


def _softmax_fwd(s):
    p = _softmax(s)
    return p, p


def _softmax_bwd(p, g):
    return (p * (g - jnp.sum(p * g, axis=-1, keepdims=True)),)


_softmax.defvjp(_softmax_fwd, _softmax_bwd)


def _rms(x, g, n):
    ms = jnp.sum(x * x, axis=-1, keepdims=True) * (1.0 / n)
    return x * lax.rsqrt(ms + EPS) * g


def _sigmoid(x):
    return 1.0 / (1.0 + jnp.exp(-x))


def _silu(x):
    return x * _sigmoid(x)


def _gelu(x):
    c = math.sqrt(2.0 / math.pi)
    return 0.5 * x * (1.0 + jnp.tanh(c * (x + 0.044715 * (x * x * x))))


@jax.custom_vjp
def _rot(x, c, s1, s2):
    return x * c + pltpu.roll(x, 96, 1) * s1 + pltpu.roll(x, 32, 1) * s2


def _rot_fwd(x, c, s1, s2):
    return _rot(x, c, s1, s2), (c, s1, s2)


def _rot_bwd(res, g):
    c, s1, s2 = res
    dx = g * c + pltpu.roll(g * s1, 32, 1) + pltpu.roll(g * s2, 96, 1)
    return dx, jnp.zeros_like(c), jnp.zeros_like(s1), jnp.zeros_like(s2)


_rot.defvjp(_rot_fwd, _rot_bwd)


def _mem_attn(xq, k, v, gq):
    outs = []
    for h in range(X_HEADS):
        sl = slice(HD * h, HD * (h + 1))
        q = _rms(xq[:, sl], gq, HD)
        p = _softmax(_mm_nt(q, k[:, sl]) * (HD ** -0.5))
        outs.append(_mm(p, v[:, sl]))
    return jnp.concatenate(outs, axis=-1)


def _merge(mix, xq, gate, k, v, gq):
    return jnp.concatenate([mix, _mem_attn(xq, k, v, gq)], axis=-1) * _silu(gate)


def _q_post(q, gqn, gqr, c, s1, s2):
    pieces = []
    for h in range(MLA_H):
        pieces.append(_rms(q[:, HD * h:HD * (h + 1)], gqn, HD))
        pieces.append(_rot(_rms(q[:, PRIM + HD * h:PRIM + HD * (h + 1)], gqr, ROPE), c, s1, s2))
    return jnp.concatenate(pieces, axis=-1)


def _kv_post(kv, krp, gkn, gkr, c, s1, s2):
    kr = _rot(_rms(krp, gkr, ROPE), c, s1, s2)
    pieces, vals = [], []
    for h in range(MLA_H):
        pieces.append(_rms(kv[:, 2 * HD * h:2 * HD * h + HD], gkn, HD))
        pieces.append(kr)
        vals.append(kv[:, 2 * HD * h + HD:2 * HD * (h + 1)])
    return jnp.concatenate(pieces, axis=-1), jnp.concatenate(vals, axis=-1)


def _rowwise(name, fn, ins, outs, nblk, sub=1, host=None):
    n_in = len(ins)

    def spec(kind, shape):
        if kind == 'r':
            return pl.BlockSpec((shape[0] // nblk, shape[1]), lambda i: (i, 0))
        zeros = (0,) * len(shape)
        return pl.BlockSpec(tuple(shape), lambda i: zeros)

    def body(*refs):
        i = pl.program_id(0)
        res = fn(*[r[...] for r in refs[:n_in]])
        for (kind, _, _), ref, val in zip(outs, refs[n_in:], res):
            if kind == 'a':
                @pl.when(i == 0)
                def _():
                    ref[...] = jnp.zeros_like(ref)
                ref[...] += val.astype(ref.dtype)
            else:
                ref[...] = val.astype(ref.dtype)

    res, hosted = _hosting_call(
        body, name, nblk, host, [a for _, a in ins], [spec(k, a.shape) for k, a in ins],
        [jax.ShapeDtypeStruct(tuple(s), d) for _, s, d in outs], [spec(k, s) for k, s, _ in outs], [])
    return res if host is None else (res, hosted)


def _matmul_tn(a, g, name, out_dtype=BF16):
    L, K = a.shape
    N = g.shape[1]
    tn = next(t for t in (512, 384, 256, 128) if N % t == 0)
    tl = min(512, L)
    nl = L // tl

    def body(a_ref, g_ref, o_ref, acc):
        l = pl.program_id(1)

        @pl.when(l == 0)
        def _():
            acc[...] = jnp.zeros_like(acc)

        acc[...] += _dot_tn(a_ref[...], g_ref[...])

        @pl.when(l == nl - 1)
        def _():
            o_ref[...] = acc[...].astype(o_ref.dtype)

    return pl.pallas_call(
        body, name=name, grid=(N // tn, nl),
        in_specs=[pl.BlockSpec((tl, K), lambda n, l: (l, 0)), pl.BlockSpec((tl, tn), lambda n, l: (l, n))],
        out_specs=pl.BlockSpec((K, tn), lambda n, l: (0, n)),
        out_shape=jax.ShapeDtypeStruct((K, N), out_dtype),
        scratch_shapes=[pltpu.VMEM((K, tn), F32)],
        compiler_params=pltpu.CompilerParams(dimension_semantics=("arbitrary", "arbitrary"),
                                             vmem_limit_bytes=VMEM_LIMIT),
    )(a, g)


def _matmul_tn_slots(a, g, name, host=None):
    L, K = a.shape
    n = g.shape[1] // N_DEV
    tl = min(512, L)
    nl = L // tl

    def body(a_ref, g_ref, o_ref, acc):
        l = pl.program_id(1)

        @pl.when(l == 0)
        def _():
            acc[...] = jnp.zeros_like(acc)

        acc[...] += _dot_tn(a_ref[...], g_ref[...])

        @pl.when(l == nl - 1)
        def _():
            o_ref[...] = acc[...].astype(o_ref.dtype)

    res, hosted = _hosting_call(
        body, name, (N_DEV, nl), host, [a, g],
        [pl.BlockSpec((tl, K), lambda d, l: (l, 0)), pl.BlockSpec((tl, n), lambda d, l: (l, d))],
        [jax.ShapeDtypeStruct((N_DEV, K, n), BF16)], [pl.BlockSpec((None, K, n), lambda d, l: (d, 0, 0))],
        [pltpu.VMEM((K, n), F32)])
    return res[0] if host is None else (res[0], hosted)


def _mm_slots(a16, w):
    return jnp.concatenate([_dot(a16, w[d]) for d in range(N_DEV)], axis=-1)


def _mm_slots_nt(g16, w):
    n = w.shape[2]
    out = _dot_nt(g16[:, 0:n], w[0])
    for d in range(1, N_DEV):
        out = out + _dot_nt(g16[:, d * n:(d + 1) * n], w[d])
    return out


class _Exchange:
    def __init__(self, ins, outs, scratch, start, finish):
        self.ins, self.outs, self.scratch, self.start, self.finish = ins, outs, scratch, start, finish


def _xyc():
    return lax.axis_index("x"), lax.axis_index("y"), lax.axis_index("c")


def _plan_all_gather(xs):
    n = len(xs)

    def build(x_refs, out_refs, sems):
        send_sems, recv_sems, local_sems = sems
        x, y, c = _xyc()

        def copies(k, block, to, own=False):
            slot = 4 * block[0] + 2 * block[1] + block[2]
            return [pltpu.make_async_remote_copy(
                src_ref=x_refs[a] if own else out_refs[a].at[slot], dst_ref=out_refs[a].at[slot],
                send_sem=send_sems.at[k * n + a], recv_sem=recv_sems.at[k * n + a], device_id=to,
                device_id_type=MESH) for a in range(n)]

        mine = [pltpu.make_async_copy(x_refs[a], out_refs[a].at[4 * x + 2 * y + c], local_sems.at[a])
                for a in range(n)]
        return copies, mine, (x, y, c), [(1 - x, y), (x, 1 - y), (1 - x, 1 - y)]

    def first_copies(copies, me, chips):
        x, y, c = me
        first = copies(0, me, (x, y, 1 - c), own=True)
        for j, chip in enumerate(chips):
            first += copies(1 + j, me, (*chip, c), own=True)
        return first

    def start(x_refs, out_refs, sems):
        copies, mine, me, chips = build(x_refs, out_refs, sems)
        for cp in mine + first_copies(copies, me, chips):
            cp.start()

    def finish(x_refs, out_refs, sems):
        copies, mine, me, chips = build(x_refs, out_refs, sems)
        x, y, c = me
        passed = []
        for j, chip in enumerate(chips):
            for cp in copies(1 + j, (*chip, c), me):
                cp.wait_recv()
            fwd = copies(4 + j, (*chip, c), (x, y, 1 - c))
            for cp in fwd:
                cp.start()
            passed += fwd
        for cp in copies(0, (x, y, 1 - c), me):
            cp.wait_recv()
        for j, chip in enumerate(chips):
            for cp in copies(4 + j, (*chip, 1 - c), me):
                cp.wait_recv()
        for cp in first_copies(copies, me, chips) + passed:
            cp.wait_send()
        for cp in mine:
            cp.wait()

    return _Exchange(list(xs), [jax.ShapeDtypeStruct((N_DEV,) + a.shape, a.dtype) for a in xs],
                     [pltpu.SemaphoreType.DMA((7 * n,)), pltpu.SemaphoreType.DMA((7 * n,)),
                      pltpu.SemaphoreType.DMA((n,))], start, finish)


_CHIPS = ((0, 0), (0, 1), (1, 0), (1, 1))


def _plan_pair(sends):
    n = len(sends)

    def build(s_refs, o_refs, sems):
        send_sems, recv_sems = sems
        x, y, c = _xyc()
        return [pltpu.make_async_remote_copy(
            src_ref=s_refs[a].at[4 * px + 2 * py + 1 - c], dst_ref=o_refs[a].at[j],
            send_sem=send_sems.at[j * n + a], recv_sem=recv_sems.at[j * n + a], device_id=(x, y, 1 - c),
            device_id_type=MESH) for j, (px, py) in enumerate(_CHIPS) for a in range(n)]

    def start(s_refs, o_refs, sems):
        for cp in build(s_refs, o_refs, sems):
            cp.start()

    def finish(s_refs, o_refs, sems):
        for cp in build(s_refs, o_refs, sems):
            cp.wait_recv()
            cp.wait_send()

    return _Exchange(list(sends), [jax.ShapeDtypeStruct((4,) + a.shape[1:], a.dtype) for a in sends],
                     [pltpu.SemaphoreType.DMA((4 * n,)), pltpu.SemaphoreType.DMA((4 * n,))], start, finish)


def _plan_chips(ts):
    n = len(ts)
    flips = ((1, 0), (0, 1), (1, 1))

    def build(t_refs, o_refs, sems):
        send_sems, recv_sems, local_sems = sems
        x, y, c = _xyc()
        mine = 2 * x + y
        local = [pltpu.make_async_copy(t_refs[a].at[mine], o_refs[a].at[mine], local_sems.at[a]) for a in range(n)]
        remote = []
        for k, (fx, fy) in enumerate(flips):
            px = 1 - x if fx else x
            py = 1 - y if fy else y
            remote += [pltpu.make_async_remote_copy(
                src_ref=t_refs[a].at[2 * px + py], dst_ref=o_refs[a].at[mine],
                send_sem=send_sems.at[k * n + a], recv_sem=recv_sems.at[k * n + a], device_id=(px, py, c),
                device_id_type=MESH) for a in range(n)]
        return local, remote

    def start(t_refs, o_refs, sems):
        local, remote = build(t_refs, o_refs, sems)
        for cp in local + remote:
            cp.start()

    def finish(t_refs, o_refs, sems):
        local, remote = build(t_refs, o_refs, sems)
        for cp in remote:
            cp.wait_recv()
        for cp in remote:
            cp.wait_send()
        for cp in local:
            cp.wait()

    return _Exchange(list(ts), [jax.ShapeDtypeStruct(a.shape, a.dtype) for a in ts],
                     [pltpu.SemaphoreType.DMA((3 * n,)), pltpu.SemaphoreType.DMA((3 * n,)),
                      pltpu.SemaphoreType.DMA((n,))], start, finish)


def _exchange_call(plan, name):
    n = len(plan.ins)

    def body(*refs):
        ins, outs, sems = refs[:n], refs[n:2 * n], refs[2 * n:]
        plan.start(ins, outs, sems)
        plan.finish(ins, outs, sems)

    return pl.pallas_call(
        body, name=name, out_shape=plan.outs,
        in_specs=[pl.BlockSpec(memory_space=pl.ANY)] * n, out_specs=[pl.BlockSpec(memory_space=pl.ANY)] * n,
        scratch_shapes=plan.scratch,
    )(*plan.ins)


def _pair_add(sends, fromsib, name):
    n = len(sends)
    nb = 8

    def body(*refs):
        c = lax.axis_index("c")
        for a in range(n):
            s_ref, f_ref, t_ref = refs[a], refs[n + a], refs[2 * n + a]
            for j in range(4):
                t_ref[j] = (s_ref[2 * j + c].astype(F32) + f_ref[j].astype(F32)).astype(t_ref.dtype)

    def spec(a, lead):
        return pl.BlockSpec((lead, a.shape[1] // nb, a.shape[2]), lambda i: (0, i, 0))

    return pl.pallas_call(
        body, name=name, grid=(nb,),
        in_specs=[spec(a, N_DEV) for a in sends] + [spec(a, 4) for a in fromsib],
        out_specs=[spec(a, 4) for a in fromsib],
        out_shape=[jax.ShapeDtypeStruct(a.shape, a.dtype) for a in fromsib],
        compiler_params=pltpu.CompilerParams(dimension_semantics=("arbitrary",), vmem_limit_bytes=VMEM_LIMIT),
    )(*sends, *fromsib)


def _adamw_vals(w, g, m, v):
    m2 = ADAM_B1 * m + (1.0 - ADAM_B1) * g
    v2 = ADAM_B2 * v + (1.0 - ADAM_B2) * (g * g)
    m_hat = m2 / (1.0 - ADAM_B1 ** ADAM_STEP)
    v_hat = v2 / (1.0 - ADAM_B2 ** ADAM_STEP)
    delta = -ADAM_LR * (m_hat / (jnp.sqrt(v_hat) + ADAM_EPS) + ADAM_WD * w)
    return delta, m2, v2


def _sum_adamw(recv, w, m, v, name):
    R, C = w.shape
    ns = recv.shape[0]
    br = next((t for t in (256, 128, 64, 32, 16) if R % t == 0), R)

    def body(r_ref, w_ref, m_ref, v_ref, g_ref, d_ref, m2_ref, v2_ref):
        g = r_ref[0].astype(F32)
        for d in range(1, ns):
            g = g + r_ref[d].astype(F32)
        dl, m2, v2 = _adamw_vals(w_ref[...], g, m_ref[...], v_ref[...])
        g_ref[...] = g
        d_ref[...] = dl
        m2_ref[...] = m2
        v2_ref[...] = v2

    spec = pl.BlockSpec((br, C), lambda i: (i, 0))
    return pl.pallas_call(
        body, name=name, grid=(R // br,),
        in_specs=[pl.BlockSpec((ns, br, C), lambda i: (0, i, 0)), spec, spec, spec], out_specs=[spec] * 4,
        out_shape=[jax.ShapeDtypeStruct((R, C), F32)] * 4,
        compiler_params=pltpu.CompilerParams(dimension_semantics=("arbitrary",)),
    )(recv, w, m, v)


def _updates_call(recvs, ws, ms, vs, name, host=None):
    n = len(recvs)
    nb = 8

    def body(*refs):
        for a in range(n):
            r_ref, w_ref, m_ref, v_ref = refs[a], refs[n + a], refs[2 * n + a], refs[3 * n + a]
            g_ref, d_ref, m2_ref, v2_ref = refs[4 * n + 4 * a:4 * n + 4 * a + 4]
            g = r_ref[0].astype(F32)
            for d in range(1, r_ref.shape[0]):
                g = g + r_ref[d].astype(F32)
            dl, m2, v2 = _adamw_vals(w_ref[...], g, m_ref[...], v_ref[...])
            g_ref[...] = g
            d_ref[...] = dl
            m2_ref[...] = m2
            v2_ref[...] = v2

    def spec2(w):
        return pl.BlockSpec((w.shape[0] // nb, w.shape[1]), lambda i: (i, 0))

    def spec3(r):
        return pl.BlockSpec((r.shape[0], r.shape[1] // nb, r.shape[2]), lambda i: (0, i, 0))

    res, hosted = _hosting_call(
        body, name, nb, host, list(recvs) + list(ws) + list(ms) + list(vs),
        [spec3(r) for r in recvs] + [spec2(w) for w in ws] * 3,
        [jax.ShapeDtypeStruct(w.shape, F32) for w in ws for _ in range(4)],
        [spec2(w) for w in ws for _ in range(4)], [])
    return [res[4 * a:4 * a + 4] for a in range(n)], hosted


def _small_update(gath, loss_g, wp, mp, vp, name):
    _, R, C = gath.shape
    br = R // 3

    def body(g_ref, l_ref, w_ref, m_ref, v_ref, go_ref, d_ref, m2_ref, v2_ref, lo_ref):
        g = g_ref[0].astype(F32)
        lsum = l_ref[0]
        for d in range(1, N_DEV):
            g = g + g_ref[d].astype(F32)
            lsum = lsum + l_ref[d]
        dl, m2, v2 = _adamw_vals(w_ref[...], g, m_ref[...], v_ref[...])
        go_ref[...] = g
        d_ref[...] = dl
        m2_ref[...] = m2
        v2_ref[...] = v2
        lo_ref[...] = lsum

    spec = pl.BlockSpec((br, C), lambda i: (i, 0))
    return pl.pallas_call(
        body, name=name, grid=(R // br,),
        in_specs=[pl.BlockSpec((N_DEV, br, C), lambda i: (0, i, 0)),
                  pl.BlockSpec((N_DEV, 8, HD), lambda i: (0, 0, 0)), spec, spec, spec],
        out_specs=[spec] * 4 + [pl.BlockSpec((8, HD), lambda i: (0, 0))],
        out_shape=[jax.ShapeDtypeStruct((R, C), F32)] * 4 + [jax.ShapeDtypeStruct((8, HD), F32)],
        compiler_params=pltpu.CompilerParams(dimension_semantics=("arbitrary",)),
    )(gath, loss_g, wp, mp, vp)


def _s5_param_fn(lr, li, ls, btr, bti):
    step = jnp.exp(ls)
    er = jnp.exp(lr * step)
    ang = li * step
    ar = er * jnp.cos(ang)
    ai = er * jnp.sin(ang)
    nr = ar - 1.0
    den = lr * lr + li * li
    fr = (nr * lr + ai * li) / den
    fi = (ai * lr - nr * li) / den
    return ar, ai, fr * btr - fi * bti, fr * bti + fi * btr


def _s5_params(lr, li, ls, btr, bti):
    def body(lr_ref, li_ref, ls_ref, br_ref, bi_ref, ar_ref, ai_ref, bbr_ref, bbi_ref):
        ar, ai, bbr, bbi = _s5_param_fn(lr_ref[...], li_ref[...], ls_ref[...], br_ref[...], bi_ref[...])
        ar_ref[...] = ar
        ai_ref[...] = ai
        bbr_ref[...] = bbr
        bbi_ref[...] = bbi

    sd = jax.ShapeDtypeStruct
    return pl.pallas_call(
        body, name="s5_params",
        out_shape=[sd(lr.shape, F32), sd(lr.shape, F32), sd(btr.shape, F32), sd(btr.shape, F32)],
    )(lr, li, ls, btr, bti)


def _s5_params_bwd(lr, li, ls, btr, bti, dar, dai, dbbr, dbbi):
    def body(lr_ref, li_ref, ls_ref, br_ref, bi_ref, dar_ref, dai_ref, dbbr_ref, dbbi_ref,
             dlr_ref, dli_ref, dls_ref, dbr_ref, dbi_ref):
        _, vjp = jax.vjp(_s5_param_fn, lr_ref[...], li_ref[...], ls_ref[...], br_ref[...], bi_ref[...])
        dlr, dli, dls, dbr, dbi = vjp((dar_ref[...], dai_ref[...], dbbr_ref[...], dbbi_ref[...]))
        dlr_ref[...] = dlr
        dli_ref[...] = dli
        dls_ref[...] = dls
        dbr_ref[...] = dbr
        dbi_ref[...] = dbi

    sd = jax.ShapeDtypeStruct
    return pl.pallas_call(
        body, name="s5_params_bwd",
        out_shape=[sd(lr.shape, F32), sd(lr.shape, F32), sd(ls.shape, F32), sd(btr.shape, F32), sd(btr.shape, F32)],
    )(lr, li, ls, btr, bti, dar, dai, dbbr, dbbi)


def _cpow(ar, ai, n):
    assert n & (n - 1) == 0
    while n > 1:
        ar, ai = ar * ar - ai * ai, 2.0 * ar * ai
        n //= 2
    return ar, ai


def _scan(st, cr, ci, init, nk, reverse, store, prev=None):
    W = S5_W

    def step(j, carry):
        k = nk - 1 - j if reverse else j
        rows = pl.ds(pl.multiple_of(k * 8, 8), 8)
        sr, si = carry[0], carry[1]
        nsr = cr * sr - ci * si + st[rows, 0:W]
        nsi = cr * si + ci * sr + st[rows, W:2 * W]
        if store:
            st[rows, 0:W] = nsr
            st[rows, W:2 * W] = nsi
        if prev is None:
            return nsr, nsi
        prows = pl.ds(pl.multiple_of(jnp.maximum(k - 1, 0) * 8, 8), 8)
        w = jnp.where(k > 0, 1.0, 0.0).astype(F32)
        pr = prev[prows, 0:W] * w
        pi = prev[prows, W:2 * W] * w
        return nsr, nsi, carry[2] + nsr * pr + nsi * pi, carry[3] + nsi * pr - nsr * pi

    return lax.fori_loop(0, nk, step, init, unroll=2)


def _chain(fin, fr, fi, pr, pi, reverse):
    W = S5_W
    fin[:, 0:W] = fr
    fin[:, W:2 * W] = fi
    rowid = lax.broadcasted_iota(jnp.int32, (8, W), 0)
    cr = jnp.zeros((1, W), F32)
    ci = jnp.zeros((1, W), F32)
    init_r = jnp.zeros((8, W), F32)
    init_i = jnp.zeros((8, W), F32)
    for s in (range(7, -1, -1) if reverse else range(8)):
        init_r = jnp.where(rowid == s, cr, init_r)
        init_i = jnp.where(rowid == s, ci, init_i)
        lr = fin[s:s + 1, 0:W]
        li = fin[s:s + 1, W:2 * W]
        cr, ci = lr + pr * cr - pi * ci, li + pr * ci + pi * cr
    return init_r, init_i


def _full_scan(st, fin, ar, ai, nk, reverse, prev=None):
    W = S5_W
    cr = jnp.broadcast_to(ar, (8, W))
    ci = jnp.broadcast_to(-ai if reverse else ai, (8, W))
    z = jnp.zeros((8, W), F32)
    fr, fi = _scan(st, cr, ci, (z, z), nk, reverse, store=False)
    pr, pi = _cpow(ar, -ai if reverse else ai, nk)
    init = _chain(fin, fr, fi, pr, pi, reverse)
    if prev is None:
        return _scan(st, cr, ci, init, nk, reverse, store=True)
    return _scan(st, cr, ci, init + (z, z), nk, reverse, store=True, prev=prev)


def _s5_specs(L):
    W2 = 2 * S5_W
    GC = S5_GB * S5_C
    col = pl.BlockSpec((L, GC), lambda g: (0, g))
    vec = pl.BlockSpec((1, GC), lambda g: (0, g))
    avec = pl.BlockSpec((1, S5_W), lambda g: (0, g))
    bmat = pl.BlockSpec((None, GC, W2), lambda g: (g, 0, 0))
    cmat = pl.BlockSpec((None, W2, GC), lambda g: (g, 0, 0))
    return col, vec, avec, bmat, cmat


def _interleave(dst, src, nk):
    for s in range(8):
        dst[pl.ds(s, nk, stride=8), :] = src[s * nk:(s + 1) * nk, :]


def _deinterleave(dst, src, nk):
    for s in range(8):
        dst[s * nk:(s + 1) * nk, :] = src[pl.ds(s, nk, stride=8), :]


def _hosting_call(body, name, nsteps, host, ins, in_specs, outs, out_specs, scratch):
    grid = (nsteps,) if isinstance(nsteps, int) else tuple(nsteps)
    params = pltpu.CompilerParams(dimension_semantics=("arbitrary",) * len(grid), vmem_limit_bytes=VMEM_LIMIT)
    if host is None:
        res = pl.pallas_call(
            body, name=name, grid=grid, in_specs=in_specs, out_specs=out_specs, out_shape=outs,
            scratch_shapes=scratch, compiler_params=params,
        )(*ins)
        return list(res), []
    n_in, n_out, n_sc = len(ins), len(outs), len(scratch)
    h_in, h_out = len(host.ins), len(host.outs)

    def hosted(*refs):
        a = refs[:n_in]
        ha = refs[n_in:n_in + h_in]
        o = refs[n_in + h_in:n_in + h_in + n_out]
        ho = refs[n_in + h_in + n_out:n_in + h_in + n_out + h_out]
        sc = refs[n_in + h_in + n_out + h_out:n_in + h_in + n_out + h_out + n_sc]
        hs = refs[n_in + h_in + n_out + h_out + n_sc:]
        first = functools.reduce(jnp.logical_and, [pl.program_id(i) == 0 for i in range(len(grid))])
        last = functools.reduce(jnp.logical_and, [pl.program_id(i) == g - 1 for i, g in enumerate(grid)])

        @pl.when(first)
        def _():
            host.start(ha, ho, hs)

        body(*a, *o, *sc)

        @pl.when(last)
        def _():
            host.finish(ha, ho, hs)

    hbm = pl.BlockSpec(memory_space=pl.ANY)
    res = pl.pallas_call(
        hosted, name=name, grid=grid,
        in_specs=list(in_specs) + [hbm] * h_in, out_specs=list(out_specs) + [hbm] * h_out,
        out_shape=list(outs) + list(host.outs), scratch_shapes=list(scratch) + list(host.scratch),
        compiler_params=params,
    )(*ins, *host.ins)
    return list(res[:n_out]), list(res[n_out:])


def _s5_fwd(u, bm, cm, ar, ai, dvec, host=None):
    L = u.shape[0]
    nk = L // 8
    GC = S5_GB * S5_C
    col, vec, avec, bmat, cmat = _s5_specs(L)

    def body(u_ref, b_ref, c_ref, ar_ref, ai_ref, d_ref, y_ref, st, fin, ui, yi):
        _interleave(ui, u_ref, nk)
        for r in range(8):
            rows = slice(r * nk, (r + 1) * nk)
            st[rows, :] = _dot(ui[rows, :].astype(BF16), b_ref[...])
        _full_scan(st, fin, ar_ref[...], ai_ref[...], nk, reverse=False)
        for r in range(8):
            rows = slice(r * nk, (r + 1) * nk)
            yi[rows, :] = _dot(st[rows, :].astype(BF16), c_ref[...]) + d_ref[...] * ui[rows, :]
        _deinterleave(y_ref, yi, nk)

    return _hosting_call(
        body, "s5_fwd", S5_G // S5_GB, host,
        [u, bm, cm, ar, ai, dvec], [col, bmat, cmat, avec, avec, vec],
        [jax.ShapeDtypeStruct(u.shape, F32)], [col],
        [pltpu.VMEM((L, 2 * S5_W), F32), pltpu.VMEM((8, 2 * S5_W), F32), pltpu.VMEM((L, GC), F32),
         pltpu.VMEM((L, GC), F32)])


def _s5_bwd(u, dy, bm, bmt, cmt, ar, ai, dvec, mask, rmat, host=None):
    L = u.shape[0]
    nk = L // 8
    W = S5_W
    GC = S5_GB * S5_C
    col, vec, avec, bmat, cmat = _s5_specs(L)
    hi = lax.Precision.HIGHEST

    def body(u_ref, dy_ref, b_ref, bt_ref, ct_ref, ar_ref, ai_ref, d_ref, mask_ref, r_ref,
             du_ref, db_ref, dc_ref, dd_ref, dar_ref, dai_ref, sa, sb, fin, ui, dyi, dui):
        ar = ar_ref[...]
        ai = ai_ref[...]
        _interleave(ui, u_ref, nk)
        _interleave(dyi, dy_ref, nk)
        for r in range(8):
            rows = slice(r * nk, (r + 1) * nk)
            sa[rows, :] = _dot(ui[rows, :].astype(BF16), b_ref[...])
            sb[rows, :] = _dot(dyi[rows, :].astype(BF16), ct_ref[...])
        _full_scan(sa, fin, ar, ai, nk, reverse=False)
        gr, gi, accr, acci = _full_scan(sb, fin, ar, ai, nk, reverse=True, prev=sa)
        rowid = lax.broadcasted_iota(jnp.int32, (8, W), 0)
        last = pl.ds((nk - 1) * 8, 8)
        pr = jnp.where(rowid == 0, 0.0, pltpu.roll(sa[last, 0:W], 1, 0))
        pi = jnp.where(rowid == 0, 0.0, pltpu.roll(sa[last, W:2 * W], 1, 0))
        accr = accr + gr * pr + gi * pi
        acci = acci + gi * pr - gr * pi
        dar_ref[...] = jnp.sum(accr, axis=0, keepdims=True)
        dai_ref[...] = jnp.sum(acci, axis=0, keepdims=True)
        dbf = jnp.zeros((GC, 2 * W), F32)
        dcf = jnp.zeros((GC, 2 * W), F32)
        dd = jnp.zeros((1, GC), F32)
        for r in range(8):
            rows = slice(r * nk, (r + 1) * nk)
            ub = ui[rows, :]
            dyb = dyi[rows, :]
            gb = sb[rows, :].astype(BF16)
            dui[rows, :] = _dot(gb, bt_ref[...]) + d_ref[...] * dyb
            dbf = dbf + _dot_tn(ub.astype(BF16), gb)
            dcf = dcf + _dot_tn(dyb.astype(BF16), sa[rows, :].astype(BF16))
            dd = dd + jnp.sum(dyb * ub, axis=0, keepdims=True)
        db_ref[...] = jnp.dot(dbf * mask_ref[...], r_ref[...], precision=hi, preferred_element_type=F32)
        dc_ref[...] = jnp.dot(dcf * mask_ref[...], r_ref[...], precision=hi, preferred_element_type=F32)
        dd_ref[...] = dd
        _deinterleave(du_ref, dui, nk)

    cmp_spec = pl.BlockSpec((GC, 2 * S5_P), lambda g: (g, 0))
    whole = lambda shape: pl.BlockSpec(shape, lambda g: (0, 0))
    sd = jax.ShapeDtypeStruct
    return _hosting_call(
        body, "s5_bwd", S5_G // S5_GB, host,
        [u, dy, bm, bmt, cmt, ar, ai, dvec, mask, rmat],
        [col, col, bmat, cmat, bmat, avec, avec, vec, whole(mask.shape), whole(rmat.shape)],
        [sd(u.shape, F32), sd((S5_G * S5_C, 2 * S5_P), F32), sd((S5_G * S5_C, 2 * S5_P), F32),
         sd((1, PRIM), F32), sd((1, S5_G * S5_P), F32), sd((1, S5_G * S5_P), F32)],
        [col, cmp_spec, cmp_spec, vec, avec, avec],
        [pltpu.VMEM((L, 2 * W), F32), pltpu.VMEM((L, 2 * W), F32), pltpu.VMEM((8, 2 * W), F32),
         pltpu.VMEM((L, GC), F32), pltpu.VMEM((L, GC), F32), pltpu.VMEM((L, GC), F32)])


def _s5_mats(bbr, bbi, cre, cim):
    nb = S5_G // S5_GB
    eye = jnp.eye(S5_GB, dtype=F32)
    bb = jnp.stack([bbr, bbi], axis=2).reshape(nb, S5_GB, S5_C, 2, S5_P)
    bm = jnp.einsum('ngcrp,gh->ngcrhp', bb, eye).reshape(nb, S5_GB * S5_C, 2 * S5_W)
    cc = jnp.stack([cre, -cim], axis=2).reshape(nb, S5_GB, S5_C, 2, S5_P)
    cmt = jnp.einsum('ngcrp,gh->ngcrhp', cc, eye).reshape(nb, S5_GB * S5_C, 2 * S5_W)
    return (bm.astype(BF16), jnp.swapaxes(bm, 1, 2).astype(BF16),
            jnp.swapaxes(cmt, 1, 2).astype(BF16), cmt.astype(BF16))


def _s5_compact_consts():
    g_row = np.arange(S5_GB * S5_C) // S5_C
    col = np.arange(2 * S5_W)
    g_col = (col % S5_W) // S5_P
    mask = (g_row[:, None] == g_col[None, :]).astype(np.float32)
    tgt = (col // S5_W) * S5_P + col % S5_P
    rmat = (tgt[:, None] == np.arange(2 * S5_P)[None, :]).astype(np.float32)
    return jnp.asarray(mask), jnp.asarray(rmat)


def _attn_scores(q_ref, k_ref, qb, bq, scale):
    ext = (qb + 1) * bq
    s = _dot_nt(q_ref[qb * bq:ext, :], k_ref[0:ext, :]) * scale
    qpos = lax.broadcasted_iota(jnp.int32, (bq, bq), 0)
    kpos = lax.broadcasted_iota(jnp.int32, (bq, bq), 1)
    diag = jnp.where(kpos <= qpos, s[:, ext - bq:], NEG)
    return diag if qb == 0 else jnp.concatenate([s[:, :ext - bq], diag], axis=-1)


def _attn_fwd(qp, kp, v, scale):
    L = qp.shape[0]
    bq = min(256, L)

    def body(q_ref, k_ref, v_ref, o_ref, lse_ref):
        for qb in range(L // bq):
            rows = slice(qb * bq, (qb + 1) * bq)
            s = _attn_scores(q_ref, k_ref, qb, bq, scale)
            m = jnp.max(s, axis=-1, keepdims=True)
            e = jnp.exp(s - m)
            l = jnp.sum(e, axis=-1, keepdims=True)
            o_ref[rows, :] = _dot(e.astype(BF16), v_ref[0:(qb + 1) * bq, :]) / l
            lse_ref[rows, :] = jnp.broadcast_to(m + jnp.log(l), (bq, HD))

    blk = pl.BlockSpec((L, HD), lambda h: (0, h))
    wide = pl.BlockSpec((L, 2 * HD), lambda h: (0, h))
    return pl.pallas_call(
        body, name="mla_attn_fwd", grid=(MLA_H,),
        in_specs=[wide, wide, blk], out_specs=[blk, blk],
        out_shape=[jax.ShapeDtypeStruct((L, MLA_H * HD), F32)] * 2,
        compiler_params=pltpu.CompilerParams(dimension_semantics=("arbitrary",), vmem_limit_bytes=VMEM_LIMIT),
    )(qp, kp, v)


def _attn_bwd(qp, kp, v, o, lse, do, scale):
    L = qp.shape[0]
    bq = min(256, L)
    nq = L // bq

    def body(q_ref, k_ref, v_ref, o_ref, lse_ref, do_ref, dq_ref, dk_ref, dv_ref):
        dk_ref[...] = jnp.zeros_like(dk_ref)
        dv_ref[...] = jnp.zeros_like(dv_ref)
        for qb in range(nq):
            rows = slice(qb * bq, (qb + 1) * bq)
            ext = (qb + 1) * bq
            do = do_ref[rows, :]
            dob = do.astype(BF16)
            p = jnp.exp(_attn_scores(q_ref, k_ref, qb, bq, scale) - lse_ref[rows, 0:1])
            dp = _dot_nt(dob, v_ref[0:ext, :])
            dsum = jnp.sum(do * o_ref[rows, :], axis=-1, keepdims=True)
            ds = (p * (dp - dsum) * scale).astype(BF16)
            dq_ref[rows, :] = _dot(ds, k_ref[0:ext, :])
            dk_ref[0:ext, :] += _dot_tn(ds, q_ref[rows, :])
            dv_ref[0:ext, :] += _dot_tn(p.astype(BF16), dob)

    sd = jax.ShapeDtypeStruct
    blk = pl.BlockSpec((L, HD), lambda h: (0, h))
    wide = pl.BlockSpec((L, 2 * HD), lambda h: (0, h))
    return pl.pallas_call(
        body, name="mla_attn_bwd", grid=(MLA_H,),
        in_specs=[wide, wide, blk, blk, blk, blk], out_specs=[wide, wide, blk],
        out_shape=[sd((L, MLA_H * 2 * HD), F32), sd((L, MLA_H * 2 * HD), F32), sd((L, MLA_H * HD), F32)],
        compiler_params=pltpu.CompilerParams(dimension_semantics=("arbitrary",), vmem_limit_bytes=VMEM_LIMIT),
    )(qp, kp, v, o, lse, do)


def _kv_fn(mem, gm, w, gk):
    kv = _mm(_rms(mem, gm, D_MODEL), w)
    k = jnp.concatenate([_rms(kv[:, HD * h:HD * (h + 1)], gk, HD) for h in range(X_HEADS)], axis=-1)
    return k, kv[:, XQ:]


def _kv_prep(mem, gm, w, gk, name):
    def fn(mem, gm, w, gk):
        return _kv_fn(mem, gm, w, gk)
    M = mem.shape[0]
    return _rowwise(name, fn, [('c', mem), ('c', gm), ('c', w), ('c', gk)],
                    [('c', (M, XQ), F32), ('c', (M, XQ), F32)], 1)


def _kv_prep_bwd(mem, gm, w, gk, dk, dv, name):
    def fn(mem, gm, w, gk, dk, dv):
        _, vjp = jax.vjp(lambda a, b, c: _kv_fn(mem, a, b, c), gm, w, gk)
        return vjp((dk, dv))
    return _rowwise(name, fn, [('c', mem), ('c', gm), ('c', w), ('c', gk), ('c', dk), ('c', dv)],
                    [('c', gm.shape, F32), ('c', w.shape, BF16), ('c', gk.shape, F32)], 1)


def _forward_merge(x, mix, mix_kind, xq, gate, k, v, gq, wout, name, nblk, sub, host=None):
    def fn(x, mix, xq, gate, k, v, gq, wout):
        o = _merge(mix, xq, gate, k, v, gq)
        return (x + _dot(o.astype(BF16), wout),)
    L = x.shape[0]
    out = _rowwise(name, fn, [('r', x), (mix_kind, mix), ('r', xq), ('r', gate), ('c', k), ('c', v), ('c', gq),
                              ('c', wout)], [('r', (L, D_MODEL), F32)], nblk, sub, host=host)
    return out[0] if host is None else (out[0][0], out[1])


def _backward_merge(dx, mix, mix_kind, xq, gate, k, v, gq, wout, name, nblk, sub):
    def fn(dx, mix, xq, gate, k, v, gq, wout):
        g16 = dx.astype(BF16)
        do = _dot_nt(g16, wout)
        o, vjp = jax.vjp(_merge, mix, xq, gate, k, v, gq)
        dmix, dxq, dgate, dk, dv, dgq = vjp(do)
        return dmix, dxq, dgate, o, g16, dk, dv, dgq
    L = dx.shape[0]
    return _rowwise(
        name, fn,
        [('r', dx), (mix_kind, mix), ('r', xq), ('r', gate), ('c', k), ('c', v), ('c', gq), ('c', wout)],
        [('r', (L, PRIM), F32), ('r', (L, XQ), F32), ('r', (L, BRANCH), F32), ('r', (L, BRANCH), BF16),
         ('r', (L, D_MODEL), BF16), ('a', k.shape, F32), ('a', v.shape, F32), ('a', gq.shape, F32)], nblk, sub)


_MLA_IN = 3392
_MLA_IN_PAD = 3456


def _from_slots(g):
    _, k, n = g.shape
    return jnp.transpose(g, (1, 0, 2)).reshape(k, N_DEV * n)


def _to_slots(w):
    k = w.shape[0]
    return jnp.transpose(w.reshape(k, N_DEV, -1), (1, 0, 2))


def _uq_to_kernel(g):
    uq = _from_slots(g).reshape(Q_LORA, MLA_H, HD + ROPE)
    return jnp.concatenate([uq[:, :, :HD].reshape(Q_LORA, PRIM),
                            jnp.pad(uq[:, :, HD:], ((0, 0), (0, 0), (0, HD - ROPE))).reshape(Q_LORA, PRIM)], axis=1)


def _uq_from_kernel(d_w_q):
    uq = jnp.concatenate([d_w_q[:, :PRIM].reshape(Q_LORA, MLA_H, HD),
                          d_w_q[:, PRIM:].reshape(Q_LORA, MLA_H, HD)[:, :, :ROPE]], axis=2)
    return _to_slots(uq.reshape(Q_LORA, MLA_H * (HD + ROPE)))


def _mla_in_perm(w):
    return jnp.concatenate([w[:, :768], w[:, 832:], w[:, 768:832], jnp.zeros((w.shape[0], 64), w.dtype)], axis=1)


def _mla_in_unperm(w):
    return jnp.concatenate([w[:, :768], w[:, 3328:3392], w[:, 768:3328]], axis=1)


_SMALL = (("ln_gain", 2048), ("mem_norm", 2048), ("xq_norm", 256), ("xk_norm", 256), ("s5_lambda_re", 6144),
          ("s5_lambda_im", 6144), ("s5_log_step", 96), ("s5_b_re", 98304), ("s5_b_im", 98304), ("s5_c_re", 98304),
          ("s5_c_im", 98304), ("s5_d", 1536), ("mla_q_lora_norm", 512), ("mla_kv_lora_norm", 256),
          ("mla_q_nope_norm", 128), ("mla_k_nope_norm", 128), ("mla_q_rope_norm", 64), ("mla_k_rope_norm", 64))
_SMALL_ROWS = 432
_SMALL_OFF = {name: sum(n for _, n in _SMALL[:i]) for i, (name, _) in enumerate(_SMALL)}


def _pack_small(d):
    flat = jnp.concatenate([d[n].reshape(-1).astype(F32) for n, _ in _SMALL])
    return jnp.pad(flat, (0, _SMALL_ROWS * 1024 - flat.shape[0])).reshape(_SMALL_ROWS, 1024)


def _unpack_small(p, name, shape):
    off = _SMALL_OFF[name]
    return p.reshape(-1)[off:off + int(np.prod(shape))].reshape(shape)


_WEIGHTS = ('ln_gain', 'w_out', 'mem_norm', 'w_mem_kv', 'xq_norm', 'xk_norm', 's5_w_in', 's5_lambda_re',
            's5_lambda_im', 's5_log_step', 's5_b_re', 's5_b_im', 's5_c_re', 's5_c_im', 's5_d', 's5_w_glu', 'mla_w_in',
            'mla_q_lora_norm', 'mla_kv_lora_norm', 'mla_w_uq', 'mla_w_ukv', 'mla_q_nope_norm', 'mla_k_nope_norm',
            'mla_q_rope_norm', 'mla_k_rope_norm')
_BIG = ('w_out', 'w_mem_kv', 's5_w_in', 's5_w_glu', 'mla_w_in', 'mla_w_uq', 'mla_w_ukv')


def _pad128(g):
    return jnp.pad(g.reshape(1, -1), ((0, 0), (0, HD - g.shape[-1])))


def kernel(x, mem, positions, ln_gain, w_out, mem_norm, w_mem_kv, xq_norm, xk_norm, s5_w_in, s5_lambda_re, s5_lambda_im, s5_log_step, s5_b_re, s5_b_im, s5_c_re, s5_c_im, s5_d, s5_w_glu, mla_w_in, mla_q_lora_norm, mla_kv_lora_norm, mla_w_uq, mla_w_ukv, mla_q_nope_norm, mla_k_nope_norm, mla_q_rope_norm, mla_k_rope_norm, loss_target, m_ln_gain, m_w_out, m_mem_norm, m_w_mem_kv, m_xq_norm, m_xk_norm, m_s5_w_in, m_s5_lambda_re, m_s5_lambda_im, m_s5_log_step, m_s5_b_re, m_s5_b_im, m_s5_c_re, m_s5_c_im, m_s5_d, m_s5_w_glu, m_mla_w_in, m_mla_q_lora_norm, m_mla_kv_lora_norm, m_mla_w_uq, m_mla_w_ukv, m_mla_q_nope_norm, m_mla_k_nope_norm, m_mla_q_rope_norm, m_mla_k_rope_norm, v_ln_gain, v_w_out, v_mem_norm, v_w_mem_kv, v_xq_norm, v_xk_norm, v_s5_w_in, v_s5_lambda_re, v_s5_lambda_im, v_s5_log_step, v_s5_b_re, v_s5_b_im, v_s5_c_re, v_s5_c_im, v_s5_d, v_s5_w_glu, v_mla_w_in, v_mla_q_lora_norm, v_mla_kv_lora_norm, v_mla_w_uq, v_mla_w_ukv, v_mla_q_nope_norm, v_mla_k_nope_norm, v_mla_q_rope_norm, v_mla_k_rope_norm):
    weights = dict(ln_gain=ln_gain, w_out=w_out, mem_norm=mem_norm, w_mem_kv=w_mem_kv, xq_norm=xq_norm,
                   xk_norm=xk_norm, s5_w_in=s5_w_in, s5_lambda_re=s5_lambda_re, s5_lambda_im=s5_lambda_im,
                   s5_log_step=s5_log_step, s5_b_re=s5_b_re, s5_b_im=s5_b_im, s5_c_re=s5_c_re, s5_c_im=s5_c_im,
                   s5_d=s5_d, s5_w_glu=s5_w_glu, mla_w_in=mla_w_in, mla_q_lora_norm=mla_q_lora_norm,
                   mla_kv_lora_norm=mla_kv_lora_norm, mla_w_uq=mla_w_uq, mla_w_ukv=mla_w_ukv,
                   mla_q_nope_norm=mla_q_nope_norm, mla_k_nope_norm=mla_k_nope_norm,
                   mla_q_rope_norm=mla_q_rope_norm, mla_k_rope_norm=mla_k_rope_norm)
    m_in = dict(zip(_WEIGHTS, (m_ln_gain, m_w_out, m_mem_norm, m_w_mem_kv, m_xq_norm, m_xk_norm, m_s5_w_in,
                               m_s5_lambda_re, m_s5_lambda_im, m_s5_log_step, m_s5_b_re, m_s5_b_im, m_s5_c_re,
                               m_s5_c_im, m_s5_d, m_s5_w_glu, m_mla_w_in, m_mla_q_lora_norm, m_mla_kv_lora_norm,
                               m_mla_w_uq, m_mla_w_ukv, m_mla_q_nope_norm, m_mla_k_nope_norm, m_mla_q_rope_norm,
                               m_mla_k_rope_norm)))
    v_in = dict(zip(_WEIGHTS, (v_ln_gain, v_w_out, v_mem_norm, v_w_mem_kv, v_xq_norm, v_xk_norm, v_s5_w_in,
                               v_s5_lambda_re, v_s5_lambda_im, v_s5_log_step, v_s5_b_re, v_s5_b_im, v_s5_c_re,
                               v_s5_c_im, v_s5_d, v_s5_w_glu, v_mla_w_in, v_mla_q_lora_norm, v_mla_kv_lora_norm,
                               v_mla_w_uq, v_mla_w_ukv, v_mla_q_nope_norm, v_mla_k_nope_norm, v_mla_q_rope_norm,
                               v_mla_k_rope_norm)))

    x0 = x[0]
    mem0 = mem[0]
    target = loss_target[0]
    L = x0.shape[0]
    nblk, sub = 8, 1
    me = 4 * lax.axis_index("x") + 2 * lax.axis_index("y") + lax.axis_index("c")

    lora = jnp.pad(jnp.concatenate([mla_q_lora_norm, mla_kv_lora_norm], axis=1), ((0, 7), (0, HD - 96)))
    def gather(*shards):
        return _plan_all_gather([s.astype(BF16) for s in shards])

    (W_in_s5,) = _exchange_call(gather(s5_w_in[0]), "ag_s5_w_in")

    ln0, ln1 = ln_gain[0:1], ln_gain[1:2]
    gq0, gq1 = xq_norm[0:1], xq_norm[1:2]
    gk0, gk1 = xk_norm[0:1], xk_norm[1:2]
    gm0, gm1 = mem_norm[0:1], mem_norm[1:2]
    gqn, gkn = mla_q_nope_norm, mla_k_nope_norm
    gqr, gkr = _pad128(mla_q_rope_norm), _pad128(mla_k_rope_norm)

    lr3 = s5_lambda_re.reshape(S5_G, 1, S5_P)
    li3 = s5_lambda_im.reshape(S5_G, 1, S5_P)
    ls3 = s5_log_step.reshape(S5_G, 1, 1)
    btr = jnp.swapaxes(s5_b_re[0], 1, 2)
    bti = jnp.swapaxes(s5_b_im[0], 1, 2)
    a_r, a_i, bbr, bbi = _s5_params(lr3, li3, ls3, btr, bti)
    bm, bmt, cm, cmt = _s5_mats(bbr, bbi, s5_c_re[0], s5_c_im[0])
    a_r2 = a_r.reshape(1, S5_G * S5_P)
    a_i2 = a_i.reshape(1, S5_G * S5_P)
    cmask, rmat = _s5_compact_consts()

    half = ROPE // 2
    inv_freq = ROPE_THETA ** (-jnp.arange(half, dtype=F32) / half)
    invf = jnp.concatenate([inv_freq, inv_freq, jnp.zeros((HD - ROPE,), F32)]).reshape(1, HD)

    def rot_tables(pos, invf):
        ang = pos.astype(F32) * invf
        lane = lax.broadcasted_iota(jnp.int32, ang.shape, 1)
        c = jnp.where(lane < ROPE, jnp.cos(ang), 0.0)
        s = jnp.sin(ang)
        return c, jnp.where(lane < half, -s, 0.0), jnp.where((lane >= half) & (lane < ROPE), s, 0.0)

    tc, ts1, ts2 = _rowwise("rot_tables", rot_tables, [('r', positions.reshape(L, 1)), ('c', invf)],
                            [('r', (L, HD), F32)] * 3, nblk, sub)

    def in_s5(x, g, w):
        proj = _mm_slots(_rms(x, g, D_MODEL).astype(BF16), w)
        return proj[:, :PRIM], proj[:, PRIM:PRIM + XQ], proj[:, PRIM + XQ:]

    (u_s5, xq_a, gate_a), (G_mkv0, G_uq) = _rowwise(
        "s5_in", in_s5, [('r', x0), ('c', ln0), ('c', W_in_s5)],
        [('r', (L, PRIM), F32), ('r', (L, XQ), F32), ('r', (L, BRANCH), F32)], nblk, sub,
        host=gather(w_mem_kv[0], mla_w_uq[0]))
    (y_s5,), (W_glu, G_out0) = _s5_fwd(u_s5, bm, cm, a_r2, a_i2, s5_d, host=gather(s5_w_glu[0], w_out[0]))

    def glu(y, w):
        z = _mm_slots(_gelu(y).astype(BF16), w)
        return (z[:, :PRIM] * _sigmoid(z[:, PRIM:]),)

    (y2,), (G_in_mla,) = _rowwise("s5_glu", glu, [('r', y_s5), ('c', W_glu)], [('r', (L, PRIM), F32)], nblk, sub,
                                  host=gather(mla_w_in[0]))
    W_mkv0 = G_mkv0.reshape(D_MODEL, 2 * XQ)
    k_a, v_a = _kv_prep(mem0, gm0, W_mkv0, gk0, "kv_prep0")
    x1, (W_kv, G_mkv1, G_lora) = _forward_merge(
        x0, y2, 'r', xq_a, gate_a, k_a, v_a, gq0, G_out0.reshape(BRANCH, D_MODEL), "merge0", nblk, sub,
        host=_plan_all_gather([mla_w_ukv[0].astype(BF16), w_mem_kv[1].astype(BF16), lora]))
    W_in_mla = _mla_in_perm(_from_slots(G_in_mla))
    W_q = _uq_to_kernel(G_uq)
    g_qlora = G_lora[:, 0, :64].reshape(1, Q_LORA)
    g_kvlora = G_lora[:, 0, 64:96].reshape(1, KV_LORA)

    def in_mla(x, g, w):
        proj = _dot(_rms(x, g, D_MODEL).astype(BF16), w)
        return proj[:, :512], proj[:, 512:768], proj[:, 768:1280], proj[:, 1280:3328], proj[:, 3328:]

    (c_q, c_kv, xq_b, gate_b, krp), (G_out1,) = _rowwise(
        "mla_in", in_mla, [('r', x1), ('c', ln1), ('c', W_in_mla)],
        [('r', (L, Q_LORA), F32), ('r', (L, KV_LORA), F32), ('r', (L, XQ), F32), ('r', (L, BRANCH), F32),
         ('r', (L, HD), F32)], nblk, sub, host=gather(w_out[1]))
    W_out = (G_out0.reshape(BRANCH, D_MODEL), G_out1.reshape(BRANCH, D_MODEL))
    W_mkv = (W_mkv0, G_mkv1.reshape(D_MODEL, 2 * XQ))

    def qkv(c_q, c_kv, krp, tc, ts1, ts2, gql, gkvl, wq, wkv, gqn, gkn, gqr, gkr):
        q = _dot(_rms(c_q, gql, Q_LORA).astype(BF16), wq)
        kv = _mm_slots(_rms(c_kv, gkvl, KV_LORA).astype(BF16), wkv)
        kp, v = _kv_post(kv, krp, gkn, gkr, tc, ts1, ts2)
        return _q_post(q, gqn, gqr, tc, ts1, ts2), kp, v

    qkv_consts = [('c', g_qlora), ('c', g_kvlora), ('c', W_q), ('c', W_kv), ('c', gqn), ('c', gkn), ('c', gqr),
                  ('c', gkr)]
    q_pad, k_pad, v_h = _rowwise(
        "mla_qkv", qkv, [('r', c_q), ('r', c_kv), ('r', krp), ('r', tc), ('r', ts1), ('r', ts2)] + qkv_consts,
        [('r', (L, 2 * PRIM), BF16), ('r', (L, 2 * PRIM), BF16), ('r', (L, PRIM), BF16)], nblk, sub)
    scale = (HD + ROPE) ** -0.5
    attn, lse = _attn_fwd(q_pad, k_pad, v_h, scale)
    k_b, v_b = _kv_prep(mem0, gm1, W_mkv[1], gk1, "kv_prep1")
    x2 = _forward_merge(x1, attn, 'r', xq_b, gate_b, k_b, v_b, gq1, W_out[1], "merge1", nblk, sub)

    def loss_fn(y, t):
        err = y - t
        part = 0.5 * jnp.sum(jnp.sum(err * err, axis=-1, keepdims=True) * (1.0 / D_MODEL), axis=0, keepdims=True)
        return err * (1.0 / D_MODEL), jnp.broadcast_to(part, (1, HD))

    dx2, loss_part = _rowwise("loss", loss_fn, [('r', x2), ('r', target)],
                              [('r', (L, D_MODEL), F32), ('a', (1, HD), F32)], nblk, sub)

    dattn, dxq_b, dgate_b, o_b, g_b, dk_b, dv_b, dgq1 = _backward_merge(
        dx2, attn, 'r', xq_b, gate_b, k_b, v_b, gq1, W_out[1], "merge1_bwd", nblk, sub)
    dgm1, dW_mkv1, dgk1 = _kv_prep_bwd(mem0, gm1, W_mkv[1], gk1, dk_b, dv_b, "kv_prep1_bwd")
    dW_out1 = _matmul_tn(o_b, g_b, "dw_out1")
    dq_pad, dk_pad, dv_h = _attn_bwd(q_pad, k_pad, v_h, attn, lse, dattn, scale)

    def qkv_bwd(c_q, c_kv, krp, tc, ts1, ts2, dqp, dkp, dv, gql, gkvl, wq, wkv, gqn, gkn, gqr, gkr):
        cqn, vjp_qn = jax.vjp(lambda a, b: _rms(a, b, Q_LORA), c_q, gql)
        ckvn, vjp_kvn = jax.vjp(lambda a, b: _rms(a, b, KV_LORA), c_kv, gkvl)
        cqn16 = cqn.astype(BF16)
        ckvn16 = ckvn.astype(BF16)
        q = _dot(cqn16, wq)
        kv = _mm_slots(ckvn16, wkv)
        _, vjp_q = jax.vjp(lambda a, b, c: _q_post(a, b, c, tc, ts1, ts2), q, gqn, gqr)
        dq, dgqn, dgqr = vjp_q(dqp)
        _, vjp_kv = jax.vjp(lambda a, b, c, d: _kv_post(a, b, c, d, tc, ts1, ts2), kv, krp, gkn, gkr)
        dkv, dkrp, dgkn, dgkr = vjp_kv((dkp, dv))
        dq16 = dq.astype(BF16)
        dkv16 = dkv.astype(BF16)
        dc_q, dgql = vjp_qn(_dot_nt(dq16, wq))
        dc_kv, dgkvl = vjp_kvn(_mm_slots_nt(dkv16, wkv))
        return dc_q, dc_kv, dkrp, cqn16, dq16, ckvn16, dkv16, dgql, dgkvl, dgqn, dgkn, dgqr, dgkr

    (dc_q, dc_kv, dkrp, cqn16, dq16, ckvn16, dkv16, dgql, dgkvl, dgqn, dgkn, dgqr, dgkr) = _rowwise(
        "mla_qkv_bwd", qkv_bwd,
        [('r', c_q), ('r', c_kv), ('r', krp), ('r', tc), ('r', ts1), ('r', ts2), ('r', dq_pad), ('r', dk_pad),
         ('r', dv_h)] + qkv_consts,
        [('r', (L, Q_LORA), F32), ('r', (L, KV_LORA), F32), ('r', (L, HD), F32), ('r', (L, Q_LORA), BF16),
         ('r', (L, 2 * PRIM), BF16), ('r', (L, KV_LORA), BF16), ('r', (L, 2 * PRIM), BF16),
         ('a', (1, Q_LORA), F32), ('a', (1, KV_LORA), F32), ('a', (1, HD), F32), ('a', (1, HD), F32),
         ('a', (1, HD), F32), ('a', (1, HD), F32)], nblk, sub)
    dW_q = _matmul_tn(cqn16, dq16, "dw_uq")
    dW_kv = _matmul_tn_slots(ckvn16, dkv16, "dw_ukv")

    def in_bwd(x, dres, g, w, *dparts):
        dproj = jnp.concatenate(dparts, axis=-1).astype(BF16)
        xn, vjp = jax.vjp(lambda a, b: _rms(a, b, D_MODEL), x, g)
        dx, dg = vjp(_mm_slots_nt(dproj, w) if w.ndim == 3 else _dot_nt(dproj, w))
        return dx + dres, xn, dproj, dg

    dx1, xn1, dproj1, dln1 = _rowwise(
        "mla_in_bwd", in_bwd,
        [('r', x1), ('r', dx2), ('c', ln1), ('c', W_in_mla), ('r', dc_q), ('r', dc_kv), ('r', dxq_b), ('r', dgate_b),
         ('r', dkrp)],
        [('r', (L, D_MODEL), F32), ('r', (L, D_MODEL), BF16), ('r', (L, _MLA_IN_PAD), BF16), ('a', (1, D_MODEL), F32)],
        nblk, sub)
    dW_in_mla = _matmul_tn(xn1, dproj1, "dw_mla_in")

    dy2, dxq_a, dgate_a, o_a, g_a, dk_a, dv_a, dgq0 = _backward_merge(
        dx1, y2, 'r', xq_a, gate_a, k_a, v_a, gq0, W_out[0], "merge0_bwd", nblk, sub)
    dgm0, dW_mkv0, dgk0 = _kv_prep_bwd(mem0, gm0, W_mkv[0], gk0, dk_a, dv_a, "kv_prep0_bwd")
    dW_out0 = _matmul_tn(o_a, g_a, "dw_out0")

    def glu_bwd(y, dy2, w):
        h, vjp_h = jax.vjp(_gelu, y)
        h16 = h.astype(BF16)
        z = _mm_slots(h16, w)
        _, vjp_z = jax.vjp(lambda z: z[:, :PRIM] * _sigmoid(z[:, PRIM:]), z)
        dz16 = vjp_z(dy2)[0].astype(BF16)
        return vjp_h(_mm_slots_nt(dz16, w))[0], h16, dz16

    early = [dW_out1.reshape(N_DEV, 256, D_MODEL), dW_mkv1.reshape(N_DEV, 128, 2 * XQ),
             _to_slots(_mla_in_unperm(dW_in_mla)), _uq_from_kernel(dW_q), dW_kv,
             dW_out0.reshape(N_DEV, 256, D_MODEL), dW_mkv0.reshape(N_DEV, 128, 2 * XQ)]
    (dy_s5, h16, dz16), early_pair = _rowwise(
        "s5_glu_bwd", glu_bwd, [('r', y_s5), ('r', dy2), ('c', W_glu)],
        [('r', (L, PRIM), F32), ('r', (L, PRIM), BF16), ('r', (L, 2 * PRIM), BF16)], nblk, sub,
        host=_plan_pair(early))
    dW_glu = _matmul_tn_slots(h16, dz16, "dw_glu")
    early_t = _pair_add(early + [dW_glu], early_pair + list(_exchange_call(_plan_pair([dW_glu]), "rs_pair_glu")),
                        "rs_add_early")
    (du_s5, dbc, dcc, dd, dar, dai), early_recv = _s5_bwd(u_s5, dy_s5, bm, bmt, cmt, a_r2, a_i2, s5_d, cmask, rmat,
                                                          host=_plan_chips(early_t))
    dx0, xn0, dproj0, dln0 = _rowwise(
        "s5_in_bwd", in_bwd,
        [('r', x0), ('r', dx1), ('c', ln0), ('c', W_in_s5), ('r', du_s5), ('r', dxq_a),
         ('r', dgate_a)],
        [('r', (L, D_MODEL), F32), ('r', (L, D_MODEL), BF16), ('r', (L, 2 * BRANCH), BF16), ('a', (1, D_MODEL), F32)],
        nblk, sub)

    dbc4 = dbc.reshape(S5_G, S5_C, 2, S5_P)
    dcc4 = dcc.reshape(S5_G, S5_C, 2, S5_P)
    dlr, dli, dls, dbtr, dbti = _s5_params_bwd(
        lr3, li3, ls3, btr, bti, dar.reshape(S5_G, 1, S5_P), dai.reshape(S5_G, 1, S5_P), dbc4[:, :, 0], dbc4[:, :, 1])

    small_part = {
        "ln_gain": jnp.concatenate([dln0, dln1]), "mem_norm": jnp.concatenate([dgm0, dgm1]),
        "xq_norm": jnp.concatenate([dgq0, dgq1]), "xk_norm": jnp.concatenate([dgk0, dgk1]),
        "s5_lambda_re": dlr, "s5_lambda_im": dli, "s5_log_step": dls,
        "s5_b_re": jnp.swapaxes(dbtr, 1, 2), "s5_b_im": jnp.swapaxes(dbti, 1, 2),
        "s5_c_re": dcc4[:, :, 0], "s5_c_im": -dcc4[:, :, 1], "s5_d": dd,
        "mla_q_lora_norm": dgql, "mla_kv_lora_norm": dgkvl, "mla_q_nope_norm": dgqn, "mla_k_nope_norm": dgkn,
        "mla_q_rope_norm": dgqr[:, :ROPE], "mla_k_rope_norm": dgkr[:, :ROPE],
    }
    loss8 = jnp.pad(loss_part, ((0, 7), (0, 0)))
    dW_in_s5, (small_gath, loss_g) = _matmul_tn_slots(
        xn0, dproj0, "dw_s5_in", host=_plan_all_gather([_pack_small(small_part).astype(BF16), loss8]))

    late = [dW_in_s5]
    late_t = _pair_add(late, list(_exchange_call(_plan_pair(late), "rs_pair_late")), "rs_add_late")
    owners = [("w_out", 1), ("w_mem_kv", 1), ("mla_w_in", 0), ("mla_w_uq", 0), ("mla_w_ukv", 0), ("w_out", 0),
              ("w_mem_kv", 0), ("s5_w_glu", 0)]
    upd, late_recv = _updates_call(early_recv, [weights[n][i] for n, i in owners], [m_in[n][i] for n, i in owners],
                                   [v_in[n][i] for n, i in owners], "update_early", host=_plan_chips(late_t))
    owners.append(("s5_w_in", 0))
    upd.append(_sum_adamw(late_recv[0], s5_w_in[0], m_s5_w_in[0], v_s5_w_in[0], "update_s5_w_in"))
    grads, delta, new_m, new_v = {}, {}, {}, {}
    for n in _BIG:
        parts = [u for u, (o, _) in sorted(zip(upd, owners), key=lambda t: t[1][1]) if o == n]
        grads[n], delta[n], new_m[n], new_v[n] = (jnp.stack([p[j] for p in parts]) for j in range(4))

    def whole(name, a):
        if name == "mla_q_lora_norm":
            return lax.dynamic_update_slice(jnp.zeros((Q_LORA,), F32), a.reshape(-1), (me * 64,))
        if name == "mla_kv_lora_norm":
            return lax.dynamic_update_slice(jnp.zeros((KV_LORA,), F32), a.reshape(-1), (me * 32,))
        return a

    wp = _pack_small({n: whole(n, weights[n]) for n, _ in _SMALL})
    mp = _pack_small({n: whole(n, m_in[n]) for n, _ in _SMALL})
    vp = _pack_small({n: whole(n, v_in[n]) for n, _ in _SMALL})
    gs, ds, ms, vs, loss_sum = _small_update(small_gath, loss_g, wp, mp, vp, "small_update")
    loss = loss_sum[0, 0]

    for n, _ in _SMALL:
        shape = weights[n].shape
        if n == "mla_q_lora_norm":
            take = lambda p: lax.dynamic_slice(_unpack_small(p, n, (Q_LORA,)), (me * 64,), (64,)).reshape(shape)
        elif n == "mla_kv_lora_norm":
            take = lambda p: lax.dynamic_slice(_unpack_small(p, n, (KV_LORA,)), (me * 32,), (32,)).reshape(shape)
        else:
            take = lambda p: _unpack_small(p, n, shape)
        grads[n], delta[n], new_m[n], new_v[n] = take(gs), take(ds), take(ms), take(vs)
    return (loss, dx0[None], *[grads[n] for n in _WEIGHTS], *[delta[n] for n in _WEIGHTS],
            *[new_m[n] for n in _WEIGHTS], *[new_v[n] for n in _WEIGHTS])
```

```python
import functools
import math

import numpy as np
import jax
import jax.numpy as jnp
from jax import lax
from jax.experimental import pallas as pl
from jax.experimental.pallas import tpu as pltpu

F32 = jnp.float32
BF16 = jnp.bfloat16
EPS = 1e-6
NEG = float(np.finfo(np.float32).min)
MESH = pl.DeviceIdType.MESH

N_DEV = 8
D_MODEL = 1024
MEM_LEN = 256
XQ = 512
PRIM = 1536
BRANCH = 2048
X_HEADS = 4
HD = 128
S5_G = 96
S5_P = 64
S5_C = 16
S5_GB = 8
S5_W = S5_GB * S5_P
MLA_H = 12
ROPE = 64
Q_LORA = 512
KV_LORA = 256
ROPE_THETA = 10000.0

ADAM_LR = 0.001
ADAM_B1 = 0.9
ADAM_B2 = 0.999
ADAM_EPS = 1e-08
ADAM_WD = 0.01
ADAM_STEP = 10

VMEM_LIMIT = 56 * 1024 * 1024


def _dot(a, b):
    return jnp.dot(a, b, preferred_element_type=F32)


def _dot_nt(a, b):
    return lax.dot_general(a, b, (((1,), (1,)), ((), ())), preferred_element_type=F32)


def _dot_tn(a, b):
    return lax.dot_general(a, b, (((0,), (0,)), ((), ())), preferred_element_type=F32)


@jax.custom_vjp
def _mm(a, b):
    return _dot(a.astype(BF16), b.astype(BF16))


def _mm_fwd(a, b):
    return _mm(a, b), (a, b)


def _mm_bwd(res, g):
    a, b = res
    gb = g.astype(BF16)
    return _dot_nt(gb, b.astype(BF16)).astype(a.dtype), _dot_tn(a.astype(BF16), gb).astype(b.dtype)


_mm.defvjp(_mm_fwd, _mm_bwd)


@jax.custom_vjp
def _mm_nt(a, b):
    return _dot_nt(a.astype(BF16), b.astype(BF16))


def _mm_nt_fwd(a, b):
    return _mm_nt(a, b), (a, b)


def _mm_nt_bwd(res, g):
    a, b = res
    gb = g.astype(BF16)
    return _dot(gb, b.astype(BF16)).astype(a.dtype), _dot_tn(gb, a.astype(BF16)).astype(b.dtype)


_mm_nt.defvjp(_mm_nt_fwd, _mm_nt_bwd)


@jax.custom_vjp
def _softmax(s):
    m = jnp.max(s, axis=-1, keepdims=True)
    e = jnp.exp(s - m)
    return e / jnp.sum(e, axis=-1, keepdims=True)


def _softmax_fwd(s):
    p = _softmax(s)
    return p, p


def _softmax_bwd(p, g):
    return (p * (g - jnp.sum(p * g, axis=-1, keepdims=True)),)


_softmax.defvjp(_softmax_fwd, _softmax_bwd)


def _rms(x, g, n):
    ms = jnp.sum(x * x, axis=-1, keepdims=True) * (1.0 / n)
    return x * lax.rsqrt(ms + EPS) * g


def _sigmoid(x):
    return 1.0 / (1.0 + jnp.exp(-x))


def _silu(x):
    return x * _sigmoid(x)


def _gelu(x):
    c = math.sqrt(2.0 / math.pi)
    return 0.5 * x * (1.0 + jnp.tanh(c * (x + 0.044715 * (x * x * x))))


@jax.custom_vjp
def _rot(x, c, s1, s2):
    return x * c + pltpu.roll(x, 96, 1) * s1 + pltpu.roll(x, 32, 1) * s2


def _rot_fwd(x, c, s1, s2):
    return _rot(x, c, s1, s2), (c, s1, s2)


def _rot_bwd(res, g):
    c, s1, s2 = res
    dx = g * c + pltpu.roll(g * s1, 32, 1) + pltpu.roll(g * s2, 96, 1)
    return dx, jnp.zeros_like(c), jnp.zeros_like(s1), jnp.zeros_like(s2)


_rot.defvjp(_rot_fwd, _rot_bwd)


def _mem_attn(xq, k, v, gq):
    outs = []
    for h in range(X_HEADS):
        sl = slice(HD * h, HD * (h + 1))
        q = _rms(xq[:, sl], gq, HD)
        p = _softmax(_mm_nt(q, k[:, sl]) * (HD ** -0.5))
        outs.append(_mm(p, v[:, sl]))
    return jnp.concatenate(outs, axis=-1)


def _merge(mix, xq, gate, k, v, gq):
    return jnp.concatenate([mix, _mem_attn(xq, k, v, gq)], axis=-1) * _silu(gate)


def _q_post(q, gqn, gqr, c, s1, s2):
    pieces = []
    for h in range(MLA_H):
        pieces.append(_rms(q[:, HD * h:HD * (h + 1)], gqn, HD))
        pieces.append(_rot(_rms(q[:, PRIM + HD * h:PRIM + HD * (h + 1)], gqr, ROPE), c, s1, s2))
    return jnp.concatenate(pieces, axis=-1)


def _kv_post(kv, krp, gkn, gkr, c, s1, s2):
    kr = _rot(_rms(krp, gkr, ROPE), c, s1, s2)
    pieces, vals = [], []
    for h in range(MLA_H):
        pieces.append(_rms(kv[:, 2 * HD * h:2 * HD * h + HD], gkn, HD))
        pieces.append(kr)
        vals.append(kv[:, 2 * HD * h + HD:2 * HD * (h + 1)])
    return jnp.concatenate(pieces, axis=-1), jnp.concatenate(vals, axis=-1)


def _rowwise(name, fn, ins, outs, nblk, sub=1, host=None):
    n_in = len(ins)

    def spec(kind, shape):
        if kind == 'r':
            return pl.BlockSpec((shape[0] // nblk, shape[1]), lambda i: (i, 0))
        if kind == 't':
            return pl.BlockSpec((shape[0], shape[1] // nblk), lambda i: (0, i))
        zeros = (0,) * len(shape)
        return pl.BlockSpec(tuple(shape), lambda i: zeros)

    def body(*refs):
        i = pl.program_id(0)
        res = fn(*[r[...] for r in refs[:n_in]])
        for (kind, _, _), ref, val in zip(outs, refs[n_in:], res):
            if kind == 'a':
                @pl.when(i == 0)
                def _():
                    ref[...] = jnp.zeros_like(ref)
                ref[...] += val.astype(ref.dtype)
            elif kind == 't':
                ref[...] = val.astype(F32).T.astype(ref.dtype)
            else:
                ref[...] = val.astype(ref.dtype)

    res, hosted = _hosting_call(
        body, name, nblk, host, [a for _, a in ins], [spec(k, a.shape) for k, a in ins],
        [jax.ShapeDtypeStruct(tuple(s), d) for _, s, d in outs], [spec(k, s) for k, s, _ in outs], [])
    return res if host is None else (res, hosted)


def _matmul_tn(at, g, name, out_dtype=BF16):
    K, L = at.shape
    N = g.shape[1]
    tn = next(t for t in (512, 384, 256, 128) if N % t == 0)

    def body(a_ref, g_ref, o_ref):
        o_ref[...] = _dot(a_ref[...], g_ref[...]).astype(o_ref.dtype)

    return pl.pallas_call(
        body, name=name, grid=(N // tn,),
        in_specs=[pl.BlockSpec((K, L), lambda n: (0, 0)), pl.BlockSpec((L, tn), lambda n: (0, n))],
        out_specs=pl.BlockSpec((K, tn), lambda n: (0, n)),
        out_shape=jax.ShapeDtypeStruct((K, N), out_dtype),
        compiler_params=pltpu.CompilerParams(dimension_semantics=("arbitrary",), vmem_limit_bytes=VMEM_LIMIT),
    )(at, g)


def _matmul_tn_slots(at, g, name, host=None):
    K, L = at.shape
    n = g.shape[1] // N_DEV

    def body(a_ref, g_ref, o_ref):
        o_ref[...] = _dot(a_ref[...], g_ref[...]).astype(o_ref.dtype)

    res, hosted = _hosting_call(
        body, name, N_DEV, host, [at, g],
        [pl.BlockSpec((K, L), lambda d: (0, 0)), pl.BlockSpec((L, n), lambda d: (0, d))],
        [jax.ShapeDtypeStruct((N_DEV, K, n), BF16)], [pl.BlockSpec((None, K, n), lambda d: (d, 0, 0))], [])
    return res[0] if host is None else (res[0], hosted)


def _mm_slots(a16, w):
    return jnp.concatenate([_dot(a16, w[d]) for d in range(N_DEV)], axis=-1)


def _mm_slots_nt(g16, w):
    n = w.shape[2]
    out = _dot_nt(g16[:, 0:n], w[0])
    for d in range(1, N_DEV):
        out = out + _dot_nt(g16[:, d * n:(d + 1) * n], w[d])
    return out


class _Exchange:
    def __init__(self, ins, outs, scratch, start, finish):
        self.ins, self.outs, self.scratch, self.start, self.finish = ins, outs, scratch, start, finish


def _xyc():
    return lax.axis_index("x"), lax.axis_index("y"), lax.axis_index("c")


def _plan_all_gather(xs):
    n = len(xs)

    def build(x_refs, out_refs, sems):
        send_sems, recv_sems, local_sems = sems
        x, y, c = _xyc()

        def copies(k, block, to, own=False):
            slot = 4 * block[0] + 2 * block[1] + block[2]
            return [pltpu.make_async_remote_copy(
                src_ref=x_refs[a] if own else out_refs[a].at[slot], dst_ref=out_refs[a].at[slot],
                send_sem=send_sems.at[k * n + a], recv_sem=recv_sems.at[k * n + a], device_id=to,
                device_id_type=MESH) for a in range(n)]

        mine = [pltpu.make_async_copy(x_refs[a], out_refs[a].at[4 * x + 2 * y + c], local_sems.at[a])
                for a in range(n)]
        return copies, mine, (x, y, c), [(1 - x, y), (x, 1 - y), (1 - x, 1 - y)]

    def first_copies(copies, me, chips):
        x, y, c = me
        first = copies(0, me, (x, y, 1 - c), own=True)
        for j, chip in enumerate(chips):
            first += copies(1 + j, me, (*chip, c), own=True)
        return first

    def start(x_refs, out_refs, sems):
        copies, mine, me, chips = build(x_refs, out_refs, sems)
        for cp in mine + first_copies(copies, me, chips):
            cp.start()

    def finish(x_refs, out_refs, sems):
        copies, mine, me, chips = build(x_refs, out_refs, sems)
        x, y, c = me
        passed = []
        for j, chip in enumerate(chips):
            for cp in copies(1 + j, (*chip, c), me):
                cp.wait_recv()
            fwd = copies(4 + j, (*chip, c), (x, y, 1 - c))
            for cp in fwd:
                cp.start()
            passed += fwd
        for cp in copies(0, (x, y, 1 - c), me):
            cp.wait_recv()
        for j, chip in enumerate(chips):
            for cp in copies(4 + j, (*chip, 1 - c), me):
                cp.wait_recv()
        for cp in first_copies(copies, me, chips) + passed:
            cp.wait_send()
        for cp in mine:
            cp.wait()

    return _Exchange(list(xs), [jax.ShapeDtypeStruct((N_DEV,) + a.shape, a.dtype) for a in xs],
                     [pltpu.SemaphoreType.DMA((7 * n,)), pltpu.SemaphoreType.DMA((7 * n,)),
                      pltpu.SemaphoreType.DMA((n,))], start, finish)


_CHIPS = ((0, 0), (0, 1), (1, 0), (1, 1))


def _plan_pair(sends):
    n = len(sends)

    def build(s_refs, o_refs, sems):
        send_sems, recv_sems = sems
        x, y, c = _xyc()
        return [pltpu.make_async_remote_copy(
            src_ref=s_refs[a].at[4 * px + 2 * py + 1 - c], dst_ref=o_refs[a].at[j],
            send_sem=send_sems.at[j * n + a], recv_sem=recv_sems.at[j * n + a], device_id=(x, y, 1 - c),
            device_id_type=MESH) for j, (px, py) in enumerate(_CHIPS) for a in range(n)]

    def start(s_refs, o_refs, sems):
        for cp in build(s_refs, o_refs, sems):
            cp.start()

    def finish(s_refs, o_refs, sems):
        for cp in build(s_refs, o_refs, sems):
            cp.wait_recv()
            cp.wait_send()

    return _Exchange(list(sends), [jax.ShapeDtypeStruct((4,) + a.shape[1:], a.dtype) for a in sends],
                     [pltpu.SemaphoreType.DMA((4 * n,)), pltpu.SemaphoreType.DMA((4 * n,))], start, finish)


def _plan_chips(ts):
    n = len(ts)
    flips = ((1, 0), (0, 1), (1, 1))

    def build(t_refs, o_refs, sems):
        send_sems, recv_sems, local_sems = sems
        x, y, c = _xyc()
        mine = 2 * x + y
        local = [pltpu.make_async_copy(t_refs[a].at[mine], o_refs[a].at[mine], local_sems.at[a]) for a in range(n)]
        remote = []
        for k, (fx, fy) in enumerate(flips):
            px = 1 - x if fx else x
            py = 1 - y if fy else y
            remote += [pltpu.make_async_remote_copy(
                src_ref=t_refs[a].at[2 * px + py], dst_ref=o_refs[a].at[mine],
                send_sem=send_sems.at[k * n + a], recv_sem=recv_sems.at[k * n + a], device_id=(px, py, c),
                device_id_type=MESH) for a in range(n)]
        return local, remote

    def start(t_refs, o_refs, sems):
        local, remote = build(t_refs, o_refs, sems)
        for cp in local + remote:
            cp.start()

    def finish(t_refs, o_refs, sems):
        local, remote = build(t_refs, o_refs, sems)
        for cp in remote:
            cp.wait_recv()
        for cp in remote:
            cp.wait_send()
        for cp in local:
            cp.wait()

    return _Exchange(list(ts), [jax.ShapeDtypeStruct(a.shape, a.dtype) for a in ts],
                     [pltpu.SemaphoreType.DMA((3 * n,)), pltpu.SemaphoreType.DMA((3 * n,)),
                      pltpu.SemaphoreType.DMA((n,))], start, finish)


def _exchange_call(plan, name):
    n = len(plan.ins)

    def body(*refs):
        ins, outs, sems = refs[:n], refs[n:2 * n], refs[2 * n:]
        plan.start(ins, outs, sems)
        plan.finish(ins, outs, sems)

    return pl.pallas_call(
        body, name=name, out_shape=plan.outs,
        in_specs=[pl.BlockSpec(memory_space=pl.ANY)] * n, out_specs=[pl.BlockSpec(memory_space=pl.ANY)] * n,
        scratch_shapes=plan.scratch,
    )(*plan.ins)


def _pair_add(sends, fromsib, name):
    n = len(sends)
    nb = 8

    def body(*refs):
        c = lax.axis_index("c")
        for a in range(n):
            s_ref, f_ref, t_ref = refs[a], refs[n + a], refs[2 * n + a]
            for j in range(4):
                t_ref[j] = (s_ref[2 * j + c].astype(F32) + f_ref[j].astype(F32)).astype(t_ref.dtype)

    def spec(a, lead):
        return pl.BlockSpec((lead, a.shape[1] // nb, a.shape[2]), lambda i: (0, i, 0))

    return pl.pallas_call(
        body, name=name, grid=(nb,),
        in_specs=[spec(a, N_DEV) for a in sends] + [spec(a, 4) for a in fromsib],
        out_specs=[spec(a, 4) for a in fromsib],
        out_shape=[jax.ShapeDtypeStruct(a.shape, a.dtype) for a in fromsib],
        compiler_params=pltpu.CompilerParams(dimension_semantics=("arbitrary",), vmem_limit_bytes=VMEM_LIMIT),
    )(*sends, *fromsib)


def _adamw_vals(w, g, m, v):
    m2 = ADAM_B1 * m + (1.0 - ADAM_B1) * g
    v2 = ADAM_B2 * v + (1.0 - ADAM_B2) * (g * g)
    m_hat = m2 / (1.0 - ADAM_B1 ** ADAM_STEP)
    v_hat = v2 / (1.0 - ADAM_B2 ** ADAM_STEP)
    delta = -ADAM_LR * (m_hat / (jnp.sqrt(v_hat) + ADAM_EPS) + ADAM_WD * w)
    return delta, m2, v2


def _sum_adamw(recv, w, m, v, name):
    R, C = w.shape
    ns = recv.shape[0]
    br = next((t for t in (256, 128, 64, 32, 16) if R % t == 0), R)

    def body(r_ref, w_ref, m_ref, v_ref, g_ref, d_ref, m2_ref, v2_ref):
        g = r_ref[0].astype(F32)
        for d in range(1, ns):
            g = g + r_ref[d].astype(F32)
        dl, m2, v2 = _adamw_vals(w_ref[...], g, m_ref[...], v_ref[...])
        g_ref[...] = g
        d_ref[...] = dl
        m2_ref[...] = m2
        v2_ref[...] = v2

    spec = pl.BlockSpec((br, C), lambda i: (i, 0))
    return pl.pallas_call(
        body, name=name, grid=(R // br,),
        in_specs=[pl.BlockSpec((ns, br, C), lambda i: (0, i, 0)), spec, spec, spec], out_specs=[spec] * 4,
        out_shape=[jax.ShapeDtypeStruct((R, C), F32)] * 4,
        compiler_params=pltpu.CompilerParams(dimension_semantics=("arbitrary",)),
    )(recv, w, m, v)


def _updates_call(recvs, ws, ms, vs, name, host=None):
    n = len(recvs)
    nb = 8

    def body(*refs):
        for a in range(n):
            r_ref, w_ref, m_ref, v_ref = refs[a], refs[n + a], refs[2 * n + a], refs[3 * n + a]
            g_ref, d_ref, m2_ref, v2_ref = refs[4 * n + 4 * a:4 * n + 4 * a + 4]
            g = r_ref[0].astype(F32)
            for d in range(1, r_ref.shape[0]):
                g = g + r_ref[d].astype(F32)
            dl, m2, v2 = _adamw_vals(w_ref[...], g, m_ref[...], v_ref[...])
            g_ref[...] = g
            d_ref[...] = dl
            m2_ref[...] = m2
            v2_ref[...] = v2

    def spec2(w):
        return pl.BlockSpec((w.shape[0] // nb, w.shape[1]), lambda i: (i, 0))

    def spec3(r):
        return pl.BlockSpec((r.shape[0], r.shape[1] // nb, r.shape[2]), lambda i: (0, i, 0))

    res, hosted = _hosting_call(
        body, name, nb, host, list(recvs) + list(ws) + list(ms) + list(vs),
        [spec3(r) for r in recvs] + [spec2(w) for w in ws] * 3,
        [jax.ShapeDtypeStruct(w.shape, F32) for w in ws for _ in range(4)],
        [spec2(w) for w in ws for _ in range(4)], [])
    return [res[4 * a:4 * a + 4] for a in range(n)], hosted


def _small_update(gath, loss_g, wp, mp, vp, name):
    _, R, C = gath.shape
    br = R // 3

    def body(g_ref, l_ref, w_ref, m_ref, v_ref, go_ref, d_ref, m2_ref, v2_ref, lo_ref):
        g = g_ref[0].astype(F32)
        lsum = l_ref[0]
        for d in range(1, N_DEV):
            g = g + g_ref[d].astype(F32)
            lsum = lsum + l_ref[d]
        dl, m2, v2 = _adamw_vals(w_ref[...], g, m_ref[...], v_ref[...])
        go_ref[...] = g
        d_ref[...] = dl
        m2_ref[...] = m2
        v2_ref[...] = v2
        lo_ref[...] = lsum

    spec = pl.BlockSpec((br, C), lambda i: (i, 0))
    return pl.pallas_call(
        body, name=name, grid=(R // br,),
        in_specs=[pl.BlockSpec((N_DEV, br, C), lambda i: (0, i, 0)),
                  pl.BlockSpec((N_DEV, 8, HD), lambda i: (0, 0, 0)), spec, spec, spec],
        out_specs=[spec] * 4 + [pl.BlockSpec((8, HD), lambda i: (0, 0))],
        out_shape=[jax.ShapeDtypeStruct((R, C), F32)] * 4 + [jax.ShapeDtypeStruct((8, HD), F32)],
        compiler_params=pltpu.CompilerParams(dimension_semantics=("arbitrary",)),
    )(gath, loss_g, wp, mp, vp)


def _s5_param_fn(lr, li, ls, btr, bti):
    step = jnp.exp(ls)
    er = jnp.exp(lr * step)
    ang = li * step
    ar = er * jnp.cos(ang)
    ai = er * jnp.sin(ang)
    nr = ar - 1.0
    den = lr * lr + li * li
    fr = (nr * lr + ai * li) / den
    fi = (ai * lr - nr * li) / den
    return ar, ai, fr * btr - fi * bti, fr * bti + fi * btr


def _s5_params(lr, li, ls, btr, bti):
    def body(lr_ref, li_ref, ls_ref, br_ref, bi_ref, ar_ref, ai_ref, bbr_ref, bbi_ref):
        ar, ai, bbr, bbi = _s5_param_fn(lr_ref[...], li_ref[...], ls_ref[...], br_ref[...], bi_ref[...])
        ar_ref[...] = ar
        ai_ref[...] = ai
        bbr_ref[...] = bbr
        bbi_ref[...] = bbi

    sd = jax.ShapeDtypeStruct
    return pl.pallas_call(
        body, name="s5_params",
        out_shape=[sd(lr.shape, F32), sd(lr.shape, F32), sd(btr.shape, F32), sd(btr.shape, F32)],
    )(lr, li, ls, btr, bti)


def _s5_params_bwd(lr, li, ls, btr, bti, dar, dai, dbbr, dbbi):
    def body(lr_ref, li_ref, ls_ref, br_ref, bi_ref, dar_ref, dai_ref, dbbr_ref, dbbi_ref,
             dlr_ref, dli_ref, dls_ref, dbr_ref, dbi_ref):
        _, vjp = jax.vjp(_s5_param_fn, lr_ref[...], li_ref[...], ls_ref[...], br_ref[...], bi_ref[...])
        dlr, dli, dls, dbr, dbi = vjp((dar_ref[...], dai_ref[...], dbbr_ref[...], dbbi_ref[...]))
        dlr_ref[...] = dlr
        dli_ref[...] = dli
        dls_ref[...] = dls
        dbr_ref[...] = dbr
        dbi_ref[...] = dbi

    sd = jax.ShapeDtypeStruct
    return pl.pallas_call(
        body, name="s5_params_bwd",
        out_shape=[sd(lr.shape, F32), sd(lr.shape, F32), sd(ls.shape, F32), sd(btr.shape, F32), sd(btr.shape, F32)],
    )(lr, li, ls, btr, bti, dar, dai, dbbr, dbbi)


def _cpow(ar, ai, n):
    assert n & (n - 1) == 0
    while n > 1:
        ar, ai = ar * ar - ai * ai, 2.0 * ar * ai
        n //= 2
    return ar, ai


def _scan(st, cr, ci, init, nk, reverse, store, prev=None):
    W = S5_W

    def step(j, carry):
        k = nk - 1 - j if reverse else j
        rows = pl.ds(pl.multiple_of(k * 8, 8), 8)
        sr, si = carry[0], carry[1]
        nsr = cr * sr - ci * si + st[rows, 0:W]
        nsi = cr * si + ci * sr + st[rows, W:2 * W]
        if store:
            st[rows, 0:W] = nsr
            st[rows, W:2 * W] = nsi
        if prev is None:
            return nsr, nsi
        prows = pl.ds(pl.multiple_of(jnp.maximum(k - 1, 0) * 8, 8), 8)
        w = jnp.where(k > 0, 1.0, 0.0).astype(F32)
        pr = prev[prows, 0:W] * w
        pi = prev[prows, W:2 * W] * w
        return nsr, nsi, carry[2] + nsr * pr + nsi * pi, carry[3] + nsi * pr - nsr * pi

    return lax.fori_loop(0, nk, step, init, unroll=2)


def _chain(fin, fr, fi, pr, pi, reverse):
    W = S5_W
    fin[:, 0:W] = fr
    fin[:, W:2 * W] = fi
    rowid = lax.broadcasted_iota(jnp.int32, (8, W), 0)
    cr = jnp.zeros((1, W), F32)
    ci = jnp.zeros((1, W), F32)
    init_r = jnp.zeros((8, W), F32)
    init_i = jnp.zeros((8, W), F32)
    for s in (range(7, -1, -1) if reverse else range(8)):
        init_r = jnp.where(rowid == s, cr, init_r)
        init_i = jnp.where(rowid == s, ci, init_i)
        lr = fin[s:s + 1, 0:W]
        li = fin[s:s + 1, W:2 * W]
        cr, ci = lr + pr * cr - pi * ci, li + pr * ci + pi * cr
    return init_r, init_i


def _full_scan(st, fin, ar, ai, nk, reverse, prev=None):
    W = S5_W
    cr = jnp.broadcast_to(ar, (8, W))
    ci = jnp.broadcast_to(-ai if reverse else ai, (8, W))
    z = jnp.zeros((8, W), F32)
    fr, fi = _scan(st, cr, ci, (z, z), nk, reverse, store=False)
    pr, pi = _cpow(ar, -ai if reverse else ai, nk)
    init = _chain(fin, fr, fi, pr, pi, reverse)
    if prev is None:
        return _scan(st, cr, ci, init, nk, reverse, store=True)
    return _scan(st, cr, ci, init + (z, z), nk, reverse, store=True, prev=prev)


def _s5_specs(L):
    W2 = 2 * S5_W
    GC = S5_GB * S5_C
    col = pl.BlockSpec((L, GC), lambda g: (0, g))
    vec = pl.BlockSpec((1, GC), lambda g: (0, g))
    avec = pl.BlockSpec((1, S5_W), lambda g: (0, g))
    bmat = pl.BlockSpec((None, GC, W2), lambda g: (g, 0, 0))
    cmat = pl.BlockSpec((None, W2, GC), lambda g: (g, 0, 0))
    return col, vec, avec, bmat, cmat


def _interleave(dst, src, nk):
    for s in range(8):
        dst[pl.ds(s, nk, stride=8), :] = src[s * nk:(s + 1) * nk, :]


def _deinterleave(dst, src, nk):
    for s in range(8):
        dst[s * nk:(s + 1) * nk, :] = src[pl.ds(s, nk, stride=8), :]


def _hosting_call(body, name, nsteps, host, ins, in_specs, outs, out_specs, scratch):
    grid = (nsteps,) if isinstance(nsteps, int) else tuple(nsteps)
    params = pltpu.CompilerParams(dimension_semantics=("arbitrary",) * len(grid), vmem_limit_bytes=VMEM_LIMIT)
    if host is None:
        res = pl.pallas_call(
            body, name=name, grid=grid, in_specs=in_specs, out_specs=out_specs, out_shape=outs,
            scratch_shapes=scratch, compiler_params=params,
        )(*ins)
        return list(res), []
    n_in, n_out, n_sc = len(ins), len(outs), len(scratch)
    h_in, h_out = len(host.ins), len(host.outs)

    def hosted(*refs):
        a = refs[:n_in]
        ha = refs[n_in:n_in + h_in]
        o = refs[n_in + h_in:n_in + h_in + n_out]
        ho = refs[n_in + h_in + n_out:n_in + h_in + n_out + h_out]
        sc = refs[n_in + h_in + n_out + h_out:n_in + h_in + n_out + h_out + n_sc]
        hs = refs[n_in + h_in + n_out + h_out + n_sc:]
        first = functools.reduce(jnp.logical_and, [pl.program_id(i) == 0 for i in range(len(grid))])
        last = functools.reduce(jnp.logical_and, [pl.program_id(i) == g - 1 for i, g in enumerate(grid)])

        @pl.when(first)
        def _():
            host.start(ha, ho, hs)

        body(*a, *o, *sc)

        @pl.when(last)
        def _():
            host.finish(ha, ho, hs)

    hbm = pl.BlockSpec(memory_space=pl.ANY)
    res = pl.pallas_call(
        hosted, name=name, grid=grid,
        in_specs=list(in_specs) + [hbm] * h_in, out_specs=list(out_specs) + [hbm] * h_out,
        out_shape=list(outs) + list(host.outs), scratch_shapes=list(scratch) + list(host.scratch),
        compiler_params=params,
    )(*ins, *host.ins)
    return list(res[:n_out]), list(res[n_out:])


def _s5_fwd(u, bm, cm, ar, ai, dvec, host=None):
    L = u.shape[0]
    nk = L // 8
    GC = S5_GB * S5_C
    col, vec, avec, bmat, cmat = _s5_specs(L)

    def body(u_ref, b_ref, c_ref, ar_ref, ai_ref, d_ref, y_ref, st, fin, ui, yi):
        _interleave(ui, u_ref, nk)
        for r in range(8):
            rows = slice(r * nk, (r + 1) * nk)
            st[rows, :] = _dot(ui[rows, :].astype(BF16), b_ref[...])
        _full_scan(st, fin, ar_ref[...], ai_ref[...], nk, reverse=False)
        for r in range(8):
            rows = slice(r * nk, (r + 1) * nk)
            yi[rows, :] = _dot(st[rows, :].astype(BF16), c_ref[...]) + d_ref[...] * ui[rows, :]
        _deinterleave(y_ref, yi, nk)

    return _hosting_call(
        body, "s5_fwd", S5_G // S5_GB, host,
        [u, bm, cm, ar, ai, dvec], [col, bmat, cmat, avec, avec, vec],
        [jax.ShapeDtypeStruct(u.shape, F32)], [col],
        [pltpu.VMEM((L, 2 * S5_W), F32), pltpu.VMEM((8, 2 * S5_W), F32), pltpu.VMEM((L, GC), F32),
         pltpu.VMEM((L, GC), F32)])


def _s5_bwd(u, dy, bm, bmt, cmt, ar, ai, dvec, mask, rmat, host=None):
    L = u.shape[0]
    nk = L // 8
    W = S5_W
    GC = S5_GB * S5_C
    col, vec, avec, bmat, cmat = _s5_specs(L)
    hi = lax.Precision.HIGHEST

    def body(u_ref, dy_ref, b_ref, bt_ref, ct_ref, ar_ref, ai_ref, d_ref, mask_ref, r_ref,
             du_ref, db_ref, dc_ref, dd_ref, dar_ref, dai_ref, sa, sb, fin, ui, dyi, dui):
        ar = ar_ref[...]
        ai = ai_ref[...]
        _interleave(ui, u_ref, nk)
        _interleave(dyi, dy_ref, nk)
        for r in range(8):
            rows = slice(r * nk, (r + 1) * nk)
            sa[rows, :] = _dot(ui[rows, :].astype(BF16), b_ref[...])
            sb[rows, :] = _dot(dyi[rows, :].astype(BF16), ct_ref[...])
        _full_scan(sa, fin, ar, ai, nk, reverse=False)
        gr, gi, accr, acci = _full_scan(sb, fin, ar, ai, nk, reverse=True, prev=sa)
        rowid = lax.broadcasted_iota(jnp.int32, (8, W), 0)
        last = pl.ds((nk - 1) * 8, 8)
        pr = jnp.where(rowid == 0, 0.0, pltpu.roll(sa[last, 0:W], 1, 0))
        pi = jnp.where(rowid == 0, 0.0, pltpu.roll(sa[last, W:2 * W], 1, 0))
        accr = accr + gr * pr + gi * pi
        acci = acci + gi * pr - gr * pi
        dar_ref[...] = jnp.sum(accr, axis=0, keepdims=True)
        dai_ref[...] = jnp.sum(acci, axis=0, keepdims=True)
        dbf = jnp.zeros((GC, 2 * W), F32)
        dcf = jnp.zeros((GC, 2 * W), F32)
        dd = jnp.zeros((1, GC), F32)
        for r in range(8):
            rows = slice(r * nk, (r + 1) * nk)
            ub = ui[rows, :]
            dyb = dyi[rows, :]
            gb = sb[rows, :].astype(BF16)
            dui[rows, :] = _dot(gb, bt_ref[...]) + d_ref[...] * dyb
            dbf = dbf + _dot_tn(ub.astype(BF16), gb)
            dcf = dcf + _dot_tn(dyb.astype(BF16), sa[rows, :].astype(BF16))
            dd = dd + jnp.sum(dyb * ub, axis=0, keepdims=True)
        db_ref[...] = jnp.dot(dbf * mask_ref[...], r_ref[...], precision=hi, preferred_element_type=F32)
        dc_ref[...] = jnp.dot(dcf * mask_ref[...], r_ref[...], precision=hi, preferred_element_type=F32)
        dd_ref[...] = dd
        _deinterleave(du_ref, dui, nk)

    cmp_spec = pl.BlockSpec((GC, 2 * S5_P), lambda g: (g, 0))
    whole = lambda shape: pl.BlockSpec(shape, lambda g: (0, 0))
    sd = jax.ShapeDtypeStruct
    return _hosting_call(
        body, "s5_bwd", S5_G // S5_GB, host,
        [u, dy, bm, bmt, cmt, ar, ai, dvec, mask, rmat],
        [col, col, bmat, cmat, bmat, avec, avec, vec, whole(mask.shape), whole(rmat.shape)],
        [sd(u.shape, F32), sd((S5_G * S5_C, 2 * S5_P), F32), sd((S5_G * S5_C, 2 * S5_P), F32),
         sd((1, PRIM), F32), sd((1, S5_G * S5_P), F32), sd((1, S5_G * S5_P), F32)],
        [col, cmp_spec, cmp_spec, vec, avec, avec],
        [pltpu.VMEM((L, 2 * W), F32), pltpu.VMEM((L, 2 * W), F32), pltpu.VMEM((8, 2 * W), F32),
         pltpu.VMEM((L, GC), F32), pltpu.VMEM((L, GC), F32), pltpu.VMEM((L, GC), F32)])


def _s5_mats(bbr, bbi, cre, cim):
    nb = S5_G // S5_GB
    eye = jnp.eye(S5_GB, dtype=F32)
    bb = jnp.stack([bbr, bbi], axis=2).reshape(nb, S5_GB, S5_C, 2, S5_P)
    bm = jnp.einsum('ngcrp,gh->ngcrhp', bb, eye).reshape(nb, S5_GB * S5_C, 2 * S5_W)
    cc = jnp.stack([cre, -cim], axis=2).reshape(nb, S5_GB, S5_C, 2, S5_P)
    cmt = jnp.einsum('ngcrp,gh->ngcrhp', cc, eye).reshape(nb, S5_GB * S5_C, 2 * S5_W)
    return (bm.astype(BF16), jnp.swapaxes(bm, 1, 2).astype(BF16),
            jnp.swapaxes(cmt, 1, 2).astype(BF16), cmt.astype(BF16))


def _s5_compact_consts():
    g_row = np.arange(S5_GB * S5_C) // S5_C
    col = np.arange(2 * S5_W)
    g_col = (col % S5_W) // S5_P
    mask = (g_row[:, None] == g_col[None, :]).astype(np.float32)
    tgt = (col // S5_W) * S5_P + col % S5_P
    rmat = (tgt[:, None] == np.arange(2 * S5_P)[None, :]).astype(np.float32)
    return jnp.asarray(mask), jnp.asarray(rmat)


def _attn_scores(q_ref, k_ref, qb, bq, scale):
    ext = (qb + 1) * bq
    s = _dot_nt(q_ref[qb * bq:ext, :], k_ref[0:ext, :]) * scale
    qpos = lax.broadcasted_iota(jnp.int32, (bq, bq), 0)
    kpos = lax.broadcasted_iota(jnp.int32, (bq, bq), 1)
    diag = jnp.where(kpos <= qpos, s[:, ext - bq:], NEG)
    return diag if qb == 0 else jnp.concatenate([s[:, :ext - bq], diag], axis=-1)


def _attn_fwd(qp, kp, v, scale):
    L = qp.shape[0]
    bq = min(256, L)

    def body(q_ref, k_ref, v_ref, o_ref, lse_ref):
        for qb in range(L // bq):
            rows = slice(qb * bq, (qb + 1) * bq)
            s = _attn_scores(q_ref, k_ref, qb, bq, scale)
            m = jnp.max(s, axis=-1, keepdims=True)
            e = jnp.exp(s - m)
            l = jnp.sum(e, axis=-1, keepdims=True)
            o_ref[rows, :] = _dot(e.astype(BF16), v_ref[0:(qb + 1) * bq, :]) / l
            lse_ref[rows, :] = jnp.broadcast_to(m + jnp.log(l), (bq, HD))

    blk = pl.BlockSpec((L, HD), lambda h: (0, h))
    wide = pl.BlockSpec((L, 2 * HD), lambda h: (0, h))
    return pl.pallas_call(
        body, name="mla_attn_fwd", grid=(MLA_H,),
        in_specs=[wide, wide, blk], out_specs=[blk, blk],
        out_shape=[jax.ShapeDtypeStruct((L, MLA_H * HD), F32)] * 2,
        compiler_params=pltpu.CompilerParams(dimension_semantics=("arbitrary",), vmem_limit_bytes=VMEM_LIMIT),
    )(qp, kp, v)


def _attn_bwd(qp, kp, v, o, lse, do, scale):
    L = qp.shape[0]
    bq = min(256, L)
    nq = L // bq

    def body(q_ref, k_ref, v_ref, o_ref, lse_ref, do_ref, dq_ref, dk_ref, dv_ref):
        dk_ref[...] = jnp.zeros_like(dk_ref)
        dv_ref[...] = jnp.zeros_like(dv_ref)
        for qb in range(nq):
            rows = slice(qb * bq, (qb + 1) * bq)
            ext = (qb + 1) * bq
            do = do_ref[rows, :]
            dob = do.astype(BF16)
            p = jnp.exp(_attn_scores(q_ref, k_ref, qb, bq, scale) - lse_ref[rows, 0:1])
            dp = _dot_nt(dob, v_ref[0:ext, :])
            dsum = jnp.sum(do * o_ref[rows, :], axis=-1, keepdims=True)
            ds = (p * (dp - dsum) * scale).astype(BF16)
            dq_ref[rows, :] = _dot(ds, k_ref[0:ext, :])
            dk_ref[0:ext, :] += _dot_tn(ds, q_ref[rows, :])
            dv_ref[0:ext, :] += _dot_tn(p.astype(BF16), dob)

    sd = jax.ShapeDtypeStruct
    blk = pl.BlockSpec((L, HD), lambda h: (0, h))
    wide = pl.BlockSpec((L, 2 * HD), lambda h: (0, h))
    return pl.pallas_call(
        body, name="mla_attn_bwd", grid=(MLA_H,),
        in_specs=[wide, wide, blk, blk, blk, blk], out_specs=[wide, wide, blk],
        out_shape=[sd((L, MLA_H * 2 * HD), F32), sd((L, MLA_H * 2 * HD), F32), sd((L, MLA_H * HD), F32)],
        compiler_params=pltpu.CompilerParams(dimension_semantics=("arbitrary",), vmem_limit_bytes=VMEM_LIMIT),
    )(qp, kp, v, o, lse, do)


def _kv_fn(mem, gm, w, gk):
    kv = _mm(_rms(mem, gm, D_MODEL), w)
    k = jnp.concatenate([_rms(kv[:, HD * h:HD * (h + 1)], gk, HD) for h in range(X_HEADS)], axis=-1)
    return k, kv[:, XQ:]


def _kv_prep(mem, gm, w, gk, name):
    def fn(mem, gm, w, gk):
        return _kv_fn(mem, gm, w, gk)
    M = mem.shape[0]
    return _rowwise(name, fn, [('c', mem), ('c', gm), ('c', w), ('c', gk)],
                    [('c', (M, XQ), F32), ('c', (M, XQ), F32)], 1)


def _kv_prep_bwd(mem, gm, w, gk, dk, dv, name):
    def fn(mem, gm, w, gk, dk, dv):
        _, vjp = jax.vjp(lambda a, b, c: _kv_fn(mem, a, b, c), gm, w, gk)
        return vjp((dk, dv))
    return _rowwise(name, fn, [('c', mem), ('c', gm), ('c', w), ('c', gk), ('c', dk), ('c', dv)],
                    [('c', gm.shape, F32), ('c', w.shape, BF16), ('c', gk.shape, F32)], 1)


def _forward_merge(x, mix, mix_kind, xq, gate, k, v, gq, wout, name, nblk, sub, host=None):
    def fn(x, mix, xq, gate, k, v, gq, wout):
        o = _merge(mix, xq, gate, k, v, gq)
        return (x + _dot(o.astype(BF16), wout),)
    L = x.shape[0]
    out = _rowwise(name, fn, [('r', x), (mix_kind, mix), ('r', xq), ('r', gate), ('c', k), ('c', v), ('c', gq),
                              ('c', wout)], [('r', (L, D_MODEL), F32)], nblk, sub, host=host)
    return out[0] if host is None else (out[0][0], out[1])


def _backward_merge(dx, mix, mix_kind, xq, gate, k, v, gq, wout, name, nblk, sub):
    def fn(dx, mix, xq, gate, k, v, gq, wout):
        g16 = dx.astype(BF16)
        do = _dot_nt(g16, wout)
        o, vjp = jax.vjp(_merge, mix, xq, gate, k, v, gq)
        dmix, dxq, dgate, dk, dv, dgq = vjp(do)
        return dmix, dxq, dgate, o, g16, dk, dv, dgq
    L = dx.shape[0]
    return _rowwise(
        name, fn,
        [('r', dx), (mix_kind, mix), ('r', xq), ('r', gate), ('c', k), ('c', v), ('c', gq), ('c', wout)],
        [('r', (L, PRIM), F32), ('r', (L, XQ), F32), ('r', (L, BRANCH), F32), ('t', (BRANCH, L), BF16),
         ('r', (L, D_MODEL), BF16), ('a', k.shape, F32), ('a', v.shape, F32), ('a', gq.shape, F32)], nblk, sub)


_MLA_IN = 3392
_MLA_IN_PAD = 3456


def _from_slots(g):
    _, k, n = g.shape
    return jnp.transpose(g, (1, 0, 2)).reshape(k, N_DEV * n)


def _to_slots(w):
    k = w.shape[0]
    return jnp.transpose(w.reshape(k, N_DEV, -1), (1, 0, 2))


def _uq_to_kernel(g):
    uq = _from_slots(g).reshape(Q_LORA, MLA_H, HD + ROPE)
    return jnp.concatenate([uq[:, :, :HD].reshape(Q_LORA, PRIM),
                            jnp.pad(uq[:, :, HD:], ((0, 0), (0, 0), (0, HD - ROPE))).reshape(Q_LORA, PRIM)], axis=1)


def _uq_from_kernel(d_w_q):
    uq = jnp.concatenate([d_w_q[:, :PRIM].reshape(Q_LORA, MLA_H, HD),
                          d_w_q[:, PRIM:].reshape(Q_LORA, MLA_H, HD)[:, :, :ROPE]], axis=2)
    return _to_slots(uq.reshape(Q_LORA, MLA_H * (HD + ROPE)))


def _mla_in_perm(w):
    return jnp.concatenate([w[:, :768], w[:, 832:], w[:, 768:832], jnp.zeros((w.shape[0], 64), w.dtype)], axis=1)


def _mla_in_unperm(w):
    return jnp.concatenate([w[:, :768], w[:, 3328:3392], w[:, 768:3328]], axis=1)


_SMALL = (("ln_gain", 2048), ("mem_norm", 2048), ("xq_norm", 256), ("xk_norm", 256), ("s5_lambda_re", 6144),
          ("s5_lambda_im", 6144), ("s5_log_step", 96), ("s5_b_re", 98304), ("s5_b_im", 98304), ("s5_c_re", 98304),
          ("s5_c_im", 98304), ("s5_d", 1536), ("mla_q_lora_norm", 512), ("mla_kv_lora_norm", 256),
          ("mla_q_nope_norm", 128), ("mla_k_nope_norm", 128), ("mla_q_rope_norm", 64), ("mla_k_rope_norm", 64))
_SMALL_ROWS = 432
_SMALL_OFF = {name: sum(n for _, n in _SMALL[:i]) for i, (name, _) in enumerate(_SMALL)}


def _pack_small(d):
    flat = jnp.concatenate([d[n].reshape(-1).astype(F32) for n, _ in _SMALL])
    return jnp.pad(flat, (0, _SMALL_ROWS * 1024 - flat.shape[0])).reshape(_SMALL_ROWS, 1024)


def _unpack_small(p, name, shape):
    off = _SMALL_OFF[name]
    return p.reshape(-1)[off:off + int(np.prod(shape))].reshape(shape)


_WEIGHTS = ('ln_gain', 'w_out', 'mem_norm', 'w_mem_kv', 'xq_norm', 'xk_norm', 's5_w_in', 's5_lambda_re',
            's5_lambda_im', 's5_log_step', 's5_b_re', 's5_b_im', 's5_c_re', 's5_c_im', 's5_d', 's5_w_glu', 'mla_w_in',
            'mla_q_lora_norm', 'mla_kv_lora_norm', 'mla_w_uq', 'mla_w_ukv', 'mla_q_nope_norm', 'mla_k_nope_norm',
            'mla_q_rope_norm', 'mla_k_rope_norm')
_BIG = ('w_out', 'w_mem_kv', 's5_w_in', 's5_w_glu', 'mla_w_in', 'mla_w_uq', 'mla_w_ukv')


def _pad128(g):
    return jnp.pad(g.reshape(1, -1), ((0, 0), (0, HD - g.shape[-1])))


def kernel(x, mem, positions, ln_gain, w_out, mem_norm, w_mem_kv, xq_norm, xk_norm, s5_w_in, s5_lambda_re, s5_lambda_im, s5_log_step, s5_b_re, s5_b_im, s5_c_re, s5_c_im, s5_d, s5_w_glu, mla_w_in, mla_q_lora_norm, mla_kv_lora_norm, mla_w_uq, mla_w_ukv, mla_q_nope_norm, mla_k_nope_norm, mla_q_rope_norm, mla_k_rope_norm, loss_target, m_ln_gain, m_w_out, m_mem_norm, m_w_mem_kv, m_xq_norm, m_xk_norm, m_s5_w_in, m_s5_lambda_re, m_s5_lambda_im, m_s5_log_step, m_s5_b_re, m_s5_b_im, m_s5_c_re, m_s5_c_im, m_s5_d, m_s5_w_glu, m_mla_w_in, m_mla_q_lora_norm, m_mla_kv_lora_norm, m_mla_w_uq, m_mla_w_ukv, m_mla_q_nope_norm, m_mla_k_nope_norm, m_mla_q_rope_norm, m_mla_k_rope_norm, v_ln_gain, v_w_out, v_mem_norm, v_w_mem_kv, v_xq_norm, v_xk_norm, v_s5_w_in, v_s5_lambda_re, v_s5_lambda_im, v_s5_log_step, v_s5_b_re, v_s5_b_im, v_s5_c_re, v_s5_c_im, v_s5_d, v_s5_w_glu, v_mla_w_in, v_mla_q_lora_norm, v_mla_kv_lora_norm, v_mla_w_uq, v_mla_w_ukv, v_mla_q_nope_norm, v_mla_k_nope_norm, v_mla_q_rope_norm, v_mla_k_rope_norm):
    weights = dict(ln_gain=ln_gain, w_out=w_out, mem_norm=mem_norm, w_mem_kv=w_mem_kv, xq_norm=xq_norm,
                   xk_norm=xk_norm, s5_w_in=s5_w_in, s5_lambda_re=s5_lambda_re, s5_lambda_im=s5_lambda_im,
                   s5_log_step=s5_log_step, s5_b_re=s5_b_re, s5_b_im=s5_b_im, s5_c_re=s5_c_re, s5_c_im=s5_c_im,
                   s5_d=s5_d, s5_w_glu=s5_w_glu, mla_w_in=mla_w_in, mla_q_lora_norm=mla_q_lora_norm,
                   mla_kv_lora_norm=mla_kv_lora_norm, mla_w_uq=mla_w_uq, mla_w_ukv=mla_w_ukv,
                   mla_q_nope_norm=mla_q_nope_norm, mla_k_nope_norm=mla_k_nope_norm,
                   mla_q_rope_norm=mla_q_rope_norm, mla_k_rope_norm=mla_k_rope_norm)
    m_in = dict(zip(_WEIGHTS, (m_ln_gain, m_w_out, m_mem_norm, m_w_mem_kv, m_xq_norm, m_xk_norm, m_s5_w_in,
                               m_s5_lambda_re, m_s5_lambda_im, m_s5_log_step, m_s5_b_re, m_s5_b_im, m_s5_c_re,
                               m_s5_c_im, m_s5_d, m_s5_w_glu, m_mla_w_in, m_mla_q_lora_norm, m_mla_kv_lora_norm,
                               m_mla_w_uq, m_mla_w_ukv, m_mla_q_nope_norm, m_mla_k_nope_norm, m_mla_q_rope_norm,
                               m_mla_k_rope_norm)))
    v_in = dict(zip(_WEIGHTS, (v_ln_gain, v_w_out, v_mem_norm, v_w_mem_kv, v_xq_norm, v_xk_norm, v_s5_w_in,
                               v_s5_lambda_re, v_s5_lambda_im, v_s5_log_step, v_s5_b_re, v_s5_b_im, v_s5_c_re,
                               v_s5_c_im, v_s5_d, v_s5_w_glu, v_mla_w_in, v_mla_q_lora_norm, v_mla_kv_lora_norm,
                               v_mla_w_uq, v_mla_w_ukv, v_mla_q_nope_norm, v_mla_k_nope_norm, v_mla_q_rope_norm,
                               v_mla_k_rope_norm)))

    x0 = x[0]
    mem0 = mem[0]
    target = loss_target[0]
    L = x0.shape[0]
    nblk, sub = 8, 1
    me = 4 * lax.axis_index("x") + 2 * lax.axis_index("y") + lax.axis_index("c")

    lora = jnp.pad(jnp.concatenate([mla_q_lora_norm, mla_kv_lora_norm], axis=1), ((0, 7), (0, HD - 96)))
    def gather(*shards):
        return _plan_all_gather([s.astype(BF16) for s in shards])

    (W_in_s5,) = _exchange_call(gather(s5_w_in[0]), "ag_s5_w_in")

    ln0, ln1 = ln_gain[0:1], ln_gain[1:2]
    gq0, gq1 = xq_norm[0:1], xq_norm[1:2]
    gk0, gk1 = xk_norm[0:1], xk_norm[1:2]
    gm0, gm1 = mem_norm[0:1], mem_norm[1:2]
    gqn, gkn = mla_q_nope_norm, mla_k_nope_norm
    gqr, gkr = _pad128(mla_q_rope_norm), _pad128(mla_k_rope_norm)

    lr3 = s5_lambda_re.reshape(S5_G, 1, S5_P)
    li3 = s5_lambda_im.reshape(S5_G, 1, S5_P)
    ls3 = s5_log_step.reshape(S5_G, 1, 1)
    btr = jnp.swapaxes(s5_b_re[0], 1, 2)
    bti = jnp.swapaxes(s5_b_im[0], 1, 2)
    a_r, a_i, bbr, bbi = _s5_params(lr3, li3, ls3, btr, bti)
    bm, bmt, cm, cmt = _s5_mats(bbr, bbi, s5_c_re[0], s5_c_im[0])
    a_r2 = a_r.reshape(1, S5_G * S5_P)
    a_i2 = a_i.reshape(1, S5_G * S5_P)
    cmask, rmat = _s5_compact_consts()

    half = ROPE // 2
    inv_freq = ROPE_THETA ** (-jnp.arange(half, dtype=F32) / half)
    invf = jnp.concatenate([inv_freq, inv_freq, jnp.zeros((HD - ROPE,), F32)]).reshape(1, HD)

    def rot_tables(pos, invf):
        ang = pos.astype(F32) * invf
        lane = lax.broadcasted_iota(jnp.int32, ang.shape, 1)
        c = jnp.where(lane < ROPE, jnp.cos(ang), 0.0)
        s = jnp.sin(ang)
        return c, jnp.where(lane < half, -s, 0.0), jnp.where((lane >= half) & (lane < ROPE), s, 0.0)

    tc, ts1, ts2 = _rowwise("rot_tables", rot_tables, [('r', positions.reshape(L, 1)), ('c', invf)],
                            [('r', (L, HD), F32)] * 3, nblk, sub)

    def in_s5(x, g, w):
        proj = _mm_slots(_rms(x, g, D_MODEL).astype(BF16), w)
        return proj[:, :PRIM], proj[:, PRIM:PRIM + XQ], proj[:, PRIM + XQ:]

    (u_s5, xq_a, gate_a), (G_mkv0, G_uq) = _rowwise(
        "s5_in", in_s5, [('r', x0), ('c', ln0), ('c', W_in_s5)],
        [('r', (L, PRIM), F32), ('r', (L, XQ), F32), ('r', (L, BRANCH), F32)], nblk, sub,
        host=gather(w_mem_kv[0], mla_w_uq[0]))
    (y_s5,), (W_glu, G_out0) = _s5_fwd(u_s5, bm, cm, a_r2, a_i2, s5_d, host=gather(s5_w_glu[0], w_out[0]))

    def glu(y, w):
        z = _mm_slots(_gelu(y).astype(BF16), w)
        return (z[:, :PRIM] * _sigmoid(z[:, PRIM:]),)

    (y2,), (G_in_mla,) = _rowwise("s5_glu", glu, [('r', y_s5), ('c', W_glu)], [('r', (L, PRIM), F32)], nblk, sub,
                                  host=gather(mla_w_in[0]))
    W_mkv0 = G_mkv0.reshape(D_MODEL, 2 * XQ)
    k_a, v_a = _kv_prep(mem0, gm0, W_mkv0, gk0, "kv_prep0")
    x1, (W_kv, G_mkv1, G_lora) = _forward_merge(
        x0, y2, 'r', xq_a, gate_a, k_a, v_a, gq0, G_out0.reshape(BRANCH, D_MODEL), "merge0", nblk, sub,
        host=_plan_all_gather([mla_w_ukv[0].astype(BF16), w_mem_kv[1].astype(BF16), lora]))
    W_in_mla = _mla_in_perm(_from_slots(G_in_mla))
    W_q = _uq_to_kernel(G_uq)
    g_qlora = G_lora[:, 0, :64].reshape(1, Q_LORA)
    g_kvlora = G_lora[:, 0, 64:96].reshape(1, KV_LORA)

    def in_mla(x, g, w):
        proj = _dot(_rms(x, g, D_MODEL).astype(BF16), w)
        return proj[:, :512], proj[:, 512:768], proj[:, 768:1280], proj[:, 1280:3328], proj[:, 3328:]

    (c_q, c_kv, xq_b, gate_b, krp), (G_out1,) = _rowwise(
        "mla_in", in_mla, [('r', x1), ('c', ln1), ('c', W_in_mla)],
        [('r', (L, Q_LORA), F32), ('r', (L, KV_LORA), F32), ('r', (L, XQ), F32), ('r', (L, BRANCH), F32),
         ('r', (L, HD), F32)], nblk, sub, host=gather(w_out[1]))
    W_out = (G_out0.reshape(BRANCH, D_MODEL), G_out1.reshape(BRANCH, D_MODEL))
    W_mkv = (W_mkv0, G_mkv1.reshape(D_MODEL, 2 * XQ))

    def qkv(c_q, c_kv, krp, tc, ts1, ts2, gql, gkvl, wq, wkv, gqn, gkn, gqr, gkr):
        q = _dot(_rms(c_q, gql, Q_LORA).astype(BF16), wq)
        kv = _mm_slots(_rms(c_kv, gkvl, KV_LORA).astype(BF16), wkv)
        kp, v = _kv_post(kv, krp, gkn, gkr, tc, ts1, ts2)
        return _q_post(q, gqn, gqr, tc, ts1, ts2), kp, v

    qkv_consts = [('c', g_qlora), ('c', g_kvlora), ('c', W_q), ('c', W_kv), ('c', gqn), ('c', gkn), ('c', gqr),
                  ('c', gkr)]
    q_pad, k_pad, v_h = _rowwise(
        "mla_qkv", qkv, [('r', c_q), ('r', c_kv), ('r', krp), ('r', tc), ('r', ts1), ('r', ts2)] + qkv_consts,
        [('r', (L, 2 * PRIM), BF16), ('r', (L, 2 * PRIM), BF16), ('r', (L, PRIM), BF16)], nblk, sub)
    scale = (HD + ROPE) ** -0.5
    attn, lse = _attn_fwd(q_pad, k_pad, v_h, scale)
    k_b, v_b = _kv_prep(mem0, gm1, W_mkv[1], gk1, "kv_prep1")
    x2 = _forward_merge(x1, attn, 'r', xq_b, gate_b, k_b, v_b, gq1, W_out[1], "merge1", nblk, sub)

    def loss_fn(y, t):
        err = y - t
        part = 0.5 * jnp.sum(jnp.sum(err * err, axis=-1, keepdims=True) * (1.0 / D_MODEL), axis=0, keepdims=True)
        return err * (1.0 / D_MODEL), jnp.broadcast_to(part, (1, HD))

    dx2, loss_part = _rowwise("loss", loss_fn, [('r', x2), ('r', target)],
                              [('r', (L, D_MODEL), F32), ('a', (1, HD), F32)], nblk, sub)

    dattn, dxq_b, dgate_b, o_b, g_b, dk_b, dv_b, dgq1 = _backward_merge(
        dx2, attn, 'r', xq_b, gate_b, k_b, v_b, gq1, W_out[1], "merge1_bwd", nblk, sub)
    dgm1, dW_mkv1, dgk1 = _kv_prep_bwd(mem0, gm1, W_mkv[1], gk1, dk_b, dv_b, "kv_prep1_bwd")
    dW_out1 = _matmul_tn(o_b, g_b, "dw_out1")
    dq_pad, dk_pad, dv_h = _attn_bwd(q_pad, k_pad, v_h, attn, lse, dattn, scale)

    def qkv_bwd(c_q, c_kv, krp, tc, ts1, ts2, dqp, dkp, dv, gql, gkvl, wq, wkv, gqn, gkn, gqr, gkr):
        cqn, vjp_qn = jax.vjp(lambda a, b: _rms(a, b, Q_LORA), c_q, gql)
        ckvn, vjp_kvn = jax.vjp(lambda a, b: _rms(a, b, KV_LORA), c_kv, gkvl)
        cqn16 = cqn.astype(BF16)
        ckvn16 = ckvn.astype(BF16)
        q = _dot(cqn16, wq)
        kv = _mm_slots(ckvn16, wkv)
        _, vjp_q = jax.vjp(lambda a, b, c: _q_post(a, b, c, tc, ts1, ts2), q, gqn, gqr)
        dq, dgqn, dgqr = vjp_q(dqp)
        _, vjp_kv = jax.vjp(lambda a, b, c, d: _kv_post(a, b, c, d, tc, ts1, ts2), kv, krp, gkn, gkr)
        dkv, dkrp, dgkn, dgkr = vjp_kv((dkp, dv))
        dq16 = dq.astype(BF16)
        dkv16 = dkv.astype(BF16)
        dc_q, dgql = vjp_qn(_dot_nt(dq16, wq))
        dc_kv, dgkvl = vjp_kvn(_mm_slots_nt(dkv16, wkv))
        return dc_q, dc_kv, dkrp, cqn16, dq16, ckvn16, dkv16, dgql, dgkvl, dgqn, dgkn, dgqr, dgkr

    (dc_q, dc_kv, dkrp, cqn16, dq16, ckvn16, dkv16, dgql, dgkvl, dgqn, dgkn, dgqr, dgkr) = _rowwise(
        "mla_qkv_bwd", qkv_bwd,
        [('r', c_q), ('r', c_kv), ('r', krp), ('r', tc), ('r', ts1), ('r', ts2), ('r', dq_pad), ('r', dk_pad),
         ('r', dv_h)] + qkv_consts,
        [('r', (L, Q_LORA), F32), ('r', (L, KV_LORA), F32), ('r', (L, HD), F32), ('t', (Q_LORA, L), BF16),
         ('r', (L, 2 * PRIM), BF16), ('t', (KV_LORA, L), BF16), ('r', (L, 2 * PRIM), BF16),
         ('a', (1, Q_LORA), F32), ('a', (1, KV_LORA), F32), ('a', (1, HD), F32), ('a', (1, HD), F32),
         ('a', (1, HD), F32), ('a', (1, HD), F32)], nblk, sub)
    dW_q = _matmul_tn(cqn16, dq16, "dw_uq")
    dW_kv = _matmul_tn_slots(ckvn16, dkv16, "dw_ukv")

    def in_bwd(x, dres, g, w, *dparts):
        dproj = jnp.concatenate(dparts, axis=-1).astype(BF16)
        xn, vjp = jax.vjp(lambda a, b: _rms(a, b, D_MODEL), x, g)
        dx, dg = vjp(_mm_slots_nt(dproj, w) if w.ndim == 3 else _dot_nt(dproj, w))
        return dx + dres, xn, dproj, dg

    dx1, xn1, dproj1, dln1 = _rowwise(
        "mla_in_bwd", in_bwd,
        [('r', x1), ('r', dx2), ('c', ln1), ('c', W_in_mla), ('r', dc_q), ('r', dc_kv), ('r', dxq_b), ('r', dgate_b),
         ('r', dkrp)],
        [('r', (L, D_MODEL), F32), ('t', (D_MODEL, L), BF16), ('r', (L, _MLA_IN_PAD), BF16), ('a', (1, D_MODEL), F32)],
        nblk, sub)
    dW_in_mla = _matmul_tn(xn1, dproj1, "dw_mla_in")

    dy2, dxq_a, dgate_a, o_a, g_a, dk_a, dv_a, dgq0 = _backward_merge(
        dx1, y2, 'r', xq_a, gate_a, k_a, v_a, gq0, W_out[0], "merge0_bwd", nblk, sub)
    dgm0, dW_mkv0, dgk0 = _kv_prep_bwd(mem0, gm0, W_mkv[0], gk0, dk_a, dv_a, "kv_prep0_bwd")
    dW_out0 = _matmul_tn(o_a, g_a, "dw_out0")

    def glu_bwd(y, dy2, w):
        h, vjp_h = jax.vjp(_gelu, y)
        h16 = h.astype(BF16)
        z = _mm_slots(h16, w)
        _, vjp_z = jax.vjp(lambda z: z[:, :PRIM] * _sigmoid(z[:, PRIM:]), z)
        dz16 = vjp_z(dy2)[0].astype(BF16)
        return vjp_h(_mm_slots_nt(dz16, w))[0], h16, dz16

    early = [dW_out1.reshape(N_DEV, 256, D_MODEL), dW_mkv1.reshape(N_DEV, 128, 2 * XQ),
             _to_slots(_mla_in_unperm(dW_in_mla)), _uq_from_kernel(dW_q), dW_kv,
             dW_out0.reshape(N_DEV, 256, D_MODEL), dW_mkv0.reshape(N_DEV, 128, 2 * XQ)]
    (dy_s5, h16, dz16), early_pair = _rowwise(
        "s5_glu_bwd", glu_bwd, [('r', y_s5), ('r', dy2), ('c', W_glu)],
        [('r', (L, PRIM), F32), ('t', (PRIM, L), BF16), ('r', (L, 2 * PRIM), BF16)], nblk, sub,
        host=_plan_pair(early))
    dW_glu = _matmul_tn_slots(h16, dz16, "dw_glu")
    early_t = _pair_add(early + [dW_glu], early_pair + list(_exchange_call(_plan_pair([dW_glu]), "rs_pair_glu")),
                        "rs_add_early")
    (du_s5, dbc, dcc, dd, dar, dai), early_recv = _s5_bwd(u_s5, dy_s5, bm, bmt, cmt, a_r2, a_i2, s5_d, cmask, rmat,
                                                          host=_plan_chips(early_t))
    dx0, xn0, dproj0, dln0 = _rowwise(
        "s5_in_bwd", in_bwd,
        [('r', x0), ('r', dx1), ('c', ln0), ('c', W_in_s5), ('r', du_s5), ('r', dxq_a),
         ('r', dgate_a)],
        [('r', (L, D_MODEL), F32), ('t', (D_MODEL, L), BF16), ('r', (L, 2 * BRANCH), BF16), ('a', (1, D_MODEL), F32)],
        nblk, sub)

    dbc4 = dbc.reshape(S5_G, S5_C, 2, S5_P)
    dcc4 = dcc.reshape(S5_G, S5_C, 2, S5_P)
    dlr, dli, dls, dbtr, dbti = _s5_params_bwd(
        lr3, li3, ls3, btr, bti, dar.reshape(S5_G, 1, S5_P), dai.reshape(S5_G, 1, S5_P), dbc4[:, :, 0], dbc4[:, :, 1])

    small_part = {
        "ln_gain": jnp.concatenate([dln0, dln1]), "mem_norm": jnp.concatenate([dgm0, dgm1]),
        "xq_norm": jnp.concatenate([dgq0, dgq1]), "xk_norm": jnp.concatenate([dgk0, dgk1]),
        "s5_lambda_re": dlr, "s5_lambda_im": dli, "s5_log_step": dls,
        "s5_b_re": jnp.swapaxes(dbtr, 1, 2), "s5_b_im": jnp.swapaxes(dbti, 1, 2),
        "s5_c_re": dcc4[:, :, 0], "s5_c_im": -dcc4[:, :, 1], "s5_d": dd,
        "mla_q_lora_norm": dgql, "mla_kv_lora_norm": dgkvl, "mla_q_nope_norm": dgqn, "mla_k_nope_norm": dgkn,
        "mla_q_rope_norm": dgqr[:, :ROPE], "mla_k_rope_norm": dgkr[:, :ROPE],
    }
    loss8 = jnp.pad(loss_part, ((0, 7), (0, 0)))
    dW_in_s5, (small_gath, loss_g) = _matmul_tn_slots(
        xn0, dproj0, "dw_s5_in", host=_plan_all_gather([_pack_small(small_part).astype(BF16), loss8]))

    late = [dW_in_s5]
    late_t = _pair_add(late, list(_exchange_call(_plan_pair(late), "rs_pair_late")), "rs_add_late")
    owners = [("w_out", 1), ("w_mem_kv", 1), ("mla_w_in", 0), ("mla_w_uq", 0), ("mla_w_ukv", 0), ("w_out", 0),
              ("w_mem_kv", 0), ("s5_w_glu", 0)]
    upd, late_recv = _updates_call(early_recv, [weights[n][i] for n, i in owners], [m_in[n][i] for n, i in owners],
                                   [v_in[n][i] for n, i in owners], "update_early", host=_plan_chips(late_t))
    owners.append(("s5_w_in", 0))
    upd.append(_sum_adamw(late_recv[0], s5_w_in[0], m_s5_w_in[0], v_s5_w_in[0], "update_s5_w_in"))
    grads, delta, new_m, new_v = {}, {}, {}, {}
    for n in _BIG:
        parts = [u for u, (o, _) in sorted(zip(upd, owners), key=lambda t: t[1][1]) if o == n]
        grads[n], delta[n], new_m[n], new_v[n] = (jnp.stack([p[j] for p in parts]) for j in range(4))

    def whole(name, a):
        if name == "mla_q_lora_norm":
            return lax.dynamic_update_slice(jnp.zeros((Q_LORA,), F32), a.reshape(-1), (me * 64,))
        if name == "mla_kv_lora_norm":
            return lax.dynamic_update_slice(jnp.zeros((KV_LORA,), F32), a.reshape(-1), (me * 32,))
        return a

    wp = _pack_small({n: whole(n, weights[n]) for n, _ in _SMALL})
    mp = _pack_small({n: whole(n, m_in[n]) for n, _ in _SMALL})
    vp = _pack_small({n: whole(n, v_in[n]) for n, _ in _SMALL})
    gs, ds, ms, vs, loss_sum = _small_update(small_gath, loss_g, wp, mp, vp, "small_update")
    loss = loss_sum[0, 0]

    for n, _ in _SMALL:
        shape = weights[n].shape
        if n == "mla_q_lora_norm":
            take = lambda p: lax.dynamic_slice(_unpack_small(p, n, (Q_LORA,)), (me * 64,), (64,)).reshape(shape)
        elif n == "mla_kv_lora_norm":
            take = lambda p: lax.dynamic_slice(_unpack_small(p, n, (KV_LORA,)), (me * 32,), (32,)).reshape(shape)
        else:
            take = lambda p: _unpack_small(p, n, shape)
        grads[n], delta[n], new_m[n], new_v[n] = take(gs), take(ds), take(ms), take(vs)
    return (loss, dx0[None], *[grads[n] for n in _WEIGHTS], *[delta[n] for n in _WEIGHTS],
            *[new_m[n] for n in _WEIGHTS], *[new_v[n] for n in _WEIGHTS])
```

```python
import functools
import math

import numpy as np
import jax
import jax.numpy as jnp
from jax import lax
from jax.experimental import pallas as pl
from jax.experimental.pallas import tpu as pltpu

F32 = jnp.float32
BF16 = jnp.bfloat16
EPS = 1e-6
NEG = float(np.finfo(np.float32).min)
MESH = pl.DeviceIdType.MESH

N_DEV = 8
D_MODEL = 1024
MEM_LEN = 256
XQ = 512
PRIM = 1536
BRANCH = 2048
X_HEADS = 4
HD = 128
S5_G = 96
S5_P = 64
S5_C = 16
S5_GB = 8
S5_W = S5_GB * S5_P
MLA_H = 12
ROPE = 64
Q_LORA = 512
KV_LORA = 256
ROPE_THETA = 10000.0

ADAM_LR = 0.001
ADAM_B1 = 0.9
ADAM_B2 = 0.999
ADAM_EPS = 1e-08
ADAM_WD = 0.01
ADAM_STEP = 10

VMEM_LIMIT = 56 * 1024 * 1024


def _dot(a, b):
    return jnp.dot(a, b, preferred_element_type=F32)


def _dot_nt(a, b):
    return lax.dot_general(a, b, (((1,), (1,)), ((), ())), preferred_element_type=F32)


def _dot_tn(a, b):
    return lax.dot_general(a, b, (((0,), (0,)), ((), ())), preferred_element_type=F32)


@jax.custom_vjp
def _mm(a, b):
    return _dot(a.astype(BF16), b.astype(BF16))


def _mm_fwd(a, b):
    return _mm(a, b), (a, b)


def _mm_bwd(res, g):
    a, b = res
    gb = g.astype(BF16)
    return _dot_nt(gb, b.astype(BF16)).astype(a.dtype), _dot_tn(a.astype(BF16), gb).astype(b.dtype)


_mm.defvjp(_mm_fwd, _mm_bwd)


@jax.custom_vjp
def _mm_nt(a, b):
    return _dot_nt(a.astype(BF16), b.astype(BF16))


def _mm_nt_fwd(a, b):
    return _mm_nt(a, b), (a, b)


def _mm_nt_bwd(res, g):
    a, b = res
    gb = g.astype(BF16)
    return _dot(gb, b.astype(BF16)).astype(a.dtype), _dot_tn(gb, a.astype(BF16)).astype(b.dtype)


_mm_nt.defvjp(_mm_nt_fwd, _mm_nt_bwd)


@jax.custom_vjp
def _softmax(s):
    m = jnp.max(s, axis=-1, keepdims=True)
    e = jnp.exp(s - m)
    return e / jnp.sum(e, axis=-1, keepdims=True)


def _softmax_fwd(s):
    p = _softmax(s)
    return p, p


def _softmax_bwd(p, g):
    return (p * (g - jnp.sum(p * g, axis=-1, keepdims=True)),)


_softmax.defvjp(_softmax_fwd, _softmax_bwd)


def _rms(x, g, n):
    ms = jnp.sum(x * x, axis=-1, keepdims=True) * (1.0 / n)
    return x * lax.rsqrt(ms + EPS) * g


def _sigmoid(x):
    return 1.0 / (1.0 + jnp.exp(-x))


def _silu(x):
    return x * _sigmoid(x)


def _gelu(x):
    c = math.sqrt(2.0 / math.pi)
    return 0.5 * x * (1.0 + jnp.tanh(c * (x + 0.044715 * (x * x * x))))


@jax.custom_vjp
def _rot(x, c, s1, s2):
    return x * c + pltpu.roll(x, 96, 1) * s1 + pltpu.roll(x, 32, 1) * s2


def _rot_fwd(x, c, s1, s2):
    return _rot(x, c, s1, s2), (c, s1, s2)


def _rot_bwd(res, g):
    c, s1, s2 = res
    dx = g * c + pltpu.roll(g * s1, 32, 1) + pltpu.roll(g * s2, 96, 1)
    return dx, jnp.zeros_like(c), jnp.zeros_like(s1), jnp.zeros_like(s2)


_rot.defvjp(_rot_fwd, _rot_bwd)


def _mem_attn(xq, k, v, gq):
    outs = []
    for h in range(X_HEADS):
        sl = slice(HD * h, HD * (h + 1))
        q = _rms(xq[:, sl], gq, HD)
        p = _softmax(_mm_nt(q, k[:, sl]) * (HD ** -0.5))
        outs.append(_mm(p, v[:, sl]))
    return jnp.concatenate(outs, axis=-1)


def _merge(mix, xq, gate, k, v, gq):
    return jnp.concatenate([mix, _mem_attn(xq, k, v, gq)], axis=-1) * _silu(gate)


def _q_post(q, gqn, gqr, c, s1, s2):
    pieces = []
    for h in range(MLA_H):
        pieces.append(_rms(q[:, HD * h:HD * (h + 1)], gqn, HD))
        pieces.append(_rot(_rms(q[:, PRIM + HD * h:PRIM + HD * (h + 1)], gqr, ROPE), c, s1, s2))
    return jnp.concatenate(pieces, axis=-1)


def _kv_post(kv, krp, gkn, gkr, c, s1, s2):
    kr = _rot(_rms(krp, gkr, ROPE), c, s1, s2)
    pieces, vals = [], []
    for h in range(MLA_H):
        pieces.append(_rms(kv[:, 2 * HD * h:2 * HD * h + HD], gkn, HD))
        pieces.append(kr)
        vals.append(kv[:, 2 * HD * h + HD:2 * HD * (h + 1)])
    return jnp.concatenate(pieces, axis=-1), jnp.concatenate(vals, axis=-1)


def _rowwise(name, fn, ins, outs, nblk, sub=1, host=None):
    n_in = len(ins)

    def spec(kind, shape):
        if kind == 'r':
            return pl.BlockSpec((shape[0] // nblk, shape[1]), lambda i: (i, 0))
        if kind == 't':
            return pl.BlockSpec((shape[0], shape[1] // nblk), lambda i: (0, i))
        zeros = (0,) * len(shape)
        return pl.BlockSpec(tuple(shape), lambda i: zeros)

    def body(*refs):
        i = pl.program_id(0)
        res = fn(*[r[...] for r in refs[:n_in]])
        for (kind, _, _), ref, val in zip(outs, refs[n_in:], res):
            if kind == 'a':
                @pl.when(i == 0)
                def _():
                    ref[...] = jnp.zeros_like(ref)
                ref[...] += val.astype(ref.dtype)
            elif kind == 't':
                ref[...] = val.astype(F32).T.astype(ref.dtype)
            else:
                ref[...] = val.astype(ref.dtype)

    res, hosted = _hosting_call(
        body, name, nblk, host, [a for _, a in ins], [spec(k, a.shape) for k, a in ins],
        [jax.ShapeDtypeStruct(tuple(s), d) for _, s, d in outs], [spec(k, s) for k, s, _ in outs], [])
    return res if host is None else (res, hosted)


def _matmul_tn(at, g, name, out_dtype=BF16):
    K, L = at.shape
    N = g.shape[1]
    tn = next(t for t in (512, 384, 256, 128) if N % t == 0)

    def body(a_ref, g_ref, o_ref):
        o_ref[...] = _dot(a_ref[...], g_ref[...]).astype(o_ref.dtype)

    return pl.pallas_call(
        body, name=name, grid=(N // tn,),
        in_specs=[pl.BlockSpec((K, L), lambda n: (0, 0)), pl.BlockSpec((L, tn), lambda n: (0, n))],
        out_specs=pl.BlockSpec((K, tn), lambda n: (0, n)),
        out_shape=jax.ShapeDtypeStruct((K, N), out_dtype),
        compiler_params=pltpu.CompilerParams(dimension_semantics=("arbitrary",), vmem_limit_bytes=VMEM_LIMIT),
    )(at, g)


def _matmul_tn_slots(at, g, name, host=None):
    K, L = at.shape
    n = g.shape[1] // N_DEV

    def body(a_ref, g_ref, o_ref):
        o_ref[...] = _dot(a_ref[...], g_ref[...]).astype(o_ref.dtype)

    res, hosted = _hosting_call(
        body, name, N_DEV, host, [at, g],
        [pl.BlockSpec((K, L), lambda d: (0, 0)), pl.BlockSpec((L, n), lambda d: (0, d))],
        [jax.ShapeDtypeStruct((N_DEV, K, n), BF16)], [pl.BlockSpec((None, K, n), lambda d: (d, 0, 0))], [])
    return res[0] if host is None else (res[0], hosted)


def _mm_slots(a16, w):
    return jnp.concatenate([_dot(a16, w[d]) for d in range(N_DEV)], axis=-1)


def _mm_slots_nt(g16, w):
    n = w.shape[2]
    out = _dot_nt(g16[:, 0:n], w[0])
    for d in range(1, N_DEV):
        out = out + _dot_nt(g16[:, d * n:(d + 1) * n], w[d])
    return out


class _Exchange:
    def __init__(self, ins, outs, scratch, start, finish):
        self.ins, self.outs, self.scratch, self.start, self.finish = ins, outs, scratch, start, finish


def _xyc():
    return lax.axis_index("x"), lax.axis_index("y"), lax.axis_index("c")


def _plan_all_gather(xs):
    n = len(xs)

    def build(x_refs, out_refs, sems):
        send_sems, recv_sems, local_sems = sems
        x, y, c = _xyc()

        def copies(k, block, to, own=False):
            slot = 4 * block[0] + 2 * block[1] + block[2]
            return [pltpu.make_async_remote_copy(
                src_ref=x_refs[a] if own else out_refs[a].at[slot], dst_ref=out_refs[a].at[slot],
                send_sem=send_sems.at[k * n + a], recv_sem=recv_sems.at[k * n + a], device_id=to,
                device_id_type=MESH) for a in range(n)]

        mine = [pltpu.make_async_copy(x_refs[a], out_refs[a].at[4 * x + 2 * y + c], local_sems.at[a])
                for a in range(n)]
        return copies, mine, (x, y, c), [(1 - x, y), (x, 1 - y), (1 - x, 1 - y)]

    def first_copies(copies, me, chips):
        x, y, c = me
        first = copies(0, me, (x, y, 1 - c), own=True)
        for j, chip in enumerate(chips):
            first += copies(1 + j, me, (*chip, c), own=True)
        return first

    def start(x_refs, out_refs, sems):
        copies, mine, me, chips = build(x_refs, out_refs, sems)
        for cp in mine + first_copies(copies, me, chips):
            cp.start()

    def finish(x_refs, out_refs, sems):
        copies, mine, me, chips = build(x_refs, out_refs, sems)
        x, y, c = me
        passed = []
        for j, chip in enumerate(chips):
            for cp in copies(1 + j, (*chip, c), me):
                cp.wait_recv()
            fwd = copies(4 + j, (*chip, c), (x, y, 1 - c))
            for cp in fwd:
                cp.start()
            passed += fwd
        for cp in copies(0, (x, y, 1 - c), me):
            cp.wait_recv()
        for j, chip in enumerate(chips):
            for cp in copies(4 + j, (*chip, 1 - c), me):
                cp.wait_recv()
        for cp in first_copies(copies, me, chips) + passed:
            cp.wait_send()
        for cp in mine:
            cp.wait()

    return _Exchange(list(xs), [jax.ShapeDtypeStruct((N_DEV,) + a.shape, a.dtype) for a in xs],
                     [pltpu.SemaphoreType.DMA((7 * n,)), pltpu.SemaphoreType.DMA((7 * n,)),
                      pltpu.SemaphoreType.DMA((n,))], start, finish)


_CHIPS = ((0, 0), (0, 1), (1, 0), (1, 1))


def _plan_pair(sends):
    n = len(sends)

    def build(s_refs, o_refs, sems):
        send_sems, recv_sems = sems
        x, y, c = _xyc()
        return [pltpu.make_async_remote_copy(
            src_ref=s_refs[a].at[4 * px + 2 * py + 1 - c], dst_ref=o_refs[a].at[j],
            send_sem=send_sems.at[j * n + a], recv_sem=recv_sems.at[j * n + a], device_id=(x, y, 1 - c),
            device_id_type=MESH) for j, (px, py) in enumerate(_CHIPS) for a in range(n)]

    def start(s_refs, o_refs, sems):
        for cp in build(s_refs, o_refs, sems):
            cp.start()

    def finish(s_refs, o_refs, sems):
        for cp in build(s_refs, o_refs, sems):
            cp.wait_recv()
            cp.wait_send()

    return _Exchange(list(sends), [jax.ShapeDtypeStruct((4,) + a.shape[1:], a.dtype) for a in sends],
                     [pltpu.SemaphoreType.DMA((4 * n,)), pltpu.SemaphoreType.DMA((4 * n,))], start, finish)


def _plan_chips(ts):
    n = len(ts)
    flips = ((1, 0), (0, 1), (1, 1))

    def build(t_refs, o_refs, sems):
        send_sems, recv_sems, local_sems = sems
        x, y, c = _xyc()
        mine = 2 * x + y
        local = [pltpu.make_async_copy(t_refs[a].at[mine], o_refs[a].at[mine], local_sems.at[a]) for a in range(n)]
        remote = []
        for k, (fx, fy) in enumerate(flips):
            px = 1 - x if fx else x
            py = 1 - y if fy else y
            remote += [pltpu.make_async_remote_copy(
                src_ref=t_refs[a].at[2 * px + py], dst_ref=o_refs[a].at[mine],
                send_sem=send_sems.at[k * n + a], recv_sem=recv_sems.at[k * n + a], device_id=(px, py, c),
                device_id_type=MESH) for a in range(n)]
        return local, remote

    def start(t_refs, o_refs, sems):
        local, remote = build(t_refs, o_refs, sems)
        for cp in local + remote:
            cp.start()

    def finish(t_refs, o_refs, sems):
        local, remote = build(t_refs, o_refs, sems)
        for cp in remote:
            cp.wait_recv()
        for cp in remote:
            cp.wait_send()
        for cp in local:
            cp.wait()

    return _Exchange(list(ts), [jax.ShapeDtypeStruct(a.shape, a.dtype) for a in ts],
                     [pltpu.SemaphoreType.DMA((3 * n,)), pltpu.SemaphoreType.DMA((3 * n,)),
                      pltpu.SemaphoreType.DMA((n,))], start, finish)


def _exchange_call(plan, name):
    n = len(plan.ins)

    def body(*refs):
        ins, outs, sems = refs[:n], refs[n:2 * n], refs[2 * n:]
        plan.start(ins, outs, sems)
        plan.finish(ins, outs, sems)

    return pl.pallas_call(
        body, name=name, out_shape=plan.outs,
        in_specs=[pl.BlockSpec(memory_space=pl.ANY)] * n, out_specs=[pl.BlockSpec(memory_space=pl.ANY)] * n,
        scratch_shapes=plan.scratch,
    )(*plan.ins)


def _pair_add(sends, fromsib, name):
    n = len(sends)
    nb = 8

    def body(*refs):
        c = lax.axis_index("c")
        for a in range(n):
            s_ref, f_ref, t_ref = refs[a], refs[n + a], refs[2 * n + a]
            for j in range(4):
                t_ref[j] = (s_ref[2 * j + c].astype(F32) + f_ref[j].astype(F32)).astype(t_ref.dtype)

    def spec(a, lead):
        return pl.BlockSpec((lead, a.shape[1] // nb, a.shape[2]), lambda i: (0, i, 0))

    return pl.pallas_call(
        body, name=name, grid=(nb,),
        in_specs=[spec(a, N_DEV) for a in sends] + [spec(a, 4) for a in fromsib],
        out_specs=[spec(a, 4) for a in fromsib],
        out_shape=[jax.ShapeDtypeStruct(a.shape, a.dtype) for a in fromsib],
        compiler_params=pltpu.CompilerParams(dimension_semantics=("arbitrary",), vmem_limit_bytes=VMEM_LIMIT),
    )(*sends, *fromsib)


def _adamw_vals(w, g, m, v):
    m2 = ADAM_B1 * m + (1.0 - ADAM_B1) * g
    v2 = ADAM_B2 * v + (1.0 - ADAM_B2) * (g * g)
    m_hat = m2 / (1.0 - ADAM_B1 ** ADAM_STEP)
    v_hat = v2 / (1.0 - ADAM_B2 ** ADAM_STEP)
    delta = -ADAM_LR * (m_hat / (jnp.sqrt(v_hat) + ADAM_EPS) + ADAM_WD * w)
    return delta, m2, v2


def _sum_adamw(recv, w, m, v, name):
    R, C = w.shape
    ns = recv.shape[0]
    br = next((t for t in (256, 128, 64, 32, 16) if R % t == 0), R)

    def body(r_ref, w_ref, m_ref, v_ref, g_ref, d_ref, m2_ref, v2_ref):
        g = r_ref[0].astype(F32)
        for d in range(1, ns):
            g = g + r_ref[d].astype(F32)
        dl, m2, v2 = _adamw_vals(w_ref[...], g, m_ref[...], v_ref[...])
        g_ref[...] = g
        d_ref[...] = dl
        m2_ref[...] = m2
        v2_ref[...] = v2

    spec = pl.BlockSpec((br, C), lambda i: (i, 0))
    return pl.pallas_call(
        body, name=name, grid=(R // br,),
        in_specs=[pl.BlockSpec((ns, br, C), lambda i: (0, i, 0)), spec, spec, spec], out_specs=[spec] * 4,
        out_shape=[jax.ShapeDtypeStruct((R, C), F32)] * 4,
        compiler_params=pltpu.CompilerParams(dimension_semantics=("arbitrary",)),
    )(recv, w, m, v)


def _updates_call(recvs, ws, ms, vs, name, host=None):
    n = len(recvs)
    nb = 8

    def body(*refs):
        for a in range(n):
            r_ref, w_ref, m_ref, v_ref = refs[a], refs[n + a], refs[2 * n + a], refs[3 * n + a]
            g_ref, d_ref, m2_ref, v2_ref = refs[4 * n + 4 * a:4 * n + 4 * a + 4]
            g = r_ref[0].astype(F32)
            for d in range(1, r_ref.shape[0]):
                g = g + r_ref[d].astype(F32)
            dl, m2, v2 = _adamw_vals(w_ref[...], g, m_ref[...], v_ref[...])
            g_ref[...] = g
            d_ref[...] = dl
            m2_ref[...] = m2
            v2_ref[...] = v2

    def spec2(w):
        return pl.BlockSpec((w.shape[0] // nb, w.shape[1]), lambda i: (i, 0))

    def spec3(r):
        return pl.BlockSpec((r.shape[0], r.shape[1] // nb, r.shape[2]), lambda i: (0, i, 0))

    res, hosted = _hosting_call(
        body, name, nb, host, list(recvs) + list(ws) + list(ms) + list(vs),
        [spec3(r) for r in recvs] + [spec2(w) for w in ws] * 3,
        [jax.ShapeDtypeStruct(w.shape, F32) for w in ws for _ in range(4)],
        [spec2(w) for w in ws for _ in range(4)], [])
    return [res[4 * a:4 * a + 4] for a in range(n)], hosted


def _small_sum(gath, loss_g, name):
    _, R, C = gath.shape
    br = R // 3

    def body(g_ref, l_ref, go_ref, lo_ref):
        g = g_ref[0].astype(F32)
        lsum = l_ref[0]
        for d in range(1, N_DEV):
            g = g + g_ref[d].astype(F32)
            lsum = lsum + l_ref[d]
        go_ref[...] = g
        lo_ref[...] = lsum

    return pl.pallas_call(
        body, name=name, grid=(R // br,),
        in_specs=[pl.BlockSpec((N_DEV, br, C), lambda i: (0, i, 0)),
                  pl.BlockSpec((N_DEV, 8, HD), lambda i: (0, 0, 0))],
        out_specs=[pl.BlockSpec((br, C), lambda i: (i, 0)), pl.BlockSpec((8, HD), lambda i: (0, 0))],
        out_shape=[jax.ShapeDtypeStruct((R, C), F32), jax.ShapeDtypeStruct((8, HD), F32)],
        compiler_params=pltpu.CompilerParams(dimension_semantics=("arbitrary",)),
    )(gath, loss_g)


def _adamw_multi(ws, gs, ms, vs, name, nblk=1):
    n = len(ws)

    def body(*refs):
        for a in range(n):
            dl, m2, v2 = _adamw_vals(refs[a][...], refs[n + a][...], refs[2 * n + a][...], refs[3 * n + a][...])
            refs[4 * n + 3 * a][...] = dl
            refs[4 * n + 3 * a + 1][...] = m2
            refs[4 * n + 3 * a + 2][...] = v2

    def spec(x):
        rest = (0,) * (x.ndim - 1)
        return pl.BlockSpec((x.shape[0] // nblk,) + tuple(x.shape[1:]), lambda i: (i,) + rest)

    res = pl.pallas_call(
        body, name=name, grid=(nblk,),
        in_specs=[spec(w) for w in ws] * 4, out_specs=[spec(w) for w in ws for _ in range(3)],
        out_shape=[jax.ShapeDtypeStruct(w.shape, F32) for w in ws for _ in range(3)],
        compiler_params=pltpu.CompilerParams(dimension_semantics=("arbitrary",), vmem_limit_bytes=VMEM_LIMIT),
    )(*ws, *gs, *ms, *vs)
    return [res[3 * a:3 * a + 3] for a in range(n)]


def _s5_param_fn(lr, li, ls, btr, bti):
    step = jnp.exp(ls)
    er = jnp.exp(lr * step)
    ang = li * step
    ar = er * jnp.cos(ang)
    ai = er * jnp.sin(ang)
    nr = ar - 1.0
    den = lr * lr + li * li
    fr = (nr * lr + ai * li) / den
    fi = (ai * lr - nr * li) / den
    return ar, ai, fr * btr - fi * bti, fr * bti + fi * btr


def _s5_params(lr, li, ls, btr, bti):
    def body(lr_ref, li_ref, ls_ref, br_ref, bi_ref, ar_ref, ai_ref, bbr_ref, bbi_ref):
        ar, ai, bbr, bbi = _s5_param_fn(lr_ref[...], li_ref[...], ls_ref[...], br_ref[...], bi_ref[...])
        ar_ref[...] = ar
        ai_ref[...] = ai
        bbr_ref[...] = bbr
        bbi_ref[...] = bbi

    sd = jax.ShapeDtypeStruct
    return pl.pallas_call(
        body, name="s5_params",
        out_shape=[sd(lr.shape, F32), sd(lr.shape, F32), sd(btr.shape, F32), sd(btr.shape, F32)],
    )(lr, li, ls, btr, bti)


def _s5_params_bwd(lr, li, ls, btr, bti, dar, dai, dbbr, dbbi):
    def body(lr_ref, li_ref, ls_ref, br_ref, bi_ref, dar_ref, dai_ref, dbbr_ref, dbbi_ref,
             dlr_ref, dli_ref, dls_ref, dbr_ref, dbi_ref):
        _, vjp = jax.vjp(_s5_param_fn, lr_ref[...], li_ref[...], ls_ref[...], br_ref[...], bi_ref[...])
        dlr, dli, dls, dbr, dbi = vjp((dar_ref[...], dai_ref[...], dbbr_ref[...], dbbi_ref[...]))
        dlr_ref[...] = dlr
        dli_ref[...] = dli
        dls_ref[...] = dls
        dbr_ref[...] = dbr
        dbi_ref[...] = dbi

    sd = jax.ShapeDtypeStruct
    return pl.pallas_call(
        body, name="s5_params_bwd",
        out_shape=[sd(lr.shape, F32), sd(lr.shape, F32), sd(ls.shape, F32), sd(btr.shape, F32), sd(btr.shape, F32)],
    )(lr, li, ls, btr, bti, dar, dai, dbbr, dbbi)


def _cpow(ar, ai, n):
    assert n & (n - 1) == 0
    while n > 1:
        ar, ai = ar * ar - ai * ai, 2.0 * ar * ai
        n //= 2
    return ar, ai


def _scan(st, cr, ci, init, nk, reverse, store, prev=None):
    W = S5_W

    def step(j, carry):
        k = nk - 1 - j if reverse else j
        rows = pl.ds(pl.multiple_of(k * 8, 8), 8)
        sr, si = carry[0], carry[1]
        nsr = cr * sr - ci * si + st[rows, 0:W]
        nsi = cr * si + ci * sr + st[rows, W:2 * W]
        if store:
            st[rows, 0:W] = nsr
            st[rows, W:2 * W] = nsi
        if prev is None:
            return nsr, nsi
        prows = pl.ds(pl.multiple_of(jnp.maximum(k - 1, 0) * 8, 8), 8)
        w = jnp.where(k > 0, 1.0, 0.0).astype(F32)
        pr = prev[prows, 0:W] * w
        pi = prev[prows, W:2 * W] * w
        return nsr, nsi, carry[2] + nsr * pr + nsi * pi, carry[3] + nsi * pr - nsr * pi

    return lax.fori_loop(0, nk, step, init, unroll=2)


def _chain(fin, fr, fi, pr, pi, reverse):
    W = S5_W
    fin[:, 0:W] = fr
    fin[:, W:2 * W] = fi
    rowid = lax.broadcasted_iota(jnp.int32, (8, W), 0)
    cr = jnp.zeros((1, W), F32)
    ci = jnp.zeros((1, W), F32)
    init_r = jnp.zeros((8, W), F32)
    init_i = jnp.zeros((8, W), F32)
    for s in (range(7, -1, -1) if reverse else range(8)):
        init_r = jnp.where(rowid == s, cr, init_r)
        init_i = jnp.where(rowid == s, ci, init_i)
        lr = fin[s:s + 1, 0:W]
        li = fin[s:s + 1, W:2 * W]
        cr, ci = lr + pr * cr - pi * ci, li + pr * ci + pi * cr
    return init_r, init_i


def _full_scan(st, fin, ar, ai, nk, reverse, prev=None):
    W = S5_W
    cr = jnp.broadcast_to(ar, (8, W))
    ci = jnp.broadcast_to(-ai if reverse else ai, (8, W))
    z = jnp.zeros((8, W), F32)
    fr, fi = _scan(st, cr, ci, (z, z), nk, reverse, store=False)
    pr, pi = _cpow(ar, -ai if reverse else ai, nk)
    init = _chain(fin, fr, fi, pr, pi, reverse)
    if prev is None:
        return _scan(st, cr, ci, init, nk, reverse, store=True)
    return _scan(st, cr, ci, init + (z, z), nk, reverse, store=True, prev=prev)


def _s5_specs(L):
    W2 = 2 * S5_W
    GC = S5_GB * S5_C
    col = pl.BlockSpec((L, GC), lambda g: (0, g))
    vec = pl.BlockSpec((1, GC), lambda g: (0, g))
    avec = pl.BlockSpec((1, S5_W), lambda g: (0, g))
    bmat = pl.BlockSpec((None, GC, W2), lambda g: (g, 0, 0))
    cmat = pl.BlockSpec((None, W2, GC), lambda g: (g, 0, 0))
    return col, vec, avec, bmat, cmat


def _interleave(dst, src, nk):
    for s in range(8):
        dst[pl.ds(s, nk, stride=8), :] = src[s * nk:(s + 1) * nk, :]


def _deinterleave(dst, src, nk):
    for s in range(8):
        dst[s * nk:(s + 1) * nk, :] = src[pl.ds(s, nk, stride=8), :]


def _hosting_call(body, name, nsteps, host, ins, in_specs, outs, out_specs, scratch):
    grid = (nsteps,) if isinstance(nsteps, int) else tuple(nsteps)
    params = pltpu.CompilerParams(dimension_semantics=("arbitrary",) * len(grid), vmem_limit_bytes=VMEM_LIMIT)
    if host is None:
        res = pl.pallas_call(
            body, name=name, grid=grid, in_specs=in_specs, out_specs=out_specs, out_shape=outs,
            scratch_shapes=scratch, compiler_params=params,
        )(*ins)
        return list(res), []
    n_in, n_out, n_sc = len(ins), len(outs), len(scratch)
    h_in, h_out = len(host.ins), len(host.outs)

    def hosted(*refs):
        a = refs[:n_in]
        ha = refs[n_in:n_in + h_in]
        o = refs[n_in + h_in:n_in + h_in + n_out]
        ho = refs[n_in + h_in + n_out:n_in + h_in + n_out + h_out]
        sc = refs[n_in + h_in + n_out + h_out:n_in + h_in + n_out + h_out + n_sc]
        hs = refs[n_in + h_in + n_out + h_out + n_sc:]
        first = functools.reduce(jnp.logical_and, [pl.program_id(i) == 0 for i in range(len(grid))])
        last = functools.reduce(jnp.logical_and, [pl.program_id(i) == g - 1 for i, g in enumerate(grid)])

        @pl.when(first)
        def _():
            host.start(ha, ho, hs)

        body(*a, *o, *sc)

        @pl.when(last)
        def _():
            host.finish(ha, ho, hs)

    hbm = pl.BlockSpec(memory_space=pl.ANY)
    res = pl.pallas_call(
        hosted, name=name, grid=grid,
        in_specs=list(in_specs) + [hbm] * h_in, out_specs=list(out_specs) + [hbm] * h_out,
        out_shape=list(outs) + list(host.outs), scratch_shapes=list(scratch) + list(host.scratch),
        compiler_params=params,
    )(*ins, *host.ins)
    return list(res[:n_out]), list(res[n_out:])


def _s5_fwd(u, bm, cm, ar, ai, dvec, host=None):
    L = u.shape[0]
    nk = L // 8
    GC = S5_GB * S5_C
    col, vec, avec, bmat, cmat = _s5_specs(L)

    def body(u_ref, b_ref, c_ref, ar_ref, ai_ref, d_ref, y_ref, st, fin, ui, yi):
        _interleave(ui, u_ref, nk)
        for r in range(8):
            rows = slice(r * nk, (r + 1) * nk)
            st[rows, :] = _dot(ui[rows, :].astype(BF16), b_ref[...])
        _full_scan(st, fin, ar_ref[...], ai_ref[...], nk, reverse=False)
        for r in range(8):
            rows = slice(r * nk, (r + 1) * nk)
            yi[rows, :] = _dot(st[rows, :].astype(BF16), c_ref[...]) + d_ref[...] * ui[rows, :]
        _deinterleave(y_ref, yi, nk)

    return _hosting_call(
        body, "s5_fwd", S5_G // S5_GB, host,
        [u, bm, cm, ar, ai, dvec], [col, bmat, cmat, avec, avec, vec],
        [jax.ShapeDtypeStruct(u.shape, F32)], [col],
        [pltpu.VMEM((L, 2 * S5_W), F32), pltpu.VMEM((8, 2 * S5_W), F32), pltpu.VMEM((L, GC), F32),
         pltpu.VMEM((L, GC), F32)])


def _s5_bwd(u, dy, bm, bmt, cmt, ar, ai, dvec, mask, rmat, host=None):
    L = u.shape[0]
    nk = L // 8
    W = S5_W
    GC = S5_GB * S5_C
    col, vec, avec, bmat, cmat = _s5_specs(L)
    hi = lax.Precision.HIGHEST

    def body(u_ref, dy_ref, b_ref, bt_ref, ct_ref, ar_ref, ai_ref, d_ref, mask_ref, r_ref,
             du_ref, db_ref, dc_ref, dd_ref, dar_ref, dai_ref, sa, sb, fin, ui, dyi, dui):
        ar = ar_ref[...]
        ai = ai_ref[...]
        _interleave(ui, u_ref, nk)
        _interleave(dyi, dy_ref, nk)
        for r in range(8):
            rows = slice(r * nk, (r + 1) * nk)
            sa[rows, :] = _dot(ui[rows, :].astype(BF16), b_ref[...])
            sb[rows, :] = _dot(dyi[rows, :].astype(BF16), ct_ref[...])
        _full_scan(sa, fin, ar, ai, nk, reverse=False)
        gr, gi, accr, acci = _full_scan(sb, fin, ar, ai, nk, reverse=True, prev=sa)
        rowid = lax.broadcasted_iota(jnp.int32, (8, W), 0)
        last = pl.ds((nk - 1) * 8, 8)
        pr = jnp.where(rowid == 0, 0.0, pltpu.roll(sa[last, 0:W], 1, 0))
        pi = jnp.where(rowid == 0, 0.0, pltpu.roll(sa[last, W:2 * W], 1, 0))
        accr = accr + gr * pr + gi * pi
        acci = acci + gi * pr - gr * pi
        dar_ref[...] = jnp.sum(accr, axis=0, keepdims=True)
        dai_ref[...] = jnp.sum(acci, axis=0, keepdims=True)
        dbf = jnp.zeros((GC, 2 * W), F32)
        dcf = jnp.zeros((GC, 2 * W), F32)
        dd = jnp.zeros((1, GC), F32)
        for r in range(8):
            rows = slice(r * nk, (r + 1) * nk)
            ub = ui[rows, :]
            dyb = dyi[rows, :]
            gb = sb[rows, :].astype(BF16)
            dui[rows, :] = _dot(gb, bt_ref[...]) + d_ref[...] * dyb
            dbf = dbf + _dot_tn(ub.astype(BF16), gb)
            dcf = dcf + _dot_tn(dyb.astype(BF16), sa[rows, :].astype(BF16))
            dd = dd + jnp.sum(dyb * ub, axis=0, keepdims=True)
        db_ref[...] = jnp.dot(dbf * mask_ref[...], r_ref[...], precision=hi, preferred_element_type=F32)
        dc_ref[...] = jnp.dot(dcf * mask_ref[...], r_ref[...], precision=hi, preferred_element_type=F32)
        dd_ref[...] = dd
        _deinterleave(du_ref, dui, nk)

    cmp_spec = pl.BlockSpec((GC, 2 * S5_P), lambda g: (g, 0))
    whole = lambda shape: pl.BlockSpec(shape, lambda g: (0, 0))
    sd = jax.ShapeDtypeStruct
    return _hosting_call(
        body, "s5_bwd", S5_G // S5_GB, host,
        [u, dy, bm, bmt, cmt, ar, ai, dvec, mask, rmat],
        [col, col, bmat, cmat, bmat, avec, avec, vec, whole(mask.shape), whole(rmat.shape)],
        [sd(u.shape, F32), sd((S5_G * S5_C, 2 * S5_P), F32), sd((S5_G * S5_C, 2 * S5_P), F32),
         sd((1, PRIM), F32), sd((1, S5_G * S5_P), F32), sd((1, S5_G * S5_P), F32)],
        [col, cmp_spec, cmp_spec, vec, avec, avec],
        [pltpu.VMEM((L, 2 * W), F32), pltpu.VMEM((L, 2 * W), F32), pltpu.VMEM((8, 2 * W), F32),
         pltpu.VMEM((L, GC), F32), pltpu.VMEM((L, GC), F32), pltpu.VMEM((L, GC), F32)])


def _s5_mats(bbr, bbi, cre, cim):
    nb = S5_G // S5_GB
    eye = jnp.eye(S5_GB, dtype=F32)
    bb = jnp.stack([bbr, bbi], axis=2).reshape(nb, S5_GB, S5_C, 2, S5_P)
    bm = jnp.einsum('ngcrp,gh->ngcrhp', bb, eye).reshape(nb, S5_GB * S5_C, 2 * S5_W)
    cc = jnp.stack([cre, -cim], axis=2).reshape(nb, S5_GB, S5_C, 2, S5_P)
    cmt = jnp.einsum('ngcrp,gh->ngcrhp', cc, eye).reshape(nb, S5_GB * S5_C, 2 * S5_W)
    return (bm.astype(BF16), jnp.swapaxes(bm, 1, 2).astype(BF16),
            jnp.swapaxes(cmt, 1, 2).astype(BF16), cmt.astype(BF16))


def _s5_compact_consts():
    g_row = np.arange(S5_GB * S5_C) // S5_C
    col = np.arange(2 * S5_W)
    g_col = (col % S5_W) // S5_P
    mask = (g_row[:, None] == g_col[None, :]).astype(np.float32)
    tgt = (col // S5_W) * S5_P + col % S5_P
    rmat = (tgt[:, None] == np.arange(2 * S5_P)[None, :]).astype(np.float32)
    return jnp.asarray(mask), jnp.asarray(rmat)


def _attn_scores(q_ref, k_ref, qb, bq, scale):
    ext = (qb + 1) * bq
    s = _dot_nt(q_ref[qb * bq:ext, :], k_ref[0:ext, :]) * scale
    qpos = lax.broadcasted_iota(jnp.int32, (bq, bq), 0)
    kpos = lax.broadcasted_iota(jnp.int32, (bq, bq), 1)
    diag = jnp.where(kpos <= qpos, s[:, ext - bq:], NEG)
    return diag if qb == 0 else jnp.concatenate([s[:, :ext - bq], diag], axis=-1)


def _attn_fwd(qp, kp, v, scale):
    L = qp.shape[0]
    bq = min(256, L)

    def body(q_ref, k_ref, v_ref, o_ref, lse_ref):
        for qb in range(L // bq):
            rows = slice(qb * bq, (qb + 1) * bq)
            s = _attn_scores(q_ref, k_ref, qb, bq, scale)
            m = jnp.max(s, axis=-1, keepdims=True)
            e = jnp.exp(s - m)
            l = jnp.sum(e, axis=-1, keepdims=True)
            o_ref[rows, :] = _dot(e.astype(BF16), v_ref[0:(qb + 1) * bq, :]) / l
            lse_ref[rows, :] = jnp.broadcast_to(m + jnp.log(l), (bq, HD))

    blk = pl.BlockSpec((L, HD), lambda h: (0, h))
    wide = pl.BlockSpec((L, 2 * HD), lambda h: (0, h))
    return pl.pallas_call(
        body, name="mla_attn_fwd", grid=(MLA_H,),
        in_specs=[wide, wide, blk], out_specs=[blk, blk],
        out_shape=[jax.ShapeDtypeStruct((L, MLA_H * HD), F32)] * 2,
        compiler_params=pltpu.CompilerParams(dimension_semantics=("arbitrary",), vmem_limit_bytes=VMEM_LIMIT),
    )(qp, kp, v)


def _attn_bwd(qp, kp, v, o, lse, do, scale):
    L = qp.shape[0]
    bq = min(256, L)
    nq = L // bq

    def body(q_ref, k_ref, v_ref, o_ref, lse_ref, do_ref, dq_ref, dk_ref, dv_ref):
        dk_ref[...] = jnp.zeros_like(dk_ref)
        dv_ref[...] = jnp.zeros_like(dv_ref)
        for qb in range(nq):
            rows = slice(qb * bq, (qb + 1) * bq)
            ext = (qb + 1) * bq
            do = do_ref[rows, :]
            dob = do.astype(BF16)
            p = jnp.exp(_attn_scores(q_ref, k_ref, qb, bq, scale) - lse_ref[rows, 0:1])
            dp = _dot_nt(dob, v_ref[0:ext, :])
            dsum = jnp.sum(do * o_ref[rows, :], axis=-1, keepdims=True)
            ds = (p * (dp - dsum) * scale).astype(BF16)
            dq_ref[rows, :] = _dot(ds, k_ref[0:ext, :])
            dk_ref[0:ext, :] += _dot_tn(ds, q_ref[rows, :])
            dv_ref[0:ext, :] += _dot_tn(p.astype(BF16), dob)

    sd = jax.ShapeDtypeStruct
    blk = pl.BlockSpec((L, HD), lambda h: (0, h))
    wide = pl.BlockSpec((L, 2 * HD), lambda h: (0, h))
    return pl.pallas_call(
        body, name="mla_attn_bwd", grid=(MLA_H,),
        in_specs=[wide, wide, blk, blk, blk, blk], out_specs=[wide, wide, blk],
        out_shape=[sd((L, MLA_H * 2 * HD), F32), sd((L, MLA_H * 2 * HD), F32), sd((L, MLA_H * HD), F32)],
        compiler_params=pltpu.CompilerParams(dimension_semantics=("arbitrary",), vmem_limit_bytes=VMEM_LIMIT),
    )(qp, kp, v, o, lse, do)


def _kv_fn(mem, gm, w, gk):
    kv = _mm(_rms(mem, gm, D_MODEL), w)
    k = jnp.concatenate([_rms(kv[:, HD * h:HD * (h + 1)], gk, HD) for h in range(X_HEADS)], axis=-1)
    return k, kv[:, XQ:]


def _kv_prep(mem, gm, w, gk, name):
    def fn(mem, gm, w, gk):
        return _kv_fn(mem, gm, w, gk)
    M = mem.shape[0]
    return _rowwise(name, fn, [('c', mem), ('c', gm), ('c', w), ('c', gk)],
                    [('c', (M, XQ), F32), ('c', (M, XQ), F32)], 1)


def _kv_prep_bwd(mem, gm, w, gk, dk, dv, name):
    def fn(mem, gm, w, gk, dk, dv):
        _, vjp = jax.vjp(lambda a, b, c: _kv_fn(mem, a, b, c), gm, w, gk)
        return vjp((dk, dv))
    return _rowwise(name, fn, [('c', mem), ('c', gm), ('c', w), ('c', gk), ('c', dk), ('c', dv)],
                    [('c', gm.shape, F32), ('c', w.shape, BF16), ('c', gk.shape, F32)], 1)


def _forward_merge(x, mix, mix_kind, xq, gate, k, v, gq, wout, name, nblk, sub, host=None):
    def fn(x, mix, xq, gate, k, v, gq, wout):
        o = _merge(mix, xq, gate, k, v, gq)
        return (x + _dot(o.astype(BF16), wout),)
    L = x.shape[0]
    out = _rowwise(name, fn, [('r', x), (mix_kind, mix), ('r', xq), ('r', gate), ('c', k), ('c', v), ('c', gq),
                              ('c', wout)], [('r', (L, D_MODEL), F32)], nblk, sub, host=host)
    return out[0] if host is None else (out[0][0], out[1])


def _backward_merge(dx, mix, mix_kind, xq, gate, k, v, gq, wout, name, nblk, sub):
    def fn(dx, mix, xq, gate, k, v, gq, wout):
        g16 = dx.astype(BF16)
        do = _dot_nt(g16, wout)
        o, vjp = jax.vjp(_merge, mix, xq, gate, k, v, gq)
        dmix, dxq, dgate, dk, dv, dgq = vjp(do)
        return dmix, dxq, dgate, o, g16, dk, dv, dgq
    L = dx.shape[0]
    return _rowwise(
        name, fn,
        [('r', dx), (mix_kind, mix), ('r', xq), ('r', gate), ('c', k), ('c', v), ('c', gq), ('c', wout)],
        [('r', (L, PRIM), F32), ('r', (L, XQ), F32), ('r', (L, BRANCH), F32), ('t', (BRANCH, L), BF16),
         ('r', (L, D_MODEL), BF16), ('a', k.shape, F32), ('a', v.shape, F32), ('a', gq.shape, F32)], nblk, sub)


_MLA_IN = 3392
_MLA_IN_PAD = 3456


def _from_slots(g):
    _, k, n = g.shape
    return jnp.transpose(g, (1, 0, 2)).reshape(k, N_DEV * n)


def _to_slots(w):
    k = w.shape[0]
    return jnp.transpose(w.reshape(k, N_DEV, -1), (1, 0, 2))


def _uq_to_kernel(g):
    uq = _from_slots(g).reshape(Q_LORA, MLA_H, HD + ROPE)
    return jnp.concatenate([uq[:, :, :HD].reshape(Q_LORA, PRIM),
                            jnp.pad(uq[:, :, HD:], ((0, 0), (0, 0), (0, HD - ROPE))).reshape(Q_LORA, PRIM)], axis=1)


def _uq_from_kernel(d_w_q):
    uq = jnp.concatenate([d_w_q[:, :PRIM].reshape(Q_LORA, MLA_H, HD),
                          d_w_q[:, PRIM:].reshape(Q_LORA, MLA_H, HD)[:, :, :ROPE]], axis=2)
    return _to_slots(uq.reshape(Q_LORA, MLA_H * (HD + ROPE)))


def _mla_in_perm(w):
    return jnp.concatenate([w[:, :768], w[:, 832:], w[:, 768:832], jnp.zeros((w.shape[0], 64), w.dtype)], axis=1)


def _mla_in_unperm(w):
    return jnp.concatenate([w[:, :768], w[:, 3328:3392], w[:, 768:3328]], axis=1)


_SMALL = (("ln_gain", 2048), ("mem_norm", 2048), ("xq_norm", 256), ("xk_norm", 256), ("s5_lambda_re", 6144),
          ("s5_lambda_im", 6144), ("s5_log_step", 96), ("s5_b_re", 98304), ("s5_b_im", 98304), ("s5_c_re", 98304),
          ("s5_c_im", 98304), ("s5_d", 1536), ("mla_q_lora_norm", 512), ("mla_kv_lora_norm", 256),
          ("mla_q_nope_norm", 128), ("mla_k_nope_norm", 128), ("mla_q_rope_norm", 64), ("mla_k_rope_norm", 64))
_SMALL_ROWS = 432
_SMALL_OFF = {name: sum(n for _, n in _SMALL[:i]) for i, (name, _) in enumerate(_SMALL)}


def _pack_small(d):
    flat = jnp.concatenate([d[n].reshape(-1).astype(F32) for n, _ in _SMALL])
    return jnp.pad(flat, (0, _SMALL_ROWS * 1024 - flat.shape[0])).reshape(_SMALL_ROWS, 1024)


def _unpack_small(p, name, shape):
    off = _SMALL_OFF[name]
    return p.reshape(-1)[off:off + int(np.prod(shape))].reshape(shape)


_WEIGHTS = ('ln_gain', 'w_out', 'mem_norm', 'w_mem_kv', 'xq_norm', 'xk_norm', 's5_w_in', 's5_lambda_re',
            's5_lambda_im', 's5_log_step', 's5_b_re', 's5_b_im', 's5_c_re', 's5_c_im', 's5_d', 's5_w_glu', 'mla_w_in',
            'mla_q_lora_norm', 'mla_kv_lora_norm', 'mla_w_uq', 'mla_w_ukv', 'mla_q_nope_norm', 'mla_k_nope_norm',
            'mla_q_rope_norm', 'mla_k_rope_norm')
_BIG = ('w_out', 'w_mem_kv', 's5_w_in', 's5_w_glu', 'mla_w_in', 'mla_w_uq', 'mla_w_ukv')


def _pad128(g):
    return jnp.pad(g.reshape(1, -1), ((0, 0), (0, HD - g.shape[-1])))


def kernel(x, mem, positions, ln_gain, w_out, mem_norm, w_mem_kv, xq_norm, xk_norm, s5_w_in, s5_lambda_re, s5_lambda_im, s5_log_step, s5_b_re, s5_b_im, s5_c_re, s5_c_im, s5_d, s5_w_glu, mla_w_in, mla_q_lora_norm, mla_kv_lora_norm, mla_w_uq, mla_w_ukv, mla_q_nope_norm, mla_k_nope_norm, mla_q_rope_norm, mla_k_rope_norm, loss_target, m_ln_gain, m_w_out, m_mem_norm, m_w_mem_kv, m_xq_norm, m_xk_norm, m_s5_w_in, m_s5_lambda_re, m_s5_lambda_im, m_s5_log_step, m_s5_b_re, m_s5_b_im, m_s5_c_re, m_s5_c_im, m_s5_d, m_s5_w_glu, m_mla_w_in, m_mla_q_lora_norm, m_mla_kv_lora_norm, m_mla_w_uq, m_mla_w_ukv, m_mla_q_nope_norm, m_mla_k_nope_norm, m_mla_q_rope_norm, m_mla_k_rope_norm, v_ln_gain, v_w_out, v_mem_norm, v_w_mem_kv, v_xq_norm, v_xk_norm, v_s5_w_in, v_s5_lambda_re, v_s5_lambda_im, v_s5_log_step, v_s5_b_re, v_s5_b_im, v_s5_c_re, v_s5_c_im, v_s5_d, v_s5_w_glu, v_mla_w_in, v_mla_q_lora_norm, v_mla_kv_lora_norm, v_mla_w_uq, v_mla_w_ukv, v_mla_q_nope_norm, v_mla_k_nope_norm, v_mla_q_rope_norm, v_mla_k_rope_norm):
    weights = dict(ln_gain=ln_gain, w_out=w_out, mem_norm=mem_norm, w_mem_kv=w_mem_kv, xq_norm=xq_norm,
                   xk_norm=xk_norm, s5_w_in=s5_w_in, s5_lambda_re=s5_lambda_re, s5_lambda_im=s5_lambda_im,
                   s5_log_step=s5_log_step, s5_b_re=s5_b_re, s5_b_im=s5_b_im, s5_c_re=s5_c_re, s5_c_im=s5_c_im,
                   s5_d=s5_d, s5_w_glu=s5_w_glu, mla_w_in=mla_w_in, mla_q_lora_norm=mla_q_lora_norm,
                   mla_kv_lora_norm=mla_kv_lora_norm, mla_w_uq=mla_w_uq, mla_w_ukv=mla_w_ukv,
                   mla_q_nope_norm=mla_q_nope_norm, mla_k_nope_norm=mla_k_nope_norm,
                   mla_q_rope_norm=mla_q_rope_norm, mla_k_rope_norm=mla_k_rope_norm)
    m_in = dict(zip(_WEIGHTS, (m_ln_gain, m_w_out, m_mem_norm, m_w_mem_kv, m_xq_norm, m_xk_norm, m_s5_w_in,
                               m_s5_lambda_re, m_s5_lambda_im, m_s5_log_step, m_s5_b_re, m_s5_b_im, m_s5_c_re,
                               m_s5_c_im, m_s5_d, m_s5_w_glu, m_mla_w_in, m_mla_q_lora_norm, m_mla_kv_lora_norm,
                               m_mla_w_uq, m_mla_w_ukv, m_mla_q_nope_norm, m_mla_k_nope_norm, m_mla_q_rope_norm,
                               m_mla_k_rope_norm)))
    v_in = dict(zip(_WEIGHTS, (v_ln_gain, v_w_out, v_mem_norm, v_w_mem_kv, v_xq_norm, v_xk_norm, v_s5_w_in,
                               v_s5_lambda_re, v_s5_lambda_im, v_s5_log_step, v_s5_b_re, v_s5_b_im, v_s5_c_re,
                               v_s5_c_im, v_s5_d, v_s5_w_glu, v_mla_w_in, v_mla_q_lora_norm, v_mla_kv_lora_norm,
                               v_mla_w_uq, v_mla_w_ukv, v_mla_q_nope_norm, v_mla_k_nope_norm, v_mla_q_rope_norm,
                               v_mla_k_rope_norm)))

    x0 = x[0]
    mem0 = mem[0]
    target = loss_target[0]
    L = x0.shape[0]
    nblk, sub = 8, 1
    me = 4 * lax.axis_index("x") + 2 * lax.axis_index("y") + lax.axis_index("c")

    lora = jnp.pad(jnp.concatenate([mla_q_lora_norm, mla_kv_lora_norm], axis=1), ((0, 7), (0, HD - 96)))
    def gather(*shards):
        return _plan_all_gather([s.astype(BF16) for s in shards])

    (W_in_s5,) = _exchange_call(gather(s5_w_in[0]), "ag_s5_w_in")

    ln0, ln1 = ln_gain[0:1], ln_gain[1:2]
    gq0, gq1 = xq_norm[0:1], xq_norm[1:2]
    gk0, gk1 = xk_norm[0:1], xk_norm[1:2]
    gm0, gm1 = mem_norm[0:1], mem_norm[1:2]
    gqn, gkn = mla_q_nope_norm, mla_k_nope_norm
    gqr, gkr = _pad128(mla_q_rope_norm), _pad128(mla_k_rope_norm)

    lr3 = s5_lambda_re.reshape(S5_G, 1, S5_P)
    li3 = s5_lambda_im.reshape(S5_G, 1, S5_P)
    ls3 = s5_log_step.reshape(S5_G, 1, 1)
    btr = jnp.swapaxes(s5_b_re[0], 1, 2)
    bti = jnp.swapaxes(s5_b_im[0], 1, 2)
    a_r, a_i, bbr, bbi = _s5_params(lr3, li3, ls3, btr, bti)
    bm, bmt, cm, cmt = _s5_mats(bbr, bbi, s5_c_re[0], s5_c_im[0])
    a_r2 = a_r.reshape(1, S5_G * S5_P)
    a_i2 = a_i.reshape(1, S5_G * S5_P)
    cmask, rmat = _s5_compact_consts()

    half = ROPE // 2
    inv_freq = ROPE_THETA ** (-jnp.arange(half, dtype=F32) / half)
    invf = jnp.concatenate([inv_freq, inv_freq, jnp.zeros((HD - ROPE,), F32)]).reshape(1, HD)

    def rot_tables(pos, invf):
        ang = pos.astype(F32) * invf
        lane = lax.broadcasted_iota(jnp.int32, ang.shape, 1)
        c = jnp.where(lane < ROPE, jnp.cos(ang), 0.0)
        s = jnp.sin(ang)
        return c, jnp.where(lane < half, -s, 0.0), jnp.where((lane >= half) & (lane < ROPE), s, 0.0)

    tc, ts1, ts2 = _rowwise("rot_tables", rot_tables, [('r', positions.reshape(L, 1)), ('c', invf)],
                            [('r', (L, HD), F32)] * 3, nblk, sub)

    def in_s5(x, g, w):
        proj = _mm_slots(_rms(x, g, D_MODEL).astype(BF16), w)
        return proj[:, :PRIM], proj[:, PRIM:PRIM + XQ], proj[:, PRIM + XQ:]

    (u_s5, xq_a, gate_a), (G_mkv0, G_uq) = _rowwise(
        "s5_in", in_s5, [('r', x0), ('c', ln0), ('c', W_in_s5)],
        [('r', (L, PRIM), F32), ('r', (L, XQ), F32), ('r', (L, BRANCH), F32)], nblk, sub,
        host=gather(w_mem_kv[0], mla_w_uq[0]))
    (y_s5,), (W_glu, G_out0) = _s5_fwd(u_s5, bm, cm, a_r2, a_i2, s5_d, host=gather(s5_w_glu[0], w_out[0]))

    def glu(y, w):
        z = _mm_slots(_gelu(y).astype(BF16), w)
        return (z[:, :PRIM] * _sigmoid(z[:, PRIM:]),)

    (y2,), (G_in_mla,) = _rowwise("s5_glu", glu, [('r', y_s5), ('c', W_glu)], [('r', (L, PRIM), F32)], nblk, sub,
                                  host=gather(mla_w_in[0]))
    W_mkv0 = G_mkv0.reshape(D_MODEL, 2 * XQ)
    k_a, v_a = _kv_prep(mem0, gm0, W_mkv0, gk0, "kv_prep0")
    x1, (W_kv, G_mkv1, G_lora) = _forward_merge(
        x0, y2, 'r', xq_a, gate_a, k_a, v_a, gq0, G_out0.reshape(BRANCH, D_MODEL), "merge0", nblk, sub,
        host=_plan_all_gather([mla_w_ukv[0].astype(BF16), w_mem_kv[1].astype(BF16), lora]))
    W_in_mla = _mla_in_perm(_from_slots(G_in_mla))
    W_q = _uq_to_kernel(G_uq)
    g_qlora = G_lora[:, 0, :64].reshape(1, Q_LORA)
    g_kvlora = G_lora[:, 0, 64:96].reshape(1, KV_LORA)

    def in_mla(x, g, w):
        proj = _dot(_rms(x, g, D_MODEL).astype(BF16), w)
        return proj[:, :512], proj[:, 512:768], proj[:, 768:1280], proj[:, 1280:3328], proj[:, 3328:]

    (c_q, c_kv, xq_b, gate_b, krp), (G_out1,) = _rowwise(
        "mla_in", in_mla, [('r', x1), ('c', ln1), ('c', W_in_mla)],
        [('r', (L, Q_LORA), F32), ('r', (L, KV_LORA), F32), ('r', (L, XQ), F32), ('r', (L, BRANCH), F32),
         ('r', (L, HD), F32)], nblk, sub, host=gather(w_out[1]))
    W_out = (G_out0.reshape(BRANCH, D_MODEL), G_out1.reshape(BRANCH, D_MODEL))
    W_mkv = (W_mkv0, G_mkv1.reshape(D_MODEL, 2 * XQ))

    def qkv(c_q, c_kv, krp, tc, ts1, ts2, gql, gkvl, wq, wkv, gqn, gkn, gqr, gkr):
        q = _dot(_rms(c_q, gql, Q_LORA).astype(BF16), wq)
        kv = _mm_slots(_rms(c_kv, gkvl, KV_LORA).astype(BF16), wkv)
        kp, v = _kv_post(kv, krp, gkn, gkr, tc, ts1, ts2)
        return _q_post(q, gqn, gqr, tc, ts1, ts2), kp, v

    qkv_consts = [('c', g_qlora), ('c', g_kvlora), ('c', W_q), ('c', W_kv), ('c', gqn), ('c', gkn), ('c', gqr),
                  ('c', gkr)]
    q_pad, k_pad, v_h = _rowwise(
        "mla_qkv", qkv, [('r', c_q), ('r', c_kv), ('r', krp), ('r', tc), ('r', ts1), ('r', ts2)] + qkv_consts,
        [('r', (L, 2 * PRIM), BF16), ('r', (L, 2 * PRIM), BF16), ('r', (L, PRIM), BF16)], nblk, sub)
    scale = (HD + ROPE) ** -0.5
    attn, lse = _attn_fwd(q_pad, k_pad, v_h, scale)
    k_b, v_b = _kv_prep(mem0, gm1, W_mkv[1], gk1, "kv_prep1")
    x2 = _forward_merge(x1, attn, 'r', xq_b, gate_b, k_b, v_b, gq1, W_out[1], "merge1", nblk, sub)

    def loss_fn(y, t):
        err = y - t
        part = 0.5 * jnp.sum(jnp.sum(err * err, axis=-1, keepdims=True) * (1.0 / D_MODEL), axis=0, keepdims=True)
        return err * (1.0 / D_MODEL), jnp.broadcast_to(part, (1, HD))

    dx2, loss_part = _rowwise("loss", loss_fn, [('r', x2), ('r', target)],
                              [('r', (L, D_MODEL), F32), ('a', (1, HD), F32)], nblk, sub)

    dattn, dxq_b, dgate_b, o_b, g_b, dk_b, dv_b, dgq1 = _backward_merge(
        dx2, attn, 'r', xq_b, gate_b, k_b, v_b, gq1, W_out[1], "merge1_bwd", nblk, sub)
    dgm1, dW_mkv1, dgk1 = _kv_prep_bwd(mem0, gm1, W_mkv[1], gk1, dk_b, dv_b, "kv_prep1_bwd")
    dW_out1 = _matmul_tn(o_b, g_b, "dw_out1")
    dq_pad, dk_pad, dv_h = _attn_bwd(q_pad, k_pad, v_h, attn, lse, dattn, scale)

    def qkv_bwd(c_q, c_kv, krp, tc, ts1, ts2, dqp, dkp, dv, gql, gkvl, wq, wkv, gqn, gkn, gqr, gkr):
        cqn, vjp_qn = jax.vjp(lambda a, b: _rms(a, b, Q_LORA), c_q, gql)
        ckvn, vjp_kvn = jax.vjp(lambda a, b: _rms(a, b, KV_LORA), c_kv, gkvl)
        cqn16 = cqn.astype(BF16)
        ckvn16 = ckvn.astype(BF16)
        q = _dot(cqn16, wq)
        kv = _mm_slots(ckvn16, wkv)
        _, vjp_q = jax.vjp(lambda a, b, c: _q_post(a, b, c, tc, ts1, ts2), q, gqn, gqr)
        dq, dgqn, dgqr = vjp_q(dqp)
        _, vjp_kv = jax.vjp(lambda a, b, c, d: _kv_post(a, b, c, d, tc, ts1, ts2), kv, krp, gkn, gkr)
        dkv, dkrp, dgkn, dgkr = vjp_kv((dkp, dv))
        dq16 = dq.astype(BF16)
        dkv16 = dkv.astype(BF16)
        dc_q, dgql = vjp_qn(_dot_nt(dq16, wq))
        dc_kv, dgkvl = vjp_kvn(_mm_slots_nt(dkv16, wkv))
        return dc_q, dc_kv, dkrp, cqn16, dq16, ckvn16, dkv16, dgql, dgkvl, dgqn, dgkn, dgqr, dgkr

    (dc_q, dc_kv, dkrp, cqn16, dq16, ckvn16, dkv16, dgql, dgkvl, dgqn, dgkn, dgqr, dgkr) = _rowwise(
        "mla_qkv_bwd", qkv_bwd,
        [('r', c_q), ('r', c_kv), ('r', krp), ('r', tc), ('r', ts1), ('r', ts2), ('r', dq_pad), ('r', dk_pad),
         ('r', dv_h)] + qkv_consts,
        [('r', (L, Q_LORA), F32), ('r', (L, KV_LORA), F32), ('r', (L, HD), F32), ('t', (Q_LORA, L), BF16),
         ('r', (L, 2 * PRIM), BF16), ('t', (KV_LORA, L), BF16), ('r', (L, 2 * PRIM), BF16),
         ('a', (1, Q_LORA), F32), ('a', (1, KV_LORA), F32), ('a', (1, HD), F32), ('a', (1, HD), F32),
         ('a', (1, HD), F32), ('a', (1, HD), F32)], nblk, sub)
    dW_q = _matmul_tn(cqn16, dq16, "dw_uq")
    dW_kv = _matmul_tn_slots(ckvn16, dkv16, "dw_ukv")

    def in_bwd(x, dres, g, w, *dparts):
        dproj = jnp.concatenate(dparts, axis=-1).astype(BF16)
        xn, vjp = jax.vjp(lambda a, b: _rms(a, b, D_MODEL), x, g)
        dx, dg = vjp(_mm_slots_nt(dproj, w) if w.ndim == 3 else _dot_nt(dproj, w))
        return dx + dres, xn, dproj, dg

    dx1, xn1, dproj1, dln1 = _rowwise(
        "mla_in_bwd", in_bwd,
        [('r', x1), ('r', dx2), ('c', ln1), ('c', W_in_mla), ('r', dc_q), ('r', dc_kv), ('r', dxq_b), ('r', dgate_b),
         ('r', dkrp)],
        [('r', (L, D_MODEL), F32), ('t', (D_MODEL, L), BF16), ('r', (L, _MLA_IN_PAD), BF16), ('a', (1, D_MODEL), F32)],
        nblk, sub)
    dW_in_mla = _matmul_tn(xn1, dproj1, "dw_mla_in")

    dy2, dxq_a, dgate_a, o_a, g_a, dk_a, dv_a, dgq0 = _backward_merge(
        dx1, y2, 'r', xq_a, gate_a, k_a, v_a, gq0, W_out[0], "merge0_bwd", nblk, sub)
    dgm0, dW_mkv0, dgk0 = _kv_prep_bwd(mem0, gm0, W_mkv[0], gk0, dk_a, dv_a, "kv_prep0_bwd")
    dW_out0 = _matmul_tn(o_a, g_a, "dw_out0")

    def glu_bwd(y, dy2, w):
        h, vjp_h = jax.vjp(_gelu, y)
        h16 = h.astype(BF16)
        z = _mm_slots(h16, w)
        _, vjp_z = jax.vjp(lambda z: z[:, :PRIM] * _sigmoid(z[:, PRIM:]), z)
        dz16 = vjp_z(dy2)[0].astype(BF16)
        return vjp_h(_mm_slots_nt(dz16, w))[0], h16, dz16

    early = [dW_out1.reshape(N_DEV, 256, D_MODEL), dW_mkv1.reshape(N_DEV, 128, 2 * XQ),
             _to_slots(_mla_in_unperm(dW_in_mla)), _uq_from_kernel(dW_q), dW_kv,
             dW_out0.reshape(N_DEV, 256, D_MODEL), dW_mkv0.reshape(N_DEV, 128, 2 * XQ)]
    (dy_s5, h16, dz16), early_pair = _rowwise(
        "s5_glu_bwd", glu_bwd, [('r', y_s5), ('r', dy2), ('c', W_glu)],
        [('r', (L, PRIM), F32), ('t', (PRIM, L), BF16), ('r', (L, 2 * PRIM), BF16)], nblk, sub,
        host=_plan_pair(early))
    dW_glu = _matmul_tn_slots(h16, dz16, "dw_glu")
    early_t = _pair_add(early + [dW_glu], early_pair + list(_exchange_call(_plan_pair([dW_glu]), "rs_pair_glu")),
                        "rs_add_early")
    (du_s5, dbc, dcc, dd, dar, dai), early_recv = _s5_bwd(u_s5, dy_s5, bm, bmt, cmt, a_r2, a_i2, s5_d, cmask, rmat,
                                                          host=_plan_chips(early_t))
    dx0, xn0, dproj0, dln0 = _rowwise(
        "s5_in_bwd", in_bwd,
        [('r', x0), ('r', dx1), ('c', ln0), ('c', W_in_s5), ('r', du_s5), ('r', dxq_a),
         ('r', dgate_a)],
        [('r', (L, D_MODEL), F32), ('t', (D_MODEL, L), BF16), ('r', (L, 2 * BRANCH), BF16), ('a', (1, D_MODEL), F32)],
        nblk, sub)

    dbc4 = dbc.reshape(S5_G, S5_C, 2, S5_P)
    dcc4 = dcc.reshape(S5_G, S5_C, 2, S5_P)
    dlr, dli, dls, dbtr, dbti = _s5_params_bwd(
        lr3, li3, ls3, btr, bti, dar.reshape(S5_G, 1, S5_P), dai.reshape(S5_G, 1, S5_P), dbc4[:, :, 0], dbc4[:, :, 1])

    small_part = {
        "ln_gain": jnp.concatenate([dln0, dln1]), "mem_norm": jnp.concatenate([dgm0, dgm1]),
        "xq_norm": jnp.concatenate([dgq0, dgq1]), "xk_norm": jnp.concatenate([dgk0, dgk1]),
        "s5_lambda_re": dlr, "s5_lambda_im": dli, "s5_log_step": dls,
        "s5_b_re": jnp.swapaxes(dbtr, 1, 2), "s5_b_im": jnp.swapaxes(dbti, 1, 2),
        "s5_c_re": dcc4[:, :, 0], "s5_c_im": -dcc4[:, :, 1], "s5_d": dd,
        "mla_q_lora_norm": dgql, "mla_kv_lora_norm": dgkvl, "mla_q_nope_norm": dgqn, "mla_k_nope_norm": dgkn,
        "mla_q_rope_norm": dgqr[:, :ROPE], "mla_k_rope_norm": dgkr[:, :ROPE],
    }
    loss8 = jnp.pad(loss_part, ((0, 7), (0, 0)))
    dW_in_s5, (small_gath, loss_g) = _matmul_tn_slots(
        xn0, dproj0, "dw_s5_in", host=_plan_all_gather([_pack_small(small_part).astype(BF16), loss8]))

    late = [dW_in_s5]
    late_t = _pair_add(late, list(_exchange_call(_plan_pair(late), "rs_pair_late")), "rs_add_late")
    owners = [("w_out", 1), ("w_mem_kv", 1), ("mla_w_in", 0), ("mla_w_uq", 0), ("mla_w_ukv", 0), ("w_out", 0),
              ("w_mem_kv", 0), ("s5_w_glu", 0)]
    upd, late_recv = _updates_call(early_recv, [weights[n][i] for n, i in owners], [m_in[n][i] for n, i in owners],
                                   [v_in[n][i] for n, i in owners], "update_early", host=_plan_chips(late_t))
    owners.append(("s5_w_in", 0))
    upd.append(_sum_adamw(late_recv[0], s5_w_in[0], m_s5_w_in[0], v_s5_w_in[0], "update_s5_w_in"))
    grads, delta, new_m, new_v = {}, {}, {}, {}
    for n in _BIG:
        parts = [u for u, (o, _) in sorted(zip(upd, owners), key=lambda t: t[1][1]) if o == n]
        grads[n], delta[n], new_m[n], new_v[n] = (jnp.stack([p[j] for p in parts]) for j in range(4))

    gs, loss_sum = _small_sum(small_gath, loss_g, "small_sum")
    loss = loss_sum[0, 0]
    for n, _ in _SMALL:
        shape = weights[n].shape
        if n == "mla_q_lora_norm":
            grads[n] = lax.dynamic_slice(_unpack_small(gs, n, (Q_LORA,)), (me * 64,), (64,)).reshape(shape)
        elif n == "mla_kv_lora_norm":
            grads[n] = lax.dynamic_slice(_unpack_small(gs, n, (KV_LORA,)), (me * 32,), (32,)).reshape(shape)
        else:
            grads[n] = _unpack_small(gs, n, shape)

    def own(a):
        return a.reshape(a.shape[1:]) if a.ndim >= 3 else a

    wide = ("s5_b_re", "s5_b_im", "s5_c_re", "s5_c_im")
    for names, nb, call in (([n for n, _ in _SMALL if n not in wide], 1, "update_small"), (wide, 12, "update_s5_bc")):
        res = _adamw_multi([own(weights[n]) for n in names], [own(grads[n]) for n in names],
                           [own(m_in[n]) for n in names], [own(v_in[n]) for n in names], call, nb)
        for n, (dl, m2, v2) in zip(names, res):
            shape = weights[n].shape
            delta[n], new_m[n], new_v[n] = dl.reshape(shape), m2.reshape(shape), v2.reshape(shape)
    return (loss, dx0[None], *[grads[n] for n in _WEIGHTS], *[delta[n] for n in _WEIGHTS],
            *[new_m[n] for n in _WEIGHTS], *[new_v[n] for n in _WEIGHTS])
```

```python
import functools
import math

import numpy as np
import jax
import jax.numpy as jnp
from jax import lax
from jax.experimental import pallas as pl
from jax.experimental.pallas import tpu as pltpu

F32 = jnp.float32
BF16 = jnp.bfloat16
EPS = 1e-6
NEG = float(np.finfo(np.float32).min)
MESH = pl.DeviceIdType.MESH

N_DEV = 8
D_MODEL = 1024
MEM_LEN = 256
XQ = 512
PRIM = 1536
BRANCH = 2048
X_HEADS = 4
HD = 128
S5_G = 96
S5_P = 64
S5_C = 16
S5_GB = 8
S5_W = S5_GB * S5_P
MLA_H = 12
ROPE = 64
Q_LORA = 512
KV_LORA = 256
ROPE_THETA = 10000.0

ADAM_LR = 0.001
ADAM_B1 = 0.9
ADAM_B2 = 0.999
ADAM_EPS = 1e-08
ADAM_WD = 0.01
ADAM_STEP = 10

VMEM_LIMIT = 56 * 1024 * 1024


def _dot(a, b):
    return jnp.dot(a, b, preferred_element_type=F32)


def _dot_nt(a, b):
    return lax.dot_general(a, b, (((1,), (1,)), ((), ())), preferred_element_type=F32)


def _dot_tn(a, b):
    return lax.dot_general(a, b, (((0,), (0,)), ((), ())), preferred_element_type=F32)


@jax.custom_vjp
def _mm(a, b):
    return _dot(a.astype(BF16), b.astype(BF16))


def _mm_fwd(a, b):
    return _mm(a, b), (a, b)


def _mm_bwd(res, g):
    a, b = res
    gb = g.astype(BF16)
    return _dot_nt(gb, b.astype(BF16)).astype(a.dtype), _dot_tn(a.astype(BF16), gb).astype(b.dtype)


_mm.defvjp(_mm_fwd, _mm_bwd)


@jax.custom_vjp
def _mm_nt(a, b):
    return _dot_nt(a.astype(BF16), b.astype(BF16))


def _mm_nt_fwd(a, b):
    return _mm_nt(a, b), (a, b)


def _mm_nt_bwd(res, g):
    a, b = res
    gb = g.astype(BF16)
    return _dot(gb, b.astype(BF16)).astype(a.dtype), _dot_tn(gb, a.astype(BF16)).astype(b.dtype)


_mm_nt.defvjp(_mm_nt_fwd, _mm_nt_bwd)


@jax.custom_vjp
def _softmax(s):
    m = jnp.max(s, axis=-1, keepdims=True)
    e = jnp.exp(s - m)
    return e / jnp.sum(e, axis=-1, keepdims=True)


def _softmax_fwd(s):
    p = _softmax(s)
    return p, p


def _softmax_bwd(p, g):
    return (p * (g - jnp.sum(p * g, axis=-1, keepdims=True)),)


_softmax.defvjp(_softmax_fwd, _softmax_bwd)


def _rms(x, g, n):
    ms = jnp.sum(x * x, axis=-1, keepdims=True) * (1.0 / n)
    return x * lax.rsqrt(ms + EPS) * g


def _sigmoid(x):
    return 1.0 / (1.0 + jnp.exp(-x))


def _silu(x):
    return x * _sigmoid(x)


def _gelu(x):
    c = math.sqrt(2.0 / math.pi)
    return 0.5 * x * (1.0 + jnp.tanh(c * (x + 0.044715 * (x * x * x))))


@jax.custom_vjp
def _rot(x, c, s1, s2):
    return x * c + pltpu.roll(x, 96, 1) * s1 + pltpu.roll(x, 32, 1) * s2


def _rot_fwd(x, c, s1, s2):
    return _rot(x, c, s1, s2), (c, s1, s2)


def _rot_bwd(res, g):
    c, s1, s2 = res
    dx = g * c + pltpu.roll(g * s1, 32, 1) + pltpu.roll(g * s2, 96, 1)
    return dx, jnp.zeros_like(c), jnp.zeros_like(s1), jnp.zeros_like(s2)


_rot.defvjp(_rot_fwd, _rot_bwd)


def _mem_attn(xq, k, v, gq):
    outs = []
    for h in range(X_HEADS):
        sl = slice(HD * h, HD * (h + 1))
        q = _rms(xq[:, sl], gq, HD)
        p = _softmax(_mm_nt(q, k[:, sl]) * (HD ** -0.5))
        outs.append(_mm(p, v[:, sl]))
    return jnp.concatenate(outs, axis=-1)


def _merge(mix, xq, gate, k, v, gq):
    return jnp.concatenate([mix, _mem_attn(xq, k, v, gq)], axis=-1) * _silu(gate)


def _q_post(q, gqn, gqr, c, s1, s2):
    pieces = []
    for h in range(MLA_H):
        pieces.append(_rms(q[:, HD * h:HD * (h + 1)], gqn, HD))
        pieces.append(_rot(_rms(q[:, PRIM + HD * h:PRIM + HD * (h + 1)], gqr, ROPE), c, s1, s2))
    return jnp.concatenate(pieces, axis=-1)


def _kv_post(kv, krp, gkn, gkr, c, s1, s2):
    kr = _rot(_rms(krp, gkr, ROPE), c, s1, s2)
    pieces, vals = [], []
    for h in range(MLA_H):
        pieces.append(_rms(kv[:, 2 * HD * h:2 * HD * h + HD], gkn, HD))
        pieces.append(kr)
        vals.append(kv[:, 2 * HD * h + HD:2 * HD * (h + 1)])
    return jnp.concatenate(pieces, axis=-1), jnp.concatenate(vals, axis=-1)


def _rowwise(name, fn, ins, outs, nblk, sub=1, host=None):
    n_in = len(ins)

    def spec(kind, shape):
        if kind == 'r':
            return pl.BlockSpec((shape[0] // nblk, shape[1]), lambda i: (i, 0))
        if kind == 't':
            return pl.BlockSpec((shape[0], shape[1] // nblk), lambda i: (0, i))
        zeros = (0,) * len(shape)
        return pl.BlockSpec(tuple(shape), lambda i: zeros)

    def body(*refs):
        i = pl.program_id(0)
        res = fn(*[r[...] for r in refs[:n_in]])
        for (kind, _, _), ref, val in zip(outs, refs[n_in:], res):
            if kind == 'a':
                @pl.when(i == 0)
                def _():
                    ref[...] = jnp.zeros_like(ref)
                ref[...] += val.astype(ref.dtype)
            elif kind == 't':
                ref[...] = val.astype(F32).T.astype(ref.dtype)
            else:
                ref[...] = val.astype(ref.dtype)

    res, hosted = _hosting_call(
        body, name, nblk, host, [a for _, a in ins], [spec(k, a.shape) for k, a in ins],
        [jax.ShapeDtypeStruct(tuple(s), d) for _, s, d in outs], [spec(k, s) for k, s, _ in outs], [])
    return res if host is None else (res, hosted)


def _matmul_tn(at, g, name, out_dtype=BF16):
    K, L = at.shape
    N = g.shape[1]
    tn = next(t for t in (512, 384, 256, 128) if N % t == 0)

    def body(a_ref, g_ref, o_ref):
        o_ref[...] = _dot(a_ref[...], g_ref[...]).astype(o_ref.dtype)

    return pl.pallas_call(
        body, name=name, grid=(N // tn,),
        in_specs=[pl.BlockSpec((K, L), lambda n: (0, 0)), pl.BlockSpec((L, tn), lambda n: (0, n))],
        out_specs=pl.BlockSpec((K, tn), lambda n: (0, n)),
        out_shape=jax.ShapeDtypeStruct((K, N), out_dtype),
        compiler_params=pltpu.CompilerParams(dimension_semantics=("arbitrary",), vmem_limit_bytes=VMEM_LIMIT),
    )(at, g)


def _matmul_tn_slots(at, g, name, host=None):
    K, L = at.shape
    n = g.shape[1] // N_DEV

    def body(a_ref, g_ref, o_ref):
        o_ref[...] = _dot(a_ref[...], g_ref[...]).astype(o_ref.dtype)

    res, hosted = _hosting_call(
        body, name, N_DEV, host, [at, g],
        [pl.BlockSpec((K, L), lambda d: (0, 0)), pl.BlockSpec((L, n), lambda d: (0, d))],
        [jax.ShapeDtypeStruct((N_DEV, K, n), BF16)], [pl.BlockSpec((None, K, n), lambda d: (d, 0, 0))], [])
    return res[0] if host is None else (res[0], hosted)


def _mm_slots(a16, w):
    return jnp.concatenate([_dot(a16, w[d]) for d in range(N_DEV)], axis=-1)


def _mm_slots_nt(g16, w):
    n = w.shape[2]
    out = _dot_nt(g16[:, 0:n], w[0])
    for d in range(1, N_DEV):
        out = out + _dot_nt(g16[:, d * n:(d + 1) * n], w[d])
    return out


class _Exchange:
    def __init__(self, ins, outs, scratch, start, finish):
        self.ins, self.outs, self.scratch, self.start, self.finish = ins, outs, scratch, start, finish


def _xyc():
    return lax.axis_index("x"), lax.axis_index("y"), lax.axis_index("c")


def _plan_all_gather(xs):
    n = len(xs)

    def build(x_refs, out_refs, sems):
        send_sems, recv_sems, local_sems = sems
        x, y, c = _xyc()

        def copies(k, block, to, own=False):
            slot = 4 * block[0] + 2 * block[1] + block[2]
            return [pltpu.make_async_remote_copy(
                src_ref=x_refs[a] if own else out_refs[a].at[slot], dst_ref=out_refs[a].at[slot],
                send_sem=send_sems.at[k * n + a], recv_sem=recv_sems.at[k * n + a], device_id=to,
                device_id_type=MESH) for a in range(n)]

        mine = [pltpu.make_async_copy(x_refs[a], out_refs[a].at[4 * x + 2 * y + c], local_sems.at[a])
                for a in range(n)]
        return copies, mine, (x, y, c), [(1 - x, y), (x, 1 - y), (1 - x, 1 - y)]

    def first_copies(copies, me, chips):
        x, y, c = me
        first = copies(0, me, (x, y, 1 - c), own=True)
        for j, chip in enumerate(chips):
            first += copies(1 + j, me, (*chip, c), own=True)
        return first

    def start(x_refs, out_refs, sems):
        copies, mine, me, chips = build(x_refs, out_refs, sems)
        for cp in mine + first_copies(copies, me, chips):
            cp.start()

    def finish(x_refs, out_refs, sems):
        copies, mine, me, chips = build(x_refs, out_refs, sems)
        x, y, c = me
        passed = []
        for j, chip in enumerate(chips):
            for cp in copies(1 + j, (*chip, c), me):
                cp.wait_recv()
            fwd = copies(4 + j, (*chip, c), (x, y, 1 - c))
            for cp in fwd:
                cp.start()
            passed += fwd
        for cp in copies(0, (x, y, 1 - c), me):
            cp.wait_recv()
        for j, chip in enumerate(chips):
            for cp in copies(4 + j, (*chip, 1 - c), me):
                cp.wait_recv()
        for cp in first_copies(copies, me, chips) + passed:
            cp.wait_send()
        for cp in mine:
            cp.wait()

    return _Exchange(list(xs), [jax.ShapeDtypeStruct((N_DEV,) + a.shape, a.dtype) for a in xs],
                     [pltpu.SemaphoreType.DMA((7 * n,)), pltpu.SemaphoreType.DMA((7 * n,)),
                      pltpu.SemaphoreType.DMA((n,))], start, finish)


_CHIPS = ((0, 0), (0, 1), (1, 0), (1, 1))


def _plan_pair(sends):
    n = len(sends)

    def build(s_refs, o_refs, sems):
        send_sems, recv_sems = sems
        x, y, c = _xyc()
        return [pltpu.make_async_remote_copy(
            src_ref=s_refs[a].at[4 * px + 2 * py + 1 - c], dst_ref=o_refs[a].at[j],
            send_sem=send_sems.at[j * n + a], recv_sem=recv_sems.at[j * n + a], device_id=(x, y, 1 - c),
            device_id_type=MESH) for j, (px, py) in enumerate(_CHIPS) for a in range(n)]

    def start(s_refs, o_refs, sems):
        for cp in build(s_refs, o_refs, sems):
            cp.start()

    def finish(s_refs, o_refs, sems):
        for cp in build(s_refs, o_refs, sems):
            cp.wait_recv()
            cp.wait_send()

    return _Exchange(list(sends), [jax.ShapeDtypeStruct((4,) + a.shape[1:], a.dtype) for a in sends],
                     [pltpu.SemaphoreType.DMA((4 * n,)), pltpu.SemaphoreType.DMA((4 * n,))], start, finish)


def _plan_chips(ts):
    n = len(ts)
    flips = ((1, 0), (0, 1), (1, 1))

    def build(t_refs, o_refs, sems):
        send_sems, recv_sems, local_sems = sems
        x, y, c = _xyc()
        mine = 2 * x + y
        local = [pltpu.make_async_copy(t_refs[a].at[mine], o_refs[a].at[mine], local_sems.at[a]) for a in range(n)]
        remote = []
        for k, (fx, fy) in enumerate(flips):
            px = 1 - x if fx else x
            py = 1 - y if fy else y
            remote += [pltpu.make_async_remote_copy(
                src_ref=t_refs[a].at[2 * px + py], dst_ref=o_refs[a].at[mine],
                send_sem=send_sems.at[k * n + a], recv_sem=recv_sems.at[k * n + a], device_id=(px, py, c),
                device_id_type=MESH) for a in range(n)]
        return local, remote

    def start(t_refs, o_refs, sems):
        local, remote = build(t_refs, o_refs, sems)
        for cp in local + remote:
            cp.start()

    def finish(t_refs, o_refs, sems):
        local, remote = build(t_refs, o_refs, sems)
        for cp in remote:
            cp.wait_recv()
        for cp in remote:
            cp.wait_send()
        for cp in local:
            cp.wait()

    return _Exchange(list(ts), [jax.ShapeDtypeStruct(a.shape, a.dtype) for a in ts],
                     [pltpu.SemaphoreType.DMA((3 * n,)), pltpu.SemaphoreType.DMA((3 * n,)),
                      pltpu.SemaphoreType.DMA((n,))], start, finish)


def _exchange_call(plan, name):
    n = len(plan.ins)

    def body(*refs):
        ins, outs, sems = refs[:n], refs[n:2 * n], refs[2 * n:]
        plan.start(ins, outs, sems)
        plan.finish(ins, outs, sems)

    return pl.pallas_call(
        body, name=name, out_shape=plan.outs,
        in_specs=[pl.BlockSpec(memory_space=pl.ANY)] * n, out_specs=[pl.BlockSpec(memory_space=pl.ANY)] * n,
        scratch_shapes=plan.scratch,
    )(*plan.ins)


def _pair_add(sends, fromsib, name):
    n = len(sends)
    nb = 8

    def body(*refs):
        c = lax.axis_index("c")
        for a in range(n):
            s_ref, f_ref, t_ref = refs[a], refs[n + a], refs[2 * n + a]
            for j in range(4):
                t_ref[j] = (s_ref[2 * j + c].astype(F32) + f_ref[j].astype(F32)).astype(t_ref.dtype)

    def spec(a, lead):
        return pl.BlockSpec((lead, a.shape[1] // nb, a.shape[2]), lambda i: (0, i, 0))

    return pl.pallas_call(
        body, name=name, grid=(nb,),
        in_specs=[spec(a, N_DEV) for a in sends] + [spec(a, 4) for a in fromsib],
        out_specs=[spec(a, 4) for a in fromsib],
        out_shape=[jax.ShapeDtypeStruct(a.shape, a.dtype) for a in fromsib],
        compiler_params=pltpu.CompilerParams(dimension_semantics=("arbitrary",), vmem_limit_bytes=VMEM_LIMIT),
    )(*sends, *fromsib)


def _adamw_vals(w, g, m, v):
    m2 = ADAM_B1 * m + (1.0 - ADAM_B1) * g
    v2 = ADAM_B2 * v + (1.0 - ADAM_B2) * (g * g)
    m_hat = m2 / (1.0 - ADAM_B1 ** ADAM_STEP)
    v_hat = v2 / (1.0 - ADAM_B2 ** ADAM_STEP)
    delta = -ADAM_LR * (m_hat / (jnp.sqrt(v_hat) + ADAM_EPS) + ADAM_WD * w)
    return delta, m2, v2


def _sum_adamw(recv, w, m, v, name):
    R, C = w.shape
    ns = recv.shape[0]
    br = next((t for t in (256, 128, 64, 32, 16) if R % t == 0), R)

    def body(r_ref, w_ref, m_ref, v_ref, g_ref, d_ref, m2_ref, v2_ref):
        g = r_ref[0].astype(F32)
        for d in range(1, ns):
            g = g + r_ref[d].astype(F32)
        dl, m2, v2 = _adamw_vals(w_ref[...], g, m_ref[...], v_ref[...])
        g_ref[...] = g
        d_ref[...] = dl
        m2_ref[...] = m2
        v2_ref[...] = v2

    spec = pl.BlockSpec((br, C), lambda i: (i, 0))
    return pl.pallas_call(
        body, name=name, grid=(R // br,),
        in_specs=[pl.BlockSpec((ns, br, C), lambda i: (0, i, 0)), spec, spec, spec], out_specs=[spec] * 4,
        out_shape=[jax.ShapeDtypeStruct((R, C), F32)] * 4,
        compiler_params=pltpu.CompilerParams(dimension_semantics=("arbitrary",)),
    )(recv, w, m, v)


def _updates_call(recvs, ws, ms, vs, name, host=None):
    n = len(recvs)
    nb = 8

    def body(*refs):
        for a in range(n):
            r_ref, w_ref, m_ref, v_ref = refs[a], refs[n + a], refs[2 * n + a], refs[3 * n + a]
            g_ref, d_ref, m2_ref, v2_ref = refs[4 * n + 4 * a:4 * n + 4 * a + 4]
            g = r_ref[0].astype(F32)
            for d in range(1, r_ref.shape[0]):
                g = g + r_ref[d].astype(F32)
            dl, m2, v2 = _adamw_vals(w_ref[...], g, m_ref[...], v_ref[...])
            g_ref[...] = g
            d_ref[...] = dl
            m2_ref[...] = m2
            v2_ref[...] = v2

    def spec2(w):
        return pl.BlockSpec((w.shape[0] // nb, w.shape[1]), lambda i: (i, 0))

    def spec3(r):
        return pl.BlockSpec((r.shape[0], r.shape[1] // nb, r.shape[2]), lambda i: (0, i, 0))

    res, hosted = _hosting_call(
        body, name, nb, host, list(recvs) + list(ws) + list(ms) + list(vs),
        [spec3(r) for r in recvs] + [spec2(w) for w in ws] * 3,
        [jax.ShapeDtypeStruct(w.shape, F32) for w in ws for _ in range(4)],
        [spec2(w) for w in ws for _ in range(4)], [])
    return [res[4 * a:4 * a + 4] for a in range(n)], hosted


def _small_sum(gath, loss_g, name):
    _, R, C = gath.shape
    br = R // 3

    def body(g_ref, l_ref, go_ref, lo_ref):
        g = g_ref[0].astype(F32)
        lsum = l_ref[0]
        for d in range(1, N_DEV):
            g = g + g_ref[d].astype(F32)
            lsum = lsum + l_ref[d]
        go_ref[...] = g
        lo_ref[...] = lsum

    return pl.pallas_call(
        body, name=name, grid=(R // br,),
        in_specs=[pl.BlockSpec((N_DEV, br, C), lambda i: (0, i, 0)),
                  pl.BlockSpec((N_DEV, 8, HD), lambda i: (0, 0, 0))],
        out_specs=[pl.BlockSpec((br, C), lambda i: (i, 0)), pl.BlockSpec((8, HD), lambda i: (0, 0))],
        out_shape=[jax.ShapeDtypeStruct((R, C), F32), jax.ShapeDtypeStruct((8, HD), F32)],
        compiler_params=pltpu.CompilerParams(dimension_semantics=("arbitrary",)),
    )(gath, loss_g)


def _adamw_multi(ws, gs, ms, vs, name, nblk=1):
    n = len(ws)

    def body(*refs):
        for a in range(n):
            dl, m2, v2 = _adamw_vals(refs[a][...], refs[n + a][...], refs[2 * n + a][...], refs[3 * n + a][...])
            refs[4 * n + 3 * a][...] = dl
            refs[4 * n + 3 * a + 1][...] = m2
            refs[4 * n + 3 * a + 2][...] = v2

    def spec(x):
        rest = (0,) * (x.ndim - 1)
        return pl.BlockSpec((x.shape[0] // nblk,) + tuple(x.shape[1:]), lambda i: (i,) + rest)

    res = pl.pallas_call(
        body, name=name, grid=(nblk,),
        in_specs=[spec(w) for w in ws] * 4, out_specs=[spec(w) for w in ws for _ in range(3)],
        out_shape=[jax.ShapeDtypeStruct(w.shape, F32) for w in ws for _ in range(3)],
        compiler_params=pltpu.CompilerParams(dimension_semantics=("arbitrary",), vmem_limit_bytes=VMEM_LIMIT),
    )(*ws, *gs, *ms, *vs)
    return [res[3 * a:3 * a + 3] for a in range(n)]


def _s5_param_fn(lr, li, ls, btr, bti):
    step = jnp.exp(ls)
    er = jnp.exp(lr * step)
    ang = li * step
    ar = er * jnp.cos(ang)
    ai = er * jnp.sin(ang)
    nr = ar - 1.0
    den = lr * lr + li * li
    fr = (nr * lr + ai * li) / den
    fi = (ai * lr - nr * li) / den
    return ar, ai, fr * btr - fi * bti, fr * bti + fi * btr


def _s5_params(lr, li, ls, btr, bti):
    def body(lr_ref, li_ref, ls_ref, br_ref, bi_ref, ar_ref, ai_ref, bbr_ref, bbi_ref):
        ar, ai, bbr, bbi = _s5_param_fn(lr_ref[...], li_ref[...], ls_ref[...], br_ref[...], bi_ref[...])
        ar_ref[...] = ar
        ai_ref[...] = ai
        bbr_ref[...] = bbr
        bbi_ref[...] = bbi

    sd = jax.ShapeDtypeStruct
    return pl.pallas_call(
        body, name="s5_params",
        out_shape=[sd(lr.shape, F32), sd(lr.shape, F32), sd(btr.shape, F32), sd(btr.shape, F32)],
    )(lr, li, ls, btr, bti)


def _s5_params_bwd(lr, li, ls, btr, bti, dar, dai, dbbr, dbbi):
    def body(lr_ref, li_ref, ls_ref, br_ref, bi_ref, dar_ref, dai_ref, dbbr_ref, dbbi_ref,
             dlr_ref, dli_ref, dls_ref, dbr_ref, dbi_ref):
        _, vjp = jax.vjp(_s5_param_fn, lr_ref[...], li_ref[...], ls_ref[...], br_ref[...], bi_ref[...])
        dlr, dli, dls, dbr, dbi = vjp((dar_ref[...], dai_ref[...], dbbr_ref[...], dbbi_ref[...]))
        dlr_ref[...] = dlr
        dli_ref[...] = dli
        dls_ref[...] = dls
        dbr_ref[...] = dbr
        dbi_ref[...] = dbi

    sd = jax.ShapeDtypeStruct
    return pl.pallas_call(
        body, name="s5_params_bwd",
        out_shape=[sd(lr.shape, F32), sd(lr.shape, F32), sd(ls.shape, F32), sd(btr.shape, F32), sd(btr.shape, F32)],
    )(lr, li, ls, btr, bti, dar, dai, dbbr, dbbi)


def _cpow(ar, ai, n):
    assert n & (n - 1) == 0
    while n > 1:
        ar, ai = ar * ar - ai * ai, 2.0 * ar * ai
        n //= 2
    return ar, ai


def _scan(st, cr, ci, init, nk, reverse, store, prev=None):
    W = S5_W

    def step(j, carry):
        k = nk - 1 - j if reverse else j
        rows = pl.ds(pl.multiple_of(k * 8, 8), 8)
        sr, si = carry[0], carry[1]
        nsr = cr * sr - ci * si + st[rows, 0:W]
        nsi = cr * si + ci * sr + st[rows, W:2 * W]
        if store:
            st[rows, 0:W] = nsr
            st[rows, W:2 * W] = nsi
        if prev is None:
            return nsr, nsi
        prows = pl.ds(pl.multiple_of(jnp.maximum(k - 1, 0) * 8, 8), 8)
        w = jnp.where(k > 0, 1.0, 0.0).astype(F32)
        pr = prev[prows, 0:W] * w
        pi = prev[prows, W:2 * W] * w
        return nsr, nsi, carry[2] + nsr * pr + nsi * pi, carry[3] + nsi * pr - nsr * pi

    return lax.fori_loop(0, nk, step, init, unroll=2)


def _chain(fin, fr, fi, pr, pi, reverse):
    W = S5_W
    fin[:, 0:W] = fr
    fin[:, W:2 * W] = fi
    rowid = lax.broadcasted_iota(jnp.int32, (8, W), 0)
    cr = jnp.zeros((1, W), F32)
    ci = jnp.zeros((1, W), F32)
    init_r = jnp.zeros((8, W), F32)
    init_i = jnp.zeros((8, W), F32)
    for s in (range(7, -1, -1) if reverse else range(8)):
        init_r = jnp.where(rowid == s, cr, init_r)
        init_i = jnp.where(rowid == s, ci, init_i)
        lr = fin[s:s + 1, 0:W]
        li = fin[s:s + 1, W:2 * W]
        cr, ci = lr + pr * cr - pi * ci, li + pr * ci + pi * cr
    return init_r, init_i


def _full_scan(st, fin, ar, ai, nk, reverse, prev=None):
    W = S5_W
    cr = jnp.broadcast_to(ar, (8, W))
    ci = jnp.broadcast_to(-ai if reverse else ai, (8, W))
    z = jnp.zeros((8, W), F32)
    fr, fi = _scan(st, cr, ci, (z, z), nk, reverse, store=False)
    pr, pi = _cpow(ar, -ai if reverse else ai, nk)
    init = _chain(fin, fr, fi, pr, pi, reverse)
    if prev is None:
        return _scan(st, cr, ci, init, nk, reverse, store=True)
    return _scan(st, cr, ci, init + (z, z), nk, reverse, store=True, prev=prev)


def _s5_specs(L):
    W2 = 2 * S5_W
    GC = S5_GB * S5_C
    col = pl.BlockSpec((L, GC), lambda g: (0, g))
    vec = pl.BlockSpec((1, GC), lambda g: (0, g))
    avec = pl.BlockSpec((1, S5_W), lambda g: (0, g))
    bmat = pl.BlockSpec((None, GC, W2), lambda g: (g, 0, 0))
    cmat = pl.BlockSpec((None, W2, GC), lambda g: (g, 0, 0))
    return col, vec, avec, bmat, cmat


def _interleave(dst, src, nk):
    for s in range(8):
        dst[pl.ds(s, nk, stride=8), :] = src[s * nk:(s + 1) * nk, :]


def _deinterleave(dst, src, nk):
    for s in range(8):
        dst[s * nk:(s + 1) * nk, :] = src[pl.ds(s, nk, stride=8), :].astype(dst.dtype)


def _hosting_call(body, name, nsteps, host, ins, in_specs, outs, out_specs, scratch):
    grid = (nsteps,) if isinstance(nsteps, int) else tuple(nsteps)
    params = pltpu.CompilerParams(dimension_semantics=("arbitrary",) * len(grid), vmem_limit_bytes=VMEM_LIMIT)
    if host is None:
        res = pl.pallas_call(
            body, name=name, grid=grid, in_specs=in_specs, out_specs=out_specs, out_shape=outs,
            scratch_shapes=scratch, compiler_params=params,
        )(*ins)
        return list(res), []
    n_in, n_out, n_sc = len(ins), len(outs), len(scratch)
    h_in, h_out = len(host.ins), len(host.outs)

    def hosted(*refs):
        a = refs[:n_in]
        ha = refs[n_in:n_in + h_in]
        o = refs[n_in + h_in:n_in + h_in + n_out]
        ho = refs[n_in + h_in + n_out:n_in + h_in + n_out + h_out]
        sc = refs[n_in + h_in + n_out + h_out:n_in + h_in + n_out + h_out + n_sc]
        hs = refs[n_in + h_in + n_out + h_out + n_sc:]
        first = functools.reduce(jnp.logical_and, [pl.program_id(i) == 0 for i in range(len(grid))])
        last = functools.reduce(jnp.logical_and, [pl.program_id(i) == g - 1 for i, g in enumerate(grid)])

        @pl.when(first)
        def _():
            host.start(ha, ho, hs)

        body(*a, *o, *sc)

        @pl.when(last)
        def _():
            host.finish(ha, ho, hs)

    hbm = pl.BlockSpec(memory_space=pl.ANY)
    res = pl.pallas_call(
        hosted, name=name, grid=grid,
        in_specs=list(in_specs) + [hbm] * h_in, out_specs=list(out_specs) + [hbm] * h_out,
        out_shape=list(outs) + list(host.outs), scratch_shapes=list(scratch) + list(host.scratch),
        compiler_params=params,
    )(*ins, *host.ins)
    return list(res[:n_out]), list(res[n_out:])


def _s5_fwd(u, bm, cm, ar, ai, dvec, host=None):
    L = u.shape[0]
    nk = L // 8
    GC = S5_GB * S5_C
    col, vec, avec, bmat, cmat = _s5_specs(L)

    def body(u_ref, b_ref, c_ref, ar_ref, ai_ref, d_ref, y_ref, st, fin, ui, yi):
        _interleave(ui, u_ref, nk)
        for r in range(8):
            rows = slice(r * nk, (r + 1) * nk)
            st[rows, :] = _dot(ui[rows, :].astype(BF16), b_ref[...])
        _full_scan(st, fin, ar_ref[...], ai_ref[...], nk, reverse=False)
        for r in range(8):
            rows = slice(r * nk, (r + 1) * nk)
            yi[rows, :] = _dot(st[rows, :].astype(BF16), c_ref[...]) + d_ref[...] * ui[rows, :]
        _deinterleave(y_ref, yi, nk)

    return _hosting_call(
        body, "s5_fwd", S5_G // S5_GB, host,
        [u, bm, cm, ar, ai, dvec], [col, bmat, cmat, avec, avec, vec],
        [jax.ShapeDtypeStruct(u.shape, F32)], [col],
        [pltpu.VMEM((L, 2 * S5_W), F32), pltpu.VMEM((8, 2 * S5_W), F32), pltpu.VMEM((L, GC), F32),
         pltpu.VMEM((L, GC), F32)])


def _s5_bwd(u, dy, bm, bmt, cmt, ar, ai, dvec, mask, rmat, host=None):
    L = u.shape[0]
    nk = L // 8
    W = S5_W
    GC = S5_GB * S5_C
    col, vec, avec, bmat, cmat = _s5_specs(L)
    hi = lax.Precision.HIGHEST

    def body(u_ref, dy_ref, b_ref, bt_ref, ct_ref, ar_ref, ai_ref, d_ref, mask_ref, r_ref,
             du_ref, db_ref, dc_ref, dd_ref, dar_ref, dai_ref, sa, sb, fin, ui, dyi, dui):
        ar = ar_ref[...]
        ai = ai_ref[...]
        _interleave(ui, u_ref, nk)
        _interleave(dyi, dy_ref, nk)
        for r in range(8):
            rows = slice(r * nk, (r + 1) * nk)
            sa[rows, :] = _dot(ui[rows, :].astype(BF16), b_ref[...])
            sb[rows, :] = _dot(dyi[rows, :].astype(BF16), ct_ref[...])
        _full_scan(sa, fin, ar, ai, nk, reverse=False)
        gr, gi, accr, acci = _full_scan(sb, fin, ar, ai, nk, reverse=True, prev=sa)
        rowid = lax.broadcasted_iota(jnp.int32, (8, W), 0)
        last = pl.ds((nk - 1) * 8, 8)
        pr = jnp.where(rowid == 0, 0.0, pltpu.roll(sa[last, 0:W], 1, 0))
        pi = jnp.where(rowid == 0, 0.0, pltpu.roll(sa[last, W:2 * W], 1, 0))
        accr = accr + gr * pr + gi * pi
        acci = acci + gi * pr - gr * pi
        dar_ref[...] = jnp.sum(accr, axis=0, keepdims=True)
        dai_ref[...] = jnp.sum(acci, axis=0, keepdims=True)
        dbf = jnp.zeros((GC, 2 * W), F32)
        dcf = jnp.zeros((GC, 2 * W), F32)
        dd = jnp.zeros((1, GC), F32)
        for r in range(8):
            rows = slice(r * nk, (r + 1) * nk)
            ub = ui[rows, :]
            dyb = dyi[rows, :]
            gb = sb[rows, :].astype(BF16)
            dui[rows, :] = _dot(gb, bt_ref[...]) + d_ref[...] * dyb
            dbf = dbf + _dot_tn(ub.astype(BF16), gb)
            dcf = dcf + _dot_tn(dyb.astype(BF16), sa[rows, :].astype(BF16))
            dd = dd + jnp.sum(dyb * ub, axis=0, keepdims=True)
        db_ref[...] = jnp.dot(dbf * mask_ref[...], r_ref[...], precision=hi, preferred_element_type=F32)
        dc_ref[...] = jnp.dot(dcf * mask_ref[...], r_ref[...], precision=hi, preferred_element_type=F32)
        dd_ref[...] = dd
        _deinterleave(du_ref, dui, nk)

    cmp_spec = pl.BlockSpec((GC, 2 * S5_P), lambda g: (g, 0))
    whole = lambda shape: pl.BlockSpec(shape, lambda g: (0, 0))
    sd = jax.ShapeDtypeStruct
    return _hosting_call(
        body, "s5_bwd", S5_G // S5_GB, host,
        [u, dy, bm, bmt, cmt, ar, ai, dvec, mask, rmat],
        [col, col, bmat, cmat, bmat, avec, avec, vec, whole(mask.shape), whole(rmat.shape)],
        [sd(u.shape, BF16), sd((S5_G * S5_C, 2 * S5_P), F32), sd((S5_G * S5_C, 2 * S5_P), F32),
         sd((1, PRIM), F32), sd((1, S5_G * S5_P), F32), sd((1, S5_G * S5_P), F32)],
        [col, cmp_spec, cmp_spec, vec, avec, avec],
        [pltpu.VMEM((L, 2 * W), F32), pltpu.VMEM((L, 2 * W), F32), pltpu.VMEM((8, 2 * W), F32),
         pltpu.VMEM((L, GC), F32), pltpu.VMEM((L, GC), F32), pltpu.VMEM((L, GC), F32)])


def _s5_mats(bbr, bbi, cre, cim):
    nb = S5_G // S5_GB
    eye = jnp.eye(S5_GB, dtype=F32)
    bb = jnp.stack([bbr, bbi], axis=2).reshape(nb, S5_GB, S5_C, 2, S5_P)
    bm = jnp.einsum('ngcrp,gh->ngcrhp', bb, eye).reshape(nb, S5_GB * S5_C, 2 * S5_W)
    cc = jnp.stack([cre, -cim], axis=2).reshape(nb, S5_GB, S5_C, 2, S5_P)
    cmt = jnp.einsum('ngcrp,gh->ngcrhp', cc, eye).reshape(nb, S5_GB * S5_C, 2 * S5_W)
    return (bm.astype(BF16), jnp.swapaxes(bm, 1, 2).astype(BF16),
            jnp.swapaxes(cmt, 1, 2).astype(BF16), cmt.astype(BF16))


def _s5_compact_consts():
    g_row = np.arange(S5_GB * S5_C) // S5_C
    col = np.arange(2 * S5_W)
    g_col = (col % S5_W) // S5_P
    mask = (g_row[:, None] == g_col[None, :]).astype(np.float32)
    tgt = (col // S5_W) * S5_P + col % S5_P
    rmat = (tgt[:, None] == np.arange(2 * S5_P)[None, :]).astype(np.float32)
    return jnp.asarray(mask), jnp.asarray(rmat)


def _attn_scores(q_ref, k_ref, qb, bq, scale):
    ext = (qb + 1) * bq
    s = _dot_nt(q_ref[qb * bq:ext, :], k_ref[0:ext, :]) * scale
    qpos = lax.broadcasted_iota(jnp.int32, (bq, bq), 0)
    kpos = lax.broadcasted_iota(jnp.int32, (bq, bq), 1)
    diag = jnp.where(kpos <= qpos, s[:, ext - bq:], NEG)
    return diag if qb == 0 else jnp.concatenate([s[:, :ext - bq], diag], axis=-1)


def _attn_fwd(qp, kp, v, scale):
    L = qp.shape[0]
    bq = min(256, L)

    def body(q_ref, k_ref, v_ref, o_ref, lse_ref):
        for qb in range(L // bq):
            rows = slice(qb * bq, (qb + 1) * bq)
            s = _attn_scores(q_ref, k_ref, qb, bq, scale)
            m = jnp.max(s, axis=-1, keepdims=True)
            e = jnp.exp(s - m)
            l = jnp.sum(e, axis=-1, keepdims=True)
            o_ref[rows, :] = _dot(e.astype(BF16), v_ref[0:(qb + 1) * bq, :]) / l
            lse_ref[rows, :] = jnp.broadcast_to(m + jnp.log(l), (bq, HD))

    blk = pl.BlockSpec((L, HD), lambda h: (0, h))
    wide = pl.BlockSpec((L, 2 * HD), lambda h: (0, h))
    return pl.pallas_call(
        body, name="mla_attn_fwd", grid=(MLA_H,),
        in_specs=[wide, wide, blk], out_specs=[blk, blk],
        out_shape=[jax.ShapeDtypeStruct((L, MLA_H * HD), F32)] * 2,
        compiler_params=pltpu.CompilerParams(dimension_semantics=("arbitrary",), vmem_limit_bytes=VMEM_LIMIT),
    )(qp, kp, v)


def _attn_bwd(qp, kp, v, o, lse, do, scale):
    L = qp.shape[0]
    bq = min(256, L)
    nq = L // bq

    def body(q_ref, k_ref, v_ref, o_ref, lse_ref, do_ref, dq_ref, dk_ref, dv_ref, dk_acc, dv_acc):
        dk_acc[...] = jnp.zeros_like(dk_acc)
        dv_acc[...] = jnp.zeros_like(dv_acc)
        for qb in range(nq):
            rows = slice(qb * bq, (qb + 1) * bq)
            ext = (qb + 1) * bq
            do = do_ref[rows, :]
            dob = do.astype(BF16)
            p = jnp.exp(_attn_scores(q_ref, k_ref, qb, bq, scale) - lse_ref[rows, 0:1])
            dp = _dot_nt(dob, v_ref[0:ext, :])
            dsum = jnp.sum(do * o_ref[rows, :], axis=-1, keepdims=True)
            ds = (p * (dp - dsum) * scale).astype(BF16)
            dq_ref[rows, :] = _dot(ds, k_ref[0:ext, :]).astype(dq_ref.dtype)
            dk_acc[0:ext, :] += _dot_tn(ds, q_ref[rows, :])
            dv_acc[0:ext, :] += _dot_tn(p.astype(BF16), dob)
        dk_ref[...] = dk_acc[...].astype(dk_ref.dtype)
        dv_ref[...] = dv_acc[...].astype(dv_ref.dtype)

    sd = jax.ShapeDtypeStruct
    blk = pl.BlockSpec((L, HD), lambda h: (0, h))
    wide = pl.BlockSpec((L, 2 * HD), lambda h: (0, h))
    return pl.pallas_call(
        body, name="mla_attn_bwd", grid=(MLA_H,),
        in_specs=[wide, wide, blk, blk, blk, blk], out_specs=[wide, wide, blk],
        out_shape=[sd((L, MLA_H * 2 * HD), BF16), sd((L, MLA_H * 2 * HD), BF16), sd((L, MLA_H * HD), BF16)],
        scratch_shapes=[pltpu.VMEM((L, 2 * HD), F32), pltpu.VMEM((L, HD), F32)],
        compiler_params=pltpu.CompilerParams(dimension_semantics=("arbitrary",), vmem_limit_bytes=VMEM_LIMIT),
    )(qp, kp, v, o, lse, do)


def _kv_fn(mem, gm, w, gk):
    kv = _mm(_rms(mem, gm, D_MODEL), w)
    k = jnp.concatenate([_rms(kv[:, HD * h:HD * (h + 1)], gk, HD) for h in range(X_HEADS)], axis=-1)
    return k, kv[:, XQ:]


def _kv_prep(mem, gm, w, gk, name):
    def fn(mem, gm, w, gk):
        return _kv_fn(mem, gm, w, gk)
    M = mem.shape[0]
    return _rowwise(name, fn, [('c', mem), ('c', gm), ('c', w), ('c', gk)],
                    [('c', (M, XQ), F32), ('c', (M, XQ), F32)], 1)


def _kv_prep_bwd(mem, gm, w, gk, dk, dv, name):
    def fn(mem, gm, w, gk, dk, dv):
        _, vjp = jax.vjp(lambda a, b, c: _kv_fn(mem, a, b, c), gm, w, gk)
        return vjp((dk, dv))
    return _rowwise(name, fn, [('c', mem), ('c', gm), ('c', w), ('c', gk), ('c', dk), ('c', dv)],
                    [('c', gm.shape, F32), ('c', w.shape, BF16), ('c', gk.shape, F32)], 1)


def _forward_merge(x, mix, mix_kind, xq, gate, k, v, gq, wout, name, nblk, sub, host=None):
    def fn(x, mix, xq, gate, k, v, gq, wout):
        o = _merge(mix, xq, gate, k, v, gq)
        return (x + _dot(o.astype(BF16), wout),)
    L = x.shape[0]
    out = _rowwise(name, fn, [('r', x), (mix_kind, mix), ('r', xq), ('r', gate), ('c', k), ('c', v), ('c', gq),
                              ('c', wout)], [('r', (L, D_MODEL), F32)], nblk, sub, host=host)
    return out[0] if host is None else (out[0][0], out[1])


def _backward_merge(dx, mix, mix_kind, xq, gate, k, v, gq, wout, name, nblk, sub):
    def fn(dx, mix, xq, gate, k, v, gq, wout):
        g16 = dx.astype(BF16)
        do = _dot_nt(g16, wout)
        o, vjp = jax.vjp(_merge, mix, xq, gate, k, v, gq)
        dmix, dxq, dgate, dk, dv, dgq = vjp(do)
        return dmix, dxq, dgate, o, g16, dk, dv, dgq
    L = dx.shape[0]
    return _rowwise(
        name, fn,
        [('r', dx), (mix_kind, mix), ('r', xq), ('r', gate), ('c', k), ('c', v), ('c', gq), ('c', wout)],
        [('r', (L, PRIM), F32), ('r', (L, XQ), BF16), ('r', (L, BRANCH), BF16), ('t', (BRANCH, L), BF16),
         ('r', (L, D_MODEL), BF16), ('a', k.shape, F32), ('a', v.shape, F32), ('a', gq.shape, F32)], nblk, sub)


_MLA_IN = 3392
_MLA_IN_PAD = 3456


def _from_slots(g):
    _, k, n = g.shape
    return jnp.transpose(g, (1, 0, 2)).reshape(k, N_DEV * n)


def _to_slots(w):
    k = w.shape[0]
    return jnp.transpose(w.reshape(k, N_DEV, -1), (1, 0, 2))


def _uq_to_kernel(g):
    uq = _from_slots(g).reshape(Q_LORA, MLA_H, HD + ROPE)
    return jnp.concatenate([uq[:, :, :HD].reshape(Q_LORA, PRIM),
                            jnp.pad(uq[:, :, HD:], ((0, 0), (0, 0), (0, HD - ROPE))).reshape(Q_LORA, PRIM)], axis=1)


def _uq_from_kernel(d_w_q):
    uq = jnp.concatenate([d_w_q[:, :PRIM].reshape(Q_LORA, MLA_H, HD),
                          d_w_q[:, PRIM:].reshape(Q_LORA, MLA_H, HD)[:, :, :ROPE]], axis=2)
    return _to_slots(uq.reshape(Q_LORA, MLA_H * (HD + ROPE)))


def _mla_in_perm(w):
    return jnp.concatenate([w[:, :768], w[:, 832:], w[:, 768:832], jnp.zeros((w.shape[0], 64), w.dtype)], axis=1)


def _mla_in_unperm(w):
    return jnp.concatenate([w[:, :768], w[:, 3328:3392], w[:, 768:3328]], axis=1)


_SMALL = (("ln_gain", 2048), ("mem_norm", 2048), ("xq_norm", 256), ("xk_norm", 256), ("s5_lambda_re", 6144),
          ("s5_lambda_im", 6144), ("s5_log_step", 96), ("s5_b_re", 98304), ("s5_b_im", 98304), ("s5_c_re", 98304),
          ("s5_c_im", 98304), ("s5_d", 1536), ("mla_q_lora_norm", 512), ("mla_kv_lora_norm", 256),
          ("mla_q_nope_norm", 128), ("mla_k_nope_norm", 128), ("mla_q_rope_norm", 64), ("mla_k_rope_norm", 64))
_SMALL_ROWS = 432
_SMALL_OFF = {name: sum(n for _, n in _SMALL[:i]) for i, (name, _) in enumerate(_SMALL)}


def _pack_small(d):
    flat = jnp.concatenate([d[n].reshape(-1).astype(F32) for n, _ in _SMALL])
    return jnp.pad(flat, (0, _SMALL_ROWS * 1024 - flat.shape[0])).reshape(_SMALL_ROWS, 1024)


def _unpack_small(p, name, shape):
    off = _SMALL_OFF[name]
    return p.reshape(-1)[off:off + int(np.prod(shape))].reshape(shape)


_WEIGHTS = ('ln_gain', 'w_out', 'mem_norm', 'w_mem_kv', 'xq_norm', 'xk_norm', 's5_w_in', 's5_lambda_re',
            's5_lambda_im', 's5_log_step', 's5_b_re', 's5_b_im', 's5_c_re', 's5_c_im', 's5_d', 's5_w_glu', 'mla_w_in',
            'mla_q_lora_norm', 'mla_kv_lora_norm', 'mla_w_uq', 'mla_w_ukv', 'mla_q_nope_norm', 'mla_k_nope_norm',
            'mla_q_rope_norm', 'mla_k_rope_norm')
_BIG = ('w_out', 'w_mem_kv', 's5_w_in', 's5_w_glu', 'mla_w_in', 'mla_w_uq', 'mla_w_ukv')


def _pad128(g):
    return jnp.pad(g.reshape(1, -1), ((0, 0), (0, HD - g.shape[-1])))


def kernel(x, mem, positions, ln_gain, w_out, mem_norm, w_mem_kv, xq_norm, xk_norm, s5_w_in, s5_lambda_re, s5_lambda_im, s5_log_step, s5_b_re, s5_b_im, s5_c_re, s5_c_im, s5_d, s5_w_glu, mla_w_in, mla_q_lora_norm, mla_kv_lora_norm, mla_w_uq, mla_w_ukv, mla_q_nope_norm, mla_k_nope_norm, mla_q_rope_norm, mla_k_rope_norm, loss_target, m_ln_gain, m_w_out, m_mem_norm, m_w_mem_kv, m_xq_norm, m_xk_norm, m_s5_w_in, m_s5_lambda_re, m_s5_lambda_im, m_s5_log_step, m_s5_b_re, m_s5_b_im, m_s5_c_re, m_s5_c_im, m_s5_d, m_s5_w_glu, m_mla_w_in, m_mla_q_lora_norm, m_mla_kv_lora_norm, m_mla_w_uq, m_mla_w_ukv, m_mla_q_nope_norm, m_mla_k_nope_norm, m_mla_q_rope_norm, m_mla_k_rope_norm, v_ln_gain, v_w_out, v_mem_norm, v_w_mem_kv, v_xq_norm, v_xk_norm, v_s5_w_in, v_s5_lambda_re, v_s5_lambda_im, v_s5_log_step, v_s5_b_re, v_s5_b_im, v_s5_c_re, v_s5_c_im, v_s5_d, v_s5_w_glu, v_mla_w_in, v_mla_q_lora_norm, v_mla_kv_lora_norm, v_mla_w_uq, v_mla_w_ukv, v_mla_q_nope_norm, v_mla_k_nope_norm, v_mla_q_rope_norm, v_mla_k_rope_norm):
    weights = dict(ln_gain=ln_gain, w_out=w_out, mem_norm=mem_norm, w_mem_kv=w_mem_kv, xq_norm=xq_norm,
                   xk_norm=xk_norm, s5_w_in=s5_w_in, s5_lambda_re=s5_lambda_re, s5_lambda_im=s5_lambda_im,
                   s5_log_step=s5_log_step, s5_b_re=s5_b_re, s5_b_im=s5_b_im, s5_c_re=s5_c_re, s5_c_im=s5_c_im,
                   s5_d=s5_d, s5_w_glu=s5_w_glu, mla_w_in=mla_w_in, mla_q_lora_norm=mla_q_lora_norm,
                   mla_kv_lora_norm=mla_kv_lora_norm, mla_w_uq=mla_w_uq, mla_w_ukv=mla_w_ukv,
                   mla_q_nope_norm=mla_q_nope_norm, mla_k_nope_norm=mla_k_nope_norm,
                   mla_q_rope_norm=mla_q_rope_norm, mla_k_rope_norm=mla_k_rope_norm)
    m_in = dict(zip(_WEIGHTS, (m_ln_gain, m_w_out, m_mem_norm, m_w_mem_kv, m_xq_norm, m_xk_norm, m_s5_w_in,
                               m_s5_lambda_re, m_s5_lambda_im, m_s5_log_step, m_s5_b_re, m_s5_b_im, m_s5_c_re,
                               m_s5_c_im, m_s5_d, m_s5_w_glu, m_mla_w_in, m_mla_q_lora_norm, m_mla_kv_lora_norm,
                               m_mla_w_uq, m_mla_w_ukv, m_mla_q_nope_norm, m_mla_k_nope_norm, m_mla_q_rope_norm,
                               m_mla_k_rope_norm)))
    v_in = dict(zip(_WEIGHTS, (v_ln_gain, v_w_out, v_mem_norm, v_w_mem_kv, v_xq_norm, v_xk_norm, v_s5_w_in,
                               v_s5_lambda_re, v_s5_lambda_im, v_s5_log_step, v_s5_b_re, v_s5_b_im, v_s5_c_re,
                               v_s5_c_im, v_s5_d, v_s5_w_glu, v_mla_w_in, v_mla_q_lora_norm, v_mla_kv_lora_norm,
                               v_mla_w_uq, v_mla_w_ukv, v_mla_q_nope_norm, v_mla_k_nope_norm, v_mla_q_rope_norm,
                               v_mla_k_rope_norm)))

    x0 = x[0]
    mem0 = mem[0]
    target = loss_target[0]
    L = x0.shape[0]
    nblk, sub = 8, 1
    me = 4 * lax.axis_index("x") + 2 * lax.axis_index("y") + lax.axis_index("c")

    lora = jnp.pad(jnp.concatenate([mla_q_lora_norm, mla_kv_lora_norm], axis=1), ((0, 7), (0, HD - 96)))
    def gather(*shards):
        return _plan_all_gather([s.astype(BF16) for s in shards])

    (W_in_s5,) = _exchange_call(gather(s5_w_in[0]), "ag_s5_w_in")

    ln0, ln1 = ln_gain[0:1], ln_gain[1:2]
    gq0, gq1 = xq_norm[0:1], xq_norm[1:2]
    gk0, gk1 = xk_norm[0:1], xk_norm[1:2]
    gm0, gm1 = mem_norm[0:1], mem_norm[1:2]
    gqn, gkn = mla_q_nope_norm, mla_k_nope_norm
    gqr, gkr = _pad128(mla_q_rope_norm), _pad128(mla_k_rope_norm)

    lr3 = s5_lambda_re.reshape(S5_G, 1, S5_P)
    li3 = s5_lambda_im.reshape(S5_G, 1, S5_P)
    ls3 = s5_log_step.reshape(S5_G, 1, 1)
    btr = jnp.swapaxes(s5_b_re[0], 1, 2)
    bti = jnp.swapaxes(s5_b_im[0], 1, 2)
    a_r, a_i, bbr, bbi = _s5_params(lr3, li3, ls3, btr, bti)
    bm, bmt, cm, cmt = _s5_mats(bbr, bbi, s5_c_re[0], s5_c_im[0])
    a_r2 = a_r.reshape(1, S5_G * S5_P)
    a_i2 = a_i.reshape(1, S5_G * S5_P)
    cmask, rmat = _s5_compact_consts()

    half = ROPE // 2
    inv_freq = ROPE_THETA ** (-jnp.arange(half, dtype=F32) / half)
    invf = jnp.concatenate([inv_freq, inv_freq, jnp.zeros((HD - ROPE,), F32)]).reshape(1, HD)

    def rot_tables(pos, invf):
        ang = pos.astype(F32) * invf
        lane = lax.broadcasted_iota(jnp.int32, ang.shape, 1)
        c = jnp.where(lane < ROPE, jnp.cos(ang), 0.0)
        s = jnp.sin(ang)
        return c, jnp.where(lane < half, -s, 0.0), jnp.where((lane >= half) & (lane < ROPE), s, 0.0)

    tc, ts1, ts2 = _rowwise("rot_tables", rot_tables, [('r', positions.reshape(L, 1)), ('c', invf)],
                            [('r', (L, HD), F32)] * 3, nblk, sub)

    def in_s5(x, g, w):
        proj = _mm_slots(_rms(x, g, D_MODEL).astype(BF16), w)
        return proj[:, :PRIM], proj[:, PRIM:PRIM + XQ], proj[:, PRIM + XQ:]

    (u_s5, xq_a, gate_a), (G_mkv0, G_uq) = _rowwise(
        "s5_in", in_s5, [('r', x0), ('c', ln0), ('c', W_in_s5)],
        [('r', (L, PRIM), F32), ('r', (L, XQ), F32), ('r', (L, BRANCH), F32)], nblk, sub,
        host=gather(w_mem_kv[0], mla_w_uq[0]))
    (y_s5,), (W_glu, G_out0) = _s5_fwd(u_s5, bm, cm, a_r2, a_i2, s5_d, host=gather(s5_w_glu[0], w_out[0]))

    def glu(y, w):
        z = _mm_slots(_gelu(y).astype(BF16), w)
        return (z[:, :PRIM] * _sigmoid(z[:, PRIM:]),)

    (y2,), (G_in_mla,) = _rowwise("s5_glu", glu, [('r', y_s5), ('c', W_glu)], [('r', (L, PRIM), F32)], nblk, sub,
                                  host=gather(mla_w_in[0]))
    W_mkv0 = G_mkv0.reshape(D_MODEL, 2 * XQ)
    k_a, v_a = _kv_prep(mem0, gm0, W_mkv0, gk0, "kv_prep0")
    x1, (W_kv, G_mkv1, G_lora) = _forward_merge(
        x0, y2, 'r', xq_a, gate_a, k_a, v_a, gq0, G_out0.reshape(BRANCH, D_MODEL), "merge0", nblk, sub,
        host=_plan_all_gather([mla_w_ukv[0].astype(BF16), w_mem_kv[1].astype(BF16), lora]))
    W_in_mla = _mla_in_perm(_from_slots(G_in_mla))
    W_q = _uq_to_kernel(G_uq)
    g_qlora = G_lora[:, 0, :64].reshape(1, Q_LORA)
    g_kvlora = G_lora[:, 0, 64:96].reshape(1, KV_LORA)

    def in_mla(x, g, w):
        proj = _dot(_rms(x, g, D_MODEL).astype(BF16), w)
        return proj[:, :512], proj[:, 512:768], proj[:, 768:1280], proj[:, 1280:3328], proj[:, 3328:]

    (c_q, c_kv, xq_b, gate_b, krp), (G_out1,) = _rowwise(
        "mla_in", in_mla, [('r', x1), ('c', ln1), ('c', W_in_mla)],
        [('r', (L, Q_LORA), F32), ('r', (L, KV_LORA), F32), ('r', (L, XQ), F32), ('r', (L, BRANCH), F32),
         ('r', (L, HD), F32)], nblk, sub, host=gather(w_out[1]))
    W_out = (G_out0.reshape(BRANCH, D_MODEL), G_out1.reshape(BRANCH, D_MODEL))
    W_mkv = (W_mkv0, G_mkv1.reshape(D_MODEL, 2 * XQ))

    def qkv(c_q, c_kv, krp, tc, ts1, ts2, gql, gkvl, wq, wkv, gqn, gkn, gqr, gkr):
        q = _dot(_rms(c_q, gql, Q_LORA).astype(BF16), wq)
        kv = _mm_slots(_rms(c_kv, gkvl, KV_LORA).astype(BF16), wkv)
        kp, v = _kv_post(kv, krp, gkn, gkr, tc, ts1, ts2)
        return _q_post(q, gqn, gqr, tc, ts1, ts2), kp, v

    qkv_consts = [('c', g_qlora), ('c', g_kvlora), ('c', W_q), ('c', W_kv), ('c', gqn), ('c', gkn), ('c', gqr),
                  ('c', gkr)]
    q_pad, k_pad, v_h = _rowwise(
        "mla_qkv", qkv, [('r', c_q), ('r', c_kv), ('r', krp), ('r', tc), ('r', ts1), ('r', ts2)] + qkv_consts,
        [('r', (L, 2 * PRIM), BF16), ('r', (L, 2 * PRIM), BF16), ('r', (L, PRIM), BF16)], nblk, sub)
    scale = (HD + ROPE) ** -0.5
    attn, lse = _attn_fwd(q_pad, k_pad, v_h, scale)
    k_b, v_b = _kv_prep(mem0, gm1, W_mkv[1], gk1, "kv_prep1")

    def merge_loss(x, mix, xq, gate, k, v, gq, wout, t):
        err = x + _dot(_merge(mix, xq, gate, k, v, gq).astype(BF16), wout) - t
        part = 0.5 * jnp.sum(jnp.sum(err * err, axis=-1, keepdims=True) * (1.0 / D_MODEL), axis=0, keepdims=True)
        return err * (1.0 / D_MODEL), jnp.broadcast_to(part, (1, HD))

    dx2, loss_part = _rowwise(
        "merge1_loss", merge_loss,
        [('r', x1), ('r', attn), ('r', xq_b), ('r', gate_b), ('c', k_b), ('c', v_b), ('c', gq1), ('c', W_out[1]),
         ('r', target)], [('r', (L, D_MODEL), F32), ('a', (1, HD), F32)], nblk, sub)

    dattn, dxq_b, dgate_b, o_b, g_b, dk_b, dv_b, dgq1 = _backward_merge(
        dx2, attn, 'r', xq_b, gate_b, k_b, v_b, gq1, W_out[1], "merge1_bwd", nblk, sub)
    dgm1, dW_mkv1, dgk1 = _kv_prep_bwd(mem0, gm1, W_mkv[1], gk1, dk_b, dv_b, "kv_prep1_bwd")
    dW_out1 = _matmul_tn(o_b, g_b, "dw_out1")
    dq_pad, dk_pad, dv_h = _attn_bwd(q_pad, k_pad, v_h, attn, lse, dattn, scale)

    def qkv_bwd(c_q, c_kv, krp, tc, ts1, ts2, dqp, dkp, dv, gql, gkvl, wq, wkv, gqn, gkn, gqr, gkr):
        cqn, vjp_qn = jax.vjp(lambda a, b: _rms(a, b, Q_LORA), c_q, gql)
        ckvn, vjp_kvn = jax.vjp(lambda a, b: _rms(a, b, KV_LORA), c_kv, gkvl)
        cqn16 = cqn.astype(BF16)
        ckvn16 = ckvn.astype(BF16)
        q = _dot(cqn16, wq)
        kv = _mm_slots(ckvn16, wkv)
        _, vjp_q = jax.vjp(lambda a, b, c: _q_post(a, b, c, tc, ts1, ts2), q, gqn, gqr)
        dq, dgqn, dgqr = vjp_q(dqp.astype(F32))
        _, vjp_kv = jax.vjp(lambda a, b, c, d: _kv_post(a, b, c, d, tc, ts1, ts2), kv, krp, gkn, gkr)
        dkv, dkrp, dgkn, dgkr = vjp_kv((dkp.astype(F32), dv.astype(F32)))
        dq16 = dq.astype(BF16)
        dkv16 = dkv.astype(BF16)
        dc_q, dgql = vjp_qn(_dot_nt(dq16, wq))
        dc_kv, dgkvl = vjp_kvn(_mm_slots_nt(dkv16, wkv))
        return dc_q, dc_kv, dkrp, cqn16, dq16, ckvn16, dkv16, dgql, dgkvl, dgqn, dgkn, dgqr, dgkr

    (dc_q, dc_kv, dkrp, cqn16, dq16, ckvn16, dkv16, dgql, dgkvl, dgqn, dgkn, dgqr, dgkr) = _rowwise(
        "mla_qkv_bwd", qkv_bwd,
        [('r', c_q), ('r', c_kv), ('r', krp), ('r', tc), ('r', ts1), ('r', ts2), ('r', dq_pad), ('r', dk_pad),
         ('r', dv_h)] + qkv_consts,
        [('r', (L, Q_LORA), BF16), ('r', (L, KV_LORA), BF16), ('r', (L, HD), BF16), ('t', (Q_LORA, L), BF16),
         ('r', (L, 2 * PRIM), BF16), ('t', (KV_LORA, L), BF16), ('r', (L, 2 * PRIM), BF16),
         ('a', (1, Q_LORA), F32), ('a', (1, KV_LORA), F32), ('a', (1, HD), F32), ('a', (1, HD), F32),
         ('a', (1, HD), F32), ('a', (1, HD), F32)], nblk, sub)
    dW_q = _matmul_tn(cqn16, dq16, "dw_uq")
    dW_kv = _matmul_tn_slots(ckvn16, dkv16, "dw_ukv")

    def in_bwd(x, dres, g, w, *dparts):
        dproj = jnp.concatenate(dparts, axis=-1).astype(BF16)
        xn, vjp = jax.vjp(lambda a, b: _rms(a, b, D_MODEL), x, g)
        dx, dg = vjp(_mm_slots_nt(dproj, w) if w.ndim == 3 else _dot_nt(dproj, w))
        return dx + dres, xn, dproj, dg

    dx1, xn1, dproj1, dln1 = _rowwise(
        "mla_in_bwd", in_bwd,
        [('r', x1), ('r', dx2), ('c', ln1), ('c', W_in_mla), ('r', dc_q), ('r', dc_kv), ('r', dxq_b), ('r', dgate_b),
         ('r', dkrp)],
        [('r', (L, D_MODEL), F32), ('t', (D_MODEL, L), BF16), ('r', (L, _MLA_IN_PAD), BF16), ('a', (1, D_MODEL), F32)],
        nblk, sub)
    dW_in_mla = _matmul_tn(xn1, dproj1, "dw_mla_in")

    dy2, dxq_a, dgate_a, o_a, g_a, dk_a, dv_a, dgq0 = _backward_merge(
        dx1, y2, 'r', xq_a, gate_a, k_a, v_a, gq0, W_out[0], "merge0_bwd", nblk, sub)
    dgm0, dW_mkv0, dgk0 = _kv_prep_bwd(mem0, gm0, W_mkv[0], gk0, dk_a, dv_a, "kv_prep0_bwd")
    dW_out0 = _matmul_tn(o_a, g_a, "dw_out0")

    def glu_bwd(y, dy2, w):
        h, vjp_h = jax.vjp(_gelu, y)
        h16 = h.astype(BF16)
        z = _mm_slots(h16, w)
        _, vjp_z = jax.vjp(lambda z: z[:, :PRIM] * _sigmoid(z[:, PRIM:]), z)
        dz16 = vjp_z(dy2)[0].astype(BF16)
        return vjp_h(_mm_slots_nt(dz16, w))[0], h16, dz16

    early = [dW_out1.reshape(N_DEV, 256, D_MODEL), dW_mkv1.reshape(N_DEV, 128, 2 * XQ),
             _to_slots(_mla_in_unperm(dW_in_mla)), _uq_from_kernel(dW_q), dW_kv,
             dW_out0.reshape(N_DEV, 256, D_MODEL), dW_mkv0.reshape(N_DEV, 128, 2 * XQ)]
    (dy_s5, h16, dz16), early_pair = _rowwise(
        "s5_glu_bwd", glu_bwd, [('r', y_s5), ('r', dy2), ('c', W_glu)],
        [('r', (L, PRIM), F32), ('t', (PRIM, L), BF16), ('r', (L, 2 * PRIM), BF16)], nblk, sub,
        host=_plan_pair(early))
    dW_glu = _matmul_tn_slots(h16, dz16, "dw_glu")
    early_t = _pair_add(early + [dW_glu], early_pair + list(_exchange_call(_plan_pair([dW_glu]), "rs_pair_glu")),
                        "rs_add_early")
    (du_s5, dbc, dcc, dd, dar, dai), early_recv = _s5_bwd(u_s5, dy_s5, bm, bmt, cmt, a_r2, a_i2, s5_d, cmask, rmat,
                                                          host=_plan_chips(early_t))
    dx0, xn0, dproj0, dln0 = _rowwise(
        "s5_in_bwd", in_bwd,
        [('r', x0), ('r', dx1), ('c', ln0), ('c', W_in_s5), ('r', du_s5), ('r', dxq_a),
         ('r', dgate_a)],
        [('r', (L, D_MODEL), F32), ('t', (D_MODEL, L), BF16), ('r', (L, 2 * BRANCH), BF16), ('a', (1, D_MODEL), F32)],
        nblk, sub)

    dbc4 = dbc.reshape(S5_G, S5_C, 2, S5_P)
    dcc4 = dcc.reshape(S5_G, S5_C, 2, S5_P)
    dlr, dli, dls, dbtr, dbti = _s5_params_bwd(
        lr3, li3, ls3, btr, bti, dar.reshape(S5_G, 1, S5_P), dai.reshape(S5_G, 1, S5_P), dbc4[:, :, 0], dbc4[:, :, 1])

    small_part = {
        "ln_gain": jnp.concatenate([dln0, dln1]), "mem_norm": jnp.concatenate([dgm0, dgm1]),
        "xq_norm": jnp.concatenate([dgq0, dgq1]), "xk_norm": jnp.concatenate([dgk0, dgk1]),
        "s5_lambda_re": dlr, "s5_lambda_im": dli, "s5_log_step": dls,
        "s5_b_re": jnp.swapaxes(dbtr, 1, 2), "s5_b_im": jnp.swapaxes(dbti, 1, 2),
        "s5_c_re": dcc4[:, :, 0], "s5_c_im": -dcc4[:, :, 1], "s5_d": dd,
        "mla_q_lora_norm": dgql, "mla_kv_lora_norm": dgkvl, "mla_q_nope_norm": dgqn, "mla_k_nope_norm": dgkn,
        "mla_q_rope_norm": dgqr[:, :ROPE], "mla_k_rope_norm": dgkr[:, :ROPE],
    }
    loss8 = jnp.pad(loss_part, ((0, 7), (0, 0)))
    dW_in_s5, (small_gath, loss_g) = _matmul_tn_slots(
        xn0, dproj0, "dw_s5_in", host=_plan_all_gather([_pack_small(small_part).astype(BF16), loss8]))

    late = [dW_in_s5]
    late_t = _pair_add(late, list(_exchange_call(_plan_pair(late), "rs_pair_late")), "rs_add_late")
    owners = [("w_out", 1), ("w_mem_kv", 1), ("mla_w_in", 0), ("mla_w_uq", 0), ("mla_w_ukv", 0), ("w_out", 0),
              ("w_mem_kv", 0), ("s5_w_glu", 0)]
    upd, late_recv = _updates_call(early_recv, [weights[n][i] for n, i in owners], [m_in[n][i] for n, i in owners],
                                   [v_in[n][i] for n, i in owners], "update_early", host=_plan_chips(late_t))
    owners.append(("s5_w_in", 0))
    upd.append(_sum_adamw(late_recv[0], s5_w_in[0], m_s5_w_in[0], v_s5_w_in[0], "update_s5_w_in"))
    grads, delta, new_m, new_v = {}, {}, {}, {}
    for n in _BIG:
        parts = [u for u, (o, _) in sorted(zip(upd, owners), key=lambda t: t[1][1]) if o == n]
        grads[n], delta[n], new_m[n], new_v[n] = (jnp.stack([p[j] for p in parts]) for j in range(4))

    gs, loss_sum = _small_sum(small_gath, loss_g, "small_sum")
    loss = loss_sum[0, 0]
    for n, _ in _SMALL:
        shape = weights[n].shape
        if n == "mla_q_lora_norm":
            grads[n] = lax.dynamic_slice(_unpack_small(gs, n, (Q_LORA,)), (me * 64,), (64,)).reshape(shape)
        elif n == "mla_kv_lora_norm":
            grads[n] = lax.dynamic_slice(_unpack_small(gs, n, (KV_LORA,)), (me * 32,), (32,)).reshape(shape)
        else:
            grads[n] = _unpack_small(gs, n, shape)

    def own(a):
        return a.reshape(a.shape[1:]) if a.ndim >= 3 else a

    wide = ("s5_b_re", "s5_b_im", "s5_c_re", "s5_c_im")
    for names, nb, call in (([n for n, _ in _SMALL if n not in wide], 1, "update_small"), (wide, 6, "update_s5_bc")):
        res = _adamw_multi([own(weights[n]) for n in names], [own(grads[n]) for n in names],
                           [own(m_in[n]) for n in names], [own(v_in[n]) for n in names], call, nb)
        for n, (dl, m2, v2) in zip(names, res):
            shape = weights[n].shape
            delta[n], new_m[n], new_v[n] = dl.reshape(shape), m2.reshape(shape), v2.reshape(shape)
    return (loss, dx0[None], *[grads[n] for n in _WEIGHTS], *[delta[n] for n in _WEIGHTS],
            *[new_m[n] for n in _WEIGHTS], *[new_v[n] for n in _WEIGHTS])
```

```python
import functools
import math

import numpy as np
import jax
import jax.numpy as jnp
from jax import lax
from jax.experimental import pallas as pl
from jax.experimental.pallas import tpu as pltpu

F32 = jnp.float32
BF16 = jnp.bfloat16
EPS = 1e-6
NEG = float(np.finfo(np.float32).min)
MESH = pl.DeviceIdType.MESH

N_DEV = 8
D_MODEL = 1024
MEM_LEN = 256
XQ = 512
PRIM = 1536
BRANCH = 2048
X_HEADS = 4
HD = 128
S5_G = 96
S5_P = 64
S5_C = 16
S5_GB = 8
S5_W = S5_GB * S5_P
MLA_H = 12
ROPE = 64
Q_LORA = 512
KV_LORA = 256
ROPE_THETA = 10000.0

ADAM_LR = 0.001
ADAM_B1 = 0.9
ADAM_B2 = 0.999
ADAM_EPS = 1e-08
ADAM_WD = 0.01
ADAM_STEP = 10

VMEM_LIMIT = 56 * 1024 * 1024


def _dot(a, b):
    return jnp.dot(a, b, preferred_element_type=F32)


def _dot_nt(a, b):
    return lax.dot_general(a, b, (((1,), (1,)), ((), ())), preferred_element_type=F32)


def _dot_tn(a, b):
    return lax.dot_general(a, b, (((0,), (0,)), ((), ())), preferred_element_type=F32)


@jax.custom_vjp
def _mm(a, b):
    return _dot(a.astype(BF16), b.astype(BF16))


def _mm_fwd(a, b):
    return _mm(a, b), (a, b)


def _mm_bwd(res, g):
    a, b = res
    gb = g.astype(BF16)
    return _dot_nt(gb, b.astype(BF16)).astype(a.dtype), _dot_tn(a.astype(BF16), gb).astype(b.dtype)


_mm.defvjp(_mm_fwd, _mm_bwd)


@jax.custom_vjp
def _mm_nt(a, b):
    return _dot_nt(a.astype(BF16), b.astype(BF16))


def _mm_nt_fwd(a, b):
    return _mm_nt(a, b), (a, b)


def _mm_nt_bwd(res, g):
    a, b = res
    gb = g.astype(BF16)
    return _dot(gb, b.astype(BF16)).astype(a.dtype), _dot_tn(gb, a.astype(BF16)).astype(b.dtype)


_mm_nt.defvjp(_mm_nt_fwd, _mm_nt_bwd)


@jax.custom_vjp
def _softmax(s):
    m = jnp.max(s, axis=-1, keepdims=True)
    e = jnp.exp(s - m)
    return e / jnp.sum(e, axis=-1, keepdims=True)


def _softmax_fwd(s):
    p = _softmax(s)
    return p, p


def _softmax_bwd(p, g):
    return (p * (g - jnp.sum(p * g, axis=-1, keepdims=True)),)


_softmax.defvjp(_softmax_fwd, _softmax_bwd)


def _rms(x, g, n):
    ms = jnp.sum(x * x, axis=-1, keepdims=True) * (1.0 / n)
    return x * lax.rsqrt(ms + EPS) * g


def _sigmoid(x):
    return 1.0 / (1.0 + jnp.exp(-x))


def _silu(x):
    return x * _sigmoid(x)


def _gelu(x):
    c = math.sqrt(2.0 / math.pi)
    return 0.5 * x * (1.0 + jnp.tanh(c * (x + 0.044715 * (x * x * x))))


@jax.custom_vjp
def _rot(x, c, s1, s2):
    return x * c + pltpu.roll(x, 96, 1) * s1 + pltpu.roll(x, 32, 1) * s2


def _rot_fwd(x, c, s1, s2):
    return _rot(x, c, s1, s2), (c, s1, s2)


def _rot_bwd(res, g):
    c, s1, s2 = res
    dx = g * c + pltpu.roll(g * s1, 32, 1) + pltpu.roll(g * s2, 96, 1)
    return dx, jnp.zeros_like(c), jnp.zeros_like(s1), jnp.zeros_like(s2)


_rot.defvjp(_rot_fwd, _rot_bwd)


def _mem_attn(xq, k, v, gq):
    outs = []
    for h in range(X_HEADS):
        sl = slice(HD * h, HD * (h + 1))
        q = _rms(xq[:, sl], gq, HD)
        p = _softmax(_mm_nt(q, k[:, sl]) * (HD ** -0.5))
        outs.append(_mm(p, v[:, sl]))
    return jnp.concatenate(outs, axis=-1)


def _merge(mix, xq, gate, k, v, gq):
    return jnp.concatenate([mix, _mem_attn(xq, k, v, gq)], axis=-1) * _silu(gate)


def _q_post(q, gqn, gqr, c, s1, s2):
    pieces = []
    for h in range(MLA_H):
        pieces.append(_rms(q[:, HD * h:HD * (h + 1)], gqn, HD))
        pieces.append(_rot(_rms(q[:, PRIM + HD * h:PRIM + HD * (h + 1)], gqr, ROPE), c, s1, s2))
    return jnp.concatenate(pieces, axis=-1)


def _kv_post(kv, krp, gkn, gkr, c, s1, s2):
    kr = _rot(_rms(krp, gkr, ROPE), c, s1, s2)
    pieces, vals = [], []
    for h in range(MLA_H):
        pieces.append(_rms(kv[:, 2 * HD * h:2 * HD * h + HD], gkn, HD))
        pieces.append(kr)
        vals.append(kv[:, 2 * HD * h + HD:2 * HD * (h + 1)])
    return jnp.concatenate(pieces, axis=-1), jnp.concatenate(vals, axis=-1)


def _rowwise(name, fn, ins, outs, nblk, sub=1, host=None):
    n_in = len(ins)

    def spec(kind, shape):
        if kind == 'r':
            return pl.BlockSpec((shape[0] // nblk, shape[1]), lambda i: (i, 0))
        if kind == 't':
            return pl.BlockSpec((shape[0], shape[1] // nblk), lambda i: (0, i))
        zeros = (0,) * len(shape)
        return pl.BlockSpec(tuple(shape), lambda i: zeros)

    def body(*refs):
        i = pl.program_id(0)
        res = fn(*[r[...] for r in refs[:n_in]])
        for (kind, _, _), ref, val in zip(outs, refs[n_in:], res):
            if kind == 'a':
                @pl.when(i == 0)
                def _():
                    ref[...] = jnp.zeros_like(ref)
                ref[...] += val.astype(ref.dtype)
            elif kind == 't':
                ref[...] = val.astype(F32).T.astype(ref.dtype)
            else:
                ref[...] = val.astype(ref.dtype)

    res, hosted = _hosting_call(
        body, name, nblk, host, [a for _, a in ins], [spec(k, a.shape) for k, a in ins],
        [jax.ShapeDtypeStruct(tuple(s), d) for _, s, d in outs], [spec(k, s) for k, s, _ in outs], [])
    return res if host is None else (res, hosted)


def _matmul_tn(at, g, name, out_dtype=BF16):
    K, L = at.shape
    N = g.shape[1]
    tn = next(t for t in (512, 384, 256, 128) if N % t == 0)

    def body(a_ref, g_ref, o_ref):
        o_ref[...] = _dot(a_ref[...], g_ref[...]).astype(o_ref.dtype)

    return pl.pallas_call(
        body, name=name, grid=(N // tn,),
        in_specs=[pl.BlockSpec((K, L), lambda n: (0, 0)), pl.BlockSpec((L, tn), lambda n: (0, n))],
        out_specs=pl.BlockSpec((K, tn), lambda n: (0, n)),
        out_shape=jax.ShapeDtypeStruct((K, N), out_dtype),
        compiler_params=pltpu.CompilerParams(dimension_semantics=("arbitrary",), vmem_limit_bytes=VMEM_LIMIT),
    )(at, g)


def _matmul_tn_slots(at, g, name, host=None):
    K, L = at.shape
    n = g.shape[1] // N_DEV

    def body(a_ref, g_ref, o_ref):
        o_ref[...] = _dot(a_ref[...], g_ref[...]).astype(o_ref.dtype)

    res, hosted = _hosting_call(
        body, name, N_DEV, host, [at, g],
        [pl.BlockSpec((K, L), lambda d: (0, 0)), pl.BlockSpec((L, n), lambda d: (0, d))],
        [jax.ShapeDtypeStruct((N_DEV, K, n), BF16)], [pl.BlockSpec((None, K, n), lambda d: (d, 0, 0))], [])
    return res[0] if host is None else (res[0], hosted)


def _mm_slots(a16, w):
    return jnp.concatenate([_dot(a16, w[d]) for d in range(N_DEV)], axis=-1)


def _mm_slots_nt(g16, w):
    n = w.shape[2]
    out = _dot_nt(g16[:, 0:n], w[0])
    for d in range(1, N_DEV):
        out = out + _dot_nt(g16[:, d * n:(d + 1) * n], w[d])
    return out


class _Exchange:
    def __init__(self, ins, outs, scratch, start, finish):
        self.ins, self.outs, self.scratch, self.start, self.finish = ins, outs, scratch, start, finish


def _xyc():
    return lax.axis_index("x"), lax.axis_index("y"), lax.axis_index("c")


def _plan_all_gather(xs):
    n = len(xs)

    def build(x_refs, out_refs, sems):
        send_sems, recv_sems, local_sems = sems
        x, y, c = _xyc()

        def copies(k, block, to, own=False):
            slot = 4 * block[0] + 2 * block[1] + block[2]
            return [pltpu.make_async_remote_copy(
                src_ref=x_refs[a] if own else out_refs[a].at[slot], dst_ref=out_refs[a].at[slot],
                send_sem=send_sems.at[k * n + a], recv_sem=recv_sems.at[k * n + a], device_id=to,
                device_id_type=MESH) for a in range(n)]

        mine = [pltpu.make_async_copy(x_refs[a], out_refs[a].at[4 * x + 2 * y + c], local_sems.at[a])
                for a in range(n)]
        return copies, mine, (x, y, c), [(1 - x, y), (x, 1 - y), (1 - x, 1 - y)]

    def first_copies(copies, me, chips):
        x, y, c = me
        first = copies(0, me, (x, y, 1 - c), own=True)
        for j, chip in enumerate(chips):
            first += copies(1 + j, me, (*chip, c), own=True)
        return first

    def start(x_refs, out_refs, sems):
        copies, mine, me, chips = build(x_refs, out_refs, sems)
        for cp in mine + first_copies(copies, me, chips):
            cp.start()

    def finish(x_refs, out_refs, sems):
        copies, mine, me, chips = build(x_refs, out_refs, sems)
        x, y, c = me
        passed = []
        for j, chip in enumerate(chips):
            for cp in copies(1 + j, (*chip, c), me):
                cp.wait_recv()
            fwd = copies(4 + j, (*chip, c), (x, y, 1 - c))
            for cp in fwd:
                cp.start()
            passed += fwd
        for cp in copies(0, (x, y, 1 - c), me):
            cp.wait_recv()
        for j, chip in enumerate(chips):
            for cp in copies(4 + j, (*chip, 1 - c), me):
                cp.wait_recv()
        for cp in first_copies(copies, me, chips) + passed:
            cp.wait_send()
        for cp in mine:
            cp.wait()

    return _Exchange(list(xs), [jax.ShapeDtypeStruct((N_DEV,) + a.shape, a.dtype) for a in xs],
                     [pltpu.SemaphoreType.DMA((7 * n,)), pltpu.SemaphoreType.DMA((7 * n,)),
                      pltpu.SemaphoreType.DMA((n,))], start, finish)


_CHIPS = ((0, 0), (0, 1), (1, 0), (1, 1))


def _plan_pair(sends):
    n = len(sends)

    def build(s_refs, o_refs, sems):
        send_sems, recv_sems = sems
        x, y, c = _xyc()
        return [pltpu.make_async_remote_copy(
            src_ref=s_refs[a].at[4 * px + 2 * py + 1 - c], dst_ref=o_refs[a].at[j],
            send_sem=send_sems.at[j * n + a], recv_sem=recv_sems.at[j * n + a], device_id=(x, y, 1 - c),
            device_id_type=MESH) for j, (px, py) in enumerate(_CHIPS) for a in range(n)]

    def start(s_refs, o_refs, sems):
        for cp in build(s_refs, o_refs, sems):
            cp.start()

    def finish(s_refs, o_refs, sems):
        for cp in build(s_refs, o_refs, sems):
            cp.wait_recv()
            cp.wait_send()

    return _Exchange(list(sends), [jax.ShapeDtypeStruct((4,) + a.shape[1:], a.dtype) for a in sends],
                     [pltpu.SemaphoreType.DMA((4 * n,)), pltpu.SemaphoreType.DMA((4 * n,))], start, finish)


def _plan_chips(ts):
    n = len(ts)
    flips = ((1, 0), (0, 1), (1, 1))

    def build(t_refs, o_refs, sems):
        send_sems, recv_sems, local_sems = sems
        x, y, c = _xyc()
        mine = 2 * x + y
        local = [pltpu.make_async_copy(t_refs[a].at[mine], o_refs[a].at[mine], local_sems.at[a]) for a in range(n)]
        remote = []
        for k, (fx, fy) in enumerate(flips):
            px = 1 - x if fx else x
            py = 1 - y if fy else y
            remote += [pltpu.make_async_remote_copy(
                src_ref=t_refs[a].at[2 * px + py], dst_ref=o_refs[a].at[mine],
                send_sem=send_sems.at[k * n + a], recv_sem=recv_sems.at[k * n + a], device_id=(px, py, c),
                device_id_type=MESH) for a in range(n)]
        return local, remote

    def start(t_refs, o_refs, sems):
        local, remote = build(t_refs, o_refs, sems)
        for cp in local + remote:
            cp.start()

    def finish(t_refs, o_refs, sems):
        local, remote = build(t_refs, o_refs, sems)
        for cp in remote:
            cp.wait_recv()
        for cp in remote:
            cp.wait_send()
        for cp in local:
            cp.wait()

    return _Exchange(list(ts), [jax.ShapeDtypeStruct(a.shape, a.dtype) for a in ts],
                     [pltpu.SemaphoreType.DMA((3 * n,)), pltpu.SemaphoreType.DMA((3 * n,)),
                      pltpu.SemaphoreType.DMA((n,))], start, finish)


def _exchange_call(plan, name):
    n = len(plan.ins)

    def body(*refs):
        ins, outs, sems = refs[:n], refs[n:2 * n], refs[2 * n:]
        plan.start(ins, outs, sems)
        plan.finish(ins, outs, sems)

    return pl.pallas_call(
        body, name=name, out_shape=plan.outs,
        in_specs=[pl.BlockSpec(memory_space=pl.ANY)] * n, out_specs=[pl.BlockSpec(memory_space=pl.ANY)] * n,
        scratch_shapes=plan.scratch,
    )(*plan.ins)


def _pair_add(sends, fromsib, name):
    n = len(sends)
    nb = 8

    def body(*refs):
        c = lax.axis_index("c")
        for a in range(n):
            s_ref, f_ref, t_ref = refs[a], refs[n + a], refs[2 * n + a]
            for j in range(4):
                t_ref[j] = (s_ref[2 * j + c].astype(F32) + f_ref[j].astype(F32)).astype(t_ref.dtype)

    def spec(a, lead):
        return pl.BlockSpec((lead, a.shape[1] // nb, a.shape[2]), lambda i: (0, i, 0))

    return pl.pallas_call(
        body, name=name, grid=(nb,),
        in_specs=[spec(a, N_DEV) for a in sends] + [spec(a, 4) for a in fromsib],
        out_specs=[spec(a, 4) for a in fromsib],
        out_shape=[jax.ShapeDtypeStruct(a.shape, a.dtype) for a in fromsib],
        compiler_params=pltpu.CompilerParams(dimension_semantics=("arbitrary",), vmem_limit_bytes=VMEM_LIMIT),
    )(*sends, *fromsib)


def _adamw_vals(w, g, m, v):
    m2 = ADAM_B1 * m + (1.0 - ADAM_B1) * g
    v2 = ADAM_B2 * v + (1.0 - ADAM_B2) * (g * g)
    m_hat = m2 / (1.0 - ADAM_B1 ** ADAM_STEP)
    v_hat = v2 / (1.0 - ADAM_B2 ** ADAM_STEP)
    delta = -ADAM_LR * (m_hat / (jnp.sqrt(v_hat) + ADAM_EPS) + ADAM_WD * w)
    return delta, m2, v2


def _sum_adamw(recv, w, m, v, name):
    R, C = w.shape
    ns = recv.shape[0]
    br = next((t for t in (256, 128, 64, 32, 16) if R % t == 0), R)

    def body(r_ref, w_ref, m_ref, v_ref, g_ref, d_ref, m2_ref, v2_ref):
        g = r_ref[0].astype(F32)
        for d in range(1, ns):
            g = g + r_ref[d].astype(F32)
        dl, m2, v2 = _adamw_vals(w_ref[...], g, m_ref[...], v_ref[...])
        g_ref[...] = g
        d_ref[...] = dl
        m2_ref[...] = m2
        v2_ref[...] = v2

    spec = pl.BlockSpec((br, C), lambda i: (i, 0))
    return pl.pallas_call(
        body, name=name, grid=(R // br,),
        in_specs=[pl.BlockSpec((ns, br, C), lambda i: (0, i, 0)), spec, spec, spec], out_specs=[spec] * 4,
        out_shape=[jax.ShapeDtypeStruct((R, C), F32)] * 4,
        compiler_params=pltpu.CompilerParams(dimension_semantics=("arbitrary",)),
    )(recv, w, m, v)


def _updates_call(recvs, ws, ms, vs, name, host=None):
    n = len(recvs)
    nb = 8

    def body(*refs):
        for a in range(n):
            r_ref, w_ref, m_ref, v_ref = refs[a], refs[n + a], refs[2 * n + a], refs[3 * n + a]
            g_ref, d_ref, m2_ref, v2_ref = refs[4 * n + 4 * a:4 * n + 4 * a + 4]
            g = r_ref[0].astype(F32)
            for d in range(1, r_ref.shape[0]):
                g = g + r_ref[d].astype(F32)
            dl, m2, v2 = _adamw_vals(w_ref[...], g, m_ref[...], v_ref[...])
            g_ref[...] = g
            d_ref[...] = dl
            m2_ref[...] = m2
            v2_ref[...] = v2

    def spec2(w):
        return pl.BlockSpec((w.shape[0] // nb, w.shape[1]), lambda i: (i, 0))

    def spec3(r):
        return pl.BlockSpec((r.shape[0], r.shape[1] // nb, r.shape[2]), lambda i: (0, i, 0))

    res, hosted = _hosting_call(
        body, name, nb, host, list(recvs) + list(ws) + list(ms) + list(vs),
        [spec3(r) for r in recvs] + [spec2(w) for w in ws] * 3,
        [jax.ShapeDtypeStruct(w.shape, F32) for w in ws for _ in range(4)],
        [spec2(w) for w in ws for _ in range(4)], [])
    return [res[4 * a:4 * a + 4] for a in range(n)], hosted


def _small_sum(gath, loss_g, name):
    _, R, C = gath.shape
    br = R // 3

    def body(g_ref, l_ref, go_ref, lo_ref):
        g = g_ref[0].astype(F32)
        lsum = l_ref[0]
        for d in range(1, N_DEV):
            g = g + g_ref[d].astype(F32)
            lsum = lsum + l_ref[d]
        go_ref[...] = g
        lo_ref[...] = lsum

    return pl.pallas_call(
        body, name=name, grid=(R // br,),
        in_specs=[pl.BlockSpec((N_DEV, br, C), lambda i: (0, i, 0)),
                  pl.BlockSpec((N_DEV, 8, HD), lambda i: (0, 0, 0))],
        out_specs=[pl.BlockSpec((br, C), lambda i: (i, 0)), pl.BlockSpec((8, HD), lambda i: (0, 0))],
        out_shape=[jax.ShapeDtypeStruct((R, C), F32), jax.ShapeDtypeStruct((8, HD), F32)],
        compiler_params=pltpu.CompilerParams(dimension_semantics=("arbitrary",)),
    )(gath, loss_g)


def _adamw_multi(ws, gs, ms, vs, name, nblk=1):
    n = len(ws)

    def body(*refs):
        for a in range(n):
            dl, m2, v2 = _adamw_vals(refs[a][...], refs[n + a][...], refs[2 * n + a][...], refs[3 * n + a][...])
            refs[4 * n + 3 * a][...] = dl
            refs[4 * n + 3 * a + 1][...] = m2
            refs[4 * n + 3 * a + 2][...] = v2

    def spec(x):
        rest = (0,) * (x.ndim - 1)
        return pl.BlockSpec((x.shape[0] // nblk,) + tuple(x.shape[1:]), lambda i: (i,) + rest)

    res = pl.pallas_call(
        body, name=name, grid=(nblk,),
        in_specs=[spec(w) for w in ws] * 4, out_specs=[spec(w) for w in ws for _ in range(3)],
        out_shape=[jax.ShapeDtypeStruct(w.shape, F32) for w in ws for _ in range(3)],
        compiler_params=pltpu.CompilerParams(dimension_semantics=("arbitrary",), vmem_limit_bytes=VMEM_LIMIT),
    )(*ws, *gs, *ms, *vs)
    return [res[3 * a:3 * a + 3] for a in range(n)]


def _s5_param_fn(lr, li, ls, btr, bti):
    step = jnp.exp(ls)
    er = jnp.exp(lr * step)
    ang = li * step
    ar = er * jnp.cos(ang)
    ai = er * jnp.sin(ang)
    nr = ar - 1.0
    den = lr * lr + li * li
    fr = (nr * lr + ai * li) / den
    fi = (ai * lr - nr * li) / den
    return ar, ai, fr * btr - fi * bti, fr * bti + fi * btr


def _s5_params(lr, li, ls, btr, bti):
    def body(lr_ref, li_ref, ls_ref, br_ref, bi_ref, ar_ref, ai_ref, bbr_ref, bbi_ref):
        ar, ai, bbr, bbi = _s5_param_fn(lr_ref[...], li_ref[...], ls_ref[...], br_ref[...], bi_ref[...])
        ar_ref[...] = ar
        ai_ref[...] = ai
        bbr_ref[...] = bbr
        bbi_ref[...] = bbi

    sd = jax.ShapeDtypeStruct
    return pl.pallas_call(
        body, name="s5_params",
        out_shape=[sd(lr.shape, F32), sd(lr.shape, F32), sd(btr.shape, F32), sd(btr.shape, F32)],
    )(lr, li, ls, btr, bti)


def _s5_params_bwd(lr, li, ls, btr, bti, dar, dai, dbbr, dbbi):
    def body(lr_ref, li_ref, ls_ref, br_ref, bi_ref, dar_ref, dai_ref, dbbr_ref, dbbi_ref,
             dlr_ref, dli_ref, dls_ref, dbr_ref, dbi_ref):
        _, vjp = jax.vjp(_s5_param_fn, lr_ref[...], li_ref[...], ls_ref[...], br_ref[...], bi_ref[...])
        dlr, dli, dls, dbr, dbi = vjp((dar_ref[...], dai_ref[...], dbbr_ref[...], dbbi_ref[...]))
        dlr_ref[...] = dlr
        dli_ref[...] = dli
        dls_ref[...] = dls
        dbr_ref[...] = dbr
        dbi_ref[...] = dbi

    sd = jax.ShapeDtypeStruct
    return pl.pallas_call(
        body, name="s5_params_bwd",
        out_shape=[sd(lr.shape, F32), sd(lr.shape, F32), sd(ls.shape, F32), sd(btr.shape, F32), sd(btr.shape, F32)],
    )(lr, li, ls, btr, bti, dar, dai, dbbr, dbbi)


def _cpow(ar, ai, n):
    assert n & (n - 1) == 0
    while n > 1:
        ar, ai = ar * ar - ai * ai, 2.0 * ar * ai
        n //= 2
    return ar, ai


def _scan(st, cr, ci, init, nk, reverse, store, prev=None):
    W = S5_W

    def step(j, carry):
        k = nk - 1 - j if reverse else j
        rows = pl.ds(pl.multiple_of(k * 8, 8), 8)
        sr, si = carry[0], carry[1]
        nsr = cr * sr - ci * si + st[rows, 0:W]
        nsi = cr * si + ci * sr + st[rows, W:2 * W]
        if store:
            st[rows, 0:W] = nsr
            st[rows, W:2 * W] = nsi
        if prev is None:
            return nsr, nsi
        prows = pl.ds(pl.multiple_of(jnp.maximum(k - 1, 0) * 8, 8), 8)
        w = jnp.where(k > 0, 1.0, 0.0).astype(F32)
        pr = prev[prows, 0:W] * w
        pi = prev[prows, W:2 * W] * w
        return nsr, nsi, carry[2] + nsr * pr + nsi * pi, carry[3] + nsi * pr - nsr * pi

    return lax.fori_loop(0, nk, step, init, unroll=2)


def _chain(fin, fr, fi, pr, pi, reverse):
    W = S5_W
    fin[:, 0:W] = fr
    fin[:, W:2 * W] = fi
    rowid = lax.broadcasted_iota(jnp.int32, (8, W), 0)
    cr = jnp.zeros((1, W), F32)
    ci = jnp.zeros((1, W), F32)
    init_r = jnp.zeros((8, W), F32)
    init_i = jnp.zeros((8, W), F32)
    for s in (range(7, -1, -1) if reverse else range(8)):
        init_r = jnp.where(rowid == s, cr, init_r)
        init_i = jnp.where(rowid == s, ci, init_i)
        lr = fin[s:s + 1, 0:W]
        li = fin[s:s + 1, W:2 * W]
        cr, ci = lr + pr * cr - pi * ci, li + pr * ci + pi * cr
    return init_r, init_i


def _full_scan(st, fin, ar, ai, nk, reverse, prev=None):
    W = S5_W
    cr = jnp.broadcast_to(ar, (8, W))
    ci = jnp.broadcast_to(-ai if reverse else ai, (8, W))
    z = jnp.zeros((8, W), F32)
    fr, fi = _scan(st, cr, ci, (z, z), nk, reverse, store=False)
    pr, pi = _cpow(ar, -ai if reverse else ai, nk)
    init = _chain(fin, fr, fi, pr, pi, reverse)
    if prev is None:
        return _scan(st, cr, ci, init, nk, reverse, store=True)
    return _scan(st, cr, ci, init + (z, z), nk, reverse, store=True, prev=prev)


def _s5_specs(L):
    W2 = 2 * S5_W
    GC = S5_GB * S5_C
    col = pl.BlockSpec((L, GC), lambda g: (0, g))
    vec = pl.BlockSpec((1, GC), lambda g: (0, g))
    avec = pl.BlockSpec((1, S5_W), lambda g: (0, g))
    bmat = pl.BlockSpec((None, GC, W2), lambda g: (g, 0, 0))
    cmat = pl.BlockSpec((None, W2, GC), lambda g: (g, 0, 0))
    return col, vec, avec, bmat, cmat


def _interleave(dst, src, nk):
    for s in range(8):
        dst[pl.ds(s, nk, stride=8), :] = src[s * nk:(s + 1) * nk, :]


def _deinterleave(dst, src, nk):
    for s in range(8):
        dst[s * nk:(s + 1) * nk, :] = src[pl.ds(s, nk, stride=8), :].astype(dst.dtype)


def _hosting_call(body, name, nsteps, host, ins, in_specs, outs, out_specs, scratch):
    grid = (nsteps,) if isinstance(nsteps, int) else tuple(nsteps)
    params = pltpu.CompilerParams(dimension_semantics=("arbitrary",) * len(grid), vmem_limit_bytes=VMEM_LIMIT)
    if host is None:
        res = pl.pallas_call(
            body, name=name, grid=grid, in_specs=in_specs, out_specs=out_specs, out_shape=outs,
            scratch_shapes=scratch, compiler_params=params,
        )(*ins)
        return list(res), []
    n_in, n_out, n_sc = len(ins), len(outs), len(scratch)
    h_in, h_out = len(host.ins), len(host.outs)

    def hosted(*refs):
        a = refs[:n_in]
        ha = refs[n_in:n_in + h_in]
        o = refs[n_in + h_in:n_in + h_in + n_out]
        ho = refs[n_in + h_in + n_out:n_in + h_in + n_out + h_out]
        sc = refs[n_in + h_in + n_out + h_out:n_in + h_in + n_out + h_out + n_sc]
        hs = refs[n_in + h_in + n_out + h_out + n_sc:]
        first = functools.reduce(jnp.logical_and, [pl.program_id(i) == 0 for i in range(len(grid))])
        last = functools.reduce(jnp.logical_and, [pl.program_id(i) == g - 1 for i, g in enumerate(grid)])

        @pl.when(first)
        def _():
            host.start(ha, ho, hs)

        body(*a, *o, *sc)

        @pl.when(last)
        def _():
            host.finish(ha, ho, hs)

    hbm = pl.BlockSpec(memory_space=pl.ANY)
    res = pl.pallas_call(
        hosted, name=name, grid=grid,
        in_specs=list(in_specs) + [hbm] * h_in, out_specs=list(out_specs) + [hbm] * h_out,
        out_shape=list(outs) + list(host.outs), scratch_shapes=list(scratch) + list(host.scratch),
        compiler_params=params,
    )(*ins, *host.ins)
    return list(res[:n_out]), list(res[n_out:])


def _s5_fwd(u, bm, cm, ar, ai, dvec, host=None):
    L = u.shape[0]
    nk = L // 8
    GC = S5_GB * S5_C
    col, vec, avec, bmat, cmat = _s5_specs(L)

    def body(u_ref, b_ref, c_ref, ar_ref, ai_ref, d_ref, y_ref, st, fin, ui, yi):
        _interleave(ui, u_ref, nk)
        for r in range(8):
            rows = slice(r * nk, (r + 1) * nk)
            st[rows, :] = _dot(ui[rows, :].astype(BF16), b_ref[...])
        _full_scan(st, fin, ar_ref[...], ai_ref[...], nk, reverse=False)
        for r in range(8):
            rows = slice(r * nk, (r + 1) * nk)
            yi[rows, :] = _dot(st[rows, :].astype(BF16), c_ref[...]) + d_ref[...] * ui[rows, :]
        _deinterleave(y_ref, yi, nk)

    return _hosting_call(
        body, "s5_fwd", S5_G // S5_GB, host,
        [u, bm, cm, ar, ai, dvec], [col, bmat, cmat, avec, avec, vec],
        [jax.ShapeDtypeStruct(u.shape, F32)], [col],
        [pltpu.VMEM((L, 2 * S5_W), F32), pltpu.VMEM((8, 2 * S5_W), F32), pltpu.VMEM((L, GC), F32),
         pltpu.VMEM((L, GC), F32)])


def _s5_bwd(u, dy, bm, bmt, cmt, ar, ai, dvec, mask, rmat, host=None):
    L = u.shape[0]
    nk = L // 8
    W = S5_W
    GC = S5_GB * S5_C
    col, vec, avec, bmat, cmat = _s5_specs(L)
    hi = lax.Precision.HIGHEST

    def body(u_ref, dy_ref, b_ref, bt_ref, ct_ref, ar_ref, ai_ref, d_ref, mask_ref, r_ref,
             du_ref, db_ref, dc_ref, dd_ref, dar_ref, dai_ref, sa, sb, fin, ui, dyi, dui):
        ar = ar_ref[...]
        ai = ai_ref[...]
        _interleave(ui, u_ref, nk)
        _interleave(dyi, dy_ref, nk)
        for r in range(8):
            rows = slice(r * nk, (r + 1) * nk)
            sa[rows, :] = _dot(ui[rows, :].astype(BF16), b_ref[...])
            sb[rows, :] = _dot(dyi[rows, :].astype(BF16), ct_ref[...])
        _full_scan(sa, fin, ar, ai, nk, reverse=False)
        gr, gi, accr, acci = _full_scan(sb, fin, ar, ai, nk, reverse=True, prev=sa)
        rowid = lax.broadcasted_iota(jnp.int32, (8, W), 0)
        last = pl.ds((nk - 1) * 8, 8)
        pr = jnp.where(rowid == 0, 0.0, pltpu.roll(sa[last, 0:W], 1, 0))
        pi = jnp.where(rowid == 0, 0.0, pltpu.roll(sa[last, W:2 * W], 1, 0))
        accr = accr + gr * pr + gi * pi
        acci = acci + gi * pr - gr * pi
        dar_ref[...] = jnp.sum(accr, axis=0, keepdims=True)
        dai_ref[...] = jnp.sum(acci, axis=0, keepdims=True)
        dbf = jnp.zeros((GC, 2 * W), F32)
        dcf = jnp.zeros((GC, 2 * W), F32)
        dd = jnp.zeros((1, GC), F32)
        for r in range(8):
            rows = slice(r * nk, (r + 1) * nk)
            ub = ui[rows, :]
            dyb = dyi[rows, :]
            gb = sb[rows, :].astype(BF16)
            dui[rows, :] = _dot(gb, bt_ref[...]) + d_ref[...] * dyb
            dbf = dbf + _dot_tn(ub.astype(BF16), gb)
            dcf = dcf + _dot_tn(dyb.astype(BF16), sa[rows, :].astype(BF16))
            dd = dd + jnp.sum(dyb * ub, axis=0, keepdims=True)
        db_ref[...] = jnp.dot(dbf * mask_ref[...], r_ref[...], precision=hi, preferred_element_type=F32)
        dc_ref[...] = jnp.dot(dcf * mask_ref[...], r_ref[...], precision=hi, preferred_element_type=F32)
        dd_ref[...] = dd
        _deinterleave(du_ref, dui, nk)

    cmp_spec = pl.BlockSpec((GC, 2 * S5_P), lambda g: (g, 0))
    whole = lambda shape: pl.BlockSpec(shape, lambda g: (0, 0))
    sd = jax.ShapeDtypeStruct
    return _hosting_call(
        body, "s5_bwd", S5_G // S5_GB, host,
        [u, dy, bm, bmt, cmt, ar, ai, dvec, mask, rmat],
        [col, col, bmat, cmat, bmat, avec, avec, vec, whole(mask.shape), whole(rmat.shape)],
        [sd(u.shape, BF16), sd((S5_G * S5_C, 2 * S5_P), F32), sd((S5_G * S5_C, 2 * S5_P), F32),
         sd((1, PRIM), F32), sd((1, S5_G * S5_P), F32), sd((1, S5_G * S5_P), F32)],
        [col, cmp_spec, cmp_spec, vec, avec, avec],
        [pltpu.VMEM((L, 2 * W), F32), pltpu.VMEM((L, 2 * W), F32), pltpu.VMEM((8, 2 * W), F32),
         pltpu.VMEM((L, GC), F32), pltpu.VMEM((L, GC), F32), pltpu.VMEM((L, GC), F32)])


def _s5_mats(bbr, bbi, cre, cim):
    nb = S5_G // S5_GB
    eye = jnp.eye(S5_GB, dtype=F32)
    bb = jnp.stack([bbr, bbi], axis=2).reshape(nb, S5_GB, S5_C, 2, S5_P)
    bm = jnp.einsum('ngcrp,gh->ngcrhp', bb, eye).reshape(nb, S5_GB * S5_C, 2 * S5_W)
    cc = jnp.stack([cre, -cim], axis=2).reshape(nb, S5_GB, S5_C, 2, S5_P)
    cmt = jnp.einsum('ngcrp,gh->ngcrhp', cc, eye).reshape(nb, S5_GB * S5_C, 2 * S5_W)
    return (bm.astype(BF16), jnp.swapaxes(bm, 1, 2).astype(BF16),
            jnp.swapaxes(cmt, 1, 2).astype(BF16), cmt.astype(BF16))


def _s5_compact_consts():
    g_row = np.arange(S5_GB * S5_C) // S5_C
    col = np.arange(2 * S5_W)
    g_col = (col % S5_W) // S5_P
    mask = (g_row[:, None] == g_col[None, :]).astype(np.float32)
    tgt = (col // S5_W) * S5_P + col % S5_P
    rmat = (tgt[:, None] == np.arange(2 * S5_P)[None, :]).astype(np.float32)
    return jnp.asarray(mask), jnp.asarray(rmat)


def _attn_scores(q_ref, k_ref, qb, bq, scale):
    ext = (qb + 1) * bq
    s = _dot_nt(q_ref[qb * bq:ext, :], k_ref[0:ext, :]) * scale
    qpos = lax.broadcasted_iota(jnp.int32, (bq, bq), 0)
    kpos = lax.broadcasted_iota(jnp.int32, (bq, bq), 1)
    diag = jnp.where(kpos <= qpos, s[:, ext - bq:], NEG)
    return diag if qb == 0 else jnp.concatenate([s[:, :ext - bq], diag], axis=-1)


def _attn_fwd(qp, kp, v, scale):
    L = qp.shape[0]
    bq = min(256, L)

    def body(q_ref, k_ref, v_ref, o_ref, lse_ref):
        for qb in range(L // bq):
            rows = slice(qb * bq, (qb + 1) * bq)
            s = _attn_scores(q_ref, k_ref, qb, bq, scale)
            m = jnp.max(s, axis=-1, keepdims=True)
            e = jnp.exp(s - m)
            l = jnp.sum(e, axis=-1, keepdims=True)
            o_ref[rows, :] = _dot(e.astype(BF16), v_ref[0:(qb + 1) * bq, :]) / l
            lse_ref[rows, :] = jnp.broadcast_to(m + jnp.log(l), (bq, HD))

    blk = pl.BlockSpec((L, HD), lambda h: (0, h))
    wide = pl.BlockSpec((L, 2 * HD), lambda h: (0, h))
    return pl.pallas_call(
        body, name="mla_attn_fwd", grid=(MLA_H,),
        in_specs=[wide, wide, blk], out_specs=[blk, blk],
        out_shape=[jax.ShapeDtypeStruct((L, MLA_H * HD), F32)] * 2,
        compiler_params=pltpu.CompilerParams(dimension_semantics=("arbitrary",), vmem_limit_bytes=VMEM_LIMIT),
    )(qp, kp, v)


def _attn_bwd(qp, kp, v, o, lse, do, scale):
    L = qp.shape[0]
    bq = min(256, L)
    nq = L // bq

    def body(q_ref, k_ref, v_ref, o_ref, lse_ref, do_ref, dq_ref, dk_ref, dv_ref, dk_acc, dv_acc):
        dk_acc[...] = jnp.zeros_like(dk_acc)
        dv_acc[...] = jnp.zeros_like(dv_acc)
        for qb in range(nq):
            rows = slice(qb * bq, (qb + 1) * bq)
            ext = (qb + 1) * bq
            do = do_ref[rows, :]
            dob = do.astype(BF16)
            p = jnp.exp(_attn_scores(q_ref, k_ref, qb, bq, scale) - lse_ref[rows, 0:1])
            dp = _dot_nt(dob, v_ref[0:ext, :])
            dsum = jnp.sum(do * o_ref[rows, :], axis=-1, keepdims=True)
            ds = (p * (dp - dsum) * scale).astype(BF16)
            dq_ref[rows, :] = _dot(ds, k_ref[0:ext, :]).astype(dq_ref.dtype)
            dk_acc[0:ext, :] += _dot_tn(ds, q_ref[rows, :])
            dv_acc[0:ext, :] += _dot_tn(p.astype(BF16), dob)
        dk_ref[...] = dk_acc[...].astype(dk_ref.dtype)
        dv_ref[...] = dv_acc[...].astype(dv_ref.dtype)

    sd = jax.ShapeDtypeStruct
    blk = pl.BlockSpec((L, HD), lambda h: (0, h))
    wide = pl.BlockSpec((L, 2 * HD), lambda h: (0, h))
    return pl.pallas_call(
        body, name="mla_attn_bwd", grid=(MLA_H,),
        in_specs=[wide, wide, blk, blk, blk, blk], out_specs=[wide, wide, blk],
        out_shape=[sd((L, MLA_H * 2 * HD), BF16), sd((L, MLA_H * 2 * HD), BF16), sd((L, MLA_H * HD), BF16)],
        scratch_shapes=[pltpu.VMEM((L, 2 * HD), F32), pltpu.VMEM((L, HD), F32)],
        compiler_params=pltpu.CompilerParams(dimension_semantics=("arbitrary",), vmem_limit_bytes=VMEM_LIMIT),
    )(qp, kp, v, o, lse, do)


def _kv_fn(mem, gm, w, gk):
    kv = _mm(_rms(mem, gm, D_MODEL), w)
    k = jnp.concatenate([_rms(kv[:, HD * h:HD * (h + 1)], gk, HD) for h in range(X_HEADS)], axis=-1)
    return k, kv[:, XQ:]


def _kv_prep(mem, gm, w, gk, name):
    def fn(mem, gm, w, gk):
        return _kv_fn(mem, gm, w, gk)
    M = mem.shape[0]
    return _rowwise(name, fn, [('c', mem), ('c', gm), ('c', w), ('c', gk)],
                    [('c', (M, XQ), F32), ('c', (M, XQ), F32)], 1)


def _kv_prep_bwd(mem, gm, w, gk, dk, dv, name):
    def fn(mem, gm, w, gk, dk, dv):
        _, vjp = jax.vjp(lambda a, b, c: _kv_fn(mem, a, b, c), gm, w, gk)
        return vjp((dk, dv))
    return _rowwise(name, fn, [('c', mem), ('c', gm), ('c', w), ('c', gk), ('c', dk), ('c', dv)],
                    [('c', gm.shape, F32), ('c', w.shape, BF16), ('c', gk.shape, F32)], 1)


def _forward_merge(x, mix, mix_kind, xq, gate, k, v, gq, wout, name, nblk, sub, host=None):
    def fn(x, mix, xq, gate, k, v, gq, wout):
        o = _merge(mix, xq, gate, k, v, gq)
        return (x + _dot(o.astype(BF16), wout),)
    L = x.shape[0]
    out = _rowwise(name, fn, [('r', x), (mix_kind, mix), ('r', xq), ('r', gate), ('c', k), ('c', v), ('c', gq),
                              ('c', wout)], [('r', (L, D_MODEL), F32)], nblk, sub, host=host)
    return out[0] if host is None else (out[0][0], out[1])


def _backward_merge(dx, mix, mix_kind, xq, gate, k, v, gq, wout, name, nblk, sub):
    def fn(dx, mix, xq, gate, k, v, gq, wout):
        g16 = dx.astype(BF16)
        do = _dot_nt(g16, wout)
        o, vjp = jax.vjp(_merge, mix, xq, gate, k, v, gq)
        dmix, dxq, dgate, dk, dv, dgq = vjp(do)
        return dmix, dxq, dgate, o, g16, dk, dv, dgq
    L = dx.shape[0]
    return _rowwise(
        name, fn,
        [('r', dx), (mix_kind, mix), ('r', xq), ('r', gate), ('c', k), ('c', v), ('c', gq), ('c', wout)],
        [('r', (L, PRIM), F32), ('r', (L, XQ), BF16), ('r', (L, BRANCH), BF16), ('t', (BRANCH, L), BF16),
         ('r', (L, D_MODEL), BF16), ('a', k.shape, F32), ('a', v.shape, F32), ('a', gq.shape, F32)], nblk, sub)


_MLA_IN = 3392
_MLA_IN_PAD = 3456


def _from_slots(g):
    _, k, n = g.shape
    return jnp.transpose(g, (1, 0, 2)).reshape(k, N_DEV * n)


def _to_slots(w):
    k = w.shape[0]
    return jnp.transpose(w.reshape(k, N_DEV, -1), (1, 0, 2))


def _uq_to_kernel(g):
    uq = _from_slots(g).reshape(Q_LORA, MLA_H, HD + ROPE)
    return jnp.concatenate([uq[:, :, :HD].reshape(Q_LORA, PRIM),
                            jnp.pad(uq[:, :, HD:], ((0, 0), (0, 0), (0, HD - ROPE))).reshape(Q_LORA, PRIM)], axis=1)


def _uq_from_kernel(d_w_q):
    uq = jnp.concatenate([d_w_q[:, :PRIM].reshape(Q_LORA, MLA_H, HD),
                          d_w_q[:, PRIM:].reshape(Q_LORA, MLA_H, HD)[:, :, :ROPE]], axis=2)
    return _to_slots(uq.reshape(Q_LORA, MLA_H * (HD + ROPE)))


def _mla_in_perm(w):
    return jnp.concatenate([w[:, :768], w[:, 832:], w[:, 768:832], jnp.zeros((w.shape[0], 64), w.dtype)], axis=1)


def _mla_in_unperm(w):
    return jnp.concatenate([w[:, :768], w[:, 3328:3392], w[:, 768:3328]], axis=1)


_SMALL = (("ln_gain", 2048), ("mem_norm", 2048), ("xq_norm", 256), ("xk_norm", 256), ("s5_lambda_re", 6144),
          ("s5_lambda_im", 6144), ("s5_log_step", 96), ("s5_b_re", 98304), ("s5_b_im", 98304), ("s5_c_re", 98304),
          ("s5_c_im", 98304), ("s5_d", 1536), ("mla_q_lora_norm", 512), ("mla_kv_lora_norm", 256),
          ("mla_q_nope_norm", 128), ("mla_k_nope_norm", 128), ("mla_q_rope_norm", 64), ("mla_k_rope_norm", 64))
_SMALL_ROWS = 432
_SMALL_OFF = {name: sum(n for _, n in _SMALL[:i]) for i, (name, _) in enumerate(_SMALL)}


def _pack_small(d):
    flat = jnp.concatenate([d[n].reshape(-1).astype(F32) for n, _ in _SMALL])
    return jnp.pad(flat, (0, _SMALL_ROWS * 1024 - flat.shape[0])).reshape(_SMALL_ROWS, 1024)


def _unpack_small(p, name, shape):
    off = _SMALL_OFF[name]
    return p.reshape(-1)[off:off + int(np.prod(shape))].reshape(shape)


_WEIGHTS = ('ln_gain', 'w_out', 'mem_norm', 'w_mem_kv', 'xq_norm', 'xk_norm', 's5_w_in', 's5_lambda_re',
            's5_lambda_im', 's5_log_step', 's5_b_re', 's5_b_im', 's5_c_re', 's5_c_im', 's5_d', 's5_w_glu', 'mla_w_in',
            'mla_q_lora_norm', 'mla_kv_lora_norm', 'mla_w_uq', 'mla_w_ukv', 'mla_q_nope_norm', 'mla_k_nope_norm',
            'mla_q_rope_norm', 'mla_k_rope_norm')
_BIG = ('w_out', 'w_mem_kv', 's5_w_in', 's5_w_glu', 'mla_w_in', 'mla_w_uq', 'mla_w_ukv')


def _pad128(g):
    return jnp.pad(g.reshape(1, -1), ((0, 0), (0, HD - g.shape[-1])))


def kernel(x, mem, positions, ln_gain, w_out, mem_norm, w_mem_kv, xq_norm, xk_norm, s5_w_in, s5_lambda_re, s5_lambda_im, s5_log_step, s5_b_re, s5_b_im, s5_c_re, s5_c_im, s5_d, s5_w_glu, mla_w_in, mla_q_lora_norm, mla_kv_lora_norm, mla_w_uq, mla_w_ukv, mla_q_nope_norm, mla_k_nope_norm, mla_q_rope_norm, mla_k_rope_norm, loss_target, m_ln_gain, m_w_out, m_mem_norm, m_w_mem_kv, m_xq_norm, m_xk_norm, m_s5_w_in, m_s5_lambda_re, m_s5_lambda_im, m_s5_log_step, m_s5_b_re, m_s5_b_im, m_s5_c_re, m_s5_c_im, m_s5_d, m_s5_w_glu, m_mla_w_in, m_mla_q_lora_norm, m_mla_kv_lora_norm, m_mla_w_uq, m_mla_w_ukv, m_mla_q_nope_norm, m_mla_k_nope_norm, m_mla_q_rope_norm, m_mla_k_rope_norm, v_ln_gain, v_w_out, v_mem_norm, v_w_mem_kv, v_xq_norm, v_xk_norm, v_s5_w_in, v_s5_lambda_re, v_s5_lambda_im, v_s5_log_step, v_s5_b_re, v_s5_b_im, v_s5_c_re, v_s5_c_im, v_s5_d, v_s5_w_glu, v_mla_w_in, v_mla_q_lora_norm, v_mla_kv_lora_norm, v_mla_w_uq, v_mla_w_ukv, v_mla_q_nope_norm, v_mla_k_nope_norm, v_mla_q_rope_norm, v_mla_k_rope_norm):
    weights = dict(ln_gain=ln_gain, w_out=w_out, mem_norm=mem_norm, w_mem_kv=w_mem_kv, xq_norm=xq_norm,
                   xk_norm=xk_norm, s5_w_in=s5_w_in, s5_lambda_re=s5_lambda_re, s5_lambda_im=s5_lambda_im,
                   s5_log_step=s5_log_step, s5_b_re=s5_b_re, s5_b_im=s5_b_im, s5_c_re=s5_c_re, s5_c_im=s5_c_im,
                   s5_d=s5_d, s5_w_glu=s5_w_glu, mla_w_in=mla_w_in, mla_q_lora_norm=mla_q_lora_norm,
                   mla_kv_lora_norm=mla_kv_lora_norm, mla_w_uq=mla_w_uq, mla_w_ukv=mla_w_ukv,
                   mla_q_nope_norm=mla_q_nope_norm, mla_k_nope_norm=mla_k_nope_norm,
                   mla_q_rope_norm=mla_q_rope_norm, mla_k_rope_norm=mla_k_rope_norm)
    m_in = dict(zip(_WEIGHTS, (m_ln_gain, m_w_out, m_mem_norm, m_w_mem_kv, m_xq_norm, m_xk_norm, m_s5_w_in,
                               m_s5_lambda_re, m_s5_lambda_im, m_s5_log_step, m_s5_b_re, m_s5_b_im, m_s5_c_re,
                               m_s5_c_im, m_s5_d, m_s5_w_glu, m_mla_w_in, m_mla_q_lora_norm, m_mla_kv_lora_norm,
                               m_mla_w_uq, m_mla_w_ukv, m_mla_q_nope_norm, m_mla_k_nope_norm, m_mla_q_rope_norm,
                               m_mla_k_rope_norm)))
    v_in = dict(zip(_WEIGHTS, (v_ln_gain, v_w_out, v_mem_norm, v_w_mem_kv, v_xq_norm, v_xk_norm, v_s5_w_in,
                               v_s5_lambda_re, v_s5_lambda_im, v_s5_log_step, v_s5_b_re, v_s5_b_im, v_s5_c_re,
                               v_s5_c_im, v_s5_d, v_s5_w_glu, v_mla_w_in, v_mla_q_lora_norm, v_mla_kv_lora_norm,
                               v_mla_w_uq, v_mla_w_ukv, v_mla_q_nope_norm, v_mla_k_nope_norm, v_mla_q_rope_norm,
                               v_mla_k_rope_norm)))

    x0 = x[0]
    mem0 = mem[0]
    target = loss_target[0]
    L = x0.shape[0]
    nblk, sub = 8, 1
    me = 4 * lax.axis_index("x") + 2 * lax.axis_index("y") + lax.axis_index("c")

    lora = jnp.pad(jnp.concatenate([mla_q_lora_norm, mla_kv_lora_norm], axis=1), ((0, 7), (0, HD - 96)))
    def gather(*shards):
        return _plan_all_gather([s.astype(BF16) for s in shards])

    (W_in_s5,) = _exchange_call(gather(s5_w_in[0]), "ag_s5_w_in")

    ln0, ln1 = ln_gain[0:1], ln_gain[1:2]
    gq0, gq1 = xq_norm[0:1], xq_norm[1:2]
    gk0, gk1 = xk_norm[0:1], xk_norm[1:2]
    gm0, gm1 = mem_norm[0:1], mem_norm[1:2]
    gqn, gkn = mla_q_nope_norm, mla_k_nope_norm
    gqr, gkr = _pad128(mla_q_rope_norm), _pad128(mla_k_rope_norm)

    lr3 = s5_lambda_re.reshape(S5_G, 1, S5_P)
    li3 = s5_lambda_im.reshape(S5_G, 1, S5_P)
    ls3 = s5_log_step.reshape(S5_G, 1, 1)
    btr = jnp.swapaxes(s5_b_re[0], 1, 2)
    bti = jnp.swapaxes(s5_b_im[0], 1, 2)
    a_r, a_i, bbr, bbi = _s5_params(lr3, li3, ls3, btr, bti)
    bm, bmt, cm, cmt = _s5_mats(bbr, bbi, s5_c_re[0], s5_c_im[0])
    a_r2 = a_r.reshape(1, S5_G * S5_P)
    a_i2 = a_i.reshape(1, S5_G * S5_P)
    cmask, rmat = _s5_compact_consts()

    half = ROPE // 2
    inv_freq = ROPE_THETA ** (-jnp.arange(half, dtype=F32) / half)
    invf = jnp.concatenate([inv_freq, inv_freq, jnp.zeros((HD - ROPE,), F32)]).reshape(1, HD)

    def rot_tables(pos, invf):
        ang = pos.astype(F32) * invf
        lane = lax.broadcasted_iota(jnp.int32, ang.shape, 1)
        c = jnp.where(lane < ROPE, jnp.cos(ang), 0.0)
        s = jnp.sin(ang)
        return c, jnp.where(lane < half, -s, 0.0), jnp.where((lane >= half) & (lane < ROPE), s, 0.0)

    tc, ts1, ts2 = _rowwise("rot_tables", rot_tables, [('r', positions.reshape(L, 1)), ('c', invf)],
                            [('r', (L, HD), F32)] * 3, nblk, sub)

    def in_s5(x, g, w):
        proj = _mm_slots(_rms(x, g, D_MODEL).astype(BF16), w)
        return proj[:, :PRIM], proj[:, PRIM:PRIM + XQ], proj[:, PRIM + XQ:]

    kh = D_MODEL // 2
    (u_s5, xq_a, gate_a), (G_mkv0,) = _rowwise(
        "s5_in", in_s5, [('r', x0), ('c', ln0), ('c', W_in_s5)],
        [('r', (L, PRIM), F32), ('r', (L, XQ), F32), ('r', (L, BRANCH), F32)], nblk, sub, host=gather(w_mem_kv[0]))
    (y_s5,), (W_glu, G_in_mla_a) = _s5_fwd(u_s5, bm, cm, a_r2, a_i2, s5_d,
                                           host=gather(s5_w_glu[0], mla_w_in[0, :kh]))

    def glu(y, w):
        z = _mm_slots(_gelu(y).astype(BF16), w)
        return (z[:, :PRIM] * _sigmoid(z[:, PRIM:]),)

    (y2,), (G_out0,) = _rowwise("s5_glu", glu, [('r', y_s5), ('c', W_glu)], [('r', (L, PRIM), F32)], nblk, sub,
                                host=gather(w_out[0]))
    W_mkv0 = G_mkv0.reshape(D_MODEL, 2 * XQ)
    k_a, v_a = _kv_prep(mem0, gm0, W_mkv0, gk0, "kv_prep0")
    x1, (G_in_mla_b,) = _forward_merge(
        x0, y2, 'r', xq_a, gate_a, k_a, v_a, gq0, G_out0.reshape(BRANCH, D_MODEL), "merge0", nblk, sub,
        host=gather(mla_w_in[0, kh:]))
    W_in_mla = _mla_in_perm(jnp.concatenate([_from_slots(G_in_mla_a), _from_slots(G_in_mla_b)], axis=0))

    def in_mla(x, g, w):
        proj = _dot(_rms(x, g, D_MODEL).astype(BF16), w)
        return proj[:, :512], proj[:, 512:768], proj[:, 768:1280], proj[:, 1280:3328], proj[:, 3328:]

    (c_q, c_kv, xq_b, gate_b, krp), (G_uq, W_kv, G_lora) = _rowwise(
        "mla_in", in_mla, [('r', x1), ('c', ln1), ('c', W_in_mla)],
        [('r', (L, Q_LORA), F32), ('r', (L, KV_LORA), F32), ('r', (L, XQ), F32), ('r', (L, BRANCH), F32),
         ('r', (L, HD), F32)], nblk, sub,
        host=_plan_all_gather([mla_w_uq[0].astype(BF16), mla_w_ukv[0].astype(BF16), lora]))
    W_q = _uq_to_kernel(G_uq)
    g_qlora = G_lora[:, 0, :64].reshape(1, Q_LORA)
    g_kvlora = G_lora[:, 0, 64:96].reshape(1, KV_LORA)

    def qkv(c_q, c_kv, krp, tc, ts1, ts2, gql, gkvl, wq, wkv, gqn, gkn, gqr, gkr):
        q = _dot(_rms(c_q, gql, Q_LORA).astype(BF16), wq)
        kv = _mm_slots(_rms(c_kv, gkvl, KV_LORA).astype(BF16), wkv)
        kp, v = _kv_post(kv, krp, gkn, gkr, tc, ts1, ts2)
        return _q_post(q, gqn, gqr, tc, ts1, ts2), kp, v

    qkv_consts = [('c', g_qlora), ('c', g_kvlora), ('c', W_q), ('c', W_kv), ('c', gqn), ('c', gkn), ('c', gqr),
                  ('c', gkr)]
    (q_pad, k_pad, v_h), (G_mkv1, G_out1) = _rowwise(
        "mla_qkv", qkv, [('r', c_q), ('r', c_kv), ('r', krp), ('r', tc), ('r', ts1), ('r', ts2)] + qkv_consts,
        [('r', (L, 2 * PRIM), BF16), ('r', (L, 2 * PRIM), BF16), ('r', (L, PRIM), BF16)], nblk, sub,
        host=gather(w_mem_kv[1], w_out[1]))
    W_out = (G_out0.reshape(BRANCH, D_MODEL), G_out1.reshape(BRANCH, D_MODEL))
    W_mkv = (W_mkv0, G_mkv1.reshape(D_MODEL, 2 * XQ))
    scale = (HD + ROPE) ** -0.5
    attn, lse = _attn_fwd(q_pad, k_pad, v_h, scale)
    k_b, v_b = _kv_prep(mem0, gm1, W_mkv[1], gk1, "kv_prep1")

    def merge_loss(x, mix, xq, gate, k, v, gq, wout, t):
        err = x + _dot(_merge(mix, xq, gate, k, v, gq).astype(BF16), wout) - t
        part = 0.5 * jnp.sum(jnp.sum(err * err, axis=-1, keepdims=True) * (1.0 / D_MODEL), axis=0, keepdims=True)
        return err * (1.0 / D_MODEL), jnp.broadcast_to(part, (1, HD))

    dx2, loss_part = _rowwise(
        "merge1_loss", merge_loss,
        [('r', x1), ('r', attn), ('r', xq_b), ('r', gate_b), ('c', k_b), ('c', v_b), ('c', gq1), ('c', W_out[1]),
         ('r', target)], [('r', (L, D_MODEL), F32), ('a', (1, HD), F32)], nblk, sub)

    dattn, dxq_b, dgate_b, o_b, g_b, dk_b, dv_b, dgq1 = _backward_merge(
        dx2, attn, 'r', xq_b, gate_b, k_b, v_b, gq1, W_out[1], "merge1_bwd", nblk, sub)
    dgm1, dW_mkv1, dgk1 = _kv_prep_bwd(mem0, gm1, W_mkv[1], gk1, dk_b, dv_b, "kv_prep1_bwd")
    dW_out1 = _matmul_tn(o_b, g_b, "dw_out1")
    dq_pad, dk_pad, dv_h = _attn_bwd(q_pad, k_pad, v_h, attn, lse, dattn, scale)

    def qkv_bwd(c_q, c_kv, krp, tc, ts1, ts2, dqp, dkp, dv, gql, gkvl, wq, wkv, gqn, gkn, gqr, gkr):
        cqn, vjp_qn = jax.vjp(lambda a, b: _rms(a, b, Q_LORA), c_q, gql)
        ckvn, vjp_kvn = jax.vjp(lambda a, b: _rms(a, b, KV_LORA), c_kv, gkvl)
        cqn16 = cqn.astype(BF16)
        ckvn16 = ckvn.astype(BF16)
        q = _dot(cqn16, wq)
        kv = _mm_slots(ckvn16, wkv)
        _, vjp_q = jax.vjp(lambda a, b, c: _q_post(a, b, c, tc, ts1, ts2), q, gqn, gqr)
        dq, dgqn, dgqr = vjp_q(dqp.astype(F32))
        _, vjp_kv = jax.vjp(lambda a, b, c, d: _kv_post(a, b, c, d, tc, ts1, ts2), kv, krp, gkn, gkr)
        dkv, dkrp, dgkn, dgkr = vjp_kv((dkp.astype(F32), dv.astype(F32)))
        dq16 = dq.astype(BF16)
        dkv16 = dkv.astype(BF16)
        dc_q, dgql = vjp_qn(_dot_nt(dq16, wq))
        dc_kv, dgkvl = vjp_kvn(_mm_slots_nt(dkv16, wkv))
        return dc_q, dc_kv, dkrp, cqn16, dq16, ckvn16, dkv16, dgql, dgkvl, dgqn, dgkn, dgqr, dgkr

    (dc_q, dc_kv, dkrp, cqn16, dq16, ckvn16, dkv16, dgql, dgkvl, dgqn, dgkn, dgqr, dgkr) = _rowwise(
        "mla_qkv_bwd", qkv_bwd,
        [('r', c_q), ('r', c_kv), ('r', krp), ('r', tc), ('r', ts1), ('r', ts2), ('r', dq_pad), ('r', dk_pad),
         ('r', dv_h)] + qkv_consts,
        [('r', (L, Q_LORA), BF16), ('r', (L, KV_LORA), BF16), ('r', (L, HD), BF16), ('t', (Q_LORA, L), BF16),
         ('r', (L, 2 * PRIM), BF16), ('t', (KV_LORA, L), BF16), ('r', (L, 2 * PRIM), BF16),
         ('a', (1, Q_LORA), F32), ('a', (1, KV_LORA), F32), ('a', (1, HD), F32), ('a', (1, HD), F32),
         ('a', (1, HD), F32), ('a', (1, HD), F32)], nblk, sub)
    dW_q = _matmul_tn(cqn16, dq16, "dw_uq")
    dW_kv = _matmul_tn_slots(ckvn16, dkv16, "dw_ukv")

    def in_bwd(x, dres, g, w, *dparts):
        dproj = jnp.concatenate(dparts, axis=-1).astype(BF16)
        xn, vjp = jax.vjp(lambda a, b: _rms(a, b, D_MODEL), x, g)
        dx, dg = vjp(_mm_slots_nt(dproj, w) if w.ndim == 3 else _dot_nt(dproj, w))
        return dx + dres, xn, dproj, dg

    dx1, xn1, dproj1, dln1 = _rowwise(
        "mla_in_bwd", in_bwd,
        [('r', x1), ('r', dx2), ('c', ln1), ('c', W_in_mla), ('r', dc_q), ('r', dc_kv), ('r', dxq_b), ('r', dgate_b),
         ('r', dkrp)],
        [('r', (L, D_MODEL), F32), ('t', (D_MODEL, L), BF16), ('r', (L, _MLA_IN_PAD), BF16), ('a', (1, D_MODEL), F32)],
        nblk, sub)
    dW_in_mla = _matmul_tn(xn1, dproj1, "dw_mla_in")

    dy2, dxq_a, dgate_a, o_a, g_a, dk_a, dv_a, dgq0 = _backward_merge(
        dx1, y2, 'r', xq_a, gate_a, k_a, v_a, gq0, W_out[0], "merge0_bwd", nblk, sub)
    dgm0, dW_mkv0, dgk0 = _kv_prep_bwd(mem0, gm0, W_mkv[0], gk0, dk_a, dv_a, "kv_prep0_bwd")
    dW_out0 = _matmul_tn(o_a, g_a, "dw_out0")

    def glu_bwd(y, dy2, w):
        h, vjp_h = jax.vjp(_gelu, y)
        h16 = h.astype(BF16)
        z = _mm_slots(h16, w)
        _, vjp_z = jax.vjp(lambda z: z[:, :PRIM] * _sigmoid(z[:, PRIM:]), z)
        dz16 = vjp_z(dy2)[0].astype(BF16)
        return vjp_h(_mm_slots_nt(dz16, w))[0], h16, dz16

    early = [dW_out1.reshape(N_DEV, 256, D_MODEL), dW_mkv1.reshape(N_DEV, 128, 2 * XQ),
             _to_slots(_mla_in_unperm(dW_in_mla)), _uq_from_kernel(dW_q), dW_kv,
             dW_out0.reshape(N_DEV, 256, D_MODEL), dW_mkv0.reshape(N_DEV, 128, 2 * XQ)]
    (dy_s5, h16, dz16), early_pair = _rowwise(
        "s5_glu_bwd", glu_bwd, [('r', y_s5), ('r', dy2), ('c', W_glu)],
        [('r', (L, PRIM), F32), ('t', (PRIM, L), BF16), ('r', (L, 2 * PRIM), BF16)], nblk, sub,
        host=_plan_pair(early))
    dW_glu = _matmul_tn_slots(h16, dz16, "dw_glu")
    early_t = _pair_add(early + [dW_glu], early_pair + list(_exchange_call(_plan_pair([dW_glu]), "rs_pair_glu")),
                        "rs_add_early")
    (du_s5, dbc, dcc, dd, dar, dai), early_recv = _s5_bwd(u_s5, dy_s5, bm, bmt, cmt, a_r2, a_i2, s5_d, cmask, rmat,
                                                          host=_plan_chips(early_t))
    dx0, xn0, dproj0, dln0 = _rowwise(
        "s5_in_bwd", in_bwd,
        [('r', x0), ('r', dx1), ('c', ln0), ('c', W_in_s5), ('r', du_s5), ('r', dxq_a),
         ('r', dgate_a)],
        [('r', (L, D_MODEL), F32), ('t', (D_MODEL, L), BF16), ('r', (L, 2 * BRANCH), BF16), ('a', (1, D_MODEL), F32)],
        nblk, sub)

    dbc4 = dbc.reshape(S5_G, S5_C, 2, S5_P)
    dcc4 = dcc.reshape(S5_G, S5_C, 2, S5_P)
    dlr, dli, dls, dbtr, dbti = _s5_params_bwd(
        lr3, li3, ls3, btr, bti, dar.reshape(S5_G, 1, S5_P), dai.reshape(S5_G, 1, S5_P), dbc4[:, :, 0], dbc4[:, :, 1])

    small_part = {
        "ln_gain": jnp.concatenate([dln0, dln1]), "mem_norm": jnp.concatenate([dgm0, dgm1]),
        "xq_norm": jnp.concatenate([dgq0, dgq1]), "xk_norm": jnp.concatenate([dgk0, dgk1]),
        "s5_lambda_re": dlr, "s5_lambda_im": dli, "s5_log_step": dls,
        "s5_b_re": jnp.swapaxes(dbtr, 1, 2), "s5_b_im": jnp.swapaxes(dbti, 1, 2),
        "s5_c_re": dcc4[:, :, 0], "s5_c_im": -dcc4[:, :, 1], "s5_d": dd,
        "mla_q_lora_norm": dgql, "mla_kv_lora_norm": dgkvl, "mla_q_nope_norm": dgqn, "mla_k_nope_norm": dgkn,
        "mla_q_rope_norm": dgqr[:, :ROPE], "mla_k_rope_norm": dgkr[:, :ROPE],
    }
    loss8 = jnp.pad(loss_part, ((0, 7), (0, 0)))
    dW_in_s5, (small_gath, loss_g) = _matmul_tn_slots(
        xn0, dproj0, "dw_s5_in", host=_plan_all_gather([_pack_small(small_part).astype(BF16), loss8]))

    late = [dW_in_s5]
    late_t = _pair_add(late, list(_exchange_call(_plan_pair(late), "rs_pair_late")), "rs_add_late")
    owners = [("w_out", 1), ("w_mem_kv", 1), ("mla_w_in", 0), ("mla_w_uq", 0), ("mla_w_ukv", 0), ("w_out", 0),
              ("w_mem_kv", 0), ("s5_w_glu", 0)]
    upd, late_recv = _updates_call(early_recv, [weights[n][i] for n, i in owners], [m_in[n][i] for n, i in owners],
                                   [v_in[n][i] for n, i in owners], "update_early", host=_plan_chips(late_t))
    owners.append(("s5_w_in", 0))
    upd.append(_sum_adamw(late_recv[0], s5_w_in[0], m_s5_w_in[0], v_s5_w_in[0], "update_s5_w_in"))
    grads, delta, new_m, new_v = {}, {}, {}, {}
    for n in _BIG:
        parts = [u for u, (o, _) in sorted(zip(upd, owners), key=lambda t: t[1][1]) if o == n]
        grads[n], delta[n], new_m[n], new_v[n] = (jnp.stack([p[j] for p in parts]) for j in range(4))

    gs, loss_sum = _small_sum(small_gath, loss_g, "small_sum")
    loss = loss_sum[0, 0]
    for n, _ in _SMALL:
        shape = weights[n].shape
        if n == "mla_q_lora_norm":
            grads[n] = lax.dynamic_slice(_unpack_small(gs, n, (Q_LORA,)), (me * 64,), (64,)).reshape(shape)
        elif n == "mla_kv_lora_norm":
            grads[n] = lax.dynamic_slice(_unpack_small(gs, n, (KV_LORA,)), (me * 32,), (32,)).reshape(shape)
        else:
            grads[n] = _unpack_small(gs, n, shape)

    def own(a):
        return a.reshape(a.shape[1:]) if a.ndim >= 3 else a

    wide = ("s5_b_re", "s5_b_im", "s5_c_re", "s5_c_im")
    for names, nb, call in (([n for n, _ in _SMALL if n not in wide], 1, "update_small"), (wide, 6, "update_s5_bc")):
        res = _adamw_multi([own(weights[n]) for n in names], [own(grads[n]) for n in names],
                           [own(m_in[n]) for n in names], [own(v_in[n]) for n in names], call, nb)
        for n, (dl, m2, v2) in zip(names, res):
            shape = weights[n].shape
            delta[n], new_m[n], new_v[n] = dl.reshape(shape), m2.reshape(shape), v2.reshape(shape)
    return (loss, dx0[None], *[grads[n] for n in _WEIGHTS], *[delta[n] for n in _WEIGHTS],
            *[new_m[n] for n in _WEIGHTS], *[new_v[n] for n in _WEIGHTS])
```

```python
import functools
import math

import numpy as np
import jax
import jax.numpy as jnp
from jax import lax
from jax.experimental import pallas as pl
from jax.experimental.pallas import tpu as pltpu

F32 = jnp.float32
BF16 = jnp.bfloat16
EPS = 1e-6
NEG = float(np.finfo(np.float32).min)
MESH = pl.DeviceIdType.MESH

N_DEV = 8
D_MODEL = 1024
MEM_LEN = 256
XQ = 512
PRIM = 1536
BRANCH = 2048
X_HEADS = 4
HD = 128
S5_G = 96
S5_P = 64
S5_C = 16
S5_GB = 8
S5_W = S5_GB * S5_P
MLA_H = 12
ROPE = 64
Q_LORA = 512
KV_LORA = 256
ROPE_THETA = 10000.0

ADAM_LR = 0.001
ADAM_B1 = 0.9
ADAM_B2 = 0.999
ADAM_EPS = 1e-08
ADAM_WD = 0.01
ADAM_STEP = 10

VMEM_LIMIT = 56 * 1024 * 1024


def _dot(a, b):
    return jnp.dot(a, b, preferred_element_type=F32)


def _dot_nt(a, b):
    return lax.dot_general(a, b, (((1,), (1,)), ((), ())), preferred_element_type=F32)


def _dot_tn(a, b):
    return lax.dot_general(a, b, (((0,), (0,)), ((), ())), preferred_element_type=F32)


@jax.custom_vjp
def _mm(a, b):
    return _dot(a.astype(BF16), b.astype(BF16))


def _mm_fwd(a, b):
    return _mm(a, b), (a, b)


def _mm_bwd(res, g):
    a, b = res
    gb = g.astype(BF16)
    return _dot_nt(gb, b.astype(BF16)).astype(a.dtype), _dot_tn(a.astype(BF16), gb).astype(b.dtype)


_mm.defvjp(_mm_fwd, _mm_bwd)


@jax.custom_vjp
def _mm_nt(a, b):
    return _dot_nt(a.astype(BF16), b.astype(BF16))


def _mm_nt_fwd(a, b):
    return _mm_nt(a, b), (a, b)


def _mm_nt_bwd(res, g):
    a, b = res
    gb = g.astype(BF16)
    return _dot(gb, b.astype(BF16)).astype(a.dtype), _dot_tn(gb, a.astype(BF16)).astype(b.dtype)


_mm_nt.defvjp(_mm_nt_fwd, _mm_nt_bwd)


@jax.custom_vjp
def _softmax(s):
    m = jnp.max(s, axis=-1, keepdims=True)
    e = jnp.exp(s - m)
    return e / jnp.sum(e, axis=-1, keepdims=True)


def _softmax_fwd(s):
    p = _softmax(s)
    return p, p


def _softmax_bwd(p, g):
    return (p * (g - jnp.sum(p * g, axis=-1, keepdims=True)),)


_softmax.defvjp(_softmax_fwd, _softmax_bwd)


def _rms(x, g, n):
    ms = jnp.sum(x * x, axis=-1, keepdims=True) * (1.0 / n)
    return x * lax.rsqrt(ms + EPS) * g


def _sigmoid(x):
    return 1.0 / (1.0 + jnp.exp(-x))


def _silu(x):
    return x * _sigmoid(x)


def _gelu(x):
    c = math.sqrt(2.0 / math.pi)
    return 0.5 * x * (1.0 + jnp.tanh(c * (x + 0.044715 * (x * x * x))))


@jax.custom_vjp
def _rot(x, c, s1, s2):
    return x * c + pltpu.roll(x, 96, 1) * s1 + pltpu.roll(x, 32, 1) * s2


def _rot_fwd(x, c, s1, s2):
    return _rot(x, c, s1, s2), (c, s1, s2)


def _rot_bwd(res, g):
    c, s1, s2 = res
    dx = g * c + pltpu.roll(g * s1, 32, 1) + pltpu.roll(g * s2, 96, 1)
    return dx, jnp.zeros_like(c), jnp.zeros_like(s1), jnp.zeros_like(s2)


_rot.defvjp(_rot_fwd, _rot_bwd)


def _mem_attn(xq, k, v, gq):
    outs = []
    for h in range(X_HEADS):
        sl = slice(HD * h, HD * (h + 1))
        q = _rms(xq[:, sl], gq, HD)
        p = _softmax(_mm_nt(q, k[:, sl]) * (HD ** -0.5))
        outs.append(_mm(p, v[:, sl]))
    return jnp.concatenate(outs, axis=-1)


def _merge(mix, xq, gate, k, v, gq):
    return jnp.concatenate([mix, _mem_attn(xq, k, v, gq)], axis=-1) * _silu(gate)


def _q_post(q, gqn, gqr, c, s1, s2):
    pieces = []
    for h in range(MLA_H):
        pieces.append(_rms(q[:, HD * h:HD * (h + 1)], gqn, HD))
        pieces.append(_rot(_rms(q[:, PRIM + HD * h:PRIM + HD * (h + 1)], gqr, ROPE), c, s1, s2))
    return jnp.concatenate(pieces, axis=-1)


def _kv_post(kv, krp, gkn, gkr, c, s1, s2):
    kr = _rot(_rms(krp, gkr, ROPE), c, s1, s2)
    pieces, vals = [], []
    for h in range(MLA_H):
        pieces.append(_rms(kv[:, 2 * HD * h:2 * HD * h + HD], gkn, HD))
        pieces.append(kr)
        vals.append(kv[:, 2 * HD * h + HD:2 * HD * (h + 1)])
    return jnp.concatenate(pieces, axis=-1), jnp.concatenate(vals, axis=-1)


def _rowwise(name, fn, ins, outs, nblk, sub=1, host=None):
    n_in = len(ins)

    def spec(kind, shape):
        if kind == 'r':
            return pl.BlockSpec((shape[0] // nblk, shape[1]), lambda i: (i, 0))
        if kind == 't':
            return pl.BlockSpec((shape[0], shape[1] // nblk), lambda i: (0, i))
        zeros = (0,) * len(shape)
        return pl.BlockSpec(tuple(shape), lambda i: zeros)

    def body(*refs):
        i = pl.program_id(0)
        res = fn(*[r[...] for r in refs[:n_in]])
        for (kind, _, _), ref, val in zip(outs, refs[n_in:], res):
            if kind == 'a':
                @pl.when(i == 0)
                def _():
                    ref[...] = jnp.zeros_like(ref)
                ref[...] += val.astype(ref.dtype)
            elif kind == 't':
                ref[...] = val.astype(F32).T.astype(ref.dtype)
            else:
                ref[...] = val.astype(ref.dtype)

    res, hosted = _hosting_call(
        body, name, nblk, host, [a for _, a in ins], [spec(k, a.shape) for k, a in ins],
        [jax.ShapeDtypeStruct(tuple(s), d) for _, s, d in outs], [spec(k, s) for k, s, _ in outs], [])
    return res if host is None else (res, hosted)


def _matmul_tn(at, g, name, out_dtype=BF16):
    K, L = at.shape
    N = g.shape[1]
    tn = next(t for t in (512, 384, 256, 128) if N % t == 0)

    def body(a_ref, g_ref, o_ref):
        o_ref[...] = _dot(a_ref[...], g_ref[...]).astype(o_ref.dtype)

    return pl.pallas_call(
        body, name=name, grid=(N // tn,),
        in_specs=[pl.BlockSpec((K, L), lambda n: (0, 0)), pl.BlockSpec((L, tn), lambda n: (0, n))],
        out_specs=pl.BlockSpec((K, tn), lambda n: (0, n)),
        out_shape=jax.ShapeDtypeStruct((K, N), out_dtype),
        compiler_params=pltpu.CompilerParams(dimension_semantics=("arbitrary",), vmem_limit_bytes=VMEM_LIMIT),
    )(at, g)


def _matmul_tn_slots(at, g, name, host=None):
    K, L = at.shape
    n = g.shape[1] // N_DEV

    def body(a_ref, g_ref, o_ref):
        o_ref[...] = _dot(a_ref[...], g_ref[...]).astype(o_ref.dtype)

    res, hosted = _hosting_call(
        body, name, N_DEV, host, [at, g],
        [pl.BlockSpec((K, L), lambda d: (0, 0)), pl.BlockSpec((L, n), lambda d: (0, d))],
        [jax.ShapeDtypeStruct((N_DEV, K, n), BF16)], [pl.BlockSpec((None, K, n), lambda d: (d, 0, 0))], [])
    return res[0] if host is None else (res[0], hosted)


def _mm_slots(a16, w):
    return jnp.concatenate([_dot(a16, w[d]) for d in range(N_DEV)], axis=-1)


def _mm_slots_nt(g16, w):
    n = w.shape[2]
    out = _dot_nt(g16[:, 0:n], w[0])
    for d in range(1, N_DEV):
        out = out + _dot_nt(g16[:, d * n:(d + 1) * n], w[d])
    return out


class _Exchange:
    def __init__(self, ins, outs, scratch, start, finish):
        self.ins, self.outs, self.scratch, self.start, self.finish = ins, outs, scratch, start, finish


def _xyc():
    return lax.axis_index("x"), lax.axis_index("y"), lax.axis_index("c")


def _plan_all_gather(xs):
    n = len(xs)

    def build(x_refs, out_refs, sems):
        send_sems, recv_sems, local_sems = sems
        x, y, c = _xyc()

        def copies(k, block, to, own=False):
            slot = 4 * block[0] + 2 * block[1] + block[2]
            return [pltpu.make_async_remote_copy(
                src_ref=x_refs[a] if own else out_refs[a].at[slot], dst_ref=out_refs[a].at[slot],
                send_sem=send_sems.at[k * n + a], recv_sem=recv_sems.at[k * n + a], device_id=to,
                device_id_type=MESH) for a in range(n)]

        mine = [pltpu.make_async_copy(x_refs[a], out_refs[a].at[4 * x + 2 * y + c], local_sems.at[a])
                for a in range(n)]
        return copies, mine, (x, y, c), [(1 - x, y), (x, 1 - y), (1 - x, 1 - y)]

    def first_copies(copies, me, chips):
        x, y, c = me
        first = copies(0, me, (x, y, 1 - c), own=True)
        for j, chip in enumerate(chips):
            first += copies(1 + j, me, (*chip, c), own=True)
        return first

    def start(x_refs, out_refs, sems):
        copies, mine, me, chips = build(x_refs, out_refs, sems)
        for cp in mine + first_copies(copies, me, chips):
            cp.start()

    def finish(x_refs, out_refs, sems):
        copies, mine, me, chips = build(x_refs, out_refs, sems)
        x, y, c = me
        passed = []
        for j, chip in enumerate(chips):
            for cp in copies(1 + j, (*chip, c), me):
                cp.wait_recv()
            fwd = copies(4 + j, (*chip, c), (x, y, 1 - c))
            for cp in fwd:
                cp.start()
            passed += fwd
        for cp in copies(0, (x, y, 1 - c), me):
            cp.wait_recv()
        for j, chip in enumerate(chips):
            for cp in copies(4 + j, (*chip, 1 - c), me):
                cp.wait_recv()
        for cp in first_copies(copies, me, chips) + passed:
            cp.wait_send()
        for cp in mine:
            cp.wait()

    return _Exchange(list(xs), [jax.ShapeDtypeStruct((N_DEV,) + a.shape, a.dtype) for a in xs],
                     [pltpu.SemaphoreType.DMA((7 * n,)), pltpu.SemaphoreType.DMA((7 * n,)),
                      pltpu.SemaphoreType.DMA((n,))], start, finish)


_CHIPS = ((0, 0), (0, 1), (1, 0), (1, 1))


def _plan_pair(sends):
    n = len(sends)

    def build(s_refs, o_refs, sems):
        send_sems, recv_sems = sems
        x, y, c = _xyc()
        return [pltpu.make_async_remote_copy(
            src_ref=s_refs[a].at[4 * px + 2 * py + 1 - c], dst_ref=o_refs[a].at[j],
            send_sem=send_sems.at[j * n + a], recv_sem=recv_sems.at[j * n + a], device_id=(x, y, 1 - c),
            device_id_type=MESH) for j, (px, py) in enumerate(_CHIPS) for a in range(n)]

    def start(s_refs, o_refs, sems):
        for cp in build(s_refs, o_refs, sems):
            cp.start()

    def finish(s_refs, o_refs, sems):
        for cp in build(s_refs, o_refs, sems):
            cp.wait_recv()
            cp.wait_send()

    return _Exchange(list(sends), [jax.ShapeDtypeStruct((4,) + a.shape[1:], a.dtype) for a in sends],
                     [pltpu.SemaphoreType.DMA((4 * n,)), pltpu.SemaphoreType.DMA((4 * n,))], start, finish)


def _plan_chips(ts):
    n = len(ts)
    flips = ((1, 0), (0, 1), (1, 1))

    def build(t_refs, o_refs, sems):
        send_sems, recv_sems, local_sems = sems
        x, y, c = _xyc()
        mine = 2 * x + y
        local = [pltpu.make_async_copy(t_refs[a].at[mine], o_refs[a].at[mine], local_sems.at[a]) for a in range(n)]
        remote = []
        for k, (fx, fy) in enumerate(flips):
            px = 1 - x if fx else x
            py = 1 - y if fy else y
            remote += [pltpu.make_async_remote_copy(
                src_ref=t_refs[a].at[2 * px + py], dst_ref=o_refs[a].at[mine],
                send_sem=send_sems.at[k * n + a], recv_sem=recv_sems.at[k * n + a], device_id=(px, py, c),
                device_id_type=MESH) for a in range(n)]
        return local, remote

    def start(t_refs, o_refs, sems):
        local, remote = build(t_refs, o_refs, sems)
        for cp in local + remote:
            cp.start()

    def finish(t_refs, o_refs, sems):
        local, remote = build(t_refs, o_refs, sems)
        for cp in remote:
            cp.wait_recv()
        for cp in remote:
            cp.wait_send()
        for cp in local:
            cp.wait()

    return _Exchange(list(ts), [jax.ShapeDtypeStruct(a.shape, a.dtype) for a in ts],
                     [pltpu.SemaphoreType.DMA((3 * n,)), pltpu.SemaphoreType.DMA((3 * n,)),
                      pltpu.SemaphoreType.DMA((n,))], start, finish)


def _exchange_call(plan, name):
    n = len(plan.ins)

    def body(*refs):
        ins, outs, sems = refs[:n], refs[n:2 * n], refs[2 * n:]
        plan.start(ins, outs, sems)
        plan.finish(ins, outs, sems)

    return pl.pallas_call(
        body, name=name, out_shape=plan.outs,
        in_specs=[pl.BlockSpec(memory_space=pl.ANY)] * n, out_specs=[pl.BlockSpec(memory_space=pl.ANY)] * n,
        scratch_shapes=plan.scratch,
    )(*plan.ins)


def _pair_add(sends, fromsib, name):
    n = len(sends)
    nb = 8

    def body(*refs):
        c = lax.axis_index("c")
        for a in range(n):
            s_ref, f_ref, t_ref = refs[a], refs[n + a], refs[2 * n + a]
            for j in range(4):
                t_ref[j] = (s_ref[2 * j + c].astype(F32) + f_ref[j].astype(F32)).astype(t_ref.dtype)

    def spec(a, lead):
        return pl.BlockSpec((lead, a.shape[1] // nb, a.shape[2]), lambda i: (0, i, 0))

    return pl.pallas_call(
        body, name=name, grid=(nb,),
        in_specs=[spec(a, N_DEV) for a in sends] + [spec(a, 4) for a in fromsib],
        out_specs=[spec(a, 4) for a in fromsib],
        out_shape=[jax.ShapeDtypeStruct(a.shape, a.dtype) for a in fromsib],
        compiler_params=pltpu.CompilerParams(dimension_semantics=("arbitrary",), vmem_limit_bytes=VMEM_LIMIT),
    )(*sends, *fromsib)


def _adamw_vals(w, g, m, v):
    m2 = ADAM_B1 * m + (1.0 - ADAM_B1) * g
    v2 = ADAM_B2 * v + (1.0 - ADAM_B2) * (g * g)
    m_hat = m2 / (1.0 - ADAM_B1 ** ADAM_STEP)
    v_hat = v2 / (1.0 - ADAM_B2 ** ADAM_STEP)
    delta = -ADAM_LR * (m_hat / (jnp.sqrt(v_hat) + ADAM_EPS) + ADAM_WD * w)
    return delta, m2, v2


def _sum_adamw(recv, w, m, v, name):
    R, C = w.shape
    ns = recv.shape[0]
    br = next((t for t in (256, 128, 64, 32, 16) if R % t == 0), R)

    def body(r_ref, w_ref, m_ref, v_ref, g_ref, d_ref, m2_ref, v2_ref):
        g = r_ref[0].astype(F32)
        for d in range(1, ns):
            g = g + r_ref[d].astype(F32)
        dl, m2, v2 = _adamw_vals(w_ref[...], g, m_ref[...], v_ref[...])
        g_ref[...] = g
        d_ref[...] = dl
        m2_ref[...] = m2
        v2_ref[...] = v2

    spec = pl.BlockSpec((br, C), lambda i: (i, 0))
    return pl.pallas_call(
        body, name=name, grid=(R // br,),
        in_specs=[pl.BlockSpec((ns, br, C), lambda i: (0, i, 0)), spec, spec, spec], out_specs=[spec] * 4,
        out_shape=[jax.ShapeDtypeStruct((R, C), F32)] * 4,
        compiler_params=pltpu.CompilerParams(dimension_semantics=("arbitrary",)),
    )(recv, w, m, v)


def _updates_call(recvs, ws, ms, vs, name, host=None):
    n = len(recvs)
    nb = 8

    def body(*refs):
        for a in range(n):
            r_ref, w_ref, m_ref, v_ref = refs[a], refs[n + a], refs[2 * n + a], refs[3 * n + a]
            g_ref, d_ref, m2_ref, v2_ref = refs[4 * n + 4 * a:4 * n + 4 * a + 4]
            g = r_ref[0].astype(F32)
            for d in range(1, r_ref.shape[0]):
                g = g + r_ref[d].astype(F32)
            dl, m2, v2 = _adamw_vals(w_ref[...], g, m_ref[...], v_ref[...])
            g_ref[...] = g
            d_ref[...] = dl
            m2_ref[...] = m2
            v2_ref[...] = v2

    def spec2(w):
        return pl.BlockSpec((w.shape[0] // nb, w.shape[1]), lambda i: (i, 0))

    def spec3(r):
        return pl.BlockSpec((r.shape[0], r.shape[1] // nb, r.shape[2]), lambda i: (0, i, 0))

    res, hosted = _hosting_call(
        body, name, nb, host, list(recvs) + list(ws) + list(ms) + list(vs),
        [spec3(r) for r in recvs] + [spec2(w) for w in ws] * 3,
        [jax.ShapeDtypeStruct(w.shape, F32) for w in ws for _ in range(4)],
        [spec2(w) for w in ws for _ in range(4)], [])
    return [res[4 * a:4 * a + 4] for a in range(n)], hosted


def _small_sum(gath, loss_g, name):
    _, R, C = gath.shape
    br = R // 3

    def body(g_ref, l_ref, go_ref, lo_ref):
        g = g_ref[0].astype(F32)
        lsum = l_ref[0]
        for d in range(1, N_DEV):
            g = g + g_ref[d].astype(F32)
            lsum = lsum + l_ref[d]
        go_ref[...] = g
        lo_ref[...] = lsum

    return pl.pallas_call(
        body, name=name, grid=(R // br,),
        in_specs=[pl.BlockSpec((N_DEV, br, C), lambda i: (0, i, 0)),
                  pl.BlockSpec((N_DEV, 8, HD), lambda i: (0, 0, 0))],
        out_specs=[pl.BlockSpec((br, C), lambda i: (i, 0)), pl.BlockSpec((8, HD), lambda i: (0, 0))],
        out_shape=[jax.ShapeDtypeStruct((R, C), F32), jax.ShapeDtypeStruct((8, HD), F32)],
        compiler_params=pltpu.CompilerParams(dimension_semantics=("arbitrary",)),
    )(gath, loss_g)


def _adamw_multi(ws, gs, ms, vs, name, nblk=1):
    n = len(ws)

    def body(*refs):
        for a in range(n):
            dl, m2, v2 = _adamw_vals(refs[a][...], refs[n + a][...], refs[2 * n + a][...], refs[3 * n + a][...])
            refs[4 * n + 3 * a][...] = dl
            refs[4 * n + 3 * a + 1][...] = m2
            refs[4 * n + 3 * a + 2][...] = v2

    def spec(x):
        rest = (0,) * (x.ndim - 1)
        return pl.BlockSpec((x.shape[0] // nblk,) + tuple(x.shape[1:]), lambda i: (i,) + rest)

    res = pl.pallas_call(
        body, name=name, grid=(nblk,),
        in_specs=[spec(w) for w in ws] * 4, out_specs=[spec(w) for w in ws for _ in range(3)],
        out_shape=[jax.ShapeDtypeStruct(w.shape, F32) for w in ws for _ in range(3)],
        compiler_params=pltpu.CompilerParams(dimension_semantics=("arbitrary",), vmem_limit_bytes=VMEM_LIMIT),
    )(*ws, *gs, *ms, *vs)
    return [res[3 * a:3 * a + 3] for a in range(n)]


def _s5_param_fn(lr, li, ls, btr, bti):
    step = jnp.exp(ls)
    er = jnp.exp(lr * step)
    ang = li * step
    ar = er * jnp.cos(ang)
    ai = er * jnp.sin(ang)
    nr = ar - 1.0
    den = lr * lr + li * li
    fr = (nr * lr + ai * li) / den
    fi = (ai * lr - nr * li) / den
    return ar, ai, fr * btr - fi * bti, fr * bti + fi * btr


def _s5_params(lr, li, ls, btr, bti):
    def body(lr_ref, li_ref, ls_ref, br_ref, bi_ref, ar_ref, ai_ref, bbr_ref, bbi_ref):
        ar, ai, bbr, bbi = _s5_param_fn(lr_ref[...], li_ref[...], ls_ref[...], br_ref[...], bi_ref[...])
        ar_ref[...] = ar
        ai_ref[...] = ai
        bbr_ref[...] = bbr
        bbi_ref[...] = bbi

    sd = jax.ShapeDtypeStruct
    return pl.pallas_call(
        body, name="s5_params",
        out_shape=[sd(lr.shape, F32), sd(lr.shape, F32), sd(btr.shape, F32), sd(btr.shape, F32)],
    )(lr, li, ls, btr, bti)


def _s5_params_bwd(lr, li, ls, btr, bti, dar, dai, dbbr, dbbi):
    def body(lr_ref, li_ref, ls_ref, br_ref, bi_ref, dar_ref, dai_ref, dbbr_ref, dbbi_ref,
             dlr_ref, dli_ref, dls_ref, dbr_ref, dbi_ref):
        _, vjp = jax.vjp(_s5_param_fn, lr_ref[...], li_ref[...], ls_ref[...], br_ref[...], bi_ref[...])
        dlr, dli, dls, dbr, dbi = vjp((dar_ref[...], dai_ref[...], dbbr_ref[...], dbbi_ref[...]))
        dlr_ref[...] = dlr
        dli_ref[...] = dli
        dls_ref[...] = dls
        dbr_ref[...] = dbr
        dbi_ref[...] = dbi

    sd = jax.ShapeDtypeStruct
    return pl.pallas_call(
        body, name="s5_params_bwd",
        out_shape=[sd(lr.shape, F32), sd(lr.shape, F32), sd(ls.shape, F32), sd(btr.shape, F32), sd(btr.shape, F32)],
    )(lr, li, ls, btr, bti, dar, dai, dbbr, dbbi)


def _cpow(ar, ai, n):
    assert n & (n - 1) == 0
    while n > 1:
        ar, ai = ar * ar - ai * ai, 2.0 * ar * ai
        n //= 2
    return ar, ai


def _scan(st, cr, ci, init, nk, reverse, store, prev=None):
    W = S5_W

    def step(j, carry):
        k = nk - 1 - j if reverse else j
        rows = pl.ds(pl.multiple_of(k * 8, 8), 8)
        sr, si = carry[0], carry[1]
        nsr = cr * sr - ci * si + st[rows, 0:W]
        nsi = cr * si + ci * sr + st[rows, W:2 * W]
        if store:
            st[rows, 0:W] = nsr
            st[rows, W:2 * W] = nsi
        if prev is None:
            return nsr, nsi
        prows = pl.ds(pl.multiple_of(jnp.maximum(k - 1, 0) * 8, 8), 8)
        w = jnp.where(k > 0, 1.0, 0.0).astype(F32)
        pr = prev[prows, 0:W] * w
        pi = prev[prows, W:2 * W] * w
        return nsr, nsi, carry[2] + nsr * pr + nsi * pi, carry[3] + nsi * pr - nsr * pi

    return lax.fori_loop(0, nk, step, init, unroll=2)


def _chain(fin, fr, fi, pr, pi, reverse):
    W = S5_W
    fin[:, 0:W] = fr
    fin[:, W:2 * W] = fi
    rowid = lax.broadcasted_iota(jnp.int32, (8, W), 0)
    cr = jnp.zeros((1, W), F32)
    ci = jnp.zeros((1, W), F32)
    init_r = jnp.zeros((8, W), F32)
    init_i = jnp.zeros((8, W), F32)
    for s in (range(7, -1, -1) if reverse else range(8)):
        init_r = jnp.where(rowid == s, cr, init_r)
        init_i = jnp.where(rowid == s, ci, init_i)
        lr = fin[s:s + 1, 0:W]
        li = fin[s:s + 1, W:2 * W]
        cr, ci = lr + pr * cr - pi * ci, li + pr * ci + pi * cr
    return init_r, init_i


def _full_scan(st, fin, ar, ai, nk, reverse, prev=None, carry_in=None, carry_out=None):
    W = S5_W
    cr = jnp.broadcast_to(ar, (8, W))
    ci = jnp.broadcast_to(-ai if reverse else ai, (8, W))
    z = jnp.zeros((8, W), F32)
    if carry_in is None:
        fr, fi = _scan(st, cr, ci, (z, z), nk, reverse, store=False)
        pr, pi = _cpow(ar, -ai if reverse else ai, nk)
        init = _chain(fin, fr, fi, pr, pi, reverse)
    else:
        init = (carry_in[:, 0:W], carry_in[:, W:2 * W])
    if carry_out is not None:
        carry_out[:, 0:W] = init[0]
        carry_out[:, W:2 * W] = init[1]
    if prev is None:
        return _scan(st, cr, ci, init, nk, reverse, store=True)
    return _scan(st, cr, ci, init + (z, z), nk, reverse, store=True, prev=prev)


def _s5_specs(L):
    W2 = 2 * S5_W
    GC = S5_GB * S5_C
    col = pl.BlockSpec((L, GC), lambda g: (0, g))
    vec = pl.BlockSpec((1, GC), lambda g: (0, g))
    avec = pl.BlockSpec((1, S5_W), lambda g: (0, g))
    bmat = pl.BlockSpec((None, GC, W2), lambda g: (g, 0, 0))
    cmat = pl.BlockSpec((None, W2, GC), lambda g: (g, 0, 0))
    return col, vec, avec, bmat, cmat


def _interleave(dst, src, nk):
    for s in range(8):
        dst[pl.ds(s, nk, stride=8), :] = src[s * nk:(s + 1) * nk, :]


def _deinterleave(dst, src, nk):
    for s in range(8):
        dst[s * nk:(s + 1) * nk, :] = src[pl.ds(s, nk, stride=8), :].astype(dst.dtype)


def _hosting_call(body, name, nsteps, host, ins, in_specs, outs, out_specs, scratch):
    grid = (nsteps,) if isinstance(nsteps, int) else tuple(nsteps)
    params = pltpu.CompilerParams(dimension_semantics=("arbitrary",) * len(grid), vmem_limit_bytes=VMEM_LIMIT)
    if host is None:
        res = pl.pallas_call(
            body, name=name, grid=grid, in_specs=in_specs, out_specs=out_specs, out_shape=outs,
            scratch_shapes=scratch, compiler_params=params,
        )(*ins)
        return list(res), []
    n_in, n_out, n_sc = len(ins), len(outs), len(scratch)
    h_in, h_out = len(host.ins), len(host.outs)

    def hosted(*refs):
        a = refs[:n_in]
        ha = refs[n_in:n_in + h_in]
        o = refs[n_in + h_in:n_in + h_in + n_out]
        ho = refs[n_in + h_in + n_out:n_in + h_in + n_out + h_out]
        sc = refs[n_in + h_in + n_out + h_out:n_in + h_in + n_out + h_out + n_sc]
        hs = refs[n_in + h_in + n_out + h_out + n_sc:]
        first = functools.reduce(jnp.logical_and, [pl.program_id(i) == 0 for i in range(len(grid))])
        last = functools.reduce(jnp.logical_and, [pl.program_id(i) == g - 1 for i, g in enumerate(grid)])

        @pl.when(first)
        def _():
            host.start(ha, ho, hs)

        body(*a, *o, *sc)

        @pl.when(last)
        def _():
            host.finish(ha, ho, hs)

    hbm = pl.BlockSpec(memory_space=pl.ANY)
    res = pl.pallas_call(
        hosted, name=name, grid=grid,
        in_specs=list(in_specs) + [hbm] * h_in, out_specs=list(out_specs) + [hbm] * h_out,
        out_shape=list(outs) + list(host.outs), scratch_shapes=list(scratch) + list(host.scratch),
        compiler_params=params,
    )(*ins, *host.ins)
    return list(res[:n_out]), list(res[n_out:])


def _s5_fwd(u, bm, cm, ar, ai, dvec, host=None):
    L = u.shape[0]
    nk = L // 8
    GC = S5_GB * S5_C
    nb = S5_G // S5_GB
    col, vec, avec, bmat, cmat = _s5_specs(L)

    def body(u_ref, b_ref, c_ref, ar_ref, ai_ref, d_ref, y_ref, carry_ref, st, fin, ui, yi):
        _interleave(ui, u_ref, nk)
        for r in range(8):
            rows = slice(r * nk, (r + 1) * nk)
            st[rows, :] = _dot(ui[rows, :].astype(BF16), b_ref[...])
        _full_scan(st, fin, ar_ref[...], ai_ref[...], nk, reverse=False, carry_out=carry_ref)
        for r in range(8):
            rows = slice(r * nk, (r + 1) * nk)
            yi[rows, :] = _dot(st[rows, :].astype(BF16), c_ref[...]) + d_ref[...] * ui[rows, :]
        _deinterleave(y_ref, yi, nk)

    return _hosting_call(
        body, "s5_fwd", nb, host,
        [u, bm, cm, ar, ai, dvec], [col, bmat, cmat, avec, avec, vec],
        [jax.ShapeDtypeStruct(u.shape, F32), jax.ShapeDtypeStruct((nb * 8, 2 * S5_W), F32)],
        [col, pl.BlockSpec((8, 2 * S5_W), lambda g: (g, 0))],
        [pltpu.VMEM((L, 2 * S5_W), F32), pltpu.VMEM((8, 2 * S5_W), F32), pltpu.VMEM((L, GC), F32),
         pltpu.VMEM((L, GC), F32)])


def _s5_bwd(u, dy, carry, bm, bmt, cmt, ar, ai, dvec, mask, rmat, host=None):
    L = u.shape[0]
    nk = L // 8
    W = S5_W
    GC = S5_GB * S5_C
    col, vec, avec, bmat, cmat = _s5_specs(L)
    hi = lax.Precision.HIGHEST

    def body(u_ref, dy_ref, carry_ref, b_ref, bt_ref, ct_ref, ar_ref, ai_ref, d_ref, mask_ref, r_ref,
             du_ref, db_ref, dc_ref, dd_ref, dar_ref, dai_ref, sa, sb, fin, ui, dyi, dui):
        ar = ar_ref[...]
        ai = ai_ref[...]
        _interleave(ui, u_ref, nk)
        _interleave(dyi, dy_ref, nk)
        for r in range(8):
            rows = slice(r * nk, (r + 1) * nk)
            sa[rows, :] = _dot(ui[rows, :].astype(BF16), b_ref[...])
            sb[rows, :] = _dot(dyi[rows, :].astype(BF16), ct_ref[...])
        _full_scan(sa, fin, ar, ai, nk, reverse=False, carry_in=carry_ref)
        gr, gi, accr, acci = _full_scan(sb, fin, ar, ai, nk, reverse=True, prev=sa)
        rowid = lax.broadcasted_iota(jnp.int32, (8, W), 0)
        last = pl.ds((nk - 1) * 8, 8)
        pr = jnp.where(rowid == 0, 0.0, pltpu.roll(sa[last, 0:W], 1, 0))
        pi = jnp.where(rowid == 0, 0.0, pltpu.roll(sa[last, W:2 * W], 1, 0))
        accr = accr + gr * pr + gi * pi
        acci = acci + gi * pr - gr * pi
        dar_ref[...] = jnp.sum(accr, axis=0, keepdims=True)
        dai_ref[...] = jnp.sum(acci, axis=0, keepdims=True)
        dbf = jnp.zeros((GC, 2 * W), F32)
        dcf = jnp.zeros((GC, 2 * W), F32)
        dd = jnp.zeros((1, GC), F32)
        for r in range(8):
            rows = slice(r * nk, (r + 1) * nk)
            ub = ui[rows, :]
            dyb = dyi[rows, :]
            gb = sb[rows, :].astype(BF16)
            dui[rows, :] = _dot(gb, bt_ref[...]) + d_ref[...] * dyb
            dbf = dbf + _dot_tn(ub.astype(BF16), gb)
            dcf = dcf + _dot_tn(dyb.astype(BF16), sa[rows, :].astype(BF16))
            dd = dd + jnp.sum(dyb * ub, axis=0, keepdims=True)
        db_ref[...] = jnp.dot(dbf * mask_ref[...], r_ref[...], precision=hi, preferred_element_type=F32)
        dc_ref[...] = jnp.dot(dcf * mask_ref[...], r_ref[...], precision=hi, preferred_element_type=F32)
        dd_ref[...] = dd
        _deinterleave(du_ref, dui, nk)

    cmp_spec = pl.BlockSpec((GC, 2 * S5_P), lambda g: (g, 0))
    whole = lambda shape: pl.BlockSpec(shape, lambda g: (0, 0))
    sd = jax.ShapeDtypeStruct
    return _hosting_call(
        body, "s5_bwd", S5_G // S5_GB, host,
        [u, dy, carry, bm, bmt, cmt, ar, ai, dvec, mask, rmat],
        [col, col, pl.BlockSpec((8, 2 * W), lambda g: (g, 0)), bmat, cmat, bmat, avec, avec, vec, whole(mask.shape),
         whole(rmat.shape)],
        [sd(u.shape, BF16), sd((S5_G * S5_C, 2 * S5_P), F32), sd((S5_G * S5_C, 2 * S5_P), F32),
         sd((1, PRIM), F32), sd((1, S5_G * S5_P), F32), sd((1, S5_G * S5_P), F32)],
        [col, cmp_spec, cmp_spec, vec, avec, avec],
        [pltpu.VMEM((L, 2 * W), F32), pltpu.VMEM((L, 2 * W), F32), pltpu.VMEM((8, 2 * W), F32),
         pltpu.VMEM((L, GC), F32), pltpu.VMEM((L, GC), F32), pltpu.VMEM((L, GC), F32)])


def _s5_mats(bbr, bbi, cre, cim):
    nb = S5_G // S5_GB
    eye = jnp.eye(S5_GB, dtype=F32)
    bb = jnp.stack([bbr, bbi], axis=2).reshape(nb, S5_GB, S5_C, 2, S5_P)
    bm = jnp.einsum('ngcrp,gh->ngcrhp', bb, eye).reshape(nb, S5_GB * S5_C, 2 * S5_W)
    cc = jnp.stack([cre, -cim], axis=2).reshape(nb, S5_GB, S5_C, 2, S5_P)
    cmt = jnp.einsum('ngcrp,gh->ngcrhp', cc, eye).reshape(nb, S5_GB * S5_C, 2 * S5_W)
    return (bm.astype(BF16), jnp.swapaxes(bm, 1, 2).astype(BF16),
            jnp.swapaxes(cmt, 1, 2).astype(BF16), cmt.astype(BF16))


def _s5_compact_consts():
    g_row = np.arange(S5_GB * S5_C) // S5_C
    col = np.arange(2 * S5_W)
    g_col = (col % S5_W) // S5_P
    mask = (g_row[:, None] == g_col[None, :]).astype(np.float32)
    tgt = (col // S5_W) * S5_P + col % S5_P
    rmat = (tgt[:, None] == np.arange(2 * S5_P)[None, :]).astype(np.float32)
    return jnp.asarray(mask), jnp.asarray(rmat)


def _attn_scores(q_ref, k_ref, qb, bq, scale):
    ext = (qb + 1) * bq
    s = _dot_nt(q_ref[qb * bq:ext, :], k_ref[0:ext, :]) * scale
    qpos = lax.broadcasted_iota(jnp.int32, (bq, bq), 0)
    kpos = lax.broadcasted_iota(jnp.int32, (bq, bq), 1)
    diag = jnp.where(kpos <= qpos, s[:, ext - bq:], NEG)
    return diag if qb == 0 else jnp.concatenate([s[:, :ext - bq], diag], axis=-1)


def _attn_fwd(qp, kp, v, scale):
    L = qp.shape[0]
    bq = min(256, L)

    def body(q_ref, k_ref, v_ref, o_ref, lse_ref):
        for qb in range(L // bq):
            rows = slice(qb * bq, (qb + 1) * bq)
            s = _attn_scores(q_ref, k_ref, qb, bq, scale)
            m = jnp.max(s, axis=-1, keepdims=True)
            e = jnp.exp(s - m)
            l = jnp.sum(e, axis=-1, keepdims=True)
            o_ref[rows, :] = _dot(e.astype(BF16), v_ref[0:(qb + 1) * bq, :]) / l
            lse_ref[rows, :] = jnp.broadcast_to(m + jnp.log(l), (bq, HD))

    blk = pl.BlockSpec((L, HD), lambda h: (0, h))
    wide = pl.BlockSpec((L, 2 * HD), lambda h: (0, h))
    return pl.pallas_call(
        body, name="mla_attn_fwd", grid=(MLA_H,),
        in_specs=[wide, wide, blk], out_specs=[blk, blk],
        out_shape=[jax.ShapeDtypeStruct((L, MLA_H * HD), F32)] * 2,
        compiler_params=pltpu.CompilerParams(dimension_semantics=("arbitrary",), vmem_limit_bytes=VMEM_LIMIT),
    )(qp, kp, v)


def _attn_bwd(qp, kp, v, o, lse, do, scale):
    L = qp.shape[0]
    bq = min(256, L)
    nq = L // bq

    def body(q_ref, k_ref, v_ref, o_ref, lse_ref, do_ref, dq_ref, dk_ref, dv_ref, dk_acc, dv_acc):
        dk_acc[...] = jnp.zeros_like(dk_acc)
        dv_acc[...] = jnp.zeros_like(dv_acc)
        for qb in range(nq):
            rows = slice(qb * bq, (qb + 1) * bq)
            ext = (qb + 1) * bq
            do = do_ref[rows, :]
            dob = do.astype(BF16)
            p = jnp.exp(_attn_scores(q_ref, k_ref, qb, bq, scale) - lse_ref[rows, 0:1])
            dp = _dot_nt(dob, v_ref[0:ext, :])
            dsum = jnp.sum(do * o_ref[rows, :], axis=-1, keepdims=True)
            ds = (p * (dp - dsum) * scale).astype(BF16)
            dq_ref[rows, :] = _dot(ds, k_ref[0:ext, :]).astype(dq_ref.dtype)
            dk_acc[0:ext, :] += _dot_tn(ds, q_ref[rows, :])
            dv_acc[0:ext, :] += _dot_tn(p.astype(BF16), dob)
        dk_ref[...] = dk_acc[...].astype(dk_ref.dtype)
        dv_ref[...] = dv_acc[...].astype(dv_ref.dtype)

    sd = jax.ShapeDtypeStruct
    blk = pl.BlockSpec((L, HD), lambda h: (0, h))
    wide = pl.BlockSpec((L, 2 * HD), lambda h: (0, h))
    return pl.pallas_call(
        body, name="mla_attn_bwd", grid=(MLA_H,),
        in_specs=[wide, wide, blk, blk, blk, blk], out_specs=[wide, wide, blk],
        out_shape=[sd((L, MLA_H * 2 * HD), BF16), sd((L, MLA_H * 2 * HD), BF16), sd((L, MLA_H * HD), BF16)],
        scratch_shapes=[pltpu.VMEM((L, 2 * HD), F32), pltpu.VMEM((L, HD), F32)],
        compiler_params=pltpu.CompilerParams(dimension_semantics=("arbitrary",), vmem_limit_bytes=VMEM_LIMIT),
    )(qp, kp, v, o, lse, do)


def _kv_fn(mem, gm, w, gk):
    kv = _mm(_rms(mem, gm, D_MODEL), w)
    k = jnp.concatenate([_rms(kv[:, HD * h:HD * (h + 1)], gk, HD) for h in range(X_HEADS)], axis=-1)
    return k, kv[:, XQ:]


def _kv_prep(mem, gm, w, gk, name):
    def fn(mem, gm, w, gk):
        return _kv_fn(mem, gm, w, gk)
    M = mem.shape[0]
    return _rowwise(name, fn, [('c', mem), ('c', gm), ('c', w), ('c', gk)],
                    [('c', (M, XQ), F32), ('c', (M, XQ), F32)], 1)


def _kv_prep_bwd(mem, gm, w, gk, dk, dv, name):
    def fn(mem, gm, w, gk, dk, dv):
        _, vjp = jax.vjp(lambda a, b, c: _kv_fn(mem, a, b, c), gm, w, gk)
        return vjp((dk, dv))
    return _rowwise(name, fn, [('c', mem), ('c', gm), ('c', w), ('c', gk), ('c', dk), ('c', dv)],
                    [('c', gm.shape, F32), ('c', w.shape, BF16), ('c', gk.shape, F32)], 1)


def _forward_merge(x, mix, mix_kind, xq, gate, k, v, gq, wout, name, nblk, sub, host=None):
    def fn(x, mix, xq, gate, k, v, gq, wout):
        o = _merge(mix, xq, gate, k, v, gq)
        return (x + _dot(o.astype(BF16), wout),)
    L = x.shape[0]
    out = _rowwise(name, fn, [('r', x), (mix_kind, mix), ('r', xq), ('r', gate), ('c', k), ('c', v), ('c', gq),
                              ('c', wout)], [('r', (L, D_MODEL), F32)], nblk, sub, host=host)
    return out[0] if host is None else (out[0][0], out[1])


def _backward_merge(dx, mix, mix_kind, xq, gate, k, v, gq, wout, name, nblk, sub):
    def fn(dx, mix, xq, gate, k, v, gq, wout):
        g16 = dx.astype(BF16)
        do = _dot_nt(g16, wout)
        o, vjp = jax.vjp(_merge, mix, xq, gate, k, v, gq)
        dmix, dxq, dgate, dk, dv, dgq = vjp(do)
        return dmix, dxq, dgate, o, g16, dk, dv, dgq
    L = dx.shape[0]
    return _rowwise(
        name, fn,
        [('r', dx), (mix_kind, mix), ('r', xq), ('r', gate), ('c', k), ('c', v), ('c', gq), ('c', wout)],
        [('r', (L, PRIM), F32), ('r', (L, XQ), BF16), ('r', (L, BRANCH), BF16), ('t', (BRANCH, L), BF16),
         ('r', (L, D_MODEL), BF16), ('a', k.shape, F32), ('a', v.shape, F32), ('a', gq.shape, F32)], nblk, sub)


_MLA_IN = 3392
_MLA_IN_PAD = 3456


def _from_slots(g):
    _, k, n = g.shape
    return jnp.transpose(g, (1, 0, 2)).reshape(k, N_DEV * n)


def _to_slots(w):
    k = w.shape[0]
    return jnp.transpose(w.reshape(k, N_DEV, -1), (1, 0, 2))


def _uq_to_kernel(g):
    uq = _from_slots(g).reshape(Q_LORA, MLA_H, HD + ROPE)
    return jnp.concatenate([uq[:, :, :HD].reshape(Q_LORA, PRIM),
                            jnp.pad(uq[:, :, HD:], ((0, 0), (0, 0), (0, HD - ROPE))).reshape(Q_LORA, PRIM)], axis=1)


def _uq_from_kernel(d_w_q):
    uq = jnp.concatenate([d_w_q[:, :PRIM].reshape(Q_LORA, MLA_H, HD),
                          d_w_q[:, PRIM:].reshape(Q_LORA, MLA_H, HD)[:, :, :ROPE]], axis=2)
    return _to_slots(uq.reshape(Q_LORA, MLA_H * (HD + ROPE)))


def _mla_in_perm(w):
    return jnp.concatenate([w[:, :768], w[:, 832:], w[:, 768:832], jnp.zeros((w.shape[0], 64), w.dtype)], axis=1)


def _mla_in_unperm(w):
    return jnp.concatenate([w[:, :768], w[:, 3328:3392], w[:, 768:3328]], axis=1)


_SMALL = (("ln_gain", 2048), ("mem_norm", 2048), ("xq_norm", 256), ("xk_norm", 256), ("s5_lambda_re", 6144),
          ("s5_lambda_im", 6144), ("s5_log_step", 96), ("s5_b_re", 98304), ("s5_b_im", 98304), ("s5_c_re", 98304),
          ("s5_c_im", 98304), ("s5_d", 1536), ("mla_q_lora_norm", 512), ("mla_kv_lora_norm", 256),
          ("mla_q_nope_norm", 128), ("mla_k_nope_norm", 128), ("mla_q_rope_norm", 64), ("mla_k_rope_norm", 64))
_SMALL_ROWS = 432
_SMALL_OFF = {name: sum(n for _, n in _SMALL[:i]) for i, (name, _) in enumerate(_SMALL)}


def _pack_small(d):
    flat = jnp.concatenate([d[n].reshape(-1).astype(F32) for n, _ in _SMALL])
    return jnp.pad(flat, (0, _SMALL_ROWS * 1024 - flat.shape[0])).reshape(_SMALL_ROWS, 1024)


def _unpack_small(p, name, shape):
    off = _SMALL_OFF[name]
    return p.reshape(-1)[off:off + int(np.prod(shape))].reshape(shape)


_WEIGHTS = ('ln_gain', 'w_out', 'mem_norm', 'w_mem_kv', 'xq_norm', 'xk_norm', 's5_w_in', 's5_lambda_re',
            's5_lambda_im', 's5_log_step', 's5_b_re', 's5_b_im', 's5_c_re', 's5_c_im', 's5_d', 's5_w_glu', 'mla_w_in',
            'mla_q_lora_norm', 'mla_kv_lora_norm', 'mla_w_uq', 'mla_w_ukv', 'mla_q_nope_norm', 'mla_k_nope_norm',
            'mla_q_rope_norm', 'mla_k_rope_norm')
_BIG = ('w_out', 'w_mem_kv', 's5_w_in', 's5_w_glu', 'mla_w_in', 'mla_w_uq', 'mla_w_ukv')


def _pad128(g):
    return jnp.pad(g.reshape(1, -1), ((0, 0), (0, HD - g.shape[-1])))


def kernel(x, mem, positions, ln_gain, w_out, mem_norm, w_mem_kv, xq_norm, xk_norm, s5_w_in, s5_lambda_re, s5_lambda_im, s5_log_step, s5_b_re, s5_b_im, s5_c_re, s5_c_im, s5_d, s5_w_glu, mla_w_in, mla_q_lora_norm, mla_kv_lora_norm, mla_w_uq, mla_w_ukv, mla_q_nope_norm, mla_k_nope_norm, mla_q_rope_norm, mla_k_rope_norm, loss_target, m_ln_gain, m_w_out, m_mem_norm, m_w_mem_kv, m_xq_norm, m_xk_norm, m_s5_w_in, m_s5_lambda_re, m_s5_lambda_im, m_s5_log_step, m_s5_b_re, m_s5_b_im, m_s5_c_re, m_s5_c_im, m_s5_d, m_s5_w_glu, m_mla_w_in, m_mla_q_lora_norm, m_mla_kv_lora_norm, m_mla_w_uq, m_mla_w_ukv, m_mla_q_nope_norm, m_mla_k_nope_norm, m_mla_q_rope_norm, m_mla_k_rope_norm, v_ln_gain, v_w_out, v_mem_norm, v_w_mem_kv, v_xq_norm, v_xk_norm, v_s5_w_in, v_s5_lambda_re, v_s5_lambda_im, v_s5_log_step, v_s5_b_re, v_s5_b_im, v_s5_c_re, v_s5_c_im, v_s5_d, v_s5_w_glu, v_mla_w_in, v_mla_q_lora_norm, v_mla_kv_lora_norm, v_mla_w_uq, v_mla_w_ukv, v_mla_q_nope_norm, v_mla_k_nope_norm, v_mla_q_rope_norm, v_mla_k_rope_norm):
    weights = dict(ln_gain=ln_gain, w_out=w_out, mem_norm=mem_norm, w_mem_kv=w_mem_kv, xq_norm=xq_norm,
                   xk_norm=xk_norm, s5_w_in=s5_w_in, s5_lambda_re=s5_lambda_re, s5_lambda_im=s5_lambda_im,
                   s5_log_step=s5_log_step, s5_b_re=s5_b_re, s5_b_im=s5_b_im, s5_c_re=s5_c_re, s5_c_im=s5_c_im,
                   s5_d=s5_d, s5_w_glu=s5_w_glu, mla_w_in=mla_w_in, mla_q_lora_norm=mla_q_lora_norm,
                   mla_kv_lora_norm=mla_kv_lora_norm, mla_w_uq=mla_w_uq, mla_w_ukv=mla_w_ukv,
                   mla_q_nope_norm=mla_q_nope_norm, mla_k_nope_norm=mla_k_nope_norm,
                   mla_q_rope_norm=mla_q_rope_norm, mla_k_rope_norm=mla_k_rope_norm)
    m_in = dict(zip(_WEIGHTS, (m_ln_gain, m_w_out, m_mem_norm, m_w_mem_kv, m_xq_norm, m_xk_norm, m_s5_w_in,
                               m_s5_lambda_re, m_s5_lambda_im, m_s5_log_step, m_s5_b_re, m_s5_b_im, m_s5_c_re,
                               m_s5_c_im, m_s5_d, m_s5_w_glu, m_mla_w_in, m_mla_q_lora_norm, m_mla_kv_lora_norm,
                               m_mla_w_uq, m_mla_w_ukv, m_mla_q_nope_norm, m_mla_k_nope_norm, m_mla_q_rope_norm,
                               m_mla_k_rope_norm)))
    v_in = dict(zip(_WEIGHTS, (v_ln_gain, v_w_out, v_mem_norm, v_w_mem_kv, v_xq_norm, v_xk_norm, v_s5_w_in,
                               v_s5_lambda_re, v_s5_lambda_im, v_s5_log_step, v_s5_b_re, v_s5_b_im, v_s5_c_re,
                               v_s5_c_im, v_s5_d, v_s5_w_glu, v_mla_w_in, v_mla_q_lora_norm, v_mla_kv_lora_norm,
                               v_mla_w_uq, v_mla_w_ukv, v_mla_q_nope_norm, v_mla_k_nope_norm, v_mla_q_rope_norm,
                               v_mla_k_rope_norm)))

    x0 = x[0]
    mem0 = mem[0]
    target = loss_target[0]
    L = x0.shape[0]
    nblk, sub = 8, 1
    me = 4 * lax.axis_index("x") + 2 * lax.axis_index("y") + lax.axis_index("c")

    lora = jnp.pad(jnp.concatenate([mla_q_lora_norm, mla_kv_lora_norm], axis=1), ((0, 7), (0, HD - 96)))
    def gather(*shards):
        return _plan_all_gather([s.astype(BF16) for s in shards])

    (W_in_s5,) = _exchange_call(gather(s5_w_in[0]), "ag_s5_w_in")

    ln0, ln1 = ln_gain[0:1], ln_gain[1:2]
    gq0, gq1 = xq_norm[0:1], xq_norm[1:2]
    gk0, gk1 = xk_norm[0:1], xk_norm[1:2]
    gm0, gm1 = mem_norm[0:1], mem_norm[1:2]
    gqn, gkn = mla_q_nope_norm, mla_k_nope_norm
    gqr, gkr = _pad128(mla_q_rope_norm), _pad128(mla_k_rope_norm)

    lr3 = s5_lambda_re.reshape(S5_G, 1, S5_P)
    li3 = s5_lambda_im.reshape(S5_G, 1, S5_P)
    ls3 = s5_log_step.reshape(S5_G, 1, 1)
    btr = jnp.swapaxes(s5_b_re[0], 1, 2)
    bti = jnp.swapaxes(s5_b_im[0], 1, 2)
    a_r, a_i, bbr, bbi = _s5_params(lr3, li3, ls3, btr, bti)
    bm, bmt, cm, cmt = _s5_mats(bbr, bbi, s5_c_re[0], s5_c_im[0])
    a_r2 = a_r.reshape(1, S5_G * S5_P)
    a_i2 = a_i.reshape(1, S5_G * S5_P)
    cmask, rmat = _s5_compact_consts()

    half = ROPE // 2
    inv_freq = ROPE_THETA ** (-jnp.arange(half, dtype=F32) / half)
    invf = jnp.concatenate([inv_freq, inv_freq, jnp.zeros((HD - ROPE,), F32)]).reshape(1, HD)

    def rot_tables(pos, invf):
        ang = pos.astype(F32) * invf
        lane = lax.broadcasted_iota(jnp.int32, ang.shape, 1)
        c = jnp.where(lane < ROPE, jnp.cos(ang), 0.0)
        s = jnp.sin(ang)
        return c, jnp.where(lane < half, -s, 0.0), jnp.where((lane >= half) & (lane < ROPE), s, 0.0)

    tc, ts1, ts2 = _rowwise("rot_tables", rot_tables, [('r', positions.reshape(L, 1)), ('c', invf)],
                            [('r', (L, HD), F32)] * 3, nblk, sub)

    def in_s5(x, g, w):
        proj = _mm_slots(_rms(x, g, D_MODEL).astype(BF16), w)
        return proj[:, :PRIM], proj[:, PRIM:PRIM + XQ], proj[:, PRIM + XQ:]

    kh = D_MODEL // 2
    (u_s5, xq_a, gate_a), (G_mkv0,) = _rowwise(
        "s5_in", in_s5, [('r', x0), ('c', ln0), ('c', W_in_s5)],
        [('r', (L, PRIM), F32), ('r', (L, XQ), F32), ('r', (L, BRANCH), F32)], nblk, sub, host=gather(w_mem_kv[0]))
    (y_s5, s5_carry), (W_glu, G_in_mla_a) = _s5_fwd(u_s5, bm, cm, a_r2, a_i2, s5_d,
                                                    host=gather(s5_w_glu[0], mla_w_in[0, :kh]))

    def glu(y, w):
        z = _mm_slots(_gelu(y).astype(BF16), w)
        return (z[:, :PRIM] * _sigmoid(z[:, PRIM:]),)

    (y2,), (G_out0,) = _rowwise("s5_glu", glu, [('r', y_s5), ('c', W_glu)], [('r', (L, PRIM), F32)], nblk, sub,
                                host=gather(w_out[0]))
    W_mkv0 = G_mkv0.reshape(D_MODEL, 2 * XQ)
    k_a, v_a = _kv_prep(mem0, gm0, W_mkv0, gk0, "kv_prep0")
    x1, (G_in_mla_b,) = _forward_merge(
        x0, y2, 'r', xq_a, gate_a, k_a, v_a, gq0, G_out0.reshape(BRANCH, D_MODEL), "merge0", nblk, sub,
        host=gather(mla_w_in[0, kh:]))
    W_in_mla = _mla_in_perm(jnp.concatenate([_from_slots(G_in_mla_a), _from_slots(G_in_mla_b)], axis=0))

    def in_mla(x, g, w):
        proj = _dot(_rms(x, g, D_MODEL).astype(BF16), w)
        return proj[:, :512], proj[:, 512:768], proj[:, 768:1280], proj[:, 1280:3328], proj[:, 3328:]

    (c_q, c_kv, xq_b, gate_b, krp), (G_uq, W_kv, G_lora) = _rowwise(
        "mla_in", in_mla, [('r', x1), ('c', ln1), ('c', W_in_mla)],
        [('r', (L, Q_LORA), F32), ('r', (L, KV_LORA), F32), ('r', (L, XQ), F32), ('r', (L, BRANCH), F32),
         ('r', (L, HD), F32)], nblk, sub,
        host=_plan_all_gather([mla_w_uq[0].astype(BF16), mla_w_ukv[0].astype(BF16), lora]))
    W_q = _uq_to_kernel(G_uq)
    g_qlora = G_lora[:, 0, :64].reshape(1, Q_LORA)
    g_kvlora = G_lora[:, 0, 64:96].reshape(1, KV_LORA)

    def qkv(c_q, c_kv, krp, tc, ts1, ts2, gql, gkvl, wq, wkv, gqn, gkn, gqr, gkr):
        q = _dot(_rms(c_q, gql, Q_LORA).astype(BF16), wq)
        kv = _mm_slots(_rms(c_kv, gkvl, KV_LORA).astype(BF16), wkv)
        kp, v = _kv_post(kv, krp, gkn, gkr, tc, ts1, ts2)
        return _q_post(q, gqn, gqr, tc, ts1, ts2), kp, v

    qkv_consts = [('c', g_qlora), ('c', g_kvlora), ('c', W_q), ('c', W_kv), ('c', gqn), ('c', gkn), ('c', gqr),
                  ('c', gkr)]
    (q_pad, k_pad, v_h), (G_mkv1, G_out1) = _rowwise(
        "mla_qkv", qkv, [('r', c_q), ('r', c_kv), ('r', krp), ('r', tc), ('r', ts1), ('r', ts2)] + qkv_consts,
        [('r', (L, 2 * PRIM), BF16), ('r', (L, 2 * PRIM), BF16), ('r', (L, PRIM), BF16)], nblk, sub,
        host=gather(w_mem_kv[1], w_out[1]))
    W_out = (G_out0.reshape(BRANCH, D_MODEL), G_out1.reshape(BRANCH, D_MODEL))
    W_mkv = (W_mkv0, G_mkv1.reshape(D_MODEL, 2 * XQ))
    scale = (HD + ROPE) ** -0.5
    attn, lse = _attn_fwd(q_pad, k_pad, v_h, scale)
    k_b, v_b = _kv_prep(mem0, gm1, W_mkv[1], gk1, "kv_prep1")

    def merge_loss(x, mix, xq, gate, k, v, gq, wout, t):
        err = x + _dot(_merge(mix, xq, gate, k, v, gq).astype(BF16), wout) - t
        part = 0.5 * jnp.sum(jnp.sum(err * err, axis=-1, keepdims=True) * (1.0 / D_MODEL), axis=0, keepdims=True)
        return err * (1.0 / D_MODEL), jnp.broadcast_to(part, (1, HD))

    dx2, loss_part = _rowwise(
        "merge1_loss", merge_loss,
        [('r', x1), ('r', attn), ('r', xq_b), ('r', gate_b), ('c', k_b), ('c', v_b), ('c', gq1), ('c', W_out[1]),
         ('r', target)], [('r', (L, D_MODEL), F32), ('a', (1, HD), F32)], nblk, sub)

    dattn, dxq_b, dgate_b, o_b, g_b, dk_b, dv_b, dgq1 = _backward_merge(
        dx2, attn, 'r', xq_b, gate_b, k_b, v_b, gq1, W_out[1], "merge1_bwd", nblk, sub)
    dgm1, dW_mkv1, dgk1 = _kv_prep_bwd(mem0, gm1, W_mkv[1], gk1, dk_b, dv_b, "kv_prep1_bwd")
    dW_out1 = _matmul_tn(o_b, g_b, "dw_out1")
    dq_pad, dk_pad, dv_h = _attn_bwd(q_pad, k_pad, v_h, attn, lse, dattn, scale)

    def qkv_bwd(c_q, c_kv, krp, tc, ts1, ts2, dqp, dkp, dv, gql, gkvl, wq, wkv, gqn, gkn, gqr, gkr):
        cqn, vjp_qn = jax.vjp(lambda a, b: _rms(a, b, Q_LORA), c_q, gql)
        ckvn, vjp_kvn = jax.vjp(lambda a, b: _rms(a, b, KV_LORA), c_kv, gkvl)
        cqn16 = cqn.astype(BF16)
        ckvn16 = ckvn.astype(BF16)
        q = _dot(cqn16, wq)
        kv = _mm_slots(ckvn16, wkv)
        _, vjp_q = jax.vjp(lambda a, b, c: _q_post(a, b, c, tc, ts1, ts2), q, gqn, gqr)
        dq, dgqn, dgqr = vjp_q(dqp.astype(F32))
        _, vjp_kv = jax.vjp(lambda a, b, c, d: _kv_post(a, b, c, d, tc, ts1, ts2), kv, krp, gkn, gkr)
        dkv, dkrp, dgkn, dgkr = vjp_kv((dkp.astype(F32), dv.astype(F32)))
        dq16 = dq.astype(BF16)
        dkv16 = dkv.astype(BF16)
        dc_q, dgql = vjp_qn(_dot_nt(dq16, wq))
        dc_kv, dgkvl = vjp_kvn(_mm_slots_nt(dkv16, wkv))
        return dc_q, dc_kv, dkrp, cqn16, dq16, ckvn16, dkv16, dgql, dgkvl, dgqn, dgkn, dgqr, dgkr

    (dc_q, dc_kv, dkrp, cqn16, dq16, ckvn16, dkv16, dgql, dgkvl, dgqn, dgkn, dgqr, dgkr) = _rowwise(
        "mla_qkv_bwd", qkv_bwd,
        [('r', c_q), ('r', c_kv), ('r', krp), ('r', tc), ('r', ts1), ('r', ts2), ('r', dq_pad), ('r', dk_pad),
         ('r', dv_h)] + qkv_consts,
        [('r', (L, Q_LORA), BF16), ('r', (L, KV_LORA), BF16), ('r', (L, HD), BF16), ('t', (Q_LORA, L), BF16),
         ('r', (L, 2 * PRIM), BF16), ('t', (KV_LORA, L), BF16), ('r', (L, 2 * PRIM), BF16),
         ('a', (1, Q_LORA), F32), ('a', (1, KV_LORA), F32), ('a', (1, HD), F32), ('a', (1, HD), F32),
         ('a', (1, HD), F32), ('a', (1, HD), F32)], nblk, sub)
    dW_q = _matmul_tn(cqn16, dq16, "dw_uq")
    dW_kv = _matmul_tn_slots(ckvn16, dkv16, "dw_ukv")

    def in_bwd(x, dres, g, w, *dparts):
        dproj = jnp.concatenate(dparts, axis=-1).astype(BF16)
        xn, vjp = jax.vjp(lambda a, b: _rms(a, b, D_MODEL), x, g)
        dx, dg = vjp(_mm_slots_nt(dproj, w) if w.ndim == 3 else _dot_nt(dproj, w))
        return dx + dres, xn, dproj, dg

    dx1, xn1, dproj1, dln1 = _rowwise(
        "mla_in_bwd", in_bwd,
        [('r', x1), ('r', dx2), ('c', ln1), ('c', W_in_mla), ('r', dc_q), ('r', dc_kv), ('r', dxq_b), ('r', dgate_b),
         ('r', dkrp)],
        [('r', (L, D_MODEL), F32), ('t', (D_MODEL, L), BF16), ('r', (L, _MLA_IN_PAD), BF16), ('a', (1, D_MODEL), F32)],
        nblk, sub)
    dW_in_mla = _matmul_tn(xn1, dproj1, "dw_mla_in")

    dy2, dxq_a, dgate_a, o_a, g_a, dk_a, dv_a, dgq0 = _backward_merge(
        dx1, y2, 'r', xq_a, gate_a, k_a, v_a, gq0, W_out[0], "merge0_bwd", nblk, sub)
    dgm0, dW_mkv0, dgk0 = _kv_prep_bwd(mem0, gm0, W_mkv[0], gk0, dk_a, dv_a, "kv_prep0_bwd")
    dW_out0 = _matmul_tn(o_a, g_a, "dw_out0")

    def glu_bwd(y, dy2, w):
        h, vjp_h = jax.vjp(_gelu, y)
        h16 = h.astype(BF16)
        z = _mm_slots(h16, w)
        _, vjp_z = jax.vjp(lambda z: z[:, :PRIM] * _sigmoid(z[:, PRIM:]), z)
        dz16 = vjp_z(dy2)[0].astype(BF16)
        return vjp_h(_mm_slots_nt(dz16, w))[0], h16, dz16

    early = [dW_out1.reshape(N_DEV, 256, D_MODEL), dW_mkv1.reshape(N_DEV, 128, 2 * XQ),
             _to_slots(_mla_in_unperm(dW_in_mla)), _uq_from_kernel(dW_q), dW_kv,
             dW_out0.reshape(N_DEV, 256, D_MODEL), dW_mkv0.reshape(N_DEV, 128, 2 * XQ)]
    (dy_s5, h16, dz16), early_pair = _rowwise(
        "s5_glu_bwd", glu_bwd, [('r', y_s5), ('r', dy2), ('c', W_glu)],
        [('r', (L, PRIM), F32), ('t', (PRIM, L), BF16), ('r', (L, 2 * PRIM), BF16)], nblk, sub,
        host=_plan_pair(early))
    dW_glu = _matmul_tn_slots(h16, dz16, "dw_glu")
    early_t = _pair_add(early + [dW_glu], early_pair + list(_exchange_call(_plan_pair([dW_glu]), "rs_pair_glu")),
                        "rs_add_early")
    (du_s5, dbc, dcc, dd, dar, dai), early_recv = _s5_bwd(u_s5, dy_s5, s5_carry, bm, bmt, cmt, a_r2, a_i2, s5_d,
                                                          cmask, rmat, host=_plan_chips(early_t))
    dx0, xn0, dproj0, dln0 = _rowwise(
        "s5_in_bwd", in_bwd,
        [('r', x0), ('r', dx1), ('c', ln0), ('c', W_in_s5), ('r', du_s5), ('r', dxq_a),
         ('r', dgate_a)],
        [('r', (L, D_MODEL), F32), ('t', (D_MODEL, L), BF16), ('r', (L, 2 * BRANCH), BF16), ('a', (1, D_MODEL), F32)],
        nblk, sub)

    dbc4 = dbc.reshape(S5_G, S5_C, 2, S5_P)
    dcc4 = dcc.reshape(S5_G, S5_C, 2, S5_P)
    dlr, dli, dls, dbtr, dbti = _s5_params_bwd(
        lr3, li3, ls3, btr, bti, dar.reshape(S5_G, 1, S5_P), dai.reshape(S5_G, 1, S5_P), dbc4[:, :, 0], dbc4[:, :, 1])

    small_part = {
        "ln_gain": jnp.concatenate([dln0, dln1]), "mem_norm": jnp.concatenate([dgm0, dgm1]),
        "xq_norm": jnp.concatenate([dgq0, dgq1]), "xk_norm": jnp.concatenate([dgk0, dgk1]),
        "s5_lambda_re": dlr, "s5_lambda_im": dli, "s5_log_step": dls,
        "s5_b_re": jnp.swapaxes(dbtr, 1, 2), "s5_b_im": jnp.swapaxes(dbti, 1, 2),
        "s5_c_re": dcc4[:, :, 0], "s5_c_im": -dcc4[:, :, 1], "s5_d": dd,
        "mla_q_lora_norm": dgql, "mla_kv_lora_norm": dgkvl, "mla_q_nope_norm": dgqn, "mla_k_nope_norm": dgkn,
        "mla_q_rope_norm": dgqr[:, :ROPE], "mla_k_rope_norm": dgkr[:, :ROPE],
    }
    loss8 = jnp.pad(loss_part, ((0, 7), (0, 0)))
    dW_in_s5, (small_gath, loss_g) = _matmul_tn_slots(
        xn0, dproj0, "dw_s5_in", host=_plan_all_gather([_pack_small(small_part).astype(BF16), loss8]))

    late = [dW_in_s5]
    late_t = _pair_add(late, list(_exchange_call(_plan_pair(late), "rs_pair_late")), "rs_add_late")
    owners = [("w_out", 1), ("w_mem_kv", 1), ("mla_w_in", 0), ("mla_w_uq", 0), ("mla_w_ukv", 0), ("w_out", 0),
              ("w_mem_kv", 0), ("s5_w_glu", 0)]
    upd, late_recv = _updates_call(early_recv, [weights[n][i] for n, i in owners], [m_in[n][i] for n, i in owners],
                                   [v_in[n][i] for n, i in owners], "update_early", host=_plan_chips(late_t))
    owners.append(("s5_w_in", 0))
    upd.append(_sum_adamw(late_recv[0], s5_w_in[0], m_s5_w_in[0], v_s5_w_in[0], "update_s5_w_in"))
    grads, delta, new_m, new_v = {}, {}, {}, {}
    for n in _BIG:
        parts = [u for u, (o, _) in sorted(zip(upd, owners), key=lambda t: t[1][1]) if o == n]
        grads[n], delta[n], new_m[n], new_v[n] = (jnp.stack([p[j] for p in parts]) for j in range(4))

    gs, loss_sum = _small_sum(small_gath, loss_g, "small_sum")
    loss = loss_sum[0, 0]
    for n, _ in _SMALL:
        shape = weights[n].shape
        if n == "mla_q_lora_norm":
            grads[n] = lax.dynamic_slice(_unpack_small(gs, n, (Q_LORA,)), (me * 64,), (64,)).reshape(shape)
        elif n == "mla_kv_lora_norm":
            grads[n] = lax.dynamic_slice(_unpack_small(gs, n, (KV_LORA,)), (me * 32,), (32,)).reshape(shape)
        else:
            grads[n] = _unpack_small(gs, n, shape)

    def own(a):
        return a.reshape(a.shape[1:]) if a.ndim >= 3 else a

    wide = ("s5_b_re", "s5_b_im", "s5_c_re", "s5_c_im")
    for names, nb, call in (([n for n, _ in _SMALL if n not in wide], 1, "update_small"), (wide, 6, "update_s5_bc")):
        res = _adamw_multi([own(weights[n]) for n in names], [own(grads[n]) for n in names],
                           [own(m_in[n]) for n in names], [own(v_in[n]) for n in names], call, nb)
        for n, (dl, m2, v2) in zip(names, res):
            shape = weights[n].shape
            delta[n], new_m[n], new_v[n] = dl.reshape(shape), m2.reshape(shape), v2.reshape(shape)
    return (loss, dx0[None], *[grads[n] for n in _WEIGHTS], *[delta[n] for n in _WEIGHTS],
            *[new_m[n] for n in _WEIGHTS], *[new_v[n] for n in _WEIGHTS])
```

```python
import functools
import math

import numpy as np
import jax
import jax.numpy as jnp
from jax import lax
from jax.experimental import pallas as pl
from jax.experimental.pallas import tpu as pltpu

F32 = jnp.float32
BF16 = jnp.bfloat16
EPS = 1e-6
NEG = float(np.finfo(np.float32).min)
MESH = pl.DeviceIdType.MESH

N_DEV = 8
D_MODEL = 1024
MEM_LEN = 256
XQ = 512
PRIM = 1536
BRANCH = 2048
X_HEADS = 4
HD = 128
S5_G = 96
S5_P = 64
S5_C = 16
S5_GB = 8
S5_W = S5_GB * S5_P
MLA_H = 12
ROPE = 64
Q_LORA = 512
KV_LORA = 256
ROPE_THETA = 10000.0

ADAM_LR = 0.001
ADAM_B1 = 0.9
ADAM_B2 = 0.999
ADAM_EPS = 1e-08
ADAM_WD = 0.01
ADAM_STEP = 10

VMEM_LIMIT = 56 * 1024 * 1024


def _dot(a, b):
    return jnp.dot(a, b, preferred_element_type=F32)


def _dot_nt(a, b):
    return lax.dot_general(a, b, (((1,), (1,)), ((), ())), preferred_element_type=F32)


def _dot_tn(a, b):
    return lax.dot_general(a, b, (((0,), (0,)), ((), ())), preferred_element_type=F32)


@jax.custom_vjp
def _mm(a, b):
    return _dot(a.astype(BF16), b.astype(BF16))


def _mm_fwd(a, b):
    return _mm(a, b), (a, b)


def _mm_bwd(res, g):
    a, b = res
    gb = g.astype(BF16)
    return _dot_nt(gb, b.astype(BF16)).astype(a.dtype), _dot_tn(a.astype(BF16), gb).astype(b.dtype)


_mm.defvjp(_mm_fwd, _mm_bwd)


@jax.custom_vjp
def _mm_nt(a, b):
    return _dot_nt(a.astype(BF16), b.astype(BF16))


def _mm_nt_fwd(a, b):
    return _mm_nt(a, b), (a, b)


def _mm_nt_bwd(res, g):
    a, b = res
    gb = g.astype(BF16)
    return _dot(gb, b.astype(BF16)).astype(a.dtype), _dot_tn(gb, a.astype(BF16)).astype(b.dtype)


_mm_nt.defvjp(_mm_nt_fwd, _mm_nt_bwd)


@jax.custom_vjp
def _softmax(s):
    m = jnp.max(s, axis=-1, keepdims=True)
    e = jnp.exp(s - m)
    return e / jnp.sum(e, axis=-1, keepdims=True)


def _softmax_fwd(s):
    p = _softmax(s)
    return p, p


def _softmax_bwd(p, g):
    return (p * (g - jnp.sum(p * g, axis=-1, keepdims=True)),)


_softmax.defvjp(_softmax_fwd, _softmax_bwd)


def _rms(x, g, n):
    ms = jnp.sum(x * x, axis=-1, keepdims=True) * (1.0 / n)
    return x * lax.rsqrt(ms + EPS) * g


def _sigmoid(x):
    return 1.0 / (1.0 + jnp.exp(-x))


def _silu(x):
    return x * _sigmoid(x)


def _gelu(x):
    c = math.sqrt(2.0 / math.pi)
    return 0.5 * x * (1.0 + jnp.tanh(c * (x + 0.044715 * (x * x * x))))


@jax.custom_vjp
def _rot(x, c, s1, s2):
    return x * c + pltpu.roll(x, 96, 1) * s1 + pltpu.roll(x, 32, 1) * s2


def _rot_fwd(x, c, s1, s2):
    return _rot(x, c, s1, s2), (c, s1, s2)


def _rot_bwd(res, g):
    c, s1, s2 = res
    dx = g * c + pltpu.roll(g * s1, 32, 1) + pltpu.roll(g * s2, 96, 1)
    return dx, jnp.zeros_like(c), jnp.zeros_like(s1), jnp.zeros_like(s2)


_rot.defvjp(_rot_fwd, _rot_bwd)


def _mem_attn(xq, k, v, gq):
    outs = []
    for h in range(X_HEADS):
        sl = slice(HD * h, HD * (h + 1))
        q = _rms(xq[:, sl], gq, HD)
        p = _softmax(_mm_nt(q, k[:, sl]) * (HD ** -0.5))
        outs.append(_mm(p, v[:, sl]))
    return jnp.concatenate(outs, axis=-1)


def _merge(mix, xq, gate, k, v, gq):
    return jnp.concatenate([mix, _mem_attn(xq, k, v, gq)], axis=-1) * _silu(gate)


def _q_post(q, gqn, gqr, c, s1, s2):
    pieces = []
    for h in range(MLA_H):
        pieces.append(_rms(q[:, HD * h:HD * (h + 1)], gqn, HD))
        pieces.append(_rot(_rms(q[:, PRIM + HD * h:PRIM + HD * (h + 1)], gqr, ROPE), c, s1, s2))
    return jnp.concatenate(pieces, axis=-1)


def _kv_post(kv, krp, gkn, gkr, c, s1, s2):
    kr = _rot(_rms(krp, gkr, ROPE), c, s1, s2)
    pieces, vals = [], []
    for h in range(MLA_H):
        pieces.append(_rms(kv[:, 2 * HD * h:2 * HD * h + HD], gkn, HD))
        pieces.append(kr)
        vals.append(kv[:, 2 * HD * h + HD:2 * HD * (h + 1)])
    return jnp.concatenate(pieces, axis=-1), jnp.concatenate(vals, axis=-1)


def _rowwise(name, fn, ins, outs, nblk, sub=1, host=None):
    n_in = len(ins)

    def spec(kind, shape):
        if kind == 'r':
            return pl.BlockSpec((shape[0] // nblk, shape[1]), lambda i: (i, 0))
        if kind == 't':
            return pl.BlockSpec((shape[0], shape[1] // nblk), lambda i: (0, i))
        zeros = (0,) * len(shape)
        return pl.BlockSpec(tuple(shape), lambda i: zeros)

    def body(*refs):
        i = pl.program_id(0)
        res = fn(*[r[...] for r in refs[:n_in]])
        for (kind, _, _), ref, val in zip(outs, refs[n_in:], res):
            if kind == 'a':
                @pl.when(i == 0)
                def _():
                    ref[...] = jnp.zeros_like(ref)
                ref[...] += val.astype(ref.dtype)
            elif kind == 't':
                ref[...] = val.astype(F32).T.astype(ref.dtype)
            else:
                ref[...] = val.astype(ref.dtype)

    res, hosted = _hosting_call(
        body, name, nblk, host, [a for _, a in ins], [spec(k, a.shape) for k, a in ins],
        [jax.ShapeDtypeStruct(tuple(s), d) for _, s, d in outs], [spec(k, s) for k, s, _ in outs], [])
    return res if host is None else (res, hosted)


def _matmul_tn(at, g, name, out_dtype=BF16):
    K, L = at.shape
    N = g.shape[1]
    tn = next(t for t in (512, 384, 256, 128) if N % t == 0)

    def body(a_ref, g_ref, o_ref):
        o_ref[...] = _dot(a_ref[...], g_ref[...]).astype(o_ref.dtype)

    return pl.pallas_call(
        body, name=name, grid=(N // tn,),
        in_specs=[pl.BlockSpec((K, L), lambda n: (0, 0)), pl.BlockSpec((L, tn), lambda n: (0, n))],
        out_specs=pl.BlockSpec((K, tn), lambda n: (0, n)),
        out_shape=jax.ShapeDtypeStruct((K, N), out_dtype),
        compiler_params=pltpu.CompilerParams(dimension_semantics=("arbitrary",), vmem_limit_bytes=VMEM_LIMIT),
    )(at, g)


def _matmul_tn_slots(at, g, name, host=None):
    K, L = at.shape
    n = g.shape[1] // N_DEV

    def body(a_ref, g_ref, o_ref):
        o_ref[...] = _dot(a_ref[...], g_ref[...]).astype(o_ref.dtype)

    res, hosted = _hosting_call(
        body, name, N_DEV, host, [at, g],
        [pl.BlockSpec((K, L), lambda d: (0, 0)), pl.BlockSpec((L, n), lambda d: (0, d))],
        [jax.ShapeDtypeStruct((N_DEV, K, n), BF16)], [pl.BlockSpec((None, K, n), lambda d: (d, 0, 0))], [])
    return res[0] if host is None else (res[0], hosted)


def _mm_slots(a16, w):
    return jnp.concatenate([_dot(a16, w[d]) for d in range(N_DEV)], axis=-1)


def _mm_slots_nt(g16, w):
    n = w.shape[2]
    out = _dot_nt(g16[:, 0:n], w[0])
    for d in range(1, N_DEV):
        out = out + _dot_nt(g16[:, d * n:(d + 1) * n], w[d])
    return out


class _Exchange:
    def __init__(self, ins, outs, scratch, start, finish):
        self.ins, self.outs, self.scratch, self.start, self.finish = ins, outs, scratch, start, finish


def _xyc():
    return lax.axis_index("x"), lax.axis_index("y"), lax.axis_index("c")


def _plan_all_gather(xs):
    n = len(xs)

    def build(x_refs, out_refs, sems):
        send_sems, recv_sems, local_sems = sems
        x, y, c = _xyc()

        def copies(k, block, to, own=False):
            slot = 4 * block[0] + 2 * block[1] + block[2]
            return [pltpu.make_async_remote_copy(
                src_ref=x_refs[a] if own else out_refs[a].at[slot], dst_ref=out_refs[a].at[slot],
                send_sem=send_sems.at[k * n + a], recv_sem=recv_sems.at[k * n + a], device_id=to,
                device_id_type=MESH) for a in range(n)]

        mine = [pltpu.make_async_copy(x_refs[a], out_refs[a].at[4 * x + 2 * y + c], local_sems.at[a])
                for a in range(n)]
        return copies, mine, (x, y, c), [(1 - x, y), (x, 1 - y), (1 - x, 1 - y)]

    def first_copies(copies, me, chips):
        x, y, c = me
        first = copies(0, me, (x, y, 1 - c), own=True)
        for j, chip in enumerate(chips):
            first += copies(1 + j, me, (*chip, c), own=True)
        return first

    def start(x_refs, out_refs, sems):
        copies, mine, me, chips = build(x_refs, out_refs, sems)
        for cp in mine + first_copies(copies, me, chips):
            cp.start()

    def finish(x_refs, out_refs, sems):
        copies, mine, me, chips = build(x_refs, out_refs, sems)
        x, y, c = me
        passed = []
        for j, chip in enumerate(chips):
            for cp in copies(1 + j, (*chip, c), me):
                cp.wait_recv()
            fwd = copies(4 + j, (*chip, c), (x, y, 1 - c))
            for cp in fwd:
                cp.start()
            passed += fwd
        for cp in copies(0, (x, y, 1 - c), me):
            cp.wait_recv()
        for j, chip in enumerate(chips):
            for cp in copies(4 + j, (*chip, 1 - c), me):
                cp.wait_recv()
        for cp in first_copies(copies, me, chips) + passed:
            cp.wait_send()
        for cp in mine:
            cp.wait()

    return _Exchange(list(xs), [jax.ShapeDtypeStruct((N_DEV,) + a.shape, a.dtype) for a in xs],
                     [pltpu.SemaphoreType.DMA((7 * n,)), pltpu.SemaphoreType.DMA((7 * n,)),
                      pltpu.SemaphoreType.DMA((n,))], start, finish)


_CHIPS = ((0, 0), (0, 1), (1, 0), (1, 1))


def _plan_pair(sends):
    n = len(sends)

    def build(s_refs, o_refs, sems):
        send_sems, recv_sems = sems
        x, y, c = _xyc()
        return [pltpu.make_async_remote_copy(
            src_ref=s_refs[a].at[4 * px + 2 * py + 1 - c], dst_ref=o_refs[a].at[j],
            send_sem=send_sems.at[j * n + a], recv_sem=recv_sems.at[j * n + a], device_id=(x, y, 1 - c),
            device_id_type=MESH) for j, (px, py) in enumerate(_CHIPS) for a in range(n)]

    def start(s_refs, o_refs, sems):
        for cp in build(s_refs, o_refs, sems):
            cp.start()

    def finish(s_refs, o_refs, sems):
        for cp in build(s_refs, o_refs, sems):
            cp.wait_recv()
            cp.wait_send()

    return _Exchange(list(sends), [jax.ShapeDtypeStruct((4,) + a.shape[1:], a.dtype) for a in sends],
                     [pltpu.SemaphoreType.DMA((4 * n,)), pltpu.SemaphoreType.DMA((4 * n,))], start, finish)


def _plan_chips(ts):
    n = len(ts)
    flips = ((1, 0), (0, 1), (1, 1))

    def build(t_refs, o_refs, sems):
        send_sems, recv_sems, local_sems = sems
        x, y, c = _xyc()
        mine = 2 * x + y
        local = [pltpu.make_async_copy(t_refs[a].at[mine], o_refs[a].at[mine], local_sems.at[a]) for a in range(n)]
        remote = []
        for k, (fx, fy) in enumerate(flips):
            px = 1 - x if fx else x
            py = 1 - y if fy else y
            remote += [pltpu.make_async_remote_copy(
                src_ref=t_refs[a].at[2 * px + py], dst_ref=o_refs[a].at[mine],
                send_sem=send_sems.at[k * n + a], recv_sem=recv_sems.at[k * n + a], device_id=(px, py, c),
                device_id_type=MESH) for a in range(n)]
        return local, remote

    def start(t_refs, o_refs, sems):
        local, remote = build(t_refs, o_refs, sems)
        for cp in local + remote:
            cp.start()

    def finish(t_refs, o_refs, sems):
        local, remote = build(t_refs, o_refs, sems)
        for cp in remote:
            cp.wait_recv()
        for cp in remote:
            cp.wait_send()
        for cp in local:
            cp.wait()

    return _Exchange(list(ts), [jax.ShapeDtypeStruct(a.shape, a.dtype) for a in ts],
                     [pltpu.SemaphoreType.DMA((3 * n,)), pltpu.SemaphoreType.DMA((3 * n,)),
                      pltpu.SemaphoreType.DMA((n,))], start, finish)


def _combine(*plans):
    def parts(refs, attr):
        out, at = [], 0
        for p in plans:
            n = len(getattr(p, attr))
            out.append(refs[at:at + n])
            at += n
        return out

    def run(half):
        def go(ins, outs, sems):
            for p, a, o, s in zip(plans, parts(ins, "ins"), parts(outs, "outs"), parts(sems, "scratch")):
                getattr(p, half)(a, o, s)
        return go

    return _Exchange(sum((p.ins for p in plans), []), sum((p.outs for p in plans), []),
                     sum((p.scratch for p in plans), []), run("start"), run("finish"))


def _exchange_call(plan, name):
    n = len(plan.ins)

    def body(*refs):
        ins, outs, sems = refs[:n], refs[n:2 * n], refs[2 * n:]
        plan.start(ins, outs, sems)
        plan.finish(ins, outs, sems)

    return pl.pallas_call(
        body, name=name, out_shape=plan.outs,
        in_specs=[pl.BlockSpec(memory_space=pl.ANY)] * n, out_specs=[pl.BlockSpec(memory_space=pl.ANY)] * n,
        scratch_shapes=plan.scratch,
    )(*plan.ins)


def _pair_add(sends, fromsib, name):
    n = len(sends)
    nb = 8

    def body(*refs):
        c = lax.axis_index("c")
        for a in range(n):
            s_ref, f_ref, t_ref = refs[a], refs[n + a], refs[2 * n + a]
            for j in range(4):
                t_ref[j] = (s_ref[2 * j + c].astype(F32) + f_ref[j].astype(F32)).astype(t_ref.dtype)

    def spec(a, lead):
        return pl.BlockSpec((lead, a.shape[1] // nb, a.shape[2]), lambda i: (0, i, 0))

    return pl.pallas_call(
        body, name=name, grid=(nb,),
        in_specs=[spec(a, N_DEV) for a in sends] + [spec(a, 4) for a in fromsib],
        out_specs=[spec(a, 4) for a in fromsib],
        out_shape=[jax.ShapeDtypeStruct(a.shape, a.dtype) for a in fromsib],
        compiler_params=pltpu.CompilerParams(dimension_semantics=("arbitrary",), vmem_limit_bytes=VMEM_LIMIT),
    )(*sends, *fromsib)


def _adamw_vals(w, g, m, v):
    m2 = ADAM_B1 * m + (1.0 - ADAM_B1) * g
    v2 = ADAM_B2 * v + (1.0 - ADAM_B2) * (g * g)
    m_hat = m2 / (1.0 - ADAM_B1 ** ADAM_STEP)
    v_hat = v2 / (1.0 - ADAM_B2 ** ADAM_STEP)
    delta = -ADAM_LR * (m_hat / (jnp.sqrt(v_hat) + ADAM_EPS) + ADAM_WD * w)
    return delta, m2, v2


def _sum_adamw(recv, w, m, v, name):
    R, C = w.shape
    ns = recv.shape[0]
    br = next((t for t in (256, 128, 64, 32, 16) if R % t == 0), R)

    def body(r_ref, w_ref, m_ref, v_ref, g_ref, d_ref, m2_ref, v2_ref):
        g = r_ref[0].astype(F32)
        for d in range(1, ns):
            g = g + r_ref[d].astype(F32)
        dl, m2, v2 = _adamw_vals(w_ref[...], g, m_ref[...], v_ref[...])
        g_ref[...] = g
        d_ref[...] = dl
        m2_ref[...] = m2
        v2_ref[...] = v2

    spec = pl.BlockSpec((br, C), lambda i: (i, 0))
    return pl.pallas_call(
        body, name=name, grid=(R // br,),
        in_specs=[pl.BlockSpec((ns, br, C), lambda i: (0, i, 0)), spec, spec, spec], out_specs=[spec] * 4,
        out_shape=[jax.ShapeDtypeStruct((R, C), F32)] * 4,
        compiler_params=pltpu.CompilerParams(dimension_semantics=("arbitrary",)),
    )(recv, w, m, v)


def _updates_call(recvs, ws, ms, vs, name, host=None):
    n = len(recvs)
    nb = 8

    def body(*refs):
        for a in range(n):
            r_ref, w_ref, m_ref, v_ref = refs[a], refs[n + a], refs[2 * n + a], refs[3 * n + a]
            g_ref, d_ref, m2_ref, v2_ref = refs[4 * n + 4 * a:4 * n + 4 * a + 4]
            g = r_ref[0].astype(F32)
            for d in range(1, r_ref.shape[0]):
                g = g + r_ref[d].astype(F32)
            dl, m2, v2 = _adamw_vals(w_ref[...], g, m_ref[...], v_ref[...])
            g_ref[...] = g
            d_ref[...] = dl
            m2_ref[...] = m2
            v2_ref[...] = v2

    def spec2(w):
        return pl.BlockSpec((w.shape[0] // nb, w.shape[1]), lambda i: (i, 0))

    def spec3(r):
        return pl.BlockSpec((r.shape[0], r.shape[1] // nb, r.shape[2]), lambda i: (0, i, 0))

    res, hosted = _hosting_call(
        body, name, nb, host, list(recvs) + list(ws) + list(ms) + list(vs),
        [spec3(r) for r in recvs] + [spec2(w) for w in ws] * 3,
        [jax.ShapeDtypeStruct(w.shape, F32) for w in ws for _ in range(4)],
        [spec2(w) for w in ws for _ in range(4)], [])
    return [res[4 * a:4 * a + 4] for a in range(n)], hosted


def _small_sum(gath, loss_g, name):
    _, R, C = gath.shape
    br = R // 3

    def body(g_ref, l_ref, go_ref, lo_ref):
        g = g_ref[0].astype(F32)
        lsum = l_ref[0]
        for d in range(1, N_DEV):
            g = g + g_ref[d].astype(F32)
            lsum = lsum + l_ref[d]
        go_ref[...] = g
        lo_ref[...] = lsum

    return pl.pallas_call(
        body, name=name, grid=(R // br,),
        in_specs=[pl.BlockSpec((N_DEV, br, C), lambda i: (0, i, 0)),
                  pl.BlockSpec((N_DEV, 8, HD), lambda i: (0, 0, 0))],
        out_specs=[pl.BlockSpec((br, C), lambda i: (i, 0)), pl.BlockSpec((8, HD), lambda i: (0, 0))],
        out_shape=[jax.ShapeDtypeStruct((R, C), F32), jax.ShapeDtypeStruct((8, HD), F32)],
        compiler_params=pltpu.CompilerParams(dimension_semantics=("arbitrary",)),
    )(gath, loss_g)


def _adamw_multi(ws, gs, ms, vs, name, nblk=1):
    n = len(ws)

    def body(*refs):
        for a in range(n):
            dl, m2, v2 = _adamw_vals(refs[a][...], refs[n + a][...], refs[2 * n + a][...], refs[3 * n + a][...])
            refs[4 * n + 3 * a][...] = dl
            refs[4 * n + 3 * a + 1][...] = m2
            refs[4 * n + 3 * a + 2][...] = v2

    def spec(x):
        rest = (0,) * (x.ndim - 1)
        return pl.BlockSpec((x.shape[0] // nblk,) + tuple(x.shape[1:]), lambda i: (i,) + rest)

    res = pl.pallas_call(
        body, name=name, grid=(nblk,),
        in_specs=[spec(w) for w in ws] * 4, out_specs=[spec(w) for w in ws for _ in range(3)],
        out_shape=[jax.ShapeDtypeStruct(w.shape, F32) for w in ws for _ in range(3)],
        compiler_params=pltpu.CompilerParams(dimension_semantics=("arbitrary",), vmem_limit_bytes=VMEM_LIMIT),
    )(*ws, *gs, *ms, *vs)
    return [res[3 * a:3 * a + 3] for a in range(n)]


def _s5_param_fn(lr, li, ls, btr, bti):
    step = jnp.exp(ls)
    er = jnp.exp(lr * step)
    ang = li * step
    ar = er * jnp.cos(ang)
    ai = er * jnp.sin(ang)
    nr = ar - 1.0
    den = lr * lr + li * li
    fr = (nr * lr + ai * li) / den
    fi = (ai * lr - nr * li) / den
    return ar, ai, fr * btr - fi * bti, fr * bti + fi * btr


def _s5_params(lr, li, ls, btr, bti):
    def body(lr_ref, li_ref, ls_ref, br_ref, bi_ref, ar_ref, ai_ref, bbr_ref, bbi_ref):
        ar, ai, bbr, bbi = _s5_param_fn(lr_ref[...], li_ref[...], ls_ref[...], br_ref[...], bi_ref[...])
        ar_ref[...] = ar
        ai_ref[...] = ai
        bbr_ref[...] = bbr
        bbi_ref[...] = bbi

    sd = jax.ShapeDtypeStruct
    return pl.pallas_call(
        body, name="s5_params",
        out_shape=[sd(lr.shape, F32), sd(lr.shape, F32), sd(btr.shape, F32), sd(btr.shape, F32)],
    )(lr, li, ls, btr, bti)


def _s5_params_bwd(lr, li, ls, btr, bti, dar, dai, dbbr, dbbi):
    def body(lr_ref, li_ref, ls_ref, br_ref, bi_ref, dar_ref, dai_ref, dbbr_ref, dbbi_ref,
             dlr_ref, dli_ref, dls_ref, dbr_ref, dbi_ref):
        _, vjp = jax.vjp(_s5_param_fn, lr_ref[...], li_ref[...], ls_ref[...], br_ref[...], bi_ref[...])
        dlr, dli, dls, dbr, dbi = vjp((dar_ref[...], dai_ref[...], dbbr_ref[...], dbbi_ref[...]))
        dlr_ref[...] = dlr
        dli_ref[...] = dli
        dls_ref[...] = dls
        dbr_ref[...] = dbr
        dbi_ref[...] = dbi

    sd = jax.ShapeDtypeStruct
    return pl.pallas_call(
        body, name="s5_params_bwd",
        out_shape=[sd(lr.shape, F32), sd(lr.shape, F32), sd(ls.shape, F32), sd(btr.shape, F32), sd(btr.shape, F32)],
    )(lr, li, ls, btr, bti, dar, dai, dbbr, dbbi)


def _cpow(ar, ai, n):
    assert n & (n - 1) == 0
    while n > 1:
        ar, ai = ar * ar - ai * ai, 2.0 * ar * ai
        n //= 2
    return ar, ai


def _scan(st, cr, ci, init, nk, reverse, store, prev=None):
    W = S5_W

    def step(j, carry):
        k = nk - 1 - j if reverse else j
        rows = pl.ds(pl.multiple_of(k * 8, 8), 8)
        sr, si = carry[0], carry[1]
        nsr = cr * sr - ci * si + st[rows, 0:W]
        nsi = cr * si + ci * sr + st[rows, W:2 * W]
        if store:
            st[rows, 0:W] = nsr
            st[rows, W:2 * W] = nsi
        if prev is None:
            return nsr, nsi
        prows = pl.ds(pl.multiple_of(jnp.maximum(k - 1, 0) * 8, 8), 8)
        w = jnp.where(k > 0, 1.0, 0.0).astype(F32)
        pr = prev[prows, 0:W] * w
        pi = prev[prows, W:2 * W] * w
        return nsr, nsi, carry[2] + nsr * pr + nsi * pi, carry[3] + nsi * pr - nsr * pi

    return lax.fori_loop(0, nk, step, init, unroll=2)


def _chain(fin, fr, fi, pr, pi, reverse):
    W = S5_W
    fin[:, 0:W] = fr
    fin[:, W:2 * W] = fi
    rowid = lax.broadcasted_iota(jnp.int32, (8, W), 0)
    cr = jnp.zeros((1, W), F32)
    ci = jnp.zeros((1, W), F32)
    init_r = jnp.zeros((8, W), F32)
    init_i = jnp.zeros((8, W), F32)
    for s in (range(7, -1, -1) if reverse else range(8)):
        init_r = jnp.where(rowid == s, cr, init_r)
        init_i = jnp.where(rowid == s, ci, init_i)
        lr = fin[s:s + 1, 0:W]
        li = fin[s:s + 1, W:2 * W]
        cr, ci = lr + pr * cr - pi * ci, li + pr * ci + pi * cr
    return init_r, init_i


def _full_scan(st, fin, ar, ai, nk, reverse, prev=None, carry_in=None, carry_out=None):
    W = S5_W
    cr = jnp.broadcast_to(ar, (8, W))
    ci = jnp.broadcast_to(-ai if reverse else ai, (8, W))
    z = jnp.zeros((8, W), F32)
    if carry_in is None:
        fr, fi = _scan(st, cr, ci, (z, z), nk, reverse, store=False)
        pr, pi = _cpow(ar, -ai if reverse else ai, nk)
        init = _chain(fin, fr, fi, pr, pi, reverse)
    else:
        init = (carry_in[:, 0:W], carry_in[:, W:2 * W])
    if carry_out is not None:
        carry_out[:, 0:W] = init[0]
        carry_out[:, W:2 * W] = init[1]
    if prev is None:
        return _scan(st, cr, ci, init, nk, reverse, store=True)
    return _scan(st, cr, ci, init + (z, z), nk, reverse, store=True, prev=prev)


def _s5_specs(L):
    W2 = 2 * S5_W
    GC = S5_GB * S5_C
    col = pl.BlockSpec((L, GC), lambda g: (0, g))
    vec = pl.BlockSpec((1, GC), lambda g: (0, g))
    avec = pl.BlockSpec((1, S5_W), lambda g: (0, g))
    bmat = pl.BlockSpec((None, GC, W2), lambda g: (g, 0, 0))
    cmat = pl.BlockSpec((None, W2, GC), lambda g: (g, 0, 0))
    return col, vec, avec, bmat, cmat


def _interleave(dst, src, nk):
    for s in range(8):
        dst[pl.ds(s, nk, stride=8), :] = src[s * nk:(s + 1) * nk, :]


def _deinterleave(dst, src, nk):
    for s in range(8):
        dst[s * nk:(s + 1) * nk, :] = src[pl.ds(s, nk, stride=8), :].astype(dst.dtype)


def _hosting_call(body, name, nsteps, host, ins, in_specs, outs, out_specs, scratch):
    grid = (nsteps,) if isinstance(nsteps, int) else tuple(nsteps)
    params = pltpu.CompilerParams(dimension_semantics=("arbitrary",) * len(grid), vmem_limit_bytes=VMEM_LIMIT)
    if host is None:
        res = pl.pallas_call(
            body, name=name, grid=grid, in_specs=in_specs, out_specs=out_specs, out_shape=outs,
            scratch_shapes=scratch, compiler_params=params,
        )(*ins)
        return list(res), []
    n_in, n_out, n_sc = len(ins), len(outs), len(scratch)
    h_in, h_out = len(host.ins), len(host.outs)

    def hosted(*refs):
        a = refs[:n_in]
        ha = refs[n_in:n_in + h_in]
        o = refs[n_in + h_in:n_in + h_in + n_out]
        ho = refs[n_in + h_in + n_out:n_in + h_in + n_out + h_out]
        sc = refs[n_in + h_in + n_out + h_out:n_in + h_in + n_out + h_out + n_sc]
        hs = refs[n_in + h_in + n_out + h_out + n_sc:]
        first = functools.reduce(jnp.logical_and, [pl.program_id(i) == 0 for i in range(len(grid))])
        last = functools.reduce(jnp.logical_and, [pl.program_id(i) == g - 1 for i, g in enumerate(grid)])

        @pl.when(first)
        def _():
            host.start(ha, ho, hs)

        body(*a, *o, *sc)

        @pl.when(last)
        def _():
            host.finish(ha, ho, hs)

    hbm = pl.BlockSpec(memory_space=pl.ANY)
    res = pl.pallas_call(
        hosted, name=name, grid=grid,
        in_specs=list(in_specs) + [hbm] * h_in, out_specs=list(out_specs) + [hbm] * h_out,
        out_shape=list(outs) + list(host.outs), scratch_shapes=list(scratch) + list(host.scratch),
        compiler_params=params,
    )(*ins, *host.ins)
    return list(res[:n_out]), list(res[n_out:])


def _s5_fwd(u, bm, cm, ar, ai, dvec, host=None):
    L = u.shape[0]
    nk = L // 8
    GC = S5_GB * S5_C
    nb = S5_G // S5_GB
    col, vec, avec, bmat, cmat = _s5_specs(L)

    def body(u_ref, b_ref, c_ref, ar_ref, ai_ref, d_ref, y_ref, carry_ref, st, fin, ui, yi):
        _interleave(ui, u_ref, nk)
        for r in range(8):
            rows = slice(r * nk, (r + 1) * nk)
            st[rows, :] = _dot(ui[rows, :].astype(BF16), b_ref[...])
        _full_scan(st, fin, ar_ref[...], ai_ref[...], nk, reverse=False, carry_out=carry_ref)
        for r in range(8):
            rows = slice(r * nk, (r + 1) * nk)
            yi[rows, :] = _dot(st[rows, :].astype(BF16), c_ref[...]) + d_ref[...] * ui[rows, :]
        _deinterleave(y_ref, yi, nk)

    return _hosting_call(
        body, "s5_fwd", nb, host,
        [u, bm, cm, ar, ai, dvec], [col, bmat, cmat, avec, avec, vec],
        [jax.ShapeDtypeStruct(u.shape, F32), jax.ShapeDtypeStruct((nb * 8, 2 * S5_W), F32)],
        [col, pl.BlockSpec((8, 2 * S5_W), lambda g: (g, 0))],
        [pltpu.VMEM((L, 2 * S5_W), F32), pltpu.VMEM((8, 2 * S5_W), F32), pltpu.VMEM((L, GC), F32),
         pltpu.VMEM((L, GC), F32)])


def _s5_bwd(u, dy, carry, bm, bmt, cmt, ar, ai, dvec, mask, rmat, host=None):
    L = u.shape[0]
    nk = L // 8
    W = S5_W
    GC = S5_GB * S5_C
    col, vec, avec, bmat, cmat = _s5_specs(L)
    hi = lax.Precision.HIGHEST

    def body(u_ref, dy_ref, carry_ref, b_ref, bt_ref, ct_ref, ar_ref, ai_ref, d_ref, mask_ref, r_ref,
             du_ref, db_ref, dc_ref, dd_ref, dar_ref, dai_ref, sa, sb, fin, ui, dyi, dui):
        ar = ar_ref[...]
        ai = ai_ref[...]
        _interleave(ui, u_ref, nk)
        _interleave(dyi, dy_ref, nk)
        for r in range(8):
            rows = slice(r * nk, (r + 1) * nk)
            sa[rows, :] = _dot(ui[rows, :].astype(BF16), b_ref[...])
            sb[rows, :] = _dot(dyi[rows, :].astype(BF16), ct_ref[...])
        _full_scan(sa, fin, ar, ai, nk, reverse=False, carry_in=carry_ref)
        gr, gi, accr, acci = _full_scan(sb, fin, ar, ai, nk, reverse=True, prev=sa)
        rowid = lax.broadcasted_iota(jnp.int32, (8, W), 0)
        last = pl.ds((nk - 1) * 8, 8)
        pr = jnp.where(rowid == 0, 0.0, pltpu.roll(sa[last, 0:W], 1, 0))
        pi = jnp.where(rowid == 0, 0.0, pltpu.roll(sa[last, W:2 * W], 1, 0))
        accr = accr + gr * pr + gi * pi
        acci = acci + gi * pr - gr * pi
        dar_ref[...] = jnp.sum(accr, axis=0, keepdims=True)
        dai_ref[...] = jnp.sum(acci, axis=0, keepdims=True)
        dbf = jnp.zeros((GC, 2 * W), F32)
        dcf = jnp.zeros((GC, 2 * W), F32)
        dd = jnp.zeros((1, GC), F32)
        for r in range(8):
            rows = slice(r * nk, (r + 1) * nk)
            ub = ui[rows, :]
            dyb = dyi[rows, :]
            gb = sb[rows, :].astype(BF16)
            dui[rows, :] = _dot(gb, bt_ref[...]) + d_ref[...] * dyb
            dbf = dbf + _dot_tn(ub.astype(BF16), gb)
            dcf = dcf + _dot_tn(dyb.astype(BF16), sa[rows, :].astype(BF16))
            dd = dd + jnp.sum(dyb * ub, axis=0, keepdims=True)
        db_ref[...] = jnp.dot(dbf * mask_ref[...], r_ref[...], precision=hi, preferred_element_type=F32)
        dc_ref[...] = jnp.dot(dcf * mask_ref[...], r_ref[...], precision=hi, preferred_element_type=F32)
        dd_ref[...] = dd
        _deinterleave(du_ref, dui, nk)

    cmp_spec = pl.BlockSpec((GC, 2 * S5_P), lambda g: (g, 0))
    whole = lambda shape: pl.BlockSpec(shape, lambda g: (0, 0))
    sd = jax.ShapeDtypeStruct
    return _hosting_call(
        body, "s5_bwd", S5_G // S5_GB, host,
        [u, dy, carry, bm, bmt, cmt, ar, ai, dvec, mask, rmat],
        [col, col, pl.BlockSpec((8, 2 * W), lambda g: (g, 0)), bmat, cmat, bmat, avec, avec, vec, whole(mask.shape),
         whole(rmat.shape)],
        [sd(u.shape, BF16), sd((S5_G * S5_C, 2 * S5_P), F32), sd((S5_G * S5_C, 2 * S5_P), F32),
         sd((1, PRIM), F32), sd((1, S5_G * S5_P), F32), sd((1, S5_G * S5_P), F32)],
        [col, cmp_spec, cmp_spec, vec, avec, avec],
        [pltpu.VMEM((L, 2 * W), F32), pltpu.VMEM((L, 2 * W), F32), pltpu.VMEM((8, 2 * W), F32),
         pltpu.VMEM((L, GC), F32), pltpu.VMEM((L, GC), F32), pltpu.VMEM((L, GC), F32)])


def _s5_mats(bbr, bbi, cre, cim):
    nb = S5_G // S5_GB
    eye = jnp.eye(S5_GB, dtype=F32)
    bb = jnp.stack([bbr, bbi], axis=2).reshape(nb, S5_GB, S5_C, 2, S5_P)
    bm = jnp.einsum('ngcrp,gh->ngcrhp', bb, eye).reshape(nb, S5_GB * S5_C, 2 * S5_W)
    cc = jnp.stack([cre, -cim], axis=2).reshape(nb, S5_GB, S5_C, 2, S5_P)
    cmt = jnp.einsum('ngcrp,gh->ngcrhp', cc, eye).reshape(nb, S5_GB * S5_C, 2 * S5_W)
    return (bm.astype(BF16), jnp.swapaxes(bm, 1, 2).astype(BF16),
            jnp.swapaxes(cmt, 1, 2).astype(BF16), cmt.astype(BF16))


def _s5_compact_consts():
    g_row = np.arange(S5_GB * S5_C) // S5_C
    col = np.arange(2 * S5_W)
    g_col = (col % S5_W) // S5_P
    mask = (g_row[:, None] == g_col[None, :]).astype(np.float32)
    tgt = (col // S5_W) * S5_P + col % S5_P
    rmat = (tgt[:, None] == np.arange(2 * S5_P)[None, :]).astype(np.float32)
    return jnp.asarray(mask), jnp.asarray(rmat)


def _attn_scores(q_ref, k_ref, qb, bq, scale):
    ext = (qb + 1) * bq
    s = _dot_nt(q_ref[qb * bq:ext, :], k_ref[0:ext, :]) * scale
    qpos = lax.broadcasted_iota(jnp.int32, (bq, bq), 0)
    kpos = lax.broadcasted_iota(jnp.int32, (bq, bq), 1)
    diag = jnp.where(kpos <= qpos, s[:, ext - bq:], NEG)
    return diag if qb == 0 else jnp.concatenate([s[:, :ext - bq], diag], axis=-1)


def _attn_fwd(qp, kp, v, scale):
    L = qp.shape[0]
    bq = min(256, L)

    def body(q_ref, k_ref, v_ref, o_ref, lse_ref):
        for qb in range(L // bq):
            rows = slice(qb * bq, (qb + 1) * bq)
            s = _attn_scores(q_ref, k_ref, qb, bq, scale)
            m = jnp.max(s, axis=-1, keepdims=True)
            e = jnp.exp(s - m)
            l = jnp.sum(e, axis=-1, keepdims=True)
            o_ref[rows, :] = _dot(e.astype(BF16), v_ref[0:(qb + 1) * bq, :]) / l
            lse_ref[rows, :] = jnp.broadcast_to(m + jnp.log(l), (bq, HD))

    blk = pl.BlockSpec((L, HD), lambda h: (0, h))
    wide = pl.BlockSpec((L, 2 * HD), lambda h: (0, h))
    return pl.pallas_call(
        body, name="mla_attn_fwd", grid=(MLA_H,),
        in_specs=[wide, wide, blk], out_specs=[blk, blk],
        out_shape=[jax.ShapeDtypeStruct((L, MLA_H * HD), F32)] * 2,
        compiler_params=pltpu.CompilerParams(dimension_semantics=("arbitrary",), vmem_limit_bytes=VMEM_LIMIT),
    )(qp, kp, v)


def _attn_bwd(qp, kp, v, o, lse, do, scale):
    L = qp.shape[0]
    bq = min(256, L)
    nq = L // bq

    def body(q_ref, k_ref, v_ref, o_ref, lse_ref, do_ref, dq_ref, dk_ref, dv_ref, dk_acc, dv_acc):
        dk_acc[...] = jnp.zeros_like(dk_acc)
        dv_acc[...] = jnp.zeros_like(dv_acc)
        for qb in range(nq):
            rows = slice(qb * bq, (qb + 1) * bq)
            ext = (qb + 1) * bq
            do = do_ref[rows, :]
            dob = do.astype(BF16)
            p = jnp.exp(_attn_scores(q_ref, k_ref, qb, bq, scale) - lse_ref[rows, 0:1])
            dp = _dot_nt(dob, v_ref[0:ext, :])
            dsum = jnp.sum(do * o_ref[rows, :], axis=-1, keepdims=True)
            ds = (p * (dp - dsum) * scale).astype(BF16)
            dq_ref[rows, :] = _dot(ds, k_ref[0:ext, :]).astype(dq_ref.dtype)
            dk_acc[0:ext, :] += _dot_tn(ds, q_ref[rows, :])
            dv_acc[0:ext, :] += _dot_tn(p.astype(BF16), dob)
        dk_ref[...] = dk_acc[...].astype(dk_ref.dtype)
        dv_ref[...] = dv_acc[...].astype(dv_ref.dtype)

    sd = jax.ShapeDtypeStruct
    blk = pl.BlockSpec((L, HD), lambda h: (0, h))
    wide = pl.BlockSpec((L, 2 * HD), lambda h: (0, h))
    return pl.pallas_call(
        body, name="mla_attn_bwd", grid=(MLA_H,),
        in_specs=[wide, wide, blk, blk, blk, blk], out_specs=[wide, wide, blk],
        out_shape=[sd((L, MLA_H * 2 * HD), BF16), sd((L, MLA_H * 2 * HD), BF16), sd((L, MLA_H * HD), BF16)],
        scratch_shapes=[pltpu.VMEM((L, 2 * HD), F32), pltpu.VMEM((L, HD), F32)],
        compiler_params=pltpu.CompilerParams(dimension_semantics=("arbitrary",), vmem_limit_bytes=VMEM_LIMIT),
    )(qp, kp, v, o, lse, do)


def _kv_fn(mem, gm, w, gk):
    kv = _mm(_rms(mem, gm, D_MODEL), w)
    k = jnp.concatenate([_rms(kv[:, HD * h:HD * (h + 1)], gk, HD) for h in range(X_HEADS)], axis=-1)
    return k, kv[:, XQ:]


def _kv_prep(mem, gm, w, gk, name):
    def fn(mem, gm, w, gk):
        return _kv_fn(mem, gm, w, gk)
    M = mem.shape[0]
    return _rowwise(name, fn, [('c', mem), ('c', gm), ('c', w), ('c', gk)],
                    [('c', (M, XQ), F32), ('c', (M, XQ), F32)], 1)


def _kv_prep_bwd(mem, gm, w, gk, dk, dv, name):
    def fn(mem, gm, w, gk, dk, dv):
        _, vjp = jax.vjp(lambda a, b, c: _kv_fn(mem, a, b, c), gm, w, gk)
        return vjp((dk, dv))
    return _rowwise(name, fn, [('c', mem), ('c', gm), ('c', w), ('c', gk), ('c', dk), ('c', dv)],
                    [('c', gm.shape, F32), ('c', w.shape, BF16), ('c', gk.shape, F32)], 1)


def _forward_merge(x, mix, mix_kind, xq, gate, k, v, gq, wout, name, nblk, sub, host=None):
    def fn(x, mix, xq, gate, k, v, gq, wout):
        o = _merge(mix, xq, gate, k, v, gq)
        return (x + _dot(o.astype(BF16), wout),)
    L = x.shape[0]
    out = _rowwise(name, fn, [('r', x), (mix_kind, mix), ('r', xq), ('r', gate), ('c', k), ('c', v), ('c', gq),
                              ('c', wout)], [('r', (L, D_MODEL), F32)], nblk, sub, host=host)
    return out[0] if host is None else (out[0][0], out[1])


def _backward_merge(dx, mix, mix_kind, xq, gate, k, v, gq, wout, name, nblk, sub, host=None):
    def fn(dx, mix, xq, gate, k, v, gq, wout):
        g16 = dx.astype(BF16)
        do = _dot_nt(g16, wout)
        o, vjp = jax.vjp(_merge, mix, xq, gate, k, v, gq)
        dmix, dxq, dgate, dk, dv, dgq = vjp(do)
        return dmix, dxq, dgate, o, g16, dk, dv, dgq
    L = dx.shape[0]
    return _rowwise(
        name, fn,
        [('r', dx), (mix_kind, mix), ('r', xq), ('r', gate), ('c', k), ('c', v), ('c', gq), ('c', wout)],
        [('r', (L, PRIM), F32), ('r', (L, XQ), BF16), ('r', (L, BRANCH), BF16), ('t', (BRANCH, L), BF16),
         ('r', (L, D_MODEL), BF16), ('a', k.shape, F32), ('a', v.shape, F32), ('a', gq.shape, F32)], nblk, sub,
        host=host)


_MLA_IN = 3392
_MLA_IN_PAD = 3456


def _from_slots(g):
    _, k, n = g.shape
    return jnp.transpose(g, (1, 0, 2)).reshape(k, N_DEV * n)


def _to_slots(w):
    k = w.shape[0]
    return jnp.transpose(w.reshape(k, N_DEV, -1), (1, 0, 2))


def _uq_to_kernel(g):
    uq = _from_slots(g).reshape(Q_LORA, MLA_H, HD + ROPE)
    return jnp.concatenate([uq[:, :, :HD].reshape(Q_LORA, PRIM),
                            jnp.pad(uq[:, :, HD:], ((0, 0), (0, 0), (0, HD - ROPE))).reshape(Q_LORA, PRIM)], axis=1)


def _uq_from_kernel(d_w_q):
    uq = jnp.concatenate([d_w_q[:, :PRIM].reshape(Q_LORA, MLA_H, HD),
                          d_w_q[:, PRIM:].reshape(Q_LORA, MLA_H, HD)[:, :, :ROPE]], axis=2)
    return _to_slots(uq.reshape(Q_LORA, MLA_H * (HD + ROPE)))


def _mla_in_perm(w):
    return jnp.concatenate([w[:, :768], w[:, 832:], w[:, 768:832], jnp.zeros((w.shape[0], 64), w.dtype)], axis=1)


def _mla_in_unperm(w):
    return jnp.concatenate([w[:, :768], w[:, 3328:3392], w[:, 768:3328]], axis=1)


_SMALL = (("ln_gain", 2048), ("mem_norm", 2048), ("xq_norm", 256), ("xk_norm", 256), ("s5_lambda_re", 6144),
          ("s5_lambda_im", 6144), ("s5_log_step", 96), ("s5_b_re", 98304), ("s5_b_im", 98304), ("s5_c_re", 98304),
          ("s5_c_im", 98304), ("s5_d", 1536), ("mla_q_lora_norm", 512), ("mla_kv_lora_norm", 256),
          ("mla_q_nope_norm", 128), ("mla_k_nope_norm", 128), ("mla_q_rope_norm", 64), ("mla_k_rope_norm", 64))
_SMALL_ROWS = 432
_SMALL_OFF = {name: sum(n for _, n in _SMALL[:i]) for i, (name, _) in enumerate(_SMALL)}


def _pack_small(d):
    flat = jnp.concatenate([d[n].reshape(-1).astype(F32) for n, _ in _SMALL])
    return jnp.pad(flat, (0, _SMALL_ROWS * 1024 - flat.shape[0])).reshape(_SMALL_ROWS, 1024)


def _unpack_small(p, name, shape):
    off = _SMALL_OFF[name]
    return p.reshape(-1)[off:off + int(np.prod(shape))].reshape(shape)


_WEIGHTS = ('ln_gain', 'w_out', 'mem_norm', 'w_mem_kv', 'xq_norm', 'xk_norm', 's5_w_in', 's5_lambda_re',
            's5_lambda_im', 's5_log_step', 's5_b_re', 's5_b_im', 's5_c_re', 's5_c_im', 's5_d', 's5_w_glu', 'mla_w_in',
            'mla_q_lora_norm', 'mla_kv_lora_norm', 'mla_w_uq', 'mla_w_ukv', 'mla_q_nope_norm', 'mla_k_nope_norm',
            'mla_q_rope_norm', 'mla_k_rope_norm')
_BIG = ('w_out', 'w_mem_kv', 's5_w_in', 's5_w_glu', 'mla_w_in', 'mla_w_uq', 'mla_w_ukv')


def _pad128(g):
    return jnp.pad(g.reshape(1, -1), ((0, 0), (0, HD - g.shape[-1])))


def kernel(x, mem, positions, ln_gain, w_out, mem_norm, w_mem_kv, xq_norm, xk_norm, s5_w_in, s5_lambda_re, s5_lambda_im, s5_log_step, s5_b_re, s5_b_im, s5_c_re, s5_c_im, s5_d, s5_w_glu, mla_w_in, mla_q_lora_norm, mla_kv_lora_norm, mla_w_uq, mla_w_ukv, mla_q_nope_norm, mla_k_nope_norm, mla_q_rope_norm, mla_k_rope_norm, loss_target, m_ln_gain, m_w_out, m_mem_norm, m_w_mem_kv, m_xq_norm, m_xk_norm, m_s5_w_in, m_s5_lambda_re, m_s5_lambda_im, m_s5_log_step, m_s5_b_re, m_s5_b_im, m_s5_c_re, m_s5_c_im, m_s5_d, m_s5_w_glu, m_mla_w_in, m_mla_q_lora_norm, m_mla_kv_lora_norm, m_mla_w_uq, m_mla_w_ukv, m_mla_q_nope_norm, m_mla_k_nope_norm, m_mla_q_rope_norm, m_mla_k_rope_norm, v_ln_gain, v_w_out, v_mem_norm, v_w_mem_kv, v_xq_norm, v_xk_norm, v_s5_w_in, v_s5_lambda_re, v_s5_lambda_im, v_s5_log_step, v_s5_b_re, v_s5_b_im, v_s5_c_re, v_s5_c_im, v_s5_d, v_s5_w_glu, v_mla_w_in, v_mla_q_lora_norm, v_mla_kv_lora_norm, v_mla_w_uq, v_mla_w_ukv, v_mla_q_nope_norm, v_mla_k_nope_norm, v_mla_q_rope_norm, v_mla_k_rope_norm):
    weights = dict(ln_gain=ln_gain, w_out=w_out, mem_norm=mem_norm, w_mem_kv=w_mem_kv, xq_norm=xq_norm,
                   xk_norm=xk_norm, s5_w_in=s5_w_in, s5_lambda_re=s5_lambda_re, s5_lambda_im=s5_lambda_im,
                   s5_log_step=s5_log_step, s5_b_re=s5_b_re, s5_b_im=s5_b_im, s5_c_re=s5_c_re, s5_c_im=s5_c_im,
                   s5_d=s5_d, s5_w_glu=s5_w_glu, mla_w_in=mla_w_in, mla_q_lora_norm=mla_q_lora_norm,
                   mla_kv_lora_norm=mla_kv_lora_norm, mla_w_uq=mla_w_uq, mla_w_ukv=mla_w_ukv,
                   mla_q_nope_norm=mla_q_nope_norm, mla_k_nope_norm=mla_k_nope_norm,
                   mla_q_rope_norm=mla_q_rope_norm, mla_k_rope_norm=mla_k_rope_norm)
    m_in = dict(zip(_WEIGHTS, (m_ln_gain, m_w_out, m_mem_norm, m_w_mem_kv, m_xq_norm, m_xk_norm, m_s5_w_in,
                               m_s5_lambda_re, m_s5_lambda_im, m_s5_log_step, m_s5_b_re, m_s5_b_im, m_s5_c_re,
                               m_s5_c_im, m_s5_d, m_s5_w_glu, m_mla_w_in, m_mla_q_lora_norm, m_mla_kv_lora_norm,
                               m_mla_w_uq, m_mla_w_ukv, m_mla_q_nope_norm, m_mla_k_nope_norm, m_mla_q_rope_norm,
                               m_mla_k_rope_norm)))
    v_in = dict(zip(_WEIGHTS, (v_ln_gain, v_w_out, v_mem_norm, v_w_mem_kv, v_xq_norm, v_xk_norm, v_s5_w_in,
                               v_s5_lambda_re, v_s5_lambda_im, v_s5_log_step, v_s5_b_re, v_s5_b_im, v_s5_c_re,
                               v_s5_c_im, v_s5_d, v_s5_w_glu, v_mla_w_in, v_mla_q_lora_norm, v_mla_kv_lora_norm,
                               v_mla_w_uq, v_mla_w_ukv, v_mla_q_nope_norm, v_mla_k_nope_norm, v_mla_q_rope_norm,
                               v_mla_k_rope_norm)))

    x0 = x[0]
    mem0 = mem[0]
    target = loss_target[0]
    L = x0.shape[0]
    nblk, sub = 8, 1
    me = 4 * lax.axis_index("x") + 2 * lax.axis_index("y") + lax.axis_index("c")

    lora = jnp.pad(jnp.concatenate([mla_q_lora_norm, mla_kv_lora_norm], axis=1), ((0, 7), (0, HD - 96)))
    def gather(*shards):
        return _plan_all_gather([s.astype(BF16) for s in shards])

    (W_in_s5,) = _exchange_call(gather(s5_w_in[0]), "ag_s5_w_in")

    ln0, ln1 = ln_gain[0:1], ln_gain[1:2]
    gq0, gq1 = xq_norm[0:1], xq_norm[1:2]
    gk0, gk1 = xk_norm[0:1], xk_norm[1:2]
    gm0, gm1 = mem_norm[0:1], mem_norm[1:2]
    gqn, gkn = mla_q_nope_norm, mla_k_nope_norm
    gqr, gkr = _pad128(mla_q_rope_norm), _pad128(mla_k_rope_norm)

    lr3 = s5_lambda_re.reshape(S5_G, 1, S5_P)
    li3 = s5_lambda_im.reshape(S5_G, 1, S5_P)
    ls3 = s5_log_step.reshape(S5_G, 1, 1)
    btr = jnp.swapaxes(s5_b_re[0], 1, 2)
    bti = jnp.swapaxes(s5_b_im[0], 1, 2)
    a_r, a_i, bbr, bbi = _s5_params(lr3, li3, ls3, btr, bti)
    bm, bmt, cm, cmt = _s5_mats(bbr, bbi, s5_c_re[0], s5_c_im[0])
    a_r2 = a_r.reshape(1, S5_G * S5_P)
    a_i2 = a_i.reshape(1, S5_G * S5_P)
    cmask, rmat = _s5_compact_consts()

    half = ROPE // 2
    inv_freq = ROPE_THETA ** (-jnp.arange(half, dtype=F32) / half)
    invf = jnp.concatenate([inv_freq, inv_freq, jnp.zeros((HD - ROPE,), F32)]).reshape(1, HD)

    def rot_tables(pos, invf):
        ang = pos.astype(F32) * invf
        lane = lax.broadcasted_iota(jnp.int32, ang.shape, 1)
        c = jnp.where(lane < ROPE, jnp.cos(ang), 0.0)
        s = jnp.sin(ang)
        return c, jnp.where(lane < half, -s, 0.0), jnp.where((lane >= half) & (lane < ROPE), s, 0.0)

    tc, ts1, ts2 = _rowwise("rot_tables", rot_tables, [('r', positions.reshape(L, 1)), ('c', invf)],
                            [('r', (L, HD), F32)] * 3, nblk, sub)

    def in_s5(x, g, w):
        proj = _mm_slots(_rms(x, g, D_MODEL).astype(BF16), w)
        return proj[:, :PRIM], proj[:, PRIM:PRIM + XQ], proj[:, PRIM + XQ:]

    kh = D_MODEL // 2
    (u_s5, xq_a, gate_a), (G_mkv0,) = _rowwise(
        "s5_in", in_s5, [('r', x0), ('c', ln0), ('c', W_in_s5)],
        [('r', (L, PRIM), F32), ('r', (L, XQ), F32), ('r', (L, BRANCH), F32)], nblk, sub, host=gather(w_mem_kv[0]))
    (y_s5, s5_carry), (W_glu, G_in_mla_a) = _s5_fwd(u_s5, bm, cm, a_r2, a_i2, s5_d,
                                                    host=gather(s5_w_glu[0], mla_w_in[0, :kh]))

    def glu(y, w):
        z = _mm_slots(_gelu(y).astype(BF16), w)
        return (z[:, :PRIM] * _sigmoid(z[:, PRIM:]),)

    (y2,), (G_out0,) = _rowwise("s5_glu", glu, [('r', y_s5), ('c', W_glu)], [('r', (L, PRIM), F32)], nblk, sub,
                                host=gather(w_out[0]))
    W_mkv0 = G_mkv0.reshape(D_MODEL, 2 * XQ)
    k_a, v_a = _kv_prep(mem0, gm0, W_mkv0, gk0, "kv_prep0")
    x1, (G_in_mla_b,) = _forward_merge(
        x0, y2, 'r', xq_a, gate_a, k_a, v_a, gq0, G_out0.reshape(BRANCH, D_MODEL), "merge0", nblk, sub,
        host=gather(mla_w_in[0, kh:]))
    W_in_mla = _mla_in_perm(jnp.concatenate([_from_slots(G_in_mla_a), _from_slots(G_in_mla_b)], axis=0))

    def in_mla(x, g, w):
        proj = _dot(_rms(x, g, D_MODEL).astype(BF16), w)
        return proj[:, :512], proj[:, 512:768], proj[:, 768:1280], proj[:, 1280:3328], proj[:, 3328:]

    (c_q, c_kv, xq_b, gate_b, krp), (G_uq, W_kv, G_lora) = _rowwise(
        "mla_in", in_mla, [('r', x1), ('c', ln1), ('c', W_in_mla)],
        [('r', (L, Q_LORA), F32), ('r', (L, KV_LORA), F32), ('r', (L, XQ), F32), ('r', (L, BRANCH), F32),
         ('r', (L, HD), F32)], nblk, sub,
        host=_plan_all_gather([mla_w_uq[0].astype(BF16), mla_w_ukv[0].astype(BF16), lora]))
    W_q = _uq_to_kernel(G_uq)
    g_qlora = G_lora[:, 0, :64].reshape(1, Q_LORA)
    g_kvlora = G_lora[:, 0, 64:96].reshape(1, KV_LORA)

    def qkv(c_q, c_kv, krp, tc, ts1, ts2, gql, gkvl, wq, wkv, gqn, gkn, gqr, gkr):
        q = _dot(_rms(c_q, gql, Q_LORA).astype(BF16), wq)
        kv = _mm_slots(_rms(c_kv, gkvl, KV_LORA).astype(BF16), wkv)
        kp, v = _kv_post(kv, krp, gkn, gkr, tc, ts1, ts2)
        return _q_post(q, gqn, gqr, tc, ts1, ts2), kp, v

    qkv_consts = [('c', g_qlora), ('c', g_kvlora), ('c', W_q), ('c', W_kv), ('c', gqn), ('c', gkn), ('c', gqr),
                  ('c', gkr)]
    (q_pad, k_pad, v_h), (G_mkv1, G_out1) = _rowwise(
        "mla_qkv", qkv, [('r', c_q), ('r', c_kv), ('r', krp), ('r', tc), ('r', ts1), ('r', ts2)] + qkv_consts,
        [('r', (L, 2 * PRIM), BF16), ('r', (L, 2 * PRIM), BF16), ('r', (L, PRIM), BF16)], nblk, sub,
        host=gather(w_mem_kv[1], w_out[1]))
    W_out = (G_out0.reshape(BRANCH, D_MODEL), G_out1.reshape(BRANCH, D_MODEL))
    W_mkv = (W_mkv0, G_mkv1.reshape(D_MODEL, 2 * XQ))
    scale = (HD + ROPE) ** -0.5
    attn, lse = _attn_fwd(q_pad, k_pad, v_h, scale)
    k_b, v_b = _kv_prep(mem0, gm1, W_mkv[1], gk1, "kv_prep1")

    def merge_loss(x, mix, xq, gate, k, v, gq, wout, t):
        err = x + _dot(_merge(mix, xq, gate, k, v, gq).astype(BF16), wout) - t
        part = 0.5 * jnp.sum(jnp.sum(err * err, axis=-1, keepdims=True) * (1.0 / D_MODEL), axis=0, keepdims=True)
        return err * (1.0 / D_MODEL), jnp.broadcast_to(part, (1, HD))

    dx2, loss_part = _rowwise(
        "merge1_loss", merge_loss,
        [('r', x1), ('r', attn), ('r', xq_b), ('r', gate_b), ('c', k_b), ('c', v_b), ('c', gq1), ('c', W_out[1]),
         ('r', target)], [('r', (L, D_MODEL), F32), ('a', (1, HD), F32)], nblk, sub)

    dattn, dxq_b, dgate_b, o_b, g_b, dk_b, dv_b, dgq1 = _backward_merge(
        dx2, attn, 'r', xq_b, gate_b, k_b, v_b, gq1, W_out[1], "merge1_bwd", nblk, sub)
    dgm1, dW_mkv1, dgk1 = _kv_prep_bwd(mem0, gm1, W_mkv[1], gk1, dk_b, dv_b, "kv_prep1_bwd")
    dW_out1 = _matmul_tn(o_b, g_b, "dw_out1")
    dq_pad, dk_pad, dv_h = _attn_bwd(q_pad, k_pad, v_h, attn, lse, dattn, scale)

    def qkv_bwd(c_q, c_kv, krp, tc, ts1, ts2, dqp, dkp, dv, gql, gkvl, wq, wkv, gqn, gkn, gqr, gkr):
        cqn, vjp_qn = jax.vjp(lambda a, b: _rms(a, b, Q_LORA), c_q, gql)
        ckvn, vjp_kvn = jax.vjp(lambda a, b: _rms(a, b, KV_LORA), c_kv, gkvl)
        cqn16 = cqn.astype(BF16)
        ckvn16 = ckvn.astype(BF16)
        q = _dot(cqn16, wq)
        kv = _mm_slots(ckvn16, wkv)
        _, vjp_q = jax.vjp(lambda a, b, c: _q_post(a, b, c, tc, ts1, ts2), q, gqn, gqr)
        dq, dgqn, dgqr = vjp_q(dqp.astype(F32))
        _, vjp_kv = jax.vjp(lambda a, b, c, d: _kv_post(a, b, c, d, tc, ts1, ts2), kv, krp, gkn, gkr)
        dkv, dkrp, dgkn, dgkr = vjp_kv((dkp.astype(F32), dv.astype(F32)))
        dq16 = dq.astype(BF16)
        dkv16 = dkv.astype(BF16)
        dc_q, dgql = vjp_qn(_dot_nt(dq16, wq))
        dc_kv, dgkvl = vjp_kvn(_mm_slots_nt(dkv16, wkv))
        return dc_q, dc_kv, dkrp, cqn16, dq16, ckvn16, dkv16, dgql, dgkvl, dgqn, dgkn, dgqr, dgkr

    (dc_q, dc_kv, dkrp, cqn16, dq16, ckvn16, dkv16, dgql, dgkvl, dgqn, dgkn, dgqr, dgkr) = _rowwise(
        "mla_qkv_bwd", qkv_bwd,
        [('r', c_q), ('r', c_kv), ('r', krp), ('r', tc), ('r', ts1), ('r', ts2), ('r', dq_pad), ('r', dk_pad),
         ('r', dv_h)] + qkv_consts,
        [('r', (L, Q_LORA), BF16), ('r', (L, KV_LORA), BF16), ('r', (L, HD), BF16), ('t', (Q_LORA, L), BF16),
         ('r', (L, 2 * PRIM), BF16), ('t', (KV_LORA, L), BF16), ('r', (L, 2 * PRIM), BF16),
         ('a', (1, Q_LORA), F32), ('a', (1, KV_LORA), F32), ('a', (1, HD), F32), ('a', (1, HD), F32),
         ('a', (1, HD), F32), ('a', (1, HD), F32)], nblk, sub)
    dW_q = _matmul_tn(cqn16, dq16, "dw_uq")
    dW_kv = _matmul_tn_slots(ckvn16, dkv16, "dw_ukv")

    def in_bwd(x, dres, g, w, *dparts):
        dproj = jnp.concatenate(dparts, axis=-1).astype(BF16)
        xn, vjp = jax.vjp(lambda a, b: _rms(a, b, D_MODEL), x, g)
        dx, dg = vjp(_mm_slots_nt(dproj, w) if w.ndim == 3 else _dot_nt(dproj, w))
        return dx + dres, xn, dproj, dg

    dx1, xn1, dproj1, dln1 = _rowwise(
        "mla_in_bwd", in_bwd,
        [('r', x1), ('r', dx2), ('c', ln1), ('c', W_in_mla), ('r', dc_q), ('r', dc_kv), ('r', dxq_b), ('r', dgate_b),
         ('r', dkrp)],
        [('r', (L, D_MODEL), F32), ('t', (D_MODEL, L), BF16), ('r', (L, _MLA_IN_PAD), BF16), ('a', (1, D_MODEL), F32)],
        nblk, sub)
    dW_in_mla = _matmul_tn(xn1, dproj1, "dw_mla_in")

    grads1 = [dW_out1.reshape(N_DEV, 256, D_MODEL), dW_mkv1.reshape(N_DEV, 128, 2 * XQ),
              _to_slots(_mla_in_unperm(dW_in_mla)), _uq_from_kernel(dW_q), dW_kv]
    (dy2, dxq_a, dgate_a, o_a, g_a, dk_a, dv_a, dgq0), pair1 = _backward_merge(
        dx1, y2, 'r', xq_a, gate_a, k_a, v_a, gq0, W_out[0], "merge0_bwd", nblk, sub, host=_plan_pair(grads1))
    dgm0, dW_mkv0, dgk0 = _kv_prep_bwd(mem0, gm0, W_mkv[0], gk0, dk_a, dv_a, "kv_prep0_bwd")
    dW_out0 = _matmul_tn(o_a, g_a, "dw_out0")
    t1 = list(_pair_add(grads1, pair1, "rs_add_layer1"))

    def glu_bwd(y, dy2, w):
        h, vjp_h = jax.vjp(_gelu, y)
        h16 = h.astype(BF16)
        z = _mm_slots(h16, w)
        _, vjp_z = jax.vjp(lambda z: z[:, :PRIM] * _sigmoid(z[:, PRIM:]), z)
        dz16 = vjp_z(dy2)[0].astype(BF16)
        return vjp_h(_mm_slots_nt(dz16, w))[0], h16, dz16

    grads0 = [dW_out0.reshape(N_DEV, 256, D_MODEL), dW_mkv0.reshape(N_DEV, 128, 2 * XQ)]
    (dy_s5, h16, dz16), glu_hosted = _rowwise(
        "s5_glu_bwd", glu_bwd, [('r', y_s5), ('r', dy2), ('c', W_glu)],
        [('r', (L, PRIM), F32), ('t', (PRIM, L), BF16), ('r', (L, 2 * PRIM), BF16)], nblk, sub,
        host=_combine(_plan_chips(t1[2:]), _plan_pair(grads0)))
    recv_proj1, pair0 = glu_hosted[:3], glu_hosted[3:]
    dW_glu = _matmul_tn_slots(h16, dz16, "dw_glu")
    t0 = list(_pair_add(grads0 + [dW_glu], pair0 + list(_exchange_call(_plan_pair([dW_glu]), "rs_pair_glu")),
                        "rs_add_layer0"))
    (du_s5, dbc, dcc, dd, dar, dai), recv_rest = _s5_bwd(u_s5, dy_s5, s5_carry, bm, bmt, cmt, a_r2, a_i2, s5_d,
                                                        cmask, rmat, host=_plan_chips(t1[:2] + t0))
    early_recv = recv_rest[:2] + recv_proj1 + recv_rest[2:]
    dx0, xn0, dproj0, dln0 = _rowwise(
        "s5_in_bwd", in_bwd,
        [('r', x0), ('r', dx1), ('c', ln0), ('c', W_in_s5), ('r', du_s5), ('r', dxq_a),
         ('r', dgate_a)],
        [('r', (L, D_MODEL), F32), ('t', (D_MODEL, L), BF16), ('r', (L, 2 * BRANCH), BF16), ('a', (1, D_MODEL), F32)],
        nblk, sub)

    dbc4 = dbc.reshape(S5_G, S5_C, 2, S5_P)
    dcc4 = dcc.reshape(S5_G, S5_C, 2, S5_P)
    dlr, dli, dls, dbtr, dbti = _s5_params_bwd(
        lr3, li3, ls3, btr, bti, dar.reshape(S5_G, 1, S5_P), dai.reshape(S5_G, 1, S5_P), dbc4[:, :, 0], dbc4[:, :, 1])

    small_part = {
        "ln_gain": jnp.concatenate([dln0, dln1]), "mem_norm": jnp.concatenate([dgm0, dgm1]),
        "xq_norm": jnp.concatenate([dgq0, dgq1]), "xk_norm": jnp.concatenate([dgk0, dgk1]),
        "s5_lambda_re": dlr, "s5_lambda_im": dli, "s5_log_step": dls,
        "s5_b_re": jnp.swapaxes(dbtr, 1, 2), "s5_b_im": jnp.swapaxes(dbti, 1, 2),
        "s5_c_re": dcc4[:, :, 0], "s5_c_im": -dcc4[:, :, 1], "s5_d": dd,
        "mla_q_lora_norm": dgql, "mla_kv_lora_norm": dgkvl, "mla_q_nope_norm": dgqn, "mla_k_nope_norm": dgkn,
        "mla_q_rope_norm": dgqr[:, :ROPE], "mla_k_rope_norm": dgkr[:, :ROPE],
    }
    loss8 = jnp.pad(loss_part, ((0, 7), (0, 0)))
    dW_in_s5, (small_gath, loss_g) = _matmul_tn_slots(
        xn0, dproj0, "dw_s5_in", host=_plan_all_gather([_pack_small(small_part).astype(BF16), loss8]))

    late = [dW_in_s5]
    late_t = _pair_add(late, list(_exchange_call(_plan_pair(late), "rs_pair_late")), "rs_add_late")
    owners = [("w_out", 1), ("w_mem_kv", 1), ("mla_w_in", 0), ("mla_w_uq", 0), ("mla_w_ukv", 0), ("w_out", 0),
              ("w_mem_kv", 0), ("s5_w_glu", 0)]
    upd, late_recv = _updates_call(early_recv, [weights[n][i] for n, i in owners], [m_in[n][i] for n, i in owners],
                                   [v_in[n][i] for n, i in owners], "update_early", host=_plan_chips(late_t))
    owners.append(("s5_w_in", 0))
    upd.append(_sum_adamw(late_recv[0], s5_w_in[0], m_s5_w_in[0], v_s5_w_in[0], "update_s5_w_in"))
    grads, delta, new_m, new_v = {}, {}, {}, {}
    for n in _BIG:
        parts = [u for u, (o, _) in sorted(zip(upd, owners), key=lambda t: t[1][1]) if o == n]
        grads[n], delta[n], new_m[n], new_v[n] = (jnp.stack([p[j] for p in parts]) for j in range(4))

    gs, loss_sum = _small_sum(small_gath, loss_g, "small_sum")
    loss = loss_sum[0, 0]
    for n, _ in _SMALL:
        shape = weights[n].shape
        if n == "mla_q_lora_norm":
            grads[n] = lax.dynamic_slice(_unpack_small(gs, n, (Q_LORA,)), (me * 64,), (64,)).reshape(shape)
        elif n == "mla_kv_lora_norm":
            grads[n] = lax.dynamic_slice(_unpack_small(gs, n, (KV_LORA,)), (me * 32,), (32,)).reshape(shape)
        else:
            grads[n] = _unpack_small(gs, n, shape)

    def own(a):
        return a.reshape(a.shape[1:]) if a.ndim >= 3 else a

    wide = ("s5_b_re", "s5_b_im", "s5_c_re", "s5_c_im")
    for names, nb, call in (([n for n, _ in _SMALL if n not in wide], 1, "update_small"), (wide, 6, "update_s5_bc")):
        res = _adamw_multi([own(weights[n]) for n in names], [own(grads[n]) for n in names],
                           [own(m_in[n]) for n in names], [own(v_in[n]) for n in names], call, nb)
        for n, (dl, m2, v2) in zip(names, res):
            shape = weights[n].shape
            delta[n], new_m[n], new_v[n] = dl.reshape(shape), m2.reshape(shape), v2.reshape(shape)
    return (loss, dx0[None], *[grads[n] for n in _WEIGHTS], *[delta[n] for n in _WEIGHTS],
            *[new_m[n] for n in _WEIGHTS], *[new_v[n] for n in _WEIGHTS])
```

```python
import functools
import math

import numpy as np
import jax
import jax.numpy as jnp
from jax import lax
from jax.experimental import pallas as pl
from jax.experimental.pallas import tpu as pltpu

F32 = jnp.float32
BF16 = jnp.bfloat16
EPS = 1e-6
NEG = float(np.finfo(np.float32).min)
MESH = pl.DeviceIdType.MESH

N_DEV = 8
D_MODEL = 1024
MEM_LEN = 256
XQ = 512
PRIM = 1536
BRANCH = 2048
X_HEADS = 4
HD = 128
S5_G = 96
S5_P = 64
S5_C = 16
S5_GB = 8
S5_W = S5_GB * S5_P
MLA_H = 12
ROPE = 64
Q_LORA = 512
KV_LORA = 256
ROPE_THETA = 10000.0

ADAM_LR = 0.001
ADAM_B1 = 0.9
ADAM_B2 = 0.999
ADAM_EPS = 1e-08
ADAM_WD = 0.01
ADAM_STEP = 10

VMEM_LIMIT = 56 * 1024 * 1024


def _dot(a, b):
    return jnp.dot(a, b, preferred_element_type=F32)


def _dot_nt(a, b):
    return lax.dot_general(a, b, (((1,), (1,)), ((), ())), preferred_element_type=F32)


def _dot_tn(a, b):
    return lax.dot_general(a, b, (((0,), (0,)), ((), ())), preferred_element_type=F32)


@jax.custom_vjp
def _mm(a, b):
    return _dot(a.astype(BF16), b.astype(BF16))


def _mm_fwd(a, b):
    return _mm(a, b), (a, b)


def _mm_bwd(res, g):
    a, b = res
    gb = g.astype(BF16)
    return _dot_nt(gb, b.astype(BF16)).astype(a.dtype), _dot_tn(a.astype(BF16), gb).astype(b.dtype)


_mm.defvjp(_mm_fwd, _mm_bwd)


@jax.custom_vjp
def _mm_nt(a, b):
    return _dot_nt(a.astype(BF16), b.astype(BF16))


def _mm_nt_fwd(a, b):
    return _mm_nt(a, b), (a, b)


def _mm_nt_bwd(res, g):
    a, b = res
    gb = g.astype(BF16)
    return _dot(gb, b.astype(BF16)).astype(a.dtype), _dot_tn(gb, a.astype(BF16)).astype(b.dtype)


_mm_nt.defvjp(_mm_nt_fwd, _mm_nt_bwd)


@jax.custom_vjp
def _softmax(s):
    m = jnp.max(s, axis=-1, keepdims=True)
    e = jnp.exp(s - m)
    return e / jnp.sum(e, axis=-1, keepdims=True)


def _softmax_fwd(s):
    p = _softmax(s)
    return p, p


def _softmax_bwd(p, g):
    return (p * (g - jnp.sum(p * g, axis=-1, keepdims=True)),)


_softmax.defvjp(_softmax_fwd, _softmax_bwd)


def _rms(x, g, n):
    ms = jnp.sum(x * x, axis=-1, keepdims=True) * (1.0 / n)
    return x * lax.rsqrt(ms + EPS) * g


def _sigmoid(x):
    return 1.0 / (1.0 + jnp.exp(-x))


def _silu(x):
    return x * _sigmoid(x)


def _gelu(x):
    c = math.sqrt(2.0 / math.pi)
    return 0.5 * x * (1.0 + jnp.tanh(c * (x + 0.044715 * (x * x * x))))


@jax.custom_vjp
def _rot(x, c, s1, s2):
    return x * c + pltpu.roll(x, 96, 1) * s1 + pltpu.roll(x, 32, 1) * s2


def _rot_fwd(x, c, s1, s2):
    return _rot(x, c, s1, s2), (c, s1, s2)


def _rot_bwd(res, g):
    c, s1, s2 = res
    dx = g * c + pltpu.roll(g * s1, 32, 1) + pltpu.roll(g * s2, 96, 1)
    return dx, jnp.zeros_like(c), jnp.zeros_like(s1), jnp.zeros_like(s2)


_rot.defvjp(_rot_fwd, _rot_bwd)


def _mem_attn(xq, k, v, gq):
    outs = []
    for h in range(X_HEADS):
        sl = slice(HD * h, HD * (h + 1))
        q = _rms(xq[:, sl], gq, HD)
        p = _softmax(_mm_nt(q, k[:, sl]) * (HD ** -0.5))
        outs.append(_mm(p, v[:, sl]))
    return jnp.concatenate(outs, axis=-1)


def _merge(mix, xq, gate, k, v, gq):
    return jnp.concatenate([mix, _mem_attn(xq, k, v, gq)], axis=-1) * _silu(gate)


def _q_post(q, gqn, gqr, c, s1, s2):
    pieces = []
    for h in range(MLA_H):
        pieces.append(_rms(q[:, HD * h:HD * (h + 1)], gqn, HD))
        pieces.append(_rot(_rms(q[:, PRIM + HD * h:PRIM + HD * (h + 1)], gqr, ROPE), c, s1, s2))
    return jnp.concatenate(pieces, axis=-1)


def _kv_post(kv, krp, gkn, gkr, c, s1, s2):
    kr = _rot(_rms(krp, gkr, ROPE), c, s1, s2)
    pieces, vals = [], []
    for h in range(MLA_H):
        pieces.append(_rms(kv[:, 2 * HD * h:2 * HD * h + HD], gkn, HD))
        pieces.append(kr)
        vals.append(kv[:, 2 * HD * h + HD:2 * HD * (h + 1)])
    return jnp.concatenate(pieces, axis=-1), jnp.concatenate(vals, axis=-1)


def _rowwise(name, fn, ins, outs, nblk, sub=1, host=None):
    n_in = len(ins)

    def spec(kind, shape):
        if kind == 'r':
            return pl.BlockSpec((shape[0] // nblk, shape[1]), lambda i: (i, 0))
        if kind == 't':
            return pl.BlockSpec((shape[0], shape[1] // nblk), lambda i: (0, i))
        zeros = (0,) * len(shape)
        return pl.BlockSpec(tuple(shape), lambda i: zeros)

    def body(*refs):
        i = pl.program_id(0)
        res = fn(*[r[...] for r in refs[:n_in]])
        for (kind, _, _), ref, val in zip(outs, refs[n_in:], res):
            if kind == 'a':
                @pl.when(i == 0)
                def _():
                    ref[...] = jnp.zeros_like(ref)
                ref[...] += val.astype(ref.dtype)
            elif kind == 't':
                ref[...] = val.astype(F32).T.astype(ref.dtype)
            else:
                ref[...] = val.astype(ref.dtype)

    res, hosted = _hosting_call(
        body, name, nblk, host, [a for _, a in ins], [spec(k, a.shape) for k, a in ins],
        [jax.ShapeDtypeStruct(tuple(s), d) for _, s, d in outs], [spec(k, s) for k, s, _ in outs], [])
    return res if host is None else (res, hosted)


def _matmul_tn(at, g, name, out_dtype=BF16):
    K, L = at.shape
    N = g.shape[1]
    tn = next(t for t in (512, 384, 256, 128) if N % t == 0)

    def body(a_ref, g_ref, o_ref):
        o_ref[...] = _dot(a_ref[...], g_ref[...]).astype(o_ref.dtype)

    return pl.pallas_call(
        body, name=name, grid=(N // tn,),
        in_specs=[pl.BlockSpec((K, L), lambda n: (0, 0)), pl.BlockSpec((L, tn), lambda n: (0, n))],
        out_specs=pl.BlockSpec((K, tn), lambda n: (0, n)),
        out_shape=jax.ShapeDtypeStruct((K, N), out_dtype),
        compiler_params=pltpu.CompilerParams(dimension_semantics=("arbitrary",), vmem_limit_bytes=VMEM_LIMIT),
    )(at, g)


def _matmul_tn_slots(at, g, name, host=None):
    K, L = at.shape
    n = g.shape[1] // N_DEV

    def body(a_ref, g_ref, o_ref):
        o_ref[...] = _dot(a_ref[...], g_ref[...]).astype(o_ref.dtype)

    res, hosted = _hosting_call(
        body, name, N_DEV, host, [at, g],
        [pl.BlockSpec((K, L), lambda d: (0, 0)), pl.BlockSpec((L, n), lambda d: (0, d))],
        [jax.ShapeDtypeStruct((N_DEV, K, n), BF16)], [pl.BlockSpec((None, K, n), lambda d: (d, 0, 0))], [])
    return res[0] if host is None else (res[0], hosted)


def _mm_slots(a16, w):
    return jnp.concatenate([_dot(a16, w[d]) for d in range(N_DEV)], axis=-1)


def _mm_slots_nt(g16, w):
    n = w.shape[2]
    out = _dot_nt(g16[:, 0:n], w[0])
    for d in range(1, N_DEV):
        out = out + _dot_nt(g16[:, d * n:(d + 1) * n], w[d])
    return out


class _Exchange:
    def __init__(self, ins, outs, scratch, start, finish):
        self.ins, self.outs, self.scratch, self.start, self.finish = ins, outs, scratch, start, finish


def _xyc():
    return lax.axis_index("x"), lax.axis_index("y"), lax.axis_index("c")


def _plan_all_gather(xs):
    n = len(xs)

    def build(x_refs, out_refs, sems):
        send_sems, recv_sems, local_sems = sems
        x, y, c = _xyc()

        def copies(k, block, to, own=False):
            slot = 4 * block[0] + 2 * block[1] + block[2]
            return [pltpu.make_async_remote_copy(
                src_ref=x_refs[a] if own else out_refs[a].at[slot], dst_ref=out_refs[a].at[slot],
                send_sem=send_sems.at[k * n + a], recv_sem=recv_sems.at[k * n + a], device_id=to,
                device_id_type=MESH) for a in range(n)]

        mine = [pltpu.make_async_copy(x_refs[a], out_refs[a].at[4 * x + 2 * y + c], local_sems.at[a])
                for a in range(n)]
        return copies, mine, (x, y, c), [(1 - x, y), (x, 1 - y), (1 - x, 1 - y)]

    def first_copies(copies, me, chips):
        x, y, c = me
        first = copies(0, me, (x, y, 1 - c), own=True)
        for j, chip in enumerate(chips):
            first += copies(1 + j, me, (*chip, c), own=True)
        return first

    def start(x_refs, out_refs, sems):
        copies, mine, me, chips = build(x_refs, out_refs, sems)
        for cp in mine + first_copies(copies, me, chips):
            cp.start()

    def finish(x_refs, out_refs, sems):
        copies, mine, me, chips = build(x_refs, out_refs, sems)
        x, y, c = me
        passed = []
        for j, chip in enumerate(chips):
            for cp in copies(1 + j, (*chip, c), me):
                cp.wait_recv()
            fwd = copies(4 + j, (*chip, c), (x, y, 1 - c))
            for cp in fwd:
                cp.start()
            passed += fwd
        for cp in copies(0, (x, y, 1 - c), me):
            cp.wait_recv()
        for j, chip in enumerate(chips):
            for cp in copies(4 + j, (*chip, 1 - c), me):
                cp.wait_recv()
        for cp in first_copies(copies, me, chips) + passed:
            cp.wait_send()
        for cp in mine:
            cp.wait()

    return _Exchange(list(xs), [jax.ShapeDtypeStruct((N_DEV,) + a.shape, a.dtype) for a in xs],
                     [pltpu.SemaphoreType.DMA((7 * n,)), pltpu.SemaphoreType.DMA((7 * n,)),
                      pltpu.SemaphoreType.DMA((n,))], start, finish)


_CHIPS = ((0, 0), (0, 1), (1, 0), (1, 1))


def _plan_pair(sends):
    n = len(sends)

    def build(s_refs, o_refs, sems):
        send_sems, recv_sems = sems
        x, y, c = _xyc()
        return [pltpu.make_async_remote_copy(
            src_ref=s_refs[a].at[4 * px + 2 * py + 1 - c], dst_ref=o_refs[a].at[j],
            send_sem=send_sems.at[j * n + a], recv_sem=recv_sems.at[j * n + a], device_id=(x, y, 1 - c),
            device_id_type=MESH) for j, (px, py) in enumerate(_CHIPS) for a in range(n)]

    def start(s_refs, o_refs, sems):
        for cp in build(s_refs, o_refs, sems):
            cp.start()

    def finish(s_refs, o_refs, sems):
        for cp in build(s_refs, o_refs, sems):
            cp.wait_recv()
            cp.wait_send()

    return _Exchange(list(sends), [jax.ShapeDtypeStruct((4,) + a.shape[1:], a.dtype) for a in sends],
                     [pltpu.SemaphoreType.DMA((4 * n,)), pltpu.SemaphoreType.DMA((4 * n,))], start, finish)


def _plan_chips(ts):
    n = len(ts)
    flips = ((1, 0), (0, 1), (1, 1))

    def build(t_refs, o_refs, sems):
        send_sems, recv_sems, local_sems = sems
        x, y, c = _xyc()
        mine = 2 * x + y
        local = [pltpu.make_async_copy(t_refs[a].at[mine], o_refs[a].at[mine], local_sems.at[a]) for a in range(n)]
        remote = []
        for k, (fx, fy) in enumerate(flips):
            px = 1 - x if fx else x
            py = 1 - y if fy else y
            remote += [pltpu.make_async_remote_copy(
                src_ref=t_refs[a].at[2 * px + py], dst_ref=o_refs[a].at[mine],
                send_sem=send_sems.at[k * n + a], recv_sem=recv_sems.at[k * n + a], device_id=(px, py, c),
                device_id_type=MESH) for a in range(n)]
        return local, remote

    def start(t_refs, o_refs, sems):
        local, remote = build(t_refs, o_refs, sems)
        for cp in local + remote:
            cp.start()

    def finish(t_refs, o_refs, sems):
        local, remote = build(t_refs, o_refs, sems)
        for cp in remote:
            cp.wait_recv()
        for cp in remote:
            cp.wait_send()
        for cp in local:
            cp.wait()

    return _Exchange(list(ts), [jax.ShapeDtypeStruct(a.shape, a.dtype) for a in ts],
                     [pltpu.SemaphoreType.DMA((3 * n,)), pltpu.SemaphoreType.DMA((3 * n,)),
                      pltpu.SemaphoreType.DMA((n,))], start, finish)


def _combine(*plans):
    def parts(refs, attr):
        out, at = [], 0
        for p in plans:
            n = len(getattr(p, attr))
            out.append(refs[at:at + n])
            at += n
        return out

    def run(half):
        def go(ins, outs, sems):
            for p, a, o, s in zip(plans, parts(ins, "ins"), parts(outs, "outs"), parts(sems, "scratch")):
                getattr(p, half)(a, o, s)
        return go

    return _Exchange(sum((p.ins for p in plans), []), sum((p.outs for p in plans), []),
                     sum((p.scratch for p in plans), []), run("start"), run("finish"))


def _exchange_call(plan, name):
    n = len(plan.ins)

    def body(*refs):
        ins, outs, sems = refs[:n], refs[n:2 * n], refs[2 * n:]
        plan.start(ins, outs, sems)
        plan.finish(ins, outs, sems)

    return pl.pallas_call(
        body, name=name, out_shape=plan.outs,
        in_specs=[pl.BlockSpec(memory_space=pl.ANY)] * n, out_specs=[pl.BlockSpec(memory_space=pl.ANY)] * n,
        scratch_shapes=plan.scratch,
    )(*plan.ins)


def _pair_add(sends, fromsib, name):
    n = len(sends)
    nb = 8

    def body(*refs):
        c = lax.axis_index("c")
        for a in range(n):
            s_ref, f_ref, t_ref = refs[a], refs[n + a], refs[2 * n + a]
            for j in range(4):
                t_ref[j] = (s_ref[2 * j + c].astype(F32) + f_ref[j].astype(F32)).astype(t_ref.dtype)

    def spec(a, lead):
        return pl.BlockSpec((lead, a.shape[1] // nb, a.shape[2]), lambda i: (0, i, 0))

    return pl.pallas_call(
        body, name=name, grid=(nb,),
        in_specs=[spec(a, N_DEV) for a in sends] + [spec(a, 4) for a in fromsib],
        out_specs=[spec(a, 4) for a in fromsib],
        out_shape=[jax.ShapeDtypeStruct(a.shape, a.dtype) for a in fromsib],
        compiler_params=pltpu.CompilerParams(dimension_semantics=("arbitrary",), vmem_limit_bytes=VMEM_LIMIT),
    )(*sends, *fromsib)


def _adamw_vals(w, g, m, v):
    m2 = ADAM_B1 * m + (1.0 - ADAM_B1) * g
    v2 = ADAM_B2 * v + (1.0 - ADAM_B2) * (g * g)
    m_hat = m2 / (1.0 - ADAM_B1 ** ADAM_STEP)
    v_hat = v2 / (1.0 - ADAM_B2 ** ADAM_STEP)
    delta = -ADAM_LR * (m_hat / (jnp.sqrt(v_hat) + ADAM_EPS) + ADAM_WD * w)
    return delta, m2, v2


def _sum_adamw(recv, w, m, v, name):
    R, C = w.shape
    ns = recv.shape[0]
    br = next((t for t in (256, 128, 64, 32, 16) if R % t == 0), R)

    def body(r_ref, w_ref, m_ref, v_ref, g_ref, d_ref, m2_ref, v2_ref):
        g = r_ref[0].astype(F32)
        for d in range(1, ns):
            g = g + r_ref[d].astype(F32)
        dl, m2, v2 = _adamw_vals(w_ref[...], g, m_ref[...], v_ref[...])
        g_ref[...] = g
        d_ref[...] = dl
        m2_ref[...] = m2
        v2_ref[...] = v2

    spec = pl.BlockSpec((br, C), lambda i: (i, 0))
    return pl.pallas_call(
        body, name=name, grid=(R // br,),
        in_specs=[pl.BlockSpec((ns, br, C), lambda i: (0, i, 0)), spec, spec, spec], out_specs=[spec] * 4,
        out_shape=[jax.ShapeDtypeStruct((R, C), F32)] * 4,
        compiler_params=pltpu.CompilerParams(dimension_semantics=("arbitrary",)),
    )(recv, w, m, v)


def _updates_call(recvs, ws, ms, vs, name, host=None):
    n = len(recvs)
    nb = 8

    def body(*refs):
        for a in range(n):
            r_ref, w_ref, m_ref, v_ref = refs[a], refs[n + a], refs[2 * n + a], refs[3 * n + a]
            g_ref, d_ref, m2_ref, v2_ref = refs[4 * n + 4 * a:4 * n + 4 * a + 4]
            g = r_ref[0].astype(F32)
            for d in range(1, r_ref.shape[0]):
                g = g + r_ref[d].astype(F32)
            dl, m2, v2 = _adamw_vals(w_ref[...], g, m_ref[...], v_ref[...])
            g_ref[...] = g
            d_ref[...] = dl
            m2_ref[...] = m2
            v2_ref[...] = v2

    def spec2(w):
        return pl.BlockSpec((w.shape[0] // nb, w.shape[1]), lambda i: (i, 0))

    def spec3(r):
        return pl.BlockSpec((r.shape[0], r.shape[1] // nb, r.shape[2]), lambda i: (0, i, 0))

    res, hosted = _hosting_call(
        body, name, nb, host, list(recvs) + list(ws) + list(ms) + list(vs),
        [spec3(r) for r in recvs] + [spec2(w) for w in ws] * 3,
        [jax.ShapeDtypeStruct(w.shape, F32) for w in ws for _ in range(4)],
        [spec2(w) for w in ws for _ in range(4)], [])
    return [res[4 * a:4 * a + 4] for a in range(n)], hosted


def _small_sum(gath, loss_g, name):
    _, R, C = gath.shape
    br = R // 3

    def body(g_ref, l_ref, go_ref, lo_ref):
        g = g_ref[0].astype(F32)
        lsum = l_ref[0]
        for d in range(1, N_DEV):
            g = g + g_ref[d].astype(F32)
            lsum = lsum + l_ref[d]
        go_ref[...] = g
        lo_ref[...] = lsum

    return pl.pallas_call(
        body, name=name, grid=(R // br,),
        in_specs=[pl.BlockSpec((N_DEV, br, C), lambda i: (0, i, 0)),
                  pl.BlockSpec((N_DEV, 8, HD), lambda i: (0, 0, 0))],
        out_specs=[pl.BlockSpec((br, C), lambda i: (i, 0)), pl.BlockSpec((8, HD), lambda i: (0, 0))],
        out_shape=[jax.ShapeDtypeStruct((R, C), F32), jax.ShapeDtypeStruct((8, HD), F32)],
        compiler_params=pltpu.CompilerParams(dimension_semantics=("arbitrary",)),
    )(gath, loss_g)


def _adamw_multi(ws, gs, ms, vs, name, nblk=1):
    n = len(ws)

    def body(*refs):
        for a in range(n):
            dl, m2, v2 = _adamw_vals(refs[a][...], refs[n + a][...], refs[2 * n + a][...], refs[3 * n + a][...])
            refs[4 * n + 3 * a][...] = dl
            refs[4 * n + 3 * a + 1][...] = m2
            refs[4 * n + 3 * a + 2][...] = v2

    def spec(x):
        rest = (0,) * (x.ndim - 1)
        return pl.BlockSpec((x.shape[0] // nblk,) + tuple(x.shape[1:]), lambda i: (i,) + rest)

    res = pl.pallas_call(
        body, name=name, grid=(nblk,),
        in_specs=[spec(w) for w in ws] * 4, out_specs=[spec(w) for w in ws for _ in range(3)],
        out_shape=[jax.ShapeDtypeStruct(w.shape, F32) for w in ws for _ in range(3)],
        compiler_params=pltpu.CompilerParams(dimension_semantics=("arbitrary",), vmem_limit_bytes=VMEM_LIMIT),
    )(*ws, *gs, *ms, *vs)
    return [res[3 * a:3 * a + 3] for a in range(n)]


def _s5_param_fn(lr, li, ls, btr, bti):
    step = jnp.exp(ls)
    er = jnp.exp(lr * step)
    ang = li * step
    ar = er * jnp.cos(ang)
    ai = er * jnp.sin(ang)
    nr = ar - 1.0
    den = lr * lr + li * li
    fr = (nr * lr + ai * li) / den
    fi = (ai * lr - nr * li) / den
    return ar, ai, fr * btr - fi * bti, fr * bti + fi * btr


def _s5_params(lr, li, ls, btr, bti):
    def body(lr_ref, li_ref, ls_ref, br_ref, bi_ref, ar_ref, ai_ref, bbr_ref, bbi_ref):
        ar, ai, bbr, bbi = _s5_param_fn(lr_ref[...], li_ref[...], ls_ref[...], br_ref[...], bi_ref[...])
        ar_ref[...] = ar
        ai_ref[...] = ai
        bbr_ref[...] = bbr
        bbi_ref[...] = bbi

    sd = jax.ShapeDtypeStruct
    return pl.pallas_call(
        body, name="s5_params",
        out_shape=[sd(lr.shape, F32), sd(lr.shape, F32), sd(btr.shape, F32), sd(btr.shape, F32)],
    )(lr, li, ls, btr, bti)


def _s5_params_bwd(lr, li, ls, btr, bti, dar, dai, dbbr, dbbi):
    def body(lr_ref, li_ref, ls_ref, br_ref, bi_ref, dar_ref, dai_ref, dbbr_ref, dbbi_ref,
             dlr_ref, dli_ref, dls_ref, dbr_ref, dbi_ref):
        _, vjp = jax.vjp(_s5_param_fn, lr_ref[...], li_ref[...], ls_ref[...], br_ref[...], bi_ref[...])
        dlr, dli, dls, dbr, dbi = vjp((dar_ref[...], dai_ref[...], dbbr_ref[...], dbbi_ref[...]))
        dlr_ref[...] = dlr
        dli_ref[...] = dli
        dls_ref[...] = dls
        dbr_ref[...] = dbr
        dbi_ref[...] = dbi

    sd = jax.ShapeDtypeStruct
    return pl.pallas_call(
        body, name="s5_params_bwd",
        out_shape=[sd(lr.shape, F32), sd(lr.shape, F32), sd(ls.shape, F32), sd(btr.shape, F32), sd(btr.shape, F32)],
    )(lr, li, ls, btr, bti, dar, dai, dbbr, dbbi)


def _cpow(ar, ai, n):
    assert n & (n - 1) == 0
    while n > 1:
        ar, ai = ar * ar - ai * ai, 2.0 * ar * ai
        n //= 2
    return ar, ai


def _scan(st, cr, ci, init, nk, reverse, store, prev=None):
    W = S5_W

    def step(j, carry):
        k = nk - 1 - j if reverse else j
        rows = pl.ds(pl.multiple_of(k * 8, 8), 8)
        sr, si = carry[0], carry[1]
        nsr = cr * sr - ci * si + st[rows, 0:W]
        nsi = cr * si + ci * sr + st[rows, W:2 * W]
        if store:
            st[rows, 0:W] = nsr
            st[rows, W:2 * W] = nsi
        if prev is None:
            return nsr, nsi
        prows = pl.ds(pl.multiple_of(jnp.maximum(k - 1, 0) * 8, 8), 8)
        w = jnp.where(k > 0, 1.0, 0.0).astype(F32)
        pr = prev[prows, 0:W] * w
        pi = prev[prows, W:2 * W] * w
        return nsr, nsi, carry[2] + nsr * pr + nsi * pi, carry[3] + nsi * pr - nsr * pi

    return lax.fori_loop(0, nk, step, init, unroll=2)


def _chain(fin, fr, fi, pr, pi, reverse):
    W = S5_W
    fin[:, 0:W] = fr
    fin[:, W:2 * W] = fi
    rowid = lax.broadcasted_iota(jnp.int32, (8, W), 0)
    cr = jnp.zeros((1, W), F32)
    ci = jnp.zeros((1, W), F32)
    init_r = jnp.zeros((8, W), F32)
    init_i = jnp.zeros((8, W), F32)
    for s in (range(7, -1, -1) if reverse else range(8)):
        init_r = jnp.where(rowid == s, cr, init_r)
        init_i = jnp.where(rowid == s, ci, init_i)
        lr = fin[s:s + 1, 0:W]
        li = fin[s:s + 1, W:2 * W]
        cr, ci = lr + pr * cr - pi * ci, li + pr * ci + pi * cr
    return init_r, init_i


def _full_scan(st, fin, ar, ai, nk, reverse, prev=None, carry_in=None, carry_out=None):
    W = S5_W
    cr = jnp.broadcast_to(ar, (8, W))
    ci = jnp.broadcast_to(-ai if reverse else ai, (8, W))
    z = jnp.zeros((8, W), F32)
    if carry_in is None:
        fr, fi = _scan(st, cr, ci, (z, z), nk, reverse, store=False)
        pr, pi = _cpow(ar, -ai if reverse else ai, nk)
        init = _chain(fin, fr, fi, pr, pi, reverse)
    else:
        init = (carry_in[:, 0:W], carry_in[:, W:2 * W])
    if carry_out is not None:
        carry_out[:, 0:W] = init[0]
        carry_out[:, W:2 * W] = init[1]
    if prev is None:
        return _scan(st, cr, ci, init, nk, reverse, store=True)
    return _scan(st, cr, ci, init + (z, z), nk, reverse, store=True, prev=prev)


def _s5_specs(L):
    W2 = 2 * S5_W
    GC = S5_GB * S5_C
    col = pl.BlockSpec((L, GC), lambda g: (0, g))
    vec = pl.BlockSpec((1, GC), lambda g: (0, g))
    avec = pl.BlockSpec((1, S5_W), lambda g: (0, g))
    bmat = pl.BlockSpec((None, GC, W2), lambda g: (g, 0, 0))
    cmat = pl.BlockSpec((None, W2, GC), lambda g: (g, 0, 0))
    return col, vec, avec, bmat, cmat


def _interleave(dst, src, nk):
    for s in range(8):
        dst[pl.ds(s, nk, stride=8), :] = src[s * nk:(s + 1) * nk, :]


def _deinterleave(dst, src, nk):
    for s in range(8):
        dst[s * nk:(s + 1) * nk, :] = src[pl.ds(s, nk, stride=8), :].astype(dst.dtype)


def _hosting_call(body, name, nsteps, host, ins, in_specs, outs, out_specs, scratch):
    grid = (nsteps,) if isinstance(nsteps, int) else tuple(nsteps)
    params = pltpu.CompilerParams(dimension_semantics=("arbitrary",) * len(grid), vmem_limit_bytes=VMEM_LIMIT)
    if host is None:
        res = pl.pallas_call(
            body, name=name, grid=grid, in_specs=in_specs, out_specs=out_specs, out_shape=outs,
            scratch_shapes=scratch, compiler_params=params,
        )(*ins)
        return list(res), []
    n_in, n_out, n_sc = len(ins), len(outs), len(scratch)
    h_in, h_out = len(host.ins), len(host.outs)

    def hosted(*refs):
        a = refs[:n_in]
        ha = refs[n_in:n_in + h_in]
        o = refs[n_in + h_in:n_in + h_in + n_out]
        ho = refs[n_in + h_in + n_out:n_in + h_in + n_out + h_out]
        sc = refs[n_in + h_in + n_out + h_out:n_in + h_in + n_out + h_out + n_sc]
        hs = refs[n_in + h_in + n_out + h_out + n_sc:]
        first = functools.reduce(jnp.logical_and, [pl.program_id(i) == 0 for i in range(len(grid))])
        last = functools.reduce(jnp.logical_and, [pl.program_id(i) == g - 1 for i, g in enumerate(grid)])

        @pl.when(first)
        def _():
            host.start(ha, ho, hs)

        body(*a, *o, *sc)

        @pl.when(last)
        def _():
            host.finish(ha, ho, hs)

    hbm = pl.BlockSpec(memory_space=pl.ANY)
    res = pl.pallas_call(
        hosted, name=name, grid=grid,
        in_specs=list(in_specs) + [hbm] * h_in, out_specs=list(out_specs) + [hbm] * h_out,
        out_shape=list(outs) + list(host.outs), scratch_shapes=list(scratch) + list(host.scratch),
        compiler_params=params,
    )(*ins, *host.ins)
    return list(res[:n_out]), list(res[n_out:])


def _s5_fwd(u, bm, cm, ar, ai, dvec, host=None):
    L = u.shape[0]
    nk = L // 8
    GC = S5_GB * S5_C
    nb = S5_G // S5_GB
    col, vec, avec, bmat, cmat = _s5_specs(L)

    def body(u_ref, b_ref, c_ref, ar_ref, ai_ref, d_ref, y_ref, carry_ref, st, fin, ui, yi):
        _interleave(ui, u_ref, nk)
        for r in range(8):
            rows = slice(r * nk, (r + 1) * nk)
            st[rows, :] = _dot(ui[rows, :].astype(BF16), b_ref[...])
        _full_scan(st, fin, ar_ref[...], ai_ref[...], nk, reverse=False, carry_out=carry_ref)
        for r in range(8):
            rows = slice(r * nk, (r + 1) * nk)
            yi[rows, :] = _dot(st[rows, :].astype(BF16), c_ref[...]) + d_ref[...] * ui[rows, :]
        _deinterleave(y_ref, yi, nk)

    return _hosting_call(
        body, "s5_fwd", nb, host,
        [u, bm, cm, ar, ai, dvec], [col, bmat, cmat, avec, avec, vec],
        [jax.ShapeDtypeStruct(u.shape, F32), jax.ShapeDtypeStruct((nb * 8, 2 * S5_W), F32)],
        [col, pl.BlockSpec((8, 2 * S5_W), lambda g: (g, 0))],
        [pltpu.VMEM((L, 2 * S5_W), F32), pltpu.VMEM((8, 2 * S5_W), F32), pltpu.VMEM((L, GC), F32),
         pltpu.VMEM((L, GC), F32)])


def _s5_bwd(u, dy, carry, bm, bmt, cmt, ar, ai, dvec, mask, rmat, host=None):
    L = u.shape[0]
    nk = L // 8
    W = S5_W
    GC = S5_GB * S5_C
    col, vec, avec, bmat, cmat = _s5_specs(L)
    hi = lax.Precision.HIGHEST

    def body(u_ref, dy_ref, carry_ref, b_ref, bt_ref, ct_ref, ar_ref, ai_ref, d_ref, mask_ref, r_ref,
             du_ref, db_ref, dc_ref, dd_ref, dar_ref, dai_ref, sa, sb, fin, ui, dyi, dui):
        ar = ar_ref[...]
        ai = ai_ref[...]
        _interleave(ui, u_ref, nk)
        _interleave(dyi, dy_ref, nk)
        for r in range(8):
            rows = slice(r * nk, (r + 1) * nk)
            sa[rows, :] = _dot(ui[rows, :].astype(BF16), b_ref[...])
            sb[rows, :] = _dot(dyi[rows, :].astype(BF16), ct_ref[...])
        _full_scan(sa, fin, ar, ai, nk, reverse=False, carry_in=carry_ref)
        gr, gi, accr, acci = _full_scan(sb, fin, ar, ai, nk, reverse=True, prev=sa)
        rowid = lax.broadcasted_iota(jnp.int32, (8, W), 0)
        last = pl.ds((nk - 1) * 8, 8)
        pr = jnp.where(rowid == 0, 0.0, pltpu.roll(sa[last, 0:W], 1, 0))
        pi = jnp.where(rowid == 0, 0.0, pltpu.roll(sa[last, W:2 * W], 1, 0))
        accr = accr + gr * pr + gi * pi
        acci = acci + gi * pr - gr * pi
        dar_ref[...] = jnp.sum(accr, axis=0, keepdims=True)
        dai_ref[...] = jnp.sum(acci, axis=0, keepdims=True)
        dbf = jnp.zeros((GC, 2 * W), F32)
        dcf = jnp.zeros((GC, 2 * W), F32)
        dd = jnp.zeros((1, GC), F32)
        for r in range(8):
            rows = slice(r * nk, (r + 1) * nk)
            ub = ui[rows, :]
            dyb = dyi[rows, :]
            gb = sb[rows, :].astype(BF16)
            dui[rows, :] = _dot(gb, bt_ref[...]) + d_ref[...] * dyb
            dbf = dbf + _dot_tn(ub.astype(BF16), gb)
            dcf = dcf + _dot_tn(dyb.astype(BF16), sa[rows, :].astype(BF16))
            dd = dd + jnp.sum(dyb * ub, axis=0, keepdims=True)
        db_ref[...] = jnp.dot(dbf * mask_ref[...], r_ref[...], precision=hi, preferred_element_type=F32)
        dc_ref[...] = jnp.dot(dcf * mask_ref[...], r_ref[...], precision=hi, preferred_element_type=F32)
        dd_ref[...] = dd
        _deinterleave(du_ref, dui, nk)

    cmp_spec = pl.BlockSpec((GC, 2 * S5_P), lambda g: (g, 0))
    whole = lambda shape: pl.BlockSpec(shape, lambda g: (0, 0))
    sd = jax.ShapeDtypeStruct
    return _hosting_call(
        body, "s5_bwd", S5_G // S5_GB, host,
        [u, dy, carry, bm, bmt, cmt, ar, ai, dvec, mask, rmat],
        [col, col, pl.BlockSpec((8, 2 * W), lambda g: (g, 0)), bmat, cmat, bmat, avec, avec, vec, whole(mask.shape),
         whole(rmat.shape)],
        [sd(u.shape, BF16), sd((S5_G * S5_C, 2 * S5_P), F32), sd((S5_G * S5_C, 2 * S5_P), F32),
         sd((1, PRIM), F32), sd((1, S5_G * S5_P), F32), sd((1, S5_G * S5_P), F32)],
        [col, cmp_spec, cmp_spec, vec, avec, avec],
        [pltpu.VMEM((L, 2 * W), F32), pltpu.VMEM((L, 2 * W), F32), pltpu.VMEM((8, 2 * W), F32),
         pltpu.VMEM((L, GC), F32), pltpu.VMEM((L, GC), F32), pltpu.VMEM((L, GC), F32)])


def _s5_mats(bbr, bbi, cre, cim):
    nb = S5_G // S5_GB
    eye = jnp.eye(S5_GB, dtype=F32)
    bb = jnp.stack([bbr, bbi], axis=2).reshape(nb, S5_GB, S5_C, 2, S5_P)
    bm = jnp.einsum('ngcrp,gh->ngcrhp', bb, eye).reshape(nb, S5_GB * S5_C, 2 * S5_W)
    cc = jnp.stack([cre, -cim], axis=2).reshape(nb, S5_GB, S5_C, 2, S5_P)
    cmt = jnp.einsum('ngcrp,gh->ngcrhp', cc, eye).reshape(nb, S5_GB * S5_C, 2 * S5_W)
    return (bm.astype(BF16), jnp.swapaxes(bm, 1, 2).astype(BF16),
            jnp.swapaxes(cmt, 1, 2).astype(BF16), cmt.astype(BF16))


def _s5_compact_consts():
    g_row = np.arange(S5_GB * S5_C) // S5_C
    col = np.arange(2 * S5_W)
    g_col = (col % S5_W) // S5_P
    mask = (g_row[:, None] == g_col[None, :]).astype(np.float32)
    tgt = (col // S5_W) * S5_P + col % S5_P
    rmat = (tgt[:, None] == np.arange(2 * S5_P)[None, :]).astype(np.float32)
    return jnp.asarray(mask), jnp.asarray(rmat)


def _attn_scores(q_ref, k_ref, qb, bq, scale):
    ext = (qb + 1) * bq
    s = _dot_nt(q_ref[qb * bq:ext, :], k_ref[0:ext, :]) * scale
    qpos = lax.broadcasted_iota(jnp.int32, (bq, bq), 0)
    kpos = lax.broadcasted_iota(jnp.int32, (bq, bq), 1)
    diag = jnp.where(kpos <= qpos, s[:, ext - bq:], NEG)
    return diag if qb == 0 else jnp.concatenate([s[:, :ext - bq], diag], axis=-1)


def _attn_fwd(qp, kp, v, scale):
    L = qp.shape[0]
    bq = min(256, L)

    def body(q_ref, k_ref, v_ref, o_ref, lse_ref):
        for qb in range(L // bq):
            rows = slice(qb * bq, (qb + 1) * bq)
            s = _attn_scores(q_ref, k_ref, qb, bq, scale)
            m = jnp.max(s, axis=-1, keepdims=True)
            e = jnp.exp(s - m)
            l = jnp.sum(e, axis=-1, keepdims=True)
            o_ref[rows, :] = _dot(e.astype(BF16), v_ref[0:(qb + 1) * bq, :]) / l
            lse_ref[rows, :] = jnp.broadcast_to(m + jnp.log(l), (bq, HD))

    blk = pl.BlockSpec((L, HD), lambda h: (0, h))
    wide = pl.BlockSpec((L, 2 * HD), lambda h: (0, h))
    return pl.pallas_call(
        body, name="mla_attn_fwd", grid=(MLA_H,),
        in_specs=[wide, wide, blk], out_specs=[blk, blk],
        out_shape=[jax.ShapeDtypeStruct((L, MLA_H * HD), F32)] * 2,
        compiler_params=pltpu.CompilerParams(dimension_semantics=("arbitrary",), vmem_limit_bytes=VMEM_LIMIT),
    )(qp, kp, v)


def _attn_bwd(qp, kp, v, o, lse, do, scale):
    L = qp.shape[0]
    bq = min(256, L)
    nq = L // bq

    def body(q_ref, k_ref, v_ref, o_ref, lse_ref, do_ref, dq_ref, dk_ref, dv_ref, dk_acc, dv_acc):
        dk_acc[...] = jnp.zeros_like(dk_acc)
        dv_acc[...] = jnp.zeros_like(dv_acc)
        for qb in range(nq):
            rows = slice(qb * bq, (qb + 1) * bq)
            ext = (qb + 1) * bq
            do = do_ref[rows, :]
            dob = do.astype(BF16)
            p = jnp.exp(_attn_scores(q_ref, k_ref, qb, bq, scale) - lse_ref[rows, 0:1])
            dp = _dot_nt(dob, v_ref[0:ext, :])
            dsum = jnp.sum(do * o_ref[rows, :], axis=-1, keepdims=True)
            ds = (p * (dp - dsum) * scale).astype(BF16)
            dq_ref[rows, :] = _dot(ds, k_ref[0:ext, :]).astype(dq_ref.dtype)
            dk_acc[0:ext, :] += _dot_tn(ds, q_ref[rows, :])
            dv_acc[0:ext, :] += _dot_tn(p.astype(BF16), dob)
        dk_ref[...] = dk_acc[...].astype(dk_ref.dtype)
        dv_ref[...] = dv_acc[...].astype(dv_ref.dtype)

    sd = jax.ShapeDtypeStruct
    blk = pl.BlockSpec((L, HD), lambda h: (0, h))
    wide = pl.BlockSpec((L, 2 * HD), lambda h: (0, h))
    return pl.pallas_call(
        body, name="mla_attn_bwd", grid=(MLA_H,),
        in_specs=[wide, wide, blk, blk, blk, blk], out_specs=[wide, wide, blk],
        out_shape=[sd((L, MLA_H * 2 * HD), BF16), sd((L, MLA_H * 2 * HD), BF16), sd((L, MLA_H * HD), BF16)],
        scratch_shapes=[pltpu.VMEM((L, 2 * HD), F32), pltpu.VMEM((L, HD), F32)],
        compiler_params=pltpu.CompilerParams(dimension_semantics=("arbitrary",), vmem_limit_bytes=VMEM_LIMIT),
    )(qp, kp, v, o, lse, do)


def _kv_fn(mem, gm, w, gk):
    kv = _mm(_rms(mem, gm, D_MODEL), w)
    k = jnp.concatenate([_rms(kv[:, HD * h:HD * (h + 1)], gk, HD) for h in range(X_HEADS)], axis=-1)
    return k, kv[:, XQ:]


def _kv_prep(mem, gm, w, gk, name):
    def fn(mem, gm, w, gk):
        return _kv_fn(mem, gm, w, gk)
    M = mem.shape[0]
    return _rowwise(name, fn, [('c', mem), ('c', gm), ('c', w), ('c', gk)],
                    [('c', (M, XQ), F32), ('c', (M, XQ), F32)], 1)


def _kv_prep_bwd(mem, gm, w, gk, dk, dv, name):
    def fn(mem, gm, w, gk, dk, dv):
        _, vjp = jax.vjp(lambda a, b, c: _kv_fn(mem, a, b, c), gm, w, gk)
        return vjp((dk, dv))
    return _rowwise(name, fn, [('c', mem), ('c', gm), ('c', w), ('c', gk), ('c', dk), ('c', dv)],
                    [('c', gm.shape, F32), ('c', w.shape, BF16), ('c', gk.shape, F32)], 1)


def _forward_merge(x, mix, mix_kind, xq, gate, k, v, gq, wout, name, nblk, sub, host=None):
    def fn(x, mix, xq, gate, k, v, gq, wout):
        o = _merge(mix, xq, gate, k, v, gq)
        return (x + _dot(o.astype(BF16), wout),)
    L = x.shape[0]
    out = _rowwise(name, fn, [('r', x), (mix_kind, mix), ('r', xq), ('r', gate), ('c', k), ('c', v), ('c', gq),
                              ('c', wout)], [('r', (L, D_MODEL), F32)], nblk, sub, host=host)
    return out[0] if host is None else (out[0][0], out[1])


def _backward_merge(dx, mix, mix_kind, xq, gate, k, v, gq, wout, name, nblk, sub, host=None):
    def fn(dx, mix, xq, gate, k, v, gq, wout):
        g16 = dx.astype(BF16)
        do = _dot_nt(g16, wout)
        o, vjp = jax.vjp(_merge, mix, xq, gate, k, v, gq)
        dmix, dxq, dgate, dk, dv, dgq = vjp(do)
        return dmix, dxq, dgate, o, g16, dk, dv, dgq
    L = dx.shape[0]
    return _rowwise(
        name, fn,
        [('r', dx), (mix_kind, mix), ('r', xq), ('r', gate), ('c', k), ('c', v), ('c', gq), ('c', wout)],
        [('r', (L, PRIM), F32), ('r', (L, XQ), BF16), ('r', (L, BRANCH), BF16), ('t', (BRANCH, L), BF16),
         ('r', (L, D_MODEL), BF16), ('a', k.shape, F32), ('a', v.shape, F32), ('a', gq.shape, F32)], nblk, sub,
        host=host)


_MLA_IN = 3392
_MLA_IN_PAD = 3456


def _from_slots(g):
    _, k, n = g.shape
    return jnp.transpose(g, (1, 0, 2)).reshape(k, N_DEV * n)


def _to_slots(w):
    k = w.shape[0]
    return jnp.transpose(w.reshape(k, N_DEV, -1), (1, 0, 2))


def _uq_to_kernel(g):
    uq = _from_slots(g).reshape(Q_LORA, MLA_H, HD + ROPE)
    return jnp.concatenate([uq[:, :, :HD].reshape(Q_LORA, PRIM),
                            jnp.pad(uq[:, :, HD:], ((0, 0), (0, 0), (0, HD - ROPE))).reshape(Q_LORA, PRIM)], axis=1)


def _uq_from_kernel(d_w_q):
    uq = jnp.concatenate([d_w_q[:, :PRIM].reshape(Q_LORA, MLA_H, HD),
                          d_w_q[:, PRIM:].reshape(Q_LORA, MLA_H, HD)[:, :, :ROPE]], axis=2)
    return _to_slots(uq.reshape(Q_LORA, MLA_H * (HD + ROPE)))


def _mla_in_perm(w):
    return jnp.concatenate([w[:, :768], w[:, 832:], w[:, 768:832], jnp.zeros((w.shape[0], 64), w.dtype)], axis=1)


def _mla_in_unperm(w):
    return jnp.concatenate([w[:, :768], w[:, 3328:3392], w[:, 768:3328]], axis=1)


_SMALL = (("ln_gain", 2048), ("mem_norm", 2048), ("xq_norm", 256), ("xk_norm", 256), ("s5_lambda_re", 6144),
          ("s5_lambda_im", 6144), ("s5_log_step", 96), ("s5_b_re", 98304), ("s5_b_im", 98304), ("s5_c_re", 98304),
          ("s5_c_im", 98304), ("s5_d", 1536), ("mla_q_lora_norm", 512), ("mla_kv_lora_norm", 256),
          ("mla_q_nope_norm", 128), ("mla_k_nope_norm", 128), ("mla_q_rope_norm", 64), ("mla_k_rope_norm", 64))
_SMALL_ROWS = 432
_SMALL_OFF = {name: sum(n for _, n in _SMALL[:i]) for i, (name, _) in enumerate(_SMALL)}


def _pack_small(d):
    flat = jnp.concatenate([d[n].reshape(-1).astype(F32) for n, _ in _SMALL])
    return jnp.pad(flat, (0, _SMALL_ROWS * 1024 - flat.shape[0])).reshape(_SMALL_ROWS, 1024)


def _unpack_small(p, name, shape):
    off = _SMALL_OFF[name]
    return p.reshape(-1)[off:off + int(np.prod(shape))].reshape(shape)


_WEIGHTS = ('ln_gain', 'w_out', 'mem_norm', 'w_mem_kv', 'xq_norm', 'xk_norm', 's5_w_in', 's5_lambda_re',
            's5_lambda_im', 's5_log_step', 's5_b_re', 's5_b_im', 's5_c_re', 's5_c_im', 's5_d', 's5_w_glu', 'mla_w_in',
            'mla_q_lora_norm', 'mla_kv_lora_norm', 'mla_w_uq', 'mla_w_ukv', 'mla_q_nope_norm', 'mla_k_nope_norm',
            'mla_q_rope_norm', 'mla_k_rope_norm')
_BIG = ('w_out', 'w_mem_kv', 's5_w_in', 's5_w_glu', 'mla_w_in', 'mla_w_uq', 'mla_w_ukv')


def _pad128(g):
    return jnp.pad(g.reshape(1, -1), ((0, 0), (0, HD - g.shape[-1])))


def kernel(x, mem, positions, ln_gain, w_out, mem_norm, w_mem_kv, xq_norm, xk_norm, s5_w_in, s5_lambda_re, s5_lambda_im, s5_log_step, s5_b_re, s5_b_im, s5_c_re, s5_c_im, s5_d, s5_w_glu, mla_w_in, mla_q_lora_norm, mla_kv_lora_norm, mla_w_uq, mla_w_ukv, mla_q_nope_norm, mla_k_nope_norm, mla_q_rope_norm, mla_k_rope_norm, loss_target, m_ln_gain, m_w_out, m_mem_norm, m_w_mem_kv, m_xq_norm, m_xk_norm, m_s5_w_in, m_s5_lambda_re, m_s5_lambda_im, m_s5_log_step, m_s5_b_re, m_s5_b_im, m_s5_c_re, m_s5_c_im, m_s5_d, m_s5_w_glu, m_mla_w_in, m_mla_q_lora_norm, m_mla_kv_lora_norm, m_mla_w_uq, m_mla_w_ukv, m_mla_q_nope_norm, m_mla_k_nope_norm, m_mla_q_rope_norm, m_mla_k_rope_norm, v_ln_gain, v_w_out, v_mem_norm, v_w_mem_kv, v_xq_norm, v_xk_norm, v_s5_w_in, v_s5_lambda_re, v_s5_lambda_im, v_s5_log_step, v_s5_b_re, v_s5_b_im, v_s5_c_re, v_s5_c_im, v_s5_d, v_s5_w_glu, v_mla_w_in, v_mla_q_lora_norm, v_mla_kv_lora_norm, v_mla_w_uq, v_mla_w_ukv, v_mla_q_nope_norm, v_mla_k_nope_norm, v_mla_q_rope_norm, v_mla_k_rope_norm):
    weights = dict(ln_gain=ln_gain, w_out=w_out, mem_norm=mem_norm, w_mem_kv=w_mem_kv, xq_norm=xq_norm,
                   xk_norm=xk_norm, s5_w_in=s5_w_in, s5_lambda_re=s5_lambda_re, s5_lambda_im=s5_lambda_im,
                   s5_log_step=s5_log_step, s5_b_re=s5_b_re, s5_b_im=s5_b_im, s5_c_re=s5_c_re, s5_c_im=s5_c_im,
                   s5_d=s5_d, s5_w_glu=s5_w_glu, mla_w_in=mla_w_in, mla_q_lora_norm=mla_q_lora_norm,
                   mla_kv_lora_norm=mla_kv_lora_norm, mla_w_uq=mla_w_uq, mla_w_ukv=mla_w_ukv,
                   mla_q_nope_norm=mla_q_nope_norm, mla_k_nope_norm=mla_k_nope_norm,
                   mla_q_rope_norm=mla_q_rope_norm, mla_k_rope_norm=mla_k_rope_norm)
    m_in = dict(zip(_WEIGHTS, (m_ln_gain, m_w_out, m_mem_norm, m_w_mem_kv, m_xq_norm, m_xk_norm, m_s5_w_in,
                               m_s5_lambda_re, m_s5_lambda_im, m_s5_log_step, m_s5_b_re, m_s5_b_im, m_s5_c_re,
                               m_s5_c_im, m_s5_d, m_s5_w_glu, m_mla_w_in, m_mla_q_lora_norm, m_mla_kv_lora_norm,
                               m_mla_w_uq, m_mla_w_ukv, m_mla_q_nope_norm, m_mla_k_nope_norm, m_mla_q_rope_norm,
                               m_mla_k_rope_norm)))
    v_in = dict(zip(_WEIGHTS, (v_ln_gain, v_w_out, v_mem_norm, v_w_mem_kv, v_xq_norm, v_xk_norm, v_s5_w_in,
                               v_s5_lambda_re, v_s5_lambda_im, v_s5_log_step, v_s5_b_re, v_s5_b_im, v_s5_c_re,
                               v_s5_c_im, v_s5_d, v_s5_w_glu, v_mla_w_in, v_mla_q_lora_norm, v_mla_kv_lora_norm,
                               v_mla_w_uq, v_mla_w_ukv, v_mla_q_nope_norm, v_mla_k_nope_norm, v_mla_q_rope_norm,
                               v_mla_k_rope_norm)))

    x0 = x[0]
    mem0 = mem[0]
    target = loss_target[0]
    L = x0.shape[0]
    nblk, sub = 8, 1
    me = 4 * lax.axis_index("x") + 2 * lax.axis_index("y") + lax.axis_index("c")

    lora = jnp.pad(jnp.concatenate([mla_q_lora_norm, mla_kv_lora_norm], axis=1), ((0, 7), (0, HD - 96)))
    def gather(*shards):
        return _plan_all_gather([s.astype(BF16) for s in shards])

    (W_in_s5,) = _exchange_call(gather(s5_w_in[0]), "ag_s5_w_in")

    ln0, ln1 = ln_gain[0:1], ln_gain[1:2]
    gq0, gq1 = xq_norm[0:1], xq_norm[1:2]
    gk0, gk1 = xk_norm[0:1], xk_norm[1:2]
    gm0, gm1 = mem_norm[0:1], mem_norm[1:2]
    gqn, gkn = mla_q_nope_norm, mla_k_nope_norm
    gqr, gkr = _pad128(mla_q_rope_norm), _pad128(mla_k_rope_norm)

    lr3 = s5_lambda_re.reshape(S5_G, 1, S5_P)
    li3 = s5_lambda_im.reshape(S5_G, 1, S5_P)
    ls3 = s5_log_step.reshape(S5_G, 1, 1)
    btr = jnp.swapaxes(s5_b_re[0], 1, 2)
    bti = jnp.swapaxes(s5_b_im[0], 1, 2)
    a_r, a_i, bbr, bbi = _s5_params(lr3, li3, ls3, btr, bti)
    bm, bmt, cm, cmt = _s5_mats(bbr, bbi, s5_c_re[0], s5_c_im[0])
    a_r2 = a_r.reshape(1, S5_G * S5_P)
    a_i2 = a_i.reshape(1, S5_G * S5_P)
    cmask, rmat = _s5_compact_consts()

    half = ROPE // 2
    inv_freq = ROPE_THETA ** (-jnp.arange(half, dtype=F32) / half)
    invf = jnp.concatenate([inv_freq, inv_freq, jnp.zeros((HD - ROPE,), F32)]).reshape(1, HD)

    def rot_tables(pos, invf):
        ang = pos.astype(F32) * invf
        lane = lax.broadcasted_iota(jnp.int32, ang.shape, 1)
        c = jnp.where(lane < ROPE, jnp.cos(ang), 0.0)
        s = jnp.sin(ang)
        return c, jnp.where(lane < half, -s, 0.0), jnp.where((lane >= half) & (lane < ROPE), s, 0.0)

    tc, ts1, ts2 = _rowwise("rot_tables", rot_tables, [('r', positions.reshape(L, 1)), ('c', invf)],
                            [('r', (L, HD), F32)] * 3, nblk, sub)

    def in_s5(x, g, w):
        proj = _mm_slots(_rms(x, g, D_MODEL).astype(BF16), w)
        return proj[:, :PRIM], proj[:, PRIM:PRIM + XQ], proj[:, PRIM + XQ:]

    kh = D_MODEL // 2
    (u_s5, xq_a, gate_a), (G_mkv0,) = _rowwise(
        "s5_in", in_s5, [('r', x0), ('c', ln0), ('c', W_in_s5)],
        [('r', (L, PRIM), F32), ('r', (L, XQ), F32), ('r', (L, BRANCH), F32)], nblk, sub, host=gather(w_mem_kv[0]))
    (y_s5, s5_carry), (W_glu, G_in_mla_a) = _s5_fwd(u_s5, bm, cm, a_r2, a_i2, s5_d,
                                                    host=gather(s5_w_glu[0], mla_w_in[0, :kh]))

    def glu(y, w):
        z = _mm_slots(_gelu(y).astype(BF16), w)
        return (z[:, :PRIM] * _sigmoid(z[:, PRIM:]),)

    (y2,), (G_out0,) = _rowwise("s5_glu", glu, [('r', y_s5), ('c', W_glu)], [('r', (L, PRIM), F32)], nblk, sub,
                                host=gather(w_out[0]))
    W_mkv0 = G_mkv0.reshape(D_MODEL, 2 * XQ)
    k_a, v_a = _kv_prep(mem0, gm0, W_mkv0, gk0, "kv_prep0")
    x1, (G_in_mla_b,) = _forward_merge(
        x0, y2, 'r', xq_a, gate_a, k_a, v_a, gq0, G_out0.reshape(BRANCH, D_MODEL), "merge0", nblk, sub,
        host=gather(mla_w_in[0, kh:]))
    W_in_mla = _mla_in_perm(jnp.concatenate([_from_slots(G_in_mla_a), _from_slots(G_in_mla_b)], axis=0))

    def in_mla(x, g, w):
        proj = _dot(_rms(x, g, D_MODEL).astype(BF16), w)
        return proj[:, :512], proj[:, 512:768], proj[:, 768:1280], proj[:, 1280:3328], proj[:, 3328:]

    (c_q, c_kv, xq_b, gate_b, krp), (G_uq, W_kv, G_lora) = _rowwise(
        "mla_in", in_mla, [('r', x1), ('c', ln1), ('c', W_in_mla)],
        [('r', (L, Q_LORA), F32), ('r', (L, KV_LORA), F32), ('r', (L, XQ), F32), ('r', (L, BRANCH), F32),
         ('r', (L, HD), F32)], nblk, sub,
        host=_plan_all_gather([mla_w_uq[0].astype(BF16), mla_w_ukv[0].astype(BF16), lora]))
    W_q = _uq_to_kernel(G_uq)
    g_qlora = G_lora[:, 0, :64].reshape(1, Q_LORA)
    g_kvlora = G_lora[:, 0, 64:96].reshape(1, KV_LORA)

    def qkv(c_q, c_kv, krp, tc, ts1, ts2, gql, gkvl, wq, wkv, gqn, gkn, gqr, gkr):
        q = _dot(_rms(c_q, gql, Q_LORA).astype(BF16), wq)
        kv = _mm_slots(_rms(c_kv, gkvl, KV_LORA).astype(BF16), wkv)
        kp, v = _kv_post(kv, krp, gkn, gkr, tc, ts1, ts2)
        return _q_post(q, gqn, gqr, tc, ts1, ts2), kp, v

    qkv_consts = [('c', g_qlora), ('c', g_kvlora), ('c', W_q), ('c', W_kv), ('c', gqn), ('c', gkn), ('c', gqr),
                  ('c', gkr)]
    (q_pad, k_pad, v_h), (G_mkv1, G_out1) = _rowwise(
        "mla_qkv", qkv, [('r', c_q), ('r', c_kv), ('r', krp), ('r', tc), ('r', ts1), ('r', ts2)] + qkv_consts,
        [('r', (L, 2 * PRIM), BF16), ('r', (L, 2 * PRIM), BF16), ('r', (L, PRIM), BF16)], nblk, sub,
        host=gather(w_mem_kv[1], w_out[1]))
    W_out = (G_out0.reshape(BRANCH, D_MODEL), G_out1.reshape(BRANCH, D_MODEL))
    W_mkv = (W_mkv0, G_mkv1.reshape(D_MODEL, 2 * XQ))
    scale = (HD + ROPE) ** -0.5
    attn, lse = _attn_fwd(q_pad, k_pad, v_h, scale)
    k_b, v_b = _kv_prep(mem0, gm1, W_mkv[1], gk1, "kv_prep1")

    def merge_loss(x, mix, xq, gate, k, v, gq, wout, t):
        err = x + _dot(_merge(mix, xq, gate, k, v, gq).astype(BF16), wout) - t
        part = 0.5 * jnp.sum(jnp.sum(err * err, axis=-1, keepdims=True) * (1.0 / D_MODEL), axis=0, keepdims=True)
        return err * (1.0 / D_MODEL), jnp.broadcast_to(part, (1, HD))

    dx2, loss_part = _rowwise(
        "merge1_loss", merge_loss,
        [('r', x1), ('r', attn), ('r', xq_b), ('r', gate_b), ('c', k_b), ('c', v_b), ('c', gq1), ('c', W_out[1]),
         ('r', target)], [('r', (L, D_MODEL), F32), ('a', (1, HD), F32)], nblk, sub)

    dattn, dxq_b, dgate_b, o_b, g_b, dk_b, dv_b, dgq1 = _backward_merge(
        dx2, attn, 'r', xq_b, gate_b, k_b, v_b, gq1, W_out[1], "merge1_bwd", nblk, sub)
    dgm1, dW_mkv1, dgk1 = _kv_prep_bwd(mem0, gm1, W_mkv[1], gk1, dk_b, dv_b, "kv_prep1_bwd")
    dW_out1 = _matmul_tn(o_b, g_b, "dw_out1")
    dq_pad, dk_pad, dv_h = _attn_bwd(q_pad, k_pad, v_h, attn, lse, dattn, scale)

    def qkv_bwd(c_q, c_kv, krp, tc, ts1, ts2, dqp, dkp, dv, gql, gkvl, wq, wkv, gqn, gkn, gqr, gkr):
        cqn, vjp_qn = jax.vjp(lambda a, b: _rms(a, b, Q_LORA), c_q, gql)
        ckvn, vjp_kvn = jax.vjp(lambda a, b: _rms(a, b, KV_LORA), c_kv, gkvl)
        cqn16 = cqn.astype(BF16)
        ckvn16 = ckvn.astype(BF16)
        q = _dot(cqn16, wq)
        kv = _mm_slots(ckvn16, wkv)
        _, vjp_q = jax.vjp(lambda a, b, c: _q_post(a, b, c, tc, ts1, ts2), q, gqn, gqr)
        dq, dgqn, dgqr = vjp_q(dqp.astype(F32))
        _, vjp_kv = jax.vjp(lambda a, b, c, d: _kv_post(a, b, c, d, tc, ts1, ts2), kv, krp, gkn, gkr)
        dkv, dkrp, dgkn, dgkr = vjp_kv((dkp.astype(F32), dv.astype(F32)))
        dq16 = dq.astype(BF16)
        dkv16 = dkv.astype(BF16)
        dc_q, dgql = vjp_qn(_dot_nt(dq16, wq))
        dc_kv, dgkvl = vjp_kvn(_mm_slots_nt(dkv16, wkv))
        return dc_q, dc_kv, dkrp, cqn16, dq16, ckvn16, dkv16, dgql, dgkvl, dgqn, dgkn, dgqr, dgkr

    (dc_q, dc_kv, dkrp, cqn16, dq16, ckvn16, dkv16, dgql, dgkvl, dgqn, dgkn, dgqr, dgkr) = _rowwise(
        "mla_qkv_bwd", qkv_bwd,
        [('r', c_q), ('r', c_kv), ('r', krp), ('r', tc), ('r', ts1), ('r', ts2), ('r', dq_pad), ('r', dk_pad),
         ('r', dv_h)] + qkv_consts,
        [('r', (L, Q_LORA), BF16), ('r', (L, KV_LORA), BF16), ('r', (L, HD), BF16), ('t', (Q_LORA, L), BF16),
         ('r', (L, 2 * PRIM), BF16), ('t', (KV_LORA, L), BF16), ('r', (L, 2 * PRIM), BF16),
         ('a', (1, Q_LORA), F32), ('a', (1, KV_LORA), F32), ('a', (1, HD), F32), ('a', (1, HD), F32),
         ('a', (1, HD), F32), ('a', (1, HD), F32)], nblk, sub)
    dW_q = _matmul_tn(cqn16, dq16, "dw_uq")
    dW_kv = _matmul_tn_slots(ckvn16, dkv16, "dw_ukv")

    def in_bwd(x, dres, g, w, *dparts):
        dproj = jnp.concatenate(dparts, axis=-1).astype(BF16)
        xn, vjp = jax.vjp(lambda a, b: _rms(a, b, D_MODEL), x, g)
        dx, dg = vjp(_mm_slots_nt(dproj, w) if w.ndim == 3 else _dot_nt(dproj, w))
        return dx + dres, xn, dproj, dg

    dx1, xn1, dproj1, dln1 = _rowwise(
        "mla_in_bwd", in_bwd,
        [('r', x1), ('r', dx2), ('c', ln1), ('c', W_in_mla), ('r', dc_q), ('r', dc_kv), ('r', dxq_b), ('r', dgate_b),
         ('r', dkrp)],
        [('r', (L, D_MODEL), F32), ('t', (D_MODEL, L), BF16), ('r', (L, _MLA_IN_PAD), BF16), ('a', (1, D_MODEL), F32)],
        nblk, sub)
    dW_in_mla = _matmul_tn(xn1, dproj1, "dw_mla_in")

    grads1 = [dW_out1.reshape(N_DEV, 256, D_MODEL), dW_mkv1.reshape(N_DEV, 128, 2 * XQ),
              _to_slots(_mla_in_unperm(dW_in_mla)), _uq_from_kernel(dW_q), dW_kv]
    (dy2, dxq_a, dgate_a, o_a, g_a, dk_a, dv_a, dgq0), pair1 = _backward_merge(
        dx1, y2, 'r', xq_a, gate_a, k_a, v_a, gq0, W_out[0], "merge0_bwd", nblk, sub, host=_plan_pair(grads1))
    dgm0, dW_mkv0, dgk0 = _kv_prep_bwd(mem0, gm0, W_mkv[0], gk0, dk_a, dv_a, "kv_prep0_bwd")
    dW_out0 = _matmul_tn(o_a, g_a, "dw_out0")
    t1 = list(_pair_add(grads1, pair1, "rs_add_layer1"))

    def glu_bwd(y, dy2, w):
        h, vjp_h = jax.vjp(_gelu, y)
        h16 = h.astype(BF16)
        z = _mm_slots(h16, w)
        _, vjp_z = jax.vjp(lambda z: z[:, :PRIM] * _sigmoid(z[:, PRIM:]), z)
        dz16 = vjp_z(dy2)[0].astype(BF16)
        return vjp_h(_mm_slots_nt(dz16, w))[0], h16, dz16

    grads0 = [dW_out0.reshape(N_DEV, 256, D_MODEL), dW_mkv0.reshape(N_DEV, 128, 2 * XQ)]
    (dy_s5, h16, dz16), glu_hosted = _rowwise(
        "s5_glu_bwd", glu_bwd, [('r', y_s5), ('r', dy2), ('c', W_glu)],
        [('r', (L, PRIM), F32), ('t', (PRIM, L), BF16), ('r', (L, 2 * PRIM), BF16)], nblk, sub,
        host=_combine(_plan_chips(t1[2:]), _plan_pair(grads0)))
    recv_proj1, pair0 = glu_hosted[:3], glu_hosted[3:]
    dW_glu = _matmul_tn_slots(h16, dz16, "dw_glu")
    t0 = list(_pair_add(grads0 + [dW_glu], pair0 + list(_exchange_call(_plan_pair([dW_glu]), "rs_pair_glu")),
                        "rs_add_layer0"))
    (du_s5, dbc, dcc, dd, dar, dai), recv_rest = _s5_bwd(u_s5, dy_s5, s5_carry, bm, bmt, cmt, a_r2, a_i2, s5_d,
                                                        cmask, rmat, host=_plan_chips(t1[:2] + t0))
    early_recv = recv_rest[:2] + recv_proj1 + recv_rest[2:]
    dx0, xn0, dproj0, dln0 = _rowwise(
        "s5_in_bwd", in_bwd,
        [('r', x0), ('r', dx1), ('c', ln0), ('c', W_in_s5), ('r', du_s5), ('r', dxq_a),
         ('r', dgate_a)],
        [('r', (L, D_MODEL), F32), ('t', (D_MODEL, L), BF16), ('r', (L, 2 * BRANCH), BF16), ('a', (1, D_MODEL), F32)],
        nblk, sub)

    dbc4 = dbc.reshape(S5_G, S5_C, 2, S5_P)
    dcc4 = dcc.reshape(S5_G, S5_C, 2, S5_P)
    dlr, dli, dls, dbtr, dbti = _s5_params_bwd(
        lr3, li3, ls3, btr, bti, dar.reshape(S5_G, 1, S5_P), dai.reshape(S5_G, 1, S5_P), dbc4[:, :, 0], dbc4[:, :, 1])

    small_part = {
        "ln_gain": jnp.concatenate([dln0, dln1]), "mem_norm": jnp.concatenate([dgm0, dgm1]),
        "xq_norm": jnp.concatenate([dgq0, dgq1]), "xk_norm": jnp.concatenate([dgk0, dgk1]),
        "s5_lambda_re": dlr, "s5_lambda_im": dli, "s5_log_step": dls,
        "s5_b_re": jnp.swapaxes(dbtr, 1, 2), "s5_b_im": jnp.swapaxes(dbti, 1, 2),
        "s5_c_re": dcc4[:, :, 0], "s5_c_im": -dcc4[:, :, 1], "s5_d": dd,
        "mla_q_lora_norm": dgql, "mla_kv_lora_norm": dgkvl, "mla_q_nope_norm": dgqn, "mla_k_nope_norm": dgkn,
        "mla_q_rope_norm": dgqr[:, :ROPE], "mla_k_rope_norm": dgkr[:, :ROPE],
    }
    loss8 = jnp.pad(loss_part, ((0, 7), (0, 0)))
    dW_in_s5, (small_gath, loss_g) = _matmul_tn_slots(
        xn0, dproj0, "dw_s5_in", host=_plan_all_gather([_pack_small(small_part).astype(BF16), loss8]))

    late = [dW_in_s5]
    late_t = _pair_add(late, list(_exchange_call(_plan_pair(late), "rs_pair_late")), "rs_add_late")
    owners = [("w_out", 1), ("w_mem_kv", 1), ("mla_w_in", 0), ("mla_w_uq", 0), ("mla_w_ukv", 0), ("w_out", 0),
              ("w_mem_kv", 0), ("s5_w_glu", 0)]
    upd, late_recv = _updates_call(early_recv, [weights[n][i] for n, i in owners], [m_in[n][i] for n, i in owners],
                                   [v_in[n][i] for n, i in owners], "update_early", host=_plan_chips(late_t))
    owners.append(("s5_w_in", 0))
    upd.append(_sum_adamw(late_recv[0], s5_w_in[0], m_s5_w_in[0], v_s5_w_in[0], "update_s5_w_in"))
    grads, delta, new_m, new_v = {}, {}, {}, {}
    for n in _BIG:
        parts = [u for u, (o, _) in sorted(zip(upd, owners), key=lambda t: t[1][1]) if o == n]
        grads[n], delta[n], new_m[n], new_v[n] = (jnp.stack([p[j] for p in parts]) for j in range(4))

    gs, loss_sum = _small_sum(small_gath, loss_g, "small_sum")
    loss = loss_sum[0, 0]
    for n, _ in _SMALL:
        shape = weights[n].shape
        if n == "mla_q_lora_norm":
            grads[n] = lax.dynamic_slice(_unpack_small(gs, n, (Q_LORA,)), (me * 64,), (64,)).reshape(shape)
        elif n == "mla_kv_lora_norm":
            grads[n] = lax.dynamic_slice(_unpack_small(gs, n, (KV_LORA,)), (me * 32,), (32,)).reshape(shape)
        else:
            grads[n] = _unpack_small(gs, n, shape)

    def own(n, a):
        if a.ndim == 4:
            a = jnp.transpose(a, (0, 2, 3, 1))
        elif a.ndim == 3:
            a = jnp.transpose(a, (0, 2, 1))
        return a.reshape(a.shape[1:]) if a.ndim >= 3 else a

    def back(n, a):
        shape = weights[n].shape
        if len(shape) == 4:
            return jnp.transpose(a.reshape((1,) + a.shape), (0, 3, 1, 2))
        if len(shape) == 3:
            return jnp.transpose(a.reshape((1,) + a.shape), (0, 2, 1))
        return a.reshape(shape)

    wide = ("s5_b_re", "s5_b_im", "s5_c_re", "s5_c_im")
    for names, nb, call in (([n for n, _ in _SMALL if n not in wide], 1, "update_small"), (wide, 4, "update_s5_bc")):
        res = _adamw_multi([own(n, weights[n]) for n in names], [own(n, grads[n]) for n in names],
                           [own(n, m_in[n]) for n in names], [own(n, v_in[n]) for n in names], call, nb)
        for n, (dl, m2, v2) in zip(names, res):
            delta[n], new_m[n], new_v[n] = back(n, dl), back(n, m2), back(n, v2)
    return (loss, dx0[None], *[grads[n] for n in _WEIGHTS], *[delta[n] for n in _WEIGHTS],
            *[new_m[n] for n in _WEIGHTS], *[new_v[n] for n in _WEIGHTS])
```

```python
import functools
import math

import numpy as np
import jax
import jax.numpy as jnp
from jax import lax
from jax.experimental import pallas as pl
from jax.experimental.pallas import tpu as pltpu

F32 = jnp.float32
BF16 = jnp.bfloat16
EPS = 1e-6
NEG = float(np.finfo(np.float32).min)
MESH = pl.DeviceIdType.MESH

N_DEV = 8
D_MODEL = 1024
MEM_LEN = 256
XQ = 512
PRIM = 1536
BRANCH = 2048
X_HEADS = 4
HD = 128
S5_G = 96
S5_P = 64
S5_C = 16
S5_GB = 8
S5_W = S5_GB * S5_P
MLA_H = 12
ROPE = 64
Q_LORA = 512
KV_LORA = 256
ROPE_THETA = 10000.0

ADAM_LR = 0.001
ADAM_B1 = 0.9
ADAM_B2 = 0.999
ADAM_EPS = 1e-08
ADAM_WD = 0.01
ADAM_STEP = 10

VMEM_LIMIT = 56 * 1024 * 1024


def _dot(a, b):
    return jnp.dot(a, b, preferred_element_type=F32)


def _dot_nt(a, b):
    return lax.dot_general(a, b, (((1,), (1,)), ((), ())), preferred_element_type=F32)


def _dot_tn(a, b):
    return lax.dot_general(a, b, (((0,), (0,)), ((), ())), preferred_element_type=F32)


@jax.custom_vjp
def _mm(a, b):
    return _dot(a.astype(BF16), b.astype(BF16))


def _mm_fwd(a, b):
    return _mm(a, b), (a, b)


def _mm_bwd(res, g):
    a, b = res
    gb = g.astype(BF16)
    return _dot_nt(gb, b.astype(BF16)).astype(a.dtype), _dot_tn(a.astype(BF16), gb).astype(b.dtype)


_mm.defvjp(_mm_fwd, _mm_bwd)


@jax.custom_vjp
def _mm_nt(a, b):
    return _dot_nt(a.astype(BF16), b.astype(BF16))


def _mm_nt_fwd(a, b):
    return _mm_nt(a, b), (a, b)


def _mm_nt_bwd(res, g):
    a, b = res
    gb = g.astype(BF16)
    return _dot(gb, b.astype(BF16)).astype(a.dtype), _dot_tn(gb, a.astype(BF16)).astype(b.dtype)


_mm_nt.defvjp(_mm_nt_fwd, _mm_nt_bwd)


@jax.custom_vjp
def _softmax(s):
    m = jnp.max(s, axis=-1, keepdims=True)
    e = jnp.exp(s - m)
    return e / jnp.sum(e, axis=-1, keepdims=True)


def _softmax_fwd(s):
    p = _softmax(s)
    return p, p


def _softmax_bwd(p, g):
    return (p * (g - jnp.sum(p * g, axis=-1, keepdims=True)),)


_softmax.defvjp(_softmax_fwd, _softmax_bwd)


def _rms(x, g, n):
    ms = jnp.sum(x * x, axis=-1, keepdims=True) * (1.0 / n)
    return x * lax.rsqrt(ms + EPS) * g


def _sigmoid(x):
    return 1.0 / (1.0 + jnp.exp(-x))


def _silu(x):
    return x * _sigmoid(x)


def _gelu(x):
    c = math.sqrt(2.0 / math.pi)
    return 0.5 * x * (1.0 + jnp.tanh(c * (x + 0.044715 * (x * x * x))))


@jax.custom_vjp
def _rot(x, c, s1, s2):
    return x * c + pltpu.roll(x, 96, 1) * s1 + pltpu.roll(x, 32, 1) * s2


def _rot_fwd(x, c, s1, s2):
    return _rot(x, c, s1, s2), (c, s1, s2)


def _rot_bwd(res, g):
    c, s1, s2 = res
    dx = g * c + pltpu.roll(g * s1, 32, 1) + pltpu.roll(g * s2, 96, 1)
    return dx, jnp.zeros_like(c), jnp.zeros_like(s1), jnp.zeros_like(s2)


_rot.defvjp(_rot_fwd, _rot_bwd)


def _mem_attn(xq, k, v, gq):
    outs = []
    for h in range(X_HEADS):
        sl = slice(HD * h, HD * (h + 1))
        q = _rms(xq[:, sl], gq, HD)
        p = _softmax(_mm_nt(q, k[:, sl]) * (HD ** -0.5))
        outs.append(_mm(p, v[:, sl]))
    return jnp.concatenate(outs, axis=-1)


def _merge(mix, xq, gate, k, v, gq):
    return jnp.concatenate([mix, _mem_attn(xq, k, v, gq)], axis=-1) * _silu(gate)


def _q_post(q, gqn, gqr, c, s1, s2):
    pieces = []
    for h in range(MLA_H):
        pieces.append(_rms(q[:, HD * h:HD * (h + 1)], gqn, HD))
        pieces.append(_rot(_rms(q[:, PRIM + HD * h:PRIM + HD * (h + 1)], gqr, ROPE), c, s1, s2))
    return jnp.concatenate(pieces, axis=-1)


def _kv_post(kv, krp, gkn, gkr, c, s1, s2):
    kr = _rot(_rms(krp, gkr, ROPE), c, s1, s2)
    pieces, vals = [], []
    for h in range(MLA_H):
        pieces.append(_rms(kv[:, 2 * HD * h:2 * HD * h + HD], gkn, HD))
        pieces.append(kr)
        vals.append(kv[:, 2 * HD * h + HD:2 * HD * (h + 1)])
    return jnp.concatenate(pieces, axis=-1), jnp.concatenate(vals, axis=-1)


def _rowwise(name, fn, ins, outs, nblk, sub=1, host=None):
    n_in = len(ins)

    def spec(kind, shape):
        if kind == 'r':
            return pl.BlockSpec((shape[0] // nblk, shape[1]), lambda i: (i, 0))
        if kind == 't':
            return pl.BlockSpec((shape[0], shape[1] // nblk), lambda i: (0, i))
        zeros = (0,) * len(shape)
        return pl.BlockSpec(tuple(shape), lambda i: zeros)

    def body(*refs):
        i = pl.program_id(0)
        res = fn(*[r[...] for r in refs[:n_in]])
        for (kind, _, _), ref, val in zip(outs, refs[n_in:], res):
            if kind == 'a':
                @pl.when(i == 0)
                def _():
                    ref[...] = jnp.zeros_like(ref)
                ref[...] += val.astype(ref.dtype)
            elif kind == 't':
                ref[...] = val.astype(F32).T.astype(ref.dtype)
            else:
                ref[...] = val.astype(ref.dtype)

    res, hosted = _hosting_call(
        body, name, nblk, host, [a for _, a in ins], [spec(k, a.shape) for k, a in ins],
        [jax.ShapeDtypeStruct(tuple(s), d) for _, s, d in outs], [spec(k, s) for k, s, _ in outs], [])
    return res if host is None else (res, hosted)


def _matmul_tn(at, g, name, out_dtype=BF16):
    K, L = at.shape
    N = g.shape[1]
    tn = next(t for t in (512, 384, 256, 128) if N % t == 0)

    def body(a_ref, g_ref, o_ref):
        o_ref[...] = _dot(a_ref[...], g_ref[...]).astype(o_ref.dtype)

    return pl.pallas_call(
        body, name=name, grid=(N // tn,),
        in_specs=[pl.BlockSpec((K, L), lambda n: (0, 0)), pl.BlockSpec((L, tn), lambda n: (0, n))],
        out_specs=pl.BlockSpec((K, tn), lambda n: (0, n)),
        out_shape=jax.ShapeDtypeStruct((K, N), out_dtype),
        compiler_params=pltpu.CompilerParams(dimension_semantics=("arbitrary",), vmem_limit_bytes=VMEM_LIMIT),
    )(at, g)


def _matmul_tn_slots(at, g, name, host=None):
    K, L = at.shape
    n = g.shape[1] // N_DEV

    def body(a_ref, g_ref, o_ref):
        o_ref[...] = _dot(a_ref[...], g_ref[...]).astype(o_ref.dtype)

    res, hosted = _hosting_call(
        body, name, N_DEV, host, [at, g],
        [pl.BlockSpec((K, L), lambda d: (0, 0)), pl.BlockSpec((L, n), lambda d: (0, d))],
        [jax.ShapeDtypeStruct((N_DEV, K, n), BF16)], [pl.BlockSpec((None, K, n), lambda d: (d, 0, 0))], [])
    return res[0] if host is None else (res[0], hosted)


def _mm_slots(a16, w):
    return jnp.concatenate([_dot(a16, w[d]) for d in range(N_DEV)], axis=-1)


def _mm_slots_nt(g16, w):
    n = w.shape[2]
    out = _dot_nt(g16[:, 0:n], w[0])
    for d in range(1, N_DEV):
        out = out + _dot_nt(g16[:, d * n:(d + 1) * n], w[d])
    return out


class _Exchange:
    def __init__(self, ins, outs, scratch, start, finish):
        self.ins, self.outs, self.scratch, self.start, self.finish = ins, outs, scratch, start, finish


def _xyc():
    return lax.axis_index("x"), lax.axis_index("y"), lax.axis_index("c")


def _plan_all_gather(xs):
    n = len(xs)

    def build(x_refs, out_refs, sems):
        send_sems, recv_sems, local_sems = sems
        x, y, c = _xyc()

        def copies(k, block, to, own=False):
            slot = 4 * block[0] + 2 * block[1] + block[2]
            return [pltpu.make_async_remote_copy(
                src_ref=x_refs[a] if own else out_refs[a].at[slot], dst_ref=out_refs[a].at[slot],
                send_sem=send_sems.at[k * n + a], recv_sem=recv_sems.at[k * n + a], device_id=to,
                device_id_type=MESH) for a in range(n)]

        mine = [pltpu.make_async_copy(x_refs[a], out_refs[a].at[4 * x + 2 * y + c], local_sems.at[a])
                for a in range(n)]
        return copies, mine, (x, y, c), [(1 - x, y), (x, 1 - y), (1 - x, 1 - y)]

    def first_copies(copies, me, chips):
        x, y, c = me
        first = copies(0, me, (x, y, 1 - c), own=True)
        for j, chip in enumerate(chips):
            first += copies(1 + j, me, (*chip, c), own=True)
        return first

    def start(x_refs, out_refs, sems):
        copies, mine, me, chips = build(x_refs, out_refs, sems)
        for cp in mine + first_copies(copies, me, chips):
            cp.start()

    def finish(x_refs, out_refs, sems):
        copies, mine, me, chips = build(x_refs, out_refs, sems)
        x, y, c = me
        passed = []
        for j, chip in enumerate(chips):
            for cp in copies(1 + j, (*chip, c), me):
                cp.wait_recv()
            fwd = copies(4 + j, (*chip, c), (x, y, 1 - c))
            for cp in fwd:
                cp.start()
            passed += fwd
        for cp in copies(0, (x, y, 1 - c), me):
            cp.wait_recv()
        for j, chip in enumerate(chips):
            for cp in copies(4 + j, (*chip, 1 - c), me):
                cp.wait_recv()
        for cp in first_copies(copies, me, chips) + passed:
            cp.wait_send()
        for cp in mine:
            cp.wait()

    return _Exchange(list(xs), [jax.ShapeDtypeStruct((N_DEV,) + a.shape, a.dtype) for a in xs],
                     [pltpu.SemaphoreType.DMA((7 * n,)), pltpu.SemaphoreType.DMA((7 * n,)),
                      pltpu.SemaphoreType.DMA((n,))], start, finish)


_CHIPS = ((0, 0), (0, 1), (1, 0), (1, 1))


def _plan_pair(sends):
    n = len(sends)

    def build(s_refs, o_refs, sems):
        send_sems, recv_sems = sems
        x, y, c = _xyc()
        return [pltpu.make_async_remote_copy(
            src_ref=s_refs[a].at[4 * px + 2 * py + 1 - c], dst_ref=o_refs[a].at[j],
            send_sem=send_sems.at[j * n + a], recv_sem=recv_sems.at[j * n + a], device_id=(x, y, 1 - c),
            device_id_type=MESH) for j, (px, py) in enumerate(_CHIPS) for a in range(n)]

    def start(s_refs, o_refs, sems):
        for cp in build(s_refs, o_refs, sems):
            cp.start()

    def finish(s_refs, o_refs, sems):
        for cp in build(s_refs, o_refs, sems):
            cp.wait_recv()
            cp.wait_send()

    return _Exchange(list(sends), [jax.ShapeDtypeStruct((4,) + a.shape[1:], a.dtype) for a in sends],
                     [pltpu.SemaphoreType.DMA((4 * n,)), pltpu.SemaphoreType.DMA((4 * n,))], start, finish)


def _plan_chips(ts):
    n = len(ts)
    flips = ((1, 0), (0, 1), (1, 1))

    def build(t_refs, o_refs, sems):
        send_sems, recv_sems, local_sems = sems
        x, y, c = _xyc()
        mine = 2 * x + y
        local = [pltpu.make_async_copy(t_refs[a].at[mine], o_refs[a].at[mine], local_sems.at[a]) for a in range(n)]
        remote = []
        for k, (fx, fy) in enumerate(flips):
            px = 1 - x if fx else x
            py = 1 - y if fy else y
            remote += [pltpu.make_async_remote_copy(
                src_ref=t_refs[a].at[2 * px + py], dst_ref=o_refs[a].at[mine],
                send_sem=send_sems.at[k * n + a], recv_sem=recv_sems.at[k * n + a], device_id=(px, py, c),
                device_id_type=MESH) for a in range(n)]
        return local, remote

    def start(t_refs, o_refs, sems):
        local, remote = build(t_refs, o_refs, sems)
        for cp in local + remote:
            cp.start()

    def finish(t_refs, o_refs, sems):
        local, remote = build(t_refs, o_refs, sems)
        for cp in remote:
            cp.wait_recv()
        for cp in remote:
            cp.wait_send()
        for cp in local:
            cp.wait()

    return _Exchange(list(ts), [jax.ShapeDtypeStruct(a.shape, a.dtype) for a in ts],
                     [pltpu.SemaphoreType.DMA((3 * n,)), pltpu.SemaphoreType.DMA((3 * n,)),
                      pltpu.SemaphoreType.DMA((n,))], start, finish)


def _combine(*plans):
    def parts(refs, attr):
        out, at = [], 0
        for p in plans:
            n = len(getattr(p, attr))
            out.append(refs[at:at + n])
            at += n
        return out

    def run(half):
        def go(ins, outs, sems):
            for p, a, o, s in zip(plans, parts(ins, "ins"), parts(outs, "outs"), parts(sems, "scratch")):
                getattr(p, half)(a, o, s)
        return go

    return _Exchange(sum((p.ins for p in plans), []), sum((p.outs for p in plans), []),
                     sum((p.scratch for p in plans), []), run("start"), run("finish"))


def _exchange_call(plan, name):
    n = len(plan.ins)

    def body(*refs):
        ins, outs, sems = refs[:n], refs[n:2 * n], refs[2 * n:]
        plan.start(ins, outs, sems)
        plan.finish(ins, outs, sems)

    return pl.pallas_call(
        body, name=name, out_shape=plan.outs,
        in_specs=[pl.BlockSpec(memory_space=pl.ANY)] * n, out_specs=[pl.BlockSpec(memory_space=pl.ANY)] * n,
        scratch_shapes=plan.scratch,
    )(*plan.ins)


def _slab_spec(lead, rows, cols, nb):
    if rows % (nb * 16) == 0:
        return pl.BlockSpec((lead, rows // nb, cols), lambda i: (0, i, 0))
    if cols % (nb * 128) == 0:
        return pl.BlockSpec((lead, rows, cols // nb), lambda i: (0, 0, i))
    return pl.BlockSpec((lead, rows, cols), lambda i: (0, 0, 0))


def _pair_add(sends, fromsib, name):
    n = len(sends)
    nb = 8

    def body(*refs):
        c = lax.axis_index("c")
        for a in range(n):
            s_ref, f_ref, t_ref = refs[a], refs[n + a], refs[2 * n + a]
            for j in range(4):
                t_ref[j] = (s_ref[2 * j + c].astype(F32) + f_ref[j].astype(F32)).astype(t_ref.dtype)

    def spec(a, lead):
        return _slab_spec(lead, a.shape[1], a.shape[2], nb)

    return pl.pallas_call(
        body, name=name, grid=(nb,),
        in_specs=[spec(a, N_DEV) for a in sends] + [spec(a, 4) for a in fromsib],
        out_specs=[spec(a, 4) for a in fromsib],
        out_shape=[jax.ShapeDtypeStruct(a.shape, a.dtype) for a in fromsib],
        compiler_params=pltpu.CompilerParams(dimension_semantics=("arbitrary",), vmem_limit_bytes=VMEM_LIMIT),
    )(*sends, *fromsib)


def _adamw_vals(w, g, m, v):
    m2 = ADAM_B1 * m + (1.0 - ADAM_B1) * g
    v2 = ADAM_B2 * v + (1.0 - ADAM_B2) * (g * g)
    m_hat = m2 / (1.0 - ADAM_B1 ** ADAM_STEP)
    v_hat = v2 / (1.0 - ADAM_B2 ** ADAM_STEP)
    delta = -ADAM_LR * (m_hat / (jnp.sqrt(v_hat) + ADAM_EPS) + ADAM_WD * w)
    return delta, m2, v2


def _sum_adamw(recv, w, m, v, name):
    R, C = w.shape
    ns = recv.shape[0]
    br = next((t for t in (256, 128, 64, 32, 16) if R % t == 0), R)

    def body(r_ref, w_ref, m_ref, v_ref, g_ref, d_ref, m2_ref, v2_ref):
        g = r_ref[0].astype(F32)
        for d in range(1, ns):
            g = g + r_ref[d].astype(F32)
        dl, m2, v2 = _adamw_vals(w_ref[...], g, m_ref[...], v_ref[...])
        g_ref[...] = g
        d_ref[...] = dl
        m2_ref[...] = m2
        v2_ref[...] = v2

    spec = pl.BlockSpec((br, C), lambda i: (i, 0))
    return pl.pallas_call(
        body, name=name, grid=(R // br,),
        in_specs=[pl.BlockSpec((ns, br, C), lambda i: (0, i, 0)), spec, spec, spec], out_specs=[spec] * 4,
        out_shape=[jax.ShapeDtypeStruct((R, C), F32)] * 4,
        compiler_params=pltpu.CompilerParams(dimension_semantics=("arbitrary",)),
    )(recv, w, m, v)


def _updates_call(recvs, ws, ms, vs, name, host=None):
    n = len(recvs)
    nb = 8

    def body(*refs):
        for a in range(n):
            r_ref, w_ref, m_ref, v_ref = refs[a], refs[n + a], refs[2 * n + a], refs[3 * n + a]
            g_ref, d_ref, m2_ref, v2_ref = refs[4 * n + 4 * a:4 * n + 4 * a + 4]
            g = r_ref[0].astype(F32)
            for d in range(1, r_ref.shape[0]):
                g = g + r_ref[d].astype(F32)
            dl, m2, v2 = _adamw_vals(w_ref[...], g, m_ref[...], v_ref[...])
            g_ref[...] = g
            d_ref[...] = dl
            m2_ref[...] = m2
            v2_ref[...] = v2

    def spec3(r):
        return _slab_spec(r.shape[0], r.shape[1], r.shape[2], nb)

    def spec2(w):
        lead3 = _slab_spec(1, w.shape[0], w.shape[1], nb)
        block = lead3.block_shape[1:]
        if block[0] != w.shape[0]:
            return pl.BlockSpec(block, lambda i: (i, 0))
        if block[1] != w.shape[1]:
            return pl.BlockSpec(block, lambda i: (0, i))
        return pl.BlockSpec(block, lambda i: (0, 0))

    res, hosted = _hosting_call(
        body, name, nb, host, list(recvs) + list(ws) + list(ms) + list(vs),
        [spec3(r) for r in recvs] + [spec2(w) for w in ws] * 3,
        [jax.ShapeDtypeStruct(w.shape, F32) for w in ws for _ in range(4)],
        [spec2(w) for w in ws for _ in range(4)], [])
    return [res[4 * a:4 * a + 4] for a in range(n)], hosted


def _small_sum(gath, loss_g, name):
    _, R, C = gath.shape
    br = R // 3

    def body(g_ref, l_ref, go_ref, lo_ref):
        g = g_ref[0].astype(F32)
        lsum = l_ref[0]
        for d in range(1, N_DEV):
            g = g + g_ref[d].astype(F32)
            lsum = lsum + l_ref[d]
        go_ref[...] = g
        lo_ref[...] = lsum

    return pl.pallas_call(
        body, name=name, grid=(R // br,),
        in_specs=[pl.BlockSpec((N_DEV, br, C), lambda i: (0, i, 0)),
                  pl.BlockSpec((N_DEV, 8, HD), lambda i: (0, 0, 0))],
        out_specs=[pl.BlockSpec((br, C), lambda i: (i, 0)), pl.BlockSpec((8, HD), lambda i: (0, 0))],
        out_shape=[jax.ShapeDtypeStruct((R, C), F32), jax.ShapeDtypeStruct((8, HD), F32)],
        compiler_params=pltpu.CompilerParams(dimension_semantics=("arbitrary",)),
    )(gath, loss_g)


def _adamw_multi(ws, gs, ms, vs, name, nblk=1):
    n = len(ws)

    def body(*refs):
        for a in range(n):
            dl, m2, v2 = _adamw_vals(refs[a][...], refs[n + a][...], refs[2 * n + a][...], refs[3 * n + a][...])
            refs[4 * n + 3 * a][...] = dl
            refs[4 * n + 3 * a + 1][...] = m2
            refs[4 * n + 3 * a + 2][...] = v2

    def spec(x):
        rest = (0,) * (x.ndim - 1)
        return pl.BlockSpec((x.shape[0] // nblk,) + tuple(x.shape[1:]), lambda i: (i,) + rest)

    res = pl.pallas_call(
        body, name=name, grid=(nblk,),
        in_specs=[spec(w) for w in ws] * 4, out_specs=[spec(w) for w in ws for _ in range(3)],
        out_shape=[jax.ShapeDtypeStruct(w.shape, F32) for w in ws for _ in range(3)],
        compiler_params=pltpu.CompilerParams(dimension_semantics=("arbitrary",), vmem_limit_bytes=VMEM_LIMIT),
    )(*ws, *gs, *ms, *vs)
    return [res[3 * a:3 * a + 3] for a in range(n)]


def _s5_param_fn(lr, li, ls, btr, bti):
    step = jnp.exp(ls)
    er = jnp.exp(lr * step)
    ang = li * step
    ar = er * jnp.cos(ang)
    ai = er * jnp.sin(ang)
    nr = ar - 1.0
    den = lr * lr + li * li
    fr = (nr * lr + ai * li) / den
    fi = (ai * lr - nr * li) / den
    return ar, ai, fr * btr - fi * bti, fr * bti + fi * btr


def _s5_params(lr, li, ls, btr, bti):
    def body(lr_ref, li_ref, ls_ref, br_ref, bi_ref, ar_ref, ai_ref, bbr_ref, bbi_ref):
        ar, ai, bbr, bbi = _s5_param_fn(lr_ref[...], li_ref[...], ls_ref[...], br_ref[...], bi_ref[...])
        ar_ref[...] = ar
        ai_ref[...] = ai
        bbr_ref[...] = bbr
        bbi_ref[...] = bbi

    sd = jax.ShapeDtypeStruct
    return pl.pallas_call(
        body, name="s5_params",
        out_shape=[sd(lr.shape, F32), sd(lr.shape, F32), sd(btr.shape, F32), sd(btr.shape, F32)],
    )(lr, li, ls, btr, bti)


def _s5_params_bwd(lr, li, ls, btr, bti, dar, dai, dbbr, dbbi):
    def body(lr_ref, li_ref, ls_ref, br_ref, bi_ref, dar_ref, dai_ref, dbbr_ref, dbbi_ref,
             dlr_ref, dli_ref, dls_ref, dbr_ref, dbi_ref):
        _, vjp = jax.vjp(_s5_param_fn, lr_ref[...], li_ref[...], ls_ref[...], br_ref[...], bi_ref[...])
        dlr, dli, dls, dbr, dbi = vjp((dar_ref[...], dai_ref[...], dbbr_ref[...], dbbi_ref[...]))
        dlr_ref[...] = dlr
        dli_ref[...] = dli
        dls_ref[...] = dls
        dbr_ref[...] = dbr
        dbi_ref[...] = dbi

    sd = jax.ShapeDtypeStruct
    return pl.pallas_call(
        body, name="s5_params_bwd",
        out_shape=[sd(lr.shape, F32), sd(lr.shape, F32), sd(ls.shape, F32), sd(btr.shape, F32), sd(btr.shape, F32)],
    )(lr, li, ls, btr, bti, dar, dai, dbbr, dbbi)


def _cpow(ar, ai, n):
    assert n & (n - 1) == 0
    while n > 1:
        ar, ai = ar * ar - ai * ai, 2.0 * ar * ai
        n //= 2
    return ar, ai


def _scan(st, cr, ci, init, nk, reverse, store, prev=None):
    W = S5_W

    def step(j, carry):
        k = nk - 1 - j if reverse else j
        rows = pl.ds(pl.multiple_of(k * 8, 8), 8)
        sr, si = carry[0], carry[1]
        nsr = cr * sr - ci * si + st[rows, 0:W]
        nsi = cr * si + ci * sr + st[rows, W:2 * W]
        if store:
            st[rows, 0:W] = nsr
            st[rows, W:2 * W] = nsi
        if prev is None:
            return nsr, nsi
        prows = pl.ds(pl.multiple_of(jnp.maximum(k - 1, 0) * 8, 8), 8)
        w = jnp.where(k > 0, 1.0, 0.0).astype(F32)
        pr = prev[prows, 0:W] * w
        pi = prev[prows, W:2 * W] * w
        return nsr, nsi, carry[2] + nsr * pr + nsi * pi, carry[3] + nsi * pr - nsr * pi

    return lax.fori_loop(0, nk, step, init, unroll=2)


def _chain(fin, fr, fi, pr, pi, reverse):
    W = S5_W
    fin[:, 0:W] = fr
    fin[:, W:2 * W] = fi
    rowid = lax.broadcasted_iota(jnp.int32, (8, W), 0)
    cr = jnp.zeros((1, W), F32)
    ci = jnp.zeros((1, W), F32)
    init_r = jnp.zeros((8, W), F32)
    init_i = jnp.zeros((8, W), F32)
    for s in (range(7, -1, -1) if reverse else range(8)):
        init_r = jnp.where(rowid == s, cr, init_r)
        init_i = jnp.where(rowid == s, ci, init_i)
        lr = fin[s:s + 1, 0:W]
        li = fin[s:s + 1, W:2 * W]
        cr, ci = lr + pr * cr - pi * ci, li + pr * ci + pi * cr
    return init_r, init_i


def _full_scan(st, fin, ar, ai, nk, reverse, prev=None, carry_in=None, carry_out=None):
    W = S5_W
    cr = jnp.broadcast_to(ar, (8, W))
    ci = jnp.broadcast_to(-ai if reverse else ai, (8, W))
    z = jnp.zeros((8, W), F32)
    if carry_in is None:
        fr, fi = _scan(st, cr, ci, (z, z), nk, reverse, store=False)
        pr, pi = _cpow(ar, -ai if reverse else ai, nk)
        init = _chain(fin, fr, fi, pr, pi, reverse)
    else:
        init = (carry_in[:, 0:W], carry_in[:, W:2 * W])
    if carry_out is not None:
        carry_out[:, 0:W] = init[0]
        carry_out[:, W:2 * W] = init[1]
    if prev is None:
        return _scan(st, cr, ci, init, nk, reverse, store=True)
    return _scan(st, cr, ci, init + (z, z), nk, reverse, store=True, prev=prev)


def _s5_specs(L):
    W2 = 2 * S5_W
    GC = S5_GB * S5_C
    col = pl.BlockSpec((L, GC), lambda g: (0, g))
    vec = pl.BlockSpec((1, GC), lambda g: (0, g))
    avec = pl.BlockSpec((1, S5_W), lambda g: (0, g))
    bmat = pl.BlockSpec((None, GC, W2), lambda g: (g, 0, 0))
    cmat = pl.BlockSpec((None, W2, GC), lambda g: (g, 0, 0))
    return col, vec, avec, bmat, cmat


def _interleave(dst, src, nk):
    for s in range(8):
        dst[pl.ds(s, nk, stride=8), :] = src[s * nk:(s + 1) * nk, :]


def _deinterleave(dst, src, nk):
    for s in range(8):
        dst[s * nk:(s + 1) * nk, :] = src[pl.ds(s, nk, stride=8), :].astype(dst.dtype)


def _hosting_call(body, name, nsteps, host, ins, in_specs, outs, out_specs, scratch):
    grid = (nsteps,) if isinstance(nsteps, int) else tuple(nsteps)
    params = pltpu.CompilerParams(dimension_semantics=("arbitrary",) * len(grid), vmem_limit_bytes=VMEM_LIMIT)
    if host is None:
        res = pl.pallas_call(
            body, name=name, grid=grid, in_specs=in_specs, out_specs=out_specs, out_shape=outs,
            scratch_shapes=scratch, compiler_params=params,
        )(*ins)
        return list(res), []
    n_in, n_out, n_sc = len(ins), len(outs), len(scratch)
    h_in, h_out = len(host.ins), len(host.outs)

    def hosted(*refs):
        a = refs[:n_in]
        ha = refs[n_in:n_in + h_in]
        o = refs[n_in + h_in:n_in + h_in + n_out]
        ho = refs[n_in + h_in + n_out:n_in + h_in + n_out + h_out]
        sc = refs[n_in + h_in + n_out + h_out:n_in + h_in + n_out + h_out + n_sc]
        hs = refs[n_in + h_in + n_out + h_out + n_sc:]
        first = functools.reduce(jnp.logical_and, [pl.program_id(i) == 0 for i in range(len(grid))])
        last = functools.reduce(jnp.logical_and, [pl.program_id(i) == g - 1 for i, g in enumerate(grid)])

        @pl.when(first)
        def _():
            host.start(ha, ho, hs)

        body(*a, *o, *sc)

        @pl.when(last)
        def _():
            host.finish(ha, ho, hs)

    hbm = pl.BlockSpec(memory_space=pl.ANY)
    res = pl.pallas_call(
        hosted, name=name, grid=grid,
        in_specs=list(in_specs) + [hbm] * h_in, out_specs=list(out_specs) + [hbm] * h_out,
        out_shape=list(outs) + list(host.outs), scratch_shapes=list(scratch) + list(host.scratch),
        compiler_params=params,
    )(*ins, *host.ins)
    return list(res[:n_out]), list(res[n_out:])


def _s5_fwd(u, bm, cm, ar, ai, dvec, host=None):
    L = u.shape[0]
    nk = L // 8
    GC = S5_GB * S5_C
    nb = S5_G // S5_GB
    col, vec, avec, bmat, cmat = _s5_specs(L)

    def body(u_ref, b_ref, c_ref, ar_ref, ai_ref, d_ref, y_ref, carry_ref, st, fin, ui, yi):
        _interleave(ui, u_ref, nk)
        for r in range(8):
            rows = slice(r * nk, (r + 1) * nk)
            st[rows, :] = _dot(ui[rows, :].astype(BF16), b_ref[...])
        _full_scan(st, fin, ar_ref[...], ai_ref[...], nk, reverse=False, carry_out=carry_ref)
        for r in range(8):
            rows = slice(r * nk, (r + 1) * nk)
            yi[rows, :] = _dot(st[rows, :].astype(BF16), c_ref[...]) + d_ref[...] * ui[rows, :]
        _deinterleave(y_ref, yi, nk)

    return _hosting_call(
        body, "s5_fwd", nb, host,
        [u, bm, cm, ar, ai, dvec], [col, bmat, cmat, avec, avec, vec],
        [jax.ShapeDtypeStruct(u.shape, F32), jax.ShapeDtypeStruct((nb * 8, 2 * S5_W), F32)],
        [col, pl.BlockSpec((8, 2 * S5_W), lambda g: (g, 0))],
        [pltpu.VMEM((L, 2 * S5_W), F32), pltpu.VMEM((8, 2 * S5_W), F32), pltpu.VMEM((L, GC), F32),
         pltpu.VMEM((L, GC), F32)])


def _s5_bwd(u, dy, carry, bm, bmt, cmt, ar, ai, dvec, mask, rmat, host=None):
    L = u.shape[0]
    nk = L // 8
    W = S5_W
    GC = S5_GB * S5_C
    col, vec, avec, bmat, cmat = _s5_specs(L)
    hi = lax.Precision.HIGHEST

    def body(u_ref, dy_ref, carry_ref, b_ref, bt_ref, ct_ref, ar_ref, ai_ref, d_ref, mask_ref, r_ref,
             du_ref, db_ref, dc_ref, dd_ref, dar_ref, dai_ref, sa, sb, fin, ui, dyi, dui):
        ar = ar_ref[...]
        ai = ai_ref[...]
        _interleave(ui, u_ref, nk)
        _interleave(dyi, dy_ref, nk)
        for r in range(8):
            rows = slice(r * nk, (r + 1) * nk)
            sa[rows, :] = _dot(ui[rows, :].astype(BF16), b_ref[...])
            sb[rows, :] = _dot(dyi[rows, :].astype(BF16), ct_ref[...])
        _full_scan(sa, fin, ar, ai, nk, reverse=False, carry_in=carry_ref)
        gr, gi, accr, acci = _full_scan(sb, fin, ar, ai, nk, reverse=True, prev=sa)
        rowid = lax.broadcasted_iota(jnp.int32, (8, W), 0)
        last = pl.ds((nk - 1) * 8, 8)
        pr = jnp.where(rowid == 0, 0.0, pltpu.roll(sa[last, 0:W], 1, 0))
        pi = jnp.where(rowid == 0, 0.0, pltpu.roll(sa[last, W:2 * W], 1, 0))
        accr = accr + gr * pr + gi * pi
        acci = acci + gi * pr - gr * pi
        dar_ref[...] = jnp.sum(accr, axis=0, keepdims=True)
        dai_ref[...] = jnp.sum(acci, axis=0, keepdims=True)
        dbf = jnp.zeros((GC, 2 * W), F32)
        dcf = jnp.zeros((GC, 2 * W), F32)
        dd = jnp.zeros((1, GC), F32)
        for r in range(8):
            rows = slice(r * nk, (r + 1) * nk)
            ub = ui[rows, :]
            dyb = dyi[rows, :]
            gb = sb[rows, :].astype(BF16)
            dui[rows, :] = _dot(gb, bt_ref[...]) + d_ref[...] * dyb
            dbf = dbf + _dot_tn(ub.astype(BF16), gb)
            dcf = dcf + _dot_tn(dyb.astype(BF16), sa[rows, :].astype(BF16))
            dd = dd + jnp.sum(dyb * ub, axis=0, keepdims=True)
        db_ref[...] = jnp.dot(dbf * mask_ref[...], r_ref[...], precision=hi, preferred_element_type=F32)
        dc_ref[...] = jnp.dot(dcf * mask_ref[...], r_ref[...], precision=hi, preferred_element_type=F32)
        dd_ref[...] = dd
        _deinterleave(du_ref, dui, nk)

    cmp_spec = pl.BlockSpec((GC, 2 * S5_P), lambda g: (g, 0))
    whole = lambda shape: pl.BlockSpec(shape, lambda g: (0, 0))
    sd = jax.ShapeDtypeStruct
    return _hosting_call(
        body, "s5_bwd", S5_G // S5_GB, host,
        [u, dy, carry, bm, bmt, cmt, ar, ai, dvec, mask, rmat],
        [col, col, pl.BlockSpec((8, 2 * W), lambda g: (g, 0)), bmat, cmat, bmat, avec, avec, vec, whole(mask.shape),
         whole(rmat.shape)],
        [sd(u.shape, BF16), sd((S5_G * S5_C, 2 * S5_P), F32), sd((S5_G * S5_C, 2 * S5_P), F32),
         sd((1, PRIM), F32), sd((1, S5_G * S5_P), F32), sd((1, S5_G * S5_P), F32)],
        [col, cmp_spec, cmp_spec, vec, avec, avec],
        [pltpu.VMEM((L, 2 * W), F32), pltpu.VMEM((L, 2 * W), F32), pltpu.VMEM((8, 2 * W), F32),
         pltpu.VMEM((L, GC), F32), pltpu.VMEM((L, GC), F32), pltpu.VMEM((L, GC), F32)])


def _s5_mats(bbr, bbi, cre, cim):
    nb = S5_G // S5_GB
    eye = jnp.eye(S5_GB, dtype=F32)
    bb = jnp.stack([bbr, bbi], axis=2).reshape(nb, S5_GB, S5_C, 2, S5_P)
    bm = jnp.einsum('ngcrp,gh->ngcrhp', bb, eye).reshape(nb, S5_GB * S5_C, 2 * S5_W)
    cc = jnp.stack([cre, -cim], axis=2).reshape(nb, S5_GB, S5_C, 2, S5_P)
    cmt = jnp.einsum('ngcrp,gh->ngcrhp', cc, eye).reshape(nb, S5_GB * S5_C, 2 * S5_W)
    return (bm.astype(BF16), jnp.swapaxes(bm, 1, 2).astype(BF16),
            jnp.swapaxes(cmt, 1, 2).astype(BF16), cmt.astype(BF16))


def _s5_compact_consts():
    g_row = np.arange(S5_GB * S5_C) // S5_C
    col = np.arange(2 * S5_W)
    g_col = (col % S5_W) // S5_P
    mask = (g_row[:, None] == g_col[None, :]).astype(np.float32)
    tgt = (col // S5_W) * S5_P + col % S5_P
    rmat = (tgt[:, None] == np.arange(2 * S5_P)[None, :]).astype(np.float32)
    return jnp.asarray(mask), jnp.asarray(rmat)


def _attn_scores(q_ref, k_ref, qb, bq, scale):
    ext = (qb + 1) * bq
    s = _dot_nt(q_ref[qb * bq:ext, :], k_ref[0:ext, :]) * scale
    qpos = lax.broadcasted_iota(jnp.int32, (bq, bq), 0)
    kpos = lax.broadcasted_iota(jnp.int32, (bq, bq), 1)
    diag = jnp.where(kpos <= qpos, s[:, ext - bq:], NEG)
    return diag if qb == 0 else jnp.concatenate([s[:, :ext - bq], diag], axis=-1)


def _attn_fwd(qp, kp, v, scale):
    L = qp.shape[0]
    bq = min(256, L)

    def body(q_ref, k_ref, v_ref, o_ref, lse_ref):
        for qb in range(L // bq):
            rows = slice(qb * bq, (qb + 1) * bq)
            s = _attn_scores(q_ref, k_ref, qb, bq, scale)
            m = jnp.max(s, axis=-1, keepdims=True)
            e = jnp.exp(s - m)
            l = jnp.sum(e, axis=-1, keepdims=True)
            o_ref[rows, :] = _dot(e.astype(BF16), v_ref[0:(qb + 1) * bq, :]) / l
            lse_ref[rows, :] = jnp.broadcast_to(m + jnp.log(l), (bq, HD))

    blk = pl.BlockSpec((L, HD), lambda h: (0, h))
    wide = pl.BlockSpec((L, 2 * HD), lambda h: (0, h))
    return pl.pallas_call(
        body, name="mla_attn_fwd", grid=(MLA_H,),
        in_specs=[wide, wide, blk], out_specs=[blk, blk],
        out_shape=[jax.ShapeDtypeStruct((L, MLA_H * HD), F32)] * 2,
        compiler_params=pltpu.CompilerParams(dimension_semantics=("arbitrary",), vmem_limit_bytes=VMEM_LIMIT),
    )(qp, kp, v)


def _attn_bwd(qp, kp, v, o, lse, do, scale):
    L = qp.shape[0]
    bq = min(256, L)
    nq = L // bq

    def body(q_ref, k_ref, v_ref, o_ref, lse_ref, do_ref, dq_ref, dk_ref, dv_ref, dk_acc, dv_acc):
        dk_acc[...] = jnp.zeros_like(dk_acc)
        dv_acc[...] = jnp.zeros_like(dv_acc)
        for qb in range(nq):
            rows = slice(qb * bq, (qb + 1) * bq)
            ext = (qb + 1) * bq
            do = do_ref[rows, :]
            dob = do.astype(BF16)
            p = jnp.exp(_attn_scores(q_ref, k_ref, qb, bq, scale) - lse_ref[rows, 0:1])
            dp = _dot_nt(dob, v_ref[0:ext, :])
            dsum = jnp.sum(do * o_ref[rows, :], axis=-1, keepdims=True)
            ds = (p * (dp - dsum) * scale).astype(BF16)
            dq_ref[rows, :] = _dot(ds, k_ref[0:ext, :]).astype(dq_ref.dtype)
            dk_acc[0:ext, :] += _dot_tn(ds, q_ref[rows, :])
            dv_acc[0:ext, :] += _dot_tn(p.astype(BF16), dob)
        dk_ref[...] = dk_acc[...].astype(dk_ref.dtype)
        dv_ref[...] = dv_acc[...].astype(dv_ref.dtype)

    sd = jax.ShapeDtypeStruct
    blk = pl.BlockSpec((L, HD), lambda h: (0, h))
    wide = pl.BlockSpec((L, 2 * HD), lambda h: (0, h))
    return pl.pallas_call(
        body, name="mla_attn_bwd", grid=(MLA_H,),
        in_specs=[wide, wide, blk, blk, blk, blk], out_specs=[wide, wide, blk],
        out_shape=[sd((L, MLA_H * 2 * HD), BF16), sd((L, MLA_H * 2 * HD), BF16), sd((L, MLA_H * HD), BF16)],
        scratch_shapes=[pltpu.VMEM((L, 2 * HD), F32), pltpu.VMEM((L, HD), F32)],
        compiler_params=pltpu.CompilerParams(dimension_semantics=("arbitrary",), vmem_limit_bytes=VMEM_LIMIT),
    )(qp, kp, v, o, lse, do)


def _kv_fn(mem, gm, w, gk):
    kv = _mm(_rms(mem, gm, D_MODEL), w)
    k = jnp.concatenate([_rms(kv[:, HD * h:HD * (h + 1)], gk, HD) for h in range(X_HEADS)], axis=-1)
    return k, kv[:, XQ:]


def _kv_prep(mem, gm, w, gk, name):
    def fn(mem, gm, w, gk):
        return _kv_fn(mem, gm, w, gk)
    M = mem.shape[0]
    return _rowwise(name, fn, [('c', mem), ('c', gm), ('c', w), ('c', gk)],
                    [('c', (M, XQ), F32), ('c', (M, XQ), F32)], 1)


def _kv_prep_bwd(mem, gm, w, gk, dk, dv, name):
    def fn(mem, gm, w, gk, dk, dv):
        _, vjp = jax.vjp(lambda a, b, c: _kv_fn(mem, a, b, c), gm, w, gk)
        return vjp((dk, dv))
    return _rowwise(name, fn, [('c', mem), ('c', gm), ('c', w), ('c', gk), ('c', dk), ('c', dv)],
                    [('c', gm.shape, F32), ('c', w.shape, BF16), ('c', gk.shape, F32)], 1)


def _forward_merge(x, mix, mix_kind, xq, gate, k, v, gq, wout, name, nblk, sub, host=None):
    def fn(x, mix, xq, gate, k, v, gq, wout):
        o = _merge(mix, xq, gate, k, v, gq)
        return (x + _dot(o.astype(BF16), wout),)
    L = x.shape[0]
    out = _rowwise(name, fn, [('r', x), (mix_kind, mix), ('r', xq), ('r', gate), ('c', k), ('c', v), ('c', gq),
                              ('c', wout)], [('r', (L, D_MODEL), F32)], nblk, sub, host=host)
    return out[0] if host is None else (out[0][0], out[1])


def _backward_merge(dx, mix, mix_kind, xq, gate, k, v, gq, wout, name, nblk, sub, host=None):
    def fn(dx, mix, xq, gate, k, v, gq, wout):
        g16 = dx.astype(BF16)
        do = _dot_nt(g16, wout)
        o, vjp = jax.vjp(_merge, mix, xq, gate, k, v, gq)
        dmix, dxq, dgate, dk, dv, dgq = vjp(do)
        return dmix, dxq, dgate, o, g16, dk, dv, dgq
    L = dx.shape[0]
    return _rowwise(
        name, fn,
        [('r', dx), (mix_kind, mix), ('r', xq), ('r', gate), ('c', k), ('c', v), ('c', gq), ('c', wout)],
        [('r', (L, PRIM), F32), ('r', (L, XQ), BF16), ('r', (L, BRANCH), BF16), ('t', (BRANCH, L), BF16),
         ('r', (L, D_MODEL), BF16), ('a', k.shape, F32), ('a', v.shape, F32), ('a', gq.shape, F32)], nblk, sub,
        host=host)


_MLA_IN = 3392
_MLA_IN_PAD = 3456


def _uq_rows(wt):
    r = wt.reshape(MLA_H, HD + ROPE, wt.shape[1])
    return jnp.concatenate([r[:, :HD].reshape(PRIM, -1),
                            jnp.pad(r[:, HD:], ((0, 0), (0, HD - ROPE), (0, 0))).reshape(PRIM, -1)], axis=0)


def _uq_rows_back(wt):
    nope = wt[:PRIM].reshape(MLA_H, HD, -1)
    rope = wt[PRIM:].reshape(MLA_H, HD, -1)[:, :ROPE]
    return jnp.concatenate([nope, rope], axis=1).reshape(MLA_H * (HD + ROPE), -1)


def _mla_in_rows(wt):
    return jnp.concatenate([wt[:768], wt[832:], wt[768:832], jnp.zeros((64, wt.shape[1]), wt.dtype)], axis=0)


def _mla_in_rows_back(wt):
    return jnp.concatenate([wt[:768], wt[3328:3392], wt[768:3328]], axis=0)


_SMALL = (("ln_gain", 2048), ("mem_norm", 2048), ("xq_norm", 256), ("xk_norm", 256), ("s5_lambda_re", 6144),
          ("s5_lambda_im", 6144), ("s5_log_step", 96), ("s5_b_re", 98304), ("s5_b_im", 98304), ("s5_c_re", 98304),
          ("s5_c_im", 98304), ("s5_d", 1536), ("mla_q_lora_norm", 512), ("mla_kv_lora_norm", 256),
          ("mla_q_nope_norm", 128), ("mla_k_nope_norm", 128), ("mla_q_rope_norm", 64), ("mla_k_rope_norm", 64))
_SMALL_ROWS = 432
_SMALL_OFF = {name: sum(n for _, n in _SMALL[:i]) for i, (name, _) in enumerate(_SMALL)}


def _pack_small(d):
    flat = jnp.concatenate([d[n].reshape(-1).astype(F32) for n, _ in _SMALL])
    return jnp.pad(flat, (0, _SMALL_ROWS * 1024 - flat.shape[0])).reshape(_SMALL_ROWS, 1024)


def _unpack_small(p, name, shape):
    off = _SMALL_OFF[name]
    return p.reshape(-1)[off:off + int(np.prod(shape))].reshape(shape)


_WEIGHTS = ('ln_gain', 'w_out', 'mem_norm', 'w_mem_kv', 'xq_norm', 'xk_norm', 's5_w_in', 's5_lambda_re',
            's5_lambda_im', 's5_log_step', 's5_b_re', 's5_b_im', 's5_c_re', 's5_c_im', 's5_d', 's5_w_glu', 'mla_w_in',
            'mla_q_lora_norm', 'mla_kv_lora_norm', 'mla_w_uq', 'mla_w_ukv', 'mla_q_nope_norm', 'mla_k_nope_norm',
            'mla_q_rope_norm', 'mla_k_rope_norm')
_BIG = ('w_out', 'w_mem_kv', 's5_w_in', 's5_w_glu', 'mla_w_in', 'mla_w_uq', 'mla_w_ukv')


def _pad128(g):
    return jnp.pad(g.reshape(1, -1), ((0, 0), (0, HD - g.shape[-1])))


def kernel(x, mem, positions, ln_gain, w_out, mem_norm, w_mem_kv, xq_norm, xk_norm, s5_w_in, s5_lambda_re, s5_lambda_im, s5_log_step, s5_b_re, s5_b_im, s5_c_re, s5_c_im, s5_d, s5_w_glu, mla_w_in, mla_q_lora_norm, mla_kv_lora_norm, mla_w_uq, mla_w_ukv, mla_q_nope_norm, mla_k_nope_norm, mla_q_rope_norm, mla_k_rope_norm, loss_target, m_ln_gain, m_w_out, m_mem_norm, m_w_mem_kv, m_xq_norm, m_xk_norm, m_s5_w_in, m_s5_lambda_re, m_s5_lambda_im, m_s5_log_step, m_s5_b_re, m_s5_b_im, m_s5_c_re, m_s5_c_im, m_s5_d, m_s5_w_glu, m_mla_w_in, m_mla_q_lora_norm, m_mla_kv_lora_norm, m_mla_w_uq, m_mla_w_ukv, m_mla_q_nope_norm, m_mla_k_nope_norm, m_mla_q_rope_norm, m_mla_k_rope_norm, v_ln_gain, v_w_out, v_mem_norm, v_w_mem_kv, v_xq_norm, v_xk_norm, v_s5_w_in, v_s5_lambda_re, v_s5_lambda_im, v_s5_log_step, v_s5_b_re, v_s5_b_im, v_s5_c_re, v_s5_c_im, v_s5_d, v_s5_w_glu, v_mla_w_in, v_mla_q_lora_norm, v_mla_kv_lora_norm, v_mla_w_uq, v_mla_w_ukv, v_mla_q_nope_norm, v_mla_k_nope_norm, v_mla_q_rope_norm, v_mla_k_rope_norm):
    weights = dict(ln_gain=ln_gain, w_out=w_out, mem_norm=mem_norm, w_mem_kv=w_mem_kv, xq_norm=xq_norm,
                   xk_norm=xk_norm, s5_w_in=s5_w_in, s5_lambda_re=s5_lambda_re, s5_lambda_im=s5_lambda_im,
                   s5_log_step=s5_log_step, s5_b_re=s5_b_re, s5_b_im=s5_b_im, s5_c_re=s5_c_re, s5_c_im=s5_c_im,
                   s5_d=s5_d, s5_w_glu=s5_w_glu, mla_w_in=mla_w_in, mla_q_lora_norm=mla_q_lora_norm,
                   mla_kv_lora_norm=mla_kv_lora_norm, mla_w_uq=mla_w_uq, mla_w_ukv=mla_w_ukv,
                   mla_q_nope_norm=mla_q_nope_norm, mla_k_nope_norm=mla_k_nope_norm,
                   mla_q_rope_norm=mla_q_rope_norm, mla_k_rope_norm=mla_k_rope_norm)
    m_in = dict(zip(_WEIGHTS, (m_ln_gain, m_w_out, m_mem_norm, m_w_mem_kv, m_xq_norm, m_xk_norm, m_s5_w_in,
                               m_s5_lambda_re, m_s5_lambda_im, m_s5_log_step, m_s5_b_re, m_s5_b_im, m_s5_c_re,
                               m_s5_c_im, m_s5_d, m_s5_w_glu, m_mla_w_in, m_mla_q_lora_norm, m_mla_kv_lora_norm,
                               m_mla_w_uq, m_mla_w_ukv, m_mla_q_nope_norm, m_mla_k_nope_norm, m_mla_q_rope_norm,
                               m_mla_k_rope_norm)))
    v_in = dict(zip(_WEIGHTS, (v_ln_gain, v_w_out, v_mem_norm, v_w_mem_kv, v_xq_norm, v_xk_norm, v_s5_w_in,
                               v_s5_lambda_re, v_s5_lambda_im, v_s5_log_step, v_s5_b_re, v_s5_b_im, v_s5_c_re,
                               v_s5_c_im, v_s5_d, v_s5_w_glu, v_mla_w_in, v_mla_q_lora_norm, v_mla_kv_lora_norm,
                               v_mla_w_uq, v_mla_w_ukv, v_mla_q_nope_norm, v_mla_k_nope_norm, v_mla_q_rope_norm,
                               v_mla_k_rope_norm)))

    x0 = x[0]
    mem0 = mem[0]
    target = loss_target[0]
    L = x0.shape[0]
    nblk, sub = 8, 1
    me = 4 * lax.axis_index("x") + 2 * lax.axis_index("y") + lax.axis_index("c")

    lora = jnp.pad(jnp.concatenate([mla_q_lora_norm, mla_kv_lora_norm], axis=1), ((0, 7), (0, HD - 96)))
    def gather(*shards):
        return _plan_all_gather([s.astype(BF16) for s in shards])

    (W_in_s5,) = _exchange_call(gather(s5_w_in[0]), "ag_s5_w_in")

    ln0, ln1 = ln_gain[0:1], ln_gain[1:2]
    gq0, gq1 = xq_norm[0:1], xq_norm[1:2]
    gk0, gk1 = xk_norm[0:1], xk_norm[1:2]
    gm0, gm1 = mem_norm[0:1], mem_norm[1:2]
    gqn, gkn = mla_q_nope_norm, mla_k_nope_norm
    gqr, gkr = _pad128(mla_q_rope_norm), _pad128(mla_k_rope_norm)

    lr3 = s5_lambda_re.reshape(S5_G, 1, S5_P)
    li3 = s5_lambda_im.reshape(S5_G, 1, S5_P)
    ls3 = s5_log_step.reshape(S5_G, 1, 1)
    btr = jnp.swapaxes(s5_b_re[0], 1, 2)
    bti = jnp.swapaxes(s5_b_im[0], 1, 2)
    a_r, a_i, bbr, bbi = _s5_params(lr3, li3, ls3, btr, bti)
    bm, bmt, cm, cmt = _s5_mats(bbr, bbi, s5_c_re[0], s5_c_im[0])
    a_r2 = a_r.reshape(1, S5_G * S5_P)
    a_i2 = a_i.reshape(1, S5_G * S5_P)
    cmask, rmat = _s5_compact_consts()

    half = ROPE // 2
    inv_freq = ROPE_THETA ** (-jnp.arange(half, dtype=F32) / half)
    invf = jnp.concatenate([inv_freq, inv_freq, jnp.zeros((HD - ROPE,), F32)]).reshape(1, HD)

    def rot_tables(pos, invf):
        ang = pos.astype(F32) * invf
        lane = lax.broadcasted_iota(jnp.int32, ang.shape, 1)
        c = jnp.where(lane < ROPE, jnp.cos(ang), 0.0)
        s = jnp.sin(ang)
        return c, jnp.where(lane < half, -s, 0.0), jnp.where((lane >= half) & (lane < ROPE), s, 0.0)

    tc, ts1, ts2 = _rowwise("rot_tables", rot_tables, [('r', positions.reshape(L, 1)), ('c', invf)],
                            [('r', (L, HD), F32)] * 3, nblk, sub)

    def in_s5(x, g, w):
        proj = _mm_slots(_rms(x, g, D_MODEL).astype(BF16), w)
        return proj[:, :PRIM], proj[:, PRIM:PRIM + XQ], proj[:, PRIM + XQ:]

    kh = D_MODEL // 2
    (u_s5, xq_a, gate_a), (G_mkv0,) = _rowwise(
        "s5_in", in_s5, [('r', x0), ('c', ln0), ('c', W_in_s5)],
        [('r', (L, PRIM), F32), ('r', (L, XQ), F32), ('r', (L, BRANCH), F32)], nblk, sub, host=gather(w_mem_kv[0]))
    wt_in = jnp.transpose(mla_w_in[0])
    (y_s5, s5_carry), (W_glu, G_in_mla_a) = _s5_fwd(u_s5, bm, cm, a_r2, a_i2, s5_d,
                                                    host=gather(s5_w_glu[0], wt_in[:, :kh]))

    def glu(y, w):
        z = _mm_slots(_gelu(y).astype(BF16), w)
        return (z[:, :PRIM] * _sigmoid(z[:, PRIM:]),)

    (y2,), (G_out0,) = _rowwise("s5_glu", glu, [('r', y_s5), ('c', W_glu)], [('r', (L, PRIM), F32)], nblk, sub,
                                host=gather(w_out[0]))
    W_mkv0 = G_mkv0.reshape(D_MODEL, 2 * XQ)
    k_a, v_a = _kv_prep(mem0, gm0, W_mkv0, gk0, "kv_prep0")
    x1, (G_in_mla_b,) = _forward_merge(
        x0, y2, 'r', xq_a, gate_a, k_a, v_a, gq0, G_out0.reshape(BRANCH, D_MODEL), "merge0", nblk, sub,
        host=gather(wt_in[:, kh:]))
    W_in_mla = _mla_in_rows(jnp.concatenate([G_in_mla_a, G_in_mla_b], axis=2).reshape(_MLA_IN, D_MODEL))

    def in_mla(x, g, w):
        proj = _dot_nt(_rms(x, g, D_MODEL).astype(BF16), w)
        return proj[:, :512], proj[:, 512:768], proj[:, 768:1280], proj[:, 1280:3328], proj[:, 3328:]

    (c_q, c_kv, xq_b, gate_b, krp), (G_uq, W_kv, G_lora) = _rowwise(
        "mla_in", in_mla, [('r', x1), ('c', ln1), ('c', W_in_mla)],
        [('r', (L, Q_LORA), F32), ('r', (L, KV_LORA), F32), ('r', (L, XQ), F32), ('r', (L, BRANCH), F32),
         ('r', (L, HD), F32)], nblk, sub,
        host=_plan_all_gather([jnp.transpose(mla_w_uq[0]).astype(BF16), mla_w_ukv[0].astype(BF16), lora]))
    W_q = _uq_rows(G_uq.reshape(MLA_H * (HD + ROPE), Q_LORA))
    g_qlora = G_lora[:, 0, :64].reshape(1, Q_LORA)
    g_kvlora = G_lora[:, 0, 64:96].reshape(1, KV_LORA)

    def qkv(c_q, c_kv, krp, tc, ts1, ts2, gql, gkvl, wq, wkv, gqn, gkn, gqr, gkr):
        q = _dot_nt(_rms(c_q, gql, Q_LORA).astype(BF16), wq)
        kv = _mm_slots(_rms(c_kv, gkvl, KV_LORA).astype(BF16), wkv)
        kp, v = _kv_post(kv, krp, gkn, gkr, tc, ts1, ts2)
        return _q_post(q, gqn, gqr, tc, ts1, ts2), kp, v

    qkv_consts = [('c', g_qlora), ('c', g_kvlora), ('c', W_q), ('c', W_kv), ('c', gqn), ('c', gkn), ('c', gqr),
                  ('c', gkr)]
    (q_pad, k_pad, v_h), (G_mkv1, G_out1) = _rowwise(
        "mla_qkv", qkv, [('r', c_q), ('r', c_kv), ('r', krp), ('r', tc), ('r', ts1), ('r', ts2)] + qkv_consts,
        [('r', (L, 2 * PRIM), BF16), ('r', (L, 2 * PRIM), BF16), ('r', (L, PRIM), BF16)], nblk, sub,
        host=gather(w_mem_kv[1], w_out[1]))
    W_out = (G_out0.reshape(BRANCH, D_MODEL), G_out1.reshape(BRANCH, D_MODEL))
    W_mkv = (W_mkv0, G_mkv1.reshape(D_MODEL, 2 * XQ))
    scale = (HD + ROPE) ** -0.5
    attn, lse = _attn_fwd(q_pad, k_pad, v_h, scale)
    k_b, v_b = _kv_prep(mem0, gm1, W_mkv[1], gk1, "kv_prep1")

    def merge_loss(x, mix, xq, gate, k, v, gq, wout, t):
        err = x + _dot(_merge(mix, xq, gate, k, v, gq).astype(BF16), wout) - t
        part = 0.5 * jnp.sum(jnp.sum(err * err, axis=-1, keepdims=True) * (1.0 / D_MODEL), axis=0, keepdims=True)
        return err * (1.0 / D_MODEL), jnp.broadcast_to(part, (1, HD))

    dx2, loss_part = _rowwise(
        "merge1_loss", merge_loss,
        [('r', x1), ('r', attn), ('r', xq_b), ('r', gate_b), ('c', k_b), ('c', v_b), ('c', gq1), ('c', W_out[1]),
         ('r', target)], [('r', (L, D_MODEL), F32), ('a', (1, HD), F32)], nblk, sub)

    dattn, dxq_b, dgate_b, o_b, g_b, dk_b, dv_b, dgq1 = _backward_merge(
        dx2, attn, 'r', xq_b, gate_b, k_b, v_b, gq1, W_out[1], "merge1_bwd", nblk, sub)
    dgm1, dW_mkv1, dgk1 = _kv_prep_bwd(mem0, gm1, W_mkv[1], gk1, dk_b, dv_b, "kv_prep1_bwd")
    dW_out1 = _matmul_tn(o_b, g_b, "dw_out1")
    dq_pad, dk_pad, dv_h = _attn_bwd(q_pad, k_pad, v_h, attn, lse, dattn, scale)

    def qkv_bwd(c_q, c_kv, krp, tc, ts1, ts2, dqp, dkp, dv, gql, gkvl, wq, wkv, gqn, gkn, gqr, gkr):
        cqn, vjp_qn = jax.vjp(lambda a, b: _rms(a, b, Q_LORA), c_q, gql)
        ckvn, vjp_kvn = jax.vjp(lambda a, b: _rms(a, b, KV_LORA), c_kv, gkvl)
        cqn16 = cqn.astype(BF16)
        ckvn16 = ckvn.astype(BF16)
        q = _dot_nt(cqn16, wq)
        kv = _mm_slots(ckvn16, wkv)
        _, vjp_q = jax.vjp(lambda a, b, c: _q_post(a, b, c, tc, ts1, ts2), q, gqn, gqr)
        dq, dgqn, dgqr = vjp_q(dqp.astype(F32))
        _, vjp_kv = jax.vjp(lambda a, b, c, d: _kv_post(a, b, c, d, tc, ts1, ts2), kv, krp, gkn, gkr)
        dkv, dkrp, dgkn, dgkr = vjp_kv((dkp.astype(F32), dv.astype(F32)))
        dq16 = dq.astype(BF16)
        dkv16 = dkv.astype(BF16)
        dc_q, dgql = vjp_qn(_dot(dq16, wq))
        dc_kv, dgkvl = vjp_kvn(_mm_slots_nt(dkv16, wkv))
        return dc_q, dc_kv, dkrp, cqn16, dq16, ckvn16, dkv16, dgql, dgkvl, dgqn, dgkn, dgqr, dgkr

    (dc_q, dc_kv, dkrp, cqn16, dq16, ckvn16, dkv16, dgql, dgkvl, dgqn, dgkn, dgqr, dgkr) = _rowwise(
        "mla_qkv_bwd", qkv_bwd,
        [('r', c_q), ('r', c_kv), ('r', krp), ('r', tc), ('r', ts1), ('r', ts2), ('r', dq_pad), ('r', dk_pad),
         ('r', dv_h)] + qkv_consts,
        [('r', (L, Q_LORA), BF16), ('r', (L, KV_LORA), BF16), ('r', (L, HD), BF16), ('r', (L, Q_LORA), BF16),
         ('t', (2 * PRIM, L), BF16), ('t', (KV_LORA, L), BF16), ('r', (L, 2 * PRIM), BF16),
         ('a', (1, Q_LORA), F32), ('a', (1, KV_LORA), F32), ('a', (1, HD), F32), ('a', (1, HD), F32),
         ('a', (1, HD), F32), ('a', (1, HD), F32)], nblk, sub)
    dW_q = _matmul_tn(dq16, cqn16, "dw_uq")
    dW_kv = _matmul_tn_slots(ckvn16, dkv16, "dw_ukv")

    def in_bwd(x, dres, g, w, *dparts):
        dproj = jnp.concatenate(dparts, axis=-1).astype(BF16)
        xn, vjp = jax.vjp(lambda a, b: _rms(a, b, D_MODEL), x, g)
        dx, dg = vjp(_mm_slots_nt(dproj, w) if w.ndim == 3 else _dot(dproj, w))
        return dx + dres, xn, dproj, dg

    dx1, xn1, dproj1, dln1 = _rowwise(
        "mla_in_bwd", in_bwd,
        [('r', x1), ('r', dx2), ('c', ln1), ('c', W_in_mla), ('r', dc_q), ('r', dc_kv), ('r', dxq_b), ('r', dgate_b),
         ('r', dkrp)],
        [('r', (L, D_MODEL), F32), ('r', (L, D_MODEL), BF16), ('t', (_MLA_IN_PAD, L), BF16), ('a', (1, D_MODEL), F32)],
        nblk, sub)
    dW_in_mla = _matmul_tn(dproj1, xn1, "dw_mla_in")

    grads1 = [dW_out1.reshape(N_DEV, 256, D_MODEL), dW_mkv1.reshape(N_DEV, 128, 2 * XQ),
              _mla_in_rows_back(dW_in_mla).reshape(N_DEV, 424, D_MODEL),
              _uq_rows_back(dW_q).reshape(N_DEV, 288, Q_LORA), dW_kv]
    (dy2, dxq_a, dgate_a, o_a, g_a, dk_a, dv_a, dgq0), pair1 = _backward_merge(
        dx1, y2, 'r', xq_a, gate_a, k_a, v_a, gq0, W_out[0], "merge0_bwd", nblk, sub, host=_plan_pair(grads1))
    dgm0, dW_mkv0, dgk0 = _kv_prep_bwd(mem0, gm0, W_mkv[0], gk0, dk_a, dv_a, "kv_prep0_bwd")
    dW_out0 = _matmul_tn(o_a, g_a, "dw_out0")
    t1 = list(_pair_add(grads1, pair1, "rs_add_layer1"))

    def glu_bwd(y, dy2, w):
        h, vjp_h = jax.vjp(_gelu, y)
        h16 = h.astype(BF16)
        z = _mm_slots(h16, w)
        _, vjp_z = jax.vjp(lambda z: z[:, :PRIM] * _sigmoid(z[:, PRIM:]), z)
        dz16 = vjp_z(dy2)[0].astype(BF16)
        return vjp_h(_mm_slots_nt(dz16, w))[0], h16, dz16

    grads0 = [dW_out0.reshape(N_DEV, 256, D_MODEL), dW_mkv0.reshape(N_DEV, 128, 2 * XQ)]
    (dy_s5, h16, dz16), glu_hosted = _rowwise(
        "s5_glu_bwd", glu_bwd, [('r', y_s5), ('r', dy2), ('c', W_glu)],
        [('r', (L, PRIM), F32), ('t', (PRIM, L), BF16), ('r', (L, 2 * PRIM), BF16)], nblk, sub,
        host=_combine(_plan_chips(t1[2:]), _plan_pair(grads0)))
    recv_proj1, pair0 = glu_hosted[:3], glu_hosted[3:]
    dW_glu = _matmul_tn_slots(h16, dz16, "dw_glu")
    t0 = list(_pair_add(grads0 + [dW_glu], pair0 + list(_exchange_call(_plan_pair([dW_glu]), "rs_pair_glu")),
                        "rs_add_layer0"))
    (du_s5, dbc, dcc, dd, dar, dai), recv_rest = _s5_bwd(u_s5, dy_s5, s5_carry, bm, bmt, cmt, a_r2, a_i2, s5_d,
                                                        cmask, rmat, host=_plan_chips(t1[:2] + t0))
    early_recv = recv_rest[:2] + recv_proj1 + recv_rest[2:]
    dx0, xn0, dproj0, dln0 = _rowwise(
        "s5_in_bwd", in_bwd,
        [('r', x0), ('r', dx1), ('c', ln0), ('c', W_in_s5), ('r', du_s5), ('r', dxq_a),
         ('r', dgate_a)],
        [('r', (L, D_MODEL), F32), ('t', (D_MODEL, L), BF16), ('r', (L, 2 * BRANCH), BF16), ('a', (1, D_MODEL), F32)],
        nblk, sub)

    dbc4 = dbc.reshape(S5_G, S5_C, 2, S5_P)
    dcc4 = dcc.reshape(S5_G, S5_C, 2, S5_P)
    dlr, dli, dls, dbtr, dbti = _s5_params_bwd(
        lr3, li3, ls3, btr, bti, dar.reshape(S5_G, 1, S5_P), dai.reshape(S5_G, 1, S5_P), dbc4[:, :, 0], dbc4[:, :, 1])

    small_part = {
        "ln_gain": jnp.concatenate([dln0, dln1]), "mem_norm": jnp.concatenate([dgm0, dgm1]),
        "xq_norm": jnp.concatenate([dgq0, dgq1]), "xk_norm": jnp.concatenate([dgk0, dgk1]),
        "s5_lambda_re": dlr, "s5_lambda_im": dli, "s5_log_step": dls,
        "s5_b_re": jnp.swapaxes(dbtr, 1, 2), "s5_b_im": jnp.swapaxes(dbti, 1, 2),
        "s5_c_re": dcc4[:, :, 0], "s5_c_im": -dcc4[:, :, 1], "s5_d": dd,
        "mla_q_lora_norm": dgql, "mla_kv_lora_norm": dgkvl, "mla_q_nope_norm": dgqn, "mla_k_nope_norm": dgkn,
        "mla_q_rope_norm": dgqr[:, :ROPE], "mla_k_rope_norm": dgkr[:, :ROPE],
    }
    loss8 = jnp.pad(loss_part, ((0, 7), (0, 0)))
    dW_in_s5, (small_gath, loss_g) = _matmul_tn_slots(
        xn0, dproj0, "dw_s5_in", host=_plan_all_gather([_pack_small(small_part).astype(BF16), loss8]))

    late = [dW_in_s5]
    late_t = _pair_add(late, list(_exchange_call(_plan_pair(late), "rs_pair_late")), "rs_add_late")
    owners = [("w_out", 1), ("w_mem_kv", 1), ("mla_w_in", 0), ("mla_w_uq", 0), ("mla_w_ukv", 0), ("w_out", 0),
              ("w_mem_kv", 0), ("s5_w_glu", 0)]
    flipped = ("mla_w_in", "mla_w_uq")

    def shard(d, n, i):
        return jnp.transpose(d[n][i]) if n in flipped else d[n][i]

    upd, late_recv = _updates_call(early_recv, [shard(weights, n, i) for n, i in owners],
                                   [shard(m_in, n, i) for n, i in owners], [shard(v_in, n, i) for n, i in owners],
                                   "update_early", host=_plan_chips(late_t))
    owners.append(("s5_w_in", 0))
    upd.append(_sum_adamw(late_recv[0], s5_w_in[0], m_s5_w_in[0], v_s5_w_in[0], "update_s5_w_in"))
    grads, delta, new_m, new_v = {}, {}, {}, {}
    for n in _BIG:
        parts = [u for u, (o, _) in sorted(zip(upd, owners), key=lambda t: t[1][1]) if o == n]
        if n in flipped:
            grads[n], delta[n], new_m[n], new_v[n] = (jnp.transpose(parts[0][j])[None] for j in range(4))
        else:
            grads[n], delta[n], new_m[n], new_v[n] = (jnp.stack([p[j] for p in parts]) for j in range(4))

    gs, loss_sum = _small_sum(small_gath, loss_g, "small_sum")
    loss = loss_sum[0, 0]
    for n, _ in _SMALL:
        shape = weights[n].shape
        if n == "mla_q_lora_norm":
            grads[n] = lax.dynamic_slice(_unpack_small(gs, n, (Q_LORA,)), (me * 64,), (64,)).reshape(shape)
        elif n == "mla_kv_lora_norm":
            grads[n] = lax.dynamic_slice(_unpack_small(gs, n, (KV_LORA,)), (me * 32,), (32,)).reshape(shape)
        else:
            grads[n] = _unpack_small(gs, n, shape)

    def own(n, a):
        if a.ndim == 4:
            a = jnp.transpose(a, (0, 2, 3, 1))
        elif a.ndim == 3:
            a = jnp.transpose(a, (0, 2, 1))
        return a.reshape(a.shape[1:]) if a.ndim >= 3 else a

    def back(n, a):
        shape = weights[n].shape
        if len(shape) == 4:
            return jnp.transpose(a.reshape((1,) + a.shape), (0, 3, 1, 2))
        if len(shape) == 3:
            return jnp.transpose(a.reshape((1,) + a.shape), (0, 2, 1))
        return a.reshape(shape)

    wide = ("s5_b_re", "s5_b_im", "s5_c_re", "s5_c_im")
    for names, nb, call in (([n for n, _ in _SMALL if n not in wide], 1, "update_small"), (wide, 4, "update_s5_bc")):
        res = _adamw_multi([own(n, weights[n]) for n in names], [own(n, grads[n]) for n in names],
                           [own(n, m_in[n]) for n in names], [own(n, v_in[n]) for n in names], call, nb)
        for n, (dl, m2, v2) in zip(names, res):
            delta[n], new_m[n], new_v[n] = back(n, dl), back(n, m2), back(n, v2)
    return (loss, dx0[None], *[grads[n] for n in _WEIGHTS], *[delta[n] for n in _WEIGHTS],
            *[new_m[n] for n in _WEIGHTS], *[new_v[n] for n in _WEIGHTS])
```

```python
import functools
import math

import numpy as np
import jax
import jax.numpy as jnp
from jax import lax
from jax.experimental import pallas as pl
from jax.experimental.pallas import tpu as pltpu

F32 = jnp.float32
BF16 = jnp.bfloat16
EPS = 1e-6
NEG = float(np.finfo(np.float32).min)
MESH = pl.DeviceIdType.MESH

N_DEV = 8
D_MODEL = 1024
MEM_LEN = 256
XQ = 512
PRIM = 1536
BRANCH = 2048
X_HEADS = 4
HD = 128
S5_G = 96
S5_P = 64
S5_C = 16
S5_GB = 8
S5_W = S5_GB * S5_P
MLA_H = 12
ROPE = 64
Q_LORA = 512
KV_LORA = 256
ROPE_THETA = 10000.0

ADAM_LR = 0.001
ADAM_B1 = 0.9
ADAM_B2 = 0.999
ADAM_EPS = 1e-08
ADAM_WD = 0.01
ADAM_STEP = 10

VMEM_LIMIT = 56 * 1024 * 1024


def _dot(a, b):
    return jnp.dot(a, b, preferred_element_type=F32)


def _dot_nt(a, b):
    return lax.dot_general(a, b, (((1,), (1,)), ((), ())), preferred_element_type=F32)


def _dot_tn(a, b):
    return lax.dot_general(a, b, (((0,), (0,)), ((), ())), preferred_element_type=F32)


@jax.custom_vjp
def _mm(a, b):
    return _dot(a.astype(BF16), b.astype(BF16))


def _mm_fwd(a, b):
    return _mm(a, b), (a, b)


def _mm_bwd(res, g):
    a, b = res
    gb = g.astype(BF16)
    return _dot_nt(gb, b.astype(BF16)).astype(a.dtype), _dot_tn(a.astype(BF16), gb).astype(b.dtype)


_mm.defvjp(_mm_fwd, _mm_bwd)


@jax.custom_vjp
def _mm_nt(a, b):
    return _dot_nt(a.astype(BF16), b.astype(BF16))


def _mm_nt_fwd(a, b):
    return _mm_nt(a, b), (a, b)


def _mm_nt_bwd(res, g):
    a, b = res
    gb = g.astype(BF16)
    return _dot(gb, b.astype(BF16)).astype(a.dtype), _dot_tn(gb, a.astype(BF16)).astype(b.dtype)


_mm_nt.defvjp(_mm_nt_fwd, _mm_nt_bwd)


@jax.custom_vjp
def _softmax(s):
    m = jnp.max(s, axis=-1, keepdims=True)
    e = jnp.exp(s - m)
    return e / jnp.sum(e, axis=-1, keepdims=True)


def _softmax_fwd(s):
    p = _softmax(s)
    return p, p


def _softmax_bwd(p, g):
    return (p * (g - jnp.sum(p * g, axis=-1, keepdims=True)),)


_softmax.defvjp(_softmax_fwd, _softmax_bwd)


def _rms(x, g, n):
    ms = jnp.sum(x * x, axis=-1, keepdims=True) * (1.0 / n)
    return x * lax.rsqrt(ms + EPS) * g


def _sigmoid(x):
    return 1.0 / (1.0 + jnp.exp(-x))


def _silu(x):
    return x * _sigmoid(x)


def _gelu(x):
    c = math.sqrt(2.0 / math.pi)
    return 0.5 * x * (1.0 + jnp.tanh(c * (x + 0.044715 * (x * x * x))))


@jax.custom_vjp
def _rot(x, c, s1, s2):
    return x * c + pltpu.roll(x, 96, 1) * s1 + pltpu.roll(x, 32, 1) * s2


def _rot_fwd(x, c, s1, s2):
    return _rot(x, c, s1, s2), (c, s1, s2)


def _rot_bwd(res, g):
    c, s1, s2 = res
    dx = g * c + pltpu.roll(g * s1, 32, 1) + pltpu.roll(g * s2, 96, 1)
    return dx, jnp.zeros_like(c), jnp.zeros_like(s1), jnp.zeros_like(s2)


_rot.defvjp(_rot_fwd, _rot_bwd)


def _mem_attn(xq, k, v, gq):
    outs = []
    for h in range(X_HEADS):
        sl = slice(HD * h, HD * (h + 1))
        q = _rms(xq[:, sl], gq, HD)
        p = _softmax(_mm_nt(q, k[:, sl]) * (HD ** -0.5))
        outs.append(_mm(p, v[:, sl]))
    return jnp.concatenate(outs, axis=-1)


def _merge(mix, xq, gate, k, v, gq):
    return jnp.concatenate([mix, _mem_attn(xq, k, v, gq)], axis=-1) * _silu(gate)


def _q_post(q, gqn, gqr, c, s1, s2):
    pieces = []
    for h in range(MLA_H):
        pieces.append(_rms(q[:, HD * h:HD * (h + 1)], gqn, HD))
        pieces.append(_rot(_rms(q[:, PRIM + HD * h:PRIM + HD * (h + 1)], gqr, ROPE), c, s1, s2))
    return jnp.concatenate(pieces, axis=-1)


def _kv_post(kv, krp, gkn, gkr, c, s1, s2):
    kr = _rot(_rms(krp, gkr, ROPE), c, s1, s2)
    pieces, vals = [], []
    for h in range(MLA_H):
        pieces.append(_rms(kv[:, 2 * HD * h:2 * HD * h + HD], gkn, HD))
        pieces.append(kr)
        vals.append(kv[:, 2 * HD * h + HD:2 * HD * (h + 1)])
    return jnp.concatenate(pieces, axis=-1), jnp.concatenate(vals, axis=-1)


def _rowwise(name, fn, ins, outs, nblk, sub=1, host=None):
    n_in = len(ins)

    def spec(kind, shape):
        if kind == 'r':
            return pl.BlockSpec((shape[0] // nblk, shape[1]), lambda i: (i, 0))
        if kind == 't':
            return pl.BlockSpec((shape[0], shape[1] // nblk), lambda i: (0, i))
        zeros = (0,) * len(shape)
        return pl.BlockSpec(tuple(shape), lambda i: zeros)

    def body(*refs):
        i = pl.program_id(0)
        res = fn(*[r[...] for r in refs[:n_in]])
        for (kind, _, _), ref, val in zip(outs, refs[n_in:], res):
            if kind == 'a':
                @pl.when(i == 0)
                def _():
                    ref[...] = jnp.zeros_like(ref)
                ref[...] += val.astype(ref.dtype)
            elif kind == 't':
                ref[...] = val.astype(F32).T.astype(ref.dtype)
            else:
                ref[...] = val.astype(ref.dtype)

    res, hosted = _hosting_call(
        body, name, nblk, host, [a for _, a in ins], [spec(k, a.shape) for k, a in ins],
        [jax.ShapeDtypeStruct(tuple(s), d) for _, s, d in outs], [spec(k, s) for k, s, _ in outs], [])
    return res if host is None else (res, hosted)


def _matmul_tn(at, g, name, out_dtype=BF16):
    K, L = at.shape
    N = g.shape[1]
    tn = next(t for t in (512, 384, 256, 128) if N % t == 0)

    def body(a_ref, g_ref, o_ref):
        o_ref[...] = _dot(a_ref[...], g_ref[...]).astype(o_ref.dtype)

    return pl.pallas_call(
        body, name=name, grid=(N // tn,),
        in_specs=[pl.BlockSpec((K, L), lambda n: (0, 0)), pl.BlockSpec((L, tn), lambda n: (0, n))],
        out_specs=pl.BlockSpec((K, tn), lambda n: (0, n)),
        out_shape=jax.ShapeDtypeStruct((K, N), out_dtype),
        compiler_params=pltpu.CompilerParams(dimension_semantics=("arbitrary",), vmem_limit_bytes=VMEM_LIMIT),
    )(at, g)


def _matmul_tn_slots(at, g, name, host=None):
    K, L = at.shape
    n = g.shape[1] // N_DEV

    def body(a_ref, g_ref, o_ref):
        o_ref[...] = _dot(a_ref[...], g_ref[...]).astype(o_ref.dtype)

    res, hosted = _hosting_call(
        body, name, N_DEV, host, [at, g],
        [pl.BlockSpec((K, L), lambda d: (0, 0)), pl.BlockSpec((L, n), lambda d: (0, d))],
        [jax.ShapeDtypeStruct((N_DEV, K, n), BF16)], [pl.BlockSpec((None, K, n), lambda d: (d, 0, 0))], [])
    return res[0] if host is None else (res[0], hosted)


def _mm_slots(a16, w):
    return jnp.concatenate([_dot(a16, w[d]) for d in range(N_DEV)], axis=-1)


def _mm_slots_nt(g16, w):
    n = w.shape[2]
    out = _dot_nt(g16[:, 0:n], w[0])
    for d in range(1, N_DEV):
        out = out + _dot_nt(g16[:, d * n:(d + 1) * n], w[d])
    return out


class _Exchange:
    def __init__(self, ins, outs, scratch, start, finish):
        self.ins, self.outs, self.scratch, self.start, self.finish = ins, outs, scratch, start, finish


def _xyc():
    return lax.axis_index("x"), lax.axis_index("y"), lax.axis_index("c")


def _plan_all_gather(xs):
    n = len(xs)

    def build(x_refs, out_refs, sems):
        send_sems, recv_sems, local_sems = sems
        x, y, c = _xyc()

        def copies(k, block, to, own=False):
            slot = 4 * block[0] + 2 * block[1] + block[2]
            return [pltpu.make_async_remote_copy(
                src_ref=x_refs[a] if own else out_refs[a].at[slot], dst_ref=out_refs[a].at[slot],
                send_sem=send_sems.at[k * n + a], recv_sem=recv_sems.at[k * n + a], device_id=to,
                device_id_type=MESH) for a in range(n)]

        mine = [pltpu.make_async_copy(x_refs[a], out_refs[a].at[4 * x + 2 * y + c], local_sems.at[a])
                for a in range(n)]
        return copies, mine, (x, y, c), [(1 - x, y), (x, 1 - y), (1 - x, 1 - y)]

    def first_copies(copies, me, chips):
        x, y, c = me
        first = copies(0, me, (x, y, 1 - c), own=True)
        for j, chip in enumerate(chips):
            first += copies(1 + j, me, (*chip, c), own=True)
        return first

    def start(x_refs, out_refs, sems):
        copies, mine, me, chips = build(x_refs, out_refs, sems)
        for cp in mine + first_copies(copies, me, chips):
            cp.start()

    def finish(x_refs, out_refs, sems):
        copies, mine, me, chips = build(x_refs, out_refs, sems)
        x, y, c = me
        passed = []
        for j, chip in enumerate(chips):
            for cp in copies(1 + j, (*chip, c), me):
                cp.wait_recv()
            fwd = copies(4 + j, (*chip, c), (x, y, 1 - c))
            for cp in fwd:
                cp.start()
            passed += fwd
        for cp in copies(0, (x, y, 1 - c), me):
            cp.wait_recv()
        for j, chip in enumerate(chips):
            for cp in copies(4 + j, (*chip, 1 - c), me):
                cp.wait_recv()
        for cp in first_copies(copies, me, chips) + passed:
            cp.wait_send()
        for cp in mine:
            cp.wait()

    return _Exchange(list(xs), [jax.ShapeDtypeStruct((N_DEV,) + a.shape, a.dtype) for a in xs],
                     [pltpu.SemaphoreType.DMA((7 * n,)), pltpu.SemaphoreType.DMA((7 * n,)),
                      pltpu.SemaphoreType.DMA((n,))], start, finish)


_CHIPS = ((0, 0), (0, 1), (1, 0), (1, 1))


def _plan_pair(sends):
    n = len(sends)

    def build(s_refs, o_refs, sems):
        send_sems, recv_sems = sems
        x, y, c = _xyc()
        return [pltpu.make_async_remote_copy(
            src_ref=s_refs[a].at[4 * px + 2 * py + 1 - c], dst_ref=o_refs[a].at[j],
            send_sem=send_sems.at[j * n + a], recv_sem=recv_sems.at[j * n + a], device_id=(x, y, 1 - c),
            device_id_type=MESH) for j, (px, py) in enumerate(_CHIPS) for a in range(n)]

    def start(s_refs, o_refs, sems):
        for cp in build(s_refs, o_refs, sems):
            cp.start()

    def finish(s_refs, o_refs, sems):
        for cp in build(s_refs, o_refs, sems):
            cp.wait_recv()
            cp.wait_send()

    return _Exchange(list(sends), [jax.ShapeDtypeStruct((4,) + a.shape[1:], a.dtype) for a in sends],
                     [pltpu.SemaphoreType.DMA((4 * n,)), pltpu.SemaphoreType.DMA((4 * n,))], start, finish)


def _plan_chips(ts):
    n = len(ts)
    flips = ((1, 0), (0, 1), (1, 1))

    def build(t_refs, o_refs, sems):
        send_sems, recv_sems, local_sems = sems
        x, y, c = _xyc()
        mine = 2 * x + y
        local = [pltpu.make_async_copy(t_refs[a].at[mine], o_refs[a].at[mine], local_sems.at[a]) for a in range(n)]
        remote = []
        for k, (fx, fy) in enumerate(flips):
            px = 1 - x if fx else x
            py = 1 - y if fy else y
            remote += [pltpu.make_async_remote_copy(
                src_ref=t_refs[a].at[2 * px + py], dst_ref=o_refs[a].at[mine],
                send_sem=send_sems.at[k * n + a], recv_sem=recv_sems.at[k * n + a], device_id=(px, py, c),
                device_id_type=MESH) for a in range(n)]
        return local, remote

    def start(t_refs, o_refs, sems):
        local, remote = build(t_refs, o_refs, sems)
        for cp in local + remote:
            cp.start()

    def finish(t_refs, o_refs, sems):
        local, remote = build(t_refs, o_refs, sems)
        for cp in remote:
            cp.wait_recv()
        for cp in remote:
            cp.wait_send()
        for cp in local:
            cp.wait()

    return _Exchange(list(ts), [jax.ShapeDtypeStruct(a.shape, a.dtype) for a in ts],
                     [pltpu.SemaphoreType.DMA((3 * n,)), pltpu.SemaphoreType.DMA((3 * n,)),
                      pltpu.SemaphoreType.DMA((n,))], start, finish)


def _combine(*plans):
    def parts(refs, attr):
        out, at = [], 0
        for p in plans:
            n = len(getattr(p, attr))
            out.append(refs[at:at + n])
            at += n
        return out

    def run(half):
        def go(ins, outs, sems):
            for p, a, o, s in zip(plans, parts(ins, "ins"), parts(outs, "outs"), parts(sems, "scratch")):
                getattr(p, half)(a, o, s)
        return go

    return _Exchange(sum((p.ins for p in plans), []), sum((p.outs for p in plans), []),
                     sum((p.scratch for p in plans), []), run("start"), run("finish"))


def _exchange_call(plan, name):
    n = len(plan.ins)

    def body(*refs):
        ins, outs, sems = refs[:n], refs[n:2 * n], refs[2 * n:]
        plan.start(ins, outs, sems)
        plan.finish(ins, outs, sems)

    return pl.pallas_call(
        body, name=name, out_shape=plan.outs,
        in_specs=[pl.BlockSpec(memory_space=pl.ANY)] * n, out_specs=[pl.BlockSpec(memory_space=pl.ANY)] * n,
        scratch_shapes=plan.scratch,
    )(*plan.ins)


def _slab_spec(lead, rows, cols, nb):
    if rows % (nb * 16) == 0:
        return pl.BlockSpec((lead, rows // nb, cols), lambda i: (0, i, 0))
    if cols % (nb * 128) == 0:
        return pl.BlockSpec((lead, rows, cols // nb), lambda i: (0, 0, i))
    return pl.BlockSpec((lead, rows, cols), lambda i: (0, 0, 0))


def _slab_spec2(rows, cols, nb):
    if rows % (nb * 16) == 0:
        return pl.BlockSpec((rows // nb, cols), lambda i: (i, 0))
    if cols % (nb * 128) == 0:
        return pl.BlockSpec((rows, cols // nb), lambda i: (0, i))
    return pl.BlockSpec((rows, cols), lambda i: (0, 0))


def _cast_call(arrays, name, host=None):
    n = len(arrays)
    nb = 8

    def body(*refs):
        for a in range(n):
            refs[n + a][...] = refs[a][...].astype(BF16)

    specs = [_slab_spec2(x.shape[0], x.shape[1], nb) for x in arrays]
    return _hosting_call(body, name, nb, host, list(arrays), specs,
                         [jax.ShapeDtypeStruct(x.shape, BF16) for x in arrays], specs, [])


def _pair_add(sends, fromsib, name):
    n = len(sends)
    nb = 8

    def body(*refs):
        c = lax.axis_index("c")
        for a in range(n):
            s_ref, f_ref, t_ref = refs[a], refs[n + a], refs[2 * n + a]
            for j in range(4):
                t_ref[j] = (s_ref[2 * j + c].astype(F32) + f_ref[j].astype(F32)).astype(t_ref.dtype)

    def spec(a, lead):
        return _slab_spec(lead, a.shape[1], a.shape[2], nb)

    return pl.pallas_call(
        body, name=name, grid=(nb,),
        in_specs=[spec(a, N_DEV) for a in sends] + [spec(a, 4) for a in fromsib],
        out_specs=[spec(a, 4) for a in fromsib],
        out_shape=[jax.ShapeDtypeStruct(a.shape, a.dtype) for a in fromsib],
        compiler_params=pltpu.CompilerParams(dimension_semantics=("arbitrary",), vmem_limit_bytes=VMEM_LIMIT),
    )(*sends, *fromsib)


def _adamw_vals(w, g, m, v):
    m2 = ADAM_B1 * m + (1.0 - ADAM_B1) * g
    v2 = ADAM_B2 * v + (1.0 - ADAM_B2) * (g * g)
    m_hat = m2 / (1.0 - ADAM_B1 ** ADAM_STEP)
    v_hat = v2 / (1.0 - ADAM_B2 ** ADAM_STEP)
    delta = -ADAM_LR * (m_hat / (jnp.sqrt(v_hat) + ADAM_EPS) + ADAM_WD * w)
    return delta, m2, v2


def _sum_adamw(recv, w, m, v, name):
    R, C = w.shape
    ns = recv.shape[0]
    br = next((t for t in (256, 128, 64, 32, 16) if R % t == 0), R)

    def body(r_ref, w_ref, m_ref, v_ref, g_ref, d_ref, m2_ref, v2_ref):
        g = r_ref[0].astype(F32)
        for d in range(1, ns):
            g = g + r_ref[d].astype(F32)
        dl, m2, v2 = _adamw_vals(w_ref[...], g, m_ref[...], v_ref[...])
        g_ref[...] = g
        d_ref[...] = dl
        m2_ref[...] = m2
        v2_ref[...] = v2

    spec = pl.BlockSpec((br, C), lambda i: (i, 0))
    return pl.pallas_call(
        body, name=name, grid=(R // br,),
        in_specs=[pl.BlockSpec((ns, br, C), lambda i: (0, i, 0)), spec, spec, spec], out_specs=[spec] * 4,
        out_shape=[jax.ShapeDtypeStruct((R, C), F32)] * 4,
        compiler_params=pltpu.CompilerParams(dimension_semantics=("arbitrary",)),
    )(recv, w, m, v)


def _updates_call(recvs, ws, ms, vs, name, host=None):
    n = len(recvs)
    nb = 8

    def body(*refs):
        for a in range(n):
            r_ref, w_ref, m_ref, v_ref = refs[a], refs[n + a], refs[2 * n + a], refs[3 * n + a]
            g_ref, d_ref, m2_ref, v2_ref = refs[4 * n + 4 * a:4 * n + 4 * a + 4]
            g = r_ref[0].astype(F32)
            for d in range(1, r_ref.shape[0]):
                g = g + r_ref[d].astype(F32)
            dl, m2, v2 = _adamw_vals(w_ref[...], g, m_ref[...], v_ref[...])
            g_ref[...] = g
            d_ref[...] = dl
            m2_ref[...] = m2
            v2_ref[...] = v2

    def spec3(r):
        return _slab_spec(r.shape[0], r.shape[1], r.shape[2], nb)

    def spec2(w):
        return _slab_spec2(w.shape[0], w.shape[1], nb)

    res, hosted = _hosting_call(
        body, name, nb, host, list(recvs) + list(ws) + list(ms) + list(vs),
        [spec3(r) for r in recvs] + [spec2(w) for w in ws] * 3,
        [jax.ShapeDtypeStruct(w.shape, F32) for w in ws for _ in range(4)],
        [spec2(w) for w in ws for _ in range(4)], [])
    return [res[4 * a:4 * a + 4] for a in range(n)], hosted


def _small_sum(gath, loss_g, row0_g, name):
    _, R, C = gath.shape
    br = R // 3

    def body(g_ref, l_ref, r_ref, go_ref, lo_ref):
        g = g_ref[0].astype(F32)
        lsum = l_ref[0]
        for d in range(1, N_DEV):
            g = g + g_ref[d].astype(F32)
            lsum = lsum + l_ref[d]
        go_ref[...] = g
        lo_ref[...] = lsum

        @pl.when(pl.program_id(0) == 0)
        def _():
            row0 = r_ref[0]
            for d in range(1, N_DEV):
                row0 = row0 + r_ref[d]
            go_ref[0:8, :] = go_ref[0:8, :] + jnp.where(lax.broadcasted_iota(jnp.int32, row0.shape, 0) == 0, row0, 0.0)

    return pl.pallas_call(
        body, name=name, grid=(R // br,),
        in_specs=[pl.BlockSpec((N_DEV, br, C), lambda i: (0, i, 0)),
                  pl.BlockSpec((N_DEV, 8, HD), lambda i: (0, 0, 0)), pl.BlockSpec((N_DEV, 8, C), lambda i: (0, 0, 0))],
        out_specs=[pl.BlockSpec((br, C), lambda i: (i, 0)), pl.BlockSpec((8, HD), lambda i: (0, 0))],
        out_shape=[jax.ShapeDtypeStruct((R, C), F32), jax.ShapeDtypeStruct((8, HD), F32)],
        compiler_params=pltpu.CompilerParams(dimension_semantics=("arbitrary",)),
    )(gath, loss_g, row0_g)


def _adamw_multi(ws, gs, ms, vs, name, nblk=1):
    n = len(ws)

    def body(*refs):
        for a in range(n):
            dl, m2, v2 = _adamw_vals(refs[a][...], refs[n + a][...], refs[2 * n + a][...], refs[3 * n + a][...])
            refs[4 * n + 3 * a][...] = dl
            refs[4 * n + 3 * a + 1][...] = m2
            refs[4 * n + 3 * a + 2][...] = v2

    def spec(x):
        rest = (0,) * (x.ndim - 1)
        return pl.BlockSpec((x.shape[0] // nblk,) + tuple(x.shape[1:]), lambda i: (i,) + rest)

    res = pl.pallas_call(
        body, name=name, grid=(nblk,),
        in_specs=[spec(w) for w in ws] * 4, out_specs=[spec(w) for w in ws for _ in range(3)],
        out_shape=[jax.ShapeDtypeStruct(w.shape, F32) for w in ws for _ in range(3)],
        compiler_params=pltpu.CompilerParams(dimension_semantics=("arbitrary",), vmem_limit_bytes=VMEM_LIMIT),
    )(*ws, *gs, *ms, *vs)
    return [res[3 * a:3 * a + 3] for a in range(n)]


def _s5_param_fn(lr, li, ls, btr, bti):
    step = jnp.exp(ls)
    er = jnp.exp(lr * step)
    ang = li * step
    ar = er * jnp.cos(ang)
    ai = er * jnp.sin(ang)
    nr = ar - 1.0
    den = lr * lr + li * li
    fr = (nr * lr + ai * li) / den
    fi = (ai * lr - nr * li) / den
    return ar, ai, fr * btr - fi * bti, fr * bti + fi * btr


def _s5_params(lr, li, ls, btr, bti):
    def body(lr_ref, li_ref, ls_ref, br_ref, bi_ref, ar_ref, ai_ref, bbr_ref, bbi_ref):
        ar, ai, bbr, bbi = _s5_param_fn(lr_ref[...], li_ref[...], ls_ref[...], br_ref[...], bi_ref[...])
        ar_ref[...] = ar
        ai_ref[...] = ai
        bbr_ref[...] = bbr
        bbi_ref[...] = bbi

    sd = jax.ShapeDtypeStruct
    return pl.pallas_call(
        body, name="s5_params",
        out_shape=[sd(lr.shape, F32), sd(lr.shape, F32), sd(btr.shape, F32), sd(btr.shape, F32)],
    )(lr, li, ls, btr, bti)


def _s5_params_bwd(lr, li, ls, btr, bti, dar, dai, dbbr, dbbi):
    def body(lr_ref, li_ref, ls_ref, br_ref, bi_ref, dar_ref, dai_ref, dbbr_ref, dbbi_ref,
             dlr_ref, dli_ref, dls_ref, dbr_ref, dbi_ref):
        _, vjp = jax.vjp(_s5_param_fn, lr_ref[...], li_ref[...], ls_ref[...], br_ref[...], bi_ref[...])
        dlr, dli, dls, dbr, dbi = vjp((dar_ref[...], dai_ref[...], dbbr_ref[...], dbbi_ref[...]))
        dlr_ref[...] = dlr
        dli_ref[...] = dli
        dls_ref[...] = dls
        dbr_ref[...] = dbr
        dbi_ref[...] = dbi

    sd = jax.ShapeDtypeStruct
    return pl.pallas_call(
        body, name="s5_params_bwd",
        out_shape=[sd(lr.shape, F32), sd(lr.shape, F32), sd(ls.shape, F32), sd(btr.shape, F32), sd(btr.shape, F32)],
    )(lr, li, ls, btr, bti, dar, dai, dbbr, dbbi)


def _cpow(ar, ai, n):
    assert n & (n - 1) == 0
    while n > 1:
        ar, ai = ar * ar - ai * ai, 2.0 * ar * ai
        n //= 2
    return ar, ai


def _scan(st, cr, ci, init, nk, reverse, store, prev=None):
    W = S5_W

    def step(j, carry):
        k = nk - 1 - j if reverse else j
        rows = pl.ds(pl.multiple_of(k * 8, 8), 8)
        sr, si = carry[0], carry[1]
        nsr = cr * sr - ci * si + st[rows, 0:W]
        nsi = cr * si + ci * sr + st[rows, W:2 * W]
        if store:
            st[rows, 0:W] = nsr
            st[rows, W:2 * W] = nsi
        if prev is None:
            return nsr, nsi
        prows = pl.ds(pl.multiple_of(jnp.maximum(k - 1, 0) * 8, 8), 8)
        w = jnp.where(k > 0, 1.0, 0.0).astype(F32)
        pr = prev[prows, 0:W] * w
        pi = prev[prows, W:2 * W] * w
        return nsr, nsi, carry[2] + nsr * pr + nsi * pi, carry[3] + nsi * pr - nsr * pi

    return lax.fori_loop(0, nk, step, init, unroll=2)


def _chain(fin, fr, fi, pr, pi, reverse):
    W = S5_W
    fin[:, 0:W] = fr
    fin[:, W:2 * W] = fi
    rowid = lax.broadcasted_iota(jnp.int32, (8, W), 0)
    cr = jnp.zeros((1, W), F32)
    ci = jnp.zeros((1, W), F32)
    init_r = jnp.zeros((8, W), F32)
    init_i = jnp.zeros((8, W), F32)
    for s in (range(7, -1, -1) if reverse else range(8)):
        init_r = jnp.where(rowid == s, cr, init_r)
        init_i = jnp.where(rowid == s, ci, init_i)
        lr = fin[s:s + 1, 0:W]
        li = fin[s:s + 1, W:2 * W]
        cr, ci = lr + pr * cr - pi * ci, li + pr * ci + pi * cr
    return init_r, init_i


def _full_scan(st, fin, ar, ai, nk, reverse, prev=None, carry_in=None, carry_out=None):
    W = S5_W
    cr = jnp.broadcast_to(ar, (8, W))
    ci = jnp.broadcast_to(-ai if reverse else ai, (8, W))
    z = jnp.zeros((8, W), F32)
    if carry_in is None:
        fr, fi = _scan(st, cr, ci, (z, z), nk, reverse, store=False)
        pr, pi = _cpow(ar, -ai if reverse else ai, nk)
        init = _chain(fin, fr, fi, pr, pi, reverse)
    else:
        init = (carry_in[:, 0:W], carry_in[:, W:2 * W])
    if carry_out is not None:
        carry_out[:, 0:W] = init[0]
        carry_out[:, W:2 * W] = init[1]
    if prev is None:
        return _scan(st, cr, ci, init, nk, reverse, store=True)
    return _scan(st, cr, ci, init + (z, z), nk, reverse, store=True, prev=prev)


def _s5_specs(L):
    W2 = 2 * S5_W
    GC = S5_GB * S5_C
    col = pl.BlockSpec((L, GC), lambda g: (0, g))
    vec = pl.BlockSpec((1, GC), lambda g: (0, g))
    avec = pl.BlockSpec((1, S5_W), lambda g: (0, g))
    bmat = pl.BlockSpec((None, GC, W2), lambda g: (g, 0, 0))
    cmat = pl.BlockSpec((None, W2, GC), lambda g: (g, 0, 0))
    return col, vec, avec, bmat, cmat


def _interleave(dst, src, nk):
    for s in range(8):
        dst[pl.ds(s, nk, stride=8), :] = src[s * nk:(s + 1) * nk, :]


def _deinterleave(dst, src, nk):
    for s in range(8):
        dst[s * nk:(s + 1) * nk, :] = src[pl.ds(s, nk, stride=8), :].astype(dst.dtype)


def _hosting_call(body, name, nsteps, host, ins, in_specs, outs, out_specs, scratch):
    grid = (nsteps,) if isinstance(nsteps, int) else tuple(nsteps)
    params = pltpu.CompilerParams(dimension_semantics=("arbitrary",) * len(grid), vmem_limit_bytes=VMEM_LIMIT)
    if host is None:
        res = pl.pallas_call(
            body, name=name, grid=grid, in_specs=in_specs, out_specs=out_specs, out_shape=outs,
            scratch_shapes=scratch, compiler_params=params,
        )(*ins)
        return list(res), []
    n_in, n_out, n_sc = len(ins), len(outs), len(scratch)
    h_in, h_out = len(host.ins), len(host.outs)

    def hosted(*refs):
        a = refs[:n_in]
        ha = refs[n_in:n_in + h_in]
        o = refs[n_in + h_in:n_in + h_in + n_out]
        ho = refs[n_in + h_in + n_out:n_in + h_in + n_out + h_out]
        sc = refs[n_in + h_in + n_out + h_out:n_in + h_in + n_out + h_out + n_sc]
        hs = refs[n_in + h_in + n_out + h_out + n_sc:]
        first = functools.reduce(jnp.logical_and, [pl.program_id(i) == 0 for i in range(len(grid))])
        last = functools.reduce(jnp.logical_and, [pl.program_id(i) == g - 1 for i, g in enumerate(grid)])

        @pl.when(first)
        def _():
            host.start(ha, ho, hs)

        body(*a, *o, *sc)

        @pl.when(last)
        def _():
            host.finish(ha, ho, hs)

    hbm = pl.BlockSpec(memory_space=pl.ANY)
    res = pl.pallas_call(
        hosted, name=name, grid=grid,
        in_specs=list(in_specs) + [hbm] * h_in, out_specs=list(out_specs) + [hbm] * h_out,
        out_shape=list(outs) + list(host.outs), scratch_shapes=list(scratch) + list(host.scratch),
        compiler_params=params,
    )(*ins, *host.ins)
    return list(res[:n_out]), list(res[n_out:])


def _s5_fwd(u, bm, cm, ar, ai, dvec, host=None):
    L = u.shape[0]
    nk = L // 8
    GC = S5_GB * S5_C
    nb = S5_G // S5_GB
    col, vec, avec, bmat, cmat = _s5_specs(L)

    def body(u_ref, b_ref, c_ref, ar_ref, ai_ref, d_ref, y_ref, carry_ref, st, fin, ui, yi):
        _interleave(ui, u_ref, nk)
        for r in range(8):
            rows = slice(r * nk, (r + 1) * nk)
            st[rows, :] = _dot(ui[rows, :].astype(BF16), b_ref[...])
        _full_scan(st, fin, ar_ref[...], ai_ref[...], nk, reverse=False, carry_out=carry_ref)
        for r in range(8):
            rows = slice(r * nk, (r + 1) * nk)
            yi[rows, :] = _dot(st[rows, :].astype(BF16), c_ref[...]) + d_ref[...] * ui[rows, :]
        _deinterleave(y_ref, yi, nk)

    return _hosting_call(
        body, "s5_fwd", nb, host,
        [u, bm, cm, ar, ai, dvec], [col, bmat, cmat, avec, avec, vec],
        [jax.ShapeDtypeStruct(u.shape, F32), jax.ShapeDtypeStruct((nb * 8, 2 * S5_W), F32)],
        [col, pl.BlockSpec((8, 2 * S5_W), lambda g: (g, 0))],
        [pltpu.VMEM((L, 2 * S5_W), F32), pltpu.VMEM((8, 2 * S5_W), F32), pltpu.VMEM((L, GC), F32),
         pltpu.VMEM((L, GC), F32)])


def _s5_bwd(u, dy, carry, bm, bmt, cmt, ar, ai, dvec, mask, rmat, host=None):
    L = u.shape[0]
    nk = L // 8
    W = S5_W
    GC = S5_GB * S5_C
    col, vec, avec, bmat, cmat = _s5_specs(L)
    hi = lax.Precision.HIGHEST

    def body(u_ref, dy_ref, carry_ref, b_ref, bt_ref, ct_ref, ar_ref, ai_ref, d_ref, mask_ref, r_ref,
             du_ref, db_ref, dc_ref, dd_ref, dar_ref, dai_ref, sa, sb, fin, ui, dyi, dui):
        ar = ar_ref[...]
        ai = ai_ref[...]
        _interleave(ui, u_ref, nk)
        _interleave(dyi, dy_ref, nk)
        for r in range(8):
            rows = slice(r * nk, (r + 1) * nk)
            sa[rows, :] = _dot(ui[rows, :].astype(BF16), b_ref[...])
            sb[rows, :] = _dot(dyi[rows, :].astype(BF16), ct_ref[...])
        _full_scan(sa, fin, ar, ai, nk, reverse=False, carry_in=carry_ref)
        gr, gi, accr, acci = _full_scan(sb, fin, ar, ai, nk, reverse=True, prev=sa)
        rowid = lax.broadcasted_iota(jnp.int32, (8, W), 0)
        last = pl.ds((nk - 1) * 8, 8)
        pr = jnp.where(rowid == 0, 0.0, pltpu.roll(sa[last, 0:W], 1, 0))
        pi = jnp.where(rowid == 0, 0.0, pltpu.roll(sa[last, W:2 * W], 1, 0))
        accr = accr + gr * pr + gi * pi
        acci = acci + gi * pr - gr * pi
        dar_ref[...] = jnp.sum(accr, axis=0, keepdims=True)
        dai_ref[...] = jnp.sum(acci, axis=0, keepdims=True)
        dbf = jnp.zeros((GC, 2 * W), F32)
        dcf = jnp.zeros((GC, 2 * W), F32)
        dd = jnp.zeros((1, GC), F32)
        for r in range(8):
            rows = slice(r * nk, (r + 1) * nk)
            ub = ui[rows, :]
            dyb = dyi[rows, :]
            gb = sb[rows, :].astype(BF16)
            dui[rows, :] = _dot(gb, bt_ref[...]) + d_ref[...] * dyb
            dbf = dbf + _dot_tn(ub.astype(BF16), gb)
            dcf = dcf + _dot_tn(dyb.astype(BF16), sa[rows, :].astype(BF16))
            dd = dd + jnp.sum(dyb * ub, axis=0, keepdims=True)
        db_ref[...] = jnp.dot(dbf * mask_ref[...], r_ref[...], precision=hi, preferred_element_type=F32)
        dc_ref[...] = jnp.dot(dcf * mask_ref[...], r_ref[...], precision=hi, preferred_element_type=F32)
        dd_ref[...] = dd
        _deinterleave(du_ref, dui, nk)

    cmp_spec = pl.BlockSpec((GC, 2 * S5_P), lambda g: (g, 0))
    whole = lambda shape: pl.BlockSpec(shape, lambda g: (0, 0))
    sd = jax.ShapeDtypeStruct
    return _hosting_call(
        body, "s5_bwd", S5_G // S5_GB, host,
        [u, dy, carry, bm, bmt, cmt, ar, ai, dvec, mask, rmat],
        [col, col, pl.BlockSpec((8, 2 * W), lambda g: (g, 0)), bmat, cmat, bmat, avec, avec, vec, whole(mask.shape),
         whole(rmat.shape)],
        [sd(u.shape, BF16), sd((S5_G * S5_C, 2 * S5_P), F32), sd((S5_G * S5_C, 2 * S5_P), F32),
         sd((1, PRIM), F32), sd((1, S5_G * S5_P), F32), sd((1, S5_G * S5_P), F32)],
        [col, cmp_spec, cmp_spec, vec, avec, avec],
        [pltpu.VMEM((L, 2 * W), F32), pltpu.VMEM((L, 2 * W), F32), pltpu.VMEM((8, 2 * W), F32),
         pltpu.VMEM((L, GC), F32), pltpu.VMEM((L, GC), F32), pltpu.VMEM((L, GC), F32)])


def _s5_mats(bbr, bbi, cre, cim):
    nb = S5_G // S5_GB
    eye = jnp.eye(S5_GB, dtype=F32)
    bb = jnp.stack([bbr, bbi], axis=2).reshape(nb, S5_GB, S5_C, 2, S5_P)
    bm = jnp.einsum('ngcrp,gh->ngcrhp', bb, eye).reshape(nb, S5_GB * S5_C, 2 * S5_W)
    cc = jnp.stack([cre, -cim], axis=2).reshape(nb, S5_GB, S5_C, 2, S5_P)
    cmt = jnp.einsum('ngcrp,gh->ngcrhp', cc, eye).reshape(nb, S5_GB * S5_C, 2 * S5_W)
    return (bm.astype(BF16), jnp.swapaxes(bm, 1, 2).astype(BF16),
            jnp.swapaxes(cmt, 1, 2).astype(BF16), cmt.astype(BF16))


def _s5_compact_consts():
    g_row = np.arange(S5_GB * S5_C) // S5_C
    col = np.arange(2 * S5_W)
    g_col = (col % S5_W) // S5_P
    mask = (g_row[:, None] == g_col[None, :]).astype(np.float32)
    tgt = (col // S5_W) * S5_P + col % S5_P
    rmat = (tgt[:, None] == np.arange(2 * S5_P)[None, :]).astype(np.float32)
    return jnp.asarray(mask), jnp.asarray(rmat)


def _attn_scores(q_ref, k_ref, qb, bq, scale):
    ext = (qb + 1) * bq
    s = _dot_nt(q_ref[qb * bq:ext, :], k_ref[0:ext, :]) * scale
    qpos = lax.broadcasted_iota(jnp.int32, (bq, bq), 0)
    kpos = lax.broadcasted_iota(jnp.int32, (bq, bq), 1)
    diag = jnp.where(kpos <= qpos, s[:, ext - bq:], NEG)
    return diag if qb == 0 else jnp.concatenate([s[:, :ext - bq], diag], axis=-1)


def _attn_fwd(qp, kp, v, scale):
    L = qp.shape[0]
    bq = min(256, L)

    def body(q_ref, k_ref, v_ref, o_ref, lse_ref):
        for qb in range(L // bq):
            rows = slice(qb * bq, (qb + 1) * bq)
            s = _attn_scores(q_ref, k_ref, qb, bq, scale)
            m = jnp.max(s, axis=-1, keepdims=True)
            e = jnp.exp(s - m)
            l = jnp.sum(e, axis=-1, keepdims=True)
            o_ref[rows, :] = _dot(e.astype(BF16), v_ref[0:(qb + 1) * bq, :]) / l
            lse_ref[rows, :] = jnp.broadcast_to(m + jnp.log(l), (bq, HD))

    blk = pl.BlockSpec((L, HD), lambda h: (0, h))
    wide = pl.BlockSpec((L, 2 * HD), lambda h: (0, h))
    return pl.pallas_call(
        body, name="mla_attn_fwd", grid=(MLA_H,),
        in_specs=[wide, wide, blk], out_specs=[blk, blk],
        out_shape=[jax.ShapeDtypeStruct((L, MLA_H * HD), F32)] * 2,
        compiler_params=pltpu.CompilerParams(dimension_semantics=("arbitrary",), vmem_limit_bytes=VMEM_LIMIT),
    )(qp, kp, v)


def _attn_bwd(qp, kp, v, o, lse, do, scale):
    L = qp.shape[0]
    bq = min(256, L)
    nq = L // bq

    def body(q_ref, k_ref, v_ref, o_ref, lse_ref, do_ref, dq_ref, dk_ref, dv_ref, dk_acc, dv_acc):
        dk_acc[...] = jnp.zeros_like(dk_acc)
        dv_acc[...] = jnp.zeros_like(dv_acc)
        for qb in range(nq):
            rows = slice(qb * bq, (qb + 1) * bq)
            ext = (qb + 1) * bq
            do = do_ref[rows, :]
            dob = do.astype(BF16)
            p = jnp.exp(_attn_scores(q_ref, k_ref, qb, bq, scale) - lse_ref[rows, 0:1])
            dp = _dot_nt(dob, v_ref[0:ext, :])
            dsum = jnp.sum(do * o_ref[rows, :], axis=-1, keepdims=True)
            ds = (p * (dp - dsum) * scale).astype(BF16)
            dq_ref[rows, :] = _dot(ds, k_ref[0:ext, :]).astype(dq_ref.dtype)
            dk_acc[0:ext, :] += _dot_tn(ds, q_ref[rows, :])
            dv_acc[0:ext, :] += _dot_tn(p.astype(BF16), dob)
        dk_ref[...] = dk_acc[...].astype(dk_ref.dtype)
        dv_ref[...] = dv_acc[...].astype(dv_ref.dtype)

    sd = jax.ShapeDtypeStruct
    blk = pl.BlockSpec((L, HD), lambda h: (0, h))
    wide = pl.BlockSpec((L, 2 * HD), lambda h: (0, h))
    return pl.pallas_call(
        body, name="mla_attn_bwd", grid=(MLA_H,),
        in_specs=[wide, wide, blk, blk, blk, blk], out_specs=[wide, wide, blk],
        out_shape=[sd((L, MLA_H * 2 * HD), BF16), sd((L, MLA_H * 2 * HD), BF16), sd((L, MLA_H * HD), BF16)],
        scratch_shapes=[pltpu.VMEM((L, 2 * HD), F32), pltpu.VMEM((L, HD), F32)],
        compiler_params=pltpu.CompilerParams(dimension_semantics=("arbitrary",), vmem_limit_bytes=VMEM_LIMIT),
    )(qp, kp, v, o, lse, do)


def _kv_fn(mem, gm, w, gk):
    kv = _mm(_rms(mem, gm, D_MODEL), w)
    k = jnp.concatenate([_rms(kv[:, HD * h:HD * (h + 1)], gk, HD) for h in range(X_HEADS)], axis=-1)
    return k, kv[:, XQ:]


def _kv_prep(mem, gm, w, gk, name):
    def fn(mem, gm, w, gk):
        return _kv_fn(mem, gm, w, gk)
    M = mem.shape[0]
    return _rowwise(name, fn, [('c', mem), ('c', gm), ('c', w), ('c', gk)],
                    [('c', (M, XQ), F32), ('c', (M, XQ), F32)], 1)


def _kv_prep_bwd(mem, gm, w, gk, dk, dv, name):
    def fn(mem, gm, w, gk, dk, dv):
        _, vjp = jax.vjp(lambda a, b, c: _kv_fn(mem, a, b, c), gm, w, gk)
        return vjp((dk, dv))
    return _rowwise(name, fn, [('c', mem), ('c', gm), ('c', w), ('c', gk), ('c', dk), ('c', dv)],
                    [('c', gm.shape, F32), ('c', w.shape, BF16), ('c', gk.shape, F32)], 1)


def _forward_merge(x, mix, mix_kind, xq, gate, k, v, gq, wout, name, nblk, sub, host=None):
    def fn(x, mix, xq, gate, k, v, gq, wout):
        o = _merge(mix, xq, gate, k, v, gq)
        return (x + _dot(o.astype(BF16), wout),)
    L = x.shape[0]
    out = _rowwise(name, fn, [('r', x), (mix_kind, mix), ('r', xq), ('r', gate), ('c', k), ('c', v), ('c', gq),
                              ('c', wout)], [('r', (L, D_MODEL), F32)], nblk, sub, host=host)
    return out[0] if host is None else (out[0][0], out[1])


def _backward_merge(dx, mix, mix_kind, xq, gate, k, v, gq, wout, name, nblk, sub, host=None):
    def fn(dx, mix, xq, gate, k, v, gq, wout):
        g16 = dx.astype(BF16)
        do = _dot_nt(g16, wout)
        o, vjp = jax.vjp(_merge, mix, xq, gate, k, v, gq)
        dmix, dxq, dgate, dk, dv, dgq = vjp(do)
        return dmix, dxq, dgate, o, g16, dk, dv, dgq
    L = dx.shape[0]
    return _rowwise(
        name, fn,
        [('r', dx), (mix_kind, mix), ('r', xq), ('r', gate), ('c', k), ('c', v), ('c', gq), ('c', wout)],
        [('r', (L, PRIM), F32), ('r', (L, XQ), BF16), ('r', (L, BRANCH), BF16), ('t', (BRANCH, L), BF16),
         ('r', (L, D_MODEL), BF16), ('a', k.shape, F32), ('a', v.shape, F32), ('a', gq.shape, F32)], nblk, sub,
        host=host)


_MLA_IN = 3392
_MLA_IN_PAD = 3456


def _uq_rows(wt):
    r = wt.reshape(MLA_H, HD + ROPE, wt.shape[1])
    return jnp.concatenate([r[:, :HD].reshape(PRIM, -1),
                            jnp.pad(r[:, HD:], ((0, 0), (0, HD - ROPE), (0, 0))).reshape(PRIM, -1)], axis=0)


def _uq_rows_back(wt):
    nope = wt[:PRIM].reshape(MLA_H, HD, -1)
    rope = wt[PRIM:].reshape(MLA_H, HD, -1)[:, :ROPE]
    return jnp.concatenate([nope, rope], axis=1).reshape(MLA_H * (HD + ROPE), -1)


def _mla_in_rows(wt):
    return jnp.concatenate([wt[:768], wt[832:], wt[768:832], jnp.zeros((64, wt.shape[1]), wt.dtype)], axis=0)


def _mla_in_rows_back(wt):
    return jnp.concatenate([wt[:768], wt[3328:3392], wt[768:3328]], axis=0)


_SMALL = (("ln_gain", 2048), ("mem_norm", 2048), ("xq_norm", 256), ("xk_norm", 256), ("s5_lambda_re", 6144),
          ("s5_lambda_im", 6144), ("s5_log_step", 96), ("s5_b_re", 98304), ("s5_b_im", 98304), ("s5_c_re", 98304),
          ("s5_c_im", 98304), ("s5_d", 1536), ("mla_q_lora_norm", 512), ("mla_kv_lora_norm", 256),
          ("mla_q_nope_norm", 128), ("mla_k_nope_norm", 128), ("mla_q_rope_norm", 64), ("mla_k_rope_norm", 64))
_SMALL_ROWS = 432
_SMALL_OFF = {name: sum(n for _, n in _SMALL[:i]) for i, (name, _) in enumerate(_SMALL)}


def _pack_small(d):
    flat = jnp.concatenate([d[n].reshape(-1).astype(F32) for n, _ in _SMALL])
    return jnp.pad(flat, (0, _SMALL_ROWS * 1024 - flat.shape[0])).reshape(_SMALL_ROWS, 1024)


def _unpack_small(p, name, shape):
    off = _SMALL_OFF[name]
    return p.reshape(-1)[off:off + int(np.prod(shape))].reshape(shape)


_WEIGHTS = ('ln_gain', 'w_out', 'mem_norm', 'w_mem_kv', 'xq_norm', 'xk_norm', 's5_w_in', 's5_lambda_re',
            's5_lambda_im', 's5_log_step', 's5_b_re', 's5_b_im', 's5_c_re', 's5_c_im', 's5_d', 's5_w_glu', 'mla_w_in',
            'mla_q_lora_norm', 'mla_kv_lora_norm', 'mla_w_uq', 'mla_w_ukv', 'mla_q_nope_norm', 'mla_k_nope_norm',
            'mla_q_rope_norm', 'mla_k_rope_norm')
_BIG = ('w_out', 'w_mem_kv', 's5_w_in', 's5_w_glu', 'mla_w_in', 'mla_w_uq', 'mla_w_ukv')


def _pad128(g):
    return jnp.pad(g.reshape(1, -1), ((0, 0), (0, HD - g.shape[-1])))


def kernel(x, mem, positions, ln_gain, w_out, mem_norm, w_mem_kv, xq_norm, xk_norm, s5_w_in, s5_lambda_re, s5_lambda_im, s5_log_step, s5_b_re, s5_b_im, s5_c_re, s5_c_im, s5_d, s5_w_glu, mla_w_in, mla_q_lora_norm, mla_kv_lora_norm, mla_w_uq, mla_w_ukv, mla_q_nope_norm, mla_k_nope_norm, mla_q_rope_norm, mla_k_rope_norm, loss_target, m_ln_gain, m_w_out, m_mem_norm, m_w_mem_kv, m_xq_norm, m_xk_norm, m_s5_w_in, m_s5_lambda_re, m_s5_lambda_im, m_s5_log_step, m_s5_b_re, m_s5_b_im, m_s5_c_re, m_s5_c_im, m_s5_d, m_s5_w_glu, m_mla_w_in, m_mla_q_lora_norm, m_mla_kv_lora_norm, m_mla_w_uq, m_mla_w_ukv, m_mla_q_nope_norm, m_mla_k_nope_norm, m_mla_q_rope_norm, m_mla_k_rope_norm, v_ln_gain, v_w_out, v_mem_norm, v_w_mem_kv, v_xq_norm, v_xk_norm, v_s5_w_in, v_s5_lambda_re, v_s5_lambda_im, v_s5_log_step, v_s5_b_re, v_s5_b_im, v_s5_c_re, v_s5_c_im, v_s5_d, v_s5_w_glu, v_mla_w_in, v_mla_q_lora_norm, v_mla_kv_lora_norm, v_mla_w_uq, v_mla_w_ukv, v_mla_q_nope_norm, v_mla_k_nope_norm, v_mla_q_rope_norm, v_mla_k_rope_norm):
    weights = dict(ln_gain=ln_gain, w_out=w_out, mem_norm=mem_norm, w_mem_kv=w_mem_kv, xq_norm=xq_norm,
                   xk_norm=xk_norm, s5_w_in=s5_w_in, s5_lambda_re=s5_lambda_re, s5_lambda_im=s5_lambda_im,
                   s5_log_step=s5_log_step, s5_b_re=s5_b_re, s5_b_im=s5_b_im, s5_c_re=s5_c_re, s5_c_im=s5_c_im,
                   s5_d=s5_d, s5_w_glu=s5_w_glu, mla_w_in=mla_w_in, mla_q_lora_norm=mla_q_lora_norm,
                   mla_kv_lora_norm=mla_kv_lora_norm, mla_w_uq=mla_w_uq, mla_w_ukv=mla_w_ukv,
                   mla_q_nope_norm=mla_q_nope_norm, mla_k_nope_norm=mla_k_nope_norm,
                   mla_q_rope_norm=mla_q_rope_norm, mla_k_rope_norm=mla_k_rope_norm)
    m_in = dict(zip(_WEIGHTS, (m_ln_gain, m_w_out, m_mem_norm, m_w_mem_kv, m_xq_norm, m_xk_norm, m_s5_w_in,
                               m_s5_lambda_re, m_s5_lambda_im, m_s5_log_step, m_s5_b_re, m_s5_b_im, m_s5_c_re,
                               m_s5_c_im, m_s5_d, m_s5_w_glu, m_mla_w_in, m_mla_q_lora_norm, m_mla_kv_lora_norm,
                               m_mla_w_uq, m_mla_w_ukv, m_mla_q_nope_norm, m_mla_k_nope_norm, m_mla_q_rope_norm,
                               m_mla_k_rope_norm)))
    v_in = dict(zip(_WEIGHTS, (v_ln_gain, v_w_out, v_mem_norm, v_w_mem_kv, v_xq_norm, v_xk_norm, v_s5_w_in,
                               v_s5_lambda_re, v_s5_lambda_im, v_s5_log_step, v_s5_b_re, v_s5_b_im, v_s5_c_re,
                               v_s5_c_im, v_s5_d, v_s5_w_glu, v_mla_w_in, v_mla_q_lora_norm, v_mla_kv_lora_norm,
                               v_mla_w_uq, v_mla_w_ukv, v_mla_q_nope_norm, v_mla_k_nope_norm, v_mla_q_rope_norm,
                               v_mla_k_rope_norm)))

    x0 = x[0]
    mem0 = mem[0]
    target = loss_target[0]
    L = x0.shape[0]
    nblk, sub = 8, 1
    me = 4 * lax.axis_index("x") + 2 * lax.axis_index("y") + lax.axis_index("c")

    lora = jnp.pad(jnp.concatenate([mla_q_lora_norm, mla_kv_lora_norm], axis=1), ((0, 7), (0, HD - 96)))
    def gather(*shards):
        return _plan_all_gather(list(shards))

    kh = D_MODEL // 2
    (b_mkv0, b_glu, b_in_mla, b_out0, b_uq, b_ukv, b_mkv1, b_out1), (W_in_s5,) = _cast_call(
        [w_mem_kv[0], s5_w_glu[0], jnp.transpose(mla_w_in[0]), w_out[0], jnp.transpose(mla_w_uq[0]), mla_w_ukv[0],
         w_mem_kv[1], w_out[1]], "cast_shards", host=gather(s5_w_in[0].astype(BF16)))

    ln0, ln1 = ln_gain[0:1], ln_gain[1:2]
    gq0, gq1 = xq_norm[0:1], xq_norm[1:2]
    gk0, gk1 = xk_norm[0:1], xk_norm[1:2]
    gm0, gm1 = mem_norm[0:1], mem_norm[1:2]
    gqn, gkn = mla_q_nope_norm, mla_k_nope_norm
    gqr, gkr = _pad128(mla_q_rope_norm), _pad128(mla_k_rope_norm)

    lr3 = s5_lambda_re.reshape(S5_G, 1, S5_P)
    li3 = s5_lambda_im.reshape(S5_G, 1, S5_P)
    ls3 = s5_log_step.reshape(S5_G, 1, 1)
    btr = jnp.swapaxes(s5_b_re[0], 1, 2)
    bti = jnp.swapaxes(s5_b_im[0], 1, 2)
    a_r, a_i, bbr, bbi = _s5_params(lr3, li3, ls3, btr, bti)
    bm, bmt, cm, cmt = _s5_mats(bbr, bbi, s5_c_re[0], s5_c_im[0])
    a_r2 = a_r.reshape(1, S5_G * S5_P)
    a_i2 = a_i.reshape(1, S5_G * S5_P)
    cmask, rmat = _s5_compact_consts()

    half = ROPE // 2
    inv_freq = ROPE_THETA ** (-jnp.arange(half, dtype=F32) / half)
    invf = jnp.concatenate([inv_freq, inv_freq, jnp.zeros((HD - ROPE,), F32)]).reshape(1, HD)

    def rot_tables(pos, invf):
        ang = pos.astype(F32) * invf
        lane = lax.broadcasted_iota(jnp.int32, ang.shape, 1)
        c = jnp.where(lane < ROPE, jnp.cos(ang), 0.0)
        s = jnp.sin(ang)
        return c, jnp.where(lane < half, -s, 0.0), jnp.where((lane >= half) & (lane < ROPE), s, 0.0)

    tc, ts1, ts2 = _rowwise("rot_tables", rot_tables, [('r', positions.reshape(L, 1)), ('c', invf)],
                            [('r', (L, HD), F32)] * 3, nblk, sub)

    def in_s5(x, g, w):
        proj = _mm_slots(_rms(x, g, D_MODEL).astype(BF16), w)
        return proj[:, :PRIM], proj[:, PRIM:PRIM + XQ], proj[:, PRIM + XQ:]

    (u_s5, xq_a, gate_a), (G_mkv0,) = _rowwise(
        "s5_in", in_s5, [('r', x0), ('c', ln0), ('c', W_in_s5)],
        [('r', (L, PRIM), F32), ('r', (L, XQ), F32), ('r', (L, BRANCH), F32)], nblk, sub, host=gather(b_mkv0))
    (y_s5, s5_carry), (W_glu, G_in_mla_a) = _s5_fwd(u_s5, bm, cm, a_r2, a_i2, s5_d,
                                                    host=gather(b_glu, b_in_mla[:, :kh]))

    def glu(y, w):
        z = _mm_slots(_gelu(y).astype(BF16), w)
        return (z[:, :PRIM] * _sigmoid(z[:, PRIM:]),)

    (y2,), (G_out0,) = _rowwise("s5_glu", glu, [('r', y_s5), ('c', W_glu)], [('r', (L, PRIM), F32)], nblk, sub,
                                host=gather(b_out0))
    W_mkv0 = G_mkv0.reshape(D_MODEL, 2 * XQ)
    k_a, v_a = _kv_prep(mem0, gm0, W_mkv0, gk0, "kv_prep0")
    x1, (G_in_mla_b,) = _forward_merge(
        x0, y2, 'r', xq_a, gate_a, k_a, v_a, gq0, G_out0.reshape(BRANCH, D_MODEL), "merge0", nblk, sub,
        host=gather(b_in_mla[:, kh:]))
    W_in_mla = _mla_in_rows(jnp.concatenate([G_in_mla_a, G_in_mla_b], axis=2).reshape(_MLA_IN, D_MODEL))

    def in_mla(x, g, w):
        proj = _dot_nt(_rms(x, g, D_MODEL).astype(BF16), w)
        return proj[:, :512], proj[:, 512:768], proj[:, 768:1280], proj[:, 1280:3328], proj[:, 3328:]

    (c_q, c_kv, xq_b, gate_b, krp), (G_uq, W_kv, G_lora) = _rowwise(
        "mla_in", in_mla, [('r', x1), ('c', ln1), ('c', W_in_mla)],
        [('r', (L, Q_LORA), F32), ('r', (L, KV_LORA), F32), ('r', (L, XQ), F32), ('r', (L, BRANCH), F32),
         ('r', (L, HD), F32)], nblk, sub,
        host=gather(b_uq, b_ukv, lora))
    W_q = _uq_rows(G_uq.reshape(MLA_H * (HD + ROPE), Q_LORA))
    g_qlora = G_lora[:, 0, :64].reshape(1, Q_LORA)
    g_kvlora = G_lora[:, 0, 64:96].reshape(1, KV_LORA)

    def qkv(c_q, c_kv, krp, tc, ts1, ts2, gql, gkvl, wq, wkv, gqn, gkn, gqr, gkr):
        q = _dot_nt(_rms(c_q, gql, Q_LORA).astype(BF16), wq)
        kv = _mm_slots(_rms(c_kv, gkvl, KV_LORA).astype(BF16), wkv)
        kp, v = _kv_post(kv, krp, gkn, gkr, tc, ts1, ts2)
        return _q_post(q, gqn, gqr, tc, ts1, ts2), kp, v

    qkv_consts = [('c', g_qlora), ('c', g_kvlora), ('c', W_q), ('c', W_kv), ('c', gqn), ('c', gkn), ('c', gqr),
                  ('c', gkr)]
    (q_pad, k_pad, v_h), (G_mkv1, G_out1) = _rowwise(
        "mla_qkv", qkv, [('r', c_q), ('r', c_kv), ('r', krp), ('r', tc), ('r', ts1), ('r', ts2)] + qkv_consts,
        [('r', (L, 2 * PRIM), BF16), ('r', (L, 2 * PRIM), BF16), ('r', (L, PRIM), BF16)], nblk, sub,
        host=gather(b_mkv1, b_out1))
    W_out = (G_out0.reshape(BRANCH, D_MODEL), G_out1.reshape(BRANCH, D_MODEL))
    W_mkv = (W_mkv0, G_mkv1.reshape(D_MODEL, 2 * XQ))
    scale = (HD + ROPE) ** -0.5
    attn, lse = _attn_fwd(q_pad, k_pad, v_h, scale)
    k_b, v_b = _kv_prep(mem0, gm1, W_mkv[1], gk1, "kv_prep1")

    def merge_loss(x, mix, xq, gate, k, v, gq, wout, t):
        err = x + _dot(_merge(mix, xq, gate, k, v, gq).astype(BF16), wout) - t
        part = 0.5 * jnp.sum(jnp.sum(err * err, axis=-1, keepdims=True) * (1.0 / D_MODEL), axis=0, keepdims=True)
        return err * (1.0 / D_MODEL), jnp.broadcast_to(part, (1, HD))

    dx2, loss_part = _rowwise(
        "merge1_loss", merge_loss,
        [('r', x1), ('r', attn), ('r', xq_b), ('r', gate_b), ('c', k_b), ('c', v_b), ('c', gq1), ('c', W_out[1]),
         ('r', target)], [('r', (L, D_MODEL), F32), ('a', (1, HD), F32)], nblk, sub)

    dattn, dxq_b, dgate_b, o_b, g_b, dk_b, dv_b, dgq1 = _backward_merge(
        dx2, attn, 'r', xq_b, gate_b, k_b, v_b, gq1, W_out[1], "merge1_bwd", nblk, sub)
    dgm1, dW_mkv1, dgk1 = _kv_prep_bwd(mem0, gm1, W_mkv[1], gk1, dk_b, dv_b, "kv_prep1_bwd")
    dW_out1 = _matmul_tn(o_b, g_b, "dw_out1")
    dq_pad, dk_pad, dv_h = _attn_bwd(q_pad, k_pad, v_h, attn, lse, dattn, scale)

    def qkv_bwd(c_q, c_kv, krp, tc, ts1, ts2, dqp, dkp, dv, gql, gkvl, wq, wkv, gqn, gkn, gqr, gkr):
        cqn, vjp_qn = jax.vjp(lambda a, b: _rms(a, b, Q_LORA), c_q, gql)
        ckvn, vjp_kvn = jax.vjp(lambda a, b: _rms(a, b, KV_LORA), c_kv, gkvl)
        cqn16 = cqn.astype(BF16)
        ckvn16 = ckvn.astype(BF16)
        q = _dot_nt(cqn16, wq)
        kv = _mm_slots(ckvn16, wkv)
        _, vjp_q = jax.vjp(lambda a, b, c: _q_post(a, b, c, tc, ts1, ts2), q, gqn, gqr)
        dq, dgqn, dgqr = vjp_q(dqp.astype(F32))
        _, vjp_kv = jax.vjp(lambda a, b, c, d: _kv_post(a, b, c, d, tc, ts1, ts2), kv, krp, gkn, gkr)
        dkv, dkrp, dgkn, dgkr = vjp_kv((dkp.astype(F32), dv.astype(F32)))
        dq16 = dq.astype(BF16)
        dkv16 = dkv.astype(BF16)
        dc_q, dgql = vjp_qn(_dot(dq16, wq))
        dc_kv, dgkvl = vjp_kvn(_mm_slots_nt(dkv16, wkv))
        return dc_q, dc_kv, dkrp, cqn16, dq16, ckvn16, dkv16, dgql, dgkvl, dgqn, dgkn, dgqr, dgkr

    (dc_q, dc_kv, dkrp, cqn16, dq16, ckvn16, dkv16, dgql, dgkvl, dgqn, dgkn, dgqr, dgkr) = _rowwise(
        "mla_qkv_bwd", qkv_bwd,
        [('r', c_q), ('r', c_kv), ('r', krp), ('r', tc), ('r', ts1), ('r', ts2), ('r', dq_pad), ('r', dk_pad),
         ('r', dv_h)] + qkv_consts,
        [('r', (L, Q_LORA), BF16), ('r', (L, KV_LORA), BF16), ('r', (L, HD), BF16), ('r', (L, Q_LORA), BF16),
         ('t', (2 * PRIM, L), BF16), ('t', (KV_LORA, L), BF16), ('r', (L, 2 * PRIM), BF16),
         ('a', (1, Q_LORA), F32), ('a', (1, KV_LORA), F32), ('a', (1, HD), F32), ('a', (1, HD), F32),
         ('a', (1, HD), F32), ('a', (1, HD), F32)], nblk, sub)
    dW_q = _matmul_tn(dq16, cqn16, "dw_uq")
    dW_kv = _matmul_tn_slots(ckvn16, dkv16, "dw_ukv")

    def in_bwd(x, dres, g, w, *dparts):
        dproj = jnp.concatenate(dparts, axis=-1).astype(BF16)
        xn, vjp = jax.vjp(lambda a, b: _rms(a, b, D_MODEL), x, g)
        dx, dg = vjp(_mm_slots_nt(dproj, w) if w.ndim == 3 else _dot(dproj, w))
        return dx + dres, xn, dproj, dg

    dx1, xn1, dproj1, dln1 = _rowwise(
        "mla_in_bwd", in_bwd,
        [('r', x1), ('r', dx2), ('c', ln1), ('c', W_in_mla), ('r', dc_q), ('r', dc_kv), ('r', dxq_b), ('r', dgate_b),
         ('r', dkrp)],
        [('r', (L, D_MODEL), F32), ('r', (L, D_MODEL), BF16), ('t', (_MLA_IN_PAD, L), BF16), ('a', (1, D_MODEL), F32)],
        nblk, sub)
    dW_in_mla = _matmul_tn(dproj1, xn1, "dw_mla_in")

    grads1 = [dW_out1.reshape(N_DEV, 256, D_MODEL), dW_mkv1.reshape(N_DEV, 128, 2 * XQ),
              _mla_in_rows_back(dW_in_mla).reshape(N_DEV, 424, D_MODEL),
              _uq_rows_back(dW_q).reshape(N_DEV, 288, Q_LORA), dW_kv]
    (dy2, dxq_a, dgate_a, o_a, g_a, dk_a, dv_a, dgq0), pair1 = _backward_merge(
        dx1, y2, 'r', xq_a, gate_a, k_a, v_a, gq0, W_out[0], "merge0_bwd", nblk, sub, host=_plan_pair(grads1))
    dgm0, dW_mkv0, dgk0 = _kv_prep_bwd(mem0, gm0, W_mkv[0], gk0, dk_a, dv_a, "kv_prep0_bwd")
    dW_out0 = _matmul_tn(o_a, g_a, "dw_out0")
    t1 = list(_pair_add(grads1, pair1, "rs_add_layer1"))

    def glu_bwd(y, dy2, w):
        h, vjp_h = jax.vjp(_gelu, y)
        h16 = h.astype(BF16)
        z = _mm_slots(h16, w)
        _, vjp_z = jax.vjp(lambda z: z[:, :PRIM] * _sigmoid(z[:, PRIM:]), z)
        dz16 = vjp_z(dy2)[0].astype(BF16)
        return vjp_h(_mm_slots_nt(dz16, w))[0], h16, dz16

    grads0 = [dW_out0.reshape(N_DEV, 256, D_MODEL), dW_mkv0.reshape(N_DEV, 128, 2 * XQ)]
    (dy_s5, h16, dz16), glu_hosted = _rowwise(
        "s5_glu_bwd", glu_bwd, [('r', y_s5), ('r', dy2), ('c', W_glu)],
        [('r', (L, PRIM), F32), ('t', (PRIM, L), BF16), ('r', (L, 2 * PRIM), BF16)], nblk, sub,
        host=_combine(_plan_chips(t1[2:]), _plan_pair(grads0)))
    recv_proj1, pair0 = glu_hosted[:3], glu_hosted[3:]
    dW_glu = _matmul_tn_slots(h16, dz16, "dw_glu")
    t0 = list(_pair_add(grads0 + [dW_glu], pair0 + list(_exchange_call(_plan_pair([dW_glu]), "rs_pair_glu")),
                        "rs_add_layer0"))
    (du_s5, dbc, dcc, dd, dar, dai), recv_rest = _s5_bwd(u_s5, dy_s5, s5_carry, bm, bmt, cmt, a_r2, a_i2, s5_d,
                                                        cmask, rmat, host=_plan_chips(t1[:2] + t0))
    early_recv = recv_rest[:2] + recv_proj1 + recv_rest[2:]
    dbc4 = dbc.reshape(S5_G, S5_C, 2, S5_P)
    dcc4 = dcc.reshape(S5_G, S5_C, 2, S5_P)
    dlr, dli, dls, dbtr, dbti = _s5_params_bwd(
        lr3, li3, ls3, btr, bti, dar.reshape(S5_G, 1, S5_P), dai.reshape(S5_G, 1, S5_P), dbc4[:, :, 0], dbc4[:, :, 1])

    small_part = {
        "ln_gain": jnp.concatenate([jnp.zeros_like(dln1), dln1]), "mem_norm": jnp.concatenate([dgm0, dgm1]),
        "xq_norm": jnp.concatenate([dgq0, dgq1]), "xk_norm": jnp.concatenate([dgk0, dgk1]),
        "s5_lambda_re": dlr, "s5_lambda_im": dli, "s5_log_step": dls,
        "s5_b_re": jnp.swapaxes(dbtr, 1, 2), "s5_b_im": jnp.swapaxes(dbti, 1, 2),
        "s5_c_re": dcc4[:, :, 0], "s5_c_im": -dcc4[:, :, 1], "s5_d": dd,
        "mla_q_lora_norm": dgql, "mla_kv_lora_norm": dgkvl, "mla_q_nope_norm": dgqn, "mla_k_nope_norm": dgkn,
        "mla_q_rope_norm": dgqr[:, :ROPE], "mla_k_rope_norm": dgkr[:, :ROPE],
    }
    loss8 = jnp.pad(loss_part, ((0, 7), (0, 0)))
    (dx0, xn0, dproj0, dln0), (small_gath, loss_g) = _rowwise(
        "s5_in_bwd", in_bwd,
        [('r', x0), ('r', dx1), ('c', ln0), ('c', W_in_s5), ('r', du_s5), ('r', dxq_a),
         ('r', dgate_a)],
        [('r', (L, D_MODEL), F32), ('t', (D_MODEL, L), BF16), ('r', (L, 2 * BRANCH), BF16), ('a', (1, D_MODEL), F32)],
        nblk, sub, host=_plan_all_gather([_pack_small(small_part).astype(BF16), loss8]))
    dW_in_s5 = _matmul_tn_slots(xn0, dproj0, "dw_s5_in")

    late = [dW_in_s5]
    late_t = _pair_add(late, list(_exchange_call(_plan_pair(late), "rs_pair_late")), "rs_add_late")
    owners = [("w_out", 1), ("w_mem_kv", 1), ("mla_w_in", 0), ("mla_w_uq", 0), ("mla_w_ukv", 0), ("w_out", 0),
              ("w_mem_kv", 0), ("s5_w_glu", 0)]
    flipped = ("mla_w_in", "mla_w_uq")

    def shard(d, n, i):
        return jnp.transpose(d[n][i]) if n in flipped else d[n][i]

    upd, (late_recv, ln0_gath) = _updates_call(
        early_recv, [shard(weights, n, i) for n, i in owners], [shard(m_in, n, i) for n, i in owners],
        [shard(v_in, n, i) for n, i in owners], "update_early",
        host=_combine(_plan_chips(late_t), _plan_all_gather([jnp.pad(dln0, ((0, 7), (0, 0)))])))
    owners.append(("s5_w_in", 0))
    upd.append(_sum_adamw(late_recv, s5_w_in[0], m_s5_w_in[0], v_s5_w_in[0], "update_s5_w_in"))
    grads, delta, new_m, new_v = {}, {}, {}, {}
    for n in _BIG:
        parts = [u for u, (o, _) in sorted(zip(upd, owners), key=lambda t: t[1][1]) if o == n]
        if n in flipped:
            grads[n], delta[n], new_m[n], new_v[n] = (jnp.transpose(parts[0][j])[None] for j in range(4))
        else:
            grads[n], delta[n], new_m[n], new_v[n] = (jnp.stack([p[j] for p in parts]) for j in range(4))

    gs, loss_sum = _small_sum(small_gath, loss_g, ln0_gath, "small_sum")
    loss = loss_sum[0, 0]
    for n, _ in _SMALL:
        shape = weights[n].shape
        if n == "mla_q_lora_norm":
            grads[n] = lax.dynamic_slice(_unpack_small(gs, n, (Q_LORA,)), (me * 64,), (64,)).reshape(shape)
        elif n == "mla_kv_lora_norm":
            grads[n] = lax.dynamic_slice(_unpack_small(gs, n, (KV_LORA,)), (me * 32,), (32,)).reshape(shape)
        else:
            grads[n] = _unpack_small(gs, n, shape)

    def own(n, a):
        if a.ndim == 4:
            a = jnp.transpose(a, (0, 2, 3, 1))
        elif a.ndim == 3:
            a = jnp.transpose(a, (0, 2, 1))
        return a.reshape(a.shape[1:]) if a.ndim >= 3 else a

    def back(n, a):
        shape = weights[n].shape
        if len(shape) == 4:
            return jnp.transpose(a.reshape((1,) + a.shape), (0, 3, 1, 2))
        if len(shape) == 3:
            return jnp.transpose(a.reshape((1,) + a.shape), (0, 2, 1))
        return a.reshape(shape)

    wide = ("s5_b_re", "s5_b_im", "s5_c_re", "s5_c_im")
    for names, nb, call in (([n for n, _ in _SMALL if n not in wide], 1, "update_small"), (wide, 4, "update_s5_bc")):
        res = _adamw_multi([own(n, weights[n]) for n in names], [own(n, grads[n]) for n in names],
                           [own(n, m_in[n]) for n in names], [own(n, v_in[n]) for n in names], call, nb)
        for n, (dl, m2, v2) in zip(names, res):
            delta[n], new_m[n], new_v[n] = back(n, dl), back(n, m2), back(n, v2)
    return (loss, dx0[None], *[grads[n] for n in _WEIGHTS], *[delta[n] for n in _WEIGHTS],
            *[new_m[n] for n in _WEIGHTS], *[new_v[n] for n in _WEIGHTS])
```

```python
import functools
import math

import numpy as np
import jax
import jax.numpy as jnp
from jax import lax
from jax.experimental import pallas as pl
from jax.experimental.pallas import tpu as pltpu

F32 = jnp.float32
BF16 = jnp.bfloat16
EPS = 1e-6
NEG = float(np.finfo(np.float32).min)
MESH = pl.DeviceIdType.MESH

N_DEV = 8
D_MODEL = 1024
MEM_LEN = 256
XQ = 512
PRIM = 1536
BRANCH = 2048
X_HEADS = 4
HD = 128
S5_G = 96
S5_P = 64
S5_C = 16
S5_GB = 8
S5_W = S5_GB * S5_P
MLA_H = 12
ROPE = 64
Q_LORA = 512
KV_LORA = 256
ROPE_THETA = 10000.0

ADAM_LR = 0.001
ADAM_B1 = 0.9
ADAM_B2 = 0.999
ADAM_EPS = 1e-08
ADAM_WD = 0.01
ADAM_STEP = 10

VMEM_LIMIT = 56 * 1024 * 1024


def _dot(a, b):
    return jnp.dot(a, b, preferred_element_type=F32)


def _dot_nt(a, b):
    return lax.dot_general(a, b, (((1,), (1,)), ((), ())), preferred_element_type=F32)


def _dot_tn(a, b):
    return lax.dot_general(a, b, (((0,), (0,)), ((), ())), preferred_element_type=F32)


@jax.custom_vjp
def _mm(a, b):
    return _dot(a.astype(BF16), b.astype(BF16))


def _mm_fwd(a, b):
    return _mm(a, b), (a, b)


def _mm_bwd(res, g):
    a, b = res
    gb = g.astype(BF16)
    return _dot_nt(gb, b.astype(BF16)).astype(a.dtype), _dot_tn(a.astype(BF16), gb).astype(b.dtype)


_mm.defvjp(_mm_fwd, _mm_bwd)


@jax.custom_vjp
def _mm_nt(a, b):
    return _dot_nt(a.astype(BF16), b.astype(BF16))


def _mm_nt_fwd(a, b):
    return _mm_nt(a, b), (a, b)


def _mm_nt_bwd(res, g):
    a, b = res
    gb = g.astype(BF16)
    return _dot(gb, b.astype(BF16)).astype(a.dtype), _dot_tn(gb, a.astype(BF16)).astype(b.dtype)


_mm_nt.defvjp(_mm_nt_fwd, _mm_nt_bwd)


@jax.custom_vjp
def _softmax(s):
    m = jnp.max(s, axis=-1, keepdims=True)
    e = jnp.exp(s - m)
    return e / jnp.sum(e, axis=-1, keepdims=True)


def _softmax_fwd(s):
    p = _softmax(s)
    return p, p


def _softmax_bwd(p, g):
    return (p * (g - jnp.sum(p * g, axis=-1, keepdims=True)),)


_softmax.defvjp(_softmax_fwd, _softmax_bwd)


def _rms(x, g, n):
    ms = jnp.sum(x * x, axis=-1, keepdims=True) * (1.0 / n)
    return x * lax.rsqrt(ms + EPS) * g


def _sigmoid(x):
    return 1.0 / (1.0 + jnp.exp(-x))


def _silu(x):
    return x * _sigmoid(x)


def _gelu(x):
    c = math.sqrt(2.0 / math.pi)
    return 0.5 * x * (1.0 + jnp.tanh(c * (x + 0.044715 * (x * x * x))))


@jax.custom_vjp
def _rot(x, c, s1, s2):
    return x * c + pltpu.roll(x, 96, 1) * s1 + pltpu.roll(x, 32, 1) * s2


def _rot_fwd(x, c, s1, s2):
    return _rot(x, c, s1, s2), (c, s1, s2)


def _rot_bwd(res, g):
    c, s1, s2 = res
    dx = g * c + pltpu.roll(g * s1, 32, 1) + pltpu.roll(g * s2, 96, 1)
    return dx, jnp.zeros_like(c), jnp.zeros_like(s1), jnp.zeros_like(s2)


_rot.defvjp(_rot_fwd, _rot_bwd)


def _mem_attn(xq, k, v, gq):
    outs = []
    for h in range(X_HEADS):
        sl = slice(HD * h, HD * (h + 1))
        q = _rms(xq[:, sl], gq, HD)
        p = _softmax(_mm_nt(q, k[:, sl]) * (HD ** -0.5))
        outs.append(_mm(p, v[:, sl]))
    return jnp.concatenate(outs, axis=-1)


def _merge(mix, xq, gate, k, v, gq):
    return jnp.concatenate([mix, _mem_attn(xq, k, v, gq)], axis=-1) * _silu(gate)


def _q_post(q, gqn, gqr, c, s1, s2):
    pieces = []
    for h in range(MLA_H):
        pieces.append(_rms(q[:, HD * h:HD * (h + 1)], gqn, HD))
        pieces.append(_rot(_rms(q[:, PRIM + HD * h:PRIM + HD * (h + 1)], gqr, ROPE), c, s1, s2))
    return jnp.concatenate(pieces, axis=-1)


def _kv_post(kv, krp, gkn, gkr, c, s1, s2):
    kr = _rot(_rms(krp, gkr, ROPE), c, s1, s2)
    pieces, vals = [], []
    for h in range(MLA_H):
        pieces.append(_rms(kv[:, 2 * HD * h:2 * HD * h + HD], gkn, HD))
        pieces.append(kr)
        vals.append(kv[:, 2 * HD * h + HD:2 * HD * (h + 1)])
    return jnp.concatenate(pieces, axis=-1), jnp.concatenate(vals, axis=-1)


def _rowwise(name, fn, ins, outs, nblk, sub=1, host=None):
    n_in = len(ins)

    def spec(kind, shape):
        if kind == 'r':
            return pl.BlockSpec((shape[0] // nblk, shape[1]), lambda i: (i, 0))
        if kind == 't':
            return pl.BlockSpec((shape[0], shape[1] // nblk), lambda i: (0, i))
        zeros = (0,) * len(shape)
        return pl.BlockSpec(tuple(shape), lambda i: zeros)

    def body(*refs):
        i = pl.program_id(0)
        res = fn(*[r[...] for r in refs[:n_in]])
        for (kind, _, _), ref, val in zip(outs, refs[n_in:], res):
            if kind == 'a':
                @pl.when(i == 0)
                def _():
                    ref[...] = jnp.zeros_like(ref)
                ref[...] += val.astype(ref.dtype)
            elif kind == 't':
                ref[...] = val.astype(F32).T.astype(ref.dtype)
            else:
                ref[...] = val.astype(ref.dtype)

    res, hosted = _hosting_call(
        body, name, nblk, host, [a for _, a in ins], [spec(k, a.shape) for k, a in ins],
        [jax.ShapeDtypeStruct(tuple(s), d) for _, s, d in outs], [spec(k, s) for k, s, _ in outs], [])
    return res if host is None else (res, hosted)


def _matmul_tn(at, g, name, out_dtype=BF16):
    K, L = at.shape
    N = g.shape[1]
    tn = next(t for t in (512, 384, 256, 128) if N % t == 0)

    def body(a_ref, g_ref, o_ref):
        o_ref[...] = _dot(a_ref[...], g_ref[...]).astype(o_ref.dtype)

    return pl.pallas_call(
        body, name=name, grid=(N // tn,),
        in_specs=[pl.BlockSpec((K, L), lambda n: (0, 0)), pl.BlockSpec((L, tn), lambda n: (0, n))],
        out_specs=pl.BlockSpec((K, tn), lambda n: (0, n)),
        out_shape=jax.ShapeDtypeStruct((K, N), out_dtype),
        compiler_params=pltpu.CompilerParams(dimension_semantics=("arbitrary",), vmem_limit_bytes=VMEM_LIMIT),
    )(at, g)


def _matmul_tn_slots(at, g, name, host=None):
    K, L = at.shape
    n = g.shape[1] // N_DEV

    def body(a_ref, g_ref, o_ref):
        o_ref[...] = _dot(a_ref[...], g_ref[...]).astype(o_ref.dtype)

    res, hosted = _hosting_call(
        body, name, N_DEV, host, [at, g],
        [pl.BlockSpec((K, L), lambda d: (0, 0)), pl.BlockSpec((L, n), lambda d: (0, d))],
        [jax.ShapeDtypeStruct((N_DEV, K, n), BF16)], [pl.BlockSpec((None, K, n), lambda d: (d, 0, 0))], [])
    return res[0] if host is None else (res[0], hosted)


def _mm_slots(a16, w):
    return jnp.concatenate([_dot(a16, w[d]) for d in range(N_DEV)], axis=-1)


def _mm_slots_nt(g16, w):
    n = w.shape[2]
    out = _dot_nt(g16[:, 0:n], w[0])
    for d in range(1, N_DEV):
        out = out + _dot_nt(g16[:, d * n:(d + 1) * n], w[d])
    return out


class _Exchange:
    def __init__(self, ins, outs, scratch, start, finish):
        self.ins, self.outs, self.scratch, self.start, self.finish = ins, outs, scratch, start, finish


def _xyc():
    return lax.axis_index("x"), lax.axis_index("y"), lax.axis_index("c")


def _plan_all_gather(xs):
    n = len(xs)

    def build(x_refs, out_refs, sems):
        send_sems, recv_sems, local_sems = sems
        x, y, c = _xyc()

        def copies(k, block, to, own=False):
            slot = 4 * block[0] + 2 * block[1] + block[2]
            return [pltpu.make_async_remote_copy(
                src_ref=x_refs[a] if own else out_refs[a].at[slot], dst_ref=out_refs[a].at[slot],
                send_sem=send_sems.at[k * n + a], recv_sem=recv_sems.at[k * n + a], device_id=to,
                device_id_type=MESH) for a in range(n)]

        mine = [pltpu.make_async_copy(x_refs[a], out_refs[a].at[4 * x + 2 * y + c], local_sems.at[a])
                for a in range(n)]
        return copies, mine, (x, y, c), [(1 - x, y), (x, 1 - y), (1 - x, 1 - y)]

    def first_copies(copies, me, chips):
        x, y, c = me
        first = copies(0, me, (x, y, 1 - c), own=True)
        for j, chip in enumerate(chips):
            first += copies(1 + j, me, (*chip, c), own=True)
        return first

    def start(x_refs, out_refs, sems):
        copies, mine, me, chips = build(x_refs, out_refs, sems)
        for cp in mine + first_copies(copies, me, chips):
            cp.start()

    def finish(x_refs, out_refs, sems):
        copies, mine, me, chips = build(x_refs, out_refs, sems)
        x, y, c = me
        passed = []
        for j, chip in enumerate(chips):
            for cp in copies(1 + j, (*chip, c), me):
                cp.wait_recv()
            fwd = copies(4 + j, (*chip, c), (x, y, 1 - c))
            for cp in fwd:
                cp.start()
            passed += fwd
        for cp in copies(0, (x, y, 1 - c), me):
            cp.wait_recv()
        for j, chip in enumerate(chips):
            for cp in copies(4 + j, (*chip, 1 - c), me):
                cp.wait_recv()
        for cp in first_copies(copies, me, chips) + passed:
            cp.wait_send()
        for cp in mine:
            cp.wait()

    return _Exchange(list(xs), [jax.ShapeDtypeStruct((N_DEV,) + a.shape, a.dtype) for a in xs],
                     [pltpu.SemaphoreType.DMA((7 * n,)), pltpu.SemaphoreType.DMA((7 * n,)),
                      pltpu.SemaphoreType.DMA((n,))], start, finish)


_CHIPS = ((0, 0), (0, 1), (1, 0), (1, 1))


def _plan_pair(sends):
    n = len(sends)

    def build(s_refs, o_refs, sems):
        send_sems, recv_sems = sems
        x, y, c = _xyc()
        return [pltpu.make_async_remote_copy(
            src_ref=s_refs[a].at[4 * px + 2 * py + 1 - c], dst_ref=o_refs[a].at[j],
            send_sem=send_sems.at[j * n + a], recv_sem=recv_sems.at[j * n + a], device_id=(x, y, 1 - c),
            device_id_type=MESH) for j, (px, py) in enumerate(_CHIPS) for a in range(n)]

    def start(s_refs, o_refs, sems):
        for cp in build(s_refs, o_refs, sems):
            cp.start()

    def finish(s_refs, o_refs, sems):
        for cp in build(s_refs, o_refs, sems):
            cp.wait_recv()
            cp.wait_send()

    return _Exchange(list(sends), [jax.ShapeDtypeStruct((4,) + a.shape[1:], a.dtype) for a in sends],
                     [pltpu.SemaphoreType.DMA((4 * n,)), pltpu.SemaphoreType.DMA((4 * n,))], start, finish)


def _plan_chips(ts):
    n = len(ts)
    flips = ((1, 0), (0, 1), (1, 1))

    def build(t_refs, o_refs, sems):
        send_sems, recv_sems, local_sems = sems
        x, y, c = _xyc()
        mine = 2 * x + y
        local = [pltpu.make_async_copy(t_refs[a].at[mine], o_refs[a].at[mine], local_sems.at[a]) for a in range(n)]
        remote = []
        for k, (fx, fy) in enumerate(flips):
            px = 1 - x if fx else x
            py = 1 - y if fy else y
            remote += [pltpu.make_async_remote_copy(
                src_ref=t_refs[a].at[2 * px + py], dst_ref=o_refs[a].at[mine],
                send_sem=send_sems.at[k * n + a], recv_sem=recv_sems.at[k * n + a], device_id=(px, py, c),
                device_id_type=MESH) for a in range(n)]
        return local, remote

    def start(t_refs, o_refs, sems):
        local, remote = build(t_refs, o_refs, sems)
        for cp in local + remote:
            cp.start()

    def finish(t_refs, o_refs, sems):
        local, remote = build(t_refs, o_refs, sems)
        for cp in remote:
            cp.wait_recv()
        for cp in remote:
            cp.wait_send()
        for cp in local:
            cp.wait()

    return _Exchange(list(ts), [jax.ShapeDtypeStruct(a.shape, a.dtype) for a in ts],
                     [pltpu.SemaphoreType.DMA((3 * n,)), pltpu.SemaphoreType.DMA((3 * n,)),
                      pltpu.SemaphoreType.DMA((n,))], start, finish)


def _combine(*plans):
    def parts(refs, attr):
        out, at = [], 0
        for p in plans:
            n = len(getattr(p, attr))
            out.append(refs[at:at + n])
            at += n
        return out

    def run(half):
        def go(ins, outs, sems):
            for p, a, o, s in zip(plans, parts(ins, "ins"), parts(outs, "outs"), parts(sems, "scratch")):
                getattr(p, half)(a, o, s)
        return go

    return _Exchange(sum((p.ins for p in plans), []), sum((p.outs for p in plans), []),
                     sum((p.scratch for p in plans), []), run("start"), run("finish"))


def _exchange_call(plan, name):
    n = len(plan.ins)

    def body(*refs):
        ins, outs, sems = refs[:n], refs[n:2 * n], refs[2 * n:]
        plan.start(ins, outs, sems)
        plan.finish(ins, outs, sems)

    return pl.pallas_call(
        body, name=name, out_shape=plan.outs,
        in_specs=[pl.BlockSpec(memory_space=pl.ANY)] * n, out_specs=[pl.BlockSpec(memory_space=pl.ANY)] * n,
        scratch_shapes=plan.scratch,
    )(*plan.ins)


def _slab_spec(lead, rows, cols, nb):
    if rows % (nb * 16) == 0:
        return pl.BlockSpec((lead, rows // nb, cols), lambda i: (0, i, 0))
    if cols % (nb * 128) == 0:
        return pl.BlockSpec((lead, rows, cols // nb), lambda i: (0, 0, i))
    return pl.BlockSpec((lead, rows, cols), lambda i: (0, 0, 0))


def _slab_spec2(rows, cols, nb):
    if rows % (nb * 16) == 0:
        return pl.BlockSpec((rows // nb, cols), lambda i: (i, 0))
    if cols % (nb * 128) == 0:
        return pl.BlockSpec((rows, cols // nb), lambda i: (0, i))
    return pl.BlockSpec((rows, cols), lambda i: (0, 0))


def _cast_call(arrays, name, host=None):
    n = len(arrays)
    nb = 8

    def body(*refs):
        for a in range(n):
            refs[n + a][...] = refs[a][...].astype(BF16)

    specs = [_slab_spec2(x.shape[0], x.shape[1], nb) for x in arrays]
    return _hosting_call(body, name, nb, host, list(arrays), specs,
                         [jax.ShapeDtypeStruct(x.shape, BF16) for x in arrays], specs, [])


def _pair_add(sends, fromsib, name):
    n = len(sends)
    nb = 8

    def body(*refs):
        c = lax.axis_index("c")
        for a in range(n):
            s_ref, f_ref, t_ref = refs[a], refs[n + a], refs[2 * n + a]
            for j in range(4):
                t_ref[j] = (s_ref[2 * j + c].astype(F32) + f_ref[j].astype(F32)).astype(t_ref.dtype)

    def spec(a, lead):
        return _slab_spec(lead, a.shape[1], a.shape[2], nb)

    return pl.pallas_call(
        body, name=name, grid=(nb,),
        in_specs=[spec(a, N_DEV) for a in sends] + [spec(a, 4) for a in fromsib],
        out_specs=[spec(a, 4) for a in fromsib],
        out_shape=[jax.ShapeDtypeStruct(a.shape, a.dtype) for a in fromsib],
        compiler_params=pltpu.CompilerParams(dimension_semantics=("arbitrary",), vmem_limit_bytes=VMEM_LIMIT),
    )(*sends, *fromsib)


def _adamw_vals(w, g, m, v):
    m2 = ADAM_B1 * m + (1.0 - ADAM_B1) * g
    v2 = ADAM_B2 * v + (1.0 - ADAM_B2) * (g * g)
    m_hat = m2 / (1.0 - ADAM_B1 ** ADAM_STEP)
    v_hat = v2 / (1.0 - ADAM_B2 ** ADAM_STEP)
    delta = -ADAM_LR * (m_hat / (jnp.sqrt(v_hat) + ADAM_EPS) + ADAM_WD * w)
    return delta, m2, v2


def _sum_adamw(recv, w, m, v, name):
    R, C = w.shape
    ns = recv.shape[0]
    br = next((t for t in (256, 128, 64, 32, 16) if R % t == 0), R)

    def body(r_ref, w_ref, m_ref, v_ref, g_ref, d_ref, m2_ref, v2_ref):
        g = r_ref[0].astype(F32)
        for d in range(1, ns):
            g = g + r_ref[d].astype(F32)
        dl, m2, v2 = _adamw_vals(w_ref[...], g, m_ref[...], v_ref[...])
        g_ref[...] = g
        d_ref[...] = dl
        m2_ref[...] = m2
        v2_ref[...] = v2

    spec = pl.BlockSpec((br, C), lambda i: (i, 0))
    return pl.pallas_call(
        body, name=name, grid=(R // br,),
        in_specs=[pl.BlockSpec((ns, br, C), lambda i: (0, i, 0)), spec, spec, spec], out_specs=[spec] * 4,
        out_shape=[jax.ShapeDtypeStruct((R, C), F32)] * 4,
        compiler_params=pltpu.CompilerParams(dimension_semantics=("arbitrary",)),
    )(recv, w, m, v)


def _updates_call(recvs, ws, ms, vs, name, host=None):
    n = len(recvs)
    nb = 8

    def body(*refs):
        for a in range(n):
            r_ref, w_ref, m_ref, v_ref = refs[a], refs[n + a], refs[2 * n + a], refs[3 * n + a]
            g_ref, d_ref, m2_ref, v2_ref = refs[4 * n + 4 * a:4 * n + 4 * a + 4]
            g = r_ref[0].astype(F32)
            for d in range(1, r_ref.shape[0]):
                g = g + r_ref[d].astype(F32)
            dl, m2, v2 = _adamw_vals(w_ref[...], g, m_ref[...], v_ref[...])
            g_ref[...] = g
            d_ref[...] = dl
            m2_ref[...] = m2
            v2_ref[...] = v2

    def spec3(r):
        return _slab_spec(r.shape[0], r.shape[1], r.shape[2], nb)

    def spec2(w):
        return _slab_spec2(w.shape[0], w.shape[1], nb)

    res, hosted = _hosting_call(
        body, name, nb, host, list(recvs) + list(ws) + list(ms) + list(vs),
        [spec3(r) for r in recvs] + [spec2(w) for w in ws] * 3,
        [jax.ShapeDtypeStruct(w.shape, F32) for w in ws for _ in range(4)],
        [spec2(w) for w in ws for _ in range(4)], [])
    return [res[4 * a:4 * a + 4] for a in range(n)], hosted


def _small_sum(gath, loss_g, row0_g, name):
    _, R, C = gath.shape
    br = R // 3

    def body(g_ref, l_ref, r_ref, go_ref, lo_ref):
        g = g_ref[0].astype(F32)
        lsum = l_ref[0]
        for d in range(1, N_DEV):
            g = g + g_ref[d].astype(F32)
            lsum = lsum + l_ref[d]
        go_ref[...] = g
        lo_ref[...] = lsum

        @pl.when(pl.program_id(0) == 0)
        def _():
            row0 = r_ref[0]
            for d in range(1, N_DEV):
                row0 = row0 + r_ref[d]
            go_ref[0:8, :] = go_ref[0:8, :] + jnp.where(lax.broadcasted_iota(jnp.int32, row0.shape, 0) == 0, row0, 0.0)

    return pl.pallas_call(
        body, name=name, grid=(R // br,),
        in_specs=[pl.BlockSpec((N_DEV, br, C), lambda i: (0, i, 0)),
                  pl.BlockSpec((N_DEV, 8, HD), lambda i: (0, 0, 0)), pl.BlockSpec((N_DEV, 8, C), lambda i: (0, 0, 0))],
        out_specs=[pl.BlockSpec((br, C), lambda i: (i, 0)), pl.BlockSpec((8, HD), lambda i: (0, 0))],
        out_shape=[jax.ShapeDtypeStruct((R, C), F32), jax.ShapeDtypeStruct((8, HD), F32)],
        compiler_params=pltpu.CompilerParams(dimension_semantics=("arbitrary",)),
    )(gath, loss_g, row0_g)


def _adamw_multi(ws, gs, ms, vs, name, nblk=1):
    n = len(ws)

    def body(*refs):
        for a in range(n):
            dl, m2, v2 = _adamw_vals(refs[a][...], refs[n + a][...], refs[2 * n + a][...], refs[3 * n + a][...])
            refs[4 * n + 3 * a][...] = dl
            refs[4 * n + 3 * a + 1][...] = m2
            refs[4 * n + 3 * a + 2][...] = v2

    def spec(x):
        rest = (0,) * (x.ndim - 1)
        return pl.BlockSpec((x.shape[0] // nblk,) + tuple(x.shape[1:]), lambda i: (i,) + rest)

    res = pl.pallas_call(
        body, name=name, grid=(nblk,),
        in_specs=[spec(w) for w in ws] * 4, out_specs=[spec(w) for w in ws for _ in range(3)],
        out_shape=[jax.ShapeDtypeStruct(w.shape, F32) for w in ws for _ in range(3)],
        compiler_params=pltpu.CompilerParams(dimension_semantics=("arbitrary",), vmem_limit_bytes=VMEM_LIMIT),
    )(*ws, *gs, *ms, *vs)
    return [res[3 * a:3 * a + 3] for a in range(n)]


def _s5_param_fn(lr, li, ls, btr, bti):
    step = jnp.exp(ls)
    er = jnp.exp(lr * step)
    ang = li * step
    ar = er * jnp.cos(ang)
    ai = er * jnp.sin(ang)
    nr = ar - 1.0
    den = lr * lr + li * li
    fr = (nr * lr + ai * li) / den
    fi = (ai * lr - nr * li) / den
    return ar, ai, fr * btr - fi * bti, fr * bti + fi * btr


def _s5_params(lr, li, ls, btr, bti):
    def body(lr_ref, li_ref, ls_ref, br_ref, bi_ref, ar_ref, ai_ref, bbr_ref, bbi_ref):
        ar, ai, bbr, bbi = _s5_param_fn(lr_ref[...], li_ref[...], ls_ref[...], br_ref[...], bi_ref[...])
        ar_ref[...] = ar
        ai_ref[...] = ai
        bbr_ref[...] = bbr
        bbi_ref[...] = bbi

    sd = jax.ShapeDtypeStruct
    return pl.pallas_call(
        body, name="s5_params",
        out_shape=[sd(lr.shape, F32), sd(lr.shape, F32), sd(btr.shape, F32), sd(btr.shape, F32)],
    )(lr, li, ls, btr, bti)


def _s5_params_bwd(lr, li, ls, btr, bti, dar, dai, dbbr, dbbi):
    def body(lr_ref, li_ref, ls_ref, br_ref, bi_ref, dar_ref, dai_ref, dbbr_ref, dbbi_ref,
             dlr_ref, dli_ref, dls_ref, dbr_ref, dbi_ref):
        _, vjp = jax.vjp(_s5_param_fn, lr_ref[...], li_ref[...], ls_ref[...], br_ref[...], bi_ref[...])
        dlr, dli, dls, dbr, dbi = vjp((dar_ref[...], dai_ref[...], dbbr_ref[...], dbbi_ref[...]))
        dlr_ref[...] = dlr
        dli_ref[...] = dli
        dls_ref[...] = dls
        dbr_ref[...] = dbr
        dbi_ref[...] = dbi

    sd = jax.ShapeDtypeStruct
    return pl.pallas_call(
        body, name="s5_params_bwd",
        out_shape=[sd(lr.shape, F32), sd(lr.shape, F32), sd(ls.shape, F32), sd(btr.shape, F32), sd(btr.shape, F32)],
    )(lr, li, ls, btr, bti, dar, dai, dbbr, dbbi)


def _cpow(ar, ai, n):
    assert n & (n - 1) == 0
    while n > 1:
        ar, ai = ar * ar - ai * ai, 2.0 * ar * ai
        n //= 2
    return ar, ai


def _scan(st, cr, ci, init, nk, reverse, store, prev=None):
    W = S5_W

    def step(j, carry):
        k = nk - 1 - j if reverse else j
        rows = pl.ds(pl.multiple_of(k * 8, 8), 8)
        sr, si = carry[0], carry[1]
        nsr = cr * sr - ci * si + st[rows, 0:W]
        nsi = cr * si + ci * sr + st[rows, W:2 * W]
        if store:
            st[rows, 0:W] = nsr
            st[rows, W:2 * W] = nsi
        if prev is None:
            return nsr, nsi
        prows = pl.ds(pl.multiple_of(jnp.maximum(k - 1, 0) * 8, 8), 8)
        w = jnp.where(k > 0, 1.0, 0.0).astype(F32)
        pr = prev[prows, 0:W] * w
        pi = prev[prows, W:2 * W] * w
        return nsr, nsi, carry[2] + nsr * pr + nsi * pi, carry[3] + nsi * pr - nsr * pi

    return lax.fori_loop(0, nk, step, init, unroll=2)


def _chain(fin, fr, fi, pr, pi, reverse):
    W = S5_W
    fin[:, 0:W] = fr
    fin[:, W:2 * W] = fi
    rowid = lax.broadcasted_iota(jnp.int32, (8, W), 0)
    cr = jnp.zeros((1, W), F32)
    ci = jnp.zeros((1, W), F32)
    init_r = jnp.zeros((8, W), F32)
    init_i = jnp.zeros((8, W), F32)
    for s in (range(7, -1, -1) if reverse else range(8)):
        init_r = jnp.where(rowid == s, cr, init_r)
        init_i = jnp.where(rowid == s, ci, init_i)
        lr = fin[s:s + 1, 0:W]
        li = fin[s:s + 1, W:2 * W]
        cr, ci = lr + pr * cr - pi * ci, li + pr * ci + pi * cr
    return init_r, init_i


def _full_scan(st, fin, ar, ai, nk, reverse, prev=None, carry_in=None, carry_out=None):
    W = S5_W
    cr = jnp.broadcast_to(ar, (8, W))
    ci = jnp.broadcast_to(-ai if reverse else ai, (8, W))
    z = jnp.zeros((8, W), F32)
    if carry_in is None:
        fr, fi = _scan(st, cr, ci, (z, z), nk, reverse, store=False)
        pr, pi = _cpow(ar, -ai if reverse else ai, nk)
        init = _chain(fin, fr, fi, pr, pi, reverse)
    else:
        init = (carry_in[:, 0:W], carry_in[:, W:2 * W])
    if carry_out is not None:
        carry_out[:, 0:W] = init[0]
        carry_out[:, W:2 * W] = init[1]
    if prev is None:
        return _scan(st, cr, ci, init, nk, reverse, store=True)
    return _scan(st, cr, ci, init + (z, z), nk, reverse, store=True, prev=prev)


def _s5_specs(L):
    W2 = 2 * S5_W
    GC = S5_GB * S5_C
    col = pl.BlockSpec((L, GC), lambda g: (0, g))
    vec = pl.BlockSpec((1, GC), lambda g: (0, g))
    avec = pl.BlockSpec((1, S5_W), lambda g: (0, g))
    bmat = pl.BlockSpec((None, GC, W2), lambda g: (g, 0, 0))
    cmat = pl.BlockSpec((None, W2, GC), lambda g: (g, 0, 0))
    return col, vec, avec, bmat, cmat


def _interleave(dst, src, nk):
    for s in range(8):
        dst[pl.ds(s, nk, stride=8), :] = src[s * nk:(s + 1) * nk, :]


def _deinterleave(dst, src, nk):
    for s in range(8):
        dst[s * nk:(s + 1) * nk, :] = src[pl.ds(s, nk, stride=8), :].astype(dst.dtype)


def _hosting_call(body, name, nsteps, host, ins, in_specs, outs, out_specs, scratch):
    grid = (nsteps,) if isinstance(nsteps, int) else tuple(nsteps)
    params = pltpu.CompilerParams(dimension_semantics=("arbitrary",) * len(grid), vmem_limit_bytes=VMEM_LIMIT)
    if host is None:
        res = pl.pallas_call(
            body, name=name, grid=grid, in_specs=in_specs, out_specs=out_specs, out_shape=outs,
            scratch_shapes=scratch, compiler_params=params,
        )(*ins)
        return list(res), []
    n_in, n_out, n_sc = len(ins), len(outs), len(scratch)
    h_in, h_out = len(host.ins), len(host.outs)

    def hosted(*refs):
        a = refs[:n_in]
        ha = refs[n_in:n_in + h_in]
        o = refs[n_in + h_in:n_in + h_in + n_out]
        ho = refs[n_in + h_in + n_out:n_in + h_in + n_out + h_out]
        sc = refs[n_in + h_in + n_out + h_out:n_in + h_in + n_out + h_out + n_sc]
        hs = refs[n_in + h_in + n_out + h_out + n_sc:]
        first = functools.reduce(jnp.logical_and, [pl.program_id(i) == 0 for i in range(len(grid))])
        last = functools.reduce(jnp.logical_and, [pl.program_id(i) == g - 1 for i, g in enumerate(grid)])

        @pl.when(first)
        def _():
            host.start(ha, ho, hs)

        body(*a, *o, *sc)

        @pl.when(last)
        def _():
            host.finish(ha, ho, hs)

    hbm = pl.BlockSpec(memory_space=pl.ANY)
    res = pl.pallas_call(
        hosted, name=name, grid=grid,
        in_specs=list(in_specs) + [hbm] * h_in, out_specs=list(out_specs) + [hbm] * h_out,
        out_shape=list(outs) + list(host.outs), scratch_shapes=list(scratch) + list(host.scratch),
        compiler_params=params,
    )(*ins, *host.ins)
    return list(res[:n_out]), list(res[n_out:])


def _s5_fwd(u, bm, cm, ar, ai, dvec, host=None):
    L = u.shape[0]
    nk = L // 8
    GC = S5_GB * S5_C
    nb = S5_G // S5_GB
    col, vec, avec, bmat, cmat = _s5_specs(L)

    def body(u_ref, b_ref, c_ref, ar_ref, ai_ref, d_ref, y_ref, carry_ref, st, fin, ui, yi):
        _interleave(ui, u_ref, nk)
        for r in range(8):
            rows = slice(r * nk, (r + 1) * nk)
            st[rows, :] = _dot(ui[rows, :].astype(BF16), b_ref[...])
        _full_scan(st, fin, ar_ref[...], ai_ref[...], nk, reverse=False, carry_out=carry_ref)
        for r in range(8):
            rows = slice(r * nk, (r + 1) * nk)
            yi[rows, :] = _dot_nt(st[rows, :].astype(BF16), c_ref[...]) + d_ref[...] * ui[rows, :]
        _deinterleave(y_ref, yi, nk)

    return _hosting_call(
        body, "s5_fwd", nb, host,
        [u, bm, cm, ar, ai, dvec], [col, bmat, bmat, avec, avec, vec],
        [jax.ShapeDtypeStruct(u.shape, F32), jax.ShapeDtypeStruct((nb * 8, 2 * S5_W), F32)],
        [col, pl.BlockSpec((8, 2 * S5_W), lambda g: (g, 0))],
        [pltpu.VMEM((L, 2 * S5_W), F32), pltpu.VMEM((8, 2 * S5_W), F32), pltpu.VMEM((L, GC), F32),
         pltpu.VMEM((L, GC), F32)])


def _s5_bwd(u, dy, carry, bm, cm, ar, ai, dvec, mask, rmat, host=None):
    L = u.shape[0]
    nk = L // 8
    W = S5_W
    GC = S5_GB * S5_C
    col, vec, avec, bmat, cmat = _s5_specs(L)
    hi = lax.Precision.HIGHEST

    def body(u_ref, dy_ref, carry_ref, b_ref, ct_ref, ar_ref, ai_ref, d_ref, mask_ref, r_ref,
             du_ref, db_ref, dc_ref, dd_ref, dar_ref, dai_ref, sa, sb, fin, ui, dyi, dui):
        ar = ar_ref[...]
        ai = ai_ref[...]
        _interleave(ui, u_ref, nk)
        _interleave(dyi, dy_ref, nk)
        for r in range(8):
            rows = slice(r * nk, (r + 1) * nk)
            sa[rows, :] = _dot(ui[rows, :].astype(BF16), b_ref[...])
            sb[rows, :] = _dot(dyi[rows, :].astype(BF16), ct_ref[...])
        _full_scan(sa, fin, ar, ai, nk, reverse=False, carry_in=carry_ref)
        gr, gi, accr, acci = _full_scan(sb, fin, ar, ai, nk, reverse=True, prev=sa)
        rowid = lax.broadcasted_iota(jnp.int32, (8, W), 0)
        last = pl.ds((nk - 1) * 8, 8)
        pr = jnp.where(rowid == 0, 0.0, pltpu.roll(sa[last, 0:W], 1, 0))
        pi = jnp.where(rowid == 0, 0.0, pltpu.roll(sa[last, W:2 * W], 1, 0))
        accr = accr + gr * pr + gi * pi
        acci = acci + gi * pr - gr * pi
        dar_ref[...] = jnp.sum(accr, axis=0, keepdims=True)
        dai_ref[...] = jnp.sum(acci, axis=0, keepdims=True)
        dbf = jnp.zeros((GC, 2 * W), F32)
        dcf = jnp.zeros((GC, 2 * W), F32)
        dd = jnp.zeros((1, GC), F32)
        for r in range(8):
            rows = slice(r * nk, (r + 1) * nk)
            ub = ui[rows, :]
            dyb = dyi[rows, :]
            gb = sb[rows, :].astype(BF16)
            dui[rows, :] = _dot_nt(gb, b_ref[...]) + d_ref[...] * dyb
            dbf = dbf + _dot_tn(ub.astype(BF16), gb)
            dcf = dcf + _dot_tn(dyb.astype(BF16), sa[rows, :].astype(BF16))
            dd = dd + jnp.sum(dyb * ub, axis=0, keepdims=True)
        db_ref[...] = jnp.dot(dbf * mask_ref[...], r_ref[...], precision=hi, preferred_element_type=F32)
        dc_ref[...] = jnp.dot(dcf * mask_ref[...], r_ref[...], precision=hi, preferred_element_type=F32)
        dd_ref[...] = dd
        _deinterleave(du_ref, dui, nk)

    cmp_spec = pl.BlockSpec((GC, 2 * S5_P), lambda g: (g, 0))
    whole = lambda shape: pl.BlockSpec(shape, lambda g: (0, 0))
    sd = jax.ShapeDtypeStruct
    return _hosting_call(
        body, "s5_bwd", S5_G // S5_GB, host,
        [u, dy, carry, bm, cm, ar, ai, dvec, mask, rmat],
        [col, col, pl.BlockSpec((8, 2 * W), lambda g: (g, 0)), bmat, bmat, avec, avec, vec, whole(mask.shape),
         whole(rmat.shape)],
        [sd(u.shape, BF16), sd((S5_G * S5_C, 2 * S5_P), F32), sd((S5_G * S5_C, 2 * S5_P), F32),
         sd((1, PRIM), F32), sd((1, S5_G * S5_P), F32), sd((1, S5_G * S5_P), F32)],
        [col, cmp_spec, cmp_spec, vec, avec, avec],
        [pltpu.VMEM((L, 2 * W), F32), pltpu.VMEM((L, 2 * W), F32), pltpu.VMEM((8, 2 * W), F32),
         pltpu.VMEM((L, GC), F32), pltpu.VMEM((L, GC), F32), pltpu.VMEM((L, GC), F32)])


def _s5_mats(bbr, bbi, cre, cim):
    nb = S5_G // S5_GB
    eye = jnp.eye(S5_GB, dtype=BF16)

    def blocks(re, im):
        x = jnp.stack([re, im], axis=2).astype(BF16).reshape(nb, S5_GB, S5_C, 2, S5_P)
        return jnp.einsum('ngcrp,gh->ngcrhp', x, eye).reshape(nb, S5_GB * S5_C, 2 * S5_W)

    return blocks(bbr, bbi), blocks(cre, -cim)


def _s5_compact_consts():
    g_row = np.arange(S5_GB * S5_C) // S5_C
    col = np.arange(2 * S5_W)
    g_col = (col % S5_W) // S5_P
    mask = (g_row[:, None] == g_col[None, :]).astype(np.float32)
    tgt = (col // S5_W) * S5_P + col % S5_P
    rmat = (tgt[:, None] == np.arange(2 * S5_P)[None, :]).astype(np.float32)
    return jnp.asarray(mask), jnp.asarray(rmat)


def _attn_scores(q_ref, k_ref, qb, bq, scale):
    ext = (qb + 1) * bq
    s = _dot_nt(q_ref[qb * bq:ext, :], k_ref[0:ext, :]) * scale
    qpos = lax.broadcasted_iota(jnp.int32, (bq, bq), 0)
    kpos = lax.broadcasted_iota(jnp.int32, (bq, bq), 1)
    diag = jnp.where(kpos <= qpos, s[:, ext - bq:], NEG)
    return diag if qb == 0 else jnp.concatenate([s[:, :ext - bq], diag], axis=-1)


def _attn_fwd(qp, kp, v, scale):
    L = qp.shape[0]
    bq = min(256, L)

    def body(q_ref, k_ref, v_ref, o_ref, lse_ref):
        for qb in range(L // bq):
            rows = slice(qb * bq, (qb + 1) * bq)
            s = _attn_scores(q_ref, k_ref, qb, bq, scale)
            m = jnp.max(s, axis=-1, keepdims=True)
            e = jnp.exp(s - m)
            l = jnp.sum(e, axis=-1, keepdims=True)
            o_ref[rows, :] = _dot(e.astype(BF16), v_ref[0:(qb + 1) * bq, :]) / l
            lse_ref[rows, :] = jnp.broadcast_to(m + jnp.log(l), (bq, HD))

    blk = pl.BlockSpec((L, HD), lambda h: (0, h))
    wide = pl.BlockSpec((L, 2 * HD), lambda h: (0, h))
    return pl.pallas_call(
        body, name="mla_attn_fwd", grid=(MLA_H,),
        in_specs=[wide, wide, blk], out_specs=[blk, blk],
        out_shape=[jax.ShapeDtypeStruct((L, MLA_H * HD), F32)] * 2,
        compiler_params=pltpu.CompilerParams(dimension_semantics=("arbitrary",), vmem_limit_bytes=VMEM_LIMIT),
    )(qp, kp, v)


def _attn_bwd(qp, kp, v, o, lse, do, scale):
    L = qp.shape[0]
    bq = min(256, L)
    nq = L // bq

    def body(q_ref, k_ref, v_ref, o_ref, lse_ref, do_ref, dq_ref, dk_ref, dv_ref, dk_acc, dv_acc):
        dk_acc[...] = jnp.zeros_like(dk_acc)
        dv_acc[...] = jnp.zeros_like(dv_acc)
        for qb in range(nq):
            rows = slice(qb * bq, (qb + 1) * bq)
            ext = (qb + 1) * bq
            do = do_ref[rows, :]
            dob = do.astype(BF16)
            p = jnp.exp(_attn_scores(q_ref, k_ref, qb, bq, scale) - lse_ref[rows, 0:1])
            dp = _dot_nt(dob, v_ref[0:ext, :])
            dsum = jnp.sum(do * o_ref[rows, :], axis=-1, keepdims=True)
            ds = (p * (dp - dsum) * scale).astype(BF16)
            dq_ref[rows, :] = _dot(ds, k_ref[0:ext, :]).astype(dq_ref.dtype)
            dk_acc[0:ext, :] += _dot_tn(ds, q_ref[rows, :])
            dv_acc[0:ext, :] += _dot_tn(p.astype(BF16), dob)
        dk_ref[...] = dk_acc[...].astype(dk_ref.dtype)
        dv_ref[...] = dv_acc[...].astype(dv_ref.dtype)

    sd = jax.ShapeDtypeStruct
    blk = pl.BlockSpec((L, HD), lambda h: (0, h))
    wide = pl.BlockSpec((L, 2 * HD), lambda h: (0, h))
    return pl.pallas_call(
        body, name="mla_attn_bwd", grid=(MLA_H,),
        in_specs=[wide, wide, blk, blk, blk, blk], out_specs=[wide, wide, blk],
        out_shape=[sd((L, MLA_H * 2 * HD), BF16), sd((L, MLA_H * 2 * HD), BF16), sd((L, MLA_H * HD), BF16)],
        scratch_shapes=[pltpu.VMEM((L, 2 * HD), F32), pltpu.VMEM((L, HD), F32)],
        compiler_params=pltpu.CompilerParams(dimension_semantics=("arbitrary",), vmem_limit_bytes=VMEM_LIMIT),
    )(qp, kp, v, o, lse, do)


def _kv_fn(mem, gm, w, gk):
    kv = _mm(_rms(mem, gm, D_MODEL), w)
    k = jnp.concatenate([_rms(kv[:, HD * h:HD * (h + 1)], gk, HD) for h in range(X_HEADS)], axis=-1)
    return k, kv[:, XQ:]


def _kv_prep(mem, gm, w, gk, name):
    def fn(mem, gm, w, gk):
        return _kv_fn(mem, gm, w, gk)
    M = mem.shape[0]
    return _rowwise(name, fn, [('c', mem), ('c', gm), ('c', w), ('c', gk)],
                    [('c', (M, XQ), F32), ('c', (M, XQ), F32)], 1)


def _kv_prep_bwd(mem, gm, w, gk, dk, dv, name):
    def fn(mem, gm, w, gk, dk, dv):
        _, vjp = jax.vjp(lambda a, b, c: _kv_fn(mem, a, b, c), gm, w, gk)
        return vjp((dk, dv))
    return _rowwise(name, fn, [('c', mem), ('c', gm), ('c', w), ('c', gk), ('c', dk), ('c', dv)],
                    [('c', gm.shape, F32), ('c', w.shape, BF16), ('c', gk.shape, F32)], 1)


def _forward_merge(x, mix, mix_kind, xq, gate, k, v, gq, wout, name, nblk, sub, host=None):
    def fn(x, mix, xq, gate, k, v, gq, wout):
        o = _merge(mix, xq, gate, k, v, gq)
        return (x + _dot(o.astype(BF16), wout),)
    L = x.shape[0]
    out = _rowwise(name, fn, [('r', x), (mix_kind, mix), ('r', xq), ('r', gate), ('c', k), ('c', v), ('c', gq),
                              ('c', wout)], [('r', (L, D_MODEL), F32)], nblk, sub, host=host)
    return out[0] if host is None else (out[0][0], out[1])


def _backward_merge(dx, mix, mix_kind, xq, gate, k, v, gq, wout, name, nblk, sub, host=None):
    def fn(dx, mix, xq, gate, k, v, gq, wout):
        g16 = dx.astype(BF16)
        do = _dot_nt(g16, wout)
        o, vjp = jax.vjp(_merge, mix, xq, gate, k, v, gq)
        dmix, dxq, dgate, dk, dv, dgq = vjp(do)
        return dmix, dxq, dgate, o, g16, dk, dv, dgq
    L = dx.shape[0]
    return _rowwise(
        name, fn,
        [('r', dx), (mix_kind, mix), ('r', xq), ('r', gate), ('c', k), ('c', v), ('c', gq), ('c', wout)],
        [('r', (L, PRIM), F32), ('r', (L, XQ), BF16), ('r', (L, BRANCH), BF16), ('t', (BRANCH, L), BF16),
         ('r', (L, D_MODEL), BF16), ('a', k.shape, F32), ('a', v.shape, F32), ('a', gq.shape, F32)], nblk, sub,
        host=host)


_MLA_IN = 3392
_MLA_IN_PAD = 3456


def _uq_rows(wt):
    r = wt.reshape(MLA_H, HD + ROPE, wt.shape[1])
    return jnp.concatenate([r[:, :HD].reshape(PRIM, -1),
                            jnp.pad(r[:, HD:], ((0, 0), (0, HD - ROPE), (0, 0))).reshape(PRIM, -1)], axis=0)


def _uq_rows_back(wt):
    nope = wt[:PRIM].reshape(MLA_H, HD, -1)
    rope = wt[PRIM:].reshape(MLA_H, HD, -1)[:, :ROPE]
    return jnp.concatenate([nope, rope], axis=1).reshape(MLA_H * (HD + ROPE), -1)


def _mla_in_rows(wt):
    return jnp.concatenate([wt[:768], wt[832:], wt[768:832], jnp.zeros((64, wt.shape[1]), wt.dtype)], axis=0)


def _mla_in_rows_back(wt):
    return jnp.concatenate([wt[:768], wt[3328:3392], wt[768:3328]], axis=0)


_SMALL = (("ln_gain", 2048), ("mem_norm", 2048), ("xq_norm", 256), ("xk_norm", 256), ("s5_lambda_re", 6144),
          ("s5_lambda_im", 6144), ("s5_log_step", 96), ("s5_b_re", 98304), ("s5_b_im", 98304), ("s5_c_re", 98304),
          ("s5_c_im", 98304), ("s5_d", 1536), ("mla_q_lora_norm", 512), ("mla_kv_lora_norm", 256),
          ("mla_q_nope_norm", 128), ("mla_k_nope_norm", 128), ("mla_q_rope_norm", 64), ("mla_k_rope_norm", 64))
_SMALL_ROWS = 432
_SMALL_OFF = {name: sum(n for _, n in _SMALL[:i]) for i, (name, _) in enumerate(_SMALL)}


def _pack_small(d):
    flat = jnp.concatenate([d[n].reshape(-1).astype(F32) for n, _ in _SMALL])
    return jnp.pad(flat, (0, _SMALL_ROWS * 1024 - flat.shape[0])).reshape(_SMALL_ROWS, 1024)


def _unpack_small(p, name, shape):
    off = _SMALL_OFF[name]
    return p.reshape(-1)[off:off + int(np.prod(shape))].reshape(shape)


_WEIGHTS = ('ln_gain', 'w_out', 'mem_norm', 'w_mem_kv', 'xq_norm', 'xk_norm', 's5_w_in', 's5_lambda_re',
            's5_lambda_im', 's5_log_step', 's5_b_re', 's5_b_im', 's5_c_re', 's5_c_im', 's5_d', 's5_w_glu', 'mla_w_in',
            'mla_q_lora_norm', 'mla_kv_lora_norm', 'mla_w_uq', 'mla_w_ukv', 'mla_q_nope_norm', 'mla_k_nope_norm',
            'mla_q_rope_norm', 'mla_k_rope_norm')
_BIG = ('w_out', 'w_mem_kv', 's5_w_in', 's5_w_glu', 'mla_w_in', 'mla_w_uq', 'mla_w_ukv')


def _pad128(g):
    return jnp.pad(g.reshape(1, -1), ((0, 0), (0, HD - g.shape[-1])))


def kernel(x, mem, positions, ln_gain, w_out, mem_norm, w_mem_kv, xq_norm, xk_norm, s5_w_in, s5_lambda_re, s5_lambda_im, s5_log_step, s5_b_re, s5_b_im, s5_c_re, s5_c_im, s5_d, s5_w_glu, mla_w_in, mla_q_lora_norm, mla_kv_lora_norm, mla_w_uq, mla_w_ukv, mla_q_nope_norm, mla_k_nope_norm, mla_q_rope_norm, mla_k_rope_norm, loss_target, m_ln_gain, m_w_out, m_mem_norm, m_w_mem_kv, m_xq_norm, m_xk_norm, m_s5_w_in, m_s5_lambda_re, m_s5_lambda_im, m_s5_log_step, m_s5_b_re, m_s5_b_im, m_s5_c_re, m_s5_c_im, m_s5_d, m_s5_w_glu, m_mla_w_in, m_mla_q_lora_norm, m_mla_kv_lora_norm, m_mla_w_uq, m_mla_w_ukv, m_mla_q_nope_norm, m_mla_k_nope_norm, m_mla_q_rope_norm, m_mla_k_rope_norm, v_ln_gain, v_w_out, v_mem_norm, v_w_mem_kv, v_xq_norm, v_xk_norm, v_s5_w_in, v_s5_lambda_re, v_s5_lambda_im, v_s5_log_step, v_s5_b_re, v_s5_b_im, v_s5_c_re, v_s5_c_im, v_s5_d, v_s5_w_glu, v_mla_w_in, v_mla_q_lora_norm, v_mla_kv_lora_norm, v_mla_w_uq, v_mla_w_ukv, v_mla_q_nope_norm, v_mla_k_nope_norm, v_mla_q_rope_norm, v_mla_k_rope_norm):
    weights = dict(ln_gain=ln_gain, w_out=w_out, mem_norm=mem_norm, w_mem_kv=w_mem_kv, xq_norm=xq_norm,
                   xk_norm=xk_norm, s5_w_in=s5_w_in, s5_lambda_re=s5_lambda_re, s5_lambda_im=s5_lambda_im,
                   s5_log_step=s5_log_step, s5_b_re=s5_b_re, s5_b_im=s5_b_im, s5_c_re=s5_c_re, s5_c_im=s5_c_im,
                   s5_d=s5_d, s5_w_glu=s5_w_glu, mla_w_in=mla_w_in, mla_q_lora_norm=mla_q_lora_norm,
                   mla_kv_lora_norm=mla_kv_lora_norm, mla_w_uq=mla_w_uq, mla_w_ukv=mla_w_ukv,
                   mla_q_nope_norm=mla_q_nope_norm, mla_k_nope_norm=mla_k_nope_norm,
                   mla_q_rope_norm=mla_q_rope_norm, mla_k_rope_norm=mla_k_rope_norm)
    m_in = dict(zip(_WEIGHTS, (m_ln_gain, m_w_out, m_mem_norm, m_w_mem_kv, m_xq_norm, m_xk_norm, m_s5_w_in,
                               m_s5_lambda_re, m_s5_lambda_im, m_s5_log_step, m_s5_b_re, m_s5_b_im, m_s5_c_re,
                               m_s5_c_im, m_s5_d, m_s5_w_glu, m_mla_w_in, m_mla_q_lora_norm, m_mla_kv_lora_norm,
                               m_mla_w_uq, m_mla_w_ukv, m_mla_q_nope_norm, m_mla_k_nope_norm, m_mla_q_rope_norm,
                               m_mla_k_rope_norm)))
    v_in = dict(zip(_WEIGHTS, (v_ln_gain, v_w_out, v_mem_norm, v_w_mem_kv, v_xq_norm, v_xk_norm, v_s5_w_in,
                               v_s5_lambda_re, v_s5_lambda_im, v_s5_log_step, v_s5_b_re, v_s5_b_im, v_s5_c_re,
                               v_s5_c_im, v_s5_d, v_s5_w_glu, v_mla_w_in, v_mla_q_lora_norm, v_mla_kv_lora_norm,
                               v_mla_w_uq, v_mla_w_ukv, v_mla_q_nope_norm, v_mla_k_nope_norm, v_mla_q_rope_norm,
                               v_mla_k_rope_norm)))

    x0 = x[0]
    mem0 = mem[0]
    target = loss_target[0]
    L = x0.shape[0]
    nblk, sub = 8, 1
    me = 4 * lax.axis_index("x") + 2 * lax.axis_index("y") + lax.axis_index("c")

    lora = jnp.pad(jnp.concatenate([mla_q_lora_norm, mla_kv_lora_norm], axis=1), ((0, 7), (0, HD - 96)))
    def gather(*shards):
        return _plan_all_gather(list(shards))

    kh = D_MODEL // 2
    (b_mkv0, b_glu, b_in_mla, b_out0, b_uq, b_ukv, b_mkv1, b_out1), (W_in_s5,) = _cast_call(
        [w_mem_kv[0], s5_w_glu[0], jnp.transpose(mla_w_in[0]), w_out[0], jnp.transpose(mla_w_uq[0]), mla_w_ukv[0],
         w_mem_kv[1], w_out[1]], "cast_shards", host=gather(s5_w_in[0].astype(BF16)))

    ln0, ln1 = ln_gain[0:1], ln_gain[1:2]
    gq0, gq1 = xq_norm[0:1], xq_norm[1:2]
    gk0, gk1 = xk_norm[0:1], xk_norm[1:2]
    gm0, gm1 = mem_norm[0:1], mem_norm[1:2]
    gqn, gkn = mla_q_nope_norm, mla_k_nope_norm
    gqr, gkr = _pad128(mla_q_rope_norm), _pad128(mla_k_rope_norm)

    lr3 = s5_lambda_re.reshape(S5_G, 1, S5_P)
    li3 = s5_lambda_im.reshape(S5_G, 1, S5_P)
    ls3 = s5_log_step.reshape(S5_G, 1, 1)
    btr = jnp.swapaxes(s5_b_re[0], 1, 2)
    bti = jnp.swapaxes(s5_b_im[0], 1, 2)
    a_r, a_i, bbr, bbi = _s5_params(lr3, li3, ls3, btr, bti)
    bm, cm = _s5_mats(bbr, bbi, s5_c_re[0], s5_c_im[0])
    a_r2 = a_r.reshape(1, S5_G * S5_P)
    a_i2 = a_i.reshape(1, S5_G * S5_P)
    cmask, rmat = _s5_compact_consts()

    half = ROPE // 2
    inv_freq = ROPE_THETA ** (-jnp.arange(half, dtype=F32) / half)
    invf = jnp.concatenate([inv_freq, inv_freq, jnp.zeros((HD - ROPE,), F32)]).reshape(1, HD)

    def rot_tables(pos, invf):
        ang = pos.astype(F32) * invf
        lane = lax.broadcasted_iota(jnp.int32, ang.shape, 1)
        c = jnp.where(lane < ROPE, jnp.cos(ang), 0.0)
        s = jnp.sin(ang)
        return c, jnp.where(lane < half, -s, 0.0), jnp.where((lane >= half) & (lane < ROPE), s, 0.0)

    tc, ts1, ts2 = _rowwise("rot_tables", rot_tables, [('r', positions.reshape(L, 1)), ('c', invf)],
                            [('r', (L, HD), F32)] * 3, nblk, sub)

    def in_s5(x, g, w):
        proj = _mm_slots(_rms(x, g, D_MODEL).astype(BF16), w)
        return proj[:, :PRIM], proj[:, PRIM:PRIM + XQ], proj[:, PRIM + XQ:]

    (u_s5, xq_a, gate_a), (G_mkv0,) = _rowwise(
        "s5_in", in_s5, [('r', x0), ('c', ln0), ('c', W_in_s5)],
        [('r', (L, PRIM), F32), ('r', (L, XQ), F32), ('r', (L, BRANCH), F32)], nblk, sub, host=gather(b_mkv0))
    (y_s5, s5_carry), (W_glu, G_in_mla_a) = _s5_fwd(u_s5, bm, cm, a_r2, a_i2, s5_d,
                                                    host=gather(b_glu, b_in_mla[:, :kh]))

    def glu(y, w):
        z = _mm_slots(_gelu(y).astype(BF16), w)
        return (z[:, :PRIM] * _sigmoid(z[:, PRIM:]),)

    (y2,), (G_out0,) = _rowwise("s5_glu", glu, [('r', y_s5), ('c', W_glu)], [('r', (L, PRIM), F32)], nblk, sub,
                                host=gather(b_out0))
    W_mkv0 = G_mkv0.reshape(D_MODEL, 2 * XQ)
    k_a, v_a = _kv_prep(mem0, gm0, W_mkv0, gk0, "kv_prep0")
    x1, (G_in_mla_b,) = _forward_merge(
        x0, y2, 'r', xq_a, gate_a, k_a, v_a, gq0, G_out0.reshape(BRANCH, D_MODEL), "merge0", nblk, sub,
        host=gather(b_in_mla[:, kh:]))
    W_in_mla = _mla_in_rows(jnp.concatenate([G_in_mla_a, G_in_mla_b], axis=2).reshape(_MLA_IN, D_MODEL))

    def in_mla(x, g, w):
        proj = _dot_nt(_rms(x, g, D_MODEL).astype(BF16), w)
        return proj[:, :512], proj[:, 512:768], proj[:, 768:1280], proj[:, 1280:3328], proj[:, 3328:]

    (c_q, c_kv, xq_b, gate_b, krp), (G_uq, W_kv, G_lora) = _rowwise(
        "mla_in", in_mla, [('r', x1), ('c', ln1), ('c', W_in_mla)],
        [('r', (L, Q_LORA), F32), ('r', (L, KV_LORA), F32), ('r', (L, XQ), F32), ('r', (L, BRANCH), F32),
         ('r', (L, HD), F32)], nblk, sub,
        host=gather(b_uq, b_ukv, lora))
    W_q = _uq_rows(G_uq.reshape(MLA_H * (HD + ROPE), Q_LORA))
    g_qlora = G_lora[:, 0, :64].reshape(1, Q_LORA)
    g_kvlora = G_lora[:, 0, 64:96].reshape(1, KV_LORA)

    def qkv(c_q, c_kv, krp, tc, ts1, ts2, gql, gkvl, wq, wkv, gqn, gkn, gqr, gkr):
        q = _dot_nt(_rms(c_q, gql, Q_LORA).astype(BF16), wq)
        kv = _mm_slots(_rms(c_kv, gkvl, KV_LORA).astype(BF16), wkv)
        kp, v = _kv_post(kv, krp, gkn, gkr, tc, ts1, ts2)
        return _q_post(q, gqn, gqr, tc, ts1, ts2), kp, v

    qkv_consts = [('c', g_qlora), ('c', g_kvlora), ('c', W_q), ('c', W_kv), ('c', gqn), ('c', gkn), ('c', gqr),
                  ('c', gkr)]
    (q_pad, k_pad, v_h), (G_mkv1, G_out1) = _rowwise(
        "mla_qkv", qkv, [('r', c_q), ('r', c_kv), ('r', krp), ('r', tc), ('r', ts1), ('r', ts2)] + qkv_consts,
        [('r', (L, 2 * PRIM), BF16), ('r', (L, 2 * PRIM), BF16), ('r', (L, PRIM), BF16)], nblk, sub,
        host=gather(b_mkv1, b_out1))
    W_out = (G_out0.reshape(BRANCH, D_MODEL), G_out1.reshape(BRANCH, D_MODEL))
    W_mkv = (W_mkv0, G_mkv1.reshape(D_MODEL, 2 * XQ))
    scale = (HD + ROPE) ** -0.5
    attn, lse = _attn_fwd(q_pad, k_pad, v_h, scale)
    k_b, v_b = _kv_prep(mem0, gm1, W_mkv[1], gk1, "kv_prep1")

    def merge_loss(x, mix, xq, gate, k, v, gq, wout, t):
        err = x + _dot(_merge(mix, xq, gate, k, v, gq).astype(BF16), wout) - t
        part = 0.5 * jnp.sum(jnp.sum(err * err, axis=-1, keepdims=True) * (1.0 / D_MODEL), axis=0, keepdims=True)
        return err * (1.0 / D_MODEL), jnp.broadcast_to(part, (1, HD))

    dx2, loss_part = _rowwise(
        "merge1_loss", merge_loss,
        [('r', x1), ('r', attn), ('r', xq_b), ('r', gate_b), ('c', k_b), ('c', v_b), ('c', gq1), ('c', W_out[1]),
         ('r', target)], [('r', (L, D_MODEL), F32), ('a', (1, HD), F32)], nblk, sub)

    dattn, dxq_b, dgate_b, o_b, g_b, dk_b, dv_b, dgq1 = _backward_merge(
        dx2, attn, 'r', xq_b, gate_b, k_b, v_b, gq1, W_out[1], "merge1_bwd", nblk, sub)
    dgm1, dW_mkv1, dgk1 = _kv_prep_bwd(mem0, gm1, W_mkv[1], gk1, dk_b, dv_b, "kv_prep1_bwd")
    dW_out1 = _matmul_tn(o_b, g_b, "dw_out1")
    dq_pad, dk_pad, dv_h = _attn_bwd(q_pad, k_pad, v_h, attn, lse, dattn, scale)

    def qkv_bwd(c_q, c_kv, krp, tc, ts1, ts2, dqp, dkp, dv, gql, gkvl, wq, wkv, gqn, gkn, gqr, gkr):
        cqn, vjp_qn = jax.vjp(lambda a, b: _rms(a, b, Q_LORA), c_q, gql)
        ckvn, vjp_kvn = jax.vjp(lambda a, b: _rms(a, b, KV_LORA), c_kv, gkvl)
        cqn16 = cqn.astype(BF16)
        ckvn16 = ckvn.astype(BF16)
        q = _dot_nt(cqn16, wq)
        kv = _mm_slots(ckvn16, wkv)
        _, vjp_q = jax.vjp(lambda a, b, c: _q_post(a, b, c, tc, ts1, ts2), q, gqn, gqr)
        dq, dgqn, dgqr = vjp_q(dqp.astype(F32))
        _, vjp_kv = jax.vjp(lambda a, b, c, d: _kv_post(a, b, c, d, tc, ts1, ts2), kv, krp, gkn, gkr)
        dkv, dkrp, dgkn, dgkr = vjp_kv((dkp.astype(F32), dv.astype(F32)))
        dq16 = dq.astype(BF16)
        dkv16 = dkv.astype(BF16)
        dc_q, dgql = vjp_qn(_dot(dq16, wq))
        dc_kv, dgkvl = vjp_kvn(_mm_slots_nt(dkv16, wkv))
        return dc_q, dc_kv, dkrp, cqn16, dq16, ckvn16, dkv16, dgql, dgkvl, dgqn, dgkn, dgqr, dgkr

    (dc_q, dc_kv, dkrp, cqn16, dq16, ckvn16, dkv16, dgql, dgkvl, dgqn, dgkn, dgqr, dgkr) = _rowwise(
        "mla_qkv_bwd", qkv_bwd,
        [('r', c_q), ('r', c_kv), ('r', krp), ('r', tc), ('r', ts1), ('r', ts2), ('r', dq_pad), ('r', dk_pad),
         ('r', dv_h)] + qkv_consts,
        [('r', (L, Q_LORA), BF16), ('r', (L, KV_LORA), BF16), ('r', (L, HD), BF16), ('r', (L, Q_LORA), BF16),
         ('t', (2 * PRIM, L), BF16), ('t', (KV_LORA, L), BF16), ('r', (L, 2 * PRIM), BF16),
         ('a', (1, Q_LORA), F32), ('a', (1, KV_LORA), F32), ('a', (1, HD), F32), ('a', (1, HD), F32),
         ('a', (1, HD), F32), ('a', (1, HD), F32)], nblk, sub)
    dW_q = _matmul_tn(dq16, cqn16, "dw_uq")
    dW_kv = _matmul_tn_slots(ckvn16, dkv16, "dw_ukv")

    def in_bwd(x, dres, g, w, *dparts):
        dproj = jnp.concatenate(dparts, axis=-1).astype(BF16)
        xn, vjp = jax.vjp(lambda a, b: _rms(a, b, D_MODEL), x, g)
        dx, dg = vjp(_mm_slots_nt(dproj, w) if w.ndim == 3 else _dot(dproj, w))
        return dx + dres, xn, dproj, dg

    dx1, xn1, dproj1, dln1 = _rowwise(
        "mla_in_bwd", in_bwd,
        [('r', x1), ('r', dx2), ('c', ln1), ('c', W_in_mla), ('r', dc_q), ('r', dc_kv), ('r', dxq_b), ('r', dgate_b),
         ('r', dkrp)],
        [('r', (L, D_MODEL), F32), ('r', (L, D_MODEL), BF16), ('t', (_MLA_IN_PAD, L), BF16), ('a', (1, D_MODEL), F32)],
        nblk, sub)
    dW_in_mla = _matmul_tn(dproj1, xn1, "dw_mla_in")

    grads1 = [dW_out1.reshape(N_DEV, 256, D_MODEL), dW_mkv1.reshape(N_DEV, 128, 2 * XQ),
              _mla_in_rows_back(dW_in_mla).reshape(N_DEV, 424, D_MODEL),
              _uq_rows_back(dW_q).reshape(N_DEV, 288, Q_LORA), dW_kv]
    (dy2, dxq_a, dgate_a, o_a, g_a, dk_a, dv_a, dgq0), pair1 = _backward_merge(
        dx1, y2, 'r', xq_a, gate_a, k_a, v_a, gq0, W_out[0], "merge0_bwd", nblk, sub, host=_plan_pair(grads1))
    dgm0, dW_mkv0, dgk0 = _kv_prep_bwd(mem0, gm0, W_mkv[0], gk0, dk_a, dv_a, "kv_prep0_bwd")
    dW_out0 = _matmul_tn(o_a, g_a, "dw_out0")
    t1 = list(_pair_add(grads1, pair1, "rs_add_layer1"))

    def glu_bwd(y, dy2, w):
        h, vjp_h = jax.vjp(_gelu, y)
        h16 = h.astype(BF16)
        z = _mm_slots(h16, w)
        _, vjp_z = jax.vjp(lambda z: z[:, :PRIM] * _sigmoid(z[:, PRIM:]), z)
        dz16 = vjp_z(dy2)[0].astype(BF16)
        return vjp_h(_mm_slots_nt(dz16, w))[0], h16, dz16

    grads0 = [dW_out0.reshape(N_DEV, 256, D_MODEL), dW_mkv0.reshape(N_DEV, 128, 2 * XQ)]
    (dy_s5, h16, dz16), glu_hosted = _rowwise(
        "s5_glu_bwd", glu_bwd, [('r', y_s5), ('r', dy2), ('c', W_glu)],
        [('r', (L, PRIM), F32), ('t', (PRIM, L), BF16), ('r', (L, 2 * PRIM), BF16)], nblk, sub,
        host=_combine(_plan_chips(t1[2:]), _plan_pair(grads0)))
    recv_proj1, pair0 = glu_hosted[:3], glu_hosted[3:]
    dW_glu = _matmul_tn_slots(h16, dz16, "dw_glu")
    t0 = list(_pair_add(grads0 + [dW_glu], pair0 + list(_exchange_call(_plan_pair([dW_glu]), "rs_pair_glu")),
                        "rs_add_layer0"))
    (du_s5, dbc, dcc, dd, dar, dai), recv_rest = _s5_bwd(u_s5, dy_s5, s5_carry, bm, cm, a_r2, a_i2, s5_d,
                                                        cmask, rmat, host=_plan_chips(t1[:2] + t0))
    early_recv = recv_rest[:2] + recv_proj1 + recv_rest[2:]
    dbc4 = dbc.reshape(S5_G, S5_C, 2, S5_P)
    dcc4 = dcc.reshape(S5_G, S5_C, 2, S5_P)
    dlr, dli, dls, dbtr, dbti = _s5_params_bwd(
        lr3, li3, ls3, btr, bti, dar.reshape(S5_G, 1, S5_P), dai.reshape(S5_G, 1, S5_P), dbc4[:, :, 0], dbc4[:, :, 1])

    small_part = {
        "ln_gain": jnp.concatenate([jnp.zeros_like(dln1), dln1]), "mem_norm": jnp.concatenate([dgm0, dgm1]),
        "xq_norm": jnp.concatenate([dgq0, dgq1]), "xk_norm": jnp.concatenate([dgk0, dgk1]),
        "s5_lambda_re": dlr, "s5_lambda_im": dli, "s5_log_step": dls,
        "s5_b_re": jnp.swapaxes(dbtr, 1, 2), "s5_b_im": jnp.swapaxes(dbti, 1, 2),
        "s5_c_re": dcc4[:, :, 0], "s5_c_im": -dcc4[:, :, 1], "s5_d": dd,
        "mla_q_lora_norm": dgql, "mla_kv_lora_norm": dgkvl, "mla_q_nope_norm": dgqn, "mla_k_nope_norm": dgkn,
        "mla_q_rope_norm": dgqr[:, :ROPE], "mla_k_rope_norm": dgkr[:, :ROPE],
    }
    loss8 = jnp.pad(loss_part, ((0, 7), (0, 0)))
    (dx0, xn0, dproj0, dln0), (small_gath, loss_g) = _rowwise(
        "s5_in_bwd", in_bwd,
        [('r', x0), ('r', dx1), ('c', ln0), ('c', W_in_s5), ('r', du_s5), ('r', dxq_a),
         ('r', dgate_a)],
        [('r', (L, D_MODEL), F32), ('t', (D_MODEL, L), BF16), ('r', (L, 2 * BRANCH), BF16), ('a', (1, D_MODEL), F32)],
        nblk, sub, host=_plan_all_gather([_pack_small(small_part).astype(BF16), loss8]))
    dW_in_s5 = _matmul_tn_slots(xn0, dproj0, "dw_s5_in")

    late = [dW_in_s5]
    late_t = _pair_add(late, list(_exchange_call(_plan_pair(late), "rs_pair_late")), "rs_add_late")
    owners = [("w_out", 1), ("w_mem_kv", 1), ("mla_w_in", 0), ("mla_w_uq", 0), ("mla_w_ukv", 0), ("w_out", 0),
              ("w_mem_kv", 0), ("s5_w_glu", 0)]
    flipped = ("mla_w_in", "mla_w_uq")

    def shard(d, n, i):
        return jnp.transpose(d[n][i]) if n in flipped else d[n][i]

    upd, (late_recv, ln0_gath) = _updates_call(
        early_recv, [shard(weights, n, i) for n, i in owners], [shard(m_in, n, i) for n, i in owners],
        [shard(v_in, n, i) for n, i in owners], "update_early",
        host=_combine(_plan_chips(late_t), _plan_all_gather([jnp.pad(dln0, ((0, 7), (0, 0)))])))
    owners.append(("s5_w_in", 0))
    upd.append(_sum_adamw(late_recv, s5_w_in[0], m_s5_w_in[0], v_s5_w_in[0], "update_s5_w_in"))
    grads, delta, new_m, new_v = {}, {}, {}, {}
    for n in _BIG:
        parts = [u for u, (o, _) in sorted(zip(upd, owners), key=lambda t: t[1][1]) if o == n]
        if n in flipped:
            grads[n], delta[n], new_m[n], new_v[n] = (jnp.transpose(parts[0][j])[None] for j in range(4))
        else:
            grads[n], delta[n], new_m[n], new_v[n] = (jnp.stack([p[j] for p in parts]) for j in range(4))

    gs, loss_sum = _small_sum(small_gath, loss_g, ln0_gath, "small_sum")
    loss = loss_sum[0, 0]
    for n, _ in _SMALL:
        shape = weights[n].shape
        if n == "mla_q_lora_norm":
            grads[n] = lax.dynamic_slice(_unpack_small(gs, n, (Q_LORA,)), (me * 64,), (64,)).reshape(shape)
        elif n == "mla_kv_lora_norm":
            grads[n] = lax.dynamic_slice(_unpack_small(gs, n, (KV_LORA,)), (me * 32,), (32,)).reshape(shape)
        else:
            grads[n] = _unpack_small(gs, n, shape)

    def own(n, a):
        if a.ndim == 4:
            a = jnp.transpose(a, (0, 2, 3, 1))
        elif a.ndim == 3:
            a = jnp.transpose(a, (0, 2, 1))
        return a.reshape(a.shape[1:]) if a.ndim >= 3 else a

    def back(n, a):
        shape = weights[n].shape
        if len(shape) == 4:
            return jnp.transpose(a.reshape((1,) + a.shape), (0, 3, 1, 2))
        if len(shape) == 3:
            return jnp.transpose(a.reshape((1,) + a.shape), (0, 2, 1))
        return a.reshape(shape)

    wide = ("s5_b_re", "s5_b_im", "s5_c_re", "s5_c_im")
    for names, nb, call in (([n for n, _ in _SMALL if n not in wide], 1, "update_small"), (wide, 4, "update_s5_bc")):
        res = _adamw_multi([own(n, weights[n]) for n in names], [own(n, grads[n]) for n in names],
                           [own(n, m_in[n]) for n in names], [own(n, v_in[n]) for n in names], call, nb)
        for n, (dl, m2, v2) in zip(names, res):
            delta[n], new_m[n], new_v[n] = back(n, dl), back(n, m2), back(n, v2)
    return (loss, dx0[None], *[grads[n] for n in _WEIGHTS], *[delta[n] for n in _WEIGHTS],
            *[new_m[n] for n in _WEIGHTS], *[new_v[n] for n in _WEIGHTS])
```

```python
import functools
import math

import numpy as np
import jax
import jax.numpy as jnp
from jax import lax
from jax.experimental import pallas as pl
from jax.experimental.pallas import tpu as pltpu

F32 = jnp.float32
BF16 = jnp.bfloat16
EPS = 1e-6
NEG = float(np.finfo(np.float32).min)
MESH = pl.DeviceIdType.MESH

N_DEV = 8
D_MODEL = 1024
MEM_LEN = 256
XQ = 512
PRIM = 1536
BRANCH = 2048
X_HEADS = 4
HD = 128
S5_G = 96
S5_P = 64
S5_C = 16
S5_GB = 8
S5_W = S5_GB * S5_P
MLA_H = 12
ROPE = 64
Q_LORA = 512
KV_LORA = 256
ROPE_THETA = 10000.0

ADAM_LR = 0.001
ADAM_B1 = 0.9
ADAM_B2 = 0.999
ADAM_EPS = 1e-08
ADAM_WD = 0.01
ADAM_STEP = 10

VMEM_LIMIT = 56 * 1024 * 1024


def _dot(a, b):
    return jnp.dot(a, b, preferred_element_type=F32)


def _dot_nt(a, b):
    return lax.dot_general(a, b, (((1,), (1,)), ((), ())), preferred_element_type=F32)


def _dot_tn(a, b):
    return lax.dot_general(a, b, (((0,), (0,)), ((), ())), preferred_element_type=F32)


@jax.custom_vjp
def _mm(a, b):
    return _dot(a.astype(BF16), b.astype(BF16))


def _mm_fwd(a, b):
    return _mm(a, b), (a, b)


def _mm_bwd(res, g):
    a, b = res
    gb = g.astype(BF16)
    return _dot_nt(gb, b.astype(BF16)).astype(a.dtype), _dot_tn(a.astype(BF16), gb).astype(b.dtype)


_mm.defvjp(_mm_fwd, _mm_bwd)


@jax.custom_vjp
def _mm_nt(a, b):
    return _dot_nt(a.astype(BF16), b.astype(BF16))


def _mm_nt_fwd(a, b):
    return _mm_nt(a, b), (a, b)


def _mm_nt_bwd(res, g):
    a, b = res
    gb = g.astype(BF16)
    return _dot(gb, b.astype(BF16)).astype(a.dtype), _dot_tn(gb, a.astype(BF16)).astype(b.dtype)


_mm_nt.defvjp(_mm_nt_fwd, _mm_nt_bwd)


@jax.custom_vjp
def _softmax(s):
    m = jnp.max(s, axis=-1, keepdims=True)
    e = jnp.exp(s - m)
    return e / jnp.sum(e, axis=-1, keepdims=True)


def _softmax_fwd(s):
    p = _softmax(s)
    return p, p


def _softmax_bwd(p, g):
    return (p * (g - jnp.sum(p * g, axis=-1, keepdims=True)),)


_softmax.defvjp(_softmax_fwd, _softmax_bwd)


def _rms(x, g, n):
    ms = jnp.sum(x * x, axis=-1, keepdims=True) * (1.0 / n)
    return x * lax.rsqrt(ms + EPS) * g


def _sigmoid(x):
    return 1.0 / (1.0 + jnp.exp(-x))


def _silu(x):
    return x * _sigmoid(x)


def _gelu(x):
    c = math.sqrt(2.0 / math.pi)
    return 0.5 * x * (1.0 + jnp.tanh(c * (x + 0.044715 * (x * x * x))))


@jax.custom_vjp
def _rot(x, c, s1, s2):
    return x * c + pltpu.roll(x, 96, 1) * s1 + pltpu.roll(x, 32, 1) * s2


def _rot_fwd(x, c, s1, s2):
    return _rot(x, c, s1, s2), (c, s1, s2)


def _rot_bwd(res, g):
    c, s1, s2 = res
    dx = g * c + pltpu.roll(g * s1, 32, 1) + pltpu.roll(g * s2, 96, 1)
    return dx, jnp.zeros_like(c), jnp.zeros_like(s1), jnp.zeros_like(s2)


_rot.defvjp(_rot_fwd, _rot_bwd)


def _mem_attn(xq, k, v, gq):
    outs = []
    for h in range(X_HEADS):
        sl = slice(HD * h, HD * (h + 1))
        q = _rms(xq[:, sl], gq, HD)
        p = _softmax(_mm_nt(q, k[:, sl]) * (HD ** -0.5))
        outs.append(_mm(p, v[:, sl]))
    return jnp.concatenate(outs, axis=-1)


def _merge(mix, xq, gate, k, v, gq):
    return jnp.concatenate([mix, _mem_attn(xq, k, v, gq)], axis=-1) * _silu(gate)


def _q_post(q, gqn, gqr, c, s1, s2):
    pieces = []
    for h in range(MLA_H):
        pieces.append(_rms(q[:, HD * h:HD * (h + 1)], gqn, HD))
        pieces.append(_rot(_rms(q[:, PRIM + HD * h:PRIM + HD * (h + 1)], gqr, ROPE), c, s1, s2))
    return jnp.concatenate(pieces, axis=-1)


def _kv_post(kv, krp, gkn, gkr, c, s1, s2):
    kr = _rot(_rms(krp, gkr, ROPE), c, s1, s2)
    pieces, vals = [], []
    for h in range(MLA_H):
        pieces.append(_rms(kv[:, 2 * HD * h:2 * HD * h + HD], gkn, HD))
        pieces.append(kr)
        vals.append(kv[:, 2 * HD * h + HD:2 * HD * (h + 1)])
    return jnp.concatenate(pieces, axis=-1), jnp.concatenate(vals, axis=-1)


def _rowwise(name, fn, ins, outs, nblk, sub=1, host=None):
    n_in = len(ins)

    def spec(kind, shape):
        if kind == 'r':
            return pl.BlockSpec((shape[0] // nblk, shape[1]), lambda i: (i, 0))
        if kind == 't':
            return pl.BlockSpec((shape[0], shape[1] // nblk), lambda i: (0, i))
        zeros = (0,) * len(shape)
        return pl.BlockSpec(tuple(shape), lambda i: zeros)

    def body(*refs):
        i = pl.program_id(0)
        res = fn(*[r[...] for r in refs[:n_in]])
        for (kind, _, _), ref, val in zip(outs, refs[n_in:], res):
            if kind == 'a':
                @pl.when(i == 0)
                def _():
                    ref[...] = jnp.zeros_like(ref)
                ref[...] += val.astype(ref.dtype)
            elif kind == 't':
                ref[...] = val.astype(F32).T.astype(ref.dtype)
            else:
                ref[...] = val.astype(ref.dtype)

    res, hosted = _hosting_call(
        body, name, nblk, host, [a for _, a in ins], [spec(k, a.shape) for k, a in ins],
        [jax.ShapeDtypeStruct(tuple(s), d) for _, s, d in outs], [spec(k, s) for k, s, _ in outs], [])
    return res if host is None else (res, hosted)


def _matmul_tn(at, g, name, out_dtype=BF16):
    K, L = at.shape
    N = g.shape[1]
    tn = next(t for t in (512, 384, 256, 128) if N % t == 0)

    def body(a_ref, g_ref, o_ref):
        o_ref[...] = _dot(a_ref[...], g_ref[...]).astype(o_ref.dtype)

    return pl.pallas_call(
        body, name=name, grid=(N // tn,),
        in_specs=[pl.BlockSpec((K, L), lambda n: (0, 0)), pl.BlockSpec((L, tn), lambda n: (0, n))],
        out_specs=pl.BlockSpec((K, tn), lambda n: (0, n)),
        out_shape=jax.ShapeDtypeStruct((K, N), out_dtype),
        compiler_params=pltpu.CompilerParams(dimension_semantics=("arbitrary",), vmem_limit_bytes=VMEM_LIMIT),
    )(at, g)


def _matmul_tn_slots(at, g, name, host=None):
    K, L = at.shape
    n = g.shape[1] // N_DEV

    def body(a_ref, g_ref, o_ref):
        o_ref[...] = _dot(a_ref[...], g_ref[...]).astype(o_ref.dtype)

    res, hosted = _hosting_call(
        body, name, N_DEV, host, [at, g],
        [pl.BlockSpec((K, L), lambda d: (0, 0)), pl.BlockSpec((L, n), lambda d: (0, d))],
        [jax.ShapeDtypeStruct((N_DEV, K, n), BF16)], [pl.BlockSpec((None, K, n), lambda d: (d, 0, 0))], [])
    return res[0] if host is None else (res[0], hosted)


def _mm_slots(a16, w):
    return jnp.concatenate([_dot(a16, w[d]) for d in range(N_DEV)], axis=-1)


def _mm_slots_nt(g16, w):
    n = w.shape[2]
    out = _dot_nt(g16[:, 0:n], w[0])
    for d in range(1, N_DEV):
        out = out + _dot_nt(g16[:, d * n:(d + 1) * n], w[d])
    return out


class _Exchange:
    def __init__(self, ins, outs, scratch, start, finish):
        self.ins, self.outs, self.scratch, self.start, self.finish = ins, outs, scratch, start, finish


def _xyc():
    return lax.axis_index("x"), lax.axis_index("y"), lax.axis_index("c")


def _plan_all_gather(xs):
    n = len(xs)

    def build(x_refs, out_refs, sems):
        send_sems, recv_sems, local_sems = sems
        x, y, c = _xyc()

        def copies(k, block, to, own=False):
            slot = 4 * block[0] + 2 * block[1] + block[2]
            return [pltpu.make_async_remote_copy(
                src_ref=x_refs[a] if own else out_refs[a].at[slot], dst_ref=out_refs[a].at[slot],
                send_sem=send_sems.at[k * n + a], recv_sem=recv_sems.at[k * n + a], device_id=to,
                device_id_type=MESH) for a in range(n)]

        mine = [pltpu.make_async_copy(x_refs[a], out_refs[a].at[4 * x + 2 * y + c], local_sems.at[a])
                for a in range(n)]
        return copies, mine, (x, y, c), [(1 - x, y), (x, 1 - y), (1 - x, 1 - y)]

    def first_copies(copies, me, chips):
        x, y, c = me
        first = copies(0, me, (x, y, 1 - c), own=True)
        for j, chip in enumerate(chips):
            first += copies(1 + j, me, (*chip, c), own=True)
        return first

    def start(x_refs, out_refs, sems):
        copies, mine, me, chips = build(x_refs, out_refs, sems)
        for cp in mine + first_copies(copies, me, chips):
            cp.start()

    def finish(x_refs, out_refs, sems):
        copies, mine, me, chips = build(x_refs, out_refs, sems)
        x, y, c = me
        passed = []
        for j, chip in enumerate(chips):
            for cp in copies(1 + j, (*chip, c), me):
                cp.wait_recv()
            fwd = copies(4 + j, (*chip, c), (x, y, 1 - c))
            for cp in fwd:
                cp.start()
            passed += fwd
        for cp in copies(0, (x, y, 1 - c), me):
            cp.wait_recv()
        for j, chip in enumerate(chips):
            for cp in copies(4 + j, (*chip, 1 - c), me):
                cp.wait_recv()
        for cp in first_copies(copies, me, chips) + passed:
            cp.wait_send()
        for cp in mine:
            cp.wait()

    return _Exchange(list(xs), [jax.ShapeDtypeStruct((N_DEV,) + a.shape, a.dtype) for a in xs],
                     [pltpu.SemaphoreType.DMA((7 * n,)), pltpu.SemaphoreType.DMA((7 * n,)),
                      pltpu.SemaphoreType.DMA((n,))], start, finish)


_CHIPS = ((0, 0), (0, 1), (1, 0), (1, 1))


def _plan_pair(sends):
    n = len(sends)

    def build(s_refs, o_refs, sems):
        send_sems, recv_sems = sems
        x, y, c = _xyc()
        return [pltpu.make_async_remote_copy(
            src_ref=s_refs[a].at[4 * px + 2 * py + 1 - c], dst_ref=o_refs[a].at[j],
            send_sem=send_sems.at[j * n + a], recv_sem=recv_sems.at[j * n + a], device_id=(x, y, 1 - c),
            device_id_type=MESH) for j, (px, py) in enumerate(_CHIPS) for a in range(n)]

    def start(s_refs, o_refs, sems):
        for cp in build(s_refs, o_refs, sems):
            cp.start()

    def finish(s_refs, o_refs, sems):
        for cp in build(s_refs, o_refs, sems):
            cp.wait_recv()
            cp.wait_send()

    return _Exchange(list(sends), [jax.ShapeDtypeStruct((4,) + a.shape[1:], a.dtype) for a in sends],
                     [pltpu.SemaphoreType.DMA((4 * n,)), pltpu.SemaphoreType.DMA((4 * n,))], start, finish)


def _plan_chips(ts):
    n = len(ts)
    flips = ((1, 0), (0, 1), (1, 1))

    def build(t_refs, o_refs, sems):
        send_sems, recv_sems, local_sems = sems
        x, y, c = _xyc()
        mine = 2 * x + y
        local = [pltpu.make_async_copy(t_refs[a].at[mine], o_refs[a].at[mine], local_sems.at[a]) for a in range(n)]
        remote = []
        for k, (fx, fy) in enumerate(flips):
            px = 1 - x if fx else x
            py = 1 - y if fy else y
            remote += [pltpu.make_async_remote_copy(
                src_ref=t_refs[a].at[2 * px + py], dst_ref=o_refs[a].at[mine],
                send_sem=send_sems.at[k * n + a], recv_sem=recv_sems.at[k * n + a], device_id=(px, py, c),
                device_id_type=MESH) for a in range(n)]
        return local, remote

    def start(t_refs, o_refs, sems):
        local, remote = build(t_refs, o_refs, sems)
        for cp in local + remote:
            cp.start()

    def finish(t_refs, o_refs, sems):
        local, remote = build(t_refs, o_refs, sems)
        for cp in remote:
            cp.wait_recv()
        for cp in remote:
            cp.wait_send()
        for cp in local:
            cp.wait()

    return _Exchange(list(ts), [jax.ShapeDtypeStruct(a.shape, a.dtype) for a in ts],
                     [pltpu.SemaphoreType.DMA((3 * n,)), pltpu.SemaphoreType.DMA((3 * n,)),
                      pltpu.SemaphoreType.DMA((n,))], start, finish)


def _combine(*plans):
    def parts(refs, attr):
        out, at = [], 0
        for p in plans:
            n = len(getattr(p, attr))
            out.append(refs[at:at + n])
            at += n
        return out

    def run(half):
        def go(ins, outs, sems):
            for p, a, o, s in zip(plans, parts(ins, "ins"), parts(outs, "outs"), parts(sems, "scratch")):
                getattr(p, half)(a, o, s)
        return go

    return _Exchange(sum((p.ins for p in plans), []), sum((p.outs for p in plans), []),
                     sum((p.scratch for p in plans), []), run("start"), run("finish"))


def _exchange_call(plan, name):
    n = len(plan.ins)

    def body(*refs):
        ins, outs, sems = refs[:n], refs[n:2 * n], refs[2 * n:]
        plan.start(ins, outs, sems)
        plan.finish(ins, outs, sems)

    return pl.pallas_call(
        body, name=name, out_shape=plan.outs,
        in_specs=[pl.BlockSpec(memory_space=pl.ANY)] * n, out_specs=[pl.BlockSpec(memory_space=pl.ANY)] * n,
        scratch_shapes=plan.scratch,
    )(*plan.ins)


def _slab_spec(lead, rows, cols, nb):
    if rows % (nb * 16) == 0:
        return pl.BlockSpec((lead, rows // nb, cols), lambda i: (0, i, 0))
    if cols % (nb * 128) == 0:
        return pl.BlockSpec((lead, rows, cols // nb), lambda i: (0, 0, i))
    return pl.BlockSpec((lead, rows, cols), lambda i: (0, 0, 0))


def _slab_spec2(rows, cols, nb):
    if rows % (nb * 16) == 0:
        return pl.BlockSpec((rows // nb, cols), lambda i: (i, 0))
    if cols % (nb * 128) == 0:
        return pl.BlockSpec((rows, cols // nb), lambda i: (0, i))
    return pl.BlockSpec((rows, cols), lambda i: (0, 0))


def _cast_call(arrays, name, host=None):
    n = len(arrays)
    nb = 8

    def body(*refs):
        for a in range(n):
            refs[n + a][...] = refs[a][...].astype(BF16)

    specs = [_slab_spec2(x.shape[0], x.shape[1], nb) for x in arrays]
    return _hosting_call(body, name, nb, host, list(arrays), specs,
                         [jax.ShapeDtypeStruct(x.shape, BF16) for x in arrays], specs, [])


def _pair_add(sends, fromsib, name):
    n = len(sends)
    nb = 8

    def body(*refs):
        for a in range(n):
            s_ref, f_ref, t_ref = refs[a], refs[n + a], refs[2 * n + a]
            t_ref[...] = (s_ref[...].astype(F32) + f_ref[...].astype(F32)).astype(t_ref.dtype)

    def spec(a, lead):
        return _slab_spec(lead, a.shape[1], a.shape[2], nb)

    def mine(a):
        band = _slab_spec(4, a.shape[1], a.shape[2], nb)
        lead, rows, cols = band.block_shape
        moves_rows, moves_cols = rows != a.shape[1], cols != a.shape[2]
        return pl.BlockSpec((lead, None, rows, cols),
                            lambda i: (0, lax.axis_index("c"), i if moves_rows else 0, i if moves_cols else 0))

    return pl.pallas_call(
        body, name=name, grid=(nb,),
        in_specs=[mine(a) for a in sends] + [spec(a, 4) for a in fromsib],
        out_specs=[spec(a, 4) for a in fromsib],
        out_shape=[jax.ShapeDtypeStruct(a.shape, a.dtype) for a in fromsib],
        compiler_params=pltpu.CompilerParams(dimension_semantics=("arbitrary",), vmem_limit_bytes=VMEM_LIMIT),
    )(*[a.reshape((4, 2) + a.shape[1:]) for a in sends], *fromsib)


def _adamw_vals(w, g, m, v):
    m2 = ADAM_B1 * m + (1.0 - ADAM_B1) * g
    v2 = ADAM_B2 * v + (1.0 - ADAM_B2) * (g * g)
    m_hat = m2 / (1.0 - ADAM_B1 ** ADAM_STEP)
    v_hat = v2 / (1.0 - ADAM_B2 ** ADAM_STEP)
    delta = -ADAM_LR * (m_hat / (jnp.sqrt(v_hat) + ADAM_EPS) + ADAM_WD * w)
    return delta, m2, v2


def _sum_adamw(recv, w, m, v, name):
    R, C = w.shape
    ns = recv.shape[0]
    br = next((t for t in (256, 128, 64, 32, 16) if R % t == 0), R)

    def body(r_ref, w_ref, m_ref, v_ref, g_ref, d_ref, m2_ref, v2_ref):
        g = r_ref[0].astype(F32)
        for d in range(1, ns):
            g = g + r_ref[d].astype(F32)
        dl, m2, v2 = _adamw_vals(w_ref[...], g, m_ref[...], v_ref[...])
        g_ref[...] = g
        d_ref[...] = dl
        m2_ref[...] = m2
        v2_ref[...] = v2

    spec = pl.BlockSpec((br, C), lambda i: (i, 0))
    return pl.pallas_call(
        body, name=name, grid=(R // br,),
        in_specs=[pl.BlockSpec((ns, br, C), lambda i: (0, i, 0)), spec, spec, spec], out_specs=[spec] * 4,
        out_shape=[jax.ShapeDtypeStruct((R, C), F32)] * 4,
        compiler_params=pltpu.CompilerParams(dimension_semantics=("arbitrary",)),
    )(recv, w, m, v)


def _updates_call(recvs, ws, ms, vs, name, host=None):
    n = len(recvs)
    nb = 8

    def body(*refs):
        for a in range(n):
            r_ref, w_ref, m_ref, v_ref = refs[a], refs[n + a], refs[2 * n + a], refs[3 * n + a]
            g_ref, d_ref, m2_ref, v2_ref = refs[4 * n + 4 * a:4 * n + 4 * a + 4]
            g = r_ref[0].astype(F32)
            for d in range(1, r_ref.shape[0]):
                g = g + r_ref[d].astype(F32)
            dl, m2, v2 = _adamw_vals(w_ref[...], g, m_ref[...], v_ref[...])
            g_ref[...] = g
            d_ref[...] = dl
            m2_ref[...] = m2
            v2_ref[...] = v2

    def spec3(r):
        return _slab_spec(r.shape[0], r.shape[1], r.shape[2], nb)

    def spec2(w):
        return _slab_spec2(w.shape[0], w.shape[1], nb)

    res, hosted = _hosting_call(
        body, name, nb, host, list(recvs) + list(ws) + list(ms) + list(vs),
        [spec3(r) for r in recvs] + [spec2(w) for w in ws] * 3,
        [jax.ShapeDtypeStruct(w.shape, F32) for w in ws for _ in range(4)],
        [spec2(w) for w in ws for _ in range(4)], [])
    return [res[4 * a:4 * a + 4] for a in range(n)], hosted


def _small_sum(gath, loss_g, row0_g, name):
    _, R, C = gath.shape
    br = R // 3

    def body(g_ref, l_ref, r_ref, go_ref, lo_ref):
        g = g_ref[0].astype(F32)
        lsum = l_ref[0]
        for d in range(1, N_DEV):
            g = g + g_ref[d].astype(F32)
            lsum = lsum + l_ref[d]
        go_ref[...] = g
        lo_ref[...] = lsum

        @pl.when(pl.program_id(0) == 0)
        def _():
            row0 = r_ref[0]
            for d in range(1, N_DEV):
                row0 = row0 + r_ref[d]
            go_ref[0:8, :] = go_ref[0:8, :] + jnp.where(lax.broadcasted_iota(jnp.int32, row0.shape, 0) == 0, row0, 0.0)

    return pl.pallas_call(
        body, name=name, grid=(R // br,),
        in_specs=[pl.BlockSpec((N_DEV, br, C), lambda i: (0, i, 0)),
                  pl.BlockSpec((N_DEV, 8, HD), lambda i: (0, 0, 0)), pl.BlockSpec((N_DEV, 8, C), lambda i: (0, 0, 0))],
        out_specs=[pl.BlockSpec((br, C), lambda i: (i, 0)), pl.BlockSpec((8, HD), lambda i: (0, 0))],
        out_shape=[jax.ShapeDtypeStruct((R, C), F32), jax.ShapeDtypeStruct((8, HD), F32)],
        compiler_params=pltpu.CompilerParams(dimension_semantics=("arbitrary",)),
    )(gath, loss_g, row0_g)


def _adamw_multi(ws, gs, ms, vs, name, nblk=1):
    n = len(ws)

    def body(*refs):
        for a in range(n):
            dl, m2, v2 = _adamw_vals(refs[a][...], refs[n + a][...], refs[2 * n + a][...], refs[3 * n + a][...])
            refs[4 * n + 3 * a][...] = dl
            refs[4 * n + 3 * a + 1][...] = m2
            refs[4 * n + 3 * a + 2][...] = v2

    def spec(x):
        rest = (0,) * (x.ndim - 1)
        return pl.BlockSpec((x.shape[0] // nblk,) + tuple(x.shape[1:]), lambda i: (i,) + rest)

    res = pl.pallas_call(
        body, name=name, grid=(nblk,),
        in_specs=[spec(w) for w in ws] * 4, out_specs=[spec(w) for w in ws for _ in range(3)],
        out_shape=[jax.ShapeDtypeStruct(w.shape, F32) for w in ws for _ in range(3)],
        compiler_params=pltpu.CompilerParams(dimension_semantics=("arbitrary",), vmem_limit_bytes=VMEM_LIMIT),
    )(*ws, *gs, *ms, *vs)
    return [res[3 * a:3 * a + 3] for a in range(n)]


def _s5_param_fn(lr, li, ls, btr, bti):
    step = jnp.exp(ls)
    er = jnp.exp(lr * step)
    ang = li * step
    ar = er * jnp.cos(ang)
    ai = er * jnp.sin(ang)
    nr = ar - 1.0
    den = lr * lr + li * li
    fr = (nr * lr + ai * li) / den
    fi = (ai * lr - nr * li) / den
    return ar, ai, fr * btr - fi * bti, fr * bti + fi * btr


def _s5_params(lr, li, ls, btr, bti):
    def body(lr_ref, li_ref, ls_ref, br_ref, bi_ref, ar_ref, ai_ref, bbr_ref, bbi_ref):
        ar, ai, bbr, bbi = _s5_param_fn(lr_ref[...], li_ref[...], ls_ref[...], br_ref[...], bi_ref[...])
        ar_ref[...] = ar
        ai_ref[...] = ai
        bbr_ref[...] = bbr
        bbi_ref[...] = bbi

    sd = jax.ShapeDtypeStruct
    return pl.pallas_call(
        body, name="s5_params",
        out_shape=[sd(lr.shape, F32), sd(lr.shape, F32), sd(btr.shape, F32), sd(btr.shape, F32)],
    )(lr, li, ls, btr, bti)


def _s5_params_bwd(lr, li, ls, btr, bti, dar, dai, dbbr, dbbi):
    def body(lr_ref, li_ref, ls_ref, br_ref, bi_ref, dar_ref, dai_ref, dbbr_ref, dbbi_ref,
             dlr_ref, dli_ref, dls_ref, dbr_ref, dbi_ref):
        _, vjp = jax.vjp(_s5_param_fn, lr_ref[...], li_ref[...], ls_ref[...], br_ref[...], bi_ref[...])
        dlr, dli, dls, dbr, dbi = vjp((dar_ref[...], dai_ref[...], dbbr_ref[...], dbbi_ref[...]))
        dlr_ref[...] = dlr
        dli_ref[...] = dli
        dls_ref[...] = dls
        dbr_ref[...] = dbr
        dbi_ref[...] = dbi

    sd = jax.ShapeDtypeStruct
    return pl.pallas_call(
        body, name="s5_params_bwd",
        out_shape=[sd(lr.shape, F32), sd(lr.shape, F32), sd(ls.shape, F32), sd(btr.shape, F32), sd(btr.shape, F32)],
    )(lr, li, ls, btr, bti, dar, dai, dbbr, dbbi)


def _cpow(ar, ai, n):
    assert n & (n - 1) == 0
    while n > 1:
        ar, ai = ar * ar - ai * ai, 2.0 * ar * ai
        n //= 2
    return ar, ai


def _scan(st, cr, ci, init, nk, reverse, store, prev=None):
    W = S5_W

    def step(j, carry):
        k = nk - 1 - j if reverse else j
        rows = pl.ds(pl.multiple_of(k * 8, 8), 8)
        sr, si = carry[0], carry[1]
        nsr = cr * sr - ci * si + st[rows, 0:W]
        nsi = cr * si + ci * sr + st[rows, W:2 * W]
        if store:
            st[rows, 0:W] = nsr
            st[rows, W:2 * W] = nsi
        if prev is None:
            return nsr, nsi
        prows = pl.ds(pl.multiple_of(jnp.maximum(k - 1, 0) * 8, 8), 8)
        w = jnp.where(k > 0, 1.0, 0.0).astype(F32)
        pr = prev[prows, 0:W] * w
        pi = prev[prows, W:2 * W] * w
        return nsr, nsi, carry[2] + nsr * pr + nsi * pi, carry[3] + nsi * pr - nsr * pi

    return lax.fori_loop(0, nk, step, init, unroll=2)


def _chain(fin, fr, fi, pr, pi, reverse):
    W = S5_W
    fin[:, 0:W] = fr
    fin[:, W:2 * W] = fi
    rowid = lax.broadcasted_iota(jnp.int32, (8, W), 0)
    cr = jnp.zeros((1, W), F32)
    ci = jnp.zeros((1, W), F32)
    init_r = jnp.zeros((8, W), F32)
    init_i = jnp.zeros((8, W), F32)
    for s in (range(7, -1, -1) if reverse else range(8)):
        init_r = jnp.where(rowid == s, cr, init_r)
        init_i = jnp.where(rowid == s, ci, init_i)
        lr = fin[s:s + 1, 0:W]
        li = fin[s:s + 1, W:2 * W]
        cr, ci = lr + pr * cr - pi * ci, li + pr * ci + pi * cr
    return init_r, init_i


def _full_scan(st, fin, ar, ai, nk, reverse, prev=None, carry_in=None, carry_out=None):
    W = S5_W
    cr = jnp.broadcast_to(ar, (8, W))
    ci = jnp.broadcast_to(-ai if reverse else ai, (8, W))
    z = jnp.zeros((8, W), F32)
    if carry_in is None:
        fr, fi = _scan(st, cr, ci, (z, z), nk, reverse, store=False)
        pr, pi = _cpow(ar, -ai if reverse else ai, nk)
        init = _chain(fin, fr, fi, pr, pi, reverse)
    else:
        init = (carry_in[:, 0:W], carry_in[:, W:2 * W])
    if carry_out is not None:
        carry_out[:, 0:W] = init[0]
        carry_out[:, W:2 * W] = init[1]
    if prev is None:
        return _scan(st, cr, ci, init, nk, reverse, store=True)
    return _scan(st, cr, ci, init + (z, z), nk, reverse, store=True, prev=prev)


def _s5_specs(L):
    W2 = 2 * S5_W
    GC = S5_GB * S5_C
    col = pl.BlockSpec((L, GC), lambda g: (0, g))
    vec = pl.BlockSpec((1, GC), lambda g: (0, g))
    avec = pl.BlockSpec((1, S5_W), lambda g: (0, g))
    bmat = pl.BlockSpec((None, GC, W2), lambda g: (g, 0, 0))
    cmat = pl.BlockSpec((None, W2, GC), lambda g: (g, 0, 0))
    return col, vec, avec, bmat, cmat


def _interleave(dst, src, nk):
    for s in range(8):
        dst[pl.ds(s, nk, stride=8), :] = src[s * nk:(s + 1) * nk, :]


def _deinterleave(dst, src, nk):
    for s in range(8):
        dst[s * nk:(s + 1) * nk, :] = src[pl.ds(s, nk, stride=8), :].astype(dst.dtype)


def _hosting_call(body, name, nsteps, host, ins, in_specs, outs, out_specs, scratch):
    grid = (nsteps,) if isinstance(nsteps, int) else tuple(nsteps)
    params = pltpu.CompilerParams(dimension_semantics=("arbitrary",) * len(grid), vmem_limit_bytes=VMEM_LIMIT)
    if host is None:
        res = pl.pallas_call(
            body, name=name, grid=grid, in_specs=in_specs, out_specs=out_specs, out_shape=outs,
            scratch_shapes=scratch, compiler_params=params,
        )(*ins)
        return list(res), []
    n_in, n_out, n_sc = len(ins), len(outs), len(scratch)
    h_in, h_out = len(host.ins), len(host.outs)

    def hosted(*refs):
        a = refs[:n_in]
        ha = refs[n_in:n_in + h_in]
        o = refs[n_in + h_in:n_in + h_in + n_out]
        ho = refs[n_in + h_in + n_out:n_in + h_in + n_out + h_out]
        sc = refs[n_in + h_in + n_out + h_out:n_in + h_in + n_out + h_out + n_sc]
        hs = refs[n_in + h_in + n_out + h_out + n_sc:]
        first = functools.reduce(jnp.logical_and, [pl.program_id(i) == 0 for i in range(len(grid))])
        last = functools.reduce(jnp.logical_and, [pl.program_id(i) == g - 1 for i, g in enumerate(grid)])

        @pl.when(first)
        def _():
            host.start(ha, ho, hs)

        body(*a, *o, *sc)

        @pl.when(last)
        def _():
            host.finish(ha, ho, hs)

    hbm = pl.BlockSpec(memory_space=pl.ANY)
    res = pl.pallas_call(
        hosted, name=name, grid=grid,
        in_specs=list(in_specs) + [hbm] * h_in, out_specs=list(out_specs) + [hbm] * h_out,
        out_shape=list(outs) + list(host.outs), scratch_shapes=list(scratch) + list(host.scratch),
        compiler_params=params,
    )(*ins, *host.ins)
    return list(res[:n_out]), list(res[n_out:])


def _s5_fwd(u, bm, cm, ar, ai, dvec, host=None):
    L = u.shape[0]
    nk = L // 8
    GC = S5_GB * S5_C
    nb = S5_G // S5_GB
    col, vec, avec, bmat, cmat = _s5_specs(L)

    def body(u_ref, b_ref, c_ref, ar_ref, ai_ref, d_ref, y_ref, carry_ref, st, fin, ui, yi):
        _interleave(ui, u_ref, nk)
        for r in range(8):
            rows = slice(r * nk, (r + 1) * nk)
            st[rows, :] = _dot(ui[rows, :].astype(BF16), b_ref[...])
        _full_scan(st, fin, ar_ref[...], ai_ref[...], nk, reverse=False, carry_out=carry_ref)
        for r in range(8):
            rows = slice(r * nk, (r + 1) * nk)
            yi[rows, :] = _dot_nt(st[rows, :].astype(BF16), c_ref[...]) + d_ref[...] * ui[rows, :]
        _deinterleave(y_ref, yi, nk)

    return _hosting_call(
        body, "s5_fwd", nb, host,
        [u, bm, cm, ar, ai, dvec], [col, bmat, bmat, avec, avec, vec],
        [jax.ShapeDtypeStruct(u.shape, F32), jax.ShapeDtypeStruct((nb * 8, 2 * S5_W), F32)],
        [col, pl.BlockSpec((8, 2 * S5_W), lambda g: (g, 0))],
        [pltpu.VMEM((L, 2 * S5_W), F32), pltpu.VMEM((8, 2 * S5_W), F32), pltpu.VMEM((L, GC), F32),
         pltpu.VMEM((L, GC), F32)])


def _s5_bwd(u, dy, carry, bm, cm, ar, ai, dvec, mask, rmat, host=None):
    L = u.shape[0]
    nk = L // 8
    W = S5_W
    GC = S5_GB * S5_C
    col, vec, avec, bmat, cmat = _s5_specs(L)
    hi = lax.Precision.HIGHEST

    def body(u_ref, dy_ref, carry_ref, b_ref, ct_ref, ar_ref, ai_ref, d_ref, mask_ref, r_ref,
             du_ref, db_ref, dc_ref, dd_ref, dar_ref, dai_ref, sa, sb, fin, ui, dyi, dui):
        ar = ar_ref[...]
        ai = ai_ref[...]
        _interleave(ui, u_ref, nk)
        _interleave(dyi, dy_ref, nk)
        for r in range(8):
            rows = slice(r * nk, (r + 1) * nk)
            sa[rows, :] = _dot(ui[rows, :].astype(BF16), b_ref[...])
            sb[rows, :] = _dot(dyi[rows, :].astype(BF16), ct_ref[...])
        _full_scan(sa, fin, ar, ai, nk, reverse=False, carry_in=carry_ref)
        gr, gi, accr, acci = _full_scan(sb, fin, ar, ai, nk, reverse=True, prev=sa)
        rowid = lax.broadcasted_iota(jnp.int32, (8, W), 0)
        last = pl.ds((nk - 1) * 8, 8)
        pr = jnp.where(rowid == 0, 0.0, pltpu.roll(sa[last, 0:W], 1, 0))
        pi = jnp.where(rowid == 0, 0.0, pltpu.roll(sa[last, W:2 * W], 1, 0))
        accr = accr + gr * pr + gi * pi
        acci = acci + gi * pr - gr * pi
        dar_ref[...] = jnp.sum(accr, axis=0, keepdims=True)
        dai_ref[...] = jnp.sum(acci, axis=0, keepdims=True)
        dbf = jnp.zeros((GC, 2 * W), F32)
        dcf = jnp.zeros((GC, 2 * W), F32)
        dd = jnp.zeros((1, GC), F32)
        for r in range(8):
            rows = slice(r * nk, (r + 1) * nk)
            ub = ui[rows, :]
            dyb = dyi[rows, :]
            gb = sb[rows, :].astype(BF16)
            dui[rows, :] = _dot_nt(gb, b_ref[...]) + d_ref[...] * dyb
            dbf = dbf + _dot_tn(ub.astype(BF16), gb)
            dcf = dcf + _dot_tn(dyb.astype(BF16), sa[rows, :].astype(BF16))
            dd = dd + jnp.sum(dyb * ub, axis=0, keepdims=True)
        db_ref[...] = jnp.dot(dbf * mask_ref[...], r_ref[...], precision=hi, preferred_element_type=F32)
        dc_ref[...] = jnp.dot(dcf * mask_ref[...], r_ref[...], precision=hi, preferred_element_type=F32)
        dd_ref[...] = dd
        _deinterleave(du_ref, dui, nk)

    cmp_spec = pl.BlockSpec((GC, 2 * S5_P), lambda g: (g, 0))
    whole = lambda shape: pl.BlockSpec(shape, lambda g: (0, 0))
    sd = jax.ShapeDtypeStruct
    return _hosting_call(
        body, "s5_bwd", S5_G // S5_GB, host,
        [u, dy, carry, bm, cm, ar, ai, dvec, mask, rmat],
        [col, col, pl.BlockSpec((8, 2 * W), lambda g: (g, 0)), bmat, bmat, avec, avec, vec, whole(mask.shape),
         whole(rmat.shape)],
        [sd(u.shape, BF16), sd((S5_G * S5_C, 2 * S5_P), F32), sd((S5_G * S5_C, 2 * S5_P), F32),
         sd((1, PRIM), F32), sd((1, S5_G * S5_P), F32), sd((1, S5_G * S5_P), F32)],
        [col, cmp_spec, cmp_spec, vec, avec, avec],
        [pltpu.VMEM((L, 2 * W), F32), pltpu.VMEM((L, 2 * W), F32), pltpu.VMEM((8, 2 * W), F32),
         pltpu.VMEM((L, GC), F32), pltpu.VMEM((L, GC), F32), pltpu.VMEM((L, GC), F32)])


def _s5_mats(bbr, bbi, cre, cim):
    nb = S5_G // S5_GB
    eye = jnp.eye(S5_GB, dtype=BF16)

    def blocks(re, im):
        x = jnp.stack([re, im], axis=2).astype(BF16).reshape(nb, S5_GB, S5_C, 2, S5_P)
        return jnp.einsum('ngcrp,gh->ngcrhp', x, eye).reshape(nb, S5_GB * S5_C, 2 * S5_W)

    return blocks(bbr, bbi), blocks(cre, -cim)


def _s5_compact_consts():
    g_row = np.arange(S5_GB * S5_C) // S5_C
    col = np.arange(2 * S5_W)
    g_col = (col % S5_W) // S5_P
    mask = (g_row[:, None] == g_col[None, :]).astype(np.float32)
    tgt = (col // S5_W) * S5_P + col % S5_P
    rmat = (tgt[:, None] == np.arange(2 * S5_P)[None, :]).astype(np.float32)
    return jnp.asarray(mask), jnp.asarray(rmat)


def _attn_scores(q_ref, k_ref, qb, bq, scale):
    ext = (qb + 1) * bq
    s = _dot_nt(q_ref[qb * bq:ext, :], k_ref[0:ext, :]) * scale
    qpos = lax.broadcasted_iota(jnp.int32, (bq, bq), 0)
    kpos = lax.broadcasted_iota(jnp.int32, (bq, bq), 1)
    diag = jnp.where(kpos <= qpos, s[:, ext - bq:], NEG)
    return diag if qb == 0 else jnp.concatenate([s[:, :ext - bq], diag], axis=-1)


def _attn_fwd(qp, kp, v, scale):
    L = qp.shape[0]
    bq = min(256, L)

    def body(q_ref, k_ref, v_ref, o_ref, lse_ref):
        for qb in range(L // bq):
            rows = slice(qb * bq, (qb + 1) * bq)
            s = _attn_scores(q_ref, k_ref, qb, bq, scale)
            m = jnp.max(s, axis=-1, keepdims=True)
            e = jnp.exp(s - m)
            l = jnp.sum(e, axis=-1, keepdims=True)
            o_ref[rows, :] = _dot(e.astype(BF16), v_ref[0:(qb + 1) * bq, :]) / l
            lse_ref[rows, :] = jnp.broadcast_to(m + jnp.log(l), (bq, HD))

    blk = pl.BlockSpec((L, HD), lambda h: (0, h))
    wide = pl.BlockSpec((L, 2 * HD), lambda h: (0, h))
    return pl.pallas_call(
        body, name="mla_attn_fwd", grid=(MLA_H,),
        in_specs=[wide, wide, blk], out_specs=[blk, blk],
        out_shape=[jax.ShapeDtypeStruct((L, MLA_H * HD), F32)] * 2,
        compiler_params=pltpu.CompilerParams(dimension_semantics=("arbitrary",), vmem_limit_bytes=VMEM_LIMIT),
    )(qp, kp, v)


def _attn_bwd(qp, kp, v, o, lse, do, scale):
    L = qp.shape[0]
    bq = min(256, L)
    nq = L // bq

    def body(q_ref, k_ref, v_ref, o_ref, lse_ref, do_ref, dq_ref, dk_ref, dv_ref, dk_acc, dv_acc):
        dk_acc[...] = jnp.zeros_like(dk_acc)
        dv_acc[...] = jnp.zeros_like(dv_acc)
        for qb in range(nq):
            rows = slice(qb * bq, (qb + 1) * bq)
            ext = (qb + 1) * bq
            do = do_ref[rows, :]
            dob = do.astype(BF16)
            p = jnp.exp(_attn_scores(q_ref, k_ref, qb, bq, scale) - lse_ref[rows, 0:1])
            dp = _dot_nt(dob, v_ref[0:ext, :])
            dsum = jnp.sum(do * o_ref[rows, :], axis=-1, keepdims=True)
            ds = (p * (dp - dsum) * scale).astype(BF16)
            dq_ref[rows, :] = _dot(ds, k_ref[0:ext, :]).astype(dq_ref.dtype)
            dk_acc[0:ext, :] += _dot_tn(ds, q_ref[rows, :])
            dv_acc[0:ext, :] += _dot_tn(p.astype(BF16), dob)
        dk_ref[...] = dk_acc[...].astype(dk_ref.dtype)
        dv_ref[...] = dv_acc[...].astype(dv_ref.dtype)

    sd = jax.ShapeDtypeStruct
    blk = pl.BlockSpec((L, HD), lambda h: (0, h))
    wide = pl.BlockSpec((L, 2 * HD), lambda h: (0, h))
    return pl.pallas_call(
        body, name="mla_attn_bwd", grid=(MLA_H,),
        in_specs=[wide, wide, blk, blk, blk, blk], out_specs=[wide, wide, blk],
        out_shape=[sd((L, MLA_H * 2 * HD), BF16), sd((L, MLA_H * 2 * HD), BF16), sd((L, MLA_H * HD), BF16)],
        scratch_shapes=[pltpu.VMEM((L, 2 * HD), F32), pltpu.VMEM((L, HD), F32)],
        compiler_params=pltpu.CompilerParams(dimension_semantics=("arbitrary",), vmem_limit_bytes=VMEM_LIMIT),
    )(qp, kp, v, o, lse, do)


def _kv_fn(mem, gm, w, gk):
    kv = _mm(_rms(mem, gm, D_MODEL), w)
    k = jnp.concatenate([_rms(kv[:, HD * h:HD * (h + 1)], gk, HD) for h in range(X_HEADS)], axis=-1)
    return k, kv[:, XQ:]


def _kv_prep(mem, gm, w, gk, name):
    def fn(mem, gm, w, gk):
        return _kv_fn(mem, gm, w, gk)
    M = mem.shape[0]
    return _rowwise(name, fn, [('c', mem), ('c', gm), ('c', w), ('c', gk)],
                    [('c', (M, XQ), F32), ('c', (M, XQ), F32)], 1)


def _kv_prep_bwd(mem, gm, w, gk, dk, dv, name):
    def fn(mem, gm, w, gk, dk, dv):
        _, vjp = jax.vjp(lambda a, b, c: _kv_fn(mem, a, b, c), gm, w, gk)
        return vjp((dk, dv))
    return _rowwise(name, fn, [('c', mem), ('c', gm), ('c', w), ('c', gk), ('c', dk), ('c', dv)],
                    [('c', gm.shape, F32), ('c', w.shape, BF16), ('c', gk.shape, F32)], 1)


def _forward_merge(x, mix, mix_kind, xq, gate, k, v, gq, wout, name, nblk, sub, host=None):
    def fn(x, mix, xq, gate, k, v, gq, wout):
        o = _merge(mix, xq, gate, k, v, gq)
        return (x + _dot(o.astype(BF16), wout),)
    L = x.shape[0]
    out = _rowwise(name, fn, [('r', x), (mix_kind, mix), ('r', xq), ('r', gate), ('c', k), ('c', v), ('c', gq),
                              ('c', wout)], [('r', (L, D_MODEL), F32)], nblk, sub, host=host)
    return out[0] if host is None else (out[0][0], out[1])


def _backward_merge(dx, mix, mix_kind, xq, gate, k, v, gq, wout, name, nblk, sub, host=None):
    def fn(dx, mix, xq, gate, k, v, gq, wout):
        g16 = dx.astype(BF16)
        do = _dot_nt(g16, wout)
        o, vjp = jax.vjp(_merge, mix, xq, gate, k, v, gq)
        dmix, dxq, dgate, dk, dv, dgq = vjp(do)
        return dmix, dxq, dgate, o, g16, dk, dv, dgq
    L = dx.shape[0]
    return _rowwise(
        name, fn,
        [('r', dx), (mix_kind, mix), ('r', xq), ('r', gate), ('c', k), ('c', v), ('c', gq), ('c', wout)],
        [('r', (L, PRIM), F32), ('r', (L, XQ), BF16), ('r', (L, BRANCH), BF16), ('t', (BRANCH, L), BF16),
         ('r', (L, D_MODEL), BF16), ('a', k.shape, F32), ('a', v.shape, F32), ('a', gq.shape, F32)], nblk, sub,
        host=host)


_MLA_IN = 3392
_MLA_IN_PAD = 3456


def _uq_rows(wt):
    r = wt.reshape(MLA_H, HD + ROPE, wt.shape[1])
    return jnp.concatenate([r[:, :HD].reshape(PRIM, -1),
                            jnp.pad(r[:, HD:], ((0, 0), (0, HD - ROPE), (0, 0))).reshape(PRIM, -1)], axis=0)


def _uq_rows_back(wt):
    nope = wt[:PRIM].reshape(MLA_H, HD, -1)
    rope = wt[PRIM:].reshape(MLA_H, HD, -1)[:, :ROPE]
    return jnp.concatenate([nope, rope], axis=1).reshape(MLA_H * (HD + ROPE), -1)


def _mla_in_rows(wt):
    return jnp.concatenate([wt[:768], wt[832:], wt[768:832], jnp.zeros((64, wt.shape[1]), wt.dtype)], axis=0)


def _mla_in_rows_back(wt):
    return jnp.concatenate([wt[:768], wt[3328:3392], wt[768:3328]], axis=0)


_SMALL = (("ln_gain", 2048), ("mem_norm", 2048), ("xq_norm", 256), ("xk_norm", 256), ("s5_lambda_re", 6144),
          ("s5_lambda_im", 6144), ("s5_log_step", 96), ("s5_b_re", 98304), ("s5_b_im", 98304), ("s5_c_re", 98304),
          ("s5_c_im", 98304), ("s5_d", 1536), ("mla_q_lora_norm", 512), ("mla_kv_lora_norm", 256),
          ("mla_q_nope_norm", 128), ("mla_k_nope_norm", 128), ("mla_q_rope_norm", 64), ("mla_k_rope_norm", 64))
_SMALL_ROWS = 432
_SMALL_OFF = {name: sum(n for _, n in _SMALL[:i]) for i, (name, _) in enumerate(_SMALL)}


def _pack_small(d):
    flat = jnp.concatenate([d[n].reshape(-1).astype(F32) for n, _ in _SMALL])
    return jnp.pad(flat, (0, _SMALL_ROWS * 1024 - flat.shape[0])).reshape(_SMALL_ROWS, 1024)


def _unpack_small(p, name, shape):
    off = _SMALL_OFF[name]
    return p.reshape(-1)[off:off + int(np.prod(shape))].reshape(shape)


_WEIGHTS = ('ln_gain', 'w_out', 'mem_norm', 'w_mem_kv', 'xq_norm', 'xk_norm', 's5_w_in', 's5_lambda_re',
            's5_lambda_im', 's5_log_step', 's5_b_re', 's5_b_im', 's5_c_re', 's5_c_im', 's5_d', 's5_w_glu', 'mla_w_in',
            'mla_q_lora_norm', 'mla_kv_lora_norm', 'mla_w_uq', 'mla_w_ukv', 'mla_q_nope_norm', 'mla_k_nope_norm',
            'mla_q_rope_norm', 'mla_k_rope_norm')
_BIG = ('w_out', 'w_mem_kv', 's5_w_in', 's5_w_glu', 'mla_w_in', 'mla_w_uq', 'mla_w_ukv')


def _pad128(g):
    return jnp.pad(g.reshape(1, -1), ((0, 0), (0, HD - g.shape[-1])))


def kernel(x, mem, positions, ln_gain, w_out, mem_norm, w_mem_kv, xq_norm, xk_norm, s5_w_in, s5_lambda_re, s5_lambda_im, s5_log_step, s5_b_re, s5_b_im, s5_c_re, s5_c_im, s5_d, s5_w_glu, mla_w_in, mla_q_lora_norm, mla_kv_lora_norm, mla_w_uq, mla_w_ukv, mla_q_nope_norm, mla_k_nope_norm, mla_q_rope_norm, mla_k_rope_norm, loss_target, m_ln_gain, m_w_out, m_mem_norm, m_w_mem_kv, m_xq_norm, m_xk_norm, m_s5_w_in, m_s5_lambda_re, m_s5_lambda_im, m_s5_log_step, m_s5_b_re, m_s5_b_im, m_s5_c_re, m_s5_c_im, m_s5_d, m_s5_w_glu, m_mla_w_in, m_mla_q_lora_norm, m_mla_kv_lora_norm, m_mla_w_uq, m_mla_w_ukv, m_mla_q_nope_norm, m_mla_k_nope_norm, m_mla_q_rope_norm, m_mla_k_rope_norm, v_ln_gain, v_w_out, v_mem_norm, v_w_mem_kv, v_xq_norm, v_xk_norm, v_s5_w_in, v_s5_lambda_re, v_s5_lambda_im, v_s5_log_step, v_s5_b_re, v_s5_b_im, v_s5_c_re, v_s5_c_im, v_s5_d, v_s5_w_glu, v_mla_w_in, v_mla_q_lora_norm, v_mla_kv_lora_norm, v_mla_w_uq, v_mla_w_ukv, v_mla_q_nope_norm, v_mla_k_nope_norm, v_mla_q_rope_norm, v_mla_k_rope_norm):
    weights = dict(ln_gain=ln_gain, w_out=w_out, mem_norm=mem_norm, w_mem_kv=w_mem_kv, xq_norm=xq_norm,
                   xk_norm=xk_norm, s5_w_in=s5_w_in, s5_lambda_re=s5_lambda_re, s5_lambda_im=s5_lambda_im,
                   s5_log_step=s5_log_step, s5_b_re=s5_b_re, s5_b_im=s5_b_im, s5_c_re=s5_c_re, s5_c_im=s5_c_im,
                   s5_d=s5_d, s5_w_glu=s5_w_glu, mla_w_in=mla_w_in, mla_q_lora_norm=mla_q_lora_norm,
                   mla_kv_lora_norm=mla_kv_lora_norm, mla_w_uq=mla_w_uq, mla_w_ukv=mla_w_ukv,
                   mla_q_nope_norm=mla_q_nope_norm, mla_k_nope_norm=mla_k_nope_norm,
                   mla_q_rope_norm=mla_q_rope_norm, mla_k_rope_norm=mla_k_rope_norm)
    m_in = dict(zip(_WEIGHTS, (m_ln_gain, m_w_out, m_mem_norm, m_w_mem_kv, m_xq_norm, m_xk_norm, m_s5_w_in,
                               m_s5_lambda_re, m_s5_lambda_im, m_s5_log_step, m_s5_b_re, m_s5_b_im, m_s5_c_re,
                               m_s5_c_im, m_s5_d, m_s5_w_glu, m_mla_w_in, m_mla_q_lora_norm, m_mla_kv_lora_norm,
                               m_mla_w_uq, m_mla_w_ukv, m_mla_q_nope_norm, m_mla_k_nope_norm, m_mla_q_rope_norm,
                               m_mla_k_rope_norm)))
    v_in = dict(zip(_WEIGHTS, (v_ln_gain, v_w_out, v_mem_norm, v_w_mem_kv, v_xq_norm, v_xk_norm, v_s5_w_in,
                               v_s5_lambda_re, v_s5_lambda_im, v_s5_log_step, v_s5_b_re, v_s5_b_im, v_s5_c_re,
                               v_s5_c_im, v_s5_d, v_s5_w_glu, v_mla_w_in, v_mla_q_lora_norm, v_mla_kv_lora_norm,
                               v_mla_w_uq, v_mla_w_ukv, v_mla_q_nope_norm, v_mla_k_nope_norm, v_mla_q_rope_norm,
                               v_mla_k_rope_norm)))

    x0 = x[0]
    mem0 = mem[0]
    target = loss_target[0]
    L = x0.shape[0]
    nblk, sub = 8, 1
    me = 4 * lax.axis_index("x") + 2 * lax.axis_index("y") + lax.axis_index("c")

    lora = jnp.pad(jnp.concatenate([mla_q_lora_norm, mla_kv_lora_norm], axis=1), ((0, 7), (0, HD - 96)))
    def gather(*shards):
        return _plan_all_gather(list(shards))

    kh = D_MODEL // 2
    (b_mkv0, b_glu, b_in_mla, b_out0, b_uq, b_ukv, b_mkv1, b_out1), (W_in_s5,) = _cast_call(
        [w_mem_kv[0], s5_w_glu[0], jnp.transpose(mla_w_in[0]), w_out[0], jnp.transpose(mla_w_uq[0]), mla_w_ukv[0],
         w_mem_kv[1], w_out[1]], "cast_shards", host=gather(s5_w_in[0].astype(BF16)))

    ln0, ln1 = ln_gain[0:1], ln_gain[1:2]
    gq0, gq1 = xq_norm[0:1], xq_norm[1:2]
    gk0, gk1 = xk_norm[0:1], xk_norm[1:2]
    gm0, gm1 = mem_norm[0:1], mem_norm[1:2]
    gqn, gkn = mla_q_nope_norm, mla_k_nope_norm
    gqr, gkr = _pad128(mla_q_rope_norm), _pad128(mla_k_rope_norm)

    lr3 = s5_lambda_re.reshape(S5_G, 1, S5_P)
    li3 = s5_lambda_im.reshape(S5_G, 1, S5_P)
    ls3 = s5_log_step.reshape(S5_G, 1, 1)
    btr = jnp.swapaxes(s5_b_re[0], 1, 2)
    bti = jnp.swapaxes(s5_b_im[0], 1, 2)
    a_r, a_i, bbr, bbi = _s5_params(lr3, li3, ls3, btr, bti)
    bm, cm = _s5_mats(bbr, bbi, s5_c_re[0], s5_c_im[0])
    a_r2 = a_r.reshape(1, S5_G * S5_P)
    a_i2 = a_i.reshape(1, S5_G * S5_P)
    cmask, rmat = _s5_compact_consts()

    half = ROPE // 2
    inv_freq = ROPE_THETA ** (-jnp.arange(half, dtype=F32) / half)
    invf = jnp.concatenate([inv_freq, inv_freq, jnp.zeros((HD - ROPE,), F32)]).reshape(1, HD)

    def rot_tables(pos, invf):
        ang = pos.astype(F32) * invf
        lane = lax.broadcasted_iota(jnp.int32, ang.shape, 1)
        c = jnp.where(lane < ROPE, jnp.cos(ang), 0.0)
        s = jnp.sin(ang)
        return c, jnp.where(lane < half, -s, 0.0), jnp.where((lane >= half) & (lane < ROPE), s, 0.0)

    tc, ts1, ts2 = _rowwise("rot_tables", rot_tables, [('r', positions.reshape(L, 1)), ('c', invf)],
                            [('r', (L, HD), F32)] * 3, nblk, sub)

    def in_s5(x, g, w):
        proj = _mm_slots(_rms(x, g, D_MODEL).astype(BF16), w)
        return proj[:, :PRIM], proj[:, PRIM:PRIM + XQ], proj[:, PRIM + XQ:]

    (u_s5, xq_a, gate_a), (G_mkv0,) = _rowwise(
        "s5_in", in_s5, [('r', x0), ('c', ln0), ('c', W_in_s5)],
        [('r', (L, PRIM), F32), ('r', (L, XQ), F32), ('r', (L, BRANCH), F32)], nblk, sub, host=gather(b_mkv0))
    (y_s5, s5_carry), (W_glu, G_in_mla_a) = _s5_fwd(u_s5, bm, cm, a_r2, a_i2, s5_d,
                                                    host=gather(b_glu, b_in_mla[:, :kh]))

    def glu(y, w):
        z = _mm_slots(_gelu(y).astype(BF16), w)
        return (z[:, :PRIM] * _sigmoid(z[:, PRIM:]),)

    (y2,), (G_out0,) = _rowwise("s5_glu", glu, [('r', y_s5), ('c', W_glu)], [('r', (L, PRIM), F32)], nblk, sub,
                                host=gather(b_out0))
    W_mkv0 = G_mkv0.reshape(D_MODEL, 2 * XQ)
    k_a, v_a = _kv_prep(mem0, gm0, W_mkv0, gk0, "kv_prep0")
    x1, (G_in_mla_b,) = _forward_merge(
        x0, y2, 'r', xq_a, gate_a, k_a, v_a, gq0, G_out0.reshape(BRANCH, D_MODEL), "merge0", nblk, sub,
        host=gather(b_in_mla[:, kh:]))
    W_in_mla = _mla_in_rows(jnp.concatenate([G_in_mla_a, G_in_mla_b], axis=2).reshape(_MLA_IN, D_MODEL))

    def in_mla(x, g, w):
        proj = _dot_nt(_rms(x, g, D_MODEL).astype(BF16), w)
        return proj[:, :512], proj[:, 512:768], proj[:, 768:1280], proj[:, 1280:3328], proj[:, 3328:]

    (c_q, c_kv, xq_b, gate_b, krp), (G_uq, W_kv, G_lora) = _rowwise(
        "mla_in", in_mla, [('r', x1), ('c', ln1), ('c', W_in_mla)],
        [('r', (L, Q_LORA), F32), ('r', (L, KV_LORA), F32), ('r', (L, XQ), F32), ('r', (L, BRANCH), F32),
         ('r', (L, HD), F32)], nblk, sub,
        host=gather(b_uq, b_ukv, lora))
    W_q = _uq_rows(G_uq.reshape(MLA_H * (HD + ROPE), Q_LORA))
    g_qlora = G_lora[:, 0, :64].reshape(1, Q_LORA)
    g_kvlora = G_lora[:, 0, 64:96].reshape(1, KV_LORA)

    def qkv(c_q, c_kv, krp, tc, ts1, ts2, gql, gkvl, wq, wkv, gqn, gkn, gqr, gkr):
        q = _dot_nt(_rms(c_q, gql, Q_LORA).astype(BF16), wq)
        kv = _mm_slots(_rms(c_kv, gkvl, KV_LORA).astype(BF16), wkv)
        kp, v = _kv_post(kv, krp, gkn, gkr, tc, ts1, ts2)
        return _q_post(q, gqn, gqr, tc, ts1, ts2), kp, v

    qkv_consts = [('c', g_qlora), ('c', g_kvlora), ('c', W_q), ('c', W_kv), ('c', gqn), ('c', gkn), ('c', gqr),
                  ('c', gkr)]
    (q_pad, k_pad, v_h), (G_mkv1, G_out1) = _rowwise(
        "mla_qkv", qkv, [('r', c_q), ('r', c_kv), ('r', krp), ('r', tc), ('r', ts1), ('r', ts2)] + qkv_consts,
        [('r', (L, 2 * PRIM), BF16), ('r', (L, 2 * PRIM), BF16), ('r', (L, PRIM), BF16)], nblk, sub,
        host=gather(b_mkv1, b_out1))
    W_out = (G_out0.reshape(BRANCH, D_MODEL), G_out1.reshape(BRANCH, D_MODEL))
    W_mkv = (W_mkv0, G_mkv1.reshape(D_MODEL, 2 * XQ))
    scale = (HD + ROPE) ** -0.5
    attn, lse = _attn_fwd(q_pad, k_pad, v_h, scale)
    k_b, v_b = _kv_prep(mem0, gm1, W_mkv[1], gk1, "kv_prep1")

    def merge_loss(x, mix, xq, gate, k, v, gq, wout, t):
        err = x + _dot(_merge(mix, xq, gate, k, v, gq).astype(BF16), wout) - t
        part = 0.5 * jnp.sum(jnp.sum(err * err, axis=-1, keepdims=True) * (1.0 / D_MODEL), axis=0, keepdims=True)
        return err * (1.0 / D_MODEL), jnp.broadcast_to(part, (1, HD))

    dx2, loss_part = _rowwise(
        "merge1_loss", merge_loss,
        [('r', x1), ('r', attn), ('r', xq_b), ('r', gate_b), ('c', k_b), ('c', v_b), ('c', gq1), ('c', W_out[1]),
         ('r', target)], [('r', (L, D_MODEL), F32), ('a', (1, HD), F32)], nblk, sub)

    dattn, dxq_b, dgate_b, o_b, g_b, dk_b, dv_b, dgq1 = _backward_merge(
        dx2, attn, 'r', xq_b, gate_b, k_b, v_b, gq1, W_out[1], "merge1_bwd", nblk, sub)
    dgm1, dW_mkv1, dgk1 = _kv_prep_bwd(mem0, gm1, W_mkv[1], gk1, dk_b, dv_b, "kv_prep1_bwd")
    dW_out1 = _matmul_tn(o_b, g_b, "dw_out1")
    dq_pad, dk_pad, dv_h = _attn_bwd(q_pad, k_pad, v_h, attn, lse, dattn, scale)

    def qkv_bwd(c_q, c_kv, krp, tc, ts1, ts2, dqp, dkp, dv, gql, gkvl, wq, wkv, gqn, gkn, gqr, gkr):
        cqn, vjp_qn = jax.vjp(lambda a, b: _rms(a, b, Q_LORA), c_q, gql)
        ckvn, vjp_kvn = jax.vjp(lambda a, b: _rms(a, b, KV_LORA), c_kv, gkvl)
        cqn16 = cqn.astype(BF16)
        ckvn16 = ckvn.astype(BF16)
        q = _dot_nt(cqn16, wq)
        kv = _mm_slots(ckvn16, wkv)
        _, vjp_q = jax.vjp(lambda a, b, c: _q_post(a, b, c, tc, ts1, ts2), q, gqn, gqr)
        dq, dgqn, dgqr = vjp_q(dqp.astype(F32))
        _, vjp_kv = jax.vjp(lambda a, b, c, d: _kv_post(a, b, c, d, tc, ts1, ts2), kv, krp, gkn, gkr)
        dkv, dkrp, dgkn, dgkr = vjp_kv((dkp.astype(F32), dv.astype(F32)))
        dq16 = dq.astype(BF16)
        dkv16 = dkv.astype(BF16)
        dc_q, dgql = vjp_qn(_dot(dq16, wq))
        dc_kv, dgkvl = vjp_kvn(_mm_slots_nt(dkv16, wkv))
        return dc_q, dc_kv, dkrp, cqn16, dq16, ckvn16, dkv16, dgql, dgkvl, dgqn, dgkn, dgqr, dgkr

    (dc_q, dc_kv, dkrp, cqn16, dq16, ckvn16, dkv16, dgql, dgkvl, dgqn, dgkn, dgqr, dgkr) = _rowwise(
        "mla_qkv_bwd", qkv_bwd,
        [('r', c_q), ('r', c_kv), ('r', krp), ('r', tc), ('r', ts1), ('r', ts2), ('r', dq_pad), ('r', dk_pad),
         ('r', dv_h)] + qkv_consts,
        [('r', (L, Q_LORA), BF16), ('r', (L, KV_LORA), BF16), ('r', (L, HD), BF16), ('r', (L, Q_LORA), BF16),
         ('t', (2 * PRIM, L), BF16), ('t', (KV_LORA, L), BF16), ('r', (L, 2 * PRIM), BF16),
         ('a', (1, Q_LORA), F32), ('a', (1, KV_LORA), F32), ('a', (1, HD), F32), ('a', (1, HD), F32),
         ('a', (1, HD), F32), ('a', (1, HD), F32)], nblk, sub)
    dW_q = _matmul_tn(dq16, cqn16, "dw_uq")
    dW_kv = _matmul_tn_slots(ckvn16, dkv16, "dw_ukv")

    def in_bwd(x, dres, g, w, *dparts):
        dproj = jnp.concatenate(dparts, axis=-1).astype(BF16)
        xn, vjp = jax.vjp(lambda a, b: _rms(a, b, D_MODEL), x, g)
        dx, dg = vjp(_mm_slots_nt(dproj, w) if w.ndim == 3 else _dot(dproj, w))
        return dx + dres, xn, dproj, dg

    dx1, xn1, dproj1, dln1 = _rowwise(
        "mla_in_bwd", in_bwd,
        [('r', x1), ('r', dx2), ('c', ln1), ('c', W_in_mla), ('r', dc_q), ('r', dc_kv), ('r', dxq_b), ('r', dgate_b),
         ('r', dkrp)],
        [('r', (L, D_MODEL), F32), ('r', (L, D_MODEL), BF16), ('t', (_MLA_IN_PAD, L), BF16), ('a', (1, D_MODEL), F32)],
        nblk, sub)
    dW_in_mla = _matmul_tn(dproj1, xn1, "dw_mla_in")

    grads1 = [dW_out1.reshape(N_DEV, 256, D_MODEL), dW_mkv1.reshape(N_DEV, 128, 2 * XQ),
              _mla_in_rows_back(dW_in_mla).reshape(N_DEV, 424, D_MODEL),
              _uq_rows_back(dW_q).reshape(N_DEV, 288, Q_LORA), dW_kv]
    (dy2, dxq_a, dgate_a, o_a, g_a, dk_a, dv_a, dgq0), pair1 = _backward_merge(
        dx1, y2, 'r', xq_a, gate_a, k_a, v_a, gq0, W_out[0], "merge0_bwd", nblk, sub, host=_plan_pair(grads1))
    dgm0, dW_mkv0, dgk0 = _kv_prep_bwd(mem0, gm0, W_mkv[0], gk0, dk_a, dv_a, "kv_prep0_bwd")
    dW_out0 = _matmul_tn(o_a, g_a, "dw_out0")
    t1 = list(_pair_add(grads1, pair1, "rs_add_layer1"))

    def glu_bwd(y, dy2, w):
        h, vjp_h = jax.vjp(_gelu, y)
        h16 = h.astype(BF16)
        z = _mm_slots(h16, w)
        _, vjp_z = jax.vjp(lambda z: z[:, :PRIM] * _sigmoid(z[:, PRIM:]), z)
        dz16 = vjp_z(dy2)[0].astype(BF16)
        return vjp_h(_mm_slots_nt(dz16, w))[0], h16, dz16

    grads0 = [dW_out0.reshape(N_DEV, 256, D_MODEL), dW_mkv0.reshape(N_DEV, 128, 2 * XQ)]
    (dy_s5, h16, dz16), glu_hosted = _rowwise(
        "s5_glu_bwd", glu_bwd, [('r', y_s5), ('r', dy2), ('c', W_glu)],
        [('r', (L, PRIM), F32), ('t', (PRIM, L), BF16), ('r', (L, 2 * PRIM), BF16)], nblk, sub,
        host=_combine(_plan_chips(t1[2:]), _plan_pair(grads0)))
    recv_proj1, pair0 = glu_hosted[:3], glu_hosted[3:]
    dW_glu = _matmul_tn_slots(h16, dz16, "dw_glu")
    t0 = list(_pair_add(grads0 + [dW_glu], pair0 + list(_exchange_call(_plan_pair([dW_glu]), "rs_pair_glu")),
                        "rs_add_layer0"))
    (du_s5, dbc, dcc, dd, dar, dai), recv_rest = _s5_bwd(u_s5, dy_s5, s5_carry, bm, cm, a_r2, a_i2, s5_d,
                                                        cmask, rmat, host=_plan_chips(t1[:2] + t0))
    early_recv = recv_rest[:2] + recv_proj1 + recv_rest[2:]
    dbc4 = dbc.reshape(S5_G, S5_C, 2, S5_P)
    dcc4 = dcc.reshape(S5_G, S5_C, 2, S5_P)
    dlr, dli, dls, dbtr, dbti = _s5_params_bwd(
        lr3, li3, ls3, btr, bti, dar.reshape(S5_G, 1, S5_P), dai.reshape(S5_G, 1, S5_P), dbc4[:, :, 0], dbc4[:, :, 1])

    small_part = {
        "ln_gain": jnp.concatenate([jnp.zeros_like(dln1), dln1]), "mem_norm": jnp.concatenate([dgm0, dgm1]),
        "xq_norm": jnp.concatenate([dgq0, dgq1]), "xk_norm": jnp.concatenate([dgk0, dgk1]),
        "s5_lambda_re": dlr, "s5_lambda_im": dli, "s5_log_step": dls,
        "s5_b_re": jnp.swapaxes(dbtr, 1, 2), "s5_b_im": jnp.swapaxes(dbti, 1, 2),
        "s5_c_re": dcc4[:, :, 0], "s5_c_im": -dcc4[:, :, 1], "s5_d": dd,
        "mla_q_lora_norm": dgql, "mla_kv_lora_norm": dgkvl, "mla_q_nope_norm": dgqn, "mla_k_nope_norm": dgkn,
        "mla_q_rope_norm": dgqr[:, :ROPE], "mla_k_rope_norm": dgkr[:, :ROPE],
    }
    loss8 = jnp.pad(loss_part, ((0, 7), (0, 0)))
    (dx0, xn0, dproj0, dln0), (small_gath, loss_g) = _rowwise(
        "s5_in_bwd", in_bwd,
        [('r', x0), ('r', dx1), ('c', ln0), ('c', W_in_s5), ('r', du_s5), ('r', dxq_a),
         ('r', dgate_a)],
        [('r', (L, D_MODEL), F32), ('t', (D_MODEL, L), BF16), ('r', (L, 2 * BRANCH), BF16), ('a', (1, D_MODEL), F32)],
        nblk, sub, host=_plan_all_gather([_pack_small(small_part).astype(BF16), loss8]))
    dW_in_s5 = _matmul_tn_slots(xn0, dproj0, "dw_s5_in")

    late = [dW_in_s5]
    late_t = _pair_add(late, list(_exchange_call(_plan_pair(late), "rs_pair_late")), "rs_add_late")
    owners = [("w_out", 1), ("w_mem_kv", 1), ("mla_w_in", 0), ("mla_w_uq", 0), ("mla_w_ukv", 0), ("w_out", 0),
              ("w_mem_kv", 0), ("s5_w_glu", 0)]
    flipped = ("mla_w_in", "mla_w_uq")

    def shard(d, n, i):
        return jnp.transpose(d[n][i]) if n in flipped else d[n][i]

    upd, (late_recv, ln0_gath) = _updates_call(
        early_recv, [shard(weights, n, i) for n, i in owners], [shard(m_in, n, i) for n, i in owners],
        [shard(v_in, n, i) for n, i in owners], "update_early",
        host=_combine(_plan_chips(late_t), _plan_all_gather([jnp.pad(dln0, ((0, 7), (0, 0)))])))
    owners.append(("s5_w_in", 0))
    upd.append(_sum_adamw(late_recv, s5_w_in[0], m_s5_w_in[0], v_s5_w_in[0], "update_s5_w_in"))
    grads, delta, new_m, new_v = {}, {}, {}, {}
    for n in _BIG:
        parts = [u for u, (o, _) in sorted(zip(upd, owners), key=lambda t: t[1][1]) if o == n]
        if n in flipped:
            grads[n], delta[n], new_m[n], new_v[n] = (jnp.transpose(parts[0][j])[None] for j in range(4))
        else:
            grads[n], delta[n], new_m[n], new_v[n] = (jnp.stack([p[j] for p in parts]) for j in range(4))

    gs, loss_sum = _small_sum(small_gath, loss_g, ln0_gath, "small_sum")
    loss = loss_sum[0, 0]
    for n, _ in _SMALL:
        shape = weights[n].shape
        if n == "mla_q_lora_norm":
            grads[n] = lax.dynamic_slice(_unpack_small(gs, n, (Q_LORA,)), (me * 64,), (64,)).reshape(shape)
        elif n == "mla_kv_lora_norm":
            grads[n] = lax.dynamic_slice(_unpack_small(gs, n, (KV_LORA,)), (me * 32,), (32,)).reshape(shape)
        else:
            grads[n] = _unpack_small(gs, n, shape)

    def own(n, a):
        if a.ndim == 4:
            a = jnp.transpose(a, (0, 2, 3, 1))
        elif a.ndim == 3:
            a = jnp.transpose(a, (0, 2, 1))
        return a.reshape(a.shape[1:]) if a.ndim >= 3 else a

    def back(n, a):
        shape = weights[n].shape
        if len(shape) == 4:
            return jnp.transpose(a.reshape((1,) + a.shape), (0, 3, 1, 2))
        if len(shape) == 3:
            return jnp.transpose(a.reshape((1,) + a.shape), (0, 2, 1))
        return a.reshape(shape)

    wide = ("s5_b_re", "s5_b_im", "s5_c_re", "s5_c_im")
    for names, nb, call in (([n for n, _ in _SMALL if n not in wide], 1, "update_small"), (wide, 4, "update_s5_bc")):
        res = _adamw_multi([own(n, weights[n]) for n in names], [own(n, grads[n]) for n in names],
                           [own(n, m_in[n]) for n in names], [own(n, v_in[n]) for n in names], call, nb)
        for n, (dl, m2, v2) in zip(names, res):
            delta[n], new_m[n], new_v[n] = back(n, dl), back(n, m2), back(n, v2)
    return (loss, dx0[None], *[grads[n] for n in _WEIGHTS], *[delta[n] for n in _WEIGHTS],
            *[new_m[n] for n in _WEIGHTS], *[new_v[n] for n in _WEIGHTS])
```

```python
import functools
import math

import numpy as np
import jax
import jax.numpy as jnp
from jax import lax
from jax.experimental import pallas as pl
from jax.experimental.pallas import tpu as pltpu

F32 = jnp.float32
BF16 = jnp.bfloat16
EPS = 1e-6
NEG = float(np.finfo(np.float32).min)
MESH = pl.DeviceIdType.MESH

N_DEV = 8
D_MODEL = 1024
MEM_LEN = 256
XQ = 512
PRIM = 1536
BRANCH = 2048
X_HEADS = 4
HD = 128
S5_G = 96
S5_P = 64
S5_C = 16
S5_GB = 8
S5_W = S5_GB * S5_P
MLA_H = 12
ROPE = 64
Q_LORA = 512
KV_LORA = 256
ROPE_THETA = 10000.0

ADAM_LR = 0.001
ADAM_B1 = 0.9
ADAM_B2 = 0.999
ADAM_EPS = 1e-08
ADAM_WD = 0.01
ADAM_STEP = 10

VMEM_LIMIT = 56 * 1024 * 1024


def _dot(a, b):
    return jnp.dot(a, b, preferred_element_type=F32)


def _dot_nt(a, b):
    return lax.dot_general(a, b, (((1,), (1,)), ((), ())), preferred_element_type=F32)


def _dot_tn(a, b):
    return lax.dot_general(a, b, (((0,), (0,)), ((), ())), preferred_element_type=F32)


@jax.custom_vjp
def _mm(a, b):
    return _dot(a.astype(BF16), b.astype(BF16))


def _mm_fwd(a, b):
    return _mm(a, b), (a, b)


def _mm_bwd(res, g):
    a, b = res
    gb = g.astype(BF16)
    return _dot_nt(gb, b.astype(BF16)).astype(a.dtype), _dot_tn(a.astype(BF16), gb).astype(b.dtype)


_mm.defvjp(_mm_fwd, _mm_bwd)


@jax.custom_vjp
def _mm_nt(a, b):
    return _dot_nt(a.astype(BF16), b.astype(BF16))


def _mm_nt_fwd(a, b):
    return _mm_nt(a, b), (a, b)


def _mm_nt_bwd(res, g):
    a, b = res
    gb = g.astype(BF16)
    return _dot(gb, b.astype(BF16)).astype(a.dtype), _dot_tn(gb, a.astype(BF16)).astype(b.dtype)


_mm_nt.defvjp(_mm_nt_fwd, _mm_nt_bwd)


@jax.custom_vjp
def _softmax(s):
    m = jnp.max(s, axis=-1, keepdims=True)
    e = jnp.exp(s - m)
    return e / jnp.sum(e, axis=-1, keepdims=True)


def _softmax_fwd(s):
    p = _softmax(s)
    return p, p


def _softmax_bwd(p, g):
    return (p * (g - jnp.sum(p * g, axis=-1, keepdims=True)),)


_softmax.defvjp(_softmax_fwd, _softmax_bwd)


def _rms(x, g, n):
    ms = jnp.sum(x * x, axis=-1, keepdims=True) * (1.0 / n)
    return x * lax.rsqrt(ms + EPS) * g


def _sigmoid(x):
    return 1.0 / (1.0 + jnp.exp(-x))


def _silu(x):
    return x * _sigmoid(x)


def _gelu(x):
    c = math.sqrt(2.0 / math.pi)
    return 0.5 * x * (1.0 + jnp.tanh(c * (x + 0.044715 * (x * x * x))))


@jax.custom_vjp
def _rot(x, c, s1, s2):
    return x * c + pltpu.roll(x, 96, 1) * s1 + pltpu.roll(x, 32, 1) * s2


def _rot_fwd(x, c, s1, s2):
    return _rot(x, c, s1, s2), (c, s1, s2)


def _rot_bwd(res, g):
    c, s1, s2 = res
    dx = g * c + pltpu.roll(g * s1, 32, 1) + pltpu.roll(g * s2, 96, 1)
    return dx, jnp.zeros_like(c), jnp.zeros_like(s1), jnp.zeros_like(s2)


_rot.defvjp(_rot_fwd, _rot_bwd)


def _mem_attn(xq, k, v, gq):
    outs = []
    for h in range(X_HEADS):
        sl = slice(HD * h, HD * (h + 1))
        q = _rms(xq[:, sl], gq, HD)
        p = _softmax(_mm_nt(q, k[:, sl]) * (HD ** -0.5))
        outs.append(_mm(p, v[:, sl]))
    return jnp.concatenate(outs, axis=-1)


def _merge(mix, xq, gate, k, v, gq):
    return jnp.concatenate([mix, _mem_attn(xq, k, v, gq)], axis=-1) * _silu(gate)


def _q_post(q, gqn, gqr, c, s1, s2):
    pieces = []
    for h in range(MLA_H):
        pieces.append(_rms(q[:, HD * h:HD * (h + 1)], gqn, HD))
        pieces.append(_rot(_rms(q[:, PRIM + HD * h:PRIM + HD * (h + 1)], gqr, ROPE), c, s1, s2))
    return jnp.concatenate(pieces, axis=-1)


def _kv_post(kv, krp, gkn, gkr, c, s1, s2):
    kr = _rot(_rms(krp, gkr, ROPE), c, s1, s2)
    pieces, vals = [], []
    for h in range(MLA_H):
        pieces.append(_rms(kv[:, 2 * HD * h:2 * HD * h + HD], gkn, HD))
        pieces.append(kr)
        vals.append(kv[:, 2 * HD * h + HD:2 * HD * (h + 1)])
    return jnp.concatenate(pieces, axis=-1), jnp.concatenate(vals, axis=-1)


def _rowwise(name, fn, ins, outs, nblk, sub=1, host=None):
    n_in = len(ins)

    def spec(kind, shape):
        if kind == 'r':
            return pl.BlockSpec((shape[0] // nblk, shape[1]), lambda i: (i, 0))
        if kind == 't':
            return pl.BlockSpec((shape[0], shape[1] // nblk), lambda i: (0, i))
        zeros = (0,) * len(shape)
        return pl.BlockSpec(tuple(shape), lambda i: zeros)

    def body(*refs):
        i = pl.program_id(0)
        res = fn(*[r[...] for r in refs[:n_in]])
        for (kind, _, _), ref, val in zip(outs, refs[n_in:], res):
            if kind == 'a':
                @pl.when(i == 0)
                def _():
                    ref[...] = jnp.zeros_like(ref)
                ref[...] += val.astype(ref.dtype)
            elif kind == 't':
                ref[...] = val.astype(F32).T.astype(ref.dtype)
            else:
                ref[...] = val.astype(ref.dtype)

    res, hosted = _hosting_call(
        body, name, nblk, host, [a for _, a in ins], [spec(k, a.shape) for k, a in ins],
        [jax.ShapeDtypeStruct(tuple(s), d) for _, s, d in outs], [spec(k, s) for k, s, _ in outs], [])
    return res if host is None else (res, hosted)


def _matmul_tn(at, g, name, out_dtype=BF16):
    K, L = at.shape
    N = g.shape[1]
    tn = next(t for t in (512, 384, 256, 128) if N % t == 0)

    def body(a_ref, g_ref, o_ref):
        o_ref[...] = _dot(a_ref[...], g_ref[...]).astype(o_ref.dtype)

    return pl.pallas_call(
        body, name=name, grid=(N // tn,),
        in_specs=[pl.BlockSpec((K, L), lambda n: (0, 0)), pl.BlockSpec((L, tn), lambda n: (0, n))],
        out_specs=pl.BlockSpec((K, tn), lambda n: (0, n)),
        out_shape=jax.ShapeDtypeStruct((K, N), out_dtype),
        compiler_params=pltpu.CompilerParams(dimension_semantics=("arbitrary",), vmem_limit_bytes=VMEM_LIMIT),
    )(at, g)


def _matmul_tn_slots(at, g, name, host=None):
    K, L = at.shape
    n = g.shape[1] // N_DEV

    def body(a_ref, g_ref, o_ref):
        o_ref[...] = _dot(a_ref[...], g_ref[...]).astype(o_ref.dtype)

    res, hosted = _hosting_call(
        body, name, N_DEV, host, [at, g],
        [pl.BlockSpec((K, L), lambda d: (0, 0)), pl.BlockSpec((L, n), lambda d: (0, d))],
        [jax.ShapeDtypeStruct((N_DEV, K, n), BF16)], [pl.BlockSpec((None, K, n), lambda d: (d, 0, 0))], [])
    return res[0] if host is None else (res[0], hosted)


def _mm_slots(a16, w):
    return jnp.concatenate([_dot(a16, w[d]) for d in range(N_DEV)], axis=-1)


def _mm_slots_nt(g16, w):
    n = w.shape[2]
    out = _dot_nt(g16[:, 0:n], w[0])
    for d in range(1, N_DEV):
        out = out + _dot_nt(g16[:, d * n:(d + 1) * n], w[d])
    return out


class _Exchange:
    def __init__(self, ins, outs, scratch, start, finish):
        self.ins, self.outs, self.scratch, self.start, self.finish = ins, outs, scratch, start, finish


def _xyc():
    return lax.axis_index("x"), lax.axis_index("y"), lax.axis_index("c")


def _plan_all_gather(xs):
    n = len(xs)

    def build(x_refs, out_refs, sems):
        send_sems, recv_sems, local_sems = sems
        x, y, c = _xyc()

        def copies(k, block, to, own=False):
            slot = 4 * block[0] + 2 * block[1] + block[2]
            return [pltpu.make_async_remote_copy(
                src_ref=x_refs[a] if own else out_refs[a].at[slot], dst_ref=out_refs[a].at[slot],
                send_sem=send_sems.at[k * n + a], recv_sem=recv_sems.at[k * n + a], device_id=to,
                device_id_type=MESH) for a in range(n)]

        mine = [pltpu.make_async_copy(x_refs[a], out_refs[a].at[4 * x + 2 * y + c], local_sems.at[a])
                for a in range(n)]
        return copies, mine, (x, y, c), [(1 - x, y), (x, 1 - y), (1 - x, 1 - y)]

    def first_copies(copies, me, chips):
        x, y, c = me
        first = copies(0, me, (x, y, 1 - c), own=True)
        for j, chip in enumerate(chips):
            first += copies(1 + j, me, (*chip, c), own=True)
        return first

    def start(x_refs, out_refs, sems):
        copies, mine, me, chips = build(x_refs, out_refs, sems)
        for cp in mine + first_copies(copies, me, chips):
            cp.start()

    def finish(x_refs, out_refs, sems):
        copies, mine, me, chips = build(x_refs, out_refs, sems)
        x, y, c = me
        passed = []
        for j, chip in enumerate(chips):
            for cp in copies(1 + j, (*chip, c), me):
                cp.wait_recv()
            fwd = copies(4 + j, (*chip, c), (x, y, 1 - c))
            for cp in fwd:
                cp.start()
            passed += fwd
        for cp in copies(0, (x, y, 1 - c), me):
            cp.wait_recv()
        for j, chip in enumerate(chips):
            for cp in copies(4 + j, (*chip, 1 - c), me):
                cp.wait_recv()
        for cp in first_copies(copies, me, chips) + passed:
            cp.wait_send()
        for cp in mine:
            cp.wait()

    return _Exchange(list(xs), [jax.ShapeDtypeStruct((N_DEV,) + a.shape, a.dtype) for a in xs],
                     [pltpu.SemaphoreType.DMA((7 * n,)), pltpu.SemaphoreType.DMA((7 * n,)),
                      pltpu.SemaphoreType.DMA((n,))], start, finish)


_CHIPS = ((0, 0), (0, 1), (1, 0), (1, 1))


def _plan_pair(sends):
    n = len(sends)

    def build(s_refs, o_refs, sems):
        send_sems, recv_sems = sems
        x, y, c = _xyc()
        return [pltpu.make_async_remote_copy(
            src_ref=s_refs[a].at[4 * px + 2 * py + 1 - c], dst_ref=o_refs[a].at[j],
            send_sem=send_sems.at[j * n + a], recv_sem=recv_sems.at[j * n + a], device_id=(x, y, 1 - c),
            device_id_type=MESH) for j, (px, py) in enumerate(_CHIPS) for a in range(n)]

    def start(s_refs, o_refs, sems):
        for cp in build(s_refs, o_refs, sems):
            cp.start()

    def finish(s_refs, o_refs, sems):
        for cp in build(s_refs, o_refs, sems):
            cp.wait_recv()
            cp.wait_send()

    return _Exchange(list(sends), [jax.ShapeDtypeStruct((4,) + a.shape[1:], a.dtype) for a in sends],
                     [pltpu.SemaphoreType.DMA((4 * n,)), pltpu.SemaphoreType.DMA((4 * n,))], start, finish)


def _plan_chips(ts):
    n = len(ts)
    flips = ((1, 0), (0, 1), (1, 1))

    def build(t_refs, o_refs, sems):
        send_sems, recv_sems, local_sems = sems
        x, y, c = _xyc()
        mine = 2 * x + y
        local = [pltpu.make_async_copy(t_refs[a].at[mine], o_refs[a].at[mine], local_sems.at[a]) for a in range(n)]
        remote = []
        for k, (fx, fy) in enumerate(flips):
            px = 1 - x if fx else x
            py = 1 - y if fy else y
            remote += [pltpu.make_async_remote_copy(
                src_ref=t_refs[a].at[2 * px + py], dst_ref=o_refs[a].at[mine],
                send_sem=send_sems.at[k * n + a], recv_sem=recv_sems.at[k * n + a], device_id=(px, py, c),
                device_id_type=MESH) for a in range(n)]
        return local, remote

    def start(t_refs, o_refs, sems):
        local, remote = build(t_refs, o_refs, sems)
        for cp in local + remote:
            cp.start()

    def finish(t_refs, o_refs, sems):
        local, remote = build(t_refs, o_refs, sems)
        for cp in remote:
            cp.wait_recv()
        for cp in remote:
            cp.wait_send()
        for cp in local:
            cp.wait()

    return _Exchange(list(ts), [jax.ShapeDtypeStruct(a.shape, a.dtype) for a in ts],
                     [pltpu.SemaphoreType.DMA((3 * n,)), pltpu.SemaphoreType.DMA((3 * n,)),
                      pltpu.SemaphoreType.DMA((n,))], start, finish)


def _combine(*plans):
    def parts(refs, attr):
        out, at = [], 0
        for p in plans:
            n = len(getattr(p, attr))
            out.append(refs[at:at + n])
            at += n
        return out

    def run(half):
        def go(ins, outs, sems):
            for p, a, o, s in zip(plans, parts(ins, "ins"), parts(outs, "outs"), parts(sems, "scratch")):
                getattr(p, half)(a, o, s)
        return go

    return _Exchange(sum((p.ins for p in plans), []), sum((p.outs for p in plans), []),
                     sum((p.scratch for p in plans), []), run("start"), run("finish"))


def _exchange_call(plan, name):
    n = len(plan.ins)

    def body(*refs):
        ins, outs, sems = refs[:n], refs[n:2 * n], refs[2 * n:]
        plan.start(ins, outs, sems)
        plan.finish(ins, outs, sems)

    return pl.pallas_call(
        body, name=name, out_shape=plan.outs,
        in_specs=[pl.BlockSpec(memory_space=pl.ANY)] * n, out_specs=[pl.BlockSpec(memory_space=pl.ANY)] * n,
        scratch_shapes=plan.scratch,
    )(*plan.ins)


def _slab_spec(lead, rows, cols, nb):
    if rows % (nb * 16) == 0:
        return pl.BlockSpec((lead, rows // nb, cols), lambda i: (0, i, 0))
    if cols % (nb * 128) == 0:
        return pl.BlockSpec((lead, rows, cols // nb), lambda i: (0, 0, i))
    return pl.BlockSpec((lead, rows, cols), lambda i: (0, 0, 0))


def _slab_spec2(rows, cols, nb):
    if rows % (nb * 16) == 0:
        return pl.BlockSpec((rows // nb, cols), lambda i: (i, 0))
    if cols % (nb * 128) == 0:
        return pl.BlockSpec((rows, cols // nb), lambda i: (0, i))
    return pl.BlockSpec((rows, cols), lambda i: (0, 0))


def _cast_call(arrays, name, host=None):
    n = len(arrays)
    nb = 8

    def body(*refs):
        for a in range(n):
            refs[n + a][...] = refs[a][...].astype(BF16)

    specs = [_slab_spec2(x.shape[0], x.shape[1], nb) for x in arrays]
    return _hosting_call(body, name, nb, host, list(arrays), specs,
                         [jax.ShapeDtypeStruct(x.shape, BF16) for x in arrays], specs, [])


def _pair_add(sends, fromsib, name):
    n = len(sends)
    nb = 8

    def body(*refs):
        c = lax.axis_index("c")
        for a in range(n):
            s_ref, f_ref, t_ref = refs[a], refs[n + a], refs[2 * n + a]
            for j in range(4):
                t_ref[j] = (s_ref[2 * j + c].astype(F32) + f_ref[j].astype(F32)).astype(t_ref.dtype)

    def spec(a, lead):
        return _slab_spec(lead, a.shape[1], a.shape[2], nb)

    return pl.pallas_call(
        body, name=name, grid=(nb,),
        in_specs=[spec(a, N_DEV) for a in sends] + [spec(a, 4) for a in fromsib],
        out_specs=[spec(a, 4) for a in fromsib],
        out_shape=[jax.ShapeDtypeStruct(a.shape, a.dtype) for a in fromsib],
        compiler_params=pltpu.CompilerParams(dimension_semantics=("arbitrary",), vmem_limit_bytes=VMEM_LIMIT),
    )(*sends, *fromsib)


def _adamw_vals(w, g, m, v):
    m2 = ADAM_B1 * m + (1.0 - ADAM_B1) * g
    v2 = ADAM_B2 * v + (1.0 - ADAM_B2) * (g * g)
    m_hat = m2 / (1.0 - ADAM_B1 ** ADAM_STEP)
    v_hat = v2 / (1.0 - ADAM_B2 ** ADAM_STEP)
    delta = -ADAM_LR * (m_hat / (jnp.sqrt(v_hat) + ADAM_EPS) + ADAM_WD * w)
    return delta, m2, v2


def _sum_adamw(recv, w, m, v, name):
    R, C = w.shape
    ns = recv.shape[0]
    br = next((t for t in (256, 128, 64, 32, 16) if R % t == 0), R)

    def body(r_ref, w_ref, m_ref, v_ref, g_ref, d_ref, m2_ref, v2_ref):
        g = r_ref[0].astype(F32)
        for d in range(1, ns):
            g = g + r_ref[d].astype(F32)
        dl, m2, v2 = _adamw_vals(w_ref[...], g, m_ref[...], v_ref[...])
        g_ref[...] = g
        d_ref[...] = dl
        m2_ref[...] = m2
        v2_ref[...] = v2

    spec = pl.BlockSpec((br, C), lambda i: (i, 0))
    return pl.pallas_call(
        body, name=name, grid=(R // br,),
        in_specs=[pl.BlockSpec((ns, br, C), lambda i: (0, i, 0)), spec, spec, spec], out_specs=[spec] * 4,
        out_shape=[jax.ShapeDtypeStruct((R, C), F32)] * 4,
        compiler_params=pltpu.CompilerParams(dimension_semantics=("arbitrary",)),
    )(recv, w, m, v)


def _updates_call(recvs, ws, ms, vs, name, host=None):
    n = len(recvs)
    nb = 8

    def body(*refs):
        for a in range(n):
            r_ref, w_ref, m_ref, v_ref = refs[a], refs[n + a], refs[2 * n + a], refs[3 * n + a]
            g_ref, d_ref, m2_ref, v2_ref = refs[4 * n + 4 * a:4 * n + 4 * a + 4]
            g = r_ref[0].astype(F32)
            for d in range(1, r_ref.shape[0]):
                g = g + r_ref[d].astype(F32)
            dl, m2, v2 = _adamw_vals(w_ref[...], g, m_ref[...], v_ref[...])
            g_ref[...] = g
            d_ref[...] = dl
            m2_ref[...] = m2
            v2_ref[...] = v2

    def spec3(r):
        return _slab_spec(r.shape[0], r.shape[1], r.shape[2], nb)

    def spec2(w):
        return _slab_spec2(w.shape[0], w.shape[1], nb)

    res, hosted = _hosting_call(
        body, name, nb, host, list(recvs) + list(ws) + list(ms) + list(vs),
        [spec3(r) for r in recvs] + [spec2(w) for w in ws] * 3,
        [jax.ShapeDtypeStruct(w.shape, F32) for w in ws for _ in range(4)],
        [spec2(w) for w in ws for _ in range(4)], [])
    return [res[4 * a:4 * a + 4] for a in range(n)], hosted


def _small_sum(gath, loss_g, row0_g, name):
    _, R, C = gath.shape
    br = R // 3

    def body(g_ref, l_ref, r_ref, go_ref, lo_ref):
        g = g_ref[0].astype(F32)
        lsum = l_ref[0]
        for d in range(1, N_DEV):
            g = g + g_ref[d].astype(F32)
            lsum = lsum + l_ref[d]
        go_ref[...] = g
        lo_ref[...] = lsum

        @pl.when(pl.program_id(0) == 0)
        def _():
            row0 = r_ref[0]
            for d in range(1, N_DEV):
                row0 = row0 + r_ref[d]
            go_ref[0:8, :] = go_ref[0:8, :] + jnp.where(lax.broadcasted_iota(jnp.int32, row0.shape, 0) == 0, row0, 0.0)

    return pl.pallas_call(
        body, name=name, grid=(R // br,),
        in_specs=[pl.BlockSpec((N_DEV, br, C), lambda i: (0, i, 0)),
                  pl.BlockSpec((N_DEV, 8, HD), lambda i: (0, 0, 0)), pl.BlockSpec((N_DEV, 8, C), lambda i: (0, 0, 0))],
        out_specs=[pl.BlockSpec((br, C), lambda i: (i, 0)), pl.BlockSpec((8, HD), lambda i: (0, 0))],
        out_shape=[jax.ShapeDtypeStruct((R, C), F32), jax.ShapeDtypeStruct((8, HD), F32)],
        compiler_params=pltpu.CompilerParams(dimension_semantics=("arbitrary",)),
    )(gath, loss_g, row0_g)


def _adamw_multi(ws, gs, ms, vs, name, nblk=1):
    n = len(ws)

    def body(*refs):
        for a in range(n):
            dl, m2, v2 = _adamw_vals(refs[a][...], refs[n + a][...], refs[2 * n + a][...], refs[3 * n + a][...])
            refs[4 * n + 3 * a][...] = dl
            refs[4 * n + 3 * a + 1][...] = m2
            refs[4 * n + 3 * a + 2][...] = v2

    def spec(x):
        rest = (0,) * (x.ndim - 1)
        return pl.BlockSpec((x.shape[0] // nblk,) + tuple(x.shape[1:]), lambda i: (i,) + rest)

    res = pl.pallas_call(
        body, name=name, grid=(nblk,),
        in_specs=[spec(w) for w in ws] * 4, out_specs=[spec(w) for w in ws for _ in range(3)],
        out_shape=[jax.ShapeDtypeStruct(w.shape, F32) for w in ws for _ in range(3)],
        compiler_params=pltpu.CompilerParams(dimension_semantics=("arbitrary",), vmem_limit_bytes=VMEM_LIMIT),
    )(*ws, *gs, *ms, *vs)
    return [res[3 * a:3 * a + 3] for a in range(n)]


def _s5_param_fn(lr, li, ls, btr, bti):
    step = jnp.exp(ls)
    er = jnp.exp(lr * step)
    ang = li * step
    ar = er * jnp.cos(ang)
    ai = er * jnp.sin(ang)
    nr = ar - 1.0
    den = lr * lr + li * li
    fr = (nr * lr + ai * li) / den
    fi = (ai * lr - nr * li) / den
    return ar, ai, fr * btr - fi * bti, fr * bti + fi * btr


def _s5_params(lr, li, ls, btr, bti, cre, cim):
    nb = S5_G // S5_GB
    GC = S5_GB * S5_C
    expand = jnp.asarray(np.tile(np.eye(S5_P, dtype=np.float32), (1, S5_GB)), BF16)
    own = jnp.asarray((np.arange(GC)[:, None] // S5_C == np.arange(S5_W)[None, :] // S5_P).astype(np.float32))

    def body(lr_ref, li_ref, ls_ref, br_ref, bi_ref, cr_ref, ci_ref, e_ref, own_ref, ar_ref, ai_ref, bm_ref, cm_ref):
        ar, ai, bbr, bbi = _s5_param_fn(lr_ref[...], li_ref[...], ls_ref[...], br_ref[...], bi_ref[...])
        ar_ref[...] = ar
        ai_ref[...] = ai

        def plane(x, n):
            rows = x[n * S5_GB:(n + 1) * S5_GB].reshape(GC, S5_P).astype(BF16)
            return _dot(rows, e_ref[...]) * own_ref[...]

        for n in range(nb):
            bm_ref[n] = jnp.concatenate([plane(bbr, n), plane(bbi, n)], axis=-1).astype(BF16)
            cm_ref[n] = jnp.concatenate([plane(cr_ref[...], n), -plane(ci_ref[...], n)], axis=-1).astype(BF16)

    sd = jax.ShapeDtypeStruct
    return pl.pallas_call(
        body, name="s5_params",
        out_shape=[sd(lr.shape, F32), sd(lr.shape, F32), sd((nb, GC, 2 * S5_W), BF16), sd((nb, GC, 2 * S5_W), BF16)],
        compiler_params=pltpu.CompilerParams(vmem_limit_bytes=VMEM_LIMIT),
    )(lr, li, ls, btr, bti, cre, cim, expand, own)


def _s5_params_bwd(lr, li, ls, btr, bti, dar, dai, dbbr, dbbi):
    def body(lr_ref, li_ref, ls_ref, br_ref, bi_ref, dar_ref, dai_ref, dbbr_ref, dbbi_ref,
             dlr_ref, dli_ref, dls_ref, dbr_ref, dbi_ref):
        _, vjp = jax.vjp(_s5_param_fn, lr_ref[...], li_ref[...], ls_ref[...], br_ref[...], bi_ref[...])
        dlr, dli, dls, dbr, dbi = vjp((dar_ref[...], dai_ref[...], dbbr_ref[...], dbbi_ref[...]))
        dlr_ref[...] = dlr
        dli_ref[...] = dli
        dls_ref[...] = dls
        dbr_ref[...] = dbr
        dbi_ref[...] = dbi

    sd = jax.ShapeDtypeStruct
    return pl.pallas_call(
        body, name="s5_params_bwd",
        out_shape=[sd(lr.shape, F32), sd(lr.shape, F32), sd(ls.shape, F32), sd(btr.shape, F32), sd(btr.shape, F32)],
    )(lr, li, ls, btr, bti, dar, dai, dbbr, dbbi)


def _cpow(ar, ai, n):
    assert n & (n - 1) == 0
    while n > 1:
        ar, ai = ar * ar - ai * ai, 2.0 * ar * ai
        n //= 2
    return ar, ai


def _scan(st, cr, ci, init, nk, reverse, store, prev=None):
    W = S5_W

    def step(j, carry):
        k = nk - 1 - j if reverse else j
        rows = pl.ds(pl.multiple_of(k * 8, 8), 8)
        sr, si = carry[0], carry[1]
        nsr = cr * sr - ci * si + st[rows, 0:W]
        nsi = cr * si + ci * sr + st[rows, W:2 * W]
        if store:
            st[rows, 0:W] = nsr
            st[rows, W:2 * W] = nsi
        if prev is None:
            return nsr, nsi
        prows = pl.ds(pl.multiple_of(jnp.maximum(k - 1, 0) * 8, 8), 8)
        w = jnp.where(k > 0, 1.0, 0.0).astype(F32)
        pr = prev[prows, 0:W] * w
        pi = prev[prows, W:2 * W] * w
        return nsr, nsi, carry[2] + nsr * pr + nsi * pi, carry[3] + nsi * pr - nsr * pi

    return lax.fori_loop(0, nk, step, init, unroll=2)


def _chain(fin, fr, fi, pr, pi, reverse):
    W = S5_W
    fin[:, 0:W] = fr
    fin[:, W:2 * W] = fi
    rowid = lax.broadcasted_iota(jnp.int32, (8, W), 0)
    cr = jnp.zeros((1, W), F32)
    ci = jnp.zeros((1, W), F32)
    init_r = jnp.zeros((8, W), F32)
    init_i = jnp.zeros((8, W), F32)
    for s in (range(7, -1, -1) if reverse else range(8)):
        init_r = jnp.where(rowid == s, cr, init_r)
        init_i = jnp.where(rowid == s, ci, init_i)
        lr = fin[s:s + 1, 0:W]
        li = fin[s:s + 1, W:2 * W]
        cr, ci = lr + pr * cr - pi * ci, li + pr * ci + pi * cr
    return init_r, init_i


def _full_scan(st, fin, ar, ai, nk, reverse, prev=None, carry_in=None, carry_out=None):
    W = S5_W
    cr = jnp.broadcast_to(ar, (8, W))
    ci = jnp.broadcast_to(-ai if reverse else ai, (8, W))
    z = jnp.zeros((8, W), F32)
    if carry_in is None:
        fr, fi = _scan(st, cr, ci, (z, z), nk, reverse, store=False)
        pr, pi = _cpow(ar, -ai if reverse else ai, nk)
        init = _chain(fin, fr, fi, pr, pi, reverse)
    else:
        init = (carry_in[:, 0:W], carry_in[:, W:2 * W])
    if carry_out is not None:
        carry_out[:, 0:W] = init[0]
        carry_out[:, W:2 * W] = init[1]
    if prev is None:
        return _scan(st, cr, ci, init, nk, reverse, store=True)
    return _scan(st, cr, ci, init + (z, z), nk, reverse, store=True, prev=prev)


def _s5_specs(L):
    W2 = 2 * S5_W
    GC = S5_GB * S5_C
    col = pl.BlockSpec((L, GC), lambda g: (0, g))
    vec = pl.BlockSpec((1, GC), lambda g: (0, g))
    avec = pl.BlockSpec((1, S5_W), lambda g: (0, g))
    bmat = pl.BlockSpec((None, GC, W2), lambda g: (g, 0, 0))
    cmat = pl.BlockSpec((None, W2, GC), lambda g: (g, 0, 0))
    return col, vec, avec, bmat, cmat


def _interleave(dst, src, nk):
    for s in range(8):
        dst[pl.ds(s, nk, stride=8), :] = src[s * nk:(s + 1) * nk, :]


def _deinterleave(dst, src, nk):
    for s in range(8):
        dst[s * nk:(s + 1) * nk, :] = src[pl.ds(s, nk, stride=8), :].astype(dst.dtype)


def _hosting_call(body, name, nsteps, host, ins, in_specs, outs, out_specs, scratch):
    grid = (nsteps,) if isinstance(nsteps, int) else tuple(nsteps)
    params = pltpu.CompilerParams(dimension_semantics=("arbitrary",) * len(grid), vmem_limit_bytes=VMEM_LIMIT)
    if host is None:
        res = pl.pallas_call(
            body, name=name, grid=grid, in_specs=in_specs, out_specs=out_specs, out_shape=outs,
            scratch_shapes=scratch, compiler_params=params,
        )(*ins)
        return list(res), []
    n_in, n_out, n_sc = len(ins), len(outs), len(scratch)
    h_in, h_out = len(host.ins), len(host.outs)

    def hosted(*refs):
        a = refs[:n_in]
        ha = refs[n_in:n_in + h_in]
        o = refs[n_in + h_in:n_in + h_in + n_out]
        ho = refs[n_in + h_in + n_out:n_in + h_in + n_out + h_out]
        sc = refs[n_in + h_in + n_out + h_out:n_in + h_in + n_out + h_out + n_sc]
        hs = refs[n_in + h_in + n_out + h_out + n_sc:]
        first = functools.reduce(jnp.logical_and, [pl.program_id(i) == 0 for i in range(len(grid))])
        last = functools.reduce(jnp.logical_and, [pl.program_id(i) == g - 1 for i, g in enumerate(grid)])

        @pl.when(first)
        def _():
            host.start(ha, ho, hs)

        body(*a, *o, *sc)

        @pl.when(last)
        def _():
            host.finish(ha, ho, hs)

    hbm = pl.BlockSpec(memory_space=pl.ANY)
    res = pl.pallas_call(
        hosted, name=name, grid=grid,
        in_specs=list(in_specs) + [hbm] * h_in, out_specs=list(out_specs) + [hbm] * h_out,
        out_shape=list(outs) + list(host.outs), scratch_shapes=list(scratch) + list(host.scratch),
        compiler_params=params,
    )(*ins, *host.ins)
    return list(res[:n_out]), list(res[n_out:])


def _s5_fwd(u, bm, cm, ar, ai, dvec, host=None):
    L = u.shape[0]
    nk = L // 8
    GC = S5_GB * S5_C
    nb = S5_G // S5_GB
    col, vec, avec, bmat, cmat = _s5_specs(L)

    def body(u_ref, b_ref, c_ref, ar_ref, ai_ref, d_ref, y_ref, carry_ref, st, fin, ui, yi):
        _interleave(ui, u_ref, nk)
        for r in range(8):
            rows = slice(r * nk, (r + 1) * nk)
            st[rows, :] = _dot(ui[rows, :].astype(BF16), b_ref[...])
        _full_scan(st, fin, ar_ref[...], ai_ref[...], nk, reverse=False, carry_out=carry_ref)
        for r in range(8):
            rows = slice(r * nk, (r + 1) * nk)
            yi[rows, :] = _dot_nt(st[rows, :].astype(BF16), c_ref[...]) + d_ref[...] * ui[rows, :]
        _deinterleave(y_ref, yi, nk)

    return _hosting_call(
        body, "s5_fwd", nb, host,
        [u, bm, cm, ar, ai, dvec], [col, bmat, bmat, avec, avec, vec],
        [jax.ShapeDtypeStruct(u.shape, F32), jax.ShapeDtypeStruct((nb * 8, 2 * S5_W), F32)],
        [col, pl.BlockSpec((8, 2 * S5_W), lambda g: (g, 0))],
        [pltpu.VMEM((L, 2 * S5_W), F32), pltpu.VMEM((8, 2 * S5_W), F32), pltpu.VMEM((L, GC), F32),
         pltpu.VMEM((L, GC), F32)])


def _s5_bwd(u, dy, carry, bm, cm, ar, ai, dvec, mask, rmat, host=None):
    L = u.shape[0]
    nk = L // 8
    W = S5_W
    GC = S5_GB * S5_C
    col, vec, avec, bmat, cmat = _s5_specs(L)
    hi = lax.Precision.HIGHEST

    def body(u_ref, dy_ref, carry_ref, b_ref, ct_ref, ar_ref, ai_ref, d_ref, mask_ref, r_ref,
             du_ref, db_ref, dc_ref, dd_ref, dar_ref, dai_ref, sa, sb, fin, ui, dyi, dui):
        ar = ar_ref[...]
        ai = ai_ref[...]
        _interleave(ui, u_ref, nk)
        _interleave(dyi, dy_ref, nk)
        for r in range(8):
            rows = slice(r * nk, (r + 1) * nk)
            sa[rows, :] = _dot(ui[rows, :].astype(BF16), b_ref[...])
            sb[rows, :] = _dot(dyi[rows, :].astype(BF16), ct_ref[...])
        _full_scan(sa, fin, ar, ai, nk, reverse=False, carry_in=carry_ref)
        gr, gi, accr, acci = _full_scan(sb, fin, ar, ai, nk, reverse=True, prev=sa)
        rowid = lax.broadcasted_iota(jnp.int32, (8, W), 0)
        last = pl.ds((nk - 1) * 8, 8)
        pr = jnp.where(rowid == 0, 0.0, pltpu.roll(sa[last, 0:W], 1, 0))
        pi = jnp.where(rowid == 0, 0.0, pltpu.roll(sa[last, W:2 * W], 1, 0))
        accr = accr + gr * pr + gi * pi
        acci = acci + gi * pr - gr * pi
        dar_ref[...] = jnp.sum(accr, axis=0, keepdims=True)
        dai_ref[...] = jnp.sum(acci, axis=0, keepdims=True)
        dbf = jnp.zeros((GC, 2 * W), F32)
        dcf = jnp.zeros((GC, 2 * W), F32)
        dd = jnp.zeros((1, GC), F32)
        for r in range(8):
            rows = slice(r * nk, (r + 1) * nk)
            ub = ui[rows, :]
            dyb = dyi[rows, :]
            gb = sb[rows, :].astype(BF16)
            dui[rows, :] = _dot_nt(gb, b_ref[...]) + d_ref[...] * dyb
            dbf = dbf + _dot_tn(ub.astype(BF16), gb)
            dcf = dcf + _dot_tn(dyb.astype(BF16), sa[rows, :].astype(BF16))
            dd = dd + jnp.sum(dyb * ub, axis=0, keepdims=True)
        db_ref[...] = jnp.dot(dbf * mask_ref[...], r_ref[...], precision=hi, preferred_element_type=F32)
        dc_ref[...] = jnp.dot(dcf * mask_ref[...], r_ref[...], precision=hi, preferred_element_type=F32)
        dd_ref[...] = dd
        _deinterleave(du_ref, dui, nk)

    cmp_spec = pl.BlockSpec((GC, 2 * S5_P), lambda g: (g, 0))
    whole = lambda shape: pl.BlockSpec(shape, lambda g: (0, 0))
    sd = jax.ShapeDtypeStruct
    return _hosting_call(
        body, "s5_bwd", S5_G // S5_GB, host,
        [u, dy, carry, bm, cm, ar, ai, dvec, mask, rmat],
        [col, col, pl.BlockSpec((8, 2 * W), lambda g: (g, 0)), bmat, bmat, avec, avec, vec, whole(mask.shape),
         whole(rmat.shape)],
        [sd(u.shape, BF16), sd((S5_G * S5_C, 2 * S5_P), F32), sd((S5_G * S5_C, 2 * S5_P), F32),
         sd((1, PRIM), F32), sd((1, S5_G * S5_P), F32), sd((1, S5_G * S5_P), F32)],
        [col, cmp_spec, cmp_spec, vec, avec, avec],
        [pltpu.VMEM((L, 2 * W), F32), pltpu.VMEM((L, 2 * W), F32), pltpu.VMEM((8, 2 * W), F32),
         pltpu.VMEM((L, GC), F32), pltpu.VMEM((L, GC), F32), pltpu.VMEM((L, GC), F32)])


def _s5_compact_consts():
    g_row = np.arange(S5_GB * S5_C) // S5_C
    col = np.arange(2 * S5_W)
    g_col = (col % S5_W) // S5_P
    mask = (g_row[:, None] == g_col[None, :]).astype(np.float32)
    tgt = (col // S5_W) * S5_P + col % S5_P
    rmat = (tgt[:, None] == np.arange(2 * S5_P)[None, :]).astype(np.float32)
    return jnp.asarray(mask), jnp.asarray(rmat)


def _attn_scores(q_ref, k_ref, qb, bq, scale):
    ext = (qb + 1) * bq
    s = _dot_nt(q_ref[qb * bq:ext, :], k_ref[0:ext, :]) * scale
    qpos = lax.broadcasted_iota(jnp.int32, (bq, bq), 0)
    kpos = lax.broadcasted_iota(jnp.int32, (bq, bq), 1)
    diag = jnp.where(kpos <= qpos, s[:, ext - bq:], NEG)
    return diag if qb == 0 else jnp.concatenate([s[:, :ext - bq], diag], axis=-1)


def _attn_fwd(qp, kp, v, scale):
    L = qp.shape[0]
    bq = min(256, L)

    def body(q_ref, k_ref, v_ref, o_ref, lse_ref):
        for qb in range(L // bq):
            rows = slice(qb * bq, (qb + 1) * bq)
            s = _attn_scores(q_ref, k_ref, qb, bq, scale)
            m = jnp.max(s, axis=-1, keepdims=True)
            e = jnp.exp(s - m)
            l = jnp.sum(e, axis=-1, keepdims=True)
            o_ref[rows, :] = _dot(e.astype(BF16), v_ref[0:(qb + 1) * bq, :]) / l
            lse_ref[rows, :] = jnp.broadcast_to(m + jnp.log(l), (bq, HD))

    blk = pl.BlockSpec((L, HD), lambda h: (0, h))
    wide = pl.BlockSpec((L, 2 * HD), lambda h: (0, h))
    return pl.pallas_call(
        body, name="mla_attn_fwd", grid=(MLA_H,),
        in_specs=[wide, wide, blk], out_specs=[blk, blk],
        out_shape=[jax.ShapeDtypeStruct((L, MLA_H * HD), F32)] * 2,
        compiler_params=pltpu.CompilerParams(dimension_semantics=("arbitrary",), vmem_limit_bytes=VMEM_LIMIT),
    )(qp, kp, v)


def _attn_bwd(qp, kp, v, o, lse, do, scale):
    L = qp.shape[0]
    bq = min(256, L)
    nq = L // bq

    def body(q_ref, k_ref, v_ref, o_ref, lse_ref, do_ref, dq_ref, dk_ref, dv_ref, dk_acc, dv_acc):
        dk_acc[...] = jnp.zeros_like(dk_acc)
        dv_acc[...] = jnp.zeros_like(dv_acc)
        for qb in range(nq):
            rows = slice(qb * bq, (qb + 1) * bq)
            ext = (qb + 1) * bq
            do = do_ref[rows, :]
            dob = do.astype(BF16)
            p = jnp.exp(_attn_scores(q_ref, k_ref, qb, bq, scale) - lse_ref[rows, 0:1])
            dp = _dot_nt(dob, v_ref[0:ext, :])
            dsum = jnp.sum(do * o_ref[rows, :], axis=-1, keepdims=True)
            ds = (p * (dp - dsum) * scale).astype(BF16)
            dq_ref[rows, :] = _dot(ds, k_ref[0:ext, :]).astype(dq_ref.dtype)
            dk_acc[0:ext, :] += _dot_tn(ds, q_ref[rows, :])
            dv_acc[0:ext, :] += _dot_tn(p.astype(BF16), dob)
        dk_ref[...] = dk_acc[...].astype(dk_ref.dtype)
        dv_ref[...] = dv_acc[...].astype(dv_ref.dtype)

    sd = jax.ShapeDtypeStruct
    blk = pl.BlockSpec((L, HD), lambda h: (0, h))
    wide = pl.BlockSpec((L, 2 * HD), lambda h: (0, h))
    return pl.pallas_call(
        body, name="mla_attn_bwd", grid=(MLA_H,),
        in_specs=[wide, wide, blk, blk, blk, blk], out_specs=[wide, wide, blk],
        out_shape=[sd((L, MLA_H * 2 * HD), BF16), sd((L, MLA_H * 2 * HD), BF16), sd((L, MLA_H * HD), BF16)],
        scratch_shapes=[pltpu.VMEM((L, 2 * HD), F32), pltpu.VMEM((L, HD), F32)],
        compiler_params=pltpu.CompilerParams(dimension_semantics=("arbitrary",), vmem_limit_bytes=VMEM_LIMIT),
    )(qp, kp, v, o, lse, do)


def _kv_fn(mem, gm, w, gk):
    kv = _mm(_rms(mem, gm, D_MODEL), w)
    k = jnp.concatenate([_rms(kv[:, HD * h:HD * (h + 1)], gk, HD) for h in range(X_HEADS)], axis=-1)
    return k, kv[:, XQ:]


def _kv_prep(mem, gm, w, gk, name):
    def fn(mem, gm, w, gk):
        return _kv_fn(mem, gm, w, gk)
    M = mem.shape[0]
    return _rowwise(name, fn, [('c', mem), ('c', gm), ('c', w), ('c', gk)],
                    [('c', (M, XQ), F32), ('c', (M, XQ), F32)], 1)


def _kv_prep_bwd(mem, gm, w, gk, dk, dv, name):
    def fn(mem, gm, w, gk, dk, dv):
        _, vjp = jax.vjp(lambda a, b, c: _kv_fn(mem, a, b, c), gm, w, gk)
        return vjp((dk, dv))
    return _rowwise(name, fn, [('c', mem), ('c', gm), ('c', w), ('c', gk), ('c', dk), ('c', dv)],
                    [('c', gm.shape, F32), ('c', w.shape, BF16), ('c', gk.shape, F32)], 1)


def _forward_merge(x, mix, mix_kind, xq, gate, k, v, gq, wout, name, nblk, sub, host=None):
    def fn(x, mix, xq, gate, k, v, gq, wout):
        o = _merge(mix, xq, gate, k, v, gq)
        return (x + _dot(o.astype(BF16), wout),)
    L = x.shape[0]
    out = _rowwise(name, fn, [('r', x), (mix_kind, mix), ('r', xq), ('r', gate), ('c', k), ('c', v), ('c', gq),
                              ('c', wout)], [('r', (L, D_MODEL), F32)], nblk, sub, host=host)
    return out[0] if host is None else (out[0][0], out[1])


def _backward_merge(dx, mix, mix_kind, xq, gate, k, v, gq, wout, name, nblk, sub, host=None):
    def fn(dx, mix, xq, gate, k, v, gq, wout):
        g16 = dx.astype(BF16)
        do = _dot_nt(g16, wout)
        o, vjp = jax.vjp(_merge, mix, xq, gate, k, v, gq)
        dmix, dxq, dgate, dk, dv, dgq = vjp(do)
        return dmix, dxq, dgate, o, g16, dk, dv, dgq
    L = dx.shape[0]
    return _rowwise(
        name, fn,
        [('r', dx), (mix_kind, mix), ('r', xq), ('r', gate), ('c', k), ('c', v), ('c', gq), ('c', wout)],
        [('r', (L, PRIM), F32), ('r', (L, XQ), BF16), ('r', (L, BRANCH), BF16), ('t', (BRANCH, L), BF16),
         ('r', (L, D_MODEL), BF16), ('a', k.shape, F32), ('a', v.shape, F32), ('a', gq.shape, F32)], nblk, sub,
        host=host)


_MLA_IN = 3392
_MLA_IN_PAD = 3456


def _uq_rows(wt):
    r = wt.reshape(MLA_H, HD + ROPE, wt.shape[1])
    return jnp.concatenate([r[:, :HD].reshape(PRIM, -1),
                            jnp.pad(r[:, HD:], ((0, 0), (0, HD - ROPE), (0, 0))).reshape(PRIM, -1)], axis=0)


def _uq_rows_back(wt):
    nope = wt[:PRIM].reshape(MLA_H, HD, -1)
    rope = wt[PRIM:].reshape(MLA_H, HD, -1)[:, :ROPE]
    return jnp.concatenate([nope, rope], axis=1).reshape(MLA_H * (HD + ROPE), -1)


def _mla_in_rows(wt):
    return jnp.concatenate([wt[:768], wt[832:], wt[768:832], jnp.zeros((64, wt.shape[1]), wt.dtype)], axis=0)


def _mla_in_rows_back(wt):
    return jnp.concatenate([wt[:768], wt[3328:3392], wt[768:3328]], axis=0)


_SMALL = (("ln_gain", 2048), ("mem_norm", 2048), ("xq_norm", 256), ("xk_norm", 256), ("s5_lambda_re", 6144),
          ("s5_lambda_im", 6144), ("s5_log_step", 96), ("s5_b_re", 98304), ("s5_b_im", 98304), ("s5_c_re", 98304),
          ("s5_c_im", 98304), ("s5_d", 1536), ("mla_q_lora_norm", 512), ("mla_kv_lora_norm", 256),
          ("mla_q_nope_norm", 128), ("mla_k_nope_norm", 128), ("mla_q_rope_norm", 64), ("mla_k_rope_norm", 64))
_SMALL_ROWS = 432
_SMALL_OFF = {name: sum(n for _, n in _SMALL[:i]) for i, (name, _) in enumerate(_SMALL)}


def _pack_small(d):
    flat = jnp.concatenate([d[n].reshape(-1).astype(F32) for n, _ in _SMALL])
    return jnp.pad(flat, (0, _SMALL_ROWS * 1024 - flat.shape[0])).reshape(_SMALL_ROWS, 1024)


def _unpack_small(p, name, shape):
    off = _SMALL_OFF[name]
    return p.reshape(-1)[off:off + int(np.prod(shape))].reshape(shape)


_WEIGHTS = ('ln_gain', 'w_out', 'mem_norm', 'w_mem_kv', 'xq_norm', 'xk_norm', 's5_w_in', 's5_lambda_re',
            's5_lambda_im', 's5_log_step', 's5_b_re', 's5_b_im', 's5_c_re', 's5_c_im', 's5_d', 's5_w_glu', 'mla_w_in',
            'mla_q_lora_norm', 'mla_kv_lora_norm', 'mla_w_uq', 'mla_w_ukv', 'mla_q_nope_norm', 'mla_k_nope_norm',
            'mla_q_rope_norm', 'mla_k_rope_norm')
_BIG = ('w_out', 'w_mem_kv', 's5_w_in', 's5_w_glu', 'mla_w_in', 'mla_w_uq', 'mla_w_ukv')


def _pad128(g):
    return jnp.pad(g.reshape(1, -1), ((0, 0), (0, HD - g.shape[-1])))


def kernel(x, mem, positions, ln_gain, w_out, mem_norm, w_mem_kv, xq_norm, xk_norm, s5_w_in, s5_lambda_re, s5_lambda_im, s5_log_step, s5_b_re, s5_b_im, s5_c_re, s5_c_im, s5_d, s5_w_glu, mla_w_in, mla_q_lora_norm, mla_kv_lora_norm, mla_w_uq, mla_w_ukv, mla_q_nope_norm, mla_k_nope_norm, mla_q_rope_norm, mla_k_rope_norm, loss_target, m_ln_gain, m_w_out, m_mem_norm, m_w_mem_kv, m_xq_norm, m_xk_norm, m_s5_w_in, m_s5_lambda_re, m_s5_lambda_im, m_s5_log_step, m_s5_b_re, m_s5_b_im, m_s5_c_re, m_s5_c_im, m_s5_d, m_s5_w_glu, m_mla_w_in, m_mla_q_lora_norm, m_mla_kv_lora_norm, m_mla_w_uq, m_mla_w_ukv, m_mla_q_nope_norm, m_mla_k_nope_norm, m_mla_q_rope_norm, m_mla_k_rope_norm, v_ln_gain, v_w_out, v_mem_norm, v_w_mem_kv, v_xq_norm, v_xk_norm, v_s5_w_in, v_s5_lambda_re, v_s5_lambda_im, v_s5_log_step, v_s5_b_re, v_s5_b_im, v_s5_c_re, v_s5_c_im, v_s5_d, v_s5_w_glu, v_mla_w_in, v_mla_q_lora_norm, v_mla_kv_lora_norm, v_mla_w_uq, v_mla_w_ukv, v_mla_q_nope_norm, v_mla_k_nope_norm, v_mla_q_rope_norm, v_mla_k_rope_norm):
    weights = dict(ln_gain=ln_gain, w_out=w_out, mem_norm=mem_norm, w_mem_kv=w_mem_kv, xq_norm=xq_norm,
                   xk_norm=xk_norm, s5_w_in=s5_w_in, s5_lambda_re=s5_lambda_re, s5_lambda_im=s5_lambda_im,
                   s5_log_step=s5_log_step, s5_b_re=s5_b_re, s5_b_im=s5_b_im, s5_c_re=s5_c_re, s5_c_im=s5_c_im,
                   s5_d=s5_d, s5_w_glu=s5_w_glu, mla_w_in=mla_w_in, mla_q_lora_norm=mla_q_lora_norm,
                   mla_kv_lora_norm=mla_kv_lora_norm, mla_w_uq=mla_w_uq, mla_w_ukv=mla_w_ukv,
                   mla_q_nope_norm=mla_q_nope_norm, mla_k_nope_norm=mla_k_nope_norm,
                   mla_q_rope_norm=mla_q_rope_norm, mla_k_rope_norm=mla_k_rope_norm)
    m_in = dict(zip(_WEIGHTS, (m_ln_gain, m_w_out, m_mem_norm, m_w_mem_kv, m_xq_norm, m_xk_norm, m_s5_w_in,
                               m_s5_lambda_re, m_s5_lambda_im, m_s5_log_step, m_s5_b_re, m_s5_b_im, m_s5_c_re,
                               m_s5_c_im, m_s5_d, m_s5_w_glu, m_mla_w_in, m_mla_q_lora_norm, m_mla_kv_lora_norm,
                               m_mla_w_uq, m_mla_w_ukv, m_mla_q_nope_norm, m_mla_k_nope_norm, m_mla_q_rope_norm,
                               m_mla_k_rope_norm)))
    v_in = dict(zip(_WEIGHTS, (v_ln_gain, v_w_out, v_mem_norm, v_w_mem_kv, v_xq_norm, v_xk_norm, v_s5_w_in,
                               v_s5_lambda_re, v_s5_lambda_im, v_s5_log_step, v_s5_b_re, v_s5_b_im, v_s5_c_re,
                               v_s5_c_im, v_s5_d, v_s5_w_glu, v_mla_w_in, v_mla_q_lora_norm, v_mla_kv_lora_norm,
                               v_mla_w_uq, v_mla_w_ukv, v_mla_q_nope_norm, v_mla_k_nope_norm, v_mla_q_rope_norm,
                               v_mla_k_rope_norm)))

    x0 = x[0]
    mem0 = mem[0]
    target = loss_target[0]
    L = x0.shape[0]
    nblk, sub = 8, 1
    me = 4 * lax.axis_index("x") + 2 * lax.axis_index("y") + lax.axis_index("c")

    lora = jnp.pad(jnp.concatenate([mla_q_lora_norm, mla_kv_lora_norm], axis=1), ((0, 7), (0, HD - 96)))
    def gather(*shards):
        return _plan_all_gather(list(shards))

    kh = D_MODEL // 2
    (b_mkv0, b_glu, b_in_mla, b_out0, b_uq, b_ukv, b_mkv1, b_out1), (W_in_s5,) = _cast_call(
        [w_mem_kv[0], s5_w_glu[0], jnp.transpose(mla_w_in[0]), w_out[0], jnp.transpose(mla_w_uq[0]), mla_w_ukv[0],
         w_mem_kv[1], w_out[1]], "cast_shards", host=gather(s5_w_in[0].astype(BF16)))

    ln0, ln1 = ln_gain[0:1], ln_gain[1:2]
    gq0, gq1 = xq_norm[0:1], xq_norm[1:2]
    gk0, gk1 = xk_norm[0:1], xk_norm[1:2]
    gm0, gm1 = mem_norm[0:1], mem_norm[1:2]
    gqn, gkn = mla_q_nope_norm, mla_k_nope_norm
    gqr, gkr = _pad128(mla_q_rope_norm), _pad128(mla_k_rope_norm)

    lr3 = s5_lambda_re.reshape(S5_G, 1, S5_P)
    li3 = s5_lambda_im.reshape(S5_G, 1, S5_P)
    ls3 = s5_log_step.reshape(S5_G, 1, 1)
    btr = jnp.swapaxes(s5_b_re[0], 1, 2)
    bti = jnp.swapaxes(s5_b_im[0], 1, 2)
    a_r, a_i, bm, cm = _s5_params(lr3, li3, ls3, btr, bti, s5_c_re[0], s5_c_im[0])
    a_r2 = a_r.reshape(1, S5_G * S5_P)
    a_i2 = a_i.reshape(1, S5_G * S5_P)
    cmask, rmat = _s5_compact_consts()

    half = ROPE // 2
    inv_freq = ROPE_THETA ** (-jnp.arange(half, dtype=F32) / half)
    invf = jnp.concatenate([inv_freq, inv_freq, jnp.zeros((HD - ROPE,), F32)]).reshape(1, HD)

    def rot_tables(pos, invf):
        ang = pos.astype(F32) * invf
        lane = lax.broadcasted_iota(jnp.int32, ang.shape, 1)
        c = jnp.where(lane < ROPE, jnp.cos(ang), 0.0)
        s = jnp.sin(ang)
        return c, jnp.where(lane < half, -s, 0.0), jnp.where((lane >= half) & (lane < ROPE), s, 0.0)

    tc, ts1, ts2 = _rowwise("rot_tables", rot_tables, [('r', positions.reshape(L, 1)), ('c', invf)],
                            [('r', (L, HD), F32)] * 3, nblk, sub)

    def in_s5(x, g, w):
        proj = _mm_slots(_rms(x, g, D_MODEL).astype(BF16), w)
        return proj[:, :PRIM], proj[:, PRIM:PRIM + XQ], proj[:, PRIM + XQ:]

    (u_s5, xq_a, gate_a), (G_mkv0,) = _rowwise(
        "s5_in", in_s5, [('r', x0), ('c', ln0), ('c', W_in_s5)],
        [('r', (L, PRIM), F32), ('r', (L, XQ), F32), ('r', (L, BRANCH), F32)], nblk, sub, host=gather(b_mkv0))
    (y_s5, s5_carry), (W_glu, G_in_mla_a) = _s5_fwd(u_s5, bm, cm, a_r2, a_i2, s5_d,
                                                    host=gather(b_glu, b_in_mla[:, :kh]))

    def glu(y, w):
        z = _mm_slots(_gelu(y).astype(BF16), w)
        return (z[:, :PRIM] * _sigmoid(z[:, PRIM:]),)

    (y2,), (G_out0,) = _rowwise("s5_glu", glu, [('r', y_s5), ('c', W_glu)], [('r', (L, PRIM), F32)], nblk, sub,
                                host=gather(b_out0))
    W_mkv0 = G_mkv0.reshape(D_MODEL, 2 * XQ)
    k_a, v_a = _kv_prep(mem0, gm0, W_mkv0, gk0, "kv_prep0")
    x1, (G_in_mla_b,) = _forward_merge(
        x0, y2, 'r', xq_a, gate_a, k_a, v_a, gq0, G_out0.reshape(BRANCH, D_MODEL), "merge0", nblk, sub,
        host=gather(b_in_mla[:, kh:]))
    W_in_mla = _mla_in_rows(jnp.concatenate([G_in_mla_a, G_in_mla_b], axis=2).reshape(_MLA_IN, D_MODEL))

    def in_mla(x, g, w):
        proj = _dot_nt(_rms(x, g, D_MODEL).astype(BF16), w)
        return proj[:, :512], proj[:, 512:768], proj[:, 768:1280], proj[:, 1280:3328], proj[:, 3328:]

    (c_q, c_kv, xq_b, gate_b, krp), (G_uq, W_kv, G_lora) = _rowwise(
        "mla_in", in_mla, [('r', x1), ('c', ln1), ('c', W_in_mla)],
        [('r', (L, Q_LORA), F32), ('r', (L, KV_LORA), F32), ('r', (L, XQ), F32), ('r', (L, BRANCH), F32),
         ('r', (L, HD), F32)], nblk, sub,
        host=gather(b_uq, b_ukv, lora))
    W_q = _uq_rows(G_uq.reshape(MLA_H * (HD + ROPE), Q_LORA))
    g_qlora = G_lora[:, 0, :64].reshape(1, Q_LORA)
    g_kvlora = G_lora[:, 0, 64:96].reshape(1, KV_LORA)

    def qkv(c_q, c_kv, krp, tc, ts1, ts2, gql, gkvl, wq, wkv, gqn, gkn, gqr, gkr):
        q = _dot_nt(_rms(c_q, gql, Q_LORA).astype(BF16), wq)
        kv = _mm_slots(_rms(c_kv, gkvl, KV_LORA).astype(BF16), wkv)
        kp, v = _kv_post(kv, krp, gkn, gkr, tc, ts1, ts2)
        return _q_post(q, gqn, gqr, tc, ts1, ts2), kp, v

    qkv_consts = [('c', g_qlora), ('c', g_kvlora), ('c', W_q), ('c', W_kv), ('c', gqn), ('c', gkn), ('c', gqr),
                  ('c', gkr)]
    (q_pad, k_pad, v_h), (G_mkv1, G_out1) = _rowwise(
        "mla_qkv", qkv, [('r', c_q), ('r', c_kv), ('r', krp), ('r', tc), ('r', ts1), ('r', ts2)] + qkv_consts,
        [('r', (L, 2 * PRIM), BF16), ('r', (L, 2 * PRIM), BF16), ('r', (L, PRIM), BF16)], nblk, sub,
        host=gather(b_mkv1, b_out1))
    W_out = (G_out0.reshape(BRANCH, D_MODEL), G_out1.reshape(BRANCH, D_MODEL))
    W_mkv = (W_mkv0, G_mkv1.reshape(D_MODEL, 2 * XQ))
    scale = (HD + ROPE) ** -0.5
    attn, lse = _attn_fwd(q_pad, k_pad, v_h, scale)
    k_b, v_b = _kv_prep(mem0, gm1, W_mkv[1], gk1, "kv_prep1")

    def merge_loss(x, mix, xq, gate, k, v, gq, wout, t):
        err = x + _dot(_merge(mix, xq, gate, k, v, gq).astype(BF16), wout) - t
        part = 0.5 * jnp.sum(jnp.sum(err * err, axis=-1, keepdims=True) * (1.0 / D_MODEL), axis=0, keepdims=True)
        return err * (1.0 / D_MODEL), jnp.broadcast_to(part, (1, HD))

    dx2, loss_part = _rowwise(
        "merge1_loss", merge_loss,
        [('r', x1), ('r', attn), ('r', xq_b), ('r', gate_b), ('c', k_b), ('c', v_b), ('c', gq1), ('c', W_out[1]),
         ('r', target)], [('r', (L, D_MODEL), F32), ('a', (1, HD), F32)], nblk, sub)

    dattn, dxq_b, dgate_b, o_b, g_b, dk_b, dv_b, dgq1 = _backward_merge(
        dx2, attn, 'r', xq_b, gate_b, k_b, v_b, gq1, W_out[1], "merge1_bwd", nblk, sub)
    dgm1, dW_mkv1, dgk1 = _kv_prep_bwd(mem0, gm1, W_mkv[1], gk1, dk_b, dv_b, "kv_prep1_bwd")
    dW_out1 = _matmul_tn(o_b, g_b, "dw_out1")
    dq_pad, dk_pad, dv_h = _attn_bwd(q_pad, k_pad, v_h, attn, lse, dattn, scale)

    def qkv_bwd(c_q, c_kv, krp, tc, ts1, ts2, dqp, dkp, dv, gql, gkvl, wq, wkv, gqn, gkn, gqr, gkr):
        cqn, vjp_qn = jax.vjp(lambda a, b: _rms(a, b, Q_LORA), c_q, gql)
        ckvn, vjp_kvn = jax.vjp(lambda a, b: _rms(a, b, KV_LORA), c_kv, gkvl)
        cqn16 = cqn.astype(BF16)
        ckvn16 = ckvn.astype(BF16)
        q = _dot_nt(cqn16, wq)
        kv = _mm_slots(ckvn16, wkv)
        _, vjp_q = jax.vjp(lambda a, b, c: _q_post(a, b, c, tc, ts1, ts2), q, gqn, gqr)
        dq, dgqn, dgqr = vjp_q(dqp.astype(F32))
        _, vjp_kv = jax.vjp(lambda a, b, c, d: _kv_post(a, b, c, d, tc, ts1, ts2), kv, krp, gkn, gkr)
        dkv, dkrp, dgkn, dgkr = vjp_kv((dkp.astype(F32), dv.astype(F32)))
        dq16 = dq.astype(BF16)
        dkv16 = dkv.astype(BF16)
        dc_q, dgql = vjp_qn(_dot(dq16, wq))
        dc_kv, dgkvl = vjp_kvn(_mm_slots_nt(dkv16, wkv))
        return dc_q, dc_kv, dkrp, cqn16, dq16, ckvn16, dkv16, dgql, dgkvl, dgqn, dgkn, dgqr, dgkr

    (dc_q, dc_kv, dkrp, cqn16, dq16, ckvn16, dkv16, dgql, dgkvl, dgqn, dgkn, dgqr, dgkr) = _rowwise(
        "mla_qkv_bwd", qkv_bwd,
        [('r', c_q), ('r', c_kv), ('r', krp), ('r', tc), ('r', ts1), ('r', ts2), ('r', dq_pad), ('r', dk_pad),
         ('r', dv_h)] + qkv_consts,
        [('r', (L, Q_LORA), BF16), ('r', (L, KV_LORA), BF16), ('r', (L, HD), BF16), ('r', (L, Q_LORA), BF16),
         ('t', (2 * PRIM, L), BF16), ('t', (KV_LORA, L), BF16), ('r', (L, 2 * PRIM), BF16),
         ('a', (1, Q_LORA), F32), ('a', (1, KV_LORA), F32), ('a', (1, HD), F32), ('a', (1, HD), F32),
         ('a', (1, HD), F32), ('a', (1, HD), F32)], nblk, sub)
    dW_q = _matmul_tn(dq16, cqn16, "dw_uq")
    dW_kv = _matmul_tn_slots(ckvn16, dkv16, "dw_ukv")

    def in_bwd(x, dres, g, w, *dparts):
        dproj = jnp.concatenate(dparts, axis=-1).astype(BF16)
        xn, vjp = jax.vjp(lambda a, b: _rms(a, b, D_MODEL), x, g)
        dx, dg = vjp(_mm_slots_nt(dproj, w) if w.ndim == 3 else _dot(dproj, w))
        return dx + dres, xn, dproj, dg

    dx1, xn1, dproj1, dln1 = _rowwise(
        "mla_in_bwd", in_bwd,
        [('r', x1), ('r', dx2), ('c', ln1), ('c', W_in_mla), ('r', dc_q), ('r', dc_kv), ('r', dxq_b), ('r', dgate_b),
         ('r', dkrp)],
        [('r', (L, D_MODEL), F32), ('r', (L, D_MODEL), BF16), ('t', (_MLA_IN_PAD, L), BF16), ('a', (1, D_MODEL), F32)],
        nblk, sub)
    dW_in_mla = _matmul_tn(dproj1, xn1, "dw_mla_in")

    grads1 = [dW_out1.reshape(N_DEV, 256, D_MODEL), dW_mkv1.reshape(N_DEV, 128, 2 * XQ),
              _mla_in_rows_back(dW_in_mla).reshape(N_DEV, 424, D_MODEL),
              _uq_rows_back(dW_q).reshape(N_DEV, 288, Q_LORA), dW_kv]
    (dy2, dxq_a, dgate_a, o_a, g_a, dk_a, dv_a, dgq0), pair1 = _backward_merge(
        dx1, y2, 'r', xq_a, gate_a, k_a, v_a, gq0, W_out[0], "merge0_bwd", nblk, sub, host=_plan_pair(grads1))
    dgm0, dW_mkv0, dgk0 = _kv_prep_bwd(mem0, gm0, W_mkv[0], gk0, dk_a, dv_a, "kv_prep0_bwd")
    dW_out0 = _matmul_tn(o_a, g_a, "dw_out0")
    t1 = list(_pair_add(grads1, pair1, "rs_add_layer1"))

    def glu_bwd(y, dy2, w):
        h, vjp_h = jax.vjp(_gelu, y)
        h16 = h.astype(BF16)
        z = _mm_slots(h16, w)
        _, vjp_z = jax.vjp(lambda z: z[:, :PRIM] * _sigmoid(z[:, PRIM:]), z)
        dz16 = vjp_z(dy2)[0].astype(BF16)
        return vjp_h(_mm_slots_nt(dz16, w))[0], h16, dz16

    grads0 = [dW_out0.reshape(N_DEV, 256, D_MODEL), dW_mkv0.reshape(N_DEV, 128, 2 * XQ)]
    (dy_s5, h16, dz16), glu_hosted = _rowwise(
        "s5_glu_bwd", glu_bwd, [('r', y_s5), ('r', dy2), ('c', W_glu)],
        [('r', (L, PRIM), F32), ('t', (PRIM, L), BF16), ('r', (L, 2 * PRIM), BF16)], nblk, sub,
        host=_combine(_plan_chips(t1[2:]), _plan_pair(grads0)))
    recv_proj1, pair0 = glu_hosted[:3], glu_hosted[3:]
    dW_glu = _matmul_tn_slots(h16, dz16, "dw_glu")
    t0 = list(_pair_add(grads0 + [dW_glu], pair0 + list(_exchange_call(_plan_pair([dW_glu]), "rs_pair_glu")),
                        "rs_add_layer0"))
    (du_s5, dbc, dcc, dd, dar, dai), recv_rest = _s5_bwd(u_s5, dy_s5, s5_carry, bm, cm, a_r2, a_i2, s5_d,
                                                        cmask, rmat, host=_plan_chips(t1[:2] + t0))
    early_recv = recv_rest[:2] + recv_proj1 + recv_rest[2:]
    dbc4 = dbc.reshape(S5_G, S5_C, 2, S5_P)
    dcc4 = dcc.reshape(S5_G, S5_C, 2, S5_P)
    dlr, dli, dls, dbtr, dbti = _s5_params_bwd(
        lr3, li3, ls3, btr, bti, dar.reshape(S5_G, 1, S5_P), dai.reshape(S5_G, 1, S5_P), dbc4[:, :, 0], dbc4[:, :, 1])

    small_part = {
        "ln_gain": jnp.concatenate([jnp.zeros_like(dln1), dln1]), "mem_norm": jnp.concatenate([dgm0, dgm1]),
        "xq_norm": jnp.concatenate([dgq0, dgq1]), "xk_norm": jnp.concatenate([dgk0, dgk1]),
        "s5_lambda_re": dlr, "s5_lambda_im": dli, "s5_log_step": dls,
        "s5_b_re": jnp.swapaxes(dbtr, 1, 2), "s5_b_im": jnp.swapaxes(dbti, 1, 2),
        "s5_c_re": dcc4[:, :, 0], "s5_c_im": -dcc4[:, :, 1], "s5_d": dd,
        "mla_q_lora_norm": dgql, "mla_kv_lora_norm": dgkvl, "mla_q_nope_norm": dgqn, "mla_k_nope_norm": dgkn,
        "mla_q_rope_norm": dgqr[:, :ROPE], "mla_k_rope_norm": dgkr[:, :ROPE],
    }
    loss8 = jnp.pad(loss_part, ((0, 7), (0, 0)))
    (dx0, xn0, dproj0, dln0), (small_gath, loss_g) = _rowwise(
        "s5_in_bwd", in_bwd,
        [('r', x0), ('r', dx1), ('c', ln0), ('c', W_in_s5), ('r', du_s5), ('r', dxq_a),
         ('r', dgate_a)],
        [('r', (L, D_MODEL), F32), ('t', (D_MODEL, L), BF16), ('r', (L, 2 * BRANCH), BF16), ('a', (1, D_MODEL), F32)],
        nblk, sub, host=_plan_all_gather([_pack_small(small_part).astype(BF16), loss8]))
    dW_in_s5 = _matmul_tn_slots(xn0, dproj0, "dw_s5_in")

    late = [dW_in_s5]
    late_t = _pair_add(late, list(_exchange_call(_plan_pair(late), "rs_pair_late")), "rs_add_late")
    owners = [("w_out", 1), ("w_mem_kv", 1), ("mla_w_in", 0), ("mla_w_uq", 0), ("mla_w_ukv", 0), ("w_out", 0),
              ("w_mem_kv", 0), ("s5_w_glu", 0)]
    flipped = ("mla_w_in", "mla_w_uq")

    def shard(d, n, i):
        return jnp.transpose(d[n][i]) if n in flipped else d[n][i]

    upd, (late_recv, ln0_gath) = _updates_call(
        early_recv, [shard(weights, n, i) for n, i in owners], [shard(m_in, n, i) for n, i in owners],
        [shard(v_in, n, i) for n, i in owners], "update_early",
        host=_combine(_plan_chips(late_t), _plan_all_gather([jnp.pad(dln0, ((0, 7), (0, 0)))])))
    owners.append(("s5_w_in", 0))
    upd.append(_sum_adamw(late_recv, s5_w_in[0], m_s5_w_in[0], v_s5_w_in[0], "update_s5_w_in"))
    grads, delta, new_m, new_v = {}, {}, {}, {}
    for n in _BIG:
        parts = [u for u, (o, _) in sorted(zip(upd, owners), key=lambda t: t[1][1]) if o == n]
        if n in flipped:
            grads[n], delta[n], new_m[n], new_v[n] = (jnp.transpose(parts[0][j])[None] for j in range(4))
        else:
            grads[n], delta[n], new_m[n], new_v[n] = (jnp.stack([p[j] for p in parts]) for j in range(4))

    gs, loss_sum = _small_sum(small_gath, loss_g, ln0_gath, "small_sum")
    loss = loss_sum[0, 0]
    for n, _ in _SMALL:
        shape = weights[n].shape
        if n == "mla_q_lora_norm":
            grads[n] = lax.dynamic_slice(_unpack_small(gs, n, (Q_LORA,)), (me * 64,), (64,)).reshape(shape)
        elif n == "mla_kv_lora_norm":
            grads[n] = lax.dynamic_slice(_unpack_small(gs, n, (KV_LORA,)), (me * 32,), (32,)).reshape(shape)
        else:
            grads[n] = _unpack_small(gs, n, shape)

    def own(n, a):
        if a.ndim == 4:
            a = jnp.transpose(a, (0, 2, 3, 1))
        elif a.ndim == 3:
            a = jnp.transpose(a, (0, 2, 1))
        return a.reshape(a.shape[1:]) if a.ndim >= 3 else a

    def back(n, a):
        shape = weights[n].shape
        if len(shape) == 4:
            return jnp.transpose(a.reshape((1,) + a.shape), (0, 3, 1, 2))
        if len(shape) == 3:
            return jnp.transpose(a.reshape((1,) + a.shape), (0, 2, 1))
        return a.reshape(shape)

    wide = ("s5_b_re", "s5_b_im", "s5_c_re", "s5_c_im")
    for names, nb, call in (([n for n, _ in _SMALL if n not in wide], 1, "update_small"), (wide, 4, "update_s5_bc")):
        res = _adamw_multi([own(n, weights[n]) for n in names], [own(n, grads[n]) for n in names],
                           [own(n, m_in[n]) for n in names], [own(n, v_in[n]) for n in names], call, nb)
        for n, (dl, m2, v2) in zip(names, res):
            delta[n], new_m[n], new_v[n] = back(n, dl), back(n, m2), back(n, v2)
    return (loss, dx0[None], *[grads[n] for n in _WEIGHTS], *[delta[n] for n in _WEIGHTS],
            *[new_m[n] for n in _WEIGHTS], *[new_v[n] for n in _WEIGHTS])
```

```python
import functools
import math

import numpy as np
import jax
import jax.numpy as jnp
from jax import lax
from jax.experimental import pallas as pl
from jax.experimental.pallas import tpu as pltpu

F32 = jnp.float32
BF16 = jnp.bfloat16
EPS = 1e-6
NEG = float(np.finfo(np.float32).min)
MESH = pl.DeviceIdType.MESH

N_DEV = 8
D_MODEL = 1024
MEM_LEN = 256
XQ = 512
PRIM = 1536
BRANCH = 2048
X_HEADS = 4
HD = 128
S5_G = 96
S5_P = 64
S5_C = 16
S5_GB = 8
S5_W = S5_GB * S5_P
MLA_H = 12
ROPE = 64
Q_LORA = 512
KV_LORA = 256
ROPE_THETA = 10000.0

ADAM_LR = 0.001
ADAM_B1 = 0.9
ADAM_B2 = 0.999
ADAM_EPS = 1e-08
ADAM_WD = 0.01
ADAM_STEP = 10

VMEM_LIMIT = 56 * 1024 * 1024


def _dot(a, b):
    return jnp.dot(a, b, preferred_element_type=F32)


def _dot_nt(a, b):
    return lax.dot_general(a, b, (((1,), (1,)), ((), ())), preferred_element_type=F32)


def _dot_tn(a, b):
    return lax.dot_general(a, b, (((0,), (0,)), ((), ())), preferred_element_type=F32)


@jax.custom_vjp
def _mm(a, b):
    return _dot(a.astype(BF16), b.astype(BF16))


def _mm_fwd(a, b):
    return _mm(a, b), (a, b)


def _mm_bwd(res, g):
    a, b = res
    gb = g.astype(BF16)
    return _dot_nt(gb, b.astype(BF16)).astype(a.dtype), _dot_tn(a.astype(BF16), gb).astype(b.dtype)


_mm.defvjp(_mm_fwd, _mm_bwd)


@jax.custom_vjp
def _mm_nt(a, b):
    return _dot_nt(a.astype(BF16), b.astype(BF16))


def _mm_nt_fwd(a, b):
    return _mm_nt(a, b), (a, b)


def _mm_nt_bwd(res, g):
    a, b = res
    gb = g.astype(BF16)
    return _dot(gb, b.astype(BF16)).astype(a.dtype), _dot_tn(gb, a.astype(BF16)).astype(b.dtype)


_mm_nt.defvjp(_mm_nt_fwd, _mm_nt_bwd)


@jax.custom_vjp
def _softmax(s):
    m = jnp.max(s, axis=-1, keepdims=True)
    e = jnp.exp(s - m)
    return e / jnp.sum(e, axis=-1, keepdims=True)


def _softmax_fwd(s):
    p = _softmax(s)
    return p, p


def _softmax_bwd(p, g):
    return (p * (g - jnp.sum(p * g, axis=-1, keepdims=True)),)


_softmax.defvjp(_softmax_fwd, _softmax_bwd)


def _rms(x, g, n):
    ms = jnp.sum(x * x, axis=-1, keepdims=True) * (1.0 / n)
    return x * lax.rsqrt(ms + EPS) * g


def _sigmoid(x):
    return 1.0 / (1.0 + jnp.exp(-x))


def _silu(x):
    return x * _sigmoid(x)


def _gelu(x):
    c = math.sqrt(2.0 / math.pi)
    return 0.5 * x * (1.0 + jnp.tanh(c * (x + 0.044715 * (x * x * x))))


@jax.custom_vjp
def _rot(x, c, s1, s2):
    return x * c + pltpu.roll(x, 96, 1) * s1 + pltpu.roll(x, 32, 1) * s2


def _rot_fwd(x, c, s1, s2):
    return _rot(x, c, s1, s2), (c, s1, s2)


def _rot_bwd(res, g):
    c, s1, s2 = res
    dx = g * c + pltpu.roll(g * s1, 32, 1) + pltpu.roll(g * s2, 96, 1)
    return dx, jnp.zeros_like(c), jnp.zeros_like(s1), jnp.zeros_like(s2)


_rot.defvjp(_rot_fwd, _rot_bwd)


def _mem_attn(xq, k, v, gq):
    outs = []
    for h in range(X_HEADS):
        sl = slice(HD * h, HD * (h + 1))
        q = _rms(xq[:, sl], gq, HD)
        p = _softmax(_mm_nt(q, k[:, sl]) * (HD ** -0.5))
        outs.append(_mm(p, v[:, sl]))
    return jnp.concatenate(outs, axis=-1)


def _merge(mix, xq, gate, k, v, gq):
    return jnp.concatenate([mix, _mem_attn(xq, k, v, gq)], axis=-1) * _silu(gate)


def _q_post(q, gqn, gqr, c, s1, s2):
    pieces = []
    for h in range(MLA_H):
        pieces.append(_rms(q[:, HD * h:HD * (h + 1)], gqn, HD))
        pieces.append(_rot(_rms(q[:, PRIM + HD * h:PRIM + HD * (h + 1)], gqr, ROPE), c, s1, s2))
    return jnp.concatenate(pieces, axis=-1)


def _kv_post(kv, krp, gkn, gkr, c, s1, s2):
    kr = _rot(_rms(krp, gkr, ROPE), c, s1, s2)
    pieces, vals = [], []
    for h in range(MLA_H):
        pieces.append(_rms(kv[:, 2 * HD * h:2 * HD * h + HD], gkn, HD))
        pieces.append(kr)
        vals.append(kv[:, 2 * HD * h + HD:2 * HD * (h + 1)])
    return jnp.concatenate(pieces, axis=-1), jnp.concatenate(vals, axis=-1)


def _rowwise(name, fn, ins, outs, nblk, sub=1, host=None):
    n_in = len(ins)

    def spec(kind, shape):
        if kind == 'r':
            return pl.BlockSpec((shape[0] // nblk, shape[1]), lambda i: (i, 0))
        if kind == 't':
            return pl.BlockSpec((shape[0], shape[1] // nblk), lambda i: (0, i))
        zeros = (0,) * len(shape)
        return pl.BlockSpec(tuple(shape), lambda i: zeros)

    def body(*refs):
        i = pl.program_id(0)
        res = fn(*[r[...] for r in refs[:n_in]])
        for (kind, _, _), ref, val in zip(outs, refs[n_in:], res):
            if kind == 'a':
                @pl.when(i == 0)
                def _():
                    ref[...] = jnp.zeros_like(ref)
                ref[...] += val.astype(ref.dtype)
            elif kind == 't':
                ref[...] = val.astype(F32).T.astype(ref.dtype)
            else:
                ref[...] = val.astype(ref.dtype)

    res, hosted = _hosting_call(
        body, name, nblk, host, [a for _, a in ins], [spec(k, a.shape) for k, a in ins],
        [jax.ShapeDtypeStruct(tuple(s), d) for _, s, d in outs], [spec(k, s) for k, s, _ in outs], [])
    return res if host is None else (res, hosted)


def _matmul_tn(at, g, name, out_dtype=BF16):
    K, L = at.shape
    N = g.shape[1]
    tn = next(t for t in (512, 384, 256, 128) if N % t == 0)

    def body(a_ref, g_ref, o_ref):
        o_ref[...] = _dot(a_ref[...], g_ref[...]).astype(o_ref.dtype)

    return pl.pallas_call(
        body, name=name, grid=(N // tn,),
        in_specs=[pl.BlockSpec((K, L), lambda n: (0, 0)), pl.BlockSpec((L, tn), lambda n: (0, n))],
        out_specs=pl.BlockSpec((K, tn), lambda n: (0, n)),
        out_shape=jax.ShapeDtypeStruct((K, N), out_dtype),
        compiler_params=pltpu.CompilerParams(dimension_semantics=("arbitrary",), vmem_limit_bytes=VMEM_LIMIT),
    )(at, g)


def _matmul_tn_slots(at, g, name, host=None):
    K, L = at.shape
    n = g.shape[1] // N_DEV

    def body(a_ref, g_ref, o_ref):
        o_ref[...] = _dot(a_ref[...], g_ref[...]).astype(o_ref.dtype)

    res, hosted = _hosting_call(
        body, name, N_DEV, host, [at, g],
        [pl.BlockSpec((K, L), lambda d: (0, 0)), pl.BlockSpec((L, n), lambda d: (0, d))],
        [jax.ShapeDtypeStruct((N_DEV, K, n), BF16)], [pl.BlockSpec((None, K, n), lambda d: (d, 0, 0))], [])
    return res[0] if host is None else (res[0], hosted)


def _mm_slots(a16, w):
    return jnp.concatenate([_dot(a16, w[d]) for d in range(N_DEV)], axis=-1)


def _mm_slots_nt(g16, w):
    n = w.shape[2]
    out = _dot_nt(g16[:, 0:n], w[0])
    for d in range(1, N_DEV):
        out = out + _dot_nt(g16[:, d * n:(d + 1) * n], w[d])
    return out


class _Exchange:
    def __init__(self, ins, outs, scratch, start, finish):
        self.ins, self.outs, self.scratch, self.start, self.finish = ins, outs, scratch, start, finish


def _xyc():
    return lax.axis_index("x"), lax.axis_index("y"), lax.axis_index("c")


def _plan_all_gather(xs):
    n = len(xs)

    def build(x_refs, out_refs, sems):
        send_sems, recv_sems, local_sems = sems
        x, y, c = _xyc()

        def copies(k, block, to, own=False):
            slot = 4 * block[0] + 2 * block[1] + block[2]
            return [pltpu.make_async_remote_copy(
                src_ref=x_refs[a] if own else out_refs[a].at[slot], dst_ref=out_refs[a].at[slot],
                send_sem=send_sems.at[k * n + a], recv_sem=recv_sems.at[k * n + a], device_id=to,
                device_id_type=MESH) for a in range(n)]

        mine = [pltpu.make_async_copy(x_refs[a], out_refs[a].at[4 * x + 2 * y + c], local_sems.at[a])
                for a in range(n)]
        return copies, mine, (x, y, c), [(1 - x, y), (x, 1 - y), (1 - x, 1 - y)]

    def first_copies(copies, me, chips):
        x, y, c = me
        first = copies(0, me, (x, y, 1 - c), own=True)
        for j, chip in enumerate(chips):
            first += copies(1 + j, me, (*chip, c), own=True)
        return first

    def start(x_refs, out_refs, sems):
        copies, mine, me, chips = build(x_refs, out_refs, sems)
        for cp in mine + first_copies(copies, me, chips):
            cp.start()

    def finish(x_refs, out_refs, sems):
        copies, mine, me, chips = build(x_refs, out_refs, sems)
        x, y, c = me
        passed = []
        for j, chip in enumerate(chips):
            for cp in copies(1 + j, (*chip, c), me):
                cp.wait_recv()
            fwd = copies(4 + j, (*chip, c), (x, y, 1 - c))
            for cp in fwd:
                cp.start()
            passed += fwd
        for cp in copies(0, (x, y, 1 - c), me):
            cp.wait_recv()
        for j, chip in enumerate(chips):
            for cp in copies(4 + j, (*chip, 1 - c), me):
                cp.wait_recv()
        for cp in first_copies(copies, me, chips) + passed:
            cp.wait_send()
        for cp in mine:
            cp.wait()

    return _Exchange(list(xs), [jax.ShapeDtypeStruct((N_DEV,) + a.shape, a.dtype) for a in xs],
                     [pltpu.SemaphoreType.DMA((7 * n,)), pltpu.SemaphoreType.DMA((7 * n,)),
                      pltpu.SemaphoreType.DMA((n,))], start, finish)


_CHIPS = ((0, 0), (0, 1), (1, 0), (1, 1))


def _plan_pair(sends):
    n = len(sends)

    def build(s_refs, o_refs, sems):
        send_sems, recv_sems = sems
        x, y, c = _xyc()
        return [pltpu.make_async_remote_copy(
            src_ref=s_refs[a].at[4 * px + 2 * py + 1 - c], dst_ref=o_refs[a].at[j],
            send_sem=send_sems.at[j * n + a], recv_sem=recv_sems.at[j * n + a], device_id=(x, y, 1 - c),
            device_id_type=MESH) for j, (px, py) in enumerate(_CHIPS) for a in range(n)]

    def start(s_refs, o_refs, sems):
        for cp in build(s_refs, o_refs, sems):
            cp.start()

    def finish(s_refs, o_refs, sems):
        for cp in build(s_refs, o_refs, sems):
            cp.wait_recv()
            cp.wait_send()

    return _Exchange(list(sends), [jax.ShapeDtypeStruct((4,) + a.shape[1:], a.dtype) for a in sends],
                     [pltpu.SemaphoreType.DMA((4 * n,)), pltpu.SemaphoreType.DMA((4 * n,))], start, finish)


def _plan_chips(ts):
    n = len(ts)
    flips = ((1, 0), (0, 1), (1, 1))

    def build(t_refs, o_refs, sems):
        send_sems, recv_sems, local_sems = sems
        x, y, c = _xyc()
        mine = 2 * x + y
        local = [pltpu.make_async_copy(t_refs[a].at[mine], o_refs[a].at[mine], local_sems.at[a]) for a in range(n)]
        remote = []
        for k, (fx, fy) in enumerate(flips):
            px = 1 - x if fx else x
            py = 1 - y if fy else y
            remote += [pltpu.make_async_remote_copy(
                src_ref=t_refs[a].at[2 * px + py], dst_ref=o_refs[a].at[mine],
                send_sem=send_sems.at[k * n + a], recv_sem=recv_sems.at[k * n + a], device_id=(px, py, c),
                device_id_type=MESH) for a in range(n)]
        return local, remote

    def start(t_refs, o_refs, sems):
        local, remote = build(t_refs, o_refs, sems)
        for cp in local + remote:
            cp.start()

    def finish(t_refs, o_refs, sems):
        local, remote = build(t_refs, o_refs, sems)
        for cp in remote:
            cp.wait_recv()
        for cp in remote:
            cp.wait_send()
        for cp in local:
            cp.wait()

    return _Exchange(list(ts), [jax.ShapeDtypeStruct(a.shape, a.dtype) for a in ts],
                     [pltpu.SemaphoreType.DMA((3 * n,)), pltpu.SemaphoreType.DMA((3 * n,)),
                      pltpu.SemaphoreType.DMA((n,))], start, finish)


def _combine(*plans):
    def parts(refs, attr):
        out, at = [], 0
        for p in plans:
            n = len(getattr(p, attr))
            out.append(refs[at:at + n])
            at += n
        return out

    def run(half):
        def go(ins, outs, sems):
            for p, a, o, s in zip(plans, parts(ins, "ins"), parts(outs, "outs"), parts(sems, "scratch")):
                getattr(p, half)(a, o, s)
        return go

    return _Exchange(sum((p.ins for p in plans), []), sum((p.outs for p in plans), []),
                     sum((p.scratch for p in plans), []), run("start"), run("finish"))


def _exchange_call(plan, name):
    n = len(plan.ins)

    def body(*refs):
        ins, outs, sems = refs[:n], refs[n:2 * n], refs[2 * n:]
        plan.start(ins, outs, sems)
        plan.finish(ins, outs, sems)

    return pl.pallas_call(
        body, name=name, out_shape=plan.outs,
        in_specs=[pl.BlockSpec(memory_space=pl.ANY)] * n, out_specs=[pl.BlockSpec(memory_space=pl.ANY)] * n,
        scratch_shapes=plan.scratch,
    )(*plan.ins)


def _slab_spec(lead, rows, cols, nb):
    if rows % (nb * 16) == 0:
        return pl.BlockSpec((lead, rows // nb, cols), lambda i: (0, i, 0))
    if cols % (nb * 128) == 0:
        return pl.BlockSpec((lead, rows, cols // nb), lambda i: (0, 0, i))
    return pl.BlockSpec((lead, rows, cols), lambda i: (0, 0, 0))


def _slab_spec2(rows, cols, nb):
    if rows % (nb * 16) == 0:
        return pl.BlockSpec((rows // nb, cols), lambda i: (i, 0))
    if cols % (nb * 128) == 0:
        return pl.BlockSpec((rows, cols // nb), lambda i: (0, i))
    return pl.BlockSpec((rows, cols), lambda i: (0, 0))


def _cast_call(arrays, name, host=None):
    n = len(arrays)
    nb = 8

    def body(*refs):
        for a in range(n):
            refs[n + a][...] = refs[a][...].astype(BF16)

    specs = [_slab_spec2(x.shape[0], x.shape[1], nb) for x in arrays]
    return _hosting_call(body, name, nb, host, list(arrays), specs,
                         [jax.ShapeDtypeStruct(x.shape, BF16) for x in arrays], specs, [])


def _pair_add(sends, fromsib, name):
    n = len(sends)
    nb = 8

    def body(*refs):
        c = lax.axis_index("c")
        for a in range(n):
            s_ref, f_ref, t_ref = refs[a], refs[n + a], refs[2 * n + a]
            for j in range(4):
                t_ref[j] = (s_ref[2 * j + c].astype(F32) + f_ref[j].astype(F32)).astype(t_ref.dtype)

    def spec(a, lead):
        return _slab_spec(lead, a.shape[1], a.shape[2], nb)

    return pl.pallas_call(
        body, name=name, grid=(nb,),
        in_specs=[spec(a, N_DEV) for a in sends] + [spec(a, 4) for a in fromsib],
        out_specs=[spec(a, 4) for a in fromsib],
        out_shape=[jax.ShapeDtypeStruct(a.shape, a.dtype) for a in fromsib],
        compiler_params=pltpu.CompilerParams(dimension_semantics=("arbitrary",), vmem_limit_bytes=VMEM_LIMIT),
    )(*sends, *fromsib)


def _adamw_vals(w, g, m, v):
    m2 = ADAM_B1 * m + (1.0 - ADAM_B1) * g
    v2 = ADAM_B2 * v + (1.0 - ADAM_B2) * (g * g)
    m_hat = m2 / (1.0 - ADAM_B1 ** ADAM_STEP)
    v_hat = v2 / (1.0 - ADAM_B2 ** ADAM_STEP)
    delta = -ADAM_LR * (m_hat / (jnp.sqrt(v_hat) + ADAM_EPS) + ADAM_WD * w)
    return delta, m2, v2


def _sum_adamw(recv, w, m, v, name):
    R, C = w.shape
    ns = recv.shape[0]
    br = next((t for t in (256, 128, 64, 32, 16) if R % t == 0), R)

    def body(r_ref, w_ref, m_ref, v_ref, g_ref, d_ref, m2_ref, v2_ref):
        g = r_ref[0].astype(F32)
        for d in range(1, ns):
            g = g + r_ref[d].astype(F32)
        dl, m2, v2 = _adamw_vals(w_ref[...], g, m_ref[...], v_ref[...])
        g_ref[...] = g
        d_ref[...] = dl
        m2_ref[...] = m2
        v2_ref[...] = v2

    spec = pl.BlockSpec((br, C), lambda i: (i, 0))
    return pl.pallas_call(
        body, name=name, grid=(R // br,),
        in_specs=[pl.BlockSpec((ns, br, C), lambda i: (0, i, 0)), spec, spec, spec], out_specs=[spec] * 4,
        out_shape=[jax.ShapeDtypeStruct((R, C), F32)] * 4,
        compiler_params=pltpu.CompilerParams(dimension_semantics=("arbitrary",)),
    )(recv, w, m, v)


def _updates_call(recvs, ws, ms, vs, name, host=None):
    n = len(recvs)
    nb = 8

    def body(*refs):
        for a in range(n):
            r_ref, w_ref, m_ref, v_ref = refs[a], refs[n + a], refs[2 * n + a], refs[3 * n + a]
            g_ref, d_ref, m2_ref, v2_ref = refs[4 * n + 4 * a:4 * n + 4 * a + 4]
            g = r_ref[0].astype(F32)
            for d in range(1, r_ref.shape[0]):
                g = g + r_ref[d].astype(F32)
            dl, m2, v2 = _adamw_vals(w_ref[...], g, m_ref[...], v_ref[...])
            g_ref[...] = g
            d_ref[...] = dl
            m2_ref[...] = m2
            v2_ref[...] = v2

    def spec3(r):
        return _slab_spec(r.shape[0], r.shape[1], r.shape[2], nb)

    def spec2(w):
        return _slab_spec2(w.shape[0], w.shape[1], nb)

    res, hosted = _hosting_call(
        body, name, nb, host, list(recvs) + list(ws) + list(ms) + list(vs),
        [spec3(r) for r in recvs] + [spec2(w) for w in ws] * 3,
        [jax.ShapeDtypeStruct(w.shape, F32) for w in ws for _ in range(4)],
        [spec2(w) for w in ws for _ in range(4)], [])
    return [res[4 * a:4 * a + 4] for a in range(n)], hosted


def _small_sum(gath, loss_g, row0_g, name):
    _, R, C = gath.shape
    br = R // 3

    def body(g_ref, l_ref, r_ref, go_ref, lo_ref):
        g = g_ref[0].astype(F32)
        lsum = l_ref[0]
        for d in range(1, N_DEV):
            g = g + g_ref[d].astype(F32)
            lsum = lsum + l_ref[d]
        go_ref[...] = g
        lo_ref[...] = lsum

        @pl.when(pl.program_id(0) == 0)
        def _():
            row0 = r_ref[0]
            for d in range(1, N_DEV):
                row0 = row0 + r_ref[d]
            go_ref[0:8, :] = go_ref[0:8, :] + jnp.where(lax.broadcasted_iota(jnp.int32, row0.shape, 0) == 0, row0, 0.0)

    return pl.pallas_call(
        body, name=name, grid=(R // br,),
        in_specs=[pl.BlockSpec((N_DEV, br, C), lambda i: (0, i, 0)),
                  pl.BlockSpec((N_DEV, 8, HD), lambda i: (0, 0, 0)), pl.BlockSpec((N_DEV, 8, C), lambda i: (0, 0, 0))],
        out_specs=[pl.BlockSpec((br, C), lambda i: (i, 0)), pl.BlockSpec((8, HD), lambda i: (0, 0))],
        out_shape=[jax.ShapeDtypeStruct((R, C), F32), jax.ShapeDtypeStruct((8, HD), F32)],
        compiler_params=pltpu.CompilerParams(dimension_semantics=("arbitrary",)),
    )(gath, loss_g, row0_g)


def _adamw_multi(ws, gs, ms, vs, name, nblk=1):
    n = len(ws)

    def body(*refs):
        for a in range(n):
            dl, m2, v2 = _adamw_vals(refs[a][...], refs[n + a][...], refs[2 * n + a][...], refs[3 * n + a][...])
            refs[4 * n + 3 * a][...] = dl
            refs[4 * n + 3 * a + 1][...] = m2
            refs[4 * n + 3 * a + 2][...] = v2

    def spec(x):
        rest = (0,) * (x.ndim - 1)
        return pl.BlockSpec((x.shape[0] // nblk,) + tuple(x.shape[1:]), lambda i: (i,) + rest)

    res = pl.pallas_call(
        body, name=name, grid=(nblk,),
        in_specs=[spec(w) for w in ws] * 4, out_specs=[spec(w) for w in ws for _ in range(3)],
        out_shape=[jax.ShapeDtypeStruct(w.shape, F32) for w in ws for _ in range(3)],
        compiler_params=pltpu.CompilerParams(dimension_semantics=("arbitrary",), vmem_limit_bytes=VMEM_LIMIT),
    )(*ws, *gs, *ms, *vs)
    return [res[3 * a:3 * a + 3] for a in range(n)]


def _s5_param_fn(lr, li, ls, btr, bti):
    step = jnp.exp(ls)
    er = jnp.exp(lr * step)
    ang = li * step
    ar = er * jnp.cos(ang)
    ai = er * jnp.sin(ang)
    nr = ar - 1.0
    den = lr * lr + li * li
    fr = (nr * lr + ai * li) / den
    fi = (ai * lr - nr * li) / den
    return ar, ai, fr * btr - fi * bti, fr * bti + fi * btr


def _s5_params(lr, li, ls, btr, bti, cre, cim):
    nb = S5_G // S5_GB
    GC = S5_GB * S5_C
    expand = jnp.asarray(np.tile(np.eye(S5_P, dtype=np.float32), (1, S5_GB)), BF16)
    own = jnp.asarray((np.arange(GC)[:, None] // S5_C == np.arange(S5_W)[None, :] // S5_P).astype(np.float32))

    def body(lr_ref, li_ref, ls_ref, br_ref, bi_ref, cr_ref, ci_ref, e_ref, own_ref, ar_ref, ai_ref, bm_ref, cm_ref):
        ar, ai, bbr, bbi = _s5_param_fn(lr_ref[...], li_ref[...], ls_ref[...], br_ref[...], bi_ref[...])
        ar_ref[...] = ar
        ai_ref[...] = ai

        def plane(x, n):
            rows = x[n * S5_GB:(n + 1) * S5_GB].reshape(GC, S5_P).astype(BF16)
            return _dot(rows, e_ref[...]) * own_ref[...]

        for n in range(nb):
            bm_ref[n] = jnp.concatenate([plane(bbr, n), plane(bbi, n)], axis=-1).astype(BF16)
            cm_ref[n] = jnp.concatenate([plane(cr_ref[...], n), -plane(ci_ref[...], n)], axis=-1).astype(BF16)

    sd = jax.ShapeDtypeStruct
    return pl.pallas_call(
        body, name="s5_params",
        out_shape=[sd(lr.shape, F32), sd(lr.shape, F32), sd((nb, GC, 2 * S5_W), BF16), sd((nb, GC, 2 * S5_W), BF16)],
        compiler_params=pltpu.CompilerParams(vmem_limit_bytes=VMEM_LIMIT),
    )(lr, li, ls, btr, bti, cre, cim, expand, own)


def _s5_params_bwd(lr, li, ls, btr, bti, dar, dai, dbbr, dbbi):
    def body(lr_ref, li_ref, ls_ref, br_ref, bi_ref, dar_ref, dai_ref, dbbr_ref, dbbi_ref,
             dlr_ref, dli_ref, dls_ref, dbr_ref, dbi_ref):
        _, vjp = jax.vjp(_s5_param_fn, lr_ref[...], li_ref[...], ls_ref[...], br_ref[...], bi_ref[...])
        dlr, dli, dls, dbr, dbi = vjp((dar_ref[...], dai_ref[...], dbbr_ref[...], dbbi_ref[...]))
        dlr_ref[...] = dlr
        dli_ref[...] = dli
        dls_ref[...] = dls
        dbr_ref[...] = dbr
        dbi_ref[...] = dbi

    sd = jax.ShapeDtypeStruct
    return pl.pallas_call(
        body, name="s5_params_bwd",
        out_shape=[sd(lr.shape, F32), sd(lr.shape, F32), sd(ls.shape, F32), sd(btr.shape, F32), sd(btr.shape, F32)],
    )(lr, li, ls, btr, bti, dar, dai, dbbr, dbbi)


def _cpow(ar, ai, n):
    assert n & (n - 1) == 0
    while n > 1:
        ar, ai = ar * ar - ai * ai, 2.0 * ar * ai
        n //= 2
    return ar, ai


def _scan(st, cr, ci, init, nk, reverse, store, prev=None):
    W = S5_W

    def step(j, carry):
        k = nk - 1 - j if reverse else j
        rows = pl.ds(pl.multiple_of(k * 8, 8), 8)
        sr, si = carry[0], carry[1]
        nsr = cr * sr - ci * si + st[rows, 0:W]
        nsi = cr * si + ci * sr + st[rows, W:2 * W]
        if store:
            st[rows, 0:W] = nsr
            st[rows, W:2 * W] = nsi
        if prev is None:
            return nsr, nsi
        prows = pl.ds(pl.multiple_of(jnp.maximum(k - 1, 0) * 8, 8), 8)
        w = jnp.where(k > 0, 1.0, 0.0).astype(F32)
        pr = prev[prows, 0:W] * w
        pi = prev[prows, W:2 * W] * w
        return nsr, nsi, carry[2] + nsr * pr + nsi * pi, carry[3] + nsi * pr - nsr * pi

    return lax.fori_loop(0, nk, step, init, unroll=2)


def _chain(fin, fr, fi, pr, pi, reverse):
    W = S5_W
    fin[:, 0:W] = fr
    fin[:, W:2 * W] = fi
    rowid = lax.broadcasted_iota(jnp.int32, (8, W), 0)
    cr = jnp.zeros((1, W), F32)
    ci = jnp.zeros((1, W), F32)
    init_r = jnp.zeros((8, W), F32)
    init_i = jnp.zeros((8, W), F32)
    for s in (range(7, -1, -1) if reverse else range(8)):
        init_r = jnp.where(rowid == s, cr, init_r)
        init_i = jnp.where(rowid == s, ci, init_i)
        lr = fin[s:s + 1, 0:W]
        li = fin[s:s + 1, W:2 * W]
        cr, ci = lr + pr * cr - pi * ci, li + pr * ci + pi * cr
    return init_r, init_i


def _full_scan(st, fin, ar, ai, nk, reverse, prev=None, carry_in=None, carry_out=None):
    W = S5_W
    cr = jnp.broadcast_to(ar, (8, W))
    ci = jnp.broadcast_to(-ai if reverse else ai, (8, W))
    z = jnp.zeros((8, W), F32)
    if carry_in is None:
        fr, fi = _scan(st, cr, ci, (z, z), nk, reverse, store=False)
        pr, pi = _cpow(ar, -ai if reverse else ai, nk)
        init = _chain(fin, fr, fi, pr, pi, reverse)
    else:
        init = (carry_in[:, 0:W], carry_in[:, W:2 * W])
    if carry_out is not None:
        carry_out[:, 0:W] = init[0]
        carry_out[:, W:2 * W] = init[1]
    if prev is None:
        return _scan(st, cr, ci, init, nk, reverse, store=True)
    return _scan(st, cr, ci, init + (z, z), nk, reverse, store=True, prev=prev)


def _s5_specs(L):
    W2 = 2 * S5_W
    GC = S5_GB * S5_C
    col = pl.BlockSpec((L, GC), lambda g: (0, g))
    vec = pl.BlockSpec((1, GC), lambda g: (0, g))
    avec = pl.BlockSpec((1, S5_W), lambda g: (0, g))
    bmat = pl.BlockSpec((None, GC, W2), lambda g: (g, 0, 0))
    cmat = pl.BlockSpec((None, W2, GC), lambda g: (g, 0, 0))
    return col, vec, avec, bmat, cmat


def _interleave(dst, src, nk):
    for s in range(8):
        dst[pl.ds(s, nk, stride=8), :] = src[s * nk:(s + 1) * nk, :]


def _deinterleave(dst, src, nk):
    for s in range(8):
        dst[s * nk:(s + 1) * nk, :] = src[pl.ds(s, nk, stride=8), :].astype(dst.dtype)


def _hosting_call(body, name, nsteps, host, ins, in_specs, outs, out_specs, scratch):
    grid = (nsteps,) if isinstance(nsteps, int) else tuple(nsteps)
    params = pltpu.CompilerParams(dimension_semantics=("arbitrary",) * len(grid), vmem_limit_bytes=VMEM_LIMIT)
    if host is None:
        res = pl.pallas_call(
            body, name=name, grid=grid, in_specs=in_specs, out_specs=out_specs, out_shape=outs,
            scratch_shapes=scratch, compiler_params=params,
        )(*ins)
        return list(res), []
    n_in, n_out, n_sc = len(ins), len(outs), len(scratch)
    h_in, h_out = len(host.ins), len(host.outs)

    def hosted(*refs):
        a = refs[:n_in]
        ha = refs[n_in:n_in + h_in]
        o = refs[n_in + h_in:n_in + h_in + n_out]
        ho = refs[n_in + h_in + n_out:n_in + h_in + n_out + h_out]
        sc = refs[n_in + h_in + n_out + h_out:n_in + h_in + n_out + h_out + n_sc]
        hs = refs[n_in + h_in + n_out + h_out + n_sc:]
        first = functools.reduce(jnp.logical_and, [pl.program_id(i) == 0 for i in range(len(grid))])
        last = functools.reduce(jnp.logical_and, [pl.program_id(i) == g - 1 for i, g in enumerate(grid)])

        @pl.when(first)
        def _():
            host.start(ha, ho, hs)

        body(*a, *o, *sc)

        @pl.when(last)
        def _():
            host.finish(ha, ho, hs)

    hbm = pl.BlockSpec(memory_space=pl.ANY)
    res = pl.pallas_call(
        hosted, name=name, grid=grid,
        in_specs=list(in_specs) + [hbm] * h_in, out_specs=list(out_specs) + [hbm] * h_out,
        out_shape=list(outs) + list(host.outs), scratch_shapes=list(scratch) + list(host.scratch),
        compiler_params=params,
    )(*ins, *host.ins)
    return list(res[:n_out]), list(res[n_out:])


def _s5_fwd(u, bm, cm, ar, ai, dvec, host=None):
    L = u.shape[0]
    nk = L // 8
    GC = S5_GB * S5_C
    nb = S5_G // S5_GB
    col, vec, avec, bmat, cmat = _s5_specs(L)

    def body(u_ref, b_ref, c_ref, ar_ref, ai_ref, d_ref, y_ref, carry_ref, st, fin, ui, yi):
        _interleave(ui, u_ref, nk)
        for r in range(8):
            rows = slice(r * nk, (r + 1) * nk)
            st[rows, :] = _dot(ui[rows, :].astype(BF16), b_ref[...])
        _full_scan(st, fin, ar_ref[...], ai_ref[...], nk, reverse=False, carry_out=carry_ref)
        for r in range(8):
            rows = slice(r * nk, (r + 1) * nk)
            yi[rows, :] = _dot_nt(st[rows, :].astype(BF16), c_ref[...]) + d_ref[...] * ui[rows, :]
        _deinterleave(y_ref, yi, nk)

    return _hosting_call(
        body, "s5_fwd", nb, host,
        [u, bm, cm, ar, ai, dvec], [col, bmat, bmat, avec, avec, vec],
        [jax.ShapeDtypeStruct(u.shape, F32), jax.ShapeDtypeStruct((nb * 8, 2 * S5_W), F32)],
        [col, pl.BlockSpec((8, 2 * S5_W), lambda g: (g, 0))],
        [pltpu.VMEM((L, 2 * S5_W), F32), pltpu.VMEM((8, 2 * S5_W), F32), pltpu.VMEM((L, GC), F32),
         pltpu.VMEM((L, GC), F32)])


def _s5_bwd(u, dy, carry, bm, cm, ar, ai, dvec, mask, rmat, host=None):
    L = u.shape[0]
    nk = L // 8
    W = S5_W
    GC = S5_GB * S5_C
    col, vec, avec, bmat, cmat = _s5_specs(L)
    hi = lax.Precision.HIGHEST

    def body(u_ref, dy_ref, carry_ref, b_ref, ct_ref, ar_ref, ai_ref, d_ref, mask_ref, r_ref,
             du_ref, db_ref, dc_ref, dd_ref, dar_ref, dai_ref, sa, sb, fin, ui, dyi, dui):
        ar = ar_ref[...]
        ai = ai_ref[...]
        _interleave(ui, u_ref, nk)
        _interleave(dyi, dy_ref, nk)
        for r in range(8):
            rows = slice(r * nk, (r + 1) * nk)
            sa[rows, :] = _dot(ui[rows, :].astype(BF16), b_ref[...])
            sb[rows, :] = _dot(dyi[rows, :].astype(BF16), ct_ref[...])
        _full_scan(sa, fin, ar, ai, nk, reverse=False, carry_in=carry_ref)
        gr, gi, accr, acci = _full_scan(sb, fin, ar, ai, nk, reverse=True, prev=sa)
        rowid = lax.broadcasted_iota(jnp.int32, (8, W), 0)
        last = pl.ds((nk - 1) * 8, 8)
        pr = jnp.where(rowid == 0, 0.0, pltpu.roll(sa[last, 0:W], 1, 0))
        pi = jnp.where(rowid == 0, 0.0, pltpu.roll(sa[last, W:2 * W], 1, 0))
        accr = accr + gr * pr + gi * pi
        acci = acci + gi * pr - gr * pi
        dar_ref[...] = jnp.sum(accr, axis=0, keepdims=True)
        dai_ref[...] = jnp.sum(acci, axis=0, keepdims=True)
        dbf = jnp.zeros((GC, 2 * W), F32)
        dcf = jnp.zeros((GC, 2 * W), F32)
        dd = jnp.zeros((1, GC), F32)
        for r in range(8):
            rows = slice(r * nk, (r + 1) * nk)
            ub = ui[rows, :]
            dyb = dyi[rows, :]
            gb = sb[rows, :].astype(BF16)
            dui[rows, :] = _dot_nt(gb, b_ref[...]) + d_ref[...] * dyb
            dbf = dbf + _dot_tn(ub.astype(BF16), gb)
            dcf = dcf + _dot_tn(dyb.astype(BF16), sa[rows, :].astype(BF16))
            dd = dd + jnp.sum(dyb * ub, axis=0, keepdims=True)
        db_ref[...] = jnp.dot(dbf * mask_ref[...], r_ref[...], precision=hi, preferred_element_type=F32)
        dc_ref[...] = jnp.dot(dcf * mask_ref[...], r_ref[...], precision=hi, preferred_element_type=F32)
        dd_ref[...] = dd
        _deinterleave(du_ref, dui, nk)

    cmp_spec = pl.BlockSpec((GC, 2 * S5_P), lambda g: (g, 0))
    whole = lambda shape: pl.BlockSpec(shape, lambda g: (0, 0))
    sd = jax.ShapeDtypeStruct
    return _hosting_call(
        body, "s5_bwd", S5_G // S5_GB, host,
        [u, dy, carry, bm, cm, ar, ai, dvec, mask, rmat],
        [col, col, pl.BlockSpec((8, 2 * W), lambda g: (g, 0)), bmat, bmat, avec, avec, vec, whole(mask.shape),
         whole(rmat.shape)],
        [sd(u.shape, BF16), sd((S5_G * S5_C, 2 * S5_P), F32), sd((S5_G * S5_C, 2 * S5_P), F32),
         sd((1, PRIM), F32), sd((1, S5_G * S5_P), F32), sd((1, S5_G * S5_P), F32)],
        [col, cmp_spec, cmp_spec, vec, avec, avec],
        [pltpu.VMEM((L, 2 * W), F32), pltpu.VMEM((L, 2 * W), F32), pltpu.VMEM((8, 2 * W), F32),
         pltpu.VMEM((L, GC), F32), pltpu.VMEM((L, GC), F32), pltpu.VMEM((L, GC), F32)])


def _s5_compact_consts():
    g_row = np.arange(S5_GB * S5_C) // S5_C
    col = np.arange(2 * S5_W)
    g_col = (col % S5_W) // S5_P
    mask = (g_row[:, None] == g_col[None, :]).astype(np.float32)
    tgt = (col // S5_W) * S5_P + col % S5_P
    rmat = (tgt[:, None] == np.arange(2 * S5_P)[None, :]).astype(np.float32)
    return jnp.asarray(mask), jnp.asarray(rmat)


def _attn_scores(q_ref, k_ref, qb, bq, scale):
    ext = (qb + 1) * bq
    s = _dot_nt(q_ref[qb * bq:ext, :], k_ref[0:ext, :]) * scale
    qpos = lax.broadcasted_iota(jnp.int32, (bq, bq), 0)
    kpos = lax.broadcasted_iota(jnp.int32, (bq, bq), 1)
    diag = jnp.where(kpos <= qpos, s[:, ext - bq:], NEG)
    return diag if qb == 0 else jnp.concatenate([s[:, :ext - bq], diag], axis=-1)


def _attn_fwd(qp, kp, v, scale):
    L = qp.shape[0]
    bq = min(256, L)

    def body(q_ref, k_ref, v_ref, o_ref, lse_ref):
        for qb in range(L // bq):
            rows = slice(qb * bq, (qb + 1) * bq)
            s = _attn_scores(q_ref, k_ref, qb, bq, scale)
            m = jnp.max(s, axis=-1, keepdims=True)
            e = jnp.exp(s - m)
            l = jnp.sum(e, axis=-1, keepdims=True)
            o_ref[rows, :] = _dot(e.astype(BF16), v_ref[0:(qb + 1) * bq, :]) / l
            lse_ref[rows, :] = jnp.broadcast_to(m + jnp.log(l), (bq, HD))

    blk = pl.BlockSpec((L, HD), lambda h: (0, h))
    wide = pl.BlockSpec((L, 2 * HD), lambda h: (0, h))
    return pl.pallas_call(
        body, name="mla_attn_fwd", grid=(MLA_H,),
        in_specs=[wide, wide, blk], out_specs=[blk, blk],
        out_shape=[jax.ShapeDtypeStruct((L, MLA_H * HD), F32)] * 2,
        compiler_params=pltpu.CompilerParams(dimension_semantics=("arbitrary",), vmem_limit_bytes=VMEM_LIMIT),
    )(qp, kp, v)


def _attn_bwd(qp, kp, v, o, lse, do, scale):
    L = qp.shape[0]
    bq = min(256, L)
    nq = L // bq

    def body(q_ref, k_ref, v_ref, o_ref, lse_ref, do_ref, dq_ref, dk_ref, dv_ref, dk_acc, dv_acc):
        dk_acc[...] = jnp.zeros_like(dk_acc)
        dv_acc[...] = jnp.zeros_like(dv_acc)
        for qb in range(nq):
            rows = slice(qb * bq, (qb + 1) * bq)
            ext = (qb + 1) * bq
            do = do_ref[rows, :]
            dob = do.astype(BF16)
            p = jnp.exp(_attn_scores(q_ref, k_ref, qb, bq, scale) - lse_ref[rows, 0:1])
            dp = _dot_nt(dob, v_ref[0:ext, :])
            dsum = jnp.sum(do * o_ref[rows, :], axis=-1, keepdims=True)
            ds = (p * (dp - dsum) * scale).astype(BF16)
            dq_ref[rows, :] = _dot(ds, k_ref[0:ext, :]).astype(dq_ref.dtype)
            dk_acc[0:ext, :] += _dot_tn(ds, q_ref[rows, :])
            dv_acc[0:ext, :] += _dot_tn(p.astype(BF16), dob)
        dk_ref[...] = dk_acc[...].astype(dk_ref.dtype)
        dv_ref[...] = dv_acc[...].astype(dv_ref.dtype)

    sd = jax.ShapeDtypeStruct
    blk = pl.BlockSpec((L, HD), lambda h: (0, h))
    wide = pl.BlockSpec((L, 2 * HD), lambda h: (0, h))
    return pl.pallas_call(
        body, name="mla_attn_bwd", grid=(MLA_H,),
        in_specs=[wide, wide, blk, blk, blk, blk], out_specs=[wide, wide, blk],
        out_shape=[sd((L, MLA_H * 2 * HD), BF16), sd((L, MLA_H * 2 * HD), BF16), sd((L, MLA_H * HD), BF16)],
        scratch_shapes=[pltpu.VMEM((L, 2 * HD), F32), pltpu.VMEM((L, HD), F32)],
        compiler_params=pltpu.CompilerParams(dimension_semantics=("arbitrary",), vmem_limit_bytes=VMEM_LIMIT),
    )(qp, kp, v, o, lse, do)


def _kv_fn(mem, gm, w, gk):
    kv = _mm(_rms(mem, gm, D_MODEL), w)
    k = jnp.concatenate([_rms(kv[:, HD * h:HD * (h + 1)], gk, HD) for h in range(X_HEADS)], axis=-1)
    return k, kv[:, XQ:]


def _kv_prep(mem, gm, w, gk, name):
    def fn(mem, gm, w, gk):
        return _kv_fn(mem, gm, w, gk)
    M = mem.shape[0]
    return _rowwise(name, fn, [('c', mem), ('c', gm), ('c', w), ('c', gk)],
                    [('c', (M, XQ), F32), ('c', (M, XQ), F32)], 1)


def _kv_prep_bwd(mem, gm, w, gk, dk, dv, name):
    def fn(mem, gm, w, gk, dk, dv):
        _, vjp = jax.vjp(lambda a, b, c: _kv_fn(mem, a, b, c), gm, w, gk)
        return vjp((dk, dv))
    return _rowwise(name, fn, [('c', mem), ('c', gm), ('c', w), ('c', gk), ('c', dk), ('c', dv)],
                    [('c', gm.shape, F32), ('c', w.shape, BF16), ('c', gk.shape, F32)], 1)


def _forward_merge(x, mix, mix_kind, xq, gate, k, v, gq, wout, name, nblk, sub, host=None):
    def fn(x, mix, xq, gate, k, v, gq, wout):
        o = _merge(mix, xq, gate, k, v, gq)
        return (x + _dot(o.astype(BF16), wout),)
    L = x.shape[0]
    out = _rowwise(name, fn, [('r', x), (mix_kind, mix), ('r', xq), ('r', gate), ('c', k), ('c', v), ('c', gq),
                              ('c', wout)], [('r', (L, D_MODEL), F32)], nblk, sub, host=host)
    return out[0] if host is None else (out[0][0], out[1])


def _backward_merge(dx, mix, mix_kind, xq, gate, k, v, gq, wout, name, nblk, sub, host=None):
    def fn(dx, mix, xq, gate, k, v, gq, wout):
        g16 = dx.astype(BF16)
        do = _dot_nt(g16, wout)
        o, vjp = jax.vjp(_merge, mix, xq, gate, k, v, gq)
        dmix, dxq, dgate, dk, dv, dgq = vjp(do)
        return dmix, dxq, dgate, o, g16, dk, dv, dgq
    L = dx.shape[0]
    return _rowwise(
        name, fn,
        [('r', dx), (mix_kind, mix), ('r', xq), ('r', gate), ('c', k), ('c', v), ('c', gq), ('c', wout)],
        [('r', (L, PRIM), F32), ('r', (L, XQ), BF16), ('r', (L, BRANCH), BF16), ('t', (BRANCH, L), BF16),
         ('r', (L, D_MODEL), BF16), ('a', k.shape, F32), ('a', v.shape, F32), ('a', gq.shape, F32)], nblk, sub,
        host=host)


_MLA_IN = 3392
_MLA_IN_PAD = 3456


def _uq_rows(wt):
    r = wt.reshape(MLA_H, HD + ROPE, wt.shape[1])
    return jnp.concatenate([r[:, :HD].reshape(PRIM, -1),
                            jnp.pad(r[:, HD:], ((0, 0), (0, HD - ROPE), (0, 0))).reshape(PRIM, -1)], axis=0)


def _uq_rows_back(wt):
    nope = wt[:PRIM].reshape(MLA_H, HD, -1)
    rope = wt[PRIM:].reshape(MLA_H, HD, -1)[:, :ROPE]
    return jnp.concatenate([nope, rope], axis=1).reshape(MLA_H * (HD + ROPE), -1)


def _mla_in_rows(wt):
    return jnp.concatenate([wt[:768], wt[832:], wt[768:832], jnp.zeros((64, wt.shape[1]), wt.dtype)], axis=0)


def _mla_in_rows_back(wt):
    return jnp.concatenate([wt[:768], wt[3328:3392], wt[768:3328]], axis=0)


_SMALL = (("ln_gain", 2048), ("mem_norm", 2048), ("xq_norm", 256), ("xk_norm", 256), ("s5_lambda_re", 6144),
          ("s5_lambda_im", 6144), ("s5_log_step", 96), ("s5_b_re", 98304), ("s5_b_im", 98304), ("s5_c_re", 98304),
          ("s5_c_im", 98304), ("s5_d", 1536), ("mla_q_lora_norm", 512), ("mla_kv_lora_norm", 256),
          ("mla_q_nope_norm", 128), ("mla_k_nope_norm", 128), ("mla_q_rope_norm", 64), ("mla_k_rope_norm", 64))
_SMALL_ROWS = 432
_SMALL_OFF = {name: sum(n for _, n in _SMALL[:i]) for i, (name, _) in enumerate(_SMALL)}


def _pack_small(d):
    flat = jnp.concatenate([d[n].reshape(-1).astype(F32) for n, _ in _SMALL])
    return jnp.pad(flat, (0, _SMALL_ROWS * 1024 - flat.shape[0])).reshape(_SMALL_ROWS, 1024)


def _unpack_small(p, name, shape):
    off = _SMALL_OFF[name]
    return p.reshape(-1)[off:off + int(np.prod(shape))].reshape(shape)


_WEIGHTS = ('ln_gain', 'w_out', 'mem_norm', 'w_mem_kv', 'xq_norm', 'xk_norm', 's5_w_in', 's5_lambda_re',
            's5_lambda_im', 's5_log_step', 's5_b_re', 's5_b_im', 's5_c_re', 's5_c_im', 's5_d', 's5_w_glu', 'mla_w_in',
            'mla_q_lora_norm', 'mla_kv_lora_norm', 'mla_w_uq', 'mla_w_ukv', 'mla_q_nope_norm', 'mla_k_nope_norm',
            'mla_q_rope_norm', 'mla_k_rope_norm')
_BIG = ('w_out', 'w_mem_kv', 's5_w_in', 's5_w_glu', 'mla_w_in', 'mla_w_uq', 'mla_w_ukv')


def _pad128(g):
    return jnp.pad(g.reshape(1, -1), ((0, 0), (0, HD - g.shape[-1])))


def kernel(x, mem, positions, ln_gain, w_out, mem_norm, w_mem_kv, xq_norm, xk_norm, s5_w_in, s5_lambda_re, s5_lambda_im, s5_log_step, s5_b_re, s5_b_im, s5_c_re, s5_c_im, s5_d, s5_w_glu, mla_w_in, mla_q_lora_norm, mla_kv_lora_norm, mla_w_uq, mla_w_ukv, mla_q_nope_norm, mla_k_nope_norm, mla_q_rope_norm, mla_k_rope_norm, loss_target, m_ln_gain, m_w_out, m_mem_norm, m_w_mem_kv, m_xq_norm, m_xk_norm, m_s5_w_in, m_s5_lambda_re, m_s5_lambda_im, m_s5_log_step, m_s5_b_re, m_s5_b_im, m_s5_c_re, m_s5_c_im, m_s5_d, m_s5_w_glu, m_mla_w_in, m_mla_q_lora_norm, m_mla_kv_lora_norm, m_mla_w_uq, m_mla_w_ukv, m_mla_q_nope_norm, m_mla_k_nope_norm, m_mla_q_rope_norm, m_mla_k_rope_norm, v_ln_gain, v_w_out, v_mem_norm, v_w_mem_kv, v_xq_norm, v_xk_norm, v_s5_w_in, v_s5_lambda_re, v_s5_lambda_im, v_s5_log_step, v_s5_b_re, v_s5_b_im, v_s5_c_re, v_s5_c_im, v_s5_d, v_s5_w_glu, v_mla_w_in, v_mla_q_lora_norm, v_mla_kv_lora_norm, v_mla_w_uq, v_mla_w_ukv, v_mla_q_nope_norm, v_mla_k_nope_norm, v_mla_q_rope_norm, v_mla_k_rope_norm):
    weights = dict(ln_gain=ln_gain, w_out=w_out, mem_norm=mem_norm, w_mem_kv=w_mem_kv, xq_norm=xq_norm,
                   xk_norm=xk_norm, s5_w_in=s5_w_in, s5_lambda_re=s5_lambda_re, s5_lambda_im=s5_lambda_im,
                   s5_log_step=s5_log_step, s5_b_re=s5_b_re, s5_b_im=s5_b_im, s5_c_re=s5_c_re, s5_c_im=s5_c_im,
                   s5_d=s5_d, s5_w_glu=s5_w_glu, mla_w_in=mla_w_in, mla_q_lora_norm=mla_q_lora_norm,
                   mla_kv_lora_norm=mla_kv_lora_norm, mla_w_uq=mla_w_uq, mla_w_ukv=mla_w_ukv,
                   mla_q_nope_norm=mla_q_nope_norm, mla_k_nope_norm=mla_k_nope_norm,
                   mla_q_rope_norm=mla_q_rope_norm, mla_k_rope_norm=mla_k_rope_norm)
    m_in = dict(zip(_WEIGHTS, (m_ln_gain, m_w_out, m_mem_norm, m_w_mem_kv, m_xq_norm, m_xk_norm, m_s5_w_in,
                               m_s5_lambda_re, m_s5_lambda_im, m_s5_log_step, m_s5_b_re, m_s5_b_im, m_s5_c_re,
                               m_s5_c_im, m_s5_d, m_s5_w_glu, m_mla_w_in, m_mla_q_lora_norm, m_mla_kv_lora_norm,
                               m_mla_w_uq, m_mla_w_ukv, m_mla_q_nope_norm, m_mla_k_nope_norm, m_mla_q_rope_norm,
                               m_mla_k_rope_norm)))
    v_in = dict(zip(_WEIGHTS, (v_ln_gain, v_w_out, v_mem_norm, v_w_mem_kv, v_xq_norm, v_xk_norm, v_s5_w_in,
                               v_s5_lambda_re, v_s5_lambda_im, v_s5_log_step, v_s5_b_re, v_s5_b_im, v_s5_c_re,
                               v_s5_c_im, v_s5_d, v_s5_w_glu, v_mla_w_in, v_mla_q_lora_norm, v_mla_kv_lora_norm,
                               v_mla_w_uq, v_mla_w_ukv, v_mla_q_nope_norm, v_mla_k_nope_norm, v_mla_q_rope_norm,
                               v_mla_k_rope_norm)))

    x0 = x[0]
    mem0 = mem[0]
    target = loss_target[0]
    L = x0.shape[0]
    nblk, sub = 4, 1
    nb_big = 8
    me = 4 * lax.axis_index("x") + 2 * lax.axis_index("y") + lax.axis_index("c")

    lora = jnp.pad(jnp.concatenate([mla_q_lora_norm, mla_kv_lora_norm], axis=1), ((0, 7), (0, HD - 96)))
    def gather(*shards):
        return _plan_all_gather(list(shards))

    kh = D_MODEL // 2
    (b_mkv0, b_glu, b_in_mla, b_out0, b_uq, b_ukv, b_mkv1, b_out1), (W_in_s5,) = _cast_call(
        [w_mem_kv[0], s5_w_glu[0], jnp.transpose(mla_w_in[0]), w_out[0], jnp.transpose(mla_w_uq[0]), mla_w_ukv[0],
         w_mem_kv[1], w_out[1]], "cast_shards", host=gather(s5_w_in[0].astype(BF16)))

    ln0, ln1 = ln_gain[0:1], ln_gain[1:2]
    gq0, gq1 = xq_norm[0:1], xq_norm[1:2]
    gk0, gk1 = xk_norm[0:1], xk_norm[1:2]
    gm0, gm1 = mem_norm[0:1], mem_norm[1:2]
    gqn, gkn = mla_q_nope_norm, mla_k_nope_norm
    gqr, gkr = _pad128(mla_q_rope_norm), _pad128(mla_k_rope_norm)

    lr3 = s5_lambda_re.reshape(S5_G, 1, S5_P)
    li3 = s5_lambda_im.reshape(S5_G, 1, S5_P)
    ls3 = s5_log_step.reshape(S5_G, 1, 1)
    btr = jnp.swapaxes(s5_b_re[0], 1, 2)
    bti = jnp.swapaxes(s5_b_im[0], 1, 2)
    a_r, a_i, bm, cm = _s5_params(lr3, li3, ls3, btr, bti, s5_c_re[0], s5_c_im[0])
    a_r2 = a_r.reshape(1, S5_G * S5_P)
    a_i2 = a_i.reshape(1, S5_G * S5_P)
    cmask, rmat = _s5_compact_consts()

    half = ROPE // 2
    inv_freq = ROPE_THETA ** (-jnp.arange(half, dtype=F32) / half)
    invf = jnp.concatenate([inv_freq, inv_freq, jnp.zeros((HD - ROPE,), F32)]).reshape(1, HD)

    def rot_tables(pos, invf):
        ang = pos.astype(F32) * invf
        lane = lax.broadcasted_iota(jnp.int32, ang.shape, 1)
        c = jnp.where(lane < ROPE, jnp.cos(ang), 0.0)
        s = jnp.sin(ang)
        return c, jnp.where(lane < half, -s, 0.0), jnp.where((lane >= half) & (lane < ROPE), s, 0.0)

    tc, ts1, ts2 = _rowwise("rot_tables", rot_tables, [('r', positions.reshape(L, 1)), ('c', invf)],
                            [('r', (L, HD), F32)] * 3, nblk, sub)

    def in_s5(x, g, w):
        proj = _mm_slots(_rms(x, g, D_MODEL).astype(BF16), w)
        return proj[:, :PRIM], proj[:, PRIM:PRIM + XQ], proj[:, PRIM + XQ:]

    (u_s5, xq_a, gate_a), (G_mkv0,) = _rowwise(
        "s5_in", in_s5, [('r', x0), ('c', ln0), ('c', W_in_s5)],
        [('r', (L, PRIM), F32), ('r', (L, XQ), F32), ('r', (L, BRANCH), F32)], nblk, sub, host=gather(b_mkv0))
    (y_s5, s5_carry), (W_glu, G_in_mla_a) = _s5_fwd(u_s5, bm, cm, a_r2, a_i2, s5_d,
                                                    host=gather(b_glu, b_in_mla[:, :kh]))

    def glu(y, w):
        z = _mm_slots(_gelu(y).astype(BF16), w)
        return (z[:, :PRIM] * _sigmoid(z[:, PRIM:]),)

    (y2,), (G_out0,) = _rowwise("s5_glu", glu, [('r', y_s5), ('c', W_glu)], [('r', (L, PRIM), F32)], nblk, sub,
                                host=gather(b_out0))
    W_mkv0 = G_mkv0.reshape(D_MODEL, 2 * XQ)
    k_a, v_a = _kv_prep(mem0, gm0, W_mkv0, gk0, "kv_prep0")
    x1, (G_in_mla_b,) = _forward_merge(
        x0, y2, 'r', xq_a, gate_a, k_a, v_a, gq0, G_out0.reshape(BRANCH, D_MODEL), "merge0", nblk, sub,
        host=gather(b_in_mla[:, kh:]))
    W_in_mla = _mla_in_rows(jnp.concatenate([G_in_mla_a, G_in_mla_b], axis=2).reshape(_MLA_IN, D_MODEL))

    def in_mla(x, g, w):
        proj = _dot_nt(_rms(x, g, D_MODEL).astype(BF16), w)
        return proj[:, :512], proj[:, 512:768], proj[:, 768:1280], proj[:, 1280:3328], proj[:, 3328:]

    (c_q, c_kv, xq_b, gate_b, krp), (G_uq, W_kv, G_lora) = _rowwise(
        "mla_in", in_mla, [('r', x1), ('c', ln1), ('c', W_in_mla)],
        [('r', (L, Q_LORA), F32), ('r', (L, KV_LORA), F32), ('r', (L, XQ), F32), ('r', (L, BRANCH), F32),
         ('r', (L, HD), F32)], nblk, sub,
        host=gather(b_uq, b_ukv, lora))
    W_q = _uq_rows(G_uq.reshape(MLA_H * (HD + ROPE), Q_LORA))
    g_qlora = G_lora[:, 0, :64].reshape(1, Q_LORA)
    g_kvlora = G_lora[:, 0, 64:96].reshape(1, KV_LORA)

    def qkv(c_q, c_kv, krp, tc, ts1, ts2, gql, gkvl, wq, wkv, gqn, gkn, gqr, gkr):
        q = _dot_nt(_rms(c_q, gql, Q_LORA).astype(BF16), wq)
        kv = _mm_slots(_rms(c_kv, gkvl, KV_LORA).astype(BF16), wkv)
        kp, v = _kv_post(kv, krp, gkn, gkr, tc, ts1, ts2)
        return _q_post(q, gqn, gqr, tc, ts1, ts2), kp, v

    qkv_consts = [('c', g_qlora), ('c', g_kvlora), ('c', W_q), ('c', W_kv), ('c', gqn), ('c', gkn), ('c', gqr),
                  ('c', gkr)]
    (q_pad, k_pad, v_h), (G_mkv1, G_out1) = _rowwise(
        "mla_qkv", qkv, [('r', c_q), ('r', c_kv), ('r', krp), ('r', tc), ('r', ts1), ('r', ts2)] + qkv_consts,
        [('r', (L, 2 * PRIM), BF16), ('r', (L, 2 * PRIM), BF16), ('r', (L, PRIM), BF16)], nblk, sub,
        host=gather(b_mkv1, b_out1))
    W_out = (G_out0.reshape(BRANCH, D_MODEL), G_out1.reshape(BRANCH, D_MODEL))
    W_mkv = (W_mkv0, G_mkv1.reshape(D_MODEL, 2 * XQ))
    scale = (HD + ROPE) ** -0.5
    attn, lse = _attn_fwd(q_pad, k_pad, v_h, scale)
    k_b, v_b = _kv_prep(mem0, gm1, W_mkv[1], gk1, "kv_prep1")

    def merge_loss(x, mix, xq, gate, k, v, gq, wout, t):
        err = x + _dot(_merge(mix, xq, gate, k, v, gq).astype(BF16), wout) - t
        part = 0.5 * jnp.sum(jnp.sum(err * err, axis=-1, keepdims=True) * (1.0 / D_MODEL), axis=0, keepdims=True)
        return err * (1.0 / D_MODEL), jnp.broadcast_to(part, (1, HD))

    dx2, loss_part = _rowwise(
        "merge1_loss", merge_loss,
        [('r', x1), ('r', attn), ('r', xq_b), ('r', gate_b), ('c', k_b), ('c', v_b), ('c', gq1), ('c', W_out[1]),
         ('r', target)], [('r', (L, D_MODEL), F32), ('a', (1, HD), F32)], nblk, sub)

    dattn, dxq_b, dgate_b, o_b, g_b, dk_b, dv_b, dgq1 = _backward_merge(
        dx2, attn, 'r', xq_b, gate_b, k_b, v_b, gq1, W_out[1], "merge1_bwd", nb_big, sub)
    dgm1, dW_mkv1, dgk1 = _kv_prep_bwd(mem0, gm1, W_mkv[1], gk1, dk_b, dv_b, "kv_prep1_bwd")
    dW_out1 = _matmul_tn(o_b, g_b, "dw_out1")
    dq_pad, dk_pad, dv_h = _attn_bwd(q_pad, k_pad, v_h, attn, lse, dattn, scale)

    def qkv_bwd(c_q, c_kv, krp, tc, ts1, ts2, dqp, dkp, dv, gql, gkvl, wq, wkv, gqn, gkn, gqr, gkr):
        cqn, vjp_qn = jax.vjp(lambda a, b: _rms(a, b, Q_LORA), c_q, gql)
        ckvn, vjp_kvn = jax.vjp(lambda a, b: _rms(a, b, KV_LORA), c_kv, gkvl)
        cqn16 = cqn.astype(BF16)
        ckvn16 = ckvn.astype(BF16)
        q = _dot_nt(cqn16, wq)
        kv = _mm_slots(ckvn16, wkv)
        _, vjp_q = jax.vjp(lambda a, b, c: _q_post(a, b, c, tc, ts1, ts2), q, gqn, gqr)
        dq, dgqn, dgqr = vjp_q(dqp.astype(F32))
        _, vjp_kv = jax.vjp(lambda a, b, c, d: _kv_post(a, b, c, d, tc, ts1, ts2), kv, krp, gkn, gkr)
        dkv, dkrp, dgkn, dgkr = vjp_kv((dkp.astype(F32), dv.astype(F32)))
        dq16 = dq.astype(BF16)
        dkv16 = dkv.astype(BF16)
        dc_q, dgql = vjp_qn(_dot(dq16, wq))
        dc_kv, dgkvl = vjp_kvn(_mm_slots_nt(dkv16, wkv))
        return dc_q, dc_kv, dkrp, cqn16, dq16, ckvn16, dkv16, dgql, dgkvl, dgqn, dgkn, dgqr, dgkr

    (dc_q, dc_kv, dkrp, cqn16, dq16, ckvn16, dkv16, dgql, dgkvl, dgqn, dgkn, dgqr, dgkr) = _rowwise(
        "mla_qkv_bwd", qkv_bwd,
        [('r', c_q), ('r', c_kv), ('r', krp), ('r', tc), ('r', ts1), ('r', ts2), ('r', dq_pad), ('r', dk_pad),
         ('r', dv_h)] + qkv_consts,
        [('r', (L, Q_LORA), BF16), ('r', (L, KV_LORA), BF16), ('r', (L, HD), BF16), ('r', (L, Q_LORA), BF16),
         ('t', (2 * PRIM, L), BF16), ('t', (KV_LORA, L), BF16), ('r', (L, 2 * PRIM), BF16),
         ('a', (1, Q_LORA), F32), ('a', (1, KV_LORA), F32), ('a', (1, HD), F32), ('a', (1, HD), F32),
         ('a', (1, HD), F32), ('a', (1, HD), F32)], nb_big, sub)
    dW_q = _matmul_tn(dq16, cqn16, "dw_uq")
    dW_kv = _matmul_tn_slots(ckvn16, dkv16, "dw_ukv")

    def in_bwd(x, dres, g, w, *dparts):
        dproj = jnp.concatenate(dparts, axis=-1).astype(BF16)
        xn, vjp = jax.vjp(lambda a, b: _rms(a, b, D_MODEL), x, g)
        dx, dg = vjp(_mm_slots_nt(dproj, w) if w.ndim == 3 else _dot(dproj, w))
        return dx + dres, xn, dproj, dg

    dx1, xn1, dproj1, dln1 = _rowwise(
        "mla_in_bwd", in_bwd,
        [('r', x1), ('r', dx2), ('c', ln1), ('c', W_in_mla), ('r', dc_q), ('r', dc_kv), ('r', dxq_b), ('r', dgate_b),
         ('r', dkrp)],
        [('r', (L, D_MODEL), F32), ('r', (L, D_MODEL), BF16), ('t', (_MLA_IN_PAD, L), BF16), ('a', (1, D_MODEL), F32)],
        nblk, sub)
    dW_in_mla = _matmul_tn(dproj1, xn1, "dw_mla_in")

    grads1 = [dW_out1.reshape(N_DEV, 256, D_MODEL), dW_mkv1.reshape(N_DEV, 128, 2 * XQ),
              _mla_in_rows_back(dW_in_mla).reshape(N_DEV, 424, D_MODEL),
              _uq_rows_back(dW_q).reshape(N_DEV, 288, Q_LORA), dW_kv]
    (dy2, dxq_a, dgate_a, o_a, g_a, dk_a, dv_a, dgq0), pair1 = _backward_merge(
        dx1, y2, 'r', xq_a, gate_a, k_a, v_a, gq0, W_out[0], "merge0_bwd", nb_big, sub, host=_plan_pair(grads1))
    dgm0, dW_mkv0, dgk0 = _kv_prep_bwd(mem0, gm0, W_mkv[0], gk0, dk_a, dv_a, "kv_prep0_bwd")
    dW_out0 = _matmul_tn(o_a, g_a, "dw_out0")
    t1 = list(_pair_add(grads1, pair1, "rs_add_layer1"))

    def glu_bwd(y, dy2, w):
        h, vjp_h = jax.vjp(_gelu, y)
        h16 = h.astype(BF16)
        z = _mm_slots(h16, w)
        _, vjp_z = jax.vjp(lambda z: z[:, :PRIM] * _sigmoid(z[:, PRIM:]), z)
        dz16 = vjp_z(dy2)[0].astype(BF16)
        return vjp_h(_mm_slots_nt(dz16, w))[0], h16, dz16

    grads0 = [dW_out0.reshape(N_DEV, 256, D_MODEL), dW_mkv0.reshape(N_DEV, 128, 2 * XQ)]
    (dy_s5, h16, dz16), glu_hosted = _rowwise(
        "s5_glu_bwd", glu_bwd, [('r', y_s5), ('r', dy2), ('c', W_glu)],
        [('r', (L, PRIM), F32), ('t', (PRIM, L), BF16), ('r', (L, 2 * PRIM), BF16)], nb_big, sub,
        host=_combine(_plan_chips(t1[2:]), _plan_pair(grads0)))
    recv_proj1, pair0 = glu_hosted[:3], glu_hosted[3:]
    dW_glu = _matmul_tn_slots(h16, dz16, "dw_glu")
    t0 = list(_pair_add(grads0 + [dW_glu], pair0 + list(_exchange_call(_plan_pair([dW_glu]), "rs_pair_glu")),
                        "rs_add_layer0"))
    (du_s5, dbc, dcc, dd, dar, dai), recv_rest = _s5_bwd(u_s5, dy_s5, s5_carry, bm, cm, a_r2, a_i2, s5_d,
                                                        cmask, rmat, host=_plan_chips(t1[:2] + t0))
    early_recv = recv_rest[:2] + recv_proj1 + recv_rest[2:]
    dbc4 = dbc.reshape(S5_G, S5_C, 2, S5_P)
    dcc4 = dcc.reshape(S5_G, S5_C, 2, S5_P)
    dlr, dli, dls, dbtr, dbti = _s5_params_bwd(
        lr3, li3, ls3, btr, bti, dar.reshape(S5_G, 1, S5_P), dai.reshape(S5_G, 1, S5_P), dbc4[:, :, 0], dbc4[:, :, 1])

    small_part = {
        "ln_gain": jnp.concatenate([jnp.zeros_like(dln1), dln1]), "mem_norm": jnp.concatenate([dgm0, dgm1]),
        "xq_norm": jnp.concatenate([dgq0, dgq1]), "xk_norm": jnp.concatenate([dgk0, dgk1]),
        "s5_lambda_re": dlr, "s5_lambda_im": dli, "s5_log_step": dls,
        "s5_b_re": jnp.swapaxes(dbtr, 1, 2), "s5_b_im": jnp.swapaxes(dbti, 1, 2),
        "s5_c_re": dcc4[:, :, 0], "s5_c_im": -dcc4[:, :, 1], "s5_d": dd,
        "mla_q_lora_norm": dgql, "mla_kv_lora_norm": dgkvl, "mla_q_nope_norm": dgqn, "mla_k_nope_norm": dgkn,
        "mla_q_rope_norm": dgqr[:, :ROPE], "mla_k_rope_norm": dgkr[:, :ROPE],
    }
    loss8 = jnp.pad(loss_part, ((0, 7), (0, 0)))
    (dx0, xn0, dproj0, dln0), (small_gath, loss_g) = _rowwise(
        "s5_in_bwd", in_bwd,
        [('r', x0), ('r', dx1), ('c', ln0), ('c', W_in_s5), ('r', du_s5), ('r', dxq_a),
         ('r', dgate_a)],
        [('r', (L, D_MODEL), F32), ('t', (D_MODEL, L), BF16), ('r', (L, 2 * BRANCH), BF16), ('a', (1, D_MODEL), F32)],
        nblk, sub, host=_plan_all_gather([_pack_small(small_part).astype(BF16), loss8]))
    dW_in_s5 = _matmul_tn_slots(xn0, dproj0, "dw_s5_in")

    late = [dW_in_s5]
    late_t = _pair_add(late, list(_exchange_call(_plan_pair(late), "rs_pair_late")), "rs_add_late")
    owners = [("w_out", 1), ("w_mem_kv", 1), ("mla_w_in", 0), ("mla_w_uq", 0), ("mla_w_ukv", 0), ("w_out", 0),
              ("w_mem_kv", 0), ("s5_w_glu", 0)]
    flipped = ("mla_w_in", "mla_w_uq")

    def shard(d, n, i):
        return jnp.transpose(d[n][i]) if n in flipped else d[n][i]

    upd, (late_recv, ln0_gath) = _updates_call(
        early_recv, [shard(weights, n, i) for n, i in owners], [shard(m_in, n, i) for n, i in owners],
        [shard(v_in, n, i) for n, i in owners], "update_early",
        host=_combine(_plan_chips(late_t), _plan_all_gather([jnp.pad(dln0, ((0, 7), (0, 0)))])))
    owners.append(("s5_w_in", 0))
    upd.append(_sum_adamw(late_recv, s5_w_in[0], m_s5_w_in[0], v_s5_w_in[0], "update_s5_w_in"))
    grads, delta, new_m, new_v = {}, {}, {}, {}
    for n in _BIG:
        parts = [u for u, (o, _) in sorted(zip(upd, owners), key=lambda t: t[1][1]) if o == n]
        if n in flipped:
            grads[n], delta[n], new_m[n], new_v[n] = (jnp.transpose(parts[0][j])[None] for j in range(4))
        else:
            grads[n], delta[n], new_m[n], new_v[n] = (jnp.stack([p[j] for p in parts]) for j in range(4))

    gs, loss_sum = _small_sum(small_gath, loss_g, ln0_gath, "small_sum")
    loss = loss_sum[0, 0]
    for n, _ in _SMALL:
        shape = weights[n].shape
        if n == "mla_q_lora_norm":
            grads[n] = lax.dynamic_slice(_unpack_small(gs, n, (Q_LORA,)), (me * 64,), (64,)).reshape(shape)
        elif n == "mla_kv_lora_norm":
            grads[n] = lax.dynamic_slice(_unpack_small(gs, n, (KV_LORA,)), (me * 32,), (32,)).reshape(shape)
        else:
            grads[n] = _unpack_small(gs, n, shape)

    def own(n, a):
        if a.ndim == 4:
            a = jnp.transpose(a, (0, 2, 3, 1))
        elif a.ndim == 3:
            a = jnp.transpose(a, (0, 2, 1))
        return a.reshape(a.shape[1:]) if a.ndim >= 3 else a

    def back(n, a):
        shape = weights[n].shape
        if len(shape) == 4:
            return jnp.transpose(a.reshape((1,) + a.shape), (0, 3, 1, 2))
        if len(shape) == 3:
            return jnp.transpose(a.reshape((1,) + a.shape), (0, 2, 1))
        return a.reshape(shape)

    wide = ("s5_b_re", "s5_b_im", "s5_c_re", "s5_c_im")
    for names, nb, call in (([n for n, _ in _SMALL if n not in wide], 1, "update_small"), (wide, 4, "update_s5_bc")):
        res = _adamw_multi([own(n, weights[n]) for n in names], [own(n, grads[n]) for n in names],
                           [own(n, m_in[n]) for n in names], [own(n, v_in[n]) for n in names], call, nb)
        for n, (dl, m2, v2) in zip(names, res):
            delta[n], new_m[n], new_v[n] = back(n, dl), back(n, m2), back(n, v2)
    return (loss, dx0[None], *[grads[n] for n in _WEIGHTS], *[delta[n] for n in _WEIGHTS],
            *[new_m[n] for n in _WEIGHTS], *[new_v[n] for n in _WEIGHTS])
```

```python
import functools
import math

import numpy as np
import jax
import jax.numpy as jnp
from jax import lax
from jax.experimental import pallas as pl
from jax.experimental.pallas import tpu as pltpu

F32 = jnp.float32
BF16 = jnp.bfloat16
EPS = 1e-6
NEG = float(np.finfo(np.float32).min)
MESH = pl.DeviceIdType.MESH

N_DEV = 8
D_MODEL = 1024
MEM_LEN = 256
XQ = 512
PRIM = 1536
BRANCH = 2048
X_HEADS = 4
HD = 128
S5_G = 96
S5_P = 64
S5_C = 16
S5_GB = 8
S5_W = S5_GB * S5_P
MLA_H = 12
ROPE = 64
Q_LORA = 512
KV_LORA = 256
ROPE_THETA = 10000.0

ADAM_LR = 0.001
ADAM_B1 = 0.9
ADAM_B2 = 0.999
ADAM_EPS = 1e-08
ADAM_WD = 0.01
ADAM_STEP = 10

VMEM_LIMIT = 56 * 1024 * 1024


def _dot(a, b):
    return jnp.dot(a, b, preferred_element_type=F32)


def _dot_nt(a, b):
    return lax.dot_general(a, b, (((1,), (1,)), ((), ())), preferred_element_type=F32)


def _dot_tn(a, b):
    return lax.dot_general(a, b, (((0,), (0,)), ((), ())), preferred_element_type=F32)


@jax.custom_vjp
def _mm(a, b):
    return _dot(a.astype(BF16), b.astype(BF16))


def _mm_fwd(a, b):
    return _mm(a, b), (a, b)


def _mm_bwd(res, g):
    a, b = res
    gb = g.astype(BF16)
    return _dot_nt(gb, b.astype(BF16)).astype(a.dtype), _dot_tn(a.astype(BF16), gb).astype(b.dtype)


_mm.defvjp(_mm_fwd, _mm_bwd)


@jax.custom_vjp
def _mm_nt(a, b):
    return _dot_nt(a.astype(BF16), b.astype(BF16))


def _mm_nt_fwd(a, b):
    return _mm_nt(a, b), (a, b)


def _mm_nt_bwd(res, g):
    a, b = res
    gb = g.astype(BF16)
    return _dot(gb, b.astype(BF16)).astype(a.dtype), _dot_tn(gb, a.astype(BF16)).astype(b.dtype)


_mm_nt.defvjp(_mm_nt_fwd, _mm_nt_bwd)


@jax.custom_vjp
def _softmax(s):
    m = jnp.max(s, axis=-1, keepdims=True)
    e = jnp.exp(s - m)
    return e / jnp.sum(e, axis=-1, keepdims=True)


def _softmax_fwd(s):
    p = _softmax(s)
    return p, p


def _softmax_bwd(p, g):
    return (p * (g - jnp.sum(p * g, axis=-1, keepdims=True)),)


_softmax.defvjp(_softmax_fwd, _softmax_bwd)


def _rms(x, g, n):
    ms = jnp.sum(x * x, axis=-1, keepdims=True) * (1.0 / n)
    return x * lax.rsqrt(ms + EPS) * g


def _sigmoid(x):
    return 1.0 / (1.0 + jnp.exp(-x))


def _silu(x):
    return x * _sigmoid(x)


def _gelu(x):
    c = math.sqrt(2.0 / math.pi)
    return 0.5 * x * (1.0 + jnp.tanh(c * (x + 0.044715 * (x * x * x))))


@jax.custom_vjp
def _rot(x, c, s1, s2):
    return x * c + pltpu.roll(x, 96, 1) * s1 + pltpu.roll(x, 32, 1) * s2


def _rot_fwd(x, c, s1, s2):
    return _rot(x, c, s1, s2), (c, s1, s2)


def _rot_bwd(res, g):
    c, s1, s2 = res
    dx = g * c + pltpu.roll(g * s1, 32, 1) + pltpu.roll(g * s2, 96, 1)
    return dx, jnp.zeros_like(c), jnp.zeros_like(s1), jnp.zeros_like(s2)


_rot.defvjp(_rot_fwd, _rot_bwd)


def _mem_attn(xq, k, v, gq):
    outs = []
    for h in range(X_HEADS):
        sl = slice(HD * h, HD * (h + 1))
        q = _rms(xq[:, sl], gq, HD)
        p = _softmax(_mm_nt(q, k[:, sl]) * (HD ** -0.5))
        outs.append(_mm(p, v[:, sl]))
    return jnp.concatenate(outs, axis=-1)


def _merge(mix, xq, gate, k, v, gq):
    return jnp.concatenate([mix, _mem_attn(xq, k, v, gq)], axis=-1) * _silu(gate)


def _q_post(q, gqn, gqr, c, s1, s2):
    pieces = []
    for h in range(MLA_H):
        pieces.append(_rms(q[:, HD * h:HD * (h + 1)], gqn, HD))
        pieces.append(_rot(_rms(q[:, PRIM + HD * h:PRIM + HD * (h + 1)], gqr, ROPE), c, s1, s2))
    return jnp.concatenate(pieces, axis=-1)


def _kv_post(kv, krp, gkn, gkr, c, s1, s2):
    kr = _rot(_rms(krp, gkr, ROPE), c, s1, s2)
    pieces, vals = [], []
    for h in range(MLA_H):
        pieces.append(_rms(kv[:, 2 * HD * h:2 * HD * h + HD], gkn, HD))
        pieces.append(kr)
        vals.append(kv[:, 2 * HD * h + HD:2 * HD * (h + 1)])
    return jnp.concatenate(pieces, axis=-1), jnp.concatenate(vals, axis=-1)


def _rowwise(name, fn, ins, outs, nblk, sub=1, host=None):
    n_in = len(ins)

    def spec(kind, shape):
        if kind == 'r':
            return pl.BlockSpec((shape[0] // nblk, shape[1]), lambda i: (i, 0))
        if kind == 't':
            return pl.BlockSpec((shape[0], shape[1] // nblk), lambda i: (0, i))
        zeros = (0,) * len(shape)
        return pl.BlockSpec(tuple(shape), lambda i: zeros)

    def body(*refs):
        i = pl.program_id(0)
        res = fn(*[r[...] for r in refs[:n_in]])
        for (kind, _, _), ref, val in zip(outs, refs[n_in:], res):
            if kind == 'a':
                @pl.when(i == 0)
                def _():
                    ref[...] = jnp.zeros_like(ref)
                ref[...] += val.astype(ref.dtype)
            elif kind == 't':
                ref[...] = val.astype(F32).T.astype(ref.dtype)
            else:
                ref[...] = val.astype(ref.dtype)

    res, hosted = _hosting_call(
        body, name, nblk, host, [a for _, a in ins], [spec(k, a.shape) for k, a in ins],
        [jax.ShapeDtypeStruct(tuple(s), d) for _, s, d in outs], [spec(k, s) for k, s, _ in outs], [])
    return res if host is None else (res, hosted)


def _matmul_tn(at, g, name, out_dtype=BF16):
    K, L = at.shape
    N = g.shape[1]
    tn = next(t for t in (512, 384, 256, 128) if N % t == 0)

    def body(a_ref, g_ref, o_ref):
        o_ref[...] = _dot(a_ref[...], g_ref[...]).astype(o_ref.dtype)

    return pl.pallas_call(
        body, name=name, grid=(N // tn,),
        in_specs=[pl.BlockSpec((K, L), lambda n: (0, 0)), pl.BlockSpec((L, tn), lambda n: (0, n))],
        out_specs=pl.BlockSpec((K, tn), lambda n: (0, n)),
        out_shape=jax.ShapeDtypeStruct((K, N), out_dtype),
        compiler_params=pltpu.CompilerParams(dimension_semantics=("arbitrary",), vmem_limit_bytes=VMEM_LIMIT),
    )(at, g)


def _matmul_tn_slots(at, g, name, host=None):
    K, L = at.shape
    n = g.shape[1] // N_DEV

    def body(a_ref, g_ref, o_ref):
        o_ref[...] = _dot(a_ref[...], g_ref[...]).astype(o_ref.dtype)

    res, hosted = _hosting_call(
        body, name, N_DEV, host, [at, g],
        [pl.BlockSpec((K, L), lambda d: (0, 0)), pl.BlockSpec((L, n), lambda d: (0, d))],
        [jax.ShapeDtypeStruct((N_DEV, K, n), BF16)], [pl.BlockSpec((None, K, n), lambda d: (d, 0, 0))], [])
    return res[0] if host is None else (res[0], hosted)


def _mm_slots(a16, w):
    return jnp.concatenate([_dot(a16, w[d]) for d in range(N_DEV)], axis=-1)


def _mm_slots_nt(g16, w):
    n = w.shape[2]
    out = _dot_nt(g16[:, 0:n], w[0])
    for d in range(1, N_DEV):
        out = out + _dot_nt(g16[:, d * n:(d + 1) * n], w[d])
    return out


class _Exchange:
    def __init__(self, ins, outs, scratch, start, finish):
        self.ins, self.outs, self.scratch, self.start, self.finish = ins, outs, scratch, start, finish


def _xyc():
    return lax.axis_index("x"), lax.axis_index("y"), lax.axis_index("c")


def _plan_all_gather(xs):
    n = len(xs)

    def build(x_refs, out_refs, sems):
        send_sems, recv_sems, local_sems = sems
        x, y, c = _xyc()

        def copies(k, block, to, own=False):
            slot = 4 * block[0] + 2 * block[1] + block[2]
            return [pltpu.make_async_remote_copy(
                src_ref=x_refs[a] if own else out_refs[a].at[slot], dst_ref=out_refs[a].at[slot],
                send_sem=send_sems.at[k * n + a], recv_sem=recv_sems.at[k * n + a], device_id=to,
                device_id_type=MESH) for a in range(n)]

        mine = [pltpu.make_async_copy(x_refs[a], out_refs[a].at[4 * x + 2 * y + c], local_sems.at[a])
                for a in range(n)]
        return copies, mine, (x, y, c), [(1 - x, y), (x, 1 - y), (1 - x, 1 - y)]

    def first_copies(copies, me, chips):
        x, y, c = me
        first = copies(0, me, (x, y, 1 - c), own=True)
        for j, chip in enumerate(chips):
            first += copies(1 + j, me, (*chip, c), own=True)
        return first

    def start(x_refs, out_refs, sems):
        copies, mine, me, chips = build(x_refs, out_refs, sems)
        for cp in mine + first_copies(copies, me, chips):
            cp.start()

    def finish(x_refs, out_refs, sems):
        copies, mine, me, chips = build(x_refs, out_refs, sems)
        x, y, c = me
        passed = []
        for j, chip in enumerate(chips):
            for cp in copies(1 + j, (*chip, c), me):
                cp.wait_recv()
            fwd = copies(4 + j, (*chip, c), (x, y, 1 - c))
            for cp in fwd:
                cp.start()
            passed += fwd
        for cp in copies(0, (x, y, 1 - c), me):
            cp.wait_recv()
        for j, chip in enumerate(chips):
            for cp in copies(4 + j, (*chip, 1 - c), me):
                cp.wait_recv()
        for cp in first_copies(copies, me, chips) + passed:
            cp.wait_send()
        for cp in mine:
            cp.wait()

    return _Exchange(list(xs), [jax.ShapeDtypeStruct((N_DEV,) + a.shape, a.dtype) for a in xs],
                     [pltpu.SemaphoreType.DMA((7 * n,)), pltpu.SemaphoreType.DMA((7 * n,)),
                      pltpu.SemaphoreType.DMA((n,))], start, finish)


_CHIPS = ((0, 0), (0, 1), (1, 0), (1, 1))


def _plan_pair(sends):
    n = len(sends)

    def build(s_refs, o_refs, sems):
        send_sems, recv_sems = sems
        x, y, c = _xyc()
        return [pltpu.make_async_remote_copy(
            src_ref=s_refs[a].at[4 * px + 2 * py + 1 - c], dst_ref=o_refs[a].at[j],
            send_sem=send_sems.at[j * n + a], recv_sem=recv_sems.at[j * n + a], device_id=(x, y, 1 - c),
            device_id_type=MESH) for j, (px, py) in enumerate(_CHIPS) for a in range(n)]

    def start(s_refs, o_refs, sems):
        for cp in build(s_refs, o_refs, sems):
            cp.start()

    def finish(s_refs, o_refs, sems):
        for cp in build(s_refs, o_refs, sems):
            cp.wait_recv()
            cp.wait_send()

    return _Exchange(list(sends), [jax.ShapeDtypeStruct((4,) + a.shape[1:], a.dtype) for a in sends],
                     [pltpu.SemaphoreType.DMA((4 * n,)), pltpu.SemaphoreType.DMA((4 * n,))], start, finish)


def _plan_chips(ts):
    n = len(ts)
    flips = ((1, 0), (0, 1), (1, 1))

    def build(t_refs, o_refs, sems):
        send_sems, recv_sems, local_sems = sems
        x, y, c = _xyc()
        mine = 2 * x + y
        local = [pltpu.make_async_copy(t_refs[a].at[mine], o_refs[a].at[mine], local_sems.at[a]) for a in range(n)]
        remote = []
        for k, (fx, fy) in enumerate(flips):
            px = 1 - x if fx else x
            py = 1 - y if fy else y
            remote += [pltpu.make_async_remote_copy(
                src_ref=t_refs[a].at[2 * px + py], dst_ref=o_refs[a].at[mine],
                send_sem=send_sems.at[k * n + a], recv_sem=recv_sems.at[k * n + a], device_id=(px, py, c),
                device_id_type=MESH) for a in range(n)]
        return local, remote

    def start(t_refs, o_refs, sems):
        local, remote = build(t_refs, o_refs, sems)
        for cp in local + remote:
            cp.start()

    def finish(t_refs, o_refs, sems):
        local, remote = build(t_refs, o_refs, sems)
        for cp in remote:
            cp.wait_recv()
        for cp in remote:
            cp.wait_send()
        for cp in local:
            cp.wait()

    return _Exchange(list(ts), [jax.ShapeDtypeStruct(a.shape, a.dtype) for a in ts],
                     [pltpu.SemaphoreType.DMA((3 * n,)), pltpu.SemaphoreType.DMA((3 * n,)),
                      pltpu.SemaphoreType.DMA((n,))], start, finish)


def _combine(*plans):
    def parts(refs, attr):
        out, at = [], 0
        for p in plans:
            n = len(getattr(p, attr))
            out.append(refs[at:at + n])
            at += n
        return out

    def run(half):
        def go(ins, outs, sems):
            for p, a, o, s in zip(plans, parts(ins, "ins"), parts(outs, "outs"), parts(sems, "scratch")):
                getattr(p, half)(a, o, s)
        return go

    return _Exchange(sum((p.ins for p in plans), []), sum((p.outs for p in plans), []),
                     sum((p.scratch for p in plans), []), run("start"), run("finish"))


def _exchange_call(plan, name):
    n = len(plan.ins)

    def body(*refs):
        ins, outs, sems = refs[:n], refs[n:2 * n], refs[2 * n:]
        plan.start(ins, outs, sems)
        plan.finish(ins, outs, sems)

    return pl.pallas_call(
        body, name=name, out_shape=plan.outs,
        in_specs=[pl.BlockSpec(memory_space=pl.ANY)] * n, out_specs=[pl.BlockSpec(memory_space=pl.ANY)] * n,
        scratch_shapes=plan.scratch,
    )(*plan.ins)


def _slab_spec(lead, rows, cols, nb):
    if rows % (nb * 16) == 0:
        return pl.BlockSpec((lead, rows // nb, cols), lambda i: (0, i, 0))
    if cols % (nb * 128) == 0:
        return pl.BlockSpec((lead, rows, cols // nb), lambda i: (0, 0, i))
    return pl.BlockSpec((lead, rows, cols), lambda i: (0, 0, 0))


def _slab_spec2(rows, cols, nb):
    if rows % (nb * 16) == 0:
        return pl.BlockSpec((rows // nb, cols), lambda i: (i, 0))
    if cols % (nb * 128) == 0:
        return pl.BlockSpec((rows, cols // nb), lambda i: (0, i))
    return pl.BlockSpec((rows, cols), lambda i: (0, 0))


def _cast_call(arrays, name, host=None):
    n = len(arrays)
    nb = 8

    def body(*refs):
        for a in range(n):
            refs[n + a][...] = refs[a][...].astype(BF16)

    specs = [_slab_spec2(x.shape[0], x.shape[1], nb) for x in arrays]
    return _hosting_call(body, name, nb, host, list(arrays), specs,
                         [jax.ShapeDtypeStruct(x.shape, BF16) for x in arrays], specs, [])


def _pair_add(sends, fromsib, name):
    n = len(sends)
    nb = 8

    def body(*refs):
        c = lax.axis_index("c")
        for a in range(n):
            s_ref, f_ref, t_ref = refs[a], refs[n + a], refs[2 * n + a]
            for j in range(4):
                t_ref[j] = (s_ref[2 * j + c].astype(F32) + f_ref[j].astype(F32)).astype(t_ref.dtype)

    def spec(a, lead):
        return _slab_spec(lead, a.shape[1], a.shape[2], nb)

    return pl.pallas_call(
        body, name=name, grid=(nb,),
        in_specs=[spec(a, N_DEV) for a in sends] + [spec(a, 4) for a in fromsib],
        out_specs=[spec(a, 4) for a in fromsib],
        out_shape=[jax.ShapeDtypeStruct(a.shape, a.dtype) for a in fromsib],
        compiler_params=pltpu.CompilerParams(dimension_semantics=("arbitrary",), vmem_limit_bytes=VMEM_LIMIT),
    )(*sends, *fromsib)


def _adamw_vals(w, g, m, v):
    m2 = ADAM_B1 * m + (1.0 - ADAM_B1) * g
    v2 = ADAM_B2 * v + (1.0 - ADAM_B2) * (g * g)
    m_hat = m2 / (1.0 - ADAM_B1 ** ADAM_STEP)
    v_hat = v2 / (1.0 - ADAM_B2 ** ADAM_STEP)
    delta = -ADAM_LR * (m_hat / (jnp.sqrt(v_hat) + ADAM_EPS) + ADAM_WD * w)
    return delta, m2, v2


def _sum_adamw(recv, w, m, v, name):
    R, C = w.shape
    ns = recv.shape[0]
    br = next((t for t in (256, 128, 64, 32, 16) if R % t == 0), R)

    def body(r_ref, w_ref, m_ref, v_ref, g_ref, d_ref, m2_ref, v2_ref):
        g = r_ref[0].astype(F32)
        for d in range(1, ns):
            g = g + r_ref[d].astype(F32)
        dl, m2, v2 = _adamw_vals(w_ref[...], g, m_ref[...], v_ref[...])
        g_ref[...] = g
        d_ref[...] = dl
        m2_ref[...] = m2
        v2_ref[...] = v2

    spec = pl.BlockSpec((br, C), lambda i: (i, 0))
    return pl.pallas_call(
        body, name=name, grid=(R // br,),
        in_specs=[pl.BlockSpec((ns, br, C), lambda i: (0, i, 0)), spec, spec, spec], out_specs=[spec] * 4,
        out_shape=[jax.ShapeDtypeStruct((R, C), F32)] * 4,
        compiler_params=pltpu.CompilerParams(dimension_semantics=("arbitrary",)),
    )(recv, w, m, v)


def _updates_call(recvs, ws, ms, vs, name, host=None):
    n = len(recvs)
    nb = 8

    def body(*refs):
        for a in range(n):
            r_ref, w_ref, m_ref, v_ref = refs[a], refs[n + a], refs[2 * n + a], refs[3 * n + a]
            g_ref, d_ref, m2_ref, v2_ref = refs[4 * n + 4 * a:4 * n + 4 * a + 4]
            g = r_ref[0].astype(F32)
            for d in range(1, r_ref.shape[0]):
                g = g + r_ref[d].astype(F32)
            dl, m2, v2 = _adamw_vals(w_ref[...], g, m_ref[...], v_ref[...])
            g_ref[...] = g
            d_ref[...] = dl
            m2_ref[...] = m2
            v2_ref[...] = v2

    def spec3(r):
        return _slab_spec(r.shape[0], r.shape[1], r.shape[2], nb)

    def spec2(w):
        return _slab_spec2(w.shape[0], w.shape[1], nb)

    res, hosted = _hosting_call(
        body, name, nb, host, list(recvs) + list(ws) + list(ms) + list(vs),
        [spec3(r) for r in recvs] + [spec2(w) for w in ws] * 3,
        [jax.ShapeDtypeStruct(w.shape, F32) for w in ws for _ in range(4)],
        [spec2(w) for w in ws for _ in range(4)], [])
    return [res[4 * a:4 * a + 4] for a in range(n)], hosted


def _small_sum(gath, loss_g, row0_g, name):
    _, R, C = gath.shape
    br = R // 3

    def body(g_ref, l_ref, r_ref, go_ref, lo_ref):
        g = g_ref[0].astype(F32)
        lsum = l_ref[0]
        for d in range(1, N_DEV):
            g = g + g_ref[d].astype(F32)
            lsum = lsum + l_ref[d]
        go_ref[...] = g
        lo_ref[...] = lsum

        @pl.when(pl.program_id(0) == 0)
        def _():
            row0 = r_ref[0]
            for d in range(1, N_DEV):
                row0 = row0 + r_ref[d]
            go_ref[0:8, :] = go_ref[0:8, :] + jnp.where(lax.broadcasted_iota(jnp.int32, row0.shape, 0) == 0, row0, 0.0)

    return pl.pallas_call(
        body, name=name, grid=(R // br,),
        in_specs=[pl.BlockSpec((N_DEV, br, C), lambda i: (0, i, 0)),
                  pl.BlockSpec((N_DEV, 8, HD), lambda i: (0, 0, 0)), pl.BlockSpec((N_DEV, 8, C), lambda i: (0, 0, 0))],
        out_specs=[pl.BlockSpec((br, C), lambda i: (i, 0)), pl.BlockSpec((8, HD), lambda i: (0, 0))],
        out_shape=[jax.ShapeDtypeStruct((R, C), F32), jax.ShapeDtypeStruct((8, HD), F32)],
        compiler_params=pltpu.CompilerParams(dimension_semantics=("arbitrary",)),
    )(gath, loss_g, row0_g)


def _adamw_multi(ws, gs, ms, vs, name, nblk=1):
    n = len(ws)

    def body(*refs):
        for a in range(n):
            dl, m2, v2 = _adamw_vals(refs[a][...], refs[n + a][...], refs[2 * n + a][...], refs[3 * n + a][...])
            refs[4 * n + 3 * a][...] = dl
            refs[4 * n + 3 * a + 1][...] = m2
            refs[4 * n + 3 * a + 2][...] = v2

    def spec(x):
        rest = (0,) * (x.ndim - 1)
        return pl.BlockSpec((x.shape[0] // nblk,) + tuple(x.shape[1:]), lambda i: (i,) + rest)

    res = pl.pallas_call(
        body, name=name, grid=(nblk,),
        in_specs=[spec(w) for w in ws] * 4, out_specs=[spec(w) for w in ws for _ in range(3)],
        out_shape=[jax.ShapeDtypeStruct(w.shape, F32) for w in ws for _ in range(3)],
        compiler_params=pltpu.CompilerParams(dimension_semantics=("arbitrary",), vmem_limit_bytes=VMEM_LIMIT),
    )(*ws, *gs, *ms, *vs)
    return [res[3 * a:3 * a + 3] for a in range(n)]


def _s5_param_fn(lr, li, ls, btr, bti):
    step = jnp.exp(ls)
    er = jnp.exp(lr * step)
    ang = li * step
    ar = er * jnp.cos(ang)
    ai = er * jnp.sin(ang)
    nr = ar - 1.0
    den = lr * lr + li * li
    fr = (nr * lr + ai * li) / den
    fi = (ai * lr - nr * li) / den
    return ar, ai, fr * btr - fi * bti, fr * bti + fi * btr


def _s5_params(lr, li, ls, btr, bti, cre, cim):
    nb = S5_G // S5_GB
    GC = S5_GB * S5_C
    expand = jnp.asarray(np.tile(np.eye(S5_P, dtype=np.float32), (1, S5_GB)), BF16)
    own = jnp.asarray((np.arange(GC)[:, None] // S5_C == np.arange(S5_W)[None, :] // S5_P).astype(np.float32))

    def body(lr_ref, li_ref, ls_ref, br_ref, bi_ref, cr_ref, ci_ref, e_ref, own_ref, ar_ref, ai_ref, bm_ref, cm_ref):
        ar, ai, bbr, bbi = _s5_param_fn(lr_ref[...], li_ref[...], ls_ref[...], br_ref[...], bi_ref[...])
        ar_ref[...] = ar
        ai_ref[...] = ai

        def plane(x, n):
            rows = x[n * S5_GB:(n + 1) * S5_GB].reshape(GC, S5_P).astype(BF16)
            return _dot(rows, e_ref[...]) * own_ref[...]

        for n in range(nb):
            bm_ref[n] = jnp.concatenate([plane(bbr, n), plane(bbi, n)], axis=-1).astype(BF16)
            cm_ref[n] = jnp.concatenate([plane(cr_ref[...], n), -plane(ci_ref[...], n)], axis=-1).astype(BF16)

    sd = jax.ShapeDtypeStruct
    return pl.pallas_call(
        body, name="s5_params",
        out_shape=[sd(lr.shape, F32), sd(lr.shape, F32), sd((nb, GC, 2 * S5_W), BF16), sd((nb, GC, 2 * S5_W), BF16)],
        compiler_params=pltpu.CompilerParams(vmem_limit_bytes=VMEM_LIMIT),
    )(lr, li, ls, btr, bti, cre, cim, expand, own)


def _s5_params_bwd(lr, li, ls, btr, bti, dar, dai, dbbr, dbbi):
    def body(lr_ref, li_ref, ls_ref, br_ref, bi_ref, dar_ref, dai_ref, dbbr_ref, dbbi_ref,
             dlr_ref, dli_ref, dls_ref, dbr_ref, dbi_ref):
        _, vjp = jax.vjp(_s5_param_fn, lr_ref[...], li_ref[...], ls_ref[...], br_ref[...], bi_ref[...])
        dlr, dli, dls, dbr, dbi = vjp((dar_ref[...], dai_ref[...], dbbr_ref[...], dbbi_ref[...]))
        dlr_ref[...] = dlr
        dli_ref[...] = dli
        dls_ref[...] = dls
        dbr_ref[...] = dbr
        dbi_ref[...] = dbi

    sd = jax.ShapeDtypeStruct
    return pl.pallas_call(
        body, name="s5_params_bwd",
        out_shape=[sd(lr.shape, F32), sd(lr.shape, F32), sd(ls.shape, F32), sd(btr.shape, F32), sd(btr.shape, F32)],
    )(lr, li, ls, btr, bti, dar, dai, dbbr, dbbi)


def _cpow(ar, ai, n):
    assert n & (n - 1) == 0
    while n > 1:
        ar, ai = ar * ar - ai * ai, 2.0 * ar * ai
        n //= 2
    return ar, ai


def _scan(st, cr, ci, init, nk, reverse, store, prev=None):
    W = S5_W

    def advance(k, sr, si):
        rows = pl.ds(k * 8 if isinstance(k, int) else pl.multiple_of(k * 8, 8), 8)
        nsr = cr * sr - ci * si + st[rows, 0:W]
        nsi = cr * si + ci * sr + st[rows, W:2 * W]
        if store:
            st[rows, 0:W] = nsr
            st[rows, W:2 * W] = nsi
        return nsr, nsi

    if prev is None:
        return lax.fori_loop(0, nk, lambda j, c: advance(nk - 1 - j if reverse else j, c[0], c[1]), init, unroll=2)
    assert reverse

    def step(j, carry):
        k = nk - 1 - j
        nsr, nsi = advance(k, carry[0], carry[1])
        prows = pl.ds(pl.multiple_of((k - 1) * 8, 8), 8)
        pr = prev[prows, 0:W]
        pi = prev[prows, W:2 * W]
        return nsr, nsi, carry[2] + nsr * pr + nsi * pi, carry[3] + nsi * pr - nsr * pi

    carry = lax.fori_loop(0, nk - 1, step, init, unroll=2)
    nsr, nsi = advance(0, carry[0], carry[1])
    return nsr, nsi, carry[2], carry[3]


def _chain(fin, fr, fi, pr, pi, reverse):
    W = S5_W
    fin[:, 0:W] = fr
    fin[:, W:2 * W] = fi
    rowid = lax.broadcasted_iota(jnp.int32, (8, W), 0)
    cr = jnp.zeros((1, W), F32)
    ci = jnp.zeros((1, W), F32)
    init_r = jnp.zeros((8, W), F32)
    init_i = jnp.zeros((8, W), F32)
    for s in (range(7, -1, -1) if reverse else range(8)):
        init_r = jnp.where(rowid == s, cr, init_r)
        init_i = jnp.where(rowid == s, ci, init_i)
        lr = fin[s:s + 1, 0:W]
        li = fin[s:s + 1, W:2 * W]
        cr, ci = lr + pr * cr - pi * ci, li + pr * ci + pi * cr
    return init_r, init_i


def _full_scan(st, fin, ar, ai, nk, reverse, prev=None, carry_in=None, carry_out=None):
    W = S5_W
    cr = jnp.broadcast_to(ar, (8, W))
    ci = jnp.broadcast_to(-ai if reverse else ai, (8, W))
    z = jnp.zeros((8, W), F32)
    if carry_in is None:
        fr, fi = _scan(st, cr, ci, (z, z), nk, reverse, store=False)
        pr, pi = _cpow(ar, -ai if reverse else ai, nk)
        init = _chain(fin, fr, fi, pr, pi, reverse)
    else:
        init = (carry_in[:, 0:W], carry_in[:, W:2 * W])
    if carry_out is not None:
        carry_out[:, 0:W] = init[0]
        carry_out[:, W:2 * W] = init[1]
    if prev is None:
        return _scan(st, cr, ci, init, nk, reverse, store=True)
    return _scan(st, cr, ci, init + (z, z), nk, reverse, store=True, prev=prev)


def _s5_specs(L):
    W2 = 2 * S5_W
    GC = S5_GB * S5_C
    col = pl.BlockSpec((L, GC), lambda g: (0, g))
    vec = pl.BlockSpec((1, GC), lambda g: (0, g))
    avec = pl.BlockSpec((1, S5_W), lambda g: (0, g))
    bmat = pl.BlockSpec((None, GC, W2), lambda g: (g, 0, 0))
    cmat = pl.BlockSpec((None, W2, GC), lambda g: (g, 0, 0))
    return col, vec, avec, bmat, cmat


def _interleave(dst, src, nk):
    for s in range(8):
        dst[pl.ds(s, nk, stride=8), :] = src[s * nk:(s + 1) * nk, :]


def _deinterleave(dst, src, nk):
    for s in range(8):
        dst[s * nk:(s + 1) * nk, :] = src[pl.ds(s, nk, stride=8), :].astype(dst.dtype)


def _hosting_call(body, name, nsteps, host, ins, in_specs, outs, out_specs, scratch):
    grid = (nsteps,) if isinstance(nsteps, int) else tuple(nsteps)
    params = pltpu.CompilerParams(dimension_semantics=("arbitrary",) * len(grid), vmem_limit_bytes=VMEM_LIMIT)
    if host is None:
        res = pl.pallas_call(
            body, name=name, grid=grid, in_specs=in_specs, out_specs=out_specs, out_shape=outs,
            scratch_shapes=scratch, compiler_params=params,
        )(*ins)
        return list(res), []
    n_in, n_out, n_sc = len(ins), len(outs), len(scratch)
    h_in, h_out = len(host.ins), len(host.outs)

    def hosted(*refs):
        a = refs[:n_in]
        ha = refs[n_in:n_in + h_in]
        o = refs[n_in + h_in:n_in + h_in + n_out]
        ho = refs[n_in + h_in + n_out:n_in + h_in + n_out + h_out]
        sc = refs[n_in + h_in + n_out + h_out:n_in + h_in + n_out + h_out + n_sc]
        hs = refs[n_in + h_in + n_out + h_out + n_sc:]
        first = functools.reduce(jnp.logical_and, [pl.program_id(i) == 0 for i in range(len(grid))])
        last = functools.reduce(jnp.logical_and, [pl.program_id(i) == g - 1 for i, g in enumerate(grid)])

        @pl.when(first)
        def _():
            host.start(ha, ho, hs)

        body(*a, *o, *sc)

        @pl.when(last)
        def _():
            host.finish(ha, ho, hs)

    hbm = pl.BlockSpec(memory_space=pl.ANY)
    res = pl.pallas_call(
        hosted, name=name, grid=grid,
        in_specs=list(in_specs) + [hbm] * h_in, out_specs=list(out_specs) + [hbm] * h_out,
        out_shape=list(outs) + list(host.outs), scratch_shapes=list(scratch) + list(host.scratch),
        compiler_params=params,
    )(*ins, *host.ins)
    return list(res[:n_out]), list(res[n_out:])


def _s5_fwd(u, bm, cm, ar, ai, dvec, host=None):
    L = u.shape[0]
    nk = L // 8
    GC = S5_GB * S5_C
    nb = S5_G // S5_GB
    col, vec, avec, bmat, cmat = _s5_specs(L)

    def body(u_ref, b_ref, c_ref, ar_ref, ai_ref, d_ref, y_ref, carry_ref, st, fin, ui, yi):
        _interleave(ui, u_ref, nk)
        for r in range(8):
            rows = slice(r * nk, (r + 1) * nk)
            st[rows, :] = _dot(ui[rows, :].astype(BF16), b_ref[...])
        _full_scan(st, fin, ar_ref[...], ai_ref[...], nk, reverse=False, carry_out=carry_ref)
        for r in range(8):
            rows = slice(r * nk, (r + 1) * nk)
            yi[rows, :] = _dot_nt(st[rows, :].astype(BF16), c_ref[...]) + d_ref[...] * ui[rows, :]
        _deinterleave(y_ref, yi, nk)

    return _hosting_call(
        body, "s5_fwd", nb, host,
        [u, bm, cm, ar, ai, dvec], [col, bmat, bmat, avec, avec, vec],
        [jax.ShapeDtypeStruct(u.shape, F32), jax.ShapeDtypeStruct((nb * 8, 2 * S5_W), F32)],
        [col, pl.BlockSpec((8, 2 * S5_W), lambda g: (g, 0))],
        [pltpu.VMEM((L, 2 * S5_W), F32), pltpu.VMEM((8, 2 * S5_W), F32), pltpu.VMEM((L, GC), F32),
         pltpu.VMEM((L, GC), F32)])


def _s5_bwd(u, dy, carry, bm, cm, ar, ai, dvec, mask, rmat, host=None):
    L = u.shape[0]
    nk = L // 8
    W = S5_W
    GC = S5_GB * S5_C
    col, vec, avec, bmat, cmat = _s5_specs(L)
    hi = lax.Precision.HIGHEST

    def body(u_ref, dy_ref, carry_ref, b_ref, ct_ref, ar_ref, ai_ref, d_ref, mask_ref, r_ref,
             du_ref, db_ref, dc_ref, dd_ref, dar_ref, dai_ref, sa, sb, fin, ui, dyi, dui):
        ar = ar_ref[...]
        ai = ai_ref[...]
        _interleave(ui, u_ref, nk)
        _interleave(dyi, dy_ref, nk)
        for r in range(8):
            rows = slice(r * nk, (r + 1) * nk)
            sa[rows, :] = _dot(ui[rows, :].astype(BF16), b_ref[...])
            sb[rows, :] = _dot(dyi[rows, :].astype(BF16), ct_ref[...])
        _full_scan(sa, fin, ar, ai, nk, reverse=False, carry_in=carry_ref)
        gr, gi, accr, acci = _full_scan(sb, fin, ar, ai, nk, reverse=True, prev=sa)
        rowid = lax.broadcasted_iota(jnp.int32, (8, W), 0)
        last = pl.ds((nk - 1) * 8, 8)
        pr = jnp.where(rowid == 0, 0.0, pltpu.roll(sa[last, 0:W], 1, 0))
        pi = jnp.where(rowid == 0, 0.0, pltpu.roll(sa[last, W:2 * W], 1, 0))
        accr = accr + gr * pr + gi * pi
        acci = acci + gi * pr - gr * pi
        dar_ref[...] = jnp.sum(accr, axis=0, keepdims=True)
        dai_ref[...] = jnp.sum(acci, axis=0, keepdims=True)
        dbf = jnp.zeros((GC, 2 * W), F32)
        dcf = jnp.zeros((GC, 2 * W), F32)
        dd = jnp.zeros((1, GC), F32)
        for r in range(8):
            rows = slice(r * nk, (r + 1) * nk)
            ub = ui[rows, :]
            dyb = dyi[rows, :]
            gb = sb[rows, :].astype(BF16)
            dui[rows, :] = _dot_nt(gb, b_ref[...]) + d_ref[...] * dyb
            dbf = dbf + _dot_tn(ub.astype(BF16), gb)
            dcf = dcf + _dot_tn(dyb.astype(BF16), sa[rows, :].astype(BF16))
            dd = dd + jnp.sum(dyb * ub, axis=0, keepdims=True)
        db_ref[...] = jnp.dot(dbf * mask_ref[...], r_ref[...], precision=hi, preferred_element_type=F32)
        dc_ref[...] = jnp.dot(dcf * mask_ref[...], r_ref[...], precision=hi, preferred_element_type=F32)
        dd_ref[...] = dd
        _deinterleave(du_ref, dui, nk)

    cmp_spec = pl.BlockSpec((GC, 2 * S5_P), lambda g: (g, 0))
    whole = lambda shape: pl.BlockSpec(shape, lambda g: (0, 0))
    sd = jax.ShapeDtypeStruct
    return _hosting_call(
        body, "s5_bwd", S5_G // S5_GB, host,
        [u, dy, carry, bm, cm, ar, ai, dvec, mask, rmat],
        [col, col, pl.BlockSpec((8, 2 * W), lambda g: (g, 0)), bmat, bmat, avec, avec, vec, whole(mask.shape),
         whole(rmat.shape)],
        [sd(u.shape, BF16), sd((S5_G * S5_C, 2 * S5_P), F32), sd((S5_G * S5_C, 2 * S5_P), F32),
         sd((1, PRIM), F32), sd((1, S5_G * S5_P), F32), sd((1, S5_G * S5_P), F32)],
        [col, cmp_spec, cmp_spec, vec, avec, avec],
        [pltpu.VMEM((L, 2 * W), F32), pltpu.VMEM((L, 2 * W), F32), pltpu.VMEM((8, 2 * W), F32),
         pltpu.VMEM((L, GC), F32), pltpu.VMEM((L, GC), F32), pltpu.VMEM((L, GC), F32)])


def _s5_compact_consts():
    g_row = np.arange(S5_GB * S5_C) // S5_C
    col = np.arange(2 * S5_W)
    g_col = (col % S5_W) // S5_P
    mask = (g_row[:, None] == g_col[None, :]).astype(np.float32)
    tgt = (col // S5_W) * S5_P + col % S5_P
    rmat = (tgt[:, None] == np.arange(2 * S5_P)[None, :]).astype(np.float32)
    return jnp.asarray(mask), jnp.asarray(rmat)


def _attn_scores(q_ref, k_ref, qb, bq, scale):
    ext = (qb + 1) * bq
    s = _dot_nt(q_ref[qb * bq:ext, :], k_ref[0:ext, :]) * scale
    qpos = lax.broadcasted_iota(jnp.int32, (bq, bq), 0)
    kpos = lax.broadcasted_iota(jnp.int32, (bq, bq), 1)
    diag = jnp.where(kpos <= qpos, s[:, ext - bq:], NEG)
    return diag if qb == 0 else jnp.concatenate([s[:, :ext - bq], diag], axis=-1)


def _attn_fwd(qp, kp, v, scale):
    L = qp.shape[0]
    bq = min(256, L)

    def body(q_ref, k_ref, v_ref, o_ref, lse_ref):
        for qb in range(L // bq):
            rows = slice(qb * bq, (qb + 1) * bq)
            s = _attn_scores(q_ref, k_ref, qb, bq, scale)
            m = jnp.max(s, axis=-1, keepdims=True)
            e = jnp.exp(s - m)
            l = jnp.sum(e, axis=-1, keepdims=True)
            o_ref[rows, :] = _dot(e.astype(BF16), v_ref[0:(qb + 1) * bq, :]) / l
            lse_ref[rows, :] = jnp.broadcast_to(m + jnp.log(l), (bq, HD))

    blk = pl.BlockSpec((L, HD), lambda h: (0, h))
    wide = pl.BlockSpec((L, 2 * HD), lambda h: (0, h))
    return pl.pallas_call(
        body, name="mla_attn_fwd", grid=(MLA_H,),
        in_specs=[wide, wide, blk], out_specs=[blk, blk],
        out_shape=[jax.ShapeDtypeStruct((L, MLA_H * HD), F32)] * 2,
        compiler_params=pltpu.CompilerParams(dimension_semantics=("arbitrary",), vmem_limit_bytes=VMEM_LIMIT),
    )(qp, kp, v)


def _attn_bwd(qp, kp, v, o, lse, do, scale):
    L = qp.shape[0]
    bq = min(256, L)
    nq = L // bq

    def body(q_ref, k_ref, v_ref, o_ref, lse_ref, do_ref, dq_ref, dk_ref, dv_ref, dk_acc, dv_acc):
        dk_acc[...] = jnp.zeros_like(dk_acc)
        dv_acc[...] = jnp.zeros_like(dv_acc)
        for qb in range(nq):
            rows = slice(qb * bq, (qb + 1) * bq)
            ext = (qb + 1) * bq
            do = do_ref[rows, :]
            dob = do.astype(BF16)
            p = jnp.exp(_attn_scores(q_ref, k_ref, qb, bq, scale) - lse_ref[rows, 0:1])
            dp = _dot_nt(dob, v_ref[0:ext, :])
            dsum = jnp.sum(do * o_ref[rows, :], axis=-1, keepdims=True)
            ds = (p * (dp - dsum) * scale).astype(BF16)
            dq_ref[rows, :] = _dot(ds, k_ref[0:ext, :]).astype(dq_ref.dtype)
            dk_acc[0:ext, :] += _dot_tn(ds, q_ref[rows, :])
            dv_acc[0:ext, :] += _dot_tn(p.astype(BF16), dob)
        dk_ref[...] = dk_acc[...].astype(dk_ref.dtype)
        dv_ref[...] = dv_acc[...].astype(dv_ref.dtype)

    sd = jax.ShapeDtypeStruct
    blk = pl.BlockSpec((L, HD), lambda h: (0, h))
    wide = pl.BlockSpec((L, 2 * HD), lambda h: (0, h))
    return pl.pallas_call(
        body, name="mla_attn_bwd", grid=(MLA_H,),
        in_specs=[wide, wide, blk, blk, blk, blk], out_specs=[wide, wide, blk],
        out_shape=[sd((L, MLA_H * 2 * HD), BF16), sd((L, MLA_H * 2 * HD), BF16), sd((L, MLA_H * HD), BF16)],
        scratch_shapes=[pltpu.VMEM((L, 2 * HD), F32), pltpu.VMEM((L, HD), F32)],
        compiler_params=pltpu.CompilerParams(dimension_semantics=("arbitrary",), vmem_limit_bytes=VMEM_LIMIT),
    )(qp, kp, v, o, lse, do)


def _kv_fn(mem, gm, w, gk):
    kv = _mm(_rms(mem, gm, D_MODEL), w)
    k = jnp.concatenate([_rms(kv[:, HD * h:HD * (h + 1)], gk, HD) for h in range(X_HEADS)], axis=-1)
    return k, kv[:, XQ:]


def _kv_prep(mem, gm, w, gk, name):
    def fn(mem, gm, w, gk):
        return _kv_fn(mem, gm, w, gk)
    M = mem.shape[0]
    return _rowwise(name, fn, [('c', mem), ('c', gm), ('c', w), ('c', gk)],
                    [('c', (M, XQ), F32), ('c', (M, XQ), F32)], 1)


def _kv_prep_bwd(mem, gm, w, gk, dk, dv, name):
    def fn(mem, gm, w, gk, dk, dv):
        _, vjp = jax.vjp(lambda a, b, c: _kv_fn(mem, a, b, c), gm, w, gk)
        return vjp((dk, dv))
    return _rowwise(name, fn, [('c', mem), ('c', gm), ('c', w), ('c', gk), ('c', dk), ('c', dv)],
                    [('c', gm.shape, F32), ('c', w.shape, BF16), ('c', gk.shape, F32)], 1)


def _forward_merge(x, mix, mix_kind, xq, gate, k, v, gq, wout, name, nblk, sub, host=None):
    def fn(x, mix, xq, gate, k, v, gq, wout):
        o = _merge(mix, xq, gate, k, v, gq)
        return (x + _dot(o.astype(BF16), wout),)
    L = x.shape[0]
    out = _rowwise(name, fn, [('r', x), (mix_kind, mix), ('r', xq), ('r', gate), ('c', k), ('c', v), ('c', gq),
                              ('c', wout)], [('r', (L, D_MODEL), F32)], nblk, sub, host=host)
    return out[0] if host is None else (out[0][0], out[1])


def _backward_merge(dx, mix, mix_kind, xq, gate, k, v, gq, wout, name, nblk, sub, host=None):
    def fn(dx, mix, xq, gate, k, v, gq, wout):
        g16 = dx.astype(BF16)
        do = _dot_nt(g16, wout)
        o, vjp = jax.vjp(_merge, mix, xq, gate, k, v, gq)
        dmix, dxq, dgate, dk, dv, dgq = vjp(do)
        return dmix, dxq, dgate, o, g16, dk, dv, dgq
    L = dx.shape[0]
    return _rowwise(
        name, fn,
        [('r', dx), (mix_kind, mix), ('r', xq), ('r', gate), ('c', k), ('c', v), ('c', gq), ('c', wout)],
        [('r', (L, PRIM), F32), ('r', (L, XQ), BF16), ('r', (L, BRANCH), BF16), ('t', (BRANCH, L), BF16),
         ('r', (L, D_MODEL), BF16), ('a', k.shape, F32), ('a', v.shape, F32), ('a', gq.shape, F32)], nblk, sub,
        host=host)


_MLA_IN = 3392
_MLA_IN_PAD = 3456


def _uq_rows(wt):
    r = wt.reshape(MLA_H, HD + ROPE, wt.shape[1])
    return jnp.concatenate([r[:, :HD].reshape(PRIM, -1),
                            jnp.pad(r[:, HD:], ((0, 0), (0, HD - ROPE), (0, 0))).reshape(PRIM, -1)], axis=0)


def _uq_rows_back(wt):
    nope = wt[:PRIM].reshape(MLA_H, HD, -1)
    rope = wt[PRIM:].reshape(MLA_H, HD, -1)[:, :ROPE]
    return jnp.concatenate([nope, rope], axis=1).reshape(MLA_H * (HD + ROPE), -1)


def _mla_in_rows(wt):
    return jnp.concatenate([wt[:768], wt[832:], wt[768:832], jnp.zeros((64, wt.shape[1]), wt.dtype)], axis=0)


def _mla_in_rows_back(wt):
    return jnp.concatenate([wt[:768], wt[3328:3392], wt[768:3328]], axis=0)


_SMALL = (("ln_gain", 2048), ("mem_norm", 2048), ("xq_norm", 256), ("xk_norm", 256), ("s5_lambda_re", 6144),
          ("s5_lambda_im", 6144), ("s5_log_step", 96), ("s5_b_re", 98304), ("s5_b_im", 98304), ("s5_c_re", 98304),
          ("s5_c_im", 98304), ("s5_d", 1536), ("mla_q_lora_norm", 512), ("mla_kv_lora_norm", 256),
          ("mla_q_nope_norm", 128), ("mla_k_nope_norm", 128), ("mla_q_rope_norm", 64), ("mla_k_rope_norm", 64))
_SMALL_ROWS = 432
_SMALL_OFF = {name: sum(n for _, n in _SMALL[:i]) for i, (name, _) in enumerate(_SMALL)}


def _pack_small(d):
    flat = jnp.concatenate([d[n].reshape(-1).astype(F32) for n, _ in _SMALL])
    return jnp.pad(flat, (0, _SMALL_ROWS * 1024 - flat.shape[0])).reshape(_SMALL_ROWS, 1024)


def _unpack_small(p, name, shape):
    off = _SMALL_OFF[name]
    return p.reshape(-1)[off:off + int(np.prod(shape))].reshape(shape)


_WEIGHTS = ('ln_gain', 'w_out', 'mem_norm', 'w_mem_kv', 'xq_norm', 'xk_norm', 's5_w_in', 's5_lambda_re',
            's5_lambda_im', 's5_log_step', 's5_b_re', 's5_b_im', 's5_c_re', 's5_c_im', 's5_d', 's5_w_glu', 'mla_w_in',
            'mla_q_lora_norm', 'mla_kv_lora_norm', 'mla_w_uq', 'mla_w_ukv', 'mla_q_nope_norm', 'mla_k_nope_norm',
            'mla_q_rope_norm', 'mla_k_rope_norm')
_BIG = ('w_out', 'w_mem_kv', 's5_w_in', 's5_w_glu', 'mla_w_in', 'mla_w_uq', 'mla_w_ukv')


def _pad128(g):
    return jnp.pad(g.reshape(1, -1), ((0, 0), (0, HD - g.shape[-1])))


def kernel(x, mem, positions, ln_gain, w_out, mem_norm, w_mem_kv, xq_norm, xk_norm, s5_w_in, s5_lambda_re, s5_lambda_im, s5_log_step, s5_b_re, s5_b_im, s5_c_re, s5_c_im, s5_d, s5_w_glu, mla_w_in, mla_q_lora_norm, mla_kv_lora_norm, mla_w_uq, mla_w_ukv, mla_q_nope_norm, mla_k_nope_norm, mla_q_rope_norm, mla_k_rope_norm, loss_target, m_ln_gain, m_w_out, m_mem_norm, m_w_mem_kv, m_xq_norm, m_xk_norm, m_s5_w_in, m_s5_lambda_re, m_s5_lambda_im, m_s5_log_step, m_s5_b_re, m_s5_b_im, m_s5_c_re, m_s5_c_im, m_s5_d, m_s5_w_glu, m_mla_w_in, m_mla_q_lora_norm, m_mla_kv_lora_norm, m_mla_w_uq, m_mla_w_ukv, m_mla_q_nope_norm, m_mla_k_nope_norm, m_mla_q_rope_norm, m_mla_k_rope_norm, v_ln_gain, v_w_out, v_mem_norm, v_w_mem_kv, v_xq_norm, v_xk_norm, v_s5_w_in, v_s5_lambda_re, v_s5_lambda_im, v_s5_log_step, v_s5_b_re, v_s5_b_im, v_s5_c_re, v_s5_c_im, v_s5_d, v_s5_w_glu, v_mla_w_in, v_mla_q_lora_norm, v_mla_kv_lora_norm, v_mla_w_uq, v_mla_w_ukv, v_mla_q_nope_norm, v_mla_k_nope_norm, v_mla_q_rope_norm, v_mla_k_rope_norm):
    weights = dict(ln_gain=ln_gain, w_out=w_out, mem_norm=mem_norm, w_mem_kv=w_mem_kv, xq_norm=xq_norm,
                   xk_norm=xk_norm, s5_w_in=s5_w_in, s5_lambda_re=s5_lambda_re, s5_lambda_im=s5_lambda_im,
                   s5_log_step=s5_log_step, s5_b_re=s5_b_re, s5_b_im=s5_b_im, s5_c_re=s5_c_re, s5_c_im=s5_c_im,
                   s5_d=s5_d, s5_w_glu=s5_w_glu, mla_w_in=mla_w_in, mla_q_lora_norm=mla_q_lora_norm,
                   mla_kv_lora_norm=mla_kv_lora_norm, mla_w_uq=mla_w_uq, mla_w_ukv=mla_w_ukv,
                   mla_q_nope_norm=mla_q_nope_norm, mla_k_nope_norm=mla_k_nope_norm,
                   mla_q_rope_norm=mla_q_rope_norm, mla_k_rope_norm=mla_k_rope_norm)
    m_in = dict(zip(_WEIGHTS, (m_ln_gain, m_w_out, m_mem_norm, m_w_mem_kv, m_xq_norm, m_xk_norm, m_s5_w_in,
                               m_s5_lambda_re, m_s5_lambda_im, m_s5_log_step, m_s5_b_re, m_s5_b_im, m_s5_c_re,
                               m_s5_c_im, m_s5_d, m_s5_w_glu, m_mla_w_in, m_mla_q_lora_norm, m_mla_kv_lora_norm,
                               m_mla_w_uq, m_mla_w_ukv, m_mla_q_nope_norm, m_mla_k_nope_norm, m_mla_q_rope_norm,
                               m_mla_k_rope_norm)))
    v_in = dict(zip(_WEIGHTS, (v_ln_gain, v_w_out, v_mem_norm, v_w_mem_kv, v_xq_norm, v_xk_norm, v_s5_w_in,
                               v_s5_lambda_re, v_s5_lambda_im, v_s5_log_step, v_s5_b_re, v_s5_b_im, v_s5_c_re,
                               v_s5_c_im, v_s5_d, v_s5_w_glu, v_mla_w_in, v_mla_q_lora_norm, v_mla_kv_lora_norm,
                               v_mla_w_uq, v_mla_w_ukv, v_mla_q_nope_norm, v_mla_k_nope_norm, v_mla_q_rope_norm,
                               v_mla_k_rope_norm)))

    x0 = x[0]
    mem0 = mem[0]
    target = loss_target[0]
    L = x0.shape[0]
    nblk, sub = 4, 1
    nb_big = 8
    me = 4 * lax.axis_index("x") + 2 * lax.axis_index("y") + lax.axis_index("c")

    lora = jnp.pad(jnp.concatenate([mla_q_lora_norm, mla_kv_lora_norm], axis=1), ((0, 7), (0, HD - 96)))
    def gather(*shards):
        return _plan_all_gather(list(shards))

    kh = D_MODEL // 2
    (b_mkv0, b_glu, b_in_mla, b_out0, b_uq, b_ukv, b_mkv1, b_out1), (W_in_s5,) = _cast_call(
        [w_mem_kv[0], s5_w_glu[0], jnp.transpose(mla_w_in[0]), w_out[0], jnp.transpose(mla_w_uq[0]), mla_w_ukv[0],
         w_mem_kv[1], w_out[1]], "cast_shards", host=gather(s5_w_in[0].astype(BF16)))

    ln0, ln1 = ln_gain[0:1], ln_gain[1:2]
    gq0, gq1 = xq_norm[0:1], xq_norm[1:2]
    gk0, gk1 = xk_norm[0:1], xk_norm[1:2]
    gm0, gm1 = mem_norm[0:1], mem_norm[1:2]
    gqn, gkn = mla_q_nope_norm, mla_k_nope_norm
    gqr, gkr = _pad128(mla_q_rope_norm), _pad128(mla_k_rope_norm)

    lr3 = s5_lambda_re.reshape(S5_G, 1, S5_P)
    li3 = s5_lambda_im.reshape(S5_G, 1, S5_P)
    ls3 = s5_log_step.reshape(S5_G, 1, 1)
    btr = jnp.swapaxes(s5_b_re[0], 1, 2)
    bti = jnp.swapaxes(s5_b_im[0], 1, 2)
    a_r, a_i, bm, cm = _s5_params(lr3, li3, ls3, btr, bti, s5_c_re[0], s5_c_im[0])
    a_r2 = a_r.reshape(1, S5_G * S5_P)
    a_i2 = a_i.reshape(1, S5_G * S5_P)
    cmask, rmat = _s5_compact_consts()

    half = ROPE // 2
    inv_freq = ROPE_THETA ** (-jnp.arange(half, dtype=F32) / half)
    invf = jnp.concatenate([inv_freq, inv_freq, jnp.zeros((HD - ROPE,), F32)]).reshape(1, HD)

    def rot_tables(pos, invf):
        ang = pos.astype(F32) * invf
        lane = lax.broadcasted_iota(jnp.int32, ang.shape, 1)
        c = jnp.where(lane < ROPE, jnp.cos(ang), 0.0)
        s = jnp.sin(ang)
        return c, jnp.where(lane < half, -s, 0.0), jnp.where((lane >= half) & (lane < ROPE), s, 0.0)

    tc, ts1, ts2 = _rowwise("rot_tables", rot_tables, [('r', positions.reshape(L, 1)), ('c', invf)],
                            [('r', (L, HD), F32)] * 3, nblk, sub)

    def in_s5(x, g, w):
        proj = _mm_slots(_rms(x, g, D_MODEL).astype(BF16), w)
        return proj[:, :PRIM], proj[:, PRIM:PRIM + XQ], proj[:, PRIM + XQ:]

    u_s5, xq_a, gate_a = _rowwise(
        "s5_in", in_s5, [('r', x0), ('c', ln0), ('c', W_in_s5)],
        [('r', (L, PRIM), F32), ('r', (L, XQ), F32), ('r', (L, BRANCH), F32)], nblk, sub)
    (y_s5, s5_carry), (W_glu, G_mkv0, G_in_mla_a) = _s5_fwd(u_s5, bm, cm, a_r2, a_i2, s5_d,
                                                            host=gather(b_glu, b_mkv0, b_in_mla[:, :kh]))

    def glu(y, w):
        z = _mm_slots(_gelu(y).astype(BF16), w)
        return (z[:, :PRIM] * _sigmoid(z[:, PRIM:]),)

    (y2,), (G_out0, G_in_mla_b) = _rowwise("s5_glu", glu, [('r', y_s5), ('c', W_glu)], [('r', (L, PRIM), F32)],
                                           nblk, sub, host=gather(b_out0, b_in_mla[:, kh:]))
    W_mkv0 = G_mkv0.reshape(D_MODEL, 2 * XQ)
    k_a, v_a = _kv_prep(mem0, gm0, W_mkv0, gk0, "kv_prep0")
    x1 = _forward_merge(x0, y2, 'r', xq_a, gate_a, k_a, v_a, gq0, G_out0.reshape(BRANCH, D_MODEL), "merge0", nblk,
                        sub)
    W_in_mla = _mla_in_rows(jnp.concatenate([G_in_mla_a, G_in_mla_b], axis=2).reshape(_MLA_IN, D_MODEL))

    def in_mla(x, g, w):
        proj = _dot_nt(_rms(x, g, D_MODEL).astype(BF16), w)
        return proj[:, :512], proj[:, 512:768], proj[:, 768:1280], proj[:, 1280:3328], proj[:, 3328:]

    (c_q, c_kv, xq_b, gate_b, krp), (G_uq, W_kv, G_lora) = _rowwise(
        "mla_in", in_mla, [('r', x1), ('c', ln1), ('c', W_in_mla)],
        [('r', (L, Q_LORA), F32), ('r', (L, KV_LORA), F32), ('r', (L, XQ), F32), ('r', (L, BRANCH), F32),
         ('r', (L, HD), F32)], nblk, sub,
        host=gather(b_uq, b_ukv, lora))
    W_q = _uq_rows(G_uq.reshape(MLA_H * (HD + ROPE), Q_LORA))
    g_qlora = G_lora[:, 0, :64].reshape(1, Q_LORA)
    g_kvlora = G_lora[:, 0, 64:96].reshape(1, KV_LORA)

    def qkv(c_q, c_kv, krp, tc, ts1, ts2, gql, gkvl, wq, wkv, gqn, gkn, gqr, gkr):
        q = _dot_nt(_rms(c_q, gql, Q_LORA).astype(BF16), wq)
        kv = _mm_slots(_rms(c_kv, gkvl, KV_LORA).astype(BF16), wkv)
        kp, v = _kv_post(kv, krp, gkn, gkr, tc, ts1, ts2)
        return _q_post(q, gqn, gqr, tc, ts1, ts2), kp, v

    qkv_consts = [('c', g_qlora), ('c', g_kvlora), ('c', W_q), ('c', W_kv), ('c', gqn), ('c', gkn), ('c', gqr),
                  ('c', gkr)]
    (q_pad, k_pad, v_h), (G_mkv1, G_out1) = _rowwise(
        "mla_qkv", qkv, [('r', c_q), ('r', c_kv), ('r', krp), ('r', tc), ('r', ts1), ('r', ts2)] + qkv_consts,
        [('r', (L, 2 * PRIM), BF16), ('r', (L, 2 * PRIM), BF16), ('r', (L, PRIM), BF16)], nblk, sub,
        host=gather(b_mkv1, b_out1))
    W_out = (G_out0.reshape(BRANCH, D_MODEL), G_out1.reshape(BRANCH, D_MODEL))
    W_mkv = (W_mkv0, G_mkv1.reshape(D_MODEL, 2 * XQ))
    scale = (HD + ROPE) ** -0.5
    attn, lse = _attn_fwd(q_pad, k_pad, v_h, scale)
    k_b, v_b = _kv_prep(mem0, gm1, W_mkv[1], gk1, "kv_prep1")

    def merge_loss(x, mix, xq, gate, k, v, gq, wout, t):
        err = x + _dot(_merge(mix, xq, gate, k, v, gq).astype(BF16), wout) - t
        part = 0.5 * jnp.sum(jnp.sum(err * err, axis=-1, keepdims=True) * (1.0 / D_MODEL), axis=0, keepdims=True)
        return err * (1.0 / D_MODEL), jnp.broadcast_to(part, (1, HD))

    dx2, loss_part = _rowwise(
        "merge1_loss", merge_loss,
        [('r', x1), ('r', attn), ('r', xq_b), ('r', gate_b), ('c', k_b), ('c', v_b), ('c', gq1), ('c', W_out[1]),
         ('r', target)], [('r', (L, D_MODEL), F32), ('a', (1, HD), F32)], nblk, sub)

    dattn, dxq_b, dgate_b, o_b, g_b, dk_b, dv_b, dgq1 = _backward_merge(
        dx2, attn, 'r', xq_b, gate_b, k_b, v_b, gq1, W_out[1], "merge1_bwd", nb_big, sub)
    dgm1, dW_mkv1, dgk1 = _kv_prep_bwd(mem0, gm1, W_mkv[1], gk1, dk_b, dv_b, "kv_prep1_bwd")
    dW_out1 = _matmul_tn(o_b, g_b, "dw_out1")
    dq_pad, dk_pad, dv_h = _attn_bwd(q_pad, k_pad, v_h, attn, lse, dattn, scale)

    def qkv_bwd(c_q, c_kv, krp, tc, ts1, ts2, dqp, dkp, dv, gql, gkvl, wq, wkv, gqn, gkn, gqr, gkr):
        cqn, vjp_qn = jax.vjp(lambda a, b: _rms(a, b, Q_LORA), c_q, gql)
        ckvn, vjp_kvn = jax.vjp(lambda a, b: _rms(a, b, KV_LORA), c_kv, gkvl)
        cqn16 = cqn.astype(BF16)
        ckvn16 = ckvn.astype(BF16)
        q = _dot_nt(cqn16, wq)
        kv = _mm_slots(ckvn16, wkv)
        _, vjp_q = jax.vjp(lambda a, b, c: _q_post(a, b, c, tc, ts1, ts2), q, gqn, gqr)
        dq, dgqn, dgqr = vjp_q(dqp.astype(F32))
        _, vjp_kv = jax.vjp(lambda a, b, c, d: _kv_post(a, b, c, d, tc, ts1, ts2), kv, krp, gkn, gkr)
        dkv, dkrp, dgkn, dgkr = vjp_kv((dkp.astype(F32), dv.astype(F32)))
        dq16 = dq.astype(BF16)
        dkv16 = dkv.astype(BF16)
        dc_q, dgql = vjp_qn(_dot(dq16, wq))
        dc_kv, dgkvl = vjp_kvn(_mm_slots_nt(dkv16, wkv))
        return dc_q, dc_kv, dkrp, cqn16, dq16, ckvn16, dkv16, dgql, dgkvl, dgqn, dgkn, dgqr, dgkr

    (dc_q, dc_kv, dkrp, cqn16, dq16, ckvn16, dkv16, dgql, dgkvl, dgqn, dgkn, dgqr, dgkr) = _rowwise(
        "mla_qkv_bwd", qkv_bwd,
        [('r', c_q), ('r', c_kv), ('r', krp), ('r', tc), ('r', ts1), ('r', ts2), ('r', dq_pad), ('r', dk_pad),
         ('r', dv_h)] + qkv_consts,
        [('r', (L, Q_LORA), BF16), ('r', (L, KV_LORA), BF16), ('r', (L, HD), BF16), ('r', (L, Q_LORA), BF16),
         ('t', (2 * PRIM, L), BF16), ('t', (KV_LORA, L), BF16), ('r', (L, 2 * PRIM), BF16),
         ('a', (1, Q_LORA), F32), ('a', (1, KV_LORA), F32), ('a', (1, HD), F32), ('a', (1, HD), F32),
         ('a', (1, HD), F32), ('a', (1, HD), F32)], nb_big, sub)
    dW_q = _matmul_tn(dq16, cqn16, "dw_uq")
    dW_kv = _matmul_tn_slots(ckvn16, dkv16, "dw_ukv")

    def in_bwd(x, dres, g, w, *dparts):
        dproj = jnp.concatenate(dparts, axis=-1).astype(BF16)
        xn, vjp = jax.vjp(lambda a, b: _rms(a, b, D_MODEL), x, g)
        dx, dg = vjp(_mm_slots_nt(dproj, w) if w.ndim == 3 else _dot(dproj, w))
        return dx + dres, xn, dproj, dg

    dx1, xn1, dproj1, dln1 = _rowwise(
        "mla_in_bwd", in_bwd,
        [('r', x1), ('r', dx2), ('c', ln1), ('c', W_in_mla), ('r', dc_q), ('r', dc_kv), ('r', dxq_b), ('r', dgate_b),
         ('r', dkrp)],
        [('r', (L, D_MODEL), F32), ('r', (L, D_MODEL), BF16), ('t', (_MLA_IN_PAD, L), BF16), ('a', (1, D_MODEL), F32)],
        nblk, sub)
    dW_in_mla = _matmul_tn(dproj1, xn1, "dw_mla_in")

    grads1 = [dW_out1.reshape(N_DEV, 256, D_MODEL), dW_mkv1.reshape(N_DEV, 128, 2 * XQ),
              _mla_in_rows_back(dW_in_mla).reshape(N_DEV, 424, D_MODEL),
              _uq_rows_back(dW_q).reshape(N_DEV, 288, Q_LORA), dW_kv]
    (dy2, dxq_a, dgate_a, o_a, g_a, dk_a, dv_a, dgq0), pair1 = _backward_merge(
        dx1, y2, 'r', xq_a, gate_a, k_a, v_a, gq0, W_out[0], "merge0_bwd", nb_big, sub, host=_plan_pair(grads1))
    dgm0, dW_mkv0, dgk0 = _kv_prep_bwd(mem0, gm0, W_mkv[0], gk0, dk_a, dv_a, "kv_prep0_bwd")
    dW_out0 = _matmul_tn(o_a, g_a, "dw_out0")
    t1 = list(_pair_add(grads1, pair1, "rs_add_layer1"))

    def glu_bwd(y, dy2, w):
        h, vjp_h = jax.vjp(_gelu, y)
        h16 = h.astype(BF16)
        z = _mm_slots(h16, w)
        _, vjp_z = jax.vjp(lambda z: z[:, :PRIM] * _sigmoid(z[:, PRIM:]), z)
        dz16 = vjp_z(dy2)[0].astype(BF16)
        return vjp_h(_mm_slots_nt(dz16, w))[0], h16, dz16

    grads0 = [dW_out0.reshape(N_DEV, 256, D_MODEL), dW_mkv0.reshape(N_DEV, 128, 2 * XQ)]
    (dy_s5, h16, dz16), glu_hosted = _rowwise(
        "s5_glu_bwd", glu_bwd, [('r', y_s5), ('r', dy2), ('c', W_glu)],
        [('r', (L, PRIM), F32), ('t', (PRIM, L), BF16), ('r', (L, 2 * PRIM), BF16)], nb_big, sub,
        host=_combine(_plan_chips(t1[2:]), _plan_pair(grads0)))
    recv_proj1, pair0 = glu_hosted[:3], glu_hosted[3:]
    dW_glu = _matmul_tn_slots(h16, dz16, "dw_glu")
    t0 = list(_pair_add(grads0 + [dW_glu], pair0 + list(_exchange_call(_plan_pair([dW_glu]), "rs_pair_glu")),
                        "rs_add_layer0"))
    (du_s5, dbc, dcc, dd, dar, dai), recv_rest = _s5_bwd(u_s5, dy_s5, s5_carry, bm, cm, a_r2, a_i2, s5_d,
                                                        cmask, rmat, host=_plan_chips(t1[:2] + t0))
    early_recv = recv_rest[:2] + recv_proj1 + recv_rest[2:]
    dbc4 = dbc.reshape(S5_G, S5_C, 2, S5_P)
    dcc4 = dcc.reshape(S5_G, S5_C, 2, S5_P)
    dlr, dli, dls, dbtr, dbti = _s5_params_bwd(
        lr3, li3, ls3, btr, bti, dar.reshape(S5_G, 1, S5_P), dai.reshape(S5_G, 1, S5_P), dbc4[:, :, 0], dbc4[:, :, 1])

    small_part = {
        "ln_gain": jnp.concatenate([jnp.zeros_like(dln1), dln1]), "mem_norm": jnp.concatenate([dgm0, dgm1]),
        "xq_norm": jnp.concatenate([dgq0, dgq1]), "xk_norm": jnp.concatenate([dgk0, dgk1]),
        "s5_lambda_re": dlr, "s5_lambda_im": dli, "s5_log_step": dls,
        "s5_b_re": jnp.swapaxes(dbtr, 1, 2), "s5_b_im": jnp.swapaxes(dbti, 1, 2),
        "s5_c_re": dcc4[:, :, 0], "s5_c_im": -dcc4[:, :, 1], "s5_d": dd,
        "mla_q_lora_norm": dgql, "mla_kv_lora_norm": dgkvl, "mla_q_nope_norm": dgqn, "mla_k_nope_norm": dgkn,
        "mla_q_rope_norm": dgqr[:, :ROPE], "mla_k_rope_norm": dgkr[:, :ROPE],
    }
    loss8 = jnp.pad(loss_part, ((0, 7), (0, 0)))
    (dx0, xn0, dproj0, dln0), (small_gath, loss_g) = _rowwise(
        "s5_in_bwd", in_bwd,
        [('r', x0), ('r', dx1), ('c', ln0), ('c', W_in_s5), ('r', du_s5), ('r', dxq_a),
         ('r', dgate_a)],
        [('r', (L, D_MODEL), F32), ('t', (D_MODEL, L), BF16), ('r', (L, 2 * BRANCH), BF16), ('a', (1, D_MODEL), F32)],
        nblk, sub, host=_plan_all_gather([_pack_small(small_part).astype(BF16), loss8]))
    dW_in_s5 = _matmul_tn_slots(xn0, dproj0, "dw_s5_in")

    late = [dW_in_s5]
    late_t = _pair_add(late, list(_exchange_call(_plan_pair(late), "rs_pair_late")), "rs_add_late")
    owners = [("w_out", 1), ("w_mem_kv", 1), ("mla_w_in", 0), ("mla_w_uq", 0), ("mla_w_ukv", 0), ("w_out", 0),
              ("w_mem_kv", 0), ("s5_w_glu", 0)]
    flipped = ("mla_w_in", "mla_w_uq")

    def shard(d, n, i):
        return jnp.transpose(d[n][i]) if n in flipped else d[n][i]

    upd, (late_recv, ln0_gath) = _updates_call(
        early_recv, [shard(weights, n, i) for n, i in owners], [shard(m_in, n, i) for n, i in owners],
        [shard(v_in, n, i) for n, i in owners], "update_early",
        host=_combine(_plan_chips(late_t), _plan_all_gather([jnp.pad(dln0, ((0, 7), (0, 0)))])))
    owners.append(("s5_w_in", 0))
    upd.append(_sum_adamw(late_recv, s5_w_in[0], m_s5_w_in[0], v_s5_w_in[0], "update_s5_w_in"))
    grads, delta, new_m, new_v = {}, {}, {}, {}
    for n in _BIG:
        parts = [u for u, (o, _) in sorted(zip(upd, owners), key=lambda t: t[1][1]) if o == n]
        if n in flipped:
            grads[n], delta[n], new_m[n], new_v[n] = (jnp.transpose(parts[0][j])[None] for j in range(4))
        else:
            grads[n], delta[n], new_m[n], new_v[n] = (jnp.stack([p[j] for p in parts]) for j in range(4))

    gs, loss_sum = _small_sum(small_gath, loss_g, ln0_gath, "small_sum")
    loss = loss_sum[0, 0]
    for n, _ in _SMALL:
        shape = weights[n].shape
        if n == "mla_q_lora_norm":
            grads[n] = lax.dynamic_slice(_unpack_small(gs, n, (Q_LORA,)), (me * 64,), (64,)).reshape(shape)
        elif n == "mla_kv_lora_norm":
            grads[n] = lax.dynamic_slice(_unpack_small(gs, n, (KV_LORA,)), (me * 32,), (32,)).reshape(shape)
        else:
            grads[n] = _unpack_small(gs, n, shape)

    def own(n, a):
        if a.ndim == 4:
            a = jnp.transpose(a, (0, 2, 3, 1))
        elif a.ndim == 3:
            a = jnp.transpose(a, (0, 2, 1))
        return a.reshape(a.shape[1:]) if a.ndim >= 3 else a

    def back(n, a):
        shape = weights[n].shape
        if len(shape) == 4:
            return jnp.transpose(a.reshape((1,) + a.shape), (0, 3, 1, 2))
        if len(shape) == 3:
            return jnp.transpose(a.reshape((1,) + a.shape), (0, 2, 1))
        return a.reshape(shape)

    wide = ("s5_b_re", "s5_b_im", "s5_c_re", "s5_c_im")
    for names, nb, call in (([n for n, _ in _SMALL if n not in wide], 1, "update_small"), (wide, 4, "update_s5_bc")):
        res = _adamw_multi([own(n, weights[n]) for n in names], [own(n, grads[n]) for n in names],
                           [own(n, m_in[n]) for n in names], [own(n, v_in[n]) for n in names], call, nb)
        for n, (dl, m2, v2) in zip(names, res):
            delta[n], new_m[n], new_v[n] = back(n, dl), back(n, m2), back(n, v2)
    return (loss, dx0[None], *[grads[n] for n in _WEIGHTS], *[delta[n] for n in _WEIGHTS],
            *[new_m[n] for n in _WEIGHTS], *[new_v[n] for n in _WEIGHTS])
```

```python
import functools
import math

import numpy as np
import jax
import jax.numpy as jnp
from jax import lax
from jax.experimental import pallas as pl
from jax.experimental.pallas import tpu as pltpu

F32 = jnp.float32
BF16 = jnp.bfloat16
EPS = 1e-6
NEG = float(np.finfo(np.float32).min)
MESH = pl.DeviceIdType.MESH

N_DEV = 8
D_MODEL = 1024
MEM_LEN = 256
XQ = 512
PRIM = 1536
BRANCH = 2048
X_HEADS = 4
HD = 128
S5_G = 96
S5_P = 64
S5_C = 16
S5_GB = 8
S5_W = S5_GB * S5_P
MLA_H = 12
ROPE = 64
Q_LORA = 512
KV_LORA = 256
ROPE_THETA = 10000.0

ADAM_LR = 0.001
ADAM_B1 = 0.9
ADAM_B2 = 0.999
ADAM_EPS = 1e-08
ADAM_WD = 0.01
ADAM_STEP = 10

VMEM_LIMIT = 56 * 1024 * 1024


def _dot(a, b):
    return jnp.dot(a, b, preferred_element_type=F32)


def _dot_nt(a, b):
    return lax.dot_general(a, b, (((1,), (1,)), ((), ())), preferred_element_type=F32)


def _dot_tn(a, b):
    return lax.dot_general(a, b, (((0,), (0,)), ((), ())), preferred_element_type=F32)


@jax.custom_vjp
def _mm(a, b):
    return _dot(a.astype(BF16), b.astype(BF16))


def _mm_fwd(a, b):
    return _mm(a, b), (a, b)


def _mm_bwd(res, g):
    a, b = res
    gb = g.astype(BF16)
    return _dot_nt(gb, b.astype(BF16)).astype(a.dtype), _dot_tn(a.astype(BF16), gb).astype(b.dtype)


_mm.defvjp(_mm_fwd, _mm_bwd)


@jax.custom_vjp
def _mm_nt(a, b):
    return _dot_nt(a.astype(BF16), b.astype(BF16))


def _mm_nt_fwd(a, b):
    return _mm_nt(a, b), (a, b)


def _mm_nt_bwd(res, g):
    a, b = res
    gb = g.astype(BF16)
    return _dot(gb, b.astype(BF16)).astype(a.dtype), _dot_tn(gb, a.astype(BF16)).astype(b.dtype)


_mm_nt.defvjp(_mm_nt_fwd, _mm_nt_bwd)


@jax.custom_vjp
def _softmax(s):
    m = jnp.max(s, axis=-1, keepdims=True)
    e = jnp.exp(s - m)
    return e / jnp.sum(e, axis=-1, keepdims=True)


def _softmax_fwd(s):
    p = _softmax(s)
    return p, p


def _softmax_bwd(p, g):
    return (p * (g - jnp.sum(p * g, axis=-1, keepdims=True)),)


_softmax.defvjp(_softmax_fwd, _softmax_bwd)


def _rms(x, g, n):
    ms = jnp.sum(x * x, axis=-1, keepdims=True) * (1.0 / n)
    return x * lax.rsqrt(ms + EPS) * g


def _sigmoid(x):
    return 1.0 / (1.0 + jnp.exp(-x))


def _silu(x):
    return x * _sigmoid(x)


def _gelu(x):
    c = math.sqrt(2.0 / math.pi)
    return 0.5 * x * (1.0 + jnp.tanh(c * (x + 0.044715 * (x * x * x))))


@jax.custom_vjp
def _rot(x, c, s1, s2):
    return x * c + pltpu.roll(x, 96, 1) * s1 + pltpu.roll(x, 32, 1) * s2


def _rot_fwd(x, c, s1, s2):
    return _rot(x, c, s1, s2), (c, s1, s2)


def _rot_bwd(res, g):
    c, s1, s2 = res
    dx = g * c + pltpu.roll(g * s1, 32, 1) + pltpu.roll(g * s2, 96, 1)
    return dx, jnp.zeros_like(c), jnp.zeros_like(s1), jnp.zeros_like(s2)


_rot.defvjp(_rot_fwd, _rot_bwd)


def _mem_attn(xq, k, v, gq):
    outs = []
    for h in range(X_HEADS):
        sl = slice(HD * h, HD * (h + 1))
        q = _rms(xq[:, sl], gq, HD)
        p = _softmax(_mm_nt(q, k[:, sl]) * (HD ** -0.5))
        outs.append(_mm(p, v[:, sl]))
    return jnp.concatenate(outs, axis=-1)


def _merge(mix, xq, gate, k, v, gq):
    return jnp.concatenate([mix, _mem_attn(xq, k, v, gq)], axis=-1) * _silu(gate)


def _q_post(q, gqn, gqr, c, s1, s2):
    pieces = []
    for h in range(MLA_H):
        pieces.append(_rms(q[:, HD * h:HD * (h + 1)], gqn, HD))
        pieces.append(_rot(_rms(q[:, PRIM + HD * h:PRIM + HD * (h + 1)], gqr, ROPE), c, s1, s2))
    return jnp.concatenate(pieces, axis=-1)


def _kv_post(kv, krp, gkn, gkr, c, s1, s2):
    kr = _rot(_rms(krp, gkr, ROPE), c, s1, s2)
    pieces, vals = [], []
    for h in range(MLA_H):
        pieces.append(_rms(kv[:, 2 * HD * h:2 * HD * h + HD], gkn, HD))
        pieces.append(kr)
        vals.append(kv[:, 2 * HD * h + HD:2 * HD * (h + 1)])
    return jnp.concatenate(pieces, axis=-1), jnp.concatenate(vals, axis=-1)


def _rowwise(name, fn, ins, outs, nblk, host=None):
    n_in = len(ins)

    def spec(kind, shape):
        if kind == 'r':
            return pl.BlockSpec((shape[0] // nblk, shape[1]), lambda i: (i, 0))
        if kind == 't':
            return pl.BlockSpec((shape[0], shape[1] // nblk), lambda i: (0, i))
        zeros = (0,) * len(shape)
        return pl.BlockSpec(tuple(shape), lambda i: zeros)

    def body(*refs):
        i = pl.program_id(0)
        res = fn(*[r[...] for r in refs[:n_in]])
        for (kind, _, _), ref, val in zip(outs, refs[n_in:], res):
            if kind == 'a':
                @pl.when(i == 0)
                def _():
                    ref[...] = jnp.zeros_like(ref)
                ref[...] += val.astype(ref.dtype)
            elif kind == 't':
                ref[...] = val.astype(F32).T.astype(ref.dtype)
            else:
                ref[...] = val.astype(ref.dtype)

    res, hosted = _hosting_call(
        body, name, nblk, host, [a for _, a in ins], [spec(k, a.shape) for k, a in ins],
        [jax.ShapeDtypeStruct(tuple(s), d) for _, s, d in outs], [spec(k, s) for k, s, _ in outs], [])
    return res if host is None else (res, hosted)


def _matmul_tn(at, g, name, out_dtype=BF16):
    K, L = at.shape
    N = g.shape[1]
    tn = next(t for t in (512, 384, 256, 128) if N % t == 0)

    def body(a_ref, g_ref, o_ref):
        o_ref[...] = _dot(a_ref[...], g_ref[...]).astype(o_ref.dtype)

    return pl.pallas_call(
        body, name=name, grid=(N // tn,),
        in_specs=[pl.BlockSpec((K, L), lambda n: (0, 0)), pl.BlockSpec((L, tn), lambda n: (0, n))],
        out_specs=pl.BlockSpec((K, tn), lambda n: (0, n)),
        out_shape=jax.ShapeDtypeStruct((K, N), out_dtype),
        compiler_params=pltpu.CompilerParams(dimension_semantics=("arbitrary",), vmem_limit_bytes=VMEM_LIMIT),
    )(at, g)


def _matmul_tn_slots(at, g, name, host=None):
    K, L = at.shape
    n = g.shape[1] // N_DEV

    def body(a_ref, g_ref, o_ref):
        o_ref[...] = _dot(a_ref[...], g_ref[...]).astype(o_ref.dtype)

    res, hosted = _hosting_call(
        body, name, N_DEV, host, [at, g],
        [pl.BlockSpec((K, L), lambda d: (0, 0)), pl.BlockSpec((L, n), lambda d: (0, d))],
        [jax.ShapeDtypeStruct((N_DEV, K, n), BF16)], [pl.BlockSpec((None, K, n), lambda d: (d, 0, 0))], [])
    return res[0] if host is None else (res[0], hosted)


def _mm_slots(a16, w):
    return jnp.concatenate([_dot(a16, w[d]) for d in range(N_DEV)], axis=-1)


def _mm_slots_nt(g16, w):
    n = w.shape[2]
    out = _dot_nt(g16[:, 0:n], w[0])
    for d in range(1, N_DEV):
        out = out + _dot_nt(g16[:, d * n:(d + 1) * n], w[d])
    return out


class _Exchange:
    def __init__(self, ins, outs, scratch, start, finish):
        self.ins, self.outs, self.scratch, self.start, self.finish = ins, outs, scratch, start, finish


def _xyc():
    return lax.axis_index("x"), lax.axis_index("y"), lax.axis_index("c")


def _plan_all_gather(xs):
    n = len(xs)

    def build(x_refs, out_refs, sems):
        send_sems, recv_sems, local_sems = sems
        x, y, c = _xyc()

        def copies(k, block, to, own=False):
            slot = 4 * block[0] + 2 * block[1] + block[2]
            return [pltpu.make_async_remote_copy(
                src_ref=x_refs[a] if own else out_refs[a].at[slot], dst_ref=out_refs[a].at[slot],
                send_sem=send_sems.at[k * n + a], recv_sem=recv_sems.at[k * n + a], device_id=to,
                device_id_type=MESH) for a in range(n)]

        mine = [pltpu.make_async_copy(x_refs[a], out_refs[a].at[4 * x + 2 * y + c], local_sems.at[a])
                for a in range(n)]
        return copies, mine, (x, y, c), [(1 - x, y), (x, 1 - y), (1 - x, 1 - y)]

    def first_copies(copies, me, chips):
        x, y, c = me
        first = copies(0, me, (x, y, 1 - c), own=True)
        for j, chip in enumerate(chips):
            first += copies(1 + j, me, (*chip, c), own=True)
        return first

    def start(x_refs, out_refs, sems):
        copies, mine, me, chips = build(x_refs, out_refs, sems)
        for cp in mine + first_copies(copies, me, chips):
            cp.start()

    def finish(x_refs, out_refs, sems):
        copies, mine, me, chips = build(x_refs, out_refs, sems)
        x, y, c = me
        passed = []
        for j, chip in enumerate(chips):
            for cp in copies(1 + j, (*chip, c), me):
                cp.wait_recv()
            fwd = copies(4 + j, (*chip, c), (x, y, 1 - c))
            for cp in fwd:
                cp.start()
            passed += fwd
        for cp in copies(0, (x, y, 1 - c), me):
            cp.wait_recv()
        for j, chip in enumerate(chips):
            for cp in copies(4 + j, (*chip, 1 - c), me):
                cp.wait_recv()
        for cp in first_copies(copies, me, chips) + passed:
            cp.wait_send()
        for cp in mine:
            cp.wait()

    return _Exchange(list(xs), [jax.ShapeDtypeStruct((N_DEV,) + a.shape, a.dtype) for a in xs],
                     [pltpu.SemaphoreType.DMA((7 * n,)), pltpu.SemaphoreType.DMA((7 * n,)),
                      pltpu.SemaphoreType.DMA((n,))], start, finish)


_CHIPS = ((0, 0), (0, 1), (1, 0), (1, 1))


def _plan_pair(sends):
    n = len(sends)

    def build(s_refs, o_refs, sems):
        send_sems, recv_sems = sems
        x, y, c = _xyc()
        return [pltpu.make_async_remote_copy(
            src_ref=s_refs[a].at[4 * px + 2 * py + 1 - c], dst_ref=o_refs[a].at[j],
            send_sem=send_sems.at[j * n + a], recv_sem=recv_sems.at[j * n + a], device_id=(x, y, 1 - c),
            device_id_type=MESH) for j, (px, py) in enumerate(_CHIPS) for a in range(n)]

    def start(s_refs, o_refs, sems):
        for cp in build(s_refs, o_refs, sems):
            cp.start()

    def finish(s_refs, o_refs, sems):
        for cp in build(s_refs, o_refs, sems):
            cp.wait_recv()
            cp.wait_send()

    return _Exchange(list(sends), [jax.ShapeDtypeStruct((4,) + a.shape[1:], a.dtype) for a in sends],
                     [pltpu.SemaphoreType.DMA((4 * n,)), pltpu.SemaphoreType.DMA((4 * n,))], start, finish)


def _plan_chips(ts):
    n = len(ts)
    flips = ((1, 0), (0, 1), (1, 1))

    def build(t_refs, o_refs, sems):
        send_sems, recv_sems, local_sems = sems
        x, y, c = _xyc()
        mine = 2 * x + y
        local = [pltpu.make_async_copy(t_refs[a].at[mine], o_refs[a].at[mine], local_sems.at[a]) for a in range(n)]
        remote = []
        for k, (fx, fy) in enumerate(flips):
            px = 1 - x if fx else x
            py = 1 - y if fy else y
            remote += [pltpu.make_async_remote_copy(
                src_ref=t_refs[a].at[2 * px + py], dst_ref=o_refs[a].at[mine],
                send_sem=send_sems.at[k * n + a], recv_sem=recv_sems.at[k * n + a], device_id=(px, py, c),
                device_id_type=MESH) for a in range(n)]
        return local, remote

    def start(t_refs, o_refs, sems):
        local, remote = build(t_refs, o_refs, sems)
        for cp in local + remote:
            cp.start()

    def finish(t_refs, o_refs, sems):
        local, remote = build(t_refs, o_refs, sems)
        for cp in remote:
            cp.wait_recv()
        for cp in remote:
            cp.wait_send()
        for cp in local:
            cp.wait()

    return _Exchange(list(ts), [jax.ShapeDtypeStruct(a.shape, a.dtype) for a in ts],
                     [pltpu.SemaphoreType.DMA((3 * n,)), pltpu.SemaphoreType.DMA((3 * n,)),
                      pltpu.SemaphoreType.DMA((n,))], start, finish)


def _combine(*plans):
    def parts(refs, attr):
        out, at = [], 0
        for p in plans:
            n = len(getattr(p, attr))
            out.append(refs[at:at + n])
            at += n
        return out

    def run(half):
        def go(ins, outs, sems):
            for p, a, o, s in zip(plans, parts(ins, "ins"), parts(outs, "outs"), parts(sems, "scratch")):
                getattr(p, half)(a, o, s)
        return go

    return _Exchange(sum((p.ins for p in plans), []), sum((p.outs for p in plans), []),
                     sum((p.scratch for p in plans), []), run("start"), run("finish"))


def _exchange_call(plan, name):
    n = len(plan.ins)

    def body(*refs):
        ins, outs, sems = refs[:n], refs[n:2 * n], refs[2 * n:]
        plan.start(ins, outs, sems)
        plan.finish(ins, outs, sems)

    return pl.pallas_call(
        body, name=name, out_shape=plan.outs,
        in_specs=[pl.BlockSpec(memory_space=pl.ANY)] * n, out_specs=[pl.BlockSpec(memory_space=pl.ANY)] * n,
        scratch_shapes=plan.scratch,
    )(*plan.ins)


def _slab_spec(lead, rows, cols, nb):
    if rows % (nb * 16) == 0:
        return pl.BlockSpec((lead, rows // nb, cols), lambda i: (0, i, 0))
    if cols % (nb * 128) == 0:
        return pl.BlockSpec((lead, rows, cols // nb), lambda i: (0, 0, i))
    return pl.BlockSpec((lead, rows, cols), lambda i: (0, 0, 0))


def _slab_spec2(rows, cols, nb):
    if rows % (nb * 16) == 0:
        return pl.BlockSpec((rows // nb, cols), lambda i: (i, 0))
    if cols % (nb * 128) == 0:
        return pl.BlockSpec((rows, cols // nb), lambda i: (0, i))
    return pl.BlockSpec((rows, cols), lambda i: (0, 0))


def _cast_call(arrays, name, host=None):
    n = len(arrays)
    nb = 8

    def body(*refs):
        for a in range(n):
            refs[n + a][...] = refs[a][...].astype(BF16)

    specs = [_slab_spec2(x.shape[0], x.shape[1], nb) for x in arrays]
    return _hosting_call(body, name, nb, host, list(arrays), specs,
                         [jax.ShapeDtypeStruct(x.shape, BF16) for x in arrays], specs, [])


def _pair_add(sends, fromsib, name):
    n = len(sends)
    nb = 8

    def body(*refs):
        c = lax.axis_index("c")
        for a in range(n):
            s_ref, f_ref, t_ref = refs[a], refs[n + a], refs[2 * n + a]
            for j in range(4):
                t_ref[j] = (s_ref[2 * j + c].astype(F32) + f_ref[j].astype(F32)).astype(t_ref.dtype)

    def spec(a, lead):
        return _slab_spec(lead, a.shape[1], a.shape[2], nb)

    return pl.pallas_call(
        body, name=name, grid=(nb,),
        in_specs=[spec(a, N_DEV) for a in sends] + [spec(a, 4) for a in fromsib],
        out_specs=[spec(a, 4) for a in fromsib],
        out_shape=[jax.ShapeDtypeStruct(a.shape, a.dtype) for a in fromsib],
        compiler_params=pltpu.CompilerParams(dimension_semantics=("arbitrary",), vmem_limit_bytes=VMEM_LIMIT),
    )(*sends, *fromsib)


def _adamw_vals(w, g, m, v):
    m2 = ADAM_B1 * m + (1.0 - ADAM_B1) * g
    v2 = ADAM_B2 * v + (1.0 - ADAM_B2) * (g * g)
    m_hat = m2 / (1.0 - ADAM_B1 ** ADAM_STEP)
    v_hat = v2 / (1.0 - ADAM_B2 ** ADAM_STEP)
    delta = -ADAM_LR * (m_hat / (jnp.sqrt(v_hat) + ADAM_EPS) + ADAM_WD * w)
    return delta, m2, v2


def _sum_adamw(recv, w, m, v, name):
    R, C = w.shape
    ns = recv.shape[0]
    br = next((t for t in (256, 128, 64, 32, 16) if R % t == 0), R)

    def body(r_ref, w_ref, m_ref, v_ref, g_ref, d_ref, m2_ref, v2_ref):
        g = r_ref[0].astype(F32)
        for d in range(1, ns):
            g = g + r_ref[d].astype(F32)
        dl, m2, v2 = _adamw_vals(w_ref[...], g, m_ref[...], v_ref[...])
        g_ref[...] = g
        d_ref[...] = dl
        m2_ref[...] = m2
        v2_ref[...] = v2

    spec = pl.BlockSpec((br, C), lambda i: (i, 0))
    return pl.pallas_call(
        body, name=name, grid=(R // br,),
        in_specs=[pl.BlockSpec((ns, br, C), lambda i: (0, i, 0)), spec, spec, spec], out_specs=[spec] * 4,
        out_shape=[jax.ShapeDtypeStruct((R, C), F32)] * 4,
        compiler_params=pltpu.CompilerParams(dimension_semantics=("arbitrary",)),
    )(recv, w, m, v)


def _updates_call(recvs, ws, ms, vs, name, host=None):
    n = len(recvs)
    nb = 8

    def body(*refs):
        for a in range(n):
            r_ref, w_ref, m_ref, v_ref = refs[a], refs[n + a], refs[2 * n + a], refs[3 * n + a]
            g_ref, d_ref, m2_ref, v2_ref = refs[4 * n + 4 * a:4 * n + 4 * a + 4]
            g = r_ref[0].astype(F32)
            for d in range(1, r_ref.shape[0]):
                g = g + r_ref[d].astype(F32)
            dl, m2, v2 = _adamw_vals(w_ref[...], g, m_ref[...], v_ref[...])
            g_ref[...] = g
            d_ref[...] = dl
            m2_ref[...] = m2
            v2_ref[...] = v2

    def spec3(r):
        return _slab_spec(r.shape[0], r.shape[1], r.shape[2], nb)

    def spec2(w):
        return _slab_spec2(w.shape[0], w.shape[1], nb)

    res, hosted = _hosting_call(
        body, name, nb, host, list(recvs) + list(ws) + list(ms) + list(vs),
        [spec3(r) for r in recvs] + [spec2(w) for w in ws] * 3,
        [jax.ShapeDtypeStruct(w.shape, F32) for w in ws for _ in range(4)],
        [spec2(w) for w in ws for _ in range(4)], [])
    return [res[4 * a:4 * a + 4] for a in range(n)], hosted


def _small_sum(gath, loss_g, row0_g, name):
    _, R, C = gath.shape
    br = R // 3

    def body(g_ref, l_ref, r_ref, go_ref, lo_ref):
        g = g_ref[0].astype(F32)
        lsum = l_ref[0]
        for d in range(1, N_DEV):
            g = g + g_ref[d].astype(F32)
            lsum = lsum + l_ref[d]
        go_ref[...] = g
        lo_ref[...] = lsum

        @pl.when(pl.program_id(0) == 0)
        def _():
            row0 = r_ref[0]
            for d in range(1, N_DEV):
                row0 = row0 + r_ref[d]
            go_ref[0:8, :] = go_ref[0:8, :] + jnp.where(lax.broadcasted_iota(jnp.int32, row0.shape, 0) == 0, row0, 0.0)

    return pl.pallas_call(
        body, name=name, grid=(R // br,),
        in_specs=[pl.BlockSpec((N_DEV, br, C), lambda i: (0, i, 0)),
                  pl.BlockSpec((N_DEV, 8, HD), lambda i: (0, 0, 0)), pl.BlockSpec((N_DEV, 8, C), lambda i: (0, 0, 0))],
        out_specs=[pl.BlockSpec((br, C), lambda i: (i, 0)), pl.BlockSpec((8, HD), lambda i: (0, 0))],
        out_shape=[jax.ShapeDtypeStruct((R, C), F32), jax.ShapeDtypeStruct((8, HD), F32)],
        compiler_params=pltpu.CompilerParams(dimension_semantics=("arbitrary",)),
    )(gath, loss_g, row0_g)


def _adamw_multi(ws, gs, ms, vs, name, nblk=1):
    n = len(ws)

    def body(*refs):
        for a in range(n):
            dl, m2, v2 = _adamw_vals(refs[a][...], refs[n + a][...], refs[2 * n + a][...], refs[3 * n + a][...])
            refs[4 * n + 3 * a][...] = dl
            refs[4 * n + 3 * a + 1][...] = m2
            refs[4 * n + 3 * a + 2][...] = v2

    def spec(x):
        rest = (0,) * (x.ndim - 1)
        return pl.BlockSpec((x.shape[0] // nblk,) + tuple(x.shape[1:]), lambda i: (i,) + rest)

    res = pl.pallas_call(
        body, name=name, grid=(nblk,),
        in_specs=[spec(w) for w in ws] * 4, out_specs=[spec(w) for w in ws for _ in range(3)],
        out_shape=[jax.ShapeDtypeStruct(w.shape, F32) for w in ws for _ in range(3)],
        compiler_params=pltpu.CompilerParams(dimension_semantics=("arbitrary",), vmem_limit_bytes=VMEM_LIMIT),
    )(*ws, *gs, *ms, *vs)
    return [res[3 * a:3 * a + 3] for a in range(n)]


def _s5_param_fn(lr, li, ls, btr, bti):
    step = jnp.exp(ls)
    er = jnp.exp(lr * step)
    ang = li * step
    ar = er * jnp.cos(ang)
    ai = er * jnp.sin(ang)
    nr = ar - 1.0
    den = lr * lr + li * li
    fr = (nr * lr + ai * li) / den
    fi = (ai * lr - nr * li) / den
    return ar, ai, fr * btr - fi * bti, fr * bti + fi * btr


def _s5_params(lr, li, ls, btr, bti, cre, cim):
    nb = S5_G // S5_GB
    GC = S5_GB * S5_C
    expand = jnp.asarray(np.tile(np.eye(S5_P, dtype=np.float32), (1, S5_GB)), BF16)
    own = jnp.asarray((np.arange(GC)[:, None] // S5_C == np.arange(S5_W)[None, :] // S5_P).astype(np.float32))

    def body(lr_ref, li_ref, ls_ref, br_ref, bi_ref, cr_ref, ci_ref, e_ref, own_ref, ar_ref, ai_ref, bm_ref, cm_ref):
        ar, ai, bbr, bbi = _s5_param_fn(lr_ref[...], li_ref[...], ls_ref[...], br_ref[...], bi_ref[...])
        ar_ref[...] = ar
        ai_ref[...] = ai

        def plane(x, n):
            rows = x[n * S5_GB:(n + 1) * S5_GB].reshape(GC, S5_P).astype(BF16)
            return _dot(rows, e_ref[...]) * own_ref[...]

        for n in range(nb):
            bm_ref[n] = jnp.concatenate([plane(bbr, n), plane(bbi, n)], axis=-1).astype(BF16)
            cm_ref[n] = jnp.concatenate([plane(cr_ref[...], n), -plane(ci_ref[...], n)], axis=-1).astype(BF16)

    sd = jax.ShapeDtypeStruct
    return pl.pallas_call(
        body, name="s5_params",
        out_shape=[sd(lr.shape, F32), sd(lr.shape, F32), sd((nb, GC, 2 * S5_W), BF16), sd((nb, GC, 2 * S5_W), BF16)],
        compiler_params=pltpu.CompilerParams(vmem_limit_bytes=VMEM_LIMIT),
    )(lr, li, ls, btr, bti, cre, cim, expand, own)


def _s5_params_bwd(lr, li, ls, btr, bti, dar, dai, dbbr, dbbi):
    def body(lr_ref, li_ref, ls_ref, br_ref, bi_ref, dar_ref, dai_ref, dbbr_ref, dbbi_ref,
             dlr_ref, dli_ref, dls_ref, dbr_ref, dbi_ref):
        _, vjp = jax.vjp(_s5_param_fn, lr_ref[...], li_ref[...], ls_ref[...], br_ref[...], bi_ref[...])
        dlr, dli, dls, dbr, dbi = vjp((dar_ref[...], dai_ref[...], dbbr_ref[...], dbbi_ref[...]))
        dlr_ref[...] = dlr
        dli_ref[...] = dli
        dls_ref[...] = dls
        dbr_ref[...] = dbr
        dbi_ref[...] = dbi

    sd = jax.ShapeDtypeStruct
    return pl.pallas_call(
        body, name="s5_params_bwd",
        out_shape=[sd(lr.shape, F32), sd(lr.shape, F32), sd(ls.shape, F32), sd(btr.shape, F32), sd(btr.shape, F32)],
    )(lr, li, ls, btr, bti, dar, dai, dbbr, dbbi)


def _cpow(ar, ai, n):
    assert n & (n - 1) == 0
    while n > 1:
        ar, ai = ar * ar - ai * ai, 2.0 * ar * ai
        n //= 2
    return ar, ai


def _scan(st, cr, ci, init, nk, reverse, store, prev=None):
    W = S5_W

    def advance(k, sr, si):
        rows = pl.ds(k * 8 if isinstance(k, int) else pl.multiple_of(k * 8, 8), 8)
        nsr = cr * sr - ci * si + st[rows, 0:W]
        nsi = cr * si + ci * sr + st[rows, W:2 * W]
        if store:
            st[rows, 0:W] = nsr
            st[rows, W:2 * W] = nsi
        return nsr, nsi

    if prev is None:
        return lax.fori_loop(0, nk, lambda j, c: advance(nk - 1 - j if reverse else j, c[0], c[1]), init, unroll=2)
    assert reverse

    def step(j, carry):
        k = nk - 1 - j
        nsr, nsi = advance(k, carry[0], carry[1])
        prows = pl.ds(pl.multiple_of((k - 1) * 8, 8), 8)
        pr = prev[prows, 0:W]
        pi = prev[prows, W:2 * W]
        return nsr, nsi, carry[2] + nsr * pr + nsi * pi, carry[3] + nsi * pr - nsr * pi

    carry = lax.fori_loop(0, nk - 1, step, init, unroll=2)
    nsr, nsi = advance(0, carry[0], carry[1])
    return nsr, nsi, carry[2], carry[3]


def _chain(fin, fr, fi, pr, pi, reverse):
    W = S5_W
    fin[:, 0:W] = fr
    fin[:, W:2 * W] = fi
    rowid = lax.broadcasted_iota(jnp.int32, (8, W), 0)
    cr = jnp.zeros((1, W), F32)
    ci = jnp.zeros((1, W), F32)
    init_r = jnp.zeros((8, W), F32)
    init_i = jnp.zeros((8, W), F32)
    for s in (range(7, -1, -1) if reverse else range(8)):
        init_r = jnp.where(rowid == s, cr, init_r)
        init_i = jnp.where(rowid == s, ci, init_i)
        lr = fin[s:s + 1, 0:W]
        li = fin[s:s + 1, W:2 * W]
        cr, ci = lr + pr * cr - pi * ci, li + pr * ci + pi * cr
    return init_r, init_i


def _full_scan(st, fin, ar, ai, nk, reverse, prev=None, carry_in=None, carry_out=None):
    W = S5_W
    cr = jnp.broadcast_to(ar, (8, W))
    ci = jnp.broadcast_to(-ai if reverse else ai, (8, W))
    z = jnp.zeros((8, W), F32)
    if carry_in is None:
        fr, fi = _scan(st, cr, ci, (z, z), nk, reverse, store=False)
        pr, pi = _cpow(ar, -ai if reverse else ai, nk)
        init = _chain(fin, fr, fi, pr, pi, reverse)
    else:
        init = (carry_in[:, 0:W], carry_in[:, W:2 * W])
    if carry_out is not None:
        carry_out[:, 0:W] = init[0]
        carry_out[:, W:2 * W] = init[1]
    if prev is None:
        return _scan(st, cr, ci, init, nk, reverse, store=True)
    return _scan(st, cr, ci, init + (z, z), nk, reverse, store=True, prev=prev)


def _s5_specs(L):
    W2 = 2 * S5_W
    GC = S5_GB * S5_C
    col = pl.BlockSpec((L, GC), lambda g: (0, g))
    vec = pl.BlockSpec((1, GC), lambda g: (0, g))
    avec = pl.BlockSpec((1, S5_W), lambda g: (0, g))
    bmat = pl.BlockSpec((None, GC, W2), lambda g: (g, 0, 0))
    cmat = pl.BlockSpec((None, W2, GC), lambda g: (g, 0, 0))
    return col, vec, avec, bmat, cmat


def _interleave(dst, src, nk):
    for s in range(8):
        dst[pl.ds(s, nk, stride=8), :] = src[s * nk:(s + 1) * nk, :]


def _deinterleave(dst, src, nk):
    for s in range(8):
        dst[s * nk:(s + 1) * nk, :] = src[pl.ds(s, nk, stride=8), :].astype(dst.dtype)


def _hosting_call(body, name, nsteps, host, ins, in_specs, outs, out_specs, scratch):
    grid = (nsteps,) if isinstance(nsteps, int) else tuple(nsteps)
    params = pltpu.CompilerParams(dimension_semantics=("arbitrary",) * len(grid), vmem_limit_bytes=VMEM_LIMIT)
    if host is None:
        res = pl.pallas_call(
            body, name=name, grid=grid, in_specs=in_specs, out_specs=out_specs, out_shape=outs,
            scratch_shapes=scratch, compiler_params=params,
        )(*ins)
        return list(res), []
    n_in, n_out, n_sc = len(ins), len(outs), len(scratch)
    h_in, h_out = len(host.ins), len(host.outs)

    def hosted(*refs):
        a = refs[:n_in]
        ha = refs[n_in:n_in + h_in]
        o = refs[n_in + h_in:n_in + h_in + n_out]
        ho = refs[n_in + h_in + n_out:n_in + h_in + n_out + h_out]
        sc = refs[n_in + h_in + n_out + h_out:n_in + h_in + n_out + h_out + n_sc]
        hs = refs[n_in + h_in + n_out + h_out + n_sc:]
        first = functools.reduce(jnp.logical_and, [pl.program_id(i) == 0 for i in range(len(grid))])
        last = functools.reduce(jnp.logical_and, [pl.program_id(i) == g - 1 for i, g in enumerate(grid)])

        @pl.when(first)
        def _():
            host.start(ha, ho, hs)

        body(*a, *o, *sc)

        @pl.when(last)
        def _():
            host.finish(ha, ho, hs)

    hbm = pl.BlockSpec(memory_space=pl.ANY)
    res = pl.pallas_call(
        hosted, name=name, grid=grid,
        in_specs=list(in_specs) + [hbm] * h_in, out_specs=list(out_specs) + [hbm] * h_out,
        out_shape=list(outs) + list(host.outs), scratch_shapes=list(scratch) + list(host.scratch),
        compiler_params=params,
    )(*ins, *host.ins)
    return list(res[:n_out]), list(res[n_out:])


def _s5_fwd(u, bm, cm, ar, ai, dvec, host=None):
    L = u.shape[0]
    nk = L // 8
    GC = S5_GB * S5_C
    nb = S5_G // S5_GB
    col, vec, avec, bmat, cmat = _s5_specs(L)

    def body(u_ref, b_ref, c_ref, ar_ref, ai_ref, d_ref, y_ref, carry_ref, st, fin, ui, yi):
        _interleave(ui, u_ref, nk)
        for r in range(8):
            rows = slice(r * nk, (r + 1) * nk)
            st[rows, :] = _dot(ui[rows, :].astype(BF16), b_ref[...])
        _full_scan(st, fin, ar_ref[...], ai_ref[...], nk, reverse=False, carry_out=carry_ref)
        for r in range(8):
            rows = slice(r * nk, (r + 1) * nk)
            yi[rows, :] = _dot_nt(st[rows, :].astype(BF16), c_ref[...]) + d_ref[...] * ui[rows, :]
        _deinterleave(y_ref, yi, nk)

    return _hosting_call(
        body, "s5_fwd", nb, host,
        [u, bm, cm, ar, ai, dvec], [col, bmat, bmat, avec, avec, vec],
        [jax.ShapeDtypeStruct(u.shape, F32), jax.ShapeDtypeStruct((nb * 8, 2 * S5_W), F32)],
        [col, pl.BlockSpec((8, 2 * S5_W), lambda g: (g, 0))],
        [pltpu.VMEM((L, 2 * S5_W), F32), pltpu.VMEM((8, 2 * S5_W), F32), pltpu.VMEM((L, GC), F32),
         pltpu.VMEM((L, GC), F32)])


def _s5_bwd(u, dy, carry, bm, cm, ar, ai, dvec, mask, rmat, host=None):
    L = u.shape[0]
    nk = L // 8
    W = S5_W
    GC = S5_GB * S5_C
    col, vec, avec, bmat, cmat = _s5_specs(L)
    hi = lax.Precision.HIGHEST

    def body(u_ref, dy_ref, carry_ref, b_ref, ct_ref, ar_ref, ai_ref, d_ref, mask_ref, r_ref,
             du_ref, db_ref, dc_ref, dd_ref, dar_ref, dai_ref, sa, sb, fin, ui, dyi, dui):
        ar = ar_ref[...]
        ai = ai_ref[...]
        _interleave(ui, u_ref, nk)
        _interleave(dyi, dy_ref, nk)
        for r in range(8):
            rows = slice(r * nk, (r + 1) * nk)
            sa[rows, :] = _dot(ui[rows, :].astype(BF16), b_ref[...])
            sb[rows, :] = _dot(dyi[rows, :].astype(BF16), ct_ref[...])
        _full_scan(sa, fin, ar, ai, nk, reverse=False, carry_in=carry_ref)
        gr, gi, accr, acci = _full_scan(sb, fin, ar, ai, nk, reverse=True, prev=sa)
        rowid = lax.broadcasted_iota(jnp.int32, (8, W), 0)
        last = pl.ds((nk - 1) * 8, 8)
        pr = jnp.where(rowid == 0, 0.0, pltpu.roll(sa[last, 0:W], 1, 0))
        pi = jnp.where(rowid == 0, 0.0, pltpu.roll(sa[last, W:2 * W], 1, 0))
        accr = accr + gr * pr + gi * pi
        acci = acci + gi * pr - gr * pi
        dar_ref[...] = jnp.sum(accr, axis=0, keepdims=True)
        dai_ref[...] = jnp.sum(acci, axis=0, keepdims=True)
        dbf = jnp.zeros((GC, 2 * W), F32)
        dcf = jnp.zeros((GC, 2 * W), F32)
        dd = jnp.zeros((1, GC), F32)
        for r in range(8):
            rows = slice(r * nk, (r + 1) * nk)
            ub = ui[rows, :]
            dyb = dyi[rows, :]
            gb = sb[rows, :].astype(BF16)
            dui[rows, :] = _dot_nt(gb, b_ref[...]) + d_ref[...] * dyb
            dbf = dbf + _dot_tn(ub.astype(BF16), gb)
            dcf = dcf + _dot_tn(dyb.astype(BF16), sa[rows, :].astype(BF16))
            dd = dd + jnp.sum(dyb * ub, axis=0, keepdims=True)
        db_ref[...] = jnp.dot(dbf * mask_ref[...], r_ref[...], precision=hi, preferred_element_type=F32)
        dc_ref[...] = jnp.dot(dcf * mask_ref[...], r_ref[...], precision=hi, preferred_element_type=F32)
        dd_ref[...] = dd
        _deinterleave(du_ref, dui, nk)

    cmp_spec = pl.BlockSpec((GC, 2 * S5_P), lambda g: (g, 0))
    whole = lambda shape: pl.BlockSpec(shape, lambda g: (0, 0))
    sd = jax.ShapeDtypeStruct
    return _hosting_call(
        body, "s5_bwd", S5_G // S5_GB, host,
        [u, dy, carry, bm, cm, ar, ai, dvec, mask, rmat],
        [col, col, pl.BlockSpec((8, 2 * W), lambda g: (g, 0)), bmat, bmat, avec, avec, vec, whole(mask.shape),
         whole(rmat.shape)],
        [sd(u.shape, BF16), sd((S5_G * S5_C, 2 * S5_P), F32), sd((S5_G * S5_C, 2 * S5_P), F32),
         sd((1, PRIM), F32), sd((1, S5_G * S5_P), F32), sd((1, S5_G * S5_P), F32)],
        [col, cmp_spec, cmp_spec, vec, avec, avec],
        [pltpu.VMEM((L, 2 * W), F32), pltpu.VMEM((L, 2 * W), F32), pltpu.VMEM((8, 2 * W), F32),
         pltpu.VMEM((L, GC), F32), pltpu.VMEM((L, GC), F32), pltpu.VMEM((L, GC), F32)])


def _s5_compact_consts():
    g_row = np.arange(S5_GB * S5_C) // S5_C
    col = np.arange(2 * S5_W)
    g_col = (col % S5_W) // S5_P
    mask = (g_row[:, None] == g_col[None, :]).astype(np.float32)
    tgt = (col // S5_W) * S5_P + col % S5_P
    rmat = (tgt[:, None] == np.arange(2 * S5_P)[None, :]).astype(np.float32)
    return jnp.asarray(mask), jnp.asarray(rmat)


def _attn_scores(q_ref, k_ref, qb, bq, scale):
    ext = (qb + 1) * bq
    s = _dot_nt(q_ref[qb * bq:ext, :], k_ref[0:ext, :]) * scale
    qpos = lax.broadcasted_iota(jnp.int32, (bq, bq), 0)
    kpos = lax.broadcasted_iota(jnp.int32, (bq, bq), 1)
    diag = jnp.where(kpos <= qpos, s[:, ext - bq:], NEG)
    return diag if qb == 0 else jnp.concatenate([s[:, :ext - bq], diag], axis=-1)


def _attn_fwd(qp, kp, v, scale):
    L = qp.shape[0]
    bq = min(256, L)

    def body(q_ref, k_ref, v_ref, o_ref, lse_ref):
        for qb in range(L // bq):
            rows = slice(qb * bq, (qb + 1) * bq)
            s = _attn_scores(q_ref, k_ref, qb, bq, scale)
            m = jnp.max(s, axis=-1, keepdims=True)
            e = jnp.exp(s - m)
            l = jnp.sum(e, axis=-1, keepdims=True)
            o_ref[rows, :] = _dot(e.astype(BF16), v_ref[0:(qb + 1) * bq, :]) / l
            lse_ref[rows, :] = jnp.broadcast_to(m + jnp.log(l), (bq, HD))

    blk = pl.BlockSpec((L, HD), lambda h: (0, h))
    wide = pl.BlockSpec((L, 2 * HD), lambda h: (0, h))
    return pl.pallas_call(
        body, name="mla_attn_fwd", grid=(MLA_H,),
        in_specs=[wide, wide, blk], out_specs=[blk, blk],
        out_shape=[jax.ShapeDtypeStruct((L, MLA_H * HD), F32)] * 2,
        compiler_params=pltpu.CompilerParams(dimension_semantics=("arbitrary",), vmem_limit_bytes=VMEM_LIMIT),
    )(qp, kp, v)


def _attn_bwd(qp, kp, v, o, lse, do, scale):
    L = qp.shape[0]
    bq = min(256, L)
    nq = L // bq

    def body(q_ref, k_ref, v_ref, o_ref, lse_ref, do_ref, dq_ref, dk_ref, dv_ref, dk_acc, dv_acc):
        dk_acc[...] = jnp.zeros_like(dk_acc)
        dv_acc[...] = jnp.zeros_like(dv_acc)
        for qb in range(nq):
            rows = slice(qb * bq, (qb + 1) * bq)
            ext = (qb + 1) * bq
            do = do_ref[rows, :]
            dob = do.astype(BF16)
            p = jnp.exp(_attn_scores(q_ref, k_ref, qb, bq, scale) - lse_ref[rows, 0:1])
            dp = _dot_nt(dob, v_ref[0:ext, :])
            dsum = jnp.sum(do * o_ref[rows, :], axis=-1, keepdims=True)
            ds = (p * (dp - dsum) * scale).astype(BF16)
            dq_ref[rows, :] = _dot(ds, k_ref[0:ext, :]).astype(dq_ref.dtype)
            dk_acc[0:ext, :] += _dot_tn(ds, q_ref[rows, :])
            dv_acc[0:ext, :] += _dot_tn(p.astype(BF16), dob)
        dk_ref[...] = dk_acc[...].astype(dk_ref.dtype)
        dv_ref[...] = dv_acc[...].astype(dv_ref.dtype)

    sd = jax.ShapeDtypeStruct
    blk = pl.BlockSpec((L, HD), lambda h: (0, h))
    wide = pl.BlockSpec((L, 2 * HD), lambda h: (0, h))
    return pl.pallas_call(
        body, name="mla_attn_bwd", grid=(MLA_H,),
        in_specs=[wide, wide, blk, blk, blk, blk], out_specs=[wide, wide, blk],
        out_shape=[sd((L, MLA_H * 2 * HD), BF16), sd((L, MLA_H * 2 * HD), BF16), sd((L, MLA_H * HD), BF16)],
        scratch_shapes=[pltpu.VMEM((L, 2 * HD), F32), pltpu.VMEM((L, HD), F32)],
        compiler_params=pltpu.CompilerParams(dimension_semantics=("arbitrary",), vmem_limit_bytes=VMEM_LIMIT),
    )(qp, kp, v, o, lse, do)


def _kv_fn(mem, gm, w, gk):
    kv = _mm(_rms(mem, gm, D_MODEL), w)
    k = jnp.concatenate([_rms(kv[:, HD * h:HD * (h + 1)], gk, HD) for h in range(X_HEADS)], axis=-1)
    return k, kv[:, XQ:]


def _kv_prep(mem, gm, w, gk, name):
    def fn(mem, gm, w, gk):
        return _kv_fn(mem, gm, w, gk)
    M = mem.shape[0]
    return _rowwise(name, fn, [('c', mem), ('c', gm), ('c', w), ('c', gk)],
                    [('c', (M, XQ), F32), ('c', (M, XQ), F32)], 1)


def _kv_prep_bwd(mem, gm, w, gk, dk, dv, name):
    def fn(mem, gm, w, gk, dk, dv):
        _, vjp = jax.vjp(lambda a, b, c: _kv_fn(mem, a, b, c), gm, w, gk)
        return vjp((dk, dv))
    return _rowwise(name, fn, [('c', mem), ('c', gm), ('c', w), ('c', gk), ('c', dk), ('c', dv)],
                    [('c', gm.shape, F32), ('c', w.shape, BF16), ('c', gk.shape, F32)], 1)


def _forward_merge(x, mix, mix_kind, xq, gate, k, v, gq, wout, name, nblk, host=None):
    def fn(x, mix, xq, gate, k, v, gq, wout):
        o = _merge(mix, xq, gate, k, v, gq)
        return (x + _dot(o.astype(BF16), wout),)
    L = x.shape[0]
    out = _rowwise(name, fn, [('r', x), (mix_kind, mix), ('r', xq), ('r', gate), ('c', k), ('c', v), ('c', gq),
                              ('c', wout)], [('r', (L, D_MODEL), F32)], nblk, host=host)
    return out[0] if host is None else (out[0][0], out[1])


def _backward_merge(dx, mix, mix_kind, xq, gate, k, v, gq, wout, name, nblk, host=None):
    def fn(dx, mix, xq, gate, k, v, gq, wout):
        g16 = dx.astype(BF16)
        do = _dot_nt(g16, wout)
        o, vjp = jax.vjp(_merge, mix, xq, gate, k, v, gq)
        dmix, dxq, dgate, dk, dv, dgq = vjp(do)
        return dmix, dxq, dgate, o, g16, dk, dv, dgq
    L = dx.shape[0]
    return _rowwise(
        name, fn,
        [('r', dx), (mix_kind, mix), ('r', xq), ('r', gate), ('c', k), ('c', v), ('c', gq), ('c', wout)],
        [('r', (L, PRIM), F32), ('r', (L, XQ), BF16), ('r', (L, BRANCH), BF16), ('t', (BRANCH, L), BF16),
         ('r', (L, D_MODEL), BF16), ('a', k.shape, F32), ('a', v.shape, F32), ('a', gq.shape, F32)], nblk,
        host=host)


_MLA_IN = 3392
_MLA_IN_PAD = 3456


def _uq_rows(wt):
    r = wt.reshape(MLA_H, HD + ROPE, wt.shape[1])
    return jnp.concatenate([r[:, :HD].reshape(PRIM, -1),
                            jnp.pad(r[:, HD:], ((0, 0), (0, HD - ROPE), (0, 0))).reshape(PRIM, -1)], axis=0)


def _uq_rows_back(wt):
    nope = wt[:PRIM].reshape(MLA_H, HD, -1)
    rope = wt[PRIM:].reshape(MLA_H, HD, -1)[:, :ROPE]
    return jnp.concatenate([nope, rope], axis=1).reshape(MLA_H * (HD + ROPE), -1)


def _mla_in_rows(wt):
    return jnp.concatenate([wt[:768], wt[832:], wt[768:832], jnp.zeros((64, wt.shape[1]), wt.dtype)], axis=0)


def _mla_in_rows_back(wt):
    return jnp.concatenate([wt[:768], wt[3328:3392], wt[768:3328]], axis=0)


_SMALL = (("ln_gain", 2048), ("mem_norm", 2048), ("xq_norm", 256), ("xk_norm", 256), ("s5_lambda_re", 6144),
          ("s5_lambda_im", 6144), ("s5_log_step", 96), ("s5_b_re", 98304), ("s5_b_im", 98304), ("s5_c_re", 98304),
          ("s5_c_im", 98304), ("s5_d", 1536), ("mla_q_lora_norm", 512), ("mla_kv_lora_norm", 256),
          ("mla_q_nope_norm", 128), ("mla_k_nope_norm", 128), ("mla_q_rope_norm", 64), ("mla_k_rope_norm", 64))
_SMALL_ROWS = 432
_SMALL_OFF = {name: sum(n for _, n in _SMALL[:i]) for i, (name, _) in enumerate(_SMALL)}


def _pack_small(d):
    flat = jnp.concatenate([d[n].reshape(-1).astype(F32) for n, _ in _SMALL])
    return jnp.pad(flat, (0, _SMALL_ROWS * 1024 - flat.shape[0])).reshape(_SMALL_ROWS, 1024)


def _unpack_small(p, name, shape):
    off = _SMALL_OFF[name]
    return p.reshape(-1)[off:off + int(np.prod(shape))].reshape(shape)


_WEIGHTS = ('ln_gain', 'w_out', 'mem_norm', 'w_mem_kv', 'xq_norm', 'xk_norm', 's5_w_in', 's5_lambda_re',
            's5_lambda_im', 's5_log_step', 's5_b_re', 's5_b_im', 's5_c_re', 's5_c_im', 's5_d', 's5_w_glu', 'mla_w_in',
            'mla_q_lora_norm', 'mla_kv_lora_norm', 'mla_w_uq', 'mla_w_ukv', 'mla_q_nope_norm', 'mla_k_nope_norm',
            'mla_q_rope_norm', 'mla_k_rope_norm')
_BIG = ('w_out', 'w_mem_kv', 's5_w_in', 's5_w_glu', 'mla_w_in', 'mla_w_uq', 'mla_w_ukv')


def _pad128(g):
    return jnp.pad(g.reshape(1, -1), ((0, 0), (0, HD - g.shape[-1])))


def kernel(x, mem, positions, ln_gain, w_out, mem_norm, w_mem_kv, xq_norm, xk_norm, s5_w_in, s5_lambda_re, s5_lambda_im, s5_log_step, s5_b_re, s5_b_im, s5_c_re, s5_c_im, s5_d, s5_w_glu, mla_w_in, mla_q_lora_norm, mla_kv_lora_norm, mla_w_uq, mla_w_ukv, mla_q_nope_norm, mla_k_nope_norm, mla_q_rope_norm, mla_k_rope_norm, loss_target, m_ln_gain, m_w_out, m_mem_norm, m_w_mem_kv, m_xq_norm, m_xk_norm, m_s5_w_in, m_s5_lambda_re, m_s5_lambda_im, m_s5_log_step, m_s5_b_re, m_s5_b_im, m_s5_c_re, m_s5_c_im, m_s5_d, m_s5_w_glu, m_mla_w_in, m_mla_q_lora_norm, m_mla_kv_lora_norm, m_mla_w_uq, m_mla_w_ukv, m_mla_q_nope_norm, m_mla_k_nope_norm, m_mla_q_rope_norm, m_mla_k_rope_norm, v_ln_gain, v_w_out, v_mem_norm, v_w_mem_kv, v_xq_norm, v_xk_norm, v_s5_w_in, v_s5_lambda_re, v_s5_lambda_im, v_s5_log_step, v_s5_b_re, v_s5_b_im, v_s5_c_re, v_s5_c_im, v_s5_d, v_s5_w_glu, v_mla_w_in, v_mla_q_lora_norm, v_mla_kv_lora_norm, v_mla_w_uq, v_mla_w_ukv, v_mla_q_nope_norm, v_mla_k_nope_norm, v_mla_q_rope_norm, v_mla_k_rope_norm):
    weights = dict(ln_gain=ln_gain, w_out=w_out, mem_norm=mem_norm, w_mem_kv=w_mem_kv, xq_norm=xq_norm,
                   xk_norm=xk_norm, s5_w_in=s5_w_in, s5_lambda_re=s5_lambda_re, s5_lambda_im=s5_lambda_im,
                   s5_log_step=s5_log_step, s5_b_re=s5_b_re, s5_b_im=s5_b_im, s5_c_re=s5_c_re, s5_c_im=s5_c_im,
                   s5_d=s5_d, s5_w_glu=s5_w_glu, mla_w_in=mla_w_in, mla_q_lora_norm=mla_q_lora_norm,
                   mla_kv_lora_norm=mla_kv_lora_norm, mla_w_uq=mla_w_uq, mla_w_ukv=mla_w_ukv,
                   mla_q_nope_norm=mla_q_nope_norm, mla_k_nope_norm=mla_k_nope_norm,
                   mla_q_rope_norm=mla_q_rope_norm, mla_k_rope_norm=mla_k_rope_norm)
    m_in = dict(zip(_WEIGHTS, (m_ln_gain, m_w_out, m_mem_norm, m_w_mem_kv, m_xq_norm, m_xk_norm, m_s5_w_in,
                               m_s5_lambda_re, m_s5_lambda_im, m_s5_log_step, m_s5_b_re, m_s5_b_im, m_s5_c_re,
                               m_s5_c_im, m_s5_d, m_s5_w_glu, m_mla_w_in, m_mla_q_lora_norm, m_mla_kv_lora_norm,
                               m_mla_w_uq, m_mla_w_ukv, m_mla_q_nope_norm, m_mla_k_nope_norm, m_mla_q_rope_norm,
                               m_mla_k_rope_norm)))
    v_in = dict(zip(_WEIGHTS, (v_ln_gain, v_w_out, v_mem_norm, v_w_mem_kv, v_xq_norm, v_xk_norm, v_s5_w_in,
                               v_s5_lambda_re, v_s5_lambda_im, v_s5_log_step, v_s5_b_re, v_s5_b_im, v_s5_c_re,
                               v_s5_c_im, v_s5_d, v_s5_w_glu, v_mla_w_in, v_mla_q_lora_norm, v_mla_kv_lora_norm,
                               v_mla_w_uq, v_mla_w_ukv, v_mla_q_nope_norm, v_mla_k_nope_norm, v_mla_q_rope_norm,
                               v_mla_k_rope_norm)))

    x0 = x[0]
    mem0 = mem[0]
    target = loss_target[0]
    L = x0.shape[0]
    nblk = 4
    nb_big = 8
    me = 4 * lax.axis_index("x") + 2 * lax.axis_index("y") + lax.axis_index("c")

    lora = jnp.pad(jnp.concatenate([mla_q_lora_norm, mla_kv_lora_norm], axis=1), ((0, 7), (0, HD - 96)))
    def gather(*shards):
        return _plan_all_gather(list(shards))

    kh = D_MODEL // 2
    (b_mkv0, b_glu, b_in_mla, b_out0, b_uq, b_ukv, b_mkv1, b_out1), (W_in_s5,) = _cast_call(
        [w_mem_kv[0], s5_w_glu[0], jnp.transpose(mla_w_in[0]), w_out[0], jnp.transpose(mla_w_uq[0]), mla_w_ukv[0],
         w_mem_kv[1], w_out[1]], "cast_shards", host=gather(s5_w_in[0].astype(BF16)))

    ln0, ln1 = ln_gain[0:1], ln_gain[1:2]
    gq0, gq1 = xq_norm[0:1], xq_norm[1:2]
    gk0, gk1 = xk_norm[0:1], xk_norm[1:2]
    gm0, gm1 = mem_norm[0:1], mem_norm[1:2]
    gqn, gkn = mla_q_nope_norm, mla_k_nope_norm
    gqr, gkr = _pad128(mla_q_rope_norm), _pad128(mla_k_rope_norm)

    lr3 = s5_lambda_re.reshape(S5_G, 1, S5_P)
    li3 = s5_lambda_im.reshape(S5_G, 1, S5_P)
    ls3 = s5_log_step.reshape(S5_G, 1, 1)
    btr = jnp.swapaxes(s5_b_re[0], 1, 2)
    bti = jnp.swapaxes(s5_b_im[0], 1, 2)
    a_r, a_i, bm, cm = _s5_params(lr3, li3, ls3, btr, bti, s5_c_re[0], s5_c_im[0])
    a_r2 = a_r.reshape(1, S5_G * S5_P)
    a_i2 = a_i.reshape(1, S5_G * S5_P)
    cmask, rmat = _s5_compact_consts()

    half = ROPE // 2
    inv_freq = ROPE_THETA ** (-jnp.arange(half, dtype=F32) / half)
    invf = jnp.concatenate([inv_freq, inv_freq, jnp.zeros((HD - ROPE,), F32)]).reshape(1, HD)

    def rot_tables(pos, invf):
        ang = pos.astype(F32) * invf
        lane = lax.broadcasted_iota(jnp.int32, ang.shape, 1)
        c = jnp.where(lane < ROPE, jnp.cos(ang), 0.0)
        s = jnp.sin(ang)
        return c, jnp.where(lane < half, -s, 0.0), jnp.where((lane >= half) & (lane < ROPE), s, 0.0)

    tc, ts1, ts2 = _rowwise("rot_tables", rot_tables, [('r', positions.reshape(L, 1)), ('c', invf)],
                            [('r', (L, HD), F32)] * 3, nblk)

    def in_s5(x, g, w):
        proj = _mm_slots(_rms(x, g, D_MODEL).astype(BF16), w)
        return proj[:, :PRIM], proj[:, PRIM:PRIM + XQ], proj[:, PRIM + XQ:]

    u_s5, xq_a, gate_a = _rowwise(
        "s5_in", in_s5, [('r', x0), ('c', ln0), ('c', W_in_s5)],
        [('r', (L, PRIM), F32), ('r', (L, XQ), F32), ('r', (L, BRANCH), F32)], nblk)
    (y_s5, s5_carry), (W_glu, G_mkv0, G_in_mla_a) = _s5_fwd(u_s5, bm, cm, a_r2, a_i2, s5_d,
                                                            host=gather(b_glu, b_mkv0, b_in_mla[:, :kh]))

    def glu(y, w):
        z = _mm_slots(_gelu(y).astype(BF16), w)
        return (z[:, :PRIM] * _sigmoid(z[:, PRIM:]),)

    (y2,), (G_out0,) = _rowwise("s5_glu", glu, [('r', y_s5), ('c', W_glu)], [('r', (L, PRIM), F32)], nblk,
                                host=gather(b_out0))
    W_mkv0 = G_mkv0.reshape(D_MODEL, 2 * XQ)
    k_a, v_a = _kv_prep(mem0, gm0, W_mkv0, gk0, "kv_prep0")
    x1, (G_in_mla_b,) = _forward_merge(
        x0, y2, 'r', xq_a, gate_a, k_a, v_a, gq0, G_out0.reshape(BRANCH, D_MODEL), "merge0", nblk,
        host=gather(b_in_mla[:, kh:]))
    W_in_mla = _mla_in_rows(jnp.concatenate([G_in_mla_a, G_in_mla_b], axis=2).reshape(_MLA_IN, D_MODEL))

    def in_mla(x, g, w):
        proj = _dot_nt(_rms(x, g, D_MODEL).astype(BF16), w)
        return proj[:, :512], proj[:, 512:768], proj[:, 768:1280], proj[:, 1280:3328], proj[:, 3328:]

    (c_q, c_kv, xq_b, gate_b, krp), (G_uq, W_kv, G_lora) = _rowwise(
        "mla_in", in_mla, [('r', x1), ('c', ln1), ('c', W_in_mla)],
        [('r', (L, Q_LORA), F32), ('r', (L, KV_LORA), F32), ('r', (L, XQ), F32), ('r', (L, BRANCH), F32),
         ('r', (L, HD), F32)], nblk,
        host=gather(b_uq, b_ukv, lora))
    W_q = _uq_rows(G_uq.reshape(MLA_H * (HD + ROPE), Q_LORA))
    g_qlora = G_lora[:, 0, :64].reshape(1, Q_LORA)
    g_kvlora = G_lora[:, 0, 64:96].reshape(1, KV_LORA)

    def qkv(c_q, c_kv, krp, tc, ts1, ts2, gql, gkvl, wq, wkv, gqn, gkn, gqr, gkr):
        q = _dot_nt(_rms(c_q, gql, Q_LORA).astype(BF16), wq)
        kv = _mm_slots(_rms(c_kv, gkvl, KV_LORA).astype(BF16), wkv)
        kp, v = _kv_post(kv, krp, gkn, gkr, tc, ts1, ts2)
        return _q_post(q, gqn, gqr, tc, ts1, ts2), kp, v

    qkv_consts = [('c', g_qlora), ('c', g_kvlora), ('c', W_q), ('c', W_kv), ('c', gqn), ('c', gkn), ('c', gqr),
                  ('c', gkr)]
    (q_pad, k_pad, v_h), (G_mkv1, G_out1) = _rowwise(
        "mla_qkv", qkv, [('r', c_q), ('r', c_kv), ('r', krp), ('r', tc), ('r', ts1), ('r', ts2)] + qkv_consts,
        [('r', (L, 2 * PRIM), BF16), ('r', (L, 2 * PRIM), BF16), ('r', (L, PRIM), BF16)], nblk,
        host=gather(b_mkv1, b_out1))
    W_out = (G_out0.reshape(BRANCH, D_MODEL), G_out1.reshape(BRANCH, D_MODEL))
    W_mkv = (W_mkv0, G_mkv1.reshape(D_MODEL, 2 * XQ))
    scale = (HD + ROPE) ** -0.5
    attn, lse = _attn_fwd(q_pad, k_pad, v_h, scale)
    k_b, v_b = _kv_prep(mem0, gm1, W_mkv[1], gk1, "kv_prep1")

    def merge_loss(x, mix, xq, gate, k, v, gq, wout, t):
        err = x + _dot(_merge(mix, xq, gate, k, v, gq).astype(BF16), wout) - t
        part = 0.5 * jnp.sum(jnp.sum(err * err, axis=-1, keepdims=True) * (1.0 / D_MODEL), axis=0, keepdims=True)
        return err * (1.0 / D_MODEL), jnp.broadcast_to(part, (1, HD))

    dx2, loss_part = _rowwise(
        "merge1_loss", merge_loss,
        [('r', x1), ('r', attn), ('r', xq_b), ('r', gate_b), ('c', k_b), ('c', v_b), ('c', gq1), ('c', W_out[1]),
         ('r', target)], [('r', (L, D_MODEL), F32), ('a', (1, HD), F32)], nblk)

    dattn, dxq_b, dgate_b, o_b, g_b, dk_b, dv_b, dgq1 = _backward_merge(
        dx2, attn, 'r', xq_b, gate_b, k_b, v_b, gq1, W_out[1], "merge1_bwd", nb_big)
    dgm1, dW_mkv1, dgk1 = _kv_prep_bwd(mem0, gm1, W_mkv[1], gk1, dk_b, dv_b, "kv_prep1_bwd")
    dW_out1 = _matmul_tn(o_b, g_b, "dw_out1")
    dq_pad, dk_pad, dv_h = _attn_bwd(q_pad, k_pad, v_h, attn, lse, dattn, scale)

    def qkv_bwd(c_q, c_kv, krp, tc, ts1, ts2, dqp, dkp, dv, gql, gkvl, wq, wkv, gqn, gkn, gqr, gkr):
        cqn, vjp_qn = jax.vjp(lambda a, b: _rms(a, b, Q_LORA), c_q, gql)
        ckvn, vjp_kvn = jax.vjp(lambda a, b: _rms(a, b, KV_LORA), c_kv, gkvl)
        cqn16 = cqn.astype(BF16)
        ckvn16 = ckvn.astype(BF16)
        q = _dot_nt(cqn16, wq)
        kv = _mm_slots(ckvn16, wkv)
        _, vjp_q = jax.vjp(lambda a, b, c: _q_post(a, b, c, tc, ts1, ts2), q, gqn, gqr)
        dq, dgqn, dgqr = vjp_q(dqp.astype(F32))
        _, vjp_kv = jax.vjp(lambda a, b, c, d: _kv_post(a, b, c, d, tc, ts1, ts2), kv, krp, gkn, gkr)
        dkv, dkrp, dgkn, dgkr = vjp_kv((dkp.astype(F32), dv.astype(F32)))
        dq16 = dq.astype(BF16)
        dkv16 = dkv.astype(BF16)
        dc_q, dgql = vjp_qn(_dot(dq16, wq))
        dc_kv, dgkvl = vjp_kvn(_mm_slots_nt(dkv16, wkv))
        return dc_q, dc_kv, dkrp, cqn16, dq16, ckvn16, dkv16, dgql, dgkvl, dgqn, dgkn, dgqr, dgkr

    (dc_q, dc_kv, dkrp, cqn16, dq16, ckvn16, dkv16, dgql, dgkvl, dgqn, dgkn, dgqr, dgkr) = _rowwise(
        "mla_qkv_bwd", qkv_bwd,
        [('r', c_q), ('r', c_kv), ('r', krp), ('r', tc), ('r', ts1), ('r', ts2), ('r', dq_pad), ('r', dk_pad),
         ('r', dv_h)] + qkv_consts,
        [('r', (L, Q_LORA), BF16), ('r', (L, KV_LORA), BF16), ('r', (L, HD), BF16), ('r', (L, Q_LORA), BF16),
         ('t', (2 * PRIM, L), BF16), ('t', (KV_LORA, L), BF16), ('r', (L, 2 * PRIM), BF16),
         ('a', (1, Q_LORA), F32), ('a', (1, KV_LORA), F32), ('a', (1, HD), F32), ('a', (1, HD), F32),
         ('a', (1, HD), F32), ('a', (1, HD), F32)], nb_big)
    dW_q = _matmul_tn(dq16, cqn16, "dw_uq")
    dW_kv = _matmul_tn_slots(ckvn16, dkv16, "dw_ukv")

    def in_bwd(x, dres, g, w, *dparts):
        dproj = jnp.concatenate(dparts, axis=-1).astype(BF16)
        xn, vjp = jax.vjp(lambda a, b: _rms(a, b, D_MODEL), x, g)
        dx, dg = vjp(_mm_slots_nt(dproj, w) if w.ndim == 3 else _dot(dproj, w))
        return dx + dres, xn, dproj, dg

    dx1, xn1, dproj1, dln1 = _rowwise(
        "mla_in_bwd", in_bwd,
        [('r', x1), ('r', dx2), ('c', ln1), ('c', W_in_mla), ('r', dc_q), ('r', dc_kv), ('r', dxq_b), ('r', dgate_b),
         ('r', dkrp)],
        [('r', (L, D_MODEL), F32), ('r', (L, D_MODEL), BF16), ('t', (_MLA_IN_PAD, L), BF16), ('a', (1, D_MODEL), F32)],
        nblk)
    dW_in_mla = _matmul_tn(dproj1, xn1, "dw_mla_in")

    grads1 = [dW_out1.reshape(N_DEV, 256, D_MODEL), dW_mkv1.reshape(N_DEV, 128, 2 * XQ),
              _mla_in_rows_back(dW_in_mla).reshape(N_DEV, 424, D_MODEL),
              _uq_rows_back(dW_q).reshape(N_DEV, 288, Q_LORA), dW_kv]
    (dy2, dxq_a, dgate_a, o_a, g_a, dk_a, dv_a, dgq0), pair1 = _backward_merge(
        dx1, y2, 'r', xq_a, gate_a, k_a, v_a, gq0, W_out[0], "merge0_bwd", nb_big, host=_plan_pair(grads1))
    dgm0, dW_mkv0, dgk0 = _kv_prep_bwd(mem0, gm0, W_mkv[0], gk0, dk_a, dv_a, "kv_prep0_bwd")
    dW_out0 = _matmul_tn(o_a, g_a, "dw_out0")
    t1 = list(_pair_add(grads1, pair1, "rs_add_layer1"))

    def glu_bwd(y, dy2, w):
        h, vjp_h = jax.vjp(_gelu, y)
        h16 = h.astype(BF16)
        z = _mm_slots(h16, w)
        _, vjp_z = jax.vjp(lambda z: z[:, :PRIM] * _sigmoid(z[:, PRIM:]), z)
        dz16 = vjp_z(dy2)[0].astype(BF16)
        return vjp_h(_mm_slots_nt(dz16, w))[0], h16, dz16

    grads0 = [dW_out0.reshape(N_DEV, 256, D_MODEL), dW_mkv0.reshape(N_DEV, 128, 2 * XQ)]
    (dy_s5, h16, dz16), glu_hosted = _rowwise(
        "s5_glu_bwd", glu_bwd, [('r', y_s5), ('r', dy2), ('c', W_glu)],
        [('r', (L, PRIM), F32), ('t', (PRIM, L), BF16), ('r', (L, 2 * PRIM), BF16)], nb_big,
        host=_combine(_plan_chips(t1[2:]), _plan_pair(grads0)))
    recv_proj1, pair0 = glu_hosted[:3], glu_hosted[3:]
    dW_glu = _matmul_tn_slots(h16, dz16, "dw_glu")
    t0 = list(_pair_add(grads0 + [dW_glu], pair0 + list(_exchange_call(_plan_pair([dW_glu]), "rs_pair_glu")),
                        "rs_add_layer0"))
    (du_s5, dbc, dcc, dd, dar, dai), recv_rest = _s5_bwd(u_s5, dy_s5, s5_carry, bm, cm, a_r2, a_i2, s5_d,
                                                        cmask, rmat, host=_plan_chips(t1[:2] + t0))
    early_recv = recv_rest[:2] + recv_proj1 + recv_rest[2:]
    dbc4 = dbc.reshape(S5_G, S5_C, 2, S5_P)
    dcc4 = dcc.reshape(S5_G, S5_C, 2, S5_P)
    dlr, dli, dls, dbtr, dbti = _s5_params_bwd(
        lr3, li3, ls3, btr, bti, dar.reshape(S5_G, 1, S5_P), dai.reshape(S5_G, 1, S5_P), dbc4[:, :, 0], dbc4[:, :, 1])

    small_part = {
        "ln_gain": jnp.concatenate([jnp.zeros_like(dln1), dln1]), "mem_norm": jnp.concatenate([dgm0, dgm1]),
        "xq_norm": jnp.concatenate([dgq0, dgq1]), "xk_norm": jnp.concatenate([dgk0, dgk1]),
        "s5_lambda_re": dlr, "s5_lambda_im": dli, "s5_log_step": dls,
        "s5_b_re": jnp.swapaxes(dbtr, 1, 2), "s5_b_im": jnp.swapaxes(dbti, 1, 2),
        "s5_c_re": dcc4[:, :, 0], "s5_c_im": -dcc4[:, :, 1], "s5_d": dd,
        "mla_q_lora_norm": dgql, "mla_kv_lora_norm": dgkvl, "mla_q_nope_norm": dgqn, "mla_k_nope_norm": dgkn,
        "mla_q_rope_norm": dgqr[:, :ROPE], "mla_k_rope_norm": dgkr[:, :ROPE],
    }
    loss8 = jnp.pad(loss_part, ((0, 7), (0, 0)))
    (dx0, xn0, dproj0, dln0), (small_gath, loss_g) = _rowwise(
        "s5_in_bwd", in_bwd,
        [('r', x0), ('r', dx1), ('c', ln0), ('c', W_in_s5), ('r', du_s5), ('r', dxq_a),
         ('r', dgate_a)],
        [('r', (L, D_MODEL), F32), ('t', (D_MODEL, L), BF16), ('r', (L, 2 * BRANCH), BF16), ('a', (1, D_MODEL), F32)],
        nblk, host=_plan_all_gather([_pack_small(small_part).astype(BF16), loss8]))
    dW_in_s5 = _matmul_tn_slots(xn0, dproj0, "dw_s5_in")

    late = [dW_in_s5]
    late_t = _pair_add(late, list(_exchange_call(_plan_pair(late), "rs_pair_late")), "rs_add_late")
    owners = [("w_out", 1), ("w_mem_kv", 1), ("mla_w_in", 0), ("mla_w_uq", 0), ("mla_w_ukv", 0), ("w_out", 0),
              ("w_mem_kv", 0), ("s5_w_glu", 0)]
    flipped = ("mla_w_in", "mla_w_uq")

    def shard(d, n, i):
        return jnp.transpose(d[n][i]) if n in flipped else d[n][i]

    upd, (late_recv, ln0_gath) = _updates_call(
        early_recv, [shard(weights, n, i) for n, i in owners], [shard(m_in, n, i) for n, i in owners],
        [shard(v_in, n, i) for n, i in owners], "update_early",
        host=_combine(_plan_chips(late_t), _plan_all_gather([jnp.pad(dln0, ((0, 7), (0, 0)))])))
    owners.append(("s5_w_in", 0))
    upd.append(_sum_adamw(late_recv, s5_w_in[0], m_s5_w_in[0], v_s5_w_in[0], "update_s5_w_in"))
    grads, delta, new_m, new_v = {}, {}, {}, {}
    for n in _BIG:
        parts = [u for u, (o, _) in sorted(zip(upd, owners), key=lambda t: t[1][1]) if o == n]
        if n in flipped:
            grads[n], delta[n], new_m[n], new_v[n] = (jnp.transpose(parts[0][j])[None] for j in range(4))
        else:
            grads[n], delta[n], new_m[n], new_v[n] = (jnp.stack([p[j] for p in parts]) for j in range(4))

    gs, loss_sum = _small_sum(small_gath, loss_g, ln0_gath, "small_sum")
    loss = loss_sum[0, 0]
    for n, _ in _SMALL:
        shape = weights[n].shape
        if n == "mla_q_lora_norm":
            grads[n] = lax.dynamic_slice(_unpack_small(gs, n, (Q_LORA,)), (me * 64,), (64,)).reshape(shape)
        elif n == "mla_kv_lora_norm":
            grads[n] = lax.dynamic_slice(_unpack_small(gs, n, (KV_LORA,)), (me * 32,), (32,)).reshape(shape)
        else:
            grads[n] = _unpack_small(gs, n, shape)

    def own(n, a):
        if a.ndim == 4:
            a = jnp.transpose(a, (0, 2, 3, 1))
        elif a.ndim == 3:
            a = jnp.transpose(a, (0, 2, 1))
        return a.reshape(a.shape[1:]) if a.ndim >= 3 else a

    def back(n, a):
        shape = weights[n].shape
        if len(shape) == 4:
            return jnp.transpose(a.reshape((1,) + a.shape), (0, 3, 1, 2))
        if len(shape) == 3:
            return jnp.transpose(a.reshape((1,) + a.shape), (0, 2, 1))
        return a.reshape(shape)

    wide = ("s5_b_re", "s5_b_im", "s5_c_re", "s5_c_im")
    for names, nb, call in (([n for n, _ in _SMALL if n not in wide], 1, "update_small"), (wide, 4, "update_s5_bc")):
        res = _adamw_multi([own(n, weights[n]) for n in names], [own(n, grads[n]) for n in names],
                           [own(n, m_in[n]) for n in names], [own(n, v_in[n]) for n in names], call, nb)
        for n, (dl, m2, v2) in zip(names, res):
            delta[n], new_m[n], new_v[n] = back(n, dl), back(n, m2), back(n, v2)
    return (loss, dx0[None], *[grads[n] for n in _WEIGHTS], *[delta[n] for n in _WEIGHTS],
            *[new_m[n] for n in _WEIGHTS], *[new_v[n] for n in _WEIGHTS])
```

```python
import functools
import math

import numpy as np
import jax
import jax.numpy as jnp
from jax import lax
from jax.experimental import pallas as pl
from jax.experimental.pallas import tpu as pltpu

F32 = jnp.float32
BF16 = jnp.bfloat16
EPS = 1e-6
NEG = float(np.finfo(np.float32).min)
MESH = pl.DeviceIdType.MESH

N_DEV = 8
D_MODEL = 1024
MEM_LEN = 256
XQ = 512
PRIM = 1536
BRANCH = 2048
X_HEADS = 4
HD = 128
S5_G = 96
S5_P = 64
S5_C = 16
S5_GB = 8
S5_W = S5_GB * S5_P
MLA_H = 12
ROPE = 64
Q_LORA = 512
KV_LORA = 256
ROPE_THETA = 10000.0

ADAM_LR = 0.001
ADAM_B1 = 0.9
ADAM_B2 = 0.999
ADAM_EPS = 1e-08
ADAM_WD = 0.01
ADAM_STEP = 10

VMEM_LIMIT = 56 * 1024 * 1024


def _dot(a, b):
    return jnp.dot(a, b, preferred_element_type=F32)


def _dot_nt(a, b):
    return lax.dot_general(a, b, (((1,), (1,)), ((), ())), preferred_element_type=F32)


def _dot_tn(a, b):
    return lax.dot_general(a, b, (((0,), (0,)), ((), ())), preferred_element_type=F32)


@jax.custom_vjp
def _mm(a, b):
    return _dot(a.astype(BF16), b.astype(BF16))


def _mm_fwd(a, b):
    return _mm(a, b), (a, b)


def _mm_bwd(res, g):
    a, b = res
    gb = g.astype(BF16)
    return _dot_nt(gb, b.astype(BF16)).astype(a.dtype), _dot_tn(a.astype(BF16), gb).astype(b.dtype)


_mm.defvjp(_mm_fwd, _mm_bwd)


@jax.custom_vjp
def _mm_nt(a, b):
    return _dot_nt(a.astype(BF16), b.astype(BF16))


def _mm_nt_fwd(a, b):
    return _mm_nt(a, b), (a, b)


def _mm_nt_bwd(res, g):
    a, b = res
    gb = g.astype(BF16)
    return _dot(gb, b.astype(BF16)).astype(a.dtype), _dot_tn(gb, a.astype(BF16)).astype(b.dtype)


_mm_nt.defvjp(_mm_nt_fwd, _mm_nt_bwd)


@jax.custom_vjp
def _softmax(s):
    m = jnp.max(s, axis=-1, keepdims=True)
    e = jnp.exp(s - m)
    return e / jnp.sum(e, axis=-1, keepdims=True)


def _softmax_fwd(s):
    p = _softmax(s)
    return p, p


def _softmax_bwd(p, g):
    return (p * (g - jnp.sum(p * g, axis=-1, keepdims=True)),)


_softmax.defvjp(_softmax_fwd, _softmax_bwd)


def _rms(x, g, n):
    ms = jnp.sum(x * x, axis=-1, keepdims=True) * (1.0 / n)
    return x * lax.rsqrt(ms + EPS) * g


def _sigmoid(x):
    return 1.0 / (1.0 + jnp.exp(-x))


def _silu(x):
    return x * _sigmoid(x)


def _gelu(x):
    c = math.sqrt(2.0 / math.pi)
    return 0.5 * x * (1.0 + jnp.tanh(c * (x + 0.044715 * (x * x * x))))


@jax.custom_vjp
def _rot(x, c, s1, s2):
    return x * c + pltpu.roll(x, 96, 1) * s1 + pltpu.roll(x, 32, 1) * s2


def _rot_fwd(x, c, s1, s2):
    return _rot(x, c, s1, s2), (c, s1, s2)


def _rot_bwd(res, g):
    c, s1, s2 = res
    dx = g * c + pltpu.roll(g * s1, 32, 1) + pltpu.roll(g * s2, 96, 1)
    return dx, jnp.zeros_like(c), jnp.zeros_like(s1), jnp.zeros_like(s2)


_rot.defvjp(_rot_fwd, _rot_bwd)


def _mem_attn(xq, k, v, gq):
    outs = []
    for h in range(X_HEADS):
        sl = slice(HD * h, HD * (h + 1))
        q = _rms(xq[:, sl], gq, HD)
        p = _softmax(_mm_nt(q, k[:, sl]) * (HD ** -0.5))
        outs.append(_mm(p, v[:, sl]))
    return jnp.concatenate(outs, axis=-1)


def _merge(mix, xq, gate, k, v, gq):
    return jnp.concatenate([mix, _mem_attn(xq, k, v, gq)], axis=-1) * _silu(gate)


def _q_chunks(q):
    return ([q[:, HD * h:HD * (h + 1)] for h in range(MLA_H)],
            [q[:, PRIM + HD * h:PRIM + HD * (h + 1)] for h in range(MLA_H)])


def _q_post(nope, rope, gqn, gqr, c, s1, s2):
    pieces = []
    for qn, qr in zip(nope, rope):
        pieces.append(_rms(qn, gqn, HD))
        pieces.append(_rot(_rms(qr, gqr, ROPE), c, s1, s2))
    return jnp.concatenate(pieces, axis=-1)


def _kv_chunks(kv):
    return ([kv[:, 2 * HD * h:2 * HD * h + HD] for h in range(MLA_H)],
            [kv[:, 2 * HD * h + HD:2 * HD * (h + 1)] for h in range(MLA_H)])


def _kv_post(kn, vals, krp, gkn, gkr, c, s1, s2):
    kr = _rot(_rms(krp, gkr, ROPE), c, s1, s2)
    pieces = []
    for k in kn:
        pieces.append(_rms(k, gkn, HD))
        pieces.append(kr)
    return jnp.concatenate(pieces, axis=-1), jnp.concatenate(vals, axis=-1)


def _rowwise(name, fn, ins, outs, nblk, host=None):
    n_in = len(ins)

    def spec(kind, shape):
        if kind == 'r':
            return pl.BlockSpec((shape[0] // nblk, shape[1]), lambda i: (i, 0))
        if kind == 't':
            return pl.BlockSpec((shape[0], shape[1] // nblk), lambda i: (0, i))
        zeros = (0,) * len(shape)
        return pl.BlockSpec(tuple(shape), lambda i: zeros)

    def body(*refs):
        i = pl.program_id(0)
        res = fn(*[r[...] for r in refs[:n_in]])
        for (kind, _, _), ref, val in zip(outs, refs[n_in:], res):
            if kind == 'a':
                @pl.when(i == 0)
                def _():
                    ref[...] = jnp.zeros_like(ref)
                ref[...] += val.astype(ref.dtype)
            elif kind == 't':
                ref[...] = val.astype(F32).T.astype(ref.dtype)
            else:
                ref[...] = val.astype(ref.dtype)

    res, hosted = _hosting_call(
        body, name, nblk, host, [a for _, a in ins], [spec(k, a.shape) for k, a in ins],
        [jax.ShapeDtypeStruct(tuple(s), d) for _, s, d in outs], [spec(k, s) for k, s, _ in outs], [])
    return res if host is None else (res, hosted)


def _matmul_tn(at, g, name, out_dtype=BF16):
    K, L = at.shape
    N = g.shape[1]
    tn = next(t for t in (512, 384, 256, 128) if N % t == 0)

    def body(a_ref, g_ref, o_ref):
        o_ref[...] = _dot(a_ref[...], g_ref[...]).astype(o_ref.dtype)

    return pl.pallas_call(
        body, name=name, grid=(N // tn,),
        in_specs=[pl.BlockSpec((K, L), lambda n: (0, 0)), pl.BlockSpec((L, tn), lambda n: (0, n))],
        out_specs=pl.BlockSpec((K, tn), lambda n: (0, n)),
        out_shape=jax.ShapeDtypeStruct((K, N), out_dtype),
        compiler_params=pltpu.CompilerParams(dimension_semantics=("arbitrary",), vmem_limit_bytes=VMEM_LIMIT),
    )(at, g)


def _matmul_tn_slots(at, g, name, host=None):
    K, L = at.shape
    n = g.shape[1] // N_DEV

    def body(a_ref, g_ref, o_ref):
        o_ref[...] = _dot(a_ref[...], g_ref[...]).astype(o_ref.dtype)

    res, hosted = _hosting_call(
        body, name, N_DEV, host, [at, g],
        [pl.BlockSpec((K, L), lambda d: (0, 0)), pl.BlockSpec((L, n), lambda d: (0, d))],
        [jax.ShapeDtypeStruct((N_DEV, K, n), BF16)], [pl.BlockSpec((None, K, n), lambda d: (d, 0, 0))], [])
    return res[0] if host is None else (res[0], hosted)


def _mm_slots(a16, w):
    return jnp.concatenate([_dot(a16, w[d]) for d in range(N_DEV)], axis=-1)


def _mm_slots_nt(g16, w):
    n = w.shape[2]
    out = _dot_nt(g16[:, 0:n], w[0])
    for d in range(1, N_DEV):
        out = out + _dot_nt(g16[:, d * n:(d + 1) * n], w[d])
    return out


class _Exchange:
    def __init__(self, ins, outs, scratch, start, finish):
        self.ins, self.outs, self.scratch, self.start, self.finish = ins, outs, scratch, start, finish


def _xyc():
    return lax.axis_index("x"), lax.axis_index("y"), lax.axis_index("c")


def _plan_all_gather(xs):
    n = len(xs)

    def build(x_refs, out_refs, sems):
        send_sems, recv_sems, local_sems = sems
        x, y, c = _xyc()

        def copies(k, block, to, own=False):
            slot = 4 * block[0] + 2 * block[1] + block[2]
            return [pltpu.make_async_remote_copy(
                src_ref=x_refs[a] if own else out_refs[a].at[slot], dst_ref=out_refs[a].at[slot],
                send_sem=send_sems.at[k * n + a], recv_sem=recv_sems.at[k * n + a], device_id=to,
                device_id_type=MESH) for a in range(n)]

        mine = [pltpu.make_async_copy(x_refs[a], out_refs[a].at[4 * x + 2 * y + c], local_sems.at[a])
                for a in range(n)]
        return copies, mine, (x, y, c), [(1 - x, y), (x, 1 - y), (1 - x, 1 - y)]

    def first_copies(copies, me, chips):
        x, y, c = me
        first = copies(0, me, (x, y, 1 - c), own=True)
        for j, chip in enumerate(chips):
            first += copies(1 + j, me, (*chip, c), own=True)
        return first

    def start(x_refs, out_refs, sems):
        copies, mine, me, chips = build(x_refs, out_refs, sems)
        for cp in mine + first_copies(copies, me, chips):
            cp.start()

    def finish(x_refs, out_refs, sems):
        copies, mine, me, chips = build(x_refs, out_refs, sems)
        x, y, c = me
        passed = []
        for j, chip in enumerate(chips):
            for cp in copies(1 + j, (*chip, c), me):
                cp.wait_recv()
            fwd = copies(4 + j, (*chip, c), (x, y, 1 - c))
            for cp in fwd:
                cp.start()
            passed += fwd
        for cp in copies(0, (x, y, 1 - c), me):
            cp.wait_recv()
        for j, chip in enumerate(chips):
            for cp in copies(4 + j, (*chip, 1 - c), me):
                cp.wait_recv()
        for cp in first_copies(copies, me, chips) + passed:
            cp.wait_send()
        for cp in mine:
            cp.wait()

    return _Exchange(list(xs), [jax.ShapeDtypeStruct((N_DEV,) + a.shape, a.dtype) for a in xs],
                     [pltpu.SemaphoreType.DMA((7 * n,)), pltpu.SemaphoreType.DMA((7 * n,)),
                      pltpu.SemaphoreType.DMA((n,))], start, finish)


_CHIPS = ((0, 0), (0, 1), (1, 0), (1, 1))


def _plan_pair(sends):
    n = len(sends)

    def build(s_refs, o_refs, sems):
        send_sems, recv_sems = sems
        x, y, c = _xyc()
        return [pltpu.make_async_remote_copy(
            src_ref=s_refs[a].at[4 * px + 2 * py + 1 - c], dst_ref=o_refs[a].at[j],
            send_sem=send_sems.at[j * n + a], recv_sem=recv_sems.at[j * n + a], device_id=(x, y, 1 - c),
            device_id_type=MESH) for j, (px, py) in enumerate(_CHIPS) for a in range(n)]

    def start(s_refs, o_refs, sems):
        for cp in build(s_refs, o_refs, sems):
            cp.start()

    def finish(s_refs, o_refs, sems):
        for cp in build(s_refs, o_refs, sems):
            cp.wait_recv()
            cp.wait_send()

    return _Exchange(list(sends), [jax.ShapeDtypeStruct((4,) + a.shape[1:], a.dtype) for a in sends],
                     [pltpu.SemaphoreType.DMA((4 * n,)), pltpu.SemaphoreType.DMA((4 * n,))], start, finish)


def _plan_chips(ts):
    n = len(ts)
    flips = ((1, 0), (0, 1), (1, 1))

    def build(t_refs, o_refs, sems):
        send_sems, recv_sems, local_sems = sems
        x, y, c = _xyc()
        mine = 2 * x + y
        local = [pltpu.make_async_copy(t_refs[a].at[mine], o_refs[a].at[mine], local_sems.at[a]) for a in range(n)]
        remote = []
        for k, (fx, fy) in enumerate(flips):
            px = 1 - x if fx else x
            py = 1 - y if fy else y
            remote += [pltpu.make_async_remote_copy(
                src_ref=t_refs[a].at[2 * px + py], dst_ref=o_refs[a].at[mine],
                send_sem=send_sems.at[k * n + a], recv_sem=recv_sems.at[k * n + a], device_id=(px, py, c),
                device_id_type=MESH) for a in range(n)]
        return local, remote

    def start(t_refs, o_refs, sems):
        local, remote = build(t_refs, o_refs, sems)
        for cp in local + remote:
            cp.start()

    def finish(t_refs, o_refs, sems):
        local, remote = build(t_refs, o_refs, sems)
        for cp in remote:
            cp.wait_recv()
        for cp in remote:
            cp.wait_send()
        for cp in local:
            cp.wait()

    return _Exchange(list(ts), [jax.ShapeDtypeStruct(a.shape, a.dtype) for a in ts],
                     [pltpu.SemaphoreType.DMA((3 * n,)), pltpu.SemaphoreType.DMA((3 * n,)),
                      pltpu.SemaphoreType.DMA((n,))], start, finish)


def _combine(*plans):
    def parts(refs, attr):
        out, at = [], 0
        for p in plans:
            n = len(getattr(p, attr))
            out.append(refs[at:at + n])
            at += n
        return out

    def run(half):
        def go(ins, outs, sems):
            for p, a, o, s in zip(plans, parts(ins, "ins"), parts(outs, "outs"), parts(sems, "scratch")):
                getattr(p, half)(a, o, s)
        return go

    return _Exchange(sum((p.ins for p in plans), []), sum((p.outs for p in plans), []),
                     sum((p.scratch for p in plans), []), run("start"), run("finish"))


def _exchange_call(plan, name):
    n = len(plan.ins)

    def body(*refs):
        ins, outs, sems = refs[:n], refs[n:2 * n], refs[2 * n:]
        plan.start(ins, outs, sems)
        plan.finish(ins, outs, sems)

    return pl.pallas_call(
        body, name=name, out_shape=plan.outs,
        in_specs=[pl.BlockSpec(memory_space=pl.ANY)] * n, out_specs=[pl.BlockSpec(memory_space=pl.ANY)] * n,
        scratch_shapes=plan.scratch,
    )(*plan.ins)


def _slab_spec(lead, rows, cols, nb):
    if rows % (nb * 16) == 0:
        return pl.BlockSpec((lead, rows // nb, cols), lambda i: (0, i, 0))
    if cols % (nb * 128) == 0:
        return pl.BlockSpec((lead, rows, cols // nb), lambda i: (0, 0, i))
    return pl.BlockSpec((lead, rows, cols), lambda i: (0, 0, 0))


def _slab_spec2(rows, cols, nb):
    if rows % (nb * 16) == 0:
        return pl.BlockSpec((rows // nb, cols), lambda i: (i, 0))
    if cols % (nb * 128) == 0:
        return pl.BlockSpec((rows, cols // nb), lambda i: (0, i))
    return pl.BlockSpec((rows, cols), lambda i: (0, 0))


def _cast_call(arrays, name, host=None):
    n = len(arrays)
    nb = 8

    def body(*refs):
        for a in range(n):
            refs[n + a][...] = refs[a][...].astype(BF16)

    specs = [_slab_spec2(x.shape[0], x.shape[1], nb) for x in arrays]
    return _hosting_call(body, name, nb, host, list(arrays), specs,
                         [jax.ShapeDtypeStruct(x.shape, BF16) for x in arrays], specs, [])


def _pair_add(sends, fromsib, name):
    n = len(sends)
    nb = 8

    def body(*refs):
        c = lax.axis_index("c")
        for a in range(n):
            s_ref, f_ref, t_ref = refs[a], refs[n + a], refs[2 * n + a]
            for j in range(4):
                t_ref[j] = (s_ref[2 * j + c].astype(F32) + f_ref[j].astype(F32)).astype(t_ref.dtype)

    def spec(a, lead):
        return _slab_spec(lead, a.shape[1], a.shape[2], nb)

    return pl.pallas_call(
        body, name=name, grid=(nb,),
        in_specs=[spec(a, N_DEV) for a in sends] + [spec(a, 4) for a in fromsib],
        out_specs=[spec(a, 4) for a in fromsib],
        out_shape=[jax.ShapeDtypeStruct(a.shape, a.dtype) for a in fromsib],
        compiler_params=pltpu.CompilerParams(dimension_semantics=("arbitrary",), vmem_limit_bytes=VMEM_LIMIT),
    )(*sends, *fromsib)


def _adamw_vals(w, g, m, v):
    m2 = ADAM_B1 * m + (1.0 - ADAM_B1) * g
    v2 = ADAM_B2 * v + (1.0 - ADAM_B2) * (g * g)
    m_hat = m2 / (1.0 - ADAM_B1 ** ADAM_STEP)
    v_hat = v2 / (1.0 - ADAM_B2 ** ADAM_STEP)
    delta = -ADAM_LR * (m_hat / (jnp.sqrt(v_hat) + ADAM_EPS) + ADAM_WD * w)
    return delta, m2, v2


def _sum_adamw(recv, w, m, v, name):
    R, C = w.shape
    ns = recv.shape[0]
    br = next((t for t in (256, 128, 64, 32, 16) if R % t == 0), R)

    def body(r_ref, w_ref, m_ref, v_ref, g_ref, d_ref, m2_ref, v2_ref):
        g = r_ref[0].astype(F32)
        for d in range(1, ns):
            g = g + r_ref[d].astype(F32)
        dl, m2, v2 = _adamw_vals(w_ref[...], g, m_ref[...], v_ref[...])
        g_ref[...] = g
        d_ref[...] = dl
        m2_ref[...] = m2
        v2_ref[...] = v2

    spec = pl.BlockSpec((br, C), lambda i: (i, 0))
    return pl.pallas_call(
        body, name=name, grid=(R // br,),
        in_specs=[pl.BlockSpec((ns, br, C), lambda i: (0, i, 0)), spec, spec, spec], out_specs=[spec] * 4,
        out_shape=[jax.ShapeDtypeStruct((R, C), F32)] * 4,
        compiler_params=pltpu.CompilerParams(dimension_semantics=("arbitrary",)),
    )(recv, w, m, v)


def _updates_call(recvs, ws, ms, vs, name, host=None):
    n = len(recvs)
    nb = 8

    def body(*refs):
        for a in range(n):
            r_ref, w_ref, m_ref, v_ref = refs[a], refs[n + a], refs[2 * n + a], refs[3 * n + a]
            g_ref, d_ref, m2_ref, v2_ref = refs[4 * n + 4 * a:4 * n + 4 * a + 4]
            g = r_ref[0].astype(F32)
            for d in range(1, r_ref.shape[0]):
                g = g + r_ref[d].astype(F32)
            dl, m2, v2 = _adamw_vals(w_ref[...], g, m_ref[...], v_ref[...])
            g_ref[...] = g
            d_ref[...] = dl
            m2_ref[...] = m2
            v2_ref[...] = v2

    def spec3(r):
        return _slab_spec(r.shape[0], r.shape[1], r.shape[2], nb)

    def spec2(w):
        return _slab_spec2(w.shape[0], w.shape[1], nb)

    res, hosted = _hosting_call(
        body, name, nb, host, list(recvs) + list(ws) + list(ms) + list(vs),
        [spec3(r) for r in recvs] + [spec2(w) for w in ws] * 3,
        [jax.ShapeDtypeStruct(w.shape, F32) for w in ws for _ in range(4)],
        [spec2(w) for w in ws for _ in range(4)], [])
    return [res[4 * a:4 * a + 4] for a in range(n)], hosted


def _small_sum(gath, loss_g, row0_g, name):
    _, R, C = gath.shape
    br = R // 3

    def body(g_ref, l_ref, r_ref, go_ref, lo_ref):
        g = g_ref[0].astype(F32)
        lsum = l_ref[0]
        for d in range(1, N_DEV):
            g = g + g_ref[d].astype(F32)
            lsum = lsum + l_ref[d]
        go_ref[...] = g
        lo_ref[...] = lsum

        @pl.when(pl.program_id(0) == 0)
        def _():
            row0 = r_ref[0]
            for d in range(1, N_DEV):
                row0 = row0 + r_ref[d]
            go_ref[0:8, :] = go_ref[0:8, :] + jnp.where(lax.broadcasted_iota(jnp.int32, row0.shape, 0) == 0, row0, 0.0)

    return pl.pallas_call(
        body, name=name, grid=(R // br,),
        in_specs=[pl.BlockSpec((N_DEV, br, C), lambda i: (0, i, 0)),
                  pl.BlockSpec((N_DEV, 8, HD), lambda i: (0, 0, 0)), pl.BlockSpec((N_DEV, 8, C), lambda i: (0, 0, 0))],
        out_specs=[pl.BlockSpec((br, C), lambda i: (i, 0)), pl.BlockSpec((8, HD), lambda i: (0, 0))],
        out_shape=[jax.ShapeDtypeStruct((R, C), F32), jax.ShapeDtypeStruct((8, HD), F32)],
        compiler_params=pltpu.CompilerParams(dimension_semantics=("arbitrary",)),
    )(gath, loss_g, row0_g)


def _adamw_multi(ws, gs, ms, vs, name, nblk=1):
    n = len(ws)

    def body(*refs):
        for a in range(n):
            dl, m2, v2 = _adamw_vals(refs[a][...], refs[n + a][...], refs[2 * n + a][...], refs[3 * n + a][...])
            refs[4 * n + 3 * a][...] = dl
            refs[4 * n + 3 * a + 1][...] = m2
            refs[4 * n + 3 * a + 2][...] = v2

    def spec(x):
        rest = (0,) * (x.ndim - 1)
        return pl.BlockSpec((x.shape[0] // nblk,) + tuple(x.shape[1:]), lambda i: (i,) + rest)

    res = pl.pallas_call(
        body, name=name, grid=(nblk,),
        in_specs=[spec(w) for w in ws] * 4, out_specs=[spec(w) for w in ws for _ in range(3)],
        out_shape=[jax.ShapeDtypeStruct(w.shape, F32) for w in ws for _ in range(3)],
        compiler_params=pltpu.CompilerParams(dimension_semantics=("arbitrary",), vmem_limit_bytes=VMEM_LIMIT),
    )(*ws, *gs, *ms, *vs)
    return [res[3 * a:3 * a + 3] for a in range(n)]


def _s5_param_fn(lr, li, ls, btr, bti):
    step = jnp.exp(ls)
    er = jnp.exp(lr * step)
    ang = li * step
    ar = er * jnp.cos(ang)
    ai = er * jnp.sin(ang)
    nr = ar - 1.0
    den = lr * lr + li * li
    fr = (nr * lr + ai * li) / den
    fi = (ai * lr - nr * li) / den
    return ar, ai, fr * btr - fi * bti, fr * bti + fi * btr


def _s5_params(lr, li, ls, btr, bti, cre, cim):
    nb = S5_G // S5_GB
    GC = S5_GB * S5_C
    expand = jnp.asarray(np.tile(np.eye(S5_P, dtype=np.float32), (1, S5_GB)), BF16)
    own = jnp.asarray((np.arange(GC)[:, None] // S5_C == np.arange(S5_W)[None, :] // S5_P).astype(np.float32))

    def body(lr_ref, li_ref, ls_ref, br_ref, bi_ref, cr_ref, ci_ref, e_ref, own_ref, ar_ref, ai_ref, bm_ref, cm_ref):
        ar, ai, bbr, bbi = _s5_param_fn(lr_ref[...], li_ref[...], ls_ref[...], br_ref[...], bi_ref[...])
        ar_ref[...] = ar
        ai_ref[...] = ai

        def plane(x, n):
            rows = x[n * S5_GB:(n + 1) * S5_GB].reshape(GC, S5_P).astype(BF16)
            return _dot(rows, e_ref[...]) * own_ref[...]

        for n in range(nb):
            bm_ref[n] = jnp.concatenate([plane(bbr, n), plane(bbi, n)], axis=-1).astype(BF16)
            cm_ref[n] = jnp.concatenate([plane(cr_ref[...], n), -plane(ci_ref[...], n)], axis=-1).astype(BF16)

    sd = jax.ShapeDtypeStruct
    return pl.pallas_call(
        body, name="s5_params",
        out_shape=[sd(lr.shape, F32), sd(lr.shape, F32), sd((nb, GC, 2 * S5_W), BF16), sd((nb, GC, 2 * S5_W), BF16)],
        compiler_params=pltpu.CompilerParams(vmem_limit_bytes=VMEM_LIMIT),
    )(lr, li, ls, btr, bti, cre, cim, expand, own)


def _s5_params_bwd(lr, li, ls, btr, bti, dar, dai, dbbr, dbbi):
    def body(lr_ref, li_ref, ls_ref, br_ref, bi_ref, dar_ref, dai_ref, dbbr_ref, dbbi_ref,
             dlr_ref, dli_ref, dls_ref, dbr_ref, dbi_ref):
        _, vjp = jax.vjp(_s5_param_fn, lr_ref[...], li_ref[...], ls_ref[...], br_ref[...], bi_ref[...])
        dlr, dli, dls, dbr, dbi = vjp((dar_ref[...], dai_ref[...], dbbr_ref[...], dbbi_ref[...]))
        dlr_ref[...] = dlr
        dli_ref[...] = dli
        dls_ref[...] = dls
        dbr_ref[...] = dbr
        dbi_ref[...] = dbi

    sd = jax.ShapeDtypeStruct
    return pl.pallas_call(
        body, name="s5_params_bwd",
        out_shape=[sd(lr.shape, F32), sd(lr.shape, F32), sd(ls.shape, F32), sd(btr.shape, F32), sd(btr.shape, F32)],
    )(lr, li, ls, btr, bti, dar, dai, dbbr, dbbi)


def _cpow(ar, ai, n):
    assert n & (n - 1) == 0
    while n > 1:
        ar, ai = ar * ar - ai * ai, 2.0 * ar * ai
        n //= 2
    return ar, ai


def _scan(st, cr, ci, init, nk, reverse, store, prev=None):
    W = S5_W

    def advance(k, sr, si):
        rows = pl.ds(k * 8 if isinstance(k, int) else pl.multiple_of(k * 8, 8), 8)
        nsr = cr * sr - ci * si + st[rows, 0:W]
        nsi = cr * si + ci * sr + st[rows, W:2 * W]
        if store:
            st[rows, 0:W] = nsr
            st[rows, W:2 * W] = nsi
        return nsr, nsi

    if prev is None:
        return lax.fori_loop(0, nk, lambda j, c: advance(nk - 1 - j if reverse else j, c[0], c[1]), init, unroll=2)
    assert reverse

    def step(j, carry):
        k = nk - 1 - j
        nsr, nsi = advance(k, carry[0], carry[1])
        prows = pl.ds(pl.multiple_of((k - 1) * 8, 8), 8)
        pr = prev[prows, 0:W]
        pi = prev[prows, W:2 * W]
        return nsr, nsi, carry[2] + nsr * pr + nsi * pi, carry[3] + nsi * pr - nsr * pi

    carry = lax.fori_loop(0, nk - 1, step, init, unroll=2)
    nsr, nsi = advance(0, carry[0], carry[1])
    return nsr, nsi, carry[2], carry[3]


def _chain(fin, fr, fi, pr, pi, reverse):
    W = S5_W
    fin[:, 0:W] = fr
    fin[:, W:2 * W] = fi
    rowid = lax.broadcasted_iota(jnp.int32, (8, W), 0)
    cr = jnp.zeros((1, W), F32)
    ci = jnp.zeros((1, W), F32)
    init_r = jnp.zeros((8, W), F32)
    init_i = jnp.zeros((8, W), F32)
    for s in (range(7, -1, -1) if reverse else range(8)):
        init_r = jnp.where(rowid == s, cr, init_r)
        init_i = jnp.where(rowid == s, ci, init_i)
        lr = fin[s:s + 1, 0:W]
        li = fin[s:s + 1, W:2 * W]
        cr, ci = lr + pr * cr - pi * ci, li + pr * ci + pi * cr
    return init_r, init_i


def _full_scan(st, fin, ar, ai, nk, reverse, prev=None, carry_in=None, carry_out=None):
    W = S5_W
    cr = jnp.broadcast_to(ar, (8, W))
    ci = jnp.broadcast_to(-ai if reverse else ai, (8, W))
    z = jnp.zeros((8, W), F32)
    if carry_in is None:
        fr, fi = _scan(st, cr, ci, (z, z), nk, reverse, store=False)
        pr, pi = _cpow(ar, -ai if reverse else ai, nk)
        init = _chain(fin, fr, fi, pr, pi, reverse)
    else:
        init = (carry_in[:, 0:W], carry_in[:, W:2 * W])
    if carry_out is not None:
        carry_out[:, 0:W] = init[0]
        carry_out[:, W:2 * W] = init[1]
    if prev is None:
        return _scan(st, cr, ci, init, nk, reverse, store=True)
    return _scan(st, cr, ci, init + (z, z), nk, reverse, store=True, prev=prev)


def _s5_specs(L):
    W2 = 2 * S5_W
    GC = S5_GB * S5_C
    col = pl.BlockSpec((L, GC), lambda g: (0, g))
    vec = pl.BlockSpec((1, GC), lambda g: (0, g))
    avec = pl.BlockSpec((1, S5_W), lambda g: (0, g))
    bmat = pl.BlockSpec((None, GC, W2), lambda g: (g, 0, 0))
    cmat = pl.BlockSpec((None, W2, GC), lambda g: (g, 0, 0))
    return col, vec, avec, bmat, cmat


def _interleave(dst, src, nk):
    for s in range(8):
        dst[pl.ds(s, nk, stride=8), :] = src[s * nk:(s + 1) * nk, :]


def _deinterleave(dst, src, nk):
    for s in range(8):
        dst[s * nk:(s + 1) * nk, :] = src[pl.ds(s, nk, stride=8), :].astype(dst.dtype)


def _hosting_call(body, name, nsteps, host, ins, in_specs, outs, out_specs, scratch):
    grid = (nsteps,) if isinstance(nsteps, int) else tuple(nsteps)
    params = pltpu.CompilerParams(dimension_semantics=("arbitrary",) * len(grid), vmem_limit_bytes=VMEM_LIMIT)
    if host is None:
        res = pl.pallas_call(
            body, name=name, grid=grid, in_specs=in_specs, out_specs=out_specs, out_shape=outs,
            scratch_shapes=scratch, compiler_params=params,
        )(*ins)
        return list(res), []
    n_in, n_out, n_sc = len(ins), len(outs), len(scratch)
    h_in, h_out = len(host.ins), len(host.outs)

    def hosted(*refs):
        a = refs[:n_in]
        ha = refs[n_in:n_in + h_in]
        o = refs[n_in + h_in:n_in + h_in + n_out]
        ho = refs[n_in + h_in + n_out:n_in + h_in + n_out + h_out]
        sc = refs[n_in + h_in + n_out + h_out:n_in + h_in + n_out + h_out + n_sc]
        hs = refs[n_in + h_in + n_out + h_out + n_sc:]
        first = functools.reduce(jnp.logical_and, [pl.program_id(i) == 0 for i in range(len(grid))])
        last = functools.reduce(jnp.logical_and, [pl.program_id(i) == g - 1 for i, g in enumerate(grid)])

        @pl.when(first)
        def _():
            host.start(ha, ho, hs)

        body(*a, *o, *sc)

        @pl.when(last)
        def _():
            host.finish(ha, ho, hs)

    hbm = pl.BlockSpec(memory_space=pl.ANY)
    res = pl.pallas_call(
        hosted, name=name, grid=grid,
        in_specs=list(in_specs) + [hbm] * h_in, out_specs=list(out_specs) + [hbm] * h_out,
        out_shape=list(outs) + list(host.outs), scratch_shapes=list(scratch) + list(host.scratch),
        compiler_params=params,
    )(*ins, *host.ins)
    return list(res[:n_out]), list(res[n_out:])


def _s5_fwd(u, bm, cm, ar, ai, dvec, host=None):
    L = u.shape[0]
    nk = L // 8
    GC = S5_GB * S5_C
    nb = S5_G // S5_GB
    col, vec, avec, bmat, cmat = _s5_specs(L)

    def body(u_ref, b_ref, c_ref, ar_ref, ai_ref, d_ref, y_ref, carry_ref, st, fin, ui, yi):
        _interleave(ui, u_ref, nk)
        for r in range(8):
            rows = slice(r * nk, (r + 1) * nk)
            st[rows, :] = _dot(ui[rows, :].astype(BF16), b_ref[...])
        _full_scan(st, fin, ar_ref[...], ai_ref[...], nk, reverse=False, carry_out=carry_ref)
        for r in range(8):
            rows = slice(r * nk, (r + 1) * nk)
            yi[rows, :] = _dot_nt(st[rows, :].astype(BF16), c_ref[...]) + d_ref[...] * ui[rows, :]
        _deinterleave(y_ref, yi, nk)

    return _hosting_call(
        body, "s5_fwd", nb, host,
        [u, bm, cm, ar, ai, dvec], [col, bmat, bmat, avec, avec, vec],
        [jax.ShapeDtypeStruct(u.shape, F32), jax.ShapeDtypeStruct((nb * 8, 2 * S5_W), F32)],
        [col, pl.BlockSpec((8, 2 * S5_W), lambda g: (g, 0))],
        [pltpu.VMEM((L, 2 * S5_W), F32), pltpu.VMEM((8, 2 * S5_W), F32), pltpu.VMEM((L, GC), F32),
         pltpu.VMEM((L, GC), F32)])


def _s5_bwd(u, dy, carry, bm, cm, ar, ai, dvec, mask, rmat, host=None):
    L = u.shape[0]
    nk = L // 8
    W = S5_W
    GC = S5_GB * S5_C
    col, vec, avec, bmat, cmat = _s5_specs(L)
    hi = lax.Precision.HIGHEST

    def body(u_ref, dy_ref, carry_ref, b_ref, ct_ref, ar_ref, ai_ref, d_ref, mask_ref, r_ref,
             du_ref, db_ref, dc_ref, dd_ref, dar_ref, dai_ref, sa, sb, fin, ui, dyi, dui):
        ar = ar_ref[...]
        ai = ai_ref[...]
        _interleave(ui, u_ref, nk)
        _interleave(dyi, dy_ref, nk)
        for r in range(8):
            rows = slice(r * nk, (r + 1) * nk)
            sa[rows, :] = _dot(ui[rows, :].astype(BF16), b_ref[...])
            sb[rows, :] = _dot(dyi[rows, :].astype(BF16), ct_ref[...])
        _full_scan(sa, fin, ar, ai, nk, reverse=False, carry_in=carry_ref)
        gr, gi, accr, acci = _full_scan(sb, fin, ar, ai, nk, reverse=True, prev=sa)
        rowid = lax.broadcasted_iota(jnp.int32, (8, W), 0)
        last = pl.ds((nk - 1) * 8, 8)
        pr = jnp.where(rowid == 0, 0.0, pltpu.roll(sa[last, 0:W], 1, 0))
        pi = jnp.where(rowid == 0, 0.0, pltpu.roll(sa[last, W:2 * W], 1, 0))
        accr = accr + gr * pr + gi * pi
        acci = acci + gi * pr - gr * pi
        dar_ref[...] = jnp.sum(accr, axis=0, keepdims=True)
        dai_ref[...] = jnp.sum(acci, axis=0, keepdims=True)
        dbf = jnp.zeros((GC, 2 * W), F32)
        dcf = jnp.zeros((GC, 2 * W), F32)
        dd = jnp.zeros((1, GC), F32)
        for r in range(8):
            rows = slice(r * nk, (r + 1) * nk)
            ub = ui[rows, :]
            dyb = dyi[rows, :]
            gb = sb[rows, :].astype(BF16)
            dui[rows, :] = _dot_nt(gb, b_ref[...]) + d_ref[...] * dyb
            dbf = dbf + _dot_tn(ub.astype(BF16), gb)
            dcf = dcf + _dot_tn(dyb.astype(BF16), sa[rows, :].astype(BF16))
            dd = dd + jnp.sum(dyb * ub, axis=0, keepdims=True)
        db_ref[...] = jnp.dot(dbf * mask_ref[...], r_ref[...], precision=hi, preferred_element_type=F32)
        dc_ref[...] = jnp.dot(dcf * mask_ref[...], r_ref[...], precision=hi, preferred_element_type=F32)
        dd_ref[...] = dd
        _deinterleave(du_ref, dui, nk)

    cmp_spec = pl.BlockSpec((GC, 2 * S5_P), lambda g: (g, 0))
    whole = lambda shape: pl.BlockSpec(shape, lambda g: (0, 0))
    sd = jax.ShapeDtypeStruct
    return _hosting_call(
        body, "s5_bwd", S5_G // S5_GB, host,
        [u, dy, carry, bm, cm, ar, ai, dvec, mask, rmat],
        [col, col, pl.BlockSpec((8, 2 * W), lambda g: (g, 0)), bmat, bmat, avec, avec, vec, whole(mask.shape),
         whole(rmat.shape)],
        [sd(u.shape, BF16), sd((S5_G * S5_C, 2 * S5_P), F32), sd((S5_G * S5_C, 2 * S5_P), F32),
         sd((1, PRIM), F32), sd((1, S5_G * S5_P), F32), sd((1, S5_G * S5_P), F32)],
        [col, cmp_spec, cmp_spec, vec, avec, avec],
        [pltpu.VMEM((L, 2 * W), F32), pltpu.VMEM((L, 2 * W), F32), pltpu.VMEM((8, 2 * W), F32),
         pltpu.VMEM((L, GC), F32), pltpu.VMEM((L, GC), F32), pltpu.VMEM((L, GC), F32)])


def _s5_compact_consts():
    g_row = np.arange(S5_GB * S5_C) // S5_C
    col = np.arange(2 * S5_W)
    g_col = (col % S5_W) // S5_P
    mask = (g_row[:, None] == g_col[None, :]).astype(np.float32)
    tgt = (col // S5_W) * S5_P + col % S5_P
    rmat = (tgt[:, None] == np.arange(2 * S5_P)[None, :]).astype(np.float32)
    return jnp.asarray(mask), jnp.asarray(rmat)


def _attn_scores(q_ref, k_ref, qb, bq, scale):
    ext = (qb + 1) * bq
    s = _dot_nt(q_ref[qb * bq:ext, :], k_ref[0:ext, :]) * scale
    qpos = lax.broadcasted_iota(jnp.int32, (bq, bq), 0)
    kpos = lax.broadcasted_iota(jnp.int32, (bq, bq), 1)
    diag = jnp.where(kpos <= qpos, s[:, ext - bq:], NEG)
    return diag if qb == 0 else jnp.concatenate([s[:, :ext - bq], diag], axis=-1)


def _attn_fwd(qp, kp, v, scale):
    L = qp.shape[0]
    bq = min(256, L)

    def body(q_ref, k_ref, v_ref, o_ref, lse_ref):
        for qb in range(L // bq):
            rows = slice(qb * bq, (qb + 1) * bq)
            s = _attn_scores(q_ref, k_ref, qb, bq, scale)
            m = jnp.max(s, axis=-1, keepdims=True)
            e = jnp.exp(s - m)
            l = jnp.sum(e, axis=-1, keepdims=True)
            o_ref[rows, :] = _dot(e.astype(BF16), v_ref[0:(qb + 1) * bq, :]) / l
            lse_ref[rows, :] = jnp.broadcast_to(m + jnp.log(l), (bq, HD))

    blk = pl.BlockSpec((L, HD), lambda h: (0, h))
    wide = pl.BlockSpec((L, 2 * HD), lambda h: (0, h))
    return pl.pallas_call(
        body, name="mla_attn_fwd", grid=(MLA_H,),
        in_specs=[wide, wide, blk], out_specs=[blk, blk],
        out_shape=[jax.ShapeDtypeStruct((L, MLA_H * HD), F32)] * 2,
        compiler_params=pltpu.CompilerParams(dimension_semantics=("arbitrary",), vmem_limit_bytes=VMEM_LIMIT),
    )(qp, kp, v)


def _attn_bwd(qp, kp, v, o, lse, do, scale):
    L = qp.shape[0]
    bq = min(256, L)
    nq = L // bq

    def body(q_ref, k_ref, v_ref, o_ref, lse_ref, do_ref, dq_ref, dk_ref, dv_ref, dk_acc, dv_acc):
        dk_acc[...] = jnp.zeros_like(dk_acc)
        dv_acc[...] = jnp.zeros_like(dv_acc)
        for qb in range(nq):
            rows = slice(qb * bq, (qb + 1) * bq)
            ext = (qb + 1) * bq
            do = do_ref[rows, :]
            dob = do.astype(BF16)
            p = jnp.exp(_attn_scores(q_ref, k_ref, qb, bq, scale) - lse_ref[rows, 0:1])
            dp = _dot_nt(dob, v_ref[0:ext, :])
            dsum = jnp.sum(do * o_ref[rows, :], axis=-1, keepdims=True)
            ds = (p * (dp - dsum) * scale).astype(BF16)
            dq_ref[rows, :] = _dot(ds, k_ref[0:ext, :]).astype(dq_ref.dtype)
            dk_acc[0:ext, :] += _dot_tn(ds, q_ref[rows, :])
            dv_acc[0:ext, :] += _dot_tn(p.astype(BF16), dob)
        dk_ref[...] = dk_acc[...].astype(dk_ref.dtype)
        dv_ref[...] = dv_acc[...].astype(dv_ref.dtype)

    sd = jax.ShapeDtypeStruct
    blk = pl.BlockSpec((L, HD), lambda h: (0, h))
    wide = pl.BlockSpec((L, 2 * HD), lambda h: (0, h))
    return pl.pallas_call(
        body, name="mla_attn_bwd", grid=(MLA_H,),
        in_specs=[wide, wide, blk, blk, blk, blk], out_specs=[wide, wide, blk],
        out_shape=[sd((L, MLA_H * 2 * HD), BF16), sd((L, MLA_H * 2 * HD), BF16), sd((L, MLA_H * HD), BF16)],
        scratch_shapes=[pltpu.VMEM((L, 2 * HD), F32), pltpu.VMEM((L, HD), F32)],
        compiler_params=pltpu.CompilerParams(dimension_semantics=("arbitrary",), vmem_limit_bytes=VMEM_LIMIT),
    )(qp, kp, v, o, lse, do)


def _kv_fn(mem, gm, w, gk):
    kv = _mm(_rms(mem, gm, D_MODEL), w)
    k = jnp.concatenate([_rms(kv[:, HD * h:HD * (h + 1)], gk, HD) for h in range(X_HEADS)], axis=-1)
    return k, kv[:, XQ:]


def _kv_prep(mem, gm, w, gk, name):
    def fn(mem, gm, w, gk):
        return _kv_fn(mem, gm, w, gk)
    M = mem.shape[0]
    return _rowwise(name, fn, [('c', mem), ('c', gm), ('c', w), ('c', gk)],
                    [('c', (M, XQ), F32), ('c', (M, XQ), F32)], 1)


def _kv_prep_bwd(mem, gm, w, gk, dk, dv, name):
    def fn(mem, gm, w, gk, dk, dv):
        _, vjp = jax.vjp(lambda a, b, c: _kv_fn(mem, a, b, c), gm, w, gk)
        return vjp((dk, dv))
    return _rowwise(name, fn, [('c', mem), ('c', gm), ('c', w), ('c', gk), ('c', dk), ('c', dv)],
                    [('c', gm.shape, F32), ('c', w.shape, BF16), ('c', gk.shape, F32)], 1)


def _forward_merge(x, mix, mix_kind, xq, gate, k, v, gq, wout, name, nblk, host=None):
    def fn(x, mix, xq, gate, k, v, gq, wout):
        o = _merge(mix, xq, gate, k, v, gq)
        return (x + _dot(o.astype(BF16), wout),)
    L = x.shape[0]
    out = _rowwise(name, fn, [('r', x), (mix_kind, mix), ('r', xq), ('r', gate), ('c', k), ('c', v), ('c', gq),
                              ('c', wout)], [('r', (L, D_MODEL), F32)], nblk, host=host)
    return out[0] if host is None else (out[0][0], out[1])


def _backward_merge(dx, mix, mix_kind, xq, gate, k, v, gq, wout, name, nblk, host=None):
    def fn(dx, mix, xq, gate, k, v, gq, wout):
        g16 = dx.astype(BF16)
        do = _dot_nt(g16, wout)
        o, vjp = jax.vjp(_merge, mix, xq, gate, k, v, gq)
        dmix, dxq, dgate, dk, dv, dgq = vjp(do)
        return dmix, dxq, dgate, o, g16, dk, dv, dgq
    L = dx.shape[0]
    return _rowwise(
        name, fn,
        [('r', dx), (mix_kind, mix), ('r', xq), ('r', gate), ('c', k), ('c', v), ('c', gq), ('c', wout)],
        [('r', (L, PRIM), F32), ('r', (L, XQ), BF16), ('r', (L, BRANCH), BF16), ('t', (BRANCH, L), BF16),
         ('r', (L, D_MODEL), BF16), ('a', k.shape, F32), ('a', v.shape, F32), ('a', gq.shape, F32)], nblk,
        host=host)


_MLA_IN = 3392
_MLA_IN_PAD = 3456


def _uq_rows(wt):
    r = wt.reshape(MLA_H, HD + ROPE, wt.shape[1])
    return jnp.concatenate([r[:, :HD].reshape(PRIM, -1),
                            jnp.pad(r[:, HD:], ((0, 0), (0, HD - ROPE), (0, 0))).reshape(PRIM, -1)], axis=0)


def _uq_rows_back(wt):
    nope = wt[:PRIM].reshape(MLA_H, HD, -1)
    rope = wt[PRIM:].reshape(MLA_H, HD, -1)[:, :ROPE]
    return jnp.concatenate([nope, rope], axis=1).reshape(MLA_H * (HD + ROPE), -1)


def _mla_in_rows(wt):
    return jnp.concatenate([wt[:768], wt[832:], wt[768:832], jnp.zeros((64, wt.shape[1]), wt.dtype)], axis=0)


def _mla_in_rows_back(wt):
    return jnp.concatenate([wt[:768], wt[3328:3392], wt[768:3328]], axis=0)


_SMALL = (("ln_gain", 2048), ("mem_norm", 2048), ("xq_norm", 256), ("xk_norm", 256), ("s5_lambda_re", 6144),
          ("s5_lambda_im", 6144), ("s5_log_step", 96), ("s5_b_re", 98304), ("s5_b_im", 98304), ("s5_c_re", 98304),
          ("s5_c_im", 98304), ("s5_d", 1536), ("mla_q_lora_norm", 512), ("mla_kv_lora_norm", 256),
          ("mla_q_nope_norm", 128), ("mla_k_nope_norm", 128), ("mla_q_rope_norm", 64), ("mla_k_rope_norm", 64))
_SMALL_ROWS = 432
_SMALL_OFF = {name: sum(n for _, n in _SMALL[:i]) for i, (name, _) in enumerate(_SMALL)}


def _pack_small(d):
    flat = jnp.concatenate([d[n].reshape(-1).astype(F32) for n, _ in _SMALL])
    return jnp.pad(flat, (0, _SMALL_ROWS * 1024 - flat.shape[0])).reshape(_SMALL_ROWS, 1024)


def _unpack_small(p, name, shape):
    off = _SMALL_OFF[name]
    return p.reshape(-1)[off:off + int(np.prod(shape))].reshape(shape)


_WEIGHTS = ('ln_gain', 'w_out', 'mem_norm', 'w_mem_kv', 'xq_norm', 'xk_norm', 's5_w_in', 's5_lambda_re',
            's5_lambda_im', 's5_log_step', 's5_b_re', 's5_b_im', 's5_c_re', 's5_c_im', 's5_d', 's5_w_glu', 'mla_w_in',
            'mla_q_lora_norm', 'mla_kv_lora_norm', 'mla_w_uq', 'mla_w_ukv', 'mla_q_nope_norm', 'mla_k_nope_norm',
            'mla_q_rope_norm', 'mla_k_rope_norm')
_BIG = ('w_out', 'w_mem_kv', 's5_w_in', 's5_w_glu', 'mla_w_in', 'mla_w_uq', 'mla_w_ukv')


def _pad128(g):
    return jnp.pad(g.reshape(1, -1), ((0, 0), (0, HD - g.shape[-1])))


def kernel(x, mem, positions, ln_gain, w_out, mem_norm, w_mem_kv, xq_norm, xk_norm, s5_w_in, s5_lambda_re, s5_lambda_im, s5_log_step, s5_b_re, s5_b_im, s5_c_re, s5_c_im, s5_d, s5_w_glu, mla_w_in, mla_q_lora_norm, mla_kv_lora_norm, mla_w_uq, mla_w_ukv, mla_q_nope_norm, mla_k_nope_norm, mla_q_rope_norm, mla_k_rope_norm, loss_target, m_ln_gain, m_w_out, m_mem_norm, m_w_mem_kv, m_xq_norm, m_xk_norm, m_s5_w_in, m_s5_lambda_re, m_s5_lambda_im, m_s5_log_step, m_s5_b_re, m_s5_b_im, m_s5_c_re, m_s5_c_im, m_s5_d, m_s5_w_glu, m_mla_w_in, m_mla_q_lora_norm, m_mla_kv_lora_norm, m_mla_w_uq, m_mla_w_ukv, m_mla_q_nope_norm, m_mla_k_nope_norm, m_mla_q_rope_norm, m_mla_k_rope_norm, v_ln_gain, v_w_out, v_mem_norm, v_w_mem_kv, v_xq_norm, v_xk_norm, v_s5_w_in, v_s5_lambda_re, v_s5_lambda_im, v_s5_log_step, v_s5_b_re, v_s5_b_im, v_s5_c_re, v_s5_c_im, v_s5_d, v_s5_w_glu, v_mla_w_in, v_mla_q_lora_norm, v_mla_kv_lora_norm, v_mla_w_uq, v_mla_w_ukv, v_mla_q_nope_norm, v_mla_k_nope_norm, v_mla_q_rope_norm, v_mla_k_rope_norm):
    weights = dict(ln_gain=ln_gain, w_out=w_out, mem_norm=mem_norm, w_mem_kv=w_mem_kv, xq_norm=xq_norm,
                   xk_norm=xk_norm, s5_w_in=s5_w_in, s5_lambda_re=s5_lambda_re, s5_lambda_im=s5_lambda_im,
                   s5_log_step=s5_log_step, s5_b_re=s5_b_re, s5_b_im=s5_b_im, s5_c_re=s5_c_re, s5_c_im=s5_c_im,
                   s5_d=s5_d, s5_w_glu=s5_w_glu, mla_w_in=mla_w_in, mla_q_lora_norm=mla_q_lora_norm,
                   mla_kv_lora_norm=mla_kv_lora_norm, mla_w_uq=mla_w_uq, mla_w_ukv=mla_w_ukv,
                   mla_q_nope_norm=mla_q_nope_norm, mla_k_nope_norm=mla_k_nope_norm,
                   mla_q_rope_norm=mla_q_rope_norm, mla_k_rope_norm=mla_k_rope_norm)
    m_in = dict(zip(_WEIGHTS, (m_ln_gain, m_w_out, m_mem_norm, m_w_mem_kv, m_xq_norm, m_xk_norm, m_s5_w_in,
                               m_s5_lambda_re, m_s5_lambda_im, m_s5_log_step, m_s5_b_re, m_s5_b_im, m_s5_c_re,
                               m_s5_c_im, m_s5_d, m_s5_w_glu, m_mla_w_in, m_mla_q_lora_norm, m_mla_kv_lora_norm,
                               m_mla_w_uq, m_mla_w_ukv, m_mla_q_nope_norm, m_mla_k_nope_norm, m_mla_q_rope_norm,
                               m_mla_k_rope_norm)))
    v_in = dict(zip(_WEIGHTS, (v_ln_gain, v_w_out, v_mem_norm, v_w_mem_kv, v_xq_norm, v_xk_norm, v_s5_w_in,
                               v_s5_lambda_re, v_s5_lambda_im, v_s5_log_step, v_s5_b_re, v_s5_b_im, v_s5_c_re,
                               v_s5_c_im, v_s5_d, v_s5_w_glu, v_mla_w_in, v_mla_q_lora_norm, v_mla_kv_lora_norm,
                               v_mla_w_uq, v_mla_w_ukv, v_mla_q_nope_norm, v_mla_k_nope_norm, v_mla_q_rope_norm,
                               v_mla_k_rope_norm)))

    x0 = x[0]
    mem0 = mem[0]
    target = loss_target[0]
    L = x0.shape[0]
    nblk = 4
    nb_big = 8
    me = 4 * lax.axis_index("x") + 2 * lax.axis_index("y") + lax.axis_index("c")

    lora = jnp.pad(jnp.concatenate([mla_q_lora_norm, mla_kv_lora_norm], axis=1), ((0, 7), (0, HD - 96)))
    def gather(*shards):
        return _plan_all_gather(list(shards))

    kh = D_MODEL // 2
    (b_mkv0, b_glu, b_in_mla, b_out0, b_uq, b_ukv, b_mkv1, b_out1), (W_in_s5,) = _cast_call(
        [w_mem_kv[0], s5_w_glu[0], jnp.transpose(mla_w_in[0]), w_out[0], jnp.transpose(mla_w_uq[0]), mla_w_ukv[0],
         w_mem_kv[1], w_out[1]], "cast_shards", host=gather(s5_w_in[0].astype(BF16)))

    ln0, ln1 = ln_gain[0:1], ln_gain[1:2]
    gq0, gq1 = xq_norm[0:1], xq_norm[1:2]
    gk0, gk1 = xk_norm[0:1], xk_norm[1:2]
    gm0, gm1 = mem_norm[0:1], mem_norm[1:2]
    gqn, gkn = mla_q_nope_norm, mla_k_nope_norm
    gqr, gkr = _pad128(mla_q_rope_norm), _pad128(mla_k_rope_norm)

    lr3 = s5_lambda_re.reshape(S5_G, 1, S5_P)
    li3 = s5_lambda_im.reshape(S5_G, 1, S5_P)
    ls3 = s5_log_step.reshape(S5_G, 1, 1)
    btr = jnp.swapaxes(s5_b_re[0], 1, 2)
    bti = jnp.swapaxes(s5_b_im[0], 1, 2)
    a_r, a_i, bm, cm = _s5_params(lr3, li3, ls3, btr, bti, s5_c_re[0], s5_c_im[0])
    a_r2 = a_r.reshape(1, S5_G * S5_P)
    a_i2 = a_i.reshape(1, S5_G * S5_P)
    cmask, rmat = _s5_compact_consts()

    half = ROPE // 2
    inv_freq = ROPE_THETA ** (-jnp.arange(half, dtype=F32) / half)
    invf = jnp.concatenate([inv_freq, inv_freq, jnp.zeros((HD - ROPE,), F32)]).reshape(1, HD)

    def rot_tables(pos, invf):
        ang = pos.astype(F32) * invf
        lane = lax.broadcasted_iota(jnp.int32, ang.shape, 1)
        c = jnp.where(lane < ROPE, jnp.cos(ang), 0.0)
        s = jnp.sin(ang)
        return c, jnp.where(lane < half, -s, 0.0), jnp.where((lane >= half) & (lane < ROPE), s, 0.0)

    tc, ts1, ts2 = _rowwise("rot_tables", rot_tables, [('r', positions.reshape(L, 1)), ('c', invf)],
                            [('r', (L, HD), F32)] * 3, nblk)

    def in_s5(x, g, w):
        proj = _mm_slots(_rms(x, g, D_MODEL).astype(BF16), w)
        return proj[:, :PRIM], proj[:, PRIM:PRIM + XQ], proj[:, PRIM + XQ:]

    u_s5, xq_a, gate_a = _rowwise(
        "s5_in", in_s5, [('r', x0), ('c', ln0), ('c', W_in_s5)],
        [('r', (L, PRIM), F32), ('r', (L, XQ), F32), ('r', (L, BRANCH), F32)], nblk)
    (y_s5, s5_carry), (W_glu, G_mkv0, G_in_mla_a) = _s5_fwd(u_s5, bm, cm, a_r2, a_i2, s5_d,
                                                            host=gather(b_glu, b_mkv0, b_in_mla[:, :kh]))

    def glu(y, w):
        z = _mm_slots(_gelu(y).astype(BF16), w)
        return (z[:, :PRIM] * _sigmoid(z[:, PRIM:]),)

    (y2,), (G_out0,) = _rowwise("s5_glu", glu, [('r', y_s5), ('c', W_glu)], [('r', (L, PRIM), F32)], nblk,
                                host=gather(b_out0))
    W_mkv0 = G_mkv0.reshape(D_MODEL, 2 * XQ)
    k_a, v_a = _kv_prep(mem0, gm0, W_mkv0, gk0, "kv_prep0")
    x1, (G_in_mla_b,) = _forward_merge(
        x0, y2, 'r', xq_a, gate_a, k_a, v_a, gq0, G_out0.reshape(BRANCH, D_MODEL), "merge0", nblk,
        host=gather(b_in_mla[:, kh:]))
    W_in_mla = _mla_in_rows(jnp.concatenate([G_in_mla_a, G_in_mla_b], axis=2).reshape(_MLA_IN, D_MODEL))

    def in_mla(x, g, w):
        proj = _dot_nt(_rms(x, g, D_MODEL).astype(BF16), w)
        return proj[:, :512], proj[:, 512:768], proj[:, 768:1280], proj[:, 1280:3328], proj[:, 3328:]

    (c_q, c_kv, xq_b, gate_b, krp), (G_uq, W_kv, G_lora) = _rowwise(
        "mla_in", in_mla, [('r', x1), ('c', ln1), ('c', W_in_mla)],
        [('r', (L, Q_LORA), F32), ('r', (L, KV_LORA), F32), ('r', (L, XQ), F32), ('r', (L, BRANCH), F32),
         ('r', (L, HD), F32)], nblk,
        host=gather(b_uq, b_ukv, lora))
    W_q = _uq_rows(G_uq.reshape(MLA_H * (HD + ROPE), Q_LORA))
    g_qlora = G_lora[:, 0, :64].reshape(1, Q_LORA)
    g_kvlora = G_lora[:, 0, 64:96].reshape(1, KV_LORA)

    def qkv(c_q, c_kv, krp, tc, ts1, ts2, gql, gkvl, wq, wkv, gqn, gkn, gqr, gkr):
        q = _dot_nt(_rms(c_q, gql, Q_LORA).astype(BF16), wq)
        kv = _mm_slots(_rms(c_kv, gkvl, KV_LORA).astype(BF16), wkv)
        kp, v = _kv_post(*_kv_chunks(kv), krp, gkn, gkr, tc, ts1, ts2)
        return _q_post(*_q_chunks(q), gqn, gqr, tc, ts1, ts2), kp, v

    qkv_consts = [('c', g_qlora), ('c', g_kvlora), ('c', W_q), ('c', W_kv), ('c', gqn), ('c', gkn), ('c', gqr),
                  ('c', gkr)]
    (q_pad, k_pad, v_h), (G_mkv1, G_out1) = _rowwise(
        "mla_qkv", qkv, [('r', c_q), ('r', c_kv), ('r', krp), ('r', tc), ('r', ts1), ('r', ts2)] + qkv_consts,
        [('r', (L, 2 * PRIM), BF16), ('r', (L, 2 * PRIM), BF16), ('r', (L, PRIM), BF16)], nblk,
        host=gather(b_mkv1, b_out1))
    W_out = (G_out0.reshape(BRANCH, D_MODEL), G_out1.reshape(BRANCH, D_MODEL))
    W_mkv = (W_mkv0, G_mkv1.reshape(D_MODEL, 2 * XQ))
    scale = (HD + ROPE) ** -0.5
    attn, lse = _attn_fwd(q_pad, k_pad, v_h, scale)
    k_b, v_b = _kv_prep(mem0, gm1, W_mkv[1], gk1, "kv_prep1")

    def merge_loss(x, mix, xq, gate, k, v, gq, wout, t):
        err = x + _dot(_merge(mix, xq, gate, k, v, gq).astype(BF16), wout) - t
        part = 0.5 * jnp.sum(jnp.sum(err * err, axis=-1, keepdims=True) * (1.0 / D_MODEL), axis=0, keepdims=True)
        return err * (1.0 / D_MODEL), jnp.broadcast_to(part, (1, HD))

    dx2, loss_part = _rowwise(
        "merge1_loss", merge_loss,
        [('r', x1), ('r', attn), ('r', xq_b), ('r', gate_b), ('c', k_b), ('c', v_b), ('c', gq1), ('c', W_out[1]),
         ('r', target)], [('r', (L, D_MODEL), F32), ('a', (1, HD), F32)], nblk)

    dattn, dxq_b, dgate_b, o_b, g_b, dk_b, dv_b, dgq1 = _backward_merge(
        dx2, attn, 'r', xq_b, gate_b, k_b, v_b, gq1, W_out[1], "merge1_bwd", nb_big)
    dgm1, dW_mkv1, dgk1 = _kv_prep_bwd(mem0, gm1, W_mkv[1], gk1, dk_b, dv_b, "kv_prep1_bwd")
    dW_out1 = _matmul_tn(o_b, g_b, "dw_out1")
    dq_pad, dk_pad, dv_h = _attn_bwd(q_pad, k_pad, v_h, attn, lse, dattn, scale)

    def qkv_bwd(c_q, c_kv, krp, tc, ts1, ts2, dqp, dkp, dv, gql, gkvl, wq, wkv, gqn, gkn, gqr, gkr):
        cqn, vjp_qn = jax.vjp(lambda a, b: _rms(a, b, Q_LORA), c_q, gql)
        ckvn, vjp_kvn = jax.vjp(lambda a, b: _rms(a, b, KV_LORA), c_kv, gkvl)
        cqn16 = cqn.astype(BF16)
        ckvn16 = ckvn.astype(BF16)
        q = _dot_nt(cqn16, wq)
        kv = _mm_slots(ckvn16, wkv)
        _, vjp_q = jax.vjp(lambda n, r, a, b: _q_post(n, r, a, b, tc, ts1, ts2), *_q_chunks(q), gqn, gqr)
        dnope, drope, dgqn, dgqr = vjp_q(dqp.astype(F32))
        dq = jnp.concatenate(dnope + drope, axis=-1)
        _, vjp_kv = jax.vjp(lambda n, v, k, a, b: _kv_post(n, v, k, a, b, tc, ts1, ts2), *_kv_chunks(kv), krp, gkn,
                            gkr)
        dkn, dvals, dkrp, dgkn, dgkr = vjp_kv((dkp.astype(F32), dv.astype(F32)))
        dkv = jnp.concatenate([x for pair in zip(dkn, dvals) for x in pair], axis=-1)
        dq16 = dq.astype(BF16)
        dkv16 = dkv.astype(BF16)
        dc_q, dgql = vjp_qn(_dot(dq16, wq))
        dc_kv, dgkvl = vjp_kvn(_mm_slots_nt(dkv16, wkv))
        return dc_q, dc_kv, dkrp, cqn16, dq16, ckvn16, dkv16, dgql, dgkvl, dgqn, dgkn, dgqr, dgkr

    (dc_q, dc_kv, dkrp, cqn16, dq16, ckvn16, dkv16, dgql, dgkvl, dgqn, dgkn, dgqr, dgkr) = _rowwise(
        "mla_qkv_bwd", qkv_bwd,
        [('r', c_q), ('r', c_kv), ('r', krp), ('r', tc), ('r', ts1), ('r', ts2), ('r', dq_pad), ('r', dk_pad),
         ('r', dv_h)] + qkv_consts,
        [('r', (L, Q_LORA), BF16), ('r', (L, KV_LORA), BF16), ('r', (L, HD), BF16), ('r', (L, Q_LORA), BF16),
         ('t', (2 * PRIM, L), BF16), ('t', (KV_LORA, L), BF16), ('r', (L, 2 * PRIM), BF16),
         ('a', (1, Q_LORA), F32), ('a', (1, KV_LORA), F32), ('a', (1, HD), F32), ('a', (1, HD), F32),
         ('a', (1, HD), F32), ('a', (1, HD), F32)], nb_big)
    dW_q = _matmul_tn(dq16, cqn16, "dw_uq")
    dW_kv = _matmul_tn_slots(ckvn16, dkv16, "dw_ukv")

    def in_bwd(x, dres, g, w, *dparts):
        dproj = jnp.concatenate(dparts, axis=-1).astype(BF16)
        xn, vjp = jax.vjp(lambda a, b: _rms(a, b, D_MODEL), x, g)
        dx, dg = vjp(_mm_slots_nt(dproj, w) if w.ndim == 3 else _dot(dproj, w))
        return dx + dres, xn, dproj, dg

    dx1, xn1, dproj1, dln1 = _rowwise(
        "mla_in_bwd", in_bwd,
        [('r', x1), ('r', dx2), ('c', ln1), ('c', W_in_mla), ('r', dc_q), ('r', dc_kv), ('r', dxq_b), ('r', dgate_b),
         ('r', dkrp)],
        [('r', (L, D_MODEL), F32), ('r', (L, D_MODEL), BF16), ('t', (_MLA_IN_PAD, L), BF16), ('a', (1, D_MODEL), F32)],
        nblk)
    dW_in_mla = _matmul_tn(dproj1, xn1, "dw_mla_in")

    grads1 = [dW_out1.reshape(N_DEV, 256, D_MODEL), dW_mkv1.reshape(N_DEV, 128, 2 * XQ),
              _mla_in_rows_back(dW_in_mla).reshape(N_DEV, 424, D_MODEL),
              _uq_rows_back(dW_q).reshape(N_DEV, 288, Q_LORA), dW_kv]
    (dy2, dxq_a, dgate_a, o_a, g_a, dk_a, dv_a, dgq0), pair1 = _backward_merge(
        dx1, y2, 'r', xq_a, gate_a, k_a, v_a, gq0, W_out[0], "merge0_bwd", nb_big, host=_plan_pair(grads1))
    dgm0, dW_mkv0, dgk0 = _kv_prep_bwd(mem0, gm0, W_mkv[0], gk0, dk_a, dv_a, "kv_prep0_bwd")
    dW_out0 = _matmul_tn(o_a, g_a, "dw_out0")
    t1 = list(_pair_add(grads1, pair1, "rs_add_layer1"))

    def glu_bwd(y, dy2, w):
        h, vjp_h = jax.vjp(_gelu, y)
        h16 = h.astype(BF16)
        z = _mm_slots(h16, w)
        _, vjp_z = jax.vjp(lambda a, b: a * _sigmoid(b), z[:, :PRIM], z[:, PRIM:])
        dz16 = jnp.concatenate(vjp_z(dy2), axis=-1).astype(BF16)
        return vjp_h(_mm_slots_nt(dz16, w))[0], h16, dz16

    grads0 = [dW_out0.reshape(N_DEV, 256, D_MODEL), dW_mkv0.reshape(N_DEV, 128, 2 * XQ)]
    (dy_s5, h16, dz16), glu_hosted = _rowwise(
        "s5_glu_bwd", glu_bwd, [('r', y_s5), ('r', dy2), ('c', W_glu)],
        [('r', (L, PRIM), F32), ('t', (PRIM, L), BF16), ('r', (L, 2 * PRIM), BF16)], nb_big,
        host=_combine(_plan_chips(t1[2:]), _plan_pair(grads0)))
    recv_proj1, pair0 = glu_hosted[:3], glu_hosted[3:]
    dW_glu = _matmul_tn_slots(h16, dz16, "dw_glu")
    t0 = list(_pair_add(grads0 + [dW_glu], pair0 + list(_exchange_call(_plan_pair([dW_glu]), "rs_pair_glu")),
                        "rs_add_layer0"))
    (du_s5, dbc, dcc, dd, dar, dai), recv_rest = _s5_bwd(u_s5, dy_s5, s5_carry, bm, cm, a_r2, a_i2, s5_d,
                                                        cmask, rmat, host=_plan_chips(t1[:2] + t0))
    early_recv = recv_rest[:2] + recv_proj1 + recv_rest[2:]
    dbc4 = dbc.reshape(S5_G, S5_C, 2, S5_P)
    dcc4 = dcc.reshape(S5_G, S5_C, 2, S5_P)
    dlr, dli, dls, dbtr, dbti = _s5_params_bwd(
        lr3, li3, ls3, btr, bti, dar.reshape(S5_G, 1, S5_P), dai.reshape(S5_G, 1, S5_P), dbc4[:, :, 0], dbc4[:, :, 1])

    small_part = {
        "ln_gain": jnp.concatenate([jnp.zeros_like(dln1), dln1]), "mem_norm": jnp.concatenate([dgm0, dgm1]),
        "xq_norm": jnp.concatenate([dgq0, dgq1]), "xk_norm": jnp.concatenate([dgk0, dgk1]),
        "s5_lambda_re": dlr, "s5_lambda_im": dli, "s5_log_step": dls,
        "s5_b_re": jnp.swapaxes(dbtr, 1, 2), "s5_b_im": jnp.swapaxes(dbti, 1, 2),
        "s5_c_re": dcc4[:, :, 0], "s5_c_im": -dcc4[:, :, 1], "s5_d": dd,
        "mla_q_lora_norm": dgql, "mla_kv_lora_norm": dgkvl, "mla_q_nope_norm": dgqn, "mla_k_nope_norm": dgkn,
        "mla_q_rope_norm": dgqr[:, :ROPE], "mla_k_rope_norm": dgkr[:, :ROPE],
    }
    loss8 = jnp.pad(loss_part, ((0, 7), (0, 0)))
    (dx0, xn0, dproj0, dln0), (small_gath, loss_g) = _rowwise(
        "s5_in_bwd", in_bwd,
        [('r', x0), ('r', dx1), ('c', ln0), ('c', W_in_s5), ('r', du_s5), ('r', dxq_a),
         ('r', dgate_a)],
        [('r', (L, D_MODEL), F32), ('t', (D_MODEL, L), BF16), ('r', (L, 2 * BRANCH), BF16), ('a', (1, D_MODEL), F32)],
        nblk, host=_plan_all_gather([_pack_small(small_part).astype(BF16), loss8]))
    dW_in_s5 = _matmul_tn_slots(xn0, dproj0, "dw_s5_in")

    late = [dW_in_s5]
    late_t = _pair_add(late, list(_exchange_call(_plan_pair(late), "rs_pair_late")), "rs_add_late")
    owners = [("w_out", 1), ("w_mem_kv", 1), ("mla_w_in", 0), ("mla_w_uq", 0), ("mla_w_ukv", 0), ("w_out", 0),
              ("w_mem_kv", 0), ("s5_w_glu", 0)]
    flipped = ("mla_w_in", "mla_w_uq")

    def shard(d, n, i):
        return jnp.transpose(d[n][i]) if n in flipped else d[n][i]

    upd, (late_recv, ln0_gath) = _updates_call(
        early_recv, [shard(weights, n, i) for n, i in owners], [shard(m_in, n, i) for n, i in owners],
        [shard(v_in, n, i) for n, i in owners], "update_early",
        host=_combine(_plan_chips(late_t), _plan_all_gather([jnp.pad(dln0, ((0, 7), (0, 0)))])))
    owners.append(("s5_w_in", 0))
    upd.append(_sum_adamw(late_recv, s5_w_in[0], m_s5_w_in[0], v_s5_w_in[0], "update_s5_w_in"))
    grads, delta, new_m, new_v = {}, {}, {}, {}
    for n in _BIG:
        parts = [u for u, (o, _) in sorted(zip(upd, owners), key=lambda t: t[1][1]) if o == n]
        if n in flipped:
            grads[n], delta[n], new_m[n], new_v[n] = (jnp.transpose(parts[0][j])[None] for j in range(4))
        else:
            grads[n], delta[n], new_m[n], new_v[n] = (jnp.stack([p[j] for p in parts]) for j in range(4))

    gs, loss_sum = _small_sum(small_gath, loss_g, ln0_gath, "small_sum")
    loss = loss_sum[0, 0]
    for n, _ in _SMALL:
        shape = weights[n].shape
        if n == "mla_q_lora_norm":
            grads[n] = lax.dynamic_slice(_unpack_small(gs, n, (Q_LORA,)), (me * 64,), (64,)).reshape(shape)
        elif n == "mla_kv_lora_norm":
            grads[n] = lax.dynamic_slice(_unpack_small(gs, n, (KV_LORA,)), (me * 32,), (32,)).reshape(shape)
        else:
            grads[n] = _unpack_small(gs, n, shape)

    def own(n, a):
        if a.ndim == 4:
            a = jnp.transpose(a, (0, 2, 3, 1))
        elif a.ndim == 3:
            a = jnp.transpose(a, (0, 2, 1))
        return a.reshape(a.shape[1:]) if a.ndim >= 3 else a

    def back(n, a):
        shape = weights[n].shape
        if len(shape) == 4:
            return jnp.transpose(a.reshape((1,) + a.shape), (0, 3, 1, 2))
        if len(shape) == 3:
            return jnp.transpose(a.reshape((1,) + a.shape), (0, 2, 1))
        return a.reshape(shape)

    wide = ("s5_b_re", "s5_b_im", "s5_c_re", "s5_c_im")
    for names, nb, call in (([n for n, _ in _SMALL if n not in wide], 1, "update_small"), (wide, 4, "update_s5_bc")):
        res = _adamw_multi([own(n, weights[n]) for n in names], [own(n, grads[n]) for n in names],
                           [own(n, m_in[n]) for n in names], [own(n, v_in[n]) for n in names], call, nb)
        for n, (dl, m2, v2) in zip(names, res):
            delta[n], new_m[n], new_v[n] = back(n, dl), back(n, m2), back(n, v2)
    return (loss, dx0[None], *[grads[n] for n in _WEIGHTS], *[delta[n] for n in _WEIGHTS],
            *[new_m[n] for n in _WEIGHTS], *[new_v[n] for n in _WEIGHTS])
```

```python
import functools
import math

import numpy as np
import jax
import jax.numpy as jnp
from jax import lax
from jax.experimental import pallas as pl
from jax.experimental.pallas import tpu as pltpu

F32 = jnp.float32
BF16 = jnp.bfloat16
EPS = 1e-6
NEG = float(np.finfo(np.float32).min)
MESH = pl.DeviceIdType.MESH

N_DEV = 8
D_MODEL = 1024
MEM_LEN = 256
XQ = 512
PRIM = 1536
BRANCH = 2048
X_HEADS = 4
HD = 128
S5_G = 96
S5_P = 64
S5_C = 16
S5_GB = 8
S5_W = S5_GB * S5_P
MLA_H = 12
ROPE = 64
Q_LORA = 512
KV_LORA = 256
ROPE_THETA = 10000.0

ADAM_LR = 0.001
ADAM_B1 = 0.9
ADAM_B2 = 0.999
ADAM_EPS = 1e-08
ADAM_WD = 0.01
ADAM_STEP = 10

VMEM_LIMIT = 56 * 1024 * 1024


def _dot(a, b):
    return jnp.dot(a, b, preferred_element_type=F32)


def _dot_nt(a, b):
    return lax.dot_general(a, b, (((1,), (1,)), ((), ())), preferred_element_type=F32)


def _dot_tn(a, b):
    return lax.dot_general(a, b, (((0,), (0,)), ((), ())), preferred_element_type=F32)


@jax.custom_vjp
def _mm(a, b):
    return _dot(a.astype(BF16), b.astype(BF16))


def _mm_fwd(a, b):
    return _mm(a, b), (a, b)


def _mm_bwd(res, g):
    a, b = res
    gb = g.astype(BF16)
    return _dot_nt(gb, b.astype(BF16)).astype(a.dtype), _dot_tn(a.astype(BF16), gb).astype(b.dtype)


_mm.defvjp(_mm_fwd, _mm_bwd)


@jax.custom_vjp
def _mm_nt(a, b):
    return _dot_nt(a.astype(BF16), b.astype(BF16))


def _mm_nt_fwd(a, b):
    return _mm_nt(a, b), (a, b)


def _mm_nt_bwd(res, g):
    a, b = res
    gb = g.astype(BF16)
    return _dot(gb, b.astype(BF16)).astype(a.dtype), _dot_tn(gb, a.astype(BF16)).astype(b.dtype)


_mm_nt.defvjp(_mm_nt_fwd, _mm_nt_bwd)


@jax.custom_vjp
def _softmax(s):
    m = jnp.max(s, axis=-1, keepdims=True)
    e = jnp.exp(s - m)
    return e / jnp.sum(e, axis=-1, keepdims=True)


def _softmax_fwd(s):
    p = _softmax(s)
    return p, p


def _softmax_bwd(p, g):
    return (p * (g - jnp.sum(p * g, axis=-1, keepdims=True)),)


_softmax.defvjp(_softmax_fwd, _softmax_bwd)


def _rms(x, g, n):
    ms = jnp.sum(x * x, axis=-1, keepdims=True) * (1.0 / n)
    return x * lax.rsqrt(ms + EPS) * g


def _sigmoid(x):
    return 1.0 / (1.0 + jnp.exp(-x))


def _silu(x):
    return x * _sigmoid(x)


def _gelu(x):
    c = math.sqrt(2.0 / math.pi)
    return 0.5 * x * (1.0 + jnp.tanh(c * (x + 0.044715 * (x * x * x))))


@jax.custom_vjp
def _rot(x, c, s1, s2):
    return x * c + pltpu.roll(x, 96, 1) * s1 + pltpu.roll(x, 32, 1) * s2


def _rot_fwd(x, c, s1, s2):
    return _rot(x, c, s1, s2), (c, s1, s2)


def _rot_bwd(res, g):
    c, s1, s2 = res
    dx = g * c + pltpu.roll(g * s1, 32, 1) + pltpu.roll(g * s2, 96, 1)
    return dx, jnp.zeros_like(c), jnp.zeros_like(s1), jnp.zeros_like(s2)


_rot.defvjp(_rot_fwd, _rot_bwd)


def _mem_attn(xq, k, v, gq):
    outs = []
    for h in range(X_HEADS):
        sl = slice(HD * h, HD * (h + 1))
        q = _rms(xq[:, sl], gq, HD)
        p = _softmax(_mm_nt(q, k[:, sl]) * (HD ** -0.5))
        outs.append(_mm(p, v[:, sl]))
    return jnp.concatenate(outs, axis=-1)


def _merge(mix, xq, gate, k, v, gq):
    return jnp.concatenate([mix, _mem_attn(xq, k, v, gq)], axis=-1) * _silu(gate)


def _q_chunks(q):
    return ([q[:, HD * h:HD * (h + 1)] for h in range(MLA_H)],
            [q[:, PRIM + HD * h:PRIM + HD * (h + 1)] for h in range(MLA_H)])


def _q_post(nope, rope, gqn, gqr, c, s1, s2):
    pieces = []
    for qn, qr in zip(nope, rope):
        pieces.append(_rms(qn, gqn, HD))
        pieces.append(_rot(_rms(qr, gqr, ROPE), c, s1, s2))
    return jnp.concatenate(pieces, axis=-1)


def _kv_chunks(kv):
    return ([kv[:, 2 * HD * h:2 * HD * h + HD] for h in range(MLA_H)],
            [kv[:, 2 * HD * h + HD:2 * HD * (h + 1)] for h in range(MLA_H)])


def _kv_post(kn, vals, krp, gkn, gkr, c, s1, s2):
    kr = _rot(_rms(krp, gkr, ROPE), c, s1, s2)
    pieces = []
    for k in kn:
        pieces.append(_rms(k, gkn, HD))
        pieces.append(kr)
    return jnp.concatenate(pieces, axis=-1), jnp.concatenate(vals, axis=-1)


def _rowwise(name, fn, ins, outs, nblk, host=None):
    n_in = len(ins)

    def spec(kind, shape):
        if kind == 'r':
            return pl.BlockSpec((shape[0] // nblk, shape[1]), lambda i: (i, 0))
        if kind == 't':
            return pl.BlockSpec((shape[0], shape[1] // nblk), lambda i: (0, i))
        zeros = (0,) * len(shape)
        return pl.BlockSpec(tuple(shape), lambda i: zeros)

    def body(*refs):
        i = pl.program_id(0)
        res = fn(*[r[...] for r in refs[:n_in]])
        for (kind, _, _), ref, val in zip(outs, refs[n_in:], res):
            if kind == 'a':
                @pl.when(i == 0)
                def _():
                    ref[...] = jnp.zeros_like(ref)
                ref[...] += val.astype(ref.dtype)
            elif kind == 't':
                ref[...] = val.astype(F32).T.astype(ref.dtype)
            else:
                ref[...] = val.astype(ref.dtype)

    res, hosted = _hosting_call(
        body, name, nblk, host, [a for _, a in ins], [spec(k, a.shape) for k, a in ins],
        [jax.ShapeDtypeStruct(tuple(s), d) for _, s, d in outs], [spec(k, s) for k, s, _ in outs], [])
    return res if host is None else (res, hosted)


def _matmul_tn(at, g, name, out_dtype=BF16):
    K, L = at.shape
    N = g.shape[1]
    tn = next(t for t in (512, 384, 256, 128) if N % t == 0)

    def body(a_ref, g_ref, o_ref):
        o_ref[...] = _dot(a_ref[...], g_ref[...]).astype(o_ref.dtype)

    return pl.pallas_call(
        body, name=name, grid=(N // tn,),
        in_specs=[pl.BlockSpec((K, L), lambda n: (0, 0)), pl.BlockSpec((L, tn), lambda n: (0, n))],
        out_specs=pl.BlockSpec((K, tn), lambda n: (0, n)),
        out_shape=jax.ShapeDtypeStruct((K, N), out_dtype),
        compiler_params=pltpu.CompilerParams(dimension_semantics=("arbitrary",), vmem_limit_bytes=VMEM_LIMIT),
    )(at, g)


def _matmul_tn_slots(at, g, name, host=None):
    K, L = at.shape
    n = g.shape[1] // N_DEV

    def body(a_ref, g_ref, o_ref):
        o_ref[...] = _dot(a_ref[...], g_ref[...]).astype(o_ref.dtype)

    res, hosted = _hosting_call(
        body, name, N_DEV, host, [at, g],
        [pl.BlockSpec((K, L), lambda d: (0, 0)), pl.BlockSpec((L, n), lambda d: (0, d))],
        [jax.ShapeDtypeStruct((N_DEV, K, n), BF16)], [pl.BlockSpec((None, K, n), lambda d: (d, 0, 0))], [])
    return res[0] if host is None else (res[0], hosted)


def _mm_slots(a16, w):
    return jnp.concatenate([_dot(a16, w[d]) for d in range(N_DEV)], axis=-1)


def _mm_slots_nt(g16, w):
    n = w.shape[2]
    out = _dot_nt(g16[:, 0:n], w[0])
    for d in range(1, N_DEV):
        out = out + _dot_nt(g16[:, d * n:(d + 1) * n], w[d])
    return out


class _Exchange:
    def __init__(self, ins, outs, scratch, start, finish):
        self.ins, self.outs, self.scratch, self.start, self.finish = ins, outs, scratch, start, finish


def _xyc():
    return lax.axis_index("x"), lax.axis_index("y"), lax.axis_index("c")


def _plan_all_gather(xs):
    n = len(xs)

    def build(x_refs, out_refs, sems):
        send_sems, recv_sems, local_sems = sems
        x, y, c = _xyc()

        def copies(k, block, to, own=False):
            slot = 4 * block[0] + 2 * block[1] + block[2]
            return [pltpu.make_async_remote_copy(
                src_ref=x_refs[a] if own else out_refs[a].at[slot], dst_ref=out_refs[a].at[slot],
                send_sem=send_sems.at[k * n + a], recv_sem=recv_sems.at[k * n + a], device_id=to,
                device_id_type=MESH) for a in range(n)]

        mine = [pltpu.make_async_copy(x_refs[a], out_refs[a].at[4 * x + 2 * y + c], local_sems.at[a])
                for a in range(n)]
        return copies, mine, (x, y, c), [(1 - x, y), (x, 1 - y), (1 - x, 1 - y)]

    def first_copies(copies, me, chips):
        x, y, c = me
        first = copies(0, me, (x, y, 1 - c), own=True)
        for j, chip in enumerate(chips):
            first += copies(1 + j, me, (*chip, c), own=True)
        return first

    def start(x_refs, out_refs, sems):
        copies, mine, me, chips = build(x_refs, out_refs, sems)
        for cp in mine + first_copies(copies, me, chips):
            cp.start()

    def finish(x_refs, out_refs, sems):
        copies, mine, me, chips = build(x_refs, out_refs, sems)
        x, y, c = me
        passed = []
        for j, chip in enumerate(chips):
            for cp in copies(1 + j, (*chip, c), me):
                cp.wait_recv()
            fwd = copies(4 + j, (*chip, c), (x, y, 1 - c))
            for cp in fwd:
                cp.start()
            passed += fwd
        for cp in copies(0, (x, y, 1 - c), me):
            cp.wait_recv()
        for j, chip in enumerate(chips):
            for cp in copies(4 + j, (*chip, 1 - c), me):
                cp.wait_recv()
        for cp in first_copies(copies, me, chips) + passed:
            cp.wait_send()
        for cp in mine:
            cp.wait()

    return _Exchange(list(xs), [jax.ShapeDtypeStruct((N_DEV,) + a.shape, a.dtype) for a in xs],
                     [pltpu.SemaphoreType.DMA((7 * n,)), pltpu.SemaphoreType.DMA((7 * n,)),
                      pltpu.SemaphoreType.DMA((n,))], start, finish)


_CHIPS = ((0, 0), (0, 1), (1, 0), (1, 1))


def _plan_pair(sends):
    n = len(sends)

    def build(s_refs, o_refs, sems):
        send_sems, recv_sems = sems
        x, y, c = _xyc()
        return [pltpu.make_async_remote_copy(
            src_ref=s_refs[a].at[4 * px + 2 * py + 1 - c], dst_ref=o_refs[a].at[j],
            send_sem=send_sems.at[j * n + a], recv_sem=recv_sems.at[j * n + a], device_id=(x, y, 1 - c),
            device_id_type=MESH) for j, (px, py) in enumerate(_CHIPS) for a in range(n)]

    def start(s_refs, o_refs, sems):
        for cp in build(s_refs, o_refs, sems):
            cp.start()

    def finish(s_refs, o_refs, sems):
        for cp in build(s_refs, o_refs, sems):
            cp.wait_recv()
            cp.wait_send()

    return _Exchange(list(sends), [jax.ShapeDtypeStruct((4,) + a.shape[1:], a.dtype) for a in sends],
                     [pltpu.SemaphoreType.DMA((4 * n,)), pltpu.SemaphoreType.DMA((4 * n,))], start, finish)


def _plan_chips(ts):
    n = len(ts)
    flips = ((1, 0), (0, 1), (1, 1))

    def build(t_refs, o_refs, sems):
        send_sems, recv_sems, local_sems = sems
        x, y, c = _xyc()
        mine = 2 * x + y
        local = [pltpu.make_async_copy(t_refs[a].at[mine], o_refs[a].at[mine], local_sems.at[a]) for a in range(n)]
        remote = []
        for k, (fx, fy) in enumerate(flips):
            px = 1 - x if fx else x
            py = 1 - y if fy else y
            remote += [pltpu.make_async_remote_copy(
                src_ref=t_refs[a].at[2 * px + py], dst_ref=o_refs[a].at[mine],
                send_sem=send_sems.at[k * n + a], recv_sem=recv_sems.at[k * n + a], device_id=(px, py, c),
                device_id_type=MESH) for a in range(n)]
        return local, remote

    def start(t_refs, o_refs, sems):
        local, remote = build(t_refs, o_refs, sems)
        for cp in local + remote:
            cp.start()

    def finish(t_refs, o_refs, sems):
        local, remote = build(t_refs, o_refs, sems)
        for cp in remote:
            cp.wait_recv()
        for cp in remote:
            cp.wait_send()
        for cp in local:
            cp.wait()

    return _Exchange(list(ts), [jax.ShapeDtypeStruct(a.shape, a.dtype) for a in ts],
                     [pltpu.SemaphoreType.DMA((3 * n,)), pltpu.SemaphoreType.DMA((3 * n,)),
                      pltpu.SemaphoreType.DMA((n,))], start, finish)


def _combine(*plans):
    def parts(refs, attr):
        out, at = [], 0
        for p in plans:
            n = len(getattr(p, attr))
            out.append(refs[at:at + n])
            at += n
        return out

    def run(half):
        def go(ins, outs, sems):
            for p, a, o, s in zip(plans, parts(ins, "ins"), parts(outs, "outs"), parts(sems, "scratch")):
                getattr(p, half)(a, o, s)
        return go

    return _Exchange(sum((p.ins for p in plans), []), sum((p.outs for p in plans), []),
                     sum((p.scratch for p in plans), []), run("start"), run("finish"))


def _exchange_call(plan, name):
    n = len(plan.ins)

    def body(*refs):
        ins, outs, sems = refs[:n], refs[n:2 * n], refs[2 * n:]
        plan.start(ins, outs, sems)
        plan.finish(ins, outs, sems)

    return pl.pallas_call(
        body, name=name, out_shape=plan.outs,
        in_specs=[pl.BlockSpec(memory_space=pl.ANY)] * n, out_specs=[pl.BlockSpec(memory_space=pl.ANY)] * n,
        scratch_shapes=plan.scratch,
    )(*plan.ins)


def _slab_spec(lead, rows, cols, nb):
    if rows % (nb * 16) == 0:
        return pl.BlockSpec((lead, rows // nb, cols), lambda i: (0, i, 0))
    if cols % (nb * 128) == 0:
        return pl.BlockSpec((lead, rows, cols // nb), lambda i: (0, 0, i))
    return pl.BlockSpec((lead, rows, cols), lambda i: (0, 0, 0))


def _slab_spec2(rows, cols, nb):
    if rows % (nb * 16) == 0:
        return pl.BlockSpec((rows // nb, cols), lambda i: (i, 0))
    if cols % (nb * 128) == 0:
        return pl.BlockSpec((rows, cols // nb), lambda i: (0, i))
    return pl.BlockSpec((rows, cols), lambda i: (0, 0))


def _cast_call(arrays, name, host=None):
    n = len(arrays)
    nb = 8

    def body(*refs):
        for a in range(n):
            refs[n + a][...] = refs[a][...].astype(BF16)

    specs = [_slab_spec2(x.shape[0], x.shape[1], nb) for x in arrays]
    return _hosting_call(body, name, nb, host, list(arrays), specs,
                         [jax.ShapeDtypeStruct(x.shape, BF16) for x in arrays], specs, [])


def _pair_add(sends, fromsib, name):
    n = len(sends)
    nb = 8

    def body(*refs):
        c = lax.axis_index("c")
        for a in range(n):
            s_ref, f_ref, t_ref = refs[a], refs[n + a], refs[2 * n + a]
            for j in range(4):
                t_ref[j] = (s_ref[2 * j + c].astype(F32) + f_ref[j].astype(F32)).astype(t_ref.dtype)

    def spec(a, lead):
        return _slab_spec(lead, a.shape[1], a.shape[2], nb)

    return pl.pallas_call(
        body, name=name, grid=(nb,),
        in_specs=[spec(a, N_DEV) for a in sends] + [spec(a, 4) for a in fromsib],
        out_specs=[spec(a, 4) for a in fromsib],
        out_shape=[jax.ShapeDtypeStruct(a.shape, a.dtype) for a in fromsib],
        compiler_params=pltpu.CompilerParams(dimension_semantics=("arbitrary",), vmem_limit_bytes=VMEM_LIMIT),
    )(*sends, *fromsib)


def _adamw_vals(w, g, m, v):
    m2 = ADAM_B1 * m + (1.0 - ADAM_B1) * g
    v2 = ADAM_B2 * v + (1.0 - ADAM_B2) * (g * g)
    m_hat = m2 / (1.0 - ADAM_B1 ** ADAM_STEP)
    v_hat = v2 / (1.0 - ADAM_B2 ** ADAM_STEP)
    delta = -ADAM_LR * (m_hat / (jnp.sqrt(v_hat) + ADAM_EPS) + ADAM_WD * w)
    return delta, m2, v2


def _sum_adamw(recv, w, m, v, name):
    R, C = w.shape
    ns = recv.shape[0]
    br = next((t for t in (256, 128, 64, 32, 16) if R % t == 0), R)

    def body(r_ref, w_ref, m_ref, v_ref, g_ref, d_ref, m2_ref, v2_ref):
        g = r_ref[0].astype(F32)
        for d in range(1, ns):
            g = g + r_ref[d].astype(F32)
        dl, m2, v2 = _adamw_vals(w_ref[...], g, m_ref[...], v_ref[...])
        g_ref[...] = g
        d_ref[...] = dl
        m2_ref[...] = m2
        v2_ref[...] = v2

    spec = pl.BlockSpec((br, C), lambda i: (i, 0))
    return pl.pallas_call(
        body, name=name, grid=(R // br,),
        in_specs=[pl.BlockSpec((ns, br, C), lambda i: (0, i, 0)), spec, spec, spec], out_specs=[spec] * 4,
        out_shape=[jax.ShapeDtypeStruct((R, C), F32)] * 4,
        compiler_params=pltpu.CompilerParams(dimension_semantics=("arbitrary",)),
    )(recv, w, m, v)


def _updates_call(recvs, ws, ms, vs, name, host=None):
    n = len(recvs)
    nb = 8

    def body(*refs):
        for a in range(n):
            r_ref, w_ref, m_ref, v_ref = refs[a], refs[n + a], refs[2 * n + a], refs[3 * n + a]
            g_ref, d_ref, m2_ref, v2_ref = refs[4 * n + 4 * a:4 * n + 4 * a + 4]
            g = r_ref[0].astype(F32)
            for d in range(1, r_ref.shape[0]):
                g = g + r_ref[d].astype(F32)
            dl, m2, v2 = _adamw_vals(w_ref[...], g, m_ref[...], v_ref[...])
            g_ref[...] = g
            d_ref[...] = dl
            m2_ref[...] = m2
            v2_ref[...] = v2

    def spec3(r):
        return _slab_spec(r.shape[0], r.shape[1], r.shape[2], nb)

    def spec2(w):
        return _slab_spec2(w.shape[0], w.shape[1], nb)

    res, hosted = _hosting_call(
        body, name, nb, host, list(recvs) + list(ws) + list(ms) + list(vs),
        [spec3(r) for r in recvs] + [spec2(w) for w in ws] * 3,
        [jax.ShapeDtypeStruct(w.shape, F32) for w in ws for _ in range(4)],
        [spec2(w) for w in ws for _ in range(4)], [])
    return [res[4 * a:4 * a + 4] for a in range(n)], hosted


def _small_sum(gath, loss_g, row0_g, name):
    _, R, C = gath.shape
    br = R // 3

    def body(g_ref, l_ref, r_ref, go_ref, lo_ref):
        g = g_ref[0].astype(F32)
        lsum = l_ref[0]
        for d in range(1, N_DEV):
            g = g + g_ref[d].astype(F32)
            lsum = lsum + l_ref[d]
        go_ref[...] = g
        lo_ref[...] = lsum

        @pl.when(pl.program_id(0) == 0)
        def _():
            row0 = r_ref[0]
            for d in range(1, N_DEV):
                row0 = row0 + r_ref[d]
            go_ref[0:8, :] = go_ref[0:8, :] + jnp.where(lax.broadcasted_iota(jnp.int32, row0.shape, 0) == 0, row0, 0.0)

    return pl.pallas_call(
        body, name=name, grid=(R // br,),
        in_specs=[pl.BlockSpec((N_DEV, br, C), lambda i: (0, i, 0)),
                  pl.BlockSpec((N_DEV, 8, HD), lambda i: (0, 0, 0)), pl.BlockSpec((N_DEV, 8, C), lambda i: (0, 0, 0))],
        out_specs=[pl.BlockSpec((br, C), lambda i: (i, 0)), pl.BlockSpec((8, HD), lambda i: (0, 0))],
        out_shape=[jax.ShapeDtypeStruct((R, C), F32), jax.ShapeDtypeStruct((8, HD), F32)],
        compiler_params=pltpu.CompilerParams(dimension_semantics=("arbitrary",)),
    )(gath, loss_g, row0_g)


def _adamw_multi(ws, gs, ms, vs, name, nblk=1):
    n = len(ws)

    def body(*refs):
        for a in range(n):
            dl, m2, v2 = _adamw_vals(refs[a][...], refs[n + a][...], refs[2 * n + a][...], refs[3 * n + a][...])
            refs[4 * n + 3 * a][...] = dl
            refs[4 * n + 3 * a + 1][...] = m2
            refs[4 * n + 3 * a + 2][...] = v2

    def spec(x):
        rest = (0,) * (x.ndim - 1)
        return pl.BlockSpec((x.shape[0] // nblk,) + tuple(x.shape[1:]), lambda i: (i,) + rest)

    res = pl.pallas_call(
        body, name=name, grid=(nblk,),
        in_specs=[spec(w) for w in ws] * 4, out_specs=[spec(w) for w in ws for _ in range(3)],
        out_shape=[jax.ShapeDtypeStruct(w.shape, F32) for w in ws for _ in range(3)],
        compiler_params=pltpu.CompilerParams(dimension_semantics=("arbitrary",), vmem_limit_bytes=VMEM_LIMIT),
    )(*ws, *gs, *ms, *vs)
    return [res[3 * a:3 * a + 3] for a in range(n)]


def _s5_param_fn(lr, li, ls, btr, bti):
    step = jnp.exp(ls)
    er = jnp.exp(lr * step)
    ang = li * step
    ar = er * jnp.cos(ang)
    ai = er * jnp.sin(ang)
    nr = ar - 1.0
    den = lr * lr + li * li
    fr = (nr * lr + ai * li) / den
    fi = (ai * lr - nr * li) / den
    return ar, ai, fr * btr - fi * bti, fr * bti + fi * btr


def _s5_params(lr, li, ls, btr, bti, cre, cim):
    nb = S5_G // S5_GB
    GC = S5_GB * S5_C
    expand = jnp.asarray(np.tile(np.eye(S5_P, dtype=np.float32), (1, S5_GB)), BF16)
    own = jnp.asarray((np.arange(GC)[:, None] // S5_C == np.arange(S5_W)[None, :] // S5_P).astype(np.float32))

    def body(lr_ref, li_ref, ls_ref, br_ref, bi_ref, cr_ref, ci_ref, e_ref, own_ref, ar_ref, ai_ref, bm_ref, cm_ref):
        ar, ai, bbr, bbi = _s5_param_fn(lr_ref[...], li_ref[...], ls_ref[...], br_ref[...], bi_ref[...])
        ar_ref[...] = ar
        ai_ref[...] = ai

        def plane(x, n):
            rows = x[n * S5_GB:(n + 1) * S5_GB].reshape(GC, S5_P).astype(BF16)
            return _dot(rows, e_ref[...]) * own_ref[...]

        for n in range(nb):
            bm_ref[n] = jnp.concatenate([plane(bbr, n), plane(bbi, n)], axis=-1).astype(BF16)
            cm_ref[n] = jnp.concatenate([plane(cr_ref[...], n), -plane(ci_ref[...], n)], axis=-1).astype(BF16)

    sd = jax.ShapeDtypeStruct
    return pl.pallas_call(
        body, name="s5_params",
        out_shape=[sd(lr.shape, F32), sd(lr.shape, F32), sd((nb, GC, 2 * S5_W), BF16), sd((nb, GC, 2 * S5_W), BF16)],
        compiler_params=pltpu.CompilerParams(vmem_limit_bytes=VMEM_LIMIT),
    )(lr, li, ls, btr, bti, cre, cim, expand, own)


def _s5_params_bwd(lr, li, ls, btr, bti, dar, dai, dbbr, dbbi):
    def body(lr_ref, li_ref, ls_ref, br_ref, bi_ref, dar_ref, dai_ref, dbbr_ref, dbbi_ref,
             dlr_ref, dli_ref, dls_ref, dbr_ref, dbi_ref):
        _, vjp = jax.vjp(_s5_param_fn, lr_ref[...], li_ref[...], ls_ref[...], br_ref[...], bi_ref[...])
        dlr, dli, dls, dbr, dbi = vjp((dar_ref[...], dai_ref[...], dbbr_ref[...], dbbi_ref[...]))
        dlr_ref[...] = dlr
        dli_ref[...] = dli
        dls_ref[...] = dls
        dbr_ref[...] = dbr
        dbi_ref[...] = dbi

    sd = jax.ShapeDtypeStruct
    return pl.pallas_call(
        body, name="s5_params_bwd",
        out_shape=[sd(lr.shape, F32), sd(lr.shape, F32), sd(ls.shape, F32), sd(btr.shape, F32), sd(btr.shape, F32)],
    )(lr, li, ls, btr, bti, dar, dai, dbbr, dbbi)


def _cpow(ar, ai, n):
    assert n & (n - 1) == 0
    while n > 1:
        ar, ai = ar * ar - ai * ai, 2.0 * ar * ai
        n //= 2
    return ar, ai


def _scan(st, cr, ci, init, nk, reverse, store, prev=None):
    W = S5_W

    def advance(k, sr, si):
        rows = pl.ds(k * 8 if isinstance(k, int) else pl.multiple_of(k * 8, 8), 8)
        nsr = cr * sr - ci * si + st[rows, 0:W]
        nsi = cr * si + ci * sr + st[rows, W:2 * W]
        if store:
            st[rows, 0:W] = nsr
            st[rows, W:2 * W] = nsi
        return nsr, nsi

    if prev is None:
        return lax.fori_loop(0, nk, lambda j, c: advance(nk - 1 - j if reverse else j, c[0], c[1]), init, unroll=2)
    assert reverse

    def step(j, carry):
        k = nk - 1 - j
        nsr, nsi = advance(k, carry[0], carry[1])
        prows = pl.ds(pl.multiple_of((k - 1) * 8, 8), 8)
        pr = prev[prows, 0:W]
        pi = prev[prows, W:2 * W]
        return nsr, nsi, carry[2] + nsr * pr + nsi * pi, carry[3] + nsi * pr - nsr * pi

    carry = lax.fori_loop(0, nk - 1, step, init, unroll=2)
    nsr, nsi = advance(0, carry[0], carry[1])
    return nsr, nsi, carry[2], carry[3]


def _chain(fin, fr, fi, pr, pi, reverse):
    W = S5_W
    fin[:, 0:W] = fr
    fin[:, W:2 * W] = fi
    rowid = lax.broadcasted_iota(jnp.int32, (8, W), 0)
    cr = jnp.zeros((1, W), F32)
    ci = jnp.zeros((1, W), F32)
    init_r = jnp.zeros((8, W), F32)
    init_i = jnp.zeros((8, W), F32)
    for s in (range(7, -1, -1) if reverse else range(8)):
        init_r = jnp.where(rowid == s, cr, init_r)
        init_i = jnp.where(rowid == s, ci, init_i)
        lr = fin[s:s + 1, 0:W]
        li = fin[s:s + 1, W:2 * W]
        cr, ci = lr + pr * cr - pi * ci, li + pr * ci + pi * cr
    return init_r, init_i


def _full_scan(st, fin, ar, ai, nk, reverse, prev=None, carry_in=None, carry_out=None):
    W = S5_W
    cr = jnp.broadcast_to(ar, (8, W))
    ci = jnp.broadcast_to(-ai if reverse else ai, (8, W))
    z = jnp.zeros((8, W), F32)
    if carry_in is None:
        fr, fi = _scan(st, cr, ci, (z, z), nk, reverse, store=False)
        pr, pi = _cpow(ar, -ai if reverse else ai, nk)
        init = _chain(fin, fr, fi, pr, pi, reverse)
    else:
        init = (carry_in[:, 0:W], carry_in[:, W:2 * W])
    if carry_out is not None:
        carry_out[:, 0:W] = init[0]
        carry_out[:, W:2 * W] = init[1]
    if prev is None:
        return _scan(st, cr, ci, init, nk, reverse, store=True)
    return _scan(st, cr, ci, init + (z, z), nk, reverse, store=True, prev=prev)


def _s5_specs(L):
    W2 = 2 * S5_W
    GC = S5_GB * S5_C
    col = pl.BlockSpec((L, GC), lambda g: (0, g))
    vec = pl.BlockSpec((1, GC), lambda g: (0, g))
    avec = pl.BlockSpec((1, S5_W), lambda g: (0, g))
    bmat = pl.BlockSpec((None, GC, W2), lambda g: (g, 0, 0))
    cmat = pl.BlockSpec((None, W2, GC), lambda g: (g, 0, 0))
    return col, vec, avec, bmat, cmat


def _interleave(dst, src, nk):
    for s in range(8):
        dst[pl.ds(s, nk, stride=8), :] = src[s * nk:(s + 1) * nk, :]


def _deinterleave(dst, src, nk):
    for s in range(8):
        dst[s * nk:(s + 1) * nk, :] = src[pl.ds(s, nk, stride=8), :].astype(dst.dtype)


def _hosting_call(body, name, nsteps, host, ins, in_specs, outs, out_specs, scratch):
    grid = (nsteps,) if isinstance(nsteps, int) else tuple(nsteps)
    params = pltpu.CompilerParams(dimension_semantics=("arbitrary",) * len(grid), vmem_limit_bytes=VMEM_LIMIT)
    if host is None:
        res = pl.pallas_call(
            body, name=name, grid=grid, in_specs=in_specs, out_specs=out_specs, out_shape=outs,
            scratch_shapes=scratch, compiler_params=params,
        )(*ins)
        return list(res), []
    n_in, n_out, n_sc = len(ins), len(outs), len(scratch)
    h_in, h_out = len(host.ins), len(host.outs)

    def hosted(*refs):
        a = refs[:n_in]
        ha = refs[n_in:n_in + h_in]
        o = refs[n_in + h_in:n_in + h_in + n_out]
        ho = refs[n_in + h_in + n_out:n_in + h_in + n_out + h_out]
        sc = refs[n_in + h_in + n_out + h_out:n_in + h_in + n_out + h_out + n_sc]
        hs = refs[n_in + h_in + n_out + h_out + n_sc:]
        first = functools.reduce(jnp.logical_and, [pl.program_id(i) == 0 for i in range(len(grid))])
        last = functools.reduce(jnp.logical_and, [pl.program_id(i) == g - 1 for i, g in enumerate(grid)])

        @pl.when(first)
        def _():
            host.start(ha, ho, hs)

        body(*a, *o, *sc)

        @pl.when(last)
        def _():
            host.finish(ha, ho, hs)

    hbm = pl.BlockSpec(memory_space=pl.ANY)
    res = pl.pallas_call(
        hosted, name=name, grid=grid,
        in_specs=list(in_specs) + [hbm] * h_in, out_specs=list(out_specs) + [hbm] * h_out,
        out_shape=list(outs) + list(host.outs), scratch_shapes=list(scratch) + list(host.scratch),
        compiler_params=params,
    )(*ins, *host.ins)
    return list(res[:n_out]), list(res[n_out:])


def _s5_fwd(u, bm, cm, ar, ai, dvec, host=None):
    L = u.shape[0]
    nk = L // 8
    GC = S5_GB * S5_C
    nb = S5_G // S5_GB
    col, vec, avec, bmat, cmat = _s5_specs(L)

    def body(u_ref, b_ref, c_ref, ar_ref, ai_ref, d_ref, y_ref, carry_ref, st, fin, ui, yi):
        _interleave(ui, u_ref, nk)
        for r in range(8):
            rows = slice(r * nk, (r + 1) * nk)
            st[rows, :] = _dot(ui[rows, :].astype(BF16), b_ref[...])
        _full_scan(st, fin, ar_ref[...], ai_ref[...], nk, reverse=False, carry_out=carry_ref)
        for r in range(8):
            rows = slice(r * nk, (r + 1) * nk)
            yi[rows, :] = _dot_nt(st[rows, :].astype(BF16), c_ref[...]) + d_ref[...] * ui[rows, :]
        _deinterleave(y_ref, yi, nk)

    return _hosting_call(
        body, "s5_fwd", nb, host,
        [u, bm, cm, ar, ai, dvec], [col, bmat, bmat, avec, avec, vec],
        [jax.ShapeDtypeStruct(u.shape, F32), jax.ShapeDtypeStruct((nb * 8, 2 * S5_W), F32)],
        [col, pl.BlockSpec((8, 2 * S5_W), lambda g: (g, 0))],
        [pltpu.VMEM((L, 2 * S5_W), F32), pltpu.VMEM((8, 2 * S5_W), F32), pltpu.VMEM((L, GC), F32),
         pltpu.VMEM((L, GC), F32)])


def _s5_bwd(u, dy, carry, bm, cm, ar, ai, dvec, mask, rmat, host=None):
    L = u.shape[0]
    nk = L // 8
    W = S5_W
    GC = S5_GB * S5_C
    col, vec, avec, bmat, cmat = _s5_specs(L)
    hi = lax.Precision.HIGHEST

    def body(u_ref, dy_ref, carry_ref, b_ref, ct_ref, ar_ref, ai_ref, d_ref, mask_ref, r_ref,
             du_ref, db_ref, dc_ref, dd_ref, dar_ref, dai_ref, sa, sb, fin, ui, dyi, dui):
        ar = ar_ref[...]
        ai = ai_ref[...]
        _interleave(ui, u_ref, nk)
        _interleave(dyi, dy_ref, nk)
        for r in range(8):
            rows = slice(r * nk, (r + 1) * nk)
            sa[rows, :] = _dot(ui[rows, :].astype(BF16), b_ref[...])
            sb[rows, :] = _dot(dyi[rows, :].astype(BF16), ct_ref[...])
        _full_scan(sa, fin, ar, ai, nk, reverse=False, carry_in=carry_ref)
        gr, gi, accr, acci = _full_scan(sb, fin, ar, ai, nk, reverse=True, prev=sa)
        rowid = lax.broadcasted_iota(jnp.int32, (8, W), 0)
        last = pl.ds((nk - 1) * 8, 8)
        pr = jnp.where(rowid == 0, 0.0, pltpu.roll(sa[last, 0:W], 1, 0))
        pi = jnp.where(rowid == 0, 0.0, pltpu.roll(sa[last, W:2 * W], 1, 0))
        accr = accr + gr * pr + gi * pi
        acci = acci + gi * pr - gr * pi
        dar_ref[...] = jnp.sum(accr, axis=0, keepdims=True)
        dai_ref[...] = jnp.sum(acci, axis=0, keepdims=True)
        dbf = jnp.zeros((GC, 2 * W), F32)
        dcf = jnp.zeros((GC, 2 * W), F32)
        dd = jnp.zeros((1, GC), F32)
        for r in range(8):
            rows = slice(r * nk, (r + 1) * nk)
            ub = ui[rows, :]
            dyb = dyi[rows, :]
            gb = sb[rows, :].astype(BF16)
            dui[rows, :] = _dot_nt(gb, b_ref[...]) + d_ref[...] * dyb
            dbf = dbf + _dot_tn(ub.astype(BF16), gb)
            dcf = dcf + _dot_tn(dyb.astype(BF16), sa[rows, :].astype(BF16))
            dd = dd + jnp.sum(dyb * ub, axis=0, keepdims=True)
        db_ref[...] = jnp.dot(dbf * mask_ref[...], r_ref[...], precision=hi, preferred_element_type=F32)
        dc_ref[...] = jnp.dot(dcf * mask_ref[...], r_ref[...], precision=hi, preferred_element_type=F32)
        dd_ref[...] = dd
        _deinterleave(du_ref, dui, nk)

    cmp_spec = pl.BlockSpec((GC, 2 * S5_P), lambda g: (g, 0))
    whole = lambda shape: pl.BlockSpec(shape, lambda g: (0, 0))
    sd = jax.ShapeDtypeStruct
    return _hosting_call(
        body, "s5_bwd", S5_G // S5_GB, host,
        [u, dy, carry, bm, cm, ar, ai, dvec, mask, rmat],
        [col, col, pl.BlockSpec((8, 2 * W), lambda g: (g, 0)), bmat, bmat, avec, avec, vec, whole(mask.shape),
         whole(rmat.shape)],
        [sd(u.shape, BF16), sd((S5_G * S5_C, 2 * S5_P), F32), sd((S5_G * S5_C, 2 * S5_P), F32),
         sd((1, PRIM), F32), sd((1, S5_G * S5_P), F32), sd((1, S5_G * S5_P), F32)],
        [col, cmp_spec, cmp_spec, vec, avec, avec],
        [pltpu.VMEM((L, 2 * W), F32), pltpu.VMEM((L, 2 * W), F32), pltpu.VMEM((8, 2 * W), F32),
         pltpu.VMEM((L, GC), F32), pltpu.VMEM((L, GC), F32), pltpu.VMEM((L, GC), F32)])


def _s5_compact_consts():
    g_row = np.arange(S5_GB * S5_C) // S5_C
    col = np.arange(2 * S5_W)
    g_col = (col % S5_W) // S5_P
    mask = (g_row[:, None] == g_col[None, :]).astype(np.float32)
    tgt = (col // S5_W) * S5_P + col % S5_P
    rmat = (tgt[:, None] == np.arange(2 * S5_P)[None, :]).astype(np.float32)
    return jnp.asarray(mask), jnp.asarray(rmat)


def _attn_scores(q_ref, k_ref, qb, bq, scale):
    ext = (qb + 1) * bq
    s = _dot_nt(q_ref[qb * bq:ext, :], k_ref[0:ext, :]) * scale
    qpos = lax.broadcasted_iota(jnp.int32, (bq, bq), 0)
    kpos = lax.broadcasted_iota(jnp.int32, (bq, bq), 1)
    diag = jnp.where(kpos <= qpos, s[:, ext - bq:], NEG)
    return diag if qb == 0 else jnp.concatenate([s[:, :ext - bq], diag], axis=-1)


def _attn_fwd(qp, kp, v, scale):
    L = qp.shape[0]
    bq = min(256, L)

    def body(q_ref, k_ref, v_ref, o_ref, lse_ref):
        for qb in range(L // bq):
            rows = slice(qb * bq, (qb + 1) * bq)
            s = _attn_scores(q_ref, k_ref, qb, bq, scale)
            m = jnp.max(s, axis=-1, keepdims=True)
            e = jnp.exp(s - m)
            l = jnp.sum(e, axis=-1, keepdims=True)
            o_ref[rows, :] = _dot(e.astype(BF16), v_ref[0:(qb + 1) * bq, :]) / l
            lse_ref[rows, :] = jnp.broadcast_to(m + jnp.log(l), (bq, HD))

    blk = pl.BlockSpec((L, HD), lambda h: (0, h))
    wide = pl.BlockSpec((L, 2 * HD), lambda h: (0, h))
    return pl.pallas_call(
        body, name="mla_attn_fwd", grid=(MLA_H,),
        in_specs=[wide, wide, blk], out_specs=[blk, blk],
        out_shape=[jax.ShapeDtypeStruct((L, MLA_H * HD), F32)] * 2,
        compiler_params=pltpu.CompilerParams(dimension_semantics=("arbitrary",), vmem_limit_bytes=VMEM_LIMIT),
    )(qp, kp, v)


def _attn_bwd(qp, kp, v, o, lse, do, scale):
    L = qp.shape[0]
    bq = min(256, L)
    nq = L // bq

    def body(q_ref, k_ref, v_ref, o_ref, lse_ref, do_ref, dq_ref, dk_ref, dv_ref, dk_acc, dv_acc):
        dk_acc[...] = jnp.zeros_like(dk_acc)
        dv_acc[...] = jnp.zeros_like(dv_acc)
        for qb in range(nq):
            rows = slice(qb * bq, (qb + 1) * bq)
            ext = (qb + 1) * bq
            do = do_ref[rows, :]
            dob = do.astype(BF16)
            p = jnp.exp(_attn_scores(q_ref, k_ref, qb, bq, scale) - lse_ref[rows, 0:1])
            dp = _dot_nt(dob, v_ref[0:ext, :])
            dsum = jnp.sum(do * o_ref[rows, :], axis=-1, keepdims=True)
            ds = (p * (dp - dsum) * scale).astype(BF16)
            dq_ref[rows, :] = _dot(ds, k_ref[0:ext, :]).astype(dq_ref.dtype)
            dk_acc[0:ext, :] += _dot_tn(ds, q_ref[rows, :])
            dv_acc[0:ext, :] += _dot_tn(p.astype(BF16), dob)
        dk_ref[...] = dk_acc[...].astype(dk_ref.dtype)
        dv_ref[...] = dv_acc[...].astype(dv_ref.dtype)

    sd = jax.ShapeDtypeStruct
    blk = pl.BlockSpec((L, HD), lambda h: (0, h))
    wide = pl.BlockSpec((L, 2 * HD), lambda h: (0, h))
    return pl.pallas_call(
        body, name="mla_attn_bwd", grid=(MLA_H,),
        in_specs=[wide, wide, blk, blk, blk, blk], out_specs=[wide, wide, blk],
        out_shape=[sd((L, MLA_H * 2 * HD), BF16), sd((L, MLA_H * 2 * HD), BF16), sd((L, MLA_H * HD), BF16)],
        scratch_shapes=[pltpu.VMEM((L, 2 * HD), F32), pltpu.VMEM((L, HD), F32)],
        compiler_params=pltpu.CompilerParams(dimension_semantics=("arbitrary",), vmem_limit_bytes=VMEM_LIMIT),
    )(qp, kp, v, o, lse, do)


def _kv_fn(mem, gm, w, gk):
    kv = _mm(_rms(mem, gm, D_MODEL), w)
    k = jnp.concatenate([_rms(kv[:, HD * h:HD * (h + 1)], gk, HD) for h in range(X_HEADS)], axis=-1)
    return k, kv[:, XQ:]


def _kv_prep(mem, gm, w, gk, name):
    def fn(mem, gm, w, gk):
        return _kv_fn(mem, gm, w, gk)
    M = mem.shape[0]
    return _rowwise(name, fn, [('c', mem), ('c', gm), ('c', w), ('c', gk)],
                    [('c', (M, XQ), F32), ('c', (M, XQ), F32)], 1)


def _kv_prep_bwd(mem, gm, w, gk, dk, dv, name):
    def fn(mem, gm, w, gk, dk, dv):
        _, vjp = jax.vjp(lambda a, b, c: _kv_fn(mem, a, b, c), gm, w, gk)
        return vjp((dk, dv))
    return _rowwise(name, fn, [('c', mem), ('c', gm), ('c', w), ('c', gk), ('c', dk), ('c', dv)],
                    [('c', gm.shape, F32), ('c', w.shape, BF16), ('c', gk.shape, F32)], 1)


def _forward_merge(x, mix, mix_kind, xq, gate, k, v, gq, wout, name, nblk, host=None):
    def fn(x, mix, xq, gate, k, v, gq, wout):
        o = _merge(mix, xq, gate, k, v, gq)
        return (x + _dot(o.astype(BF16), wout),)
    L = x.shape[0]
    out = _rowwise(name, fn, [('r', x), (mix_kind, mix), ('r', xq), ('r', gate), ('c', k), ('c', v), ('c', gq),
                              ('c', wout)], [('r', (L, D_MODEL), F32)], nblk, host=host)
    return out[0] if host is None else (out[0][0], out[1])


def _backward_merge(dx, mix, mix_kind, xq, gate, k, v, gq, wout, name, nblk, host=None):
    def fn(dx, mix, xq, gate, k, v, gq, wout):
        g16 = dx.astype(BF16)
        do = _dot_nt(g16, wout)
        o, vjp = jax.vjp(_merge, mix, xq, gate, k, v, gq)
        dmix, dxq, dgate, dk, dv, dgq = vjp(do)
        return dmix, dxq, dgate, o, g16, dk, dv, dgq
    L = dx.shape[0]
    return _rowwise(
        name, fn,
        [('r', dx), (mix_kind, mix), ('r', xq), ('r', gate), ('c', k), ('c', v), ('c', gq), ('c', wout)],
        [('r', (L, PRIM), F32), ('r', (L, XQ), BF16), ('r', (L, BRANCH), BF16), ('t', (BRANCH, L), BF16),
         ('r', (L, D_MODEL), BF16), ('a', k.shape, F32), ('a', v.shape, F32), ('a', gq.shape, F32)], nblk,
        host=host)


_MLA_IN = 3392
_MLA_IN_PAD = 3456


def _uq_rows(wt):
    r = wt.reshape(MLA_H, HD + ROPE, wt.shape[1])
    return jnp.concatenate([r[:, :HD].reshape(PRIM, -1),
                            jnp.pad(r[:, HD:], ((0, 0), (0, HD - ROPE), (0, 0))).reshape(PRIM, -1)], axis=0)


def _uq_rows_back(wt):
    nope = wt[:PRIM].reshape(MLA_H, HD, -1)
    rope = wt[PRIM:].reshape(MLA_H, HD, -1)[:, :ROPE]
    return jnp.concatenate([nope, rope], axis=1).reshape(MLA_H * (HD + ROPE), -1)


def _mla_in_rows(wt):
    return jnp.concatenate([wt[:768], wt[832:], wt[768:832], jnp.zeros((64, wt.shape[1]), wt.dtype)], axis=0)


def _mla_in_rows_back(wt):
    return jnp.concatenate([wt[:768], wt[3328:3392], wt[768:3328]], axis=0)


_SMALL = (("ln_gain", 2048), ("mem_norm", 2048), ("xq_norm", 256), ("xk_norm", 256), ("s5_lambda_re", 6144),
          ("s5_lambda_im", 6144), ("s5_log_step", 96), ("s5_b_re", 98304), ("s5_b_im", 98304), ("s5_c_re", 98304),
          ("s5_c_im", 98304), ("s5_d", 1536), ("mla_q_lora_norm", 512), ("mla_kv_lora_norm", 256),
          ("mla_q_nope_norm", 128), ("mla_k_nope_norm", 128), ("mla_q_rope_norm", 64), ("mla_k_rope_norm", 64))
_SMALL_ROWS = 432
_SMALL_OFF = {name: sum(n for _, n in _SMALL[:i]) for i, (name, _) in enumerate(_SMALL)}


def _pack_small(d):
    flat = jnp.concatenate([d[n].reshape(-1).astype(F32) for n, _ in _SMALL])
    return jnp.pad(flat, (0, _SMALL_ROWS * 1024 - flat.shape[0])).reshape(_SMALL_ROWS, 1024)


def _unpack_small(p, name, shape):
    off = _SMALL_OFF[name]
    return p.reshape(-1)[off:off + int(np.prod(shape))].reshape(shape)


_WEIGHTS = ('ln_gain', 'w_out', 'mem_norm', 'w_mem_kv', 'xq_norm', 'xk_norm', 's5_w_in', 's5_lambda_re',
            's5_lambda_im', 's5_log_step', 's5_b_re', 's5_b_im', 's5_c_re', 's5_c_im', 's5_d', 's5_w_glu', 'mla_w_in',
            'mla_q_lora_norm', 'mla_kv_lora_norm', 'mla_w_uq', 'mla_w_ukv', 'mla_q_nope_norm', 'mla_k_nope_norm',
            'mla_q_rope_norm', 'mla_k_rope_norm')
_BIG = ('w_out', 'w_mem_kv', 's5_w_in', 's5_w_glu', 'mla_w_in', 'mla_w_uq', 'mla_w_ukv')


def _pad128(g):
    return jnp.pad(g.reshape(1, -1), ((0, 0), (0, HD - g.shape[-1])))


def kernel(x, mem, positions, ln_gain, w_out, mem_norm, w_mem_kv, xq_norm, xk_norm, s5_w_in, s5_lambda_re, s5_lambda_im, s5_log_step, s5_b_re, s5_b_im, s5_c_re, s5_c_im, s5_d, s5_w_glu, mla_w_in, mla_q_lora_norm, mla_kv_lora_norm, mla_w_uq, mla_w_ukv, mla_q_nope_norm, mla_k_nope_norm, mla_q_rope_norm, mla_k_rope_norm, loss_target, m_ln_gain, m_w_out, m_mem_norm, m_w_mem_kv, m_xq_norm, m_xk_norm, m_s5_w_in, m_s5_lambda_re, m_s5_lambda_im, m_s5_log_step, m_s5_b_re, m_s5_b_im, m_s5_c_re, m_s5_c_im, m_s5_d, m_s5_w_glu, m_mla_w_in, m_mla_q_lora_norm, m_mla_kv_lora_norm, m_mla_w_uq, m_mla_w_ukv, m_mla_q_nope_norm, m_mla_k_nope_norm, m_mla_q_rope_norm, m_mla_k_rope_norm, v_ln_gain, v_w_out, v_mem_norm, v_w_mem_kv, v_xq_norm, v_xk_norm, v_s5_w_in, v_s5_lambda_re, v_s5_lambda_im, v_s5_log_step, v_s5_b_re, v_s5_b_im, v_s5_c_re, v_s5_c_im, v_s5_d, v_s5_w_glu, v_mla_w_in, v_mla_q_lora_norm, v_mla_kv_lora_norm, v_mla_w_uq, v_mla_w_ukv, v_mla_q_nope_norm, v_mla_k_nope_norm, v_mla_q_rope_norm, v_mla_k_rope_norm):
    weights = dict(ln_gain=ln_gain, w_out=w_out, mem_norm=mem_norm, w_mem_kv=w_mem_kv, xq_norm=xq_norm,
                   xk_norm=xk_norm, s5_w_in=s5_w_in, s5_lambda_re=s5_lambda_re, s5_lambda_im=s5_lambda_im,
                   s5_log_step=s5_log_step, s5_b_re=s5_b_re, s5_b_im=s5_b_im, s5_c_re=s5_c_re, s5_c_im=s5_c_im,
                   s5_d=s5_d, s5_w_glu=s5_w_glu, mla_w_in=mla_w_in, mla_q_lora_norm=mla_q_lora_norm,
                   mla_kv_lora_norm=mla_kv_lora_norm, mla_w_uq=mla_w_uq, mla_w_ukv=mla_w_ukv,
                   mla_q_nope_norm=mla_q_nope_norm, mla_k_nope_norm=mla_k_nope_norm,
                   mla_q_rope_norm=mla_q_rope_norm, mla_k_rope_norm=mla_k_rope_norm)
    m_in = dict(zip(_WEIGHTS, (m_ln_gain, m_w_out, m_mem_norm, m_w_mem_kv, m_xq_norm, m_xk_norm, m_s5_w_in,
                               m_s5_lambda_re, m_s5_lambda_im, m_s5_log_step, m_s5_b_re, m_s5_b_im, m_s5_c_re,
                               m_s5_c_im, m_s5_d, m_s5_w_glu, m_mla_w_in, m_mla_q_lora_norm, m_mla_kv_lora_norm,
                               m_mla_w_uq, m_mla_w_ukv, m_mla_q_nope_norm, m_mla_k_nope_norm, m_mla_q_rope_norm,
                               m_mla_k_rope_norm)))
    v_in = dict(zip(_WEIGHTS, (v_ln_gain, v_w_out, v_mem_norm, v_w_mem_kv, v_xq_norm, v_xk_norm, v_s5_w_in,
                               v_s5_lambda_re, v_s5_lambda_im, v_s5_log_step, v_s5_b_re, v_s5_b_im, v_s5_c_re,
                               v_s5_c_im, v_s5_d, v_s5_w_glu, v_mla_w_in, v_mla_q_lora_norm, v_mla_kv_lora_norm,
                               v_mla_w_uq, v_mla_w_ukv, v_mla_q_nope_norm, v_mla_k_nope_norm, v_mla_q_rope_norm,
                               v_mla_k_rope_norm)))

    x0 = x[0]
    mem0 = mem[0]
    target = loss_target[0]
    L = x0.shape[0]
    nblk = 4
    nb_big = 8
    me = 4 * lax.axis_index("x") + 2 * lax.axis_index("y") + lax.axis_index("c")

    lora = jnp.pad(jnp.concatenate([mla_q_lora_norm, mla_kv_lora_norm], axis=1), ((0, 7), (0, HD - 96)))
    def gather(*shards):
        return _plan_all_gather(list(shards))

    kh = D_MODEL // 2
    (b_mkv0, b_glu, b_in_mla, b_out0, b_uq, b_ukv, b_mkv1, b_out1), (W_in_s5,) = _cast_call(
        [w_mem_kv[0], s5_w_glu[0], jnp.transpose(mla_w_in[0]), w_out[0], jnp.transpose(mla_w_uq[0]), mla_w_ukv[0],
         w_mem_kv[1], w_out[1]], "cast_shards", host=gather(s5_w_in[0].astype(BF16)))

    ln0, ln1 = ln_gain[0:1], ln_gain[1:2]
    gq0, gq1 = xq_norm[0:1], xq_norm[1:2]
    gk0, gk1 = xk_norm[0:1], xk_norm[1:2]
    gm0, gm1 = mem_norm[0:1], mem_norm[1:2]
    gqn, gkn = mla_q_nope_norm, mla_k_nope_norm
    gqr, gkr = _pad128(mla_q_rope_norm), _pad128(mla_k_rope_norm)

    lr3 = s5_lambda_re.reshape(S5_G, 1, S5_P)
    li3 = s5_lambda_im.reshape(S5_G, 1, S5_P)
    ls3 = s5_log_step.reshape(S5_G, 1, 1)
    btr = jnp.swapaxes(s5_b_re[0], 1, 2)
    bti = jnp.swapaxes(s5_b_im[0], 1, 2)
    a_r, a_i, bm, cm = _s5_params(lr3, li3, ls3, btr, bti, s5_c_re[0], s5_c_im[0])
    a_r2 = a_r.reshape(1, S5_G * S5_P)
    a_i2 = a_i.reshape(1, S5_G * S5_P)
    cmask, rmat = _s5_compact_consts()

    half = ROPE // 2
    inv_freq = ROPE_THETA ** (-jnp.arange(half, dtype=F32) / half)
    invf = jnp.concatenate([inv_freq, inv_freq, jnp.zeros((HD - ROPE,), F32)]).reshape(1, HD)

    def rot_tables(pos, invf):
        ang = pos.astype(F32) * invf
        lane = lax.broadcasted_iota(jnp.int32, ang.shape, 1)
        c = jnp.where(lane < ROPE, jnp.cos(ang), 0.0)
        s = jnp.sin(ang)
        return c, jnp.where(lane < half, -s, 0.0), jnp.where((lane >= half) & (lane < ROPE), s, 0.0)

    tc, ts1, ts2 = _rowwise("rot_tables", rot_tables, [('r', positions.reshape(L, 1)), ('c', invf)],
                            [('r', (L, HD), F32)] * 3, nblk)

    def in_s5(x, g, w):
        proj = _mm_slots(_rms(x, g, D_MODEL).astype(BF16), w)
        return proj[:, :PRIM], proj[:, PRIM:PRIM + XQ], proj[:, PRIM + XQ:]

    u_s5, xq_a, gate_a = _rowwise(
        "s5_in", in_s5, [('r', x0), ('c', ln0), ('c', W_in_s5)],
        [('r', (L, PRIM), F32), ('r', (L, XQ), F32), ('r', (L, BRANCH), F32)], nblk)
    (y_s5, s5_carry), (W_glu, G_mkv0, G_in_mla_a) = _s5_fwd(u_s5, bm, cm, a_r2, a_i2, s5_d,
                                                            host=gather(b_glu, b_mkv0, b_in_mla[:, :kh]))

    def glu(y, w):
        z = _mm_slots(_gelu(y).astype(BF16), w)
        return z[:, :PRIM] * _sigmoid(z[:, PRIM:]), z

    (y2, z_glu), (G_out0,) = _rowwise("s5_glu", glu, [('r', y_s5), ('c', W_glu)],
                                      [('r', (L, PRIM), F32), ('r', (L, 2 * PRIM), F32)], nblk, host=gather(b_out0))
    W_mkv0 = G_mkv0.reshape(D_MODEL, 2 * XQ)
    k_a, v_a = _kv_prep(mem0, gm0, W_mkv0, gk0, "kv_prep0")
    x1, (G_in_mla_b,) = _forward_merge(
        x0, y2, 'r', xq_a, gate_a, k_a, v_a, gq0, G_out0.reshape(BRANCH, D_MODEL), "merge0", nblk,
        host=gather(b_in_mla[:, kh:]))
    W_in_mla = _mla_in_rows(jnp.concatenate([G_in_mla_a, G_in_mla_b], axis=2).reshape(_MLA_IN, D_MODEL))

    def in_mla(x, g, w):
        proj = _dot_nt(_rms(x, g, D_MODEL).astype(BF16), w)
        return proj[:, :512], proj[:, 512:768], proj[:, 768:1280], proj[:, 1280:3328], proj[:, 3328:]

    (c_q, c_kv, xq_b, gate_b, krp), (G_uq, W_kv, G_lora) = _rowwise(
        "mla_in", in_mla, [('r', x1), ('c', ln1), ('c', W_in_mla)],
        [('r', (L, Q_LORA), F32), ('r', (L, KV_LORA), F32), ('r', (L, XQ), F32), ('r', (L, BRANCH), F32),
         ('r', (L, HD), F32)], nblk,
        host=gather(b_uq, b_ukv, lora))
    W_q = _uq_rows(G_uq.reshape(MLA_H * (HD + ROPE), Q_LORA))
    g_qlora = G_lora[:, 0, :64].reshape(1, Q_LORA)
    g_kvlora = G_lora[:, 0, 64:96].reshape(1, KV_LORA)

    def qkv(c_q, c_kv, krp, tc, ts1, ts2, gql, gkvl, wq, wkv, gqn, gkn, gqr, gkr):
        q = _dot_nt(_rms(c_q, gql, Q_LORA).astype(BF16), wq)
        kv = _mm_slots(_rms(c_kv, gkvl, KV_LORA).astype(BF16), wkv)
        kp, v = _kv_post(*_kv_chunks(kv), krp, gkn, gkr, tc, ts1, ts2)
        return _q_post(*_q_chunks(q), gqn, gqr, tc, ts1, ts2), kp, v

    qkv_consts = [('c', g_qlora), ('c', g_kvlora), ('c', W_q), ('c', W_kv), ('c', gqn), ('c', gkn), ('c', gqr),
                  ('c', gkr)]
    (q_pad, k_pad, v_h), (G_mkv1, G_out1) = _rowwise(
        "mla_qkv", qkv, [('r', c_q), ('r', c_kv), ('r', krp), ('r', tc), ('r', ts1), ('r', ts2)] + qkv_consts,
        [('r', (L, 2 * PRIM), BF16), ('r', (L, 2 * PRIM), BF16), ('r', (L, PRIM), BF16)], nblk,
        host=gather(b_mkv1, b_out1))
    W_out = (G_out0.reshape(BRANCH, D_MODEL), G_out1.reshape(BRANCH, D_MODEL))
    W_mkv = (W_mkv0, G_mkv1.reshape(D_MODEL, 2 * XQ))
    scale = (HD + ROPE) ** -0.5
    attn, lse = _attn_fwd(q_pad, k_pad, v_h, scale)
    k_b, v_b = _kv_prep(mem0, gm1, W_mkv[1], gk1, "kv_prep1")

    def merge_loss(x, mix, xq, gate, k, v, gq, wout, t):
        err = x + _dot(_merge(mix, xq, gate, k, v, gq).astype(BF16), wout) - t
        part = 0.5 * jnp.sum(jnp.sum(err * err, axis=-1, keepdims=True) * (1.0 / D_MODEL), axis=0, keepdims=True)
        return err * (1.0 / D_MODEL), jnp.broadcast_to(part, (1, HD))

    dx2, loss_part = _rowwise(
        "merge1_loss", merge_loss,
        [('r', x1), ('r', attn), ('r', xq_b), ('r', gate_b), ('c', k_b), ('c', v_b), ('c', gq1), ('c', W_out[1]),
         ('r', target)], [('r', (L, D_MODEL), F32), ('a', (1, HD), F32)], nblk)

    dattn, dxq_b, dgate_b, o_b, g_b, dk_b, dv_b, dgq1 = _backward_merge(
        dx2, attn, 'r', xq_b, gate_b, k_b, v_b, gq1, W_out[1], "merge1_bwd", nb_big)
    dgm1, dW_mkv1, dgk1 = _kv_prep_bwd(mem0, gm1, W_mkv[1], gk1, dk_b, dv_b, "kv_prep1_bwd")
    dW_out1 = _matmul_tn(o_b, g_b, "dw_out1")
    dq_pad, dk_pad, dv_h = _attn_bwd(q_pad, k_pad, v_h, attn, lse, dattn, scale)

    def qkv_bwd(c_q, c_kv, krp, tc, ts1, ts2, dqp, dkp, dv, gql, gkvl, wq, wkv, gqn, gkn, gqr, gkr):
        cqn, vjp_qn = jax.vjp(lambda a, b: _rms(a, b, Q_LORA), c_q, gql)
        ckvn, vjp_kvn = jax.vjp(lambda a, b: _rms(a, b, KV_LORA), c_kv, gkvl)
        cqn16 = cqn.astype(BF16)
        ckvn16 = ckvn.astype(BF16)
        q = _dot_nt(cqn16, wq)
        kv = _mm_slots(ckvn16, wkv)
        _, vjp_q = jax.vjp(lambda n, r, a, b: _q_post(n, r, a, b, tc, ts1, ts2), *_q_chunks(q), gqn, gqr)
        dnope, drope, dgqn, dgqr = vjp_q(dqp.astype(F32))
        dq = jnp.concatenate(dnope + drope, axis=-1)
        _, vjp_kv = jax.vjp(lambda n, v, k, a, b: _kv_post(n, v, k, a, b, tc, ts1, ts2), *_kv_chunks(kv), krp, gkn,
                            gkr)
        dkn, dvals, dkrp, dgkn, dgkr = vjp_kv((dkp.astype(F32), dv.astype(F32)))
        dkv = jnp.concatenate([x for pair in zip(dkn, dvals) for x in pair], axis=-1)
        dq16 = dq.astype(BF16)
        dkv16 = dkv.astype(BF16)
        dc_q, dgql = vjp_qn(_dot(dq16, wq))
        dc_kv, dgkvl = vjp_kvn(_mm_slots_nt(dkv16, wkv))
        return dc_q, dc_kv, dkrp, cqn16, dq16, ckvn16, dkv16, dgql, dgkvl, dgqn, dgkn, dgqr, dgkr

    (dc_q, dc_kv, dkrp, cqn16, dq16, ckvn16, dkv16, dgql, dgkvl, dgqn, dgkn, dgqr, dgkr) = _rowwise(
        "mla_qkv_bwd", qkv_bwd,
        [('r', c_q), ('r', c_kv), ('r', krp), ('r', tc), ('r', ts1), ('r', ts2), ('r', dq_pad), ('r', dk_pad),
         ('r', dv_h)] + qkv_consts,
        [('r', (L, Q_LORA), BF16), ('r', (L, KV_LORA), BF16), ('r', (L, HD), BF16), ('r', (L, Q_LORA), BF16),
         ('t', (2 * PRIM, L), BF16), ('t', (KV_LORA, L), BF16), ('r', (L, 2 * PRIM), BF16),
         ('a', (1, Q_LORA), F32), ('a', (1, KV_LORA), F32), ('a', (1, HD), F32), ('a', (1, HD), F32),
         ('a', (1, HD), F32), ('a', (1, HD), F32)], nb_big)
    dW_q = _matmul_tn(dq16, cqn16, "dw_uq")
    dW_kv = _matmul_tn_slots(ckvn16, dkv16, "dw_ukv")

    def in_bwd(x, dres, g, w, *dparts):
        dproj = jnp.concatenate(dparts, axis=-1).astype(BF16)
        xn, vjp = jax.vjp(lambda a, b: _rms(a, b, D_MODEL), x, g)
        dx, dg = vjp(_mm_slots_nt(dproj, w) if w.ndim == 3 else _dot(dproj, w))
        return dx + dres, xn, dproj, dg

    dx1, xn1, dproj1, dln1 = _rowwise(
        "mla_in_bwd", in_bwd,
        [('r', x1), ('r', dx2), ('c', ln1), ('c', W_in_mla), ('r', dc_q), ('r', dc_kv), ('r', dxq_b), ('r', dgate_b),
         ('r', dkrp)],
        [('r', (L, D_MODEL), F32), ('r', (L, D_MODEL), BF16), ('t', (_MLA_IN_PAD, L), BF16), ('a', (1, D_MODEL), F32)],
        nblk)
    dW_in_mla = _matmul_tn(dproj1, xn1, "dw_mla_in")

    grads1 = [dW_out1.reshape(N_DEV, 256, D_MODEL), dW_mkv1.reshape(N_DEV, 128, 2 * XQ),
              _mla_in_rows_back(dW_in_mla).reshape(N_DEV, 424, D_MODEL),
              _uq_rows_back(dW_q).reshape(N_DEV, 288, Q_LORA), dW_kv]
    (dy2, dxq_a, dgate_a, o_a, g_a, dk_a, dv_a, dgq0), pair1 = _backward_merge(
        dx1, y2, 'r', xq_a, gate_a, k_a, v_a, gq0, W_out[0], "merge0_bwd", nb_big, host=_plan_pair(grads1))
    dgm0, dW_mkv0, dgk0 = _kv_prep_bwd(mem0, gm0, W_mkv[0], gk0, dk_a, dv_a, "kv_prep0_bwd")
    dW_out0 = _matmul_tn(o_a, g_a, "dw_out0")
    t1 = list(_pair_add(grads1, pair1, "rs_add_layer1"))

    def glu_bwd(y, z, dy2, w):
        h, vjp_h = jax.vjp(_gelu, y)
        _, vjp_z = jax.vjp(lambda a, b: a * _sigmoid(b), z[:, :PRIM], z[:, PRIM:])
        dz16 = jnp.concatenate(vjp_z(dy2), axis=-1).astype(BF16)
        return vjp_h(_mm_slots_nt(dz16, w))[0], h.astype(BF16), dz16

    grads0 = [dW_out0.reshape(N_DEV, 256, D_MODEL), dW_mkv0.reshape(N_DEV, 128, 2 * XQ)]
    (dy_s5, h16, dz16), glu_hosted = _rowwise(
        "s5_glu_bwd", glu_bwd, [('r', y_s5), ('r', z_glu), ('r', dy2), ('c', W_glu)],
        [('r', (L, PRIM), F32), ('t', (PRIM, L), BF16), ('r', (L, 2 * PRIM), BF16)], nb_big,
        host=_combine(_plan_chips(t1[2:]), _plan_pair(grads0)))
    recv_proj1, pair0 = glu_hosted[:3], glu_hosted[3:]
    dW_glu = _matmul_tn_slots(h16, dz16, "dw_glu")
    t0 = list(_pair_add(grads0 + [dW_glu], pair0 + list(_exchange_call(_plan_pair([dW_glu]), "rs_pair_glu")),
                        "rs_add_layer0"))
    (du_s5, dbc, dcc, dd, dar, dai), recv_rest = _s5_bwd(u_s5, dy_s5, s5_carry, bm, cm, a_r2, a_i2, s5_d,
                                                        cmask, rmat, host=_plan_chips(t1[:2] + t0))
    early_recv = recv_rest[:2] + recv_proj1 + recv_rest[2:]
    dbc4 = dbc.reshape(S5_G, S5_C, 2, S5_P)
    dcc4 = dcc.reshape(S5_G, S5_C, 2, S5_P)
    dlr, dli, dls, dbtr, dbti = _s5_params_bwd(
        lr3, li3, ls3, btr, bti, dar.reshape(S5_G, 1, S5_P), dai.reshape(S5_G, 1, S5_P), dbc4[:, :, 0], dbc4[:, :, 1])

    small_part = {
        "ln_gain": jnp.concatenate([jnp.zeros_like(dln1), dln1]), "mem_norm": jnp.concatenate([dgm0, dgm1]),
        "xq_norm": jnp.concatenate([dgq0, dgq1]), "xk_norm": jnp.concatenate([dgk0, dgk1]),
        "s5_lambda_re": dlr, "s5_lambda_im": dli, "s5_log_step": dls,
        "s5_b_re": jnp.swapaxes(dbtr, 1, 2), "s5_b_im": jnp.swapaxes(dbti, 1, 2),
        "s5_c_re": dcc4[:, :, 0], "s5_c_im": -dcc4[:, :, 1], "s5_d": dd,
        "mla_q_lora_norm": dgql, "mla_kv_lora_norm": dgkvl, "mla_q_nope_norm": dgqn, "mla_k_nope_norm": dgkn,
        "mla_q_rope_norm": dgqr[:, :ROPE], "mla_k_rope_norm": dgkr[:, :ROPE],
    }
    loss8 = jnp.pad(loss_part, ((0, 7), (0, 0)))
    (dx0, xn0, dproj0, dln0), (small_gath, loss_g) = _rowwise(
        "s5_in_bwd", in_bwd,
        [('r', x0), ('r', dx1), ('c', ln0), ('c', W_in_s5), ('r', du_s5), ('r', dxq_a),
         ('r', dgate_a)],
        [('r', (L, D_MODEL), F32), ('t', (D_MODEL, L), BF16), ('r', (L, 2 * BRANCH), BF16), ('a', (1, D_MODEL), F32)],
        nblk, host=_plan_all_gather([_pack_small(small_part).astype(BF16), loss8]))
    dW_in_s5 = _matmul_tn_slots(xn0, dproj0, "dw_s5_in")

    late = [dW_in_s5]
    late_t = _pair_add(late, list(_exchange_call(_plan_pair(late), "rs_pair_late")), "rs_add_late")
    owners = [("w_out", 1), ("w_mem_kv", 1), ("mla_w_in", 0), ("mla_w_uq", 0), ("mla_w_ukv", 0), ("w_out", 0),
              ("w_mem_kv", 0), ("s5_w_glu", 0)]
    flipped = ("mla_w_in", "mla_w_uq")

    def shard(d, n, i):
        return jnp.transpose(d[n][i]) if n in flipped else d[n][i]

    upd, (late_recv, ln0_gath) = _updates_call(
        early_recv, [shard(weights, n, i) for n, i in owners], [shard(m_in, n, i) for n, i in owners],
        [shard(v_in, n, i) for n, i in owners], "update_early",
        host=_combine(_plan_chips(late_t), _plan_all_gather([jnp.pad(dln0, ((0, 7), (0, 0)))])))
    owners.append(("s5_w_in", 0))
    upd.append(_sum_adamw(late_recv, s5_w_in[0], m_s5_w_in[0], v_s5_w_in[0], "update_s5_w_in"))
    grads, delta, new_m, new_v = {}, {}, {}, {}
    for n in _BIG:
        parts = [u for u, (o, _) in sorted(zip(upd, owners), key=lambda t: t[1][1]) if o == n]
        if n in flipped:
            grads[n], delta[n], new_m[n], new_v[n] = (jnp.transpose(parts[0][j])[None] for j in range(4))
        else:
            grads[n], delta[n], new_m[n], new_v[n] = (jnp.stack([p[j] for p in parts]) for j in range(4))

    gs, loss_sum = _small_sum(small_gath, loss_g, ln0_gath, "small_sum")
    loss = loss_sum[0, 0]
    for n, _ in _SMALL:
        shape = weights[n].shape
        if n == "mla_q_lora_norm":
            grads[n] = lax.dynamic_slice(_unpack_small(gs, n, (Q_LORA,)), (me * 64,), (64,)).reshape(shape)
        elif n == "mla_kv_lora_norm":
            grads[n] = lax.dynamic_slice(_unpack_small(gs, n, (KV_LORA,)), (me * 32,), (32,)).reshape(shape)
        else:
            grads[n] = _unpack_small(gs, n, shape)

    def own(n, a):
        if a.ndim == 4:
            a = jnp.transpose(a, (0, 2, 3, 1))
        elif a.ndim == 3:
            a = jnp.transpose(a, (0, 2, 1))
        return a.reshape(a.shape[1:]) if a.ndim >= 3 else a

    def back(n, a):
        shape = weights[n].shape
        if len(shape) == 4:
            return jnp.transpose(a.reshape((1,) + a.shape), (0, 3, 1, 2))
        if len(shape) == 3:
            return jnp.transpose(a.reshape((1,) + a.shape), (0, 2, 1))
        return a.reshape(shape)

    wide = ("s5_b_re", "s5_b_im", "s5_c_re", "s5_c_im")
    for names, nb, call in (([n for n, _ in _SMALL if n not in wide], 1, "update_small"), (wide, 4, "update_s5_bc")):
        res = _adamw_multi([own(n, weights[n]) for n in names], [own(n, grads[n]) for n in names],
                           [own(n, m_in[n]) for n in names], [own(n, v_in[n]) for n in names], call, nb)
        for n, (dl, m2, v2) in zip(names, res):
            delta[n], new_m[n], new_v[n] = back(n, dl), back(n, m2), back(n, v2)
    return (loss, dx0[None], *[grads[n] for n in _WEIGHTS], *[delta[n] for n in _WEIGHTS],
            *[new_m[n] for n in _WEIGHTS], *[new_v[n] for n in _WEIGHTS])
```

```python
import functools
import math

import numpy as np
import jax
import jax.numpy as jnp
from jax import lax
from jax.experimental import pallas as pl
from jax.experimental.pallas import tpu as pltpu

F32 = jnp.float32
BF16 = jnp.bfloat16
EPS = 1e-6
NEG = float(np.finfo(np.float32).min)
MESH = pl.DeviceIdType.MESH

N_DEV = 8
D_MODEL = 1024
MEM_LEN = 256
XQ = 512
PRIM = 1536
BRANCH = 2048
X_HEADS = 4
HD = 128
S5_G = 96
S5_P = 64
S5_C = 16
S5_GB = 8
S5_W = S5_GB * S5_P
MLA_H = 12
ROPE = 64
Q_LORA = 512
KV_LORA = 256
ROPE_THETA = 10000.0

ADAM_LR = 0.001
ADAM_B1 = 0.9
ADAM_B2 = 0.999
ADAM_EPS = 1e-08
ADAM_WD = 0.01
ADAM_STEP = 10

VMEM_LIMIT = 56 * 1024 * 1024


def _dot(a, b):
    return jnp.dot(a, b, preferred_element_type=F32)


def _dot_nt(a, b):
    return lax.dot_general(a, b, (((1,), (1,)), ((), ())), preferred_element_type=F32)


def _dot_tn(a, b):
    return lax.dot_general(a, b, (((0,), (0,)), ((), ())), preferred_element_type=F32)


@jax.custom_vjp
def _mm(a, b):
    return _dot(a.astype(BF16), b.astype(BF16))


def _mm_fwd(a, b):
    return _mm(a, b), (a, b)


def _mm_bwd(res, g):
    a, b = res
    gb = g.astype(BF16)
    return _dot_nt(gb, b.astype(BF16)).astype(a.dtype), _dot_tn(a.astype(BF16), gb).astype(b.dtype)


_mm.defvjp(_mm_fwd, _mm_bwd)


@jax.custom_vjp
def _mm_nt(a, b):
    return _dot_nt(a.astype(BF16), b.astype(BF16))


def _mm_nt_fwd(a, b):
    return _mm_nt(a, b), (a, b)


def _mm_nt_bwd(res, g):
    a, b = res
    gb = g.astype(BF16)
    return _dot(gb, b.astype(BF16)).astype(a.dtype), _dot_tn(gb, a.astype(BF16)).astype(b.dtype)


_mm_nt.defvjp(_mm_nt_fwd, _mm_nt_bwd)


@jax.custom_vjp
def _softmax(s):
    m = jnp.max(s, axis=-1, keepdims=True)
    e = jnp.exp(s - m)
    return e / jnp.sum(e, axis=-1, keepdims=True)


def _softmax_fwd(s):
    p = _softmax(s)
    return p, p


def _softmax_bwd(p, g):
    return (p * (g - jnp.sum(p * g, axis=-1, keepdims=True)),)


_softmax.defvjp(_softmax_fwd, _softmax_bwd)


def _rms(x, g, n):
    ms = jnp.sum(x * x, axis=-1, keepdims=True) * (1.0 / n)
    return x * lax.rsqrt(ms + EPS) * g


def _sigmoid(x):
    return 1.0 / (1.0 + jnp.exp(-x))


def _silu(x):
    return x * _sigmoid(x)


def _gelu(x):
    c = math.sqrt(2.0 / math.pi)
    return 0.5 * x * (1.0 + jnp.tanh(c * (x + 0.044715 * (x * x * x))))


@jax.custom_vjp
def _rot(x, c, s1, s2):
    return x * c + pltpu.roll(x, 96, 1) * s1 + pltpu.roll(x, 32, 1) * s2


def _rot_fwd(x, c, s1, s2):
    return _rot(x, c, s1, s2), (c, s1, s2)


def _rot_bwd(res, g):
    c, s1, s2 = res
    dx = g * c + pltpu.roll(g * s1, 32, 1) + pltpu.roll(g * s2, 96, 1)
    return dx, jnp.zeros_like(c), jnp.zeros_like(s1), jnp.zeros_like(s2)


_rot.defvjp(_rot_fwd, _rot_bwd)


def _mem_attn(xq, k, v, gq):
    outs = []
    for h in range(X_HEADS):
        sl = slice(HD * h, HD * (h + 1))
        q = _rms(xq[:, sl], gq, HD)
        p = _softmax(_mm_nt(q, k[:, sl]) * (HD ** -0.5))
        outs.append(_mm(p, v[:, sl]))
    return jnp.concatenate(outs, axis=-1)


def _merge(mix, xq, gate, k, v, gq):
    return jnp.concatenate([mix, _mem_attn(xq, k, v, gq)], axis=-1) * _silu(gate)


def _q_chunks(q):
    return ([q[:, HD * h:HD * (h + 1)] for h in range(MLA_H)],
            [q[:, PRIM + HD * h:PRIM + HD * (h + 1)] for h in range(MLA_H)])


def _q_post(nope, rope, gqn, gqr, c, s1, s2):
    pieces = []
    for qn, qr in zip(nope, rope):
        pieces.append(_rms(qn, gqn, HD))
        pieces.append(_rot(_rms(qr, gqr, ROPE), c, s1, s2))
    return jnp.concatenate(pieces, axis=-1)


def _kv_chunks(kv):
    return ([kv[:, 2 * HD * h:2 * HD * h + HD] for h in range(MLA_H)],
            [kv[:, 2 * HD * h + HD:2 * HD * (h + 1)] for h in range(MLA_H)])


def _kv_post(kn, vals, krp, gkn, gkr, c, s1, s2):
    kr = _rot(_rms(krp, gkr, ROPE), c, s1, s2)
    pieces = []
    for k in kn:
        pieces.append(_rms(k, gkn, HD))
        pieces.append(kr)
    return jnp.concatenate(pieces, axis=-1), jnp.concatenate(vals, axis=-1)


def _rowwise(name, fn, ins, outs, nblk, host=None):
    n_in = len(ins)

    def spec(kind, shape):
        if kind == 'r':
            return pl.BlockSpec((shape[0] // nblk, shape[1]), lambda i: (i, 0))
        if kind == 't':
            return pl.BlockSpec((shape[0], shape[1] // nblk), lambda i: (0, i))
        zeros = (0,) * len(shape)
        return pl.BlockSpec(tuple(shape), lambda i: zeros)

    def body(*refs):
        i = pl.program_id(0)
        res = fn(*[r[...] for r in refs[:n_in]])
        for (kind, _, _), ref, val in zip(outs, refs[n_in:], res):
            if kind == 'a':
                @pl.when(i == 0)
                def _():
                    ref[...] = jnp.zeros_like(ref)
                ref[...] += val.astype(ref.dtype)
            elif kind == 't':
                ref[...] = val.astype(F32).T.astype(ref.dtype)
            else:
                ref[...] = val.astype(ref.dtype)

    res, hosted = _hosting_call(
        body, name, nblk, host, [a for _, a in ins], [spec(k, a.shape) for k, a in ins],
        [jax.ShapeDtypeStruct(tuple(s), d) for _, s, d in outs], [spec(k, s) for k, s, _ in outs], [])
    return res if host is None else (res, hosted)


def _matmul_tn(at, g, name, out_dtype=BF16):
    K, L = at.shape
    N = g.shape[1]
    tn = next(t for t in (512, 384, 256, 128) if N % t == 0)

    def body(a_ref, g_ref, o_ref):
        o_ref[...] = _dot(a_ref[...], g_ref[...]).astype(o_ref.dtype)

    return pl.pallas_call(
        body, name=name, grid=(N // tn,),
        in_specs=[pl.BlockSpec((K, L), lambda n: (0, 0)), pl.BlockSpec((L, tn), lambda n: (0, n))],
        out_specs=pl.BlockSpec((K, tn), lambda n: (0, n)),
        out_shape=jax.ShapeDtypeStruct((K, N), out_dtype),
        compiler_params=pltpu.CompilerParams(dimension_semantics=("arbitrary",), vmem_limit_bytes=VMEM_LIMIT),
    )(at, g)


def _matmul_tn_slots(at, g, name, host=None):
    K, L = at.shape
    n = g.shape[1] // N_DEV

    def body(a_ref, g_ref, o_ref):
        o_ref[...] = _dot(a_ref[...], g_ref[...]).astype(o_ref.dtype)

    res, hosted = _hosting_call(
        body, name, N_DEV, host, [at, g],
        [pl.BlockSpec((K, L), lambda d: (0, 0)), pl.BlockSpec((L, n), lambda d: (0, d))],
        [jax.ShapeDtypeStruct((N_DEV, K, n), BF16)], [pl.BlockSpec((None, K, n), lambda d: (d, 0, 0))], [])
    return res[0] if host is None else (res[0], hosted)


def _mm_slots(a16, w):
    return jnp.concatenate([_dot(a16, w[d]) for d in range(N_DEV)], axis=-1)


def _mm_slots_nt(g16, w):
    n = w.shape[2]
    out = _dot_nt(g16[:, 0:n], w[0])
    for d in range(1, N_DEV):
        out = out + _dot_nt(g16[:, d * n:(d + 1) * n], w[d])
    return out


class _Exchange:
    def __init__(self, ins, outs, scratch, start, finish):
        self.ins, self.outs, self.scratch, self.start, self.finish = ins, outs, scratch, start, finish


def _xyc():
    return lax.axis_index("x"), lax.axis_index("y"), lax.axis_index("c")


def _plan_all_gather(xs):
    n = len(xs)

    def build(x_refs, out_refs, sems):
        send_sems, recv_sems, local_sems = sems
        x, y, c = _xyc()

        def copies(k, block, to, own=False):
            slot = 4 * block[0] + 2 * block[1] + block[2]
            return [pltpu.make_async_remote_copy(
                src_ref=x_refs[a] if own else out_refs[a].at[slot], dst_ref=out_refs[a].at[slot],
                send_sem=send_sems.at[k * n + a], recv_sem=recv_sems.at[k * n + a], device_id=to,
                device_id_type=MESH) for a in range(n)]

        mine = [pltpu.make_async_copy(x_refs[a], out_refs[a].at[4 * x + 2 * y + c], local_sems.at[a])
                for a in range(n)]
        return copies, mine, (x, y, c), [(1 - x, y), (x, 1 - y), (1 - x, 1 - y)]

    def first_copies(copies, me, chips):
        x, y, c = me
        first = copies(0, me, (x, y, 1 - c), own=True)
        for j, chip in enumerate(chips):
            first += copies(1 + j, me, (*chip, c), own=True)
        return first

    def start(x_refs, out_refs, sems):
        copies, mine, me, chips = build(x_refs, out_refs, sems)
        for cp in mine + first_copies(copies, me, chips):
            cp.start()

    def finish(x_refs, out_refs, sems):
        copies, mine, me, chips = build(x_refs, out_refs, sems)
        x, y, c = me
        passed = []
        for j, chip in enumerate(chips):
            for cp in copies(1 + j, (*chip, c), me):
                cp.wait_recv()
            fwd = copies(4 + j, (*chip, c), (x, y, 1 - c))
            for cp in fwd:
                cp.start()
            passed += fwd
        for cp in copies(0, (x, y, 1 - c), me):
            cp.wait_recv()
        for j, chip in enumerate(chips):
            for cp in copies(4 + j, (*chip, 1 - c), me):
                cp.wait_recv()
        for cp in first_copies(copies, me, chips) + passed:
            cp.wait_send()
        for cp in mine:
            cp.wait()

    return _Exchange(list(xs), [jax.ShapeDtypeStruct((N_DEV,) + a.shape, a.dtype) for a in xs],
                     [pltpu.SemaphoreType.DMA((7 * n,)), pltpu.SemaphoreType.DMA((7 * n,)),
                      pltpu.SemaphoreType.DMA((n,))], start, finish)


_CHIPS = ((0, 0), (0, 1), (1, 0), (1, 1))


def _plan_pair(sends):
    n = len(sends)

    def build(s_refs, o_refs, sems):
        send_sems, recv_sems = sems
        x, y, c = _xyc()
        return [pltpu.make_async_remote_copy(
            src_ref=s_refs[a].at[4 * px + 2 * py + 1 - c], dst_ref=o_refs[a].at[j],
            send_sem=send_sems.at[j * n + a], recv_sem=recv_sems.at[j * n + a], device_id=(x, y, 1 - c),
            device_id_type=MESH) for j, (px, py) in enumerate(_CHIPS) for a in range(n)]

    def start(s_refs, o_refs, sems):
        for cp in build(s_refs, o_refs, sems):
            cp.start()

    def finish(s_refs, o_refs, sems):
        for cp in build(s_refs, o_refs, sems):
            cp.wait_recv()
            cp.wait_send()

    return _Exchange(list(sends), [jax.ShapeDtypeStruct((4,) + a.shape[1:], a.dtype) for a in sends],
                     [pltpu.SemaphoreType.DMA((4 * n,)), pltpu.SemaphoreType.DMA((4 * n,))], start, finish)


def _plan_chips(ts):
    n = len(ts)
    flips = ((1, 0), (0, 1), (1, 1))

    def build(t_refs, o_refs, sems):
        send_sems, recv_sems, local_sems = sems
        x, y, c = _xyc()
        mine = 2 * x + y
        local = [pltpu.make_async_copy(t_refs[a].at[mine], o_refs[a].at[mine], local_sems.at[a]) for a in range(n)]
        remote = []
        for k, (fx, fy) in enumerate(flips):
            px = 1 - x if fx else x
            py = 1 - y if fy else y
            remote += [pltpu.make_async_remote_copy(
                src_ref=t_refs[a].at[2 * px + py], dst_ref=o_refs[a].at[mine],
                send_sem=send_sems.at[k * n + a], recv_sem=recv_sems.at[k * n + a], device_id=(px, py, c),
                device_id_type=MESH) for a in range(n)]
        return local, remote

    def start(t_refs, o_refs, sems):
        local, remote = build(t_refs, o_refs, sems)
        for cp in local + remote:
            cp.start()

    def finish(t_refs, o_refs, sems):
        local, remote = build(t_refs, o_refs, sems)
        for cp in remote:
            cp.wait_recv()
        for cp in remote:
            cp.wait_send()
        for cp in local:
            cp.wait()

    return _Exchange(list(ts), [jax.ShapeDtypeStruct(a.shape, a.dtype) for a in ts],
                     [pltpu.SemaphoreType.DMA((3 * n,)), pltpu.SemaphoreType.DMA((3 * n,)),
                      pltpu.SemaphoreType.DMA((n,))], start, finish)


def _combine(*plans):
    def parts(refs, attr):
        out, at = [], 0
        for p in plans:
            n = len(getattr(p, attr))
            out.append(refs[at:at + n])
            at += n
        return out

    def run(half):
        def go(ins, outs, sems):
            for p, a, o, s in zip(plans, parts(ins, "ins"), parts(outs, "outs"), parts(sems, "scratch")):
                getattr(p, half)(a, o, s)
        return go

    return _Exchange(sum((p.ins for p in plans), []), sum((p.outs for p in plans), []),
                     sum((p.scratch for p in plans), []), run("start"), run("finish"))


def _exchange_call(plan, name):
    n = len(plan.ins)

    def body(*refs):
        ins, outs, sems = refs[:n], refs[n:2 * n], refs[2 * n:]
        plan.start(ins, outs, sems)
        plan.finish(ins, outs, sems)

    return pl.pallas_call(
        body, name=name, out_shape=plan.outs,
        in_specs=[pl.BlockSpec(memory_space=pl.ANY)] * n, out_specs=[pl.BlockSpec(memory_space=pl.ANY)] * n,
        scratch_shapes=plan.scratch,
    )(*plan.ins)


def _slab_spec(lead, rows, cols, nb):
    if rows % (nb * 16) == 0:
        return pl.BlockSpec((lead, rows // nb, cols), lambda i: (0, i, 0))
    if cols % (nb * 128) == 0:
        return pl.BlockSpec((lead, rows, cols // nb), lambda i: (0, 0, i))
    return pl.BlockSpec((lead, rows, cols), lambda i: (0, 0, 0))


def _slab_spec2(rows, cols, nb):
    if rows % (nb * 16) == 0:
        return pl.BlockSpec((rows // nb, cols), lambda i: (i, 0))
    if cols % (nb * 128) == 0:
        return pl.BlockSpec((rows, cols // nb), lambda i: (0, i))
    return pl.BlockSpec((rows, cols), lambda i: (0, 0))


def _cast_call(arrays, name, host=None):
    n = len(arrays)
    nb = 8

    def body(*refs):
        for a in range(n):
            refs[n + a][...] = refs[a][...].astype(BF16)

    specs = [_slab_spec2(x.shape[0], x.shape[1], nb) for x in arrays]
    return _hosting_call(body, name, nb, host, list(arrays), specs,
                         [jax.ShapeDtypeStruct(x.shape, BF16) for x in arrays], specs, [])


def _pair_add(sends, fromsib, name):
    n = len(sends)
    nb = 8

    def body(*refs):
        c = lax.axis_index("c")
        for a in range(n):
            s_ref, f_ref, t_ref = refs[a], refs[n + a], refs[2 * n + a]
            for j in range(4):
                t_ref[j] = (s_ref[2 * j + c].astype(F32) + f_ref[j].astype(F32)).astype(t_ref.dtype)

    def spec(a, lead):
        return _slab_spec(lead, a.shape[1], a.shape[2], nb)

    return pl.pallas_call(
        body, name=name, grid=(nb,),
        in_specs=[spec(a, N_DEV) for a in sends] + [spec(a, 4) for a in fromsib],
        out_specs=[spec(a, 4) for a in fromsib],
        out_shape=[jax.ShapeDtypeStruct(a.shape, a.dtype) for a in fromsib],
        compiler_params=pltpu.CompilerParams(dimension_semantics=("arbitrary",), vmem_limit_bytes=VMEM_LIMIT),
    )(*sends, *fromsib)


def _adamw_vals(w, g, m, v):
    m2 = ADAM_B1 * m + (1.0 - ADAM_B1) * g
    v2 = ADAM_B2 * v + (1.0 - ADAM_B2) * (g * g)
    m_hat = m2 / (1.0 - ADAM_B1 ** ADAM_STEP)
    v_hat = v2 / (1.0 - ADAM_B2 ** ADAM_STEP)
    delta = -ADAM_LR * (m_hat / (jnp.sqrt(v_hat) + ADAM_EPS) + ADAM_WD * w)
    return delta, m2, v2


def _sum_adamw(recv, w, m, v, name):
    R, C = w.shape
    ns = recv.shape[0]
    br = next((t for t in (256, 128, 64, 32, 16) if R % t == 0), R)

    def body(r_ref, w_ref, m_ref, v_ref, g_ref, d_ref, m2_ref, v2_ref):
        g = r_ref[0].astype(F32)
        for d in range(1, ns):
            g = g + r_ref[d].astype(F32)
        dl, m2, v2 = _adamw_vals(w_ref[...], g, m_ref[...], v_ref[...])
        g_ref[...] = g
        d_ref[...] = dl
        m2_ref[...] = m2
        v2_ref[...] = v2

    spec = pl.BlockSpec((br, C), lambda i: (i, 0))
    return pl.pallas_call(
        body, name=name, grid=(R // br,),
        in_specs=[pl.BlockSpec((ns, br, C), lambda i: (0, i, 0)), spec, spec, spec], out_specs=[spec] * 4,
        out_shape=[jax.ShapeDtypeStruct((R, C), F32)] * 4,
        compiler_params=pltpu.CompilerParams(dimension_semantics=("arbitrary",)),
    )(recv, w, m, v)


def _updates_call(recvs, ws, ms, vs, name, host=None):
    n = len(recvs)
    nb = 8

    def body(*refs):
        for a in range(n):
            r_ref, w_ref, m_ref, v_ref = refs[a], refs[n + a], refs[2 * n + a], refs[3 * n + a]
            g_ref, d_ref, m2_ref, v2_ref = refs[4 * n + 4 * a:4 * n + 4 * a + 4]
            g = r_ref[0].astype(F32)
            for d in range(1, r_ref.shape[0]):
                g = g + r_ref[d].astype(F32)
            dl, m2, v2 = _adamw_vals(w_ref[...], g, m_ref[...], v_ref[...])
            g_ref[...] = g
            d_ref[...] = dl
            m2_ref[...] = m2
            v2_ref[...] = v2

    def spec3(r):
        return _slab_spec(r.shape[0], r.shape[1], r.shape[2], nb)

    def spec2(w):
        return _slab_spec2(w.shape[0], w.shape[1], nb)

    res, hosted = _hosting_call(
        body, name, nb, host, list(recvs) + list(ws) + list(ms) + list(vs),
        [spec3(r) for r in recvs] + [spec2(w) for w in ws] * 3,
        [jax.ShapeDtypeStruct(w.shape, F32) for w in ws for _ in range(4)],
        [spec2(w) for w in ws for _ in range(4)], [])
    return [res[4 * a:4 * a + 4] for a in range(n)], hosted


def _small_sum(gath, loss_g, row0_g, name):
    _, R, C = gath.shape
    br = R // 3

    def body(g_ref, l_ref, r_ref, go_ref, lo_ref):
        g = g_ref[0].astype(F32)
        lsum = l_ref[0]
        for d in range(1, N_DEV):
            g = g + g_ref[d].astype(F32)
            lsum = lsum + l_ref[d]
        go_ref[...] = g
        lo_ref[...] = lsum

        @pl.when(pl.program_id(0) == 0)
        def _():
            row0 = r_ref[0]
            for d in range(1, N_DEV):
                row0 = row0 + r_ref[d]
            go_ref[0:8, :] = go_ref[0:8, :] + jnp.where(lax.broadcasted_iota(jnp.int32, row0.shape, 0) == 0, row0, 0.0)

    return pl.pallas_call(
        body, name=name, grid=(R // br,),
        in_specs=[pl.BlockSpec((N_DEV, br, C), lambda i: (0, i, 0)),
                  pl.BlockSpec((N_DEV, 8, HD), lambda i: (0, 0, 0)), pl.BlockSpec((N_DEV, 8, C), lambda i: (0, 0, 0))],
        out_specs=[pl.BlockSpec((br, C), lambda i: (i, 0)), pl.BlockSpec((8, HD), lambda i: (0, 0))],
        out_shape=[jax.ShapeDtypeStruct((R, C), F32), jax.ShapeDtypeStruct((8, HD), F32)],
        compiler_params=pltpu.CompilerParams(dimension_semantics=("arbitrary",)),
    )(gath, loss_g, row0_g)


def _adamw_multi(ws, gs, ms, vs, name, nblk=1):
    n = len(ws)

    def body(*refs):
        for a in range(n):
            dl, m2, v2 = _adamw_vals(refs[a][...], refs[n + a][...], refs[2 * n + a][...], refs[3 * n + a][...])
            refs[4 * n + 3 * a][...] = dl
            refs[4 * n + 3 * a + 1][...] = m2
            refs[4 * n + 3 * a + 2][...] = v2

    def spec(x):
        rest = (0,) * (x.ndim - 1)
        return pl.BlockSpec((x.shape[0] // nblk,) + tuple(x.shape[1:]), lambda i: (i,) + rest)

    res = pl.pallas_call(
        body, name=name, grid=(nblk,),
        in_specs=[spec(w) for w in ws] * 4, out_specs=[spec(w) for w in ws for _ in range(3)],
        out_shape=[jax.ShapeDtypeStruct(w.shape, F32) for w in ws for _ in range(3)],
        compiler_params=pltpu.CompilerParams(dimension_semantics=("arbitrary",), vmem_limit_bytes=VMEM_LIMIT),
    )(*ws, *gs, *ms, *vs)
    return [res[3 * a:3 * a + 3] for a in range(n)]


def _s5_param_fn(lr, li, ls, btr, bti):
    step = jnp.exp(ls)
    er = jnp.exp(lr * step)
    ang = li * step
    ar = er * jnp.cos(ang)
    ai = er * jnp.sin(ang)
    nr = ar - 1.0
    den = lr * lr + li * li
    fr = (nr * lr + ai * li) / den
    fi = (ai * lr - nr * li) / den
    return ar, ai, fr * btr - fi * bti, fr * bti + fi * btr


def _s5_params(lr, li, ls, btr, bti, cre, cim):
    nb = S5_G // S5_GB
    GC = S5_GB * S5_C
    expand = jnp.asarray(np.tile(np.eye(S5_P, dtype=np.float32), (1, S5_GB)), BF16)
    own = jnp.asarray((np.arange(GC)[:, None] // S5_C == np.arange(S5_W)[None, :] // S5_P).astype(np.float32))

    def body(lr_ref, li_ref, ls_ref, br_ref, bi_ref, cr_ref, ci_ref, e_ref, own_ref, ar_ref, ai_ref, bm_ref, cm_ref):
        ar, ai, bbr, bbi = _s5_param_fn(lr_ref[...], li_ref[...], ls_ref[...], br_ref[...], bi_ref[...])
        ar_ref[...] = ar
        ai_ref[...] = ai

        def plane(x, n):
            rows = x[n * S5_GB:(n + 1) * S5_GB].reshape(GC, S5_P).astype(BF16)
            return _dot(rows, e_ref[...]) * own_ref[...]

        for n in range(nb):
            bm_ref[n] = jnp.concatenate([plane(bbr, n), plane(bbi, n)], axis=-1).astype(BF16)
            cm_ref[n] = jnp.concatenate([plane(cr_ref[...], n), -plane(ci_ref[...], n)], axis=-1).astype(BF16)

    sd = jax.ShapeDtypeStruct
    return pl.pallas_call(
        body, name="s5_params",
        out_shape=[sd(lr.shape, F32), sd(lr.shape, F32), sd((nb, GC, 2 * S5_W), BF16), sd((nb, GC, 2 * S5_W), BF16)],
        compiler_params=pltpu.CompilerParams(vmem_limit_bytes=VMEM_LIMIT),
    )(lr, li, ls, btr, bti, cre, cim, expand, own)


def _s5_params_bwd(lr, li, ls, btr, bti, dar, dai, dbbr, dbbi):
    def body(lr_ref, li_ref, ls_ref, br_ref, bi_ref, dar_ref, dai_ref, dbbr_ref, dbbi_ref,
             dlr_ref, dli_ref, dls_ref, dbr_ref, dbi_ref):
        _, vjp = jax.vjp(_s5_param_fn, lr_ref[...], li_ref[...], ls_ref[...], br_ref[...], bi_ref[...])
        dlr, dli, dls, dbr, dbi = vjp((dar_ref[...], dai_ref[...], dbbr_ref[...], dbbi_ref[...]))
        dlr_ref[...] = dlr
        dli_ref[...] = dli
        dls_ref[...] = dls
        dbr_ref[...] = dbr
        dbi_ref[...] = dbi

    sd = jax.ShapeDtypeStruct
    return pl.pallas_call(
        body, name="s5_params_bwd",
        out_shape=[sd(lr.shape, F32), sd(lr.shape, F32), sd(ls.shape, F32), sd(btr.shape, F32), sd(btr.shape, F32)],
    )(lr, li, ls, btr, bti, dar, dai, dbbr, dbbi)


def _cpow(ar, ai, n):
    assert n & (n - 1) == 0
    while n > 1:
        ar, ai = ar * ar - ai * ai, 2.0 * ar * ai
        n //= 2
    return ar, ai


def _scan(st, cr, ci, init, nk, reverse, store, prev=None):
    W = S5_W

    def advance(k, sr, si):
        rows = pl.ds(k * 8 if isinstance(k, int) else pl.multiple_of(k * 8, 8), 8)
        nsr = cr * sr - ci * si + st[rows, 0:W]
        nsi = cr * si + ci * sr + st[rows, W:2 * W]
        if store:
            st[rows, 0:W] = nsr
            st[rows, W:2 * W] = nsi
        return nsr, nsi

    if prev is None:
        return lax.fori_loop(0, nk, lambda j, c: advance(nk - 1 - j if reverse else j, c[0], c[1]), init, unroll=2)
    assert reverse

    def step(j, carry):
        k = nk - 1 - j
        nsr, nsi = advance(k, carry[0], carry[1])
        prows = pl.ds(pl.multiple_of((k - 1) * 8, 8), 8)
        pr = prev[prows, 0:W]
        pi = prev[prows, W:2 * W]
        return nsr, nsi, carry[2] + nsr * pr + nsi * pi, carry[3] + nsi * pr - nsr * pi

    carry = lax.fori_loop(0, nk - 1, step, init, unroll=2)
    nsr, nsi = advance(0, carry[0], carry[1])
    return nsr, nsi, carry[2], carry[3]


def _chain(fin, fr, fi, pr, pi, reverse):
    W = S5_W
    fin[:, 0:W] = fr
    fin[:, W:2 * W] = fi
    rowid = lax.broadcasted_iota(jnp.int32, (8, W), 0)
    cr = jnp.zeros((1, W), F32)
    ci = jnp.zeros((1, W), F32)
    init_r = jnp.zeros((8, W), F32)
    init_i = jnp.zeros((8, W), F32)
    for s in (range(7, -1, -1) if reverse else range(8)):
        init_r = jnp.where(rowid == s, cr, init_r)
        init_i = jnp.where(rowid == s, ci, init_i)
        lr = fin[s:s + 1, 0:W]
        li = fin[s:s + 1, W:2 * W]
        cr, ci = lr + pr * cr - pi * ci, li + pr * ci + pi * cr
    return init_r, init_i


def _full_scan(st, fin, ar, ai, nk, reverse, prev=None, carry_in=None, carry_out=None):
    W = S5_W
    cr = jnp.broadcast_to(ar, (8, W))
    ci = jnp.broadcast_to(-ai if reverse else ai, (8, W))
    z = jnp.zeros((8, W), F32)
    if carry_in is None:
        fr, fi = _scan(st, cr, ci, (z, z), nk, reverse, store=False)
        pr, pi = _cpow(ar, -ai if reverse else ai, nk)
        init = _chain(fin, fr, fi, pr, pi, reverse)
    else:
        init = (carry_in[:, 0:W], carry_in[:, W:2 * W])
    if carry_out is not None:
        carry_out[:, 0:W] = init[0]
        carry_out[:, W:2 * W] = init[1]
    if prev is None:
        return _scan(st, cr, ci, init, nk, reverse, store=True)
    return _scan(st, cr, ci, init + (z, z), nk, reverse, store=True, prev=prev)


def _s5_specs(L):
    W2 = 2 * S5_W
    GC = S5_GB * S5_C
    col = pl.BlockSpec((L, GC), lambda g: (0, g))
    vec = pl.BlockSpec((1, GC), lambda g: (0, g))
    avec = pl.BlockSpec((1, S5_W), lambda g: (0, g))
    bmat = pl.BlockSpec((None, GC, W2), lambda g: (g, 0, 0))
    cmat = pl.BlockSpec((None, W2, GC), lambda g: (g, 0, 0))
    return col, vec, avec, bmat, cmat


def _interleave(dst, src, nk):
    for s in range(8):
        dst[pl.ds(s, nk, stride=8), :] = src[s * nk:(s + 1) * nk, :]


def _deinterleave(dst, src, nk):
    for s in range(8):
        dst[s * nk:(s + 1) * nk, :] = src[pl.ds(s, nk, stride=8), :].astype(dst.dtype)


def _hosting_call(body, name, nsteps, host, ins, in_specs, outs, out_specs, scratch):
    grid = (nsteps,) if isinstance(nsteps, int) else tuple(nsteps)
    params = pltpu.CompilerParams(dimension_semantics=("arbitrary",) * len(grid), vmem_limit_bytes=VMEM_LIMIT)
    if host is None:
        res = pl.pallas_call(
            body, name=name, grid=grid, in_specs=in_specs, out_specs=out_specs, out_shape=outs,
            scratch_shapes=scratch, compiler_params=params,
        )(*ins)
        return list(res), []
    n_in, n_out, n_sc = len(ins), len(outs), len(scratch)
    h_in, h_out = len(host.ins), len(host.outs)

    def hosted(*refs):
        a = refs[:n_in]
        ha = refs[n_in:n_in + h_in]
        o = refs[n_in + h_in:n_in + h_in + n_out]
        ho = refs[n_in + h_in + n_out:n_in + h_in + n_out + h_out]
        sc = refs[n_in + h_in + n_out + h_out:n_in + h_in + n_out + h_out + n_sc]
        hs = refs[n_in + h_in + n_out + h_out + n_sc:]
        first = functools.reduce(jnp.logical_and, [pl.program_id(i) == 0 for i in range(len(grid))])
        last = functools.reduce(jnp.logical_and, [pl.program_id(i) == g - 1 for i, g in enumerate(grid)])

        @pl.when(first)
        def _():
            host.start(ha, ho, hs)

        body(*a, *o, *sc)

        @pl.when(last)
        def _():
            host.finish(ha, ho, hs)

    hbm = pl.BlockSpec(memory_space=pl.ANY)
    res = pl.pallas_call(
        hosted, name=name, grid=grid,
        in_specs=list(in_specs) + [hbm] * h_in, out_specs=list(out_specs) + [hbm] * h_out,
        out_shape=list(outs) + list(host.outs), scratch_shapes=list(scratch) + list(host.scratch),
        compiler_params=params,
    )(*ins, *host.ins)
    return list(res[:n_out]), list(res[n_out:])


def _s5_fwd(u, bm, cm, ar, ai, dvec, host=None):
    L = u.shape[0]
    nk = L // 8
    GC = S5_GB * S5_C
    nb = S5_G // S5_GB
    col, vec, avec, bmat, cmat = _s5_specs(L)

    def body(u_ref, b_ref, c_ref, ar_ref, ai_ref, d_ref, y_ref, carry_ref, st, fin, ui, yi):
        _interleave(ui, u_ref, nk)
        for r in range(8):
            rows = slice(r * nk, (r + 1) * nk)
            st[rows, :] = _dot(ui[rows, :].astype(BF16), b_ref[...])
        _full_scan(st, fin, ar_ref[...], ai_ref[...], nk, reverse=False, carry_out=carry_ref)
        for r in range(8):
            rows = slice(r * nk, (r + 1) * nk)
            yi[rows, :] = _dot_nt(st[rows, :].astype(BF16), c_ref[...]) + d_ref[...] * ui[rows, :]
        _deinterleave(y_ref, yi, nk)

    return _hosting_call(
        body, "s5_fwd", nb, host,
        [u, bm, cm, ar, ai, dvec], [col, bmat, bmat, avec, avec, vec],
        [jax.ShapeDtypeStruct(u.shape, F32), jax.ShapeDtypeStruct((nb * 8, 2 * S5_W), F32)],
        [col, pl.BlockSpec((8, 2 * S5_W), lambda g: (g, 0))],
        [pltpu.VMEM((L, 2 * S5_W), F32), pltpu.VMEM((8, 2 * S5_W), F32), pltpu.VMEM((L, GC), F32),
         pltpu.VMEM((L, GC), F32)])


def _s5_bwd(u, dy, carry, bm, cm, ar, ai, dvec, mask, rmat, host=None):
    L = u.shape[0]
    nk = L // 8
    W = S5_W
    GC = S5_GB * S5_C
    col, vec, avec, bmat, cmat = _s5_specs(L)
    hi = lax.Precision.HIGHEST

    def body(u_ref, dy_ref, carry_ref, b_ref, ct_ref, ar_ref, ai_ref, d_ref, mask_ref, r_ref,
             du_ref, db_ref, dc_ref, dd_ref, dar_ref, dai_ref, sa, sb, fin, ui, dyi, dui):
        ar = ar_ref[...]
        ai = ai_ref[...]
        _interleave(ui, u_ref, nk)
        _interleave(dyi, dy_ref, nk)
        for r in range(8):
            rows = slice(r * nk, (r + 1) * nk)
            sa[rows, :] = _dot(ui[rows, :].astype(BF16), b_ref[...])
            sb[rows, :] = _dot(dyi[rows, :].astype(BF16), ct_ref[...])
        _full_scan(sa, fin, ar, ai, nk, reverse=False, carry_in=carry_ref)
        gr, gi, accr, acci = _full_scan(sb, fin, ar, ai, nk, reverse=True, prev=sa)
        rowid = lax.broadcasted_iota(jnp.int32, (8, W), 0)
        last = pl.ds((nk - 1) * 8, 8)
        pr = jnp.where(rowid == 0, 0.0, pltpu.roll(sa[last, 0:W], 1, 0))
        pi = jnp.where(rowid == 0, 0.0, pltpu.roll(sa[last, W:2 * W], 1, 0))
        accr = accr + gr * pr + gi * pi
        acci = acci + gi * pr - gr * pi
        dar_ref[...] = jnp.sum(accr, axis=0, keepdims=True)
        dai_ref[...] = jnp.sum(acci, axis=0, keepdims=True)
        dbf = jnp.zeros((GC, 2 * W), F32)
        dcf = jnp.zeros((GC, 2 * W), F32)
        dd = jnp.zeros((1, GC), F32)
        for r in range(8):
            rows = slice(r * nk, (r + 1) * nk)
            ub = ui[rows, :]
            dyb = dyi[rows, :]
            gb = sb[rows, :].astype(BF16)
            dui[rows, :] = _dot_nt(gb, b_ref[...]) + d_ref[...] * dyb
            dbf = dbf + _dot_tn(ub.astype(BF16), gb)
            dcf = dcf + _dot_tn(dyb.astype(BF16), sa[rows, :].astype(BF16))
            dd = dd + jnp.sum(dyb * ub, axis=0, keepdims=True)
        db_ref[...] = jnp.dot(dbf * mask_ref[...], r_ref[...], precision=hi, preferred_element_type=F32)
        dc_ref[...] = jnp.dot(dcf * mask_ref[...], r_ref[...], precision=hi, preferred_element_type=F32)
        dd_ref[...] = dd
        _deinterleave(du_ref, dui, nk)

    cmp_spec = pl.BlockSpec((GC, 2 * S5_P), lambda g: (g, 0))
    whole = lambda shape: pl.BlockSpec(shape, lambda g: (0, 0))
    sd = jax.ShapeDtypeStruct
    return _hosting_call(
        body, "s5_bwd", S5_G // S5_GB, host,
        [u, dy, carry, bm, cm, ar, ai, dvec, mask, rmat],
        [col, col, pl.BlockSpec((8, 2 * W), lambda g: (g, 0)), bmat, bmat, avec, avec, vec, whole(mask.shape),
         whole(rmat.shape)],
        [sd(u.shape, BF16), sd((S5_G * S5_C, 2 * S5_P), F32), sd((S5_G * S5_C, 2 * S5_P), F32),
         sd((1, PRIM), F32), sd((1, S5_G * S5_P), F32), sd((1, S5_G * S5_P), F32)],
        [col, cmp_spec, cmp_spec, vec, avec, avec],
        [pltpu.VMEM((L, 2 * W), F32), pltpu.VMEM((L, 2 * W), F32), pltpu.VMEM((8, 2 * W), F32),
         pltpu.VMEM((L, GC), F32), pltpu.VMEM((L, GC), F32), pltpu.VMEM((L, GC), F32)])


def _s5_compact_consts():
    g_row = np.arange(S5_GB * S5_C) // S5_C
    col = np.arange(2 * S5_W)
    g_col = (col % S5_W) // S5_P
    mask = (g_row[:, None] == g_col[None, :]).astype(np.float32)
    tgt = (col // S5_W) * S5_P + col % S5_P
    rmat = (tgt[:, None] == np.arange(2 * S5_P)[None, :]).astype(np.float32)
    return jnp.asarray(mask), jnp.asarray(rmat)


def _attn_scores(q_ref, k_ref, qb, bq, scale):
    ext = (qb + 1) * bq
    s = _dot_nt(q_ref[qb * bq:ext, :], k_ref[0:ext, :]) * scale
    qpos = lax.broadcasted_iota(jnp.int32, (bq, bq), 0)
    kpos = lax.broadcasted_iota(jnp.int32, (bq, bq), 1)
    diag = jnp.where(kpos <= qpos, s[:, ext - bq:], NEG)
    return diag if qb == 0 else jnp.concatenate([s[:, :ext - bq], diag], axis=-1)


def _attn_fwd(qp, kp, v, scale):
    L = qp.shape[0]
    bq = min(256, L)

    def body(q_ref, k_ref, v_ref, o_ref, lse_ref):
        for qb in range(L // bq):
            rows = slice(qb * bq, (qb + 1) * bq)
            s = _attn_scores(q_ref, k_ref, qb, bq, scale)
            m = jnp.max(s, axis=-1, keepdims=True)
            e = jnp.exp(s - m)
            l = jnp.sum(e, axis=-1, keepdims=True)
            o_ref[rows, :] = _dot(e.astype(BF16), v_ref[0:(qb + 1) * bq, :]) / l
            lse_ref[rows, :] = jnp.broadcast_to(m + jnp.log(l), (bq, HD))

    blk = pl.BlockSpec((L, HD), lambda h: (0, h))
    wide = pl.BlockSpec((L, 2 * HD), lambda h: (0, h))
    return pl.pallas_call(
        body, name="mla_attn_fwd", grid=(MLA_H,),
        in_specs=[wide, wide, blk], out_specs=[blk, blk],
        out_shape=[jax.ShapeDtypeStruct((L, MLA_H * HD), F32)] * 2,
        compiler_params=pltpu.CompilerParams(dimension_semantics=("arbitrary",), vmem_limit_bytes=VMEM_LIMIT),
    )(qp, kp, v)


def _attn_bwd(qp, kp, v, o, lse, do, scale):
    L = qp.shape[0]
    bq = min(256, L)
    nq = L // bq

    def body(q_ref, k_ref, v_ref, o_ref, lse_ref, do_ref, dq_ref, dk_ref, dv_ref, dk_acc, dv_acc):
        dk_acc[...] = jnp.zeros_like(dk_acc)
        dv_acc[...] = jnp.zeros_like(dv_acc)
        for qb in range(nq):
            rows = slice(qb * bq, (qb + 1) * bq)
            ext = (qb + 1) * bq
            do = do_ref[rows, :]
            dob = do.astype(BF16)
            p = jnp.exp(_attn_scores(q_ref, k_ref, qb, bq, scale) - lse_ref[rows, 0:1])
            dp = _dot_nt(dob, v_ref[0:ext, :])
            dsum = jnp.sum(do * o_ref[rows, :], axis=-1, keepdims=True)
            ds = (p * (dp - dsum) * scale).astype(BF16)
            dq_ref[rows, :] = _dot(ds, k_ref[0:ext, :]).astype(dq_ref.dtype)
            dk_acc[0:ext, :] += _dot_tn(ds, q_ref[rows, :])
            dv_acc[0:ext, :] += _dot_tn(p.astype(BF16), dob)
        dk_ref[...] = dk_acc[...].astype(dk_ref.dtype)
        dv_ref[...] = dv_acc[...].astype(dv_ref.dtype)

    sd = jax.ShapeDtypeStruct
    blk = pl.BlockSpec((L, HD), lambda h: (0, h))
    wide = pl.BlockSpec((L, 2 * HD), lambda h: (0, h))
    return pl.pallas_call(
        body, name="mla_attn_bwd", grid=(MLA_H,),
        in_specs=[wide, wide, blk, blk, blk, blk], out_specs=[wide, wide, blk],
        out_shape=[sd((L, MLA_H * 2 * HD), BF16), sd((L, MLA_H * 2 * HD), BF16), sd((L, MLA_H * HD), BF16)],
        scratch_shapes=[pltpu.VMEM((L, 2 * HD), F32), pltpu.VMEM((L, HD), F32)],
        compiler_params=pltpu.CompilerParams(dimension_semantics=("arbitrary",), vmem_limit_bytes=VMEM_LIMIT),
    )(qp, kp, v, o, lse, do)


def _kv_fn(mem, gm, w, gk):
    kv = _mm(_rms(mem, gm, D_MODEL), w)
    k = jnp.concatenate([_rms(kv[:, HD * h:HD * (h + 1)], gk, HD) for h in range(X_HEADS)], axis=-1)
    return k, kv[:, XQ:]


def _kv_prep(mem, gm, w, gk, name):
    def fn(mem, gm, w, gk):
        return _kv_fn(mem, gm, w, gk)
    M = mem.shape[0]
    return _rowwise(name, fn, [('c', mem), ('c', gm), ('c', w), ('c', gk)],
                    [('c', (M, XQ), F32), ('c', (M, XQ), F32)], 1)


def _kv_prep_bwd(mem, gm, w, gk, dk, dv, name):
    def fn(mem, gm, w, gk, dk, dv):
        _, vjp = jax.vjp(lambda a, b, c: _kv_fn(mem, a, b, c), gm, w, gk)
        return vjp((dk, dv))
    return _rowwise(name, fn, [('c', mem), ('c', gm), ('c', w), ('c', gk), ('c', dk), ('c', dv)],
                    [('c', gm.shape, F32), ('c', w.shape, BF16), ('c', gk.shape, F32)], 1)


def _forward_merge(x, mix, mix_kind, xq, gate, k, v, gq, wout, name, nblk, host=None):
    def fn(x, mix, xq, gate, k, v, gq, wout):
        o = _merge(mix, xq, gate, k, v, gq)
        return (x + _dot(o.astype(BF16), wout),)
    L = x.shape[0]
    out = _rowwise(name, fn, [('r', x), (mix_kind, mix), ('r', xq), ('r', gate), ('c', k), ('c', v), ('c', gq),
                              ('c', wout)], [('r', (L, D_MODEL), F32)], nblk, host=host)
    return out[0] if host is None else (out[0][0], out[1])


def _backward_merge(dx, mix, mix_kind, xq, gate, k, v, gq, wout, name, nblk, host=None):
    def fn(dx, mix, xq, gate, k, v, gq, wout):
        g16 = dx.astype(BF16)
        do = _dot_nt(g16, wout)
        o, vjp = jax.vjp(_merge, mix, xq, gate, k, v, gq)
        dmix, dxq, dgate, dk, dv, dgq = vjp(do)
        return dmix, dxq, dgate, o, g16, dk, dv, dgq
    L = dx.shape[0]
    return _rowwise(
        name, fn,
        [('r', dx), (mix_kind, mix), ('r', xq), ('r', gate), ('c', k), ('c', v), ('c', gq), ('c', wout)],
        [('r', (L, PRIM), F32), ('r', (L, XQ), BF16), ('r', (L, BRANCH), BF16), ('t', (BRANCH, L), BF16),
         ('r', (L, D_MODEL), BF16), ('a', k.shape, F32), ('a', v.shape, F32), ('a', gq.shape, F32)], nblk,
        host=host)


_MLA_IN = 3392
_MLA_IN_PAD = 3456


def _uq_rows(wt):
    r = wt.reshape(MLA_H, HD + ROPE, wt.shape[1])
    return jnp.concatenate([r[:, :HD].reshape(PRIM, -1),
                            jnp.pad(r[:, HD:], ((0, 0), (0, HD - ROPE), (0, 0))).reshape(PRIM, -1)], axis=0)


def _uq_rows_back(wt):
    nope = wt[:PRIM].reshape(MLA_H, HD, -1)
    rope = wt[PRIM:].reshape(MLA_H, HD, -1)[:, :ROPE]
    return jnp.concatenate([nope, rope], axis=1).reshape(MLA_H * (HD + ROPE), -1)


def _mla_in_rows(wt):
    return jnp.concatenate([wt[:768], wt[832:], wt[768:832], jnp.zeros((64, wt.shape[1]), wt.dtype)], axis=0)


def _mla_in_rows_back(wt):
    return jnp.concatenate([wt[:768], wt[3328:3392], wt[768:3328]], axis=0)


_SMALL = (("ln_gain", 2048), ("mem_norm", 2048), ("xq_norm", 256), ("xk_norm", 256), ("s5_lambda_re", 6144),
          ("s5_lambda_im", 6144), ("s5_log_step", 96), ("s5_b_re", 98304), ("s5_b_im", 98304), ("s5_c_re", 98304),
          ("s5_c_im", 98304), ("s5_d", 1536), ("mla_q_lora_norm", 512), ("mla_kv_lora_norm", 256),
          ("mla_q_nope_norm", 128), ("mla_k_nope_norm", 128), ("mla_q_rope_norm", 64), ("mla_k_rope_norm", 64))
_SMALL_ROWS = 432
_SMALL_OFF = {name: sum(n for _, n in _SMALL[:i]) for i, (name, _) in enumerate(_SMALL)}


def _pack_small(d):
    flat = jnp.concatenate([d[n].reshape(-1).astype(F32) for n, _ in _SMALL])
    return jnp.pad(flat, (0, _SMALL_ROWS * 1024 - flat.shape[0])).reshape(_SMALL_ROWS, 1024)


def _unpack_small(p, name, shape):
    off = _SMALL_OFF[name]
    return p.reshape(-1)[off:off + int(np.prod(shape))].reshape(shape)


_WEIGHTS = ('ln_gain', 'w_out', 'mem_norm', 'w_mem_kv', 'xq_norm', 'xk_norm', 's5_w_in', 's5_lambda_re',
            's5_lambda_im', 's5_log_step', 's5_b_re', 's5_b_im', 's5_c_re', 's5_c_im', 's5_d', 's5_w_glu', 'mla_w_in',
            'mla_q_lora_norm', 'mla_kv_lora_norm', 'mla_w_uq', 'mla_w_ukv', 'mla_q_nope_norm', 'mla_k_nope_norm',
            'mla_q_rope_norm', 'mla_k_rope_norm')
_BIG = ('w_out', 'w_mem_kv', 's5_w_in', 's5_w_glu', 'mla_w_in', 'mla_w_uq', 'mla_w_ukv')


def _pad128(g):
    return jnp.pad(g.reshape(1, -1), ((0, 0), (0, HD - g.shape[-1])))


def kernel(x, mem, positions, ln_gain, w_out, mem_norm, w_mem_kv, xq_norm, xk_norm, s5_w_in, s5_lambda_re, s5_lambda_im, s5_log_step, s5_b_re, s5_b_im, s5_c_re, s5_c_im, s5_d, s5_w_glu, mla_w_in, mla_q_lora_norm, mla_kv_lora_norm, mla_w_uq, mla_w_ukv, mla_q_nope_norm, mla_k_nope_norm, mla_q_rope_norm, mla_k_rope_norm, loss_target, m_ln_gain, m_w_out, m_mem_norm, m_w_mem_kv, m_xq_norm, m_xk_norm, m_s5_w_in, m_s5_lambda_re, m_s5_lambda_im, m_s5_log_step, m_s5_b_re, m_s5_b_im, m_s5_c_re, m_s5_c_im, m_s5_d, m_s5_w_glu, m_mla_w_in, m_mla_q_lora_norm, m_mla_kv_lora_norm, m_mla_w_uq, m_mla_w_ukv, m_mla_q_nope_norm, m_mla_k_nope_norm, m_mla_q_rope_norm, m_mla_k_rope_norm, v_ln_gain, v_w_out, v_mem_norm, v_w_mem_kv, v_xq_norm, v_xk_norm, v_s5_w_in, v_s5_lambda_re, v_s5_lambda_im, v_s5_log_step, v_s5_b_re, v_s5_b_im, v_s5_c_re, v_s5_c_im, v_s5_d, v_s5_w_glu, v_mla_w_in, v_mla_q_lora_norm, v_mla_kv_lora_norm, v_mla_w_uq, v_mla_w_ukv, v_mla_q_nope_norm, v_mla_k_nope_norm, v_mla_q_rope_norm, v_mla_k_rope_norm):
    weights = dict(ln_gain=ln_gain, w_out=w_out, mem_norm=mem_norm, w_mem_kv=w_mem_kv, xq_norm=xq_norm,
                   xk_norm=xk_norm, s5_w_in=s5_w_in, s5_lambda_re=s5_lambda_re, s5_lambda_im=s5_lambda_im,
                   s5_log_step=s5_log_step, s5_b_re=s5_b_re, s5_b_im=s5_b_im, s5_c_re=s5_c_re, s5_c_im=s5_c_im,
                   s5_d=s5_d, s5_w_glu=s5_w_glu, mla_w_in=mla_w_in, mla_q_lora_norm=mla_q_lora_norm,
                   mla_kv_lora_norm=mla_kv_lora_norm, mla_w_uq=mla_w_uq, mla_w_ukv=mla_w_ukv,
                   mla_q_nope_norm=mla_q_nope_norm, mla_k_nope_norm=mla_k_nope_norm,
                   mla_q_rope_norm=mla_q_rope_norm, mla_k_rope_norm=mla_k_rope_norm)
    m_in = dict(zip(_WEIGHTS, (m_ln_gain, m_w_out, m_mem_norm, m_w_mem_kv, m_xq_norm, m_xk_norm, m_s5_w_in,
                               m_s5_lambda_re, m_s5_lambda_im, m_s5_log_step, m_s5_b_re, m_s5_b_im, m_s5_c_re,
                               m_s5_c_im, m_s5_d, m_s5_w_glu, m_mla_w_in, m_mla_q_lora_norm, m_mla_kv_lora_norm,
                               m_mla_w_uq, m_mla_w_ukv, m_mla_q_nope_norm, m_mla_k_nope_norm, m_mla_q_rope_norm,
                               m_mla_k_rope_norm)))
    v_in = dict(zip(_WEIGHTS, (v_ln_gain, v_w_out, v_mem_norm, v_w_mem_kv, v_xq_norm, v_xk_norm, v_s5_w_in,
                               v_s5_lambda_re, v_s5_lambda_im, v_s5_log_step, v_s5_b_re, v_s5_b_im, v_s5_c_re,
                               v_s5_c_im, v_s5_d, v_s5_w_glu, v_mla_w_in, v_mla_q_lora_norm, v_mla_kv_lora_norm,
                               v_mla_w_uq, v_mla_w_ukv, v_mla_q_nope_norm, v_mla_k_nope_norm, v_mla_q_rope_norm,
                               v_mla_k_rope_norm)))

    x0 = x[0]
    mem0 = mem[0]
    target = loss_target[0]
    L = x0.shape[0]
    nblk = 4
    nb_big = 8
    me = 4 * lax.axis_index("x") + 2 * lax.axis_index("y") + lax.axis_index("c")

    lora = jnp.pad(jnp.concatenate([mla_q_lora_norm, mla_kv_lora_norm], axis=1), ((0, 7), (0, HD - 96)))
    def gather(*shards):
        return _plan_all_gather(list(shards))

    kh = D_MODEL // 2
    (b_mkv0, b_glu, b_in_mla, b_out0, b_uq, b_ukv, b_mkv1, b_out1), (W_in_s5,) = _cast_call(
        [w_mem_kv[0], s5_w_glu[0], jnp.transpose(mla_w_in[0]), w_out[0], jnp.transpose(mla_w_uq[0]), mla_w_ukv[0],
         w_mem_kv[1], w_out[1]], "cast_shards", host=gather(s5_w_in[0].astype(BF16)))

    ln0, ln1 = ln_gain[0:1], ln_gain[1:2]
    gq0, gq1 = xq_norm[0:1], xq_norm[1:2]
    gk0, gk1 = xk_norm[0:1], xk_norm[1:2]
    gm0, gm1 = mem_norm[0:1], mem_norm[1:2]
    gqn, gkn = mla_q_nope_norm, mla_k_nope_norm
    gqr, gkr = _pad128(mla_q_rope_norm), _pad128(mla_k_rope_norm)

    lr3 = s5_lambda_re.reshape(S5_G, 1, S5_P)
    li3 = s5_lambda_im.reshape(S5_G, 1, S5_P)
    ls3 = s5_log_step.reshape(S5_G, 1, 1)
    btr = jnp.swapaxes(s5_b_re[0], 1, 2)
    bti = jnp.swapaxes(s5_b_im[0], 1, 2)
    a_r, a_i, bm, cm = _s5_params(lr3, li3, ls3, btr, bti, s5_c_re[0], s5_c_im[0])
    a_r2 = a_r.reshape(1, S5_G * S5_P)
    a_i2 = a_i.reshape(1, S5_G * S5_P)
    cmask, rmat = _s5_compact_consts()

    half = ROPE // 2
    inv_freq = ROPE_THETA ** (-jnp.arange(half, dtype=F32) / half)
    invf = jnp.concatenate([inv_freq, inv_freq, jnp.zeros((HD - ROPE,), F32)]).reshape(1, HD)

    def rot_tables(pos, invf):
        ang = pos.astype(F32) * invf
        lane = lax.broadcasted_iota(jnp.int32, ang.shape, 1)
        c = jnp.where(lane < ROPE, jnp.cos(ang), 0.0)
        s = jnp.sin(ang)
        return c, jnp.where(lane < half, -s, 0.0), jnp.where((lane >= half) & (lane < ROPE), s, 0.0)

    tc, ts1, ts2 = _rowwise("rot_tables", rot_tables, [('r', positions.reshape(L, 1)), ('c', invf)],
                            [('r', (L, HD), F32)] * 3, nblk)

    def in_s5(x, g, w):
        proj = _mm_slots(_rms(x, g, D_MODEL).astype(BF16), w)
        return proj[:, :PRIM], proj[:, PRIM:PRIM + XQ], proj[:, PRIM + XQ:]

    u_s5, xq_a, gate_a = _rowwise(
        "s5_in", in_s5, [('r', x0), ('c', ln0), ('c', W_in_s5)],
        [('r', (L, PRIM), F32), ('r', (L, XQ), F32), ('r', (L, BRANCH), F32)], nblk)
    (y_s5, s5_carry), (W_glu, G_mkv0, G_in_mla_a) = _s5_fwd(u_s5, bm, cm, a_r2, a_i2, s5_d,
                                                            host=gather(b_glu, b_mkv0, b_in_mla[:, :kh]))

    def glu(y, w):
        z = _mm_slots(_gelu(y).astype(BF16), w)
        return z[:, :PRIM] * _sigmoid(z[:, PRIM:]), z

    (y2, z_glu), (G_out0,) = _rowwise("s5_glu", glu, [('r', y_s5), ('c', W_glu)],
                                      [('r', (L, PRIM), F32), ('r', (L, 2 * PRIM), F32)], nblk, host=gather(b_out0))
    W_mkv0 = G_mkv0.reshape(D_MODEL, 2 * XQ)
    k_a, v_a = _kv_prep(mem0, gm0, W_mkv0, gk0, "kv_prep0")
    x1, (G_in_mla_b,) = _forward_merge(
        x0, y2, 'r', xq_a, gate_a, k_a, v_a, gq0, G_out0.reshape(BRANCH, D_MODEL), "merge0", nblk,
        host=gather(b_in_mla[:, kh:]))
    W_in_mla = _mla_in_rows(jnp.concatenate([G_in_mla_a, G_in_mla_b], axis=2).reshape(_MLA_IN, D_MODEL))

    def in_mla(x, g, w):
        proj = _dot_nt(_rms(x, g, D_MODEL).astype(BF16), w)
        return proj[:, :512], proj[:, 512:768], proj[:, 768:1280], proj[:, 1280:3328], proj[:, 3328:]

    (c_q, c_kv, xq_b, gate_b, krp), (G_uq, W_kv, G_lora) = _rowwise(
        "mla_in", in_mla, [('r', x1), ('c', ln1), ('c', W_in_mla)],
        [('r', (L, Q_LORA), F32), ('r', (L, KV_LORA), F32), ('r', (L, XQ), F32), ('r', (L, BRANCH), F32),
         ('r', (L, HD), F32)], nblk,
        host=gather(b_uq, b_ukv, lora))
    W_q = _uq_rows(G_uq.reshape(MLA_H * (HD + ROPE), Q_LORA))
    g_qlora = G_lora[:, 0, :64].reshape(1, Q_LORA)
    g_kvlora = G_lora[:, 0, 64:96].reshape(1, KV_LORA)

    def qkv(c_q, c_kv, krp, tc, ts1, ts2, gql, gkvl, wq, wkv, gqn, gkn, gqr, gkr):
        q = _dot_nt(_rms(c_q, gql, Q_LORA).astype(BF16), wq)
        kv = _mm_slots(_rms(c_kv, gkvl, KV_LORA).astype(BF16), wkv)
        kp, v = _kv_post(*_kv_chunks(kv), krp, gkn, gkr, tc, ts1, ts2)
        return _q_post(*_q_chunks(q), gqn, gqr, tc, ts1, ts2), kp, v

    qkv_consts = [('c', g_qlora), ('c', g_kvlora), ('c', W_q), ('c', W_kv), ('c', gqn), ('c', gkn), ('c', gqr),
                  ('c', gkr)]
    (q_pad, k_pad, v_h), (G_mkv1, G_out1) = _rowwise(
        "mla_qkv", qkv, [('r', c_q), ('r', c_kv), ('r', krp), ('r', tc), ('r', ts1), ('r', ts2)] + qkv_consts,
        [('r', (L, 2 * PRIM), BF16), ('r', (L, 2 * PRIM), BF16), ('r', (L, PRIM), BF16)], nblk,
        host=gather(b_mkv1, b_out1))
    W_out = (G_out0.reshape(BRANCH, D_MODEL), G_out1.reshape(BRANCH, D_MODEL))
    W_mkv = (W_mkv0, G_mkv1.reshape(D_MODEL, 2 * XQ))
    scale = (HD + ROPE) ** -0.5
    attn, lse = _attn_fwd(q_pad, k_pad, v_h, scale)
    k_b, v_b = _kv_prep(mem0, gm1, W_mkv[1], gk1, "kv_prep1")

    def merge_loss(x, mix, xq, gate, k, v, gq, wout, t):
        err = x + _dot(_merge(mix, xq, gate, k, v, gq).astype(BF16), wout) - t
        part = 0.5 * jnp.sum(jnp.sum(err * err, axis=-1, keepdims=True) * (1.0 / D_MODEL), axis=0, keepdims=True)
        return err * (1.0 / D_MODEL), jnp.broadcast_to(part, (1, HD))

    dx2, loss_part = _rowwise(
        "merge1_loss", merge_loss,
        [('r', x1), ('r', attn), ('r', xq_b), ('r', gate_b), ('c', k_b), ('c', v_b), ('c', gq1), ('c', W_out[1]),
         ('r', target)], [('r', (L, D_MODEL), F32), ('a', (1, HD), F32)], nblk)

    dattn, dxq_b, dgate_b, o_b, g_b, dk_b, dv_b, dgq1 = _backward_merge(
        dx2, attn, 'r', xq_b, gate_b, k_b, v_b, gq1, W_out[1], "merge1_bwd", nb_big)
    dgm1, dW_mkv1, dgk1 = _kv_prep_bwd(mem0, gm1, W_mkv[1], gk1, dk_b, dv_b, "kv_prep1_bwd")
    dW_out1 = _matmul_tn(o_b, g_b, "dw_out1")
    dq_pad, dk_pad, dv_h = _attn_bwd(q_pad, k_pad, v_h, attn, lse, dattn, scale)

    def qkv_bwd(c_q, c_kv, krp, tc, ts1, ts2, dqp, dkp, dv, gql, gkvl, wq, wkv, gqn, gkn, gqr, gkr):
        cqn, vjp_qn = jax.vjp(lambda a, b: _rms(a, b, Q_LORA), c_q, gql)
        ckvn, vjp_kvn = jax.vjp(lambda a, b: _rms(a, b, KV_LORA), c_kv, gkvl)
        cqn16 = cqn.astype(BF16)
        ckvn16 = ckvn.astype(BF16)
        q = _dot_nt(cqn16, wq)
        kv = _mm_slots(ckvn16, wkv)
        _, vjp_q = jax.vjp(lambda n, r, a, b: _q_post(n, r, a, b, tc, ts1, ts2), *_q_chunks(q), gqn, gqr)
        dnope, drope, dgqn, dgqr = vjp_q(dqp.astype(F32))
        dq = jnp.concatenate(dnope + drope, axis=-1)
        _, vjp_kv = jax.vjp(lambda n, v, k, a, b: _kv_post(n, v, k, a, b, tc, ts1, ts2), *_kv_chunks(kv), krp, gkn,
                            gkr)
        dkn, dvals, dkrp, dgkn, dgkr = vjp_kv((dkp.astype(F32), dv.astype(F32)))
        dkv = jnp.concatenate([x for pair in zip(dkn, dvals) for x in pair], axis=-1)
        dq16 = dq.astype(BF16)
        dkv16 = dkv.astype(BF16)
        dc_q, dgql = vjp_qn(_dot(dq16, wq))
        dc_kv, dgkvl = vjp_kvn(_mm_slots_nt(dkv16, wkv))
        return dc_q, dc_kv, dkrp, cqn16, dq16, ckvn16, dkv16, dgql, dgkvl, dgqn, dgkn, dgqr, dgkr

    (dc_q, dc_kv, dkrp, cqn16, dq16, ckvn16, dkv16, dgql, dgkvl, dgqn, dgkn, dgqr, dgkr) = _rowwise(
        "mla_qkv_bwd", qkv_bwd,
        [('r', c_q), ('r', c_kv), ('r', krp), ('r', tc), ('r', ts1), ('r', ts2), ('r', dq_pad), ('r', dk_pad),
         ('r', dv_h)] + qkv_consts,
        [('r', (L, Q_LORA), BF16), ('r', (L, KV_LORA), BF16), ('r', (L, HD), BF16), ('r', (L, Q_LORA), BF16),
         ('t', (2 * PRIM, L), BF16), ('t', (KV_LORA, L), BF16), ('r', (L, 2 * PRIM), BF16),
         ('a', (1, Q_LORA), F32), ('a', (1, KV_LORA), F32), ('a', (1, HD), F32), ('a', (1, HD), F32),
         ('a', (1, HD), F32), ('a', (1, HD), F32)], nb_big)
    dW_q = _matmul_tn(dq16, cqn16, "dw_uq")
    dW_kv = _matmul_tn_slots(ckvn16, dkv16, "dw_ukv")

    def in_bwd(x, dres, g, w, *dparts):
        dproj = jnp.concatenate(dparts, axis=-1).astype(BF16)
        xn, vjp = jax.vjp(lambda a, b: _rms(a, b, D_MODEL), x, g)
        dx, dg = vjp(_mm_slots_nt(dproj, w) if w.ndim == 3 else _dot(dproj, w))
        return dx + dres, xn, dproj, dg

    dx1, xn1, dproj1, dln1 = _rowwise(
        "mla_in_bwd", in_bwd,
        [('r', x1), ('r', dx2), ('c', ln1), ('c', W_in_mla), ('r', dc_q), ('r', dc_kv), ('r', dxq_b), ('r', dgate_b),
         ('r', dkrp)],
        [('r', (L, D_MODEL), F32), ('r', (L, D_MODEL), BF16), ('t', (_MLA_IN_PAD, L), BF16), ('a', (1, D_MODEL), F32)],
        nblk)
    dW_in_mla = _matmul_tn(dproj1, xn1, "dw_mla_in")

    grads1 = [dW_out1.reshape(N_DEV, 256, D_MODEL), dW_mkv1.reshape(N_DEV, 128, 2 * XQ),
              _mla_in_rows_back(dW_in_mla).reshape(N_DEV, 424, D_MODEL),
              _uq_rows_back(dW_q).reshape(N_DEV, 288, Q_LORA), dW_kv]
    (dy2, dxq_a, dgate_a, o_a, g_a, dk_a, dv_a, dgq0), pair1 = _backward_merge(
        dx1, y2, 'r', xq_a, gate_a, k_a, v_a, gq0, W_out[0], "merge0_bwd", nb_big, host=_plan_pair(grads1))
    dgm0, dW_mkv0, dgk0 = _kv_prep_bwd(mem0, gm0, W_mkv[0], gk0, dk_a, dv_a, "kv_prep0_bwd")
    dW_out0 = _matmul_tn(o_a, g_a, "dw_out0")
    t1 = list(_pair_add(grads1, pair1, "rs_add_layer1"))

    def glu_bwd(y, z, dy2, w):
        h, vjp_h = jax.vjp(_gelu, y)
        _, vjp_z = jax.vjp(lambda a, b: a * _sigmoid(b), z[:, :PRIM], z[:, PRIM:])
        dz16 = jnp.concatenate(vjp_z(dy2), axis=-1).astype(BF16)
        return vjp_h(_mm_slots_nt(dz16, w))[0], h.astype(BF16), dz16

    grads0 = [dW_out0.reshape(N_DEV, 256, D_MODEL), dW_mkv0.reshape(N_DEV, 128, 2 * XQ)]
    (dy_s5, h16, dz16), glu_hosted = _rowwise(
        "s5_glu_bwd", glu_bwd, [('r', y_s5), ('r', z_glu), ('r', dy2), ('c', W_glu)],
        [('r', (L, PRIM), F32), ('t', (PRIM, L), BF16), ('r', (L, 2 * PRIM), BF16)], nb_big,
        host=_combine(_plan_chips(t1[2:3]), _plan_pair(grads0)))
    recv_in_mla, pair0 = glu_hosted[:1], glu_hosted[1:]
    dW_glu = _matmul_tn_slots(h16, dz16, "dw_glu")
    t0 = list(_pair_add(grads0 + [dW_glu], pair0 + list(_exchange_call(_plan_pair([dW_glu]), "rs_pair_glu")),
                        "rs_add_layer0"))
    (du_s5, dbc, dcc, dd, dar, dai), recv_rest = _s5_bwd(u_s5, dy_s5, s5_carry, bm, cm, a_r2, a_i2, s5_d,
                                                        cmask, rmat, host=_plan_chips(t1[:2] + t1[3:] + t0))
    early_recv = recv_rest[:2] + recv_in_mla + recv_rest[2:]
    dbc4 = dbc.reshape(S5_G, S5_C, 2, S5_P)
    dcc4 = dcc.reshape(S5_G, S5_C, 2, S5_P)
    dlr, dli, dls, dbtr, dbti = _s5_params_bwd(
        lr3, li3, ls3, btr, bti, dar.reshape(S5_G, 1, S5_P), dai.reshape(S5_G, 1, S5_P), dbc4[:, :, 0], dbc4[:, :, 1])

    small_part = {
        "ln_gain": jnp.concatenate([jnp.zeros_like(dln1), dln1]), "mem_norm": jnp.concatenate([dgm0, dgm1]),
        "xq_norm": jnp.concatenate([dgq0, dgq1]), "xk_norm": jnp.concatenate([dgk0, dgk1]),
        "s5_lambda_re": dlr, "s5_lambda_im": dli, "s5_log_step": dls,
        "s5_b_re": jnp.swapaxes(dbtr, 1, 2), "s5_b_im": jnp.swapaxes(dbti, 1, 2),
        "s5_c_re": dcc4[:, :, 0], "s5_c_im": -dcc4[:, :, 1], "s5_d": dd,
        "mla_q_lora_norm": dgql, "mla_kv_lora_norm": dgkvl, "mla_q_nope_norm": dgqn, "mla_k_nope_norm": dgkn,
        "mla_q_rope_norm": dgqr[:, :ROPE], "mla_k_rope_norm": dgkr[:, :ROPE],
    }
    loss8 = jnp.pad(loss_part, ((0, 7), (0, 0)))
    (dx0, xn0, dproj0, dln0), (small_gath, loss_g) = _rowwise(
        "s5_in_bwd", in_bwd,
        [('r', x0), ('r', dx1), ('c', ln0), ('c', W_in_s5), ('r', du_s5), ('r', dxq_a),
         ('r', dgate_a)],
        [('r', (L, D_MODEL), F32), ('t', (D_MODEL, L), BF16), ('r', (L, 2 * BRANCH), BF16), ('a', (1, D_MODEL), F32)],
        nblk, host=_plan_all_gather([_pack_small(small_part).astype(BF16), loss8]))
    dW_in_s5 = _matmul_tn_slots(xn0, dproj0, "dw_s5_in")

    late = [dW_in_s5]
    late_t = _pair_add(late, list(_exchange_call(_plan_pair(late), "rs_pair_late")), "rs_add_late")
    owners = [("w_out", 1), ("w_mem_kv", 1), ("mla_w_in", 0), ("mla_w_uq", 0), ("mla_w_ukv", 0), ("w_out", 0),
              ("w_mem_kv", 0), ("s5_w_glu", 0)]
    flipped = ("mla_w_in", "mla_w_uq")

    def shard(d, n, i):
        return jnp.transpose(d[n][i]) if n in flipped else d[n][i]

    upd, (late_recv, ln0_gath) = _updates_call(
        early_recv, [shard(weights, n, i) for n, i in owners], [shard(m_in, n, i) for n, i in owners],
        [shard(v_in, n, i) for n, i in owners], "update_early",
        host=_combine(_plan_chips(late_t), _plan_all_gather([jnp.pad(dln0, ((0, 7), (0, 0)))])))
    owners.append(("s5_w_in", 0))
    upd.append(_sum_adamw(late_recv, s5_w_in[0], m_s5_w_in[0], v_s5_w_in[0], "update_s5_w_in"))
    grads, delta, new_m, new_v = {}, {}, {}, {}
    for n in _BIG:
        parts = [u for u, (o, _) in sorted(zip(upd, owners), key=lambda t: t[1][1]) if o == n]
        if n in flipped:
            grads[n], delta[n], new_m[n], new_v[n] = (jnp.transpose(parts[0][j])[None] for j in range(4))
        else:
            grads[n], delta[n], new_m[n], new_v[n] = (jnp.stack([p[j] for p in parts]) for j in range(4))

    gs, loss_sum = _small_sum(small_gath, loss_g, ln0_gath, "small_sum")
    loss = loss_sum[0, 0]
    for n, _ in _SMALL:
        shape = weights[n].shape
        if n == "mla_q_lora_norm":
            grads[n] = lax.dynamic_slice(_unpack_small(gs, n, (Q_LORA,)), (me * 64,), (64,)).reshape(shape)
        elif n == "mla_kv_lora_norm":
            grads[n] = lax.dynamic_slice(_unpack_small(gs, n, (KV_LORA,)), (me * 32,), (32,)).reshape(shape)
        else:
            grads[n] = _unpack_small(gs, n, shape)

    def own(n, a):
        if a.ndim == 4:
            a = jnp.transpose(a, (0, 2, 3, 1))
        elif a.ndim == 3:
            a = jnp.transpose(a, (0, 2, 1))
        return a.reshape(a.shape[1:]) if a.ndim >= 3 else a

    def back(n, a):
        shape = weights[n].shape
        if len(shape) == 4:
            return jnp.transpose(a.reshape((1,) + a.shape), (0, 3, 1, 2))
        if len(shape) == 3:
            return jnp.transpose(a.reshape((1,) + a.shape), (0, 2, 1))
        return a.reshape(shape)

    wide = ("s5_b_re", "s5_b_im", "s5_c_re", "s5_c_im")
    for names, nb, call in (([n for n, _ in _SMALL if n not in wide], 1, "update_small"), (wide, 4, "update_s5_bc")):
        res = _adamw_multi([own(n, weights[n]) for n in names], [own(n, grads[n]) for n in names],
                           [own(n, m_in[n]) for n in names], [own(n, v_in[n]) for n in names], call, nb)
        for n, (dl, m2, v2) in zip(names, res):
            delta[n], new_m[n], new_v[n] = back(n, dl), back(n, m2), back(n, v2)
    return (loss, dx0[None], *[grads[n] for n in _WEIGHTS], *[delta[n] for n in _WEIGHTS],
            *[new_m[n] for n in _WEIGHTS], *[new_v[n] for n in _WEIGHTS])
```

```python
import functools
import math

import numpy as np
import jax
import jax.numpy as jnp
from jax import lax
from jax.experimental import pallas as pl
from jax.experimental.pallas import tpu as pltpu

F32 = jnp.float32
BF16 = jnp.bfloat16
EPS = 1e-6
NEG = float(np.finfo(np.float32).min)
MESH = pl.DeviceIdType.MESH

N_DEV = 8
D_MODEL = 1024
MEM_LEN = 256
XQ = 512
PRIM = 1536
BRANCH = 2048
X_HEADS = 4
HD = 128
S5_G = 96
S5_P = 64
S5_C = 16
S5_GB = 8
S5_W = S5_GB * S5_P
MLA_H = 12
ROPE = 64
Q_LORA = 512
KV_LORA = 256
ROPE_THETA = 10000.0

ADAM_LR = 0.001
ADAM_B1 = 0.9
ADAM_B2 = 0.999
ADAM_EPS = 1e-08
ADAM_WD = 0.01
ADAM_STEP = 10

VMEM_LIMIT = 56 * 1024 * 1024


def _dot(a, b):
    return jnp.dot(a, b, preferred_element_type=F32)


def _dot_nt(a, b):
    return lax.dot_general(a, b, (((1,), (1,)), ((), ())), preferred_element_type=F32)


def _dot_tn(a, b):
    return lax.dot_general(a, b, (((0,), (0,)), ((), ())), preferred_element_type=F32)


@jax.custom_vjp
def _mm(a, b):
    return _dot(a.astype(BF16), b.astype(BF16))


def _mm_fwd(a, b):
    return _mm(a, b), (a, b)


def _mm_bwd(res, g):
    a, b = res
    gb = g.astype(BF16)
    return _dot_nt(gb, b.astype(BF16)).astype(a.dtype), _dot_tn(a.astype(BF16), gb).astype(b.dtype)


_mm.defvjp(_mm_fwd, _mm_bwd)


@jax.custom_vjp
def _mm_nt(a, b):
    return _dot_nt(a.astype(BF16), b.astype(BF16))


def _mm_nt_fwd(a, b):
    return _mm_nt(a, b), (a, b)


def _mm_nt_bwd(res, g):
    a, b = res
    gb = g.astype(BF16)
    return _dot(gb, b.astype(BF16)).astype(a.dtype), _dot_tn(gb, a.astype(BF16)).astype(b.dtype)


_mm_nt.defvjp(_mm_nt_fwd, _mm_nt_bwd)


@jax.custom_vjp
def _softmax(s):
    m = jnp.max(s, axis=-1, keepdims=True)
    e = jnp.exp(s - m)
    return e / jnp.sum(e, axis=-1, keepdims=True)


def _softmax_fwd(s):
    p = _softmax(s)
    return p, p


def _softmax_bwd(p, g):
    return (p * (g - jnp.sum(p * g, axis=-1, keepdims=True)),)


_softmax.defvjp(_softmax_fwd, _softmax_bwd)


def _rms(x, g, n):
    ms = jnp.sum(x * x, axis=-1, keepdims=True) * (1.0 / n)
    return x * lax.rsqrt(ms + EPS) * g


def _sigmoid(x):
    return 1.0 / (1.0 + jnp.exp(-x))


def _silu(x):
    return x * _sigmoid(x)


def _gelu(x):
    c = math.sqrt(2.0 / math.pi)
    return 0.5 * x * (1.0 + jnp.tanh(c * (x + 0.044715 * (x * x * x))))


@jax.custom_vjp
def _rot(x, c, s1, s2):
    return x * c + pltpu.roll(x, 96, 1) * s1 + pltpu.roll(x, 32, 1) * s2


def _rot_fwd(x, c, s1, s2):
    return _rot(x, c, s1, s2), (c, s1, s2)


def _rot_bwd(res, g):
    c, s1, s2 = res
    dx = g * c + pltpu.roll(g * s1, 32, 1) + pltpu.roll(g * s2, 96, 1)
    return dx, jnp.zeros_like(c), jnp.zeros_like(s1), jnp.zeros_like(s2)


_rot.defvjp(_rot_fwd, _rot_bwd)


def _mem_attn(xq, k, v, gq):
    outs = []
    for h in range(X_HEADS):
        sl = slice(HD * h, HD * (h + 1))
        q = _rms(xq[:, sl], gq, HD)
        p = _softmax(_mm_nt(q, k[:, sl]) * (HD ** -0.5))
        outs.append(_mm(p, v[:, sl]))
    return jnp.concatenate(outs, axis=-1)


def _merge(mix, xq, gate, k, v, gq):
    return jnp.concatenate([mix, _mem_attn(xq, k, v, gq)], axis=-1) * _silu(gate)


def _q_chunks(q):
    return ([q[:, HD * h:HD * (h + 1)] for h in range(MLA_H)],
            [q[:, PRIM + HD * h:PRIM + HD * (h + 1)] for h in range(MLA_H)])


def _q_post(nope, rope, gqn, gqr, c, s1, s2):
    pieces = []
    for qn, qr in zip(nope, rope):
        pieces.append(_rms(qn, gqn, HD))
        pieces.append(_rot(_rms(qr, gqr, ROPE), c, s1, s2))
    return jnp.concatenate(pieces, axis=-1)


def _kv_chunks(kv):
    return ([kv[:, 2 * HD * h:2 * HD * h + HD] for h in range(MLA_H)],
            [kv[:, 2 * HD * h + HD:2 * HD * (h + 1)] for h in range(MLA_H)])


def _kv_post(kn, vals, krp, gkn, gkr, c, s1, s2):
    kr = _rot(_rms(krp, gkr, ROPE), c, s1, s2)
    pieces = []
    for k in kn:
        pieces.append(_rms(k, gkn, HD))
        pieces.append(kr)
    return jnp.concatenate(pieces, axis=-1), jnp.concatenate(vals, axis=-1)


def _rowwise(name, fn, ins, outs, nblk, host=None):
    n_in = len(ins)

    def spec(kind, shape):
        if kind == 'r':
            return pl.BlockSpec((shape[0] // nblk, shape[1]), lambda i: (i, 0))
        if kind == 't':
            return pl.BlockSpec((shape[0], shape[1] // nblk), lambda i: (0, i))
        zeros = (0,) * len(shape)
        return pl.BlockSpec(tuple(shape), lambda i: zeros)

    def body(*refs):
        i = pl.program_id(0)
        res = fn(*[r[...] for r in refs[:n_in]])
        for (kind, _, _), ref, val in zip(outs, refs[n_in:], res):
            if kind == 'a':
                @pl.when(i == 0)
                def _():
                    ref[...] = jnp.zeros_like(ref)
                ref[...] += val.astype(ref.dtype)
            elif kind == 't':
                ref[...] = val.astype(F32).T.astype(ref.dtype)
            else:
                ref[...] = val.astype(ref.dtype)

    res, hosted = _hosting_call(
        body, name, nblk, host, [a for _, a in ins], [spec(k, a.shape) for k, a in ins],
        [jax.ShapeDtypeStruct(tuple(s), d) for _, s, d in outs], [spec(k, s) for k, s, _ in outs], [])
    return res if host is None else (res, hosted)


def _matmul_tn(at, g, name, out_dtype=BF16):
    K, L = at.shape
    N = g.shape[1]
    tn = next(t for t in (512, 384, 256, 128) if N % t == 0)

    def body(a_ref, g_ref, o_ref):
        o_ref[...] = _dot(a_ref[...], g_ref[...]).astype(o_ref.dtype)

    return pl.pallas_call(
        body, name=name, grid=(N // tn,),
        in_specs=[pl.BlockSpec((K, L), lambda n: (0, 0)), pl.BlockSpec((L, tn), lambda n: (0, n))],
        out_specs=pl.BlockSpec((K, tn), lambda n: (0, n)),
        out_shape=jax.ShapeDtypeStruct((K, N), out_dtype),
        compiler_params=pltpu.CompilerParams(dimension_semantics=("arbitrary",), vmem_limit_bytes=VMEM_LIMIT),
    )(at, g)


def _matmul_tn_slots(at, g, name, host=None):
    K, L = at.shape
    n = g.shape[1] // N_DEV

    def body(a_ref, g_ref, o_ref):
        o_ref[...] = _dot(a_ref[...], g_ref[...]).astype(o_ref.dtype)

    res, hosted = _hosting_call(
        body, name, N_DEV, host, [at, g],
        [pl.BlockSpec((K, L), lambda d: (0, 0)), pl.BlockSpec((L, n), lambda d: (0, d))],
        [jax.ShapeDtypeStruct((N_DEV, K, n), BF16)], [pl.BlockSpec((None, K, n), lambda d: (d, 0, 0))], [])
    return res[0] if host is None else (res[0], hosted)


def _mm_slots(a16, w):
    return jnp.concatenate([_dot(a16, w[d]) for d in range(N_DEV)], axis=-1)


def _mm_slots_nt(g16, w):
    n = w.shape[2]
    out = _dot_nt(g16[:, 0:n], w[0])
    for d in range(1, N_DEV):
        out = out + _dot_nt(g16[:, d * n:(d + 1) * n], w[d])
    return out


class _Exchange:
    def __init__(self, ins, outs, scratch, start, finish):
        self.ins, self.outs, self.scratch, self.start, self.finish = ins, outs, scratch, start, finish


def _xyc():
    return lax.axis_index("x"), lax.axis_index("y"), lax.axis_index("c")


def _plan_all_gather(xs):
    n = len(xs)

    def build(x_refs, out_refs, sems):
        send_sems, recv_sems, local_sems = sems
        x, y, c = _xyc()

        def copies(k, block, to, own=False):
            slot = 4 * block[0] + 2 * block[1] + block[2]
            return [pltpu.make_async_remote_copy(
                src_ref=x_refs[a] if own else out_refs[a].at[slot], dst_ref=out_refs[a].at[slot],
                send_sem=send_sems.at[k * n + a], recv_sem=recv_sems.at[k * n + a], device_id=to,
                device_id_type=MESH) for a in range(n)]

        mine = [pltpu.make_async_copy(x_refs[a], out_refs[a].at[4 * x + 2 * y + c], local_sems.at[a])
                for a in range(n)]
        return copies, mine, (x, y, c), [(1 - x, y), (x, 1 - y), (1 - x, 1 - y)]

    def first_copies(copies, me, chips):
        x, y, c = me
        first = copies(0, me, (x, y, 1 - c), own=True)
        for j, chip in enumerate(chips):
            first += copies(1 + j, me, (*chip, c), own=True)
        return first

    def start(x_refs, out_refs, sems):
        copies, mine, me, chips = build(x_refs, out_refs, sems)
        for cp in mine + first_copies(copies, me, chips):
            cp.start()

    def finish(x_refs, out_refs, sems):
        copies, mine, me, chips = build(x_refs, out_refs, sems)
        x, y, c = me
        passed = []
        for j, chip in enumerate(chips):
            for cp in copies(1 + j, (*chip, c), me):
                cp.wait_recv()
            fwd = copies(4 + j, (*chip, c), (x, y, 1 - c))
            for cp in fwd:
                cp.start()
            passed += fwd
        for cp in copies(0, (x, y, 1 - c), me):
            cp.wait_recv()
        for j, chip in enumerate(chips):
            for cp in copies(4 + j, (*chip, 1 - c), me):
                cp.wait_recv()
        for cp in first_copies(copies, me, chips) + passed:
            cp.wait_send()
        for cp in mine:
            cp.wait()

    return _Exchange(list(xs), [jax.ShapeDtypeStruct((N_DEV,) + a.shape, a.dtype) for a in xs],
                     [pltpu.SemaphoreType.DMA((7 * n,)), pltpu.SemaphoreType.DMA((7 * n,)),
                      pltpu.SemaphoreType.DMA((n,))], start, finish)


_CHIPS = ((0, 0), (0, 1), (1, 0), (1, 1))


def _plan_pair(sends):
    n = len(sends)

    def build(s_refs, o_refs, sems):
        send_sems, recv_sems = sems
        x, y, c = _xyc()
        return [pltpu.make_async_remote_copy(
            src_ref=s_refs[a].at[4 * px + 2 * py + 1 - c], dst_ref=o_refs[a].at[j],
            send_sem=send_sems.at[j * n + a], recv_sem=recv_sems.at[j * n + a], device_id=(x, y, 1 - c),
            device_id_type=MESH) for j, (px, py) in enumerate(_CHIPS) for a in range(n)]

    def start(s_refs, o_refs, sems):
        for cp in build(s_refs, o_refs, sems):
            cp.start()

    def finish(s_refs, o_refs, sems):
        for cp in build(s_refs, o_refs, sems):
            cp.wait_recv()
            cp.wait_send()

    return _Exchange(list(sends), [jax.ShapeDtypeStruct((4,) + a.shape[1:], a.dtype) for a in sends],
                     [pltpu.SemaphoreType.DMA((4 * n,)), pltpu.SemaphoreType.DMA((4 * n,))], start, finish)


def _plan_chips(ts):
    n = len(ts)
    flips = ((1, 0), (0, 1), (1, 1))

    def build(t_refs, o_refs, sems):
        send_sems, recv_sems, local_sems = sems
        x, y, c = _xyc()
        mine = 2 * x + y
        local = [pltpu.make_async_copy(t_refs[a].at[mine], o_refs[a].at[mine], local_sems.at[a]) for a in range(n)]
        remote = []
        for k, (fx, fy) in enumerate(flips):
            px = 1 - x if fx else x
            py = 1 - y if fy else y
            remote += [pltpu.make_async_remote_copy(
                src_ref=t_refs[a].at[2 * px + py], dst_ref=o_refs[a].at[mine],
                send_sem=send_sems.at[k * n + a], recv_sem=recv_sems.at[k * n + a], device_id=(px, py, c),
                device_id_type=MESH) for a in range(n)]
        return local, remote

    def start(t_refs, o_refs, sems):
        local, remote = build(t_refs, o_refs, sems)
        for cp in local + remote:
            cp.start()

    def finish(t_refs, o_refs, sems):
        local, remote = build(t_refs, o_refs, sems)
        for cp in remote:
            cp.wait_recv()
        for cp in remote:
            cp.wait_send()
        for cp in local:
            cp.wait()

    return _Exchange(list(ts), [jax.ShapeDtypeStruct(a.shape, a.dtype) for a in ts],
                     [pltpu.SemaphoreType.DMA((3 * n,)), pltpu.SemaphoreType.DMA((3 * n,)),
                      pltpu.SemaphoreType.DMA((n,))], start, finish)


def _combine(*plans):
    def parts(refs, attr):
        out, at = [], 0
        for p in plans:
            n = len(getattr(p, attr))
            out.append(refs[at:at + n])
            at += n
        return out

    def run(half):
        def go(ins, outs, sems):
            for p, a, o, s in zip(plans, parts(ins, "ins"), parts(outs, "outs"), parts(sems, "scratch")):
                getattr(p, half)(a, o, s)
        return go

    return _Exchange(sum((p.ins for p in plans), []), sum((p.outs for p in plans), []),
                     sum((p.scratch for p in plans), []), run("start"), run("finish"))


def _exchange_call(plan, name):
    n = len(plan.ins)

    def body(*refs):
        ins, outs, sems = refs[:n], refs[n:2 * n], refs[2 * n:]
        plan.start(ins, outs, sems)
        plan.finish(ins, outs, sems)

    return pl.pallas_call(
        body, name=name, out_shape=plan.outs,
        in_specs=[pl.BlockSpec(memory_space=pl.ANY)] * n, out_specs=[pl.BlockSpec(memory_space=pl.ANY)] * n,
        scratch_shapes=plan.scratch,
    )(*plan.ins)


def _slab_spec(lead, rows, cols, nb):
    if rows % (nb * 16) == 0:
        return pl.BlockSpec((lead, rows // nb, cols), lambda i: (0, i, 0))
    if cols % (nb * 128) == 0:
        return pl.BlockSpec((lead, rows, cols // nb), lambda i: (0, 0, i))
    return pl.BlockSpec((lead, rows, cols), lambda i: (0, 0, 0))


def _slab_spec2(rows, cols, nb):
    if rows % (nb * 16) == 0:
        return pl.BlockSpec((rows // nb, cols), lambda i: (i, 0))
    if cols % (nb * 128) == 0:
        return pl.BlockSpec((rows, cols // nb), lambda i: (0, i))
    return pl.BlockSpec((rows, cols), lambda i: (0, 0))


def _cast_call(arrays, name, host=None):
    n = len(arrays)
    nb = 8

    def body(*refs):
        for a in range(n):
            refs[n + a][...] = refs[a][...].astype(BF16)

    specs = [_slab_spec2(x.shape[0], x.shape[1], nb) for x in arrays]
    return _hosting_call(body, name, nb, host, list(arrays), specs,
                         [jax.ShapeDtypeStruct(x.shape, BF16) for x in arrays], specs, [])


def _pair_add(sends, fromsib, name):
    n = len(sends)
    nb = 8

    def body(*refs):
        c = lax.axis_index("c")
        for a in range(n):
            s_ref, f_ref, t_ref = refs[a], refs[n + a], refs[2 * n + a]
            for j in range(4):
                t_ref[j] = (s_ref[2 * j + c].astype(F32) + f_ref[j].astype(F32)).astype(t_ref.dtype)

    def spec(a, lead):
        return _slab_spec(lead, a.shape[1], a.shape[2], nb)

    return pl.pallas_call(
        body, name=name, grid=(nb,),
        in_specs=[spec(a, N_DEV) for a in sends] + [spec(a, 4) for a in fromsib],
        out_specs=[spec(a, 4) for a in fromsib],
        out_shape=[jax.ShapeDtypeStruct(a.shape, a.dtype) for a in fromsib],
        compiler_params=pltpu.CompilerParams(dimension_semantics=("arbitrary",), vmem_limit_bytes=VMEM_LIMIT),
    )(*sends, *fromsib)


def _adamw_vals(w, g, m, v):
    m2 = ADAM_B1 * m + (1.0 - ADAM_B1) * g
    v2 = ADAM_B2 * v + (1.0 - ADAM_B2) * (g * g)
    m_hat = m2 / (1.0 - ADAM_B1 ** ADAM_STEP)
    v_hat = v2 / (1.0 - ADAM_B2 ** ADAM_STEP)
    delta = -ADAM_LR * (m_hat / (jnp.sqrt(v_hat) + ADAM_EPS) + ADAM_WD * w)
    return delta, m2, v2


def _sum_adamw(recv, w, m, v, name):
    R, C = w.shape
    ns = recv.shape[0]
    br = next((t for t in (256, 128, 64, 32, 16) if R % t == 0), R)

    def body(r_ref, w_ref, m_ref, v_ref, g_ref, d_ref, m2_ref, v2_ref):
        g = r_ref[0].astype(F32)
        for d in range(1, ns):
            g = g + r_ref[d].astype(F32)
        dl, m2, v2 = _adamw_vals(w_ref[...], g, m_ref[...], v_ref[...])
        g_ref[...] = g
        d_ref[...] = dl
        m2_ref[...] = m2
        v2_ref[...] = v2

    spec = pl.BlockSpec((br, C), lambda i: (i, 0))
    return pl.pallas_call(
        body, name=name, grid=(R // br,),
        in_specs=[pl.BlockSpec((ns, br, C), lambda i: (0, i, 0)), spec, spec, spec], out_specs=[spec] * 4,
        out_shape=[jax.ShapeDtypeStruct((R, C), F32)] * 4,
        compiler_params=pltpu.CompilerParams(dimension_semantics=("arbitrary",)),
    )(recv, w, m, v)


def _updates_call(recvs, ws, ms, vs, name, host=None):
    n = len(recvs)
    nb = 8

    def body(*refs):
        for a in range(n):
            r_ref, w_ref, m_ref, v_ref = refs[a], refs[n + a], refs[2 * n + a], refs[3 * n + a]
            g_ref, d_ref, m2_ref, v2_ref = refs[4 * n + 4 * a:4 * n + 4 * a + 4]
            g = r_ref[0].astype(F32)
            for d in range(1, r_ref.shape[0]):
                g = g + r_ref[d].astype(F32)
            dl, m2, v2 = _adamw_vals(w_ref[...], g, m_ref[...], v_ref[...])
            g_ref[...] = g
            d_ref[...] = dl
            m2_ref[...] = m2
            v2_ref[...] = v2

    def spec3(r):
        return _slab_spec(r.shape[0], r.shape[1], r.shape[2], nb)

    def spec2(w):
        return _slab_spec2(w.shape[0], w.shape[1], nb)

    res, hosted = _hosting_call(
        body, name, nb, host, list(recvs) + list(ws) + list(ms) + list(vs),
        [spec3(r) for r in recvs] + [spec2(w) for w in ws] * 3,
        [jax.ShapeDtypeStruct(w.shape, F32) for w in ws for _ in range(4)],
        [spec2(w) for w in ws for _ in range(4)], [])
    return [res[4 * a:4 * a + 4] for a in range(n)], hosted


def _small_sum(gath, loss_g, row0_g, name):
    _, R, C = gath.shape
    br = R // 3

    def body(g_ref, l_ref, r_ref, go_ref, lo_ref):
        g = g_ref[0].astype(F32)
        lsum = l_ref[0]
        for d in range(1, N_DEV):
            g = g + g_ref[d].astype(F32)
            lsum = lsum + l_ref[d]
        go_ref[...] = g
        lo_ref[...] = lsum

        @pl.when(pl.program_id(0) == 0)
        def _():
            row0 = r_ref[0]
            for d in range(1, N_DEV):
                row0 = row0 + r_ref[d]
            go_ref[0:8, :] = go_ref[0:8, :] + jnp.where(lax.broadcasted_iota(jnp.int32, row0.shape, 0) == 0, row0, 0.0)

    return pl.pallas_call(
        body, name=name, grid=(R // br,),
        in_specs=[pl.BlockSpec((N_DEV, br, C), lambda i: (0, i, 0)),
                  pl.BlockSpec((N_DEV, 8, HD), lambda i: (0, 0, 0)), pl.BlockSpec((N_DEV, 8, C), lambda i: (0, 0, 0))],
        out_specs=[pl.BlockSpec((br, C), lambda i: (i, 0)), pl.BlockSpec((8, HD), lambda i: (0, 0))],
        out_shape=[jax.ShapeDtypeStruct((R, C), F32), jax.ShapeDtypeStruct((8, HD), F32)],
        compiler_params=pltpu.CompilerParams(dimension_semantics=("arbitrary",)),
    )(gath, loss_g, row0_g)


def _adamw_multi(ws, gs, ms, vs, name, nblk=1):
    n = len(ws)

    def body(*refs):
        for a in range(n):
            dl, m2, v2 = _adamw_vals(refs[a][...], refs[n + a][...], refs[2 * n + a][...], refs[3 * n + a][...])
            refs[4 * n + 3 * a][...] = dl
            refs[4 * n + 3 * a + 1][...] = m2
            refs[4 * n + 3 * a + 2][...] = v2

    def spec(x):
        rest = (0,) * (x.ndim - 1)
        return pl.BlockSpec((x.shape[0] // nblk,) + tuple(x.shape[1:]), lambda i: (i,) + rest)

    res = pl.pallas_call(
        body, name=name, grid=(nblk,),
        in_specs=[spec(w) for w in ws] * 4, out_specs=[spec(w) for w in ws for _ in range(3)],
        out_shape=[jax.ShapeDtypeStruct(w.shape, F32) for w in ws for _ in range(3)],
        compiler_params=pltpu.CompilerParams(dimension_semantics=("arbitrary",), vmem_limit_bytes=VMEM_LIMIT),
    )(*ws, *gs, *ms, *vs)
    return [res[3 * a:3 * a + 3] for a in range(n)]


def _s5_param_fn(lr, li, ls, btr, bti):
    step = jnp.exp(ls)
    er = jnp.exp(lr * step)
    ang = li * step
    ar = er * jnp.cos(ang)
    ai = er * jnp.sin(ang)
    nr = ar - 1.0
    den = lr * lr + li * li
    fr = (nr * lr + ai * li) / den
    fi = (ai * lr - nr * li) / den
    return ar, ai, fr * btr - fi * bti, fr * bti + fi * btr


def _s5_params(lr, li, ls, btr, bti, cre, cim):
    nb = S5_G // S5_GB
    GC = S5_GB * S5_C
    expand = jnp.asarray(np.tile(np.eye(S5_P, dtype=np.float32), (1, S5_GB)), BF16)
    own = jnp.asarray((np.arange(GC)[:, None] // S5_C == np.arange(S5_W)[None, :] // S5_P).astype(np.float32))

    def body(lr_ref, li_ref, ls_ref, br_ref, bi_ref, cr_ref, ci_ref, e_ref, own_ref, ar_ref, ai_ref, bm_ref, cm_ref):
        ar, ai, bbr, bbi = _s5_param_fn(lr_ref[...], li_ref[...], ls_ref[...], br_ref[...], bi_ref[...])
        ar_ref[...] = ar
        ai_ref[...] = ai

        def plane(x, n):
            rows = x[n * S5_GB:(n + 1) * S5_GB].reshape(GC, S5_P).astype(BF16)
            return _dot(rows, e_ref[...]) * own_ref[...]

        for n in range(nb):
            bm_ref[n] = jnp.concatenate([plane(bbr, n), plane(bbi, n)], axis=-1).astype(BF16)
            cm_ref[n] = jnp.concatenate([plane(cr_ref[...], n), -plane(ci_ref[...], n)], axis=-1).astype(BF16)

    sd = jax.ShapeDtypeStruct
    return pl.pallas_call(
        body, name="s5_params",
        out_shape=[sd(lr.shape, F32), sd(lr.shape, F32), sd((nb, GC, 2 * S5_W), BF16), sd((nb, GC, 2 * S5_W), BF16)],
        compiler_params=pltpu.CompilerParams(vmem_limit_bytes=VMEM_LIMIT),
    )(lr, li, ls, btr, bti, cre, cim, expand, own)


def _s5_params_bwd(lr, li, ls, btr, bti, dar, dai, dbbr, dbbi):
    def body(lr_ref, li_ref, ls_ref, br_ref, bi_ref, dar_ref, dai_ref, dbbr_ref, dbbi_ref,
             dlr_ref, dli_ref, dls_ref, dbr_ref, dbi_ref):
        _, vjp = jax.vjp(_s5_param_fn, lr_ref[...], li_ref[...], ls_ref[...], br_ref[...], bi_ref[...])
        dlr, dli, dls, dbr, dbi = vjp((dar_ref[...], dai_ref[...], dbbr_ref[...], dbbi_ref[...]))
        dlr_ref[...] = dlr
        dli_ref[...] = dli
        dls_ref[...] = dls
        dbr_ref[...] = dbr
        dbi_ref[...] = dbi

    sd = jax.ShapeDtypeStruct
    return pl.pallas_call(
        body, name="s5_params_bwd",
        out_shape=[sd(lr.shape, F32), sd(lr.shape, F32), sd(ls.shape, F32), sd(btr.shape, F32), sd(btr.shape, F32)],
    )(lr, li, ls, btr, bti, dar, dai, dbbr, dbbi)


def _cpow(ar, ai, n):
    assert n & (n - 1) == 0
    while n > 1:
        ar, ai = ar * ar - ai * ai, 2.0 * ar * ai
        n //= 2
    return ar, ai


def _scan(st, cr, ci, init, nk, reverse, store, prev=None):
    W = S5_W

    def advance(k, sr, si):
        rows = pl.ds(k * 8 if isinstance(k, int) else pl.multiple_of(k * 8, 8), 8)
        nsr = cr * sr - ci * si + st[rows, 0:W]
        nsi = cr * si + ci * sr + st[rows, W:2 * W]
        if store:
            st[rows, 0:W] = nsr
            st[rows, W:2 * W] = nsi
        return nsr, nsi

    if prev is None:
        return lax.fori_loop(0, nk, lambda j, c: advance(nk - 1 - j if reverse else j, c[0], c[1]), init, unroll=2)
    assert reverse

    def step(j, carry):
        k = nk - 1 - j
        nsr, nsi = advance(k, carry[0], carry[1])
        prows = pl.ds(pl.multiple_of((k - 1) * 8, 8), 8)
        pr = prev[prows, 0:W]
        pi = prev[prows, W:2 * W]
        return nsr, nsi, carry[2] + nsr * pr + nsi * pi, carry[3] + nsi * pr - nsr * pi

    carry = lax.fori_loop(0, nk - 1, step, init, unroll=2)
    nsr, nsi = advance(0, carry[0], carry[1])
    return nsr, nsi, carry[2], carry[3]


def _chain(fin, fr, fi, pr, pi, reverse):
    W = S5_W
    fin[:, 0:W] = fr
    fin[:, W:2 * W] = fi
    rowid = lax.broadcasted_iota(jnp.int32, (8, W), 0)
    cr = jnp.zeros((1, W), F32)
    ci = jnp.zeros((1, W), F32)
    init_r = jnp.zeros((8, W), F32)
    init_i = jnp.zeros((8, W), F32)
    for s in (range(7, -1, -1) if reverse else range(8)):
        init_r = jnp.where(rowid == s, cr, init_r)
        init_i = jnp.where(rowid == s, ci, init_i)
        lr = fin[s:s + 1, 0:W]
        li = fin[s:s + 1, W:2 * W]
        cr, ci = lr + pr * cr - pi * ci, li + pr * ci + pi * cr
    return init_r, init_i


def _full_scan(st, fin, ar, ai, nk, reverse, prev=None, carry_in=None, carry_out=None):
    W = S5_W
    cr = jnp.broadcast_to(ar, (8, W))
    ci = jnp.broadcast_to(-ai if reverse else ai, (8, W))
    z = jnp.zeros((8, W), F32)
    if carry_in is None:
        fr, fi = _scan(st, cr, ci, (z, z), nk, reverse, store=False)
        pr, pi = _cpow(ar, -ai if reverse else ai, nk)
        init = _chain(fin, fr, fi, pr, pi, reverse)
    else:
        init = (carry_in[:, 0:W], carry_in[:, W:2 * W])
    if carry_out is not None:
        carry_out[:, 0:W] = init[0]
        carry_out[:, W:2 * W] = init[1]
    if prev is None:
        return _scan(st, cr, ci, init, nk, reverse, store=True)
    return _scan(st, cr, ci, init + (z, z), nk, reverse, store=True, prev=prev)


def _s5_specs(L):
    W2 = 2 * S5_W
    GC = S5_GB * S5_C
    col = pl.BlockSpec((L, GC), lambda g: (0, g))
    vec = pl.BlockSpec((1, GC), lambda g: (0, g))
    avec = pl.BlockSpec((1, S5_W), lambda g: (0, g))
    bmat = pl.BlockSpec((None, GC, W2), lambda g: (g, 0, 0))
    cmat = pl.BlockSpec((None, W2, GC), lambda g: (g, 0, 0))
    return col, vec, avec, bmat, cmat


def _interleave(dst, src, nk):
    for s in range(8):
        dst[pl.ds(s, nk, stride=8), :] = src[s * nk:(s + 1) * nk, :]


def _deinterleave(dst, src, nk):
    for s in range(8):
        dst[s * nk:(s + 1) * nk, :] = src[pl.ds(s, nk, stride=8), :].astype(dst.dtype)


def _hosting_call(body, name, nsteps, host, ins, in_specs, outs, out_specs, scratch):
    grid = (nsteps,) if isinstance(nsteps, int) else tuple(nsteps)
    params = pltpu.CompilerParams(dimension_semantics=("arbitrary",) * len(grid), vmem_limit_bytes=VMEM_LIMIT)
    if host is None:
        res = pl.pallas_call(
            body, name=name, grid=grid, in_specs=in_specs, out_specs=out_specs, out_shape=outs,
            scratch_shapes=scratch, compiler_params=params,
        )(*ins)
        return list(res), []
    n_in, n_out, n_sc = len(ins), len(outs), len(scratch)
    h_in, h_out = len(host.ins), len(host.outs)

    def hosted(*refs):
        a = refs[:n_in]
        ha = refs[n_in:n_in + h_in]
        o = refs[n_in + h_in:n_in + h_in + n_out]
        ho = refs[n_in + h_in + n_out:n_in + h_in + n_out + h_out]
        sc = refs[n_in + h_in + n_out + h_out:n_in + h_in + n_out + h_out + n_sc]
        hs = refs[n_in + h_in + n_out + h_out + n_sc:]
        first = functools.reduce(jnp.logical_and, [pl.program_id(i) == 0 for i in range(len(grid))])
        last = functools.reduce(jnp.logical_and, [pl.program_id(i) == g - 1 for i, g in enumerate(grid)])

        @pl.when(first)
        def _():
            host.start(ha, ho, hs)

        body(*a, *o, *sc)

        @pl.when(last)
        def _():
            host.finish(ha, ho, hs)

    hbm = pl.BlockSpec(memory_space=pl.ANY)
    res = pl.pallas_call(
        hosted, name=name, grid=grid,
        in_specs=list(in_specs) + [hbm] * h_in, out_specs=list(out_specs) + [hbm] * h_out,
        out_shape=list(outs) + list(host.outs), scratch_shapes=list(scratch) + list(host.scratch),
        compiler_params=params,
    )(*ins, *host.ins)
    return list(res[:n_out]), list(res[n_out:])


def _s5_fwd(u, bm, cm, ar, ai, dvec, host=None):
    L = u.shape[0]
    nk = L // 8
    GC = S5_GB * S5_C
    nb = S5_G // S5_GB
    col, vec, avec, bmat, cmat = _s5_specs(L)

    def body(u_ref, b_ref, c_ref, ar_ref, ai_ref, d_ref, y_ref, carry_ref, st, fin, ui, yi):
        _interleave(ui, u_ref, nk)
        for r in range(8):
            rows = slice(r * nk, (r + 1) * nk)
            st[rows, :] = _dot(ui[rows, :].astype(BF16), b_ref[...])
        _full_scan(st, fin, ar_ref[...], ai_ref[...], nk, reverse=False, carry_out=carry_ref)
        for r in range(8):
            rows = slice(r * nk, (r + 1) * nk)
            yi[rows, :] = _dot_nt(st[rows, :].astype(BF16), c_ref[...]) + d_ref[...] * ui[rows, :]
        _deinterleave(y_ref, yi, nk)

    return _hosting_call(
        body, "s5_fwd", nb, host,
        [u, bm, cm, ar, ai, dvec], [col, bmat, bmat, avec, avec, vec],
        [jax.ShapeDtypeStruct(u.shape, F32), jax.ShapeDtypeStruct((nb * 8, 2 * S5_W), F32)],
        [col, pl.BlockSpec((8, 2 * S5_W), lambda g: (g, 0))],
        [pltpu.VMEM((L, 2 * S5_W), F32), pltpu.VMEM((8, 2 * S5_W), F32), pltpu.VMEM((L, GC), F32),
         pltpu.VMEM((L, GC), F32)])


def _s5_bwd(u, dy, carry, bm, cm, ar, ai, dvec, mask, rmat, host=None):
    L = u.shape[0]
    nk = L // 8
    W = S5_W
    GC = S5_GB * S5_C
    col, vec, avec, bmat, cmat = _s5_specs(L)
    hi = lax.Precision.HIGHEST

    def body(u_ref, dy_ref, carry_ref, b_ref, ct_ref, ar_ref, ai_ref, d_ref, mask_ref, r_ref,
             du_ref, db_ref, dc_ref, dd_ref, dar_ref, dai_ref, sa, sb, fin, ui, dyi, dui):
        ar = ar_ref[...]
        ai = ai_ref[...]
        _interleave(ui, u_ref, nk)
        _interleave(dyi, dy_ref, nk)
        for r in range(8):
            rows = slice(r * nk, (r + 1) * nk)
            sa[rows, :] = _dot(ui[rows, :].astype(BF16), b_ref[...])
            sb[rows, :] = _dot(dyi[rows, :].astype(BF16), ct_ref[...])
        _full_scan(sa, fin, ar, ai, nk, reverse=False, carry_in=carry_ref)
        gr, gi, accr, acci = _full_scan(sb, fin, ar, ai, nk, reverse=True, prev=sa)
        rowid = lax.broadcasted_iota(jnp.int32, (8, W), 0)
        last = pl.ds((nk - 1) * 8, 8)
        pr = jnp.where(rowid == 0, 0.0, pltpu.roll(sa[last, 0:W], 1, 0))
        pi = jnp.where(rowid == 0, 0.0, pltpu.roll(sa[last, W:2 * W], 1, 0))
        accr = accr + gr * pr + gi * pi
        acci = acci + gi * pr - gr * pi
        dar_ref[...] = jnp.sum(accr, axis=0, keepdims=True)
        dai_ref[...] = jnp.sum(acci, axis=0, keepdims=True)
        dbf = jnp.zeros((GC, 2 * W), F32)
        dcf = jnp.zeros((GC, 2 * W), F32)
        dd = jnp.zeros((1, GC), F32)
        for r in range(8):
            rows = slice(r * nk, (r + 1) * nk)
            ub = ui[rows, :]
            dyb = dyi[rows, :]
            gb = sb[rows, :].astype(BF16)
            dui[rows, :] = _dot_nt(gb, b_ref[...]) + d_ref[...] * dyb
            dbf = dbf + _dot_tn(ub.astype(BF16), gb)
            dcf = dcf + _dot_tn(dyb.astype(BF16), sa[rows, :].astype(BF16))
            dd = dd + jnp.sum(dyb * ub, axis=0, keepdims=True)
        db_ref[...] = jnp.dot(dbf * mask_ref[...], r_ref[...], precision=hi, preferred_element_type=F32)
        dc_ref[...] = jnp.dot(dcf * mask_ref[...], r_ref[...], precision=hi, preferred_element_type=F32)
        dd_ref[...] = dd
        _deinterleave(du_ref, dui, nk)

    cmp_spec = pl.BlockSpec((GC, 2 * S5_P), lambda g: (g, 0))
    whole = lambda shape: pl.BlockSpec(shape, lambda g: (0, 0))
    sd = jax.ShapeDtypeStruct
    return _hosting_call(
        body, "s5_bwd", S5_G // S5_GB, host,
        [u, dy, carry, bm, cm, ar, ai, dvec, mask, rmat],
        [col, col, pl.BlockSpec((8, 2 * W), lambda g: (g, 0)), bmat, bmat, avec, avec, vec, whole(mask.shape),
         whole(rmat.shape)],
        [sd(u.shape, BF16), sd((S5_G * S5_C, 2 * S5_P), F32), sd((S5_G * S5_C, 2 * S5_P), F32),
         sd((1, PRIM), F32), sd((1, S5_G * S5_P), F32), sd((1, S5_G * S5_P), F32)],
        [col, cmp_spec, cmp_spec, vec, avec, avec],
        [pltpu.VMEM((L, 2 * W), F32), pltpu.VMEM((L, 2 * W), F32), pltpu.VMEM((8, 2 * W), F32),
         pltpu.VMEM((L, GC), F32), pltpu.VMEM((L, GC), F32), pltpu.VMEM((L, GC), F32)])


def _s5_compact_consts():
    g_row = np.arange(S5_GB * S5_C) // S5_C
    col = np.arange(2 * S5_W)
    g_col = (col % S5_W) // S5_P
    mask = (g_row[:, None] == g_col[None, :]).astype(np.float32)
    tgt = (col // S5_W) * S5_P + col % S5_P
    rmat = (tgt[:, None] == np.arange(2 * S5_P)[None, :]).astype(np.float32)
    return jnp.asarray(mask), jnp.asarray(rmat)


def _attn_scores(q_ref, k_ref, qb, bq, scale):
    ext = (qb + 1) * bq
    s = _dot_nt(q_ref[qb * bq:ext, :], k_ref[0:ext, :]) * scale
    qpos = lax.broadcasted_iota(jnp.int32, (bq, bq), 0)
    kpos = lax.broadcasted_iota(jnp.int32, (bq, bq), 1)
    diag = jnp.where(kpos <= qpos, s[:, ext - bq:], NEG)
    return diag if qb == 0 else jnp.concatenate([s[:, :ext - bq], diag], axis=-1)


def _attn_fwd(qp, kp, v, scale):
    L = qp.shape[0]
    bq = min(256, L)

    def body(q_ref, k_ref, v_ref, o_ref, lse_ref):
        for qb in range(L // bq):
            rows = slice(qb * bq, (qb + 1) * bq)
            s = _attn_scores(q_ref, k_ref, qb, bq, scale)
            m = jnp.max(s, axis=-1, keepdims=True)
            e = jnp.exp(s - m)
            l = jnp.sum(e, axis=-1, keepdims=True)
            o_ref[rows, :] = _dot(e.astype(BF16), v_ref[0:(qb + 1) * bq, :]) / l
            lse_ref[rows, :] = jnp.broadcast_to(m + jnp.log(l), (bq, HD))

    blk = pl.BlockSpec((L, HD), lambda h: (0, h))
    wide = pl.BlockSpec((L, 2 * HD), lambda h: (0, h))
    return pl.pallas_call(
        body, name="mla_attn_fwd", grid=(MLA_H,),
        in_specs=[wide, wide, blk], out_specs=[blk, blk],
        out_shape=[jax.ShapeDtypeStruct((L, MLA_H * HD), F32)] * 2,
        compiler_params=pltpu.CompilerParams(dimension_semantics=("arbitrary",), vmem_limit_bytes=VMEM_LIMIT),
    )(qp, kp, v)


def _attn_bwd(qp, kp, v, o, lse, do, scale):
    L = qp.shape[0]
    bq = min(256, L)
    nq = L // bq

    def body(q_ref, k_ref, v_ref, o_ref, lse_ref, do_ref, dq_ref, dk_ref, dv_ref, dk_acc, dv_acc):
        dk_acc[...] = jnp.zeros_like(dk_acc)
        dv_acc[...] = jnp.zeros_like(dv_acc)
        for qb in range(nq):
            rows = slice(qb * bq, (qb + 1) * bq)
            ext = (qb + 1) * bq
            do = do_ref[rows, :]
            dob = do.astype(BF16)
            p = jnp.exp(_attn_scores(q_ref, k_ref, qb, bq, scale) - lse_ref[rows, 0:1])
            dp = _dot_nt(dob, v_ref[0:ext, :])
            dsum = jnp.sum(do * o_ref[rows, :], axis=-1, keepdims=True)
            ds = (p * (dp - dsum) * scale).astype(BF16)
            dq_ref[rows, :] = _dot(ds, k_ref[0:ext, :]).astype(dq_ref.dtype)
            dk_acc[0:ext, :] += _dot_tn(ds, q_ref[rows, :])
            dv_acc[0:ext, :] += _dot_tn(p.astype(BF16), dob)
        dk_ref[...] = dk_acc[...].astype(dk_ref.dtype)
        dv_ref[...] = dv_acc[...].astype(dv_ref.dtype)

    sd = jax.ShapeDtypeStruct
    blk = pl.BlockSpec((L, HD), lambda h: (0, h))
    wide = pl.BlockSpec((L, 2 * HD), lambda h: (0, h))
    return pl.pallas_call(
        body, name="mla_attn_bwd", grid=(MLA_H,),
        in_specs=[wide, wide, blk, blk, blk, blk], out_specs=[wide, wide, blk],
        out_shape=[sd((L, MLA_H * 2 * HD), BF16), sd((L, MLA_H * 2 * HD), BF16), sd((L, MLA_H * HD), BF16)],
        scratch_shapes=[pltpu.VMEM((L, 2 * HD), F32), pltpu.VMEM((L, HD), F32)],
        compiler_params=pltpu.CompilerParams(dimension_semantics=("arbitrary",), vmem_limit_bytes=VMEM_LIMIT),
    )(qp, kp, v, o, lse, do)


def _kv_fn(mem, gm, w, gk):
    kv = _mm(_rms(mem, gm, D_MODEL), w)
    k = jnp.concatenate([_rms(kv[:, HD * h:HD * (h + 1)], gk, HD) for h in range(X_HEADS)], axis=-1)
    return k, kv[:, XQ:]


def _kv_prep(mem, gm, w, gk, name):
    def fn(mem, gm, w, gk):
        return _kv_fn(mem, gm, w, gk)
    M = mem.shape[0]
    return _rowwise(name, fn, [('c', mem), ('c', gm), ('c', w), ('c', gk)],
                    [('c', (M, XQ), F32), ('c', (M, XQ), F32)], 1)


def _kv_prep_bwd(mem, gm, w, gk, dk, dv, name):
    def fn(mem, gm, w, gk, dk, dv):
        _, vjp = jax.vjp(lambda a, b, c: _kv_fn(mem, a, b, c), gm, w, gk)
        return vjp((dk, dv))
    return _rowwise(name, fn, [('c', mem), ('c', gm), ('c', w), ('c', gk), ('c', dk), ('c', dv)],
                    [('c', gm.shape, F32), ('c', w.shape, BF16), ('c', gk.shape, F32)], 1)


def _forward_merge(x, mix, mix_kind, xq, gate, k, v, gq, wout, name, nblk, host=None):
    def fn(x, mix, xq, gate, k, v, gq, wout):
        o = _merge(mix, xq, gate, k, v, gq)
        return (x + _dot(o.astype(BF16), wout),)
    L = x.shape[0]
    out = _rowwise(name, fn, [('r', x), (mix_kind, mix), ('r', xq), ('r', gate), ('c', k), ('c', v), ('c', gq),
                              ('c', wout)], [('r', (L, D_MODEL), F32)], nblk, host=host)
    return out[0] if host is None else (out[0][0], out[1])


def _backward_merge(dx, mix, mix_kind, xq, gate, k, v, gq, wout, name, nblk, host=None):
    def fn(dx, mix, xq, gate, k, v, gq, wout):
        g16 = dx.astype(BF16)
        do = _dot_nt(g16, wout)
        o, vjp = jax.vjp(_merge, mix, xq, gate, k, v, gq)
        dmix, dxq, dgate, dk, dv, dgq = vjp(do)
        return dmix, dxq, dgate, o, g16, dk, dv, dgq
    L = dx.shape[0]
    return _rowwise(
        name, fn,
        [('r', dx), (mix_kind, mix), ('r', xq), ('r', gate), ('c', k), ('c', v), ('c', gq), ('c', wout)],
        [('r', (L, PRIM), F32), ('r', (L, XQ), BF16), ('r', (L, BRANCH), BF16), ('t', (BRANCH, L), BF16),
         ('r', (L, D_MODEL), BF16), ('a', k.shape, F32), ('a', v.shape, F32), ('a', gq.shape, F32)], nblk,
        host=host)


_MLA_IN = 3392
_MLA_IN_PAD = 3456


def _uq_rows(wt):
    r = wt.reshape(MLA_H, HD + ROPE, wt.shape[1])
    return jnp.concatenate([r[:, :HD].reshape(PRIM, -1),
                            jnp.pad(r[:, HD:], ((0, 0), (0, HD - ROPE), (0, 0))).reshape(PRIM, -1)], axis=0)


def _uq_rows_back(wt):
    nope = wt[:PRIM].reshape(MLA_H, HD, -1)
    rope = wt[PRIM:].reshape(MLA_H, HD, -1)[:, :ROPE]
    return jnp.concatenate([nope, rope], axis=1).reshape(MLA_H * (HD + ROPE), -1)


def _mla_in_rows(wt):
    return jnp.concatenate([wt[:768], wt[832:], wt[768:832], jnp.zeros((64, wt.shape[1]), wt.dtype)], axis=0)


def _mla_in_rows_back(wt):
    return jnp.concatenate([wt[:768], wt[3328:3392], wt[768:3328]], axis=0)


_SMALL = (("ln_gain", 2048), ("mem_norm", 2048), ("xq_norm", 256), ("xk_norm", 256), ("s5_lambda_re", 6144),
          ("s5_lambda_im", 6144), ("s5_log_step", 96), ("s5_b_re", 98304), ("s5_b_im", 98304), ("s5_c_re", 98304),
          ("s5_c_im", 98304), ("s5_d", 1536), ("mla_q_lora_norm", 512), ("mla_kv_lora_norm", 256),
          ("mla_q_nope_norm", 128), ("mla_k_nope_norm", 128), ("mla_q_rope_norm", 64), ("mla_k_rope_norm", 64))
_SMALL_ROWS = 432
_SMALL_OFF = {name: sum(n for _, n in _SMALL[:i]) for i, (name, _) in enumerate(_SMALL)}


def _pack_small(d):
    flat = jnp.concatenate([d[n].reshape(-1).astype(F32) for n, _ in _SMALL])
    return jnp.pad(flat, (0, _SMALL_ROWS * 1024 - flat.shape[0])).reshape(_SMALL_ROWS, 1024)


def _unpack_small(p, name, shape):
    off = _SMALL_OFF[name]
    return p.reshape(-1)[off:off + int(np.prod(shape))].reshape(shape)


_WEIGHTS = ('ln_gain', 'w_out', 'mem_norm', 'w_mem_kv', 'xq_norm', 'xk_norm', 's5_w_in', 's5_lambda_re',
            's5_lambda_im', 's5_log_step', 's5_b_re', 's5_b_im', 's5_c_re', 's5_c_im', 's5_d', 's5_w_glu', 'mla_w_in',
            'mla_q_lora_norm', 'mla_kv_lora_norm', 'mla_w_uq', 'mla_w_ukv', 'mla_q_nope_norm', 'mla_k_nope_norm',
            'mla_q_rope_norm', 'mla_k_rope_norm')
_BIG = ('w_out', 'w_mem_kv', 's5_w_in', 's5_w_glu', 'mla_w_in', 'mla_w_uq', 'mla_w_ukv')


def _pad128(g):
    return jnp.pad(g.reshape(1, -1), ((0, 0), (0, HD - g.shape[-1])))


def kernel(x, mem, positions, ln_gain, w_out, mem_norm, w_mem_kv, xq_norm, xk_norm, s5_w_in, s5_lambda_re, s5_lambda_im, s5_log_step, s5_b_re, s5_b_im, s5_c_re, s5_c_im, s5_d, s5_w_glu, mla_w_in, mla_q_lora_norm, mla_kv_lora_norm, mla_w_uq, mla_w_ukv, mla_q_nope_norm, mla_k_nope_norm, mla_q_rope_norm, mla_k_rope_norm, loss_target, m_ln_gain, m_w_out, m_mem_norm, m_w_mem_kv, m_xq_norm, m_xk_norm, m_s5_w_in, m_s5_lambda_re, m_s5_lambda_im, m_s5_log_step, m_s5_b_re, m_s5_b_im, m_s5_c_re, m_s5_c_im, m_s5_d, m_s5_w_glu, m_mla_w_in, m_mla_q_lora_norm, m_mla_kv_lora_norm, m_mla_w_uq, m_mla_w_ukv, m_mla_q_nope_norm, m_mla_k_nope_norm, m_mla_q_rope_norm, m_mla_k_rope_norm, v_ln_gain, v_w_out, v_mem_norm, v_w_mem_kv, v_xq_norm, v_xk_norm, v_s5_w_in, v_s5_lambda_re, v_s5_lambda_im, v_s5_log_step, v_s5_b_re, v_s5_b_im, v_s5_c_re, v_s5_c_im, v_s5_d, v_s5_w_glu, v_mla_w_in, v_mla_q_lora_norm, v_mla_kv_lora_norm, v_mla_w_uq, v_mla_w_ukv, v_mla_q_nope_norm, v_mla_k_nope_norm, v_mla_q_rope_norm, v_mla_k_rope_norm):
    weights = dict(ln_gain=ln_gain, w_out=w_out, mem_norm=mem_norm, w_mem_kv=w_mem_kv, xq_norm=xq_norm,
                   xk_norm=xk_norm, s5_w_in=s5_w_in, s5_lambda_re=s5_lambda_re, s5_lambda_im=s5_lambda_im,
                   s5_log_step=s5_log_step, s5_b_re=s5_b_re, s5_b_im=s5_b_im, s5_c_re=s5_c_re, s5_c_im=s5_c_im,
                   s5_d=s5_d, s5_w_glu=s5_w_glu, mla_w_in=mla_w_in, mla_q_lora_norm=mla_q_lora_norm,
                   mla_kv_lora_norm=mla_kv_lora_norm, mla_w_uq=mla_w_uq, mla_w_ukv=mla_w_ukv,
                   mla_q_nope_norm=mla_q_nope_norm, mla_k_nope_norm=mla_k_nope_norm,
                   mla_q_rope_norm=mla_q_rope_norm, mla_k_rope_norm=mla_k_rope_norm)
    m_in = dict(zip(_WEIGHTS, (m_ln_gain, m_w_out, m_mem_norm, m_w_mem_kv, m_xq_norm, m_xk_norm, m_s5_w_in,
                               m_s5_lambda_re, m_s5_lambda_im, m_s5_log_step, m_s5_b_re, m_s5_b_im, m_s5_c_re,
                               m_s5_c_im, m_s5_d, m_s5_w_glu, m_mla_w_in, m_mla_q_lora_norm, m_mla_kv_lora_norm,
                               m_mla_w_uq, m_mla_w_ukv, m_mla_q_nope_norm, m_mla_k_nope_norm, m_mla_q_rope_norm,
                               m_mla_k_rope_norm)))
    v_in = dict(zip(_WEIGHTS, (v_ln_gain, v_w_out, v_mem_norm, v_w_mem_kv, v_xq_norm, v_xk_norm, v_s5_w_in,
                               v_s5_lambda_re, v_s5_lambda_im, v_s5_log_step, v_s5_b_re, v_s5_b_im, v_s5_c_re,
                               v_s5_c_im, v_s5_d, v_s5_w_glu, v_mla_w_in, v_mla_q_lora_norm, v_mla_kv_lora_norm,
                               v_mla_w_uq, v_mla_w_ukv, v_mla_q_nope_norm, v_mla_k_nope_norm, v_mla_q_rope_norm,
                               v_mla_k_rope_norm)))

    x0 = x[0]
    mem0 = mem[0]
    target = loss_target[0]
    L = x0.shape[0]
    nblk = 4
    nb_big = 8
    me = 4 * lax.axis_index("x") + 2 * lax.axis_index("y") + lax.axis_index("c")

    lora = jnp.pad(jnp.concatenate([mla_q_lora_norm, mla_kv_lora_norm], axis=1), ((0, 7), (0, HD - 96)))
    def gather(*shards):
        return _plan_all_gather(list(shards))

    kh = D_MODEL // 2
    (b_mkv0, b_glu, b_in_mla, b_out0, b_uq, b_ukv, b_mkv1, b_out1), (W_in_s5,) = _cast_call(
        [w_mem_kv[0], s5_w_glu[0], jnp.transpose(mla_w_in[0]), w_out[0], jnp.transpose(mla_w_uq[0]), mla_w_ukv[0],
         w_mem_kv[1], w_out[1]], "cast_shards", host=gather(s5_w_in[0].astype(BF16)))

    ln0, ln1 = ln_gain[0:1], ln_gain[1:2]
    gq0, gq1 = xq_norm[0:1], xq_norm[1:2]
    gk0, gk1 = xk_norm[0:1], xk_norm[1:2]
    gm0, gm1 = mem_norm[0:1], mem_norm[1:2]
    gqn, gkn = mla_q_nope_norm, mla_k_nope_norm
    gqr, gkr = _pad128(mla_q_rope_norm), _pad128(mla_k_rope_norm)

    lr3 = s5_lambda_re.reshape(S5_G, 1, S5_P)
    li3 = s5_lambda_im.reshape(S5_G, 1, S5_P)
    ls3 = s5_log_step.reshape(S5_G, 1, 1)
    btr = jnp.swapaxes(s5_b_re[0], 1, 2)
    bti = jnp.swapaxes(s5_b_im[0], 1, 2)
    a_r, a_i, bm, cm = _s5_params(lr3, li3, ls3, btr, bti, s5_c_re[0], s5_c_im[0])
    a_r2 = a_r.reshape(1, S5_G * S5_P)
    a_i2 = a_i.reshape(1, S5_G * S5_P)
    cmask, rmat = _s5_compact_consts()

    half = ROPE // 2
    inv_freq = ROPE_THETA ** (-jnp.arange(half, dtype=F32) / half)
    invf = jnp.concatenate([inv_freq, inv_freq, jnp.zeros((HD - ROPE,), F32)]).reshape(1, HD)

    def rot_tables(pos, invf):
        ang = pos.astype(F32) * invf
        lane = lax.broadcasted_iota(jnp.int32, ang.shape, 1)
        c = jnp.where(lane < ROPE, jnp.cos(ang), 0.0)
        s = jnp.sin(ang)
        return c, jnp.where(lane < half, -s, 0.0), jnp.where((lane >= half) & (lane < ROPE), s, 0.0)

    tc, ts1, ts2 = _rowwise("rot_tables", rot_tables, [('r', positions.reshape(L, 1)), ('c', invf)],
                            [('r', (L, HD), F32)] * 3, nblk)

    def in_s5(x, g, w):
        proj = _mm_slots(_rms(x, g, D_MODEL).astype(BF16), w)
        return proj[:, :PRIM], proj[:, PRIM:PRIM + XQ], proj[:, PRIM + XQ:]

    (u_s5, xq_a, gate_a), (G_in_mla_a,) = _rowwise(
        "s5_in", in_s5, [('r', x0), ('c', ln0), ('c', W_in_s5)],
        [('r', (L, PRIM), F32), ('r', (L, XQ), F32), ('r', (L, BRANCH), F32)], nblk,
        host=gather(b_in_mla[:, :kh]))
    (y_s5, s5_carry), (W_glu, G_mkv0) = _s5_fwd(u_s5, bm, cm, a_r2, a_i2, s5_d, host=gather(b_glu, b_mkv0))

    def glu(y, w):
        z = _mm_slots(_gelu(y).astype(BF16), w)
        return z[:, :PRIM] * _sigmoid(z[:, PRIM:]), z

    (y2, z_glu), (G_out0,) = _rowwise("s5_glu", glu, [('r', y_s5), ('c', W_glu)],
                                      [('r', (L, PRIM), F32), ('r', (L, 2 * PRIM), F32)], nblk, host=gather(b_out0))
    W_mkv0 = G_mkv0.reshape(D_MODEL, 2 * XQ)
    k_a, v_a = _kv_prep(mem0, gm0, W_mkv0, gk0, "kv_prep0")
    x1, (G_in_mla_b,) = _forward_merge(
        x0, y2, 'r', xq_a, gate_a, k_a, v_a, gq0, G_out0.reshape(BRANCH, D_MODEL), "merge0", nblk,
        host=gather(b_in_mla[:, kh:]))
    W_in_mla = _mla_in_rows(jnp.concatenate([G_in_mla_a, G_in_mla_b], axis=2).reshape(_MLA_IN, D_MODEL))

    def in_mla(x, g, w):
        proj = _dot_nt(_rms(x, g, D_MODEL).astype(BF16), w)
        return proj[:, :512], proj[:, 512:768], proj[:, 768:1280], proj[:, 1280:3328], proj[:, 3328:]

    (c_q, c_kv, xq_b, gate_b, krp), (G_uq, W_kv, G_lora) = _rowwise(
        "mla_in", in_mla, [('r', x1), ('c', ln1), ('c', W_in_mla)],
        [('r', (L, Q_LORA), F32), ('r', (L, KV_LORA), F32), ('r', (L, XQ), F32), ('r', (L, BRANCH), F32),
         ('r', (L, HD), F32)], nblk,
        host=gather(b_uq, b_ukv, lora))
    W_q = _uq_rows(G_uq.reshape(MLA_H * (HD + ROPE), Q_LORA))
    g_qlora = G_lora[:, 0, :64].reshape(1, Q_LORA)
    g_kvlora = G_lora[:, 0, 64:96].reshape(1, KV_LORA)

    def qkv(c_q, c_kv, krp, tc, ts1, ts2, gql, gkvl, wq, wkv, gqn, gkn, gqr, gkr):
        q = _dot_nt(_rms(c_q, gql, Q_LORA).astype(BF16), wq)
        kv = _mm_slots(_rms(c_kv, gkvl, KV_LORA).astype(BF16), wkv)
        kp, v = _kv_post(*_kv_chunks(kv), krp, gkn, gkr, tc, ts1, ts2)
        return _q_post(*_q_chunks(q), gqn, gqr, tc, ts1, ts2), kp, v

    qkv_consts = [('c', g_qlora), ('c', g_kvlora), ('c', W_q), ('c', W_kv), ('c', gqn), ('c', gkn), ('c', gqr),
                  ('c', gkr)]
    (q_pad, k_pad, v_h), (G_mkv1, G_out1) = _rowwise(
        "mla_qkv", qkv, [('r', c_q), ('r', c_kv), ('r', krp), ('r', tc), ('r', ts1), ('r', ts2)] + qkv_consts,
        [('r', (L, 2 * PRIM), BF16), ('r', (L, 2 * PRIM), BF16), ('r', (L, PRIM), BF16)], nblk,
        host=gather(b_mkv1, b_out1))
    W_out = (G_out0.reshape(BRANCH, D_MODEL), G_out1.reshape(BRANCH, D_MODEL))
    W_mkv = (W_mkv0, G_mkv1.reshape(D_MODEL, 2 * XQ))
    scale = (HD + ROPE) ** -0.5
    attn, lse = _attn_fwd(q_pad, k_pad, v_h, scale)
    k_b, v_b = _kv_prep(mem0, gm1, W_mkv[1], gk1, "kv_prep1")

    def merge_loss(x, mix, xq, gate, k, v, gq, wout, t):
        err = x + _dot(_merge(mix, xq, gate, k, v, gq).astype(BF16), wout) - t
        part = 0.5 * jnp.sum(jnp.sum(err * err, axis=-1, keepdims=True) * (1.0 / D_MODEL), axis=0, keepdims=True)
        return err * (1.0 / D_MODEL), jnp.broadcast_to(part, (1, HD))

    dx2, loss_part = _rowwise(
        "merge1_loss", merge_loss,
        [('r', x1), ('r', attn), ('r', xq_b), ('r', gate_b), ('c', k_b), ('c', v_b), ('c', gq1), ('c', W_out[1]),
         ('r', target)], [('r', (L, D_MODEL), F32), ('a', (1, HD), F32)], nblk)

    dattn, dxq_b, dgate_b, o_b, g_b, dk_b, dv_b, dgq1 = _backward_merge(
        dx2, attn, 'r', xq_b, gate_b, k_b, v_b, gq1, W_out[1], "merge1_bwd", nb_big)
    dgm1, dW_mkv1, dgk1 = _kv_prep_bwd(mem0, gm1, W_mkv[1], gk1, dk_b, dv_b, "kv_prep1_bwd")
    dW_out1 = _matmul_tn(o_b, g_b, "dw_out1")
    dq_pad, dk_pad, dv_h = _attn_bwd(q_pad, k_pad, v_h, attn, lse, dattn, scale)

    def qkv_bwd(c_q, c_kv, krp, tc, ts1, ts2, dqp, dkp, dv, gql, gkvl, wq, wkv, gqn, gkn, gqr, gkr):
        cqn, vjp_qn = jax.vjp(lambda a, b: _rms(a, b, Q_LORA), c_q, gql)
        ckvn, vjp_kvn = jax.vjp(lambda a, b: _rms(a, b, KV_LORA), c_kv, gkvl)
        cqn16 = cqn.astype(BF16)
        ckvn16 = ckvn.astype(BF16)
        q = _dot_nt(cqn16, wq)
        kv = _mm_slots(ckvn16, wkv)
        _, vjp_q = jax.vjp(lambda n, r, a, b: _q_post(n, r, a, b, tc, ts1, ts2), *_q_chunks(q), gqn, gqr)
        dnope, drope, dgqn, dgqr = vjp_q(dqp.astype(F32))
        dq = jnp.concatenate(dnope + drope, axis=-1)
        _, vjp_kv = jax.vjp(lambda n, v, k, a, b: _kv_post(n, v, k, a, b, tc, ts1, ts2), *_kv_chunks(kv), krp, gkn,
                            gkr)
        dkn, dvals, dkrp, dgkn, dgkr = vjp_kv((dkp.astype(F32), dv.astype(F32)))
        dkv = jnp.concatenate([x for pair in zip(dkn, dvals) for x in pair], axis=-1)
        dq16 = dq.astype(BF16)
        dkv16 = dkv.astype(BF16)
        dc_q, dgql = vjp_qn(_dot(dq16, wq))
        dc_kv, dgkvl = vjp_kvn(_mm_slots_nt(dkv16, wkv))
        return dc_q, dc_kv, dkrp, cqn16, dq16, ckvn16, dkv16, dgql, dgkvl, dgqn, dgkn, dgqr, dgkr

    (dc_q, dc_kv, dkrp, cqn16, dq16, ckvn16, dkv16, dgql, dgkvl, dgqn, dgkn, dgqr, dgkr) = _rowwise(
        "mla_qkv_bwd", qkv_bwd,
        [('r', c_q), ('r', c_kv), ('r', krp), ('r', tc), ('r', ts1), ('r', ts2), ('r', dq_pad), ('r', dk_pad),
         ('r', dv_h)] + qkv_consts,
        [('r', (L, Q_LORA), BF16), ('r', (L, KV_LORA), BF16), ('r', (L, HD), BF16), ('r', (L, Q_LORA), BF16),
         ('t', (2 * PRIM, L), BF16), ('t', (KV_LORA, L), BF16), ('r', (L, 2 * PRIM), BF16),
         ('a', (1, Q_LORA), F32), ('a', (1, KV_LORA), F32), ('a', (1, HD), F32), ('a', (1, HD), F32),
         ('a', (1, HD), F32), ('a', (1, HD), F32)], nb_big)
    dW_q = _matmul_tn(dq16, cqn16, "dw_uq")
    dW_kv = _matmul_tn_slots(ckvn16, dkv16, "dw_ukv")

    def in_bwd(x, dres, g, w, *dparts):
        dproj = jnp.concatenate(dparts, axis=-1).astype(BF16)
        xn, vjp = jax.vjp(lambda a, b: _rms(a, b, D_MODEL), x, g)
        dx, dg = vjp(_mm_slots_nt(dproj, w) if w.ndim == 3 else _dot(dproj, w))
        return dx + dres, xn, dproj, dg

    dx1, xn1, dproj1, dln1 = _rowwise(
        "mla_in_bwd", in_bwd,
        [('r', x1), ('r', dx2), ('c', ln1), ('c', W_in_mla), ('r', dc_q), ('r', dc_kv), ('r', dxq_b), ('r', dgate_b),
         ('r', dkrp)],
        [('r', (L, D_MODEL), F32), ('r', (L, D_MODEL), BF16), ('t', (_MLA_IN_PAD, L), BF16), ('a', (1, D_MODEL), F32)],
        nblk)
    dW_in_mla = _matmul_tn(dproj1, xn1, "dw_mla_in")

    grads1 = [dW_out1.reshape(N_DEV, 256, D_MODEL), dW_mkv1.reshape(N_DEV, 128, 2 * XQ),
              _mla_in_rows_back(dW_in_mla).reshape(N_DEV, 424, D_MODEL),
              _uq_rows_back(dW_q).reshape(N_DEV, 288, Q_LORA), dW_kv]
    (dy2, dxq_a, dgate_a, o_a, g_a, dk_a, dv_a, dgq0), pair1 = _backward_merge(
        dx1, y2, 'r', xq_a, gate_a, k_a, v_a, gq0, W_out[0], "merge0_bwd", nb_big, host=_plan_pair(grads1))
    dgm0, dW_mkv0, dgk0 = _kv_prep_bwd(mem0, gm0, W_mkv[0], gk0, dk_a, dv_a, "kv_prep0_bwd")
    dW_out0 = _matmul_tn(o_a, g_a, "dw_out0")
    t1 = list(_pair_add(grads1, pair1, "rs_add_layer1"))

    def glu_bwd(y, z, dy2, w):
        h, vjp_h = jax.vjp(_gelu, y)
        _, vjp_z = jax.vjp(lambda a, b: a * _sigmoid(b), z[:, :PRIM], z[:, PRIM:])
        dz16 = jnp.concatenate(vjp_z(dy2), axis=-1).astype(BF16)
        return vjp_h(_mm_slots_nt(dz16, w))[0], h.astype(BF16), dz16

    grads0 = [dW_out0.reshape(N_DEV, 256, D_MODEL), dW_mkv0.reshape(N_DEV, 128, 2 * XQ)]
    (dy_s5, h16, dz16), glu_hosted = _rowwise(
        "s5_glu_bwd", glu_bwd, [('r', y_s5), ('r', z_glu), ('r', dy2), ('c', W_glu)],
        [('r', (L, PRIM), F32), ('t', (PRIM, L), BF16), ('r', (L, 2 * PRIM), BF16)], nb_big,
        host=_combine(_plan_chips(t1[2:3]), _plan_pair(grads0)))
    recv_in_mla, pair0 = glu_hosted[:1], glu_hosted[1:]
    dW_glu = _matmul_tn_slots(h16, dz16, "dw_glu")
    t0 = list(_pair_add(grads0 + [dW_glu], pair0 + list(_exchange_call(_plan_pair([dW_glu]), "rs_pair_glu")),
                        "rs_add_layer0"))
    (du_s5, dbc, dcc, dd, dar, dai), recv_rest = _s5_bwd(u_s5, dy_s5, s5_carry, bm, cm, a_r2, a_i2, s5_d,
                                                        cmask, rmat, host=_plan_chips(t1[:2] + t1[3:] + t0))
    early_recv = recv_rest[:2] + recv_in_mla + recv_rest[2:]
    dbc4 = dbc.reshape(S5_G, S5_C, 2, S5_P)
    dcc4 = dcc.reshape(S5_G, S5_C, 2, S5_P)
    dlr, dli, dls, dbtr, dbti = _s5_params_bwd(
        lr3, li3, ls3, btr, bti, dar.reshape(S5_G, 1, S5_P), dai.reshape(S5_G, 1, S5_P), dbc4[:, :, 0], dbc4[:, :, 1])

    small_part = {
        "ln_gain": jnp.concatenate([jnp.zeros_like(dln1), dln1]), "mem_norm": jnp.concatenate([dgm0, dgm1]),
        "xq_norm": jnp.concatenate([dgq0, dgq1]), "xk_norm": jnp.concatenate([dgk0, dgk1]),
        "s5_lambda_re": dlr, "s5_lambda_im": dli, "s5_log_step": dls,
        "s5_b_re": jnp.swapaxes(dbtr, 1, 2), "s5_b_im": jnp.swapaxes(dbti, 1, 2),
        "s5_c_re": dcc4[:, :, 0], "s5_c_im": -dcc4[:, :, 1], "s5_d": dd,
        "mla_q_lora_norm": dgql, "mla_kv_lora_norm": dgkvl, "mla_q_nope_norm": dgqn, "mla_k_nope_norm": dgkn,
        "mla_q_rope_norm": dgqr[:, :ROPE], "mla_k_rope_norm": dgkr[:, :ROPE],
    }
    loss8 = jnp.pad(loss_part, ((0, 7), (0, 0)))
    (dx0, xn0, dproj0, dln0), (small_gath, loss_g) = _rowwise(
        "s5_in_bwd", in_bwd,
        [('r', x0), ('r', dx1), ('c', ln0), ('c', W_in_s5), ('r', du_s5), ('r', dxq_a),
         ('r', dgate_a)],
        [('r', (L, D_MODEL), F32), ('t', (D_MODEL, L), BF16), ('r', (L, 2 * BRANCH), BF16), ('a', (1, D_MODEL), F32)],
        nblk, host=_plan_all_gather([_pack_small(small_part).astype(BF16), loss8]))
    dW_in_s5 = _matmul_tn_slots(xn0, dproj0, "dw_s5_in")

    late = [dW_in_s5]
    late_t = _pair_add(late, list(_exchange_call(_plan_pair(late), "rs_pair_late")), "rs_add_late")
    owners = [("w_out", 1), ("w_mem_kv", 1), ("mla_w_in", 0), ("mla_w_uq", 0), ("mla_w_ukv", 0), ("w_out", 0),
              ("w_mem_kv", 0), ("s5_w_glu", 0)]
    flipped = ("mla_w_in", "mla_w_uq")

    def shard(d, n, i):
        return jnp.transpose(d[n][i]) if n in flipped else d[n][i]

    upd, (late_recv, ln0_gath) = _updates_call(
        early_recv, [shard(weights, n, i) for n, i in owners], [shard(m_in, n, i) for n, i in owners],
        [shard(v_in, n, i) for n, i in owners], "update_early",
        host=_combine(_plan_chips(late_t), _plan_all_gather([jnp.pad(dln0, ((0, 7), (0, 0)))])))
    owners.append(("s5_w_in", 0))
    upd.append(_sum_adamw(late_recv, s5_w_in[0], m_s5_w_in[0], v_s5_w_in[0], "update_s5_w_in"))
    grads, delta, new_m, new_v = {}, {}, {}, {}
    for n in _BIG:
        parts = [u for u, (o, _) in sorted(zip(upd, owners), key=lambda t: t[1][1]) if o == n]
        if n in flipped:
            grads[n], delta[n], new_m[n], new_v[n] = (jnp.transpose(parts[0][j])[None] for j in range(4))
        else:
            grads[n], delta[n], new_m[n], new_v[n] = (jnp.stack([p[j] for p in parts]) for j in range(4))

    gs, loss_sum = _small_sum(small_gath, loss_g, ln0_gath, "small_sum")
    loss = loss_sum[0, 0]
    for n, _ in _SMALL:
        shape = weights[n].shape
        if n == "mla_q_lora_norm":
            grads[n] = lax.dynamic_slice(_unpack_small(gs, n, (Q_LORA,)), (me * 64,), (64,)).reshape(shape)
        elif n == "mla_kv_lora_norm":
            grads[n] = lax.dynamic_slice(_unpack_small(gs, n, (KV_LORA,)), (me * 32,), (32,)).reshape(shape)
        else:
            grads[n] = _unpack_small(gs, n, shape)

    def own(n, a):
        if a.ndim == 4:
            a = jnp.transpose(a, (0, 2, 3, 1))
        elif a.ndim == 3:
            a = jnp.transpose(a, (0, 2, 1))
        return a.reshape(a.shape[1:]) if a.ndim >= 3 else a

    def back(n, a):
        shape = weights[n].shape
        if len(shape) == 4:
            return jnp.transpose(a.reshape((1,) + a.shape), (0, 3, 1, 2))
        if len(shape) == 3:
            return jnp.transpose(a.reshape((1,) + a.shape), (0, 2, 1))
        return a.reshape(shape)

    wide = ("s5_b_re", "s5_b_im", "s5_c_re", "s5_c_im")
    for names, nb, call in (([n for n, _ in _SMALL if n not in wide], 1, "update_small"), (wide, 4, "update_s5_bc")):
        res = _adamw_multi([own(n, weights[n]) for n in names], [own(n, grads[n]) for n in names],
                           [own(n, m_in[n]) for n in names], [own(n, v_in[n]) for n in names], call, nb)
        for n, (dl, m2, v2) in zip(names, res):
            delta[n], new_m[n], new_v[n] = back(n, dl), back(n, m2), back(n, v2)
    return (loss, dx0[None], *[grads[n] for n in _WEIGHTS], *[delta[n] for n in _WEIGHTS],
            *[new_m[n] for n in _WEIGHTS], *[new_v[n] for n in _WEIGHTS])
```

```python
import functools
import math

import numpy as np
import jax
import jax.numpy as jnp
from jax import lax
from jax.experimental import pallas as pl
from jax.experimental.pallas import tpu as pltpu

F32 = jnp.float32
BF16 = jnp.bfloat16
EPS = 1e-6
NEG = float(np.finfo(np.float32).min)
MESH = pl.DeviceIdType.MESH

N_DEV = 8
D_MODEL = 1024
MEM_LEN = 256
XQ = 512
PRIM = 1536
BRANCH = 2048
X_HEADS = 4
HD = 128
S5_G = 96
S5_P = 64
S5_C = 16
S5_GB = 8
S5_W = S5_GB * S5_P
MLA_H = 12
ROPE = 64
Q_LORA = 512
KV_LORA = 256
ROPE_THETA = 10000.0

ADAM_LR = 0.001
ADAM_B1 = 0.9
ADAM_B2 = 0.999
ADAM_EPS = 1e-08
ADAM_WD = 0.01
ADAM_STEP = 10

VMEM_LIMIT = 56 * 1024 * 1024


def _dot(a, b):
    return jnp.dot(a, b, preferred_element_type=F32)


def _dot_nt(a, b):
    return lax.dot_general(a, b, (((1,), (1,)), ((), ())), preferred_element_type=F32)


def _dot_tn(a, b):
    return lax.dot_general(a, b, (((0,), (0,)), ((), ())), preferred_element_type=F32)


@jax.custom_vjp
def _mm(a, b):
    return _dot(a.astype(BF16), b.astype(BF16))


def _mm_fwd(a, b):
    return _mm(a, b), (a, b)


def _mm_bwd(res, g):
    a, b = res
    gb = g.astype(BF16)
    return _dot_nt(gb, b.astype(BF16)).astype(a.dtype), _dot_tn(a.astype(BF16), gb).astype(b.dtype)


_mm.defvjp(_mm_fwd, _mm_bwd)


@jax.custom_vjp
def _mm_nt(a, b):
    return _dot_nt(a.astype(BF16), b.astype(BF16))


def _mm_nt_fwd(a, b):
    return _mm_nt(a, b), (a, b)


def _mm_nt_bwd(res, g):
    a, b = res
    gb = g.astype(BF16)
    return _dot(gb, b.astype(BF16)).astype(a.dtype), _dot_tn(gb, a.astype(BF16)).astype(b.dtype)


_mm_nt.defvjp(_mm_nt_fwd, _mm_nt_bwd)


@jax.custom_vjp
def _softmax(s):
    m = jnp.max(s, axis=-1, keepdims=True)
    e = jnp.exp(s - m)
    return e / jnp.sum(e, axis=-1, keepdims=True)


def _softmax_fwd(s):
    p = _softmax(s)
    return p, p


def _softmax_bwd(p, g):
    return (p * (g - jnp.sum(p * g, axis=-1, keepdims=True)),)


_softmax.defvjp(_softmax_fwd, _softmax_bwd)


def _rms(x, g, n):
    ms = jnp.sum(x * x, axis=-1, keepdims=True) * (1.0 / n)
    return x * lax.rsqrt(ms + EPS) * g


def _sigmoid(x):
    return 1.0 / (1.0 + jnp.exp(-x))


def _silu(x):
    return x * _sigmoid(x)


def _gelu(x):
    c = math.sqrt(2.0 / math.pi)
    return 0.5 * x * (1.0 + jnp.tanh(c * (x + 0.044715 * (x * x * x))))


@jax.custom_vjp
def _rot(x, c, s1, s2):
    return x * c + pltpu.roll(x, 96, 1) * s1 + pltpu.roll(x, 32, 1) * s2


def _rot_fwd(x, c, s1, s2):
    return _rot(x, c, s1, s2), (c, s1, s2)


def _rot_bwd(res, g):
    c, s1, s2 = res
    dx = g * c + pltpu.roll(g * s1, 32, 1) + pltpu.roll(g * s2, 96, 1)
    return dx, jnp.zeros_like(c), jnp.zeros_like(s1), jnp.zeros_like(s2)


_rot.defvjp(_rot_fwd, _rot_bwd)


def _mem_attn(xq, k, v, gq):
    outs = []
    for h in range(X_HEADS):
        sl = slice(HD * h, HD * (h + 1))
        q = _rms(xq[:, sl], gq, HD)
        p = _softmax(_mm_nt(q, k[:, sl]) * (HD ** -0.5))
        outs.append(_mm(p, v[:, sl]))
    return jnp.concatenate(outs, axis=-1)


def _merge(mix, xq, gate, k, v, gq):
    return jnp.concatenate([mix, _mem_attn(xq, k, v, gq)], axis=-1) * _silu(gate)


def _q_chunks(q):
    return ([q[:, HD * h:HD * (h + 1)] for h in range(MLA_H)],
            [q[:, PRIM + HD * h:PRIM + HD * (h + 1)] for h in range(MLA_H)])


def _q_post(nope, rope, gqn, gqr, c, s1, s2):
    pieces = []
    for qn, qr in zip(nope, rope):
        pieces.append(_rms(qn, gqn, HD))
        pieces.append(_rot(_rms(qr, gqr, ROPE), c, s1, s2))
    return jnp.concatenate(pieces, axis=-1)


def _kv_chunks(kv):
    return ([kv[:, 2 * HD * h:2 * HD * h + HD] for h in range(MLA_H)],
            [kv[:, 2 * HD * h + HD:2 * HD * (h + 1)] for h in range(MLA_H)])


def _kv_post(kn, vals, krp, gkn, gkr, c, s1, s2):
    kr = _rot(_rms(krp, gkr, ROPE), c, s1, s2)
    pieces = []
    for k in kn:
        pieces.append(_rms(k, gkn, HD))
        pieces.append(kr)
    return jnp.concatenate(pieces, axis=-1), jnp.concatenate(vals, axis=-1)


def _rowwise(name, fn, ins, outs, nblk, host=None):
    n_in = len(ins)

    def spec(kind, shape):
        if kind == 'r':
            return pl.BlockSpec((shape[0] // nblk, shape[1]), lambda i: (i, 0))
        if kind == 't':
            return pl.BlockSpec((shape[0], shape[1] // nblk), lambda i: (0, i))
        zeros = (0,) * len(shape)
        return pl.BlockSpec(tuple(shape), lambda i: zeros)

    def body(*refs):
        i = pl.program_id(0)
        res = fn(*[r[...] for r in refs[:n_in]])
        for (kind, _, _), ref, val in zip(outs, refs[n_in:], res):
            if kind == 'a':
                @pl.when(i == 0)
                def _():
                    ref[...] = jnp.zeros_like(ref)
                ref[...] += val.astype(ref.dtype)
            elif kind == 't':
                ref[...] = val.astype(F32).T.astype(ref.dtype)
            else:
                ref[...] = val.astype(ref.dtype)

    res, hosted = _hosting_call(
        body, name, nblk, host, [a for _, a in ins], [spec(k, a.shape) for k, a in ins],
        [jax.ShapeDtypeStruct(tuple(s), d) for _, s, d in outs], [spec(k, s) for k, s, _ in outs], [])
    return res if host is None else (res, hosted)


def _matmul_tn(at, g, name, out_dtype=BF16):
    K, L = at.shape
    N = g.shape[1]
    tn = next(t for t in (512, 384, 256, 128) if N % t == 0)

    def body(a_ref, g_ref, o_ref):
        o_ref[...] = _dot(a_ref[...], g_ref[...]).astype(o_ref.dtype)

    return pl.pallas_call(
        body, name=name, grid=(N // tn,),
        in_specs=[pl.BlockSpec((K, L), lambda n: (0, 0)), pl.BlockSpec((L, tn), lambda n: (0, n))],
        out_specs=pl.BlockSpec((K, tn), lambda n: (0, n)),
        out_shape=jax.ShapeDtypeStruct((K, N), out_dtype),
        compiler_params=pltpu.CompilerParams(dimension_semantics=("arbitrary",), vmem_limit_bytes=VMEM_LIMIT),
    )(at, g)


def _matmul_tn_slots(at, g, name, host=None):
    K, L = at.shape
    n = g.shape[1] // N_DEV

    def body(a_ref, g_ref, o_ref):
        o_ref[...] = _dot(a_ref[...], g_ref[...]).astype(o_ref.dtype)

    res, hosted = _hosting_call(
        body, name, N_DEV, host, [at, g],
        [pl.BlockSpec((K, L), lambda d: (0, 0)), pl.BlockSpec((L, n), lambda d: (0, d))],
        [jax.ShapeDtypeStruct((N_DEV, K, n), BF16)], [pl.BlockSpec((None, K, n), lambda d: (d, 0, 0))], [])
    return res[0] if host is None else (res[0], hosted)


def _mm_slots(a16, w):
    return jnp.concatenate([_dot(a16, w[d]) for d in range(N_DEV)], axis=-1)


def _mm_slots_nt(g16, w):
    n = w.shape[2]
    out = _dot_nt(g16[:, 0:n], w[0])
    for d in range(1, N_DEV):
        out = out + _dot_nt(g16[:, d * n:(d + 1) * n], w[d])
    return out


class _Exchange:
    def __init__(self, ins, outs, scratch, start, finish, mid=None):
        self.ins, self.outs, self.scratch, self.start, self.finish = ins, outs, scratch, start, finish
        self.mid = mid if mid is not None else (lambda ins, outs, sems: None)


def _xyc():
    return lax.axis_index("x"), lax.axis_index("y"), lax.axis_index("c")


def _plan_all_gather(xs):
    n = len(xs)

    def build(x_refs, out_refs, sems):
        send_sems, recv_sems, local_sems = sems
        x, y, c = _xyc()

        def copies(k, block, to, own=False):
            slot = 4 * block[0] + 2 * block[1] + block[2]
            return [pltpu.make_async_remote_copy(
                src_ref=x_refs[a] if own else out_refs[a].at[slot], dst_ref=out_refs[a].at[slot],
                send_sem=send_sems.at[k * n + a], recv_sem=recv_sems.at[k * n + a], device_id=to,
                device_id_type=MESH) for a in range(n)]

        mine = [pltpu.make_async_copy(x_refs[a], out_refs[a].at[4 * x + 2 * y + c], local_sems.at[a])
                for a in range(n)]
        return copies, mine, (x, y, c), [(1 - x, y), (x, 1 - y), (1 - x, 1 - y)]

    def first_copies(copies, me, chips):
        x, y, c = me
        first = copies(0, me, (x, y, 1 - c), own=True)
        for j, chip in enumerate(chips):
            first += copies(1 + j, me, (*chip, c), own=True)
        return first

    def start(x_refs, out_refs, sems):
        copies, mine, me, chips = build(x_refs, out_refs, sems)
        for cp in mine + first_copies(copies, me, chips):
            cp.start()

    def mid(x_refs, out_refs, sems):
        copies, mine, me, chips = build(x_refs, out_refs, sems)
        x, y, c = me
        for j, chip in enumerate(chips):
            for cp in copies(1 + j, (*chip, c), me):
                cp.wait_recv()
            for cp in copies(4 + j, (*chip, c), (x, y, 1 - c)):
                cp.start()

    def finish(x_refs, out_refs, sems):
        copies, mine, me, chips = build(x_refs, out_refs, sems)
        x, y, c = me
        passed = []
        for j, chip in enumerate(chips):
            passed += copies(4 + j, (*chip, c), (x, y, 1 - c))
        for cp in copies(0, (x, y, 1 - c), me):
            cp.wait_recv()
        for j, chip in enumerate(chips):
            for cp in copies(4 + j, (*chip, 1 - c), me):
                cp.wait_recv()
        for cp in first_copies(copies, me, chips) + passed:
            cp.wait_send()
        for cp in mine:
            cp.wait()

    return _Exchange(list(xs), [jax.ShapeDtypeStruct((N_DEV,) + a.shape, a.dtype) for a in xs],
                     [pltpu.SemaphoreType.DMA((7 * n,)), pltpu.SemaphoreType.DMA((7 * n,)),
                      pltpu.SemaphoreType.DMA((n,))], start, finish, mid)


_CHIPS = ((0, 0), (0, 1), (1, 0), (1, 1))


def _plan_pair(sends):
    n = len(sends)

    def build(s_refs, o_refs, sems):
        send_sems, recv_sems = sems
        x, y, c = _xyc()
        return [pltpu.make_async_remote_copy(
            src_ref=s_refs[a].at[4 * px + 2 * py + 1 - c], dst_ref=o_refs[a].at[j],
            send_sem=send_sems.at[j * n + a], recv_sem=recv_sems.at[j * n + a], device_id=(x, y, 1 - c),
            device_id_type=MESH) for j, (px, py) in enumerate(_CHIPS) for a in range(n)]

    def start(s_refs, o_refs, sems):
        for cp in build(s_refs, o_refs, sems):
            cp.start()

    def finish(s_refs, o_refs, sems):
        for cp in build(s_refs, o_refs, sems):
            cp.wait_recv()
            cp.wait_send()

    return _Exchange(list(sends), [jax.ShapeDtypeStruct((4,) + a.shape[1:], a.dtype) for a in sends],
                     [pltpu.SemaphoreType.DMA((4 * n,)), pltpu.SemaphoreType.DMA((4 * n,))], start, finish)


def _plan_chips(ts):
    n = len(ts)
    flips = ((1, 0), (0, 1), (1, 1))

    def build(t_refs, o_refs, sems):
        send_sems, recv_sems, local_sems = sems
        x, y, c = _xyc()
        mine = 2 * x + y
        local = [pltpu.make_async_copy(t_refs[a].at[mine], o_refs[a].at[mine], local_sems.at[a]) for a in range(n)]
        remote = []
        for k, (fx, fy) in enumerate(flips):
            px = 1 - x if fx else x
            py = 1 - y if fy else y
            remote += [pltpu.make_async_remote_copy(
                src_ref=t_refs[a].at[2 * px + py], dst_ref=o_refs[a].at[mine],
                send_sem=send_sems.at[k * n + a], recv_sem=recv_sems.at[k * n + a], device_id=(px, py, c),
                device_id_type=MESH) for a in range(n)]
        return local, remote

    def start(t_refs, o_refs, sems):
        local, remote = build(t_refs, o_refs, sems)
        for cp in local + remote:
            cp.start()

    def finish(t_refs, o_refs, sems):
        local, remote = build(t_refs, o_refs, sems)
        for cp in remote:
            cp.wait_recv()
        for cp in remote:
            cp.wait_send()
        for cp in local:
            cp.wait()

    return _Exchange(list(ts), [jax.ShapeDtypeStruct(a.shape, a.dtype) for a in ts],
                     [pltpu.SemaphoreType.DMA((3 * n,)), pltpu.SemaphoreType.DMA((3 * n,)),
                      pltpu.SemaphoreType.DMA((n,))], start, finish)


def _combine(*plans):
    def parts(refs, attr):
        out, at = [], 0
        for p in plans:
            n = len(getattr(p, attr))
            out.append(refs[at:at + n])
            at += n
        return out

    def run(half):
        def go(ins, outs, sems):
            for p, a, o, s in zip(plans, parts(ins, "ins"), parts(outs, "outs"), parts(sems, "scratch")):
                getattr(p, half)(a, o, s)
        return go

    return _Exchange(sum((p.ins for p in plans), []), sum((p.outs for p in plans), []),
                     sum((p.scratch for p in plans), []), run("start"), run("finish"), run("mid"))


def _exchange_call(plan, name):
    n = len(plan.ins)

    def body(*refs):
        ins, outs, sems = refs[:n], refs[n:2 * n], refs[2 * n:]
        plan.start(ins, outs, sems)
        plan.mid(ins, outs, sems)
        plan.finish(ins, outs, sems)

    return pl.pallas_call(
        body, name=name, out_shape=plan.outs,
        in_specs=[pl.BlockSpec(memory_space=pl.ANY)] * n, out_specs=[pl.BlockSpec(memory_space=pl.ANY)] * n,
        scratch_shapes=plan.scratch,
    )(*plan.ins)


def _slab_spec(lead, rows, cols, nb):
    if rows % (nb * 16) == 0:
        return pl.BlockSpec((lead, rows // nb, cols), lambda i: (0, i, 0))
    if cols % (nb * 128) == 0:
        return pl.BlockSpec((lead, rows, cols // nb), lambda i: (0, 0, i))
    return pl.BlockSpec((lead, rows, cols), lambda i: (0, 0, 0))


def _slab_spec2(rows, cols, nb):
    if rows % (nb * 16) == 0:
        return pl.BlockSpec((rows // nb, cols), lambda i: (i, 0))
    if cols % (nb * 128) == 0:
        return pl.BlockSpec((rows, cols // nb), lambda i: (0, i))
    return pl.BlockSpec((rows, cols), lambda i: (0, 0))


def _cast_call(arrays, name, host=None):
    n = len(arrays)
    nb = 8

    def body(*refs):
        for a in range(n):
            refs[n + a][...] = refs[a][...].astype(BF16)

    specs = [_slab_spec2(x.shape[0], x.shape[1], nb) for x in arrays]
    return _hosting_call(body, name, nb, host, list(arrays), specs,
                         [jax.ShapeDtypeStruct(x.shape, BF16) for x in arrays], specs, [])


def _pair_add(sends, fromsib, name):
    n = len(sends)
    nb = 8

    def body(*refs):
        c = lax.axis_index("c")
        for a in range(n):
            s_ref, f_ref, t_ref = refs[a], refs[n + a], refs[2 * n + a]
            for j in range(4):
                t_ref[j] = (s_ref[2 * j + c].astype(F32) + f_ref[j].astype(F32)).astype(t_ref.dtype)

    def spec(a, lead):
        return _slab_spec(lead, a.shape[1], a.shape[2], nb)

    return pl.pallas_call(
        body, name=name, grid=(nb,),
        in_specs=[spec(a, N_DEV) for a in sends] + [spec(a, 4) for a in fromsib],
        out_specs=[spec(a, 4) for a in fromsib],
        out_shape=[jax.ShapeDtypeStruct(a.shape, a.dtype) for a in fromsib],
        compiler_params=pltpu.CompilerParams(dimension_semantics=("arbitrary",), vmem_limit_bytes=VMEM_LIMIT),
    )(*sends, *fromsib)


def _adamw_vals(w, g, m, v):
    m2 = ADAM_B1 * m + (1.0 - ADAM_B1) * g
    v2 = ADAM_B2 * v + (1.0 - ADAM_B2) * (g * g)
    m_hat = m2 / (1.0 - ADAM_B1 ** ADAM_STEP)
    v_hat = v2 / (1.0 - ADAM_B2 ** ADAM_STEP)
    delta = -ADAM_LR * (m_hat / (jnp.sqrt(v_hat) + ADAM_EPS) + ADAM_WD * w)
    return delta, m2, v2


def _sum_adamw(recv, w, m, v, name):
    R, C = w.shape
    ns = recv.shape[0]
    br = next((t for t in (256, 128, 64, 32, 16) if R % t == 0), R)

    def body(r_ref, w_ref, m_ref, v_ref, g_ref, d_ref, m2_ref, v2_ref):
        g = r_ref[0].astype(F32)
        for d in range(1, ns):
            g = g + r_ref[d].astype(F32)
        dl, m2, v2 = _adamw_vals(w_ref[...], g, m_ref[...], v_ref[...])
        g_ref[...] = g
        d_ref[...] = dl
        m2_ref[...] = m2
        v2_ref[...] = v2

    spec = pl.BlockSpec((br, C), lambda i: (i, 0))
    return pl.pallas_call(
        body, name=name, grid=(R // br,),
        in_specs=[pl.BlockSpec((ns, br, C), lambda i: (0, i, 0)), spec, spec, spec], out_specs=[spec] * 4,
        out_shape=[jax.ShapeDtypeStruct((R, C), F32)] * 4,
        compiler_params=pltpu.CompilerParams(dimension_semantics=("arbitrary",)),
    )(recv, w, m, v)


def _updates_call(recvs, ws, ms, vs, name, host=None):
    n = len(recvs)
    nb = 8

    def body(*refs):
        for a in range(n):
            r_ref, w_ref, m_ref, v_ref = refs[a], refs[n + a], refs[2 * n + a], refs[3 * n + a]
            g_ref, d_ref, m2_ref, v2_ref = refs[4 * n + 4 * a:4 * n + 4 * a + 4]
            g = r_ref[0].astype(F32)
            for d in range(1, r_ref.shape[0]):
                g = g + r_ref[d].astype(F32)
            dl, m2, v2 = _adamw_vals(w_ref[...], g, m_ref[...], v_ref[...])
            g_ref[...] = g
            d_ref[...] = dl
            m2_ref[...] = m2
            v2_ref[...] = v2

    def spec3(r):
        return _slab_spec(r.shape[0], r.shape[1], r.shape[2], nb)

    def spec2(w):
        return _slab_spec2(w.shape[0], w.shape[1], nb)

    res, hosted = _hosting_call(
        body, name, nb, host, list(recvs) + list(ws) + list(ms) + list(vs),
        [spec3(r) for r in recvs] + [spec2(w) for w in ws] * 3,
        [jax.ShapeDtypeStruct(w.shape, F32) for w in ws for _ in range(4)],
        [spec2(w) for w in ws for _ in range(4)], [])
    return [res[4 * a:4 * a + 4] for a in range(n)], hosted


def _small_sum(gath, loss_g, row0_g, name):
    _, R, C = gath.shape
    br = R // 3

    def body(g_ref, l_ref, r_ref, go_ref, lo_ref):
        g = g_ref[0].astype(F32)
        lsum = l_ref[0]
        for d in range(1, N_DEV):
            g = g + g_ref[d].astype(F32)
            lsum = lsum + l_ref[d]
        go_ref[...] = g
        lo_ref[...] = lsum

        @pl.when(pl.program_id(0) == 0)
        def _():
            row0 = r_ref[0]
            for d in range(1, N_DEV):
                row0 = row0 + r_ref[d]
            go_ref[0:8, :] = go_ref[0:8, :] + jnp.where(lax.broadcasted_iota(jnp.int32, row0.shape, 0) == 0, row0, 0.0)

    return pl.pallas_call(
        body, name=name, grid=(R // br,),
        in_specs=[pl.BlockSpec((N_DEV, br, C), lambda i: (0, i, 0)),
                  pl.BlockSpec((N_DEV, 8, HD), lambda i: (0, 0, 0)), pl.BlockSpec((N_DEV, 8, C), lambda i: (0, 0, 0))],
        out_specs=[pl.BlockSpec((br, C), lambda i: (i, 0)), pl.BlockSpec((8, HD), lambda i: (0, 0))],
        out_shape=[jax.ShapeDtypeStruct((R, C), F32), jax.ShapeDtypeStruct((8, HD), F32)],
        compiler_params=pltpu.CompilerParams(dimension_semantics=("arbitrary",)),
    )(gath, loss_g, row0_g)


def _adamw_multi(ws, gs, ms, vs, name, nblk=1):
    n = len(ws)

    def body(*refs):
        for a in range(n):
            dl, m2, v2 = _adamw_vals(refs[a][...], refs[n + a][...], refs[2 * n + a][...], refs[3 * n + a][...])
            refs[4 * n + 3 * a][...] = dl
            refs[4 * n + 3 * a + 1][...] = m2
            refs[4 * n + 3 * a + 2][...] = v2

    def spec(x):
        rest = (0,) * (x.ndim - 1)
        return pl.BlockSpec((x.shape[0] // nblk,) + tuple(x.shape[1:]), lambda i: (i,) + rest)

    res = pl.pallas_call(
        body, name=name, grid=(nblk,),
        in_specs=[spec(w) for w in ws] * 4, out_specs=[spec(w) for w in ws for _ in range(3)],
        out_shape=[jax.ShapeDtypeStruct(w.shape, F32) for w in ws for _ in range(3)],
        compiler_params=pltpu.CompilerParams(dimension_semantics=("arbitrary",), vmem_limit_bytes=VMEM_LIMIT),
    )(*ws, *gs, *ms, *vs)
    return [res[3 * a:3 * a + 3] for a in range(n)]


def _s5_param_fn(lr, li, ls, btr, bti):
    step = jnp.exp(ls)
    er = jnp.exp(lr * step)
    ang = li * step
    ar = er * jnp.cos(ang)
    ai = er * jnp.sin(ang)
    nr = ar - 1.0
    den = lr * lr + li * li
    fr = (nr * lr + ai * li) / den
    fi = (ai * lr - nr * li) / den
    return ar, ai, fr * btr - fi * bti, fr * bti + fi * btr


def _s5_params(lr, li, ls, btr, bti, cre, cim):
    nb = S5_G // S5_GB
    GC = S5_GB * S5_C
    expand = jnp.asarray(np.tile(np.eye(S5_P, dtype=np.float32), (1, S5_GB)), BF16)
    own = jnp.asarray((np.arange(GC)[:, None] // S5_C == np.arange(S5_W)[None, :] // S5_P).astype(np.float32))

    def body(lr_ref, li_ref, ls_ref, br_ref, bi_ref, cr_ref, ci_ref, e_ref, own_ref, ar_ref, ai_ref, bm_ref, cm_ref):
        ar, ai, bbr, bbi = _s5_param_fn(lr_ref[...], li_ref[...], ls_ref[...], br_ref[...], bi_ref[...])
        ar_ref[...] = ar
        ai_ref[...] = ai

        def plane(x, n):
            rows = x[n * S5_GB:(n + 1) * S5_GB].reshape(GC, S5_P).astype(BF16)
            return _dot(rows, e_ref[...]) * own_ref[...]

        for n in range(nb):
            bm_ref[n] = jnp.concatenate([plane(bbr, n), plane(bbi, n)], axis=-1).astype(BF16)
            cm_ref[n] = jnp.concatenate([plane(cr_ref[...], n), -plane(ci_ref[...], n)], axis=-1).astype(BF16)

    sd = jax.ShapeDtypeStruct
    return pl.pallas_call(
        body, name="s5_params",
        out_shape=[sd(lr.shape, F32), sd(lr.shape, F32), sd((nb, GC, 2 * S5_W), BF16), sd((nb, GC, 2 * S5_W), BF16)],
        compiler_params=pltpu.CompilerParams(vmem_limit_bytes=VMEM_LIMIT),
    )(lr, li, ls, btr, bti, cre, cim, expand, own)


def _s5_params_bwd(lr, li, ls, btr, bti, dar, dai, dbbr, dbbi):
    def body(lr_ref, li_ref, ls_ref, br_ref, bi_ref, dar_ref, dai_ref, dbbr_ref, dbbi_ref,
             dlr_ref, dli_ref, dls_ref, dbr_ref, dbi_ref):
        _, vjp = jax.vjp(_s5_param_fn, lr_ref[...], li_ref[...], ls_ref[...], br_ref[...], bi_ref[...])
        dlr, dli, dls, dbr, dbi = vjp((dar_ref[...], dai_ref[...], dbbr_ref[...], dbbi_ref[...]))
        dlr_ref[...] = dlr
        dli_ref[...] = dli
        dls_ref[...] = dls
        dbr_ref[...] = dbr
        dbi_ref[...] = dbi

    sd = jax.ShapeDtypeStruct
    return pl.pallas_call(
        body, name="s5_params_bwd",
        out_shape=[sd(lr.shape, F32), sd(lr.shape, F32), sd(ls.shape, F32), sd(btr.shape, F32), sd(btr.shape, F32)],
    )(lr, li, ls, btr, bti, dar, dai, dbbr, dbbi)


def _cpow(ar, ai, n):
    assert n & (n - 1) == 0
    while n > 1:
        ar, ai = ar * ar - ai * ai, 2.0 * ar * ai
        n //= 2
    return ar, ai


def _scan(st, cr, ci, init, nk, reverse, store, prev=None):
    W = S5_W

    def advance(k, sr, si):
        rows = pl.ds(k * 8 if isinstance(k, int) else pl.multiple_of(k * 8, 8), 8)
        nsr = cr * sr - ci * si + st[rows, 0:W]
        nsi = cr * si + ci * sr + st[rows, W:2 * W]
        if store:
            st[rows, 0:W] = nsr
            st[rows, W:2 * W] = nsi
        return nsr, nsi

    if prev is None:
        return lax.fori_loop(0, nk, lambda j, c: advance(nk - 1 - j if reverse else j, c[0], c[1]), init, unroll=2)
    assert reverse

    def step(j, carry):
        k = nk - 1 - j
        nsr, nsi = advance(k, carry[0], carry[1])
        prows = pl.ds(pl.multiple_of((k - 1) * 8, 8), 8)
        pr = prev[prows, 0:W]
        pi = prev[prows, W:2 * W]
        return nsr, nsi, carry[2] + nsr * pr + nsi * pi, carry[3] + nsi * pr - nsr * pi

    carry = lax.fori_loop(0, nk - 1, step, init, unroll=2)
    nsr, nsi = advance(0, carry[0], carry[1])
    return nsr, nsi, carry[2], carry[3]


def _chain(fin, fr, fi, pr, pi, reverse):
    W = S5_W
    fin[:, 0:W] = fr
    fin[:, W:2 * W] = fi
    rowid = lax.broadcasted_iota(jnp.int32, (8, W), 0)
    cr = jnp.zeros((1, W), F32)
    ci = jnp.zeros((1, W), F32)
    init_r = jnp.zeros((8, W), F32)
    init_i = jnp.zeros((8, W), F32)
    for s in (range(7, -1, -1) if reverse else range(8)):
        init_r = jnp.where(rowid == s, cr, init_r)
        init_i = jnp.where(rowid == s, ci, init_i)
        lr = fin[s:s + 1, 0:W]
        li = fin[s:s + 1, W:2 * W]
        cr, ci = lr + pr * cr - pi * ci, li + pr * ci + pi * cr
    return init_r, init_i


def _full_scan(st, fin, ar, ai, nk, reverse, prev=None, carry_in=None, carry_out=None):
    W = S5_W
    cr = jnp.broadcast_to(ar, (8, W))
    ci = jnp.broadcast_to(-ai if reverse else ai, (8, W))
    z = jnp.zeros((8, W), F32)
    if carry_in is None:
        fr, fi = _scan(st, cr, ci, (z, z), nk, reverse, store=False)
        pr, pi = _cpow(ar, -ai if reverse else ai, nk)
        init = _chain(fin, fr, fi, pr, pi, reverse)
    else:
        init = (carry_in[:, 0:W], carry_in[:, W:2 * W])
    if carry_out is not None:
        carry_out[:, 0:W] = init[0]
        carry_out[:, W:2 * W] = init[1]
    if prev is None:
        return _scan(st, cr, ci, init, nk, reverse, store=True)
    return _scan(st, cr, ci, init + (z, z), nk, reverse, store=True, prev=prev)


def _s5_specs(L):
    W2 = 2 * S5_W
    GC = S5_GB * S5_C
    col = pl.BlockSpec((L, GC), lambda g: (0, g))
    vec = pl.BlockSpec((1, GC), lambda g: (0, g))
    avec = pl.BlockSpec((1, S5_W), lambda g: (0, g))
    bmat = pl.BlockSpec((None, GC, W2), lambda g: (g, 0, 0))
    cmat = pl.BlockSpec((None, W2, GC), lambda g: (g, 0, 0))
    return col, vec, avec, bmat, cmat


def _interleave(dst, src, nk):
    for s in range(8):
        dst[pl.ds(s, nk, stride=8), :] = src[s * nk:(s + 1) * nk, :]


def _deinterleave(dst, src, nk):
    for s in range(8):
        dst[s * nk:(s + 1) * nk, :] = src[pl.ds(s, nk, stride=8), :].astype(dst.dtype)


def _hosting_call(body, name, nsteps, host, ins, in_specs, outs, out_specs, scratch):
    grid = (nsteps,) if isinstance(nsteps, int) else tuple(nsteps)
    params = pltpu.CompilerParams(dimension_semantics=("arbitrary",) * len(grid), vmem_limit_bytes=VMEM_LIMIT)
    if host is None:
        res = pl.pallas_call(
            body, name=name, grid=grid, in_specs=in_specs, out_specs=out_specs, out_shape=outs,
            scratch_shapes=scratch, compiler_params=params,
        )(*ins)
        return list(res), []
    n_in, n_out, n_sc = len(ins), len(outs), len(scratch)
    h_in, h_out = len(host.ins), len(host.outs)

    def hosted(*refs):
        a = refs[:n_in]
        ha = refs[n_in:n_in + h_in]
        o = refs[n_in + h_in:n_in + h_in + n_out]
        ho = refs[n_in + h_in + n_out:n_in + h_in + n_out + h_out]
        sc = refs[n_in + h_in + n_out + h_out:n_in + h_in + n_out + h_out + n_sc]
        hs = refs[n_in + h_in + n_out + h_out + n_sc:]
        first = functools.reduce(jnp.logical_and, [pl.program_id(i) == 0 for i in range(len(grid))])
        last = functools.reduce(jnp.logical_and, [pl.program_id(i) == g - 1 for i, g in enumerate(grid)])

        @pl.when(first)
        def _():
            host.start(ha, ho, hs)

        @pl.when(last)
        def _():
            host.mid(ha, ho, hs)

        body(*a, *o, *sc)

        @pl.when(last)
        def _():
            host.finish(ha, ho, hs)

    hbm = pl.BlockSpec(memory_space=pl.ANY)
    res = pl.pallas_call(
        hosted, name=name, grid=grid,
        in_specs=list(in_specs) + [hbm] * h_in, out_specs=list(out_specs) + [hbm] * h_out,
        out_shape=list(outs) + list(host.outs), scratch_shapes=list(scratch) + list(host.scratch),
        compiler_params=params,
    )(*ins, *host.ins)
    return list(res[:n_out]), list(res[n_out:])


def _s5_fwd(u, bm, cm, ar, ai, dvec, host=None):
    L = u.shape[0]
    nk = L // 8
    GC = S5_GB * S5_C
    nb = S5_G // S5_GB
    col, vec, avec, bmat, cmat = _s5_specs(L)

    def body(u_ref, b_ref, c_ref, ar_ref, ai_ref, d_ref, y_ref, carry_ref, st, fin, ui, yi):
        _interleave(ui, u_ref, nk)
        for r in range(8):
            rows = slice(r * nk, (r + 1) * nk)
            st[rows, :] = _dot(ui[rows, :].astype(BF16), b_ref[...])
        _full_scan(st, fin, ar_ref[...], ai_ref[...], nk, reverse=False, carry_out=carry_ref)
        for r in range(8):
            rows = slice(r * nk, (r + 1) * nk)
            yi[rows, :] = _dot_nt(st[rows, :].astype(BF16), c_ref[...]) + d_ref[...] * ui[rows, :]
        _deinterleave(y_ref, yi, nk)

    return _hosting_call(
        body, "s5_fwd", nb, host,
        [u, bm, cm, ar, ai, dvec], [col, bmat, bmat, avec, avec, vec],
        [jax.ShapeDtypeStruct(u.shape, F32), jax.ShapeDtypeStruct((nb * 8, 2 * S5_W), F32)],
        [col, pl.BlockSpec((8, 2 * S5_W), lambda g: (g, 0))],
        [pltpu.VMEM((L, 2 * S5_W), F32), pltpu.VMEM((8, 2 * S5_W), F32), pltpu.VMEM((L, GC), F32),
         pltpu.VMEM((L, GC), F32)])


def _s5_bwd(u, dy, carry, bm, cm, ar, ai, dvec, mask, rmat, host=None):
    L = u.shape[0]
    nk = L // 8
    W = S5_W
    GC = S5_GB * S5_C
    col, vec, avec, bmat, cmat = _s5_specs(L)
    hi = lax.Precision.HIGHEST

    def body(u_ref, dy_ref, carry_ref, b_ref, ct_ref, ar_ref, ai_ref, d_ref, mask_ref, r_ref,
             du_ref, db_ref, dc_ref, dd_ref, dar_ref, dai_ref, sa, sb, fin, ui, dyi, dui):
        ar = ar_ref[...]
        ai = ai_ref[...]
        _interleave(ui, u_ref, nk)
        _interleave(dyi, dy_ref, nk)
        for r in range(8):
            rows = slice(r * nk, (r + 1) * nk)
            sa[rows, :] = _dot(ui[rows, :].astype(BF16), b_ref[...])
            sb[rows, :] = _dot(dyi[rows, :].astype(BF16), ct_ref[...])
        _full_scan(sa, fin, ar, ai, nk, reverse=False, carry_in=carry_ref)
        gr, gi, accr, acci = _full_scan(sb, fin, ar, ai, nk, reverse=True, prev=sa)
        rowid = lax.broadcasted_iota(jnp.int32, (8, W), 0)
        last = pl.ds((nk - 1) * 8, 8)
        pr = jnp.where(rowid == 0, 0.0, pltpu.roll(sa[last, 0:W], 1, 0))
        pi = jnp.where(rowid == 0, 0.0, pltpu.roll(sa[last, W:2 * W], 1, 0))
        accr = accr + gr * pr + gi * pi
        acci = acci + gi * pr - gr * pi
        dar_ref[...] = jnp.sum(accr, axis=0, keepdims=True)
        dai_ref[...] = jnp.sum(acci, axis=0, keepdims=True)
        dbf = jnp.zeros((GC, 2 * W), F32)
        dcf = jnp.zeros((GC, 2 * W), F32)
        dd = jnp.zeros((1, GC), F32)
        for r in range(8):
            rows = slice(r * nk, (r + 1) * nk)
            ub = ui[rows, :]
            dyb = dyi[rows, :]
            gb = sb[rows, :].astype(BF16)
            dui[rows, :] = _dot_nt(gb, b_ref[...]) + d_ref[...] * dyb
            dbf = dbf + _dot_tn(ub.astype(BF16), gb)
            dcf = dcf + _dot_tn(dyb.astype(BF16), sa[rows, :].astype(BF16))
            dd = dd + jnp.sum(dyb * ub, axis=0, keepdims=True)
        db_ref[...] = jnp.dot(dbf * mask_ref[...], r_ref[...], precision=hi, preferred_element_type=F32)
        dc_ref[...] = jnp.dot(dcf * mask_ref[...], r_ref[...], precision=hi, preferred_element_type=F32)
        dd_ref[...] = dd
        _deinterleave(du_ref, dui, nk)

    cmp_spec = pl.BlockSpec((GC, 2 * S5_P), lambda g: (g, 0))
    whole = lambda shape: pl.BlockSpec(shape, lambda g: (0, 0))
    sd = jax.ShapeDtypeStruct
    return _hosting_call(
        body, "s5_bwd", S5_G // S5_GB, host,
        [u, dy, carry, bm, cm, ar, ai, dvec, mask, rmat],
        [col, col, pl.BlockSpec((8, 2 * W), lambda g: (g, 0)), bmat, bmat, avec, avec, vec, whole(mask.shape),
         whole(rmat.shape)],
        [sd(u.shape, BF16), sd((S5_G * S5_C, 2 * S5_P), F32), sd((S5_G * S5_C, 2 * S5_P), F32),
         sd((1, PRIM), F32), sd((1, S5_G * S5_P), F32), sd((1, S5_G * S5_P), F32)],
        [col, cmp_spec, cmp_spec, vec, avec, avec],
        [pltpu.VMEM((L, 2 * W), F32), pltpu.VMEM((L, 2 * W), F32), pltpu.VMEM((8, 2 * W), F32),
         pltpu.VMEM((L, GC), F32), pltpu.VMEM((L, GC), F32), pltpu.VMEM((L, GC), F32)])


def _s5_compact_consts():
    g_row = np.arange(S5_GB * S5_C) // S5_C
    col = np.arange(2 * S5_W)
    g_col = (col % S5_W) // S5_P
    mask = (g_row[:, None] == g_col[None, :]).astype(np.float32)
    tgt = (col // S5_W) * S5_P + col % S5_P
    rmat = (tgt[:, None] == np.arange(2 * S5_P)[None, :]).astype(np.float32)
    return jnp.asarray(mask), jnp.asarray(rmat)


def _attn_scores(q_ref, k_ref, qb, bq, scale):
    ext = (qb + 1) * bq
    s = _dot_nt(q_ref[qb * bq:ext, :], k_ref[0:ext, :]) * scale
    qpos = lax.broadcasted_iota(jnp.int32, (bq, bq), 0)
    kpos = lax.broadcasted_iota(jnp.int32, (bq, bq), 1)
    diag = jnp.where(kpos <= qpos, s[:, ext - bq:], NEG)
    return diag if qb == 0 else jnp.concatenate([s[:, :ext - bq], diag], axis=-1)


def _attn_fwd(qp, kp, v, scale):
    L = qp.shape[0]
    bq = min(256, L)

    def body(q_ref, k_ref, v_ref, o_ref, lse_ref):
        for qb in range(L // bq):
            rows = slice(qb * bq, (qb + 1) * bq)
            s = _attn_scores(q_ref, k_ref, qb, bq, scale)
            m = jnp.max(s, axis=-1, keepdims=True)
            e = jnp.exp(s - m)
            l = jnp.sum(e, axis=-1, keepdims=True)
            o_ref[rows, :] = _dot(e.astype(BF16), v_ref[0:(qb + 1) * bq, :]) / l
            lse_ref[rows, :] = jnp.broadcast_to(m + jnp.log(l), (bq, HD))

    blk = pl.BlockSpec((L, HD), lambda h: (0, h))
    wide = pl.BlockSpec((L, 2 * HD), lambda h: (0, h))
    return pl.pallas_call(
        body, name="mla_attn_fwd", grid=(MLA_H,),
        in_specs=[wide, wide, blk], out_specs=[blk, blk],
        out_shape=[jax.ShapeDtypeStruct((L, MLA_H * HD), F32)] * 2,
        compiler_params=pltpu.CompilerParams(dimension_semantics=("arbitrary",), vmem_limit_bytes=VMEM_LIMIT),
    )(qp, kp, v)


def _attn_bwd(qp, kp, v, o, lse, do, scale):
    L = qp.shape[0]
    bq = min(256, L)
    nq = L // bq

    def body(q_ref, k_ref, v_ref, o_ref, lse_ref, do_ref, dq_ref, dk_ref, dv_ref, dk_acc, dv_acc):
        dk_acc[...] = jnp.zeros_like(dk_acc)
        dv_acc[...] = jnp.zeros_like(dv_acc)
        for qb in range(nq):
            rows = slice(qb * bq, (qb + 1) * bq)
            ext = (qb + 1) * bq
            do = do_ref[rows, :]
            dob = do.astype(BF16)
            p = jnp.exp(_attn_scores(q_ref, k_ref, qb, bq, scale) - lse_ref[rows, 0:1])
            dp = _dot_nt(dob, v_ref[0:ext, :])
            dsum = jnp.sum(do * o_ref[rows, :], axis=-1, keepdims=True)
            ds = (p * (dp - dsum) * scale).astype(BF16)
            dq_ref[rows, :] = _dot(ds, k_ref[0:ext, :]).astype(dq_ref.dtype)
            dk_acc[0:ext, :] += _dot_tn(ds, q_ref[rows, :])
            dv_acc[0:ext, :] += _dot_tn(p.astype(BF16), dob)
        dk_ref[...] = dk_acc[...].astype(dk_ref.dtype)
        dv_ref[...] = dv_acc[...].astype(dv_ref.dtype)

    sd = jax.ShapeDtypeStruct
    blk = pl.BlockSpec((L, HD), lambda h: (0, h))
    wide = pl.BlockSpec((L, 2 * HD), lambda h: (0, h))
    return pl.pallas_call(
        body, name="mla_attn_bwd", grid=(MLA_H,),
        in_specs=[wide, wide, blk, blk, blk, blk], out_specs=[wide, wide, blk],
        out_shape=[sd((L, MLA_H * 2 * HD), BF16), sd((L, MLA_H * 2 * HD), BF16), sd((L, MLA_H * HD), BF16)],
        scratch_shapes=[pltpu.VMEM((L, 2 * HD), F32), pltpu.VMEM((L, HD), F32)],
        compiler_params=pltpu.CompilerParams(dimension_semantics=("arbitrary",), vmem_limit_bytes=VMEM_LIMIT),
    )(qp, kp, v, o, lse, do)


def _kv_fn(mem, gm, w, gk):
    kv = _mm(_rms(mem, gm, D_MODEL), w)
    k = jnp.concatenate([_rms(kv[:, HD * h:HD * (h + 1)], gk, HD) for h in range(X_HEADS)], axis=-1)
    return k, kv[:, XQ:]


def _kv_prep(mem, gm, w, gk, name):
    def fn(mem, gm, w, gk):
        return _kv_fn(mem, gm, w, gk)
    M = mem.shape[0]
    return _rowwise(name, fn, [('c', mem), ('c', gm), ('c', w), ('c', gk)],
                    [('c', (M, XQ), F32), ('c', (M, XQ), F32)], 1)


def _kv_prep_bwd(mem, gm, w, gk, dk, dv, name):
    def fn(mem, gm, w, gk, dk, dv):
        _, vjp = jax.vjp(lambda a, b, c: _kv_fn(mem, a, b, c), gm, w, gk)
        return vjp((dk, dv))
    return _rowwise(name, fn, [('c', mem), ('c', gm), ('c', w), ('c', gk), ('c', dk), ('c', dv)],
                    [('c', gm.shape, F32), ('c', w.shape, BF16), ('c', gk.shape, F32)], 1)


def _forward_merge(x, mix, mix_kind, xq, gate, k, v, gq, wout, name, nblk, host=None):
    def fn(x, mix, xq, gate, k, v, gq, wout):
        o = _merge(mix, xq, gate, k, v, gq)
        return (x + _dot(o.astype(BF16), wout),)
    L = x.shape[0]
    out = _rowwise(name, fn, [('r', x), (mix_kind, mix), ('r', xq), ('r', gate), ('c', k), ('c', v), ('c', gq),
                              ('c', wout)], [('r', (L, D_MODEL), F32)], nblk, host=host)
    return out[0] if host is None else (out[0][0], out[1])


def _backward_merge(dx, mix, mix_kind, xq, gate, k, v, gq, wout, name, nblk, host=None):
    def fn(dx, mix, xq, gate, k, v, gq, wout):
        g16 = dx.astype(BF16)
        do = _dot_nt(g16, wout)
        o, vjp = jax.vjp(_merge, mix, xq, gate, k, v, gq)
        dmix, dxq, dgate, dk, dv, dgq = vjp(do)
        return dmix, dxq, dgate, o, g16, dk, dv, dgq
    L = dx.shape[0]
    return _rowwise(
        name, fn,
        [('r', dx), (mix_kind, mix), ('r', xq), ('r', gate), ('c', k), ('c', v), ('c', gq), ('c', wout)],
        [('r', (L, PRIM), F32), ('r', (L, XQ), BF16), ('r', (L, BRANCH), BF16), ('t', (BRANCH, L), BF16),
         ('r', (L, D_MODEL), BF16), ('a', k.shape, F32), ('a', v.shape, F32), ('a', gq.shape, F32)], nblk,
        host=host)


_MLA_IN = 3392
_MLA_IN_PAD = 3456


def _uq_rows(wt):
    r = wt.reshape(MLA_H, HD + ROPE, wt.shape[1])
    return jnp.concatenate([r[:, :HD].reshape(PRIM, -1),
                            jnp.pad(r[:, HD:], ((0, 0), (0, HD - ROPE), (0, 0))).reshape(PRIM, -1)], axis=0)


def _uq_rows_back(wt):
    nope = wt[:PRIM].reshape(MLA_H, HD, -1)
    rope = wt[PRIM:].reshape(MLA_H, HD, -1)[:, :ROPE]
    return jnp.concatenate([nope, rope], axis=1).reshape(MLA_H * (HD + ROPE), -1)


def _mla_in_rows(wt):
    return jnp.concatenate([wt[:768], wt[832:], wt[768:832], jnp.zeros((64, wt.shape[1]), wt.dtype)], axis=0)


def _mla_in_rows_back(wt):
    return jnp.concatenate([wt[:768], wt[3328:3392], wt[768:3328]], axis=0)


_SMALL = (("ln_gain", 2048), ("mem_norm", 2048), ("xq_norm", 256), ("xk_norm", 256), ("s5_lambda_re", 6144),
          ("s5_lambda_im", 6144), ("s5_log_step", 96), ("s5_b_re", 98304), ("s5_b_im", 98304), ("s5_c_re", 98304),
          ("s5_c_im", 98304), ("s5_d", 1536), ("mla_q_lora_norm", 512), ("mla_kv_lora_norm", 256),
          ("mla_q_nope_norm", 128), ("mla_k_nope_norm", 128), ("mla_q_rope_norm", 64), ("mla_k_rope_norm", 64))
_SMALL_ROWS = 432
_SMALL_OFF = {name: sum(n for _, n in _SMALL[:i]) for i, (name, _) in enumerate(_SMALL)}


def _pack_small(d):
    flat = jnp.concatenate([d[n].reshape(-1).astype(F32) for n, _ in _SMALL])
    return jnp.pad(flat, (0, _SMALL_ROWS * 1024 - flat.shape[0])).reshape(_SMALL_ROWS, 1024)


def _unpack_small(p, name, shape):
    off = _SMALL_OFF[name]
    return p.reshape(-1)[off:off + int(np.prod(shape))].reshape(shape)


_WEIGHTS = ('ln_gain', 'w_out', 'mem_norm', 'w_mem_kv', 'xq_norm', 'xk_norm', 's5_w_in', 's5_lambda_re',
            's5_lambda_im', 's5_log_step', 's5_b_re', 's5_b_im', 's5_c_re', 's5_c_im', 's5_d', 's5_w_glu', 'mla_w_in',
            'mla_q_lora_norm', 'mla_kv_lora_norm', 'mla_w_uq', 'mla_w_ukv', 'mla_q_nope_norm', 'mla_k_nope_norm',
            'mla_q_rope_norm', 'mla_k_rope_norm')
_BIG = ('w_out', 'w_mem_kv', 's5_w_in', 's5_w_glu', 'mla_w_in', 'mla_w_uq', 'mla_w_ukv')


def _pad128(g):
    return jnp.pad(g.reshape(1, -1), ((0, 0), (0, HD - g.shape[-1])))


def kernel(x, mem, positions, ln_gain, w_out, mem_norm, w_mem_kv, xq_norm, xk_norm, s5_w_in, s5_lambda_re, s5_lambda_im, s5_log_step, s5_b_re, s5_b_im, s5_c_re, s5_c_im, s5_d, s5_w_glu, mla_w_in, mla_q_lora_norm, mla_kv_lora_norm, mla_w_uq, mla_w_ukv, mla_q_nope_norm, mla_k_nope_norm, mla_q_rope_norm, mla_k_rope_norm, loss_target, m_ln_gain, m_w_out, m_mem_norm, m_w_mem_kv, m_xq_norm, m_xk_norm, m_s5_w_in, m_s5_lambda_re, m_s5_lambda_im, m_s5_log_step, m_s5_b_re, m_s5_b_im, m_s5_c_re, m_s5_c_im, m_s5_d, m_s5_w_glu, m_mla_w_in, m_mla_q_lora_norm, m_mla_kv_lora_norm, m_mla_w_uq, m_mla_w_ukv, m_mla_q_nope_norm, m_mla_k_nope_norm, m_mla_q_rope_norm, m_mla_k_rope_norm, v_ln_gain, v_w_out, v_mem_norm, v_w_mem_kv, v_xq_norm, v_xk_norm, v_s5_w_in, v_s5_lambda_re, v_s5_lambda_im, v_s5_log_step, v_s5_b_re, v_s5_b_im, v_s5_c_re, v_s5_c_im, v_s5_d, v_s5_w_glu, v_mla_w_in, v_mla_q_lora_norm, v_mla_kv_lora_norm, v_mla_w_uq, v_mla_w_ukv, v_mla_q_nope_norm, v_mla_k_nope_norm, v_mla_q_rope_norm, v_mla_k_rope_norm):
    weights = dict(ln_gain=ln_gain, w_out=w_out, mem_norm=mem_norm, w_mem_kv=w_mem_kv, xq_norm=xq_norm,
                   xk_norm=xk_norm, s5_w_in=s5_w_in, s5_lambda_re=s5_lambda_re, s5_lambda_im=s5_lambda_im,
                   s5_log_step=s5_log_step, s5_b_re=s5_b_re, s5_b_im=s5_b_im, s5_c_re=s5_c_re, s5_c_im=s5_c_im,
                   s5_d=s5_d, s5_w_glu=s5_w_glu, mla_w_in=mla_w_in, mla_q_lora_norm=mla_q_lora_norm,
                   mla_kv_lora_norm=mla_kv_lora_norm, mla_w_uq=mla_w_uq, mla_w_ukv=mla_w_ukv,
                   mla_q_nope_norm=mla_q_nope_norm, mla_k_nope_norm=mla_k_nope_norm,
                   mla_q_rope_norm=mla_q_rope_norm, mla_k_rope_norm=mla_k_rope_norm)
    m_in = dict(zip(_WEIGHTS, (m_ln_gain, m_w_out, m_mem_norm, m_w_mem_kv, m_xq_norm, m_xk_norm, m_s5_w_in,
                               m_s5_lambda_re, m_s5_lambda_im, m_s5_log_step, m_s5_b_re, m_s5_b_im, m_s5_c_re,
                               m_s5_c_im, m_s5_d, m_s5_w_glu, m_mla_w_in, m_mla_q_lora_norm, m_mla_kv_lora_norm,
                               m_mla_w_uq, m_mla_w_ukv, m_mla_q_nope_norm, m_mla_k_nope_norm, m_mla_q_rope_norm,
                               m_mla_k_rope_norm)))
    v_in = dict(zip(_WEIGHTS, (v_ln_gain, v_w_out, v_mem_norm, v_w_mem_kv, v_xq_norm, v_xk_norm, v_s5_w_in,
                               v_s5_lambda_re, v_s5_lambda_im, v_s5_log_step, v_s5_b_re, v_s5_b_im, v_s5_c_re,
                               v_s5_c_im, v_s5_d, v_s5_w_glu, v_mla_w_in, v_mla_q_lora_norm, v_mla_kv_lora_norm,
                               v_mla_w_uq, v_mla_w_ukv, v_mla_q_nope_norm, v_mla_k_nope_norm, v_mla_q_rope_norm,
                               v_mla_k_rope_norm)))

    x0 = x[0]
    mem0 = mem[0]
    target = loss_target[0]
    L = x0.shape[0]
    nblk = 4
    nb_big = 8
    me = 4 * lax.axis_index("x") + 2 * lax.axis_index("y") + lax.axis_index("c")

    lora = jnp.pad(jnp.concatenate([mla_q_lora_norm, mla_kv_lora_norm], axis=1), ((0, 7), (0, HD - 96)))
    def gather(*shards):
        return _plan_all_gather(list(shards))

    kh = D_MODEL // 2
    (b_mkv0, b_glu, b_in_mla, b_out0, b_uq, b_ukv, b_mkv1, b_out1), (W_in_s5,) = _cast_call(
        [w_mem_kv[0], s5_w_glu[0], jnp.transpose(mla_w_in[0]), w_out[0], jnp.transpose(mla_w_uq[0]), mla_w_ukv[0],
         w_mem_kv[1], w_out[1]], "cast_shards", host=gather(s5_w_in[0].astype(BF16)))

    ln0, ln1 = ln_gain[0:1], ln_gain[1:2]
    gq0, gq1 = xq_norm[0:1], xq_norm[1:2]
    gk0, gk1 = xk_norm[0:1], xk_norm[1:2]
    gm0, gm1 = mem_norm[0:1], mem_norm[1:2]
    gqn, gkn = mla_q_nope_norm, mla_k_nope_norm
    gqr, gkr = _pad128(mla_q_rope_norm), _pad128(mla_k_rope_norm)

    lr3 = s5_lambda_re.reshape(S5_G, 1, S5_P)
    li3 = s5_lambda_im.reshape(S5_G, 1, S5_P)
    ls3 = s5_log_step.reshape(S5_G, 1, 1)
    btr = jnp.swapaxes(s5_b_re[0], 1, 2)
    bti = jnp.swapaxes(s5_b_im[0], 1, 2)
    a_r, a_i, bm, cm = _s5_params(lr3, li3, ls3, btr, bti, s5_c_re[0], s5_c_im[0])
    a_r2 = a_r.reshape(1, S5_G * S5_P)
    a_i2 = a_i.reshape(1, S5_G * S5_P)
    cmask, rmat = _s5_compact_consts()

    half = ROPE // 2
    inv_freq = ROPE_THETA ** (-jnp.arange(half, dtype=F32) / half)
    invf = jnp.concatenate([inv_freq, inv_freq, jnp.zeros((HD - ROPE,), F32)]).reshape(1, HD)

    def rot_tables(pos, invf):
        ang = pos.astype(F32) * invf
        lane = lax.broadcasted_iota(jnp.int32, ang.shape, 1)
        c = jnp.where(lane < ROPE, jnp.cos(ang), 0.0)
        s = jnp.sin(ang)
        return c, jnp.where(lane < half, -s, 0.0), jnp.where((lane >= half) & (lane < ROPE), s, 0.0)

    tc, ts1, ts2 = _rowwise("rot_tables", rot_tables, [('r', positions.reshape(L, 1)), ('c', invf)],
                            [('r', (L, HD), F32)] * 3, nblk)

    def in_s5(x, g, w):
        proj = _mm_slots(_rms(x, g, D_MODEL).astype(BF16), w)
        return proj[:, :PRIM], proj[:, PRIM:PRIM + XQ], proj[:, PRIM + XQ:]

    u_s5, xq_a, gate_a = _rowwise(
        "s5_in", in_s5, [('r', x0), ('c', ln0), ('c', W_in_s5)],
        [('r', (L, PRIM), F32), ('r', (L, XQ), F32), ('r', (L, BRANCH), F32)], nblk)
    (y_s5, s5_carry), (W_glu, G_mkv0, G_in_mla_a) = _s5_fwd(u_s5, bm, cm, a_r2, a_i2, s5_d,
                                                            host=gather(b_glu, b_mkv0, b_in_mla[:, :kh]))

    def glu(y, w):
        z = _mm_slots(_gelu(y).astype(BF16), w)
        return z[:, :PRIM] * _sigmoid(z[:, PRIM:]), z

    (y2, z_glu), (G_out0,) = _rowwise("s5_glu", glu, [('r', y_s5), ('c', W_glu)],
                                      [('r', (L, PRIM), F32), ('r', (L, 2 * PRIM), F32)], nblk, host=gather(b_out0))
    W_mkv0 = G_mkv0.reshape(D_MODEL, 2 * XQ)
    k_a, v_a = _kv_prep(mem0, gm0, W_mkv0, gk0, "kv_prep0")
    x1, (G_in_mla_b,) = _forward_merge(
        x0, y2, 'r', xq_a, gate_a, k_a, v_a, gq0, G_out0.reshape(BRANCH, D_MODEL), "merge0", nblk,
        host=gather(b_in_mla[:, kh:]))
    W_in_mla = _mla_in_rows(jnp.concatenate([G_in_mla_a, G_in_mla_b], axis=2).reshape(_MLA_IN, D_MODEL))

    def in_mla(x, g, w):
        proj = _dot_nt(_rms(x, g, D_MODEL).astype(BF16), w)
        return proj[:, :512], proj[:, 512:768], proj[:, 768:1280], proj[:, 1280:3328], proj[:, 3328:]

    (c_q, c_kv, xq_b, gate_b, krp), (G_uq, W_kv, G_lora) = _rowwise(
        "mla_in", in_mla, [('r', x1), ('c', ln1), ('c', W_in_mla)],
        [('r', (L, Q_LORA), F32), ('r', (L, KV_LORA), F32), ('r', (L, XQ), F32), ('r', (L, BRANCH), F32),
         ('r', (L, HD), F32)], nblk,
        host=gather(b_uq, b_ukv, lora))
    W_q = _uq_rows(G_uq.reshape(MLA_H * (HD + ROPE), Q_LORA))
    g_qlora = G_lora[:, 0, :64].reshape(1, Q_LORA)
    g_kvlora = G_lora[:, 0, 64:96].reshape(1, KV_LORA)

    def qkv(c_q, c_kv, krp, tc, ts1, ts2, gql, gkvl, wq, wkv, gqn, gkn, gqr, gkr):
        q = _dot_nt(_rms(c_q, gql, Q_LORA).astype(BF16), wq)
        kv = _mm_slots(_rms(c_kv, gkvl, KV_LORA).astype(BF16), wkv)
        kp, v = _kv_post(*_kv_chunks(kv), krp, gkn, gkr, tc, ts1, ts2)
        return _q_post(*_q_chunks(q), gqn, gqr, tc, ts1, ts2), kp, v

    qkv_consts = [('c', g_qlora), ('c', g_kvlora), ('c', W_q), ('c', W_kv), ('c', gqn), ('c', gkn), ('c', gqr),
                  ('c', gkr)]
    (q_pad, k_pad, v_h), (G_mkv1, G_out1) = _rowwise(
        "mla_qkv", qkv, [('r', c_q), ('r', c_kv), ('r', krp), ('r', tc), ('r', ts1), ('r', ts2)] + qkv_consts,
        [('r', (L, 2 * PRIM), BF16), ('r', (L, 2 * PRIM), BF16), ('r', (L, PRIM), BF16)], nblk,
        host=gather(b_mkv1, b_out1))
    W_out = (G_out0.reshape(BRANCH, D_MODEL), G_out1.reshape(BRANCH, D_MODEL))
    W_mkv = (W_mkv0, G_mkv1.reshape(D_MODEL, 2 * XQ))
    scale = (HD + ROPE) ** -0.5
    attn, lse = _attn_fwd(q_pad, k_pad, v_h, scale)
    k_b, v_b = _kv_prep(mem0, gm1, W_mkv[1], gk1, "kv_prep1")

    def merge_loss(x, mix, xq, gate, k, v, gq, wout, t):
        err = x + _dot(_merge(mix, xq, gate, k, v, gq).astype(BF16), wout) - t
        part = 0.5 * jnp.sum(jnp.sum(err * err, axis=-1, keepdims=True) * (1.0 / D_MODEL), axis=0, keepdims=True)
        return err * (1.0 / D_MODEL), jnp.broadcast_to(part, (1, HD))

    dx2, loss_part = _rowwise(
        "merge1_loss", merge_loss,
        [('r', x1), ('r', attn), ('r', xq_b), ('r', gate_b), ('c', k_b), ('c', v_b), ('c', gq1), ('c', W_out[1]),
         ('r', target)], [('r', (L, D_MODEL), F32), ('a', (1, HD), F32)], nblk)

    dattn, dxq_b, dgate_b, o_b, g_b, dk_b, dv_b, dgq1 = _backward_merge(
        dx2, attn, 'r', xq_b, gate_b, k_b, v_b, gq1, W_out[1], "merge1_bwd", nb_big)
    dgm1, dW_mkv1, dgk1 = _kv_prep_bwd(mem0, gm1, W_mkv[1], gk1, dk_b, dv_b, "kv_prep1_bwd")
    dW_out1 = _matmul_tn(o_b, g_b, "dw_out1")
    dq_pad, dk_pad, dv_h = _attn_bwd(q_pad, k_pad, v_h, attn, lse, dattn, scale)

    def qkv_bwd(c_q, c_kv, krp, tc, ts1, ts2, dqp, dkp, dv, gql, gkvl, wq, wkv, gqn, gkn, gqr, gkr):
        cqn, vjp_qn = jax.vjp(lambda a, b: _rms(a, b, Q_LORA), c_q, gql)
        ckvn, vjp_kvn = jax.vjp(lambda a, b: _rms(a, b, KV_LORA), c_kv, gkvl)
        cqn16 = cqn.astype(BF16)
        ckvn16 = ckvn.astype(BF16)
        q = _dot_nt(cqn16, wq)
        kv = _mm_slots(ckvn16, wkv)
        _, vjp_q = jax.vjp(lambda n, r, a, b: _q_post(n, r, a, b, tc, ts1, ts2), *_q_chunks(q), gqn, gqr)
        dnope, drope, dgqn, dgqr = vjp_q(dqp.astype(F32))
        dq = jnp.concatenate(dnope + drope, axis=-1)
        _, vjp_kv = jax.vjp(lambda n, v, k, a, b: _kv_post(n, v, k, a, b, tc, ts1, ts2), *_kv_chunks(kv), krp, gkn,
                            gkr)
        dkn, dvals, dkrp, dgkn, dgkr = vjp_kv((dkp.astype(F32), dv.astype(F32)))
        dkv = jnp.concatenate([x for pair in zip(dkn, dvals) for x in pair], axis=-1)
        dq16 = dq.astype(BF16)
        dkv16 = dkv.astype(BF16)
        dc_q, dgql = vjp_qn(_dot(dq16, wq))
        dc_kv, dgkvl = vjp_kvn(_mm_slots_nt(dkv16, wkv))
        return dc_q, dc_kv, dkrp, cqn16, dq16, ckvn16, dkv16, dgql, dgkvl, dgqn, dgkn, dgqr, dgkr

    (dc_q, dc_kv, dkrp, cqn16, dq16, ckvn16, dkv16, dgql, dgkvl, dgqn, dgkn, dgqr, dgkr) = _rowwise(
        "mla_qkv_bwd", qkv_bwd,
        [('r', c_q), ('r', c_kv), ('r', krp), ('r', tc), ('r', ts1), ('r', ts2), ('r', dq_pad), ('r', dk_pad),
         ('r', dv_h)] + qkv_consts,
        [('r', (L, Q_LORA), BF16), ('r', (L, KV_LORA), BF16), ('r', (L, HD), BF16), ('r', (L, Q_LORA), BF16),
         ('t', (2 * PRIM, L), BF16), ('t', (KV_LORA, L), BF16), ('r', (L, 2 * PRIM), BF16),
         ('a', (1, Q_LORA), F32), ('a', (1, KV_LORA), F32), ('a', (1, HD), F32), ('a', (1, HD), F32),
         ('a', (1, HD), F32), ('a', (1, HD), F32)], nb_big)
    dW_q = _matmul_tn(dq16, cqn16, "dw_uq")
    dW_kv = _matmul_tn_slots(ckvn16, dkv16, "dw_ukv")

    def in_bwd(x, dres, g, w, *dparts):
        dproj = jnp.concatenate(dparts, axis=-1).astype(BF16)
        xn, vjp = jax.vjp(lambda a, b: _rms(a, b, D_MODEL), x, g)
        dx, dg = vjp(_mm_slots_nt(dproj, w) if w.ndim == 3 else _dot(dproj, w))
        return dx + dres, xn, dproj, dg

    dx1, xn1, dproj1, dln1 = _rowwise(
        "mla_in_bwd", in_bwd,
        [('r', x1), ('r', dx2), ('c', ln1), ('c', W_in_mla), ('r', dc_q), ('r', dc_kv), ('r', dxq_b), ('r', dgate_b),
         ('r', dkrp)],
        [('r', (L, D_MODEL), F32), ('r', (L, D_MODEL), BF16), ('t', (_MLA_IN_PAD, L), BF16), ('a', (1, D_MODEL), F32)],
        nblk)
    dW_in_mla = _matmul_tn(dproj1, xn1, "dw_mla_in")

    grads1 = [dW_out1.reshape(N_DEV, 256, D_MODEL), dW_mkv1.reshape(N_DEV, 128, 2 * XQ),
              _mla_in_rows_back(dW_in_mla).reshape(N_DEV, 424, D_MODEL),
              _uq_rows_back(dW_q).reshape(N_DEV, 288, Q_LORA), dW_kv]
    (dy2, dxq_a, dgate_a, o_a, g_a, dk_a, dv_a, dgq0), pair1 = _backward_merge(
        dx1, y2, 'r', xq_a, gate_a, k_a, v_a, gq0, W_out[0], "merge0_bwd", nb_big, host=_plan_pair(grads1))
    dgm0, dW_mkv0, dgk0 = _kv_prep_bwd(mem0, gm0, W_mkv[0], gk0, dk_a, dv_a, "kv_prep0_bwd")
    dW_out0 = _matmul_tn(o_a, g_a, "dw_out0")
    t1 = list(_pair_add(grads1, pair1, "rs_add_layer1"))

    def glu_bwd(y, z, dy2, w):
        h, vjp_h = jax.vjp(_gelu, y)
        _, vjp_z = jax.vjp(lambda a, b: a * _sigmoid(b), z[:, :PRIM], z[:, PRIM:])
        dz16 = jnp.concatenate(vjp_z(dy2), axis=-1).astype(BF16)
        return vjp_h(_mm_slots_nt(dz16, w))[0], h.astype(BF16), dz16

    grads0 = [dW_out0.reshape(N_DEV, 256, D_MODEL), dW_mkv0.reshape(N_DEV, 128, 2 * XQ)]
    (dy_s5, h16, dz16), glu_hosted = _rowwise(
        "s5_glu_bwd", glu_bwd, [('r', y_s5), ('r', z_glu), ('r', dy2), ('c', W_glu)],
        [('r', (L, PRIM), F32), ('t', (PRIM, L), BF16), ('r', (L, 2 * PRIM), BF16)], nb_big,
        host=_combine(_plan_chips(t1[2:3]), _plan_pair(grads0)))
    recv_in_mla, pair0 = glu_hosted[:1], glu_hosted[1:]
    dW_glu = _matmul_tn_slots(h16, dz16, "dw_glu")
    t0 = list(_pair_add(grads0 + [dW_glu], pair0 + list(_exchange_call(_plan_pair([dW_glu]), "rs_pair_glu")),
                        "rs_add_layer0"))
    (du_s5, dbc, dcc, dd, dar, dai), recv_rest = _s5_bwd(u_s5, dy_s5, s5_carry, bm, cm, a_r2, a_i2, s5_d,
                                                        cmask, rmat, host=_plan_chips(t1[:2] + t1[3:] + t0))
    early_recv = recv_rest[:2] + recv_in_mla + recv_rest[2:]
    dbc4 = dbc.reshape(S5_G, S5_C, 2, S5_P)
    dcc4 = dcc.reshape(S5_G, S5_C, 2, S5_P)
    dlr, dli, dls, dbtr, dbti = _s5_params_bwd(
        lr3, li3, ls3, btr, bti, dar.reshape(S5_G, 1, S5_P), dai.reshape(S5_G, 1, S5_P), dbc4[:, :, 0], dbc4[:, :, 1])

    small_part = {
        "ln_gain": jnp.concatenate([jnp.zeros_like(dln1), dln1]), "mem_norm": jnp.concatenate([dgm0, dgm1]),
        "xq_norm": jnp.concatenate([dgq0, dgq1]), "xk_norm": jnp.concatenate([dgk0, dgk1]),
        "s5_lambda_re": dlr, "s5_lambda_im": dli, "s5_log_step": dls,
        "s5_b_re": jnp.swapaxes(dbtr, 1, 2), "s5_b_im": jnp.swapaxes(dbti, 1, 2),
        "s5_c_re": dcc4[:, :, 0], "s5_c_im": -dcc4[:, :, 1], "s5_d": dd,
        "mla_q_lora_norm": dgql, "mla_kv_lora_norm": dgkvl, "mla_q_nope_norm": dgqn, "mla_k_nope_norm": dgkn,
        "mla_q_rope_norm": dgqr[:, :ROPE], "mla_k_rope_norm": dgkr[:, :ROPE],
    }
    loss8 = jnp.pad(loss_part, ((0, 7), (0, 0)))
    (dx0, xn0, dproj0, dln0), (small_gath, loss_g) = _rowwise(
        "s5_in_bwd", in_bwd,
        [('r', x0), ('r', dx1), ('c', ln0), ('c', W_in_s5), ('r', du_s5), ('r', dxq_a),
         ('r', dgate_a)],
        [('r', (L, D_MODEL), F32), ('t', (D_MODEL, L), BF16), ('r', (L, 2 * BRANCH), BF16), ('a', (1, D_MODEL), F32)],
        nblk, host=_plan_all_gather([_pack_small(small_part).astype(BF16), loss8]))
    dW_in_s5 = _matmul_tn_slots(xn0, dproj0, "dw_s5_in")

    late = [dW_in_s5]
    late_t = _pair_add(late, list(_exchange_call(_plan_pair(late), "rs_pair_late")), "rs_add_late")
    owners = [("w_out", 1), ("w_mem_kv", 1), ("mla_w_in", 0), ("mla_w_uq", 0), ("mla_w_ukv", 0), ("w_out", 0),
              ("w_mem_kv", 0), ("s5_w_glu", 0)]
    flipped = ("mla_w_in", "mla_w_uq")

    def shard(d, n, i):
        return jnp.transpose(d[n][i]) if n in flipped else d[n][i]

    upd, (late_recv, ln0_gath) = _updates_call(
        early_recv, [shard(weights, n, i) for n, i in owners], [shard(m_in, n, i) for n, i in owners],
        [shard(v_in, n, i) for n, i in owners], "update_early",
        host=_combine(_plan_chips(late_t), _plan_all_gather([jnp.pad(dln0, ((0, 7), (0, 0)))])))
    owners.append(("s5_w_in", 0))
    upd.append(_sum_adamw(late_recv, s5_w_in[0], m_s5_w_in[0], v_s5_w_in[0], "update_s5_w_in"))
    grads, delta, new_m, new_v = {}, {}, {}, {}
    for n in _BIG:
        parts = [u for u, (o, _) in sorted(zip(upd, owners), key=lambda t: t[1][1]) if o == n]
        if n in flipped:
            grads[n], delta[n], new_m[n], new_v[n] = (jnp.transpose(parts[0][j])[None] for j in range(4))
        else:
            grads[n], delta[n], new_m[n], new_v[n] = (jnp.stack([p[j] for p in parts]) for j in range(4))

    gs, loss_sum = _small_sum(small_gath, loss_g, ln0_gath, "small_sum")
    loss = loss_sum[0, 0]
    for n, _ in _SMALL:
        shape = weights[n].shape
        if n == "mla_q_lora_norm":
            grads[n] = lax.dynamic_slice(_unpack_small(gs, n, (Q_LORA,)), (me * 64,), (64,)).reshape(shape)
        elif n == "mla_kv_lora_norm":
            grads[n] = lax.dynamic_slice(_unpack_small(gs, n, (KV_LORA,)), (me * 32,), (32,)).reshape(shape)
        else:
            grads[n] = _unpack_small(gs, n, shape)

    def own(n, a):
        if a.ndim == 4:
            a = jnp.transpose(a, (0, 2, 3, 1))
        elif a.ndim == 3:
            a = jnp.transpose(a, (0, 2, 1))
        return a.reshape(a.shape[1:]) if a.ndim >= 3 else a

    def back(n, a):
        shape = weights[n].shape
        if len(shape) == 4:
            return jnp.transpose(a.reshape((1,) + a.shape), (0, 3, 1, 2))
        if len(shape) == 3:
            return jnp.transpose(a.reshape((1,) + a.shape), (0, 2, 1))
        return a.reshape(shape)

    wide = ("s5_b_re", "s5_b_im", "s5_c_re", "s5_c_im")
    for names, nb, call in (([n for n, _ in _SMALL if n not in wide], 1, "update_small"), (wide, 4, "update_s5_bc")):
        res = _adamw_multi([own(n, weights[n]) for n in names], [own(n, grads[n]) for n in names],
                           [own(n, m_in[n]) for n in names], [own(n, v_in[n]) for n in names], call, nb)
        for n, (dl, m2, v2) in zip(names, res):
            delta[n], new_m[n], new_v[n] = back(n, dl), back(n, m2), back(n, v2)
    return (loss, dx0[None], *[grads[n] for n in _WEIGHTS], *[delta[n] for n in _WEIGHTS],
            *[new_m[n] for n in _WEIGHTS], *[new_v[n] for n in _WEIGHTS])
```

```python
import functools
import math

import numpy as np
import jax
import jax.numpy as jnp
from jax import lax
from jax.experimental import pallas as pl
from jax.experimental.pallas import tpu as pltpu

F32 = jnp.float32
BF16 = jnp.bfloat16
EPS = 1e-6
NEG = float(np.finfo(np.float32).min)
MESH = pl.DeviceIdType.MESH

N_DEV = 8
D_MODEL = 1024
MEM_LEN = 256
XQ = 512
PRIM = 1536
BRANCH = 2048
X_HEADS = 4
HD = 128
S5_G = 96
S5_P = 64
S5_C = 16
S5_GB = 8
S5_W = S5_GB * S5_P
MLA_H = 12
ROPE = 64
Q_LORA = 512
KV_LORA = 256
ROPE_THETA = 10000.0

ADAM_LR = 0.001
ADAM_B1 = 0.9
ADAM_B2 = 0.999
ADAM_EPS = 1e-08
ADAM_WD = 0.01
ADAM_STEP = 10

VMEM_LIMIT = 56 * 1024 * 1024


def _dot(a, b):
    return jnp.dot(a, b, preferred_element_type=F32)


def _dot_nt(a, b):
    return lax.dot_general(a, b, (((1,), (1,)), ((), ())), preferred_element_type=F32)


def _dot_tn(a, b):
    return lax.dot_general(a, b, (((0,), (0,)), ((), ())), preferred_element_type=F32)


@jax.custom_vjp
def _mm(a, b):
    return _dot(a.astype(BF16), b.astype(BF16))


def _mm_fwd(a, b):
    return _mm(a, b), (a, b)


def _mm_bwd(res, g):
    a, b = res
    gb = g.astype(BF16)
    return _dot_nt(gb, b.astype(BF16)).astype(a.dtype), _dot_tn(a.astype(BF16), gb).astype(b.dtype)


_mm.defvjp(_mm_fwd, _mm_bwd)


@jax.custom_vjp
def _mm_nt(a, b):
    return _dot_nt(a.astype(BF16), b.astype(BF16))


def _mm_nt_fwd(a, b):
    return _mm_nt(a, b), (a, b)


def _mm_nt_bwd(res, g):
    a, b = res
    gb = g.astype(BF16)
    return _dot(gb, b.astype(BF16)).astype(a.dtype), _dot_tn(gb, a.astype(BF16)).astype(b.dtype)


_mm_nt.defvjp(_mm_nt_fwd, _mm_nt_bwd)


@jax.custom_vjp
def _softmax(s):
    m = jnp.max(s, axis=-1, keepdims=True)
    e = jnp.exp(s - m)
    return e / jnp.sum(e, axis=-1, keepdims=True)


def _softmax_fwd(s):
    p = _softmax(s)
    return p, p


def _softmax_bwd(p, g):
    return (p * (g - jnp.sum(p * g, axis=-1, keepdims=True)),)


_softmax.defvjp(_softmax_fwd, _softmax_bwd)


def _rms(x, g, n):
    ms = jnp.sum(x * x, axis=-1, keepdims=True) * (1.0 / n)
    return x * lax.rsqrt(ms + EPS) * g


def _sigmoid(x):
    return 1.0 / (1.0 + jnp.exp(-x))


def _silu(x):
    return x * _sigmoid(x)


def _gelu(x):
    c = math.sqrt(2.0 / math.pi)
    return 0.5 * x * (1.0 + jnp.tanh(c * (x + 0.044715 * (x * x * x))))


@jax.custom_vjp
def _rot(x, c, s1, s2):
    return x * c + pltpu.roll(x, 96, 1) * s1 + pltpu.roll(x, 32, 1) * s2


def _rot_fwd(x, c, s1, s2):
    return _rot(x, c, s1, s2), (c, s1, s2)


def _rot_bwd(res, g):
    c, s1, s2 = res
    dx = g * c + pltpu.roll(g * s1, 32, 1) + pltpu.roll(g * s2, 96, 1)
    return dx, jnp.zeros_like(c), jnp.zeros_like(s1), jnp.zeros_like(s2)


_rot.defvjp(_rot_fwd, _rot_bwd)


def _mem_attn(xq, k, v, gq):
    outs = []
    for h in range(X_HEADS):
        sl = slice(HD * h, HD * (h + 1))
        q = _rms(xq[:, sl], gq, HD)
        p = _softmax(_mm_nt(q, k[:, sl]) * (HD ** -0.5))
        outs.append(_mm(p, v[:, sl]))
    return jnp.concatenate(outs, axis=-1)


def _merge(mix, xq, gate, k, v, gq):
    return jnp.concatenate([mix, _mem_attn(xq, k, v, gq)], axis=-1) * _silu(gate)


def _q_chunks(q):
    return ([q[:, HD * h:HD * (h + 1)] for h in range(MLA_H)],
            [q[:, PRIM + HD * h:PRIM + HD * (h + 1)] for h in range(MLA_H)])


def _q_post(nope, rope, gqn, gqr, c, s1, s2):
    pieces = []
    for qn, qr in zip(nope, rope):
        pieces.append(_rms(qn, gqn, HD))
        pieces.append(_rot(_rms(qr, gqr, ROPE), c, s1, s2))
    return jnp.concatenate(pieces, axis=-1)


def _kv_chunks(kv):
    return ([kv[:, 2 * HD * h:2 * HD * h + HD] for h in range(MLA_H)],
            [kv[:, 2 * HD * h + HD:2 * HD * (h + 1)] for h in range(MLA_H)])


def _kv_post(kn, vals, krp, gkn, gkr, c, s1, s2):
    kr = _rot(_rms(krp, gkr, ROPE), c, s1, s2)
    pieces = []
    for k in kn:
        pieces.append(_rms(k, gkn, HD))
        pieces.append(kr)
    return jnp.concatenate(pieces, axis=-1), jnp.concatenate(vals, axis=-1)


def _rowwise(name, fn, ins, outs, nblk, host=None):
    n_in = len(ins)

    def spec(kind, shape):
        if kind == 'r':
            return pl.BlockSpec((shape[0] // nblk, shape[1]), lambda i: (i, 0))
        if kind == 't':
            return pl.BlockSpec((shape[0], shape[1] // nblk), lambda i: (0, i))
        zeros = (0,) * len(shape)
        return pl.BlockSpec(tuple(shape), lambda i: zeros)

    def body(*refs):
        i = pl.program_id(0)
        res = fn(*[r[...] for r in refs[:n_in]])
        for (kind, _, _), ref, val in zip(outs, refs[n_in:], res):
            if kind == 'a':
                @pl.when(i == 0)
                def _():
                    ref[...] = jnp.zeros_like(ref)
                ref[...] += val.astype(ref.dtype)
            elif kind == 't':
                ref[...] = val.astype(F32).T.astype(ref.dtype)
            else:
                ref[...] = val.astype(ref.dtype)

    res, hosted = _hosting_call(
        body, name, nblk, host, [a for _, a in ins], [spec(k, a.shape) for k, a in ins],
        [jax.ShapeDtypeStruct(tuple(s), d) for _, s, d in outs], [spec(k, s) for k, s, _ in outs], [])
    return res if host is None else (res, hosted)


def _matmul_tn(at, g, name, out_dtype=BF16):
    K, L = at.shape
    N = g.shape[1]
    tn = next(t for t in (512, 384, 256, 128) if N % t == 0)

    def body(a_ref, g_ref, o_ref):
        o_ref[...] = _dot(a_ref[...], g_ref[...]).astype(o_ref.dtype)

    return pl.pallas_call(
        body, name=name, grid=(N // tn,),
        in_specs=[pl.BlockSpec((K, L), lambda n: (0, 0)), pl.BlockSpec((L, tn), lambda n: (0, n))],
        out_specs=pl.BlockSpec((K, tn), lambda n: (0, n)),
        out_shape=jax.ShapeDtypeStruct((K, N), out_dtype),
        compiler_params=pltpu.CompilerParams(dimension_semantics=("arbitrary",), vmem_limit_bytes=VMEM_LIMIT),
    )(at, g)


def _matmul_tn_slots(at, g, name, host=None):
    K, L = at.shape
    n = g.shape[1] // N_DEV

    def body(a_ref, g_ref, o_ref):
        o_ref[...] = _dot(a_ref[...], g_ref[...]).astype(o_ref.dtype)

    res, hosted = _hosting_call(
        body, name, N_DEV, host, [at, g],
        [pl.BlockSpec((K, L), lambda d: (0, 0)), pl.BlockSpec((L, n), lambda d: (0, d))],
        [jax.ShapeDtypeStruct((N_DEV, K, n), BF16)], [pl.BlockSpec((None, K, n), lambda d: (d, 0, 0))], [])
    return res[0] if host is None else (res[0], hosted)


def _mm_slots(a16, w):
    return jnp.concatenate([_dot(a16, w[d]) for d in range(N_DEV)], axis=-1)


def _mm_slots_nt(g16, w):
    n = w.shape[2]
    out = _dot_nt(g16[:, 0:n], w[0])
    for d in range(1, N_DEV):
        out = out + _dot_nt(g16[:, d * n:(d + 1) * n], w[d])
    return out


class _Exchange:
    def __init__(self, ins, outs, scratch, start, finish):
        self.ins, self.outs, self.scratch, self.start, self.finish = ins, outs, scratch, start, finish


def _xyc():
    return lax.axis_index("x"), lax.axis_index("y"), lax.axis_index("c")


def _plan_all_gather(xs):
    n = len(xs)

    def build(x_refs, out_refs, sems):
        send_sems, recv_sems, local_sems = sems
        x, y, c = _xyc()

        def copies(k, block, to, own=False):
            slot = 4 * block[0] + 2 * block[1] + block[2]
            return [pltpu.make_async_remote_copy(
                src_ref=x_refs[a] if own else out_refs[a].at[slot], dst_ref=out_refs[a].at[slot],
                send_sem=send_sems.at[k * n + a], recv_sem=recv_sems.at[k * n + a], device_id=to,
                device_id_type=MESH) for a in range(n)]

        mine = [pltpu.make_async_copy(x_refs[a], out_refs[a].at[4 * x + 2 * y + c], local_sems.at[a])
                for a in range(n)]
        return copies, mine, (x, y, c), [(1 - x, y), (x, 1 - y), (1 - x, 1 - y)]

    def first_copies(copies, me, chips):
        x, y, c = me
        first = copies(0, me, (x, y, 1 - c), own=True)
        for j, chip in enumerate(chips):
            first += copies(1 + j, me, (*chip, c), own=True)
        return first

    def start(x_refs, out_refs, sems):
        copies, mine, me, chips = build(x_refs, out_refs, sems)
        for cp in mine + first_copies(copies, me, chips):
            cp.start()

    def finish(x_refs, out_refs, sems):
        copies, mine, me, chips = build(x_refs, out_refs, sems)
        x, y, c = me
        passed = []
        for j, chip in enumerate(chips):
            for cp in copies(1 + j, (*chip, c), me):
                cp.wait_recv()
            fwd = copies(4 + j, (*chip, c), (x, y, 1 - c))
            for cp in fwd:
                cp.start()
            passed += fwd
        for cp in copies(0, (x, y, 1 - c), me):
            cp.wait_recv()
        for j, chip in enumerate(chips):
            for cp in copies(4 + j, (*chip, 1 - c), me):
                cp.wait_recv()
        for cp in first_copies(copies, me, chips) + passed:
            cp.wait_send()
        for cp in mine:
            cp.wait()

    return _Exchange(list(xs), [jax.ShapeDtypeStruct((N_DEV,) + a.shape, a.dtype) for a in xs],
                     [pltpu.SemaphoreType.DMA((7 * n,)), pltpu.SemaphoreType.DMA((7 * n,)),
                      pltpu.SemaphoreType.DMA((n,))], start, finish)


_CHIPS = ((0, 0), (0, 1), (1, 0), (1, 1))


def _plan_pair(sends):
    n = len(sends)

    def build(s_refs, o_refs, sems):
        send_sems, recv_sems = sems
        x, y, c = _xyc()
        return [pltpu.make_async_remote_copy(
            src_ref=s_refs[a].at[4 * px + 2 * py + 1 - c], dst_ref=o_refs[a].at[j],
            send_sem=send_sems.at[j * n + a], recv_sem=recv_sems.at[j * n + a], device_id=(x, y, 1 - c),
            device_id_type=MESH) for j, (px, py) in enumerate(_CHIPS) for a in range(n)]

    def start(s_refs, o_refs, sems):
        for cp in build(s_refs, o_refs, sems):
            cp.start()

    def finish(s_refs, o_refs, sems):
        for cp in build(s_refs, o_refs, sems):
            cp.wait_recv()
            cp.wait_send()

    return _Exchange(list(sends), [jax.ShapeDtypeStruct((4,) + a.shape[1:], a.dtype) for a in sends],
                     [pltpu.SemaphoreType.DMA((4 * n,)), pltpu.SemaphoreType.DMA((4 * n,))], start, finish)


def _plan_chips(ts):
    n = len(ts)
    flips = ((1, 0), (0, 1), (1, 1))

    def build(t_refs, o_refs, sems):
        send_sems, recv_sems, local_sems = sems
        x, y, c = _xyc()
        mine = 2 * x + y
        local = [pltpu.make_async_copy(t_refs[a].at[mine], o_refs[a].at[mine], local_sems.at[a]) for a in range(n)]
        remote = []
        for k, (fx, fy) in enumerate(flips):
            px = 1 - x if fx else x
            py = 1 - y if fy else y
            remote += [pltpu.make_async_remote_copy(
                src_ref=t_refs[a].at[2 * px + py], dst_ref=o_refs[a].at[mine],
                send_sem=send_sems.at[k * n + a], recv_sem=recv_sems.at[k * n + a], device_id=(px, py, c),
                device_id_type=MESH) for a in range(n)]
        return local, remote

    def start(t_refs, o_refs, sems):
        local, remote = build(t_refs, o_refs, sems)
        for cp in local + remote:
            cp.start()

    def finish(t_refs, o_refs, sems):
        local, remote = build(t_refs, o_refs, sems)
        for cp in remote:
            cp.wait_recv()
        for cp in remote:
            cp.wait_send()
        for cp in local:
            cp.wait()

    return _Exchange(list(ts), [jax.ShapeDtypeStruct(a.shape, a.dtype) for a in ts],
                     [pltpu.SemaphoreType.DMA((3 * n,)), pltpu.SemaphoreType.DMA((3 * n,)),
                      pltpu.SemaphoreType.DMA((n,))], start, finish)


def _combine(*plans):
    def parts(refs, attr):
        out, at = [], 0
        for p in plans:
            n = len(getattr(p, attr))
            out.append(refs[at:at + n])
            at += n
        return out

    def run(half):
        def go(ins, outs, sems):
            for p, a, o, s in zip(plans, parts(ins, "ins"), parts(outs, "outs"), parts(sems, "scratch")):
                getattr(p, half)(a, o, s)
        return go

    return _Exchange(sum((p.ins for p in plans), []), sum((p.outs for p in plans), []),
                     sum((p.scratch for p in plans), []), run("start"), run("finish"))


def _exchange_call(plan, name):
    n = len(plan.ins)

    def body(*refs):
        ins, outs, sems = refs[:n], refs[n:2 * n], refs[2 * n:]
        plan.start(ins, outs, sems)
        plan.finish(ins, outs, sems)

    return pl.pallas_call(
        body, name=name, out_shape=plan.outs,
        in_specs=[pl.BlockSpec(memory_space=pl.ANY)] * n, out_specs=[pl.BlockSpec(memory_space=pl.ANY)] * n,
        scratch_shapes=plan.scratch,
    )(*plan.ins)


def _slab_spec(lead, rows, cols, nb):
    if rows % (nb * 16) == 0:
        return pl.BlockSpec((lead, rows // nb, cols), lambda i: (0, i, 0))
    if cols % (nb * 128) == 0:
        return pl.BlockSpec((lead, rows, cols // nb), lambda i: (0, 0, i))
    return pl.BlockSpec((lead, rows, cols), lambda i: (0, 0, 0))


def _slab_spec2(rows, cols, nb):
    if rows % (nb * 16) == 0:
        return pl.BlockSpec((rows // nb, cols), lambda i: (i, 0))
    if cols % (nb * 128) == 0:
        return pl.BlockSpec((rows, cols // nb), lambda i: (0, i))
    return pl.BlockSpec((rows, cols), lambda i: (0, 0))


def _cast_call(arrays, name, host=None):
    n = len(arrays)
    nb = 8

    def body(*refs):
        for a in range(n):
            refs[n + a][...] = refs[a][...].astype(BF16)

    specs = [_slab_spec2(x.shape[0], x.shape[1], nb) for x in arrays]
    return _hosting_call(body, name, nb, host, list(arrays), specs,
                         [jax.ShapeDtypeStruct(x.shape, BF16) for x in arrays], specs, [])


def _pair_add(sends, fromsib, name):
    n = len(sends)
    nb = 8

    def body(*refs):
        c = lax.axis_index("c")
        for a in range(n):
            s_ref, f_ref, t_ref = refs[a], refs[n + a], refs[2 * n + a]
            for j in range(4):
                t_ref[j] = (s_ref[2 * j + c].astype(F32) + f_ref[j].astype(F32)).astype(t_ref.dtype)

    def spec(a, lead):
        return _slab_spec(lead, a.shape[1], a.shape[2], nb)

    return pl.pallas_call(
        body, name=name, grid=(nb,),
        in_specs=[spec(a, N_DEV) for a in sends] + [spec(a, 4) for a in fromsib],
        out_specs=[spec(a, 4) for a in fromsib],
        out_shape=[jax.ShapeDtypeStruct(a.shape, a.dtype) for a in fromsib],
        compiler_params=pltpu.CompilerParams(dimension_semantics=("arbitrary",), vmem_limit_bytes=VMEM_LIMIT),
    )(*sends, *fromsib)


def _adamw_vals(w, g, m, v):
    m2 = ADAM_B1 * m + (1.0 - ADAM_B1) * g
    v2 = ADAM_B2 * v + (1.0 - ADAM_B2) * (g * g)
    m_hat = m2 / (1.0 - ADAM_B1 ** ADAM_STEP)
    v_hat = v2 / (1.0 - ADAM_B2 ** ADAM_STEP)
    delta = -ADAM_LR * (m_hat / (jnp.sqrt(v_hat) + ADAM_EPS) + ADAM_WD * w)
    return delta, m2, v2


def _sum_adamw(recv, w, m, v, name):
    R, C = w.shape
    ns = recv.shape[0]
    br = next((t for t in (256, 128, 64, 32, 16) if R % t == 0), R)

    def body(r_ref, w_ref, m_ref, v_ref, g_ref, d_ref, m2_ref, v2_ref):
        g = r_ref[0].astype(F32)
        for d in range(1, ns):
            g = g + r_ref[d].astype(F32)
        dl, m2, v2 = _adamw_vals(w_ref[...], g, m_ref[...], v_ref[...])
        g_ref[...] = g
        d_ref[...] = dl
        m2_ref[...] = m2
        v2_ref[...] = v2

    spec = pl.BlockSpec((br, C), lambda i: (i, 0))
    return pl.pallas_call(
        body, name=name, grid=(R // br,),
        in_specs=[pl.BlockSpec((ns, br, C), lambda i: (0, i, 0)), spec, spec, spec], out_specs=[spec] * 4,
        out_shape=[jax.ShapeDtypeStruct((R, C), F32)] * 4,
        compiler_params=pltpu.CompilerParams(dimension_semantics=("arbitrary",)),
    )(recv, w, m, v)


def _updates_call(recvs, ws, ms, vs, name, host=None):
    n = len(recvs)
    nb = 8

    def body(*refs):
        for a in range(n):
            r_ref, w_ref, m_ref, v_ref = refs[a], refs[n + a], refs[2 * n + a], refs[3 * n + a]
            g_ref, d_ref, m2_ref, v2_ref = refs[4 * n + 4 * a:4 * n + 4 * a + 4]
            g = r_ref[0].astype(F32)
            for d in range(1, r_ref.shape[0]):
                g = g + r_ref[d].astype(F32)
            dl, m2, v2 = _adamw_vals(w_ref[...], g, m_ref[...], v_ref[...])
            g_ref[...] = g
            d_ref[...] = dl
            m2_ref[...] = m2
            v2_ref[...] = v2

    def spec3(r):
        return _slab_spec(r.shape[0], r.shape[1], r.shape[2], nb)

    def spec2(w):
        return _slab_spec2(w.shape[0], w.shape[1], nb)

    res, hosted = _hosting_call(
        body, name, nb, host, list(recvs) + list(ws) + list(ms) + list(vs),
        [spec3(r) for r in recvs] + [spec2(w) for w in ws] * 3,
        [jax.ShapeDtypeStruct(w.shape, F32) for w in ws for _ in range(4)],
        [spec2(w) for w in ws for _ in range(4)], [])
    return [res[4 * a:4 * a + 4] for a in range(n)], hosted


def _small_sum(gath, loss_g, row0_g, name):
    _, R, C = gath.shape
    br = R // 3

    def body(g_ref, l_ref, r_ref, go_ref, lo_ref):
        g = g_ref[0].astype(F32)
        lsum = l_ref[0]
        for d in range(1, N_DEV):
            g = g + g_ref[d].astype(F32)
            lsum = lsum + l_ref[d]
        go_ref[...] = g
        lo_ref[...] = lsum

        @pl.when(pl.program_id(0) == 0)
        def _():
            row0 = r_ref[0]
            for d in range(1, N_DEV):
                row0 = row0 + r_ref[d]
            go_ref[0:8, :] = go_ref[0:8, :] + jnp.where(lax.broadcasted_iota(jnp.int32, row0.shape, 0) == 0, row0, 0.0)

    return pl.pallas_call(
        body, name=name, grid=(R // br,),
        in_specs=[pl.BlockSpec((N_DEV, br, C), lambda i: (0, i, 0)),
                  pl.BlockSpec((N_DEV, 8, HD), lambda i: (0, 0, 0)), pl.BlockSpec((N_DEV, 8, C), lambda i: (0, 0, 0))],
        out_specs=[pl.BlockSpec((br, C), lambda i: (i, 0)), pl.BlockSpec((8, HD), lambda i: (0, 0))],
        out_shape=[jax.ShapeDtypeStruct((R, C), F32), jax.ShapeDtypeStruct((8, HD), F32)],
        compiler_params=pltpu.CompilerParams(dimension_semantics=("arbitrary",)),
    )(gath, loss_g, row0_g)


def _adamw_multi(ws, gs, ms, vs, name, nblk=1):
    n = len(ws)

    def body(*refs):
        for a in range(n):
            dl, m2, v2 = _adamw_vals(refs[a][...], refs[n + a][...], refs[2 * n + a][...], refs[3 * n + a][...])
            refs[4 * n + 3 * a][...] = dl
            refs[4 * n + 3 * a + 1][...] = m2
            refs[4 * n + 3 * a + 2][...] = v2

    def spec(x):
        rest = (0,) * (x.ndim - 1)
        return pl.BlockSpec((x.shape[0] // nblk,) + tuple(x.shape[1:]), lambda i: (i,) + rest)

    res = pl.pallas_call(
        body, name=name, grid=(nblk,),
        in_specs=[spec(w) for w in ws] * 4, out_specs=[spec(w) for w in ws for _ in range(3)],
        out_shape=[jax.ShapeDtypeStruct(w.shape, F32) for w in ws for _ in range(3)],
        compiler_params=pltpu.CompilerParams(dimension_semantics=("arbitrary",), vmem_limit_bytes=VMEM_LIMIT),
    )(*ws, *gs, *ms, *vs)
    return [res[3 * a:3 * a + 3] for a in range(n)]


def _s5_param_fn(lr, li, ls, btr, bti):
    step = jnp.exp(ls)
    er = jnp.exp(lr * step)
    ang = li * step
    ar = er * jnp.cos(ang)
    ai = er * jnp.sin(ang)
    nr = ar - 1.0
    den = lr * lr + li * li
    fr = (nr * lr + ai * li) / den
    fi = (ai * lr - nr * li) / den
    return ar, ai, fr * btr - fi * bti, fr * bti + fi * btr


def _s5_params(lr, li, ls, btr, bti, cre, cim):
    nb = S5_G // S5_GB
    GC = S5_GB * S5_C
    expand = jnp.asarray(np.tile(np.eye(S5_P, dtype=np.float32), (1, S5_GB)), BF16)
    own = jnp.asarray((np.arange(GC)[:, None] // S5_C == np.arange(S5_W)[None, :] // S5_P).astype(np.float32))

    def body(lr_ref, li_ref, ls_ref, br_ref, bi_ref, cr_ref, ci_ref, e_ref, own_ref, ar_ref, ai_ref, bm_ref, cm_ref):
        ar, ai, bbr, bbi = _s5_param_fn(lr_ref[...], li_ref[...], ls_ref[...], br_ref[...], bi_ref[...])
        ar_ref[...] = ar
        ai_ref[...] = ai

        def plane(x, n):
            rows = x[n * S5_GB:(n + 1) * S5_GB].reshape(GC, S5_P).astype(BF16)
            return _dot(rows, e_ref[...]) * own_ref[...]

        for n in range(nb):
            bm_ref[n] = jnp.concatenate([plane(bbr, n), plane(bbi, n)], axis=-1).astype(BF16)
            cm_ref[n] = jnp.concatenate([plane(cr_ref[...], n), -plane(ci_ref[...], n)], axis=-1).astype(BF16)

    sd = jax.ShapeDtypeStruct
    return pl.pallas_call(
        body, name="s5_params",
        out_shape=[sd(lr.shape, F32), sd(lr.shape, F32), sd((nb, GC, 2 * S5_W), BF16), sd((nb, GC, 2 * S5_W), BF16)],
        compiler_params=pltpu.CompilerParams(vmem_limit_bytes=VMEM_LIMIT),
    )(lr, li, ls, btr, bti, cre, cim, expand, own)


def _s5_params_bwd(lr, li, ls, btr, bti, dar, dai, dbbr, dbbi):
    def body(lr_ref, li_ref, ls_ref, br_ref, bi_ref, dar_ref, dai_ref, dbbr_ref, dbbi_ref,
             dlr_ref, dli_ref, dls_ref, dbr_ref, dbi_ref):
        _, vjp = jax.vjp(_s5_param_fn, lr_ref[...], li_ref[...], ls_ref[...], br_ref[...], bi_ref[...])
        dlr, dli, dls, dbr, dbi = vjp((dar_ref[...], dai_ref[...], dbbr_ref[...], dbbi_ref[...]))
        dlr_ref[...] = dlr
        dli_ref[...] = dli
        dls_ref[...] = dls
        dbr_ref[...] = dbr
        dbi_ref[...] = dbi

    sd = jax.ShapeDtypeStruct
    return pl.pallas_call(
        body, name="s5_params_bwd",
        out_shape=[sd(lr.shape, F32), sd(lr.shape, F32), sd(ls.shape, F32), sd(btr.shape, F32), sd(btr.shape, F32)],
    )(lr, li, ls, btr, bti, dar, dai, dbbr, dbbi)


def _cpow(ar, ai, n):
    assert n & (n - 1) == 0
    while n > 1:
        ar, ai = ar * ar - ai * ai, 2.0 * ar * ai
        n //= 2
    return ar, ai


def _scan(st, cr, ci, init, nk, reverse, store, prev=None):
    W = S5_W

    def advance(k, sr, si):
        rows = pl.ds(k * 8 if isinstance(k, int) else pl.multiple_of(k * 8, 8), 8)
        nsr = cr * sr - ci * si + st[rows, 0:W]
        nsi = cr * si + ci * sr + st[rows, W:2 * W]
        if store:
            st[rows, 0:W] = nsr
            st[rows, W:2 * W] = nsi
        return nsr, nsi

    if prev is None:
        return lax.fori_loop(0, nk, lambda j, c: advance(nk - 1 - j if reverse else j, c[0], c[1]), init, unroll=2)
    assert reverse

    def step(j, carry):
        k = nk - 1 - j
        nsr, nsi = advance(k, carry[0], carry[1])
        prows = pl.ds(pl.multiple_of((k - 1) * 8, 8), 8)
        pr = prev[prows, 0:W]
        pi = prev[prows, W:2 * W]
        return nsr, nsi, carry[2] + nsr * pr + nsi * pi, carry[3] + nsi * pr - nsr * pi

    carry = lax.fori_loop(0, nk - 1, step, init, unroll=2)
    nsr, nsi = advance(0, carry[0], carry[1])
    return nsr, nsi, carry[2], carry[3]


def _chain(fin, fr, fi, pr, pi, reverse):
    W = S5_W
    fin[:, 0:W] = fr
    fin[:, W:2 * W] = fi
    rowid = lax.broadcasted_iota(jnp.int32, (8, W), 0)
    cr = jnp.zeros((1, W), F32)
    ci = jnp.zeros((1, W), F32)
    init_r = jnp.zeros((8, W), F32)
    init_i = jnp.zeros((8, W), F32)
    for s in (range(7, -1, -1) if reverse else range(8)):
        init_r = jnp.where(rowid == s, cr, init_r)
        init_i = jnp.where(rowid == s, ci, init_i)
        lr = fin[s:s + 1, 0:W]
        li = fin[s:s + 1, W:2 * W]
        cr, ci = lr + pr * cr - pi * ci, li + pr * ci + pi * cr
    return init_r, init_i


def _full_scan(st, fin, ar, ai, nk, reverse, prev=None, carry_in=None, carry_out=None):
    W = S5_W
    cr = jnp.broadcast_to(ar, (8, W))
    ci = jnp.broadcast_to(-ai if reverse else ai, (8, W))
    z = jnp.zeros((8, W), F32)
    if carry_in is None:
        fr, fi = _scan(st, cr, ci, (z, z), nk, reverse, store=False)
        pr, pi = _cpow(ar, -ai if reverse else ai, nk)
        init = _chain(fin, fr, fi, pr, pi, reverse)
    else:
        init = (carry_in[:, 0:W], carry_in[:, W:2 * W])
    if carry_out is not None:
        carry_out[:, 0:W] = init[0]
        carry_out[:, W:2 * W] = init[1]
    if prev is None:
        return _scan(st, cr, ci, init, nk, reverse, store=True)
    return _scan(st, cr, ci, init + (z, z), nk, reverse, store=True, prev=prev)


def _s5_specs(L):
    W2 = 2 * S5_W
    GC = S5_GB * S5_C
    col = pl.BlockSpec((L, GC), lambda g: (0, g))
    vec = pl.BlockSpec((1, GC), lambda g: (0, g))
    avec = pl.BlockSpec((1, S5_W), lambda g: (0, g))
    bmat = pl.BlockSpec((None, GC, W2), lambda g: (g, 0, 0))
    cmat = pl.BlockSpec((None, W2, GC), lambda g: (g, 0, 0))
    return col, vec, avec, bmat, cmat


def _interleave(dst, src, nk):
    for s in range(8):
        dst[pl.ds(s, nk, stride=8), :] = src[s * nk:(s + 1) * nk, :]


def _deinterleave(dst, src, nk):
    for s in range(8):
        dst[s * nk:(s + 1) * nk, :] = src[pl.ds(s, nk, stride=8), :].astype(dst.dtype)


def _hosting_call(body, name, nsteps, host, ins, in_specs, outs, out_specs, scratch):
    grid = (nsteps,) if isinstance(nsteps, int) else tuple(nsteps)
    params = pltpu.CompilerParams(dimension_semantics=("arbitrary",) * len(grid), vmem_limit_bytes=VMEM_LIMIT)
    if host is None:
        res = pl.pallas_call(
            body, name=name, grid=grid, in_specs=in_specs, out_specs=out_specs, out_shape=outs,
            scratch_shapes=scratch, compiler_params=params,
        )(*ins)
        return list(res), []
    n_in, n_out, n_sc = len(ins), len(outs), len(scratch)
    h_in, h_out = len(host.ins), len(host.outs)

    def hosted(*refs):
        a = refs[:n_in]
        ha = refs[n_in:n_in + h_in]
        o = refs[n_in + h_in:n_in + h_in + n_out]
        ho = refs[n_in + h_in + n_out:n_in + h_in + n_out + h_out]
        sc = refs[n_in + h_in + n_out + h_out:n_in + h_in + n_out + h_out + n_sc]
        hs = refs[n_in + h_in + n_out + h_out + n_sc:]
        first = functools.reduce(jnp.logical_and, [pl.program_id(i) == 0 for i in range(len(grid))])
        last = functools.reduce(jnp.logical_and, [pl.program_id(i) == g - 1 for i, g in enumerate(grid)])

        @pl.when(first)
        def _():
            host.start(ha, ho, hs)

        body(*a, *o, *sc)

        @pl.when(last)
        def _():
            host.finish(ha, ho, hs)

    hbm = pl.BlockSpec(memory_space=pl.ANY)
    res = pl.pallas_call(
        hosted, name=name, grid=grid,
        in_specs=list(in_specs) + [hbm] * h_in, out_specs=list(out_specs) + [hbm] * h_out,
        out_shape=list(outs) + list(host.outs), scratch_shapes=list(scratch) + list(host.scratch),
        compiler_params=params,
    )(*ins, *host.ins)
    return list(res[:n_out]), list(res[n_out:])


def _s5_fwd(u, bm, cm, ar, ai, dvec, host=None):
    L = u.shape[0]
    nk = L // 8
    GC = S5_GB * S5_C
    nb = S5_G // S5_GB
    col, vec, avec, bmat, cmat = _s5_specs(L)

    def body(u_ref, b_ref, c_ref, ar_ref, ai_ref, d_ref, y_ref, carry_ref, st, fin, ui, yi):
        _interleave(ui, u_ref, nk)
        for r in range(8):
            rows = slice(r * nk, (r + 1) * nk)
            st[rows, :] = _dot(ui[rows, :].astype(BF16), b_ref[...])
        _full_scan(st, fin, ar_ref[...], ai_ref[...], nk, reverse=False, carry_out=carry_ref)
        for r in range(8):
            rows = slice(r * nk, (r + 1) * nk)
            yi[rows, :] = _dot_nt(st[rows, :].astype(BF16), c_ref[...]) + d_ref[...] * ui[rows, :]
        _deinterleave(y_ref, yi, nk)

    return _hosting_call(
        body, "s5_fwd", nb, host,
        [u, bm, cm, ar, ai, dvec], [col, bmat, bmat, avec, avec, vec],
        [jax.ShapeDtypeStruct(u.shape, F32), jax.ShapeDtypeStruct((nb * 8, 2 * S5_W), F32)],
        [col, pl.BlockSpec((8, 2 * S5_W), lambda g: (g, 0))],
        [pltpu.VMEM((L, 2 * S5_W), F32), pltpu.VMEM((8, 2 * S5_W), F32), pltpu.VMEM((L, GC), F32),
         pltpu.VMEM((L, GC), F32)])


def _s5_bwd(u, dy, carry, bm, cm, ar, ai, dvec, mask, rmat, host=None):
    L = u.shape[0]
    nk = L // 8
    W = S5_W
    GC = S5_GB * S5_C
    col, vec, avec, bmat, cmat = _s5_specs(L)
    hi = lax.Precision.HIGHEST

    def body(u_ref, dy_ref, carry_ref, b_ref, ct_ref, ar_ref, ai_ref, d_ref, mask_ref, r_ref,
             du_ref, db_ref, dc_ref, dd_ref, dar_ref, dai_ref, sa, sb, fin, ui, dyi, dui):
        ar = ar_ref[...]
        ai = ai_ref[...]
        _interleave(ui, u_ref, nk)
        _interleave(dyi, dy_ref, nk)
        for r in range(8):
            rows = slice(r * nk, (r + 1) * nk)
            sa[rows, :] = _dot(ui[rows, :].astype(BF16), b_ref[...])
            sb[rows, :] = _dot(dyi[rows, :].astype(BF16), ct_ref[...])
        _full_scan(sa, fin, ar, ai, nk, reverse=False, carry_in=carry_ref)
        gr, gi, accr, acci = _full_scan(sb, fin, ar, ai, nk, reverse=True, prev=sa)
        rowid = lax.broadcasted_iota(jnp.int32, (8, W), 0)
        last = pl.ds((nk - 1) * 8, 8)
        pr = jnp.where(rowid == 0, 0.0, pltpu.roll(sa[last, 0:W], 1, 0))
        pi = jnp.where(rowid == 0, 0.0, pltpu.roll(sa[last, W:2 * W], 1, 0))
        accr = accr + gr * pr + gi * pi
        acci = acci + gi * pr - gr * pi
        dar_ref[...] = jnp.sum(accr, axis=0, keepdims=True)
        dai_ref[...] = jnp.sum(acci, axis=0, keepdims=True)
        dbf = jnp.zeros((GC, 2 * W), F32)
        dcf = jnp.zeros((GC, 2 * W), F32)
        dd = jnp.zeros((1, GC), F32)
        for r in range(8):
            rows = slice(r * nk, (r + 1) * nk)
            ub = ui[rows, :]
            dyb = dyi[rows, :]
            gb = sb[rows, :].astype(BF16)
            dui[rows, :] = _dot_nt(gb, b_ref[...]) + d_ref[...] * dyb
            dbf = dbf + _dot_tn(ub.astype(BF16), gb)
            dcf = dcf + _dot_tn(dyb.astype(BF16), sa[rows, :].astype(BF16))
            dd = dd + jnp.sum(dyb * ub, axis=0, keepdims=True)
        db_ref[...] = jnp.dot(dbf * mask_ref[...], r_ref[...], precision=hi, preferred_element_type=F32)
        dc_ref[...] = jnp.dot(dcf * mask_ref[...], r_ref[...], precision=hi, preferred_element_type=F32)
        dd_ref[...] = dd
        _deinterleave(du_ref, dui, nk)

    cmp_spec = pl.BlockSpec((GC, 2 * S5_P), lambda g: (g, 0))
    whole = lambda shape: pl.BlockSpec(shape, lambda g: (0, 0))
    sd = jax.ShapeDtypeStruct
    return _hosting_call(
        body, "s5_bwd", S5_G // S5_GB, host,
        [u, dy, carry, bm, cm, ar, ai, dvec, mask, rmat],
        [col, col, pl.BlockSpec((8, 2 * W), lambda g: (g, 0)), bmat, bmat, avec, avec, vec, whole(mask.shape),
         whole(rmat.shape)],
        [sd(u.shape, BF16), sd((S5_G * S5_C, 2 * S5_P), F32), sd((S5_G * S5_C, 2 * S5_P), F32),
         sd((1, PRIM), F32), sd((1, S5_G * S5_P), F32), sd((1, S5_G * S5_P), F32)],
        [col, cmp_spec, cmp_spec, vec, avec, avec],
        [pltpu.VMEM((L, 2 * W), F32), pltpu.VMEM((L, 2 * W), F32), pltpu.VMEM((8, 2 * W), F32),
         pltpu.VMEM((L, GC), F32), pltpu.VMEM((L, GC), F32), pltpu.VMEM((L, GC), F32)])


def _s5_compact_consts():
    g_row = np.arange(S5_GB * S5_C) // S5_C
    col = np.arange(2 * S5_W)
    g_col = (col % S5_W) // S5_P
    mask = (g_row[:, None] == g_col[None, :]).astype(np.float32)
    tgt = (col // S5_W) * S5_P + col % S5_P
    rmat = (tgt[:, None] == np.arange(2 * S5_P)[None, :]).astype(np.float32)
    return jnp.asarray(mask), jnp.asarray(rmat)


def _attn_scores(q_ref, k_ref, qb, bq, scale):
    ext = (qb + 1) * bq
    s = _dot_nt(q_ref[qb * bq:ext, :], k_ref[0:ext, :]) * scale
    qpos = lax.broadcasted_iota(jnp.int32, (bq, bq), 0)
    kpos = lax.broadcasted_iota(jnp.int32, (bq, bq), 1)
    diag = jnp.where(kpos <= qpos, s[:, ext - bq:], NEG)
    return diag if qb == 0 else jnp.concatenate([s[:, :ext - bq], diag], axis=-1)


def _attn_fwd(qp, kp, v, scale):
    L = qp.shape[0]
    bq = min(256, L)

    def body(q_ref, k_ref, v_ref, o_ref, lse_ref):
        for qb in range(L // bq):
            rows = slice(qb * bq, (qb + 1) * bq)
            s = _attn_scores(q_ref, k_ref, qb, bq, scale)
            m = jnp.max(s, axis=-1, keepdims=True)
            e = jnp.exp(s - m)
            l = jnp.sum(e, axis=-1, keepdims=True)
            o_ref[rows, :] = _dot(e.astype(BF16), v_ref[0:(qb + 1) * bq, :]) / l
            lse_ref[rows, :] = jnp.broadcast_to(m + jnp.log(l), (bq, HD))

    blk = pl.BlockSpec((L, HD), lambda h: (0, h))
    wide = pl.BlockSpec((L, 2 * HD), lambda h: (0, h))
    return pl.pallas_call(
        body, name="mla_attn_fwd", grid=(MLA_H,),
        in_specs=[wide, wide, blk], out_specs=[blk, blk],
        out_shape=[jax.ShapeDtypeStruct((L, MLA_H * HD), F32)] * 2,
        compiler_params=pltpu.CompilerParams(dimension_semantics=("arbitrary",), vmem_limit_bytes=VMEM_LIMIT),
    )(qp, kp, v)


def _attn_bwd(qp, kp, v, o, lse, do, scale):
    L = qp.shape[0]
    bq = min(256, L)
    nq = L // bq

    def body(q_ref, k_ref, v_ref, o_ref, lse_ref, do_ref, dq_ref, dk_ref, dv_ref, dk_acc, dv_acc):
        dk_acc[...] = jnp.zeros_like(dk_acc)
        dv_acc[...] = jnp.zeros_like(dv_acc)
        for qb in range(nq):
            rows = slice(qb * bq, (qb + 1) * bq)
            ext = (qb + 1) * bq
            do = do_ref[rows, :]
            dob = do.astype(BF16)
            p = jnp.exp(_attn_scores(q_ref, k_ref, qb, bq, scale) - lse_ref[rows, 0:1])
            dp = _dot_nt(dob, v_ref[0:ext, :])
            dsum = jnp.sum(do * o_ref[rows, :], axis=-1, keepdims=True)
            ds = (p * (dp - dsum) * scale).astype(BF16)
            dq_ref[rows, :] = _dot(ds, k_ref[0:ext, :]).astype(dq_ref.dtype)
            dk_acc[0:ext, :] += _dot_tn(ds, q_ref[rows, :])
            dv_acc[0:ext, :] += _dot_tn(p.astype(BF16), dob)
        dk_ref[...] = dk_acc[...].astype(dk_ref.dtype)
        dv_ref[...] = dv_acc[...].astype(dv_ref.dtype)

    sd = jax.ShapeDtypeStruct
    blk = pl.BlockSpec((L, HD), lambda h: (0, h))
    wide = pl.BlockSpec((L, 2 * HD), lambda h: (0, h))
    return pl.pallas_call(
        body, name="mla_attn_bwd", grid=(MLA_H,),
        in_specs=[wide, wide, blk, blk, blk, blk], out_specs=[wide, wide, blk],
        out_shape=[sd((L, MLA_H * 2 * HD), BF16), sd((L, MLA_H * 2 * HD), BF16), sd((L, MLA_H * HD), BF16)],
        scratch_shapes=[pltpu.VMEM((L, 2 * HD), F32), pltpu.VMEM((L, HD), F32)],
        compiler_params=pltpu.CompilerParams(dimension_semantics=("arbitrary",), vmem_limit_bytes=VMEM_LIMIT),
    )(qp, kp, v, o, lse, do)


def _kv_fn(mem, gm, w, gk):
    kv = _mm(_rms(mem, gm, D_MODEL), w)
    k = jnp.concatenate([_rms(kv[:, HD * h:HD * (h + 1)], gk, HD) for h in range(X_HEADS)], axis=-1)
    return k, kv[:, XQ:]


def _kv_prep(mem, gm, w, gk, name):
    def fn(mem, gm, w, gk):
        return _kv_fn(mem, gm, w, gk)
    M = mem.shape[0]
    return _rowwise(name, fn, [('c', mem), ('c', gm), ('c', w), ('c', gk)],
                    [('c', (M, XQ), F32), ('c', (M, XQ), F32)], 1)


def _kv_prep_bwd(mem, gm, w, gk, dk, dv, name):
    def fn(mem, gm, w, gk, dk, dv):
        _, vjp = jax.vjp(lambda a, b, c: _kv_fn(mem, a, b, c), gm, w, gk)
        return vjp((dk, dv))
    return _rowwise(name, fn, [('c', mem), ('c', gm), ('c', w), ('c', gk), ('c', dk), ('c', dv)],
                    [('c', gm.shape, F32), ('c', w.shape, BF16), ('c', gk.shape, F32)], 1)


def _forward_merge(x, mix, mix_kind, xq, gate, k, v, gq, wout, name, nblk, host=None):
    def fn(x, mix, xq, gate, k, v, gq, wout):
        o = _merge(mix, xq, gate, k, v, gq)
        return (x + _dot(o.astype(BF16), wout),)
    L = x.shape[0]
    out = _rowwise(name, fn, [('r', x), (mix_kind, mix), ('r', xq), ('r', gate), ('c', k), ('c', v), ('c', gq),
                              ('c', wout)], [('r', (L, D_MODEL), F32)], nblk, host=host)
    return out[0] if host is None else (out[0][0], out[1])


def _backward_merge(dx, mix, mix_kind, xq, gate, k, v, gq, wout, name, nblk, host=None):
    def fn(dx, mix, xq, gate, k, v, gq, wout):
        g16 = dx.astype(BF16)
        do = _dot_nt(g16, wout)
        o, vjp = jax.vjp(_merge, mix, xq, gate, k, v, gq)
        dmix, dxq, dgate, dk, dv, dgq = vjp(do)
        return dmix, dxq, dgate, o, g16, dk, dv, dgq
    L = dx.shape[0]
    return _rowwise(
        name, fn,
        [('r', dx), (mix_kind, mix), ('r', xq), ('r', gate), ('c', k), ('c', v), ('c', gq), ('c', wout)],
        [('r', (L, PRIM), F32), ('r', (L, XQ), BF16), ('r', (L, BRANCH), BF16), ('t', (BRANCH, L), BF16),
         ('r', (L, D_MODEL), BF16), ('a', k.shape, F32), ('a', v.shape, F32), ('a', gq.shape, F32)], nblk,
        host=host)


_MLA_IN = 3392
_MLA_IN_PAD = 3456


def _uq_rows(wt):
    r = wt.reshape(MLA_H, HD + ROPE, wt.shape[1])
    return jnp.concatenate([r[:, :HD].reshape(PRIM, -1),
                            jnp.pad(r[:, HD:], ((0, 0), (0, HD - ROPE), (0, 0))).reshape(PRIM, -1)], axis=0)


def _uq_rows_back(wt):
    nope = wt[:PRIM].reshape(MLA_H, HD, -1)
    rope = wt[PRIM:].reshape(MLA_H, HD, -1)[:, :ROPE]
    return jnp.concatenate([nope, rope], axis=1).reshape(MLA_H * (HD + ROPE), -1)


def _mla_in_rows_back(wt):
    return jnp.concatenate([wt[:768], wt[3328:3392], wt[768:3328]], axis=0)


_SMALL = (("ln_gain", 2048), ("mem_norm", 2048), ("xq_norm", 256), ("xk_norm", 256), ("s5_lambda_re", 6144),
          ("s5_lambda_im", 6144), ("s5_log_step", 96), ("s5_b_re", 98304), ("s5_b_im", 98304), ("s5_c_re", 98304),
          ("s5_c_im", 98304), ("s5_d", 1536), ("mla_q_lora_norm", 512), ("mla_kv_lora_norm", 256),
          ("mla_q_nope_norm", 128), ("mla_k_nope_norm", 128), ("mla_q_rope_norm", 64), ("mla_k_rope_norm", 64))
_SMALL_ROWS = 432
_SMALL_OFF = {name: sum(n for _, n in _SMALL[:i]) for i, (name, _) in enumerate(_SMALL)}


def _pack_small(d):
    flat = jnp.concatenate([d[n].reshape(-1).astype(F32) for n, _ in _SMALL])
    return jnp.pad(flat, (0, _SMALL_ROWS * 1024 - flat.shape[0])).reshape(_SMALL_ROWS, 1024)


def _unpack_small(p, name, shape):
    off = _SMALL_OFF[name]
    return p.reshape(-1)[off:off + int(np.prod(shape))].reshape(shape)


_WEIGHTS = ('ln_gain', 'w_out', 'mem_norm', 'w_mem_kv', 'xq_norm', 'xk_norm', 's5_w_in', 's5_lambda_re',
            's5_lambda_im', 's5_log_step', 's5_b_re', 's5_b_im', 's5_c_re', 's5_c_im', 's5_d', 's5_w_glu', 'mla_w_in',
            'mla_q_lora_norm', 'mla_kv_lora_norm', 'mla_w_uq', 'mla_w_ukv', 'mla_q_nope_norm', 'mla_k_nope_norm',
            'mla_q_rope_norm', 'mla_k_rope_norm')
_BIG = ('w_out', 'w_mem_kv', 's5_w_in', 's5_w_glu', 'mla_w_in', 'mla_w_uq', 'mla_w_ukv')


def _pad128(g):
    return jnp.pad(g.reshape(1, -1), ((0, 0), (0, HD - g.shape[-1])))


def kernel(x, mem, positions, ln_gain, w_out, mem_norm, w_mem_kv, xq_norm, xk_norm, s5_w_in, s5_lambda_re, s5_lambda_im, s5_log_step, s5_b_re, s5_b_im, s5_c_re, s5_c_im, s5_d, s5_w_glu, mla_w_in, mla_q_lora_norm, mla_kv_lora_norm, mla_w_uq, mla_w_ukv, mla_q_nope_norm, mla_k_nope_norm, mla_q_rope_norm, mla_k_rope_norm, loss_target, m_ln_gain, m_w_out, m_mem_norm, m_w_mem_kv, m_xq_norm, m_xk_norm, m_s5_w_in, m_s5_lambda_re, m_s5_lambda_im, m_s5_log_step, m_s5_b_re, m_s5_b_im, m_s5_c_re, m_s5_c_im, m_s5_d, m_s5_w_glu, m_mla_w_in, m_mla_q_lora_norm, m_mla_kv_lora_norm, m_mla_w_uq, m_mla_w_ukv, m_mla_q_nope_norm, m_mla_k_nope_norm, m_mla_q_rope_norm, m_mla_k_rope_norm, v_ln_gain, v_w_out, v_mem_norm, v_w_mem_kv, v_xq_norm, v_xk_norm, v_s5_w_in, v_s5_lambda_re, v_s5_lambda_im, v_s5_log_step, v_s5_b_re, v_s5_b_im, v_s5_c_re, v_s5_c_im, v_s5_d, v_s5_w_glu, v_mla_w_in, v_mla_q_lora_norm, v_mla_kv_lora_norm, v_mla_w_uq, v_mla_w_ukv, v_mla_q_nope_norm, v_mla_k_nope_norm, v_mla_q_rope_norm, v_mla_k_rope_norm):
    weights = dict(ln_gain=ln_gain, w_out=w_out, mem_norm=mem_norm, w_mem_kv=w_mem_kv, xq_norm=xq_norm,
                   xk_norm=xk_norm, s5_w_in=s5_w_in, s5_lambda_re=s5_lambda_re, s5_lambda_im=s5_lambda_im,
                   s5_log_step=s5_log_step, s5_b_re=s5_b_re, s5_b_im=s5_b_im, s5_c_re=s5_c_re, s5_c_im=s5_c_im,
                   s5_d=s5_d, s5_w_glu=s5_w_glu, mla_w_in=mla_w_in, mla_q_lora_norm=mla_q_lora_norm,
                   mla_kv_lora_norm=mla_kv_lora_norm, mla_w_uq=mla_w_uq, mla_w_ukv=mla_w_ukv,
                   mla_q_nope_norm=mla_q_nope_norm, mla_k_nope_norm=mla_k_nope_norm,
                   mla_q_rope_norm=mla_q_rope_norm, mla_k_rope_norm=mla_k_rope_norm)
    m_in = dict(zip(_WEIGHTS, (m_ln_gain, m_w_out, m_mem_norm, m_w_mem_kv, m_xq_norm, m_xk_norm, m_s5_w_in,
                               m_s5_lambda_re, m_s5_lambda_im, m_s5_log_step, m_s5_b_re, m_s5_b_im, m_s5_c_re,
                               m_s5_c_im, m_s5_d, m_s5_w_glu, m_mla_w_in, m_mla_q_lora_norm, m_mla_kv_lora_norm,
                               m_mla_w_uq, m_mla_w_ukv, m_mla_q_nope_norm, m_mla_k_nope_norm, m_mla_q_rope_norm,
                               m_mla_k_rope_norm)))
    v_in = dict(zip(_WEIGHTS, (v_ln_gain, v_w_out, v_mem_norm, v_w_mem_kv, v_xq_norm, v_xk_norm, v_s5_w_in,
                               v_s5_lambda_re, v_s5_lambda_im, v_s5_log_step, v_s5_b_re, v_s5_b_im, v_s5_c_re,
                               v_s5_c_im, v_s5_d, v_s5_w_glu, v_mla_w_in, v_mla_q_lora_norm, v_mla_kv_lora_norm,
                               v_mla_w_uq, v_mla_w_ukv, v_mla_q_nope_norm, v_mla_k_nope_norm, v_mla_q_rope_norm,
                               v_mla_k_rope_norm)))

    x0 = x[0]
    mem0 = mem[0]
    target = loss_target[0]
    L = x0.shape[0]
    nblk = 4
    nb_big = 8
    me = 4 * lax.axis_index("x") + 2 * lax.axis_index("y") + lax.axis_index("c")

    lora = jnp.pad(jnp.concatenate([mla_q_lora_norm, mla_kv_lora_norm], axis=1), ((0, 7), (0, HD - 96)))
    def gather(*shards):
        return _plan_all_gather(list(shards))

    kh = D_MODEL // 2
    (b_mkv0, b_glu, b_in_mla, b_out0, b_uq, b_ukv, b_mkv1, b_out1), (W_in_s5,) = _cast_call(
        [w_mem_kv[0], s5_w_glu[0], jnp.transpose(mla_w_in[0]), w_out[0], jnp.transpose(mla_w_uq[0]), mla_w_ukv[0],
         w_mem_kv[1], w_out[1]], "cast_shards", host=gather(s5_w_in[0].astype(BF16)))

    ln0, ln1 = ln_gain[0:1], ln_gain[1:2]
    gq0, gq1 = xq_norm[0:1], xq_norm[1:2]
    gk0, gk1 = xk_norm[0:1], xk_norm[1:2]
    gm0, gm1 = mem_norm[0:1], mem_norm[1:2]
    gqn, gkn = mla_q_nope_norm, mla_k_nope_norm
    gqr, gkr = _pad128(mla_q_rope_norm), _pad128(mla_k_rope_norm)

    lr3 = s5_lambda_re.reshape(S5_G, 1, S5_P)
    li3 = s5_lambda_im.reshape(S5_G, 1, S5_P)
    ls3 = s5_log_step.reshape(S5_G, 1, 1)
    btr = jnp.swapaxes(s5_b_re[0], 1, 2)
    bti = jnp.swapaxes(s5_b_im[0], 1, 2)
    a_r, a_i, bm, cm = _s5_params(lr3, li3, ls3, btr, bti, s5_c_re[0], s5_c_im[0])
    a_r2 = a_r.reshape(1, S5_G * S5_P)
    a_i2 = a_i.reshape(1, S5_G * S5_P)
    cmask, rmat = _s5_compact_consts()

    half = ROPE // 2
    inv_freq = ROPE_THETA ** (-jnp.arange(half, dtype=F32) / half)
    invf = jnp.concatenate([inv_freq, inv_freq, jnp.zeros((HD - ROPE,), F32)]).reshape(1, HD)

    def rot_tables(pos, invf):
        ang = pos.astype(F32) * invf
        lane = lax.broadcasted_iota(jnp.int32, ang.shape, 1)
        c = jnp.where(lane < ROPE, jnp.cos(ang), 0.0)
        s = jnp.sin(ang)
        return c, jnp.where(lane < half, -s, 0.0), jnp.where((lane >= half) & (lane < ROPE), s, 0.0)

    tc, ts1, ts2 = _rowwise("rot_tables", rot_tables, [('r', positions.reshape(L, 1)), ('c', invf)],
                            [('r', (L, HD), F32)] * 3, nblk)

    def in_s5(x, g, w):
        proj = _mm_slots(_rms(x, g, D_MODEL).astype(BF16), w)
        return proj[:, :PRIM], proj[:, PRIM:PRIM + XQ], proj[:, PRIM + XQ:]

    u_s5, xq_a, gate_a = _rowwise(
        "s5_in", in_s5, [('r', x0), ('c', ln0), ('c', W_in_s5)],
        [('r', (L, PRIM), F32), ('r', (L, XQ), F32), ('r', (L, BRANCH), F32)], nblk)
    (y_s5, s5_carry), (W_glu, G_mkv0, G_in_mla_a) = _s5_fwd(u_s5, bm, cm, a_r2, a_i2, s5_d,
                                                            host=gather(b_glu, b_mkv0, b_in_mla[:, :kh]))

    def glu(y, w):
        z = _mm_slots(_gelu(y).astype(BF16), w)
        return z[:, :PRIM] * _sigmoid(z[:, PRIM:]), z

    (y2, z_glu), (G_out0,) = _rowwise("s5_glu", glu, [('r', y_s5), ('c', W_glu)],
                                      [('r', (L, PRIM), F32), ('r', (L, 2 * PRIM), F32)], nblk, host=gather(b_out0))
    W_mkv0 = G_mkv0.reshape(D_MODEL, 2 * XQ)
    k_a, v_a = _kv_prep(mem0, gm0, W_mkv0, gk0, "kv_prep0")
    x1, (G_in_mla_b,) = _forward_merge(
        x0, y2, 'r', xq_a, gate_a, k_a, v_a, gq0, G_out0.reshape(BRANCH, D_MODEL), "merge0", nblk,
        host=gather(b_in_mla[:, kh:]))
    W_in_mla = jnp.concatenate([G_in_mla_a, G_in_mla_b], axis=2).reshape(_MLA_IN, D_MODEL)

    def in_mla(x, g, w):
        xn = _rms(x, g, D_MODEL).astype(BF16)
        a = _dot_nt(xn, w[0:768])
        kx = _dot_nt(xn, w[768:896])
        b = _dot_nt(xn, w[832:_MLA_IN])
        lane = lax.broadcasted_iota(jnp.int32, kx.shape, 1)
        return a[:, :512], a[:, 512:], b[:, :XQ], b[:, XQ:], jnp.where(lane < ROPE, kx, 0.0)

    (c_q, c_kv, xq_b, gate_b, krp), (G_uq, W_kv, G_lora) = _rowwise(
        "mla_in", in_mla, [('r', x1), ('c', ln1), ('c', W_in_mla)],
        [('r', (L, Q_LORA), F32), ('r', (L, KV_LORA), F32), ('r', (L, XQ), F32), ('r', (L, BRANCH), F32),
         ('r', (L, HD), F32)], nblk,
        host=gather(b_uq, b_ukv, lora))
    W_q = _uq_rows(G_uq.reshape(MLA_H * (HD + ROPE), Q_LORA))
    g_qlora = G_lora[:, 0, :64].reshape(1, Q_LORA)
    g_kvlora = G_lora[:, 0, 64:96].reshape(1, KV_LORA)

    def qkv(c_q, c_kv, krp, tc, ts1, ts2, gql, gkvl, wq, wkv, gqn, gkn, gqr, gkr):
        q = _dot_nt(_rms(c_q, gql, Q_LORA).astype(BF16), wq)
        kv = _mm_slots(_rms(c_kv, gkvl, KV_LORA).astype(BF16), wkv)
        kp, v = _kv_post(*_kv_chunks(kv), krp, gkn, gkr, tc, ts1, ts2)
        return _q_post(*_q_chunks(q), gqn, gqr, tc, ts1, ts2), kp, v

    qkv_consts = [('c', g_qlora), ('c', g_kvlora), ('c', W_q), ('c', W_kv), ('c', gqn), ('c', gkn), ('c', gqr),
                  ('c', gkr)]
    (q_pad, k_pad, v_h), (G_mkv1, G_out1) = _rowwise(
        "mla_qkv", qkv, [('r', c_q), ('r', c_kv), ('r', krp), ('r', tc), ('r', ts1), ('r', ts2)] + qkv_consts,
        [('r', (L, 2 * PRIM), BF16), ('r', (L, 2 * PRIM), BF16), ('r', (L, PRIM), BF16)], nblk,
        host=gather(b_mkv1, b_out1))
    W_out = (G_out0.reshape(BRANCH, D_MODEL), G_out1.reshape(BRANCH, D_MODEL))
    W_mkv = (W_mkv0, G_mkv1.reshape(D_MODEL, 2 * XQ))
    scale = (HD + ROPE) ** -0.5
    attn, lse = _attn_fwd(q_pad, k_pad, v_h, scale)
    k_b, v_b = _kv_prep(mem0, gm1, W_mkv[1], gk1, "kv_prep1")

    def merge_loss(x, mix, xq, gate, k, v, gq, wout, t):
        err = x + _dot(_merge(mix, xq, gate, k, v, gq).astype(BF16), wout) - t
        part = 0.5 * jnp.sum(jnp.sum(err * err, axis=-1, keepdims=True) * (1.0 / D_MODEL), axis=0, keepdims=True)
        return err * (1.0 / D_MODEL), jnp.broadcast_to(part, (1, HD))

    dx2, loss_part = _rowwise(
        "merge1_loss", merge_loss,
        [('r', x1), ('r', attn), ('r', xq_b), ('r', gate_b), ('c', k_b), ('c', v_b), ('c', gq1), ('c', W_out[1]),
         ('r', target)], [('r', (L, D_MODEL), F32), ('a', (1, HD), F32)], nblk)

    dattn, dxq_b, dgate_b, o_b, g_b, dk_b, dv_b, dgq1 = _backward_merge(
        dx2, attn, 'r', xq_b, gate_b, k_b, v_b, gq1, W_out[1], "merge1_bwd", nb_big)
    dgm1, dW_mkv1, dgk1 = _kv_prep_bwd(mem0, gm1, W_mkv[1], gk1, dk_b, dv_b, "kv_prep1_bwd")
    dW_out1 = _matmul_tn(o_b, g_b, "dw_out1")
    dq_pad, dk_pad, dv_h = _attn_bwd(q_pad, k_pad, v_h, attn, lse, dattn, scale)

    def qkv_bwd(c_q, c_kv, krp, tc, ts1, ts2, dqp, dkp, dv, gql, gkvl, wq, wkv, gqn, gkn, gqr, gkr):
        cqn, vjp_qn = jax.vjp(lambda a, b: _rms(a, b, Q_LORA), c_q, gql)
        ckvn, vjp_kvn = jax.vjp(lambda a, b: _rms(a, b, KV_LORA), c_kv, gkvl)
        cqn16 = cqn.astype(BF16)
        ckvn16 = ckvn.astype(BF16)
        q = _dot_nt(cqn16, wq)
        kv = _mm_slots(ckvn16, wkv)
        _, vjp_q = jax.vjp(lambda n, r, a, b: _q_post(n, r, a, b, tc, ts1, ts2), *_q_chunks(q), gqn, gqr)
        dnope, drope, dgqn, dgqr = vjp_q(dqp.astype(F32))
        dq = jnp.concatenate(dnope + drope, axis=-1)
        _, vjp_kv = jax.vjp(lambda n, v, k, a, b: _kv_post(n, v, k, a, b, tc, ts1, ts2), *_kv_chunks(kv), krp, gkn,
                            gkr)
        dkn, dvals, dkrp, dgkn, dgkr = vjp_kv((dkp.astype(F32), dv.astype(F32)))
        dkv = jnp.concatenate([x for pair in zip(dkn, dvals) for x in pair], axis=-1)
        dq16 = dq.astype(BF16)
        dkv16 = dkv.astype(BF16)
        dc_q, dgql = vjp_qn(_dot(dq16, wq))
        dc_kv, dgkvl = vjp_kvn(_mm_slots_nt(dkv16, wkv))
        return dc_q, dc_kv, dkrp, cqn16, dq16, ckvn16, dkv16, dgql, dgkvl, dgqn, dgkn, dgqr, dgkr

    (dc_q, dc_kv, dkrp, cqn16, dq16, ckvn16, dkv16, dgql, dgkvl, dgqn, dgkn, dgqr, dgkr) = _rowwise(
        "mla_qkv_bwd", qkv_bwd,
        [('r', c_q), ('r', c_kv), ('r', krp), ('r', tc), ('r', ts1), ('r', ts2), ('r', dq_pad), ('r', dk_pad),
         ('r', dv_h)] + qkv_consts,
        [('r', (L, Q_LORA), BF16), ('r', (L, KV_LORA), BF16), ('r', (L, HD), BF16), ('r', (L, Q_LORA), BF16),
         ('t', (2 * PRIM, L), BF16), ('t', (KV_LORA, L), BF16), ('r', (L, 2 * PRIM), BF16),
         ('a', (1, Q_LORA), F32), ('a', (1, KV_LORA), F32), ('a', (1, HD), F32), ('a', (1, HD), F32),
         ('a', (1, HD), F32), ('a', (1, HD), F32)], nb_big)
    dW_q = _matmul_tn(dq16, cqn16, "dw_uq")
    dW_kv = _matmul_tn_slots(ckvn16, dkv16, "dw_ukv")

    def in_bwd(x, dres, g, w, *dparts):
        dproj = jnp.concatenate(dparts, axis=-1).astype(BF16)
        xn, vjp = jax.vjp(lambda a, b: _rms(a, b, D_MODEL), x, g)
        if w.ndim == 3:
            dxn = _mm_slots_nt(dproj, w)
        else:
            dkr = dproj[:, 3328:]
            dkr = jnp.where(lax.broadcasted_iota(jnp.int32, dkr.shape, 1) < ROPE, dkr, jnp.zeros_like(dkr))
            dxn = _dot(dproj[:, :768], w[0:768]) + _dot(dproj[:, 768:3328], w[832:_MLA_IN]) + _dot(dkr, w[768:896])
        dx, dg = vjp(dxn)
        return dx + dres, xn, dproj, dg

    dx1, xn1, dproj1, dln1 = _rowwise(
        "mla_in_bwd", in_bwd,
        [('r', x1), ('r', dx2), ('c', ln1), ('c', W_in_mla), ('r', dc_q), ('r', dc_kv), ('r', dxq_b), ('r', dgate_b),
         ('r', dkrp)],
        [('r', (L, D_MODEL), F32), ('r', (L, D_MODEL), BF16), ('t', (_MLA_IN_PAD, L), BF16), ('a', (1, D_MODEL), F32)],
        nblk)
    dW_in_mla = _matmul_tn(dproj1, xn1, "dw_mla_in")

    grads1 = [dW_out1.reshape(N_DEV, 256, D_MODEL), dW_mkv1.reshape(N_DEV, 128, 2 * XQ),
              _mla_in_rows_back(dW_in_mla).reshape(N_DEV, 424, D_MODEL),
              _uq_rows_back(dW_q).reshape(N_DEV, 288, Q_LORA), dW_kv]
    (dy2, dxq_a, dgate_a, o_a, g_a, dk_a, dv_a, dgq0), pair1 = _backward_merge(
        dx1, y2, 'r', xq_a, gate_a, k_a, v_a, gq0, W_out[0], "merge0_bwd", nb_big, host=_plan_pair(grads1))
    dgm0, dW_mkv0, dgk0 = _kv_prep_bwd(mem0, gm0, W_mkv[0], gk0, dk_a, dv_a, "kv_prep0_bwd")
    dW_out0 = _matmul_tn(o_a, g_a, "dw_out0")
    t1 = list(_pair_add(grads1, pair1, "rs_add_layer1"))

    def glu_bwd(y, z, dy2, w):
        h, vjp_h = jax.vjp(_gelu, y)
        _, vjp_z = jax.vjp(lambda a, b: a * _sigmoid(b), z[:, :PRIM], z[:, PRIM:])
        dz16 = jnp.concatenate(vjp_z(dy2), axis=-1).astype(BF16)
        return vjp_h(_mm_slots_nt(dz16, w))[0], h.astype(BF16), dz16

    grads0 = [dW_out0.reshape(N_DEV, 256, D_MODEL), dW_mkv0.reshape(N_DEV, 128, 2 * XQ)]
    (dy_s5, h16, dz16), glu_hosted = _rowwise(
        "s5_glu_bwd", glu_bwd, [('r', y_s5), ('r', z_glu), ('r', dy2), ('c', W_glu)],
        [('r', (L, PRIM), F32), ('t', (PRIM, L), BF16), ('r', (L, 2 * PRIM), BF16)], nb_big,
        host=_combine(_plan_chips(t1[2:3]), _plan_pair(grads0)))
    recv_in_mla, pair0 = glu_hosted[:1], glu_hosted[1:]
    dW_glu = _matmul_tn_slots(h16, dz16, "dw_glu")
    t0 = list(_pair_add(grads0 + [dW_glu], pair0 + list(_exchange_call(_plan_pair([dW_glu]), "rs_pair_glu")),
                        "rs_add_layer0"))
    (du_s5, dbc, dcc, dd, dar, dai), recv_rest = _s5_bwd(u_s5, dy_s5, s5_carry, bm, cm, a_r2, a_i2, s5_d,
                                                        cmask, rmat, host=_plan_chips(t1[:2] + t1[3:] + t0))
    early_recv = recv_rest[:2] + recv_in_mla + recv_rest[2:]
    dbc4 = dbc.reshape(S5_G, S5_C, 2, S5_P)
    dcc4 = dcc.reshape(S5_G, S5_C, 2, S5_P)
    dlr, dli, dls, dbtr, dbti = _s5_params_bwd(
        lr3, li3, ls3, btr, bti, dar.reshape(S5_G, 1, S5_P), dai.reshape(S5_G, 1, S5_P), dbc4[:, :, 0], dbc4[:, :, 1])

    small_part = {
        "ln_gain": jnp.concatenate([jnp.zeros_like(dln1), dln1]), "mem_norm": jnp.concatenate([dgm0, dgm1]),
        "xq_norm": jnp.concatenate([dgq0, dgq1]), "xk_norm": jnp.concatenate([dgk0, dgk1]),
        "s5_lambda_re": dlr, "s5_lambda_im": dli, "s5_log_step": dls,
        "s5_b_re": jnp.swapaxes(dbtr, 1, 2), "s5_b_im": jnp.swapaxes(dbti, 1, 2),
        "s5_c_re": dcc4[:, :, 0], "s5_c_im": -dcc4[:, :, 1], "s5_d": dd,
        "mla_q_lora_norm": dgql, "mla_kv_lora_norm": dgkvl, "mla_q_nope_norm": dgqn, "mla_k_nope_norm": dgkn,
        "mla_q_rope_norm": dgqr[:, :ROPE], "mla_k_rope_norm": dgkr[:, :ROPE],
    }
    loss8 = jnp.pad(loss_part, ((0, 7), (0, 0)))
    (dx0, xn0, dproj0, dln0), (small_gath, loss_g) = _rowwise(
        "s5_in_bwd", in_bwd,
        [('r', x0), ('r', dx1), ('c', ln0), ('c', W_in_s5), ('r', du_s5), ('r', dxq_a),
         ('r', dgate_a)],
        [('r', (L, D_MODEL), F32), ('t', (D_MODEL, L), BF16), ('r', (L, 2 * BRANCH), BF16), ('a', (1, D_MODEL), F32)],
        nblk, host=_plan_all_gather([_pack_small(small_part).astype(BF16), loss8]))
    dW_in_s5 = _matmul_tn_slots(xn0, dproj0, "dw_s5_in")

    late = [dW_in_s5]
    late_t = _pair_add(late, list(_exchange_call(_plan_pair(late), "rs_pair_late")), "rs_add_late")
    owners = [("w_out", 1), ("w_mem_kv", 1), ("mla_w_in", 0), ("mla_w_uq", 0), ("mla_w_ukv", 0), ("w_out", 0),
              ("w_mem_kv", 0), ("s5_w_glu", 0)]
    flipped = ("mla_w_in", "mla_w_uq")

    def shard(d, n, i):
        return jnp.transpose(d[n][i]) if n in flipped else d[n][i]

    upd, (late_recv, ln0_gath) = _updates_call(
        early_recv, [shard(weights, n, i) for n, i in owners], [shard(m_in, n, i) for n, i in owners],
        [shard(v_in, n, i) for n, i in owners], "update_early",
        host=_combine(_plan_chips(late_t), _plan_all_gather([jnp.pad(dln0, ((0, 7), (0, 0)))])))
    owners.append(("s5_w_in", 0))
    upd.append(_sum_adamw(late_recv, s5_w_in[0], m_s5_w_in[0], v_s5_w_in[0], "update_s5_w_in"))
    grads, delta, new_m, new_v = {}, {}, {}, {}
    for n in _BIG:
        parts = [u for u, (o, _) in sorted(zip(upd, owners), key=lambda t: t[1][1]) if o == n]
        if n in flipped:
            grads[n], delta[n], new_m[n], new_v[n] = (jnp.transpose(parts[0][j])[None] for j in range(4))
        else:
            grads[n], delta[n], new_m[n], new_v[n] = (jnp.stack([p[j] for p in parts]) for j in range(4))

    gs, loss_sum = _small_sum(small_gath, loss_g, ln0_gath, "small_sum")
    loss = loss_sum[0, 0]
    for n, _ in _SMALL:
        shape = weights[n].shape
        if n == "mla_q_lora_norm":
            grads[n] = lax.dynamic_slice(_unpack_small(gs, n, (Q_LORA,)), (me * 64,), (64,)).reshape(shape)
        elif n == "mla_kv_lora_norm":
            grads[n] = lax.dynamic_slice(_unpack_small(gs, n, (KV_LORA,)), (me * 32,), (32,)).reshape(shape)
        else:
            grads[n] = _unpack_small(gs, n, shape)

    def own(n, a):
        if a.ndim == 4:
            a = jnp.transpose(a, (0, 2, 3, 1))
        elif a.ndim == 3:
            a = jnp.transpose(a, (0, 2, 1))
        return a.reshape(a.shape[1:]) if a.ndim >= 3 else a

    def back(n, a):
        shape = weights[n].shape
        if len(shape) == 4:
            return jnp.transpose(a.reshape((1,) + a.shape), (0, 3, 1, 2))
        if len(shape) == 3:
            return jnp.transpose(a.reshape((1,) + a.shape), (0, 2, 1))
        return a.reshape(shape)

    wide = ("s5_b_re", "s5_b_im", "s5_c_re", "s5_c_im")
    for names, nb, call in (([n for n, _ in _SMALL if n not in wide], 1, "update_small"), (wide, 4, "update_s5_bc")):
        res = _adamw_multi([own(n, weights[n]) for n in names], [own(n, grads[n]) for n in names],
                           [own(n, m_in[n]) for n in names], [own(n, v_in[n]) for n in names], call, nb)
        for n, (dl, m2, v2) in zip(names, res):
            delta[n], new_m[n], new_v[n] = back(n, dl), back(n, m2), back(n, v2)
    return (loss, dx0[None], *[grads[n] for n in _WEIGHTS], *[delta[n] for n in _WEIGHTS],
            *[new_m[n] for n in _WEIGHTS], *[new_v[n] for n in _WEIGHTS])
```

```python
import functools
import math

import numpy as np
import jax
import jax.numpy as jnp
from jax import lax
from jax.experimental import pallas as pl
from jax.experimental.pallas import tpu as pltpu

F32 = jnp.float32
BF16 = jnp.bfloat16
EPS = 1e-6
NEG = float(np.finfo(np.float32).min)
MESH = pl.DeviceIdType.MESH

N_DEV = 8
D_MODEL = 1024
MEM_LEN = 256
XQ = 512
PRIM = 1536
BRANCH = 2048
X_HEADS = 4
HD = 128
S5_G = 96
S5_P = 64
S5_C = 16
S5_GB = 8
S5_W = S5_GB * S5_P
MLA_H = 12
ROPE = 64
Q_LORA = 512
KV_LORA = 256
ROPE_THETA = 10000.0

ADAM_LR = 0.001
ADAM_B1 = 0.9
ADAM_B2 = 0.999
ADAM_EPS = 1e-08
ADAM_WD = 0.01
ADAM_STEP = 10

VMEM_LIMIT = 56 * 1024 * 1024


def _dot(a, b):
    return jnp.dot(a, b, preferred_element_type=F32)


def _dot_nt(a, b):
    return lax.dot_general(a, b, (((1,), (1,)), ((), ())), preferred_element_type=F32)


def _dot_tn(a, b):
    return lax.dot_general(a, b, (((0,), (0,)), ((), ())), preferred_element_type=F32)


@jax.custom_vjp
def _mm(a, b):
    return _dot(a.astype(BF16), b.astype(BF16))


def _mm_fwd(a, b):
    return _mm(a, b), (a, b)


def _mm_bwd(res, g):
    a, b = res
    gb = g.astype(BF16)
    return _dot_nt(gb, b.astype(BF16)).astype(a.dtype), _dot_tn(a.astype(BF16), gb).astype(b.dtype)


_mm.defvjp(_mm_fwd, _mm_bwd)


@jax.custom_vjp
def _mm_nt(a, b):
    return _dot_nt(a.astype(BF16), b.astype(BF16))


def _mm_nt_fwd(a, b):
    return _mm_nt(a, b), (a, b)


def _mm_nt_bwd(res, g):
    a, b = res
    gb = g.astype(BF16)
    return _dot(gb, b.astype(BF16)).astype(a.dtype), _dot_tn(gb, a.astype(BF16)).astype(b.dtype)


_mm_nt.defvjp(_mm_nt_fwd, _mm_nt_bwd)


@jax.custom_vjp
def _softmax(s):
    m = jnp.max(s, axis=-1, keepdims=True)
    e = jnp.exp(s - m)
    return e / jnp.sum(e, axis=-1, keepdims=True)


def _softmax_fwd(s):
    p = _softmax(s)
    return p, p


def _softmax_bwd(p, g):
    return (p * (g - jnp.sum(p * g, axis=-1, keepdims=True)),)


_softmax.defvjp(_softmax_fwd, _softmax_bwd)


def _rms(x, g, n):
    ms = jnp.sum(x * x, axis=-1, keepdims=True) * (1.0 / n)
    return x * lax.rsqrt(ms + EPS) * g


def _sigmoid(x):
    return 1.0 / (1.0 + jnp.exp(-x))


def _silu(x):
    return x * _sigmoid(x)


def _gelu(x):
    c = math.sqrt(2.0 / math.pi)
    return 0.5 * x * (1.0 + jnp.tanh(c * (x + 0.044715 * (x * x * x))))


@jax.custom_vjp
def _rot(x, c, s1, s2):
    return x * c + pltpu.roll(x, 96, 1) * s1 + pltpu.roll(x, 32, 1) * s2


def _rot_fwd(x, c, s1, s2):
    return _rot(x, c, s1, s2), (c, s1, s2)


def _rot_bwd(res, g):
    c, s1, s2 = res
    dx = g * c + pltpu.roll(g * s1, 32, 1) + pltpu.roll(g * s2, 96, 1)
    return dx, jnp.zeros_like(c), jnp.zeros_like(s1), jnp.zeros_like(s2)


_rot.defvjp(_rot_fwd, _rot_bwd)


def _mem_attn(xq, k, v, gq):
    outs = []
    for h in range(X_HEADS):
        sl = slice(HD * h, HD * (h + 1))
        q = _rms(xq[:, sl], gq, HD)
        p = _softmax(_mm_nt(q, k[:, sl]) * (HD ** -0.5))
        outs.append(_mm(p, v[:, sl]))
    return jnp.concatenate(outs, axis=-1)


def _merge(mix, xq, gate, k, v, gq):
    return jnp.concatenate([mix, _mem_attn(xq, k, v, gq)], axis=-1) * _silu(gate)


def _q_chunks(q):
    return ([q[:, HD * h:HD * (h + 1)] for h in range(MLA_H)],
            [q[:, PRIM + HD * h:PRIM + HD * (h + 1)] for h in range(MLA_H)])


def _q_post(nope, rope, gqn, gqr, c, s1, s2):
    pieces = []
    for qn, qr in zip(nope, rope):
        pieces.append(_rms(qn, gqn, HD))
        pieces.append(_rot(_rms(qr, gqr, ROPE), c, s1, s2))
    return jnp.concatenate(pieces, axis=-1)


def _kv_chunks(kv):
    return ([kv[:, 2 * HD * h:2 * HD * h + HD] for h in range(MLA_H)],
            [kv[:, 2 * HD * h + HD:2 * HD * (h + 1)] for h in range(MLA_H)])


def _kv_post(kn, vals, krp, gkn, gkr, c, s1, s2):
    kr = _rot(_rms(krp, gkr, ROPE), c, s1, s2)
    pieces = []
    for k in kn:
        pieces.append(_rms(k, gkn, HD))
        pieces.append(kr)
    return jnp.concatenate(pieces, axis=-1), jnp.concatenate(vals, axis=-1)


def _rowwise(name, fn, ins, outs, nblk, host=None):
    n_in = len(ins)

    def spec(kind, shape):
        if kind == 'r':
            return pl.BlockSpec((shape[0] // nblk, shape[1]), lambda i: (i, 0))
        if kind == 't':
            return pl.BlockSpec((shape[0], shape[1] // nblk), lambda i: (0, i))
        zeros = (0,) * len(shape)
        return pl.BlockSpec(tuple(shape), lambda i: zeros)

    def body(*refs):
        i = pl.program_id(0)
        res = fn(*[r[...] for r in refs[:n_in]])
        for (kind, _, _), ref, val in zip(outs, refs[n_in:], res):
            if kind == 'a':
                @pl.when(i == 0)
                def _():
                    ref[...] = jnp.zeros_like(ref)
                ref[...] += val.astype(ref.dtype)
            elif kind == 't':
                ref[...] = val.astype(F32).T.astype(ref.dtype)
            else:
                ref[...] = val.astype(ref.dtype)

    res, hosted = _hosting_call(
        body, name, nblk, host, [a for _, a in ins], [spec(k, a.shape) for k, a in ins],
        [jax.ShapeDtypeStruct(tuple(s), d) for _, s, d in outs], [spec(k, s) for k, s, _ in outs], [])
    return res if host is None else (res, hosted)


def _matmul_tn(at, g, name, out_dtype=BF16):
    K, L = at.shape
    N = g.shape[1]
    tn = next(t for t in (512, 384, 256, 128) if N % t == 0)

    def body(a_ref, g_ref, o_ref):
        o_ref[...] = _dot(a_ref[...], g_ref[...]).astype(o_ref.dtype)

    return pl.pallas_call(
        body, name=name, grid=(N // tn,),
        in_specs=[pl.BlockSpec((K, L), lambda n: (0, 0)), pl.BlockSpec((L, tn), lambda n: (0, n))],
        out_specs=pl.BlockSpec((K, tn), lambda n: (0, n)),
        out_shape=jax.ShapeDtypeStruct((K, N), out_dtype),
        compiler_params=pltpu.CompilerParams(dimension_semantics=("arbitrary",), vmem_limit_bytes=VMEM_LIMIT),
    )(at, g)


def _matmul_tn_slots(at, g, name, host=None):
    K, L = at.shape
    n = g.shape[1] // N_DEV

    def body(a_ref, g_ref, o_ref):
        o_ref[...] = _dot(a_ref[...], g_ref[...]).astype(o_ref.dtype)

    res, hosted = _hosting_call(
        body, name, N_DEV, host, [at, g],
        [pl.BlockSpec((K, L), lambda d: (0, 0)), pl.BlockSpec((L, n), lambda d: (0, d))],
        [jax.ShapeDtypeStruct((N_DEV, K, n), BF16)], [pl.BlockSpec((None, K, n), lambda d: (d, 0, 0))], [])
    return res[0] if host is None else (res[0], hosted)


def _mm_slots(a16, w):
    return jnp.concatenate([_dot(a16, w[d]) for d in range(N_DEV)], axis=-1)


def _mm_slots_nt(g16, w):
    n = w.shape[2]
    out = _dot_nt(g16[:, 0:n], w[0])
    for d in range(1, N_DEV):
        out = out + _dot_nt(g16[:, d * n:(d + 1) * n], w[d])
    return out


class _Exchange:
    def __init__(self, ins, outs, scratch, start, finish):
        self.ins, self.outs, self.scratch, self.start, self.finish = ins, outs, scratch, start, finish


def _xyc():
    return lax.axis_index("x"), lax.axis_index("y"), lax.axis_index("c")


def _plan_all_gather(xs):
    n = len(xs)

    def build(x_refs, out_refs, sems):
        send_sems, recv_sems, local_sems = sems
        x, y, c = _xyc()

        def copies(k, block, to, own=False):
            slot = 4 * block[0] + 2 * block[1] + block[2]
            return [pltpu.make_async_remote_copy(
                src_ref=x_refs[a] if own else out_refs[a].at[slot], dst_ref=out_refs[a].at[slot],
                send_sem=send_sems.at[k * n + a], recv_sem=recv_sems.at[k * n + a], device_id=to,
                device_id_type=MESH) for a in range(n)]

        mine = [pltpu.make_async_copy(x_refs[a], out_refs[a].at[4 * x + 2 * y + c], local_sems.at[a])
                for a in range(n)]
        return copies, mine, (x, y, c), [(1 - x, y), (x, 1 - y), (1 - x, 1 - y)]

    def first_copies(copies, me, chips):
        x, y, c = me
        first = copies(0, me, (x, y, 1 - c), own=True)
        for j, chip in enumerate(chips):
            first += copies(1 + j, me, (*chip, c), own=True)
        return first

    def start(x_refs, out_refs, sems):
        copies, mine, me, chips = build(x_refs, out_refs, sems)
        for cp in mine + first_copies(copies, me, chips):
            cp.start()

    def finish(x_refs, out_refs, sems):
        copies, mine, me, chips = build(x_refs, out_refs, sems)
        x, y, c = me
        passed = []
        for j, chip in enumerate(chips):
            for cp in copies(1 + j, (*chip, c), me):
                cp.wait_recv()
            fwd = copies(4 + j, (*chip, c), (x, y, 1 - c))
            for cp in fwd:
                cp.start()
            passed += fwd
        for cp in copies(0, (x, y, 1 - c), me):
            cp.wait_recv()
        for j, chip in enumerate(chips):
            for cp in copies(4 + j, (*chip, 1 - c), me):
                cp.wait_recv()
        for cp in first_copies(copies, me, chips) + passed:
            cp.wait_send()
        for cp in mine:
            cp.wait()

    return _Exchange(list(xs), [jax.ShapeDtypeStruct((N_DEV,) + a.shape, a.dtype) for a in xs],
                     [pltpu.SemaphoreType.DMA((7 * n,)), pltpu.SemaphoreType.DMA((7 * n,)),
                      pltpu.SemaphoreType.DMA((n,))], start, finish)


_CHIPS = ((0, 0), (0, 1), (1, 0), (1, 1))


def _plan_pair(sends):
    n = len(sends)

    def build(s_refs, o_refs, sems):
        send_sems, recv_sems = sems
        x, y, c = _xyc()
        return [pltpu.make_async_remote_copy(
            src_ref=s_refs[a].at[4 * px + 2 * py + 1 - c], dst_ref=o_refs[a].at[j],
            send_sem=send_sems.at[j * n + a], recv_sem=recv_sems.at[j * n + a], device_id=(x, y, 1 - c),
            device_id_type=MESH) for j, (px, py) in enumerate(_CHIPS) for a in range(n)]

    def start(s_refs, o_refs, sems):
        for cp in build(s_refs, o_refs, sems):
            cp.start()

    def finish(s_refs, o_refs, sems):
        for cp in build(s_refs, o_refs, sems):
            cp.wait_recv()
            cp.wait_send()

    return _Exchange(list(sends), [jax.ShapeDtypeStruct((4,) + a.shape[1:], a.dtype) for a in sends],
                     [pltpu.SemaphoreType.DMA((4 * n,)), pltpu.SemaphoreType.DMA((4 * n,))], start, finish)


def _plan_chips(ts):
    n = len(ts)
    flips = ((1, 0), (0, 1), (1, 1))

    def build(t_refs, o_refs, sems):
        send_sems, recv_sems, local_sems = sems
        x, y, c = _xyc()
        mine = 2 * x + y
        local = [pltpu.make_async_copy(t_refs[a].at[mine], o_refs[a].at[mine], local_sems.at[a]) for a in range(n)]
        remote = []
        for k, (fx, fy) in enumerate(flips):
            px = 1 - x if fx else x
            py = 1 - y if fy else y
            remote += [pltpu.make_async_remote_copy(
                src_ref=t_refs[a].at[2 * px + py], dst_ref=o_refs[a].at[mine],
                send_sem=send_sems.at[k * n + a], recv_sem=recv_sems.at[k * n + a], device_id=(px, py, c),
                device_id_type=MESH) for a in range(n)]
        return local, remote

    def start(t_refs, o_refs, sems):
        local, remote = build(t_refs, o_refs, sems)
        for cp in local + remote:
            cp.start()

    def finish(t_refs, o_refs, sems):
        local, remote = build(t_refs, o_refs, sems)
        for cp in remote:
            cp.wait_recv()
        for cp in remote:
            cp.wait_send()
        for cp in local:
            cp.wait()

    return _Exchange(list(ts), [jax.ShapeDtypeStruct(a.shape, a.dtype) for a in ts],
                     [pltpu.SemaphoreType.DMA((3 * n,)), pltpu.SemaphoreType.DMA((3 * n,)),
                      pltpu.SemaphoreType.DMA((n,))], start, finish)


def _combine(*plans):
    def parts(refs, attr):
        out, at = [], 0
        for p in plans:
            n = len(getattr(p, attr))
            out.append(refs[at:at + n])
            at += n
        return out

    def run(half):
        def go(ins, outs, sems):
            for p, a, o, s in zip(plans, parts(ins, "ins"), parts(outs, "outs"), parts(sems, "scratch")):
                getattr(p, half)(a, o, s)
        return go

    return _Exchange(sum((p.ins for p in plans), []), sum((p.outs for p in plans), []),
                     sum((p.scratch for p in plans), []), run("start"), run("finish"))


def _exchange_call(plan, name):
    n = len(plan.ins)

    def body(*refs):
        ins, outs, sems = refs[:n], refs[n:2 * n], refs[2 * n:]
        plan.start(ins, outs, sems)
        plan.finish(ins, outs, sems)

    return pl.pallas_call(
        body, name=name, out_shape=plan.outs,
        in_specs=[pl.BlockSpec(memory_space=pl.ANY)] * n, out_specs=[pl.BlockSpec(memory_space=pl.ANY)] * n,
        scratch_shapes=plan.scratch,
    )(*plan.ins)


def _slab_spec(lead, rows, cols, nb):
    if rows % (nb * 16) == 0:
        return pl.BlockSpec((lead, rows // nb, cols), lambda i: (0, i, 0))
    if cols % (nb * 128) == 0:
        return pl.BlockSpec((lead, rows, cols // nb), lambda i: (0, 0, i))
    return pl.BlockSpec((lead, rows, cols), lambda i: (0, 0, 0))


def _slab_spec2(rows, cols, nb):
    if rows % (nb * 16) == 0:
        return pl.BlockSpec((rows // nb, cols), lambda i: (i, 0))
    if cols % (nb * 128) == 0:
        return pl.BlockSpec((rows, cols // nb), lambda i: (0, i))
    return pl.BlockSpec((rows, cols), lambda i: (0, 0))


def _cast_call(arrays, name, host=None):
    n = len(arrays)
    nb = 8

    def body(*refs):
        for a in range(n):
            refs[n + a][...] = refs[a][...].astype(BF16)

    specs = [_slab_spec2(x.shape[0], x.shape[1], nb) for x in arrays]
    return _hosting_call(body, name, nb, host, list(arrays), specs,
                         [jax.ShapeDtypeStruct(x.shape, BF16) for x in arrays], specs, [])


def _pair_add(sends, fromsib, name):
    n = len(sends)
    nb = 8

    def body(*refs):
        c = lax.axis_index("c")
        for a in range(n):
            s_ref, f_ref, t_ref = refs[a], refs[n + a], refs[2 * n + a]
            for j in range(4):
                t_ref[j] = (s_ref[2 * j + c].astype(F32) + f_ref[j].astype(F32)).astype(t_ref.dtype)

    def spec(a, lead):
        return _slab_spec(lead, a.shape[1], a.shape[2], nb)

    return pl.pallas_call(
        body, name=name, grid=(nb,),
        in_specs=[spec(a, N_DEV) for a in sends] + [spec(a, 4) for a in fromsib],
        out_specs=[spec(a, 4) for a in fromsib],
        out_shape=[jax.ShapeDtypeStruct(a.shape, a.dtype) for a in fromsib],
        compiler_params=pltpu.CompilerParams(dimension_semantics=("arbitrary",), vmem_limit_bytes=VMEM_LIMIT),
    )(*sends, *fromsib)


def _adamw_vals(w, g, m, v):
    m2 = ADAM_B1 * m + (1.0 - ADAM_B1) * g
    v2 = ADAM_B2 * v + (1.0 - ADAM_B2) * (g * g)
    m_hat = m2 / (1.0 - ADAM_B1 ** ADAM_STEP)
    v_hat = v2 / (1.0 - ADAM_B2 ** ADAM_STEP)
    delta = -ADAM_LR * (m_hat / (jnp.sqrt(v_hat) + ADAM_EPS) + ADAM_WD * w)
    return delta, m2, v2


def _sum_adamw(recv, w, m, v, name):
    R, C = w.shape
    ns = recv.shape[0]
    br = next((t for t in (256, 128, 64, 32, 16) if R % t == 0), R)

    def body(r_ref, w_ref, m_ref, v_ref, g_ref, d_ref, m2_ref, v2_ref):
        g = r_ref[0].astype(F32)
        for d in range(1, ns):
            g = g + r_ref[d].astype(F32)
        dl, m2, v2 = _adamw_vals(w_ref[...], g, m_ref[...], v_ref[...])
        g_ref[...] = g
        d_ref[...] = dl
        m2_ref[...] = m2
        v2_ref[...] = v2

    spec = pl.BlockSpec((br, C), lambda i: (i, 0))
    return pl.pallas_call(
        body, name=name, grid=(R // br,),
        in_specs=[pl.BlockSpec((ns, br, C), lambda i: (0, i, 0)), spec, spec, spec], out_specs=[spec] * 4,
        out_shape=[jax.ShapeDtypeStruct((R, C), F32)] * 4,
        compiler_params=pltpu.CompilerParams(dimension_semantics=("arbitrary",)),
    )(recv, w, m, v)


def _updates_call(recvs, ws, ms, vs, name, host=None):
    n = len(recvs)
    nb = 8

    def body(*refs):
        for a in range(n):
            r_ref, w_ref, m_ref, v_ref = refs[a], refs[n + a], refs[2 * n + a], refs[3 * n + a]
            g_ref, d_ref, m2_ref, v2_ref = refs[4 * n + 4 * a:4 * n + 4 * a + 4]
            g = r_ref[0].astype(F32)
            for d in range(1, r_ref.shape[0]):
                g = g + r_ref[d].astype(F32)
            dl, m2, v2 = _adamw_vals(w_ref[...], g, m_ref[...], v_ref[...])
            g_ref[...] = g
            d_ref[...] = dl
            m2_ref[...] = m2
            v2_ref[...] = v2

    def spec3(r):
        return _slab_spec(r.shape[0], r.shape[1], r.shape[2], nb)

    def spec2(w):
        return _slab_spec2(w.shape[0], w.shape[1], nb)

    res, hosted = _hosting_call(
        body, name, nb, host, list(recvs) + list(ws) + list(ms) + list(vs),
        [spec3(r) for r in recvs] + [spec2(w) for w in ws] * 3,
        [jax.ShapeDtypeStruct(w.shape, F32) for w in ws for _ in range(4)],
        [spec2(w) for w in ws for _ in range(4)], [])
    return [res[4 * a:4 * a + 4] for a in range(n)], hosted


def _small_sum(gath, loss_g, row0_g, name):
    _, R, C = gath.shape
    br = R // 3

    def body(g_ref, l_ref, r_ref, go_ref, lo_ref):
        g = g_ref[0].astype(F32)
        lsum = l_ref[0]
        for d in range(1, N_DEV):
            g = g + g_ref[d].astype(F32)
            lsum = lsum + l_ref[d]
        go_ref[...] = g
        lo_ref[...] = lsum

        @pl.when(pl.program_id(0) == 0)
        def _():
            row0 = r_ref[0]
            for d in range(1, N_DEV):
                row0 = row0 + r_ref[d]
            go_ref[0:8, :] = go_ref[0:8, :] + jnp.where(lax.broadcasted_iota(jnp.int32, row0.shape, 0) == 0, row0, 0.0)

    return pl.pallas_call(
        body, name=name, grid=(R // br,),
        in_specs=[pl.BlockSpec((N_DEV, br, C), lambda i: (0, i, 0)),
                  pl.BlockSpec((N_DEV, 8, HD), lambda i: (0, 0, 0)), pl.BlockSpec((N_DEV, 8, C), lambda i: (0, 0, 0))],
        out_specs=[pl.BlockSpec((br, C), lambda i: (i, 0)), pl.BlockSpec((8, HD), lambda i: (0, 0))],
        out_shape=[jax.ShapeDtypeStruct((R, C), F32), jax.ShapeDtypeStruct((8, HD), F32)],
        compiler_params=pltpu.CompilerParams(dimension_semantics=("arbitrary",)),
    )(gath, loss_g, row0_g)


def _adamw_multi(ws, gs, ms, vs, name, nblk=1):
    n = len(ws)

    def body(*refs):
        for a in range(n):
            dl, m2, v2 = _adamw_vals(refs[a][...], refs[n + a][...], refs[2 * n + a][...], refs[3 * n + a][...])
            refs[4 * n + 3 * a][...] = dl
            refs[4 * n + 3 * a + 1][...] = m2
            refs[4 * n + 3 * a + 2][...] = v2

    def spec(x):
        rest = (0,) * (x.ndim - 1)
        return pl.BlockSpec((x.shape[0] // nblk,) + tuple(x.shape[1:]), lambda i: (i,) + rest)

    res = pl.pallas_call(
        body, name=name, grid=(nblk,),
        in_specs=[spec(w) for w in ws] * 4, out_specs=[spec(w) for w in ws for _ in range(3)],
        out_shape=[jax.ShapeDtypeStruct(w.shape, F32) for w in ws for _ in range(3)],
        compiler_params=pltpu.CompilerParams(dimension_semantics=("arbitrary",), vmem_limit_bytes=VMEM_LIMIT),
    )(*ws, *gs, *ms, *vs)
    return [res[3 * a:3 * a + 3] for a in range(n)]


def _s5_param_fn(lr, li, ls, btr, bti):
    step = jnp.exp(ls)
    er = jnp.exp(lr * step)
    ang = li * step
    ar = er * jnp.cos(ang)
    ai = er * jnp.sin(ang)
    nr = ar - 1.0
    den = lr * lr + li * li
    fr = (nr * lr + ai * li) / den
    fi = (ai * lr - nr * li) / den
    return ar, ai, fr * btr - fi * bti, fr * bti + fi * btr


def _s5_params(lr, li, ls, btr, bti, cre, cim):
    nb = S5_G // S5_GB
    GC = S5_GB * S5_C
    expand = jnp.asarray(np.tile(np.eye(S5_P, dtype=np.float32), (1, S5_GB)), BF16)
    own = jnp.asarray((np.arange(GC)[:, None] // S5_C == np.arange(S5_W)[None, :] // S5_P).astype(np.float32))

    def body(lr_ref, li_ref, ls_ref, br_ref, bi_ref, cr_ref, ci_ref, e_ref, own_ref, ar_ref, ai_ref, bm_ref, cm_ref):
        ar, ai, bbr, bbi = _s5_param_fn(lr_ref[...], li_ref[...], ls_ref[...], br_ref[...], bi_ref[...])
        ar_ref[...] = ar
        ai_ref[...] = ai

        def plane(x, n):
            rows = x[n * S5_GB:(n + 1) * S5_GB].reshape(GC, S5_P).astype(BF16)
            return _dot(rows, e_ref[...]) * own_ref[...]

        for n in range(nb):
            bm_ref[n] = jnp.concatenate([plane(bbr, n), plane(bbi, n)], axis=-1).astype(BF16)
            cm_ref[n] = jnp.concatenate([plane(cr_ref[...], n), -plane(ci_ref[...], n)], axis=-1).astype(BF16)

    sd = jax.ShapeDtypeStruct
    return pl.pallas_call(
        body, name="s5_params",
        out_shape=[sd(lr.shape, F32), sd(lr.shape, F32), sd((nb, GC, 2 * S5_W), BF16), sd((nb, GC, 2 * S5_W), BF16)],
        compiler_params=pltpu.CompilerParams(vmem_limit_bytes=VMEM_LIMIT),
    )(lr, li, ls, btr, bti, cre, cim, expand, own)


def _s5_params_bwd(lr, li, ls, btr, bti, dar, dai, dbbr, dbbi):
    def body(lr_ref, li_ref, ls_ref, br_ref, bi_ref, dar_ref, dai_ref, dbbr_ref, dbbi_ref,
             dlr_ref, dli_ref, dls_ref, dbr_ref, dbi_ref):
        _, vjp = jax.vjp(_s5_param_fn, lr_ref[...], li_ref[...], ls_ref[...], br_ref[...], bi_ref[...])
        dlr, dli, dls, dbr, dbi = vjp((dar_ref[...], dai_ref[...], dbbr_ref[...], dbbi_ref[...]))
        dlr_ref[...] = dlr
        dli_ref[...] = dli
        dls_ref[...] = dls
        dbr_ref[...] = dbr
        dbi_ref[...] = dbi

    sd = jax.ShapeDtypeStruct
    return pl.pallas_call(
        body, name="s5_params_bwd",
        out_shape=[sd(lr.shape, F32), sd(lr.shape, F32), sd(ls.shape, F32), sd(btr.shape, F32), sd(btr.shape, F32)],
    )(lr, li, ls, btr, bti, dar, dai, dbbr, dbbi)


def _cpow(ar, ai, n):
    assert n & (n - 1) == 0
    while n > 1:
        ar, ai = ar * ar - ai * ai, 2.0 * ar * ai
        n //= 2
    return ar, ai


def _scan(st, cr, ci, init, nk, reverse, store, prev=None):
    W = S5_W

    def advance(k, sr, si):
        rows = pl.ds(k * 8 if isinstance(k, int) else pl.multiple_of(k * 8, 8), 8)
        nsr = cr * sr - ci * si + st[rows, 0:W]
        nsi = cr * si + ci * sr + st[rows, W:2 * W]
        if store:
            st[rows, 0:W] = nsr
            st[rows, W:2 * W] = nsi
        return nsr, nsi

    if prev is None:
        return lax.fori_loop(0, nk, lambda j, c: advance(nk - 1 - j if reverse else j, c[0], c[1]), init, unroll=2)
    assert reverse

    def step(j, carry):
        k = nk - 1 - j
        nsr, nsi = advance(k, carry[0], carry[1])
        prows = pl.ds(pl.multiple_of((k - 1) * 8, 8), 8)
        pr = prev[prows, 0:W]
        pi = prev[prows, W:2 * W]
        return nsr, nsi, carry[2] + nsr * pr + nsi * pi, carry[3] + nsi * pr - nsr * pi

    carry = lax.fori_loop(0, nk - 1, step, init, unroll=2)
    nsr, nsi = advance(0, carry[0], carry[1])
    return nsr, nsi, carry[2], carry[3]


def _chain(fin, fr, fi, pr, pi, reverse):
    W = S5_W
    fin[:, 0:W] = fr
    fin[:, W:2 * W] = fi
    rowid = lax.broadcasted_iota(jnp.int32, (8, W), 0)
    cr = jnp.zeros((1, W), F32)
    ci = jnp.zeros((1, W), F32)
    init_r = jnp.zeros((8, W), F32)
    init_i = jnp.zeros((8, W), F32)
    for s in (range(7, -1, -1) if reverse else range(8)):
        init_r = jnp.where(rowid == s, cr, init_r)
        init_i = jnp.where(rowid == s, ci, init_i)
        lr = fin[s:s + 1, 0:W]
        li = fin[s:s + 1, W:2 * W]
        cr, ci = lr + pr * cr - pi * ci, li + pr * ci + pi * cr
    return init_r, init_i


def _full_scan(st, fin, ar, ai, nk, reverse, prev=None, carry_in=None, carry_out=None):
    W = S5_W
    cr = jnp.broadcast_to(ar, (8, W))
    ci = jnp.broadcast_to(-ai if reverse else ai, (8, W))
    z = jnp.zeros((8, W), F32)
    if carry_in is None:
        fr, fi = _scan(st, cr, ci, (z, z), nk, reverse, store=False)
        pr, pi = _cpow(ar, -ai if reverse else ai, nk)
        init = _chain(fin, fr, fi, pr, pi, reverse)
    else:
        init = (carry_in[:, 0:W], carry_in[:, W:2 * W])
    if carry_out is not None:
        carry_out[:, 0:W] = init[0]
        carry_out[:, W:2 * W] = init[1]
    if prev is None:
        return _scan(st, cr, ci, init, nk, reverse, store=True)
    return _scan(st, cr, ci, init + (z, z), nk, reverse, store=True, prev=prev)


def _s5_specs(L):
    W2 = 2 * S5_W
    GC = S5_GB * S5_C
    col = pl.BlockSpec((L, GC), lambda g: (0, g))
    vec = pl.BlockSpec((1, GC), lambda g: (0, g))
    avec = pl.BlockSpec((1, S5_W), lambda g: (0, g))
    bmat = pl.BlockSpec((None, GC, W2), lambda g: (g, 0, 0))
    cmat = pl.BlockSpec((None, W2, GC), lambda g: (g, 0, 0))
    return col, vec, avec, bmat, cmat


def _interleave(dst, src, nk):
    for s in range(8):
        dst[pl.ds(s, nk, stride=8), :] = src[s * nk:(s + 1) * nk, :]


def _deinterleave(dst, src, nk):
    for s in range(8):
        dst[s * nk:(s + 1) * nk, :] = src[pl.ds(s, nk, stride=8), :].astype(dst.dtype)


def _hosting_call(body, name, nsteps, host, ins, in_specs, outs, out_specs, scratch):
    grid = (nsteps,) if isinstance(nsteps, int) else tuple(nsteps)
    params = pltpu.CompilerParams(dimension_semantics=("arbitrary",) * len(grid), vmem_limit_bytes=VMEM_LIMIT)
    if host is None:
        res = pl.pallas_call(
            body, name=name, grid=grid, in_specs=in_specs, out_specs=out_specs, out_shape=outs,
            scratch_shapes=scratch, compiler_params=params,
        )(*ins)
        return list(res), []
    n_in, n_out, n_sc = len(ins), len(outs), len(scratch)
    h_in, h_out = len(host.ins), len(host.outs)

    def hosted(*refs):
        a = refs[:n_in]
        ha = refs[n_in:n_in + h_in]
        o = refs[n_in + h_in:n_in + h_in + n_out]
        ho = refs[n_in + h_in + n_out:n_in + h_in + n_out + h_out]
        sc = refs[n_in + h_in + n_out + h_out:n_in + h_in + n_out + h_out + n_sc]
        hs = refs[n_in + h_in + n_out + h_out + n_sc:]
        first = functools.reduce(jnp.logical_and, [pl.program_id(i) == 0 for i in range(len(grid))])
        last = functools.reduce(jnp.logical_and, [pl.program_id(i) == g - 1 for i, g in enumerate(grid)])

        @pl.when(first)
        def _():
            host.start(ha, ho, hs)

        body(*a, *o, *sc)

        @pl.when(last)
        def _():
            host.finish(ha, ho, hs)

    hbm = pl.BlockSpec(memory_space=pl.ANY)
    res = pl.pallas_call(
        hosted, name=name, grid=grid,
        in_specs=list(in_specs) + [hbm] * h_in, out_specs=list(out_specs) + [hbm] * h_out,
        out_shape=list(outs) + list(host.outs), scratch_shapes=list(scratch) + list(host.scratch),
        compiler_params=params,
    )(*ins, *host.ins)
    return list(res[:n_out]), list(res[n_out:])


def _s5_fwd(u, bm, cm, ar, ai, dvec, host=None):
    L = u.shape[0]
    nk = L // 8
    GC = S5_GB * S5_C
    nb = S5_G // S5_GB
    col, vec, avec, bmat, cmat = _s5_specs(L)

    def body(u_ref, b_ref, c_ref, ar_ref, ai_ref, d_ref, y_ref, carry_ref, st, fin, ui, yi):
        _interleave(ui, u_ref, nk)
        for r in range(8):
            rows = slice(r * nk, (r + 1) * nk)
            st[rows, :] = _dot(ui[rows, :].astype(BF16), b_ref[...])
        _full_scan(st, fin, ar_ref[...], ai_ref[...], nk, reverse=False, carry_out=carry_ref)
        for r in range(8):
            rows = slice(r * nk, (r + 1) * nk)
            yi[rows, :] = _dot_nt(st[rows, :].astype(BF16), c_ref[...]) + d_ref[...] * ui[rows, :]
        _deinterleave(y_ref, yi, nk)

    return _hosting_call(
        body, "s5_fwd", nb, host,
        [u, bm, cm, ar, ai, dvec], [col, bmat, bmat, avec, avec, vec],
        [jax.ShapeDtypeStruct(u.shape, F32), jax.ShapeDtypeStruct((nb * 8, 2 * S5_W), F32)],
        [col, pl.BlockSpec((8, 2 * S5_W), lambda g: (g, 0))],
        [pltpu.VMEM((L, 2 * S5_W), F32), pltpu.VMEM((8, 2 * S5_W), F32), pltpu.VMEM((L, GC), F32),
         pltpu.VMEM((L, GC), F32)])


def _s5_bwd(u, dy, carry, bm, cm, ar, ai, dvec, mask, rmat, host=None):
    L = u.shape[0]
    nk = L // 8
    W = S5_W
    GC = S5_GB * S5_C
    col, vec, avec, bmat, cmat = _s5_specs(L)
    hi = lax.Precision.HIGHEST

    def body(u_ref, dy_ref, carry_ref, b_ref, ct_ref, ar_ref, ai_ref, d_ref, mask_ref, r_ref,
             du_ref, db_ref, dc_ref, dd_ref, dar_ref, dai_ref, sa, sb, fin, ui, dyi, dui):
        ar = ar_ref[...]
        ai = ai_ref[...]
        _interleave(ui, u_ref, nk)
        _interleave(dyi, dy_ref, nk)
        for r in range(8):
            rows = slice(r * nk, (r + 1) * nk)
            sa[rows, :] = _dot(ui[rows, :].astype(BF16), b_ref[...])
            sb[rows, :] = _dot(dyi[rows, :].astype(BF16), ct_ref[...])
        _full_scan(sa, fin, ar, ai, nk, reverse=False, carry_in=carry_ref)
        gr, gi, accr, acci = _full_scan(sb, fin, ar, ai, nk, reverse=True, prev=sa)
        rowid = lax.broadcasted_iota(jnp.int32, (8, W), 0)
        last = pl.ds((nk - 1) * 8, 8)
        pr = jnp.where(rowid == 0, 0.0, pltpu.roll(sa[last, 0:W], 1, 0))
        pi = jnp.where(rowid == 0, 0.0, pltpu.roll(sa[last, W:2 * W], 1, 0))
        accr = accr + gr * pr + gi * pi
        acci = acci + gi * pr - gr * pi
        dar_ref[...] = jnp.sum(accr, axis=0, keepdims=True)
        dai_ref[...] = jnp.sum(acci, axis=0, keepdims=True)
        dbf = jnp.zeros((GC, 2 * W), F32)
        dcf = jnp.zeros((GC, 2 * W), F32)
        dd = jnp.zeros((1, GC), F32)
        for r in range(8):
            rows = slice(r * nk, (r + 1) * nk)
            ub = ui[rows, :]
            dyb = dyi[rows, :]
            gb = sb[rows, :].astype(BF16)
            dui[rows, :] = _dot_nt(gb, b_ref[...]) + d_ref[...] * dyb
            dbf = dbf + _dot_tn(ub.astype(BF16), gb)
            dcf = dcf + _dot_tn(dyb.astype(BF16), sa[rows, :].astype(BF16))
            dd = dd + jnp.sum(dyb * ub, axis=0, keepdims=True)
        db_ref[...] = jnp.dot(dbf * mask_ref[...], r_ref[...], precision=hi, preferred_element_type=F32)
        dc_ref[...] = jnp.dot(dcf * mask_ref[...], r_ref[...], precision=hi, preferred_element_type=F32)
        dd_ref[...] = dd
        _deinterleave(du_ref, dui, nk)

    cmp_spec = pl.BlockSpec((GC, 2 * S5_P), lambda g: (g, 0))
    whole = lambda shape: pl.BlockSpec(shape, lambda g: (0, 0))
    sd = jax.ShapeDtypeStruct
    return _hosting_call(
        body, "s5_bwd", S5_G // S5_GB, host,
        [u, dy, carry, bm, cm, ar, ai, dvec, mask, rmat],
        [col, col, pl.BlockSpec((8, 2 * W), lambda g: (g, 0)), bmat, bmat, avec, avec, vec, whole(mask.shape),
         whole(rmat.shape)],
        [sd(u.shape, BF16), sd((S5_G * S5_C, 2 * S5_P), F32), sd((S5_G * S5_C, 2 * S5_P), F32),
         sd((1, PRIM), F32), sd((1, S5_G * S5_P), F32), sd((1, S5_G * S5_P), F32)],
        [col, cmp_spec, cmp_spec, vec, avec, avec],
        [pltpu.VMEM((L, 2 * W), F32), pltpu.VMEM((L, 2 * W), F32), pltpu.VMEM((8, 2 * W), F32),
         pltpu.VMEM((L, GC), F32), pltpu.VMEM((L, GC), F32), pltpu.VMEM((L, GC), F32)])


def _s5_compact_consts():
    g_row = np.arange(S5_GB * S5_C) // S5_C
    col = np.arange(2 * S5_W)
    g_col = (col % S5_W) // S5_P
    mask = (g_row[:, None] == g_col[None, :]).astype(np.float32)
    tgt = (col // S5_W) * S5_P + col % S5_P
    rmat = (tgt[:, None] == np.arange(2 * S5_P)[None, :]).astype(np.float32)
    return jnp.asarray(mask), jnp.asarray(rmat)


def _attn_scores(q_ref, k_ref, qb, bq, scale):
    ext = (qb + 1) * bq
    s = _dot_nt(q_ref[qb * bq:ext, :], k_ref[0:ext, :]) * scale
    qpos = lax.broadcasted_iota(jnp.int32, (bq, bq), 0)
    kpos = lax.broadcasted_iota(jnp.int32, (bq, bq), 1)
    diag = jnp.where(kpos <= qpos, s[:, ext - bq:], NEG)
    return diag if qb == 0 else jnp.concatenate([s[:, :ext - bq], diag], axis=-1)


def _attn_fwd(qp, kp, v, scale):
    L = qp.shape[0]
    bq = min(256, L)

    def body(q_ref, k_ref, v_ref, o_ref, lse_ref):
        for qb in range(L // bq):
            rows = slice(qb * bq, (qb + 1) * bq)
            s = _attn_scores(q_ref, k_ref, qb, bq, scale)
            m = jnp.max(s, axis=-1, keepdims=True)
            e = jnp.exp(s - m)
            l = jnp.sum(e, axis=-1, keepdims=True)
            o_ref[rows, :] = _dot(e.astype(BF16), v_ref[0:(qb + 1) * bq, :]) / l
            lse_ref[rows, :] = jnp.broadcast_to(m + jnp.log(l), (bq, HD))

    blk = pl.BlockSpec((L, HD), lambda h: (0, h))
    wide = pl.BlockSpec((L, 2 * HD), lambda h: (0, h))
    return pl.pallas_call(
        body, name="mla_attn_fwd", grid=(MLA_H,),
        in_specs=[wide, wide, blk], out_specs=[blk, blk],
        out_shape=[jax.ShapeDtypeStruct((L, MLA_H * HD), F32)] * 2,
        compiler_params=pltpu.CompilerParams(dimension_semantics=("arbitrary",), vmem_limit_bytes=VMEM_LIMIT),
    )(qp, kp, v)


def _attn_bwd(qp, kp, v, o, lse, do, scale):
    L = qp.shape[0]
    bq = min(256, L)
    nq = L // bq

    def body(q_ref, k_ref, v_ref, o_ref, lse_ref, do_ref, dq_ref, dk_ref, dv_ref, dk_acc, dv_acc):
        dk_acc[...] = jnp.zeros_like(dk_acc)
        dv_acc[...] = jnp.zeros_like(dv_acc)
        for qb in range(nq):
            rows = slice(qb * bq, (qb + 1) * bq)
            ext = (qb + 1) * bq
            do = do_ref[rows, :]
            dob = do.astype(BF16)
            p = jnp.exp(_attn_scores(q_ref, k_ref, qb, bq, scale) - lse_ref[rows, 0:1])
            dp = _dot_nt(dob, v_ref[0:ext, :])
            dsum = jnp.sum(do * o_ref[rows, :], axis=-1, keepdims=True)
            ds = (p * (dp - dsum) * scale).astype(BF16)
            dq_ref[rows, :] = _dot(ds, k_ref[0:ext, :]).astype(dq_ref.dtype)
            dk_acc[0:ext, :] += _dot_tn(ds, q_ref[rows, :])
            dv_acc[0:ext, :] += _dot_tn(p.astype(BF16), dob)
        dk_ref[...] = dk_acc[...].astype(dk_ref.dtype)
        dv_ref[...] = dv_acc[...].astype(dv_ref.dtype)

    sd = jax.ShapeDtypeStruct
    blk = pl.BlockSpec((L, HD), lambda h: (0, h))
    wide = pl.BlockSpec((L, 2 * HD), lambda h: (0, h))
    return pl.pallas_call(
        body, name="mla_attn_bwd", grid=(MLA_H,),
        in_specs=[wide, wide, blk, blk, blk, blk], out_specs=[wide, wide, blk],
        out_shape=[sd((L, MLA_H * 2 * HD), BF16), sd((L, MLA_H * 2 * HD), BF16), sd((L, MLA_H * HD), BF16)],
        scratch_shapes=[pltpu.VMEM((L, 2 * HD), F32), pltpu.VMEM((L, HD), F32)],
        compiler_params=pltpu.CompilerParams(dimension_semantics=("arbitrary",), vmem_limit_bytes=VMEM_LIMIT),
    )(qp, kp, v, o, lse, do)


def _kv_fn(mem, gm, w, gk):
    kv = _mm(_rms(mem, gm, D_MODEL), w)
    k = jnp.concatenate([_rms(kv[:, HD * h:HD * (h + 1)], gk, HD) for h in range(X_HEADS)], axis=-1)
    return k, kv[:, XQ:]


def _kv_prep(mem, gm, w, gk, name):
    def fn(mem, gm, w, gk):
        return _kv_fn(mem, gm, w, gk)
    M = mem.shape[0]
    return _rowwise(name, fn, [('c', mem), ('c', gm), ('c', w), ('c', gk)],
                    [('c', (M, XQ), F32), ('c', (M, XQ), F32)], 1)


def _kv_prep_bwd(mem, gm, w, gk, dk, dv, name):
    def fn(mem, gm, w, gk, dk, dv):
        _, vjp = jax.vjp(lambda a, b, c: _kv_fn(mem, a, b, c), gm, w, gk)
        return vjp((dk, dv))
    return _rowwise(name, fn, [('c', mem), ('c', gm), ('c', w), ('c', gk), ('c', dk), ('c', dv)],
                    [('c', gm.shape, F32), ('c', w.shape, BF16), ('c', gk.shape, F32)], 1)


def _forward_merge(x, mix, mix_kind, xq, gate, k, v, gq, wout, name, nblk, host=None):
    def fn(x, mix, xq, gate, k, v, gq, wout):
        o = _merge(mix, xq, gate, k, v, gq)
        return (x + _dot(o.astype(BF16), wout),)
    L = x.shape[0]
    out = _rowwise(name, fn, [('r', x), (mix_kind, mix), ('r', xq), ('r', gate), ('c', k), ('c', v), ('c', gq),
                              ('c', wout)], [('r', (L, D_MODEL), F32)], nblk, host=host)
    return out[0] if host is None else (out[0][0], out[1])


def _backward_merge(dx, mix, mix_kind, xq, gate, k, v, gq, wout, name, nblk, host=None):
    def fn(dx, mix, xq, gate, k, v, gq, wout):
        g16 = dx.astype(BF16)
        do = _dot_nt(g16, wout)
        o, vjp = jax.vjp(_merge, mix, xq, gate, k, v, gq)
        dmix, dxq, dgate, dk, dv, dgq = vjp(do)
        return dmix, dxq, dgate, o, g16, dk, dv, dgq
    L = dx.shape[0]
    return _rowwise(
        name, fn,
        [('r', dx), (mix_kind, mix), ('r', xq), ('r', gate), ('c', k), ('c', v), ('c', gq), ('c', wout)],
        [('r', (L, PRIM), F32), ('r', (L, XQ), BF16), ('r', (L, BRANCH), BF16), ('t', (BRANCH, L), BF16),
         ('r', (L, D_MODEL), BF16), ('a', k.shape, F32), ('a', v.shape, F32), ('a', gq.shape, F32)], nblk,
        host=host)


_MLA_IN = 3392
_MLA_IN_PAD = 3456


def _uq_rows(wt):
    r = wt.reshape(MLA_H, HD + ROPE, wt.shape[1])
    return jnp.concatenate([r[:, :HD].reshape(PRIM, -1),
                            jnp.pad(r[:, HD:], ((0, 0), (0, HD - ROPE), (0, 0))).reshape(PRIM, -1)], axis=0)


def _uq_rows_back(wt):
    nope = wt[:PRIM].reshape(MLA_H, HD, -1)
    rope = wt[PRIM:].reshape(MLA_H, HD, -1)[:, :ROPE]
    return jnp.concatenate([nope, rope], axis=1).reshape(MLA_H * (HD + ROPE), -1)


def _mla_in_rows_back(wt):
    return jnp.concatenate([wt[:768], wt[3328:3392], wt[768:3328]], axis=0)


_SMALL = (("ln_gain", 2048), ("mem_norm", 2048), ("xq_norm", 256), ("xk_norm", 256), ("s5_lambda_re", 6144),
          ("s5_lambda_im", 6144), ("s5_log_step", 96), ("s5_b_re", 98304), ("s5_b_im", 98304), ("s5_c_re", 98304),
          ("s5_c_im", 98304), ("s5_d", 1536), ("mla_q_lora_norm", 512), ("mla_kv_lora_norm", 256),
          ("mla_q_nope_norm", 128), ("mla_k_nope_norm", 128), ("mla_q_rope_norm", 64), ("mla_k_rope_norm", 64))
_SMALL_ROWS = 432
_SMALL_OFF = {name: sum(n for _, n in _SMALL[:i]) for i, (name, _) in enumerate(_SMALL)}


def _pack_small(d):
    flat = jnp.concatenate([d[n].reshape(-1).astype(F32) for n, _ in _SMALL])
    return jnp.pad(flat, (0, _SMALL_ROWS * 1024 - flat.shape[0])).reshape(_SMALL_ROWS, 1024)


def _unpack_small(p, name, shape):
    off = _SMALL_OFF[name]
    return p.reshape(-1)[off:off + int(np.prod(shape))].reshape(shape)


_WEIGHTS = ('ln_gain', 'w_out', 'mem_norm', 'w_mem_kv', 'xq_norm', 'xk_norm', 's5_w_in', 's5_lambda_re',
            's5_lambda_im', 's5_log_step', 's5_b_re', 's5_b_im', 's5_c_re', 's5_c_im', 's5_d', 's5_w_glu', 'mla_w_in',
            'mla_q_lora_norm', 'mla_kv_lora_norm', 'mla_w_uq', 'mla_w_ukv', 'mla_q_nope_norm', 'mla_k_nope_norm',
            'mla_q_rope_norm', 'mla_k_rope_norm')
_BIG = ('w_out', 'w_mem_kv', 's5_w_in', 's5_w_glu', 'mla_w_in', 'mla_w_uq', 'mla_w_ukv')


def _pad128(g):
    return jnp.pad(g.reshape(1, -1), ((0, 0), (0, HD - g.shape[-1])))


def kernel(x, mem, positions, ln_gain, w_out, mem_norm, w_mem_kv, xq_norm, xk_norm, s5_w_in, s5_lambda_re, s5_lambda_im, s5_log_step, s5_b_re, s5_b_im, s5_c_re, s5_c_im, s5_d, s5_w_glu, mla_w_in, mla_q_lora_norm, mla_kv_lora_norm, mla_w_uq, mla_w_ukv, mla_q_nope_norm, mla_k_nope_norm, mla_q_rope_norm, mla_k_rope_norm, loss_target, m_ln_gain, m_w_out, m_mem_norm, m_w_mem_kv, m_xq_norm, m_xk_norm, m_s5_w_in, m_s5_lambda_re, m_s5_lambda_im, m_s5_log_step, m_s5_b_re, m_s5_b_im, m_s5_c_re, m_s5_c_im, m_s5_d, m_s5_w_glu, m_mla_w_in, m_mla_q_lora_norm, m_mla_kv_lora_norm, m_mla_w_uq, m_mla_w_ukv, m_mla_q_nope_norm, m_mla_k_nope_norm, m_mla_q_rope_norm, m_mla_k_rope_norm, v_ln_gain, v_w_out, v_mem_norm, v_w_mem_kv, v_xq_norm, v_xk_norm, v_s5_w_in, v_s5_lambda_re, v_s5_lambda_im, v_s5_log_step, v_s5_b_re, v_s5_b_im, v_s5_c_re, v_s5_c_im, v_s5_d, v_s5_w_glu, v_mla_w_in, v_mla_q_lora_norm, v_mla_kv_lora_norm, v_mla_w_uq, v_mla_w_ukv, v_mla_q_nope_norm, v_mla_k_nope_norm, v_mla_q_rope_norm, v_mla_k_rope_norm):
    weights = dict(ln_gain=ln_gain, w_out=w_out, mem_norm=mem_norm, w_mem_kv=w_mem_kv, xq_norm=xq_norm,
                   xk_norm=xk_norm, s5_w_in=s5_w_in, s5_lambda_re=s5_lambda_re, s5_lambda_im=s5_lambda_im,
                   s5_log_step=s5_log_step, s5_b_re=s5_b_re, s5_b_im=s5_b_im, s5_c_re=s5_c_re, s5_c_im=s5_c_im,
                   s5_d=s5_d, s5_w_glu=s5_w_glu, mla_w_in=mla_w_in, mla_q_lora_norm=mla_q_lora_norm,
                   mla_kv_lora_norm=mla_kv_lora_norm, mla_w_uq=mla_w_uq, mla_w_ukv=mla_w_ukv,
                   mla_q_nope_norm=mla_q_nope_norm, mla_k_nope_norm=mla_k_nope_norm,
                   mla_q_rope_norm=mla_q_rope_norm, mla_k_rope_norm=mla_k_rope_norm)
    m_in = dict(zip(_WEIGHTS, (m_ln_gain, m_w_out, m_mem_norm, m_w_mem_kv, m_xq_norm, m_xk_norm, m_s5_w_in,
                               m_s5_lambda_re, m_s5_lambda_im, m_s5_log_step, m_s5_b_re, m_s5_b_im, m_s5_c_re,
                               m_s5_c_im, m_s5_d, m_s5_w_glu, m_mla_w_in, m_mla_q_lora_norm, m_mla_kv_lora_norm,
                               m_mla_w_uq, m_mla_w_ukv, m_mla_q_nope_norm, m_mla_k_nope_norm, m_mla_q_rope_norm,
                               m_mla_k_rope_norm)))
    v_in = dict(zip(_WEIGHTS, (v_ln_gain, v_w_out, v_mem_norm, v_w_mem_kv, v_xq_norm, v_xk_norm, v_s5_w_in,
                               v_s5_lambda_re, v_s5_lambda_im, v_s5_log_step, v_s5_b_re, v_s5_b_im, v_s5_c_re,
                               v_s5_c_im, v_s5_d, v_s5_w_glu, v_mla_w_in, v_mla_q_lora_norm, v_mla_kv_lora_norm,
                               v_mla_w_uq, v_mla_w_ukv, v_mla_q_nope_norm, v_mla_k_nope_norm, v_mla_q_rope_norm,
                               v_mla_k_rope_norm)))

    x0 = x[0]
    mem0 = mem[0]
    target = loss_target[0]
    L = x0.shape[0]
    nblk = 4
    nb_big = 8
    me = 4 * lax.axis_index("x") + 2 * lax.axis_index("y") + lax.axis_index("c")

    lora = jnp.pad(jnp.concatenate([mla_q_lora_norm, mla_kv_lora_norm], axis=1), ((0, 7), (0, HD - 96)))
    def gather(*shards):
        return _plan_all_gather(list(shards))

    kh = D_MODEL // 2
    (b_mkv0, b_glu, b_in_mla, b_out0, b_uq, b_ukv, b_mkv1, b_out1), (W_in_s5,) = _cast_call(
        [w_mem_kv[0], s5_w_glu[0], jnp.transpose(mla_w_in[0]), w_out[0], jnp.transpose(mla_w_uq[0]), mla_w_ukv[0],
         w_mem_kv[1], w_out[1]], "cast_shards", host=gather(s5_w_in[0].astype(BF16)))

    ln0, ln1 = ln_gain[0:1], ln_gain[1:2]
    gq0, gq1 = xq_norm[0:1], xq_norm[1:2]
    gk0, gk1 = xk_norm[0:1], xk_norm[1:2]
    gm0, gm1 = mem_norm[0:1], mem_norm[1:2]
    gqn, gkn = mla_q_nope_norm, mla_k_nope_norm
    gqr, gkr = _pad128(mla_q_rope_norm), _pad128(mla_k_rope_norm)

    lr3 = s5_lambda_re.reshape(S5_G, 1, S5_P)
    li3 = s5_lambda_im.reshape(S5_G, 1, S5_P)
    ls3 = s5_log_step.reshape(S5_G, 1, 1)
    btr = jnp.swapaxes(s5_b_re[0], 1, 2)
    bti = jnp.swapaxes(s5_b_im[0], 1, 2)
    a_r, a_i, bm, cm = _s5_params(lr3, li3, ls3, btr, bti, s5_c_re[0], s5_c_im[0])
    a_r2 = a_r.reshape(1, S5_G * S5_P)
    a_i2 = a_i.reshape(1, S5_G * S5_P)
    cmask, rmat = _s5_compact_consts()

    half = ROPE // 2
    inv_freq = ROPE_THETA ** (-jnp.arange(half, dtype=F32) / half)
    invf = jnp.concatenate([inv_freq, inv_freq, jnp.zeros((HD - ROPE,), F32)]).reshape(1, HD)

    def rot_tables(pos, invf):
        ang = pos.astype(F32) * invf
        lane = lax.broadcasted_iota(jnp.int32, ang.shape, 1)
        c = jnp.where(lane < ROPE, jnp.cos(ang), 0.0)
        s = jnp.sin(ang)
        return c, jnp.where(lane < half, -s, 0.0), jnp.where((lane >= half) & (lane < ROPE), s, 0.0)

    tc, ts1, ts2 = _rowwise("rot_tables", rot_tables, [('r', positions.reshape(L, 1)), ('c', invf)],
                            [('r', (L, HD), F32)] * 3, nblk)

    def in_s5(x, g, w):
        proj = _mm_slots(_rms(x, g, D_MODEL).astype(BF16), w)
        return proj[:, :PRIM], proj[:, PRIM:PRIM + XQ], proj[:, PRIM + XQ:]

    u_s5, xq_a, gate_a = _rowwise(
        "s5_in", in_s5, [('r', x0), ('c', ln0), ('c', W_in_s5)],
        [('r', (L, PRIM), F32), ('r', (L, XQ), F32), ('r', (L, BRANCH), F32)], nblk)
    (y_s5, s5_carry), (W_glu, G_mkv0) = _s5_fwd(u_s5, bm, cm, a_r2, a_i2, s5_d, host=gather(b_glu, b_mkv0))

    def glu(y, w):
        z = _mm_slots(_gelu(y).astype(BF16), w)
        return z[:, :PRIM] * _sigmoid(z[:, PRIM:]), z

    (y2, z_glu), (G_out0,) = _rowwise("s5_glu", glu, [('r', y_s5), ('c', W_glu)],
                                      [('r', (L, PRIM), F32), ('r', (L, 2 * PRIM), F32)], nblk, host=gather(b_out0))
    W_mkv0 = G_mkv0.reshape(D_MODEL, 2 * XQ)
    k_a, v_a = _kv_prep(mem0, gm0, W_mkv0, gk0, "kv_prep0")
    x1, (G_in_mla,) = _forward_merge(
        x0, y2, 'r', xq_a, gate_a, k_a, v_a, gq0, G_out0.reshape(BRANCH, D_MODEL), "merge0", nblk,
        host=gather(b_in_mla))
    W_in_mla = G_in_mla.reshape(_MLA_IN, D_MODEL)

    def in_mla(x, g, w):
        xn = _rms(x, g, D_MODEL).astype(BF16)
        a = _dot_nt(xn, w[0:768])
        kx = _dot_nt(xn, w[768:896])
        b = _dot_nt(xn, w[832:_MLA_IN])
        lane = lax.broadcasted_iota(jnp.int32, kx.shape, 1)
        return a[:, :512], a[:, 512:], b[:, :XQ], b[:, XQ:], jnp.where(lane < ROPE, kx, 0.0)

    (c_q, c_kv, xq_b, gate_b, krp), (G_uq, W_kv, G_lora) = _rowwise(
        "mla_in", in_mla, [('r', x1), ('c', ln1), ('c', W_in_mla)],
        [('r', (L, Q_LORA), F32), ('r', (L, KV_LORA), F32), ('r', (L, XQ), F32), ('r', (L, BRANCH), F32),
         ('r', (L, HD), F32)], nblk,
        host=gather(b_uq, b_ukv, lora))
    W_q = _uq_rows(G_uq.reshape(MLA_H * (HD + ROPE), Q_LORA))
    g_qlora = G_lora[:, 0, :64].reshape(1, Q_LORA)
    g_kvlora = G_lora[:, 0, 64:96].reshape(1, KV_LORA)

    def qkv(c_q, c_kv, krp, tc, ts1, ts2, gql, gkvl, wq, wkv, gqn, gkn, gqr, gkr):
        q = _dot_nt(_rms(c_q, gql, Q_LORA).astype(BF16), wq)
        kv = _mm_slots(_rms(c_kv, gkvl, KV_LORA).astype(BF16), wkv)
        kp, v = _kv_post(*_kv_chunks(kv), krp, gkn, gkr, tc, ts1, ts2)
        return _q_post(*_q_chunks(q), gqn, gqr, tc, ts1, ts2), kp, v

    qkv_consts = [('c', g_qlora), ('c', g_kvlora), ('c', W_q), ('c', W_kv), ('c', gqn), ('c', gkn), ('c', gqr),
                  ('c', gkr)]
    (q_pad, k_pad, v_h), (G_mkv1, G_out1) = _rowwise(
        "mla_qkv", qkv, [('r', c_q), ('r', c_kv), ('r', krp), ('r', tc), ('r', ts1), ('r', ts2)] + qkv_consts,
        [('r', (L, 2 * PRIM), BF16), ('r', (L, 2 * PRIM), BF16), ('r', (L, PRIM), BF16)], nblk,
        host=gather(b_mkv1, b_out1))
    W_out = (G_out0.reshape(BRANCH, D_MODEL), G_out1.reshape(BRANCH, D_MODEL))
    W_mkv = (W_mkv0, G_mkv1.reshape(D_MODEL, 2 * XQ))
    scale = (HD + ROPE) ** -0.5
    attn, lse = _attn_fwd(q_pad, k_pad, v_h, scale)
    k_b, v_b = _kv_prep(mem0, gm1, W_mkv[1], gk1, "kv_prep1")

    def merge_loss(x, mix, xq, gate, k, v, gq, wout, t):
        err = x + _dot(_merge(mix, xq, gate, k, v, gq).astype(BF16), wout) - t
        part = 0.5 * jnp.sum(jnp.sum(err * err, axis=-1, keepdims=True) * (1.0 / D_MODEL), axis=0, keepdims=True)
        return err * (1.0 / D_MODEL), jnp.broadcast_to(part, (1, HD))

    dx2, loss_part = _rowwise(
        "merge1_loss", merge_loss,
        [('r', x1), ('r', attn), ('r', xq_b), ('r', gate_b), ('c', k_b), ('c', v_b), ('c', gq1), ('c', W_out[1]),
         ('r', target)], [('r', (L, D_MODEL), F32), ('a', (1, HD), F32)], nblk)

    dattn, dxq_b, dgate_b, o_b, g_b, dk_b, dv_b, dgq1 = _backward_merge(
        dx2, attn, 'r', xq_b, gate_b, k_b, v_b, gq1, W_out[1], "merge1_bwd", nb_big)
    dgm1, dW_mkv1, dgk1 = _kv_prep_bwd(mem0, gm1, W_mkv[1], gk1, dk_b, dv_b, "kv_prep1_bwd")
    dW_out1 = _matmul_tn(o_b, g_b, "dw_out1")
    dq_pad, dk_pad, dv_h = _attn_bwd(q_pad, k_pad, v_h, attn, lse, dattn, scale)

    def qkv_bwd(c_q, c_kv, krp, tc, ts1, ts2, dqp, dkp, dv, gql, gkvl, wq, wkv, gqn, gkn, gqr, gkr):
        cqn, vjp_qn = jax.vjp(lambda a, b: _rms(a, b, Q_LORA), c_q, gql)
        ckvn, vjp_kvn = jax.vjp(lambda a, b: _rms(a, b, KV_LORA), c_kv, gkvl)
        cqn16 = cqn.astype(BF16)
        ckvn16 = ckvn.astype(BF16)
        q = _dot_nt(cqn16, wq)
        kv = _mm_slots(ckvn16, wkv)
        _, vjp_q = jax.vjp(lambda n, r, a, b: _q_post(n, r, a, b, tc, ts1, ts2), *_q_chunks(q), gqn, gqr)
        dnope, drope, dgqn, dgqr = vjp_q(dqp.astype(F32))
        dq = jnp.concatenate(dnope + drope, axis=-1)
        _, vjp_kv = jax.vjp(lambda n, v, k, a, b: _kv_post(n, v, k, a, b, tc, ts1, ts2), *_kv_chunks(kv), krp, gkn,
                            gkr)
        dkn, dvals, dkrp, dgkn, dgkr = vjp_kv((dkp.astype(F32), dv.astype(F32)))
        dkv = jnp.concatenate([x for pair in zip(dkn, dvals) for x in pair], axis=-1)
        dq16 = dq.astype(BF16)
        dkv16 = dkv.astype(BF16)
        dc_q, dgql = vjp_qn(_dot(dq16, wq))
        dc_kv, dgkvl = vjp_kvn(_mm_slots_nt(dkv16, wkv))
        return dc_q, dc_kv, dkrp, cqn16, dq16, ckvn16, dkv16, dgql, dgkvl, dgqn, dgkn, dgqr, dgkr

    (dc_q, dc_kv, dkrp, cqn16, dq16, ckvn16, dkv16, dgql, dgkvl, dgqn, dgkn, dgqr, dgkr) = _rowwise(
        "mla_qkv_bwd", qkv_bwd,
        [('r', c_q), ('r', c_kv), ('r', krp), ('r', tc), ('r', ts1), ('r', ts2), ('r', dq_pad), ('r', dk_pad),
         ('r', dv_h)] + qkv_consts,
        [('r', (L, Q_LORA), BF16), ('r', (L, KV_LORA), BF16), ('r', (L, HD), BF16), ('r', (L, Q_LORA), BF16),
         ('t', (2 * PRIM, L), BF16), ('t', (KV_LORA, L), BF16), ('r', (L, 2 * PRIM), BF16),
         ('a', (1, Q_LORA), F32), ('a', (1, KV_LORA), F32), ('a', (1, HD), F32), ('a', (1, HD), F32),
         ('a', (1, HD), F32), ('a', (1, HD), F32)], nb_big)
    dW_q = _matmul_tn(dq16, cqn16, "dw_uq")
    dW_kv = _matmul_tn_slots(ckvn16, dkv16, "dw_ukv")

    def in_bwd(x, dres, g, w, *dparts):
        dproj = jnp.concatenate(dparts, axis=-1).astype(BF16)
        xn, vjp = jax.vjp(lambda a, b: _rms(a, b, D_MODEL), x, g)
        if w.ndim == 3:
            dxn = _mm_slots_nt(dproj, w)
        else:
            dkr = dproj[:, 3328:]
            dkr = jnp.where(lax.broadcasted_iota(jnp.int32, dkr.shape, 1) < ROPE, dkr, jnp.zeros_like(dkr))
            dxn = _dot(dproj[:, :768], w[0:768]) + _dot(dproj[:, 768:3328], w[832:_MLA_IN]) + _dot(dkr, w[768:896])
        dx, dg = vjp(dxn)
        return dx + dres, xn, dproj, dg

    dx1, xn1, dproj1, dln1 = _rowwise(
        "mla_in_bwd", in_bwd,
        [('r', x1), ('r', dx2), ('c', ln1), ('c', W_in_mla), ('r', dc_q), ('r', dc_kv), ('r', dxq_b), ('r', dgate_b),
         ('r', dkrp)],
        [('r', (L, D_MODEL), F32), ('r', (L, D_MODEL), BF16), ('t', (_MLA_IN_PAD, L), BF16), ('a', (1, D_MODEL), F32)],
        nblk)
    dW_in_mla = _matmul_tn(dproj1, xn1, "dw_mla_in")

    grads1 = [dW_out1.reshape(N_DEV, 256, D_MODEL), dW_mkv1.reshape(N_DEV, 128, 2 * XQ),
              _mla_in_rows_back(dW_in_mla).reshape(N_DEV, 424, D_MODEL),
              _uq_rows_back(dW_q).reshape(N_DEV, 288, Q_LORA), dW_kv]
    (dy2, dxq_a, dgate_a, o_a, g_a, dk_a, dv_a, dgq0), pair1 = _backward_merge(
        dx1, y2, 'r', xq_a, gate_a, k_a, v_a, gq0, W_out[0], "merge0_bwd", nb_big, host=_plan_pair(grads1))
    dgm0, dW_mkv0, dgk0 = _kv_prep_bwd(mem0, gm0, W_mkv[0], gk0, dk_a, dv_a, "kv_prep0_bwd")
    dW_out0 = _matmul_tn(o_a, g_a, "dw_out0")
    t1 = list(_pair_add(grads1, pair1, "rs_add_layer1"))

    def glu_bwd(y, z, dy2, w):
        h, vjp_h = jax.vjp(_gelu, y)
        _, vjp_z = jax.vjp(lambda a, b: a * _sigmoid(b), z[:, :PRIM], z[:, PRIM:])
        dz16 = jnp.concatenate(vjp_z(dy2), axis=-1).astype(BF16)
        return vjp_h(_mm_slots_nt(dz16, w))[0], h.astype(BF16), dz16

    grads0 = [dW_out0.reshape(N_DEV, 256, D_MODEL), dW_mkv0.reshape(N_DEV, 128, 2 * XQ)]
    (dy_s5, h16, dz16), glu_hosted = _rowwise(
        "s5_glu_bwd", glu_bwd, [('r', y_s5), ('r', z_glu), ('r', dy2), ('c', W_glu)],
        [('r', (L, PRIM), F32), ('t', (PRIM, L), BF16), ('r', (L, 2 * PRIM), BF16)], nb_big,
        host=_combine(_plan_chips(t1[2:3]), _plan_pair(grads0)))
    recv_in_mla, pair0 = glu_hosted[:1], glu_hosted[1:]
    dW_glu = _matmul_tn_slots(h16, dz16, "dw_glu")
    t0 = list(_pair_add(grads0 + [dW_glu], pair0 + list(_exchange_call(_plan_pair([dW_glu]), "rs_pair_glu")),
                        "rs_add_layer0"))
    (du_s5, dbc, dcc, dd, dar, dai), recv_rest = _s5_bwd(u_s5, dy_s5, s5_carry, bm, cm, a_r2, a_i2, s5_d,
                                                        cmask, rmat, host=_plan_chips(t1[:2] + t1[3:] + t0))
    early_recv = recv_rest[:2] + recv_in_mla + recv_rest[2:]
    dbc4 = dbc.reshape(S5_G, S5_C, 2, S5_P)
    dcc4 = dcc.reshape(S5_G, S5_C, 2, S5_P)
    dlr, dli, dls, dbtr, dbti = _s5_params_bwd(
        lr3, li3, ls3, btr, bti, dar.reshape(S5_G, 1, S5_P), dai.reshape(S5_G, 1, S5_P), dbc4[:, :, 0], dbc4[:, :, 1])

    small_part = {
        "ln_gain": jnp.concatenate([jnp.zeros_like(dln1), dln1]), "mem_norm": jnp.concatenate([dgm0, dgm1]),
        "xq_norm": jnp.concatenate([dgq0, dgq1]), "xk_norm": jnp.concatenate([dgk0, dgk1]),
        "s5_lambda_re": dlr, "s5_lambda_im": dli, "s5_log_step": dls,
        "s5_b_re": jnp.swapaxes(dbtr, 1, 2), "s5_b_im": jnp.swapaxes(dbti, 1, 2),
        "s5_c_re": dcc4[:, :, 0], "s5_c_im": -dcc4[:, :, 1], "s5_d": dd,
        "mla_q_lora_norm": dgql, "mla_kv_lora_norm": dgkvl, "mla_q_nope_norm": dgqn, "mla_k_nope_norm": dgkn,
        "mla_q_rope_norm": dgqr[:, :ROPE], "mla_k_rope_norm": dgkr[:, :ROPE],
    }
    loss8 = jnp.pad(loss_part, ((0, 7), (0, 0)))
    (dx0, xn0, dproj0, dln0), (small_gath, loss_g) = _rowwise(
        "s5_in_bwd", in_bwd,
        [('r', x0), ('r', dx1), ('c', ln0), ('c', W_in_s5), ('r', du_s5), ('r', dxq_a),
         ('r', dgate_a)],
        [('r', (L, D_MODEL), F32), ('t', (D_MODEL, L), BF16), ('r', (L, 2 * BRANCH), BF16), ('a', (1, D_MODEL), F32)],
        nblk, host=_plan_all_gather([_pack_small(small_part).astype(BF16), loss8]))
    dW_in_s5 = _matmul_tn_slots(xn0, dproj0, "dw_s5_in")

    late = [dW_in_s5]
    late_t = _pair_add(late, list(_exchange_call(_plan_pair(late), "rs_pair_late")), "rs_add_late")
    owners = [("w_out", 1), ("w_mem_kv", 1), ("mla_w_in", 0), ("mla_w_uq", 0), ("mla_w_ukv", 0), ("w_out", 0),
              ("w_mem_kv", 0), ("s5_w_glu", 0)]
    flipped = ("mla_w_in", "mla_w_uq")

    def shard(d, n, i):
        return jnp.transpose(d[n][i]) if n in flipped else d[n][i]

    upd, (late_recv, ln0_gath) = _updates_call(
        early_recv, [shard(weights, n, i) for n, i in owners], [shard(m_in, n, i) for n, i in owners],
        [shard(v_in, n, i) for n, i in owners], "update_early",
        host=_combine(_plan_chips(late_t), _plan_all_gather([jnp.pad(dln0, ((0, 7), (0, 0)))])))
    owners.append(("s5_w_in", 0))
    upd.append(_sum_adamw(late_recv, s5_w_in[0], m_s5_w_in[0], v_s5_w_in[0], "update_s5_w_in"))
    grads, delta, new_m, new_v = {}, {}, {}, {}
    for n in _BIG:
        parts = [u for u, (o, _) in sorted(zip(upd, owners), key=lambda t: t[1][1]) if o == n]
        if n in flipped:
            grads[n], delta[n], new_m[n], new_v[n] = (jnp.transpose(parts[0][j])[None] for j in range(4))
        else:
            grads[n], delta[n], new_m[n], new_v[n] = (jnp.stack([p[j] for p in parts]) for j in range(4))

    gs, loss_sum = _small_sum(small_gath, loss_g, ln0_gath, "small_sum")
    loss = loss_sum[0, 0]
    for n, _ in _SMALL:
        shape = weights[n].shape
        if n == "mla_q_lora_norm":
            grads[n] = lax.dynamic_slice(_unpack_small(gs, n, (Q_LORA,)), (me * 64,), (64,)).reshape(shape)
        elif n == "mla_kv_lora_norm":
            grads[n] = lax.dynamic_slice(_unpack_small(gs, n, (KV_LORA,)), (me * 32,), (32,)).reshape(shape)
        else:
            grads[n] = _unpack_small(gs, n, shape)

    def own(n, a):
        if a.ndim == 4:
            a = jnp.transpose(a, (0, 2, 3, 1))
        elif a.ndim == 3:
            a = jnp.transpose(a, (0, 2, 1))
        return a.reshape(a.shape[1:]) if a.ndim >= 3 else a

    def back(n, a):
        shape = weights[n].shape
        if len(shape) == 4:
            return jnp.transpose(a.reshape((1,) + a.shape), (0, 3, 1, 2))
        if len(shape) == 3:
            return jnp.transpose(a.reshape((1,) + a.shape), (0, 2, 1))
        return a.reshape(shape)

    wide = ("s5_b_re", "s5_b_im", "s5_c_re", "s5_c_im")
    for names, nb, call in (([n for n, _ in _SMALL if n not in wide], 1, "update_small"), (wide, 4, "update_s5_bc")):
        res = _adamw_multi([own(n, weights[n]) for n in names], [own(n, grads[n]) for n in names],
                           [own(n, m_in[n]) for n in names], [own(n, v_in[n]) for n in names], call, nb)
        for n, (dl, m2, v2) in zip(names, res):
            delta[n], new_m[n], new_v[n] = back(n, dl), back(n, m2), back(n, v2)
    return (loss, dx0[None], *[grads[n] for n in _WEIGHTS], *[delta[n] for n in _WEIGHTS],
            *[new_m[n] for n in _WEIGHTS], *[new_v[n] for n in _WEIGHTS])
```

```python
import functools
import math

import numpy as np
import jax
import jax.numpy as jnp
from jax import lax
from jax.experimental import pallas as pl
from jax.experimental.pallas import tpu as pltpu

F32 = jnp.float32
BF16 = jnp.bfloat16
EPS = 1e-6
NEG = float(np.finfo(np.float32).min)
MESH = pl.DeviceIdType.MESH

N_DEV = 8
D_MODEL = 1024
MEM_LEN = 256
XQ = 512
PRIM = 1536
BRANCH = 2048
X_HEADS = 4
HD = 128
S5_G = 96
S5_P = 64
S5_C = 16
S5_GB = 8
S5_W = S5_GB * S5_P
MLA_H = 12
ROPE = 64
Q_LORA = 512
KV_LORA = 256
ROPE_THETA = 10000.0

ADAM_LR = 0.001
ADAM_B1 = 0.9
ADAM_B2 = 0.999
ADAM_EPS = 1e-08
ADAM_WD = 0.01
ADAM_STEP = 10

VMEM_LIMIT = 56 * 1024 * 1024


def _dot(a, b):
    return jnp.dot(a, b, preferred_element_type=F32)


def _dot_nt(a, b):
    return lax.dot_general(a, b, (((1,), (1,)), ((), ())), preferred_element_type=F32)


def _dot_tn(a, b):
    return lax.dot_general(a, b, (((0,), (0,)), ((), ())), preferred_element_type=F32)


@jax.custom_vjp
def _mm(a, b):
    return _dot(a.astype(BF16), b.astype(BF16))


def _mm_fwd(a, b):
    return _mm(a, b), (a, b)


def _mm_bwd(res, g):
    a, b = res
    gb = g.astype(BF16)
    return _dot_nt(gb, b.astype(BF16)).astype(a.dtype), _dot_tn(a.astype(BF16), gb).astype(b.dtype)


_mm.defvjp(_mm_fwd, _mm_bwd)


@jax.custom_vjp
def _mm_nt(a, b):
    return _dot_nt(a.astype(BF16), b.astype(BF16))


def _mm_nt_fwd(a, b):
    return _mm_nt(a, b), (a, b)


def _mm_nt_bwd(res, g):
    a, b = res
    gb = g.astype(BF16)
    return _dot(gb, b.astype(BF16)).astype(a.dtype), _dot_tn(gb, a.astype(BF16)).astype(b.dtype)


_mm_nt.defvjp(_mm_nt_fwd, _mm_nt_bwd)


@jax.custom_vjp
def _softmax(s):
    m = jnp.max(s, axis=-1, keepdims=True)
    e = jnp.exp(s - m)
    return e / jnp.sum(e, axis=-1, keepdims=True)


def _softmax_fwd(s):
    p = _softmax(s)
    return p, p


def _softmax_bwd(p, g):
    return (p * (g - jnp.sum(p * g, axis=-1, keepdims=True)),)


_softmax.defvjp(_softmax_fwd, _softmax_bwd)


def _rms(x, g, n):
    ms = jnp.sum(x * x, axis=-1, keepdims=True) * (1.0 / n)
    return x * lax.rsqrt(ms + EPS) * g


def _sigmoid(x):
    return 1.0 / (1.0 + jnp.exp(-x))


def _silu(x):
    return x * _sigmoid(x)


def _gelu(x):
    c = math.sqrt(2.0 / math.pi)
    return 0.5 * x * (1.0 + jnp.tanh(c * (x + 0.044715 * (x * x * x))))


@jax.custom_vjp
def _rot(x, c, s1, s2):
    return x * c + pltpu.roll(x, 96, 1) * s1 + pltpu.roll(x, 32, 1) * s2


def _rot_fwd(x, c, s1, s2):
    return _rot(x, c, s1, s2), (c, s1, s2)


def _rot_bwd(res, g):
    c, s1, s2 = res
    dx = g * c + pltpu.roll(g * s1, 32, 1) + pltpu.roll(g * s2, 96, 1)
    return dx, jnp.zeros_like(c), jnp.zeros_like(s1), jnp.zeros_like(s2)


_rot.defvjp(_rot_fwd, _rot_bwd)


def _mem_attn(xq, k, v, gq):
    outs = []
    for h in range(X_HEADS):
        sl = slice(HD * h, HD * (h + 1))
        q = _rms(xq[:, sl], gq, HD)
        p = _softmax(_mm_nt(q, k[:, sl]) * (HD ** -0.5))
        outs.append(_mm(p, v[:, sl]))
    return jnp.concatenate(outs, axis=-1)


def _merge(mix, xq, gate, k, v, gq):
    return jnp.concatenate([mix, _mem_attn(xq, k, v, gq)], axis=-1) * _silu(gate)


def _q_chunks(q):
    return ([q[:, HD * h:HD * (h + 1)] for h in range(MLA_H)],
            [q[:, PRIM + HD * h:PRIM + HD * (h + 1)] for h in range(MLA_H)])


def _q_post(nope, rope, gqn, gqr, c, s1, s2):
    pieces = []
    for qn, qr in zip(nope, rope):
        pieces.append(_rms(qn, gqn, HD))
        pieces.append(_rot(_rms(qr, gqr, ROPE), c, s1, s2))
    return jnp.concatenate(pieces, axis=-1)


def _kv_chunks(kv):
    return ([kv[:, 2 * HD * h:2 * HD * h + HD] for h in range(MLA_H)],
            [kv[:, 2 * HD * h + HD:2 * HD * (h + 1)] for h in range(MLA_H)])


def _kv_post(kn, vals, krp, gkn, gkr, c, s1, s2):
    kr = _rot(_rms(krp, gkr, ROPE), c, s1, s2)
    pieces = []
    for k in kn:
        pieces.append(_rms(k, gkn, HD))
        pieces.append(kr)
    return jnp.concatenate(pieces, axis=-1), jnp.concatenate(vals, axis=-1)


def _rowwise(name, fn, ins, outs, nblk, host=None):
    n_in = len(ins)

    def spec(kind, shape):
        if kind == 'r':
            return pl.BlockSpec((shape[0] // nblk, shape[1]), lambda i: (i, 0))
        if kind == 't':
            return pl.BlockSpec((shape[0], shape[1] // nblk), lambda i: (0, i))
        zeros = (0,) * len(shape)
        return pl.BlockSpec(tuple(shape), lambda i: zeros)

    def body(*refs):
        i = pl.program_id(0)
        res = fn(*[r[...] for r in refs[:n_in]])
        for (kind, _, _), ref, val in zip(outs, refs[n_in:], res):
            if kind == 'a':
                @pl.when(i == 0)
                def _():
                    ref[...] = jnp.zeros_like(ref)
                ref[...] += val.astype(ref.dtype)
            elif kind == 't':
                ref[...] = val.astype(F32).T.astype(ref.dtype)
            else:
                ref[...] = val.astype(ref.dtype)

    res, hosted = _hosting_call(
        body, name, nblk, host, [a for _, a in ins], [spec(k, a.shape) for k, a in ins],
        [jax.ShapeDtypeStruct(tuple(s), d) for _, s, d in outs], [spec(k, s) for k, s, _ in outs], [])
    return res if host is None else (res, hosted)


def _matmul_tn(at, g, name, out_dtype=BF16):
    K, L = at.shape
    N = g.shape[1]
    tn = next(t for t in (512, 384, 256, 128) if N % t == 0)

    def body(a_ref, g_ref, o_ref):
        o_ref[...] = _dot(a_ref[...], g_ref[...]).astype(o_ref.dtype)

    return pl.pallas_call(
        body, name=name, grid=(N // tn,),
        in_specs=[pl.BlockSpec((K, L), lambda n: (0, 0)), pl.BlockSpec((L, tn), lambda n: (0, n))],
        out_specs=pl.BlockSpec((K, tn), lambda n: (0, n)),
        out_shape=jax.ShapeDtypeStruct((K, N), out_dtype),
        compiler_params=pltpu.CompilerParams(dimension_semantics=("arbitrary",), vmem_limit_bytes=VMEM_LIMIT),
    )(at, g)


def _matmul_tn_slots(at, g, name, host=None):
    K, L = at.shape
    n = g.shape[1] // N_DEV

    def body(a_ref, g_ref, o_ref):
        o_ref[...] = _dot(a_ref[...], g_ref[...]).astype(o_ref.dtype)

    res, hosted = _hosting_call(
        body, name, N_DEV, host, [at, g],
        [pl.BlockSpec((K, L), lambda d: (0, 0)), pl.BlockSpec((L, n), lambda d: (0, d))],
        [jax.ShapeDtypeStruct((N_DEV, K, n), BF16)], [pl.BlockSpec((None, K, n), lambda d: (d, 0, 0))], [])
    return res[0] if host is None else (res[0], hosted)


def _mm_slots(a16, w):
    return jnp.concatenate([_dot(a16, w[d]) for d in range(N_DEV)], axis=-1)


def _mm_slots_nt(g16, w):
    n = w.shape[2]
    out = _dot_nt(g16[:, 0:n], w[0])
    for d in range(1, N_DEV):
        out = out + _dot_nt(g16[:, d * n:(d + 1) * n], w[d])
    return out


class _Exchange:
    def __init__(self, ins, outs, scratch, start, finish):
        self.ins, self.outs, self.scratch, self.start, self.finish = ins, outs, scratch, start, finish


def _xyc():
    return lax.axis_index("x"), lax.axis_index("y"), lax.axis_index("c")


def _plan_all_gather(xs):
    n = len(xs)

    def build(x_refs, out_refs, sems):
        send_sems, recv_sems, local_sems = sems
        x, y, c = _xyc()

        def copies(k, block, to, own=False):
            slot = 4 * block[0] + 2 * block[1] + block[2]
            return [pltpu.make_async_remote_copy(
                src_ref=x_refs[a] if own else out_refs[a].at[slot], dst_ref=out_refs[a].at[slot],
                send_sem=send_sems.at[k * n + a], recv_sem=recv_sems.at[k * n + a], device_id=to,
                device_id_type=MESH) for a in range(n)]

        mine = [pltpu.make_async_copy(x_refs[a], out_refs[a].at[4 * x + 2 * y + c], local_sems.at[a])
                for a in range(n)]
        return copies, mine, (x, y, c), [(1 - x, y), (x, 1 - y), (1 - x, 1 - y)]

    def first_copies(copies, me, chips):
        x, y, c = me
        first = copies(0, me, (x, y, 1 - c), own=True)
        for j, chip in enumerate(chips):
            first += copies(1 + j, me, (*chip, c), own=True)
        return first

    def start(x_refs, out_refs, sems):
        copies, mine, me, chips = build(x_refs, out_refs, sems)
        for cp in mine + first_copies(copies, me, chips):
            cp.start()

    def finish(x_refs, out_refs, sems):
        copies, mine, me, chips = build(x_refs, out_refs, sems)
        x, y, c = me
        passed = []
        for j, chip in enumerate(chips):
            for cp in copies(1 + j, (*chip, c), me):
                cp.wait_recv()
            fwd = copies(4 + j, (*chip, c), (x, y, 1 - c))
            for cp in fwd:
                cp.start()
            passed += fwd
        for cp in copies(0, (x, y, 1 - c), me):
            cp.wait_recv()
        for j, chip in enumerate(chips):
            for cp in copies(4 + j, (*chip, 1 - c), me):
                cp.wait_recv()
        for cp in first_copies(copies, me, chips) + passed:
            cp.wait_send()
        for cp in mine:
            cp.wait()

    return _Exchange(list(xs), [jax.ShapeDtypeStruct((N_DEV,) + a.shape, a.dtype) for a in xs],
                     [pltpu.SemaphoreType.DMA((7 * n,)), pltpu.SemaphoreType.DMA((7 * n,)),
                      pltpu.SemaphoreType.DMA((n,))], start, finish)


_CHIPS = ((0, 0), (0, 1), (1, 0), (1, 1))


def _plan_pair(sends):
    n = len(sends)

    def build(s_refs, o_refs, sems):
        send_sems, recv_sems = sems
        x, y, c = _xyc()
        return [pltpu.make_async_remote_copy(
            src_ref=s_refs[a].at[4 * px + 2 * py + 1 - c], dst_ref=o_refs[a].at[j],
            send_sem=send_sems.at[j * n + a], recv_sem=recv_sems.at[j * n + a], device_id=(x, y, 1 - c),
            device_id_type=MESH) for j, (px, py) in enumerate(_CHIPS) for a in range(n)]

    def start(s_refs, o_refs, sems):
        for cp in build(s_refs, o_refs, sems):
            cp.start()

    def finish(s_refs, o_refs, sems):
        for cp in build(s_refs, o_refs, sems):
            cp.wait_recv()
            cp.wait_send()

    return _Exchange(list(sends), [jax.ShapeDtypeStruct((4,) + a.shape[1:], a.dtype) for a in sends],
                     [pltpu.SemaphoreType.DMA((4 * n,)), pltpu.SemaphoreType.DMA((4 * n,))], start, finish)


def _plan_chips(ts):
    n = len(ts)
    flips = ((1, 0), (0, 1), (1, 1))

    def build(t_refs, o_refs, sems):
        send_sems, recv_sems, local_sems = sems
        x, y, c = _xyc()
        mine = 2 * x + y
        local = [pltpu.make_async_copy(t_refs[a].at[mine], o_refs[a].at[mine], local_sems.at[a]) for a in range(n)]
        remote = []
        for k, (fx, fy) in enumerate(flips):
            px = 1 - x if fx else x
            py = 1 - y if fy else y
            remote += [pltpu.make_async_remote_copy(
                src_ref=t_refs[a].at[2 * px + py], dst_ref=o_refs[a].at[mine],
                send_sem=send_sems.at[k * n + a], recv_sem=recv_sems.at[k * n + a], device_id=(px, py, c),
                device_id_type=MESH) for a in range(n)]
        return local, remote

    def start(t_refs, o_refs, sems):
        local, remote = build(t_refs, o_refs, sems)
        for cp in local + remote:
            cp.start()

    def finish(t_refs, o_refs, sems):
        local, remote = build(t_refs, o_refs, sems)
        for cp in remote:
            cp.wait_recv()
        for cp in remote:
            cp.wait_send()
        for cp in local:
            cp.wait()

    return _Exchange(list(ts), [jax.ShapeDtypeStruct(a.shape, a.dtype) for a in ts],
                     [pltpu.SemaphoreType.DMA((3 * n,)), pltpu.SemaphoreType.DMA((3 * n,)),
                      pltpu.SemaphoreType.DMA((n,))], start, finish)


def _combine(*plans):
    def parts(refs, attr):
        out, at = [], 0
        for p in plans:
            n = len(getattr(p, attr))
            out.append(refs[at:at + n])
            at += n
        return out

    def run(half):
        def go(ins, outs, sems):
            for p, a, o, s in zip(plans, parts(ins, "ins"), parts(outs, "outs"), parts(sems, "scratch")):
                getattr(p, half)(a, o, s)
        return go

    return _Exchange(sum((p.ins for p in plans), []), sum((p.outs for p in plans), []),
                     sum((p.scratch for p in plans), []), run("start"), run("finish"))


def _exchange_call(plan, name):
    n = len(plan.ins)

    def body(*refs):
        ins, outs, sems = refs[:n], refs[n:2 * n], refs[2 * n:]
        plan.start(ins, outs, sems)
        plan.finish(ins, outs, sems)

    return pl.pallas_call(
        body, name=name, out_shape=plan.outs,
        in_specs=[pl.BlockSpec(memory_space=pl.ANY)] * n, out_specs=[pl.BlockSpec(memory_space=pl.ANY)] * n,
        scratch_shapes=plan.scratch,
    )(*plan.ins)


def _slab_spec(lead, rows, cols, nb):
    if rows % (nb * 16) == 0:
        return pl.BlockSpec((lead, rows // nb, cols), lambda i: (0, i, 0))
    if cols % (nb * 128) == 0:
        return pl.BlockSpec((lead, rows, cols // nb), lambda i: (0, 0, i))
    return pl.BlockSpec((lead, rows, cols), lambda i: (0, 0, 0))


def _slab_spec2(rows, cols, nb):
    if rows % (nb * 16) == 0:
        return pl.BlockSpec((rows // nb, cols), lambda i: (i, 0))
    if cols % (nb * 128) == 0:
        return pl.BlockSpec((rows, cols // nb), lambda i: (0, i))
    return pl.BlockSpec((rows, cols), lambda i: (0, 0))


def _cast_call(arrays, name, host=None):
    n = len(arrays)
    nb = 8

    def body(*refs):
        for a in range(n):
            refs[n + a][...] = refs[a][...].astype(BF16)

    specs = [_slab_spec2(x.shape[0], x.shape[1], nb) for x in arrays]
    return _hosting_call(body, name, nb, host, list(arrays), specs,
                         [jax.ShapeDtypeStruct(x.shape, BF16) for x in arrays], specs, [])


def _pair_add(sends, fromsib, name):
    n = len(sends)
    nb = 8

    def body(*refs):
        c = lax.axis_index("c")
        for a in range(n):
            s_ref, f_ref, t_ref = refs[a], refs[n + a], refs[2 * n + a]
            for j in range(4):
                t_ref[j] = (s_ref[2 * j + c].astype(F32) + f_ref[j].astype(F32)).astype(t_ref.dtype)

    def spec(a, lead):
        return _slab_spec(lead, a.shape[1], a.shape[2], nb)

    return pl.pallas_call(
        body, name=name, grid=(nb,),
        in_specs=[spec(a, N_DEV) for a in sends] + [spec(a, 4) for a in fromsib],
        out_specs=[spec(a, 4) for a in fromsib],
        out_shape=[jax.ShapeDtypeStruct(a.shape, a.dtype) for a in fromsib],
        compiler_params=pltpu.CompilerParams(dimension_semantics=("arbitrary",), vmem_limit_bytes=VMEM_LIMIT),
    )(*sends, *fromsib)


def _adamw_vals(w, g, m, v):
    m2 = ADAM_B1 * m + (1.0 - ADAM_B1) * g
    v2 = ADAM_B2 * v + (1.0 - ADAM_B2) * (g * g)
    m_hat = m2 / (1.0 - ADAM_B1 ** ADAM_STEP)
    v_hat = v2 / (1.0 - ADAM_B2 ** ADAM_STEP)
    delta = -ADAM_LR * (m_hat / (jnp.sqrt(v_hat) + ADAM_EPS) + ADAM_WD * w)
    return delta, m2, v2


def _sum_adamw(recv, w, m, v, name):
    R, C = w.shape
    ns = recv.shape[0]
    br = next((t for t in (256, 128, 64, 32, 16) if R % t == 0), R)

    def body(r_ref, w_ref, m_ref, v_ref, g_ref, d_ref, m2_ref, v2_ref):
        g = r_ref[0].astype(F32)
        for d in range(1, ns):
            g = g + r_ref[d].astype(F32)
        dl, m2, v2 = _adamw_vals(w_ref[...], g, m_ref[...], v_ref[...])
        g_ref[...] = g
        d_ref[...] = dl
        m2_ref[...] = m2
        v2_ref[...] = v2

    spec = pl.BlockSpec((br, C), lambda i: (i, 0))
    return pl.pallas_call(
        body, name=name, grid=(R // br,),
        in_specs=[pl.BlockSpec((ns, br, C), lambda i: (0, i, 0)), spec, spec, spec], out_specs=[spec] * 4,
        out_shape=[jax.ShapeDtypeStruct((R, C), F32)] * 4,
        compiler_params=pltpu.CompilerParams(dimension_semantics=("arbitrary",)),
    )(recv, w, m, v)


def _updates_call(recvs, ws, ms, vs, name, host=None):
    n = len(recvs)
    nb = 8

    def body(*refs):
        for a in range(n):
            r_ref, w_ref, m_ref, v_ref = refs[a], refs[n + a], refs[2 * n + a], refs[3 * n + a]
            g_ref, d_ref, m2_ref, v2_ref = refs[4 * n + 4 * a:4 * n + 4 * a + 4]
            g = r_ref[0].astype(F32)
            for d in range(1, r_ref.shape[0]):
                g = g + r_ref[d].astype(F32)
            dl, m2, v2 = _adamw_vals(w_ref[...], g, m_ref[...], v_ref[...])
            g_ref[...] = g
            d_ref[...] = dl
            m2_ref[...] = m2
            v2_ref[...] = v2

    def spec3(r):
        return _slab_spec(r.shape[0], r.shape[1], r.shape[2], nb)

    def spec2(w):
        return _slab_spec2(w.shape[0], w.shape[1], nb)

    res, hosted = _hosting_call(
        body, name, nb, host, list(recvs) + list(ws) + list(ms) + list(vs),
        [spec3(r) for r in recvs] + [spec2(w) for w in ws] * 3,
        [jax.ShapeDtypeStruct(w.shape, F32) for w in ws for _ in range(4)],
        [spec2(w) for w in ws for _ in range(4)], [])
    return [res[4 * a:4 * a + 4] for a in range(n)], hosted


def _small_sum(gath, loss_g, row0_g, name):
    _, R, C = gath.shape
    br = R // 3

    def body(g_ref, l_ref, r_ref, go_ref, lo_ref):
        g = g_ref[0].astype(F32)
        lsum = l_ref[0]
        for d in range(1, N_DEV):
            g = g + g_ref[d].astype(F32)
            lsum = lsum + l_ref[d]
        go_ref[...] = g
        lo_ref[...] = lsum

        @pl.when(pl.program_id(0) == 0)
        def _():
            row0 = r_ref[0]
            for d in range(1, N_DEV):
                row0 = row0 + r_ref[d]
            go_ref[0:8, :] = go_ref[0:8, :] + jnp.where(lax.broadcasted_iota(jnp.int32, row0.shape, 0) == 0, row0, 0.0)

    return pl.pallas_call(
        body, name=name, grid=(R // br,),
        in_specs=[pl.BlockSpec((N_DEV, br, C), lambda i: (0, i, 0)),
                  pl.BlockSpec((N_DEV, 8, HD), lambda i: (0, 0, 0)), pl.BlockSpec((N_DEV, 8, C), lambda i: (0, 0, 0))],
        out_specs=[pl.BlockSpec((br, C), lambda i: (i, 0)), pl.BlockSpec((8, HD), lambda i: (0, 0))],
        out_shape=[jax.ShapeDtypeStruct((R, C), F32), jax.ShapeDtypeStruct((8, HD), F32)],
        compiler_params=pltpu.CompilerParams(dimension_semantics=("arbitrary",)),
    )(gath, loss_g, row0_g)


def _adamw_multi(ws, gs, ms, vs, name, nblk=1):
    n = len(ws)

    def body(*refs):
        for a in range(n):
            dl, m2, v2 = _adamw_vals(refs[a][...], refs[n + a][...], refs[2 * n + a][...], refs[3 * n + a][...])
            refs[4 * n + 3 * a][...] = dl
            refs[4 * n + 3 * a + 1][...] = m2
            refs[4 * n + 3 * a + 2][...] = v2

    def spec(x):
        rest = (0,) * (x.ndim - 1)
        return pl.BlockSpec((x.shape[0] // nblk,) + tuple(x.shape[1:]), lambda i: (i,) + rest)

    res = pl.pallas_call(
        body, name=name, grid=(nblk,),
        in_specs=[spec(w) for w in ws] * 4, out_specs=[spec(w) for w in ws for _ in range(3)],
        out_shape=[jax.ShapeDtypeStruct(w.shape, F32) for w in ws for _ in range(3)],
        compiler_params=pltpu.CompilerParams(dimension_semantics=("arbitrary",), vmem_limit_bytes=VMEM_LIMIT),
    )(*ws, *gs, *ms, *vs)
    return [res[3 * a:3 * a + 3] for a in range(n)]


def _s5_param_fn(lr, li, ls, btr, bti):
    step = jnp.exp(ls)
    er = jnp.exp(lr * step)
    ang = li * step
    ar = er * jnp.cos(ang)
    ai = er * jnp.sin(ang)
    nr = ar - 1.0
    den = lr * lr + li * li
    fr = (nr * lr + ai * li) / den
    fi = (ai * lr - nr * li) / den
    return ar, ai, fr * btr - fi * bti, fr * bti + fi * btr


def _s5_params(lr, li, ls, btr, bti, cre, cim):
    nb = S5_G // S5_GB
    GC = S5_GB * S5_C
    expand = jnp.asarray(np.tile(np.eye(S5_P, dtype=np.float32), (1, S5_GB)), BF16)
    own = jnp.asarray((np.arange(GC)[:, None] // S5_C == np.arange(S5_W)[None, :] // S5_P).astype(np.float32))

    def body(lr_ref, li_ref, ls_ref, br_ref, bi_ref, cr_ref, ci_ref, e_ref, own_ref, ar_ref, ai_ref, bm_ref, cm_ref):
        ar, ai, bbr, bbi = _s5_param_fn(lr_ref[...], li_ref[...], ls_ref[...], br_ref[...], bi_ref[...])
        ar_ref[...] = ar
        ai_ref[...] = ai

        def plane(x, n):
            rows = x[n * S5_GB:(n + 1) * S5_GB].reshape(GC, S5_P).astype(BF16)
            return _dot(rows, e_ref[...]) * own_ref[...]

        for n in range(nb):
            bm_ref[n] = jnp.concatenate([plane(bbr, n), plane(bbi, n)], axis=-1).astype(BF16)
            cm_ref[n] = jnp.concatenate([plane(cr_ref[...], n), -plane(ci_ref[...], n)], axis=-1).astype(BF16)

    sd = jax.ShapeDtypeStruct
    return pl.pallas_call(
        body, name="s5_params",
        out_shape=[sd(lr.shape, F32), sd(lr.shape, F32), sd((nb, GC, 2 * S5_W), BF16), sd((nb, GC, 2 * S5_W), BF16)],
        compiler_params=pltpu.CompilerParams(vmem_limit_bytes=VMEM_LIMIT),
    )(lr, li, ls, btr, bti, cre, cim, expand, own)


def _s5_params_bwd(lr, li, ls, btr, bti, dar, dai, dbbr, dbbi):
    def body(lr_ref, li_ref, ls_ref, br_ref, bi_ref, dar_ref, dai_ref, dbbr_ref, dbbi_ref,
             dlr_ref, dli_ref, dls_ref, dbr_ref, dbi_ref):
        _, vjp = jax.vjp(_s5_param_fn, lr_ref[...], li_ref[...], ls_ref[...], br_ref[...], bi_ref[...])
        dlr, dli, dls, dbr, dbi = vjp((dar_ref[...], dai_ref[...], dbbr_ref[...], dbbi_ref[...]))
        dlr_ref[...] = dlr
        dli_ref[...] = dli
        dls_ref[...] = dls
        dbr_ref[...] = dbr
        dbi_ref[...] = dbi

    sd = jax.ShapeDtypeStruct
    return pl.pallas_call(
        body, name="s5_params_bwd",
        out_shape=[sd(lr.shape, F32), sd(lr.shape, F32), sd(ls.shape, F32), sd(btr.shape, F32), sd(btr.shape, F32)],
    )(lr, li, ls, btr, bti, dar, dai, dbbr, dbbi)


def _cpow(ar, ai, n):
    assert n & (n - 1) == 0
    while n > 1:
        ar, ai = ar * ar - ai * ai, 2.0 * ar * ai
        n //= 2
    return ar, ai


def _scan(st, cr, ci, init, nk, reverse, store, prev=None):
    W = S5_W

    def advance(k, sr, si):
        rows = pl.ds(k * 8 if isinstance(k, int) else pl.multiple_of(k * 8, 8), 8)
        nsr = cr * sr - ci * si + st[rows, 0:W]
        nsi = cr * si + ci * sr + st[rows, W:2 * W]
        if store:
            st[rows, 0:W] = nsr
            st[rows, W:2 * W] = nsi
        return nsr, nsi

    if prev is None:
        return lax.fori_loop(0, nk, lambda j, c: advance(nk - 1 - j if reverse else j, c[0], c[1]), init, unroll=2)
    assert reverse

    def step(j, carry):
        k = nk - 1 - j
        nsr, nsi = advance(k, carry[0], carry[1])
        prows = pl.ds(pl.multiple_of((k - 1) * 8, 8), 8)
        pr = prev[prows, 0:W]
        pi = prev[prows, W:2 * W]
        return nsr, nsi, carry[2] + nsr * pr + nsi * pi, carry[3] + nsi * pr - nsr * pi

    carry = lax.fori_loop(0, nk - 1, step, init, unroll=2)
    nsr, nsi = advance(0, carry[0], carry[1])
    return nsr, nsi, carry[2], carry[3]


def _chain(fin, fr, fi, pr, pi, reverse):
    W = S5_W
    fin[:, 0:W] = fr
    fin[:, W:2 * W] = fi
    rowid = lax.broadcasted_iota(jnp.int32, (8, W), 0)
    cr = jnp.zeros((1, W), F32)
    ci = jnp.zeros((1, W), F32)
    init_r = jnp.zeros((8, W), F32)
    init_i = jnp.zeros((8, W), F32)
    for s in (range(7, -1, -1) if reverse else range(8)):
        init_r = jnp.where(rowid == s, cr, init_r)
        init_i = jnp.where(rowid == s, ci, init_i)
        lr = fin[s:s + 1, 0:W]
        li = fin[s:s + 1, W:2 * W]
        cr, ci = lr + pr * cr - pi * ci, li + pr * ci + pi * cr
    return init_r, init_i


def _full_scan(st, fin, ar, ai, nk, reverse, prev=None, carry_in=None, carry_out=None):
    W = S5_W
    cr = jnp.broadcast_to(ar, (8, W))
    ci = jnp.broadcast_to(-ai if reverse else ai, (8, W))
    z = jnp.zeros((8, W), F32)
    if carry_in is None:
        fr, fi = _scan(st, cr, ci, (z, z), nk, reverse, store=False)
        pr, pi = _cpow(ar, -ai if reverse else ai, nk)
        init = _chain(fin, fr, fi, pr, pi, reverse)
    else:
        init = (carry_in[:, 0:W], carry_in[:, W:2 * W])
    if carry_out is not None:
        carry_out[:, 0:W] = init[0]
        carry_out[:, W:2 * W] = init[1]
    if prev is None:
        return _scan(st, cr, ci, init, nk, reverse, store=True)
    return _scan(st, cr, ci, init + (z, z), nk, reverse, store=True, prev=prev)


def _s5_specs(L):
    W2 = 2 * S5_W
    GC = S5_GB * S5_C
    col = pl.BlockSpec((L, GC), lambda g: (0, g))
    vec = pl.BlockSpec((1, GC), lambda g: (0, g))
    avec = pl.BlockSpec((1, S5_W), lambda g: (0, g))
    bmat = pl.BlockSpec((None, GC, W2), lambda g: (g, 0, 0))
    cmat = pl.BlockSpec((None, W2, GC), lambda g: (g, 0, 0))
    return col, vec, avec, bmat, cmat


def _interleave(dst, src, nk):
    for s in range(8):
        dst[pl.ds(s, nk, stride=8), :] = src[s * nk:(s + 1) * nk, :]


def _deinterleave(dst, src, nk):
    for s in range(8):
        dst[s * nk:(s + 1) * nk, :] = src[pl.ds(s, nk, stride=8), :].astype(dst.dtype)


def _hosting_call(body, name, nsteps, host, ins, in_specs, outs, out_specs, scratch):
    grid = (nsteps,) if isinstance(nsteps, int) else tuple(nsteps)
    params = pltpu.CompilerParams(dimension_semantics=("arbitrary",) * len(grid), vmem_limit_bytes=VMEM_LIMIT)
    if host is None:
        res = pl.pallas_call(
            body, name=name, grid=grid, in_specs=in_specs, out_specs=out_specs, out_shape=outs,
            scratch_shapes=scratch, compiler_params=params,
        )(*ins)
        return list(res), []
    n_in, n_out, n_sc = len(ins), len(outs), len(scratch)
    h_in, h_out = len(host.ins), len(host.outs)

    def hosted(*refs):
        a = refs[:n_in]
        ha = refs[n_in:n_in + h_in]
        o = refs[n_in + h_in:n_in + h_in + n_out]
        ho = refs[n_in + h_in + n_out:n_in + h_in + n_out + h_out]
        sc = refs[n_in + h_in + n_out + h_out:n_in + h_in + n_out + h_out + n_sc]
        hs = refs[n_in + h_in + n_out + h_out + n_sc:]
        first = functools.reduce(jnp.logical_and, [pl.program_id(i) == 0 for i in range(len(grid))])
        last = functools.reduce(jnp.logical_and, [pl.program_id(i) == g - 1 for i, g in enumerate(grid)])

        @pl.when(first)
        def _():
            host.start(ha, ho, hs)

        body(*a, *o, *sc)

        @pl.when(last)
        def _():
            host.finish(ha, ho, hs)

    hbm = pl.BlockSpec(memory_space=pl.ANY)
    res = pl.pallas_call(
        hosted, name=name, grid=grid,
        in_specs=list(in_specs) + [hbm] * h_in, out_specs=list(out_specs) + [hbm] * h_out,
        out_shape=list(outs) + list(host.outs), scratch_shapes=list(scratch) + list(host.scratch),
        compiler_params=params,
    )(*ins, *host.ins)
    return list(res[:n_out]), list(res[n_out:])


def _s5_fwd(u, bm, cm, ar, ai, dvec, host=None):
    L = u.shape[0]
    nk = L // 8
    GC = S5_GB * S5_C
    nb = S5_G // S5_GB
    col, vec, avec, bmat, cmat = _s5_specs(L)

    def body(u_ref, b_ref, c_ref, ar_ref, ai_ref, d_ref, y_ref, carry_ref, st, fin, ui, yi):
        _interleave(ui, u_ref, nk)
        for r in range(8):
            rows = slice(r * nk, (r + 1) * nk)
            st[rows, :] = _dot(ui[rows, :].astype(BF16), b_ref[...])
        _full_scan(st, fin, ar_ref[...], ai_ref[...], nk, reverse=False, carry_out=carry_ref)
        for r in range(8):
            rows = slice(r * nk, (r + 1) * nk)
            yi[rows, :] = _dot_nt(st[rows, :].astype(BF16), c_ref[...]) + d_ref[...] * ui[rows, :]
        _deinterleave(y_ref, yi, nk)

    return _hosting_call(
        body, "s5_fwd", nb, host,
        [u, bm, cm, ar, ai, dvec], [col, bmat, bmat, avec, avec, vec],
        [jax.ShapeDtypeStruct(u.shape, F32), jax.ShapeDtypeStruct((nb * 8, 2 * S5_W), F32)],
        [col, pl.BlockSpec((8, 2 * S5_W), lambda g: (g, 0))],
        [pltpu.VMEM((L, 2 * S5_W), F32), pltpu.VMEM((8, 2 * S5_W), F32), pltpu.VMEM((L, GC), F32),
         pltpu.VMEM((L, GC), F32)])


def _s5_bwd(u, dy, carry, bm, cm, ar, ai, dvec, mask, rmat, host=None):
    L = u.shape[0]
    nk = L // 8
    W = S5_W
    GC = S5_GB * S5_C
    col, vec, avec, bmat, cmat = _s5_specs(L)
    hi = lax.Precision.HIGHEST

    def body(u_ref, dy_ref, carry_ref, b_ref, ct_ref, ar_ref, ai_ref, d_ref, mask_ref, r_ref,
             du_ref, db_ref, dc_ref, dd_ref, dar_ref, dai_ref, sa, sb, fin, ui, dyi, dui):
        ar = ar_ref[...]
        ai = ai_ref[...]
        _interleave(ui, u_ref, nk)
        _interleave(dyi, dy_ref, nk)
        for r in range(8):
            rows = slice(r * nk, (r + 1) * nk)
            sa[rows, :] = _dot(ui[rows, :].astype(BF16), b_ref[...])
            sb[rows, :] = _dot(dyi[rows, :].astype(BF16), ct_ref[...])
        _full_scan(sa, fin, ar, ai, nk, reverse=False, carry_in=carry_ref)
        gr, gi, accr, acci = _full_scan(sb, fin, ar, ai, nk, reverse=True, prev=sa)
        rowid = lax.broadcasted_iota(jnp.int32, (8, W), 0)
        last = pl.ds((nk - 1) * 8, 8)
        pr = jnp.where(rowid == 0, 0.0, pltpu.roll(sa[last, 0:W], 1, 0))
        pi = jnp.where(rowid == 0, 0.0, pltpu.roll(sa[last, W:2 * W], 1, 0))
        accr = accr + gr * pr + gi * pi
        acci = acci + gi * pr - gr * pi
        dar_ref[...] = jnp.sum(accr, axis=0, keepdims=True)
        dai_ref[...] = jnp.sum(acci, axis=0, keepdims=True)
        dbf = jnp.zeros((GC, 2 * W), F32)
        dcf = jnp.zeros((GC, 2 * W), F32)
        dd = jnp.zeros((1, GC), F32)
        for r in range(8):
            rows = slice(r * nk, (r + 1) * nk)
            ub = ui[rows, :]
            dyb = dyi[rows, :]
            gb = sb[rows, :].astype(BF16)
            dui[rows, :] = _dot_nt(gb, b_ref[...]) + d_ref[...] * dyb
            dbf = dbf + _dot_tn(ub.astype(BF16), gb)
            dcf = dcf + _dot_tn(dyb.astype(BF16), sa[rows, :].astype(BF16))
            dd = dd + jnp.sum(dyb * ub, axis=0, keepdims=True)
        db_ref[...] = jnp.dot(dbf * mask_ref[...], r_ref[...], precision=hi, preferred_element_type=F32)
        dc_ref[...] = jnp.dot(dcf * mask_ref[...], r_ref[...], precision=hi, preferred_element_type=F32)
        dd_ref[...] = dd
        _deinterleave(du_ref, dui, nk)

    cmp_spec = pl.BlockSpec((GC, 2 * S5_P), lambda g: (g, 0))
    whole = lambda shape: pl.BlockSpec(shape, lambda g: (0, 0))
    sd = jax.ShapeDtypeStruct
    return _hosting_call(
        body, "s5_bwd", S5_G // S5_GB, host,
        [u, dy, carry, bm, cm, ar, ai, dvec, mask, rmat],
        [col, col, pl.BlockSpec((8, 2 * W), lambda g: (g, 0)), bmat, bmat, avec, avec, vec, whole(mask.shape),
         whole(rmat.shape)],
        [sd(u.shape, BF16), sd((S5_G * S5_C, 2 * S5_P), F32), sd((S5_G * S5_C, 2 * S5_P), F32),
         sd((1, PRIM), F32), sd((1, S5_G * S5_P), F32), sd((1, S5_G * S5_P), F32)],
        [col, cmp_spec, cmp_spec, vec, avec, avec],
        [pltpu.VMEM((L, 2 * W), F32), pltpu.VMEM((L, 2 * W), F32), pltpu.VMEM((8, 2 * W), F32),
         pltpu.VMEM((L, GC), F32), pltpu.VMEM((L, GC), F32), pltpu.VMEM((L, GC), F32)])


def _s5_compact_consts():
    g_row = np.arange(S5_GB * S5_C) // S5_C
    col = np.arange(2 * S5_W)
    g_col = (col % S5_W) // S5_P
    mask = (g_row[:, None] == g_col[None, :]).astype(np.float32)
    tgt = (col // S5_W) * S5_P + col % S5_P
    rmat = (tgt[:, None] == np.arange(2 * S5_P)[None, :]).astype(np.float32)
    return jnp.asarray(mask), jnp.asarray(rmat)


def _attn_scores(q_ref, k_ref, qb, bq, scale):
    ext = (qb + 1) * bq
    s = _dot_nt(q_ref[qb * bq:ext, :], k_ref[0:ext, :]) * scale
    qpos = lax.broadcasted_iota(jnp.int32, (bq, bq), 0)
    kpos = lax.broadcasted_iota(jnp.int32, (bq, bq), 1)
    diag = jnp.where(kpos <= qpos, s[:, ext - bq:], NEG)
    return diag if qb == 0 else jnp.concatenate([s[:, :ext - bq], diag], axis=-1)


def _attn_fwd(qp, kp, v, scale):
    L = qp.shape[0]
    bq = min(256, L)

    def body(q_ref, k_ref, v_ref, o_ref, lse_ref):
        for qb in range(L // bq):
            rows = slice(qb * bq, (qb + 1) * bq)
            s = _attn_scores(q_ref, k_ref, qb, bq, scale)
            m = jnp.max(s, axis=-1, keepdims=True)
            e = jnp.exp(s - m)
            l = jnp.sum(e, axis=-1, keepdims=True)
            o_ref[rows, :] = _dot(e.astype(BF16), v_ref[0:(qb + 1) * bq, :]) / l
            lse_ref[rows, :] = jnp.broadcast_to(m + jnp.log(l), (bq, HD))

    blk = pl.BlockSpec((L, HD), lambda h: (0, h))
    wide = pl.BlockSpec((L, 2 * HD), lambda h: (0, h))
    return pl.pallas_call(
        body, name="mla_attn_fwd", grid=(MLA_H,),
        in_specs=[wide, wide, blk], out_specs=[blk, blk],
        out_shape=[jax.ShapeDtypeStruct((L, MLA_H * HD), F32)] * 2,
        compiler_params=pltpu.CompilerParams(dimension_semantics=("arbitrary",), vmem_limit_bytes=VMEM_LIMIT),
    )(qp, kp, v)


def _attn_bwd(qp, kp, v, o, lse, do, scale):
    L = qp.shape[0]
    bq = min(256, L)
    nq = L // bq

    def body(q_ref, k_ref, v_ref, o_ref, lse_ref, do_ref, dq_ref, dk_ref, dv_ref, dk_acc, dv_acc):
        dk_acc[...] = jnp.zeros_like(dk_acc)
        dv_acc[...] = jnp.zeros_like(dv_acc)
        for qb in range(nq):
            rows = slice(qb * bq, (qb + 1) * bq)
            ext = (qb + 1) * bq
            do = do_ref[rows, :]
            dob = do.astype(BF16)
            p = jnp.exp(_attn_scores(q_ref, k_ref, qb, bq, scale) - lse_ref[rows, 0:1])
            dp = _dot_nt(dob, v_ref[0:ext, :])
            dsum = jnp.sum(do * o_ref[rows, :], axis=-1, keepdims=True)
            ds = (p * (dp - dsum) * scale).astype(BF16)
            dq_ref[rows, :] = _dot(ds, k_ref[0:ext, :]).astype(dq_ref.dtype)
            dk_acc[0:ext, :] += _dot_tn(ds, q_ref[rows, :])
            dv_acc[0:ext, :] += _dot_tn(p.astype(BF16), dob)
        dk_ref[...] = dk_acc[...].astype(dk_ref.dtype)
        dv_ref[...] = dv_acc[...].astype(dv_ref.dtype)

    sd = jax.ShapeDtypeStruct
    blk = pl.BlockSpec((L, HD), lambda h: (0, h))
    wide = pl.BlockSpec((L, 2 * HD), lambda h: (0, h))
    return pl.pallas_call(
        body, name="mla_attn_bwd", grid=(MLA_H,),
        in_specs=[wide, wide, blk, blk, blk, blk], out_specs=[wide, wide, blk],
        out_shape=[sd((L, MLA_H * 2 * HD), BF16), sd((L, MLA_H * 2 * HD), BF16), sd((L, MLA_H * HD), BF16)],
        scratch_shapes=[pltpu.VMEM((L, 2 * HD), F32), pltpu.VMEM((L, HD), F32)],
        compiler_params=pltpu.CompilerParams(dimension_semantics=("arbitrary",), vmem_limit_bytes=VMEM_LIMIT),
    )(qp, kp, v, o, lse, do)


def _kv_fn(mem, gm, w, gk):
    kv = _mm(_rms(mem, gm, D_MODEL), w)
    k = jnp.concatenate([_rms(kv[:, HD * h:HD * (h + 1)], gk, HD) for h in range(X_HEADS)], axis=-1)
    return k, kv[:, XQ:]


def _kv_prep(mem, gm, w, gk, name):
    def fn(mem, gm, w, gk):
        return _kv_fn(mem, gm, w, gk)
    M = mem.shape[0]
    return _rowwise(name, fn, [('c', mem), ('c', gm), ('c', w), ('c', gk)],
                    [('c', (M, XQ), F32), ('c', (M, XQ), F32)], 1)


def _kv_prep_bwd(mem, gm, w, gk, dk, dv, name):
    def fn(mem, gm, w, gk, dk, dv):
        _, vjp = jax.vjp(lambda a, b, c: _kv_fn(mem, a, b, c), gm, w, gk)
        return vjp((dk, dv))
    return _rowwise(name, fn, [('c', mem), ('c', gm), ('c', w), ('c', gk), ('c', dk), ('c', dv)],
                    [('c', gm.shape, F32), ('c', w.shape, BF16), ('c', gk.shape, F32)], 1)


def _forward_merge(x, mix, mix_kind, xq, gate, k, v, gq, wout, name, nblk, host=None):
    def fn(x, mix, xq, gate, k, v, gq, wout):
        o = _merge(mix, xq, gate, k, v, gq)
        return (x + _dot(o.astype(BF16), wout),)
    L = x.shape[0]
    out = _rowwise(name, fn, [('r', x), (mix_kind, mix), ('r', xq), ('r', gate), ('c', k), ('c', v), ('c', gq),
                              ('c', wout)], [('r', (L, D_MODEL), F32)], nblk, host=host)
    return out[0] if host is None else (out[0][0], out[1])


def _backward_merge(dx, mix, mix_kind, xq, gate, k, v, gq, wout, name, nblk, host=None):
    def fn(dx, mix, xq, gate, k, v, gq, wout):
        g16 = dx.astype(BF16)
        do = _dot_nt(g16, wout)
        o, vjp = jax.vjp(_merge, mix, xq, gate, k, v, gq)
        dmix, dxq, dgate, dk, dv, dgq = vjp(do)
        return dmix, dxq, dgate, o, g16, dk, dv, dgq
    L = dx.shape[0]
    return _rowwise(
        name, fn,
        [('r', dx), (mix_kind, mix), ('r', xq), ('r', gate), ('c', k), ('c', v), ('c', gq), ('c', wout)],
        [('r', (L, PRIM), F32), ('r', (L, XQ), BF16), ('r', (L, BRANCH), BF16), ('t', (BRANCH, L), BF16),
         ('r', (L, D_MODEL), BF16), ('a', k.shape, F32), ('a', v.shape, F32), ('a', gq.shape, F32)], nblk,
        host=host)


_MLA_IN = 3392
_MLA_IN_PAD = 3456


def _uq_rows(wt):
    r = wt.reshape(MLA_H, HD + ROPE, wt.shape[1])
    return jnp.concatenate([r[:, :HD].reshape(PRIM, -1),
                            jnp.pad(r[:, HD:], ((0, 0), (0, HD - ROPE), (0, 0))).reshape(PRIM, -1)], axis=0)


def _uq_rows_back(wt):
    nope = wt[:PRIM].reshape(MLA_H, HD, -1)
    rope = wt[PRIM:].reshape(MLA_H, HD, -1)[:, :ROPE]
    return jnp.concatenate([nope, rope], axis=1).reshape(MLA_H * (HD + ROPE), -1)


def _mla_in_rows_back(wt):
    return jnp.concatenate([wt[:768], wt[3328:3392], wt[768:3328]], axis=0)


_SMALL = (("ln_gain", 2048), ("mem_norm", 2048), ("xq_norm", 256), ("xk_norm", 256), ("s5_lambda_re", 6144),
          ("s5_lambda_im", 6144), ("s5_log_step", 96), ("s5_b_re", 98304), ("s5_b_im", 98304), ("s5_c_re", 98304),
          ("s5_c_im", 98304), ("s5_d", 1536), ("mla_q_lora_norm", 512), ("mla_kv_lora_norm", 256),
          ("mla_q_nope_norm", 128), ("mla_k_nope_norm", 128), ("mla_q_rope_norm", 64), ("mla_k_rope_norm", 64))
_SMALL_ROWS = 432
_SMALL_OFF = {name: sum(n for _, n in _SMALL[:i]) for i, (name, _) in enumerate(_SMALL)}


def _pack_small(d):
    flat = jnp.concatenate([d[n].reshape(-1).astype(F32) for n, _ in _SMALL])
    return jnp.pad(flat, (0, _SMALL_ROWS * 1024 - flat.shape[0])).reshape(_SMALL_ROWS, 1024)


def _unpack_small(p, name, shape):
    off = _SMALL_OFF[name]
    return p.reshape(-1)[off:off + int(np.prod(shape))].reshape(shape)


_WEIGHTS = ('ln_gain', 'w_out', 'mem_norm', 'w_mem_kv', 'xq_norm', 'xk_norm', 's5_w_in', 's5_lambda_re',
            's5_lambda_im', 's5_log_step', 's5_b_re', 's5_b_im', 's5_c_re', 's5_c_im', 's5_d', 's5_w_glu', 'mla_w_in',
            'mla_q_lora_norm', 'mla_kv_lora_norm', 'mla_w_uq', 'mla_w_ukv', 'mla_q_nope_norm', 'mla_k_nope_norm',
            'mla_q_rope_norm', 'mla_k_rope_norm')


def _pad128(g):
    return jnp.pad(g.reshape(1, -1), ((0, 0), (0, HD - g.shape[-1])))


def kernel(x, mem, positions, ln_gain, w_out, mem_norm, w_mem_kv, xq_norm, xk_norm, s5_w_in, s5_lambda_re, s5_lambda_im, s5_log_step, s5_b_re, s5_b_im, s5_c_re, s5_c_im, s5_d, s5_w_glu, mla_w_in, mla_q_lora_norm, mla_kv_lora_norm, mla_w_uq, mla_w_ukv, mla_q_nope_norm, mla_k_nope_norm, mla_q_rope_norm, mla_k_rope_norm, loss_target, m_ln_gain, m_w_out, m_mem_norm, m_w_mem_kv, m_xq_norm, m_xk_norm, m_s5_w_in, m_s5_lambda_re, m_s5_lambda_im, m_s5_log_step, m_s5_b_re, m_s5_b_im, m_s5_c_re, m_s5_c_im, m_s5_d, m_s5_w_glu, m_mla_w_in, m_mla_q_lora_norm, m_mla_kv_lora_norm, m_mla_w_uq, m_mla_w_ukv, m_mla_q_nope_norm, m_mla_k_nope_norm, m_mla_q_rope_norm, m_mla_k_rope_norm, v_ln_gain, v_w_out, v_mem_norm, v_w_mem_kv, v_xq_norm, v_xk_norm, v_s5_w_in, v_s5_lambda_re, v_s5_lambda_im, v_s5_log_step, v_s5_b_re, v_s5_b_im, v_s5_c_re, v_s5_c_im, v_s5_d, v_s5_w_glu, v_mla_w_in, v_mla_q_lora_norm, v_mla_kv_lora_norm, v_mla_w_uq, v_mla_w_ukv, v_mla_q_nope_norm, v_mla_k_nope_norm, v_mla_q_rope_norm, v_mla_k_rope_norm):
    weights = dict(ln_gain=ln_gain, w_out=w_out, mem_norm=mem_norm, w_mem_kv=w_mem_kv, xq_norm=xq_norm,
                   xk_norm=xk_norm, s5_w_in=s5_w_in, s5_lambda_re=s5_lambda_re, s5_lambda_im=s5_lambda_im,
                   s5_log_step=s5_log_step, s5_b_re=s5_b_re, s5_b_im=s5_b_im, s5_c_re=s5_c_re, s5_c_im=s5_c_im,
                   s5_d=s5_d, s5_w_glu=s5_w_glu, mla_w_in=mla_w_in, mla_q_lora_norm=mla_q_lora_norm,
                   mla_kv_lora_norm=mla_kv_lora_norm, mla_w_uq=mla_w_uq, mla_w_ukv=mla_w_ukv,
                   mla_q_nope_norm=mla_q_nope_norm, mla_k_nope_norm=mla_k_nope_norm,
                   mla_q_rope_norm=mla_q_rope_norm, mla_k_rope_norm=mla_k_rope_norm)
    m_in = dict(zip(_WEIGHTS, (m_ln_gain, m_w_out, m_mem_norm, m_w_mem_kv, m_xq_norm, m_xk_norm, m_s5_w_in,
                               m_s5_lambda_re, m_s5_lambda_im, m_s5_log_step, m_s5_b_re, m_s5_b_im, m_s5_c_re,
                               m_s5_c_im, m_s5_d, m_s5_w_glu, m_mla_w_in, m_mla_q_lora_norm, m_mla_kv_lora_norm,
                               m_mla_w_uq, m_mla_w_ukv, m_mla_q_nope_norm, m_mla_k_nope_norm, m_mla_q_rope_norm,
                               m_mla_k_rope_norm)))
    v_in = dict(zip(_WEIGHTS, (v_ln_gain, v_w_out, v_mem_norm, v_w_mem_kv, v_xq_norm, v_xk_norm, v_s5_w_in,
                               v_s5_lambda_re, v_s5_lambda_im, v_s5_log_step, v_s5_b_re, v_s5_b_im, v_s5_c_re,
                               v_s5_c_im, v_s5_d, v_s5_w_glu, v_mla_w_in, v_mla_q_lora_norm, v_mla_kv_lora_norm,
                               v_mla_w_uq, v_mla_w_ukv, v_mla_q_nope_norm, v_mla_k_nope_norm, v_mla_q_rope_norm,
                               v_mla_k_rope_norm)))

    x0 = x[0]
    mem0 = mem[0]
    target = loss_target[0]
    L = x0.shape[0]
    nblk = 4
    nb_big = 8
    me = 4 * lax.axis_index("x") + 2 * lax.axis_index("y") + lax.axis_index("c")

    lora = jnp.pad(jnp.concatenate([mla_q_lora_norm, mla_kv_lora_norm], axis=1), ((0, 7), (0, HD - 96)))
    def gather(*shards):
        return _plan_all_gather(list(shards))

    kh = D_MODEL // 2
    (b_mkv0, b_glu, b_in_mla, b_out0, b_uq, b_ukv, b_mkv1, b_out1), (W_in_s5,) = _cast_call(
        [w_mem_kv[0], s5_w_glu[0], jnp.transpose(mla_w_in[0]), w_out[0], jnp.transpose(mla_w_uq[0]), mla_w_ukv[0],
         w_mem_kv[1], w_out[1]], "cast_shards", host=gather(s5_w_in[0].astype(BF16)))

    ln0, ln1 = ln_gain[0:1], ln_gain[1:2]
    gq0, gq1 = xq_norm[0:1], xq_norm[1:2]
    gk0, gk1 = xk_norm[0:1], xk_norm[1:2]
    gm0, gm1 = mem_norm[0:1], mem_norm[1:2]
    gqn, gkn = mla_q_nope_norm, mla_k_nope_norm
    gqr, gkr = _pad128(mla_q_rope_norm), _pad128(mla_k_rope_norm)

    lr3 = s5_lambda_re.reshape(S5_G, 1, S5_P)
    li3 = s5_lambda_im.reshape(S5_G, 1, S5_P)
    ls3 = s5_log_step.reshape(S5_G, 1, 1)
    btr = jnp.swapaxes(s5_b_re[0], 1, 2)
    bti = jnp.swapaxes(s5_b_im[0], 1, 2)
    a_r, a_i, bm, cm = _s5_params(lr3, li3, ls3, btr, bti, s5_c_re[0], s5_c_im[0])
    a_r2 = a_r.reshape(1, S5_G * S5_P)
    a_i2 = a_i.reshape(1, S5_G * S5_P)
    cmask, rmat = _s5_compact_consts()

    half = ROPE // 2
    inv_freq = ROPE_THETA ** (-jnp.arange(half, dtype=F32) / half)
    invf = jnp.concatenate([inv_freq, inv_freq, jnp.zeros((HD - ROPE,), F32)]).reshape(1, HD)

    def rot_tables(pos, invf):
        ang = pos.astype(F32) * invf
        lane = lax.broadcasted_iota(jnp.int32, ang.shape, 1)
        c = jnp.where(lane < ROPE, jnp.cos(ang), 0.0)
        s = jnp.sin(ang)
        return c, jnp.where(lane < half, -s, 0.0), jnp.where((lane >= half) & (lane < ROPE), s, 0.0)

    tc, ts1, ts2 = _rowwise("rot_tables", rot_tables, [('r', positions.reshape(L, 1)), ('c', invf)],
                            [('r', (L, HD), F32)] * 3, nblk)

    def in_s5(x, g, w):
        proj = _mm_slots(_rms(x, g, D_MODEL).astype(BF16), w)
        return proj[:, :PRIM], proj[:, PRIM:PRIM + XQ], proj[:, PRIM + XQ:]

    u_s5, xq_a, gate_a = _rowwise(
        "s5_in", in_s5, [('r', x0), ('c', ln0), ('c', W_in_s5)],
        [('r', (L, PRIM), F32), ('r', (L, XQ), F32), ('r', (L, BRANCH), F32)], nblk)
    (y_s5, s5_carry), (W_glu, G_mkv0, G_in_mla_a) = _s5_fwd(u_s5, bm, cm, a_r2, a_i2, s5_d,
                                                            host=gather(b_glu, b_mkv0, b_in_mla[:, :kh]))

    def glu(y, w):
        z = _mm_slots(_gelu(y).astype(BF16), w)
        return z[:, :PRIM] * _sigmoid(z[:, PRIM:]), z

    (y2, z_glu), (G_out0,) = _rowwise("s5_glu", glu, [('r', y_s5), ('c', W_glu)],
                                      [('r', (L, PRIM), F32), ('r', (L, 2 * PRIM), F32)], nblk, host=gather(b_out0))
    W_mkv0 = G_mkv0.reshape(D_MODEL, 2 * XQ)
    k_a, v_a = _kv_prep(mem0, gm0, W_mkv0, gk0, "kv_prep0")
    x1, (G_in_mla_b,) = _forward_merge(
        x0, y2, 'r', xq_a, gate_a, k_a, v_a, gq0, G_out0.reshape(BRANCH, D_MODEL), "merge0", nblk,
        host=gather(b_in_mla[:, kh:]))
    W_in_mla = jnp.concatenate([G_in_mla_a, G_in_mla_b], axis=2).reshape(_MLA_IN, D_MODEL)

    def in_mla(x, g, w):
        xn = _rms(x, g, D_MODEL).astype(BF16)
        a = _dot_nt(xn, w[0:768])
        kx = _dot_nt(xn, w[768:896])
        b = _dot_nt(xn, w[832:_MLA_IN])
        lane = lax.broadcasted_iota(jnp.int32, kx.shape, 1)
        return a[:, :512], a[:, 512:], b[:, :XQ], b[:, XQ:], jnp.where(lane < ROPE, kx, 0.0)

    (c_q, c_kv, xq_b, gate_b, krp), (G_uq, W_kv, G_lora) = _rowwise(
        "mla_in", in_mla, [('r', x1), ('c', ln1), ('c', W_in_mla)],
        [('r', (L, Q_LORA), F32), ('r', (L, KV_LORA), F32), ('r', (L, XQ), F32), ('r', (L, BRANCH), F32),
         ('r', (L, HD), F32)], nblk,
        host=gather(b_uq, b_ukv, lora))
    W_q = _uq_rows(G_uq.reshape(MLA_H * (HD + ROPE), Q_LORA))
    g_qlora = G_lora[:, 0, :64].reshape(1, Q_LORA)
    g_kvlora = G_lora[:, 0, 64:96].reshape(1, KV_LORA)

    def qkv(c_q, c_kv, krp, tc, ts1, ts2, gql, gkvl, wq, wkv, gqn, gkn, gqr, gkr):
        q = _dot_nt(_rms(c_q, gql, Q_LORA).astype(BF16), wq)
        kv = _mm_slots(_rms(c_kv, gkvl, KV_LORA).astype(BF16), wkv)
        kp, v = _kv_post(*_kv_chunks(kv), krp, gkn, gkr, tc, ts1, ts2)
        return _q_post(*_q_chunks(q), gqn, gqr, tc, ts1, ts2), kp, v

    qkv_consts = [('c', g_qlora), ('c', g_kvlora), ('c', W_q), ('c', W_kv), ('c', gqn), ('c', gkn), ('c', gqr),
                  ('c', gkr)]
    (q_pad, k_pad, v_h), (G_mkv1, G_out1) = _rowwise(
        "mla_qkv", qkv, [('r', c_q), ('r', c_kv), ('r', krp), ('r', tc), ('r', ts1), ('r', ts2)] + qkv_consts,
        [('r', (L, 2 * PRIM), BF16), ('r', (L, 2 * PRIM), BF16), ('r', (L, PRIM), BF16)], nblk,
        host=gather(b_mkv1, b_out1))
    W_out = (G_out0.reshape(BRANCH, D_MODEL), G_out1.reshape(BRANCH, D_MODEL))
    W_mkv = (W_mkv0, G_mkv1.reshape(D_MODEL, 2 * XQ))
    scale = (HD + ROPE) ** -0.5
    attn, lse = _attn_fwd(q_pad, k_pad, v_h, scale)
    k_b, v_b = _kv_prep(mem0, gm1, W_mkv[1], gk1, "kv_prep1")

    def merge_loss(x, mix, xq, gate, k, v, gq, wout, t):
        err = x + _dot(_merge(mix, xq, gate, k, v, gq).astype(BF16), wout) - t
        part = 0.5 * jnp.sum(jnp.sum(err * err, axis=-1, keepdims=True) * (1.0 / D_MODEL), axis=0, keepdims=True)
        return err * (1.0 / D_MODEL), jnp.broadcast_to(part, (1, HD))

    dx2, loss_part = _rowwise(
        "merge1_loss", merge_loss,
        [('r', x1), ('r', attn), ('r', xq_b), ('r', gate_b), ('c', k_b), ('c', v_b), ('c', gq1), ('c', W_out[1]),
         ('r', target)], [('r', (L, D_MODEL), F32), ('a', (1, HD), F32)], nblk)

    dattn, dxq_b, dgate_b, o_b, g_b, dk_b, dv_b, dgq1 = _backward_merge(
        dx2, attn, 'r', xq_b, gate_b, k_b, v_b, gq1, W_out[1], "merge1_bwd", nb_big)
    dgm1, dW_mkv1, dgk1 = _kv_prep_bwd(mem0, gm1, W_mkv[1], gk1, dk_b, dv_b, "kv_prep1_bwd")
    dW_out1 = _matmul_tn(o_b, g_b, "dw_out1")
    dq_pad, dk_pad, dv_h = _attn_bwd(q_pad, k_pad, v_h, attn, lse, dattn, scale)

    def qkv_bwd(c_q, c_kv, krp, tc, ts1, ts2, dqp, dkp, dv, gql, gkvl, wq, wkv, gqn, gkn, gqr, gkr):
        cqn, vjp_qn = jax.vjp(lambda a, b: _rms(a, b, Q_LORA), c_q, gql)
        ckvn, vjp_kvn = jax.vjp(lambda a, b: _rms(a, b, KV_LORA), c_kv, gkvl)
        cqn16 = cqn.astype(BF16)
        ckvn16 = ckvn.astype(BF16)
        q = _dot_nt(cqn16, wq)
        kv = _mm_slots(ckvn16, wkv)
        _, vjp_q = jax.vjp(lambda n, r, a, b: _q_post(n, r, a, b, tc, ts1, ts2), *_q_chunks(q), gqn, gqr)
        dnope, drope, dgqn, dgqr = vjp_q(dqp.astype(F32))
        dq = jnp.concatenate(dnope + drope, axis=-1)
        _, vjp_kv = jax.vjp(lambda n, v, k, a, b: _kv_post(n, v, k, a, b, tc, ts1, ts2), *_kv_chunks(kv), krp, gkn,
                            gkr)
        dkn, dvals, dkrp, dgkn, dgkr = vjp_kv((dkp.astype(F32), dv.astype(F32)))
        dkv = jnp.concatenate([x for pair in zip(dkn, dvals) for x in pair], axis=-1)
        dq16 = dq.astype(BF16)
        dkv16 = dkv.astype(BF16)
        dc_q, dgql = vjp_qn(_dot(dq16, wq))
        dc_kv, dgkvl = vjp_kvn(_mm_slots_nt(dkv16, wkv))
        return dc_q, dc_kv, dkrp, cqn16, dq16, ckvn16, dkv16, dgql, dgkvl, dgqn, dgkn, dgqr, dgkr

    (dc_q, dc_kv, dkrp, cqn16, dq16, ckvn16, dkv16, dgql, dgkvl, dgqn, dgkn, dgqr, dgkr) = _rowwise(
        "mla_qkv_bwd", qkv_bwd,
        [('r', c_q), ('r', c_kv), ('r', krp), ('r', tc), ('r', ts1), ('r', ts2), ('r', dq_pad), ('r', dk_pad),
         ('r', dv_h)] + qkv_consts,
        [('r', (L, Q_LORA), BF16), ('r', (L, KV_LORA), BF16), ('r', (L, HD), BF16), ('r', (L, Q_LORA), BF16),
         ('t', (2 * PRIM, L), BF16), ('t', (KV_LORA, L), BF16), ('r', (L, 2 * PRIM), BF16),
         ('a', (1, Q_LORA), F32), ('a', (1, KV_LORA), F32), ('a', (1, HD), F32), ('a', (1, HD), F32),
         ('a', (1, HD), F32), ('a', (1, HD), F32)], nb_big)
    dW_q = _matmul_tn(dq16, cqn16, "dw_uq")
    dW_kv = _matmul_tn_slots(ckvn16, dkv16, "dw_ukv")

    def in_bwd(x, dres, g, w, *dparts):
        dproj = jnp.concatenate(dparts, axis=-1).astype(BF16)
        xn, vjp = jax.vjp(lambda a, b: _rms(a, b, D_MODEL), x, g)
        if w.ndim == 3:
            dxn = _mm_slots_nt(dproj, w)
        else:
            dkr = dproj[:, 3328:]
            dkr = jnp.where(lax.broadcasted_iota(jnp.int32, dkr.shape, 1) < ROPE, dkr, jnp.zeros_like(dkr))
            dxn = _dot(dproj[:, :768], w[0:768]) + _dot(dproj[:, 768:3328], w[832:_MLA_IN]) + _dot(dkr, w[768:896])
        dx, dg = vjp(dxn)
        return dx + dres, xn, dproj, dg

    dx1, xn1, dproj1, dln1 = _rowwise(
        "mla_in_bwd", in_bwd,
        [('r', x1), ('r', dx2), ('c', ln1), ('c', W_in_mla), ('r', dc_q), ('r', dc_kv), ('r', dxq_b), ('r', dgate_b),
         ('r', dkrp)],
        [('r', (L, D_MODEL), F32), ('r', (L, D_MODEL), BF16), ('t', (_MLA_IN_PAD, L), BF16), ('a', (1, D_MODEL), F32)],
        nblk)
    dW_in_mla = _matmul_tn(dproj1, xn1, "dw_mla_in")

    grads1 = [dW_out1.reshape(N_DEV, 256, D_MODEL), dW_mkv1.reshape(N_DEV, 128, 2 * XQ),
              _mla_in_rows_back(dW_in_mla).reshape(N_DEV, 424, D_MODEL),
              _uq_rows_back(dW_q).reshape(N_DEV, 288, Q_LORA), dW_kv]
    (dy2, dxq_a, dgate_a, o_a, g_a, dk_a, dv_a, dgq0), pair1 = _backward_merge(
        dx1, y2, 'r', xq_a, gate_a, k_a, v_a, gq0, W_out[0], "merge0_bwd", nb_big, host=_plan_pair(grads1))
    dgm0, dW_mkv0, dgk0 = _kv_prep_bwd(mem0, gm0, W_mkv[0], gk0, dk_a, dv_a, "kv_prep0_bwd")
    dW_out0 = _matmul_tn(o_a, g_a, "dw_out0")
    t1 = list(_pair_add(grads1, pair1, "rs_add_layer1"))

    def glu_bwd(y, z, dy2, w):
        h, vjp_h = jax.vjp(_gelu, y)
        _, vjp_z = jax.vjp(lambda a, b: a * _sigmoid(b), z[:, :PRIM], z[:, PRIM:])
        dz16 = jnp.concatenate(vjp_z(dy2), axis=-1).astype(BF16)
        return vjp_h(_mm_slots_nt(dz16, w))[0], h.astype(BF16), dz16

    grads0 = [dW_out0.reshape(N_DEV, 256, D_MODEL), dW_mkv0.reshape(N_DEV, 128, 2 * XQ)]
    (dy_s5, h16, dz16), glu_hosted = _rowwise(
        "s5_glu_bwd", glu_bwd, [('r', y_s5), ('r', z_glu), ('r', dy2), ('c', W_glu)],
        [('r', (L, PRIM), F32), ('t', (PRIM, L), BF16), ('r', (L, 2 * PRIM), BF16)], nb_big,
        host=_combine(_plan_chips(t1[2:3]), _plan_pair(grads0)))
    recv_in_mla, pair0 = glu_hosted[:1], glu_hosted[1:]
    dW_glu = _matmul_tn_slots(h16, dz16, "dw_glu")
    t0 = list(_pair_add(grads0 + [dW_glu], pair0 + list(_exchange_call(_plan_pair([dW_glu]), "rs_pair_glu")),
                        "rs_add_layer0"))
    both = [jnp.concatenate([t0[i], t1[i]], axis=1) for i in range(2)]
    (du_s5, dbc, dcc, dd, dar, dai), recv_rest = _s5_bwd(u_s5, dy_s5, s5_carry, bm, cm, a_r2, a_i2, s5_d,
                                                        cmask, rmat, host=_plan_chips(both + t1[3:] + t0[2:]))
    early_recv = recv_rest[:2] + recv_in_mla + recv_rest[2:]
    dbc4 = dbc.reshape(S5_G, S5_C, 2, S5_P)
    dcc4 = dcc.reshape(S5_G, S5_C, 2, S5_P)
    dlr, dli, dls, dbtr, dbti = _s5_params_bwd(
        lr3, li3, ls3, btr, bti, dar.reshape(S5_G, 1, S5_P), dai.reshape(S5_G, 1, S5_P), dbc4[:, :, 0], dbc4[:, :, 1])

    small_part = {
        "ln_gain": jnp.concatenate([jnp.zeros_like(dln1), dln1]), "mem_norm": jnp.concatenate([dgm0, dgm1]),
        "xq_norm": jnp.concatenate([dgq0, dgq1]), "xk_norm": jnp.concatenate([dgk0, dgk1]),
        "s5_lambda_re": dlr, "s5_lambda_im": dli, "s5_log_step": dls,
        "s5_b_re": jnp.swapaxes(dbtr, 1, 2), "s5_b_im": jnp.swapaxes(dbti, 1, 2),
        "s5_c_re": dcc4[:, :, 0], "s5_c_im": -dcc4[:, :, 1], "s5_d": dd,
        "mla_q_lora_norm": dgql, "mla_kv_lora_norm": dgkvl, "mla_q_nope_norm": dgqn, "mla_k_nope_norm": dgkn,
        "mla_q_rope_norm": dgqr[:, :ROPE], "mla_k_rope_norm": dgkr[:, :ROPE],
    }
    loss8 = jnp.pad(loss_part, ((0, 7), (0, 0)))
    (dx0, xn0, dproj0, dln0), (small_gath, loss_g) = _rowwise(
        "s5_in_bwd", in_bwd,
        [('r', x0), ('r', dx1), ('c', ln0), ('c', W_in_s5), ('r', du_s5), ('r', dxq_a),
         ('r', dgate_a)],
        [('r', (L, D_MODEL), F32), ('t', (D_MODEL, L), BF16), ('r', (L, 2 * BRANCH), BF16), ('a', (1, D_MODEL), F32)],
        nblk, host=_plan_all_gather([_pack_small(small_part).astype(BF16), loss8]))
    dW_in_s5 = _matmul_tn_slots(xn0, dproj0, "dw_s5_in")

    late = [dW_in_s5]
    late_t = _pair_add(late, list(_exchange_call(_plan_pair(late), "rs_pair_late")), "rs_add_late")
    owners = ["w_out", "w_mem_kv", "mla_w_in", "mla_w_uq", "mla_w_ukv", "s5_w_glu"]
    flipped = ("mla_w_in", "mla_w_uq")

    def shard(d, n):
        a = d[n]
        return jnp.transpose(a[0]) if n in flipped else a.reshape(-1, a.shape[-1])

    upd, (late_recv, ln0_gath) = _updates_call(
        early_recv, [shard(weights, n) for n in owners], [shard(m_in, n) for n in owners],
        [shard(v_in, n) for n in owners], "update_early",
        host=_combine(_plan_chips(late_t), _plan_all_gather([jnp.pad(dln0, ((0, 7), (0, 0)))])))
    owners.append("s5_w_in")
    upd.append(_sum_adamw(late_recv, s5_w_in[0], m_s5_w_in[0], v_s5_w_in[0], "update_s5_w_in"))
    grads, delta, new_m, new_v = {}, {}, {}, {}
    for n, res in zip(owners, upd):
        shape = weights[n].shape
        grads[n], delta[n], new_m[n], new_v[n] = (
            (jnp.transpose(r)[None] if n in flipped else r.reshape(shape)) for r in res)

    gs, loss_sum = _small_sum(small_gath, loss_g, ln0_gath, "small_sum")
    loss = loss_sum[0, 0]
    for n, _ in _SMALL:
        shape = weights[n].shape
        if n == "mla_q_lora_norm":
            grads[n] = lax.dynamic_slice(_unpack_small(gs, n, (Q_LORA,)), (me * 64,), (64,)).reshape(shape)
        elif n == "mla_kv_lora_norm":
            grads[n] = lax.dynamic_slice(_unpack_small(gs, n, (KV_LORA,)), (me * 32,), (32,)).reshape(shape)
        else:
            grads[n] = _unpack_small(gs, n, shape)

    def own(n, a):
        if a.ndim == 4:
            a = jnp.transpose(a, (0, 2, 3, 1))
        elif a.ndim == 3:
            a = jnp.transpose(a, (0, 2, 1))
        return a.reshape(a.shape[1:]) if a.ndim >= 3 else a

    def back(n, a):
        shape = weights[n].shape
        if len(shape) == 4:
            return jnp.transpose(a.reshape((1,) + a.shape), (0, 3, 1, 2))
        if len(shape) == 3:
            return jnp.transpose(a.reshape((1,) + a.shape), (0, 2, 1))
        return a.reshape(shape)

    wide = ("s5_b_re", "s5_b_im", "s5_c_re", "s5_c_im")
    for names, nb, call in (([n for n, _ in _SMALL if n not in wide], 1, "update_small"), (wide, 4, "update_s5_bc")):
        res = _adamw_multi([own(n, weights[n]) for n in names], [own(n, grads[n]) for n in names],
                           [own(n, m_in[n]) for n in names], [own(n, v_in[n]) for n in names], call, nb)
        for n, (dl, m2, v2) in zip(names, res):
            delta[n], new_m[n], new_v[n] = back(n, dl), back(n, m2), back(n, v2)
    return (loss, dx0[None], *[grads[n] for n in _WEIGHTS], *[delta[n] for n in _WEIGHTS],
            *[new_m[n] for n in _WEIGHTS], *[new_v[n] for n in _WEIGHTS])
```

```python
import functools
import math

import numpy as np
import jax
import jax.numpy as jnp
from jax import lax
from jax.experimental import pallas as pl
from jax.experimental.pallas import tpu as pltpu

F32 = jnp.float32
BF16 = jnp.bfloat16
EPS = 1e-6
NEG = float(np.finfo(np.float32).min)
MESH = pl.DeviceIdType.MESH

N_DEV = 8
D_MODEL = 1024
MEM_LEN = 256
XQ = 512
PRIM = 1536
BRANCH = 2048
X_HEADS = 4
HD = 128
S5_G = 96
S5_P = 64
S5_C = 16
S5_GB = 8
S5_W = S5_GB * S5_P
MLA_H = 12
ROPE = 64
Q_LORA = 512
KV_LORA = 256
ROPE_THETA = 10000.0

ADAM_LR = 0.001
ADAM_B1 = 0.9
ADAM_B2 = 0.999
ADAM_EPS = 1e-08
ADAM_WD = 0.01
ADAM_STEP = 10

VMEM_LIMIT = 56 * 1024 * 1024


def _dot(a, b):
    return jnp.dot(a, b, preferred_element_type=F32)


def _dot_nt(a, b):
    return lax.dot_general(a, b, (((1,), (1,)), ((), ())), preferred_element_type=F32)


def _dot_tn(a, b):
    return lax.dot_general(a, b, (((0,), (0,)), ((), ())), preferred_element_type=F32)


@jax.custom_vjp
def _mm(a, b):
    return _dot(a.astype(BF16), b.astype(BF16))


def _mm_fwd(a, b):
    return _mm(a, b), (a, b)


def _mm_bwd(res, g):
    a, b = res
    gb = g.astype(BF16)
    return _dot_nt(gb, b.astype(BF16)).astype(a.dtype), _dot_tn(a.astype(BF16), gb).astype(b.dtype)


_mm.defvjp(_mm_fwd, _mm_bwd)


@jax.custom_vjp
def _mm_nt(a, b):
    return _dot_nt(a.astype(BF16), b.astype(BF16))


def _mm_nt_fwd(a, b):
    return _mm_nt(a, b), (a, b)


def _mm_nt_bwd(res, g):
    a, b = res
    gb = g.astype(BF16)
    return _dot(gb, b.astype(BF16)).astype(a.dtype), _dot_tn(gb, a.astype(BF16)).astype(b.dtype)


_mm_nt.defvjp(_mm_nt_fwd, _mm_nt_bwd)


@jax.custom_vjp
def _softmax(s):
    m = jnp.max(s, axis=-1, keepdims=True)
    e = jnp.exp(s - m)
    return e / jnp.sum(e, axis=-1, keepdims=True)


def _softmax_fwd(s):
    p = _softmax(s)
    return p, p


def _softmax_bwd(p, g):
    return (p * (g - jnp.sum(p * g, axis=-1, keepdims=True)),)


_softmax.defvjp(_softmax_fwd, _softmax_bwd)


def _rms(x, g, n):
    ms = jnp.sum(x * x, axis=-1, keepdims=True) * (1.0 / n)
    return x * lax.rsqrt(ms + EPS) * g


def _sigmoid(x):
    return 1.0 / (1.0 + jnp.exp(-x))


def _silu(x):
    return x * _sigmoid(x)


def _gelu(x):
    c = math.sqrt(2.0 / math.pi)
    return 0.5 * x * (1.0 + jnp.tanh(c * (x + 0.044715 * (x * x * x))))


@jax.custom_vjp
def _rot(x, c, s1, s2):
    return x * c + pltpu.roll(x, 96, 1) * s1 + pltpu.roll(x, 32, 1) * s2


def _rot_fwd(x, c, s1, s2):
    return _rot(x, c, s1, s2), (c, s1, s2)


def _rot_bwd(res, g):
    c, s1, s2 = res
    dx = g * c + pltpu.roll(g * s1, 32, 1) + pltpu.roll(g * s2, 96, 1)
    return dx, jnp.zeros_like(c), jnp.zeros_like(s1), jnp.zeros_like(s2)


_rot.defvjp(_rot_fwd, _rot_bwd)


def _mem_attn(xq, k, v, gq):
    outs = []
    for h in range(X_HEADS):
        sl = slice(HD * h, HD * (h + 1))
        q = _rms(xq[:, sl], gq, HD)
        p = _softmax(_mm_nt(q, k[:, sl]) * (HD ** -0.5))
        outs.append(_mm(p, v[:, sl]))
    return jnp.concatenate(outs, axis=-1)


def _merge(mix, xq, gate, k, v, gq):
    return jnp.concatenate([mix, _mem_attn(xq, k, v, gq)], axis=-1) * _silu(gate)


def _q_chunks(q):
    return ([q[:, HD * h:HD * (h + 1)] for h in range(MLA_H)],
            [q[:, PRIM + HD * h:PRIM + HD * (h + 1)] for h in range(MLA_H)])


def _q_post(nope, rope, gqn, gqr, c, s1, s2):
    pieces = []
    for qn, qr in zip(nope, rope):
        pieces.append(_rms(qn, gqn, HD))
        pieces.append(_rot(_rms(qr, gqr, ROPE), c, s1, s2))
    return jnp.concatenate(pieces, axis=-1)


def _kv_chunks(kv):
    return ([kv[:, 2 * HD * h:2 * HD * h + HD] for h in range(MLA_H)],
            [kv[:, 2 * HD * h + HD:2 * HD * (h + 1)] for h in range(MLA_H)])


def _kv_post(kn, vals, krp, gkn, gkr, c, s1, s2):
    kr = _rot(_rms(krp, gkr, ROPE), c, s1, s2)
    pieces = []
    for k in kn:
        pieces.append(_rms(k, gkn, HD))
        pieces.append(kr)
    return jnp.concatenate(pieces, axis=-1), jnp.concatenate(vals, axis=-1)


def _rowwise(name, fn, ins, outs, nblk, host=None):
    n_in = len(ins)

    def spec(kind, shape):
        if kind == 'r':
            return pl.BlockSpec((shape[0] // nblk, shape[1]), lambda i: (i, 0))
        if kind == 't':
            return pl.BlockSpec((shape[0], shape[1] // nblk), lambda i: (0, i))
        zeros = (0,) * len(shape)
        return pl.BlockSpec(tuple(shape), lambda i: zeros)

    def body(*refs):
        i = pl.program_id(0)
        res = fn(*[r[...] for r in refs[:n_in]])
        for (kind, _, _), ref, val in zip(outs, refs[n_in:], res):
            if kind == 'a':
                @pl.when(i == 0)
                def _():
                    ref[...] = jnp.zeros_like(ref)
                ref[...] += val.astype(ref.dtype)
            elif kind == 't':
                ref[...] = val.astype(F32).T.astype(ref.dtype)
            else:
                ref[...] = val.astype(ref.dtype)

    res, hosted = _hosting_call(
        body, name, nblk, host, [a for _, a in ins], [spec(k, a.shape) for k, a in ins],
        [jax.ShapeDtypeStruct(tuple(s), d) for _, s, d in outs], [spec(k, s) for k, s, _ in outs], [])
    return res if host is None else (res, hosted)


def _matmul_tn(at, g, name, out_dtype=BF16):
    K, L = at.shape
    N = g.shape[1]
    tn = next(t for t in (512, 384, 256, 128) if N % t == 0)

    def body(a_ref, g_ref, o_ref):
        o_ref[...] = _dot(a_ref[...], g_ref[...]).astype(o_ref.dtype)

    return pl.pallas_call(
        body, name=name, grid=(N // tn,),
        in_specs=[pl.BlockSpec((K, L), lambda n: (0, 0)), pl.BlockSpec((L, tn), lambda n: (0, n))],
        out_specs=pl.BlockSpec((K, tn), lambda n: (0, n)),
        out_shape=jax.ShapeDtypeStruct((K, N), out_dtype),
        compiler_params=pltpu.CompilerParams(dimension_semantics=("arbitrary",), vmem_limit_bytes=VMEM_LIMIT),
    )(at, g)


def _matmul_tn_slots(at, g, name, host=None):
    K, L = at.shape
    n = g.shape[1] // N_DEV

    def body(a_ref, g_ref, o_ref):
        o_ref[...] = _dot(a_ref[...], g_ref[...]).astype(o_ref.dtype)

    res, hosted = _hosting_call(
        body, name, N_DEV, host, [at, g],
        [pl.BlockSpec((K, L), lambda d: (0, 0)), pl.BlockSpec((L, n), lambda d: (0, d))],
        [jax.ShapeDtypeStruct((N_DEV, K, n), BF16)], [pl.BlockSpec((None, K, n), lambda d: (d, 0, 0))], [])
    return res[0] if host is None else (res[0], hosted)


def _mm_slots(a16, w):
    return jnp.concatenate([_dot(a16, w[d]) for d in range(N_DEV)], axis=-1)


def _mm_slots_nt(g16, w):
    n = w.shape[2]
    out = _dot_nt(g16[:, 0:n], w[0])
    for d in range(1, N_DEV):
        out = out + _dot_nt(g16[:, d * n:(d + 1) * n], w[d])
    return out


class _Exchange:
    def __init__(self, ins, outs, scratch, start, finish):
        self.ins, self.outs, self.scratch, self.start, self.finish = ins, outs, scratch, start, finish


def _xyc():
    return lax.axis_index("x"), lax.axis_index("y"), lax.axis_index("c")


def _plan_all_gather(xs):
    n = len(xs)

    def build(x_refs, out_refs, sems):
        send_sems, recv_sems, local_sems = sems
        x, y, c = _xyc()

        def copies(k, block, to, own=False):
            slot = 4 * block[0] + 2 * block[1] + block[2]
            return [pltpu.make_async_remote_copy(
                src_ref=x_refs[a] if own else out_refs[a].at[slot], dst_ref=out_refs[a].at[slot],
                send_sem=send_sems.at[k * n + a], recv_sem=recv_sems.at[k * n + a], device_id=to,
                device_id_type=MESH) for a in range(n)]

        mine = [pltpu.make_async_copy(x_refs[a], out_refs[a].at[4 * x + 2 * y + c], local_sems.at[a])
                for a in range(n)]
        return copies, mine, (x, y, c), [(1 - x, y), (x, 1 - y), (1 - x, 1 - y)]

    def first_copies(copies, me, chips):
        x, y, c = me
        first = copies(0, me, (x, y, 1 - c), own=True)
        for j, chip in enumerate(chips):
            first += copies(1 + j, me, (*chip, c), own=True)
        return first

    def start(x_refs, out_refs, sems):
        copies, mine, me, chips = build(x_refs, out_refs, sems)
        for cp in mine + first_copies(copies, me, chips):
            cp.start()

    def finish(x_refs, out_refs, sems):
        copies, mine, me, chips = build(x_refs, out_refs, sems)
        x, y, c = me
        passed = []
        for j, chip in enumerate(chips):
            for cp in copies(1 + j, (*chip, c), me):
                cp.wait_recv()
            fwd = copies(4 + j, (*chip, c), (x, y, 1 - c))
            for cp in fwd:
                cp.start()
            passed += fwd
        for cp in copies(0, (x, y, 1 - c), me):
            cp.wait_recv()
        for j, chip in enumerate(chips):
            for cp in copies(4 + j, (*chip, 1 - c), me):
                cp.wait_recv()
        for cp in first_copies(copies, me, chips) + passed:
            cp.wait_send()
        for cp in mine:
            cp.wait()

    return _Exchange(list(xs), [jax.ShapeDtypeStruct((N_DEV,) + a.shape, a.dtype) for a in xs],
                     [pltpu.SemaphoreType.DMA((7 * n,)), pltpu.SemaphoreType.DMA((7 * n,)),
                      pltpu.SemaphoreType.DMA((n,))], start, finish)


_CHIPS = ((0, 0), (0, 1), (1, 0), (1, 1))


def _plan_pair(sends):
    n = len(sends)

    def build(s_refs, o_refs, sems):
        send_sems, recv_sems = sems
        x, y, c = _xyc()
        return [pltpu.make_async_remote_copy(
            src_ref=s_refs[a].at[4 * px + 2 * py + 1 - c], dst_ref=o_refs[a].at[j],
            send_sem=send_sems.at[j * n + a], recv_sem=recv_sems.at[j * n + a], device_id=(x, y, 1 - c),
            device_id_type=MESH) for j, (px, py) in enumerate(_CHIPS) for a in range(n)]

    def start(s_refs, o_refs, sems):
        for cp in build(s_refs, o_refs, sems):
            cp.start()

    def finish(s_refs, o_refs, sems):
        for cp in build(s_refs, o_refs, sems):
            cp.wait_recv()
            cp.wait_send()

    return _Exchange(list(sends), [jax.ShapeDtypeStruct((4,) + a.shape[1:], a.dtype) for a in sends],
                     [pltpu.SemaphoreType.DMA((4 * n,)), pltpu.SemaphoreType.DMA((4 * n,))], start, finish)


def _plan_chips(ts):
    n = len(ts)
    flips = ((1, 0), (0, 1), (1, 1))

    def build(t_refs, o_refs, sems):
        send_sems, recv_sems, local_sems = sems
        x, y, c = _xyc()
        mine = 2 * x + y
        local = [pltpu.make_async_copy(t_refs[a].at[mine], o_refs[a].at[mine], local_sems.at[a]) for a in range(n)]
        remote = []
        for k, (fx, fy) in enumerate(flips):
            px = 1 - x if fx else x
            py = 1 - y if fy else y
            remote += [pltpu.make_async_remote_copy(
                src_ref=t_refs[a].at[2 * px + py], dst_ref=o_refs[a].at[mine],
                send_sem=send_sems.at[k * n + a], recv_sem=recv_sems.at[k * n + a], device_id=(px, py, c),
                device_id_type=MESH) for a in range(n)]
        return local, remote

    def start(t_refs, o_refs, sems):
        local, remote = build(t_refs, o_refs, sems)
        for cp in local + remote:
            cp.start()

    def finish(t_refs, o_refs, sems):
        local, remote = build(t_refs, o_refs, sems)
        for cp in remote:
            cp.wait_recv()
        for cp in remote:
            cp.wait_send()
        for cp in local:
            cp.wait()

    return _Exchange(list(ts), [jax.ShapeDtypeStruct(a.shape, a.dtype) for a in ts],
                     [pltpu.SemaphoreType.DMA((3 * n,)), pltpu.SemaphoreType.DMA((3 * n,)),
                      pltpu.SemaphoreType.DMA((n,))], start, finish)


def _combine(*plans):
    def parts(refs, attr):
        out, at = [], 0
        for p in plans:
            n = len(getattr(p, attr))
            out.append(refs[at:at + n])
            at += n
        return out

    def run(half):
        def go(ins, outs, sems):
            for p, a, o, s in zip(plans, parts(ins, "ins"), parts(outs, "outs"), parts(sems, "scratch")):
                getattr(p, half)(a, o, s)
        return go

    return _Exchange(sum((p.ins for p in plans), []), sum((p.outs for p in plans), []),
                     sum((p.scratch for p in plans), []), run("start"), run("finish"))


def _exchange_call(plan, name):
    n = len(plan.ins)

    def body(*refs):
        ins, outs, sems = refs[:n], refs[n:2 * n], refs[2 * n:]
        plan.start(ins, outs, sems)
        plan.finish(ins, outs, sems)

    return pl.pallas_call(
        body, name=name, out_shape=plan.outs,
        in_specs=[pl.BlockSpec(memory_space=pl.ANY)] * n, out_specs=[pl.BlockSpec(memory_space=pl.ANY)] * n,
        scratch_shapes=plan.scratch,
    )(*plan.ins)


def _slab_spec(lead, rows, cols, nb):
    if rows % (nb * 16) == 0:
        return pl.BlockSpec((lead, rows // nb, cols), lambda i: (0, i, 0))
    if cols % (nb * 128) == 0:
        return pl.BlockSpec((lead, rows, cols // nb), lambda i: (0, 0, i))
    return pl.BlockSpec((lead, rows, cols), lambda i: (0, 0, 0))


def _slab_spec2(rows, cols, nb):
    if rows % (nb * 16) == 0:
        return pl.BlockSpec((rows // nb, cols), lambda i: (i, 0))
    if cols % (nb * 128) == 0:
        return pl.BlockSpec((rows, cols // nb), lambda i: (0, i))
    return pl.BlockSpec((rows, cols), lambda i: (0, 0))


def _cast_call(arrays, name, host=None):
    n = len(arrays)
    nb = 8

    def body(*refs):
        for a in range(n):
            refs[n + a][...] = refs[a][...].astype(BF16)

    specs = [_slab_spec2(x.shape[0], x.shape[1], nb) for x in arrays]
    return _hosting_call(body, name, nb, host, list(arrays), specs,
                         [jax.ShapeDtypeStruct(x.shape, BF16) for x in arrays], specs, [])


def _pair_add(sends, fromsib, name):
    n = len(sends)
    nb = 8

    def body(*refs):
        c = lax.axis_index("c")
        for a in range(n):
            s_ref, f_ref, t_ref = refs[a], refs[n + a], refs[2 * n + a]
            for j in range(4):
                t_ref[j] = (s_ref[2 * j + c].astype(F32) + f_ref[j].astype(F32)).astype(t_ref.dtype)

    def spec(a, lead):
        return _slab_spec(lead, a.shape[1], a.shape[2], nb)

    return pl.pallas_call(
        body, name=name, grid=(nb,),
        in_specs=[spec(a, N_DEV) for a in sends] + [spec(a, 4) for a in fromsib],
        out_specs=[spec(a, 4) for a in fromsib],
        out_shape=[jax.ShapeDtypeStruct(a.shape, a.dtype) for a in fromsib],
        compiler_params=pltpu.CompilerParams(dimension_semantics=("arbitrary",), vmem_limit_bytes=VMEM_LIMIT),
    )(*sends, *fromsib)


def _adamw_vals(w, g, m, v):
    m2 = ADAM_B1 * m + (1.0 - ADAM_B1) * g
    v2 = ADAM_B2 * v + (1.0 - ADAM_B2) * (g * g)
    m_hat = m2 / (1.0 - ADAM_B1 ** ADAM_STEP)
    v_hat = v2 / (1.0 - ADAM_B2 ** ADAM_STEP)
    delta = -ADAM_LR * (m_hat / (jnp.sqrt(v_hat) + ADAM_EPS) + ADAM_WD * w)
    return delta, m2, v2


def _updates_call(recvs, ws, ms, vs, name, host=None):
    n = len(recvs)
    nb = 8

    def body(*refs):
        for a in range(n):
            r_ref, w_ref, m_ref, v_ref = refs[a], refs[n + a], refs[2 * n + a], refs[3 * n + a]
            g_ref, d_ref, m2_ref, v2_ref = refs[4 * n + 4 * a:4 * n + 4 * a + 4]
            g = r_ref[0].astype(F32)
            for d in range(1, r_ref.shape[0]):
                g = g + r_ref[d].astype(F32)
            dl, m2, v2 = _adamw_vals(w_ref[...], g, m_ref[...], v_ref[...])
            g_ref[...] = g
            d_ref[...] = dl
            m2_ref[...] = m2
            v2_ref[...] = v2

    def spec3(r):
        return _slab_spec(r.shape[0], r.shape[1], r.shape[2], nb)

    def spec2(w):
        return _slab_spec2(w.shape[0], w.shape[1], nb)

    res, hosted = _hosting_call(
        body, name, nb, host, list(recvs) + list(ws) + list(ms) + list(vs),
        [spec3(r) for r in recvs] + [spec2(w) for w in ws] * 3,
        [jax.ShapeDtypeStruct(w.shape, F32) for w in ws for _ in range(4)],
        [spec2(w) for w in ws for _ in range(4)], [])
    return [res[4 * a:4 * a + 4] for a in range(n)], hosted


def _small_sum(gath, loss_g, row0_g, name):
    _, R, C = gath.shape
    br = R // 3

    def body(g_ref, l_ref, r_ref, go_ref, lo_ref):
        g = g_ref[0].astype(F32)
        lsum = l_ref[0]
        for d in range(1, N_DEV):
            g = g + g_ref[d].astype(F32)
            lsum = lsum + l_ref[d]
        go_ref[...] = g
        lo_ref[...] = lsum

        @pl.when(pl.program_id(0) == 0)
        def _():
            row0 = r_ref[0]
            for d in range(1, N_DEV):
                row0 = row0 + r_ref[d]
            go_ref[0:8, :] = go_ref[0:8, :] + jnp.where(lax.broadcasted_iota(jnp.int32, row0.shape, 0) == 0, row0, 0.0)

    return pl.pallas_call(
        body, name=name, grid=(R // br,),
        in_specs=[pl.BlockSpec((N_DEV, br, C), lambda i: (0, i, 0)),
                  pl.BlockSpec((N_DEV, 8, HD), lambda i: (0, 0, 0)), pl.BlockSpec((N_DEV, 8, C), lambda i: (0, 0, 0))],
        out_specs=[pl.BlockSpec((br, C), lambda i: (i, 0)), pl.BlockSpec((8, HD), lambda i: (0, 0))],
        out_shape=[jax.ShapeDtypeStruct((R, C), F32), jax.ShapeDtypeStruct((8, HD), F32)],
        compiler_params=pltpu.CompilerParams(dimension_semantics=("arbitrary",)),
    )(gath, loss_g, row0_g)


def _adamw_multi(ws, gs, ms, vs, name, nblk=1):
    n = len(ws)

    def body(*refs):
        for a in range(n):
            dl, m2, v2 = _adamw_vals(refs[a][...], refs[n + a][...], refs[2 * n + a][...], refs[3 * n + a][...])
            refs[4 * n + 3 * a][...] = dl
            refs[4 * n + 3 * a + 1][...] = m2
            refs[4 * n + 3 * a + 2][...] = v2

    def spec(x):
        rest = (0,) * (x.ndim - 1)
        return pl.BlockSpec((x.shape[0] // nblk,) + tuple(x.shape[1:]), lambda i: (i,) + rest)

    res = pl.pallas_call(
        body, name=name, grid=(nblk,),
        in_specs=[spec(w) for w in ws] * 4, out_specs=[spec(w) for w in ws for _ in range(3)],
        out_shape=[jax.ShapeDtypeStruct(w.shape, F32) for w in ws for _ in range(3)],
        compiler_params=pltpu.CompilerParams(dimension_semantics=("arbitrary",), vmem_limit_bytes=VMEM_LIMIT),
    )(*ws, *gs, *ms, *vs)
    return [res[3 * a:3 * a + 3] for a in range(n)]


def _s5_param_fn(lr, li, ls, btr, bti):
    step = jnp.exp(ls)
    er = jnp.exp(lr * step)
    ang = li * step
    ar = er * jnp.cos(ang)
    ai = er * jnp.sin(ang)
    nr = ar - 1.0
    den = lr * lr + li * li
    fr = (nr * lr + ai * li) / den
    fi = (ai * lr - nr * li) / den
    return ar, ai, fr * btr - fi * bti, fr * bti + fi * btr


def _s5_params(lr, li, ls, btr, bti, cre, cim):
    nb = S5_G // S5_GB
    GC = S5_GB * S5_C
    expand = jnp.asarray(np.tile(np.eye(S5_P, dtype=np.float32), (1, S5_GB)), BF16)
    own = jnp.asarray((np.arange(GC)[:, None] // S5_C == np.arange(S5_W)[None, :] // S5_P).astype(np.float32))

    def body(lr_ref, li_ref, ls_ref, br_ref, bi_ref, cr_ref, ci_ref, e_ref, own_ref, ar_ref, ai_ref, bm_ref, cm_ref):
        ar, ai, bbr, bbi = _s5_param_fn(lr_ref[...], li_ref[...], ls_ref[...], br_ref[...], bi_ref[...])
        ar_ref[...] = ar
        ai_ref[...] = ai

        def plane(x, n):
            rows = x[n * S5_GB:(n + 1) * S5_GB].reshape(GC, S5_P).astype(BF16)
            return _dot(rows, e_ref[...]) * own_ref[...]

        for n in range(nb):
            bm_ref[n] = jnp.concatenate([plane(bbr, n), plane(bbi, n)], axis=-1).astype(BF16)
            cm_ref[n] = jnp.concatenate([plane(cr_ref[...], n), -plane(ci_ref[...], n)], axis=-1).astype(BF16)

    sd = jax.ShapeDtypeStruct
    return pl.pallas_call(
        body, name="s5_params",
        out_shape=[sd(lr.shape, F32), sd(lr.shape, F32), sd((nb, GC, 2 * S5_W), BF16), sd((nb, GC, 2 * S5_W), BF16)],
        compiler_params=pltpu.CompilerParams(vmem_limit_bytes=VMEM_LIMIT),
    )(lr, li, ls, btr, bti, cre, cim, expand, own)


def _s5_params_bwd(lr, li, ls, btr, bti, dar, dai, dbbr, dbbi):
    def body(lr_ref, li_ref, ls_ref, br_ref, bi_ref, dar_ref, dai_ref, dbbr_ref, dbbi_ref,
             dlr_ref, dli_ref, dls_ref, dbr_ref, dbi_ref):
        _, vjp = jax.vjp(_s5_param_fn, lr_ref[...], li_ref[...], ls_ref[...], br_ref[...], bi_ref[...])
        dlr, dli, dls, dbr, dbi = vjp((dar_ref[...], dai_ref[...], dbbr_ref[...], dbbi_ref[...]))
        dlr_ref[...] = dlr
        dli_ref[...] = dli
        dls_ref[...] = dls
        dbr_ref[...] = dbr
        dbi_ref[...] = dbi

    sd = jax.ShapeDtypeStruct
    return pl.pallas_call(
        body, name="s5_params_bwd",
        out_shape=[sd(lr.shape, F32), sd(lr.shape, F32), sd(ls.shape, F32), sd(btr.shape, F32), sd(btr.shape, F32)],
    )(lr, li, ls, btr, bti, dar, dai, dbbr, dbbi)


def _cpow(ar, ai, n):
    assert n & (n - 1) == 0
    while n > 1:
        ar, ai = ar * ar - ai * ai, 2.0 * ar * ai
        n //= 2
    return ar, ai


def _scan(st, cr, ci, init, nk, reverse, store, prev=None):
    W = S5_W

    def advance(k, sr, si):
        rows = pl.ds(k * 8 if isinstance(k, int) else pl.multiple_of(k * 8, 8), 8)
        nsr = cr * sr - ci * si + st[rows, 0:W]
        nsi = cr * si + ci * sr + st[rows, W:2 * W]
        if store:
            st[rows, 0:W] = nsr
            st[rows, W:2 * W] = nsi
        return nsr, nsi

    if prev is None:
        return lax.fori_loop(0, nk, lambda j, c: advance(nk - 1 - j if reverse else j, c[0], c[1]), init, unroll=2)
    assert reverse

    def step(j, carry):
        k = nk - 1 - j
        nsr, nsi = advance(k, carry[0], carry[1])
        prows = pl.ds(pl.multiple_of((k - 1) * 8, 8), 8)
        pr = prev[prows, 0:W]
        pi = prev[prows, W:2 * W]
        return nsr, nsi, carry[2] + nsr * pr + nsi * pi, carry[3] + nsi * pr - nsr * pi

    carry = lax.fori_loop(0, nk - 1, step, init, unroll=2)
    nsr, nsi = advance(0, carry[0], carry[1])
    return nsr, nsi, carry[2], carry[3]


def _chain(fin, fr, fi, pr, pi, reverse):
    W = S5_W
    fin[:, 0:W] = fr
    fin[:, W:2 * W] = fi
    rowid = lax.broadcasted_iota(jnp.int32, (8, W), 0)
    cr = jnp.zeros((1, W), F32)
    ci = jnp.zeros((1, W), F32)
    init_r = jnp.zeros((8, W), F32)
    init_i = jnp.zeros((8, W), F32)
    for s in (range(7, -1, -1) if reverse else range(8)):
        init_r = jnp.where(rowid == s, cr, init_r)
        init_i = jnp.where(rowid == s, ci, init_i)
        lr = fin[s:s + 1, 0:W]
        li = fin[s:s + 1, W:2 * W]
        cr, ci = lr + pr * cr - pi * ci, li + pr * ci + pi * cr
    return init_r, init_i


def _full_scan(st, fin, ar, ai, nk, reverse, prev=None, carry_in=None, carry_out=None):
    W = S5_W
    cr = jnp.broadcast_to(ar, (8, W))
    ci = jnp.broadcast_to(-ai if reverse else ai, (8, W))
    z = jnp.zeros((8, W), F32)
    if carry_in is None:
        fr, fi = _scan(st, cr, ci, (z, z), nk, reverse, store=False)
        pr, pi = _cpow(ar, -ai if reverse else ai, nk)
        init = _chain(fin, fr, fi, pr, pi, reverse)
    else:
        init = (carry_in[:, 0:W], carry_in[:, W:2 * W])
    if carry_out is not None:
        carry_out[:, 0:W] = init[0]
        carry_out[:, W:2 * W] = init[1]
    if prev is None:
        return _scan(st, cr, ci, init, nk, reverse, store=True)
    return _scan(st, cr, ci, init + (z, z), nk, reverse, store=True, prev=prev)


def _s5_specs(L):
    W2 = 2 * S5_W
    GC = S5_GB * S5_C
    col = pl.BlockSpec((L, GC), lambda g: (0, g))
    vec = pl.BlockSpec((1, GC), lambda g: (0, g))
    avec = pl.BlockSpec((1, S5_W), lambda g: (0, g))
    bmat = pl.BlockSpec((None, GC, W2), lambda g: (g, 0, 0))
    cmat = pl.BlockSpec((None, W2, GC), lambda g: (g, 0, 0))
    return col, vec, avec, bmat, cmat


def _interleave(dst, src, nk):
    for s in range(8):
        dst[pl.ds(s, nk, stride=8), :] = src[s * nk:(s + 1) * nk, :]


def _deinterleave(dst, src, nk):
    for s in range(8):
        dst[s * nk:(s + 1) * nk, :] = src[pl.ds(s, nk, stride=8), :].astype(dst.dtype)


def _hosting_call(body, name, nsteps, host, ins, in_specs, outs, out_specs, scratch):
    grid = (nsteps,) if isinstance(nsteps, int) else tuple(nsteps)
    params = pltpu.CompilerParams(dimension_semantics=("arbitrary",) * len(grid), vmem_limit_bytes=VMEM_LIMIT)
    if host is None:
        res = pl.pallas_call(
            body, name=name, grid=grid, in_specs=in_specs, out_specs=out_specs, out_shape=outs,
            scratch_shapes=scratch, compiler_params=params,
        )(*ins)
        return list(res), []
    n_in, n_out, n_sc = len(ins), len(outs), len(scratch)
    h_in, h_out = len(host.ins), len(host.outs)

    def hosted(*refs):
        a = refs[:n_in]
        ha = refs[n_in:n_in + h_in]
        o = refs[n_in + h_in:n_in + h_in + n_out]
        ho = refs[n_in + h_in + n_out:n_in + h_in + n_out + h_out]
        sc = refs[n_in + h_in + n_out + h_out:n_in + h_in + n_out + h_out + n_sc]
        hs = refs[n_in + h_in + n_out + h_out + n_sc:]
        first = functools.reduce(jnp.logical_and, [pl.program_id(i) == 0 for i in range(len(grid))])
        last = functools.reduce(jnp.logical_and, [pl.program_id(i) == g - 1 for i, g in enumerate(grid)])

        @pl.when(first)
        def _():
            host.start(ha, ho, hs)

        body(*a, *o, *sc)

        @pl.when(last)
        def _():
            host.finish(ha, ho, hs)

    hbm = pl.BlockSpec(memory_space=pl.ANY)
    res = pl.pallas_call(
        hosted, name=name, grid=grid,
        in_specs=list(in_specs) + [hbm] * h_in, out_specs=list(out_specs) + [hbm] * h_out,
        out_shape=list(outs) + list(host.outs), scratch_shapes=list(scratch) + list(host.scratch),
        compiler_params=params,
    )(*ins, *host.ins)
    return list(res[:n_out]), list(res[n_out:])


def _s5_fwd(u, bm, cm, ar, ai, dvec, host=None):
    L = u.shape[0]
    nk = L // 8
    GC = S5_GB * S5_C
    nb = S5_G // S5_GB
    col, vec, avec, bmat, cmat = _s5_specs(L)

    def body(u_ref, b_ref, c_ref, ar_ref, ai_ref, d_ref, y_ref, carry_ref, st, fin, ui, yi):
        _interleave(ui, u_ref, nk)
        for r in range(8):
            rows = slice(r * nk, (r + 1) * nk)
            st[rows, :] = _dot(ui[rows, :].astype(BF16), b_ref[...])
        _full_scan(st, fin, ar_ref[...], ai_ref[...], nk, reverse=False, carry_out=carry_ref)
        for r in range(8):
            rows = slice(r * nk, (r + 1) * nk)
            yi[rows, :] = _dot_nt(st[rows, :].astype(BF16), c_ref[...]) + d_ref[...] * ui[rows, :]
        _deinterleave(y_ref, yi, nk)

    return _hosting_call(
        body, "s5_fwd", nb, host,
        [u, bm, cm, ar, ai, dvec], [col, bmat, bmat, avec, avec, vec],
        [jax.ShapeDtypeStruct(u.shape, F32), jax.ShapeDtypeStruct((nb * 8, 2 * S5_W), F32)],
        [col, pl.BlockSpec((8, 2 * S5_W), lambda g: (g, 0))],
        [pltpu.VMEM((L, 2 * S5_W), F32), pltpu.VMEM((8, 2 * S5_W), F32), pltpu.VMEM((L, GC), F32),
         pltpu.VMEM((L, GC), F32)])


def _s5_bwd(u, dy, carry, bm, cm, ar, ai, dvec, mask, rmat, host=None):
    L = u.shape[0]
    nk = L // 8
    W = S5_W
    GC = S5_GB * S5_C
    col, vec, avec, bmat, cmat = _s5_specs(L)
    hi = lax.Precision.HIGHEST

    def body(u_ref, dy_ref, carry_ref, b_ref, ct_ref, ar_ref, ai_ref, d_ref, mask_ref, r_ref,
             du_ref, db_ref, dc_ref, dd_ref, dar_ref, dai_ref, sa, sb, fin, ui, dyi, dui):
        ar = ar_ref[...]
        ai = ai_ref[...]
        _interleave(ui, u_ref, nk)
        _interleave(dyi, dy_ref, nk)
        for r in range(8):
            rows = slice(r * nk, (r + 1) * nk)
            sa[rows, :] = _dot(ui[rows, :].astype(BF16), b_ref[...])
            sb[rows, :] = _dot(dyi[rows, :].astype(BF16), ct_ref[...])
        _full_scan(sa, fin, ar, ai, nk, reverse=False, carry_in=carry_ref)
        gr, gi, accr, acci = _full_scan(sb, fin, ar, ai, nk, reverse=True, prev=sa)
        rowid = lax.broadcasted_iota(jnp.int32, (8, W), 0)
        last = pl.ds((nk - 1) * 8, 8)
        pr = jnp.where(rowid == 0, 0.0, pltpu.roll(sa[last, 0:W], 1, 0))
        pi = jnp.where(rowid == 0, 0.0, pltpu.roll(sa[last, W:2 * W], 1, 0))
        accr = accr + gr * pr + gi * pi
        acci = acci + gi * pr - gr * pi
        dar_ref[...] = jnp.sum(accr, axis=0, keepdims=True)
        dai_ref[...] = jnp.sum(acci, axis=0, keepdims=True)
        dbf = jnp.zeros((GC, 2 * W), F32)
        dcf = jnp.zeros((GC, 2 * W), F32)
        dd = jnp.zeros((1, GC), F32)
        for r in range(8):
            rows = slice(r * nk, (r + 1) * nk)
            ub = ui[rows, :]
            dyb = dyi[rows, :]
            gb = sb[rows, :].astype(BF16)
            dui[rows, :] = _dot_nt(gb, b_ref[...]) + d_ref[...] * dyb
            dbf = dbf + _dot_tn(ub.astype(BF16), gb)
            dcf = dcf + _dot_tn(dyb.astype(BF16), sa[rows, :].astype(BF16))
            dd = dd + jnp.sum(dyb * ub, axis=0, keepdims=True)
        db_ref[...] = jnp.dot(dbf * mask_ref[...], r_ref[...], precision=hi, preferred_element_type=F32)
        dc_ref[...] = jnp.dot(dcf * mask_ref[...], r_ref[...], precision=hi, preferred_element_type=F32)
        dd_ref[...] = dd
        _deinterleave(du_ref, dui, nk)

    cmp_spec = pl.BlockSpec((GC, 2 * S5_P), lambda g: (g, 0))
    whole = lambda shape: pl.BlockSpec(shape, lambda g: (0, 0))
    sd = jax.ShapeDtypeStruct
    return _hosting_call(
        body, "s5_bwd", S5_G // S5_GB, host,
        [u, dy, carry, bm, cm, ar, ai, dvec, mask, rmat],
        [col, col, pl.BlockSpec((8, 2 * W), lambda g: (g, 0)), bmat, bmat, avec, avec, vec, whole(mask.shape),
         whole(rmat.shape)],
        [sd(u.shape, BF16), sd((S5_G * S5_C, 2 * S5_P), F32), sd((S5_G * S5_C, 2 * S5_P), F32),
         sd((1, PRIM), F32), sd((1, S5_G * S5_P), F32), sd((1, S5_G * S5_P), F32)],
        [col, cmp_spec, cmp_spec, vec, avec, avec],
        [pltpu.VMEM((L, 2 * W), F32), pltpu.VMEM((L, 2 * W), F32), pltpu.VMEM((8, 2 * W), F32),
         pltpu.VMEM((L, GC), F32), pltpu.VMEM((L, GC), F32), pltpu.VMEM((L, GC), F32)])


def _s5_compact_consts():
    g_row = np.arange(S5_GB * S5_C) // S5_C
    col = np.arange(2 * S5_W)
    g_col = (col % S5_W) // S5_P
    mask = (g_row[:, None] == g_col[None, :]).astype(np.float32)
    tgt = (col // S5_W) * S5_P + col % S5_P
    rmat = (tgt[:, None] == np.arange(2 * S5_P)[None, :]).astype(np.float32)
    return jnp.asarray(mask), jnp.asarray(rmat)


def _attn_scores(q_ref, k_ref, qb, bq, scale):
    ext = (qb + 1) * bq
    s = _dot_nt(q_ref[qb * bq:ext, :], k_ref[0:ext, :]) * scale
    qpos = lax.broadcasted_iota(jnp.int32, (bq, bq), 0)
    kpos = lax.broadcasted_iota(jnp.int32, (bq, bq), 1)
    diag = jnp.where(kpos <= qpos, s[:, ext - bq:], NEG)
    return diag if qb == 0 else jnp.concatenate([s[:, :ext - bq], diag], axis=-1)


def _attn_fwd(qp, kp, v, scale):
    L = qp.shape[0]
    bq = min(256, L)

    def body(q_ref, k_ref, v_ref, o_ref, lse_ref):
        for qb in range(L // bq):
            rows = slice(qb * bq, (qb + 1) * bq)
            s = _attn_scores(q_ref, k_ref, qb, bq, scale)
            m = jnp.max(s, axis=-1, keepdims=True)
            e = jnp.exp(s - m)
            l = jnp.sum(e, axis=-1, keepdims=True)
            o_ref[rows, :] = _dot(e.astype(BF16), v_ref[0:(qb + 1) * bq, :]) / l
            lse_ref[rows, :] = jnp.broadcast_to(m + jnp.log(l), (bq, HD))

    blk = pl.BlockSpec((L, HD), lambda h: (0, h))
    wide = pl.BlockSpec((L, 2 * HD), lambda h: (0, h))
    return pl.pallas_call(
        body, name="mla_attn_fwd", grid=(MLA_H,),
        in_specs=[wide, wide, blk], out_specs=[blk, blk],
        out_shape=[jax.ShapeDtypeStruct((L, MLA_H * HD), F32)] * 2,
        compiler_params=pltpu.CompilerParams(dimension_semantics=("arbitrary",), vmem_limit_bytes=VMEM_LIMIT),
    )(qp, kp, v)


def _attn_bwd(qp, kp, v, o, lse, do, scale):
    L = qp.shape[0]
    bq = min(256, L)
    nq = L // bq

    def body(q_ref, k_ref, v_ref, o_ref, lse_ref, do_ref, dq_ref, dk_ref, dv_ref, dk_acc, dv_acc):
        dk_acc[...] = jnp.zeros_like(dk_acc)
        dv_acc[...] = jnp.zeros_like(dv_acc)
        for qb in range(nq):
            rows = slice(qb * bq, (qb + 1) * bq)
            ext = (qb + 1) * bq
            do = do_ref[rows, :]
            dob = do.astype(BF16)
            p = jnp.exp(_attn_scores(q_ref, k_ref, qb, bq, scale) - lse_ref[rows, 0:1])
            dp = _dot_nt(dob, v_ref[0:ext, :])
            dsum = jnp.sum(do * o_ref[rows, :], axis=-1, keepdims=True)
            ds = (p * (dp - dsum) * scale).astype(BF16)
            dq_ref[rows, :] = _dot(ds, k_ref[0:ext, :]).astype(dq_ref.dtype)
            dk_acc[0:ext, :] += _dot_tn(ds, q_ref[rows, :])
            dv_acc[0:ext, :] += _dot_tn(p.astype(BF16), dob)
        dk_ref[...] = dk_acc[...].astype(dk_ref.dtype)
        dv_ref[...] = dv_acc[...].astype(dv_ref.dtype)

    sd = jax.ShapeDtypeStruct
    blk = pl.BlockSpec((L, HD), lambda h: (0, h))
    wide = pl.BlockSpec((L, 2 * HD), lambda h: (0, h))
    return pl.pallas_call(
        body, name="mla_attn_bwd", grid=(MLA_H,),
        in_specs=[wide, wide, blk, blk, blk, blk], out_specs=[wide, wide, blk],
        out_shape=[sd((L, MLA_H * 2 * HD), BF16), sd((L, MLA_H * 2 * HD), BF16), sd((L, MLA_H * HD), BF16)],
        scratch_shapes=[pltpu.VMEM((L, 2 * HD), F32), pltpu.VMEM((L, HD), F32)],
        compiler_params=pltpu.CompilerParams(dimension_semantics=("arbitrary",), vmem_limit_bytes=VMEM_LIMIT),
    )(qp, kp, v, o, lse, do)


def _kv_fn(mem, gm, w, gk):
    kv = _mm(_rms(mem, gm, D_MODEL), w)
    k = jnp.concatenate([_rms(kv[:, HD * h:HD * (h + 1)], gk, HD) for h in range(X_HEADS)], axis=-1)
    return k, kv[:, XQ:]


def _kv_prep(mem, gm, w, gk, name):
    def fn(mem, gm, w, gk):
        return _kv_fn(mem, gm, w, gk)
    M = mem.shape[0]
    return _rowwise(name, fn, [('c', mem), ('c', gm), ('c', w), ('c', gk)],
                    [('c', (M, XQ), F32), ('c', (M, XQ), F32)], 1)


def _kv_prep_bwd(mem, gm, w, gk, dk, dv, name):
    def fn(mem, gm, w, gk, dk, dv):
        _, vjp = jax.vjp(lambda a, b, c: _kv_fn(mem, a, b, c), gm, w, gk)
        return vjp((dk, dv))
    return _rowwise(name, fn, [('c', mem), ('c', gm), ('c', w), ('c', gk), ('c', dk), ('c', dv)],
                    [('c', gm.shape, F32), ('c', w.shape, BF16), ('c', gk.shape, F32)], 1)


def _forward_merge(x, mix, mix_kind, xq, gate, k, v, gq, wout, name, nblk, host=None):
    def fn(x, mix, xq, gate, k, v, gq, wout):
        o = _merge(mix, xq, gate, k, v, gq)
        return (x + _dot(o.astype(BF16), wout),)
    L = x.shape[0]
    out = _rowwise(name, fn, [('r', x), (mix_kind, mix), ('r', xq), ('r', gate), ('c', k), ('c', v), ('c', gq),
                              ('c', wout)], [('r', (L, D_MODEL), F32)], nblk, host=host)
    return out[0] if host is None else (out[0][0], out[1])


def _backward_merge(dx, mix, mix_kind, xq, gate, k, v, gq, wout, name, nblk, host=None):
    def fn(dx, mix, xq, gate, k, v, gq, wout):
        g16 = dx.astype(BF16)
        do = _dot_nt(g16, wout)
        o, vjp = jax.vjp(_merge, mix, xq, gate, k, v, gq)
        dmix, dxq, dgate, dk, dv, dgq = vjp(do)
        return dmix, dxq, dgate, o, g16, dk, dv, dgq
    L = dx.shape[0]
    return _rowwise(
        name, fn,
        [('r', dx), (mix_kind, mix), ('r', xq), ('r', gate), ('c', k), ('c', v), ('c', gq), ('c', wout)],
        [('r', (L, PRIM), F32), ('r', (L, XQ), BF16), ('r', (L, BRANCH), BF16), ('t', (BRANCH, L), BF16),
         ('r', (L, D_MODEL), BF16), ('a', k.shape, F32), ('a', v.shape, F32), ('a', gq.shape, F32)], nblk,
        host=host)


_MLA_IN = 3392
_MLA_IN_PAD = 3456


def _uq_rows(wt):
    r = wt.reshape(MLA_H, HD + ROPE, wt.shape[1])
    return jnp.concatenate([r[:, :HD].reshape(PRIM, -1),
                            jnp.pad(r[:, HD:], ((0, 0), (0, HD - ROPE), (0, 0))).reshape(PRIM, -1)], axis=0)


def _uq_rows_back(wt):
    nope = wt[:PRIM].reshape(MLA_H, HD, -1)
    rope = wt[PRIM:].reshape(MLA_H, HD, -1)[:, :ROPE]
    return jnp.concatenate([nope, rope], axis=1).reshape(MLA_H * (HD + ROPE), -1)


def _mla_in_rows_back(wt):
    return jnp.concatenate([wt[:768], wt[3328:3392], wt[768:3328]], axis=0)


_SMALL = (("ln_gain", 2048), ("mem_norm", 2048), ("xq_norm", 256), ("xk_norm", 256), ("s5_lambda_re", 6144),
          ("s5_lambda_im", 6144), ("s5_log_step", 96), ("s5_b_re", 98304), ("s5_b_im", 98304), ("s5_c_re", 98304),
          ("s5_c_im", 98304), ("s5_d", 1536), ("mla_q_lora_norm", 512), ("mla_kv_lora_norm", 256),
          ("mla_q_nope_norm", 128), ("mla_k_nope_norm", 128), ("mla_q_rope_norm", 64), ("mla_k_rope_norm", 64))
_SMALL_ROWS = 432
_SMALL_OFF = {name: sum(n for _, n in _SMALL[:i]) for i, (name, _) in enumerate(_SMALL)}


def _pack_small(d):
    flat = jnp.concatenate([d[n].reshape(-1).astype(F32) for n, _ in _SMALL])
    return jnp.pad(flat, (0, _SMALL_ROWS * 1024 - flat.shape[0])).reshape(_SMALL_ROWS, 1024)


def _unpack_small(p, name, shape):
    off = _SMALL_OFF[name]
    return p.reshape(-1)[off:off + int(np.prod(shape))].reshape(shape)


_WEIGHTS = ('ln_gain', 'w_out', 'mem_norm', 'w_mem_kv', 'xq_norm', 'xk_norm', 's5_w_in', 's5_lambda_re',
            's5_lambda_im', 's5_log_step', 's5_b_re', 's5_b_im', 's5_c_re', 's5_c_im', 's5_d', 's5_w_glu', 'mla_w_in',
            'mla_q_lora_norm', 'mla_kv_lora_norm', 'mla_w_uq', 'mla_w_ukv', 'mla_q_nope_norm', 'mla_k_nope_norm',
            'mla_q_rope_norm', 'mla_k_rope_norm')


def _pad128(g):
    return jnp.pad(g.reshape(1, -1), ((0, 0), (0, HD - g.shape[-1])))


def kernel(x, mem, positions, ln_gain, w_out, mem_norm, w_mem_kv, xq_norm, xk_norm, s5_w_in, s5_lambda_re, s5_lambda_im, s5_log_step, s5_b_re, s5_b_im, s5_c_re, s5_c_im, s5_d, s5_w_glu, mla_w_in, mla_q_lora_norm, mla_kv_lora_norm, mla_w_uq, mla_w_ukv, mla_q_nope_norm, mla_k_nope_norm, mla_q_rope_norm, mla_k_rope_norm, loss_target, m_ln_gain, m_w_out, m_mem_norm, m_w_mem_kv, m_xq_norm, m_xk_norm, m_s5_w_in, m_s5_lambda_re, m_s5_lambda_im, m_s5_log_step, m_s5_b_re, m_s5_b_im, m_s5_c_re, m_s5_c_im, m_s5_d, m_s5_w_glu, m_mla_w_in, m_mla_q_lora_norm, m_mla_kv_lora_norm, m_mla_w_uq, m_mla_w_ukv, m_mla_q_nope_norm, m_mla_k_nope_norm, m_mla_q_rope_norm, m_mla_k_rope_norm, v_ln_gain, v_w_out, v_mem_norm, v_w_mem_kv, v_xq_norm, v_xk_norm, v_s5_w_in, v_s5_lambda_re, v_s5_lambda_im, v_s5_log_step, v_s5_b_re, v_s5_b_im, v_s5_c_re, v_s5_c_im, v_s5_d, v_s5_w_glu, v_mla_w_in, v_mla_q_lora_norm, v_mla_kv_lora_norm, v_mla_w_uq, v_mla_w_ukv, v_mla_q_nope_norm, v_mla_k_nope_norm, v_mla_q_rope_norm, v_mla_k_rope_norm):
    weights = dict(ln_gain=ln_gain, w_out=w_out, mem_norm=mem_norm, w_mem_kv=w_mem_kv, xq_norm=xq_norm,
                   xk_norm=xk_norm, s5_w_in=s5_w_in, s5_lambda_re=s5_lambda_re, s5_lambda_im=s5_lambda_im,
                   s5_log_step=s5_log_step, s5_b_re=s5_b_re, s5_b_im=s5_b_im, s5_c_re=s5_c_re, s5_c_im=s5_c_im,
                   s5_d=s5_d, s5_w_glu=s5_w_glu, mla_w_in=mla_w_in, mla_q_lora_norm=mla_q_lora_norm,
                   mla_kv_lora_norm=mla_kv_lora_norm, mla_w_uq=mla_w_uq, mla_w_ukv=mla_w_ukv,
                   mla_q_nope_norm=mla_q_nope_norm, mla_k_nope_norm=mla_k_nope_norm,
                   mla_q_rope_norm=mla_q_rope_norm, mla_k_rope_norm=mla_k_rope_norm)
    m_in = dict(zip(_WEIGHTS, (m_ln_gain, m_w_out, m_mem_norm, m_w_mem_kv, m_xq_norm, m_xk_norm, m_s5_w_in,
                               m_s5_lambda_re, m_s5_lambda_im, m_s5_log_step, m_s5_b_re, m_s5_b_im, m_s5_c_re,
                               m_s5_c_im, m_s5_d, m_s5_w_glu, m_mla_w_in, m_mla_q_lora_norm, m_mla_kv_lora_norm,
                               m_mla_w_uq, m_mla_w_ukv, m_mla_q_nope_norm, m_mla_k_nope_norm, m_mla_q_rope_norm,
                               m_mla_k_rope_norm)))
    v_in = dict(zip(_WEIGHTS, (v_ln_gain, v_w_out, v_mem_norm, v_w_mem_kv, v_xq_norm, v_xk_norm, v_s5_w_in,
                               v_s5_lambda_re, v_s5_lambda_im, v_s5_log_step, v_s5_b_re, v_s5_b_im, v_s5_c_re,
                               v_s5_c_im, v_s5_d, v_s5_w_glu, v_mla_w_in, v_mla_q_lora_norm, v_mla_kv_lora_norm,
                               v_mla_w_uq, v_mla_w_ukv, v_mla_q_nope_norm, v_mla_k_nope_norm, v_mla_q_rope_norm,
                               v_mla_k_rope_norm)))

    x0 = x[0]
    mem0 = mem[0]
    target = loss_target[0]
    L = x0.shape[0]
    nblk = 4
    nb_big = 8
    me = 4 * lax.axis_index("x") + 2 * lax.axis_index("y") + lax.axis_index("c")

    lora = jnp.pad(jnp.concatenate([mla_q_lora_norm, mla_kv_lora_norm], axis=1), ((0, 7), (0, HD - 96)))
    def gather(*shards):
        return _plan_all_gather(list(shards))

    kh = D_MODEL // 2
    (b_mkv0, b_glu, b_in_mla, b_out0, b_uq, b_ukv, b_mkv1, b_out1), (W_in_s5,) = _cast_call(
        [w_mem_kv[0], s5_w_glu[0], jnp.transpose(mla_w_in[0]), w_out[0], jnp.transpose(mla_w_uq[0]), mla_w_ukv[0],
         w_mem_kv[1], w_out[1]], "cast_shards", host=gather(s5_w_in[0].astype(BF16)))

    ln0, ln1 = ln_gain[0:1], ln_gain[1:2]
    gq0, gq1 = xq_norm[0:1], xq_norm[1:2]
    gk0, gk1 = xk_norm[0:1], xk_norm[1:2]
    gm0, gm1 = mem_norm[0:1], mem_norm[1:2]
    gqn, gkn = mla_q_nope_norm, mla_k_nope_norm
    gqr, gkr = _pad128(mla_q_rope_norm), _pad128(mla_k_rope_norm)

    lr3 = s5_lambda_re.reshape(S5_G, 1, S5_P)
    li3 = s5_lambda_im.reshape(S5_G, 1, S5_P)
    ls3 = s5_log_step.reshape(S5_G, 1, 1)
    btr = jnp.swapaxes(s5_b_re[0], 1, 2)
    bti = jnp.swapaxes(s5_b_im[0], 1, 2)
    a_r, a_i, bm, cm = _s5_params(lr3, li3, ls3, btr, bti, s5_c_re[0], s5_c_im[0])
    a_r2 = a_r.reshape(1, S5_G * S5_P)
    a_i2 = a_i.reshape(1, S5_G * S5_P)
    cmask, rmat = _s5_compact_consts()

    half = ROPE // 2
    inv_freq = ROPE_THETA ** (-jnp.arange(half, dtype=F32) / half)
    invf = jnp.concatenate([inv_freq, inv_freq, jnp.zeros((HD - ROPE,), F32)]).reshape(1, HD)

    def rot_tables(pos, invf):
        ang = pos.astype(F32) * invf
        lane = lax.broadcasted_iota(jnp.int32, ang.shape, 1)
        c = jnp.where(lane < ROPE, jnp.cos(ang), 0.0)
        s = jnp.sin(ang)
        return c, jnp.where(lane < half, -s, 0.0), jnp.where((lane >= half) & (lane < ROPE), s, 0.0)

    tc, ts1, ts2 = _rowwise("rot_tables", rot_tables, [('r', positions.reshape(L, 1)), ('c', invf)],
                            [('r', (L, HD), F32)] * 3, nblk)

    def in_s5(x, g, w):
        proj = _mm_slots(_rms(x, g, D_MODEL).astype(BF16), w)
        return proj[:, :PRIM], proj[:, PRIM:PRIM + XQ], proj[:, PRIM + XQ:]

    u_s5, xq_a, gate_a = _rowwise(
        "s5_in", in_s5, [('r', x0), ('c', ln0), ('c', W_in_s5)],
        [('r', (L, PRIM), F32), ('r', (L, XQ), F32), ('r', (L, BRANCH), F32)], nblk)
    (y_s5, s5_carry), (W_glu, G_mkv0, G_in_mla_a) = _s5_fwd(u_s5, bm, cm, a_r2, a_i2, s5_d,
                                                            host=gather(b_glu, b_mkv0, b_in_mla[:, :kh]))

    def glu(y, w):
        z = _mm_slots(_gelu(y).astype(BF16), w)
        return z[:, :PRIM] * _sigmoid(z[:, PRIM:]), z

    (y2, z_glu), (G_out0,) = _rowwise("s5_glu", glu, [('r', y_s5), ('c', W_glu)],
                                      [('r', (L, PRIM), F32), ('r', (L, 2 * PRIM), F32)], nblk, host=gather(b_out0))
    W_mkv0 = G_mkv0.reshape(D_MODEL, 2 * XQ)
    k_a, v_a = _kv_prep(mem0, gm0, W_mkv0, gk0, "kv_prep0")
    x1, (G_in_mla_b,) = _forward_merge(
        x0, y2, 'r', xq_a, gate_a, k_a, v_a, gq0, G_out0.reshape(BRANCH, D_MODEL), "merge0", nblk,
        host=gather(b_in_mla[:, kh:]))
    W_in_mla = jnp.concatenate([G_in_mla_a, G_in_mla_b], axis=2).reshape(_MLA_IN, D_MODEL)

    def in_mla(x, g, w):
        xn = _rms(x, g, D_MODEL).astype(BF16)
        a = _dot_nt(xn, w[0:768])
        kx = _dot_nt(xn, w[768:896])
        b = _dot_nt(xn, w[832:_MLA_IN])
        lane = lax.broadcasted_iota(jnp.int32, kx.shape, 1)
        return a[:, :512], a[:, 512:], b[:, :XQ], b[:, XQ:], jnp.where(lane < ROPE, kx, 0.0)

    (c_q, c_kv, xq_b, gate_b, krp), (G_uq, W_kv, G_lora) = _rowwise(
        "mla_in", in_mla, [('r', x1), ('c', ln1), ('c', W_in_mla)],
        [('r', (L, Q_LORA), F32), ('r', (L, KV_LORA), F32), ('r', (L, XQ), F32), ('r', (L, BRANCH), F32),
         ('r', (L, HD), F32)], nblk,
        host=gather(b_uq, b_ukv, lora))
    W_q = _uq_rows(G_uq.reshape(MLA_H * (HD + ROPE), Q_LORA))
    g_qlora = G_lora[:, 0, :64].reshape(1, Q_LORA)
    g_kvlora = G_lora[:, 0, 64:96].reshape(1, KV_LORA)

    def qkv(c_q, c_kv, krp, tc, ts1, ts2, gql, gkvl, wq, wkv, gqn, gkn, gqr, gkr):
        q = _dot_nt(_rms(c_q, gql, Q_LORA).astype(BF16), wq)
        kv = _mm_slots(_rms(c_kv, gkvl, KV_LORA).astype(BF16), wkv)
        kp, v = _kv_post(*_kv_chunks(kv), krp, gkn, gkr, tc, ts1, ts2)
        return _q_post(*_q_chunks(q), gqn, gqr, tc, ts1, ts2), kp, v

    qkv_consts = [('c', g_qlora), ('c', g_kvlora), ('c', W_q), ('c', W_kv), ('c', gqn), ('c', gkn), ('c', gqr),
                  ('c', gkr)]
    (q_pad, k_pad, v_h), (G_mkv1, G_out1) = _rowwise(
        "mla_qkv", qkv, [('r', c_q), ('r', c_kv), ('r', krp), ('r', tc), ('r', ts1), ('r', ts2)] + qkv_consts,
        [('r', (L, 2 * PRIM), BF16), ('r', (L, 2 * PRIM), BF16), ('r', (L, PRIM), BF16)], nblk,
        host=gather(b_mkv1, b_out1))
    W_out = (G_out0.reshape(BRANCH, D_MODEL), G_out1.reshape(BRANCH, D_MODEL))
    W_mkv = (W_mkv0, G_mkv1.reshape(D_MODEL, 2 * XQ))
    scale = (HD + ROPE) ** -0.5
    attn, lse = _attn_fwd(q_pad, k_pad, v_h, scale)
    k_b, v_b = _kv_prep(mem0, gm1, W_mkv[1], gk1, "kv_prep1")

    def merge_loss(x, mix, xq, gate, k, v, gq, wout, t):
        err = x + _dot(_merge(mix, xq, gate, k, v, gq).astype(BF16), wout) - t
        part = 0.5 * jnp.sum(jnp.sum(err * err, axis=-1, keepdims=True) * (1.0 / D_MODEL), axis=0, keepdims=True)
        return err * (1.0 / D_MODEL), jnp.broadcast_to(part, (1, HD))

    dx2, loss_part = _rowwise(
        "merge1_loss", merge_loss,
        [('r', x1), ('r', attn), ('r', xq_b), ('r', gate_b), ('c', k_b), ('c', v_b), ('c', gq1), ('c', W_out[1]),
         ('r', target)], [('r', (L, D_MODEL), F32), ('a', (1, HD), F32)], nblk)

    dattn, dxq_b, dgate_b, o_b, g_b, dk_b, dv_b, dgq1 = _backward_merge(
        dx2, attn, 'r', xq_b, gate_b, k_b, v_b, gq1, W_out[1], "merge1_bwd", nb_big)
    dgm1, dW_mkv1, dgk1 = _kv_prep_bwd(mem0, gm1, W_mkv[1], gk1, dk_b, dv_b, "kv_prep1_bwd")
    dW_out1 = _matmul_tn(o_b, g_b, "dw_out1")
    dq_pad, dk_pad, dv_h = _attn_bwd(q_pad, k_pad, v_h, attn, lse, dattn, scale)

    def qkv_bwd(c_q, c_kv, krp, tc, ts1, ts2, dqp, dkp, dv, gql, gkvl, wq, wkv, gqn, gkn, gqr, gkr):
        cqn, vjp_qn = jax.vjp(lambda a, b: _rms(a, b, Q_LORA), c_q, gql)
        ckvn, vjp_kvn = jax.vjp(lambda a, b: _rms(a, b, KV_LORA), c_kv, gkvl)
        cqn16 = cqn.astype(BF16)
        ckvn16 = ckvn.astype(BF16)
        q = _dot_nt(cqn16, wq)
        kv = _mm_slots(ckvn16, wkv)
        _, vjp_q = jax.vjp(lambda n, r, a, b: _q_post(n, r, a, b, tc, ts1, ts2), *_q_chunks(q), gqn, gqr)
        dnope, drope, dgqn, dgqr = vjp_q(dqp.astype(F32))
        dq = jnp.concatenate(dnope + drope, axis=-1)
        _, vjp_kv = jax.vjp(lambda n, v, k, a, b: _kv_post(n, v, k, a, b, tc, ts1, ts2), *_kv_chunks(kv), krp, gkn,
                            gkr)
        dkn, dvals, dkrp, dgkn, dgkr = vjp_kv((dkp.astype(F32), dv.astype(F32)))
        dkv = jnp.concatenate([x for pair in zip(dkn, dvals) for x in pair], axis=-1)
        dq16 = dq.astype(BF16)
        dkv16 = dkv.astype(BF16)
        dc_q, dgql = vjp_qn(_dot(dq16, wq))
        dc_kv, dgkvl = vjp_kvn(_mm_slots_nt(dkv16, wkv))
        return dc_q, dc_kv, dkrp, cqn16, dq16, ckvn16, dkv16, dgql, dgkvl, dgqn, dgkn, dgqr, dgkr

    (dc_q, dc_kv, dkrp, cqn16, dq16, ckvn16, dkv16, dgql, dgkvl, dgqn, dgkn, dgqr, dgkr) = _rowwise(
        "mla_qkv_bwd", qkv_bwd,
        [('r', c_q), ('r', c_kv), ('r', krp), ('r', tc), ('r', ts1), ('r', ts2), ('r', dq_pad), ('r', dk_pad),
         ('r', dv_h)] + qkv_consts,
        [('r', (L, Q_LORA), BF16), ('r', (L, KV_LORA), BF16), ('r', (L, HD), BF16), ('r', (L, Q_LORA), BF16),
         ('t', (2 * PRIM, L), BF16), ('t', (KV_LORA, L), BF16), ('r', (L, 2 * PRIM), BF16),
         ('a', (1, Q_LORA), F32), ('a', (1, KV_LORA), F32), ('a', (1, HD), F32), ('a', (1, HD), F32),
         ('a', (1, HD), F32), ('a', (1, HD), F32)], nb_big)
    dW_q = _matmul_tn(dq16, cqn16, "dw_uq")
    dW_kv = _matmul_tn_slots(ckvn16, dkv16, "dw_ukv")

    def in_bwd(x, dres, g, w, *dparts):
        dproj = jnp.concatenate(dparts, axis=-1).astype(BF16)
        xn, vjp = jax.vjp(lambda a, b: _rms(a, b, D_MODEL), x, g)
        if w.ndim == 3:
            dxn = _mm_slots_nt(dproj, w)
        else:
            dkr = dproj[:, 3328:]
            dkr = jnp.where(lax.broadcasted_iota(jnp.int32, dkr.shape, 1) < ROPE, dkr, jnp.zeros_like(dkr))
            dxn = _dot(dproj[:, :768], w[0:768]) + _dot(dproj[:, 768:3328], w[832:_MLA_IN]) + _dot(dkr, w[768:896])
        dx, dg = vjp(dxn)
        return dx + dres, xn, dproj, dg

    dx1, xn1, dproj1, dln1 = _rowwise(
        "mla_in_bwd", in_bwd,
        [('r', x1), ('r', dx2), ('c', ln1), ('c', W_in_mla), ('r', dc_q), ('r', dc_kv), ('r', dxq_b), ('r', dgate_b),
         ('r', dkrp)],
        [('r', (L, D_MODEL), F32), ('r', (L, D_MODEL), BF16), ('t', (_MLA_IN_PAD, L), BF16), ('a', (1, D_MODEL), F32)],
        nblk)
    dW_in_mla = _matmul_tn(dproj1, xn1, "dw_mla_in")

    grads1 = [dW_out1.reshape(N_DEV, 256, D_MODEL), dW_mkv1.reshape(N_DEV, 128, 2 * XQ),
              _mla_in_rows_back(dW_in_mla).reshape(N_DEV, 424, D_MODEL),
              _uq_rows_back(dW_q).reshape(N_DEV, 288, Q_LORA), dW_kv]
    (dy2, dxq_a, dgate_a, o_a, g_a, dk_a, dv_a, dgq0), pair1 = _backward_merge(
        dx1, y2, 'r', xq_a, gate_a, k_a, v_a, gq0, W_out[0], "merge0_bwd", nb_big, host=_plan_pair(grads1))
    dgm0, dW_mkv0, dgk0 = _kv_prep_bwd(mem0, gm0, W_mkv[0], gk0, dk_a, dv_a, "kv_prep0_bwd")
    dW_out0 = _matmul_tn(o_a, g_a, "dw_out0")
    t1 = list(_pair_add(grads1, pair1, "rs_add_layer1"))

    def glu_bwd(y, z, dy2, w):
        h, vjp_h = jax.vjp(_gelu, y)
        _, vjp_z = jax.vjp(lambda a, b: a * _sigmoid(b), z[:, :PRIM], z[:, PRIM:])
        dz16 = jnp.concatenate(vjp_z(dy2), axis=-1).astype(BF16)
        return vjp_h(_mm_slots_nt(dz16, w))[0], h.astype(BF16), dz16

    grads0 = [dW_out0.reshape(N_DEV, 256, D_MODEL), dW_mkv0.reshape(N_DEV, 128, 2 * XQ)]
    (dy_s5, h16, dz16), glu_hosted = _rowwise(
        "s5_glu_bwd", glu_bwd, [('r', y_s5), ('r', z_glu), ('r', dy2), ('c', W_glu)],
        [('r', (L, PRIM), F32), ('t', (PRIM, L), BF16), ('r', (L, 2 * PRIM), BF16)], nb_big,
        host=_combine(_plan_chips(t1[2:3]), _plan_pair(grads0)))
    recv_in_mla, pair0 = glu_hosted[:1], glu_hosted[1:]
    dW_glu = _matmul_tn_slots(h16, dz16, "dw_glu")
    t0 = list(_pair_add(grads0 + [dW_glu], pair0 + list(_exchange_call(_plan_pair([dW_glu]), "rs_pair_glu")),
                        "rs_add_layer0"))
    both = [jnp.concatenate([t0[i], t1[i]], axis=1) for i in range(2)]
    (du_s5, dbc, dcc, dd, dar, dai), recv_rest = _s5_bwd(u_s5, dy_s5, s5_carry, bm, cm, a_r2, a_i2, s5_d,
                                                        cmask, rmat, host=_plan_chips(both + t1[3:] + t0[2:]))
    early_recv = recv_rest[:2] + recv_in_mla + recv_rest[2:]
    dbc4 = dbc.reshape(S5_G, S5_C, 2, S5_P)
    dcc4 = dcc.reshape(S5_G, S5_C, 2, S5_P)
    dlr, dli, dls, dbtr, dbti = _s5_params_bwd(
        lr3, li3, ls3, btr, bti, dar.reshape(S5_G, 1, S5_P), dai.reshape(S5_G, 1, S5_P), dbc4[:, :, 0], dbc4[:, :, 1])

    small_part = {
        "ln_gain": jnp.concatenate([jnp.zeros_like(dln1), dln1]), "mem_norm": jnp.concatenate([dgm0, dgm1]),
        "xq_norm": jnp.concatenate([dgq0, dgq1]), "xk_norm": jnp.concatenate([dgk0, dgk1]),
        "s5_lambda_re": dlr, "s5_lambda_im": dli, "s5_log_step": dls,
        "s5_b_re": jnp.swapaxes(dbtr, 1, 2), "s5_b_im": jnp.swapaxes(dbti, 1, 2),
        "s5_c_re": dcc4[:, :, 0], "s5_c_im": -dcc4[:, :, 1], "s5_d": dd,
        "mla_q_lora_norm": dgql, "mla_kv_lora_norm": dgkvl, "mla_q_nope_norm": dgqn, "mla_k_nope_norm": dgkn,
        "mla_q_rope_norm": dgqr[:, :ROPE], "mla_k_rope_norm": dgkr[:, :ROPE],
    }
    loss8 = jnp.pad(loss_part, ((0, 7), (0, 0)))
    (dx0, xn0, dproj0, dln0), (small_gath, loss_g) = _rowwise(
        "s5_in_bwd", in_bwd,
        [('r', x0), ('r', dx1), ('c', ln0), ('c', W_in_s5), ('r', du_s5), ('r', dxq_a),
         ('r', dgate_a)],
        [('r', (L, D_MODEL), F32), ('t', (D_MODEL, L), BF16), ('r', (L, 2 * BRANCH), BF16), ('a', (1, D_MODEL), F32)],
        nblk, host=_plan_all_gather([_pack_small(small_part).astype(BF16), loss8]))
    dW_in_s5, (ln0_gath,) = _matmul_tn_slots(
        xn0, dproj0, "dw_s5_in", host=_plan_all_gather([jnp.pad(dln0, ((0, 7), (0, 0)))]))

    late = [dW_in_s5]
    late_t = _pair_add(late, list(_exchange_call(_plan_pair(late), "rs_pair_late")), "rs_add_late")
    late_recv = list(_exchange_call(_plan_chips(late_t), "rs_chips_late"))
    owners = ["w_out", "w_mem_kv", "mla_w_in", "mla_w_uq", "mla_w_ukv", "s5_w_glu", "s5_w_in"]
    flipped = ("mla_w_in", "mla_w_uq")

    def shard(d, n):
        a = d[n]
        return jnp.transpose(a[0]) if n in flipped else a.reshape(-1, a.shape[-1])

    upd, _ = _updates_call(
        early_recv + late_recv, [shard(weights, n) for n in owners], [shard(m_in, n) for n in owners],
        [shard(v_in, n) for n in owners], "update_big")
    grads, delta, new_m, new_v = {}, {}, {}, {}
    for n, res in zip(owners, upd):
        shape = weights[n].shape
        grads[n], delta[n], new_m[n], new_v[n] = (
            (jnp.transpose(r)[None] if n in flipped else r.reshape(shape)) for r in res)

    gs, loss_sum = _small_sum(small_gath, loss_g, ln0_gath, "small_sum")
    loss = loss_sum[0, 0]
    for n, _ in _SMALL:
        shape = weights[n].shape
        if n == "mla_q_lora_norm":
            grads[n] = lax.dynamic_slice(_unpack_small(gs, n, (Q_LORA,)), (me * 64,), (64,)).reshape(shape)
        elif n == "mla_kv_lora_norm":
            grads[n] = lax.dynamic_slice(_unpack_small(gs, n, (KV_LORA,)), (me * 32,), (32,)).reshape(shape)
        else:
            grads[n] = _unpack_small(gs, n, shape)

    def own(n, a):
        if a.ndim == 4:
            a = jnp.transpose(a, (0, 2, 3, 1))
        elif a.ndim == 3:
            a = jnp.transpose(a, (0, 2, 1))
        return a.reshape(a.shape[1:]) if a.ndim >= 3 else a

    def back(n, a):
        shape = weights[n].shape
        if len(shape) == 4:
            return jnp.transpose(a.reshape((1,) + a.shape), (0, 3, 1, 2))
        if len(shape) == 3:
            return jnp.transpose(a.reshape((1,) + a.shape), (0, 2, 1))
        return a.reshape(shape)

    wide = ("s5_b_re", "s5_b_im", "s5_c_re", "s5_c_im")
    for names, nb, call in (([n for n, _ in _SMALL if n not in wide], 1, "update_small"), (wide, 4, "update_s5_bc")):
        res = _adamw_multi([own(n, weights[n]) for n in names], [own(n, grads[n]) for n in names],
                           [own(n, m_in[n]) for n in names], [own(n, v_in[n]) for n in names], call, nb)
        for n, (dl, m2, v2) in zip(names, res):
            delta[n], new_m[n], new_v[n] = back(n, dl), back(n, m2), back(n, v2)
    return (loss, dx0[None], *[grads[n] for n in _WEIGHTS], *[delta[n] for n in _WEIGHTS],
            *[new_m[n] for n in _WEIGHTS], *[new_v[n] for n in _WEIGHTS])
```

```python
import functools
import math

import numpy as np
import jax
import jax.numpy as jnp
from jax import lax
from jax.experimental import pallas as pl
from jax.experimental.pallas import tpu as pltpu

F32 = jnp.float32
BF16 = jnp.bfloat16
EPS = 1e-6
NEG = float(np.finfo(np.float32).min)
MESH = pl.DeviceIdType.MESH

N_DEV = 8
D_MODEL = 1024
MEM_LEN = 256
XQ = 512
PRIM = 1536
BRANCH = 2048
X_HEADS = 4
HD = 128
S5_G = 96
S5_P = 64
S5_C = 16
S5_GB = 8
S5_W = S5_GB * S5_P
MLA_H = 12
ROPE = 64
Q_LORA = 512
KV_LORA = 256
ROPE_THETA = 10000.0

ADAM_LR = 0.001
ADAM_B1 = 0.9
ADAM_B2 = 0.999
ADAM_EPS = 1e-08
ADAM_WD = 0.01
ADAM_STEP = 10

VMEM_LIMIT = 56 * 1024 * 1024


def _dot(a, b):
    return jnp.dot(a, b, preferred_element_type=F32)


def _dot_nt(a, b):
    return lax.dot_general(a, b, (((1,), (1,)), ((), ())), preferred_element_type=F32)


def _dot_tn(a, b):
    return lax.dot_general(a, b, (((0,), (0,)), ((), ())), preferred_element_type=F32)


@jax.custom_vjp
def _mm(a, b):
    return _dot(a.astype(BF16), b.astype(BF16))


def _mm_fwd(a, b):
    return _mm(a, b), (a, b)


def _mm_bwd(res, g):
    a, b = res
    gb = g.astype(BF16)
    return _dot_nt(gb, b.astype(BF16)).astype(a.dtype), _dot_tn(a.astype(BF16), gb).astype(b.dtype)


_mm.defvjp(_mm_fwd, _mm_bwd)


@jax.custom_vjp
def _mm_nt(a, b):
    return _dot_nt(a.astype(BF16), b.astype(BF16))


def _mm_nt_fwd(a, b):
    return _mm_nt(a, b), (a, b)


def _mm_nt_bwd(res, g):
    a, b = res
    gb = g.astype(BF16)
    return _dot(gb, b.astype(BF16)).astype(a.dtype), _dot_tn(gb, a.astype(BF16)).astype(b.dtype)


_mm_nt.defvjp(_mm_nt_fwd, _mm_nt_bwd)


@jax.custom_vjp
def _softmax(s):
    m = jnp.max(s, axis=-1, keepdims=True)
    e = jnp.exp(s - m)
    return e / jnp.sum(e, axis=-1, keepdims=True)


def _softmax_fwd(s):
    p = _softmax(s)
    return p, p


def _softmax_bwd(p, g):
    return (p * (g - jnp.sum(p * g, axis=-1, keepdims=True)),)


_softmax.defvjp(_softmax_fwd, _softmax_bwd)


def _rms(x, g, n):
    ms = jnp.sum(x * x, axis=-1, keepdims=True) * (1.0 / n)
    return x * lax.rsqrt(ms + EPS) * g


def _sigmoid(x):
    return 1.0 / (1.0 + jnp.exp(-x))


def _silu(x):
    return x * _sigmoid(x)


def _gelu(x):
    c = math.sqrt(2.0 / math.pi)
    return 0.5 * x * (1.0 + jnp.tanh(c * (x + 0.044715 * (x * x * x))))


@jax.custom_vjp
def _rot(x, c, s1, s2):
    return x * c + pltpu.roll(x, 96, 1) * s1 + pltpu.roll(x, 32, 1) * s2


def _rot_fwd(x, c, s1, s2):
    return _rot(x, c, s1, s2), (c, s1, s2)


def _rot_bwd(res, g):
    c, s1, s2 = res
    dx = g * c + pltpu.roll(g * s1, 32, 1) + pltpu.roll(g * s2, 96, 1)
    return dx, jnp.zeros_like(c), jnp.zeros_like(s1), jnp.zeros_like(s2)


_rot.defvjp(_rot_fwd, _rot_bwd)


def _mem_attn(xq, k, v, gq):
    outs = []
    for h in range(X_HEADS):
        sl = slice(HD * h, HD * (h + 1))
        q = _rms(xq[:, sl], gq, HD)
        p = _softmax(_mm_nt(q, k[:, sl]) * (HD ** -0.5))
        outs.append(_mm(p, v[:, sl]))
    return jnp.concatenate(outs, axis=-1)


def _merge(mix, xq, gate, k, v, gq):
    return jnp.concatenate([mix, _mem_attn(xq, k, v, gq)], axis=-1) * _silu(gate)


def _q_chunks(q):
    return ([q[:, HD * h:HD * (h + 1)] for h in range(MLA_H)],
            [q[:, PRIM + HD * h:PRIM + HD * (h + 1)] for h in range(MLA_H)])


def _q_post(nope, rope, gqn, gqr, c, s1, s2):
    pieces = []
    for qn, qr in zip(nope, rope):
        pieces.append(_rms(qn, gqn, HD))
        pieces.append(_rot(_rms(qr, gqr, ROPE), c, s1, s2))
    return jnp.concatenate(pieces, axis=-1)


def _kv_chunks(kv):
    return ([kv[:, 2 * HD * h:2 * HD * h + HD] for h in range(MLA_H)],
            [kv[:, 2 * HD * h + HD:2 * HD * (h + 1)] for h in range(MLA_H)])


def _kv_post(kn, vals, krp, gkn, gkr, c, s1, s2):
    kr = _rot(_rms(krp, gkr, ROPE), c, s1, s2)
    pieces = []
    for k in kn:
        pieces.append(_rms(k, gkn, HD))
        pieces.append(kr)
    return jnp.concatenate(pieces, axis=-1), jnp.concatenate(vals, axis=-1)


def _rowwise(name, fn, ins, outs, nblk, host=None):
    n_in = len(ins)

    def spec(kind, shape):
        if kind == 'r':
            return pl.BlockSpec((shape[0] // nblk, shape[1]), lambda i: (i, 0))
        if kind == 't':
            return pl.BlockSpec((shape[0], shape[1] // nblk), lambda i: (0, i))
        zeros = (0,) * len(shape)
        return pl.BlockSpec(tuple(shape), lambda i: zeros)

    def body(*refs):
        i = pl.program_id(0)
        res = fn(*[r[...] for r in refs[:n_in]])
        for (kind, _, _), ref, val in zip(outs, refs[n_in:], res):
            if kind == 'a':
                @pl.when(i == 0)
                def _():
                    ref[...] = jnp.zeros_like(ref)
                ref[...] += val.astype(ref.dtype)
            elif kind == 't':
                ref[...] = val.astype(F32).T.astype(ref.dtype)
            else:
                ref[...] = val.astype(ref.dtype)

    res, hosted = _hosting_call(
        body, name, nblk, host, [a for _, a in ins], [spec(k, a.shape) for k, a in ins],
        [jax.ShapeDtypeStruct(tuple(s), d) for _, s, d in outs], [spec(k, s) for k, s, _ in outs], [])
    return res if host is None else (res, hosted)


def _matmul_tn(at, g, name, out_dtype=BF16, row_bands=None):
    K, L = at.shape
    N = g.shape[1]
    tn = next(t for t in (512, 384, 256, 128) if N % t == 0)
    bands = [(0, K)] if row_bands is None else row_bands
    rows_out = sum(n for _, n in bands)

    def body(a_ref, g_ref, o_ref):
        res = _dot(a_ref[...], g_ref[...]).astype(o_ref.dtype)
        row = 0
        for start, n in bands:
            o_ref[row:row + n, :] = res[start:start + n]
            row += n

    return pl.pallas_call(
        body, name=name, grid=(N // tn,),
        in_specs=[pl.BlockSpec((K, L), lambda n: (0, 0)), pl.BlockSpec((L, tn), lambda n: (0, n))],
        out_specs=pl.BlockSpec((rows_out, tn), lambda n: (0, n)),
        out_shape=jax.ShapeDtypeStruct((rows_out, N), out_dtype),
        compiler_params=pltpu.CompilerParams(dimension_semantics=("arbitrary",), vmem_limit_bytes=VMEM_LIMIT),
    )(at, g)


def _matmul_tn_slots(at, g, name, host=None):
    K, L = at.shape
    n = g.shape[1] // N_DEV

    def body(a_ref, g_ref, o_ref):
        o_ref[...] = _dot(a_ref[...], g_ref[...]).astype(o_ref.dtype)

    res, hosted = _hosting_call(
        body, name, N_DEV, host, [at, g],
        [pl.BlockSpec((K, L), lambda d: (0, 0)), pl.BlockSpec((L, n), lambda d: (0, d))],
        [jax.ShapeDtypeStruct((N_DEV, K, n), BF16)], [pl.BlockSpec((None, K, n), lambda d: (d, 0, 0))], [])
    return res[0] if host is None else (res[0], hosted)


def _mm_slots(a16, w):
    return jnp.concatenate([_dot(a16, w[d]) for d in range(N_DEV)], axis=-1)


def _mm_slots_nt(g16, w):
    n = w.shape[2]
    out = _dot_nt(g16[:, 0:n], w[0])
    for d in range(1, N_DEV):
        out = out + _dot_nt(g16[:, d * n:(d + 1) * n], w[d])
    return out


class _Exchange:
    def __init__(self, ins, outs, scratch, start, finish):
        self.ins, self.outs, self.scratch, self.start, self.finish = ins, outs, scratch, start, finish


def _xyc():
    return lax.axis_index("x"), lax.axis_index("y"), lax.axis_index("c")


def _plan_all_gather(xs):
    n = len(xs)

    def build(x_refs, out_refs, sems):
        send_sems, recv_sems, local_sems = sems
        x, y, c = _xyc()

        def copies(k, block, to, own=False):
            slot = 4 * block[0] + 2 * block[1] + block[2]
            return [pltpu.make_async_remote_copy(
                src_ref=x_refs[a] if own else out_refs[a].at[slot], dst_ref=out_refs[a].at[slot],
                send_sem=send_sems.at[k * n + a], recv_sem=recv_sems.at[k * n + a], device_id=to,
                device_id_type=MESH) for a in range(n)]

        mine = [pltpu.make_async_copy(x_refs[a], out_refs[a].at[4 * x + 2 * y + c], local_sems.at[a])
                for a in range(n)]
        return copies, mine, (x, y, c), [(1 - x, y), (x, 1 - y), (1 - x, 1 - y)]

    def first_copies(copies, me, chips):
        x, y, c = me
        first = copies(0, me, (x, y, 1 - c), own=True)
        for j, chip in enumerate(chips):
            first += copies(1 + j, me, (*chip, c), own=True)
        return first

    def start(x_refs, out_refs, sems):
        copies, mine, me, chips = build(x_refs, out_refs, sems)
        for cp in mine + first_copies(copies, me, chips):
            cp.start()

    def finish(x_refs, out_refs, sems):
        copies, mine, me, chips = build(x_refs, out_refs, sems)
        x, y, c = me
        passed = []
        for j, chip in enumerate(chips):
            for cp in copies(1 + j, (*chip, c), me):
                cp.wait_recv()
            fwd = copies(4 + j, (*chip, c), (x, y, 1 - c))
            for cp in fwd:
                cp.start()
            passed += fwd
        for cp in copies(0, (x, y, 1 - c), me):
            cp.wait_recv()
        for j, chip in enumerate(chips):
            for cp in copies(4 + j, (*chip, 1 - c), me):
                cp.wait_recv()
        for cp in first_copies(copies, me, chips) + passed:
            cp.wait_send()
        for cp in mine:
            cp.wait()

    return _Exchange(list(xs), [jax.ShapeDtypeStruct((N_DEV,) + a.shape, a.dtype) for a in xs],
                     [pltpu.SemaphoreType.DMA((7 * n,)), pltpu.SemaphoreType.DMA((7 * n,)),
                      pltpu.SemaphoreType.DMA((n,))], start, finish)


_CHIPS = ((0, 0), (0, 1), (1, 0), (1, 1))


def _plan_pair(sends):
    n = len(sends)

    def build(s_refs, o_refs, sems):
        send_sems, recv_sems = sems
        x, y, c = _xyc()
        return [pltpu.make_async_remote_copy(
            src_ref=s_refs[a].at[4 * px + 2 * py + 1 - c], dst_ref=o_refs[a].at[j],
            send_sem=send_sems.at[j * n + a], recv_sem=recv_sems.at[j * n + a], device_id=(x, y, 1 - c),
            device_id_type=MESH) for j, (px, py) in enumerate(_CHIPS) for a in range(n)]

    def start(s_refs, o_refs, sems):
        for cp in build(s_refs, o_refs, sems):
            cp.start()

    def finish(s_refs, o_refs, sems):
        for cp in build(s_refs, o_refs, sems):
            cp.wait_recv()
            cp.wait_send()

    return _Exchange(list(sends), [jax.ShapeDtypeStruct((4,) + a.shape[1:], a.dtype) for a in sends],
                     [pltpu.SemaphoreType.DMA((4 * n,)), pltpu.SemaphoreType.DMA((4 * n,))], start, finish)


def _plan_chips(ts):
    n = len(ts)
    flips = ((1, 0), (0, 1), (1, 1))

    def build(t_refs, o_refs, sems):
        send_sems, recv_sems, local_sems = sems
        x, y, c = _xyc()
        mine = 2 * x + y
        local = [pltpu.make_async_copy(t_refs[a].at[mine], o_refs[a].at[mine], local_sems.at[a]) for a in range(n)]
        remote = []
        for k, (fx, fy) in enumerate(flips):
            px = 1 - x if fx else x
            py = 1 - y if fy else y
            remote += [pltpu.make_async_remote_copy(
                src_ref=t_refs[a].at[2 * px + py], dst_ref=o_refs[a].at[mine],
                send_sem=send_sems.at[k * n + a], recv_sem=recv_sems.at[k * n + a], device_id=(px, py, c),
                device_id_type=MESH) for a in range(n)]
        return local, remote

    def start(t_refs, o_refs, sems):
        local, remote = build(t_refs, o_refs, sems)
        for cp in local + remote:
            cp.start()

    def finish(t_refs, o_refs, sems):
        local, remote = build(t_refs, o_refs, sems)
        for cp in remote:
            cp.wait_recv()
        for cp in remote:
            cp.wait_send()
        for cp in local:
            cp.wait()

    return _Exchange(list(ts), [jax.ShapeDtypeStruct(a.shape, a.dtype) for a in ts],
                     [pltpu.SemaphoreType.DMA((3 * n,)), pltpu.SemaphoreType.DMA((3 * n,)),
                      pltpu.SemaphoreType.DMA((n,))], start, finish)


def _combine(*plans):
    def parts(refs, attr):
        out, at = [], 0
        for p in plans:
            n = len(getattr(p, attr))
            out.append(refs[at:at + n])
            at += n
        return out

    def run(half):
        def go(ins, outs, sems):
            for p, a, o, s in zip(plans, parts(ins, "ins"), parts(outs, "outs"), parts(sems, "scratch")):
                getattr(p, half)(a, o, s)
        return go

    return _Exchange(sum((p.ins for p in plans), []), sum((p.outs for p in plans), []),
                     sum((p.scratch for p in plans), []), run("start"), run("finish"))


def _exchange_call(plan, name):
    n = len(plan.ins)

    def body(*refs):
        ins, outs, sems = refs[:n], refs[n:2 * n], refs[2 * n:]
        plan.start(ins, outs, sems)
        plan.finish(ins, outs, sems)

    return pl.pallas_call(
        body, name=name, out_shape=plan.outs,
        in_specs=[pl.BlockSpec(memory_space=pl.ANY)] * n, out_specs=[pl.BlockSpec(memory_space=pl.ANY)] * n,
        scratch_shapes=plan.scratch,
    )(*plan.ins)


def _slab_spec(lead, rows, cols, nb):
    if rows % (nb * 16) == 0:
        return pl.BlockSpec((lead, rows // nb, cols), lambda i: (0, i, 0))
    if cols % (nb * 128) == 0:
        return pl.BlockSpec((lead, rows, cols // nb), lambda i: (0, 0, i))
    return pl.BlockSpec((lead, rows, cols), lambda i: (0, 0, 0))


def _slab_spec2(rows, cols, nb):
    if rows % (nb * 16) == 0:
        return pl.BlockSpec((rows // nb, cols), lambda i: (i, 0))
    if cols % (nb * 128) == 0:
        return pl.BlockSpec((rows, cols // nb), lambda i: (0, i))
    return pl.BlockSpec((rows, cols), lambda i: (0, 0))


def _cast_call(arrays, name, host=None):
    n = len(arrays)
    nb = 8

    def body(*refs):
        for a in range(n):
            refs[n + a][...] = refs[a][...].astype(BF16)

    specs = [_slab_spec2(x.shape[0], x.shape[1], nb) for x in arrays]
    return _hosting_call(body, name, nb, host, list(arrays), specs,
                         [jax.ShapeDtypeStruct(x.shape, BF16) for x in arrays], specs, [])


def _pair_add(sends, fromsib, name):
    n = len(sends)
    nb = 8

    def body(*refs):
        c = lax.axis_index("c")
        for a in range(n):
            s_ref, f_ref, t_ref = refs[a], refs[n + a], refs[2 * n + a]
            for j in range(4):
                t_ref[j] = (s_ref[2 * j + c].astype(F32) + f_ref[j].astype(F32)).astype(t_ref.dtype)

    def spec(a, lead):
        return _slab_spec(lead, a.shape[1], a.shape[2], nb)

    return pl.pallas_call(
        body, name=name, grid=(nb,),
        in_specs=[spec(a, N_DEV) for a in sends] + [spec(a, 4) for a in fromsib],
        out_specs=[spec(a, 4) for a in fromsib],
        out_shape=[jax.ShapeDtypeStruct(a.shape, a.dtype) for a in fromsib],
        compiler_params=pltpu.CompilerParams(dimension_semantics=("arbitrary",), vmem_limit_bytes=VMEM_LIMIT),
    )(*sends, *fromsib)


def _adamw_vals(w, g, m, v):
    m2 = ADAM_B1 * m + (1.0 - ADAM_B1) * g
    v2 = ADAM_B2 * v + (1.0 - ADAM_B2) * (g * g)
    m_hat = m2 / (1.0 - ADAM_B1 ** ADAM_STEP)
    v_hat = v2 / (1.0 - ADAM_B2 ** ADAM_STEP)
    delta = -ADAM_LR * (m_hat / (jnp.sqrt(v_hat) + ADAM_EPS) + ADAM_WD * w)
    return delta, m2, v2


def _updates_call(recvs, ws, ms, vs, name, host=None):
    n = len(recvs)
    nb = 8

    def body(*refs):
        for a in range(n):
            r_ref, w_ref, m_ref, v_ref = refs[a], refs[n + a], refs[2 * n + a], refs[3 * n + a]
            g_ref, d_ref, m2_ref, v2_ref = refs[4 * n + 4 * a:4 * n + 4 * a + 4]
            g = r_ref[0].astype(F32)
            for d in range(1, r_ref.shape[0]):
                g = g + r_ref[d].astype(F32)
            dl, m2, v2 = _adamw_vals(w_ref[...], g, m_ref[...], v_ref[...])
            g_ref[...] = g
            d_ref[...] = dl
            m2_ref[...] = m2
            v2_ref[...] = v2

    def spec3(r):
        return _slab_spec(r.shape[0], r.shape[1], r.shape[2], nb)

    def spec2(w):
        return _slab_spec2(w.shape[0], w.shape[1], nb)

    res, hosted = _hosting_call(
        body, name, nb, host, list(recvs) + list(ws) + list(ms) + list(vs),
        [spec3(r) for r in recvs] + [spec2(w) for w in ws] * 3,
        [jax.ShapeDtypeStruct(w.shape, F32) for w in ws for _ in range(4)],
        [spec2(w) for w in ws for _ in range(4)], [])
    return [res[4 * a:4 * a + 4] for a in range(n)], hosted


def _small_sum(gath, loss_g, row0_g, name):
    _, R, C = gath.shape
    br = R // 3

    def body(g_ref, l_ref, r_ref, go_ref, lo_ref):
        g = g_ref[0].astype(F32)
        lsum = l_ref[0]
        for d in range(1, N_DEV):
            g = g + g_ref[d].astype(F32)
            lsum = lsum + l_ref[d]
        go_ref[...] = g
        lo_ref[...] = lsum

        @pl.when(pl.program_id(0) == 0)
        def _():
            row0 = r_ref[0]
            for d in range(1, N_DEV):
                row0 = row0 + r_ref[d]
            go_ref[0:8, :] = go_ref[0:8, :] + jnp.where(lax.broadcasted_iota(jnp.int32, row0.shape, 0) == 0, row0, 0.0)

    return pl.pallas_call(
        body, name=name, grid=(R // br,),
        in_specs=[pl.BlockSpec((N_DEV, br, C), lambda i: (0, i, 0)),
                  pl.BlockSpec((N_DEV, 8, HD), lambda i: (0, 0, 0)), pl.BlockSpec((N_DEV, 8, C), lambda i: (0, 0, 0))],
        out_specs=[pl.BlockSpec((br, C), lambda i: (i, 0)), pl.BlockSpec((8, HD), lambda i: (0, 0))],
        out_shape=[jax.ShapeDtypeStruct((R, C), F32), jax.ShapeDtypeStruct((8, HD), F32)],
        compiler_params=pltpu.CompilerParams(dimension_semantics=("arbitrary",)),
    )(gath, loss_g, row0_g)


def _adamw_multi(ws, gs, ms, vs, name, nblk=1):
    n = len(ws)

    def body(*refs):
        for a in range(n):
            dl, m2, v2 = _adamw_vals(refs[a][...], refs[n + a][...], refs[2 * n + a][...], refs[3 * n + a][...])
            refs[4 * n + 3 * a][...] = dl
            refs[4 * n + 3 * a + 1][...] = m2
            refs[4 * n + 3 * a + 2][...] = v2

    def spec(x):
        rest = (0,) * (x.ndim - 1)
        return pl.BlockSpec((x.shape[0] // nblk,) + tuple(x.shape[1:]), lambda i: (i,) + rest)

    res = pl.pallas_call(
        body, name=name, grid=(nblk,),
        in_specs=[spec(w) for w in ws] * 4, out_specs=[spec(w) for w in ws for _ in range(3)],
        out_shape=[jax.ShapeDtypeStruct(w.shape, F32) for w in ws for _ in range(3)],
        compiler_params=pltpu.CompilerParams(dimension_semantics=("arbitrary",), vmem_limit_bytes=VMEM_LIMIT),
    )(*ws, *gs, *ms, *vs)
    return [res[3 * a:3 * a + 3] for a in range(n)]


def _s5_param_fn(lr, li, ls, btr, bti):
    step = jnp.exp(ls)
    er = jnp.exp(lr * step)
    ang = li * step
    ar = er * jnp.cos(ang)
    ai = er * jnp.sin(ang)
    nr = ar - 1.0
    den = lr * lr + li * li
    fr = (nr * lr + ai * li) / den
    fi = (ai * lr - nr * li) / den
    return ar, ai, fr * btr - fi * bti, fr * bti + fi * btr


def _s5_params(lr, li, ls, btr, bti, cre, cim):
    nb = S5_G // S5_GB
    GC = S5_GB * S5_C
    expand = jnp.asarray(np.tile(np.eye(S5_P, dtype=np.float32), (1, S5_GB)), BF16)
    own = jnp.asarray((np.arange(GC)[:, None] // S5_C == np.arange(S5_W)[None, :] // S5_P).astype(np.float32))

    def body(lr_ref, li_ref, ls_ref, br_ref, bi_ref, cr_ref, ci_ref, e_ref, own_ref, ar_ref, ai_ref, bm_ref, cm_ref):
        ar, ai, bbr, bbi = _s5_param_fn(lr_ref[...], li_ref[...], ls_ref[...], br_ref[...], bi_ref[...])
        ar_ref[...] = ar
        ai_ref[...] = ai

        def plane(x, n):
            rows = x[n * S5_GB:(n + 1) * S5_GB].reshape(GC, S5_P).astype(BF16)
            return _dot(rows, e_ref[...]) * own_ref[...]

        for n in range(nb):
            bm_ref[n] = jnp.concatenate([plane(bbr, n), plane(bbi, n)], axis=-1).astype(BF16)
            cm_ref[n] = jnp.concatenate([plane(cr_ref[...], n), -plane(ci_ref[...], n)], axis=-1).astype(BF16)

    sd = jax.ShapeDtypeStruct
    return pl.pallas_call(
        body, name="s5_params",
        out_shape=[sd(lr.shape, F32), sd(lr.shape, F32), sd((nb, GC, 2 * S5_W), BF16), sd((nb, GC, 2 * S5_W), BF16)],
        compiler_params=pltpu.CompilerParams(vmem_limit_bytes=VMEM_LIMIT),
    )(lr, li, ls, btr, bti, cre, cim, expand, own)


def _s5_params_bwd(lr, li, ls, btr, bti, dar, dai, dbbr, dbbi):
    def body(lr_ref, li_ref, ls_ref, br_ref, bi_ref, dar_ref, dai_ref, dbbr_ref, dbbi_ref,
             dlr_ref, dli_ref, dls_ref, dbr_ref, dbi_ref):
        _, vjp = jax.vjp(_s5_param_fn, lr_ref[...], li_ref[...], ls_ref[...], br_ref[...], bi_ref[...])
        dlr, dli, dls, dbr, dbi = vjp((dar_ref[...], dai_ref[...], dbbr_ref[...], dbbi_ref[...]))
        dlr_ref[...] = dlr
        dli_ref[...] = dli
        dls_ref[...] = dls
        dbr_ref[...] = dbr
        dbi_ref[...] = dbi

    sd = jax.ShapeDtypeStruct
    return pl.pallas_call(
        body, name="s5_params_bwd",
        out_shape=[sd(lr.shape, F32), sd(lr.shape, F32), sd(ls.shape, F32), sd(btr.shape, F32), sd(btr.shape, F32)],
    )(lr, li, ls, btr, bti, dar, dai, dbbr, dbbi)


def _cpow(ar, ai, n):
    assert n & (n - 1) == 0
    while n > 1:
        ar, ai = ar * ar - ai * ai, 2.0 * ar * ai
        n //= 2
    return ar, ai


def _scan(st, cr, ci, init, nk, reverse, store, prev=None):
    W = S5_W

    def advance(k, sr, si):
        rows = pl.ds(k * 8 if isinstance(k, int) else pl.multiple_of(k * 8, 8), 8)
        nsr = cr * sr - ci * si + st[rows, 0:W]
        nsi = cr * si + ci * sr + st[rows, W:2 * W]
        if store:
            st[rows, 0:W] = nsr
            st[rows, W:2 * W] = nsi
        return nsr, nsi

    if prev is None:
        return lax.fori_loop(0, nk, lambda j, c: advance(nk - 1 - j if reverse else j, c[0], c[1]), init, unroll=2)
    assert reverse

    def step(j, carry):
        k = nk - 1 - j
        nsr, nsi = advance(k, carry[0], carry[1])
        prows = pl.ds(pl.multiple_of((k - 1) * 8, 8), 8)
        pr = prev[prows, 0:W]
        pi = prev[prows, W:2 * W]
        return nsr, nsi, carry[2] + nsr * pr + nsi * pi, carry[3] + nsi * pr - nsr * pi

    carry = lax.fori_loop(0, nk - 1, step, init, unroll=2)
    nsr, nsi = advance(0, carry[0], carry[1])
    return nsr, nsi, carry[2], carry[3]


def _chain(fin, fr, fi, pr, pi, reverse):
    W = S5_W
    fin[:, 0:W] = fr
    fin[:, W:2 * W] = fi
    rowid = lax.broadcasted_iota(jnp.int32, (8, W), 0)
    cr = jnp.zeros((1, W), F32)
    ci = jnp.zeros((1, W), F32)
    init_r = jnp.zeros((8, W), F32)
    init_i = jnp.zeros((8, W), F32)
    for s in (range(7, -1, -1) if reverse else range(8)):
        init_r = jnp.where(rowid == s, cr, init_r)
        init_i = jnp.where(rowid == s, ci, init_i)
        lr = fin[s:s + 1, 0:W]
        li = fin[s:s + 1, W:2 * W]
        cr, ci = lr + pr * cr - pi * ci, li + pr * ci + pi * cr
    return init_r, init_i


def _full_scan(st, fin, ar, ai, nk, reverse, prev=None, carry_in=None, carry_out=None):
    W = S5_W
    cr = jnp.broadcast_to(ar, (8, W))
    ci = jnp.broadcast_to(-ai if reverse else ai, (8, W))
    z = jnp.zeros((8, W), F32)
    if carry_in is None:
        fr, fi = _scan(st, cr, ci, (z, z), nk, reverse, store=False)
        pr, pi = _cpow(ar, -ai if reverse else ai, nk)
        init = _chain(fin, fr, fi, pr, pi, reverse)
    else:
        init = (carry_in[:, 0:W], carry_in[:, W:2 * W])
    if carry_out is not None:
        carry_out[:, 0:W] = init[0]
        carry_out[:, W:2 * W] = init[1]
    if prev is None:
        return _scan(st, cr, ci, init, nk, reverse, store=True)
    return _scan(st, cr, ci, init + (z, z), nk, reverse, store=True, prev=prev)


def _s5_specs(L):
    W2 = 2 * S5_W
    GC = S5_GB * S5_C
    col = pl.BlockSpec((L, GC), lambda g: (0, g))
    vec = pl.BlockSpec((1, GC), lambda g: (0, g))
    avec = pl.BlockSpec((1, S5_W), lambda g: (0, g))
    bmat = pl.BlockSpec((None, GC, W2), lambda g: (g, 0, 0))
    cmat = pl.BlockSpec((None, W2, GC), lambda g: (g, 0, 0))
    return col, vec, avec, bmat, cmat


def _interleave(dst, src, nk):
    for s in range(8):
        dst[pl.ds(s, nk, stride=8), :] = src[s * nk:(s + 1) * nk, :]


def _deinterleave(dst, src, nk):
    for s in range(8):
        dst[s * nk:(s + 1) * nk, :] = src[pl.ds(s, nk, stride=8), :].astype(dst.dtype)


def _hosting_call(body, name, nsteps, host, ins, in_specs, outs, out_specs, scratch):
    grid = (nsteps,) if isinstance(nsteps, int) else tuple(nsteps)
    params = pltpu.CompilerParams(dimension_semantics=("arbitrary",) * len(grid), vmem_limit_bytes=VMEM_LIMIT)
    if host is None:
        res = pl.pallas_call(
            body, name=name, grid=grid, in_specs=in_specs, out_specs=out_specs, out_shape=outs,
            scratch_shapes=scratch, compiler_params=params,
        )(*ins)
        return list(res), []
    n_in, n_out, n_sc = len(ins), len(outs), len(scratch)
    h_in, h_out = len(host.ins), len(host.outs)

    def hosted(*refs):
        a = refs[:n_in]
        ha = refs[n_in:n_in + h_in]
        o = refs[n_in + h_in:n_in + h_in + n_out]
        ho = refs[n_in + h_in + n_out:n_in + h_in + n_out + h_out]
        sc = refs[n_in + h_in + n_out + h_out:n_in + h_in + n_out + h_out + n_sc]
        hs = refs[n_in + h_in + n_out + h_out + n_sc:]
        first = functools.reduce(jnp.logical_and, [pl.program_id(i) == 0 for i in range(len(grid))])
        last = functools.reduce(jnp.logical_and, [pl.program_id(i) == g - 1 for i, g in enumerate(grid)])

        @pl.when(first)
        def _():
            host.start(ha, ho, hs)

        body(*a, *o, *sc)

        @pl.when(last)
        def _():
            host.finish(ha, ho, hs)

    hbm = pl.BlockSpec(memory_space=pl.ANY)
    res = pl.pallas_call(
        hosted, name=name, grid=grid,
        in_specs=list(in_specs) + [hbm] * h_in, out_specs=list(out_specs) + [hbm] * h_out,
        out_shape=list(outs) + list(host.outs), scratch_shapes=list(scratch) + list(host.scratch),
        compiler_params=params,
    )(*ins, *host.ins)
    return list(res[:n_out]), list(res[n_out:])


def _s5_fwd(u, bm, cm, ar, ai, dvec, host=None):
    L = u.shape[0]
    nk = L // 8
    GC = S5_GB * S5_C
    nb = S5_G // S5_GB
    col, vec, avec, bmat, cmat = _s5_specs(L)

    def body(u_ref, b_ref, c_ref, ar_ref, ai_ref, d_ref, y_ref, carry_ref, st, fin, ui, yi):
        _interleave(ui, u_ref, nk)
        for r in range(8):
            rows = slice(r * nk, (r + 1) * nk)
            st[rows, :] = _dot(ui[rows, :].astype(BF16), b_ref[...])
        _full_scan(st, fin, ar_ref[...], ai_ref[...], nk, reverse=False, carry_out=carry_ref)
        for r in range(8):
            rows = slice(r * nk, (r + 1) * nk)
            yi[rows, :] = _dot_nt(st[rows, :].astype(BF16), c_ref[...]) + d_ref[...] * ui[rows, :]
        _deinterleave(y_ref, yi, nk)

    return _hosting_call(
        body, "s5_fwd", nb, host,
        [u, bm, cm, ar, ai, dvec], [col, bmat, bmat, avec, avec, vec],
        [jax.ShapeDtypeStruct(u.shape, F32), jax.ShapeDtypeStruct((nb * 8, 2 * S5_W), F32)],
        [col, pl.BlockSpec((8, 2 * S5_W), lambda g: (g, 0))],
        [pltpu.VMEM((L, 2 * S5_W), F32), pltpu.VMEM((8, 2 * S5_W), F32), pltpu.VMEM((L, GC), F32),
         pltpu.VMEM((L, GC), F32)])


def _s5_bwd(u, dy, carry, bm, cm, ar, ai, dvec, mask, rmat, host=None):
    L = u.shape[0]
    nk = L // 8
    W = S5_W
    GC = S5_GB * S5_C
    col, vec, avec, bmat, cmat = _s5_specs(L)
    hi = lax.Precision.HIGHEST

    def body(u_ref, dy_ref, carry_ref, b_ref, ct_ref, ar_ref, ai_ref, d_ref, mask_ref, r_ref,
             du_ref, db_ref, dc_ref, dd_ref, dar_ref, dai_ref, sa, sb, fin, ui, dyi, dui):
        ar = ar_ref[...]
        ai = ai_ref[...]
        _interleave(ui, u_ref, nk)
        _interleave(dyi, dy_ref, nk)
        for r in range(8):
            rows = slice(r * nk, (r + 1) * nk)
            sa[rows, :] = _dot(ui[rows, :].astype(BF16), b_ref[...])
            sb[rows, :] = _dot(dyi[rows, :].astype(BF16), ct_ref[...])
        _full_scan(sa, fin, ar, ai, nk, reverse=False, carry_in=carry_ref)
        gr, gi, accr, acci = _full_scan(sb, fin, ar, ai, nk, reverse=True, prev=sa)
        rowid = lax.broadcasted_iota(jnp.int32, (8, W), 0)
        last = pl.ds((nk - 1) * 8, 8)
        pr = jnp.where(rowid == 0, 0.0, pltpu.roll(sa[last, 0:W], 1, 0))
        pi = jnp.where(rowid == 0, 0.0, pltpu.roll(sa[last, W:2 * W], 1, 0))
        accr = accr + gr * pr + gi * pi
        acci = acci + gi * pr - gr * pi
        dar_ref[...] = jnp.sum(accr, axis=0, keepdims=True)
        dai_ref[...] = jnp.sum(acci, axis=0, keepdims=True)
        dbf = jnp.zeros((GC, 2 * W), F32)
        dcf = jnp.zeros((GC, 2 * W), F32)
        dd = jnp.zeros((1, GC), F32)
        for r in range(8):
            rows = slice(r * nk, (r + 1) * nk)
            ub = ui[rows, :]
            dyb = dyi[rows, :]
            gb = sb[rows, :].astype(BF16)
            dui[rows, :] = _dot_nt(gb, b_ref[...]) + d_ref[...] * dyb
            dbf = dbf + _dot_tn(ub.astype(BF16), gb)
            dcf = dcf + _dot_tn(dyb.astype(BF16), sa[rows, :].astype(BF16))
            dd = dd + jnp.sum(dyb * ub, axis=0, keepdims=True)
        db_ref[...] = jnp.dot(dbf * mask_ref[...], r_ref[...], precision=hi, preferred_element_type=F32)
        dc_ref[...] = jnp.dot(dcf * mask_ref[...], r_ref[...], precision=hi, preferred_element_type=F32)
        dd_ref[...] = dd
        _deinterleave(du_ref, dui, nk)

    cmp_spec = pl.BlockSpec((GC, 2 * S5_P), lambda g: (g, 0))
    whole = lambda shape: pl.BlockSpec(shape, lambda g: (0, 0))
    sd = jax.ShapeDtypeStruct
    return _hosting_call(
        body, "s5_bwd", S5_G // S5_GB, host,
        [u, dy, carry, bm, cm, ar, ai, dvec, mask, rmat],
        [col, col, pl.BlockSpec((8, 2 * W), lambda g: (g, 0)), bmat, bmat, avec, avec, vec, whole(mask.shape),
         whole(rmat.shape)],
        [sd(u.shape, BF16), sd((S5_G * S5_C, 2 * S5_P), F32), sd((S5_G * S5_C, 2 * S5_P), F32),
         sd((1, PRIM), F32), sd((1, S5_G * S5_P), F32), sd((1, S5_G * S5_P), F32)],
        [col, cmp_spec, cmp_spec, vec, avec, avec],
        [pltpu.VMEM((L, 2 * W), F32), pltpu.VMEM((L, 2 * W), F32), pltpu.VMEM((8, 2 * W), F32),
         pltpu.VMEM((L, GC), F32), pltpu.VMEM((L, GC), F32), pltpu.VMEM((L, GC), F32)])


def _s5_compact_consts():
    g_row = np.arange(S5_GB * S5_C) // S5_C
    col = np.arange(2 * S5_W)
    g_col = (col % S5_W) // S5_P
    mask = (g_row[:, None] == g_col[None, :]).astype(np.float32)
    tgt = (col // S5_W) * S5_P + col % S5_P
    rmat = (tgt[:, None] == np.arange(2 * S5_P)[None, :]).astype(np.float32)
    return jnp.asarray(mask), jnp.asarray(rmat)


def _attn_scores(q_ref, k_ref, qb, bq, scale):
    ext = (qb + 1) * bq
    s = _dot_nt(q_ref[qb * bq:ext, :], k_ref[0:ext, :]) * scale
    qpos = lax.broadcasted_iota(jnp.int32, (bq, bq), 0)
    kpos = lax.broadcasted_iota(jnp.int32, (bq, bq), 1)
    diag = jnp.where(kpos <= qpos, s[:, ext - bq:], NEG)
    return diag if qb == 0 else jnp.concatenate([s[:, :ext - bq], diag], axis=-1)


def _attn_fwd(qp, kp, v, scale):
    L = qp.shape[0]
    bq = min(256, L)

    def body(q_ref, k_ref, v_ref, o_ref, lse_ref):
        for qb in range(L // bq):
            rows = slice(qb * bq, (qb + 1) * bq)
            s = _attn_scores(q_ref, k_ref, qb, bq, scale)
            m = jnp.max(s, axis=-1, keepdims=True)
            e = jnp.exp(s - m)
            l = jnp.sum(e, axis=-1, keepdims=True)
            o_ref[rows, :] = _dot(e.astype(BF16), v_ref[0:(qb + 1) * bq, :]) / l
            lse_ref[rows, :] = jnp.broadcast_to(m + jnp.log(l), (bq, HD))

    blk = pl.BlockSpec((L, HD), lambda h: (0, h))
    wide = pl.BlockSpec((L, 2 * HD), lambda h: (0, h))
    return pl.pallas_call(
        body, name="mla_attn_fwd", grid=(MLA_H,),
        in_specs=[wide, wide, blk], out_specs=[blk, blk],
        out_shape=[jax.ShapeDtypeStruct((L, MLA_H * HD), F32)] * 2,
        compiler_params=pltpu.CompilerParams(dimension_semantics=("arbitrary",), vmem_limit_bytes=VMEM_LIMIT),
    )(qp, kp, v)


def _attn_bwd(qp, kp, v, o, lse, do, scale):
    L = qp.shape[0]
    bq = min(256, L)
    nq = L // bq

    def body(q_ref, k_ref, v_ref, o_ref, lse_ref, do_ref, dq_ref, dk_ref, dv_ref, dk_acc, dv_acc):
        dk_acc[...] = jnp.zeros_like(dk_acc)
        dv_acc[...] = jnp.zeros_like(dv_acc)
        for qb in range(nq):
            rows = slice(qb * bq, (qb + 1) * bq)
            ext = (qb + 1) * bq
            do = do_ref[rows, :]
            dob = do.astype(BF16)
            p = jnp.exp(_attn_scores(q_ref, k_ref, qb, bq, scale) - lse_ref[rows, 0:1])
            dp = _dot_nt(dob, v_ref[0:ext, :])
            dsum = jnp.sum(do * o_ref[rows, :], axis=-1, keepdims=True)
            ds = (p * (dp - dsum) * scale).astype(BF16)
            dq_ref[rows, :] = _dot(ds, k_ref[0:ext, :]).astype(dq_ref.dtype)
            dk_acc[0:ext, :] += _dot_tn(ds, q_ref[rows, :])
            dv_acc[0:ext, :] += _dot_tn(p.astype(BF16), dob)
        dk_ref[...] = dk_acc[...].astype(dk_ref.dtype)
        dv_ref[...] = dv_acc[...].astype(dv_ref.dtype)

    sd = jax.ShapeDtypeStruct
    blk = pl.BlockSpec((L, HD), lambda h: (0, h))
    wide = pl.BlockSpec((L, 2 * HD), lambda h: (0, h))
    return pl.pallas_call(
        body, name="mla_attn_bwd", grid=(MLA_H,),
        in_specs=[wide, wide, blk, blk, blk, blk], out_specs=[wide, wide, blk],
        out_shape=[sd((L, MLA_H * 2 * HD), BF16), sd((L, MLA_H * 2 * HD), BF16), sd((L, MLA_H * HD), BF16)],
        scratch_shapes=[pltpu.VMEM((L, 2 * HD), F32), pltpu.VMEM((L, HD), F32)],
        compiler_params=pltpu.CompilerParams(dimension_semantics=("arbitrary",), vmem_limit_bytes=VMEM_LIMIT),
    )(qp, kp, v, o, lse, do)


def _kv_fn(mem, gm, w, gk):
    kv = _mm(_rms(mem, gm, D_MODEL), w)
    k = jnp.concatenate([_rms(kv[:, HD * h:HD * (h + 1)], gk, HD) for h in range(X_HEADS)], axis=-1)
    return k, kv[:, XQ:]


def _kv_prep(mem, gm, w, gk, name):
    def fn(mem, gm, w, gk):
        return _kv_fn(mem, gm, w, gk)
    M = mem.shape[0]
    return _rowwise(name, fn, [('c', mem), ('c', gm), ('c', w), ('c', gk)],
                    [('c', (M, XQ), F32), ('c', (M, XQ), F32)], 1)


def _kv_prep_bwd(mem, gm, w, gk, dk, dv, name):
    def fn(mem, gm, w, gk, dk, dv):
        _, vjp = jax.vjp(lambda a, b, c: _kv_fn(mem, a, b, c), gm, w, gk)
        return vjp((dk, dv))
    return _rowwise(name, fn, [('c', mem), ('c', gm), ('c', w), ('c', gk), ('c', dk), ('c', dv)],
                    [('c', gm.shape, F32), ('c', w.shape, BF16), ('c', gk.shape, F32)], 1)


def _forward_merge(x, mix, mix_kind, xq, gate, k, v, gq, wout, name, nblk, host=None):
    def fn(x, mix, xq, gate, k, v, gq, wout):
        o = _merge(mix, xq, gate, k, v, gq)
        return (x + _dot(o.astype(BF16), wout),)
    L = x.shape[0]
    out = _rowwise(name, fn, [('r', x), (mix_kind, mix), ('r', xq), ('r', gate), ('c', k), ('c', v), ('c', gq),
                              ('c', wout)], [('r', (L, D_MODEL), F32)], nblk, host=host)
    return out[0] if host is None else (out[0][0], out[1])


def _backward_merge(dx, mix, mix_kind, xq, gate, k, v, gq, wout, name, nblk, host=None):
    def fn(dx, mix, xq, gate, k, v, gq, wout):
        g16 = dx.astype(BF16)
        do = _dot_nt(g16, wout)
        o, vjp = jax.vjp(_merge, mix, xq, gate, k, v, gq)
        dmix, dxq, dgate, dk, dv, dgq = vjp(do)
        return dmix, dxq, dgate, o, g16, dk, dv, dgq
    L = dx.shape[0]
    return _rowwise(
        name, fn,
        [('r', dx), (mix_kind, mix), ('r', xq), ('r', gate), ('c', k), ('c', v), ('c', gq), ('c', wout)],
        [('r', (L, PRIM), F32), ('r', (L, XQ), BF16), ('r', (L, BRANCH), BF16), ('t', (BRANCH, L), BF16),
         ('r', (L, D_MODEL), BF16), ('a', k.shape, F32), ('a', v.shape, F32), ('a', gq.shape, F32)], nblk,
        host=host)


_MLA_IN = 3392
_MLA_IN_PAD = 3456


def _uq_rows(wt):
    r = wt.reshape(MLA_H, HD + ROPE, wt.shape[1])
    return jnp.concatenate([r[:, :HD].reshape(PRIM, -1),
                            jnp.pad(r[:, HD:], ((0, 0), (0, HD - ROPE), (0, 0))).reshape(PRIM, -1)], axis=0)


_UQ_ROW_BANDS = [band for h in range(MLA_H) for band in ((h * HD, HD), (PRIM + h * HD, ROPE))]


_MLA_IN_ROW_BANDS = [(0, 768), (3328, 64), (768, 2560)]


_SMALL = (("ln_gain", 2048), ("mem_norm", 2048), ("xq_norm", 256), ("xk_norm", 256), ("s5_lambda_re", 6144),
          ("s5_lambda_im", 6144), ("s5_log_step", 96), ("s5_b_re", 98304), ("s5_b_im", 98304), ("s5_c_re", 98304),
          ("s5_c_im", 98304), ("s5_d", 1536), ("mla_q_lora_norm", 512), ("mla_kv_lora_norm", 256),
          ("mla_q_nope_norm", 128), ("mla_k_nope_norm", 128), ("mla_q_rope_norm", 64), ("mla_k_rope_norm", 64))
_SMALL_ROWS = 432
_SMALL_OFF = {name: sum(n for _, n in _SMALL[:i]) for i, (name, _) in enumerate(_SMALL)}


def _pack_small(d):
    flat = jnp.concatenate([d[n].reshape(-1).astype(F32) for n, _ in _SMALL])
    return jnp.pad(flat, (0, _SMALL_ROWS * 1024 - flat.shape[0])).reshape(_SMALL_ROWS, 1024)


def _unpack_small(p, name, shape):
    off = _SMALL_OFF[name]
    return p.reshape(-1)[off:off + int(np.prod(shape))].reshape(shape)


_WEIGHTS = ('ln_gain', 'w_out', 'mem_norm', 'w_mem_kv', 'xq_norm', 'xk_norm', 's5_w_in', 's5_lambda_re',
            's5_lambda_im', 's5_log_step', 's5_b_re', 's5_b_im', 's5_c_re', 's5_c_im', 's5_d', 's5_w_glu', 'mla_w_in',
            'mla_q_lora_norm', 'mla_kv_lora_norm', 'mla_w_uq', 'mla_w_ukv', 'mla_q_nope_norm', 'mla_k_nope_norm',
            'mla_q_rope_norm', 'mla_k_rope_norm')


def _pad128(g):
    return jnp.pad(g.reshape(1, -1), ((0, 0), (0, HD - g.shape[-1])))


def kernel(x, mem, positions, ln_gain, w_out, mem_norm, w_mem_kv, xq_norm, xk_norm, s5_w_in, s5_lambda_re, s5_lambda_im, s5_log_step, s5_b_re, s5_b_im, s5_c_re, s5_c_im, s5_d, s5_w_glu, mla_w_in, mla_q_lora_norm, mla_kv_lora_norm, mla_w_uq, mla_w_ukv, mla_q_nope_norm, mla_k_nope_norm, mla_q_rope_norm, mla_k_rope_norm, loss_target, m_ln_gain, m_w_out, m_mem_norm, m_w_mem_kv, m_xq_norm, m_xk_norm, m_s5_w_in, m_s5_lambda_re, m_s5_lambda_im, m_s5_log_step, m_s5_b_re, m_s5_b_im, m_s5_c_re, m_s5_c_im, m_s5_d, m_s5_w_glu, m_mla_w_in, m_mla_q_lora_norm, m_mla_kv_lora_norm, m_mla_w_uq, m_mla_w_ukv, m_mla_q_nope_norm, m_mla_k_nope_norm, m_mla_q_rope_norm, m_mla_k_rope_norm, v_ln_gain, v_w_out, v_mem_norm, v_w_mem_kv, v_xq_norm, v_xk_norm, v_s5_w_in, v_s5_lambda_re, v_s5_lambda_im, v_s5_log_step, v_s5_b_re, v_s5_b_im, v_s5_c_re, v_s5_c_im, v_s5_d, v_s5_w_glu, v_mla_w_in, v_mla_q_lora_norm, v_mla_kv_lora_norm, v_mla_w_uq, v_mla_w_ukv, v_mla_q_nope_norm, v_mla_k_nope_norm, v_mla_q_rope_norm, v_mla_k_rope_norm):
    weights = dict(ln_gain=ln_gain, w_out=w_out, mem_norm=mem_norm, w_mem_kv=w_mem_kv, xq_norm=xq_norm,
                   xk_norm=xk_norm, s5_w_in=s5_w_in, s5_lambda_re=s5_lambda_re, s5_lambda_im=s5_lambda_im,
                   s5_log_step=s5_log_step, s5_b_re=s5_b_re, s5_b_im=s5_b_im, s5_c_re=s5_c_re, s5_c_im=s5_c_im,
                   s5_d=s5_d, s5_w_glu=s5_w_glu, mla_w_in=mla_w_in, mla_q_lora_norm=mla_q_lora_norm,
                   mla_kv_lora_norm=mla_kv_lora_norm, mla_w_uq=mla_w_uq, mla_w_ukv=mla_w_ukv,
                   mla_q_nope_norm=mla_q_nope_norm, mla_k_nope_norm=mla_k_nope_norm,
                   mla_q_rope_norm=mla_q_rope_norm, mla_k_rope_norm=mla_k_rope_norm)
    m_in = dict(zip(_WEIGHTS, (m_ln_gain, m_w_out, m_mem_norm, m_w_mem_kv, m_xq_norm, m_xk_norm, m_s5_w_in,
                               m_s5_lambda_re, m_s5_lambda_im, m_s5_log_step, m_s5_b_re, m_s5_b_im, m_s5_c_re,
                               m_s5_c_im, m_s5_d, m_s5_w_glu, m_mla_w_in, m_mla_q_lora_norm, m_mla_kv_lora_norm,
                               m_mla_w_uq, m_mla_w_ukv, m_mla_q_nope_norm, m_mla_k_nope_norm, m_mla_q_rope_norm,
                               m_mla_k_rope_norm)))
    v_in = dict(zip(_WEIGHTS, (v_ln_gain, v_w_out, v_mem_norm, v_w_mem_kv, v_xq_norm, v_xk_norm, v_s5_w_in,
                               v_s5_lambda_re, v_s5_lambda_im, v_s5_log_step, v_s5_b_re, v_s5_b_im, v_s5_c_re,
                               v_s5_c_im, v_s5_d, v_s5_w_glu, v_mla_w_in, v_mla_q_lora_norm, v_mla_kv_lora_norm,
                               v_mla_w_uq, v_mla_w_ukv, v_mla_q_nope_norm, v_mla_k_nope_norm, v_mla_q_rope_norm,
                               v_mla_k_rope_norm)))

    x0 = x[0]
    mem0 = mem[0]
    target = loss_target[0]
    L = x0.shape[0]
    nblk = 4
    nb_big = 8
    me = 4 * lax.axis_index("x") + 2 * lax.axis_index("y") + lax.axis_index("c")

    lora = jnp.pad(jnp.concatenate([mla_q_lora_norm, mla_kv_lora_norm], axis=1), ((0, 7), (0, HD - 96)))
    def gather(*shards):
        return _plan_all_gather(list(shards))

    kh = D_MODEL // 2
    (b_mkv0, b_glu, b_in_mla, b_out0, b_uq, b_ukv, b_mkv1, b_out1), (W_in_s5,) = _cast_call(
        [w_mem_kv[0], s5_w_glu[0], jnp.transpose(mla_w_in[0]), w_out[0], jnp.transpose(mla_w_uq[0]), mla_w_ukv[0],
         w_mem_kv[1], w_out[1]], "cast_shards", host=gather(s5_w_in[0].astype(BF16)))

    ln0, ln1 = ln_gain[0:1], ln_gain[1:2]
    gq0, gq1 = xq_norm[0:1], xq_norm[1:2]
    gk0, gk1 = xk_norm[0:1], xk_norm[1:2]
    gm0, gm1 = mem_norm[0:1], mem_norm[1:2]
    gqn, gkn = mla_q_nope_norm, mla_k_nope_norm
    gqr, gkr = _pad128(mla_q_rope_norm), _pad128(mla_k_rope_norm)

    lr3 = s5_lambda_re.reshape(S5_G, 1, S5_P)
    li3 = s5_lambda_im.reshape(S5_G, 1, S5_P)
    ls3 = s5_log_step.reshape(S5_G, 1, 1)
    btr = jnp.swapaxes(s5_b_re[0], 1, 2)
    bti = jnp.swapaxes(s5_b_im[0], 1, 2)
    a_r, a_i, bm, cm = _s5_params(lr3, li3, ls3, btr, bti, s5_c_re[0], s5_c_im[0])
    a_r2 = a_r.reshape(1, S5_G * S5_P)
    a_i2 = a_i.reshape(1, S5_G * S5_P)
    cmask, rmat = _s5_compact_consts()

    half = ROPE // 2
    inv_freq = ROPE_THETA ** (-jnp.arange(half, dtype=F32) / half)
    invf = jnp.concatenate([inv_freq, inv_freq, jnp.zeros((HD - ROPE,), F32)]).reshape(1, HD)

    def rot_tables(pos, invf):
        ang = pos.astype(F32) * invf
        lane = lax.broadcasted_iota(jnp.int32, ang.shape, 1)
        c = jnp.where(lane < ROPE, jnp.cos(ang), 0.0)
        s = jnp.sin(ang)
        return c, jnp.where(lane < half, -s, 0.0), jnp.where((lane >= half) & (lane < ROPE), s, 0.0)

    tc, ts1, ts2 = _rowwise("rot_tables", rot_tables, [('r', positions.reshape(L, 1)), ('c', invf)],
                            [('r', (L, HD), F32)] * 3, nblk)

    def in_s5(x, g, w):
        proj = _mm_slots(_rms(x, g, D_MODEL).astype(BF16), w)
        return proj[:, :PRIM], proj[:, PRIM:PRIM + XQ], proj[:, PRIM + XQ:]

    u_s5, xq_a, gate_a = _rowwise(
        "s5_in", in_s5, [('r', x0), ('c', ln0), ('c', W_in_s5)],
        [('r', (L, PRIM), F32), ('r', (L, XQ), F32), ('r', (L, BRANCH), F32)], nblk)
    (y_s5, s5_carry), (W_glu, G_mkv0, G_in_mla_a) = _s5_fwd(u_s5, bm, cm, a_r2, a_i2, s5_d,
                                                            host=gather(b_glu, b_mkv0, b_in_mla[:, :kh]))

    def glu(y, w):
        z = _mm_slots(_gelu(y).astype(BF16), w)
        return z[:, :PRIM] * _sigmoid(z[:, PRIM:]), z

    (y2, z_glu), (G_out0,) = _rowwise("s5_glu", glu, [('r', y_s5), ('c', W_glu)],
                                      [('r', (L, PRIM), F32), ('r', (L, 2 * PRIM), F32)], nblk, host=gather(b_out0))
    W_mkv0 = G_mkv0.reshape(D_MODEL, 2 * XQ)
    k_a, v_a = _kv_prep(mem0, gm0, W_mkv0, gk0, "kv_prep0")
    x1, (G_in_mla_b,) = _forward_merge(
        x0, y2, 'r', xq_a, gate_a, k_a, v_a, gq0, G_out0.reshape(BRANCH, D_MODEL), "merge0", nblk,
        host=gather(b_in_mla[:, kh:]))
    W_in_mla = jnp.concatenate([G_in_mla_a, G_in_mla_b], axis=2).reshape(_MLA_IN, D_MODEL)

    def in_mla(x, g, w):
        xn = _rms(x, g, D_MODEL).astype(BF16)
        a = _dot_nt(xn, w[0:768])
        kx = _dot_nt(xn, w[768:896])
        b = _dot_nt(xn, w[832:_MLA_IN])
        lane = lax.broadcasted_iota(jnp.int32, kx.shape, 1)
        return a[:, :512], a[:, 512:], b[:, :XQ], b[:, XQ:], jnp.where(lane < ROPE, kx, 0.0)

    (c_q, c_kv, xq_b, gate_b, krp), (G_uq, W_kv, G_lora) = _rowwise(
        "mla_in", in_mla, [('r', x1), ('c', ln1), ('c', W_in_mla)],
        [('r', (L, Q_LORA), F32), ('r', (L, KV_LORA), F32), ('r', (L, XQ), F32), ('r', (L, BRANCH), F32),
         ('r', (L, HD), F32)], nblk,
        host=gather(b_uq, b_ukv, lora))
    W_q = _uq_rows(G_uq.reshape(MLA_H * (HD + ROPE), Q_LORA))
    g_qlora = G_lora[:, 0, :64].reshape(1, Q_LORA)
    g_kvlora = G_lora[:, 0, 64:96].reshape(1, KV_LORA)

    def qkv(c_q, c_kv, krp, tc, ts1, ts2, gql, gkvl, wq, wkv, gqn, gkn, gqr, gkr):
        q = _dot_nt(_rms(c_q, gql, Q_LORA).astype(BF16), wq)
        kv = _mm_slots(_rms(c_kv, gkvl, KV_LORA).astype(BF16), wkv)
        kp, v = _kv_post(*_kv_chunks(kv), krp, gkn, gkr, tc, ts1, ts2)
        return _q_post(*_q_chunks(q), gqn, gqr, tc, ts1, ts2), kp, v

    qkv_consts = [('c', g_qlora), ('c', g_kvlora), ('c', W_q), ('c', W_kv), ('c', gqn), ('c', gkn), ('c', gqr),
                  ('c', gkr)]
    (q_pad, k_pad, v_h), (G_mkv1, G_out1) = _rowwise(
        "mla_qkv", qkv, [('r', c_q), ('r', c_kv), ('r', krp), ('r', tc), ('r', ts1), ('r', ts2)] + qkv_consts,
        [('r', (L, 2 * PRIM), BF16), ('r', (L, 2 * PRIM), BF16), ('r', (L, PRIM), BF16)], nblk,
        host=gather(b_mkv1, b_out1))
    W_out = (G_out0.reshape(BRANCH, D_MODEL), G_out1.reshape(BRANCH, D_MODEL))
    W_mkv = (W_mkv0, G_mkv1.reshape(D_MODEL, 2 * XQ))
    scale = (HD + ROPE) ** -0.5
    attn, lse = _attn_fwd(q_pad, k_pad, v_h, scale)
    k_b, v_b = _kv_prep(mem0, gm1, W_mkv[1], gk1, "kv_prep1")

    def merge_loss(x, mix, xq, gate, k, v, gq, wout, t):
        err = x + _dot(_merge(mix, xq, gate, k, v, gq).astype(BF16), wout) - t
        part = 0.5 * jnp.sum(jnp.sum(err * err, axis=-1, keepdims=True) * (1.0 / D_MODEL), axis=0, keepdims=True)
        return err * (1.0 / D_MODEL), jnp.broadcast_to(part, (1, HD))

    dx2, loss_part = _rowwise(
        "merge1_loss", merge_loss,
        [('r', x1), ('r', attn), ('r', xq_b), ('r', gate_b), ('c', k_b), ('c', v_b), ('c', gq1), ('c', W_out[1]),
         ('r', target)], [('r', (L, D_MODEL), F32), ('a', (1, HD), F32)], nblk)

    dattn, dxq_b, dgate_b, o_b, g_b, dk_b, dv_b, dgq1 = _backward_merge(
        dx2, attn, 'r', xq_b, gate_b, k_b, v_b, gq1, W_out[1], "merge1_bwd", nb_big)
    dgm1, dW_mkv1, dgk1 = _kv_prep_bwd(mem0, gm1, W_mkv[1], gk1, dk_b, dv_b, "kv_prep1_bwd")
    dW_out1 = _matmul_tn(o_b, g_b, "dw_out1")
    dq_pad, dk_pad, dv_h = _attn_bwd(q_pad, k_pad, v_h, attn, lse, dattn, scale)

    def qkv_bwd(c_q, c_kv, krp, tc, ts1, ts2, dqp, dkp, dv, gql, gkvl, wq, wkv, gqn, gkn, gqr, gkr):
        cqn, vjp_qn = jax.vjp(lambda a, b: _rms(a, b, Q_LORA), c_q, gql)
        ckvn, vjp_kvn = jax.vjp(lambda a, b: _rms(a, b, KV_LORA), c_kv, gkvl)
        cqn16 = cqn.astype(BF16)
        ckvn16 = ckvn.astype(BF16)
        q = _dot_nt(cqn16, wq)
        kv = _mm_slots(ckvn16, wkv)
        _, vjp_q = jax.vjp(lambda n, r, a, b: _q_post(n, r, a, b, tc, ts1, ts2), *_q_chunks(q), gqn, gqr)
        dnope, drope, dgqn, dgqr = vjp_q(dqp.astype(F32))
        dq = jnp.concatenate(dnope + drope, axis=-1)
        _, vjp_kv = jax.vjp(lambda n, v, k, a, b: _kv_post(n, v, k, a, b, tc, ts1, ts2), *_kv_chunks(kv), krp, gkn,
                            gkr)
        dkn, dvals, dkrp, dgkn, dgkr = vjp_kv((dkp.astype(F32), dv.astype(F32)))
        dkv = jnp.concatenate([x for pair in zip(dkn, dvals) for x in pair], axis=-1)
        dq16 = dq.astype(BF16)
        dkv16 = dkv.astype(BF16)
        dc_q, dgql = vjp_qn(_dot(dq16, wq))
        dc_kv, dgkvl = vjp_kvn(_mm_slots_nt(dkv16, wkv))
        return dc_q, dc_kv, dkrp, cqn16, dq16, ckvn16, dkv16, dgql, dgkvl, dgqn, dgkn, dgqr, dgkr

    (dc_q, dc_kv, dkrp, cqn16, dq16, ckvn16, dkv16, dgql, dgkvl, dgqn, dgkn, dgqr, dgkr) = _rowwise(
        "mla_qkv_bwd", qkv_bwd,
        [('r', c_q), ('r', c_kv), ('r', krp), ('r', tc), ('r', ts1), ('r', ts2), ('r', dq_pad), ('r', dk_pad),
         ('r', dv_h)] + qkv_consts,
        [('r', (L, Q_LORA), BF16), ('r', (L, KV_LORA), BF16), ('r', (L, HD), BF16), ('r', (L, Q_LORA), BF16),
         ('t', (2 * PRIM, L), BF16), ('t', (KV_LORA, L), BF16), ('r', (L, 2 * PRIM), BF16),
         ('a', (1, Q_LORA), F32), ('a', (1, KV_LORA), F32), ('a', (1, HD), F32), ('a', (1, HD), F32),
         ('a', (1, HD), F32), ('a', (1, HD), F32)], nb_big)
    dW_q = _matmul_tn(dq16, cqn16, "dw_uq", row_bands=_UQ_ROW_BANDS)
    dW_kv = _matmul_tn_slots(ckvn16, dkv16, "dw_ukv")

    def in_bwd(x, dres, g, w, *dparts):
        dproj = jnp.concatenate(dparts, axis=-1).astype(BF16)
        xn, vjp = jax.vjp(lambda a, b: _rms(a, b, D_MODEL), x, g)
        if w.ndim == 3:
            dxn = _mm_slots_nt(dproj, w)
        else:
            dkr = dproj[:, 3328:]
            dkr = jnp.where(lax.broadcasted_iota(jnp.int32, dkr.shape, 1) < ROPE, dkr, jnp.zeros_like(dkr))
            dxn = _dot(dproj[:, :768], w[0:768]) + _dot(dproj[:, 768:3328], w[832:_MLA_IN]) + _dot(dkr, w[768:896])
        dx, dg = vjp(dxn)
        return dx + dres, xn, dproj, dg

    dx1, xn1, dproj1, dln1 = _rowwise(
        "mla_in_bwd", in_bwd,
        [('r', x1), ('r', dx2), ('c', ln1), ('c', W_in_mla), ('r', dc_q), ('r', dc_kv), ('r', dxq_b), ('r', dgate_b),
         ('r', dkrp)],
        [('r', (L, D_MODEL), F32), ('r', (L, D_MODEL), BF16), ('t', (_MLA_IN_PAD, L), BF16), ('a', (1, D_MODEL), F32)],
        nblk)
    dW_in_mla = _matmul_tn(dproj1, xn1, "dw_mla_in", row_bands=_MLA_IN_ROW_BANDS)

    grads1 = [dW_out1.reshape(N_DEV, 256, D_MODEL), dW_mkv1.reshape(N_DEV, 128, 2 * XQ),
              dW_in_mla.reshape(N_DEV, 424, D_MODEL),
              dW_q.reshape(N_DEV, 288, Q_LORA), dW_kv]
    (dy2, dxq_a, dgate_a, o_a, g_a, dk_a, dv_a, dgq0), pair1 = _backward_merge(
        dx1, y2, 'r', xq_a, gate_a, k_a, v_a, gq0, W_out[0], "merge0_bwd", nb_big, host=_plan_pair(grads1))
    dgm0, dW_mkv0, dgk0 = _kv_prep_bwd(mem0, gm0, W_mkv[0], gk0, dk_a, dv_a, "kv_prep0_bwd")
    dW_out0 = _matmul_tn(o_a, g_a, "dw_out0")
    t1 = list(_pair_add(grads1, pair1, "rs_add_layer1"))

    def glu_bwd(y, z, dy2, w):
        h, vjp_h = jax.vjp(_gelu, y)
        _, vjp_z = jax.vjp(lambda a, b: a * _sigmoid(b), z[:, :PRIM], z[:, PRIM:])
        dz16 = jnp.concatenate(vjp_z(dy2), axis=-1).astype(BF16)
        return vjp_h(_mm_slots_nt(dz16, w))[0], h.astype(BF16), dz16

    grads0 = [dW_out0.reshape(N_DEV, 256, D_MODEL), dW_mkv0.reshape(N_DEV, 128, 2 * XQ)]
    (dy_s5, h16, dz16), glu_hosted = _rowwise(
        "s5_glu_bwd", glu_bwd, [('r', y_s5), ('r', z_glu), ('r', dy2), ('c', W_glu)],
        [('r', (L, PRIM), F32), ('t', (PRIM, L), BF16), ('r', (L, 2 * PRIM), BF16)], nb_big,
        host=_combine(_plan_chips(t1[2:3]), _plan_pair(grads0)))
    recv_in_mla, pair0 = glu_hosted[:1], glu_hosted[1:]
    dW_glu = _matmul_tn_slots(h16, dz16, "dw_glu")
    t0 = list(_pair_add(grads0 + [dW_glu], pair0 + list(_exchange_call(_plan_pair([dW_glu]), "rs_pair_glu")),
                        "rs_add_layer0"))
    both = [jnp.concatenate([t0[i], t1[i]], axis=1) for i in range(2)]
    (du_s5, dbc, dcc, dd, dar, dai), recv_rest = _s5_bwd(u_s5, dy_s5, s5_carry, bm, cm, a_r2, a_i2, s5_d,
                                                        cmask, rmat, host=_plan_chips(both + t1[3:] + t0[2:]))
    early_recv = recv_rest[:2] + recv_in_mla + recv_rest[2:]
    dbc4 = dbc.reshape(S5_G, S5_C, 2, S5_P)
    dcc4 = dcc.reshape(S5_G, S5_C, 2, S5_P)
    dlr, dli, dls, dbtr, dbti = _s5_params_bwd(
        lr3, li3, ls3, btr, bti, dar.reshape(S5_G, 1, S5_P), dai.reshape(S5_G, 1, S5_P), dbc4[:, :, 0], dbc4[:, :, 1])

    small_part = {
        "ln_gain": jnp.concatenate([jnp.zeros_like(dln1), dln1]), "mem_norm": jnp.concatenate([dgm0, dgm1]),
        "xq_norm": jnp.concatenate([dgq0, dgq1]), "xk_norm": jnp.concatenate([dgk0, dgk1]),
        "s5_lambda_re": dlr, "s5_lambda_im": dli, "s5_log_step": dls,
        "s5_b_re": jnp.swapaxes(dbtr, 1, 2), "s5_b_im": jnp.swapaxes(dbti, 1, 2),
        "s5_c_re": dcc4[:, :, 0], "s5_c_im": -dcc4[:, :, 1], "s5_d": dd,
        "mla_q_lora_norm": dgql, "mla_kv_lora_norm": dgkvl, "mla_q_nope_norm": dgqn, "mla_k_nope_norm": dgkn,
        "mla_q_rope_norm": dgqr[:, :ROPE], "mla_k_rope_norm": dgkr[:, :ROPE],
    }
    loss8 = jnp.pad(loss_part, ((0, 7), (0, 0)))
    (dx0, xn0, dproj0, dln0), (small_gath, loss_g) = _rowwise(
        "s5_in_bwd", in_bwd,
        [('r', x0), ('r', dx1), ('c', ln0), ('c', W_in_s5), ('r', du_s5), ('r', dxq_a),
         ('r', dgate_a)],
        [('r', (L, D_MODEL), F32), ('t', (D_MODEL, L), BF16), ('r', (L, 2 * BRANCH), BF16), ('a', (1, D_MODEL), F32)],
        nblk, host=_plan_all_gather([_pack_small(small_part).astype(BF16), loss8]))
    dW_in_s5, (ln0_gath,) = _matmul_tn_slots(
        xn0, dproj0, "dw_s5_in", host=_plan_all_gather([jnp.pad(dln0, ((0, 7), (0, 0)))]))

    late = [dW_in_s5]
    late_t = _pair_add(late, list(_exchange_call(_plan_pair(late), "rs_pair_late")), "rs_add_late")
    late_recv = list(_exchange_call(_plan_chips(late_t), "rs_chips_late"))
    owners = ["w_out", "w_mem_kv", "mla_w_in", "mla_w_uq", "mla_w_ukv", "s5_w_glu", "s5_w_in"]
    flipped = ("mla_w_in", "mla_w_uq")

    def shard(d, n):
        a = d[n]
        return jnp.transpose(a[0]) if n in flipped else a.reshape(-1, a.shape[-1])

    upd, _ = _updates_call(
        early_recv + late_recv, [shard(weights, n) for n in owners], [shard(m_in, n) for n in owners],
        [shard(v_in, n) for n in owners], "update_big")
    grads, delta, new_m, new_v = {}, {}, {}, {}
    for n, res in zip(owners, upd):
        shape = weights[n].shape
        grads[n], delta[n], new_m[n], new_v[n] = (
            (jnp.transpose(r)[None] if n in flipped else r.reshape(shape)) for r in res)

    gs, loss_sum = _small_sum(small_gath, loss_g, ln0_gath, "small_sum")
    loss = loss_sum[0, 0]
    for n, _ in _SMALL:
        shape = weights[n].shape
        if n == "mla_q_lora_norm":
            grads[n] = lax.dynamic_slice(_unpack_small(gs, n, (Q_LORA,)), (me * 64,), (64,)).reshape(shape)
        elif n == "mla_kv_lora_norm":
            grads[n] = lax.dynamic_slice(_unpack_small(gs, n, (KV_LORA,)), (me * 32,), (32,)).reshape(shape)
        else:
            grads[n] = _unpack_small(gs, n, shape)

    def own(n, a):
        if a.ndim == 4:
            a = jnp.transpose(a, (0, 2, 3, 1))
        elif a.ndim == 3:
            a = jnp.transpose(a, (0, 2, 1))
        return a.reshape(a.shape[1:]) if a.ndim >= 3 else a

    def back(n, a):
        shape = weights[n].shape
        if len(shape) == 4:
            return jnp.transpose(a.reshape((1,) + a.shape), (0, 3, 1, 2))
        if len(shape) == 3:
            return jnp.transpose(a.reshape((1,) + a.shape), (0, 2, 1))
        return a.reshape(shape)

    wide = ("s5_b_re", "s5_b_im", "s5_c_re", "s5_c_im")
    for names, nb, call in (([n for n, _ in _SMALL if n not in wide], 1, "update_small"), (wide, 4, "update_s5_bc")):
        res = _adamw_multi([own(n, weights[n]) for n in names], [own(n, grads[n]) for n in names],
                           [own(n, m_in[n]) for n in names], [own(n, v_in[n]) for n in names], call, nb)
        for n, (dl, m2, v2) in zip(names, res):
            delta[n], new_m[n], new_v[n] = back(n, dl), back(n, m2), back(n, v2)
    return (loss, dx0[None], *[grads[n] for n in _WEIGHTS], *[delta[n] for n in _WEIGHTS],
            *[new_m[n] for n in _WEIGHTS], *[new_v[n] for n in _WEIGHTS])
```

```python
import functools
import math

import numpy as np
import jax
import jax.numpy as jnp
from jax import lax
from jax.experimental import pallas as pl
from jax.experimental.pallas import tpu as pltpu

F32 = jnp.float32
BF16 = jnp.bfloat16
EPS = 1e-6
NEG = float(np.finfo(np.float32).min)
MESH = pl.DeviceIdType.MESH

N_DEV = 8
D_MODEL = 1024
MEM_LEN = 256
XQ = 512
PRIM = 1536
BRANCH = 2048
X_HEADS = 4
HD = 128
S5_G = 96
S5_P = 64
S5_C = 16
S5_GB = 8
S5_W = S5_GB * S5_P
MLA_H = 12
ROPE = 64
Q_LORA = 512
KV_LORA = 256
ROPE_THETA = 10000.0

ADAM_LR = 0.001
ADAM_B1 = 0.9
ADAM_B2 = 0.999
ADAM_EPS = 1e-08
ADAM_WD = 0.01
ADAM_STEP = 10

VMEM_LIMIT = 56 * 1024 * 1024


def _dot(a, b):
    return jnp.dot(a, b, preferred_element_type=F32)


def _dot_nt(a, b):
    return lax.dot_general(a, b, (((1,), (1,)), ((), ())), preferred_element_type=F32)


def _dot_tn(a, b):
    return lax.dot_general(a, b, (((0,), (0,)), ((), ())), preferred_element_type=F32)


@jax.custom_vjp
def _mm(a, b):
    return _dot(a.astype(BF16), b.astype(BF16))


def _mm_fwd(a, b):
    return _mm(a, b), (a, b)


def _mm_bwd(res, g):
    a, b = res
    gb = g.astype(BF16)
    return _dot_nt(gb, b.astype(BF16)).astype(a.dtype), _dot_tn(a.astype(BF16), gb).astype(b.dtype)


_mm.defvjp(_mm_fwd, _mm_bwd)


@jax.custom_vjp
def _mm_nt(a, b):
    return _dot_nt(a.astype(BF16), b.astype(BF16))


def _mm_nt_fwd(a, b):
    return _mm_nt(a, b), (a, b)


def _mm_nt_bwd(res, g):
    a, b = res
    gb = g.astype(BF16)
    return _dot(gb, b.astype(BF16)).astype(a.dtype), _dot_tn(gb, a.astype(BF16)).astype(b.dtype)


_mm_nt.defvjp(_mm_nt_fwd, _mm_nt_bwd)


@jax.custom_vjp
def _softmax(s):
    m = jnp.max(s, axis=-1, keepdims=True)
    e = jnp.exp(s - m)
    return e / jnp.sum(e, axis=-1, keepdims=True)


def _softmax_fwd(s):
    p = _softmax(s)
    return p, p


def _softmax_bwd(p, g):
    return (p * (g - jnp.sum(p * g, axis=-1, keepdims=True)),)


_softmax.defvjp(_softmax_fwd, _softmax_bwd)


def _rms(x, g, n):
    ms = jnp.sum(x * x, axis=-1, keepdims=True) * (1.0 / n)
    return x * lax.rsqrt(ms + EPS) * g


def _sigmoid(x):
    return 1.0 / (1.0 + jnp.exp(-x))


def _silu(x):
    return x * _sigmoid(x)


def _gelu(x):
    c = math.sqrt(2.0 / math.pi)
    return 0.5 * x * (1.0 + jnp.tanh(c * (x + 0.044715 * (x * x * x))))


@jax.custom_vjp
def _rot(x, c, s1, s2):
    return x * c + pltpu.roll(x, 96, 1) * s1 + pltpu.roll(x, 32, 1) * s2


def _rot_fwd(x, c, s1, s2):
    return _rot(x, c, s1, s2), (c, s1, s2)


def _rot_bwd(res, g):
    c, s1, s2 = res
    dx = g * c + pltpu.roll(g * s1, 32, 1) + pltpu.roll(g * s2, 96, 1)
    return dx, jnp.zeros_like(c), jnp.zeros_like(s1), jnp.zeros_like(s2)


_rot.defvjp(_rot_fwd, _rot_bwd)


def _mem_attn(xq, k, v, gq):
    outs = []
    for h in range(X_HEADS):
        sl = slice(HD * h, HD * (h + 1))
        q = _rms(xq[:, sl], gq, HD)
        p = _softmax(_mm_nt(q, k[:, sl]) * (HD ** -0.5))
        outs.append(_mm(p, v[:, sl]))
    return jnp.concatenate(outs, axis=-1)


def _merge(mix, xq, gate, k, v, gq):
    return jnp.concatenate([mix, _mem_attn(xq, k, v, gq)], axis=-1) * _silu(gate)


def _q_chunks(q):
    return ([q[:, HD * h:HD * (h + 1)] for h in range(MLA_H)],
            [q[:, PRIM + HD * h:PRIM + HD * (h + 1)] for h in range(MLA_H)])


def _q_post(nope, rope, gqn, gqr, c, s1, s2):
    pieces = []
    for qn, qr in zip(nope, rope):
        pieces.append(_rms(qn, gqn, HD))
        pieces.append(_rot(_rms(qr, gqr, ROPE), c, s1, s2))
    return jnp.concatenate(pieces, axis=-1)


def _kv_chunks(kv):
    return ([kv[:, 2 * HD * h:2 * HD * h + HD] for h in range(MLA_H)],
            [kv[:, 2 * HD * h + HD:2 * HD * (h + 1)] for h in range(MLA_H)])


def _kv_post(kn, vals, krp, gkn, gkr, c, s1, s2):
    kr = _rot(_rms(krp, gkr, ROPE), c, s1, s2)
    pieces = []
    for k in kn:
        pieces.append(_rms(k, gkn, HD))
        pieces.append(kr)
    return jnp.concatenate(pieces, axis=-1), jnp.concatenate(vals, axis=-1)


def _rowwise(name, fn, ins, outs, nblk, host=None):
    n_in = len(ins)

    def spec(kind, shape):
        if kind == 'r':
            return pl.BlockSpec((shape[0] // nblk, shape[1]), lambda i: (i, 0))
        if kind == 't':
            return pl.BlockSpec((shape[0], shape[1] // nblk), lambda i: (0, i))
        zeros = (0,) * len(shape)
        return pl.BlockSpec(tuple(shape), lambda i: zeros)

    def body(*refs):
        i = pl.program_id(0)
        res = fn(*[r[...] for r in refs[:n_in]])
        for (kind, _, _), ref, val in zip(outs, refs[n_in:], res):
            if kind == 'a':
                @pl.when(i == 0)
                def _():
                    ref[...] = jnp.zeros_like(ref)
                ref[...] += val.astype(ref.dtype)
            elif kind == 't':
                ref[...] = val.astype(F32).T.astype(ref.dtype)
            else:
                ref[...] = val.astype(ref.dtype)

    res, hosted = _hosting_call(
        body, name, nblk, host, [a for _, a in ins], [spec(k, a.shape) for k, a in ins],
        [jax.ShapeDtypeStruct(tuple(s), d) for _, s, d in outs], [spec(k, s) for k, s, _ in outs], [])
    return res if host is None else (res, hosted)


def _matmul_tn(at, g, name, out_dtype=BF16, row_bands=None):
    K, L = at.shape
    N = g.shape[1]
    tn = next(t for t in (512, 384, 256, 128) if N % t == 0)
    bands = [(0, K)] if row_bands is None else row_bands
    rows_out = sum(n for _, n in bands)

    def body(a_ref, g_ref, o_ref):
        res = _dot(a_ref[...], g_ref[...]).astype(o_ref.dtype)
        row = 0
        for start, n in bands:
            o_ref[row:row + n, :] = res[start:start + n]
            row += n

    return pl.pallas_call(
        body, name=name, grid=(N // tn,),
        in_specs=[pl.BlockSpec((K, L), lambda n: (0, 0)), pl.BlockSpec((L, tn), lambda n: (0, n))],
        out_specs=pl.BlockSpec((rows_out, tn), lambda n: (0, n)),
        out_shape=jax.ShapeDtypeStruct((rows_out, N), out_dtype),
        compiler_params=pltpu.CompilerParams(dimension_semantics=("arbitrary",), vmem_limit_bytes=VMEM_LIMIT),
    )(at, g)


def _matmul_tn_slots(at, g, name, host=None):
    K, L = at.shape
    n = g.shape[1] // N_DEV

    def body(a_ref, g_ref, o_ref):
        o_ref[...] = _dot(a_ref[...], g_ref[...]).astype(o_ref.dtype)

    res, hosted = _hosting_call(
        body, name, N_DEV, host, [at, g],
        [pl.BlockSpec((K, L), lambda d: (0, 0)), pl.BlockSpec((L, n), lambda d: (0, d))],
        [jax.ShapeDtypeStruct((N_DEV, K, n), BF16)], [pl.BlockSpec((None, K, n), lambda d: (d, 0, 0))], [])
    return res[0] if host is None else (res[0], hosted)


def _mm_slots(a16, w):
    return jnp.concatenate([_dot(a16, w[d]) for d in range(N_DEV)], axis=-1)


def _mm_slots_nt(g16, w):
    n = w.shape[2]
    out = _dot_nt(g16[:, 0:n], w[0])
    for d in range(1, N_DEV):
        out = out + _dot_nt(g16[:, d * n:(d + 1) * n], w[d])
    return out


class _Exchange:
    def __init__(self, ins, outs, scratch, start, finish):
        self.ins, self.outs, self.scratch, self.start, self.finish = ins, outs, scratch, start, finish


def _xyc():
    return lax.axis_index("x"), lax.axis_index("y"), lax.axis_index("c")


def _plan_all_gather(xs):
    n = len(xs)

    def build(x_refs, out_refs, sems):
        send_sems, recv_sems, local_sems = sems
        x, y, c = _xyc()

        def copies(k, block, to, own=False):
            slot = 4 * block[0] + 2 * block[1] + block[2]
            return [pltpu.make_async_remote_copy(
                src_ref=x_refs[a] if own else out_refs[a].at[slot], dst_ref=out_refs[a].at[slot],
                send_sem=send_sems.at[k * n + a], recv_sem=recv_sems.at[k * n + a], device_id=to,
                device_id_type=MESH) for a in range(n)]

        mine = [pltpu.make_async_copy(x_refs[a], out_refs[a].at[4 * x + 2 * y + c], local_sems.at[a])
                for a in range(n)]
        return copies, mine, (x, y, c), [(1 - x, y), (x, 1 - y), (1 - x, 1 - y)]

    def first_copies(copies, me, chips):
        x, y, c = me
        first = copies(0, me, (x, y, 1 - c), own=True)
        for j, chip in enumerate(chips):
            first += copies(1 + j, me, (*chip, c), own=True)
        return first

    def start(x_refs, out_refs, sems):
        copies, mine, me, chips = build(x_refs, out_refs, sems)
        for cp in mine + first_copies(copies, me, chips):
            cp.start()

    def finish(x_refs, out_refs, sems):
        copies, mine, me, chips = build(x_refs, out_refs, sems)
        x, y, c = me
        passed = []
        for j, chip in enumerate(chips):
            for cp in copies(1 + j, (*chip, c), me):
                cp.wait_recv()
            fwd = copies(4 + j, (*chip, c), (x, y, 1 - c))
            for cp in fwd:
                cp.start()
            passed += fwd
        for cp in copies(0, (x, y, 1 - c), me):
            cp.wait_recv()
        for j, chip in enumerate(chips):
            for cp in copies(4 + j, (*chip, 1 - c), me):
                cp.wait_recv()
        for cp in first_copies(copies, me, chips) + passed:
            cp.wait_send()
        for cp in mine:
            cp.wait()

    return _Exchange(list(xs), [jax.ShapeDtypeStruct((N_DEV,) + a.shape, a.dtype) for a in xs],
                     [pltpu.SemaphoreType.DMA((7 * n,)), pltpu.SemaphoreType.DMA((7 * n,)),
                      pltpu.SemaphoreType.DMA((n,))], start, finish)


_CHIPS = ((0, 0), (0, 1), (1, 0), (1, 1))


def _plan_pair(sends):
    n = len(sends)

    def build(s_refs, o_refs, sems):
        send_sems, recv_sems = sems
        x, y, c = _xyc()
        return [pltpu.make_async_remote_copy(
            src_ref=s_refs[a].at[4 * px + 2 * py + 1 - c], dst_ref=o_refs[a].at[j],
            send_sem=send_sems.at[j * n + a], recv_sem=recv_sems.at[j * n + a], device_id=(x, y, 1 - c),
            device_id_type=MESH) for j, (px, py) in enumerate(_CHIPS) for a in range(n)]

    def start(s_refs, o_refs, sems):
        for cp in build(s_refs, o_refs, sems):
            cp.start()

    def finish(s_refs, o_refs, sems):
        for cp in build(s_refs, o_refs, sems):
            cp.wait_recv()
            cp.wait_send()

    return _Exchange(list(sends), [jax.ShapeDtypeStruct((4,) + a.shape[1:], a.dtype) for a in sends],
                     [pltpu.SemaphoreType.DMA((4 * n,)), pltpu.SemaphoreType.DMA((4 * n,))], start, finish)


def _plan_chips(ts):
    n = len(ts)
    flips = ((1, 0), (0, 1), (1, 1))

    def build(t_refs, o_refs, sems):
        send_sems, recv_sems, local_sems = sems
        x, y, c = _xyc()
        mine = 2 * x + y
        local = [pltpu.make_async_copy(t_refs[a].at[mine], o_refs[a].at[mine], local_sems.at[a]) for a in range(n)]
        remote = []
        for k, (fx, fy) in enumerate(flips):
            px = 1 - x if fx else x
            py = 1 - y if fy else y
            remote += [pltpu.make_async_remote_copy(
                src_ref=t_refs[a].at[2 * px + py], dst_ref=o_refs[a].at[mine],
                send_sem=send_sems.at[k * n + a], recv_sem=recv_sems.at[k * n + a], device_id=(px, py, c),
                device_id_type=MESH) for a in range(n)]
        return local, remote

    def start(t_refs, o_refs, sems):
        local, remote = build(t_refs, o_refs, sems)
        for cp in local + remote:
            cp.start()

    def finish(t_refs, o_refs, sems):
        local, remote = build(t_refs, o_refs, sems)
        for cp in remote:
            cp.wait_recv()
        for cp in remote:
            cp.wait_send()
        for cp in local:
            cp.wait()

    return _Exchange(list(ts), [jax.ShapeDtypeStruct(a.shape, a.dtype) for a in ts],
                     [pltpu.SemaphoreType.DMA((3 * n,)), pltpu.SemaphoreType.DMA((3 * n,)),
                      pltpu.SemaphoreType.DMA((n,))], start, finish)


def _combine(*plans):
    def parts(refs, attr):
        out, at = [], 0
        for p in plans:
            n = len(getattr(p, attr))
            out.append(refs[at:at + n])
            at += n
        return out

    def run(half):
        def go(ins, outs, sems):
            for p, a, o, s in zip(plans, parts(ins, "ins"), parts(outs, "outs"), parts(sems, "scratch")):
                getattr(p, half)(a, o, s)
        return go

    return _Exchange(sum((p.ins for p in plans), []), sum((p.outs for p in plans), []),
                     sum((p.scratch for p in plans), []), run("start"), run("finish"))


def _exchange_call(plan, name):
    n = len(plan.ins)

    def body(*refs):
        ins, outs, sems = refs[:n], refs[n:2 * n], refs[2 * n:]
        plan.start(ins, outs, sems)
        plan.finish(ins, outs, sems)

    return pl.pallas_call(
        body, name=name, out_shape=plan.outs,
        in_specs=[pl.BlockSpec(memory_space=pl.ANY)] * n, out_specs=[pl.BlockSpec(memory_space=pl.ANY)] * n,
        scratch_shapes=plan.scratch,
    )(*plan.ins)


def _slab_spec(lead, rows, cols, nb):
    if rows % (nb * 16) == 0:
        return pl.BlockSpec((lead, rows // nb, cols), lambda i: (0, i, 0))
    if cols % (nb * 128) == 0:
        return pl.BlockSpec((lead, rows, cols // nb), lambda i: (0, 0, i))
    return pl.BlockSpec((lead, rows, cols), lambda i: (0, 0, 0))


def _slab_spec2(rows, cols, nb):
    if rows % (nb * 16) == 0:
        return pl.BlockSpec((rows // nb, cols), lambda i: (i, 0))
    if cols % (nb * 128) == 0:
        return pl.BlockSpec((rows, cols // nb), lambda i: (0, i))
    return pl.BlockSpec((rows, cols), lambda i: (0, 0))


def _cast_call(arrays, name, host=None):
    n = len(arrays)
    nb = 8

    def body(*refs):
        for a in range(n):
            refs[n + a][...] = refs[a][...].astype(BF16)

    specs = [_slab_spec2(x.shape[0], x.shape[1], nb) for x in arrays]
    return _hosting_call(body, name, nb, host, list(arrays), specs,
                         [jax.ShapeDtypeStruct(x.shape, BF16) for x in arrays], specs, [])


def _pair_add(sends, fromsib, name):
    n = len(sends)
    nb = 8

    def body(*refs):
        c = lax.axis_index("c")
        for a in range(n):
            s_ref, f_ref, t_ref = refs[a], refs[n + a], refs[2 * n + a]
            for j in range(4):
                t_ref[j] = (s_ref[2 * j + c].astype(F32) + f_ref[j].astype(F32)).astype(t_ref.dtype)

    def spec(a, lead):
        return _slab_spec(lead, a.shape[1], a.shape[2], nb)

    return pl.pallas_call(
        body, name=name, grid=(nb,),
        in_specs=[spec(a, N_DEV) for a in sends] + [spec(a, 4) for a in fromsib],
        out_specs=[spec(a, 4) for a in fromsib],
        out_shape=[jax.ShapeDtypeStruct(a.shape, a.dtype) for a in fromsib],
        compiler_params=pltpu.CompilerParams(dimension_semantics=("arbitrary",), vmem_limit_bytes=VMEM_LIMIT),
    )(*sends, *fromsib)


def _adamw_vals(w, g, m, v):
    m2 = ADAM_B1 * m + (1.0 - ADAM_B1) * g
    v2 = ADAM_B2 * v + (1.0 - ADAM_B2) * (g * g)
    m_hat = m2 / (1.0 - ADAM_B1 ** ADAM_STEP)
    v_hat = v2 / (1.0 - ADAM_B2 ** ADAM_STEP)
    delta = -ADAM_LR * (m_hat / (jnp.sqrt(v_hat) + ADAM_EPS) + ADAM_WD * w)
    return delta, m2, v2


def _updates_call(recvs, ws, ms, vs, name, host=None):
    n = len(recvs)
    nb = 8

    def body(*refs):
        for a in range(n):
            r_ref, w_ref, m_ref, v_ref = refs[a], refs[n + a], refs[2 * n + a], refs[3 * n + a]
            g_ref, d_ref, m2_ref, v2_ref = refs[4 * n + 4 * a:4 * n + 4 * a + 4]
            g = r_ref[0].astype(F32)
            for d in range(1, r_ref.shape[0]):
                g = g + r_ref[d].astype(F32)
            dl, m2, v2 = _adamw_vals(w_ref[...], g, m_ref[...], v_ref[...])
            g_ref[...] = g
            d_ref[...] = dl
            m2_ref[...] = m2
            v2_ref[...] = v2

    def spec3(r):
        return _slab_spec(r.shape[0], r.shape[1], r.shape[2], nb)

    def spec2(w):
        return _slab_spec2(w.shape[0], w.shape[1], nb)

    res, hosted = _hosting_call(
        body, name, nb, host, list(recvs) + list(ws) + list(ms) + list(vs),
        [spec3(r) for r in recvs] + [spec2(w) for w in ws] * 3,
        [jax.ShapeDtypeStruct(w.shape, F32) for w in ws for _ in range(4)],
        [spec2(w) for w in ws for _ in range(4)], [])
    return [res[4 * a:4 * a + 4] for a in range(n)], hosted


def _small_sum(gath, loss_g, row0_g, name):
    _, R, C = gath.shape
    br = R // 3

    def body(g_ref, l_ref, r_ref, go_ref, lo_ref):
        g = g_ref[0].astype(F32)
        lsum = l_ref[0]
        for d in range(1, N_DEV):
            g = g + g_ref[d].astype(F32)
            lsum = lsum + l_ref[d]
        go_ref[...] = g
        lo_ref[...] = lsum

        @pl.when(pl.program_id(0) == 0)
        def _():
            row0 = r_ref[0]
            for d in range(1, N_DEV):
                row0 = row0 + r_ref[d]
            go_ref[0:8, :] = go_ref[0:8, :] + jnp.where(lax.broadcasted_iota(jnp.int32, row0.shape, 0) == 0, row0, 0.0)

    return pl.pallas_call(
        body, name=name, grid=(R // br,),
        in_specs=[pl.BlockSpec((N_DEV, br, C), lambda i: (0, i, 0)),
                  pl.BlockSpec((N_DEV, 8, HD), lambda i: (0, 0, 0)), pl.BlockSpec((N_DEV, 8, C), lambda i: (0, 0, 0))],
        out_specs=[pl.BlockSpec((br, C), lambda i: (i, 0)), pl.BlockSpec((8, HD), lambda i: (0, 0))],
        out_shape=[jax.ShapeDtypeStruct((R, C), F32), jax.ShapeDtypeStruct((8, HD), F32)],
        compiler_params=pltpu.CompilerParams(dimension_semantics=("arbitrary",)),
    )(gath, loss_g, row0_g)


def _adamw_multi(ws, gs, ms, vs, name, nblk=1):
    n = len(ws)

    def body(*refs):
        for a in range(n):
            dl, m2, v2 = _adamw_vals(refs[a][...], refs[n + a][...], refs[2 * n + a][...], refs[3 * n + a][...])
            refs[4 * n + 3 * a][...] = dl
            refs[4 * n + 3 * a + 1][...] = m2
            refs[4 * n + 3 * a + 2][...] = v2

    def spec(x):
        rest = (0,) * (x.ndim - 1)
        return pl.BlockSpec((x.shape[0] // nblk,) + tuple(x.shape[1:]), lambda i: (i,) + rest)

    res = pl.pallas_call(
        body, name=name, grid=(nblk,),
        in_specs=[spec(w) for w in ws] * 4, out_specs=[spec(w) for w in ws for _ in range(3)],
        out_shape=[jax.ShapeDtypeStruct(w.shape, F32) for w in ws for _ in range(3)],
        compiler_params=pltpu.CompilerParams(dimension_semantics=("arbitrary",), vmem_limit_bytes=VMEM_LIMIT),
    )(*ws, *gs, *ms, *vs)
    return [res[3 * a:3 * a + 3] for a in range(n)]


def _s5_param_fn(lr, li, ls, btr, bti):
    step = jnp.exp(ls)
    er = jnp.exp(lr * step)
    ang = li * step
    ar = er * jnp.cos(ang)
    ai = er * jnp.sin(ang)
    nr = ar - 1.0
    den = lr * lr + li * li
    fr = (nr * lr + ai * li) / den
    fi = (ai * lr - nr * li) / den
    return ar, ai, fr * btr - fi * bti, fr * bti + fi * btr


def _s5_params(lr, li, ls, btr, bti, cre, cim):
    nb = S5_G // S5_GB
    GC = S5_GB * S5_C
    expand = jnp.asarray(np.tile(np.eye(S5_P, dtype=np.float32), (1, S5_GB)), BF16)
    own = jnp.asarray((np.arange(GC)[:, None] // S5_C == np.arange(S5_W)[None, :] // S5_P).astype(np.float32))

    def body(lr_ref, li_ref, ls_ref, br_ref, bi_ref, cr_ref, ci_ref, e_ref, own_ref, ar_ref, ai_ref, bm_ref, cm_ref):
        ar, ai, bbr, bbi = _s5_param_fn(lr_ref[...], li_ref[...], ls_ref[...], br_ref[...], bi_ref[...])
        ar_ref[...] = ar
        ai_ref[...] = ai

        def plane(x, n):
            rows = x[n * S5_GB:(n + 1) * S5_GB].reshape(GC, S5_P).astype(BF16)
            return _dot(rows, e_ref[...]) * own_ref[...]

        for n in range(nb):
            bm_ref[n] = jnp.concatenate([plane(bbr, n), plane(bbi, n)], axis=-1).astype(BF16)
            cm_ref[n] = jnp.concatenate([plane(cr_ref[...], n), -plane(ci_ref[...], n)], axis=-1).astype(BF16)

    sd = jax.ShapeDtypeStruct
    return pl.pallas_call(
        body, name="s5_params",
        out_shape=[sd(lr.shape, F32), sd(lr.shape, F32), sd((nb, GC, 2 * S5_W), BF16), sd((nb, GC, 2 * S5_W), BF16)],
        compiler_params=pltpu.CompilerParams(vmem_limit_bytes=VMEM_LIMIT),
    )(lr, li, ls, btr, bti, cre, cim, expand, own)


def _s5_params_bwd(lr, li, ls, btr, bti, dar, dai, dbbr, dbbi):
    def body(lr_ref, li_ref, ls_ref, br_ref, bi_ref, dar_ref, dai_ref, dbbr_ref, dbbi_ref,
             dlr_ref, dli_ref, dls_ref, dbr_ref, dbi_ref):
        _, vjp = jax.vjp(_s5_param_fn, lr_ref[...], li_ref[...], ls_ref[...], br_ref[...], bi_ref[...])
        dlr, dli, dls, dbr, dbi = vjp((dar_ref[...], dai_ref[...], dbbr_ref[...], dbbi_ref[...]))
        dlr_ref[...] = dlr
        dli_ref[...] = dli
        dls_ref[...] = dls
        dbr_ref[...] = dbr
        dbi_ref[...] = dbi

    sd = jax.ShapeDtypeStruct
    return pl.pallas_call(
        body, name="s5_params_bwd",
        out_shape=[sd(lr.shape, F32), sd(lr.shape, F32), sd(ls.shape, F32), sd(btr.shape, F32), sd(btr.shape, F32)],
    )(lr, li, ls, btr, bti, dar, dai, dbbr, dbbi)


def _cpow(ar, ai, n):
    assert n & (n - 1) == 0
    while n > 1:
        ar, ai = ar * ar - ai * ai, 2.0 * ar * ai
        n //= 2
    return ar, ai


def _scan(st, cr, ci, init, nk, reverse, store, prev=None):
    W = S5_W

    def advance(k, sr, si):
        rows = pl.ds(k * 8 if isinstance(k, int) else pl.multiple_of(k * 8, 8), 8)
        nsr = cr * sr - ci * si + st[rows, 0:W]
        nsi = cr * si + ci * sr + st[rows, W:2 * W]
        if store:
            st[rows, 0:W] = nsr
            st[rows, W:2 * W] = nsi
        return nsr, nsi

    if prev is None:
        return lax.fori_loop(0, nk, lambda j, c: advance(nk - 1 - j if reverse else j, c[0], c[1]), init, unroll=2)
    assert reverse

    def step(j, carry):
        k = nk - 1 - j
        nsr, nsi = advance(k, carry[0], carry[1])
        prows = pl.ds(pl.multiple_of((k - 1) * 8, 8), 8)
        pr = prev[prows, 0:W]
        pi = prev[prows, W:2 * W]
        return nsr, nsi, carry[2] + nsr * pr + nsi * pi, carry[3] + nsi * pr - nsr * pi

    carry = lax.fori_loop(0, nk - 1, step, init, unroll=2)
    nsr, nsi = advance(0, carry[0], carry[1])
    return nsr, nsi, carry[2], carry[3]


def _chain(fin, fr, fi, pr, pi, reverse):
    W = S5_W
    fin[:, 0:W] = fr
    fin[:, W:2 * W] = fi
    rowid = lax.broadcasted_iota(jnp.int32, (8, W), 0)
    cr = jnp.zeros((1, W), F32)
    ci = jnp.zeros((1, W), F32)
    init_r = jnp.zeros((8, W), F32)
    init_i = jnp.zeros((8, W), F32)
    for s in (range(7, -1, -1) if reverse else range(8)):
        init_r = jnp.where(rowid == s, cr, init_r)
        init_i = jnp.where(rowid == s, ci, init_i)
        lr = fin[s:s + 1, 0:W]
        li = fin[s:s + 1, W:2 * W]
        cr, ci = lr + pr * cr - pi * ci, li + pr * ci + pi * cr
    return init_r, init_i


def _full_scan(st, fin, ar, ai, nk, reverse, prev=None, carry_in=None, carry_out=None):
    W = S5_W
    cr = jnp.broadcast_to(ar, (8, W))
    ci = jnp.broadcast_to(-ai if reverse else ai, (8, W))
    z = jnp.zeros((8, W), F32)
    if carry_in is None:
        fr, fi = _scan(st, cr, ci, (z, z), nk, reverse, store=False)
        pr, pi = _cpow(ar, -ai if reverse else ai, nk)
        init = _chain(fin, fr, fi, pr, pi, reverse)
    else:
        init = (carry_in[:, 0:W], carry_in[:, W:2 * W])
    if carry_out is not None:
        carry_out[:, 0:W] = init[0]
        carry_out[:, W:2 * W] = init[1]
    if prev is None:
        return _scan(st, cr, ci, init, nk, reverse, store=True)
    return _scan(st, cr, ci, init + (z, z), nk, reverse, store=True, prev=prev)


def _s5_specs(L):
    W2 = 2 * S5_W
    GC = S5_GB * S5_C
    col = pl.BlockSpec((L, GC), lambda g: (0, g))
    vec = pl.BlockSpec((1, GC), lambda g: (0, g))
    avec = pl.BlockSpec((1, S5_W), lambda g: (0, g))
    bmat = pl.BlockSpec((None, GC, W2), lambda g: (g, 0, 0))
    cmat = pl.BlockSpec((None, W2, GC), lambda g: (g, 0, 0))
    return col, vec, avec, bmat, cmat


def _interleave(dst, src, nk):
    for s in range(8):
        dst[pl.ds(s, nk, stride=8), :] = src[s * nk:(s + 1) * nk, :]


def _deinterleave(dst, src, nk):
    for s in range(8):
        dst[s * nk:(s + 1) * nk, :] = src[pl.ds(s, nk, stride=8), :].astype(dst.dtype)


def _hosting_call(body, name, nsteps, host, ins, in_specs, outs, out_specs, scratch):
    grid = (nsteps,) if isinstance(nsteps, int) else tuple(nsteps)
    params = pltpu.CompilerParams(dimension_semantics=("arbitrary",) * len(grid), vmem_limit_bytes=VMEM_LIMIT)
    if host is None:
        res = pl.pallas_call(
            body, name=name, grid=grid, in_specs=in_specs, out_specs=out_specs, out_shape=outs,
            scratch_shapes=scratch, compiler_params=params,
        )(*ins)
        return list(res), []
    n_in, n_out, n_sc = len(ins), len(outs), len(scratch)
    h_in, h_out = len(host.ins), len(host.outs)

    def hosted(*refs):
        a = refs[:n_in]
        ha = refs[n_in:n_in + h_in]
        o = refs[n_in + h_in:n_in + h_in + n_out]
        ho = refs[n_in + h_in + n_out:n_in + h_in + n_out + h_out]
        sc = refs[n_in + h_in + n_out + h_out:n_in + h_in + n_out + h_out + n_sc]
        hs = refs[n_in + h_in + n_out + h_out + n_sc:]
        first = functools.reduce(jnp.logical_and, [pl.program_id(i) == 0 for i in range(len(grid))])
        last = functools.reduce(jnp.logical_and, [pl.program_id(i) == g - 1 for i, g in enumerate(grid)])

        @pl.when(first)
        def _():
            host.start(ha, ho, hs)

        body(*a, *o, *sc)

        @pl.when(last)
        def _():
            host.finish(ha, ho, hs)

    hbm = pl.BlockSpec(memory_space=pl.ANY)
    res = pl.pallas_call(
        hosted, name=name, grid=grid,
        in_specs=list(in_specs) + [hbm] * h_in, out_specs=list(out_specs) + [hbm] * h_out,
        out_shape=list(outs) + list(host.outs), scratch_shapes=list(scratch) + list(host.scratch),
        compiler_params=params,
    )(*ins, *host.ins)
    return list(res[:n_out]), list(res[n_out:])


def _s5_fwd(u, bm, cm, ar, ai, dvec, host=None):
    L = u.shape[0]
    nk = L // 8
    GC = S5_GB * S5_C
    nb = S5_G // S5_GB
    col, vec, avec, bmat, cmat = _s5_specs(L)

    def body(u_ref, b_ref, c_ref, ar_ref, ai_ref, d_ref, y_ref, carry_ref, st, fin, ui, yi):
        _interleave(ui, u_ref, nk)
        for r in range(8):
            rows = slice(r * nk, (r + 1) * nk)
            st[rows, :] = _dot(ui[rows, :].astype(BF16), b_ref[...])
        _full_scan(st, fin, ar_ref[...], ai_ref[...], nk, reverse=False, carry_out=carry_ref)
        for r in range(8):
            rows = slice(r * nk, (r + 1) * nk)
            yi[rows, :] = _dot_nt(st[rows, :].astype(BF16), c_ref[...]) + d_ref[...] * ui[rows, :]
        _deinterleave(y_ref, yi, nk)

    return _hosting_call(
        body, "s5_fwd", nb, host,
        [u, bm, cm, ar, ai, dvec], [col, bmat, bmat, avec, avec, vec],
        [jax.ShapeDtypeStruct(u.shape, F32), jax.ShapeDtypeStruct((nb * 8, 2 * S5_W), F32)],
        [col, pl.BlockSpec((8, 2 * S5_W), lambda g: (g, 0))],
        [pltpu.VMEM((L, 2 * S5_W), F32), pltpu.VMEM((8, 2 * S5_W), F32), pltpu.VMEM((L, GC), F32),
         pltpu.VMEM((L, GC), F32)])


def _s5_bwd(u, dy, carry, bm, cm, ar, ai, dvec, mask, rmat, host=None):
    L = u.shape[0]
    nk = L // 8
    W = S5_W
    GC = S5_GB * S5_C
    col, vec, avec, bmat, cmat = _s5_specs(L)
    hi = lax.Precision.HIGHEST

    def body(u_ref, dy_ref, carry_ref, b_ref, ct_ref, ar_ref, ai_ref, d_ref, mask_ref, r_ref,
             du_ref, db_ref, dc_ref, dd_ref, dar_ref, dai_ref, sa, sb, fin, ui, dyi, dui):
        ar = ar_ref[...]
        ai = ai_ref[...]
        _interleave(ui, u_ref, nk)
        _interleave(dyi, dy_ref, nk)
        for r in range(8):
            rows = slice(r * nk, (r + 1) * nk)
            sa[rows, :] = _dot(ui[rows, :].astype(BF16), b_ref[...])
            sb[rows, :] = _dot(dyi[rows, :].astype(BF16), ct_ref[...])
        _full_scan(sa, fin, ar, ai, nk, reverse=False, carry_in=carry_ref)
        gr, gi, accr, acci = _full_scan(sb, fin, ar, ai, nk, reverse=True, prev=sa)
        rowid = lax.broadcasted_iota(jnp.int32, (8, W), 0)
        last = pl.ds((nk - 1) * 8, 8)
        pr = jnp.where(rowid == 0, 0.0, pltpu.roll(sa[last, 0:W], 1, 0))
        pi = jnp.where(rowid == 0, 0.0, pltpu.roll(sa[last, W:2 * W], 1, 0))
        accr = accr + gr * pr + gi * pi
        acci = acci + gi * pr - gr * pi
        dar_ref[...] = jnp.sum(accr, axis=0, keepdims=True)
        dai_ref[...] = jnp.sum(acci, axis=0, keepdims=True)
        dbf = jnp.zeros((GC, 2 * W), F32)
        dcf = jnp.zeros((GC, 2 * W), F32)
        dd = jnp.zeros((1, GC), F32)
        for r in range(8):
            rows = slice(r * nk, (r + 1) * nk)
            ub = ui[rows, :]
            dyb = dyi[rows, :]
            gb = sb[rows, :].astype(BF16)
            dui[rows, :] = _dot_nt(gb, b_ref[...]) + d_ref[...] * dyb
            dbf = dbf + _dot_tn(ub.astype(BF16), gb)
            dcf = dcf + _dot_tn(dyb.astype(BF16), sa[rows, :].astype(BF16))
            dd = dd + jnp.sum(dyb * ub, axis=0, keepdims=True)
        db_ref[...] = jnp.dot(dbf * mask_ref[...], r_ref[...], precision=hi, preferred_element_type=F32)
        dc_ref[...] = jnp.dot(dcf * mask_ref[...], r_ref[...], precision=hi, preferred_element_type=F32)
        dd_ref[...] = dd
        _deinterleave(du_ref, dui, nk)

    cmp_spec = pl.BlockSpec((GC, 2 * S5_P), lambda g: (g, 0))
    whole = lambda shape: pl.BlockSpec(shape, lambda g: (0, 0))
    sd = jax.ShapeDtypeStruct
    return _hosting_call(
        body, "s5_bwd", S5_G // S5_GB, host,
        [u, dy, carry, bm, cm, ar, ai, dvec, mask, rmat],
        [col, col, pl.BlockSpec((8, 2 * W), lambda g: (g, 0)), bmat, bmat, avec, avec, vec, whole(mask.shape),
         whole(rmat.shape)],
        [sd(u.shape, BF16), sd((S5_G * S5_C, 2 * S5_P), F32), sd((S5_G * S5_C, 2 * S5_P), F32),
         sd((1, PRIM), F32), sd((1, S5_G * S5_P), F32), sd((1, S5_G * S5_P), F32)],
        [col, cmp_spec, cmp_spec, vec, avec, avec],
        [pltpu.VMEM((L, 2 * W), F32), pltpu.VMEM((L, 2 * W), F32), pltpu.VMEM((8, 2 * W), F32),
         pltpu.VMEM((L, GC), F32), pltpu.VMEM((L, GC), F32), pltpu.VMEM((L, GC), F32)])


def _s5_compact_consts():
    g_row = np.arange(S5_GB * S5_C) // S5_C
    col = np.arange(2 * S5_W)
    g_col = (col % S5_W) // S5_P
    mask = (g_row[:, None] == g_col[None, :]).astype(np.float32)
    tgt = (col // S5_W) * S5_P + col % S5_P
    rmat = (tgt[:, None] == np.arange(2 * S5_P)[None, :]).astype(np.float32)
    return jnp.asarray(mask), jnp.asarray(rmat)


def _attn_scores(q_ref, k_ref, qb, bq, scale):
    ext = (qb + 1) * bq
    s = _dot_nt(q_ref[qb * bq:ext, :], k_ref[0:ext, :]) * scale
    qpos = lax.broadcasted_iota(jnp.int32, (bq, bq), 0)
    kpos = lax.broadcasted_iota(jnp.int32, (bq, bq), 1)
    diag = jnp.where(kpos <= qpos, s[:, ext - bq:], NEG)
    return diag if qb == 0 else jnp.concatenate([s[:, :ext - bq], diag], axis=-1)


def _attn_fwd(qp, kp, v, scale):
    L = qp.shape[0]
    bq = min(256, L)

    def body(q_ref, k_ref, v_ref, o_ref, lse_ref):
        for qb in range(L // bq):
            rows = slice(qb * bq, (qb + 1) * bq)
            s = _attn_scores(q_ref, k_ref, qb, bq, scale)
            m = jnp.max(s, axis=-1, keepdims=True)
            e = jnp.exp(s - m)
            l = jnp.sum(e, axis=-1, keepdims=True)
            o_ref[rows, :] = _dot(e.astype(BF16), v_ref[0:(qb + 1) * bq, :]) / l
            lse_ref[rows, :] = jnp.broadcast_to(m + jnp.log(l), (bq, HD))

    blk = pl.BlockSpec((L, HD), lambda h: (0, h))
    wide = pl.BlockSpec((L, 2 * HD), lambda h: (0, h))
    return pl.pallas_call(
        body, name="mla_attn_fwd", grid=(MLA_H,),
        in_specs=[wide, wide, blk], out_specs=[blk, blk],
        out_shape=[jax.ShapeDtypeStruct((L, MLA_H * HD), F32)] * 2,
        compiler_params=pltpu.CompilerParams(dimension_semantics=("arbitrary",), vmem_limit_bytes=VMEM_LIMIT),
    )(qp, kp, v)


def _attn_bwd(qp, kp, v, o, lse, do, scale):
    L = qp.shape[0]
    bq = min(256, L)
    nq = L // bq

    def body(q_ref, k_ref, v_ref, o_ref, lse_ref, do_ref, dq_ref, dk_ref, dv_ref, dk_acc, dv_acc):
        dk_acc[...] = jnp.zeros_like(dk_acc)
        dv_acc[...] = jnp.zeros_like(dv_acc)
        for qb in range(nq):
            rows = slice(qb * bq, (qb + 1) * bq)
            ext = (qb + 1) * bq
            do = do_ref[rows, :]
            dob = do.astype(BF16)
            p = jnp.exp(_attn_scores(q_ref, k_ref, qb, bq, scale) - lse_ref[rows, 0:1])
            dp = _dot_nt(dob, v_ref[0:ext, :])
            dsum = jnp.sum(do * o_ref[rows, :], axis=-1, keepdims=True)
            ds = (p * (dp - dsum) * scale).astype(BF16)
            dq_ref[rows, :] = _dot(ds, k_ref[0:ext, :]).astype(dq_ref.dtype)
            dk_acc[0:ext, :] += _dot_tn(ds, q_ref[rows, :])
            dv_acc[0:ext, :] += _dot_tn(p.astype(BF16), dob)
        dk_ref[...] = dk_acc[...].astype(dk_ref.dtype)
        dv_ref[...] = dv_acc[...].astype(dv_ref.dtype)

    sd = jax.ShapeDtypeStruct
    blk = pl.BlockSpec((L, HD), lambda h: (0, h))
    wide = pl.BlockSpec((L, 2 * HD), lambda h: (0, h))
    return pl.pallas_call(
        body, name="mla_attn_bwd", grid=(MLA_H,),
        in_specs=[wide, wide, blk, blk, blk, blk], out_specs=[wide, wide, blk],
        out_shape=[sd((L, MLA_H * 2 * HD), BF16), sd((L, MLA_H * 2 * HD), BF16), sd((L, MLA_H * HD), BF16)],
        scratch_shapes=[pltpu.VMEM((L, 2 * HD), F32), pltpu.VMEM((L, HD), F32)],
        compiler_params=pltpu.CompilerParams(dimension_semantics=("arbitrary",), vmem_limit_bytes=VMEM_LIMIT),
    )(qp, kp, v, o, lse, do)


def _kv_fn(mem, gm, w, gk):
    kv = _mm(_rms(mem, gm, D_MODEL), w)
    k = jnp.concatenate([_rms(kv[:, HD * h:HD * (h + 1)], gk, HD) for h in range(X_HEADS)], axis=-1)
    return k, kv[:, XQ:]


def _kv_prep(mem, gm, w, gk, name):
    def fn(mem, gm, w, gk):
        return _kv_fn(mem, gm, w, gk)
    M = mem.shape[0]
    return _rowwise(name, fn, [('c', mem), ('c', gm), ('c', w), ('c', gk)],
                    [('c', (M, XQ), F32), ('c', (M, XQ), F32)], 1)


def _kv_prep_bwd(mem, gm, w, gk, dk, dv, name):
    def fn(mem, gm, w, gk, dk, dv):
        _, vjp = jax.vjp(lambda a, b, c: _kv_fn(mem, a, b, c), gm, w, gk)
        return vjp((dk, dv))
    return _rowwise(name, fn, [('c', mem), ('c', gm), ('c', w), ('c', gk), ('c', dk), ('c', dv)],
                    [('c', gm.shape, F32), ('c', w.shape, BF16), ('c', gk.shape, F32)], 1)


def _forward_merge(x, mix, mix_kind, xq, gate, k, v, gq, wout, name, nblk, host=None):
    def fn(x, mix, xq, gate, k, v, gq, wout):
        o = _merge(mix, xq, gate, k, v, gq)
        return (x + _dot(o.astype(BF16), wout),)
    L = x.shape[0]
    out = _rowwise(name, fn, [('r', x), (mix_kind, mix), ('r', xq), ('r', gate), ('c', k), ('c', v), ('c', gq),
                              ('c', wout)], [('r', (L, D_MODEL), F32)], nblk, host=host)
    return out[0] if host is None else (out[0][0], out[1])


def _backward_merge(dx, mix, mix_kind, xq, gate, k, v, gq, wout, name, nblk, host=None):
    def fn(dx, mix, xq, gate, k, v, gq, wout):
        g16 = dx.astype(BF16)
        do = _dot_nt(g16, wout)
        o, vjp = jax.vjp(_merge, mix, xq, gate, k, v, gq)
        dmix, dxq, dgate, dk, dv, dgq = vjp(do)
        return dmix, dxq, dgate, o, g16, dk, dv, dgq
    L = dx.shape[0]
    return _rowwise(
        name, fn,
        [('r', dx), (mix_kind, mix), ('r', xq), ('r', gate), ('c', k), ('c', v), ('c', gq), ('c', wout)],
        [('r', (L, PRIM), F32), ('r', (L, XQ), BF16), ('r', (L, BRANCH), BF16), ('t', (BRANCH, L), BF16),
         ('r', (L, D_MODEL), BF16), ('a', k.shape, F32), ('a', v.shape, F32), ('a', gq.shape, F32)], nblk,
        host=host)


_MLA_IN = 3392
_MLA_IN_PAD = 3456


def _uq_rows(wt):
    r = wt.reshape(MLA_H, HD + ROPE, wt.shape[1])
    return jnp.concatenate([r[:, :HD].reshape(PRIM, -1),
                            jnp.pad(r[:, HD:], ((0, 0), (0, HD - ROPE), (0, 0))).reshape(PRIM, -1)], axis=0)


_UQ_ROW_BANDS = [band for h in range(MLA_H) for band in ((h * HD, HD), (PRIM + h * HD, ROPE))]


_MLA_IN_ROW_BANDS = [(0, 768), (3328, 64), (768, 2560)]


_SMALL = (("ln_gain", 2048), ("mem_norm", 2048), ("xq_norm", 256), ("xk_norm", 256), ("s5_lambda_re", 6144),
          ("s5_lambda_im", 6144), ("s5_log_step", 96), ("s5_b_re", 98304), ("s5_b_im", 98304), ("s5_c_re", 98304),
          ("s5_c_im", 98304), ("s5_d", 1536), ("mla_q_lora_norm", 512), ("mla_kv_lora_norm", 256),
          ("mla_q_nope_norm", 128), ("mla_k_nope_norm", 128), ("mla_q_rope_norm", 64), ("mla_k_rope_norm", 64))
_SMALL_ROWS = 432
_SMALL_OFF = {name: sum(n for _, n in _SMALL[:i]) for i, (name, _) in enumerate(_SMALL)}


def _pack_small(d):
    flat = jnp.concatenate([d[n].reshape(-1).astype(F32) for n, _ in _SMALL])
    return jnp.pad(flat, (0, _SMALL_ROWS * 1024 - flat.shape[0])).reshape(_SMALL_ROWS, 1024)


def _unpack_small(p, name, shape):
    off = _SMALL_OFF[name]
    return p.reshape(-1)[off:off + int(np.prod(shape))].reshape(shape)


_WEIGHTS = ('ln_gain', 'w_out', 'mem_norm', 'w_mem_kv', 'xq_norm', 'xk_norm', 's5_w_in', 's5_lambda_re',
            's5_lambda_im', 's5_log_step', 's5_b_re', 's5_b_im', 's5_c_re', 's5_c_im', 's5_d', 's5_w_glu', 'mla_w_in',
            'mla_q_lora_norm', 'mla_kv_lora_norm', 'mla_w_uq', 'mla_w_ukv', 'mla_q_nope_norm', 'mla_k_nope_norm',
            'mla_q_rope_norm', 'mla_k_rope_norm')


def _pad128(g):
    return jnp.pad(g.reshape(1, -1), ((0, 0), (0, HD - g.shape[-1])))


def kernel(x, mem, positions, ln_gain, w_out, mem_norm, w_mem_kv, xq_norm, xk_norm, s5_w_in, s5_lambda_re, s5_lambda_im, s5_log_step, s5_b_re, s5_b_im, s5_c_re, s5_c_im, s5_d, s5_w_glu, mla_w_in, mla_q_lora_norm, mla_kv_lora_norm, mla_w_uq, mla_w_ukv, mla_q_nope_norm, mla_k_nope_norm, mla_q_rope_norm, mla_k_rope_norm, loss_target, m_ln_gain, m_w_out, m_mem_norm, m_w_mem_kv, m_xq_norm, m_xk_norm, m_s5_w_in, m_s5_lambda_re, m_s5_lambda_im, m_s5_log_step, m_s5_b_re, m_s5_b_im, m_s5_c_re, m_s5_c_im, m_s5_d, m_s5_w_glu, m_mla_w_in, m_mla_q_lora_norm, m_mla_kv_lora_norm, m_mla_w_uq, m_mla_w_ukv, m_mla_q_nope_norm, m_mla_k_nope_norm, m_mla_q_rope_norm, m_mla_k_rope_norm, v_ln_gain, v_w_out, v_mem_norm, v_w_mem_kv, v_xq_norm, v_xk_norm, v_s5_w_in, v_s5_lambda_re, v_s5_lambda_im, v_s5_log_step, v_s5_b_re, v_s5_b_im, v_s5_c_re, v_s5_c_im, v_s5_d, v_s5_w_glu, v_mla_w_in, v_mla_q_lora_norm, v_mla_kv_lora_norm, v_mla_w_uq, v_mla_w_ukv, v_mla_q_nope_norm, v_mla_k_nope_norm, v_mla_q_rope_norm, v_mla_k_rope_norm):
    weights = dict(ln_gain=ln_gain, w_out=w_out, mem_norm=mem_norm, w_mem_kv=w_mem_kv, xq_norm=xq_norm,
                   xk_norm=xk_norm, s5_w_in=s5_w_in, s5_lambda_re=s5_lambda_re, s5_lambda_im=s5_lambda_im,
                   s5_log_step=s5_log_step, s5_b_re=s5_b_re, s5_b_im=s5_b_im, s5_c_re=s5_c_re, s5_c_im=s5_c_im,
                   s5_d=s5_d, s5_w_glu=s5_w_glu, mla_w_in=mla_w_in, mla_q_lora_norm=mla_q_lora_norm,
                   mla_kv_lora_norm=mla_kv_lora_norm, mla_w_uq=mla_w_uq, mla_w_ukv=mla_w_ukv,
                   mla_q_nope_norm=mla_q_nope_norm, mla_k_nope_norm=mla_k_nope_norm,
                   mla_q_rope_norm=mla_q_rope_norm, mla_k_rope_norm=mla_k_rope_norm)
    m_in = dict(zip(_WEIGHTS, (m_ln_gain, m_w_out, m_mem_norm, m_w_mem_kv, m_xq_norm, m_xk_norm, m_s5_w_in,
                               m_s5_lambda_re, m_s5_lambda_im, m_s5_log_step, m_s5_b_re, m_s5_b_im, m_s5_c_re,
                               m_s5_c_im, m_s5_d, m_s5_w_glu, m_mla_w_in, m_mla_q_lora_norm, m_mla_kv_lora_norm,
                               m_mla_w_uq, m_mla_w_ukv, m_mla_q_nope_norm, m_mla_k_nope_norm, m_mla_q_rope_norm,
                               m_mla_k_rope_norm)))
    v_in = dict(zip(_WEIGHTS, (v_ln_gain, v_w_out, v_mem_norm, v_w_mem_kv, v_xq_norm, v_xk_norm, v_s5_w_in,
                               v_s5_lambda_re, v_s5_lambda_im, v_s5_log_step, v_s5_b_re, v_s5_b_im, v_s5_c_re,
                               v_s5_c_im, v_s5_d, v_s5_w_glu, v_mla_w_in, v_mla_q_lora_norm, v_mla_kv_lora_norm,
                               v_mla_w_uq, v_mla_w_ukv, v_mla_q_nope_norm, v_mla_k_nope_norm, v_mla_q_rope_norm,
                               v_mla_k_rope_norm)))

    x0 = x[0]
    mem0 = mem[0]
    target = loss_target[0]
    L = x0.shape[0]
    nblk = 4
    nb_big = 8
    me = 4 * lax.axis_index("x") + 2 * lax.axis_index("y") + lax.axis_index("c")

    lora = jnp.pad(jnp.concatenate([mla_q_lora_norm, mla_kv_lora_norm], axis=1), ((0, 7), (0, HD - 96)))
    def gather(*shards):
        return _plan_all_gather(list(shards))

    kh = D_MODEL // 2
    (b_mkv0, b_glu, b_in_mla, b_out0, b_uq, b_ukv, b_mkv1, b_out1), (W_in_s5,) = _cast_call(
        [w_mem_kv[0], s5_w_glu[0], jnp.transpose(mla_w_in[0]), w_out[0], jnp.transpose(mla_w_uq[0]), mla_w_ukv[0],
         w_mem_kv[1], w_out[1]], "cast_shards", host=gather(s5_w_in[0].astype(BF16)))

    ln0, ln1 = ln_gain[0:1], ln_gain[1:2]
    gq0, gq1 = xq_norm[0:1], xq_norm[1:2]
    gk0, gk1 = xk_norm[0:1], xk_norm[1:2]
    gm0, gm1 = mem_norm[0:1], mem_norm[1:2]
    gqn, gkn = mla_q_nope_norm, mla_k_nope_norm
    gqr, gkr = _pad128(mla_q_rope_norm), _pad128(mla_k_rope_norm)

    lr3 = s5_lambda_re.reshape(S5_G, 1, S5_P)
    li3 = s5_lambda_im.reshape(S5_G, 1, S5_P)
    ls3 = s5_log_step.reshape(S5_G, 1, 1)
    btr = jnp.swapaxes(s5_b_re[0], 1, 2)
    bti = jnp.swapaxes(s5_b_im[0], 1, 2)
    a_r, a_i, bm, cm = _s5_params(lr3, li3, ls3, btr, bti, s5_c_re[0], s5_c_im[0])
    a_r2 = a_r.reshape(1, S5_G * S5_P)
    a_i2 = a_i.reshape(1, S5_G * S5_P)
    cmask, rmat = _s5_compact_consts()

    half = ROPE // 2
    inv_freq = ROPE_THETA ** (-jnp.arange(half, dtype=F32) / half)
    invf = jnp.concatenate([inv_freq, inv_freq, jnp.zeros((HD - ROPE,), F32)]).reshape(1, HD)

    def rot_tables(pos, invf):
        ang = pos.astype(F32) * invf
        lane = lax.broadcasted_iota(jnp.int32, ang.shape, 1)
        c = jnp.where(lane < ROPE, jnp.cos(ang), 0.0)
        s = jnp.sin(ang)
        return c, jnp.where(lane < half, -s, 0.0), jnp.where((lane >= half) & (lane < ROPE), s, 0.0)

    tc, ts1, ts2 = _rowwise("rot_tables", rot_tables, [('r', positions.reshape(L, 1)), ('c', invf)],
                            [('r', (L, HD), F32)] * 3, nblk)

    def in_s5(x, g, w):
        proj = _mm_slots(_rms(x, g, D_MODEL).astype(BF16), w)
        return proj[:, :PRIM], proj[:, PRIM:PRIM + XQ], proj[:, PRIM + XQ:]

    u_s5, xq_a, gate_a = _rowwise(
        "s5_in", in_s5, [('r', x0), ('c', ln0), ('c', W_in_s5)],
        [('r', (L, PRIM), F32), ('r', (L, XQ), F32), ('r', (L, BRANCH), F32)], nblk)
    (y_s5, s5_carry), (W_glu, G_mkv0, G_in_mla_a) = _s5_fwd(u_s5, bm, cm, a_r2, a_i2, s5_d,
                                                            host=gather(b_glu, b_mkv0, b_in_mla[:, :kh]))

    def glu(y, w):
        z = _mm_slots(_gelu(y).astype(BF16), w)
        return z[:, :PRIM] * _sigmoid(z[:, PRIM:]), z

    (y2, z_glu), (G_out0,) = _rowwise("s5_glu", glu, [('r', y_s5), ('c', W_glu)],
                                      [('r', (L, PRIM), F32), ('r', (L, 2 * PRIM), F32)], nblk, host=gather(b_out0))
    W_mkv0 = G_mkv0.reshape(D_MODEL, 2 * XQ)
    k_a, v_a = _kv_prep(mem0, gm0, W_mkv0, gk0, "kv_prep0")
    x1, (G_in_mla_b,) = _forward_merge(
        x0, y2, 'r', xq_a, gate_a, k_a, v_a, gq0, G_out0.reshape(BRANCH, D_MODEL), "merge0", nblk,
        host=gather(b_in_mla[:, kh:]))
    W_in_mla = [G.reshape(_MLA_IN, kh) for G in (G_in_mla_a, G_in_mla_b)]

    def in_mla(x, g, wa, wb):
        xn = _rms(x, g, D_MODEL).astype(BF16)

        def band(lo, hi):
            return _dot_nt(xn[:, :kh], wa[lo:hi]) + _dot_nt(xn[:, kh:], wb[lo:hi])

        a = band(0, 768)
        kx = band(768, 896)
        b = band(832, _MLA_IN)
        lane = lax.broadcasted_iota(jnp.int32, kx.shape, 1)
        return a[:, :512], a[:, 512:], b[:, :XQ], b[:, XQ:], jnp.where(lane < ROPE, kx, 0.0)

    (c_q, c_kv, xq_b, gate_b, krp), (G_uq, W_kv, G_lora) = _rowwise(
        "mla_in", in_mla, [('r', x1), ('c', ln1), ('c', W_in_mla[0]), ('c', W_in_mla[1])],
        [('r', (L, Q_LORA), F32), ('r', (L, KV_LORA), F32), ('r', (L, XQ), F32), ('r', (L, BRANCH), F32),
         ('r', (L, HD), F32)], nblk,
        host=gather(b_uq, b_ukv, lora))
    W_q = _uq_rows(G_uq.reshape(MLA_H * (HD + ROPE), Q_LORA))
    g_qlora = G_lora[:, 0, :64].reshape(1, Q_LORA)
    g_kvlora = G_lora[:, 0, 64:96].reshape(1, KV_LORA)

    def qkv(c_q, c_kv, krp, tc, ts1, ts2, gql, gkvl, wq, wkv, gqn, gkn, gqr, gkr):
        q = _dot_nt(_rms(c_q, gql, Q_LORA).astype(BF16), wq)
        kv = _mm_slots(_rms(c_kv, gkvl, KV_LORA).astype(BF16), wkv)
        kp, v = _kv_post(*_kv_chunks(kv), krp, gkn, gkr, tc, ts1, ts2)
        return _q_post(*_q_chunks(q), gqn, gqr, tc, ts1, ts2), kp, v

    qkv_consts = [('c', g_qlora), ('c', g_kvlora), ('c', W_q), ('c', W_kv), ('c', gqn), ('c', gkn), ('c', gqr),
                  ('c', gkr)]
    (q_pad, k_pad, v_h), (G_mkv1, G_out1) = _rowwise(
        "mla_qkv", qkv, [('r', c_q), ('r', c_kv), ('r', krp), ('r', tc), ('r', ts1), ('r', ts2)] + qkv_consts,
        [('r', (L, 2 * PRIM), BF16), ('r', (L, 2 * PRIM), BF16), ('r', (L, PRIM), BF16)], nblk,
        host=gather(b_mkv1, b_out1))
    W_out = (G_out0.reshape(BRANCH, D_MODEL), G_out1.reshape(BRANCH, D_MODEL))
    W_mkv = (W_mkv0, G_mkv1.reshape(D_MODEL, 2 * XQ))
    scale = (HD + ROPE) ** -0.5
    attn, lse = _attn_fwd(q_pad, k_pad, v_h, scale)
    k_b, v_b = _kv_prep(mem0, gm1, W_mkv[1], gk1, "kv_prep1")

    def merge_loss(x, mix, xq, gate, k, v, gq, wout, t):
        err = x + _dot(_merge(mix, xq, gate, k, v, gq).astype(BF16), wout) - t
        part = 0.5 * jnp.sum(jnp.sum(err * err, axis=-1, keepdims=True) * (1.0 / D_MODEL), axis=0, keepdims=True)
        return err * (1.0 / D_MODEL), jnp.broadcast_to(part, (1, HD))

    dx2, loss_part = _rowwise(
        "merge1_loss", merge_loss,
        [('r', x1), ('r', attn), ('r', xq_b), ('r', gate_b), ('c', k_b), ('c', v_b), ('c', gq1), ('c', W_out[1]),
         ('r', target)], [('r', (L, D_MODEL), F32), ('a', (1, HD), F32)], nblk)

    dattn, dxq_b, dgate_b, o_b, g_b, dk_b, dv_b, dgq1 = _backward_merge(
        dx2, attn, 'r', xq_b, gate_b, k_b, v_b, gq1, W_out[1], "merge1_bwd", nb_big)
    dgm1, dW_mkv1, dgk1 = _kv_prep_bwd(mem0, gm1, W_mkv[1], gk1, dk_b, dv_b, "kv_prep1_bwd")
    dW_out1 = _matmul_tn(o_b, g_b, "dw_out1")
    dq_pad, dk_pad, dv_h = _attn_bwd(q_pad, k_pad, v_h, attn, lse, dattn, scale)

    def qkv_bwd(c_q, c_kv, krp, tc, ts1, ts2, dqp, dkp, dv, gql, gkvl, wq, wkv, gqn, gkn, gqr, gkr):
        cqn, vjp_qn = jax.vjp(lambda a, b: _rms(a, b, Q_LORA), c_q, gql)
        ckvn, vjp_kvn = jax.vjp(lambda a, b: _rms(a, b, KV_LORA), c_kv, gkvl)
        cqn16 = cqn.astype(BF16)
        ckvn16 = ckvn.astype(BF16)
        q = _dot_nt(cqn16, wq)
        kv = _mm_slots(ckvn16, wkv)
        _, vjp_q = jax.vjp(lambda n, r, a, b: _q_post(n, r, a, b, tc, ts1, ts2), *_q_chunks(q), gqn, gqr)
        dnope, drope, dgqn, dgqr = vjp_q(dqp.astype(F32))
        dq = jnp.concatenate(dnope + drope, axis=-1)
        _, vjp_kv = jax.vjp(lambda n, v, k, a, b: _kv_post(n, v, k, a, b, tc, ts1, ts2), *_kv_chunks(kv), krp, gkn,
                            gkr)
        dkn, dvals, dkrp, dgkn, dgkr = vjp_kv((dkp.astype(F32), dv.astype(F32)))
        dkv = jnp.concatenate([x for pair in zip(dkn, dvals) for x in pair], axis=-1)
        dq16 = dq.astype(BF16)
        dkv16 = dkv.astype(BF16)
        dc_q, dgql = vjp_qn(_dot(dq16, wq))
        dc_kv, dgkvl = vjp_kvn(_mm_slots_nt(dkv16, wkv))
        return dc_q, dc_kv, dkrp, cqn16, dq16, ckvn16, dkv16, dgql, dgkvl, dgqn, dgkn, dgqr, dgkr

    (dc_q, dc_kv, dkrp, cqn16, dq16, ckvn16, dkv16, dgql, dgkvl, dgqn, dgkn, dgqr, dgkr) = _rowwise(
        "mla_qkv_bwd", qkv_bwd,
        [('r', c_q), ('r', c_kv), ('r', krp), ('r', tc), ('r', ts1), ('r', ts2), ('r', dq_pad), ('r', dk_pad),
         ('r', dv_h)] + qkv_consts,
        [('r', (L, Q_LORA), BF16), ('r', (L, KV_LORA), BF16), ('r', (L, HD), BF16), ('r', (L, Q_LORA), BF16),
         ('t', (2 * PRIM, L), BF16), ('t', (KV_LORA, L), BF16), ('r', (L, 2 * PRIM), BF16),
         ('a', (1, Q_LORA), F32), ('a', (1, KV_LORA), F32), ('a', (1, HD), F32), ('a', (1, HD), F32),
         ('a', (1, HD), F32), ('a', (1, HD), F32)], nb_big)
    dW_q = _matmul_tn(dq16, cqn16, "dw_uq", row_bands=_UQ_ROW_BANDS)
    dW_kv = _matmul_tn_slots(ckvn16, dkv16, "dw_ukv")

    def in_bwd(n_w, x, dres, g, *rest):
        ws, dparts = rest[:n_w], rest[n_w:]
        dproj = jnp.concatenate(dparts, axis=-1).astype(BF16)
        xn, vjp = jax.vjp(lambda a, b: _rms(a, b, D_MODEL), x, g)
        if n_w == 1:
            dxn = _mm_slots_nt(dproj, ws[0])
        else:
            dkr = dproj[:, 3328:]
            dkr = jnp.where(lax.broadcasted_iota(jnp.int32, dkr.shape, 1) < ROPE, dkr, jnp.zeros_like(dkr))
            dxn = jnp.concatenate(
                [_dot(dproj[:, :768], w[0:768]) + _dot(dproj[:, 768:3328], w[832:_MLA_IN]) + _dot(dkr, w[768:896])
                 for w in ws], axis=-1)
        dx, dg = vjp(dxn)
        return dx + dres, xn, dproj, dg

    dx1, xn1, dproj1, dln1 = _rowwise(
        "mla_in_bwd", functools.partial(in_bwd, 2),
        [('r', x1), ('r', dx2), ('c', ln1), ('c', W_in_mla[0]), ('c', W_in_mla[1]), ('r', dc_q), ('r', dc_kv), ('r', dxq_b), ('r', dgate_b),
         ('r', dkrp)],
        [('r', (L, D_MODEL), F32), ('r', (L, D_MODEL), BF16), ('t', (_MLA_IN_PAD, L), BF16), ('a', (1, D_MODEL), F32)],
        nblk)
    dW_in_mla = _matmul_tn(dproj1, xn1, "dw_mla_in", row_bands=_MLA_IN_ROW_BANDS)

    grads1 = [dW_out1.reshape(N_DEV, 256, D_MODEL), dW_mkv1.reshape(N_DEV, 128, 2 * XQ),
              dW_in_mla.reshape(N_DEV, 424, D_MODEL),
              dW_q.reshape(N_DEV, 288, Q_LORA), dW_kv]
    (dy2, dxq_a, dgate_a, o_a, g_a, dk_a, dv_a, dgq0), pair1 = _backward_merge(
        dx1, y2, 'r', xq_a, gate_a, k_a, v_a, gq0, W_out[0], "merge0_bwd", nb_big, host=_plan_pair(grads1))
    dgm0, dW_mkv0, dgk0 = _kv_prep_bwd(mem0, gm0, W_mkv[0], gk0, dk_a, dv_a, "kv_prep0_bwd")
    dW_out0 = _matmul_tn(o_a, g_a, "dw_out0")
    t1 = list(_pair_add(grads1, pair1, "rs_add_layer1"))

    def glu_bwd(y, z, dy2, w):
        h, vjp_h = jax.vjp(_gelu, y)
        _, vjp_z = jax.vjp(lambda a, b: a * _sigmoid(b), z[:, :PRIM], z[:, PRIM:])
        dz16 = jnp.concatenate(vjp_z(dy2), axis=-1).astype(BF16)
        return vjp_h(_mm_slots_nt(dz16, w))[0], h.astype(BF16), dz16

    grads0 = [dW_out0.reshape(N_DEV, 256, D_MODEL), dW_mkv0.reshape(N_DEV, 128, 2 * XQ)]
    (dy_s5, h16, dz16), glu_hosted = _rowwise(
        "s5_glu_bwd", glu_bwd, [('r', y_s5), ('r', z_glu), ('r', dy2), ('c', W_glu)],
        [('r', (L, PRIM), F32), ('t', (PRIM, L), BF16), ('r', (L, 2 * PRIM), BF16)], nb_big,
        host=_combine(_plan_chips(t1[2:3]), _plan_pair(grads0)))
    recv_in_mla, pair0 = glu_hosted[:1], glu_hosted[1:]
    dW_glu = _matmul_tn_slots(h16, dz16, "dw_glu")
    t0 = list(_pair_add(grads0 + [dW_glu], pair0 + list(_exchange_call(_plan_pair([dW_glu]), "rs_pair_glu")),
                        "rs_add_layer0"))
    both = [jnp.concatenate([t0[i], t1[i]], axis=1) for i in range(2)]
    (du_s5, dbc, dcc, dd, dar, dai), recv_rest = _s5_bwd(u_s5, dy_s5, s5_carry, bm, cm, a_r2, a_i2, s5_d,
                                                        cmask, rmat, host=_plan_chips(both + t1[3:] + t0[2:]))
    early_recv = recv_rest[:2] + recv_in_mla + recv_rest[2:]
    dbc4 = dbc.reshape(S5_G, S5_C, 2, S5_P)
    dcc4 = dcc.reshape(S5_G, S5_C, 2, S5_P)
    dlr, dli, dls, dbtr, dbti = _s5_params_bwd(
        lr3, li3, ls3, btr, bti, dar.reshape(S5_G, 1, S5_P), dai.reshape(S5_G, 1, S5_P), dbc4[:, :, 0], dbc4[:, :, 1])

    small_part = {
        "ln_gain": jnp.concatenate([jnp.zeros_like(dln1), dln1]), "mem_norm": jnp.concatenate([dgm0, dgm1]),
        "xq_norm": jnp.concatenate([dgq0, dgq1]), "xk_norm": jnp.concatenate([dgk0, dgk1]),
        "s5_lambda_re": dlr, "s5_lambda_im": dli, "s5_log_step": dls,
        "s5_b_re": jnp.swapaxes(dbtr, 1, 2), "s5_b_im": jnp.swapaxes(dbti, 1, 2),
        "s5_c_re": dcc4[:, :, 0], "s5_c_im": -dcc4[:, :, 1], "s5_d": dd,
        "mla_q_lora_norm": dgql, "mla_kv_lora_norm": dgkvl, "mla_q_nope_norm": dgqn, "mla_k_nope_norm": dgkn,
        "mla_q_rope_norm": dgqr[:, :ROPE], "mla_k_rope_norm": dgkr[:, :ROPE],
    }
    loss8 = jnp.pad(loss_part, ((0, 7), (0, 0)))
    (dx0, xn0, dproj0, dln0), (small_gath, loss_g) = _rowwise(
        "s5_in_bwd", functools.partial(in_bwd, 1),
        [('r', x0), ('r', dx1), ('c', ln0), ('c', W_in_s5), ('r', du_s5), ('r', dxq_a),
         ('r', dgate_a)],
        [('r', (L, D_MODEL), F32), ('t', (D_MODEL, L), BF16), ('r', (L, 2 * BRANCH), BF16), ('a', (1, D_MODEL), F32)],
        nblk, host=_plan_all_gather([_pack_small(small_part).astype(BF16), loss8]))
    dW_in_s5, (ln0_gath,) = _matmul_tn_slots(
        xn0, dproj0, "dw_s5_in", host=_plan_all_gather([jnp.pad(dln0, ((0, 7), (0, 0)))]))

    late = [dW_in_s5]
    late_t = _pair_add(late, list(_exchange_call(_plan_pair(late), "rs_pair_late")), "rs_add_late")
    late_recv = list(_exchange_call(_plan_chips(late_t), "rs_chips_late"))
    owners = ["w_out", "w_mem_kv", "mla_w_in", "mla_w_uq", "mla_w_ukv", "s5_w_glu", "s5_w_in"]
    flipped = ("mla_w_in", "mla_w_uq")

    def shard(d, n):
        a = d[n]
        return jnp.transpose(a[0]) if n in flipped else a.reshape(-1, a.shape[-1])

    upd, _ = _updates_call(
        early_recv + late_recv, [shard(weights, n) for n in owners], [shard(m_in, n) for n in owners],
        [shard(v_in, n) for n in owners], "update_big")
    grads, delta, new_m, new_v = {}, {}, {}, {}
    for n, res in zip(owners, upd):
        shape = weights[n].shape
        grads[n], delta[n], new_m[n], new_v[n] = (
            (jnp.transpose(r)[None] if n in flipped else r.reshape(shape)) for r in res)

    gs, loss_sum = _small_sum(small_gath, loss_g, ln0_gath, "small_sum")
    loss = loss_sum[0, 0]
    for n, _ in _SMALL:
        shape = weights[n].shape
        if n == "mla_q_lora_norm":
            grads[n] = lax.dynamic_slice(_unpack_small(gs, n, (Q_LORA,)), (me * 64,), (64,)).reshape(shape)
        elif n == "mla_kv_lora_norm":
            grads[n] = lax.dynamic_slice(_unpack_small(gs, n, (KV_LORA,)), (me * 32,), (32,)).reshape(shape)
        else:
            grads[n] = _unpack_small(gs, n, shape)

    def own(n, a):
        if a.ndim == 4:
            a = jnp.transpose(a, (0, 2, 3, 1))
        elif a.ndim == 3:
            a = jnp.transpose(a, (0, 2, 1))
        return a.reshape(a.shape[1:]) if a.ndim >= 3 else a

    def back(n, a):
        shape = weights[n].shape
        if len(shape) == 4:
            return jnp.transpose(a.reshape((1,) + a.shape), (0, 3, 1, 2))
        if len(shape) == 3:
            return jnp.transpose(a.reshape((1,) + a.shape), (0, 2, 1))
        return a.reshape(shape)

    wide = ("s5_b_re", "s5_b_im", "s5_c_re", "s5_c_im")
    for names, nb, call in (([n for n, _ in _SMALL if n not in wide], 1, "update_small"), (wide, 4, "update_s5_bc")):
        res = _adamw_multi([own(n, weights[n]) for n in names], [own(n, grads[n]) for n in names],
                           [own(n, m_in[n]) for n in names], [own(n, v_in[n]) for n in names], call, nb)
        for n, (dl, m2, v2) in zip(names, res):
            delta[n], new_m[n], new_v[n] = back(n, dl), back(n, m2), back(n, v2)
    return (loss, dx0[None], *[grads[n] for n in _WEIGHTS], *[delta[n] for n in _WEIGHTS],
            *[new_m[n] for n in _WEIGHTS], *[new_v[n] for n in _WEIGHTS])
```

```python
import functools
import math

import numpy as np
import jax
import jax.numpy as jnp
from jax import lax
from jax.experimental import pallas as pl
from jax.experimental.pallas import tpu as pltpu

F32 = jnp.float32
BF16 = jnp.bfloat16
EPS = 1e-6
NEG = float(np.finfo(np.float32).min)
MESH = pl.DeviceIdType.MESH

N_DEV = 8
D_MODEL = 1024
MEM_LEN = 256
XQ = 512
PRIM = 1536
BRANCH = 2048
X_HEADS = 4
HD = 128
S5_G = 96
S5_P = 64
S5_C = 16
S5_GB = 8
S5_W = S5_GB * S5_P
MLA_H = 12
ROPE = 64
Q_LORA = 512
KV_LORA = 256
ROPE_THETA = 10000.0

ADAM_LR = 0.001
ADAM_B1 = 0.9
ADAM_B2 = 0.999
ADAM_EPS = 1e-08
ADAM_WD = 0.01
ADAM_STEP = 10

VMEM_LIMIT = 56 * 1024 * 1024


def _dot(a, b):
    return jnp.dot(a, b, preferred_element_type=F32)


def _dot_nt(a, b):
    return lax.dot_general(a, b, (((1,), (1,)), ((), ())), preferred_element_type=F32)


def _dot_tn(a, b):
    return lax.dot_general(a, b, (((0,), (0,)), ((), ())), preferred_element_type=F32)


@jax.custom_vjp
def _mm(a, b):
    return _dot(a.astype(BF16), b.astype(BF16))


def _mm_fwd(a, b):
    return _mm(a, b), (a, b)


def _mm_bwd(res, g):
    a, b = res
    gb = g.astype(BF16)
    return _dot_nt(gb, b.astype(BF16)).astype(a.dtype), _dot_tn(a.astype(BF16), gb).astype(b.dtype)


_mm.defvjp(_mm_fwd, _mm_bwd)


@jax.custom_vjp
def _mm_nt(a, b):
    return _dot_nt(a.astype(BF16), b.astype(BF16))


def _mm_nt_fwd(a, b):
    return _mm_nt(a, b), (a, b)


def _mm_nt_bwd(res, g):
    a, b = res
    gb = g.astype(BF16)
    return _dot(gb, b.astype(BF16)).astype(a.dtype), _dot_tn(gb, a.astype(BF16)).astype(b.dtype)


_mm_nt.defvjp(_mm_nt_fwd, _mm_nt_bwd)


@jax.custom_vjp
def _softmax(s):
    m = jnp.max(s, axis=-1, keepdims=True)
    e = jnp.exp(s - m)
    return e / jnp.sum(e, axis=-1, keepdims=True)


def _softmax_fwd(s):
    p = _softmax(s)
    return p, p


def _softmax_bwd(p, g):
    return (p * (g - jnp.sum(p * g, axis=-1, keepdims=True)),)


_softmax.defvjp(_softmax_fwd, _softmax_bwd)


def _rms(x, g, n):
    ms = jnp.sum(x * x, axis=-1, keepdims=True) * (1.0 / n)
    return x * lax.rsqrt(ms + EPS) * g


def _sigmoid(x):
    return 1.0 / (1.0 + jnp.exp(-x))


def _silu(x):
    return x * _sigmoid(x)


def _gelu(x):
    c = math.sqrt(2.0 / math.pi)
    return 0.5 * x * (1.0 + jnp.tanh(c * (x + 0.044715 * (x * x * x))))


@jax.custom_vjp
def _rot(x, c, s1, s2):
    return x * c + pltpu.roll(x, 96, 1) * s1 + pltpu.roll(x, 32, 1) * s2


def _rot_fwd(x, c, s1, s2):
    return _rot(x, c, s1, s2), (c, s1, s2)


def _rot_bwd(res, g):
    c, s1, s2 = res
    dx = g * c + pltpu.roll(g * s1, 32, 1) + pltpu.roll(g * s2, 96, 1)
    return dx, jnp.zeros_like(c), jnp.zeros_like(s1), jnp.zeros_like(s2)


_rot.defvjp(_rot_fwd, _rot_bwd)


def _mem_attn(xq, k, v, gq):
    outs = []
    for h in range(X_HEADS):
        sl = slice(HD * h, HD * (h + 1))
        q = _rms(xq[:, sl], gq, HD)
        p = _softmax(_mm_nt(q, k[:, sl]) * (HD ** -0.5))
        outs.append(_mm(p, v[:, sl]))
    return jnp.concatenate(outs, axis=-1)


def _merge(mix, xq, gate, k, v, gq):
    return jnp.concatenate([mix, _mem_attn(xq, k, v, gq)], axis=-1) * _silu(gate)


def _q_chunks(q):
    return ([q[:, HD * h:HD * (h + 1)] for h in range(MLA_H)],
            [q[:, PRIM + HD * h:PRIM + HD * (h + 1)] for h in range(MLA_H)])


def _q_post(nope, rope, gqn, gqr, c, s1, s2):
    pieces = []
    for qn, qr in zip(nope, rope):
        pieces.append(_rms(qn, gqn, HD))
        pieces.append(_rot(_rms(qr, gqr, ROPE), c, s1, s2))
    return jnp.concatenate(pieces, axis=-1)


def _kv_chunks(kv):
    return ([kv[:, 2 * HD * h:2 * HD * h + HD] for h in range(MLA_H)],
            [kv[:, 2 * HD * h + HD:2 * HD * (h + 1)] for h in range(MLA_H)])


def _kv_post(kn, vals, krp, gkn, gkr, c, s1, s2):
    kr = _rot(_rms(krp, gkr, ROPE), c, s1, s2)
    pieces = []
    for k in kn:
        pieces.append(_rms(k, gkn, HD))
        pieces.append(kr)
    return jnp.concatenate(pieces, axis=-1), jnp.concatenate(vals, axis=-1)


def _rowwise(name, fn, ins, outs, nblk, host=None):
    n_in = len(ins)

    def spec(kind, shape):
        if kind == 'r':
            return pl.BlockSpec((shape[0] // nblk, shape[1]), lambda i: (i, 0))
        if kind == 't':
            return pl.BlockSpec((shape[0], shape[1] // nblk), lambda i: (0, i))
        zeros = (0,) * len(shape)
        return pl.BlockSpec(tuple(shape), lambda i: zeros)

    def body(*refs):
        i = pl.program_id(0)
        res = fn(*[r[...] for r in refs[:n_in]])
        for (kind, _, _), ref, val in zip(outs, refs[n_in:], res):
            if kind == 'a':
                @pl.when(i == 0)
                def _():
                    ref[...] = jnp.zeros_like(ref)
                ref[...] += val.astype(ref.dtype)
            elif kind == 't':
                ref[...] = val.astype(F32).T.astype(ref.dtype)
            else:
                ref[...] = val.astype(ref.dtype)

    res, hosted = _hosting_call(
        body, name, nblk, host, [a for _, a in ins], [spec(k, a.shape) for k, a in ins],
        [jax.ShapeDtypeStruct(tuple(s), d) for _, s, d in outs], [spec(k, s) for k, s, _ in outs], [])
    return res if host is None else (res, hosted)


def _matmul_tn(at, g, name, out_dtype=BF16, row_bands=None):
    K, L = at.shape
    N = g.shape[1]
    tn = next(t for t in (512, 384, 256, 128) if N % t == 0)
    bands = [(0, K)] if row_bands is None else row_bands
    rows_out = sum(n for _, n in bands)

    def body(a_ref, g_ref, o_ref):
        res = _dot(a_ref[...], g_ref[...]).astype(o_ref.dtype)
        row = 0
        for start, n in bands:
            o_ref[row:row + n, :] = res[start:start + n]
            row += n

    return pl.pallas_call(
        body, name=name, grid=(N // tn,),
        in_specs=[pl.BlockSpec((K, L), lambda n: (0, 0)), pl.BlockSpec((L, tn), lambda n: (0, n))],
        out_specs=pl.BlockSpec((rows_out, tn), lambda n: (0, n)),
        out_shape=jax.ShapeDtypeStruct((rows_out, N), out_dtype),
        compiler_params=pltpu.CompilerParams(dimension_semantics=("arbitrary",), vmem_limit_bytes=VMEM_LIMIT),
    )(at, g)


def _matmul_tn_slots(at, g, name, host=None):
    K, L = at.shape
    n = g.shape[1] // N_DEV

    def body(a_ref, g_ref, o_ref):
        o_ref[...] = _dot(a_ref[...], g_ref[...]).astype(o_ref.dtype)

    res, hosted = _hosting_call(
        body, name, N_DEV, host, [at, g],
        [pl.BlockSpec((K, L), lambda d: (0, 0)), pl.BlockSpec((L, n), lambda d: (0, d))],
        [jax.ShapeDtypeStruct((N_DEV, K, n), BF16)], [pl.BlockSpec((None, K, n), lambda d: (d, 0, 0))], [])
    return res[0] if host is None else (res[0], hosted)


def _mm_slots(a16, w):
    return jnp.concatenate([_dot(a16, w[d]) for d in range(N_DEV)], axis=-1)


def _mm_slots_nt(g16, w):
    n = w.shape[2]
    out = _dot_nt(g16[:, 0:n], w[0])
    for d in range(1, N_DEV):
        out = out + _dot_nt(g16[:, d * n:(d + 1) * n], w[d])
    return out


class _Exchange:
    def __init__(self, ins, outs, scratch, start, finish):
        self.ins, self.outs, self.scratch, self.start, self.finish = ins, outs, scratch, start, finish


def _xyc():
    return lax.axis_index("x"), lax.axis_index("y"), lax.axis_index("c")


def _plan_all_gather(xs):
    n = len(xs)

    def build(x_refs, out_refs, sems):
        send_sems, recv_sems, local_sems = sems
        x, y, c = _xyc()

        def copies(k, block, to, own=False):
            slot = 4 * block[0] + 2 * block[1] + block[2]
            return [pltpu.make_async_remote_copy(
                src_ref=x_refs[a] if own else out_refs[a].at[slot], dst_ref=out_refs[a].at[slot],
                send_sem=send_sems.at[k * n + a], recv_sem=recv_sems.at[k * n + a], device_id=to,
                device_id_type=MESH) for a in range(n)]

        mine = [pltpu.make_async_copy(x_refs[a], out_refs[a].at[4 * x + 2 * y + c], local_sems.at[a])
                for a in range(n)]
        return copies, mine, (x, y, c), [(1 - x, y), (x, 1 - y), (1 - x, 1 - y)]

    def first_copies(copies, me, chips):
        x, y, c = me
        first = copies(0, me, (x, y, 1 - c), own=True)
        for j, chip in enumerate(chips):
            first += copies(1 + j, me, (*chip, c), own=True)
        return first

    def start(x_refs, out_refs, sems):
        copies, mine, me, chips = build(x_refs, out_refs, sems)
        for cp in mine + first_copies(copies, me, chips):
            cp.start()

    def finish(x_refs, out_refs, sems):
        copies, mine, me, chips = build(x_refs, out_refs, sems)
        x, y, c = me
        passed = []
        for j, chip in enumerate(chips):
            for cp in copies(1 + j, (*chip, c), me):
                cp.wait_recv()
            fwd = copies(4 + j, (*chip, c), (x, y, 1 - c))
            for cp in fwd:
                cp.start()
            passed += fwd
        for cp in copies(0, (x, y, 1 - c), me):
            cp.wait_recv()
        for j, chip in enumerate(chips):
            for cp in copies(4 + j, (*chip, 1 - c), me):
                cp.wait_recv()
        for cp in first_copies(copies, me, chips) + passed:
            cp.wait_send()
        for cp in mine:
            cp.wait()

    return _Exchange(list(xs), [jax.ShapeDtypeStruct((N_DEV,) + a.shape, a.dtype) for a in xs],
                     [pltpu.SemaphoreType.DMA((7 * n,)), pltpu.SemaphoreType.DMA((7 * n,)),
                      pltpu.SemaphoreType.DMA((n,))], start, finish)


_CHIPS = ((0, 0), (0, 1), (1, 0), (1, 1))


def _plan_pair(sends):
    n = len(sends)

    def build(s_refs, o_refs, sems):
        send_sems, recv_sems = sems
        x, y, c = _xyc()
        return [pltpu.make_async_remote_copy(
            src_ref=s_refs[a].at[4 * px + 2 * py + 1 - c], dst_ref=o_refs[a].at[j],
            send_sem=send_sems.at[j * n + a], recv_sem=recv_sems.at[j * n + a], device_id=(x, y, 1 - c),
            device_id_type=MESH) for j, (px, py) in enumerate(_CHIPS) for a in range(n)]

    def start(s_refs, o_refs, sems):
        for cp in build(s_refs, o_refs, sems):
            cp.start()

    def finish(s_refs, o_refs, sems):
        for cp in build(s_refs, o_refs, sems):
            cp.wait_recv()
            cp.wait_send()

    return _Exchange(list(sends), [jax.ShapeDtypeStruct((4,) + a.shape[1:], a.dtype) for a in sends],
                     [pltpu.SemaphoreType.DMA((4 * n,)), pltpu.SemaphoreType.DMA((4 * n,))], start, finish)


def _plan_chips(ts):
    n = len(ts)
    flips = ((1, 0), (0, 1), (1, 1))

    def build(t_refs, o_refs, sems):
        send_sems, recv_sems, local_sems = sems
        x, y, c = _xyc()
        mine = 2 * x + y
        local = [pltpu.make_async_copy(t_refs[a].at[mine], o_refs[a].at[mine], local_sems.at[a]) for a in range(n)]
        remote = []
        for k, (fx, fy) in enumerate(flips):
            px = 1 - x if fx else x
            py = 1 - y if fy else y
            remote += [pltpu.make_async_remote_copy(
                src_ref=t_refs[a].at[2 * px + py], dst_ref=o_refs[a].at[mine],
                send_sem=send_sems.at[k * n + a], recv_sem=recv_sems.at[k * n + a], device_id=(px, py, c),
                device_id_type=MESH) for a in range(n)]
        return local, remote

    def start(t_refs, o_refs, sems):
        local, remote = build(t_refs, o_refs, sems)
        for cp in local + remote:
            cp.start()

    def finish(t_refs, o_refs, sems):
        local, remote = build(t_refs, o_refs, sems)
        for cp in remote:
            cp.wait_recv()
        for cp in remote:
            cp.wait_send()
        for cp in local:
            cp.wait()

    return _Exchange(list(ts), [jax.ShapeDtypeStruct(a.shape, a.dtype) for a in ts],
                     [pltpu.SemaphoreType.DMA((3 * n,)), pltpu.SemaphoreType.DMA((3 * n,)),
                      pltpu.SemaphoreType.DMA((n,))], start, finish)


def _combine(*plans):
    def parts(refs, attr):
        out, at = [], 0
        for p in plans:
            n = len(getattr(p, attr))
            out.append(refs[at:at + n])
            at += n
        return out

    def run(half):
        def go(ins, outs, sems):
            for p, a, o, s in zip(plans, parts(ins, "ins"), parts(outs, "outs"), parts(sems, "scratch")):
                getattr(p, half)(a, o, s)
        return go

    return _Exchange(sum((p.ins for p in plans), []), sum((p.outs for p in plans), []),
                     sum((p.scratch for p in plans), []), run("start"), run("finish"))


def _exchange_call(plan, name):
    n = len(plan.ins)

    def body(*refs):
        ins, outs, sems = refs[:n], refs[n:2 * n], refs[2 * n:]
        plan.start(ins, outs, sems)
        plan.finish(ins, outs, sems)

    return pl.pallas_call(
        body, name=name, out_shape=plan.outs,
        in_specs=[pl.BlockSpec(memory_space=pl.ANY)] * n, out_specs=[pl.BlockSpec(memory_space=pl.ANY)] * n,
        scratch_shapes=plan.scratch,
    )(*plan.ins)


def _slab_spec(lead, rows, cols, nb):
    if rows % (nb * 16) == 0:
        return pl.BlockSpec((lead, rows // nb, cols), lambda i: (0, i, 0))
    if cols % (nb * 128) == 0:
        return pl.BlockSpec((lead, rows, cols // nb), lambda i: (0, 0, i))
    return pl.BlockSpec((lead, rows, cols), lambda i: (0, 0, 0))


def _slab_spec2(rows, cols, nb):
    if rows % (nb * 16) == 0:
        return pl.BlockSpec((rows // nb, cols), lambda i: (i, 0))
    if cols % (nb * 128) == 0:
        return pl.BlockSpec((rows, cols // nb), lambda i: (0, i))
    return pl.BlockSpec((rows, cols), lambda i: (0, 0))


def _cast_call(arrays, name, host=None):
    n = len(arrays)
    nb = 8

    def body(*refs):
        for a in range(n):
            refs[n + a][...] = refs[a][...].astype(BF16)

    specs = [_slab_spec2(x.shape[0], x.shape[1], nb) for x in arrays]
    return _hosting_call(body, name, nb, host, list(arrays), specs,
                         [jax.ShapeDtypeStruct(x.shape, BF16) for x in arrays], specs, [])


def _pair_add(sends, fromsib, name):
    n = len(sends)
    nb = 8

    def body(*refs):
        c = lax.axis_index("c")
        for a in range(n):
            s_ref, f_ref, t_ref = refs[a], refs[n + a], refs[2 * n + a]
            for j in range(4):
                t_ref[j] = (s_ref[2 * j + c].astype(F32) + f_ref[j].astype(F32)).astype(t_ref.dtype)

    def spec(a, lead):
        return _slab_spec(lead, a.shape[1], a.shape[2], nb)

    return pl.pallas_call(
        body, name=name, grid=(nb,),
        in_specs=[spec(a, N_DEV) for a in sends] + [spec(a, 4) for a in fromsib],
        out_specs=[spec(a, 4) for a in fromsib],
        out_shape=[jax.ShapeDtypeStruct(a.shape, a.dtype) for a in fromsib],
        compiler_params=pltpu.CompilerParams(dimension_semantics=("arbitrary",), vmem_limit_bytes=VMEM_LIMIT),
    )(*sends, *fromsib)


def _adamw_vals(w, g, m, v):
    m2 = ADAM_B1 * m + (1.0 - ADAM_B1) * g
    v2 = ADAM_B2 * v + (1.0 - ADAM_B2) * (g * g)
    m_hat = m2 / (1.0 - ADAM_B1 ** ADAM_STEP)
    v_hat = v2 / (1.0 - ADAM_B2 ** ADAM_STEP)
    delta = -ADAM_LR * (m_hat / (jnp.sqrt(v_hat) + ADAM_EPS) + ADAM_WD * w)
    return delta, m2, v2


def _updates_call(recvs, ws, ms, vs, name, host=None):
    n = len(recvs)
    nb = 8

    def body(*refs):
        for a in range(n):
            r_ref, w_ref, m_ref, v_ref = refs[a], refs[n + a], refs[2 * n + a], refs[3 * n + a]
            g_ref, d_ref, m2_ref, v2_ref = refs[4 * n + 4 * a:4 * n + 4 * a + 4]
            g = r_ref[0].astype(F32)
            for d in range(1, r_ref.shape[0]):
                g = g + r_ref[d].astype(F32)
            dl, m2, v2 = _adamw_vals(w_ref[...], g, m_ref[...], v_ref[...])
            g_ref[...] = g
            d_ref[...] = dl
            m2_ref[...] = m2
            v2_ref[...] = v2

    def spec3(r):
        return _slab_spec(r.shape[0], r.shape[1], r.shape[2], nb)

    def spec2(w):
        return _slab_spec2(w.shape[0], w.shape[1], nb)

    res, hosted = _hosting_call(
        body, name, nb, host, list(recvs) + list(ws) + list(ms) + list(vs),
        [spec3(r) for r in recvs] + [spec2(w) for w in ws] * 3,
        [jax.ShapeDtypeStruct(w.shape, F32) for w in ws for _ in range(4)],
        [spec2(w) for w in ws for _ in range(4)], [])
    return [res[4 * a:4 * a + 4] for a in range(n)], hosted


def _small_sum(gath, loss_g, row0_g, name):
    _, R, C = gath.shape
    br = R // 3

    def body(g_ref, l_ref, r_ref, go_ref, lo_ref):
        g = g_ref[0].astype(F32)
        lsum = l_ref[0]
        for d in range(1, N_DEV):
            g = g + g_ref[d].astype(F32)
            lsum = lsum + l_ref[d]
        go_ref[...] = g
        lo_ref[...] = lsum

        @pl.when(pl.program_id(0) == 0)
        def _():
            row0 = r_ref[0]
            for d in range(1, N_DEV):
                row0 = row0 + r_ref[d]
            go_ref[0:8, :] = go_ref[0:8, :] + jnp.where(lax.broadcasted_iota(jnp.int32, row0.shape, 0) == 0, row0, 0.0)

    return pl.pallas_call(
        body, name=name, grid=(R // br,),
        in_specs=[pl.BlockSpec((N_DEV, br, C), lambda i: (0, i, 0)),
                  pl.BlockSpec((N_DEV, 8, HD), lambda i: (0, 0, 0)), pl.BlockSpec((N_DEV, 8, C), lambda i: (0, 0, 0))],
        out_specs=[pl.BlockSpec((br, C), lambda i: (i, 0)), pl.BlockSpec((8, HD), lambda i: (0, 0))],
        out_shape=[jax.ShapeDtypeStruct((R, C), F32), jax.ShapeDtypeStruct((8, HD), F32)],
        compiler_params=pltpu.CompilerParams(dimension_semantics=("arbitrary",)),
    )(gath, loss_g, row0_g)


def _adamw_multi(ws, gs, ms, vs, name, nblk=1):
    n = len(ws)

    def body(*refs):
        for a in range(n):
            dl, m2, v2 = _adamw_vals(refs[a][...], refs[n + a][...], refs[2 * n + a][...], refs[3 * n + a][...])
            refs[4 * n + 3 * a][...] = dl
            refs[4 * n + 3 * a + 1][...] = m2
            refs[4 * n + 3 * a + 2][...] = v2

    def spec(x):
        rest = (0,) * (x.ndim - 1)
        return pl.BlockSpec((x.shape[0] // nblk,) + tuple(x.shape[1:]), lambda i: (i,) + rest)

    res = pl.pallas_call(
        body, name=name, grid=(nblk,),
        in_specs=[spec(w) for w in ws] * 4, out_specs=[spec(w) for w in ws for _ in range(3)],
        out_shape=[jax.ShapeDtypeStruct(w.shape, F32) for w in ws for _ in range(3)],
        compiler_params=pltpu.CompilerParams(dimension_semantics=("arbitrary",), vmem_limit_bytes=VMEM_LIMIT),
    )(*ws, *gs, *ms, *vs)
    return [res[3 * a:3 * a + 3] for a in range(n)]


def _s5_param_fn(lr, li, ls, btr, bti):
    step = jnp.exp(ls)
    er = jnp.exp(lr * step)
    ang = li * step
    ar = er * jnp.cos(ang)
    ai = er * jnp.sin(ang)
    nr = ar - 1.0
    den = lr * lr + li * li
    fr = (nr * lr + ai * li) / den
    fi = (ai * lr - nr * li) / den
    return ar, ai, fr * btr - fi * bti, fr * bti + fi * btr


def _s5_params(lr, li, ls, btr, bti, cre, cim):
    nb = S5_G // S5_GB
    GC = S5_GB * S5_C
    expand = jnp.asarray(np.tile(np.eye(S5_P, dtype=np.float32), (1, S5_GB)), BF16)
    own = jnp.asarray((np.arange(GC)[:, None] // S5_C == np.arange(S5_W)[None, :] // S5_P).astype(np.float32))

    def body(lr_ref, li_ref, ls_ref, br_ref, bi_ref, cr_ref, ci_ref, e_ref, own_ref, ar_ref, ai_ref, bm_ref, cm_ref):
        ar, ai, bbr, bbi = _s5_param_fn(lr_ref[...], li_ref[...], ls_ref[...], br_ref[...], bi_ref[...])
        ar_ref[...] = ar
        ai_ref[...] = ai

        def plane(x, n):
            rows = x[n * S5_GB:(n + 1) * S5_GB].reshape(GC, S5_P).astype(BF16)
            return _dot(rows, e_ref[...]) * own_ref[...]

        for n in range(nb):
            bm_ref[n] = jnp.concatenate([plane(bbr, n), plane(bbi, n)], axis=-1).astype(BF16)
            cm_ref[n] = jnp.concatenate([plane(cr_ref[...], n), -plane(ci_ref[...], n)], axis=-1).astype(BF16)

    sd = jax.ShapeDtypeStruct
    return pl.pallas_call(
        body, name="s5_params",
        out_shape=[sd(lr.shape, F32), sd(lr.shape, F32), sd((nb, GC, 2 * S5_W), BF16), sd((nb, GC, 2 * S5_W), BF16)],
        compiler_params=pltpu.CompilerParams(vmem_limit_bytes=VMEM_LIMIT),
    )(lr, li, ls, btr, bti, cre, cim, expand, own)


def _s5_params_bwd(lr, li, ls, btr, bti, dar, dai, dbbr, dbbi):
    def body(lr_ref, li_ref, ls_ref, br_ref, bi_ref, dar_ref, dai_ref, dbbr_ref, dbbi_ref,
             dlr_ref, dli_ref, dls_ref, dbr_ref, dbi_ref):
        _, vjp = jax.vjp(_s5_param_fn, lr_ref[...], li_ref[...], ls_ref[...], br_ref[...], bi_ref[...])
        dlr, dli, dls, dbr, dbi = vjp((dar_ref[...], dai_ref[...], dbbr_ref[...], dbbi_ref[...]))
        dlr_ref[...] = dlr
        dli_ref[...] = dli
        dls_ref[...] = dls
        dbr_ref[...] = dbr
        dbi_ref[...] = dbi

    sd = jax.ShapeDtypeStruct
    return pl.pallas_call(
        body, name="s5_params_bwd",
        out_shape=[sd(lr.shape, F32), sd(lr.shape, F32), sd(ls.shape, F32), sd(btr.shape, F32), sd(btr.shape, F32)],
    )(lr, li, ls, btr, bti, dar, dai, dbbr, dbbi)


def _cpow(ar, ai, n):
    assert n & (n - 1) == 0
    while n > 1:
        ar, ai = ar * ar - ai * ai, 2.0 * ar * ai
        n //= 2
    return ar, ai


def _scan(st, cr, ci, init, nk, reverse, store, prev=None):
    W = S5_W

    def advance(k, sr, si):
        rows = pl.ds(k * 8 if isinstance(k, int) else pl.multiple_of(k * 8, 8), 8)
        nsr = cr * sr - ci * si + st[rows, 0:W]
        nsi = cr * si + ci * sr + st[rows, W:2 * W]
        if store:
            st[rows, 0:W] = nsr
            st[rows, W:2 * W] = nsi
        return nsr, nsi

    if prev is None:
        return lax.fori_loop(0, nk, lambda j, c: advance(nk - 1 - j if reverse else j, c[0], c[1]), init, unroll=2)
    assert reverse

    def step(j, carry):
        k = nk - 1 - j
        nsr, nsi = advance(k, carry[0], carry[1])
        prows = pl.ds(pl.multiple_of((k - 1) * 8, 8), 8)
        pr = prev[prows, 0:W]
        pi = prev[prows, W:2 * W]
        return nsr, nsi, carry[2] + nsr * pr + nsi * pi, carry[3] + nsi * pr - nsr * pi

    carry = lax.fori_loop(0, nk - 1, step, init, unroll=2)
    nsr, nsi = advance(0, carry[0], carry[1])
    return nsr, nsi, carry[2], carry[3]


def _chain(fin, fr, fi, pr, pi, reverse):
    W = S5_W
    fin[:, 0:W] = fr
    fin[:, W:2 * W] = fi
    rowid = lax.broadcasted_iota(jnp.int32, (8, W), 0)
    cr = jnp.zeros((1, W), F32)
    ci = jnp.zeros((1, W), F32)
    init_r = jnp.zeros((8, W), F32)
    init_i = jnp.zeros((8, W), F32)
    for s in (range(7, -1, -1) if reverse else range(8)):
        init_r = jnp.where(rowid == s, cr, init_r)
        init_i = jnp.where(rowid == s, ci, init_i)
        lr = fin[s:s + 1, 0:W]
        li = fin[s:s + 1, W:2 * W]
        cr, ci = lr + pr * cr - pi * ci, li + pr * ci + pi * cr
    return init_r, init_i


def _full_scan(st, fin, ar, ai, nk, reverse, prev=None, carry_in=None, carry_out=None):
    W = S5_W
    cr = jnp.broadcast_to(ar, (8, W))
    ci = jnp.broadcast_to(-ai if reverse else ai, (8, W))
    z = jnp.zeros((8, W), F32)
    if carry_in is None:
        fr, fi = _scan(st, cr, ci, (z, z), nk, reverse, store=False)
        pr, pi = _cpow(ar, -ai if reverse else ai, nk)
        init = _chain(fin, fr, fi, pr, pi, reverse)
    else:
        init = (carry_in[:, 0:W], carry_in[:, W:2 * W])
    if carry_out is not None:
        carry_out[:, 0:W] = init[0]
        carry_out[:, W:2 * W] = init[1]
    if prev is None:
        return _scan(st, cr, ci, init, nk, reverse, store=True)
    return _scan(st, cr, ci, init + (z, z), nk, reverse, store=True, prev=prev)


def _s5_specs(L):
    W2 = 2 * S5_W
    GC = S5_GB * S5_C
    col = pl.BlockSpec((L, GC), lambda g: (0, g))
    vec = pl.BlockSpec((1, GC), lambda g: (0, g))
    avec = pl.BlockSpec((1, S5_W), lambda g: (0, g))
    bmat = pl.BlockSpec((None, GC, W2), lambda g: (g, 0, 0))
    cmat = pl.BlockSpec((None, W2, GC), lambda g: (g, 0, 0))
    return col, vec, avec, bmat, cmat


def _interleave(dst, src, nk):
    for s in range(8):
        dst[pl.ds(s, nk, stride=8), :] = src[s * nk:(s + 1) * nk, :]


def _deinterleave(dst, src, nk):
    for s in range(8):
        dst[s * nk:(s + 1) * nk, :] = src[pl.ds(s, nk, stride=8), :].astype(dst.dtype)


def _hosting_call(body, name, nsteps, host, ins, in_specs, outs, out_specs, scratch):
    grid = (nsteps,) if isinstance(nsteps, int) else tuple(nsteps)
    params = pltpu.CompilerParams(dimension_semantics=("arbitrary",) * len(grid), vmem_limit_bytes=VMEM_LIMIT)
    if host is None:
        res = pl.pallas_call(
            body, name=name, grid=grid, in_specs=in_specs, out_specs=out_specs, out_shape=outs,
            scratch_shapes=scratch, compiler_params=params,
        )(*ins)
        return list(res), []
    n_in, n_out, n_sc = len(ins), len(outs), len(scratch)
    h_in, h_out = len(host.ins), len(host.outs)

    def hosted(*refs):
        a = refs[:n_in]
        ha = refs[n_in:n_in + h_in]
        o = refs[n_in + h_in:n_in + h_in + n_out]
        ho = refs[n_in + h_in + n_out:n_in + h_in + n_out + h_out]
        sc = refs[n_in + h_in + n_out + h_out:n_in + h_in + n_out + h_out + n_sc]
        hs = refs[n_in + h_in + n_out + h_out + n_sc:]
        first = functools.reduce(jnp.logical_and, [pl.program_id(i) == 0 for i in range(len(grid))])
        last = functools.reduce(jnp.logical_and, [pl.program_id(i) == g - 1 for i, g in enumerate(grid)])

        @pl.when(first)
        def _():
            host.start(ha, ho, hs)

        body(*a, *o, *sc)

        @pl.when(last)
        def _():
            host.finish(ha, ho, hs)

    hbm = pl.BlockSpec(memory_space=pl.ANY)
    res = pl.pallas_call(
        hosted, name=name, grid=grid,
        in_specs=list(in_specs) + [hbm] * h_in, out_specs=list(out_specs) + [hbm] * h_out,
        out_shape=list(outs) + list(host.outs), scratch_shapes=list(scratch) + list(host.scratch),
        compiler_params=params,
    )(*ins, *host.ins)
    return list(res[:n_out]), list(res[n_out:])


def _s5_fwd(u, bm, cm, ar, ai, dvec, host=None):
    L = u.shape[0]
    nk = L // 8
    GC = S5_GB * S5_C
    nb = S5_G // S5_GB
    col, vec, avec, bmat, cmat = _s5_specs(L)

    def body(u_ref, b_ref, c_ref, ar_ref, ai_ref, d_ref, y_ref, carry_ref, st, fin, ui, yi):
        _interleave(ui, u_ref, nk)
        for r in range(8):
            rows = slice(r * nk, (r + 1) * nk)
            st[rows, :] = _dot(ui[rows, :].astype(BF16), b_ref[...])
        _full_scan(st, fin, ar_ref[...], ai_ref[...], nk, reverse=False, carry_out=carry_ref)
        for r in range(8):
            rows = slice(r * nk, (r + 1) * nk)
            yi[rows, :] = _dot_nt(st[rows, :].astype(BF16), c_ref[...]) + d_ref[...] * ui[rows, :]
        _deinterleave(y_ref, yi, nk)

    return _hosting_call(
        body, "s5_fwd", nb, host,
        [u, bm, cm, ar, ai, dvec], [col, bmat, bmat, avec, avec, vec],
        [jax.ShapeDtypeStruct(u.shape, F32), jax.ShapeDtypeStruct((nb * 8, 2 * S5_W), F32)],
        [col, pl.BlockSpec((8, 2 * S5_W), lambda g: (g, 0))],
        [pltpu.VMEM((L, 2 * S5_W), F32), pltpu.VMEM((8, 2 * S5_W), F32), pltpu.VMEM((L, GC), F32),
         pltpu.VMEM((L, GC), F32)])


def _s5_bwd(u, dy, carry, bm, cm, ar, ai, dvec, mask, rmat, host=None):
    L = u.shape[0]
    nk = L // 8
    W = S5_W
    GC = S5_GB * S5_C
    col, vec, avec, bmat, cmat = _s5_specs(L)
    hi = lax.Precision.HIGHEST

    def body(u_ref, dy_ref, carry_ref, b_ref, ct_ref, ar_ref, ai_ref, d_ref, mask_ref, r_ref,
             du_ref, db_ref, dc_ref, dd_ref, dar_ref, dai_ref, sa, sb, fin, ui, dyi, dui):
        ar = ar_ref[...]
        ai = ai_ref[...]
        _interleave(ui, u_ref, nk)
        _interleave(dyi, dy_ref, nk)
        for r in range(8):
            rows = slice(r * nk, (r + 1) * nk)
            sa[rows, :] = _dot(ui[rows, :].astype(BF16), b_ref[...])
            sb[rows, :] = _dot(dyi[rows, :].astype(BF16), ct_ref[...])
        _full_scan(sa, fin, ar, ai, nk, reverse=False, carry_in=carry_ref)
        gr, gi, accr, acci = _full_scan(sb, fin, ar, ai, nk, reverse=True, prev=sa)
        rowid = lax.broadcasted_iota(jnp.int32, (8, W), 0)
        last = pl.ds((nk - 1) * 8, 8)
        pr = jnp.where(rowid == 0, 0.0, pltpu.roll(sa[last, 0:W], 1, 0))
        pi = jnp.where(rowid == 0, 0.0, pltpu.roll(sa[last, W:2 * W], 1, 0))
        accr = accr + gr * pr + gi * pi
        acci = acci + gi * pr - gr * pi
        dar_ref[...] = jnp.sum(accr, axis=0, keepdims=True)
        dai_ref[...] = jnp.sum(acci, axis=0, keepdims=True)
        dbf = jnp.zeros((GC, 2 * W), F32)
        dcf = jnp.zeros((GC, 2 * W), F32)
        dd = jnp.zeros((1, GC), F32)
        for r in range(8):
            rows = slice(r * nk, (r + 1) * nk)
            ub = ui[rows, :]
            dyb = dyi[rows, :]
            gb = sb[rows, :].astype(BF16)
            dui[rows, :] = _dot_nt(gb, b_ref[...]) + d_ref[...] * dyb
            dbf = dbf + _dot_tn(ub.astype(BF16), gb)
            dcf = dcf + _dot_tn(dyb.astype(BF16), sa[rows, :].astype(BF16))
            dd = dd + jnp.sum(dyb * ub, axis=0, keepdims=True)
        db_ref[...] = jnp.dot(dbf * mask_ref[...], r_ref[...], precision=hi, preferred_element_type=F32)
        dc_ref[...] = jnp.dot(dcf * mask_ref[...], r_ref[...], precision=hi, preferred_element_type=F32)
        dd_ref[...] = dd
        _deinterleave(du_ref, dui, nk)

    cmp_spec = pl.BlockSpec((GC, 2 * S5_P), lambda g: (g, 0))
    whole = lambda shape: pl.BlockSpec(shape, lambda g: (0, 0))
    sd = jax.ShapeDtypeStruct
    return _hosting_call(
        body, "s5_bwd", S5_G // S5_GB, host,
        [u, dy, carry, bm, cm, ar, ai, dvec, mask, rmat],
        [col, col, pl.BlockSpec((8, 2 * W), lambda g: (g, 0)), bmat, bmat, avec, avec, vec, whole(mask.shape),
         whole(rmat.shape)],
        [sd(u.shape, BF16), sd((S5_G * S5_C, 2 * S5_P), F32), sd((S5_G * S5_C, 2 * S5_P), F32),
         sd((1, PRIM), F32), sd((1, S5_G * S5_P), F32), sd((1, S5_G * S5_P), F32)],
        [col, cmp_spec, cmp_spec, vec, avec, avec],
        [pltpu.VMEM((L, 2 * W), F32), pltpu.VMEM((L, 2 * W), F32), pltpu.VMEM((8, 2 * W), F32),
         pltpu.VMEM((L, GC), F32), pltpu.VMEM((L, GC), F32), pltpu.VMEM((L, GC), F32)])


def _s5_compact_consts():
    g_row = np.arange(S5_GB * S5_C) // S5_C
    col = np.arange(2 * S5_W)
    g_col = (col % S5_W) // S5_P
    mask = (g_row[:, None] == g_col[None, :]).astype(np.float32)
    tgt = (col // S5_W) * S5_P + col % S5_P
    rmat = (tgt[:, None] == np.arange(2 * S5_P)[None, :]).astype(np.float32)
    return jnp.asarray(mask), jnp.asarray(rmat)


def _attn_scores(q_ref, k_ref, qb, bq, scale):
    ext = (qb + 1) * bq
    s = _dot_nt(q_ref[qb * bq:ext, :], k_ref[0:ext, :]) * scale
    qpos = lax.broadcasted_iota(jnp.int32, (bq, bq), 0)
    kpos = lax.broadcasted_iota(jnp.int32, (bq, bq), 1)
    diag = jnp.where(kpos <= qpos, s[:, ext - bq:], NEG)
    return diag if qb == 0 else jnp.concatenate([s[:, :ext - bq], diag], axis=-1)


def _attn_fwd(qp, kp, v, scale):
    L = qp.shape[0]
    bq = min(256, L)

    def body(q_ref, k_ref, v_ref, o_ref, lse_ref):
        for qb in range(L // bq):
            rows = slice(qb * bq, (qb + 1) * bq)
            s = _attn_scores(q_ref, k_ref, qb, bq, scale)
            m = jnp.max(s, axis=-1, keepdims=True)
            e = jnp.exp(s - m)
            l = jnp.sum(e, axis=-1, keepdims=True)
            o_ref[rows, :] = _dot(e.astype(BF16), v_ref[0:(qb + 1) * bq, :]) / l
            lse_ref[rows, :] = jnp.broadcast_to(m + jnp.log(l), (bq, HD))

    blk = pl.BlockSpec((L, HD), lambda h: (0, h))
    wide = pl.BlockSpec((L, 2 * HD), lambda h: (0, h))
    return pl.pallas_call(
        body, name="mla_attn_fwd", grid=(MLA_H,),
        in_specs=[wide, wide, blk], out_specs=[blk, blk],
        out_shape=[jax.ShapeDtypeStruct((L, MLA_H * HD), F32)] * 2,
        compiler_params=pltpu.CompilerParams(dimension_semantics=("arbitrary",), vmem_limit_bytes=VMEM_LIMIT),
    )(qp, kp, v)


def _attn_bwd(qp, kp, v, o, lse, do, scale):
    L = qp.shape[0]
    bq = min(256, L)
    nq = L // bq

    def body(q_ref, k_ref, v_ref, o_ref, lse_ref, do_ref, dq_ref, dk_ref, dv_ref, dk_acc, dv_acc):
        dk_acc[...] = jnp.zeros_like(dk_acc)
        dv_acc[...] = jnp.zeros_like(dv_acc)
        for qb in range(nq):
            rows = slice(qb * bq, (qb + 1) * bq)
            ext = (qb + 1) * bq
            do = do_ref[rows, :]
            dob = do.astype(BF16)
            p = jnp.exp(_attn_scores(q_ref, k_ref, qb, bq, scale) - lse_ref[rows, 0:1])
            dp = _dot_nt(dob, v_ref[0:ext, :])
            dsum = jnp.sum(do * o_ref[rows, :], axis=-1, keepdims=True)
            ds = (p * (dp - dsum) * scale).astype(BF16)
            dq_ref[rows, :] = _dot(ds, k_ref[0:ext, :]).astype(dq_ref.dtype)
            dk_acc[0:ext, :] += _dot_tn(ds, q_ref[rows, :])
            dv_acc[0:ext, :] += _dot_tn(p.astype(BF16), dob)
        dk_ref[...] = dk_acc[...].astype(dk_ref.dtype)
        dv_ref[...] = dv_acc[...].astype(dv_ref.dtype)

    sd = jax.ShapeDtypeStruct
    blk = pl.BlockSpec((L, HD), lambda h: (0, h))
    wide = pl.BlockSpec((L, 2 * HD), lambda h: (0, h))
    return pl.pallas_call(
        body, name="mla_attn_bwd", grid=(MLA_H,),
        in_specs=[wide, wide, blk, blk, blk, blk], out_specs=[wide, wide, blk],
        out_shape=[sd((L, MLA_H * 2 * HD), BF16), sd((L, MLA_H * 2 * HD), BF16), sd((L, MLA_H * HD), BF16)],
        scratch_shapes=[pltpu.VMEM((L, 2 * HD), F32), pltpu.VMEM((L, HD), F32)],
        compiler_params=pltpu.CompilerParams(dimension_semantics=("arbitrary",), vmem_limit_bytes=VMEM_LIMIT),
    )(qp, kp, v, o, lse, do)


def _kv_fn(mem, gm, w, gk):
    kv = _mm(_rms(mem, gm, D_MODEL), w)
    k = jnp.concatenate([_rms(kv[:, HD * h:HD * (h + 1)], gk, HD) for h in range(X_HEADS)], axis=-1)
    return k, kv[:, XQ:]


def _kv_prep(mem, gm, w, gk, name):
    def fn(mem, gm, w, gk):
        return _kv_fn(mem, gm, w, gk)
    M = mem.shape[0]
    return _rowwise(name, fn, [('c', mem), ('c', gm), ('c', w), ('c', gk)],
                    [('c', (M, XQ), F32), ('c', (M, XQ), F32)], 1)


def _kv_prep_bwd(mem, gm, w, gk, dk, dv, name):
    def fn(mem, gm, w, gk, dk, dv):
        _, vjp = jax.vjp(lambda a, b, c: _kv_fn(mem, a, b, c), gm, w, gk)
        return vjp((dk, dv))
    return _rowwise(name, fn, [('c', mem), ('c', gm), ('c', w), ('c', gk), ('c', dk), ('c', dv)],
                    [('c', gm.shape, F32), ('c', w.shape, BF16), ('c', gk.shape, F32)], 1)


def _forward_merge(x, mix, mix_kind, xq, gate, k, v, gq, wout, name, nblk, host=None):
    def fn(x, mix, xq, gate, k, v, gq, wout):
        o = _merge(mix, xq, gate, k, v, gq)
        return (x + _dot(o.astype(BF16), wout),)
    L = x.shape[0]
    out = _rowwise(name, fn, [('r', x), (mix_kind, mix), ('r', xq), ('r', gate), ('c', k), ('c', v), ('c', gq),
                              ('c', wout)], [('r', (L, D_MODEL), F32)], nblk, host=host)
    return out[0] if host is None else (out[0][0], out[1])


def _backward_merge(dx, mix, mix_kind, xq, gate, k, v, gq, wout, name, nblk, host=None):
    def fn(dx, mix, xq, gate, k, v, gq, wout):
        g16 = dx.astype(BF16)
        do = _dot_nt(g16, wout)
        o, vjp = jax.vjp(_merge, mix, xq, gate, k, v, gq)
        dmix, dxq, dgate, dk, dv, dgq = vjp(do)
        return dmix, dxq, dgate, o, g16, dk, dv, dgq
    L = dx.shape[0]
    return _rowwise(
        name, fn,
        [('r', dx), (mix_kind, mix), ('r', xq), ('r', gate), ('c', k), ('c', v), ('c', gq), ('c', wout)],
        [('r', (L, PRIM), F32), ('r', (L, XQ), BF16), ('r', (L, BRANCH), BF16), ('t', (BRANCH, L), BF16),
         ('r', (L, D_MODEL), BF16), ('a', k.shape, F32), ('a', v.shape, F32), ('a', gq.shape, F32)], nblk,
        host=host)


_MLA_IN = 3392
_MLA_IN_PAD = 3456


def _uq_rows(wt):
    r = wt.reshape(MLA_H, HD + ROPE, wt.shape[1])
    return jnp.concatenate([r[:, :HD].reshape(PRIM, -1),
                            jnp.pad(r[:, HD:], ((0, 0), (0, HD - ROPE), (0, 0))).reshape(PRIM, -1)], axis=0)


_UQ_ROW_BANDS = [band for h in range(MLA_H) for band in ((h * HD, HD), (PRIM + h * HD, ROPE))]


_MLA_IN_ROW_BANDS = [(0, 768), (3328, 64), (768, 2560)]


_SMALL = (("ln_gain", 2048), ("mem_norm", 2048), ("xq_norm", 256), ("xk_norm", 256), ("s5_lambda_re", 6144),
          ("s5_lambda_im", 6144), ("s5_log_step", 96), ("s5_b_re", 98304), ("s5_b_im", 98304), ("s5_c_re", 98304),
          ("s5_c_im", 98304), ("s5_d", 1536), ("mla_q_lora_norm", 512), ("mla_kv_lora_norm", 256),
          ("mla_q_nope_norm", 128), ("mla_k_nope_norm", 128), ("mla_q_rope_norm", 64), ("mla_k_rope_norm", 64))
_SMALL_ROWS = 432
_SMALL_OFF = {name: sum(n for _, n in _SMALL[:i]) for i, (name, _) in enumerate(_SMALL)}


def _pack_small(d):
    flat = jnp.concatenate([d[n].reshape(-1).astype(F32) for n, _ in _SMALL])
    return jnp.pad(flat, (0, _SMALL_ROWS * 1024 - flat.shape[0])).reshape(_SMALL_ROWS, 1024)


def _unpack_small(p, name, shape):
    off = _SMALL_OFF[name]
    return p.reshape(-1)[off:off + int(np.prod(shape))].reshape(shape)


_WEIGHTS = ('ln_gain', 'w_out', 'mem_norm', 'w_mem_kv', 'xq_norm', 'xk_norm', 's5_w_in', 's5_lambda_re',
            's5_lambda_im', 's5_log_step', 's5_b_re', 's5_b_im', 's5_c_re', 's5_c_im', 's5_d', 's5_w_glu', 'mla_w_in',
            'mla_q_lora_norm', 'mla_kv_lora_norm', 'mla_w_uq', 'mla_w_ukv', 'mla_q_nope_norm', 'mla_k_nope_norm',
            'mla_q_rope_norm', 'mla_k_rope_norm')


def _pad128(g):
    return jnp.pad(g.reshape(1, -1), ((0, 0), (0, HD - g.shape[-1])))


def kernel(x, mem, positions, ln_gain, w_out, mem_norm, w_mem_kv, xq_norm, xk_norm, s5_w_in, s5_lambda_re, s5_lambda_im, s5_log_step, s5_b_re, s5_b_im, s5_c_re, s5_c_im, s5_d, s5_w_glu, mla_w_in, mla_q_lora_norm, mla_kv_lora_norm, mla_w_uq, mla_w_ukv, mla_q_nope_norm, mla_k_nope_norm, mla_q_rope_norm, mla_k_rope_norm, loss_target, m_ln_gain, m_w_out, m_mem_norm, m_w_mem_kv, m_xq_norm, m_xk_norm, m_s5_w_in, m_s5_lambda_re, m_s5_lambda_im, m_s5_log_step, m_s5_b_re, m_s5_b_im, m_s5_c_re, m_s5_c_im, m_s5_d, m_s5_w_glu, m_mla_w_in, m_mla_q_lora_norm, m_mla_kv_lora_norm, m_mla_w_uq, m_mla_w_ukv, m_mla_q_nope_norm, m_mla_k_nope_norm, m_mla_q_rope_norm, m_mla_k_rope_norm, v_ln_gain, v_w_out, v_mem_norm, v_w_mem_kv, v_xq_norm, v_xk_norm, v_s5_w_in, v_s5_lambda_re, v_s5_lambda_im, v_s5_log_step, v_s5_b_re, v_s5_b_im, v_s5_c_re, v_s5_c_im, v_s5_d, v_s5_w_glu, v_mla_w_in, v_mla_q_lora_norm, v_mla_kv_lora_norm, v_mla_w_uq, v_mla_w_ukv, v_mla_q_nope_norm, v_mla_k_nope_norm, v_mla_q_rope_norm, v_mla_k_rope_norm):
    weights = dict(ln_gain=ln_gain, w_out=w_out, mem_norm=mem_norm, w_mem_kv=w_mem_kv, xq_norm=xq_norm,
                   xk_norm=xk_norm, s5_w_in=s5_w_in, s5_lambda_re=s5_lambda_re, s5_lambda_im=s5_lambda_im,
                   s5_log_step=s5_log_step, s5_b_re=s5_b_re, s5_b_im=s5_b_im, s5_c_re=s5_c_re, s5_c_im=s5_c_im,
                   s5_d=s5_d, s5_w_glu=s5_w_glu, mla_w_in=mla_w_in, mla_q_lora_norm=mla_q_lora_norm,
                   mla_kv_lora_norm=mla_kv_lora_norm, mla_w_uq=mla_w_uq, mla_w_ukv=mla_w_ukv,
                   mla_q_nope_norm=mla_q_nope_norm, mla_k_nope_norm=mla_k_nope_norm,
                   mla_q_rope_norm=mla_q_rope_norm, mla_k_rope_norm=mla_k_rope_norm)
    m_in = dict(zip(_WEIGHTS, (m_ln_gain, m_w_out, m_mem_norm, m_w_mem_kv, m_xq_norm, m_xk_norm, m_s5_w_in,
                               m_s5_lambda_re, m_s5_lambda_im, m_s5_log_step, m_s5_b_re, m_s5_b_im, m_s5_c_re,
                               m_s5_c_im, m_s5_d, m_s5_w_glu, m_mla_w_in, m_mla_q_lora_norm, m_mla_kv_lora_norm,
                               m_mla_w_uq, m_mla_w_ukv, m_mla_q_nope_norm, m_mla_k_nope_norm, m_mla_q_rope_norm,
                               m_mla_k_rope_norm)))
    v_in = dict(zip(_WEIGHTS, (v_ln_gain, v_w_out, v_mem_norm, v_w_mem_kv, v_xq_norm, v_xk_norm, v_s5_w_in,
                               v_s5_lambda_re, v_s5_lambda_im, v_s5_log_step, v_s5_b_re, v_s5_b_im, v_s5_c_re,
                               v_s5_c_im, v_s5_d, v_s5_w_glu, v_mla_w_in, v_mla_q_lora_norm, v_mla_kv_lora_norm,
                               v_mla_w_uq, v_mla_w_ukv, v_mla_q_nope_norm, v_mla_k_nope_norm, v_mla_q_rope_norm,
                               v_mla_k_rope_norm)))

    x0 = x[0]
    mem0 = mem[0]
    target = loss_target[0]
    L = x0.shape[0]
    nblk = 4
    nb_big = 8
    me = 4 * lax.axis_index("x") + 2 * lax.axis_index("y") + lax.axis_index("c")

    lora = jnp.pad(jnp.concatenate([mla_q_lora_norm, mla_kv_lora_norm], axis=1), ((0, 7), (0, HD - 96)))
    def gather(*shards):
        return _plan_all_gather(list(shards))

    kh = D_MODEL // 2
    (b_mkv0, b_glu, b_in_mla, b_out0, b_uq, b_ukv, b_mkv1, b_out1), (W_in_s5,) = _cast_call(
        [w_mem_kv[0], s5_w_glu[0], jnp.transpose(mla_w_in[0]), w_out[0], jnp.transpose(mla_w_uq[0]), mla_w_ukv[0],
         w_mem_kv[1], w_out[1]], "cast_shards", host=gather(s5_w_in[0].astype(BF16)))

    ln0, ln1 = ln_gain[0:1], ln_gain[1:2]
    gq0, gq1 = xq_norm[0:1], xq_norm[1:2]
    gk0, gk1 = xk_norm[0:1], xk_norm[1:2]
    gm0, gm1 = mem_norm[0:1], mem_norm[1:2]
    gqn, gkn = mla_q_nope_norm, mla_k_nope_norm
    gqr, gkr = _pad128(mla_q_rope_norm), _pad128(mla_k_rope_norm)

    lr3 = s5_lambda_re.reshape(S5_G, 1, S5_P)
    li3 = s5_lambda_im.reshape(S5_G, 1, S5_P)
    ls3 = s5_log_step.reshape(S5_G, 1, 1)
    btr = jnp.swapaxes(s5_b_re[0], 1, 2)
    bti = jnp.swapaxes(s5_b_im[0], 1, 2)
    a_r, a_i, bm, cm = _s5_params(lr3, li3, ls3, btr, bti, s5_c_re[0], s5_c_im[0])
    a_r2 = a_r.reshape(1, S5_G * S5_P)
    a_i2 = a_i.reshape(1, S5_G * S5_P)
    cmask, rmat = _s5_compact_consts()

    half = ROPE // 2
    inv_freq = ROPE_THETA ** (-jnp.arange(half, dtype=F32) / half)
    invf = jnp.concatenate([inv_freq, inv_freq, jnp.zeros((HD - ROPE,), F32)]).reshape(1, HD)

    def rot_tables(pos, invf):
        ang = pos.astype(F32) * invf
        lane = lax.broadcasted_iota(jnp.int32, ang.shape, 1)
        c = jnp.where(lane < ROPE, jnp.cos(ang), 0.0)
        s = jnp.sin(ang)
        return c, jnp.where(lane < half, -s, 0.0), jnp.where((lane >= half) & (lane < ROPE), s, 0.0)

    tc, ts1, ts2 = _rowwise("rot_tables", rot_tables, [('r', positions.reshape(L, 1)), ('c', invf)],
                            [('r', (L, HD), F32)] * 3, nblk)

    def in_s5(x, g, w):
        proj = _mm_slots(_rms(x, g, D_MODEL).astype(BF16), w)
        return proj[:, :PRIM], proj[:, PRIM:PRIM + XQ], proj[:, PRIM + XQ:]

    u_s5, xq_a, gate_a = _rowwise(
        "s5_in", in_s5, [('r', x0), ('c', ln0), ('c', W_in_s5)],
        [('r', (L, PRIM), F32), ('r', (L, XQ), F32), ('r', (L, BRANCH), F32)], nblk)
    (y_s5, s5_carry), (W_glu, G_mkv0, G_in_mla_a) = _s5_fwd(u_s5, bm, cm, a_r2, a_i2, s5_d,
                                                            host=gather(b_glu, b_mkv0, b_in_mla[:, :kh]))

    def glu(y, w):
        z = _mm_slots(_gelu(y).astype(BF16), w)
        return z[:, :PRIM] * _sigmoid(z[:, PRIM:]), z

    (y2, z_glu), (G_out0,) = _rowwise("s5_glu", glu, [('r', y_s5), ('c', W_glu)],
                                      [('r', (L, PRIM), F32), ('r', (L, 2 * PRIM), F32)], nblk, host=gather(b_out0))
    W_mkv0 = G_mkv0.reshape(D_MODEL, 2 * XQ)
    k_a, v_a = _kv_prep(mem0, gm0, W_mkv0, gk0, "kv_prep0")
    x1, (G_in_mla_b,) = _forward_merge(
        x0, y2, 'r', xq_a, gate_a, k_a, v_a, gq0, G_out0.reshape(BRANCH, D_MODEL), "merge0", nblk,
        host=gather(b_in_mla[:, kh:]))
    W_in_mla = jnp.concatenate([G_in_mla_a, G_in_mla_b], axis=2).reshape(_MLA_IN, D_MODEL)

    def in_mla(x, g, w):
        xn = _rms(x, g, D_MODEL).astype(BF16)
        a = _dot_nt(xn, w[0:768])
        kx = _dot_nt(xn, w[768:896])
        b = _dot_nt(xn, w[832:_MLA_IN])
        lane = lax.broadcasted_iota(jnp.int32, kx.shape, 1)
        return a[:, :512], a[:, 512:], b[:, :XQ], b[:, XQ:], jnp.where(lane < ROPE, kx, 0.0)

    (c_q, c_kv, xq_b, gate_b, krp), (G_uq, W_kv, G_lora) = _rowwise(
        "mla_in", in_mla, [('r', x1), ('c', ln1), ('c', W_in_mla)],
        [('r', (L, Q_LORA), F32), ('r', (L, KV_LORA), F32), ('r', (L, XQ), F32), ('r', (L, BRANCH), F32),
         ('r', (L, HD), F32)], nblk,
        host=gather(b_uq, b_ukv, lora))
    W_q = _uq_rows(G_uq.reshape(MLA_H * (HD + ROPE), Q_LORA))
    g_qlora = G_lora[:, 0, :64].reshape(1, Q_LORA)
    g_kvlora = G_lora[:, 0, 64:96].reshape(1, KV_LORA)

    def qkv(c_q, c_kv, krp, tc, ts1, ts2, gql, gkvl, wq, wkv, gqn, gkn, gqr, gkr):
        q = _dot_nt(_rms(c_q, gql, Q_LORA).astype(BF16), wq)
        kv = _mm_slots(_rms(c_kv, gkvl, KV_LORA).astype(BF16), wkv)
        kp, v = _kv_post(*_kv_chunks(kv), krp, gkn, gkr, tc, ts1, ts2)
        return _q_post(*_q_chunks(q), gqn, gqr, tc, ts1, ts2), kp, v

    qkv_consts = [('c', g_qlora), ('c', g_kvlora), ('c', W_q), ('c', W_kv), ('c', gqn), ('c', gkn), ('c', gqr),
                  ('c', gkr)]
    (q_pad, k_pad, v_h), (G_mkv1, G_out1) = _rowwise(
        "mla_qkv", qkv, [('r', c_q), ('r', c_kv), ('r', krp), ('r', tc), ('r', ts1), ('r', ts2)] + qkv_consts,
        [('r', (L, 2 * PRIM), BF16), ('r', (L, 2 * PRIM), BF16), ('r', (L, PRIM), BF16)], nblk,
        host=gather(b_mkv1, b_out1))
    W_out = (G_out0.reshape(BRANCH, D_MODEL), G_out1.reshape(BRANCH, D_MODEL))
    W_mkv = (W_mkv0, G_mkv1.reshape(D_MODEL, 2 * XQ))
    scale = (HD + ROPE) ** -0.5
    attn, lse = _attn_fwd(q_pad, k_pad, v_h, scale)
    k_b, v_b = _kv_prep(mem0, gm1, W_mkv[1], gk1, "kv_prep1")

    def merge_loss(x, mix, xq, gate, k, v, gq, wout, t):
        err = x + _dot(_merge(mix, xq, gate, k, v, gq).astype(BF16), wout) - t
        part = 0.5 * jnp.sum(jnp.sum(err * err, axis=-1, keepdims=True) * (1.0 / D_MODEL), axis=0, keepdims=True)
        return err * (1.0 / D_MODEL), jnp.broadcast_to(part, (1, HD))

    dx2, loss_part = _rowwise(
        "merge1_loss", merge_loss,
        [('r', x1), ('r', attn), ('r', xq_b), ('r', gate_b), ('c', k_b), ('c', v_b), ('c', gq1), ('c', W_out[1]),
         ('r', target)], [('r', (L, D_MODEL), F32), ('a', (1, HD), F32)], nblk)

    dattn, dxq_b, dgate_b, o_b, g_b, dk_b, dv_b, dgq1 = _backward_merge(
        dx2, attn, 'r', xq_b, gate_b, k_b, v_b, gq1, W_out[1], "merge1_bwd", nb_big)
    dgm1, dW_mkv1, dgk1 = _kv_prep_bwd(mem0, gm1, W_mkv[1], gk1, dk_b, dv_b, "kv_prep1_bwd")
    dW_out1 = _matmul_tn(o_b, g_b, "dw_out1")
    dq_pad, dk_pad, dv_h = _attn_bwd(q_pad, k_pad, v_h, attn, lse, dattn, scale)

    def qkv_bwd(c_q, c_kv, krp, tc, ts1, ts2, dqp, dkp, dv, gql, gkvl, wq, wkv, gqn, gkn, gqr, gkr):
        cqn, vjp_qn = jax.vjp(lambda a, b: _rms(a, b, Q_LORA), c_q, gql)
        ckvn, vjp_kvn = jax.vjp(lambda a, b: _rms(a, b, KV_LORA), c_kv, gkvl)
        cqn16 = cqn.astype(BF16)
        ckvn16 = ckvn.astype(BF16)
        q = _dot_nt(cqn16, wq)
        kv = _mm_slots(ckvn16, wkv)
        _, vjp_q = jax.vjp(lambda n, r, a, b: _q_post(n, r, a, b, tc, ts1, ts2), *_q_chunks(q), gqn, gqr)
        dnope, drope, dgqn, dgqr = vjp_q(dqp.astype(F32))
        dq = jnp.concatenate(dnope + drope, axis=-1)
        _, vjp_kv = jax.vjp(lambda n, v, k, a, b: _kv_post(n, v, k, a, b, tc, ts1, ts2), *_kv_chunks(kv), krp, gkn,
                            gkr)
        dkn, dvals, dkrp, dgkn, dgkr = vjp_kv((dkp.astype(F32), dv.astype(F32)))
        dkv = jnp.concatenate([x for pair in zip(dkn, dvals) for x in pair], axis=-1)
        dq16 = dq.astype(BF16)
        dkv16 = dkv.astype(BF16)
        dc_q, dgql = vjp_qn(_dot(dq16, wq))
        dc_kv, dgkvl = vjp_kvn(_mm_slots_nt(dkv16, wkv))
        return dc_q, dc_kv, dkrp, cqn16, dq16, ckvn16, dkv16, dgql, dgkvl, dgqn, dgkn, dgqr, dgkr

    (dc_q, dc_kv, dkrp, cqn16, dq16, ckvn16, dkv16, dgql, dgkvl, dgqn, dgkn, dgqr, dgkr) = _rowwise(
        "mla_qkv_bwd", qkv_bwd,
        [('r', c_q), ('r', c_kv), ('r', krp), ('r', tc), ('r', ts1), ('r', ts2), ('r', dq_pad), ('r', dk_pad),
         ('r', dv_h)] + qkv_consts,
        [('r', (L, Q_LORA), BF16), ('r', (L, KV_LORA), BF16), ('r', (L, HD), BF16), ('r', (L, Q_LORA), BF16),
         ('t', (2 * PRIM, L), BF16), ('t', (KV_LORA, L), BF16), ('r', (L, 2 * PRIM), BF16),
         ('a', (1, Q_LORA), F32), ('a', (1, KV_LORA), F32), ('a', (1, HD), F32), ('a', (1, HD), F32),
         ('a', (1, HD), F32), ('a', (1, HD), F32)], nb_big)
    dW_q = _matmul_tn(dq16, cqn16, "dw_uq", row_bands=_UQ_ROW_BANDS)
    dW_kv = _matmul_tn_slots(ckvn16, dkv16, "dw_ukv")

    def in_bwd(x, dres, g, w, *dparts):
        dproj = jnp.concatenate(dparts, axis=-1).astype(BF16)
        xn, vjp = jax.vjp(lambda a, b: _rms(a, b, D_MODEL), x, g)
        if w.ndim == 3:
            dxn = _mm_slots_nt(dproj, w)
        else:
            dkr = dproj[:, 3328:]
            dkr = jnp.where(lax.broadcasted_iota(jnp.int32, dkr.shape, 1) < ROPE, dkr, jnp.zeros_like(dkr))
            dxn = _dot(dproj[:, :768], w[0:768]) + _dot(dproj[:, 768:3328], w[832:_MLA_IN]) + _dot(dkr, w[768:896])
        dx, dg = vjp(dxn)
        return dx + dres, xn, dproj, dg

    dx1, xn1, dproj1, dln1 = _rowwise(
        "mla_in_bwd", in_bwd,
        [('r', x1), ('r', dx2), ('c', ln1), ('c', W_in_mla), ('r', dc_q), ('r', dc_kv), ('r', dxq_b), ('r', dgate_b),
         ('r', dkrp)],
        [('r', (L, D_MODEL), F32), ('r', (L, D_MODEL), BF16), ('t', (_MLA_IN_PAD, L), BF16), ('a', (1, D_MODEL), F32)],
        nblk)
    dW_in_mla = _matmul_tn(dproj1, xn1, "dw_mla_in", row_bands=_MLA_IN_ROW_BANDS)

    grads1 = [dW_out1.reshape(N_DEV, 256, D_MODEL), dW_mkv1.reshape(N_DEV, 128, 2 * XQ),
              dW_in_mla.reshape(N_DEV, 424, D_MODEL),
              dW_q.reshape(N_DEV, 288, Q_LORA), dW_kv]
    (dy2, dxq_a, dgate_a, o_a, g_a, dk_a, dv_a, dgq0), pair1 = _backward_merge(
        dx1, y2, 'r', xq_a, gate_a, k_a, v_a, gq0, W_out[0], "merge0_bwd", nb_big, host=_plan_pair(grads1))
    dgm0, dW_mkv0, dgk0 = _kv_prep_bwd(mem0, gm0, W_mkv[0], gk0, dk_a, dv_a, "kv_prep0_bwd")
    dW_out0 = _matmul_tn(o_a, g_a, "dw_out0")
    t1 = list(_pair_add(grads1, pair1, "rs_add_layer1"))

    def glu_bwd(y, z, dy2, w):
        h, vjp_h = jax.vjp(_gelu, y)
        _, vjp_z = jax.vjp(lambda a, b: a * _sigmoid(b), z[:, :PRIM], z[:, PRIM:])
        dz16 = jnp.concatenate(vjp_z(dy2), axis=-1).astype(BF16)
        return vjp_h(_mm_slots_nt(dz16, w))[0], h.astype(BF16), dz16

    grads0 = [dW_out0.reshape(N_DEV, 256, D_MODEL), dW_mkv0.reshape(N_DEV, 128, 2 * XQ)]
    (dy_s5, h16, dz16), glu_hosted = _rowwise(
        "s5_glu_bwd", glu_bwd, [('r', y_s5), ('r', z_glu), ('r', dy2), ('c', W_glu)],
        [('r', (L, PRIM), F32), ('t', (PRIM, L), BF16), ('r', (L, 2 * PRIM), BF16)], nb_big,
        host=_combine(_plan_chips(t1[2:3]), _plan_pair(grads0)))
    recv_in_mla, pair0 = glu_hosted[:1], glu_hosted[1:]
    dW_glu = _matmul_tn_slots(h16, dz16, "dw_glu")
    t0 = list(_pair_add(grads0 + [dW_glu], pair0 + list(_exchange_call(_plan_pair([dW_glu]), "rs_pair_glu")),
                        "rs_add_layer0"))
    both = [jnp.concatenate([t0[i], t1[i]], axis=1) for i in range(2)]
    (du_s5, dbc, dcc, dd, dar, dai), recv_rest = _s5_bwd(u_s5, dy_s5, s5_carry, bm, cm, a_r2, a_i2, s5_d,
                                                        cmask, rmat, host=_plan_chips(both + t1[3:] + t0[2:]))
    early_recv = recv_rest[:2] + recv_in_mla + recv_rest[2:]
    dbc4 = dbc.reshape(S5_G, S5_C, 2, S5_P)
    dcc4 = dcc.reshape(S5_G, S5_C, 2, S5_P)
    dlr, dli, dls, dbtr, dbti = _s5_params_bwd(
        lr3, li3, ls3, btr, bti, dar.reshape(S5_G, 1, S5_P), dai.reshape(S5_G, 1, S5_P), dbc4[:, :, 0], dbc4[:, :, 1])

    small_part = {
        "ln_gain": jnp.concatenate([jnp.zeros_like(dln1), dln1]), "mem_norm": jnp.concatenate([dgm0, dgm1]),
        "xq_norm": jnp.concatenate([dgq0, dgq1]), "xk_norm": jnp.concatenate([dgk0, dgk1]),
        "s5_lambda_re": dlr, "s5_lambda_im": dli, "s5_log_step": dls,
        "s5_b_re": jnp.swapaxes(dbtr, 1, 2), "s5_b_im": jnp.swapaxes(dbti, 1, 2),
        "s5_c_re": dcc4[:, :, 0], "s5_c_im": -dcc4[:, :, 1], "s5_d": dd,
        "mla_q_lora_norm": dgql, "mla_kv_lora_norm": dgkvl, "mla_q_nope_norm": dgqn, "mla_k_nope_norm": dgkn,
        "mla_q_rope_norm": dgqr[:, :ROPE], "mla_k_rope_norm": dgkr[:, :ROPE],
    }
    loss8 = jnp.pad(loss_part, ((0, 7), (0, 0)))
    packed = _pack_small(small_part).astype(BF16)
    first_rows = 224
    (dx0, xn0, dproj0, dln0), (small_gath_a, loss_g) = _rowwise(
        "s5_in_bwd", in_bwd,
        [('r', x0), ('r', dx1), ('c', ln0), ('c', W_in_s5), ('r', du_s5), ('r', dxq_a),
         ('r', dgate_a)],
        [('r', (L, D_MODEL), F32), ('t', (D_MODEL, L), BF16), ('r', (L, 2 * BRANCH), BF16), ('a', (1, D_MODEL), F32)],
        nblk, host=_plan_all_gather([packed[:first_rows], loss8]))
    dW_in_s5, (small_gath_b, ln0_gath) = _matmul_tn_slots(
        xn0, dproj0, "dw_s5_in", host=_plan_all_gather([packed[first_rows:], jnp.pad(dln0, ((0, 7), (0, 0)))]))
    small_gath = jnp.concatenate([small_gath_a, small_gath_b], axis=1)

    late = [dW_in_s5]
    late_t = _pair_add(late, list(_exchange_call(_plan_pair(late), "rs_pair_late")), "rs_add_late")
    late_recv = list(_exchange_call(_plan_chips(late_t), "rs_chips_late"))
    owners = ["w_out", "w_mem_kv", "mla_w_in", "mla_w_uq", "mla_w_ukv", "s5_w_glu", "s5_w_in"]
    flipped = ("mla_w_in", "mla_w_uq")

    def shard(d, n):
        a = d[n]
        return jnp.transpose(a[0]) if n in flipped else a.reshape(-1, a.shape[-1])

    upd, _ = _updates_call(
        early_recv + late_recv, [shard(weights, n) for n in owners], [shard(m_in, n) for n in owners],
        [shard(v_in, n) for n in owners], "update_big")
    grads, delta, new_m, new_v = {}, {}, {}, {}
    for n, res in zip(owners, upd):
        shape = weights[n].shape
        grads[n], delta[n], new_m[n], new_v[n] = (
            (jnp.transpose(r)[None] if n in flipped else r.reshape(shape)) for r in res)

    gs, loss_sum = _small_sum(small_gath, loss_g, ln0_gath, "small_sum")
    loss = loss_sum[0, 0]
    for n, _ in _SMALL:
        shape = weights[n].shape
        if n == "mla_q_lora_norm":
            grads[n] = lax.dynamic_slice(_unpack_small(gs, n, (Q_LORA,)), (me * 64,), (64,)).reshape(shape)
        elif n == "mla_kv_lora_norm":
            grads[n] = lax.dynamic_slice(_unpack_small(gs, n, (KV_LORA,)), (me * 32,), (32,)).reshape(shape)
        else:
            grads[n] = _unpack_small(gs, n, shape)

    def own(n, a):
        if a.ndim == 4:
            a = jnp.transpose(a, (0, 2, 3, 1))
        elif a.ndim == 3:
            a = jnp.transpose(a, (0, 2, 1))
        return a.reshape(a.shape[1:]) if a.ndim >= 3 else a

    def back(n, a):
        shape = weights[n].shape
        if len(shape) == 4:
            return jnp.transpose(a.reshape((1,) + a.shape), (0, 3, 1, 2))
        if len(shape) == 3:
            return jnp.transpose(a.reshape((1,) + a.shape), (0, 2, 1))
        return a.reshape(shape)

    wide = ("s5_b_re", "s5_b_im", "s5_c_re", "s5_c_im")
    for names, nb, call in (([n for n, _ in _SMALL if n not in wide], 1, "update_small"), (wide, 4, "update_s5_bc")):
        res = _adamw_multi([own(n, weights[n]) for n in names], [own(n, grads[n]) for n in names],
                           [own(n, m_in[n]) for n in names], [own(n, v_in[n]) for n in names], call, nb)
        for n, (dl, m2, v2) in zip(names, res):
            delta[n], new_m[n], new_v[n] = back(n, dl), back(n, m2), back(n, v2)
    return (loss, dx0[None], *[grads[n] for n in _WEIGHTS], *[delta[n] for n in _WEIGHTS],
            *[new_m[n] for n in _WEIGHTS], *[new_v[n] for n in _WEIGHTS])
```

```python
import functools
import math

import numpy as np
import jax
import jax.numpy as jnp
from jax import lax
from jax.experimental import pallas as pl
from jax.experimental.pallas import tpu as pltpu

F32 = jnp.float32
BF16 = jnp.bfloat16
EPS = 1e-6
NEG = float(np.finfo(np.float32).min)
MESH = pl.DeviceIdType.MESH

N_DEV = 8
D_MODEL = 1024
MEM_LEN = 256
XQ = 512
PRIM = 1536
BRANCH = 2048
X_HEADS = 4
HD = 128
S5_G = 96
S5_P = 64
S5_C = 16
S5_GB = 8
S5_W = S5_GB * S5_P
MLA_H = 12
ROPE = 64
Q_LORA = 512
KV_LORA = 256
ROPE_THETA = 10000.0

ADAM_LR = 0.001
ADAM_B1 = 0.9
ADAM_B2 = 0.999
ADAM_EPS = 1e-08
ADAM_WD = 0.01
ADAM_STEP = 10

VMEM_LIMIT = 56 * 1024 * 1024


def _dot(a, b):
    return jnp.dot(a, b, preferred_element_type=F32)


def _dot_nt(a, b):
    return lax.dot_general(a, b, (((1,), (1,)), ((), ())), preferred_element_type=F32)


def _dot_tn(a, b):
    return lax.dot_general(a, b, (((0,), (0,)), ((), ())), preferred_element_type=F32)


@jax.custom_vjp
def _mm(a, b):
    return _dot(a.astype(BF16), b.astype(BF16))


def _mm_fwd(a, b):
    return _mm(a, b), (a, b)


def _mm_bwd(res, g):
    a, b = res
    gb = g.astype(BF16)
    return _dot_nt(gb, b.astype(BF16)).astype(a.dtype), _dot_tn(a.astype(BF16), gb).astype(b.dtype)


_mm.defvjp(_mm_fwd, _mm_bwd)


@jax.custom_vjp
def _mm_nt(a, b):
    return _dot_nt(a.astype(BF16), b.astype(BF16))


def _mm_nt_fwd(a, b):
    return _mm_nt(a, b), (a, b)


def _mm_nt_bwd(res, g):
    a, b = res
    gb = g.astype(BF16)
    return _dot(gb, b.astype(BF16)).astype(a.dtype), _dot_tn(gb, a.astype(BF16)).astype(b.dtype)


_mm_nt.defvjp(_mm_nt_fwd, _mm_nt_bwd)


@jax.custom_vjp
def _softmax(s):
    m = jnp.max(s, axis=-1, keepdims=True)
    e = jnp.exp(s - m)
    return e / jnp.sum(e, axis=-1, keepdims=True)


def _softmax_fwd(s):
    p = _softmax(s)
    return p, p


def _softmax_bwd(p, g):
    return (p * (g - jnp.sum(p * g, axis=-1, keepdims=True)),)


_softmax.defvjp(_softmax_fwd, _softmax_bwd)


def _rms(x, g, n):
    ms = jnp.sum(x * x, axis=-1, keepdims=True) * (1.0 / n)
    return x * lax.rsqrt(ms + EPS) * g


def _sigmoid(x):
    return 1.0 / (1.0 + jnp.exp(-x))


def _silu(x):
    return x * _sigmoid(x)


def _gelu(x):
    c = math.sqrt(2.0 / math.pi)
    return 0.5 * x * (1.0 + jnp.tanh(c * (x + 0.044715 * (x * x * x))))


@jax.custom_vjp
def _rot(x, c, s1, s2):
    return x * c + pltpu.roll(x, 96, 1) * s1 + pltpu.roll(x, 32, 1) * s2


def _rot_fwd(x, c, s1, s2):
    return _rot(x, c, s1, s2), (c, s1, s2)


def _rot_bwd(res, g):
    c, s1, s2 = res
    dx = g * c + pltpu.roll(g * s1, 32, 1) + pltpu.roll(g * s2, 96, 1)
    return dx, jnp.zeros_like(c), jnp.zeros_like(s1), jnp.zeros_like(s2)


_rot.defvjp(_rot_fwd, _rot_bwd)


def _mem_attn(xq, k, v, gq):
    outs = []
    for h in range(X_HEADS):
        sl = slice(HD * h, HD * (h + 1))
        q = _rms(xq[:, sl], gq, HD)
        p = _softmax(_mm_nt(q, k[:, sl]) * (HD ** -0.5))
        outs.append(_mm(p, v[:, sl]))
    return jnp.concatenate(outs, axis=-1)


def _merge(mix, xq, gate, k, v, gq):
    return jnp.concatenate([mix, _mem_attn(xq, k, v, gq)], axis=-1) * _silu(gate)


def _q_chunks(q):
    return ([q[:, HD * h:HD * (h + 1)] for h in range(MLA_H)],
            [q[:, PRIM + HD * h:PRIM + HD * (h + 1)] for h in range(MLA_H)])


def _q_post(nope, rope, gqn, gqr, c, s1, s2):
    pieces = []
    for qn, qr in zip(nope, rope):
        pieces.append(_rms(qn, gqn, HD))
        pieces.append(_rot(_rms(qr, gqr, ROPE), c, s1, s2))
    return jnp.concatenate(pieces, axis=-1)


def _kv_chunks(kv):
    return ([kv[:, 2 * HD * h:2 * HD * h + HD] for h in range(MLA_H)],
            [kv[:, 2 * HD * h + HD:2 * HD * (h + 1)] for h in range(MLA_H)])


def _kv_post(kn, vals, krp, gkn, gkr, c, s1, s2):
    kr = _rot(_rms(krp, gkr, ROPE), c, s1, s2)
    pieces = []
    for k in kn:
        pieces.append(_rms(k, gkn, HD))
        pieces.append(kr)
    return jnp.concatenate(pieces, axis=-1), jnp.concatenate(vals, axis=-1)


def _rowwise(name, fn, ins, outs, nblk, host=None):
    n_in = len(ins)

    def spec(kind, shape):
        if kind == 'r':
            return pl.BlockSpec((shape[0] // nblk, shape[1]), lambda i: (i, 0))
        if kind == 't':
            return pl.BlockSpec((shape[0], shape[1] // nblk), lambda i: (0, i))
        zeros = (0,) * len(shape)
        return pl.BlockSpec(tuple(shape), lambda i: zeros)

    def body(*refs):
        i = pl.program_id(0)
        res = fn(*[r[...] for r in refs[:n_in]])
        for (kind, _, _), ref, val in zip(outs, refs[n_in:], res):
            if kind == 'a':
                @pl.when(i == 0)
                def _():
                    ref[...] = jnp.zeros_like(ref)
                ref[...] += val.astype(ref.dtype)
            elif kind == 't':
                ref[...] = val.astype(F32).T.astype(ref.dtype)
            else:
                ref[...] = val.astype(ref.dtype)

    res, hosted = _hosting_call(
        body, name, nblk, host, [a for _, a in ins], [spec(k, a.shape) for k, a in ins],
        [jax.ShapeDtypeStruct(tuple(s), d) for _, s, d in outs], [spec(k, s) for k, s, _ in outs], [])
    return res if host is None else (res, hosted)


def _matmul_tn(at, g, name, out_dtype=BF16, row_bands=None):
    K, L = at.shape
    N = g.shape[1]
    tn = next(t for t in (512, 384, 256, 128) if N % t == 0)
    bands = [(0, K)] if row_bands is None else row_bands
    rows_out = sum(n for _, n in bands)

    def body(a_ref, g_ref, o_ref):
        res = _dot(a_ref[...], g_ref[...]).astype(o_ref.dtype)
        row = 0
        for start, n in bands:
            o_ref[row:row + n, :] = res[start:start + n]
            row += n

    return pl.pallas_call(
        body, name=name, grid=(N // tn,),
        in_specs=[pl.BlockSpec((K, L), lambda n: (0, 0)), pl.BlockSpec((L, tn), lambda n: (0, n))],
        out_specs=pl.BlockSpec((rows_out, tn), lambda n: (0, n)),
        out_shape=jax.ShapeDtypeStruct((rows_out, N), out_dtype),
        compiler_params=pltpu.CompilerParams(dimension_semantics=("arbitrary",), vmem_limit_bytes=VMEM_LIMIT),
    )(at, g)


def _matmul_tn_slots(at, g, name, host=None):
    K, L = at.shape
    n = g.shape[1] // N_DEV

    def body(a_ref, g_ref, o_ref):
        o_ref[...] = _dot(a_ref[...], g_ref[...]).astype(o_ref.dtype)

    res, hosted = _hosting_call(
        body, name, N_DEV, host, [at, g],
        [pl.BlockSpec((K, L), lambda d: (0, 0)), pl.BlockSpec((L, n), lambda d: (0, d))],
        [jax.ShapeDtypeStruct((N_DEV, K, n), BF16)], [pl.BlockSpec((None, K, n), lambda d: (d, 0, 0))], [])
    return res[0] if host is None else (res[0], hosted)


def _mm_slots(a16, w):
    return jnp.concatenate([_dot(a16, w[d]) for d in range(N_DEV)], axis=-1)


def _mm_slots_nt(g16, w):
    n = w.shape[2]
    out = _dot_nt(g16[:, 0:n], w[0])
    for d in range(1, N_DEV):
        out = out + _dot_nt(g16[:, d * n:(d + 1) * n], w[d])
    return out


class _Exchange:
    def __init__(self, ins, outs, scratch, start, finish):
        self.ins, self.outs, self.scratch, self.start, self.finish = ins, outs, scratch, start, finish


def _xyc():
    return lax.axis_index("x"), lax.axis_index("y"), lax.axis_index("c")


def _plan_all_gather(xs):
    n = len(xs)

    def build(x_refs, out_refs, sems):
        send_sems, recv_sems, local_sems = sems
        x, y, c = _xyc()

        def copies(k, block, to, own=False):
            slot = 4 * block[0] + 2 * block[1] + block[2]
            return [pltpu.make_async_remote_copy(
                src_ref=x_refs[a] if own else out_refs[a].at[slot], dst_ref=out_refs[a].at[slot],
                send_sem=send_sems.at[k * n + a], recv_sem=recv_sems.at[k * n + a], device_id=to,
                device_id_type=MESH) for a in range(n)]

        mine = [pltpu.make_async_copy(x_refs[a], out_refs[a].at[4 * x + 2 * y + c], local_sems.at[a])
                for a in range(n)]
        return copies, mine, (x, y, c), [(1 - x, y), (x, 1 - y), (1 - x, 1 - y)]

    def first_copies(copies, me, chips):
        x, y, c = me
        first = copies(0, me, (x, y, 1 - c), own=True)
        for j, chip in enumerate(chips):
            first += copies(1 + j, me, (*chip, c), own=True)
        return first

    def start(x_refs, out_refs, sems):
        copies, mine, me, chips = build(x_refs, out_refs, sems)
        for cp in mine + first_copies(copies, me, chips):
            cp.start()

    def finish(x_refs, out_refs, sems):
        copies, mine, me, chips = build(x_refs, out_refs, sems)
        x, y, c = me
        passed = []
        for j, chip in enumerate(chips):
            for cp in copies(1 + j, (*chip, c), me):
                cp.wait_recv()
            fwd = copies(4 + j, (*chip, c), (x, y, 1 - c))
            for cp in fwd:
                cp.start()
            passed += fwd
        for cp in copies(0, (x, y, 1 - c), me):
            cp.wait_recv()
        for j, chip in enumerate(chips):
            for cp in copies(4 + j, (*chip, 1 - c), me):
                cp.wait_recv()
        for cp in first_copies(copies, me, chips) + passed:
            cp.wait_send()
        for cp in mine:
            cp.wait()

    return _Exchange(list(xs), [jax.ShapeDtypeStruct((N_DEV,) + a.shape, a.dtype) for a in xs],
                     [pltpu.SemaphoreType.DMA((7 * n,)), pltpu.SemaphoreType.DMA((7 * n,)),
                      pltpu.SemaphoreType.DMA((n,))], start, finish)


_CHIPS = ((0, 0), (0, 1), (1, 0), (1, 1))


def _plan_pair(sends):
    n = len(sends)

    def build(s_refs, o_refs, sems):
        send_sems, recv_sems = sems
        x, y, c = _xyc()
        return [pltpu.make_async_remote_copy(
            src_ref=s_refs[a].at[4 * px + 2 * py + 1 - c], dst_ref=o_refs[a].at[j],
            send_sem=send_sems.at[j * n + a], recv_sem=recv_sems.at[j * n + a], device_id=(x, y, 1 - c),
            device_id_type=MESH) for j, (px, py) in enumerate(_CHIPS) for a in range(n)]

    def start(s_refs, o_refs, sems):
        for cp in build(s_refs, o_refs, sems):
            cp.start()

    def finish(s_refs, o_refs, sems):
        for cp in build(s_refs, o_refs, sems):
            cp.wait_recv()
            cp.wait_send()

    return _Exchange(list(sends), [jax.ShapeDtypeStruct((4,) + a.shape[1:], a.dtype) for a in sends],
                     [pltpu.SemaphoreType.DMA((4 * n,)), pltpu.SemaphoreType.DMA((4 * n,))], start, finish)


def _plan_chips(ts):
    n = len(ts)
    flips = ((1, 0), (0, 1), (1, 1))

    def build(t_refs, o_refs, sems):
        send_sems, recv_sems, local_sems = sems
        x, y, c = _xyc()
        mine = 2 * x + y
        local = [pltpu.make_async_copy(t_refs[a].at[mine], o_refs[a].at[mine], local_sems.at[a]) for a in range(n)]
        remote = []
        for k, (fx, fy) in enumerate(flips):
            px = 1 - x if fx else x
            py = 1 - y if fy else y
            remote += [pltpu.make_async_remote_copy(
                src_ref=t_refs[a].at[2 * px + py], dst_ref=o_refs[a].at[mine],
                send_sem=send_sems.at[k * n + a], recv_sem=recv_sems.at[k * n + a], device_id=(px, py, c),
                device_id_type=MESH) for a in range(n)]
        return local, remote

    def start(t_refs, o_refs, sems):
        local, remote = build(t_refs, o_refs, sems)
        for cp in local + remote:
            cp.start()

    def finish(t_refs, o_refs, sems):
        local, remote = build(t_refs, o_refs, sems)
        for cp in remote:
            cp.wait_recv()
        for cp in remote:
            cp.wait_send()
        for cp in local:
            cp.wait()

    return _Exchange(list(ts), [jax.ShapeDtypeStruct(a.shape, a.dtype) for a in ts],
                     [pltpu.SemaphoreType.DMA((3 * n,)), pltpu.SemaphoreType.DMA((3 * n,)),
                      pltpu.SemaphoreType.DMA((n,))], start, finish)


def _combine(*plans):
    def parts(refs, attr):
        out, at = [], 0
        for p in plans:
            n = len(getattr(p, attr))
            out.append(refs[at:at + n])
            at += n
        return out

    def run(half):
        def go(ins, outs, sems):
            for p, a, o, s in zip(plans, parts(ins, "ins"), parts(outs, "outs"), parts(sems, "scratch")):
                getattr(p, half)(a, o, s)
        return go

    return _Exchange(sum((p.ins for p in plans), []), sum((p.outs for p in plans), []),
                     sum((p.scratch for p in plans), []), run("start"), run("finish"))


def _exchange_call(plan, name):
    n = len(plan.ins)

    def body(*refs):
        ins, outs, sems = refs[:n], refs[n:2 * n], refs[2 * n:]
        plan.start(ins, outs, sems)
        plan.finish(ins, outs, sems)

    return pl.pallas_call(
        body, name=name, out_shape=plan.outs,
        in_specs=[pl.BlockSpec(memory_space=pl.ANY)] * n, out_specs=[pl.BlockSpec(memory_space=pl.ANY)] * n,
        scratch_shapes=plan.scratch,
    )(*plan.ins)


def _slab_spec(lead, rows, cols, nb):
    if rows % (nb * 16) == 0:
        return pl.BlockSpec((lead, rows // nb, cols), lambda i: (0, i, 0))
    if cols % (nb * 128) == 0:
        return pl.BlockSpec((lead, rows, cols // nb), lambda i: (0, 0, i))
    return pl.BlockSpec((lead, rows, cols), lambda i: (0, 0, 0))


def _slab_spec2(rows, cols, nb):
    if rows % (nb * 16) == 0:
        return pl.BlockSpec((rows // nb, cols), lambda i: (i, 0))
    if cols % (nb * 128) == 0:
        return pl.BlockSpec((rows, cols // nb), lambda i: (0, i))
    return pl.BlockSpec((rows, cols), lambda i: (0, 0))


def _cast_call(arrays, name, host=None):
    n = len(arrays)
    nb = 8

    def body(*refs):
        for a in range(n):
            refs[n + a][...] = refs[a][...].astype(BF16)

    specs = [_slab_spec2(x.shape[0], x.shape[1], nb) for x in arrays]
    return _hosting_call(body, name, nb, host, list(arrays), specs,
                         [jax.ShapeDtypeStruct(x.shape, BF16) for x in arrays], specs, [])


def _pair_add(sends, fromsib, name):
    n = len(sends)
    nb = 8

    def body(*refs):
        c = lax.axis_index("c")
        for a in range(n):
            s_ref, f_ref, t_ref = refs[a], refs[n + a], refs[2 * n + a]
            for j in range(4):
                t_ref[j] = (s_ref[2 * j + c].astype(F32) + f_ref[j].astype(F32)).astype(t_ref.dtype)

    def spec(a, lead):
        return _slab_spec(lead, a.shape[1], a.shape[2], nb)

    return pl.pallas_call(
        body, name=name, grid=(nb,),
        in_specs=[spec(a, N_DEV) for a in sends] + [spec(a, 4) for a in fromsib],
        out_specs=[spec(a, 4) for a in fromsib],
        out_shape=[jax.ShapeDtypeStruct(a.shape, a.dtype) for a in fromsib],
        compiler_params=pltpu.CompilerParams(dimension_semantics=("arbitrary",), vmem_limit_bytes=VMEM_LIMIT),
    )(*sends, *fromsib)


def _adamw_vals(w, g, m, v):
    m2 = ADAM_B1 * m + (1.0 - ADAM_B1) * g
    v2 = ADAM_B2 * v + (1.0 - ADAM_B2) * (g * g)
    m_hat = m2 / (1.0 - ADAM_B1 ** ADAM_STEP)
    v_hat = v2 / (1.0 - ADAM_B2 ** ADAM_STEP)
    delta = -ADAM_LR * (m_hat / (jnp.sqrt(v_hat) + ADAM_EPS) + ADAM_WD * w)
    return delta, m2, v2


def _updates_call(recvs, ws, ms, vs, name, host=None):
    n = len(recvs)
    nb = 8

    def body(*refs):
        for a in range(n):
            r_ref, w_ref, m_ref, v_ref = refs[a], refs[n + a], refs[2 * n + a], refs[3 * n + a]
            g_ref, d_ref, m2_ref, v2_ref = refs[4 * n + 4 * a:4 * n + 4 * a + 4]
            g = r_ref[0].astype(F32)
            for d in range(1, r_ref.shape[0]):
                g = g + r_ref[d].astype(F32)
            dl, m2, v2 = _adamw_vals(w_ref[...], g, m_ref[...], v_ref[...])
            g_ref[...] = g
            d_ref[...] = dl
            m2_ref[...] = m2
            v2_ref[...] = v2

    def spec3(r):
        return _slab_spec(r.shape[0], r.shape[1], r.shape[2], nb)

    def spec2(w):
        return _slab_spec2(w.shape[0], w.shape[1], nb)

    res, hosted = _hosting_call(
        body, name, nb, host, list(recvs) + list(ws) + list(ms) + list(vs),
        [spec3(r) for r in recvs] + [spec2(w) for w in ws] * 3,
        [jax.ShapeDtypeStruct(w.shape, F32) for w in ws for _ in range(4)],
        [spec2(w) for w in ws for _ in range(4)], [])
    return [res[4 * a:4 * a + 4] for a in range(n)], hosted


def _small_sum(gath_a, gath_b, loss_g, row0_g, name):
    _, Ra, C = gath_a.shape
    Rb = gath_b.shape[1]

    def body(a_ref, b_ref, l_ref, r_ref, go_ref, lo_ref):
        ga, gb = a_ref[0].astype(F32), b_ref[0].astype(F32)
        lsum, row0 = l_ref[0], r_ref[0]
        for d in range(1, N_DEV):
            ga = ga + a_ref[d].astype(F32)
            gb = gb + b_ref[d].astype(F32)
            lsum = lsum + l_ref[d]
            row0 = row0 + r_ref[d]
        go_ref[0:Ra, :] = ga
        go_ref[Ra:Ra + Rb, :] = gb
        lo_ref[...] = lsum
        go_ref[0:8, :] = go_ref[0:8, :] + jnp.where(lax.broadcasted_iota(jnp.int32, row0.shape, 0) == 0, row0, 0.0)

    return pl.pallas_call(
        body, name=name, grid=(1,),
        in_specs=[pl.BlockSpec((N_DEV, Ra, C), lambda i: (0, 0, 0)), pl.BlockSpec((N_DEV, Rb, C), lambda i: (0, 0, 0)),
                  pl.BlockSpec((N_DEV, 8, HD), lambda i: (0, 0, 0)), pl.BlockSpec((N_DEV, 8, C), lambda i: (0, 0, 0))],
        out_specs=[pl.BlockSpec((Ra + Rb, C), lambda i: (0, 0)), pl.BlockSpec((8, HD), lambda i: (0, 0))],
        out_shape=[jax.ShapeDtypeStruct((Ra + Rb, C), F32), jax.ShapeDtypeStruct((8, HD), F32)],
        compiler_params=pltpu.CompilerParams(dimension_semantics=("arbitrary",), vmem_limit_bytes=VMEM_LIMIT),
    )(gath_a, gath_b, loss_g, row0_g)


def _adamw_multi(ws, gs, ms, vs, name, nblk=1):
    n = len(ws)

    def body(*refs):
        for a in range(n):
            dl, m2, v2 = _adamw_vals(refs[a][...], refs[n + a][...], refs[2 * n + a][...], refs[3 * n + a][...])
            refs[4 * n + 3 * a][...] = dl
            refs[4 * n + 3 * a + 1][...] = m2
            refs[4 * n + 3 * a + 2][...] = v2

    def spec(x):
        rest = (0,) * (x.ndim - 1)
        return pl.BlockSpec((x.shape[0] // nblk,) + tuple(x.shape[1:]), lambda i: (i,) + rest)

    res = pl.pallas_call(
        body, name=name, grid=(nblk,),
        in_specs=[spec(w) for w in ws] * 4, out_specs=[spec(w) for w in ws for _ in range(3)],
        out_shape=[jax.ShapeDtypeStruct(w.shape, F32) for w in ws for _ in range(3)],
        compiler_params=pltpu.CompilerParams(dimension_semantics=("arbitrary",), vmem_limit_bytes=VMEM_LIMIT),
    )(*ws, *gs, *ms, *vs)
    return [res[3 * a:3 * a + 3] for a in range(n)]


def _s5_param_fn(lr, li, ls, btr, bti):
    step = jnp.exp(ls)
    er = jnp.exp(lr * step)
    ang = li * step
    ar = er * jnp.cos(ang)
    ai = er * jnp.sin(ang)
    nr = ar - 1.0
    den = lr * lr + li * li
    fr = (nr * lr + ai * li) / den
    fi = (ai * lr - nr * li) / den
    return ar, ai, fr * btr - fi * bti, fr * bti + fi * btr


def _s5_params(lr, li, ls, btr, bti, cre, cim):
    nb = S5_G // S5_GB
    GC = S5_GB * S5_C
    expand = jnp.asarray(np.tile(np.eye(S5_P, dtype=np.float32), (1, S5_GB)), BF16)
    own = jnp.asarray((np.arange(GC)[:, None] // S5_C == np.arange(S5_W)[None, :] // S5_P).astype(np.float32))

    def body(lr_ref, li_ref, ls_ref, br_ref, bi_ref, cr_ref, ci_ref, e_ref, own_ref, ar_ref, ai_ref, bm_ref, cm_ref):
        ar, ai, bbr, bbi = _s5_param_fn(lr_ref[...], li_ref[...], ls_ref[...], br_ref[...], bi_ref[...])
        ar_ref[...] = ar
        ai_ref[...] = ai

        def plane(x, n):
            rows = x[n * S5_GB:(n + 1) * S5_GB].reshape(GC, S5_P).astype(BF16)
            return _dot(rows, e_ref[...]) * own_ref[...]

        for n in range(nb):
            bm_ref[n] = jnp.concatenate([plane(bbr, n), plane(bbi, n)], axis=-1).astype(BF16)
            cm_ref[n] = jnp.concatenate([plane(cr_ref[...], n), -plane(ci_ref[...], n)], axis=-1).astype(BF16)

    sd = jax.ShapeDtypeStruct
    return pl.pallas_call(
        body, name="s5_params",
        out_shape=[sd(lr.shape, F32), sd(lr.shape, F32), sd((nb, GC, 2 * S5_W), BF16), sd((nb, GC, 2 * S5_W), BF16)],
        compiler_params=pltpu.CompilerParams(vmem_limit_bytes=VMEM_LIMIT),
    )(lr, li, ls, btr, bti, cre, cim, expand, own)


def _s5_params_bwd(lr, li, ls, btr, bti, dar, dai, dbbr, dbbi):
    def body(lr_ref, li_ref, ls_ref, br_ref, bi_ref, dar_ref, dai_ref, dbbr_ref, dbbi_ref,
             dlr_ref, dli_ref, dls_ref, dbr_ref, dbi_ref):
        _, vjp = jax.vjp(_s5_param_fn, lr_ref[...], li_ref[...], ls_ref[...], br_ref[...], bi_ref[...])
        dlr, dli, dls, dbr, dbi = vjp((dar_ref[...], dai_ref[...], dbbr_ref[...], dbbi_ref[...]))
        dlr_ref[...] = dlr
        dli_ref[...] = dli
        dls_ref[...] = dls
        dbr_ref[...] = dbr
        dbi_ref[...] = dbi

    sd = jax.ShapeDtypeStruct
    return pl.pallas_call(
        body, name="s5_params_bwd",
        out_shape=[sd(lr.shape, F32), sd(lr.shape, F32), sd(ls.shape, F32), sd(btr.shape, F32), sd(btr.shape, F32)],
    )(lr, li, ls, btr, bti, dar, dai, dbbr, dbbi)


def _cpow(ar, ai, n):
    assert n & (n - 1) == 0
    while n > 1:
        ar, ai = ar * ar - ai * ai, 2.0 * ar * ai
        n //= 2
    return ar, ai


def _scan(st, cr, ci, init, nk, reverse, store, prev=None):
    W = S5_W

    def advance(k, sr, si):
        rows = pl.ds(k * 8 if isinstance(k, int) else pl.multiple_of(k * 8, 8), 8)
        nsr = cr * sr - ci * si + st[rows, 0:W]
        nsi = cr * si + ci * sr + st[rows, W:2 * W]
        if store:
            st[rows, 0:W] = nsr
            st[rows, W:2 * W] = nsi
        return nsr, nsi

    if prev is None:
        return lax.fori_loop(0, nk, lambda j, c: advance(nk - 1 - j if reverse else j, c[0], c[1]), init, unroll=2)
    assert reverse

    def step(j, carry):
        k = nk - 1 - j
        nsr, nsi = advance(k, carry[0], carry[1])
        prows = pl.ds(pl.multiple_of((k - 1) * 8, 8), 8)
        pr = prev[prows, 0:W]
        pi = prev[prows, W:2 * W]
        return nsr, nsi, carry[2] + nsr * pr + nsi * pi, carry[3] + nsi * pr - nsr * pi

    carry = lax.fori_loop(0, nk - 1, step, init, unroll=2)
    nsr, nsi = advance(0, carry[0], carry[1])
    return nsr, nsi, carry[2], carry[3]


def _chain(fin, fr, fi, pr, pi, reverse):
    W = S5_W
    fin[:, 0:W] = fr
    fin[:, W:2 * W] = fi
    rowid = lax.broadcasted_iota(jnp.int32, (8, W), 0)
    cr = jnp.zeros((1, W), F32)
    ci = jnp.zeros((1, W), F32)
    init_r = jnp.zeros((8, W), F32)
    init_i = jnp.zeros((8, W), F32)
    for s in (range(7, -1, -1) if reverse else range(8)):
        init_r = jnp.where(rowid == s, cr, init_r)
        init_i = jnp.where(rowid == s, ci, init_i)
        lr = fin[s:s + 1, 0:W]
        li = fin[s:s + 1, W:2 * W]
        cr, ci = lr + pr * cr - pi * ci, li + pr * ci + pi * cr
    return init_r, init_i


def _full_scan(st, fin, ar, ai, nk, reverse, prev=None, carry_in=None, carry_out=None):
    W = S5_W
    cr = jnp.broadcast_to(ar, (8, W))
    ci = jnp.broadcast_to(-ai if reverse else ai, (8, W))
    z = jnp.zeros((8, W), F32)
    if carry_in is None:
        fr, fi = _scan(st, cr, ci, (z, z), nk, reverse, store=False)
        pr, pi = _cpow(ar, -ai if reverse else ai, nk)
        init = _chain(fin, fr, fi, pr, pi, reverse)
    else:
        init = (carry_in[:, 0:W], carry_in[:, W:2 * W])
    if carry_out is not None:
        carry_out[:, 0:W] = init[0]
        carry_out[:, W:2 * W] = init[1]
    if prev is None:
        return _scan(st, cr, ci, init, nk, reverse, store=True)
    return _scan(st, cr, ci, init + (z, z), nk, reverse, store=True, prev=prev)


def _s5_specs(L):
    W2 = 2 * S5_W
    GC = S5_GB * S5_C
    col = pl.BlockSpec((L, GC), lambda g: (0, g))
    vec = pl.BlockSpec((1, GC), lambda g: (0, g))
    avec = pl.BlockSpec((1, S5_W), lambda g: (0, g))
    bmat = pl.BlockSpec((None, GC, W2), lambda g: (g, 0, 0))
    cmat = pl.BlockSpec((None, W2, GC), lambda g: (g, 0, 0))
    return col, vec, avec, bmat, cmat


def _interleave(dst, src, nk):
    for s in range(8):
        dst[pl.ds(s, nk, stride=8), :] = src[s * nk:(s + 1) * nk, :]


def _deinterleave(dst, src, nk):
    for s in range(8):
        dst[s * nk:(s + 1) * nk, :] = src[pl.ds(s, nk, stride=8), :].astype(dst.dtype)


def _hosting_call(body, name, nsteps, host, ins, in_specs, outs, out_specs, scratch):
    grid = (nsteps,) if isinstance(nsteps, int) else tuple(nsteps)
    params = pltpu.CompilerParams(dimension_semantics=("arbitrary",) * len(grid), vmem_limit_bytes=VMEM_LIMIT)
    if host is None:
        res = pl.pallas_call(
            body, name=name, grid=grid, in_specs=in_specs, out_specs=out_specs, out_shape=outs,
            scratch_shapes=scratch, compiler_params=params,
        )(*ins)
        return list(res), []
    n_in, n_out, n_sc = len(ins), len(outs), len(scratch)
    h_in, h_out = len(host.ins), len(host.outs)

    def hosted(*refs):
        a = refs[:n_in]
        ha = refs[n_in:n_in + h_in]
        o = refs[n_in + h_in:n_in + h_in + n_out]
        ho = refs[n_in + h_in + n_out:n_in + h_in + n_out + h_out]
        sc = refs[n_in + h_in + n_out + h_out:n_in + h_in + n_out + h_out + n_sc]
        hs = refs[n_in + h_in + n_out + h_out + n_sc:]
        first = functools.reduce(jnp.logical_and, [pl.program_id(i) == 0 for i in range(len(grid))])
        last = functools.reduce(jnp.logical_and, [pl.program_id(i) == g - 1 for i, g in enumerate(grid)])

        @pl.when(first)
        def _():
            host.start(ha, ho, hs)

        body(*a, *o, *sc)

        @pl.when(last)
        def _():
            host.finish(ha, ho, hs)

    hbm = pl.BlockSpec(memory_space=pl.ANY)
    res = pl.pallas_call(
        hosted, name=name, grid=grid,
        in_specs=list(in_specs) + [hbm] * h_in, out_specs=list(out_specs) + [hbm] * h_out,
        out_shape=list(outs) + list(host.outs), scratch_shapes=list(scratch) + list(host.scratch),
        compiler_params=params,
    )(*ins, *host.ins)
    return list(res[:n_out]), list(res[n_out:])


def _s5_fwd(u, bm, cm, ar, ai, dvec, host=None):
    L = u.shape[0]
    nk = L // 8
    GC = S5_GB * S5_C
    nb = S5_G // S5_GB
    col, vec, avec, bmat, cmat = _s5_specs(L)

    def body(u_ref, b_ref, c_ref, ar_ref, ai_ref, d_ref, y_ref, carry_ref, st, fin, ui, yi):
        _interleave(ui, u_ref, nk)
        for r in range(8):
            rows = slice(r * nk, (r + 1) * nk)
            st[rows, :] = _dot(ui[rows, :].astype(BF16), b_ref[...])
        _full_scan(st, fin, ar_ref[...], ai_ref[...], nk, reverse=False, carry_out=carry_ref)
        for r in range(8):
            rows = slice(r * nk, (r + 1) * nk)
            yi[rows, :] = _dot_nt(st[rows, :].astype(BF16), c_ref[...]) + d_ref[...] * ui[rows, :]
        _deinterleave(y_ref, yi, nk)

    return _hosting_call(
        body, "s5_fwd", nb, host,
        [u, bm, cm, ar, ai, dvec], [col, bmat, bmat, avec, avec, vec],
        [jax.ShapeDtypeStruct(u.shape, F32), jax.ShapeDtypeStruct((nb * 8, 2 * S5_W), F32)],
        [col, pl.BlockSpec((8, 2 * S5_W), lambda g: (g, 0))],
        [pltpu.VMEM((L, 2 * S5_W), F32), pltpu.VMEM((8, 2 * S5_W), F32), pltpu.VMEM((L, GC), F32),
         pltpu.VMEM((L, GC), F32)])


def _s5_bwd(u, dy, carry, bm, cm, ar, ai, dvec, mask, rmat, host=None):
    L = u.shape[0]
    nk = L // 8
    W = S5_W
    GC = S5_GB * S5_C
    col, vec, avec, bmat, cmat = _s5_specs(L)
    hi = lax.Precision.HIGHEST

    def body(u_ref, dy_ref, carry_ref, b_ref, ct_ref, ar_ref, ai_ref, d_ref, mask_ref, r_ref,
             du_ref, db_ref, dc_ref, dd_ref, dar_ref, dai_ref, sa, sb, fin, ui, dyi, dui):
        ar = ar_ref[...]
        ai = ai_ref[...]
        _interleave(ui, u_ref, nk)
        _interleave(dyi, dy_ref, nk)
        for r in range(8):
            rows = slice(r * nk, (r + 1) * nk)
            sa[rows, :] = _dot(ui[rows, :].astype(BF16), b_ref[...])
            sb[rows, :] = _dot(dyi[rows, :].astype(BF16), ct_ref[...])
        _full_scan(sa, fin, ar, ai, nk, reverse=False, carry_in=carry_ref)
        gr, gi, accr, acci = _full_scan(sb, fin, ar, ai, nk, reverse=True, prev=sa)
        rowid = lax.broadcasted_iota(jnp.int32, (8, W), 0)
        last = pl.ds((nk - 1) * 8, 8)
        pr = jnp.where(rowid == 0, 0.0, pltpu.roll(sa[last, 0:W], 1, 0))
        pi = jnp.where(rowid == 0, 0.0, pltpu.roll(sa[last, W:2 * W], 1, 0))
        accr = accr + gr * pr + gi * pi
        acci = acci + gi * pr - gr * pi
        dar_ref[...] = jnp.sum(accr, axis=0, keepdims=True)
        dai_ref[...] = jnp.sum(acci, axis=0, keepdims=True)
        dbf = jnp.zeros((GC, 2 * W), F32)
        dcf = jnp.zeros((GC, 2 * W), F32)
        dd = jnp.zeros((1, GC), F32)
        for r in range(8):
            rows = slice(r * nk, (r + 1) * nk)
            ub = ui[rows, :]
            dyb = dyi[rows, :]
            gb = sb[rows, :].astype(BF16)
            dui[rows, :] = _dot_nt(gb, b_ref[...]) + d_ref[...] * dyb
            dbf = dbf + _dot_tn(ub.astype(BF16), gb)
            dcf = dcf + _dot_tn(dyb.astype(BF16), sa[rows, :].astype(BF16))
            dd = dd + jnp.sum(dyb * ub, axis=0, keepdims=True)
        db_ref[...] = jnp.dot(dbf * mask_ref[...], r_ref[...], precision=hi, preferred_element_type=F32)
        dc_ref[...] = jnp.dot(dcf * mask_ref[...], r_ref[...], precision=hi, preferred_element_type=F32)
        dd_ref[...] = dd
        _deinterleave(du_ref, dui, nk)

    cmp_spec = pl.BlockSpec((GC, 2 * S5_P), lambda g: (g, 0))
    whole = lambda shape: pl.BlockSpec(shape, lambda g: (0, 0))
    sd = jax.ShapeDtypeStruct
    return _hosting_call(
        body, "s5_bwd", S5_G // S5_GB, host,
        [u, dy, carry, bm, cm, ar, ai, dvec, mask, rmat],
        [col, col, pl.BlockSpec((8, 2 * W), lambda g: (g, 0)), bmat, bmat, avec, avec, vec, whole(mask.shape),
         whole(rmat.shape)],
        [sd(u.shape, BF16), sd((S5_G * S5_C, 2 * S5_P), F32), sd((S5_G * S5_C, 2 * S5_P), F32),
         sd((1, PRIM), F32), sd((1, S5_G * S5_P), F32), sd((1, S5_G * S5_P), F32)],
        [col, cmp_spec, cmp_spec, vec, avec, avec],
        [pltpu.VMEM((L, 2 * W), F32), pltpu.VMEM((L, 2 * W), F32), pltpu.VMEM((8, 2 * W), F32),
         pltpu.VMEM((L, GC), F32), pltpu.VMEM((L, GC), F32), pltpu.VMEM((L, GC), F32)])


def _s5_compact_consts():
    g_row = np.arange(S5_GB * S5_C) // S5_C
    col = np.arange(2 * S5_W)
    g_col = (col % S5_W) // S5_P
    mask = (g_row[:, None] == g_col[None, :]).astype(np.float32)
    tgt = (col // S5_W) * S5_P + col % S5_P
    rmat = (tgt[:, None] == np.arange(2 * S5_P)[None, :]).astype(np.float32)
    return jnp.asarray(mask), jnp.asarray(rmat)


def _attn_scores(q_ref, k_ref, qb, bq, scale):
    ext = (qb + 1) * bq
    s = _dot_nt(q_ref[qb * bq:ext, :], k_ref[0:ext, :]) * scale
    qpos = lax.broadcasted_iota(jnp.int32, (bq, bq), 0)
    kpos = lax.broadcasted_iota(jnp.int32, (bq, bq), 1)
    diag = jnp.where(kpos <= qpos, s[:, ext - bq:], NEG)
    return diag if qb == 0 else jnp.concatenate([s[:, :ext - bq], diag], axis=-1)


def _attn_fwd(qp, kp, v, scale):
    L = qp.shape[0]
    bq = min(256, L)

    def body(q_ref, k_ref, v_ref, o_ref, lse_ref):
        for qb in range(L // bq):
            rows = slice(qb * bq, (qb + 1) * bq)
            s = _attn_scores(q_ref, k_ref, qb, bq, scale)
            m = jnp.max(s, axis=-1, keepdims=True)
            e = jnp.exp(s - m)
            l = jnp.sum(e, axis=-1, keepdims=True)
            o_ref[rows, :] = _dot(e.astype(BF16), v_ref[0:(qb + 1) * bq, :]) / l
            lse_ref[rows, :] = jnp.broadcast_to(m + jnp.log(l), (bq, HD))

    blk = pl.BlockSpec((L, HD), lambda h: (0, h))
    wide = pl.BlockSpec((L, 2 * HD), lambda h: (0, h))
    return pl.pallas_call(
        body, name="mla_attn_fwd", grid=(MLA_H,),
        in_specs=[wide, wide, blk], out_specs=[blk, blk],
        out_shape=[jax.ShapeDtypeStruct((L, MLA_H * HD), F32)] * 2,
        compiler_params=pltpu.CompilerParams(dimension_semantics=("arbitrary",), vmem_limit_bytes=VMEM_LIMIT),
    )(qp, kp, v)


def _attn_bwd(qp, kp, v, o, lse, do, scale):
    L = qp.shape[0]
    bq = min(256, L)
    nq = L // bq

    def body(q_ref, k_ref, v_ref, o_ref, lse_ref, do_ref, dq_ref, dk_ref, dv_ref, dk_acc, dv_acc):
        dk_acc[...] = jnp.zeros_like(dk_acc)
        dv_acc[...] = jnp.zeros_like(dv_acc)
        for qb in range(nq):
            rows = slice(qb * bq, (qb + 1) * bq)
            ext = (qb + 1) * bq
            do = do_ref[rows, :]
            dob = do.astype(BF16)
            p = jnp.exp(_attn_scores(q_ref, k_ref, qb, bq, scale) - lse_ref[rows, 0:1])
            dp = _dot_nt(dob, v_ref[0:ext, :])
            dsum = jnp.sum(do * o_ref[rows, :], axis=-1, keepdims=True)
            ds = (p * (dp - dsum) * scale).astype(BF16)
            dq_ref[rows, :] = _dot(ds, k_ref[0:ext, :]).astype(dq_ref.dtype)
            dk_acc[0:ext, :] += _dot_tn(ds, q_ref[rows, :])
            dv_acc[0:ext, :] += _dot_tn(p.astype(BF16), dob)
        dk_ref[...] = dk_acc[...].astype(dk_ref.dtype)
        dv_ref[...] = dv_acc[...].astype(dv_ref.dtype)

    sd = jax.ShapeDtypeStruct
    blk = pl.BlockSpec((L, HD), lambda h: (0, h))
    wide = pl.BlockSpec((L, 2 * HD), lambda h: (0, h))
    return pl.pallas_call(
        body, name="mla_attn_bwd", grid=(MLA_H,),
        in_specs=[wide, wide, blk, blk, blk, blk], out_specs=[wide, wide, blk],
        out_shape=[sd((L, MLA_H * 2 * HD), BF16), sd((L, MLA_H * 2 * HD), BF16), sd((L, MLA_H * HD), BF16)],
        scratch_shapes=[pltpu.VMEM((L, 2 * HD), F32), pltpu.VMEM((L, HD), F32)],
        compiler_params=pltpu.CompilerParams(dimension_semantics=("arbitrary",), vmem_limit_bytes=VMEM_LIMIT),
    )(qp, kp, v, o, lse, do)


def _kv_fn(mem, gm, w, gk):
    kv = _mm(_rms(mem, gm, D_MODEL), w)
    k = jnp.concatenate([_rms(kv[:, HD * h:HD * (h + 1)], gk, HD) for h in range(X_HEADS)], axis=-1)
    return k, kv[:, XQ:]


def _kv_prep(mem, gm, w, gk, name):
    def fn(mem, gm, w, gk):
        return _kv_fn(mem, gm, w, gk)
    M = mem.shape[0]
    return _rowwise(name, fn, [('c', mem), ('c', gm), ('c', w), ('c', gk)],
                    [('c', (M, XQ), F32), ('c', (M, XQ), F32)], 1)


def _kv_prep_bwd(mem, gm, w, gk, dk, dv, name):
    def fn(mem, gm, w, gk, dk, dv):
        _, vjp = jax.vjp(lambda a, b, c: _kv_fn(mem, a, b, c), gm, w, gk)
        return vjp((dk, dv))
    return _rowwise(name, fn, [('c', mem), ('c', gm), ('c', w), ('c', gk), ('c', dk), ('c', dv)],
                    [('c', gm.shape, F32), ('c', w.shape, BF16), ('c', gk.shape, F32)], 1)


def _forward_merge(x, mix, mix_kind, xq, gate, k, v, gq, wout, name, nblk, host=None):
    def fn(x, mix, xq, gate, k, v, gq, wout):
        o = _merge(mix, xq, gate, k, v, gq)
        return (x + _dot(o.astype(BF16), wout),)
    L = x.shape[0]
    out = _rowwise(name, fn, [('r', x), (mix_kind, mix), ('r', xq), ('r', gate), ('c', k), ('c', v), ('c', gq),
                              ('c', wout)], [('r', (L, D_MODEL), F32)], nblk, host=host)
    return out[0] if host is None else (out[0][0], out[1])


def _backward_merge(dx, mix, mix_kind, xq, gate, k, v, gq, wout, name, nblk, host=None):
    def fn(dx, mix, xq, gate, k, v, gq, wout):
        g16 = dx.astype(BF16)
        do = _dot_nt(g16, wout)
        o, vjp = jax.vjp(_merge, mix, xq, gate, k, v, gq)
        dmix, dxq, dgate, dk, dv, dgq = vjp(do)
        return dmix, dxq, dgate, o, g16, dk, dv, dgq
    L = dx.shape[0]
    return _rowwise(
        name, fn,
        [('r', dx), (mix_kind, mix), ('r', xq), ('r', gate), ('c', k), ('c', v), ('c', gq), ('c', wout)],
        [('r', (L, PRIM), F32), ('r', (L, XQ), BF16), ('r', (L, BRANCH), BF16), ('t', (BRANCH, L), BF16),
         ('r', (L, D_MODEL), BF16), ('a', k.shape, F32), ('a', v.shape, F32), ('a', gq.shape, F32)], nblk,
        host=host)


_MLA_IN = 3392
_MLA_IN_PAD = 3456


def _uq_rows(wt):
    r = wt.reshape(MLA_H, HD + ROPE, wt.shape[1])
    return jnp.concatenate([r[:, :HD].reshape(PRIM, -1),
                            jnp.pad(r[:, HD:], ((0, 0), (0, HD - ROPE), (0, 0))).reshape(PRIM, -1)], axis=0)


_UQ_ROW_BANDS = [band for h in range(MLA_H) for band in ((h * HD, HD), (PRIM + h * HD, ROPE))]


_MLA_IN_ROW_BANDS = [(0, 768), (3328, 64), (768, 2560)]


_SMALL = (("ln_gain", 2048), ("mem_norm", 2048), ("xq_norm", 256), ("xk_norm", 256), ("s5_lambda_re", 6144),
          ("s5_lambda_im", 6144), ("s5_log_step", 96), ("s5_b_re", 98304), ("s5_b_im", 98304), ("s5_c_re", 98304),
          ("s5_c_im", 98304), ("s5_d", 1536), ("mla_q_lora_norm", 512), ("mla_kv_lora_norm", 256),
          ("mla_q_nope_norm", 128), ("mla_k_nope_norm", 128), ("mla_q_rope_norm", 64), ("mla_k_rope_norm", 64))
_SMALL_ROWS = 432
_SMALL_OFF = {name: sum(n for _, n in _SMALL[:i]) for i, (name, _) in enumerate(_SMALL)}


def _pack_small(d):
    flat = jnp.concatenate([d[n].reshape(-1).astype(F32) for n, _ in _SMALL])
    return jnp.pad(flat, (0, _SMALL_ROWS * 1024 - flat.shape[0])).reshape(_SMALL_ROWS, 1024)


def _unpack_small(p, name, shape):
    off = _SMALL_OFF[name]
    return p.reshape(-1)[off:off + int(np.prod(shape))].reshape(shape)


_WEIGHTS = ('ln_gain', 'w_out', 'mem_norm', 'w_mem_kv', 'xq_norm', 'xk_norm', 's5_w_in', 's5_lambda_re',
            's5_lambda_im', 's5_log_step', 's5_b_re', 's5_b_im', 's5_c_re', 's5_c_im', 's5_d', 's5_w_glu', 'mla_w_in',
            'mla_q_lora_norm', 'mla_kv_lora_norm', 'mla_w_uq', 'mla_w_ukv', 'mla_q_nope_norm', 'mla_k_nope_norm',
            'mla_q_rope_norm', 'mla_k_rope_norm')


def _pad128(g):
    return jnp.pad(g.reshape(1, -1), ((0, 0), (0, HD - g.shape[-1])))


def kernel(x, mem, positions, ln_gain, w_out, mem_norm, w_mem_kv, xq_norm, xk_norm, s5_w_in, s5_lambda_re, s5_lambda_im, s5_log_step, s5_b_re, s5_b_im, s5_c_re, s5_c_im, s5_d, s5_w_glu, mla_w_in, mla_q_lora_norm, mla_kv_lora_norm, mla_w_uq, mla_w_ukv, mla_q_nope_norm, mla_k_nope_norm, mla_q_rope_norm, mla_k_rope_norm, loss_target, m_ln_gain, m_w_out, m_mem_norm, m_w_mem_kv, m_xq_norm, m_xk_norm, m_s5_w_in, m_s5_lambda_re, m_s5_lambda_im, m_s5_log_step, m_s5_b_re, m_s5_b_im, m_s5_c_re, m_s5_c_im, m_s5_d, m_s5_w_glu, m_mla_w_in, m_mla_q_lora_norm, m_mla_kv_lora_norm, m_mla_w_uq, m_mla_w_ukv, m_mla_q_nope_norm, m_mla_k_nope_norm, m_mla_q_rope_norm, m_mla_k_rope_norm, v_ln_gain, v_w_out, v_mem_norm, v_w_mem_kv, v_xq_norm, v_xk_norm, v_s5_w_in, v_s5_lambda_re, v_s5_lambda_im, v_s5_log_step, v_s5_b_re, v_s5_b_im, v_s5_c_re, v_s5_c_im, v_s5_d, v_s5_w_glu, v_mla_w_in, v_mla_q_lora_norm, v_mla_kv_lora_norm, v_mla_w_uq, v_mla_w_ukv, v_mla_q_nope_norm, v_mla_k_nope_norm, v_mla_q_rope_norm, v_mla_k_rope_norm):
    weights = dict(ln_gain=ln_gain, w_out=w_out, mem_norm=mem_norm, w_mem_kv=w_mem_kv, xq_norm=xq_norm,
                   xk_norm=xk_norm, s5_w_in=s5_w_in, s5_lambda_re=s5_lambda_re, s5_lambda_im=s5_lambda_im,
                   s5_log_step=s5_log_step, s5_b_re=s5_b_re, s5_b_im=s5_b_im, s5_c_re=s5_c_re, s5_c_im=s5_c_im,
                   s5_d=s5_d, s5_w_glu=s5_w_glu, mla_w_in=mla_w_in, mla_q_lora_norm=mla_q_lora_norm,
                   mla_kv_lora_norm=mla_kv_lora_norm, mla_w_uq=mla_w_uq, mla_w_ukv=mla_w_ukv,
                   mla_q_nope_norm=mla_q_nope_norm, mla_k_nope_norm=mla_k_nope_norm,
                   mla_q_rope_norm=mla_q_rope_norm, mla_k_rope_norm=mla_k_rope_norm)
    m_in = dict(zip(_WEIGHTS, (m_ln_gain, m_w_out, m_mem_norm, m_w_mem_kv, m_xq_norm, m_xk_norm, m_s5_w_in,
                               m_s5_lambda_re, m_s5_lambda_im, m_s5_log_step, m_s5_b_re, m_s5_b_im, m_s5_c_re,
                               m_s5_c_im, m_s5_d, m_s5_w_glu, m_mla_w_in, m_mla_q_lora_norm, m_mla_kv_lora_norm,
                               m_mla_w_uq, m_mla_w_ukv, m_mla_q_nope_norm, m_mla_k_nope_norm, m_mla_q_rope_norm,
                               m_mla_k_rope_norm)))
    v_in = dict(zip(_WEIGHTS, (v_ln_gain, v_w_out, v_mem_norm, v_w_mem_kv, v_xq_norm, v_xk_norm, v_s5_w_in,
                               v_s5_lambda_re, v_s5_lambda_im, v_s5_log_step, v_s5_b_re, v_s5_b_im, v_s5_c_re,
                               v_s5_c_im, v_s5_d, v_s5_w_glu, v_mla_w_in, v_mla_q_lora_norm, v_mla_kv_lora_norm,
                               v_mla_w_uq, v_mla_w_ukv, v_mla_q_nope_norm, v_mla_k_nope_norm, v_mla_q_rope_norm,
                               v_mla_k_rope_norm)))

    x0 = x[0]
    mem0 = mem[0]
    target = loss_target[0]
    L = x0.shape[0]
    nblk = 4
    nb_big = 8
    me = 4 * lax.axis_index("x") + 2 * lax.axis_index("y") + lax.axis_index("c")

    lora = jnp.pad(jnp.concatenate([mla_q_lora_norm, mla_kv_lora_norm], axis=1), ((0, 7), (0, HD - 96)))
    def gather(*shards):
        return _plan_all_gather(list(shards))

    kh = D_MODEL // 2
    (b_mkv0, b_glu, b_in_mla, b_out0, b_uq, b_ukv, b_mkv1, b_out1), (W_in_s5,) = _cast_call(
        [w_mem_kv[0], s5_w_glu[0], jnp.transpose(mla_w_in[0]), w_out[0], jnp.transpose(mla_w_uq[0]), mla_w_ukv[0],
         w_mem_kv[1], w_out[1]], "cast_shards", host=gather(s5_w_in[0].astype(BF16)))

    ln0, ln1 = ln_gain[0:1], ln_gain[1:2]
    gq0, gq1 = xq_norm[0:1], xq_norm[1:2]
    gk0, gk1 = xk_norm[0:1], xk_norm[1:2]
    gm0, gm1 = mem_norm[0:1], mem_norm[1:2]
    gqn, gkn = mla_q_nope_norm, mla_k_nope_norm
    gqr, gkr = _pad128(mla_q_rope_norm), _pad128(mla_k_rope_norm)

    lr3 = s5_lambda_re.reshape(S5_G, 1, S5_P)
    li3 = s5_lambda_im.reshape(S5_G, 1, S5_P)
    ls3 = s5_log_step.reshape(S5_G, 1, 1)
    btr = jnp.swapaxes(s5_b_re[0], 1, 2)
    bti = jnp.swapaxes(s5_b_im[0], 1, 2)
    a_r, a_i, bm, cm = _s5_params(lr3, li3, ls3, btr, bti, s5_c_re[0], s5_c_im[0])
    a_r2 = a_r.reshape(1, S5_G * S5_P)
    a_i2 = a_i.reshape(1, S5_G * S5_P)
    cmask, rmat = _s5_compact_consts()

    half = ROPE // 2
    inv_freq = ROPE_THETA ** (-jnp.arange(half, dtype=F32) / half)
    invf = jnp.concatenate([inv_freq, inv_freq, jnp.zeros((HD - ROPE,), F32)]).reshape(1, HD)

    def rot_tables(pos, invf):
        ang = pos.astype(F32) * invf
        lane = lax.broadcasted_iota(jnp.int32, ang.shape, 1)
        c = jnp.where(lane < ROPE, jnp.cos(ang), 0.0)
        s = jnp.sin(ang)
        return c, jnp.where(lane < half, -s, 0.0), jnp.where((lane >= half) & (lane < ROPE), s, 0.0)

    tc, ts1, ts2 = _rowwise("rot_tables", rot_tables, [('r', positions.reshape(L, 1)), ('c', invf)],
                            [('r', (L, HD), F32)] * 3, nblk)

    def in_s5(x, g, w):
        proj = _mm_slots(_rms(x, g, D_MODEL).astype(BF16), w)
        return proj[:, :PRIM], proj[:, PRIM:PRIM + XQ], proj[:, PRIM + XQ:]

    u_s5, xq_a, gate_a = _rowwise(
        "s5_in", in_s5, [('r', x0), ('c', ln0), ('c', W_in_s5)],
        [('r', (L, PRIM), F32), ('r', (L, XQ), F32), ('r', (L, BRANCH), F32)], nblk)
    (y_s5, s5_carry), (W_glu, G_mkv0, G_in_mla_a) = _s5_fwd(u_s5, bm, cm, a_r2, a_i2, s5_d,
                                                            host=gather(b_glu, b_mkv0, b_in_mla[:, :kh]))

    def glu(y, w):
        z = _mm_slots(_gelu(y).astype(BF16), w)
        return z[:, :PRIM] * _sigmoid(z[:, PRIM:]), z

    (y2, z_glu), (G_out0,) = _rowwise("s5_glu", glu, [('r', y_s5), ('c', W_glu)],
                                      [('r', (L, PRIM), F32), ('r', (L, 2 * PRIM), F32)], nblk, host=gather(b_out0))
    W_mkv0 = G_mkv0.reshape(D_MODEL, 2 * XQ)
    k_a, v_a = _kv_prep(mem0, gm0, W_mkv0, gk0, "kv_prep0")
    x1, (G_in_mla_b,) = _forward_merge(
        x0, y2, 'r', xq_a, gate_a, k_a, v_a, gq0, G_out0.reshape(BRANCH, D_MODEL), "merge0", nblk,
        host=gather(b_in_mla[:, kh:]))
    W_in_mla = jnp.concatenate([G_in_mla_a, G_in_mla_b], axis=2).reshape(_MLA_IN, D_MODEL)

    def in_mla(x, g, w):
        xn = _rms(x, g, D_MODEL).astype(BF16)
        a = _dot_nt(xn, w[0:768])
        kx = _dot_nt(xn, w[768:896])
        b = _dot_nt(xn, w[832:_MLA_IN])
        lane = lax.broadcasted_iota(jnp.int32, kx.shape, 1)
        return a[:, :512], a[:, 512:], b[:, :XQ], b[:, XQ:], jnp.where(lane < ROPE, kx, 0.0)

    (c_q, c_kv, xq_b, gate_b, krp), (G_uq, W_kv, G_lora) = _rowwise(
        "mla_in", in_mla, [('r', x1), ('c', ln1), ('c', W_in_mla)],
        [('r', (L, Q_LORA), F32), ('r', (L, KV_LORA), F32), ('r', (L, XQ), F32), ('r', (L, BRANCH), F32),
         ('r', (L, HD), F32)], nblk,
        host=gather(b_uq, b_ukv, lora))
    W_q = _uq_rows(G_uq.reshape(MLA_H * (HD + ROPE), Q_LORA))
    g_qlora = G_lora[:, 0, :64].reshape(1, Q_LORA)
    g_kvlora = G_lora[:, 0, 64:96].reshape(1, KV_LORA)

    def qkv(c_q, c_kv, krp, tc, ts1, ts2, gql, gkvl, wq, wkv, gqn, gkn, gqr, gkr):
        q = _dot_nt(_rms(c_q, gql, Q_LORA).astype(BF16), wq)
        kv = _mm_slots(_rms(c_kv, gkvl, KV_LORA).astype(BF16), wkv)
        kp, v = _kv_post(*_kv_chunks(kv), krp, gkn, gkr, tc, ts1, ts2)
        return _q_post(*_q_chunks(q), gqn, gqr, tc, ts1, ts2), kp, v

    qkv_consts = [('c', g_qlora), ('c', g_kvlora), ('c', W_q), ('c', W_kv), ('c', gqn), ('c', gkn), ('c', gqr),
                  ('c', gkr)]
    (q_pad, k_pad, v_h), (G_mkv1, G_out1) = _rowwise(
        "mla_qkv", qkv, [('r', c_q), ('r', c_kv), ('r', krp), ('r', tc), ('r', ts1), ('r', ts2)] + qkv_consts,
        [('r', (L, 2 * PRIM), BF16), ('r', (L, 2 * PRIM), BF16), ('r', (L, PRIM), BF16)], nblk,
        host=gather(b_mkv1, b_out1))
    W_out = (G_out0.reshape(BRANCH, D_MODEL), G_out1.reshape(BRANCH, D_MODEL))
    W_mkv = (W_mkv0, G_mkv1.reshape(D_MODEL, 2 * XQ))
    scale = (HD + ROPE) ** -0.5
    attn, lse = _attn_fwd(q_pad, k_pad, v_h, scale)
    k_b, v_b = _kv_prep(mem0, gm1, W_mkv[1], gk1, "kv_prep1")

    def merge_loss(x, mix, xq, gate, k, v, gq, wout, t):
        err = x + _dot(_merge(mix, xq, gate, k, v, gq).astype(BF16), wout) - t
        part = 0.5 * jnp.sum(jnp.sum(err * err, axis=-1, keepdims=True) * (1.0 / D_MODEL), axis=0, keepdims=True)
        return err * (1.0 / D_MODEL), jnp.broadcast_to(part, (1, HD))

    dx2, loss_part = _rowwise(
        "merge1_loss", merge_loss,
        [('r', x1), ('r', attn), ('r', xq_b), ('r', gate_b), ('c', k_b), ('c', v_b), ('c', gq1), ('c', W_out[1]),
         ('r', target)], [('r', (L, D_MODEL), F32), ('a', (1, HD), F32)], nblk)

    dattn, dxq_b, dgate_b, o_b, g_b, dk_b, dv_b, dgq1 = _backward_merge(
        dx2, attn, 'r', xq_b, gate_b, k_b, v_b, gq1, W_out[1], "merge1_bwd", nb_big)
    dgm1, dW_mkv1, dgk1 = _kv_prep_bwd(mem0, gm1, W_mkv[1], gk1, dk_b, dv_b, "kv_prep1_bwd")
    dW_out1 = _matmul_tn(o_b, g_b, "dw_out1")
    dq_pad, dk_pad, dv_h = _attn_bwd(q_pad, k_pad, v_h, attn, lse, dattn, scale)

    def qkv_bwd(c_q, c_kv, krp, tc, ts1, ts2, dqp, dkp, dv, gql, gkvl, wq, wkv, gqn, gkn, gqr, gkr):
        cqn, vjp_qn = jax.vjp(lambda a, b: _rms(a, b, Q_LORA), c_q, gql)
        ckvn, vjp_kvn = jax.vjp(lambda a, b: _rms(a, b, KV_LORA), c_kv, gkvl)
        cqn16 = cqn.astype(BF16)
        ckvn16 = ckvn.astype(BF16)
        q = _dot_nt(cqn16, wq)
        kv = _mm_slots(ckvn16, wkv)
        _, vjp_q = jax.vjp(lambda n, r, a, b: _q_post(n, r, a, b, tc, ts1, ts2), *_q_chunks(q), gqn, gqr)
        dnope, drope, dgqn, dgqr = vjp_q(dqp.astype(F32))
        dq = jnp.concatenate(dnope + drope, axis=-1)
        _, vjp_kv = jax.vjp(lambda n, v, k, a, b: _kv_post(n, v, k, a, b, tc, ts1, ts2), *_kv_chunks(kv), krp, gkn,
                            gkr)
        dkn, dvals, dkrp, dgkn, dgkr = vjp_kv((dkp.astype(F32), dv.astype(F32)))
        dkv = jnp.concatenate([x for pair in zip(dkn, dvals) for x in pair], axis=-1)
        dq16 = dq.astype(BF16)
        dkv16 = dkv.astype(BF16)
        dc_q, dgql = vjp_qn(_dot(dq16, wq))
        dc_kv, dgkvl = vjp_kvn(_mm_slots_nt(dkv16, wkv))
        return dc_q, dc_kv, dkrp, cqn16, dq16, ckvn16, dkv16, dgql, dgkvl, dgqn, dgkn, dgqr, dgkr

    (dc_q, dc_kv, dkrp, cqn16, dq16, ckvn16, dkv16, dgql, dgkvl, dgqn, dgkn, dgqr, dgkr) = _rowwise(
        "mla_qkv_bwd", qkv_bwd,
        [('r', c_q), ('r', c_kv), ('r', krp), ('r', tc), ('r', ts1), ('r', ts2), ('r', dq_pad), ('r', dk_pad),
         ('r', dv_h)] + qkv_consts,
        [('r', (L, Q_LORA), BF16), ('r', (L, KV_LORA), BF16), ('r', (L, HD), BF16), ('r', (L, Q_LORA), BF16),
         ('t', (2 * PRIM, L), BF16), ('t', (KV_LORA, L), BF16), ('r', (L, 2 * PRIM), BF16),
         ('a', (1, Q_LORA), F32), ('a', (1, KV_LORA), F32), ('a', (1, HD), F32), ('a', (1, HD), F32),
         ('a', (1, HD), F32), ('a', (1, HD), F32)], nb_big)
    dW_q = _matmul_tn(dq16, cqn16, "dw_uq", row_bands=_UQ_ROW_BANDS)
    dW_kv = _matmul_tn_slots(ckvn16, dkv16, "dw_ukv")

    def in_bwd(x, dres, g, w, *dparts):
        dproj = jnp.concatenate(dparts, axis=-1).astype(BF16)
        xn, vjp = jax.vjp(lambda a, b: _rms(a, b, D_MODEL), x, g)
        if w.ndim == 3:
            dxn = _mm_slots_nt(dproj, w)
        else:
            dkr = dproj[:, 3328:]
            dkr = jnp.where(lax.broadcasted_iota(jnp.int32, dkr.shape, 1) < ROPE, dkr, jnp.zeros_like(dkr))
            dxn = _dot(dproj[:, :768], w[0:768]) + _dot(dproj[:, 768:3328], w[832:_MLA_IN]) + _dot(dkr, w[768:896])
        dx, dg = vjp(dxn)
        return dx + dres, xn, dproj, dg

    dx1, xn1, dproj1, dln1 = _rowwise(
        "mla_in_bwd", in_bwd,
        [('r', x1), ('r', dx2), ('c', ln1), ('c', W_in_mla), ('r', dc_q), ('r', dc_kv), ('r', dxq_b), ('r', dgate_b),
         ('r', dkrp)],
        [('r', (L, D_MODEL), F32), ('r', (L, D_MODEL), BF16), ('t', (_MLA_IN_PAD, L), BF16), ('a', (1, D_MODEL), F32)],
        nblk)
    dW_in_mla = _matmul_tn(dproj1, xn1, "dw_mla_in", row_bands=_MLA_IN_ROW_BANDS)

    grads1 = [dW_out1.reshape(N_DEV, 256, D_MODEL), dW_mkv1.reshape(N_DEV, 128, 2 * XQ),
              dW_in_mla.reshape(N_DEV, 424, D_MODEL),
              dW_q.reshape(N_DEV, 288, Q_LORA), dW_kv]
    (dy2, dxq_a, dgate_a, o_a, g_a, dk_a, dv_a, dgq0), pair1 = _backward_merge(
        dx1, y2, 'r', xq_a, gate_a, k_a, v_a, gq0, W_out[0], "merge0_bwd", nb_big, host=_plan_pair(grads1))
    dgm0, dW_mkv0, dgk0 = _kv_prep_bwd(mem0, gm0, W_mkv[0], gk0, dk_a, dv_a, "kv_prep0_bwd")
    dW_out0 = _matmul_tn(o_a, g_a, "dw_out0")
    t1 = list(_pair_add(grads1, pair1, "rs_add_layer1"))

    def glu_bwd(y, z, dy2, w):
        h, vjp_h = jax.vjp(_gelu, y)
        _, vjp_z = jax.vjp(lambda a, b: a * _sigmoid(b), z[:, :PRIM], z[:, PRIM:])
        dz16 = jnp.concatenate(vjp_z(dy2), axis=-1).astype(BF16)
        return vjp_h(_mm_slots_nt(dz16, w))[0], h.astype(BF16), dz16

    grads0 = [dW_out0.reshape(N_DEV, 256, D_MODEL), dW_mkv0.reshape(N_DEV, 128, 2 * XQ)]
    (dy_s5, h16, dz16), glu_hosted = _rowwise(
        "s5_glu_bwd", glu_bwd, [('r', y_s5), ('r', z_glu), ('r', dy2), ('c', W_glu)],
        [('r', (L, PRIM), F32), ('t', (PRIM, L), BF16), ('r', (L, 2 * PRIM), BF16)], nb_big,
        host=_combine(_plan_chips(t1[2:3]), _plan_pair(grads0)))
    recv_in_mla, pair0 = glu_hosted[:1], glu_hosted[1:]
    dW_glu = _matmul_tn_slots(h16, dz16, "dw_glu")
    t0 = list(_pair_add(grads0 + [dW_glu], pair0 + list(_exchange_call(_plan_pair([dW_glu]), "rs_pair_glu")),
                        "rs_add_layer0"))
    both = [jnp.concatenate([t0[i], t1[i]], axis=1) for i in range(2)]
    (du_s5, dbc, dcc, dd, dar, dai), recv_rest = _s5_bwd(u_s5, dy_s5, s5_carry, bm, cm, a_r2, a_i2, s5_d,
                                                        cmask, rmat, host=_plan_chips(both + t1[3:] + t0[2:]))
    early_recv = recv_rest[:2] + recv_in_mla + recv_rest[2:]
    dbc4 = dbc.reshape(S5_G, S5_C, 2, S5_P)
    dcc4 = dcc.reshape(S5_G, S5_C, 2, S5_P)
    dlr, dli, dls, dbtr, dbti = _s5_params_bwd(
        lr3, li3, ls3, btr, bti, dar.reshape(S5_G, 1, S5_P), dai.reshape(S5_G, 1, S5_P), dbc4[:, :, 0], dbc4[:, :, 1])

    small_part = {
        "ln_gain": jnp.concatenate([jnp.zeros_like(dln1), dln1]), "mem_norm": jnp.concatenate([dgm0, dgm1]),
        "xq_norm": jnp.concatenate([dgq0, dgq1]), "xk_norm": jnp.concatenate([dgk0, dgk1]),
        "s5_lambda_re": dlr, "s5_lambda_im": dli, "s5_log_step": dls,
        "s5_b_re": jnp.swapaxes(dbtr, 1, 2), "s5_b_im": jnp.swapaxes(dbti, 1, 2),
        "s5_c_re": dcc4[:, :, 0], "s5_c_im": -dcc4[:, :, 1], "s5_d": dd,
        "mla_q_lora_norm": dgql, "mla_kv_lora_norm": dgkvl, "mla_q_nope_norm": dgqn, "mla_k_nope_norm": dgkn,
        "mla_q_rope_norm": dgqr[:, :ROPE], "mla_k_rope_norm": dgkr[:, :ROPE],
    }
    loss8 = jnp.pad(loss_part, ((0, 7), (0, 0)))
    packed = _pack_small(small_part).astype(BF16)
    first_rows = 224
    (dx0, xn0, dproj0, dln0), (small_gath_a, loss_g) = _rowwise(
        "s5_in_bwd", in_bwd,
        [('r', x0), ('r', dx1), ('c', ln0), ('c', W_in_s5), ('r', du_s5), ('r', dxq_a),
         ('r', dgate_a)],
        [('r', (L, D_MODEL), F32), ('t', (D_MODEL, L), BF16), ('r', (L, 2 * BRANCH), BF16), ('a', (1, D_MODEL), F32)],
        nblk, host=_plan_all_gather([packed[:first_rows], loss8]))
    dW_in_s5, (small_gath_b, ln0_gath) = _matmul_tn_slots(
        xn0, dproj0, "dw_s5_in", host=_plan_all_gather([packed[first_rows:], jnp.pad(dln0, ((0, 7), (0, 0)))]))

    late = [dW_in_s5]
    late_t = _pair_add(late, list(_exchange_call(_plan_pair(late), "rs_pair_late")), "rs_add_late")
    late_recv = list(_exchange_call(_plan_chips(late_t), "rs_chips_late"))
    owners = ["w_out", "w_mem_kv", "mla_w_in", "mla_w_uq", "mla_w_ukv", "s5_w_glu", "s5_w_in"]
    flipped = ("mla_w_in", "mla_w_uq")

    def shard(d, n):
        a = d[n]
        return jnp.transpose(a[0]) if n in flipped else a.reshape(-1, a.shape[-1])

    upd, _ = _updates_call(
        early_recv + late_recv, [shard(weights, n) for n in owners], [shard(m_in, n) for n in owners],
        [shard(v_in, n) for n in owners], "update_big")
    grads, delta, new_m, new_v = {}, {}, {}, {}
    for n, res in zip(owners, upd):
        shape = weights[n].shape
        grads[n], delta[n], new_m[n], new_v[n] = (
            (jnp.transpose(r)[None] if n in flipped else r.reshape(shape)) for r in res)

    gs, loss_sum = _small_sum(small_gath_a, small_gath_b, loss_g, ln0_gath, "small_sum")
    loss = loss_sum[0, 0]
    for n, _ in _SMALL:
        shape = weights[n].shape
        if n == "mla_q_lora_norm":
            grads[n] = lax.dynamic_slice(_unpack_small(gs, n, (Q_LORA,)), (me * 64,), (64,)).reshape(shape)
        elif n == "mla_kv_lora_norm":
            grads[n] = lax.dynamic_slice(_unpack_small(gs, n, (KV_LORA,)), (me * 32,), (32,)).reshape(shape)
        else:
            grads[n] = _unpack_small(gs, n, shape)

    def own(n, a):
        if a.ndim == 4:
            a = jnp.transpose(a, (0, 2, 3, 1))
        elif a.ndim == 3:
            a = jnp.transpose(a, (0, 2, 1))
        return a.reshape(a.shape[1:]) if a.ndim >= 3 else a

    def back(n, a):
        shape = weights[n].shape
        if len(shape) == 4:
            return jnp.transpose(a.reshape((1,) + a.shape), (0, 3, 1, 2))
        if len(shape) == 3:
            return jnp.transpose(a.reshape((1,) + a.shape), (0, 2, 1))
        return a.reshape(shape)

    wide = ("s5_b_re", "s5_b_im", "s5_c_re", "s5_c_im")
    for names, nb, call in (([n for n, _ in _SMALL if n not in wide], 1, "update_small"), (wide, 4, "update_s5_bc")):
        res = _adamw_multi([own(n, weights[n]) for n in names], [own(n, grads[n]) for n in names],
                           [own(n, m_in[n]) for n in names], [own(n, v_in[n]) for n in names], call, nb)
        for n, (dl, m2, v2) in zip(names, res):
            delta[n], new_m[n], new_v[n] = back(n, dl), back(n, m2), back(n, v2)
    return (loss, dx0[None], *[grads[n] for n in _WEIGHTS], *[delta[n] for n in _WEIGHTS],
            *[new_m[n] for n in _WEIGHTS], *[new_v[n] for n in _WEIGHTS])
```
